```python
import math
import jax, jax.numpy as jnp
from jax import lax
import numpy as np

D_MODEL = 1024
BATCH = 16
SEQ = 2048
DEPTH = 1

CHUNK = 64
EPS = 1e-6
S5_WIDTH = 512
S5_GROUP = 16
S5_GROUPS = S5_WIDTH // S5_GROUP
S5_STATE = 64
HG_WIDTH = 512
HG_HEAD_DIM = 128
HG_HEADS = HG_WIDTH // HG_HEAD_DIM
D_FF = 2816
CONV_W = 3
N_IN = S5_WIDTH + 4 * HG_WIDTH + 2 * D_MODEL

kernel_name = "hybrid_s5_hgrn2_convffn_block"


def rmsnorm(x, gain):
    x32 = x.astype(jnp.float32)
    y = x32 * lax.rsqrt(jnp.mean(x32 * x32, axis=-1, keepdims=True) + EPS)
    return (y * gain.astype(jnp.float32)).astype(x.dtype)


def s5_mixer(u, a_re, a_im, log_dt, b_re, b_im, c_re, c_im, d_skip):
    bsz, seq, _ = u.shape
    f32 = jnp.float32
    ug = u.astype(f32).reshape(bsz, seq, S5_GROUPS, S5_GROUP)
    a_re = a_re.astype(f32); a_im = a_im.astype(f32)
    dt = jnp.exp(log_dt.astype(f32))[:, None]
    mag = jnp.exp(a_re * dt)
    ang = a_im * dt
    lb_re = mag * jnp.cos(ang)
    lb_im = mag * jnp.sin(ang)
    den = a_re * a_re + a_im * a_im
    n_re = lb_re - 1.0
    n_im = lb_im
    co_re = ((n_re * a_re + n_im * a_im) / den)[..., None]
    co_im = ((n_im * a_re - n_re * a_im) / den)[..., None]
    b_re = b_re.astype(f32); b_im = b_im.astype(f32)
    bb_re = co_re * b_re - co_im * b_im
    bb_im = co_re * b_im + co_im * b_re
    bu_re = jnp.einsum('bsgh,gph->bsgp', ug, bb_re)
    bu_im = jnp.einsum('bsgh,gph->bsgp', ug, bb_im)
    al_re = jnp.broadcast_to(lb_re, (1, seq, S5_GROUPS, S5_STATE))
    al_im = jnp.broadcast_to(lb_im, (1, seq, S5_GROUPS, S5_STATE))

    def combine(left, right):
        ar1, ai1, br1, bi1 = left
        ar2, ai2, br2, bi2 = right
        return (ar2 * ar1 - ai2 * ai1,
                ar2 * ai1 + ai2 * ar1,
                ar2 * br1 - ai2 * bi1 + br2,
                ar2 * bi1 + ai2 * br1 + bi2)

    _, _, x_re, x_im = lax.associative_scan(combine, (al_re, al_im, bu_re, bu_im), axis=1)
    y = (jnp.einsum('bsgp,ghp->bsgh', x_re, c_re.astype(f32))
         - jnp.einsum('bsgp,ghp->bsgh', x_im, c_im.astype(f32))
         + d_skip.astype(f32).reshape(S5_GROUPS, S5_GROUP) * ug)
    return y.reshape(bsz, seq, S5_WIDTH).astype(u.dtype)


def hgrn2_mixer(q, f_raw, i, g, lb, norm_gain):
    bsz, seq, _ = q.shape
    nc = seq // CHUNK
    f32 = jnp.float32
    lb = lb.astype(f32).reshape(HG_HEADS, HG_HEAD_DIM)
    f = lb + (1.0 - lb) * jax.nn.sigmoid(f_raw.astype(f32).reshape(bsz, seq, HG_HEADS, HG_HEAD_DIM))
    log_f = jnp.log(f)
    k = 1.0 - f
    qa = jax.nn.silu(q.astype(f32)) * (HG_HEAD_DIM ** -0.5)

    def blocks(t):
        return t.astype(f32).reshape(bsz, nc, CHUNK, HG_HEADS, HG_HEAD_DIM).transpose(1, 0, 2, 3, 4)

    causal = jnp.tril(jnp.ones((CHUNK, CHUNK), dtype=bool))[None, :, :, None, None]

    def step(state, xs):
        qc, kc, ic, lfc = xs
        bcum = jnp.cumsum(lfc, axis=1)
        o_inter = jnp.einsum('bthk,bhkv->bthv', qc * jnp.exp(bcum), state)
        decay = jnp.exp(jnp.where(causal, bcum[:, :, None] - bcum[:, None, :], -jnp.inf))
        scores = jnp.sum(qc[:, :, None] * kc[:, None, :] * decay, axis=-1)
        o_intra = jnp.einsum('btsh,bshv->bthv', scores, ic)
        b_last = bcum[:, -1]
        state = (jnp.exp(b_last)[..., None] * state
                 + jnp.einsum('bshk,bshv->bhkv', kc * jnp.exp(b_last[:, None] - bcum), ic))
        return state, o_inter + o_intra

    state0 = jnp.zeros((bsz, HG_HEADS, HG_HEAD_DIM, HG_HEAD_DIM), f32)
    _, o = lax.scan(step, state0, (blocks(qa), blocks(k), blocks(i), blocks(log_f)))
    o = o.transpose(1, 0, 2, 3, 4).reshape(bsz, seq, HG_HEADS, HG_HEAD_DIM)
    o = rmsnorm(o, norm_gain.reshape(HG_HEADS, HG_HEAD_DIM))
    o = o.reshape(bsz, seq, HG_WIDTH) * jax.nn.silu(g.astype(f32))
    return o.astype(q.dtype)


def conv_ffn(u, w_up, w_conv, b_conv, w_down):
    seq = u.shape[1]
    h = u @ w_up
    hp = jnp.pad(h, ((0, 0), (CONV_W - 1, 0), (0, 0)))
    hc = hp[:, 0:seq] * w_conv[0]
    for j in range(1, CONV_W):
        hc = hc + hp[:, j:j + seq] * w_conv[j]
    hc = hc + b_conv
    gate, val = jnp.split(hc, 2, axis=-1)
    return (jax.nn.silu(gate) * val) @ w_down


def _fwd_setup_inputs(seed: int = 0) -> dict:
    key = jax.random.key(seed)
    ks = jax.random.split(key, 24)
    f32 = jnp.float32
    L = DEPTH
    nrm = lambda k, shp, s: jax.random.normal(k, shp, f32) * s
    x = jax.random.normal(ks[0], (BATCH, SEQ, D_MODEL), f32)
    g_mix = 1.0 + nrm(ks[1], (L, D_MODEL), 0.01)
    w_in = nrm(ks[2], (L, D_MODEL, N_IN), D_MODEL ** -0.5)
    s5_a_re = -0.5 * (1.0 + nrm(ks[3], (L, S5_GROUPS, S5_STATE), 0.02))
    s5_a_im = jnp.broadcast_to(jnp.pi * jnp.arange(S5_STATE, dtype=f32), (L, S5_GROUPS, S5_STATE)) + nrm(ks[4], (L, S5_GROUPS, S5_STATE), 0.01)
    s5_log_dt = jax.random.uniform(ks[5], (L, S5_GROUPS), f32, math.log(1e-3), math.log(1e-1))
    s5_b_re = nrm(ks[6], (L, S5_GROUPS, S5_STATE, S5_GROUP), (2 * S5_GROUP) ** -0.5)
    s5_b_im = nrm(ks[7], (L, S5_GROUPS, S5_STATE, S5_GROUP), (2 * S5_GROUP) ** -0.5)
    s5_c_re = nrm(ks[8], (L, S5_GROUPS, S5_GROUP, S5_STATE), (2 * S5_STATE) ** -0.5)
    s5_c_im = nrm(ks[9], (L, S5_GROUPS, S5_GROUP, S5_STATE), (2 * S5_STATE) ** -0.5)
    s5_d = nrm(ks[10], (L, S5_WIDTH), 1.0)
    w_glu = nrm(ks[11], (L, S5_WIDTH, S5_WIDTH), S5_WIDTH ** -0.5)
    b_glu = nrm(ks[12], (L, S5_WIDTH), 0.01)
    hg_lb_logits = nrm(ks[13], (L + 1, HG_WIDTH), 0.1)
    hg_norm_gain = 1.0 + nrm(ks[14], (L, HG_WIDTH), 0.01)
    w_pa = nrm(ks[15], (L, S5_WIDTH, D_MODEL), S5_WIDTH ** -0.5)
    w_pb = nrm(ks[16], (L, HG_WIDTH, D_MODEL), HG_WIDTH ** -0.5)
    w_out = nrm(ks[17], (L, D_MODEL, D_MODEL), D_MODEL ** -0.5)
    g_ffn = 1.0 + nrm(ks[18], (L, D_MODEL), 0.01)
    w_up = nrm(ks[19], (L, D_MODEL, 2 * D_FF), D_MODEL ** -0.5)
    w_conv = nrm(ks[20], (L, CONV_W, 2 * D_FF), CONV_W ** -0.5)
    b_conv = nrm(ks[21], (L, 2 * D_FF), 0.01)
    w_down = nrm(ks[22], (L, D_FF, D_MODEL), D_FF ** -0.5)
    g_final = 1.0 + nrm(ks[23], (D_MODEL,), 0.01)
    return {"x": x, "g_mix": g_mix, "w_in": w_in, "s5_a_re": s5_a_re, "s5_a_im": s5_a_im,
            "s5_log_dt": s5_log_dt, "s5_b_re": s5_b_re, "s5_b_im": s5_b_im, "s5_c_re": s5_c_re,
            "s5_c_im": s5_c_im, "s5_d": s5_d, "w_glu": w_glu, "b_glu": b_glu,
            "hg_lb_logits": hg_lb_logits, "hg_norm_gain": hg_norm_gain, "w_pa": w_pa, "w_pb": w_pb,
            "w_out": w_out, "g_ffn": g_ffn, "w_up": w_up, "w_conv": w_conv, "b_conv": b_conv,
            "w_down": w_down, "g_final": g_final}


def _fwd_reference(x, g_mix, w_in, s5_a_re, s5_a_im, s5_log_dt, s5_b_re, s5_b_im, s5_c_re, s5_c_im,
              s5_d, w_glu, b_glu, hg_lb_logits, hg_norm_gain, w_pa, w_pb, w_out, g_ffn, w_up,
              w_conv, b_conv, w_down, g_final):
    lb_all = jnp.cumsum(jax.nn.softmax(hg_lb_logits.astype(jnp.float32), axis=0), axis=0)
    splits = [S5_WIDTH + j * HG_WIDTH for j in range(5)] + [S5_WIDTH + 4 * HG_WIDTH + D_MODEL]
    for l in range(DEPTH):
        u = rmsnorm(x, g_mix[l])
        z = u @ w_in[l]
        za, zq, zf, zi, zg, zga, zgb = jnp.split(z, splits, axis=-1)
        ya = s5_mixer(za, s5_a_re[l], s5_a_im[l], s5_log_dt[l], s5_b_re[l], s5_b_im[l],
                      s5_c_re[l], s5_c_im[l], s5_d[l])
        ya = jax.nn.gelu(ya)
        ya = ya * jax.nn.sigmoid(ya @ w_glu[l] + b_glu[l])
        yb = hgrn2_mixer(zq, zf, zi, zg, lb_all[l], hg_norm_gain[l])
        m = jax.nn.sigmoid(zga) * (ya @ w_pa[l]) + jax.nn.sigmoid(zgb) * (yb @ w_pb[l])
        x = x + m @ w_out[l]
        x = x + conv_ffn(rmsnorm(x, g_ffn[l]), w_up[l], w_conv[l], b_conv[l], w_down[l])
    return rmsnorm(x, g_final)


import jax as _jax
import jax.numpy as _jnp

TWIN_FORMAT = 'train_step'
FWD_PARAMS = ['x', 'g_mix', 'w_in', 's5_a_re', 's5_a_im', 's5_log_dt', 's5_b_re', 's5_b_im', 's5_c_re', 's5_c_im', 's5_d', 'w_glu', 'b_glu', 'hg_lb_logits', 'hg_norm_gain', 'w_pa', 'w_pb', 'w_out', 'g_ffn', 'w_up', 'w_conv', 'b_conv', 'w_down', 'g_final']
TWIN_WEIGHTS = ['g_mix', 'w_in', 's5_a_re', 's5_a_im', 's5_log_dt', 's5_b_re', 's5_b_im', 's5_c_re', 's5_c_im', 's5_d', 'w_glu', 'b_glu', 'hg_lb_logits', 'hg_norm_gain', 'w_pa', 'w_pb', 'w_out', 'g_ffn', 'w_up', 'w_conv', 'b_conv', 'w_down', 'g_final']
TWIN_DIFF_INPUT = 'x'
TWIN_INPUTS = ['x', 'g_mix', 'w_in', 's5_a_re', 's5_a_im', 's5_log_dt', 's5_b_re', 's5_b_im', 's5_c_re', 's5_c_im', 's5_d', 'w_glu', 'b_glu', 'hg_lb_logits', 'hg_norm_gain', 'w_pa', 'w_pb', 'w_out', 'g_ffn', 'w_up', 'w_conv', 'b_conv', 'w_down', 'g_final', 'loss_target', 'm_g_mix', 'm_w_in', 'm_s5_a_re', 'm_s5_a_im', 'm_s5_log_dt', 'm_s5_b_re', 'm_s5_b_im', 'm_s5_c_re', 'm_s5_c_im', 'm_s5_d', 'm_w_glu', 'm_b_glu', 'm_hg_lb_logits', 'm_hg_norm_gain', 'm_w_pa', 'm_w_pb', 'm_w_out', 'm_g_ffn', 'm_w_up', 'm_w_conv', 'm_b_conv', 'm_w_down', 'm_g_final', 'v_g_mix', 'v_w_in', 'v_s5_a_re', 'v_s5_a_im', 'v_s5_log_dt', 'v_s5_b_re', 'v_s5_b_im', 'v_s5_c_re', 'v_s5_c_im', 'v_s5_d', 'v_w_glu', 'v_b_glu', 'v_hg_lb_logits', 'v_hg_norm_gain', 'v_w_pa', 'v_w_pb', 'v_w_out', 'v_g_ffn', 'v_w_up', 'v_w_conv', 'v_b_conv', 'v_w_down', 'v_g_final']
TWIN_OUTPUTS = ['loss', 'grad_x', 'grad_g_mix', 'grad_w_in', 'grad_s5_a_re', 'grad_s5_a_im', 'grad_s5_log_dt', 'grad_s5_b_re', 'grad_s5_b_im', 'grad_s5_c_re', 'grad_s5_c_im', 'grad_s5_d', 'grad_w_glu', 'grad_b_glu', 'grad_hg_lb_logits', 'grad_hg_norm_gain', 'grad_w_pa', 'grad_w_pb', 'grad_w_out', 'grad_g_ffn', 'grad_w_up', 'grad_w_conv', 'grad_b_conv', 'grad_w_down', 'grad_g_final', 'delta_g_mix', 'delta_w_in', 'delta_s5_a_re', 'delta_s5_a_im', 'delta_s5_log_dt', 'delta_s5_b_re', 'delta_s5_b_im', 'delta_s5_c_re', 'delta_s5_c_im', 'delta_s5_d', 'delta_w_glu', 'delta_b_glu', 'delta_hg_lb_logits', 'delta_hg_norm_gain', 'delta_w_pa', 'delta_w_pb', 'delta_w_out', 'delta_g_ffn', 'delta_w_up', 'delta_w_conv', 'delta_b_conv', 'delta_w_down', 'delta_g_final', 'new_m_g_mix', 'new_m_w_in', 'new_m_s5_a_re', 'new_m_s5_a_im', 'new_m_s5_log_dt', 'new_m_s5_b_re', 'new_m_s5_b_im', 'new_m_s5_c_re', 'new_m_s5_c_im', 'new_m_s5_d', 'new_m_w_glu', 'new_m_b_glu', 'new_m_hg_lb_logits', 'new_m_hg_norm_gain', 'new_m_w_pa', 'new_m_w_pb', 'new_m_w_out', 'new_m_g_ffn', 'new_m_w_up', 'new_m_w_conv', 'new_m_b_conv', 'new_m_w_down', 'new_m_g_final', 'new_v_g_mix', 'new_v_w_in', 'new_v_s5_a_re', 'new_v_s5_a_im', 'new_v_s5_log_dt', 'new_v_s5_b_re', 'new_v_s5_b_im', 'new_v_s5_c_re', 'new_v_s5_c_im', 'new_v_s5_d', 'new_v_w_glu', 'new_v_b_glu', 'new_v_hg_lb_logits', 'new_v_hg_norm_gain', 'new_v_w_pa', 'new_v_w_pb', 'new_v_w_out', 'new_v_g_ffn', 'new_v_w_up', 'new_v_w_conv', 'new_v_b_conv', 'new_v_w_down', 'new_v_g_final']
TWIN_LEAF_KINDS = {'loss': 'loss', 'grad_x': 'grad_x', 'grad_g_mix': 'grad_w', 'grad_w_in': 'grad_w', 'grad_s5_a_re': 'grad_w', 'grad_s5_a_im': 'grad_w', 'grad_s5_log_dt': 'grad_w', 'grad_s5_b_re': 'grad_w', 'grad_s5_b_im': 'grad_w', 'grad_s5_c_re': 'grad_w', 'grad_s5_c_im': 'grad_w', 'grad_s5_d': 'grad_w', 'grad_w_glu': 'grad_w', 'grad_b_glu': 'grad_w', 'grad_hg_lb_logits': 'grad_w', 'grad_hg_norm_gain': 'grad_w', 'grad_w_pa': 'grad_w', 'grad_w_pb': 'grad_w', 'grad_w_out': 'grad_w', 'grad_g_ffn': 'grad_w', 'grad_w_up': 'grad_w', 'grad_w_conv': 'grad_w', 'grad_b_conv': 'grad_w', 'grad_w_down': 'grad_w', 'grad_g_final': 'grad_w', 'delta_g_mix': 'delta_w', 'delta_w_in': 'delta_w', 'delta_s5_a_re': 'delta_w', 'delta_s5_a_im': 'delta_w', 'delta_s5_log_dt': 'delta_w', 'delta_s5_b_re': 'delta_w', 'delta_s5_b_im': 'delta_w', 'delta_s5_c_re': 'delta_w', 'delta_s5_c_im': 'delta_w', 'delta_s5_d': 'delta_w', 'delta_w_glu': 'delta_w', 'delta_b_glu': 'delta_w', 'delta_hg_lb_logits': 'delta_w', 'delta_hg_norm_gain': 'delta_w', 'delta_w_pa': 'delta_w', 'delta_w_pb': 'delta_w', 'delta_w_out': 'delta_w', 'delta_g_ffn': 'delta_w', 'delta_w_up': 'delta_w', 'delta_w_conv': 'delta_w', 'delta_b_conv': 'delta_w', 'delta_w_down': 'delta_w', 'delta_g_final': 'delta_w', 'new_m_g_mix': 'new_m', 'new_m_w_in': 'new_m', 'new_m_s5_a_re': 'new_m', 'new_m_s5_a_im': 'new_m', 'new_m_s5_log_dt': 'new_m', 'new_m_s5_b_re': 'new_m', 'new_m_s5_b_im': 'new_m', 'new_m_s5_c_re': 'new_m', 'new_m_s5_c_im': 'new_m', 'new_m_s5_d': 'new_m', 'new_m_w_glu': 'new_m', 'new_m_b_glu': 'new_m', 'new_m_hg_lb_logits': 'new_m', 'new_m_hg_norm_gain': 'new_m', 'new_m_w_pa': 'new_m', 'new_m_w_pb': 'new_m', 'new_m_w_out': 'new_m', 'new_m_g_ffn': 'new_m', 'new_m_w_up': 'new_m', 'new_m_w_conv': 'new_m', 'new_m_b_conv': 'new_m', 'new_m_w_down': 'new_m', 'new_m_g_final': 'new_m', 'new_v_g_mix': 'new_v', 'new_v_w_in': 'new_v', 'new_v_s5_a_re': 'new_v', 'new_v_s5_a_im': 'new_v', 'new_v_s5_log_dt': 'new_v', 'new_v_s5_b_re': 'new_v', 'new_v_s5_b_im': 'new_v', 'new_v_s5_c_re': 'new_v', 'new_v_s5_c_im': 'new_v', 'new_v_s5_d': 'new_v', 'new_v_w_glu': 'new_v', 'new_v_b_glu': 'new_v', 'new_v_hg_lb_logits': 'new_v', 'new_v_hg_norm_gain': 'new_v', 'new_v_w_pa': 'new_v', 'new_v_w_pb': 'new_v', 'new_v_w_out': 'new_v', 'new_v_g_ffn': 'new_v', 'new_v_w_up': 'new_v', 'new_v_w_conv': 'new_v', 'new_v_b_conv': 'new_v', 'new_v_w_down': 'new_v', 'new_v_g_final': 'new_v'}


def _forward(args):
    return _fwd_reference(*[args[k] for k in FWD_PARAMS])


def _output_shape():
    out = _jax.eval_shape(lambda: _forward(_fwd_setup_inputs(0)))
    return out.shape, out.dtype

N_MICROBATCH = 1
ADAM_LR = 0.001
ADAM_B1 = 0.9
ADAM_B2 = 0.999
ADAM_EPS = 1e-08
ADAM_WD = 0.01
ADAM_STEP = 10
PER_EXAMPLE_BATCH_AXIS = {'x': 0, 'loss_target': 0}
SHARED_INPUTS = []
_WEIGHT_DTYPES = {'g_mix': _jnp.float32, 'w_in': _jnp.float32, 's5_a_re': _jnp.float32, 's5_a_im': _jnp.float32, 's5_log_dt': _jnp.float32, 's5_b_re': _jnp.float32, 's5_b_im': _jnp.float32, 's5_c_re': _jnp.float32, 's5_c_im': _jnp.float32, 's5_d': _jnp.float32, 'w_glu': _jnp.float32, 'b_glu': _jnp.float32, 'hg_lb_logits': _jnp.float32, 'hg_norm_gain': _jnp.float32, 'w_pa': _jnp.float32, 'w_pb': _jnp.float32, 'w_out': _jnp.float32, 'g_ffn': _jnp.float32, 'w_up': _jnp.float32, 'w_conv': _jnp.float32, 'b_conv': _jnp.float32, 'w_down': _jnp.float32, 'g_final': _jnp.float32}
MOMENT_SCALE = {'g_mix': 9.659296e-02, 'w_in': 4.678006e-02, 's5_a_re': 2.413377e-03, 's5_a_im': 2.853812e-03, 's5_log_dt': 1.272032e+00, 's5_b_re': 1.869921e-03, 's5_b_im': 1.803229e-03, 's5_c_re': 3.552808e-03, 's5_c_im': 3.697594e-03, 's5_d': 5.681358e-02, 'w_glu': 1.566587e-02, 'b_glu': 2.211446e-02, 'hg_lb_logits': 8.766811e-03, 'hg_norm_gain': 8.458009e-02, 'w_pa': 3.652053e-02, 'w_pb': 6.003515e-02, 'w_out': 7.013830e-02, 'g_ffn': 1.313171e-01, 'w_up': 5.353316e-02, 'w_conv': 5.484242e-02, 'b_conv': 5.320261e-02, 'w_down': 8.734193e-02, 'g_final': 3.201198e+01}


def _to_microbatches(a, axis):
    t = _jnp.moveaxis(a, axis, 0)
    t = t.reshape((N_MICROBATCH, t.shape[0] // N_MICROBATCH) + t.shape[1:])
    return _jnp.moveaxis(t, 1, axis + 1)


def setup_inputs(seed: int = 0) -> dict:
    inp = _fwd_setup_inputs(seed)
    key = _jax.random.fold_in(_jax.random.key(seed), 7919)
    shape, _ = _output_shape()
    out = dict(inp)
    out["loss_target"] = _jax.random.normal(_jax.random.fold_in(key, 0), shape, _jnp.float32)
    for i, name in enumerate(TWIN_WEIGHTS):
        w = inp[name].astype(_jnp.float32)
        if MOMENT_SCALE is None:
            s = _jnp.sqrt(_jnp.mean(_jnp.square(w)) + 1e-30)
        else:
            s = MOMENT_SCALE[name]
        km, kv = _jax.random.split(_jax.random.fold_in(key, i + 1))
        out[name] = w
        out["m_" + name] = s * _jax.random.normal(km, w.shape, _jnp.float32)
        out["v_" + name] = (s * s) * _jax.random.uniform(kv, w.shape, _jnp.float32, 0.5, 1.5)
    if N_MICROBATCH > 1:
        for name, axis in PER_EXAMPLE_BATCH_AXIS.items():
            out[name] = _to_microbatches(out[name], axis)
    return {'x': out['x'], 'g_mix': out['g_mix'], 'w_in': out['w_in'], 's5_a_re': out['s5_a_re'], 's5_a_im': out['s5_a_im'], 's5_log_dt': out['s5_log_dt'], 's5_b_re': out['s5_b_re'], 's5_b_im': out['s5_b_im'], 's5_c_re': out['s5_c_re'], 's5_c_im': out['s5_c_im'], 's5_d': out['s5_d'], 'w_glu': out['w_glu'], 'b_glu': out['b_glu'], 'hg_lb_logits': out['hg_lb_logits'], 'hg_norm_gain': out['hg_norm_gain'], 'w_pa': out['w_pa'], 'w_pb': out['w_pb'], 'w_out': out['w_out'], 'g_ffn': out['g_ffn'], 'w_up': out['w_up'], 'w_conv': out['w_conv'], 'b_conv': out['b_conv'], 'w_down': out['w_down'], 'g_final': out['g_final'], 'loss_target': out['loss_target'], 'm_g_mix': out['m_g_mix'], 'm_w_in': out['m_w_in'], 'm_s5_a_re': out['m_s5_a_re'], 'm_s5_a_im': out['m_s5_a_im'], 'm_s5_log_dt': out['m_s5_log_dt'], 'm_s5_b_re': out['m_s5_b_re'], 'm_s5_b_im': out['m_s5_b_im'], 'm_s5_c_re': out['m_s5_c_re'], 'm_s5_c_im': out['m_s5_c_im'], 'm_s5_d': out['m_s5_d'], 'm_w_glu': out['m_w_glu'], 'm_b_glu': out['m_b_glu'], 'm_hg_lb_logits': out['m_hg_lb_logits'], 'm_hg_norm_gain': out['m_hg_norm_gain'], 'm_w_pa': out['m_w_pa'], 'm_w_pb': out['m_w_pb'], 'm_w_out': out['m_w_out'], 'm_g_ffn': out['m_g_ffn'], 'm_w_up': out['m_w_up'], 'm_w_conv': out['m_w_conv'], 'm_b_conv': out['m_b_conv'], 'm_w_down': out['m_w_down'], 'm_g_final': out['m_g_final'], 'v_g_mix': out['v_g_mix'], 'v_w_in': out['v_w_in'], 'v_s5_a_re': out['v_s5_a_re'], 'v_s5_a_im': out['v_s5_a_im'], 'v_s5_log_dt': out['v_s5_log_dt'], 'v_s5_b_re': out['v_s5_b_re'], 'v_s5_b_im': out['v_s5_b_im'], 'v_s5_c_re': out['v_s5_c_re'], 'v_s5_c_im': out['v_s5_c_im'], 'v_s5_d': out['v_s5_d'], 'v_w_glu': out['v_w_glu'], 'v_b_glu': out['v_b_glu'], 'v_hg_lb_logits': out['v_hg_lb_logits'], 'v_hg_norm_gain': out['v_hg_norm_gain'], 'v_w_pa': out['v_w_pa'], 'v_w_pb': out['v_w_pb'], 'v_w_out': out['v_w_out'], 'v_g_ffn': out['v_g_ffn'], 'v_w_up': out['v_w_up'], 'v_w_conv': out['v_w_conv'], 'v_b_conv': out['v_b_conv'], 'v_w_down': out['v_w_down'], 'v_g_final': out['v_g_final']}


def _loss(weights, diff, rest, loss_target):
    with _jax.named_scope("forward"):
        args = {**rest, TWIN_DIFF_INPUT: diff, **{k: w.astype(_WEIGHT_DTYPES[k]) for k, w in weights.items()}}
        y = _forward(args)
    with _jax.named_scope("loss_head"):
        err = _jnp.square(y.astype(_jnp.float32) - loss_target)
        return 0.5 * _jnp.sum(_jnp.mean(err, axis=-1)) if err.ndim else 0.5 * err


def _adamw(w, g, m, v):
    m = ADAM_B1 * m + (1.0 - ADAM_B1) * g
    v = ADAM_B2 * v + (1.0 - ADAM_B2) * _jnp.square(g)
    m_hat = m / (1.0 - ADAM_B1 ** ADAM_STEP)
    v_hat = v / (1.0 - ADAM_B2 ** ADAM_STEP)
    delta = -ADAM_LR * (m_hat / (_jnp.sqrt(v_hat) + ADAM_EPS) + ADAM_WD * w)
    return delta, m, v


def reference(x, g_mix, w_in, s5_a_re, s5_a_im, s5_log_dt, s5_b_re, s5_b_im, s5_c_re, s5_c_im, s5_d, w_glu, b_glu, hg_lb_logits, hg_norm_gain, w_pa, w_pb, w_out, g_ffn, w_up, w_conv, b_conv, w_down, g_final, loss_target, m_g_mix, m_w_in, m_s5_a_re, m_s5_a_im, m_s5_log_dt, m_s5_b_re, m_s5_b_im, m_s5_c_re, m_s5_c_im, m_s5_d, m_w_glu, m_b_glu, m_hg_lb_logits, m_hg_norm_gain, m_w_pa, m_w_pb, m_w_out, m_g_ffn, m_w_up, m_w_conv, m_b_conv, m_w_down, m_g_final, v_g_mix, v_w_in, v_s5_a_re, v_s5_a_im, v_s5_log_dt, v_s5_b_re, v_s5_b_im, v_s5_c_re, v_s5_c_im, v_s5_d, v_w_glu, v_b_glu, v_hg_lb_logits, v_hg_norm_gain, v_w_pa, v_w_pb, v_w_out, v_g_ffn, v_w_up, v_w_conv, v_b_conv, v_w_down, v_g_final):
    given = dict(x=x, g_mix=g_mix, w_in=w_in, s5_a_re=s5_a_re, s5_a_im=s5_a_im, s5_log_dt=s5_log_dt, s5_b_re=s5_b_re, s5_b_im=s5_b_im, s5_c_re=s5_c_re, s5_c_im=s5_c_im, s5_d=s5_d, w_glu=w_glu, b_glu=b_glu, hg_lb_logits=hg_lb_logits, hg_norm_gain=hg_norm_gain, w_pa=w_pa, w_pb=w_pb, w_out=w_out, g_ffn=g_ffn, w_up=w_up, w_conv=w_conv, b_conv=b_conv, w_down=w_down, g_final=g_final, loss_target=loss_target, m_g_mix=m_g_mix, m_w_in=m_w_in, m_s5_a_re=m_s5_a_re, m_s5_a_im=m_s5_a_im, m_s5_log_dt=m_s5_log_dt, m_s5_b_re=m_s5_b_re, m_s5_b_im=m_s5_b_im, m_s5_c_re=m_s5_c_re, m_s5_c_im=m_s5_c_im, m_s5_d=m_s5_d, m_w_glu=m_w_glu, m_b_glu=m_b_glu, m_hg_lb_logits=m_hg_lb_logits, m_hg_norm_gain=m_hg_norm_gain, m_w_pa=m_w_pa, m_w_pb=m_w_pb, m_w_out=m_w_out, m_g_ffn=m_g_ffn, m_w_up=m_w_up, m_w_conv=m_w_conv, m_b_conv=m_b_conv, m_w_down=m_w_down, m_g_final=m_g_final, v_g_mix=v_g_mix, v_w_in=v_w_in, v_s5_a_re=v_s5_a_re, v_s5_a_im=v_s5_a_im, v_s5_log_dt=v_s5_log_dt, v_s5_b_re=v_s5_b_re, v_s5_b_im=v_s5_b_im, v_s5_c_re=v_s5_c_re, v_s5_c_im=v_s5_c_im, v_s5_d=v_s5_d, v_w_glu=v_w_glu, v_b_glu=v_b_glu, v_hg_lb_logits=v_hg_lb_logits, v_hg_norm_gain=v_hg_norm_gain, v_w_pa=v_w_pa, v_w_pb=v_w_pb, v_w_out=v_w_out, v_g_ffn=v_g_ffn, v_w_up=v_w_up, v_w_conv=v_w_conv, v_b_conv=v_b_conv, v_w_down=v_w_down, v_g_final=v_g_final)
    weights = {n: given[n] for n in TWIN_WEIGHTS}
    shared = {n: given[n] for n in SHARED_INPUTS}
    per_example = {n: given[n] for n in ['x']}
    grad_fn = _jax.value_and_grad(_loss, argnums=(0, 1))

    def one_microbatch(ex, loss_target):
        ex = dict(ex)
        diff = ex.pop(TWIN_DIFF_INPUT)
        return grad_fn(weights, diff, {**shared, **ex}, loss_target)

    if N_MICROBATCH == 1:
        loss, (grad_w, grad_x) = one_microbatch(per_example, given["loss_target"])
    else:
        def body(carry, xs):
            loss_sum, grad_sum = carry
            l_k, (gw_k, gx_k) = one_microbatch(xs[0], xs[1])
            with _jax.named_scope("update"):
                return (loss_sum + l_k, _jax.tree.map(_jnp.add, grad_sum, gw_k)), gx_k

        init = (_jnp.zeros((), _jnp.float32), _jax.tree.map(_jnp.zeros_like, weights))
        (loss, grad_w), grad_x = _jax.lax.scan(body, init, (per_example, given["loss_target"]))
    with _jax.named_scope("update"):
        delta_w, new_m, new_v = {}, {}, {}
        for n in TWIN_WEIGHTS:
            delta_w[n], new_m[n], new_v[n] = _adamw(weights[n], grad_w[n], given["m_" + n], given["v_" + n])
    return (loss, grad_x, *[grad_w[n] for n in TWIN_WEIGHTS], *[delta_w[n] for n in TWIN_WEIGHTS],
            *[new_m[n] for n in TWIN_WEIGHTS], *[new_v[n] for n in TWIN_WEIGHTS])
```

```python
import functools
import math

import jax
import jax.numpy as jnp
from jax import lax
from jax.experimental import pallas as pl
from jax.experimental.pallas import tpu as pltpu

F32 = jnp.float32
BF16 = jnp.bfloat16
MESH = pl.DeviceIdType.MESH

EPS = 1e-6
S5_GROUP = 16
S5_STATE = 64
S5_BLOCK_GROUPS = 8
HEAD = 128
CHUNK = 64
CONV_W = 3
LANES = 128
SUBLANES = 8
GATE_BLOCK = 512
VMEM_LIMIT_BYTES = 56 * 1024 * 1024

ADAM_LR = 0.001
ADAM_B1 = 0.9
ADAM_B2 = 0.999
ADAM_EPS = 1e-08
ADAM_WD = 0.01
ADAM_STEP = 10

N_CHIPS = 4
N_DEV = 8


def _params(*sem):
    return pltpu.CompilerParams(dimension_semantics=sem, vmem_limit_bytes=VMEM_LIMIT_BYTES)


def _row_tile(rows, cap):
    if rows <= cap:
        return rows
    for t in range(cap - cap % 8, 7, -8):
        if rows % t == 0:
            return t
    raise ValueError(f"no row tile for {rows}")


def _dot(a, b):
    return jnp.dot(a.astype(BF16), b.astype(BF16), preferred_element_type=F32)


def _dot_nt(a, b):
    return lax.dot_general(a.astype(BF16), b.astype(BF16), (((1,), (1,)), ((), ())), preferred_element_type=F32)


def _dot_tn(a, b):
    return lax.dot_general(a.astype(BF16), b.astype(BF16), (((0,), (0,)), ((), ())), preferred_element_type=F32)


def _sigmoid(x):
    return 1.0 / (1.0 + jnp.exp(-x))


_GELU_C = math.sqrt(2.0 / math.pi)


def _gelu(x):
    return 0.5 * x * (1.0 + jnp.tanh(_GELU_C * (x + 0.044715 * x * x * x)))


def _gelu_grad(x):
    th = jnp.tanh(_GELU_C * (x + 0.044715 * x * x * x))
    return 0.5 * (1.0 + th) + 0.5 * x * (1.0 - th * th) * _GELU_C * (1.0 + 3.0 * 0.044715 * x * x)


def _rowwise(name, fn, ins, outs, accs=(), *, rows, tm, ncol=1):
    n_in, n_out = len(ins), len(outs)

    def body(*refs):
        res = fn(*[r[...] for r in refs[:n_in]])
        for r, v in zip(refs[n_in:n_in + n_out], res[:n_out]):
            r[...] = v.astype(r.dtype)
        first = pl.program_id(1) == 0
        for r, v in zip(refs[n_in + n_out:], res[n_out:]):
            @pl.when(first)
            def _():
                r[...] = v

            @pl.when(jnp.logical_not(first))
            def _():
                r[...] += v

    in_specs = []
    for _, width, base, kind in ins:
        if kind == "row":
            in_specs.append(pl.BlockSpec((tm, width), lambda j, i, b=base: (i, b + j)))
        else:
            in_specs.append(pl.BlockSpec((1, width), lambda j, i, b=base: (0, b + j)))
    out_specs = [pl.BlockSpec((tm, width), lambda j, i: (i, j)) for _, width, _ in outs]
    out_specs += [pl.BlockSpec((1, width), lambda j, i: (0, j)) for _, width in accs]
    out_shape = [jax.ShapeDtypeStruct((rows, total), dt) for total, _, dt in outs]
    out_shape += [jax.ShapeDtypeStruct((1, total), F32) for total, _ in accs]
    return pl.pallas_call(
        body, name=name, grid=(ncol, rows // tm), in_specs=in_specs, out_specs=out_specs, out_shape=out_shape,
        compiler_params=_params("arbitrary", "arbitrary"),
    )(*[a for a, _, _, _ in ins])


def _mm(name, a, b, *, mode, grid, a_spec, b_spec, o_spec, out_shape, acc_shape, res=None, res_spec=None):
    nk = grid[2]
    dot = {"nn": _dot, "nt": _dot_nt, "tn": _dot_tn}[mode]

    def body(*refs):
        if res is None:
            a_ref, b_ref, o_ref, acc_ref = refs
        else:
            a_ref, b_ref, r_ref, o_ref, acc_ref = refs
        k = pl.program_id(2)

        @pl.when(k == 0)
        def _():
            acc_ref[...] = jnp.zeros_like(acc_ref)

        acc_ref[...] += dot(a_ref[...], b_ref[...])

        @pl.when(k == nk - 1)
        def _():
            v = acc_ref[...]
            if res is not None:
                v = v + r_ref[...]
            o_ref[...] = v.astype(o_ref.dtype)

    operands = [a, b] + ([] if res is None else [res])
    in_specs = [a_spec, b_spec] + ([] if res is None else [res_spec])
    return pl.pallas_call(
        body, name=name, grid=grid, in_specs=in_specs, out_specs=o_spec, out_shape=out_shape,
        scratch_shapes=[pltpu.VMEM(acc_shape, F32)],
        compiler_params=_params("arbitrary", "arbitrary", "arbitrary"),
    )(*operands)


def _mm_fwd_cols(name, a, w3, out_dtype=F32, tm_cap=512):
    t, k = a.shape
    ns = w3.shape[2]
    tm = _row_tile(t, tm_cap)
    return _mm(name, a, w3, mode="nn", grid=(t // tm, N_CHIPS, 1),
               a_spec=pl.BlockSpec((tm, k), lambda i, j, kk: (i, 0)),
               b_spec=pl.BlockSpec((None, k, ns), lambda i, j, kk: (j, 0, 0)),
               o_spec=pl.BlockSpec((tm, ns), lambda i, j, kk: (i, j)),
               out_shape=jax.ShapeDtypeStruct((t, N_CHIPS * ns), out_dtype), acc_shape=(tm, ns))


def _mm_bwd_cols(name, d, w3, out_dtype=F32, tm_cap=512):
    t = d.shape[0]
    k, ns = w3.shape[1], w3.shape[2]
    tm = _row_tile(t, tm_cap)
    return _mm(name, d, w3, mode="nt", grid=(t // tm, 1, N_CHIPS),
               a_spec=pl.BlockSpec((tm, ns), lambda i, j, kk: (i, kk)),
               b_spec=pl.BlockSpec((None, k, ns), lambda i, j, kk: (kk, 0, 0)),
               o_spec=pl.BlockSpec((tm, k), lambda i, j, kk: (i, 0)),
               out_shape=jax.ShapeDtypeStruct((t, k), out_dtype), acc_shape=(tm, k))


def _mm_wgrad_cols(name, a, d, tk_cap=512):
    t, k = a.shape
    ns = d.shape[1] // N_CHIPS
    tk = _row_tile(t, tk_cap)
    return _mm(name, a, d, mode="tn", grid=(N_CHIPS, 1, t // tk),
               a_spec=pl.BlockSpec((tk, k), lambda j, i, kk: (kk, 0)),
               b_spec=pl.BlockSpec((tk, ns), lambda j, i, kk: (kk, j)),
               o_spec=pl.BlockSpec((None, k, ns), lambda j, i, kk: (j, 0, 0)),
               out_shape=jax.ShapeDtypeStruct((N_CHIPS, k, ns), BF16), acc_shape=(k, ns))


def _mm_fwd_rows(name, a, w, res=None, out_dtype=F32, tm_cap=512, tk_cap=1408):
    t, k = a.shape
    n = w.shape[1]
    tm = _row_tile(t, tm_cap)
    tk = k if k <= tk_cap else tk_cap
    assert k % tk == 0
    return _mm(name, a, w, mode="nn", grid=(t // tm, 1, k // tk),
               a_spec=pl.BlockSpec((tm, tk), lambda i, j, kk: (i, kk)),
               b_spec=pl.BlockSpec((tk, n), lambda i, j, kk: (kk, 0)),
               o_spec=pl.BlockSpec((tm, n), lambda i, j, kk: (i, 0)),
               out_shape=jax.ShapeDtypeStruct((t, n), out_dtype), acc_shape=(tm, n),
               res=res, res_spec=None if res is None else pl.BlockSpec((tm, n), lambda i, j, kk: (i, 0)))


def _mm_bwd_rows(name, d, w, out_dtype=F32, tm_cap=512, tn_cap=1408):
    t, n = d.shape
    k = w.shape[0]
    tm = _row_tile(t, tm_cap)
    tn = k if k <= tn_cap else tn_cap
    assert k % tn == 0
    return _mm(name, d, w, mode="nt", grid=(t // tm, k // tn, 1),
               a_spec=pl.BlockSpec((tm, n), lambda i, j, kk: (i, 0)),
               b_spec=pl.BlockSpec((tn, n), lambda i, j, kk: (j, 0)),
               o_spec=pl.BlockSpec((tm, tn), lambda i, j, kk: (i, j)),
               out_shape=jax.ShapeDtypeStruct((t, k), out_dtype), acc_shape=(tm, tn))


def _mm_wgrad_rows(name, a, d, tk_cap=512):
    t, k = a.shape
    n = d.shape[1]
    nblk = next(b for b in (4, 2, 1) if (k // b) % LANES == 0)
    ks = k // nblk
    tk = _row_tile(t, tk_cap)
    return _mm(name, a, d, mode="tn", grid=(nblk, 1, t // tk),
               a_spec=pl.BlockSpec((tk, ks), lambda j, i, kk: (kk, j)),
               b_spec=pl.BlockSpec((tk, n), lambda j, i, kk: (kk, 0)),
               o_spec=pl.BlockSpec((ks, n), lambda j, i, kk: (j, 0)),
               out_shape=jax.ShapeDtypeStruct((k, n), BF16), acc_shape=(ks, n))


def _s5_discretize(a_re, a_im, log_dt, b_re, b_im):
    dt = jnp.exp(log_dt)[:, None]
    mag = jnp.exp(a_re * dt)
    ang = a_im * dt
    lb_re = mag * jnp.cos(ang)
    lb_im = mag * jnp.sin(ang)
    den = a_re * a_re + a_im * a_im
    n_re = lb_re - 1.0
    n_im = lb_im
    co_re = ((n_re * a_re + n_im * a_im) / den)[..., None]
    co_im = ((n_im * a_re - n_re * a_im) / den)[..., None]
    bb_re = co_re * b_re - co_im * b_im
    bb_im = co_re * b_im + co_im * b_re
    return lb_re, lb_im, bb_re, bb_im


def _s5_in_blocks(bb):
    g = bb.shape[0]
    nb = g // S5_BLOCK_GROUPS
    t = bb.reshape(nb, S5_BLOCK_GROUPS, S5_STATE, S5_GROUP).transpose(0, 1, 3, 2)
    eye = jnp.eye(S5_BLOCK_GROUPS, dtype=bb.dtype)
    full = t[:, :, :, None, :] * eye[None, :, None, :, None]
    return full.reshape(nb, S5_BLOCK_GROUPS * S5_GROUP, S5_BLOCK_GROUPS * S5_STATE)


def _s5_in_blocks_diag(blocks):
    nb = blocks.shape[0]
    t = blocks.reshape(nb, S5_BLOCK_GROUPS, S5_GROUP, S5_BLOCK_GROUPS, S5_STATE)
    d = jnp.einsum("bghgp->bghp", t)
    return d.transpose(0, 1, 3, 2).reshape(nb * S5_BLOCK_GROUPS, S5_STATE, S5_GROUP)


def _s5_out_blocks(c):
    g = c.shape[0]
    nb = g // S5_BLOCK_GROUPS
    t = c.reshape(nb, S5_BLOCK_GROUPS, S5_GROUP, S5_STATE).transpose(0, 1, 3, 2)
    eye = jnp.eye(S5_BLOCK_GROUPS, dtype=c.dtype)
    full = t[:, :, :, None, :] * eye[None, :, None, :, None]
    return full.reshape(nb, S5_BLOCK_GROUPS * S5_STATE, S5_BLOCK_GROUPS * S5_GROUP)


def _s5_out_blocks_diag(blocks):
    nb = blocks.shape[0]
    t = blocks.reshape(nb, S5_BLOCK_GROUPS, S5_STATE, S5_BLOCK_GROUPS, S5_GROUP)
    d = jnp.einsum("bgpgh->bgph", t)
    return d.transpose(0, 1, 3, 2).reshape(nb * S5_BLOCK_GROUPS, S5_GROUP, S5_STATE)


def _s5_scan_tables(lr, li, reverse):
    def cmul(a, b):
        return a[0] * b[0] - a[1] * b[1], a[0] * b[1] + a[1] * b[0]

    lam = (lr, -li) if reverse else (lr, li)
    pw = [lam]
    for _ in range(SUBLANES - 1):
        pw.append(cmul(pw[-1], lam))
    sub = jnp.arange(SUBLANES)[:, None]
    rows = []
    for s in (1, 2, 4):
        keep = (sub <= SUBLANES - 1 - s) if reverse else (sub >= s)
        rows.append(jnp.where(keep, pw[s - 1][0][None, :], 0.0))
        rows.append(jnp.where(keep, pw[s - 1][1][None, :], 0.0))
    order = list(range(SUBLANES - 1, -1, -1)) if reverse else list(range(SUBLANES))
    rows.append(jnp.stack([pw[i][0] for i in order]))
    rows.append(jnp.stack([pw[i][1] for i in order]))
    return jnp.concatenate(rows, axis=0)


def _s5_scan(vre_ref, vim_ref, coef_ref, seq, width, reverse, xre_ref=None, xim_ref=None):
    nt = seq // SUBLANES
    sub = lax.broadcasted_iota(jnp.int32, (SUBLANES, LANES), 0)
    sums = []
    for j in range(width // LANES):
        lanes = slice(j * LANES, (j + 1) * LANES)
        co = [coef_ref[SUBLANES * q:SUBLANES * (q + 1), lanes] for q in range(8)]

        def step(k, carry, lanes=lanes, co=co):
            cr, ci = carry[0], carry[1]
            kk = (nt - 1 - k) if reverse else k
            rows = pl.ds(pl.multiple_of(kk * SUBLANES, SUBLANES), SUBLANES)
            vr = vre_ref[rows, lanes]
            vi = vim_ref[rows, lanes]
            for q, s in enumerate((1, 2, 4)):
                sh = SUBLANES - s if reverse else s
                rr = pltpu.roll(vr, sh, 0)
                ri = pltpu.roll(vi, sh, 0)
                ar, ai = co[2 * q], co[2 * q + 1]
                vr, vi = vr + ar * rr - ai * ri, vi + ar * ri + ai * rr
            edge = 0 if reverse else SUBLANES - 1
            cbr = jnp.broadcast_to(cr[edge:edge + 1, :], (SUBLANES, LANES))
            cbi = jnp.broadcast_to(ci[edge:edge + 1, :], (SUBLANES, LANES))
            pr, pi = co[6], co[7]
            vr, vi = vr + pr * cbr - pi * cbi, vi + pr * cbi + pi * cbr
            vre_ref[rows, lanes] = vr
            vim_ref[rows, lanes] = vi
            if xre_ref is None:
                return vr, vi
            nr = jnp.where(sub == SUBLANES - 1, cbr, pltpu.roll(vr, SUBLANES - 1, 0))
            ni = jnp.where(sub == SUBLANES - 1, cbi, pltpu.roll(vi, SUBLANES - 1, 0))
            xr = xre_ref[rows, lanes]
            xi = xim_ref[rows, lanes]
            return vr, vi, carry[2] + nr * xr + ni * xi, carry[3] + ni * xr - nr * xi

        zero = jnp.zeros((SUBLANES, LANES), F32)
        init = (zero, zero) if xre_ref is None else (zero, zero, zero, zero)
        out = lax.fori_loop(0, nt, step, init)
        if xre_ref is not None:
            sums.append((jnp.sum(out[2], axis=0, keepdims=True), jnp.sum(out[3], axis=0, keepdims=True)))
    if xre_ref is None:
        return None
    return jnp.concatenate([jnp.concatenate([s[0] for s in sums], axis=1),
                            jnp.concatenate([s[1] for s in sums], axis=1)], axis=0)


def _s5_fwd(z, bre3, bim3, cre3, cim3, coef, dskip, *, nseq, seq):
    nb = bre3.shape[0]
    ch, ns = bre3.shape[1], bre3.shape[2]

    def body(za_ref, bre_ref, bim_ref, cre_ref, cim_ref, coef_ref, d_ref, y_ref, xre_ref, xim_ref):
        za = za_ref[...]
        xre_ref[...] = _dot(za, bre_ref[...])
        xim_ref[...] = _dot(za, bim_ref[...])
        _s5_scan(xre_ref, xim_ref, coef_ref, seq, ns, False)
        y_ref[...] = _dot(xre_ref[...], cre_ref[...]) - _dot(xim_ref[...], cim_ref[...]) + d_ref[...] * za

    blk3 = lambda r, c: pl.BlockSpec((None, r, c), lambda b, j: (j, 0, 0))
    return pl.pallas_call(
        body, name="s5_fwd", grid=(nseq, nb),
        in_specs=[pl.BlockSpec((seq, ch), lambda b, j: (b, j)), blk3(ch, ns), blk3(ch, ns), blk3(ns, ch), blk3(ns, ch),
                  pl.BlockSpec((8 * SUBLANES, ns), lambda b, j: (0, j)), pl.BlockSpec((1, ch), lambda b, j: (0, j))],
        out_specs=[pl.BlockSpec((seq, ch), lambda b, j: (b, j)), pl.BlockSpec((seq, ns), lambda b, j: (b, j)),
                   pl.BlockSpec((seq, ns), lambda b, j: (b, j))],
        out_shape=[jax.ShapeDtypeStruct((nseq * seq, nb * ch), F32), jax.ShapeDtypeStruct((nseq * seq, nb * ns), F32),
                   jax.ShapeDtypeStruct((nseq * seq, nb * ns), F32)],
        compiler_params=_params("arbitrary", "arbitrary"),
    )(z, bre3, bim3, cre3, cim3, coef, dskip)


def _s5_bwd(dy, z, xre, xim, bre3, bim3, cre3, cim3, coef_rev, dskip, *, nseq, seq):
    nb = bre3.shape[0]
    ch, ns = bre3.shape[1], bre3.shape[2]

    def body(dy_ref, za_ref, xre_ref, xim_ref, bre_ref, bim_ref, cre_ref, cim_ref, coef_ref, d_ref,
             dza_ref, dbre_ref, dbim_ref, dcre_ref, dcim_ref, dlam_ref, dd_ref, are_ref, aim_ref):
        dy = dy_ref[...]
        za = za_ref[...]
        are_ref[...] = _dot_nt(dy, cre_ref[...])
        aim_ref[...] = -_dot_nt(dy, cim_ref[...])
        dlam = _s5_scan(are_ref, aim_ref, coef_ref, seq, ns, True, xre_ref, xim_ref)
        are = are_ref[...]
        aim = aim_ref[...]
        dza_ref[...] = (_dot_nt(are, bre_ref[...]) + _dot_nt(aim, bim_ref[...]) + d_ref[...] * dy).astype(dza_ref.dtype)
        parts = (_dot_tn(za, are), _dot_tn(za, aim), _dot_tn(xre_ref[...], dy), -_dot_tn(xim_ref[...], dy),
                 dlam, jnp.sum(dy * za, axis=0, keepdims=True))
        first = pl.program_id(1) == 0
        for r, v in zip((dbre_ref, dbim_ref, dcre_ref, dcim_ref, dlam_ref, dd_ref), parts):
            @pl.when(first)
            def _():
                r[...] = v

            @pl.when(jnp.logical_not(first))
            def _():
                r[...] += v

    blk3 = lambda r, c: pl.BlockSpec((None, r, c), lambda j, b: (j, 0, 0))
    tok = lambda c: pl.BlockSpec((seq, c), lambda j, b: (b, j))
    return pl.pallas_call(
        body, name="s5_bwd", grid=(nb, nseq),
        in_specs=[tok(ch), tok(ch), tok(ns), tok(ns), blk3(ch, ns), blk3(ch, ns), blk3(ns, ch), blk3(ns, ch),
                  pl.BlockSpec((8 * SUBLANES, ns), lambda j, b: (0, j)), pl.BlockSpec((1, ch), lambda j, b: (0, j))],
        out_specs=[tok(ch), blk3(ch, ns), blk3(ch, ns), blk3(ns, ch), blk3(ns, ch),
                   pl.BlockSpec((None, 2, ns), lambda j, b: (j, 0, 0)), pl.BlockSpec((1, ch), lambda j, b: (0, j))],
        out_shape=[jax.ShapeDtypeStruct((nseq * seq, nb * ch), BF16),
                   jax.ShapeDtypeStruct((nb, ch, ns), F32), jax.ShapeDtypeStruct((nb, ch, ns), F32),
                   jax.ShapeDtypeStruct((nb, ns, ch), F32), jax.ShapeDtypeStruct((nb, ns, ch), F32),
                   jax.ShapeDtypeStruct((nb, 2, ns), F32), jax.ShapeDtypeStruct((1, nb * ch), F32)],
        scratch_shapes=[pltpu.VMEM((seq, ns), F32), pltpu.VMEM((seq, ns), F32)],
        compiler_params=_params("arbitrary", "arbitrary"),
    )(dy, z, xre, xim, bre3, bim3, cre3, cim3, coef_rev, dskip)


def _cumsum_rows(x, reverse=False):
    n = x.shape[0]
    row = lax.broadcasted_iota(jnp.int32, x.shape, 0)
    s = 1
    while s < n:
        if reverse:
            x = x + jnp.where(row < n - s, pltpu.roll(x, n - s, 0), 0.0)
        else:
            x = x + jnp.where(row >= s, pltpu.roll(x, s, 0), 0.0)
        s *= 2
    return x


def _hg_gates(zq, zf, lb):
    sg = _sigmoid(zf)
    f = lb + (1.0 - lb) * sg
    sq = _sigmoid(zq)
    qa = zq * sq * (HEAD ** -0.5)
    b = _cumsum_rows(jnp.log(f))
    return sg, f, sq, qa, 1.0 - f, b


def _hg_scores(qa, kk, b):
    c = qa.shape[0]
    row = lax.broadcasted_iota(jnp.int32, qa.shape, 0)
    dmat = lax.broadcasted_iota(jnp.int32, (c, c), 0) - lax.broadcasted_iota(jnp.int32, (c, c), 1)
    p = jnp.zeros((c, c), F32)
    for d in range(c):
        if d == 0:
            fd = qa * kk
        else:
            e = jnp.exp(jnp.minimum(b - pltpu.roll(b, d, 0), 0.0))
            fd = jnp.where(row >= d, qa * pltpu.roll(kk, d, 0) * e, 0.0)
        p = jnp.where(dmat == d, jnp.sum(fd, axis=1, keepdims=True), p)
    return p


def _hg_scores_bwd(dp, qa, kk, b):
    c = qa.shape[0]
    row = lax.broadcasted_iota(jnp.int32, qa.shape, 0)
    dmat = lax.broadcasted_iota(jnp.int32, (c, c), 0) - lax.broadcasted_iota(jnp.int32, (c, c), 1)
    dqa = jnp.zeros_like(qa)
    dkk = jnp.zeros_like(qa)
    db = jnp.zeros_like(qa)
    for d in range(c):
        dcol = jnp.sum(jnp.where(dmat == d, dp, 0.0), axis=1, keepdims=True)
        if d == 0:
            dqa = dqa + dcol * kk
            dkk = dkk + dcol * qa
        else:
            e = jnp.exp(jnp.minimum(b - pltpu.roll(b, d, 0), 0.0))
            w = jnp.where(row >= d, dcol * e, 0.0)
            kr = pltpu.roll(kk, d, 0)
            dqa = dqa + w * kr
            tmp = w * qa
            dkk = dkk + pltpu.roll(tmp, c - d, 0)
            x = tmp * kr
            db = db + x - pltpu.roll(x, c - d, 0)
    return dqa, dkk, db


def _hg_specs(nseq, seq, heads, qoff):
    nc = seq // CHUNK
    zspec = lambda off: pl.BlockSpec((CHUNK, HEAD), lambda h, b, n, off=off: (b * nc + n, off + h))
    return nc, [zspec(qoff), zspec(qoff + heads), zspec(qoff + 2 * heads), zspec(qoff + 3 * heads)]


def _hg_fwd(z, lbrow, gain, *, nseq, seq, heads, qoff):
    nc, zspecs = _hg_specs(nseq, seq, heads, qoff)

    def body(zq_ref, zf_ref, zi_ref, zg_ref, lb_ref, gn_ref, o_ref, yb_ref, st_ref, state):
        @pl.when(pl.program_id(2) == 0)
        def _():
            state[...] = jnp.zeros_like(state)

        st = state[...]
        st_ref[...] = st
        zi = zi_ref[...]
        zg = zg_ref[...]
        _, _, _, qa, kk, b = _hg_gates(zq_ref[...], zf_ref[...], lb_ref[...])
        o = _dot_nt(qa * jnp.exp(b), st) + _dot(_hg_scores(qa, kk, b), zi)
        bl = b[CHUNK - 1:CHUNK, :]
        state[...] = st * jnp.exp(bl) + _dot_tn(zi, kk * jnp.exp(bl - b))
        o_ref[...] = o
        r = lax.rsqrt(jnp.mean(o * o, axis=1, keepdims=True) + EPS)
        yb_ref[...] = (o * r * gn_ref[...] * zg * _sigmoid(zg)).astype(yb_ref.dtype)

    tok = pl.BlockSpec((CHUNK, HEAD), lambda h, b, n: (b * nc + n, h))
    vec = pl.BlockSpec((1, HEAD), lambda h, b, n: (0, h))
    rows = nseq * seq
    return pl.pallas_call(
        body, name="hgrn2_fwd", grid=(heads, nseq, nc),
        in_specs=zspecs + [vec, vec],
        out_specs=[tok, tok, pl.BlockSpec((None, None, None, HEAD, HEAD), lambda h, b, n: (h, b, n, 0, 0))],
        out_shape=[jax.ShapeDtypeStruct((rows, heads * HEAD), F32), jax.ShapeDtypeStruct((rows, heads * HEAD), BF16),
                   jax.ShapeDtypeStruct((heads, nseq, nc, HEAD, HEAD), F32)],
        scratch_shapes=[pltpu.VMEM((HEAD, HEAD), F32)],
        compiler_params=_params("arbitrary", "arbitrary", "arbitrary"),
    )(z, z, z, z, lbrow, gain)


def _hg_bwd(dyb, z, o, states, lbrow, gain, *, nseq, seq, heads, qoff):
    nc = seq // CHUNK
    rev = lambda n: nc - 1 - n
    zspec = lambda off: pl.BlockSpec((CHUNK, HEAD), lambda h, b, n, off=off: (b * nc + rev(n), off + h))

    def body(dyb_ref, zq_ref, zf_ref, zi_ref, zg_ref, o_ref, st_ref, lb_ref, gn_ref,
             dzq_ref, dzf_ref, dzi_ref, dzg_ref, dlb_ref, dgn_ref, dstate):
        @pl.when(pl.program_id(2) == 0)
        def _():
            dstate[...] = jnp.zeros_like(dstate)

        lb = lb_ref[...]
        gain_v = gn_ref[...]
        zq = zq_ref[...]
        zi = zi_ref[...]
        zg = zg_ref[...]
        sg, f, sq, qa, kk, b = _hg_gates(zq, zf_ref[...], lb)
        eb = jnp.exp(b)
        qt = qa * eb
        bl = b[CHUNK - 1:CHUNK, :]
        ebl = jnp.exp(bl)
        ekb = jnp.exp(bl - b)
        kh = kk * ekb
        st = st_ref[...]
        dst = dstate[...]
        o = o_ref[...]
        r = lax.rsqrt(jnp.mean(o * o, axis=1, keepdims=True) + EPS)
        oh = o * r
        szg = _sigmoid(zg)
        dyb = dyb_ref[...]
        don = dyb * zg * szg
        dzg_ref[...] = (dyb * oh * gain_v * szg * (1.0 + zg * (1.0 - szg))).astype(dzg_ref.dtype)
        doh = don * gain_v
        do = r * (doh - oh * jnp.mean(doh * oh, axis=1, keepdims=True))
        c = CHUNK
        causal = lax.broadcasted_iota(jnp.int32, (c, c), 0) >= lax.broadcasted_iota(jnp.int32, (c, c), 1)
        dqt = _dot(do, st)
        dp = jnp.where(causal, _dot_nt(do, zi), 0.0)
        p = _hg_scores(qa, kk, b)
        dzi_ref[...] = (_dot_tn(p, do) + _dot_nt(kh, dst)).astype(dzi_ref.dtype)
        dkh = _dot(zi, dst)
        dbl = jnp.sum(dkh * kh, axis=0, keepdims=True) + jnp.sum(dst * st, axis=0, keepdims=True) * ebl
        dstate[...] = _dot_tn(do, qt) + dst * ebl
        dqa_s, dkk_s, db_s = _hg_scores_bwd(dp, qa, kk, b)
        dqa = dqt * eb + dqa_s
        dkk = dkh * ekb + dkk_s
        db = dqt * qt - dkh * kh + db_s
        row = lax.broadcasted_iota(jnp.int32, db.shape, 0)
        db = db + jnp.where(row == c - 1, dbl, 0.0)
        df = _cumsum_rows(db, reverse=True) / f - dkk
        dzf_ref[...] = (df * (1.0 - lb) * sg * (1.0 - sg)).astype(dzf_ref.dtype)
        dzq_ref[...] = (dqa * (HEAD ** -0.5) * sq * (1.0 + zq * (1.0 - sq))).astype(dzq_ref.dtype)
        parts = (jnp.sum(df * (1.0 - sg), axis=0, keepdims=True), jnp.sum(don * oh, axis=0, keepdims=True))
        first = jnp.logical_and(pl.program_id(1) == 0, pl.program_id(2) == 0)
        for rf, v in zip((dlb_ref, dgn_ref), parts):
            @pl.when(first)
            def _():
                rf[...] = v

            @pl.when(jnp.logical_not(first))
            def _():
                rf[...] += v

    tok = pl.BlockSpec((CHUNK, HEAD), lambda h, b, n: (b * nc + rev(n), h))
    vec = pl.BlockSpec((1, HEAD), lambda h, b, n: (0, h))
    rows = nseq * seq
    return pl.pallas_call(
        body, name="hgrn2_bwd", grid=(heads, nseq, nc),
        in_specs=[tok, zspec(qoff), zspec(qoff + heads), zspec(qoff + 2 * heads), zspec(qoff + 3 * heads), tok,
                  pl.BlockSpec((None, None, None, HEAD, HEAD), lambda h, b, n: (h, b, rev(n), 0, 0)), vec, vec],
        out_specs=[tok, tok, tok, tok, vec, vec],
        out_shape=[jax.ShapeDtypeStruct((rows, heads * HEAD), BF16)] * 4
        + [jax.ShapeDtypeStruct((1, heads * HEAD), F32)] * 2,
        scratch_shapes=[pltpu.VMEM((HEAD, HEAD), F32)],
        compiler_params=_params("arbitrary", "arbitrary", "arbitrary"),
    )(dyb, z, z, z, z, o, states, lbrow, gain)


def _conv_taps(h, w, bias):
    row = lax.broadcasted_iota(jnp.int32, h.shape, 0)
    h1 = jnp.where(row >= 1, pltpu.roll(h, 1, 0), 0.0)
    h2 = jnp.where(row >= 2, pltpu.roll(h, 2, 0), 0.0)
    return h2 * w[0:1, :] + h1 * w[1:2, :] + h * w[2:3, :] + bias, h1, h2


def _conv_fwd(h, wconv, bconv, *, nseq, seq):
    ff2 = h.shape[1]
    ncol = ff2 // 2 // LANES

    def body(hg_ref, hv_ref, wg_ref, wv_ref, bg_ref, bv_ref, a_ref):
        g, _, _ = _conv_taps(hg_ref[...], wg_ref[...], bg_ref[...])
        v, _, _ = _conv_taps(hv_ref[...], wv_ref[...], bv_ref[...])
        a_ref[...] = (g * _sigmoid(g) * v).astype(a_ref.dtype)

    tok = lambda off: pl.BlockSpec((seq, LANES), lambda j, b, off=off: (b, off + j))
    wsp = lambda off: pl.BlockSpec((CONV_W, LANES), lambda j, b, off=off: (0, off + j))
    bsp = lambda off: pl.BlockSpec((1, LANES), lambda j, b, off=off: (0, off + j))
    return pl.pallas_call(
        body, name="conv_fwd", grid=(ncol, nseq),
        in_specs=[tok(0), tok(ncol), wsp(0), wsp(ncol), bsp(0), bsp(ncol)],
        out_specs=tok(0), out_shape=jax.ShapeDtypeStruct((nseq * seq, ff2 // 2), BF16),
        compiler_params=_params("arbitrary", "arbitrary"),
    )(h, h, wconv, wconv, bconv, bconv)


def _conv_bwd(da, h, wconv, bconv, *, nseq, seq):
    ff2 = h.shape[1]
    ncol = ff2 // 2 // LANES

    def half_bwd(d, hcur, h1, h2, w):
        n = d.shape[0]
        row = lax.broadcasted_iota(jnp.int32, d.shape, 0)
        d1 = jnp.where(row < n - 1, pltpu.roll(d, n - 1, 0), 0.0)
        d2 = jnp.where(row < n - 2, pltpu.roll(d, n - 2, 0), 0.0)
        dh = d * w[2:3, :] + d1 * w[1:2, :] + d2 * w[0:1, :]
        stats = jnp.concatenate(
            [jnp.sum(h2 * d, axis=0, keepdims=True), jnp.sum(h1 * d, axis=0, keepdims=True),
             jnp.sum(hcur * d, axis=0, keepdims=True), jnp.sum(d, axis=0, keepdims=True),
             jnp.zeros((SUBLANES - 4, d.shape[1]), F32)], axis=0)
        return dh, stats

    def body(da_ref, hg_ref, hv_ref, wg_ref, wv_ref, bg_ref, bv_ref, dhg_ref, dhv_ref, sg_ref, sv_ref):
        hg = hg_ref[...]
        hv = hv_ref[...]
        wg = wg_ref[...]
        wv = wv_ref[...]
        g, g1, g2 = _conv_taps(hg, wg, bg_ref[...])
        v, v1, v2 = _conv_taps(hv, wv, bv_ref[...])
        da = da_ref[...]
        s = _sigmoid(g)
        dhg, stg = half_bwd(da * v * s * (1.0 + g * (1.0 - s)), hg, g1, g2, wg)
        dhv, stv = half_bwd(da * g * s, hv, v1, v2, wv)
        dhg_ref[...] = dhg.astype(dhg_ref.dtype)
        dhv_ref[...] = dhv.astype(dhv_ref.dtype)
        first = pl.program_id(1) == 0
        for r, val in ((sg_ref, stg), (sv_ref, stv)):
            @pl.when(first)
            def _():
                r[...] = val

            @pl.when(jnp.logical_not(first))
            def _():
                r[...] += val

    tok = lambda off: pl.BlockSpec((seq, LANES), lambda j, b, off=off: (b, off + j))
    wsp = lambda off: pl.BlockSpec((CONV_W, LANES), lambda j, b, off=off: (0, off + j))
    bsp = lambda off: pl.BlockSpec((1, LANES), lambda j, b, off=off: (0, off + j))
    ssp = pl.BlockSpec((SUBLANES, LANES), lambda j, b: (0, j))
    dhg, dhv, stg, stv = pl.pallas_call(
        body, name="conv_bwd", grid=(ncol, nseq),
        in_specs=[tok(0), tok(0), tok(ncol), wsp(0), wsp(ncol), bsp(0), bsp(ncol)],
        out_specs=[tok(0), tok(0), ssp, ssp],
        out_shape=[jax.ShapeDtypeStruct((nseq * seq, ff2 // 2), BF16)] * 2
        + [jax.ShapeDtypeStruct((SUBLANES, ff2 // 2), F32)] * 2,
        compiler_params=_params("arbitrary", "arbitrary"),
    )(da, h, h, wconv, wconv, bconv, bconv)
    return jnp.concatenate([dhg, dhv], axis=1), jnp.concatenate([stg, stv], axis=1)


def _rms_fwd(xv, g):
    r = lax.rsqrt(jnp.mean(xv * xv, axis=1, keepdims=True) + EPS)
    return (xv * r * g,)


def _rms_bwd(xv, g, dy, res):
    r = lax.rsqrt(jnp.mean(xv * xv, axis=1, keepdims=True) + EPS)
    xh = xv * r
    dxh = dy * g
    dx = r * (dxh - xh * jnp.mean(dxh * xh, axis=1, keepdims=True)) + res
    return dx, jnp.sum(dy * xh, axis=0, keepdims=True)


def _loss_head(x2, tgt, g):
    d = x2.shape[1]
    r = lax.rsqrt(jnp.mean(x2 * x2, axis=1, keepdims=True) + EPS)
    xh = x2 * r
    err = xh * g - tgt
    dy = err * (1.0 / d)
    dxh = dy * g
    dx = r * (dxh - xh * jnp.mean(dxh * xh, axis=1, keepdims=True))
    loss = 0.5 * jnp.sum(jnp.mean(err * err, axis=1, keepdims=True), axis=0, keepdims=True)
    return dx, jnp.sum(dy * xh, axis=0, keepdims=True), jnp.broadcast_to(loss, (1, LANES))


def _local_step(x, tgt, p, *, nseq, seq):
    t, d = x.shape
    s5w = p["s5_d"].shape[1]
    hgw = p["gain"].shape[1]
    heads = hgw // HEAD
    qoff = s5w // LANES
    gblk = (s5w + 4 * hgw) // GATE_BLOCK
    ngb = d // GATE_BLOCK
    tm = _row_tile(t, 256)
    row = lambda a, w=None, base=0: (a, a.shape[1] if w is None else w, base, "row")
    vec = lambda a, w=None, base=0: (a, a.shape[1] if w is None else w, base, "vec")
    rw = functools.partial(_rowwise, rows=t, tm=tm)

    (u,) = rw("rms_mix", _rms_fwd, [row(x), vec(p["g_mix"])], [(d, d, BF16)])
    z = _mm_fwd_cols("in_proj", u, p["w_in"])

    lam_re, lam_im, bb_re, bb_im = _s5_discretize(p["s5_a_re"], p["s5_a_im"], p["s5_log_dt"], p["s5_b_re"], p["s5_b_im"])
    bre3 = _s5_in_blocks(bb_re).astype(BF16)
    bim3 = _s5_in_blocks(bb_im).astype(BF16)
    cre3 = _s5_out_blocks(p["s5_c_re"]).astype(BF16)
    cim3 = _s5_out_blocks(p["s5_c_im"]).astype(BF16)
    coef_f = _s5_scan_tables(lam_re.reshape(-1), lam_im.reshape(-1), False)
    coef_r = _s5_scan_tables(lam_re.reshape(-1), lam_im.reshape(-1), True)
    y5, xre, xim = _s5_fwd(z, bre3, bim3, cre3, cim3, coef_f, p["s5_d"], nseq=nseq, seq=seq)
    (ya0,) = rw("s5_gelu", lambda y: (_gelu(y),), [row(y5)], [(s5w, s5w, BF16)])
    gl = _mm_fwd_rows("glu_proj", ya0, p["w_glu"])
    (ya,) = rw("s5_glu", lambda y, g, b: (_gelu(y) * _sigmoid(g + b),), [row(y5), row(gl), vec(p["b_glu"])],
               [(s5w, s5w, BF16)])

    o, yb, states = _hg_fwd(z, p["lbrow"], p["gain"], nseq=nseq, seq=seq, heads=heads, qoff=qoff)

    pa = _mm_fwd_cols("proj_a", ya, p["w_pa"])
    pb = _mm_fwd_cols("proj_b", yb, p["w_pb"])
    gb = GATE_BLOCK
    (m,) = rw("merge", lambda ga, gbv, a, b: (_sigmoid(ga) * a + _sigmoid(gbv) * b,),
              [row(z, gb, gblk), row(z, gb, gblk + ngb), row(pa, gb), row(pb, gb)], [(d, gb, BF16)], ncol=ngb)
    x1 = _mm_fwd_rows("out_proj", m, p["w_out"], res=x)

    (u2,) = rw("rms_ffn", _rms_fwd, [row(x1), vec(p["g_ffn"])], [(d, d, BF16)])
    h = _mm_fwd_cols("up_proj", u2, p["w_up"])
    a = _conv_fwd(h, p["w_conv"], p["b_conv"], nseq=nseq, seq=seq)
    x2 = _mm_fwd_rows("down_proj", a, p["w_down"], res=x1)

    dx2, dg_final, lossv = rw("loss_head", _loss_head, [row(x2), row(tgt), vec(p["g_final"])], [(d, d, F32)],
                              accs=[(d, d), (LANES, LANES)])

    da = _mm_bwd_rows("down_bwd", dx2, p["w_down"])
    g_wdown = _mm_wgrad_rows("down_wgrad", a, dx2)
    dh, cstats = _conv_bwd(da, h, p["w_conv"], p["b_conv"], nseq=nseq, seq=seq)
    du2 = _mm_bwd_cols("up_bwd", dh, p["w_up"])
    g_wup = _mm_wgrad_cols("up_wgrad", u2, dh)
    dx1, dg_ffn = rw("rms_ffn_bwd", _rms_bwd, [row(x1), vec(p["g_ffn"]), row(du2), row(dx2)], [(d, d, F32)],
                     accs=[(d, d)])

    dm = _mm_bwd_rows("out_bwd", dx1, p["w_out"])
    g_wout = _mm_wgrad_rows("out_wgrad", m, dx1)

    def merge_bwd(ga, gbv, av, bv, dmv):
        sa = _sigmoid(ga)
        sb = _sigmoid(gbv)
        return dmv * sa, dmv * sb, dmv * av * sa * (1.0 - sa), dmv * bv * sb * (1.0 - sb)

    dpa, dpb, dzga, dzgb = rw("merge_bwd", merge_bwd,
                              [row(z, gb, gblk), row(z, gb, gblk + ngb), row(pa, gb), row(pb, gb), row(dm, gb)],
                              [(d, gb, BF16)] * 4, ncol=ngb)
    dya = _mm_bwd_cols("proj_a_bwd", dpa, p["w_pa"])
    g_wpa = _mm_wgrad_cols("proj_a_wgrad", ya, dpa)
    dyb = _mm_bwd_cols("proj_b_bwd", dpb, p["w_pb"])
    g_wpb = _mm_wgrad_cols("proj_b_wgrad", yb, dpb)

    def glu_bwd1(y, g, b, dyv):
        s = _sigmoid(g + b)
        dgl = dyv * _gelu(y) * s * (1.0 - s)
        return dgl, jnp.sum(dgl, axis=0, keepdims=True)

    dgl, db_glu = rw("s5_glu_bwd", glu_bwd1, [row(y5), row(gl), vec(p["b_glu"]), row(dya)], [(s5w, s5w, BF16)],
                     accs=[(s5w, s5w)])
    dgl_in = _mm_bwd_rows("glu_bwd", dgl, p["w_glu"])
    g_wglu = _mm_wgrad_rows("glu_wgrad", ya0, dgl)
    (dy5,) = rw("s5_gelu_bwd", lambda y, g, b, dyv, tv: ((dyv * _sigmoid(g + b) + tv) * _gelu_grad(y),),
                [row(y5), row(gl), vec(p["b_glu"]), row(dya), row(dgl_in)], [(s5w, s5w, F32)])
    dza, dbre3, dbim3, dcre3, dcim3, dlam, dd = _s5_bwd(dy5, z, xre, xim, bre3, bim3, cre3, cim3, coef_r, p["s5_d"],
                                                        nseq=nseq, seq=seq)

    dzq, dzf, dzi, dzg, dlb, dgain = _hg_bwd(dyb, z, o, states, p["lbrow"], p["gain"], nseq=nseq, seq=seq,
                                             heads=heads, qoff=qoff)

    dz = jnp.concatenate([dza, dzq, dzf, dzi, dzg, dzga, dzgb], axis=1)
    du = _mm_bwd_cols("in_bwd", dz, p["w_in"])
    g_win = _mm_wgrad_cols("in_wgrad", u, dz)
    dx, dg_mix = rw("rms_mix_bwd", _rms_bwd, [row(x), vec(p["g_mix"]), row(du), row(dx1)], [(d, d, F32)],
                    accs=[(d, d)])

    gshape = lam_re.shape
    big = {"w_in": g_win, "w_glu": g_wglu, "w_pa": g_wpa, "w_pb": g_wpb, "w_out": g_wout, "w_up": g_wup,
           "w_down": g_wdown}
    small = {
        "loss": lossv, "g_mix": dg_mix, "g_ffn": dg_ffn, "g_final": dg_final, "b_glu": db_glu, "gain": dgain,
        "lbrow": dlb, "s5_d": dd, "w_conv": cstats[0:CONV_W], "b_conv": cstats[CONV_W:CONV_W + 1],
        "lam_re": dlam[:, 0, :].reshape(gshape), "lam_im": dlam[:, 1, :].reshape(gshape),
        "bb_re": _s5_in_blocks_diag(dbre3), "bb_im": _s5_in_blocks_diag(dbim3),
        "s5_c_re": _s5_out_blocks_diag(dcre3), "s5_c_im": _s5_out_blocks_diag(dcim3),
    }
    return dx, big, small


ANY = pl.BlockSpec(memory_space=pl.ANY)


def _place():
    x, y, c = lax.axis_index("x"), lax.axis_index("y"), lax.axis_index("c")
    chips = [(1 - x, y), (x, 1 - y), (1 - x, 1 - y)]
    return x, y, c, chips


def _gather_weights(wloc):
    r = wloc.shape[0]
    rh = r // 2

    def body(w_ref, out_ref, send_sems, recv_sems, local_sem):
        x, y, c, chips = _place()
        me = 2 * x + y
        mine_half = pl.ds(pl.multiple_of(c * rh, 16), rh)
        other_half = pl.ds(pl.multiple_of((1 - c) * rh, 16), rh)

        def copy(k, src, dst, to):
            return pltpu.make_async_remote_copy(src_ref=src, dst_ref=dst, send_sem=send_sems.at[k],
                                                recv_sem=recv_sems.at[k], device_id=to, device_id_type=MESH)

        mine = pltpu.make_async_copy(w_ref, out_ref.at[me], local_sem)
        mine.start()
        first = [copy(j, w_ref.at[mine_half], out_ref.at[me, mine_half], (cx, cy, c)) for j, (cx, cy) in enumerate(chips)]
        for cp in first:
            cp.start()
        passed = []
        for j, (cx, cy) in enumerate(chips):
            landed = out_ref.at[2 * cx + cy, mine_half]
            copy(j, landed, landed, (x, y, c)).wait_recv()
            passed.append(copy(3 + j, landed, landed, (x, y, 1 - c)))
            passed[j].start()
        for j, (cx, cy) in enumerate(chips):
            landed = out_ref.at[2 * cx + cy, other_half]
            copy(3 + j, landed, landed, (x, y, c)).wait_recv()
        for cp in first + passed:
            cp.wait_send()
        mine.wait()

    return pl.pallas_call(
        body, name="gather_weights", out_shape=jax.ShapeDtypeStruct((N_CHIPS,) + wloc.shape, wloc.dtype),
        in_specs=[ANY], out_specs=ANY,
        scratch_shapes=[pltpu.SemaphoreType.DMA((6,)), pltpu.SemaphoreType.DMA((6,)), pltpu.SemaphoreType.DMA],
    )(wloc)


def _swap_halves(g4):
    n, r, w = g4.shape
    rh = r // 2

    def body(g_ref, out_ref, send_sem, recv_sem):
        x, y, c, _ = _place()
        theirs = g_ref.at[:, pl.ds(pl.multiple_of((1 - c) * rh, 16), rh), :]
        cp = pltpu.make_async_remote_copy(src_ref=theirs, dst_ref=out_ref, send_sem=send_sem, recv_sem=recv_sem,
                                          device_id=(x, y, 1 - c), device_id_type=MESH)
        cp.start()
        cp.wait()

    return pl.pallas_call(
        body, name="grad_swap_halves", out_shape=jax.ShapeDtypeStruct((n, rh, w), g4.dtype),
        in_specs=[ANY], out_specs=ANY, scratch_shapes=[pltpu.SemaphoreType.DMA, pltpu.SemaphoreType.DMA],
    )(g4)


def _scatter_to_chips(h4):
    n, rh, w = h4.shape

    def body(h_ref, out_ref, send_sems, recv_sems):
        x, y, c, chips = _place()
        copies = [pltpu.make_async_remote_copy(src_ref=h_ref.at[2 * cx + cy], dst_ref=out_ref.at[j],
                                               send_sem=send_sems.at[j], recv_sem=recv_sems.at[j],
                                               device_id=(cx, cy, c), device_id_type=MESH)
                  for j, (cx, cy) in enumerate(chips)]
        for cp in copies:
            cp.start()
        for cp in copies:
            cp.wait()

    return pl.pallas_call(
        body, name="grad_scatter_chips", out_shape=jax.ShapeDtypeStruct((N_CHIPS - 1, rh, w), h4.dtype),
        in_specs=[ANY], out_specs=ANY, scratch_shapes=[pltpu.SemaphoreType.DMA((3,)), pltpu.SemaphoreType.DMA((3,))],
    )(h4)


def _join_halves(gh):
    rh, w = gh.shape

    def body(g_ref, out_ref, send_sem, recv_sem, local_sem):
        x, y, c, _ = _place()
        mine_half = out_ref.at[pl.ds(pl.multiple_of(c * rh, 8), rh)]
        other_half = out_ref.at[pl.ds(pl.multiple_of((1 - c) * rh, 8), rh)]
        mine = pltpu.make_async_copy(g_ref, mine_half, local_sem)
        mine.start()
        cp = pltpu.make_async_remote_copy(src_ref=g_ref, dst_ref=mine_half, send_sem=send_sem, recv_sem=recv_sem,
                                          device_id=(x, y, 1 - c), device_id_type=MESH)
        cp.start()
        pltpu.make_async_remote_copy(src_ref=g_ref, dst_ref=other_half, send_sem=send_sem, recv_sem=recv_sem,
                                     device_id=(x, y, c), device_id_type=MESH).wait_recv()
        cp.wait_send()
        mine.wait()

    return pl.pallas_call(
        body, name="grad_join_halves", out_shape=jax.ShapeDtypeStruct((2 * rh, w), gh.dtype),
        in_specs=[ANY], out_specs=ANY,
        scratch_shapes=[pltpu.SemaphoreType.DMA, pltpu.SemaphoreType.DMA, pltpu.SemaphoreType.DMA],
    )(gh)


def _gather_all(v):
    m_per, n = v.shape

    def body(x_ref, out_ref, send_sems, recv_sems, local_sem):
        x, y, c, chips = _place()
        me, sibling = (x, y, c), (x, y, 1 - c)

        def rows(px, py, pc):
            return out_ref.at[pl.ds(pl.multiple_of((4 * px + 2 * py + pc) * m_per, 8), m_per), :]

        def copy(k, block, to, src=None):
            return pltpu.make_async_remote_copy(src_ref=rows(*block) if src is None else src, dst_ref=rows(*block),
                                                send_sem=send_sems.at[k], recv_sem=recv_sems.at[k], device_id=to,
                                                device_id_type=MESH)

        mine = pltpu.make_async_copy(x_ref, rows(*me), local_sem)
        mine.start()
        first = [copy(0, me, sibling, src=x_ref)]
        first += [copy(1 + j, me, (*chip, c), src=x_ref) for j, chip in enumerate(chips)]
        for cp in first:
            cp.start()
        passed = [copy(4 + j, (*chip, c), sibling) for j, chip in enumerate(chips)]
        for j, chip in enumerate(chips):
            copy(1 + j, (*chip, c), me).wait_recv()
            passed[j].start()
        copy(0, sibling, me).wait_recv()
        for j, chip in enumerate(chips):
            copy(4 + j, (*chip, 1 - c), me).wait_recv()
        for cp in first + passed:
            cp.wait_send()
        mine.wait()

    return pl.pallas_call(
        body, name="gather_small_grads", out_shape=jax.ShapeDtypeStruct((N_DEV * m_per, n), v.dtype),
        in_specs=[pl.BlockSpec(memory_space=pltpu.VMEM)], out_specs=pl.BlockSpec(memory_space=pltpu.VMEM),
        scratch_shapes=[pltpu.SemaphoreType.DMA((7,)), pltpu.SemaphoreType.DMA((7,)), pltpu.SemaphoreType.DMA],
        compiler_params=pltpu.CompilerParams(vmem_limit_bytes=VMEM_LIMIT_BYTES),
    )(v)


def _sum_blocks(name, parts, out_dtype, rows, tm):
    def body(*refs):
        acc = refs[0][...].astype(F32)
        for r in refs[1:-1]:
            acc = acc + r[...].astype(F32)
        refs[-1][...] = acc.astype(refs[-1].dtype)

    spec = pl.BlockSpec((tm, LANES), lambda i: (i, 0))
    return pl.pallas_call(
        body, name=name, grid=(rows // tm,), in_specs=[spec] * len(parts), out_specs=spec,
        out_shape=jax.ShapeDtypeStruct((rows, LANES), out_dtype), compiler_params=_params("arbitrary"),
    )(*parts)


def _adamw(name, w, g, m, v):
    rows, cols = w.shape
    tm = _row_tile(rows, 256)
    c1 = 1.0 - ADAM_B1 ** ADAM_STEP
    c2 = 1.0 - ADAM_B2 ** ADAM_STEP

    def fn(wv, gv, mv, vv):
        m2 = ADAM_B1 * mv + (1.0 - ADAM_B1) * gv
        v2 = ADAM_B2 * vv + (1.0 - ADAM_B2) * (gv * gv)
        delta = -ADAM_LR * ((m2 / c1) / (jnp.sqrt(v2 / c2) + ADAM_EPS) + ADAM_WD * wv)
        return delta, m2, v2

    ins = [(a, cols, 0, "row") for a in (w, g, m, v)]
    return _rowwise(name, fn, ins, [(cols, cols, F32)] * 3, rows=rows, tm=tm)


PACK_ROWS = 2048


def _pack(flat_parts, dtype, lead=()):
    parts = [a.astype(dtype).reshape(lead + (-1,)) for a in flat_parts]
    n = sum(a.shape[-1] for a in parts)
    chunk = PACK_ROWS * LANES
    total = -(-n // chunk) * chunk
    if total > n:
        parts.append(jnp.zeros(lead + (total - n,), dtype))
    return jnp.concatenate(parts, axis=-1).reshape(lead + (total // LANES, LANES))


def _unpack(buf, shapes, lead=()):
    flat = buf.reshape(lead + (-1,))
    out, off = [], 0
    for shp in shapes:
        n = math.prod(shp)
        out.append(lax.slice_in_dim(flat, off, off + n, axis=len(lead)).reshape(lead + tuple(shp)))
        off += n
    return out


BIG = ("w_in", "w_glu", "w_pa", "w_pb", "w_out", "w_up", "w_down")
WEIGHTS = ("g_mix", "w_in", "s5_a_re", "s5_a_im", "s5_log_dt", "s5_b_re", "s5_b_im", "s5_c_re", "s5_c_im", "s5_d",
           "w_glu", "b_glu", "hg_lb_logits", "hg_norm_gain", "w_pa", "w_pb", "w_out", "g_ffn", "w_up", "w_conv",
           "b_conv", "w_down", "g_final")
SMALL = tuple(n for n in WEIGHTS if n not in BIG)
SMALL_PARTS = ("loss", "g_mix", "g_ffn", "g_final", "b_glu", "gain", "lbrow", "s5_d", "w_conv", "b_conv", "lam_re",
               "lam_im", "bb_re", "bb_im", "s5_c_re", "s5_c_im")


def _lower_bound(logits):
    return jnp.cumsum(jax.nn.softmax(logits, axis=0), axis=0)[0:1]


def kernel(x, g_mix, w_in, s5_a_re, s5_a_im, s5_log_dt, s5_b_re, s5_b_im, s5_c_re, s5_c_im, s5_d, w_glu, b_glu, hg_lb_logits, hg_norm_gain, w_pa, w_pb, w_out, g_ffn, w_up, w_conv, b_conv, w_down, g_final, loss_target, m_g_mix, m_w_in, m_s5_a_re, m_s5_a_im, m_s5_log_dt, m_s5_b_re, m_s5_b_im, m_s5_c_re, m_s5_c_im, m_s5_d, m_w_glu, m_b_glu, m_hg_lb_logits, m_hg_norm_gain, m_w_pa, m_w_pb, m_w_out, m_g_ffn, m_w_up, m_w_conv, m_b_conv, m_w_down, m_g_final, v_g_mix, v_w_in, v_s5_a_re, v_s5_a_im, v_s5_log_dt, v_s5_b_re, v_s5_b_im, v_s5_c_re, v_s5_c_im, v_s5_d, v_w_glu, v_b_glu, v_hg_lb_logits, v_hg_norm_gain, v_w_pa, v_w_pb, v_w_out, v_g_ffn, v_w_up, v_w_conv, v_b_conv, v_w_down, v_g_final):
    args = dict(locals())
    w = {n: args[n] for n in WEIGHTS}
    mom = {n: args["m_" + n] for n in WEIGHTS}
    var = {n: args["v_" + n] for n in WEIGHTS}
    nseq, seq, d = x.shape
    xi, yi = lax.axis_index("x"), lax.axis_index("y")
    chip = 2 * xi + yi

    shard = {n: w[n][0] for n in BIG}
    conv_bits = lax.bitcast_convert_type(w_conv[0], BF16)
    wall = _gather_weights(_pack([shard[n] for n in BIG] + [conv_bits], BF16))
    got = _unpack(wall, [shard[n].shape for n in BIG] + [conv_bits.shape], lead=(N_CHIPS,))
    full = dict(zip(BIG, got[:-1]))
    for n in ("w_glu", "w_out", "w_down"):
        full[n] = full[n].reshape(-1, full[n].shape[-1])
    conv_all = lax.bitcast_convert_type(got[-1], F32)
    conv_full = conv_all.transpose(1, 0, 2).reshape(CONV_W, -1)

    p = dict(full)
    p.update(g_mix=g_mix, g_ffn=g_ffn, g_final=g_final.reshape(1, -1), b_glu=b_glu, gain=hg_norm_gain, s5_d=s5_d,
             b_conv=b_conv, w_conv=conv_full, lbrow=_lower_bound(hg_lb_logits),
             s5_a_re=s5_a_re[0], s5_a_im=s5_a_im[0], s5_log_dt=s5_log_dt[0], s5_b_re=s5_b_re[0], s5_b_im=s5_b_im[0],
             s5_c_re=s5_c_re[0], s5_c_im=s5_c_im[0])

    dx, gbig, gsmall = _local_step(x.reshape(nseq * seq, d), loss_target.reshape(nseq * seq, d), p, nseq=nseq, seq=seq)

    g4 = _pack([gbig[n] for n in BIG], BF16, lead=(N_CHIPS,))
    r = g4.shape[1]
    rh = r // 2
    ci = lax.axis_index("c")
    sib = _swap_halves(g4)
    own = lax.dynamic_slice_in_dim(g4, ci * rh, rh, axis=1)
    tsum = _row_tile(N_CHIPS * rh, 1024)
    h4 = _sum_blocks("grad_pair_sum", [own.reshape(-1, LANES), sib.reshape(-1, LANES)], BF16, N_CHIPS * rh, tsum)
    h4 = h4.reshape(N_CHIPS, rh, LANES)
    others = _scatter_to_chips(h4)
    mine = lax.dynamic_index_in_dim(h4, chip, axis=0, keepdims=False)
    ghalf = _sum_blocks("grad_chip_sum", [mine, others[0], others[1], others[2]], F32, rh, _row_tile(rh, 1024))
    gfull = _join_halves(ghalf)
    grads = dict(zip(BIG, _unpack(gfull, [shard[n].shape for n in BIG])))

    small_shapes = [gsmall[n].shape for n in SMALL_PARTS]
    vec = _pack([gsmall[n] for n in SMALL_PARTS], F32)
    gathered = _gather_all(vec)
    mrows = vec.shape[0]
    vsum = _sum_blocks("small_grad_sum", [gathered[i * mrows:(i + 1) * mrows] for i in range(N_DEV)], F32, mrows,
                       _row_tile(mrows, 1024))
    sm = dict(zip(SMALL_PARTS, _unpack(vsum, small_shapes)))
    loss = sm["loss"][0, 0]

    _, disc_vjp = jax.vjp(_s5_discretize, p["s5_a_re"], p["s5_a_im"], p["s5_log_dt"], p["s5_b_re"], p["s5_b_im"])
    da_re, da_im, dlog_dt, db_re, db_im = disc_vjp((sm["lam_re"], sm["lam_im"], sm["bb_re"], sm["bb_im"]))
    _, lb_vjp = jax.vjp(_lower_bound, hg_lb_logits)
    (dlogits,) = lb_vjp(sm["lbrow"])
    fcols = w_conv.shape[-1]
    grads.update(
        g_mix=sm["g_mix"], g_ffn=sm["g_ffn"], g_final=sm["g_final"].reshape(-1), b_glu=sm["b_glu"],
        hg_norm_gain=sm["gain"], hg_lb_logits=dlogits, s5_d=sm["s5_d"], b_conv=sm["b_conv"],
        w_conv=lax.dynamic_slice_in_dim(sm["w_conv"], chip * fcols, fcols, axis=1),
        s5_a_re=da_re, s5_a_im=da_im, s5_log_dt=dlog_dt, s5_b_re=db_re, s5_b_im=db_im,
        s5_c_re=sm["s5_c_re"], s5_c_im=sm["s5_c_im"])
    grads = {n: grads[n].reshape(w[n].shape) for n in WEIGHTS}

    delta, new_m, new_v = {}, {}, {}
    for n in BIG:
        shp = shard[n].shape
        dl, m2, v2 = _adamw("adamw_" + n, shard[n], grads[n].reshape(shp), mom[n].reshape(shp), var[n].reshape(shp))
        delta[n], new_m[n], new_v[n] = dl, m2, v2
    sshapes = [w[n].shape for n in SMALL]
    packed = [_pack([src[n] for n in SMALL], F32) for src in (w, grads, mom, var)]
    outs = _adamw("adamw_small", *packed)
    for dst, buf in zip((delta, new_m, new_v), outs):
        dst.update(zip(SMALL, _unpack(buf, sshapes)))
    res = [loss, dx.reshape(x.shape)]
    for group in (grads, delta, new_m, new_v):
        res += [group[n].reshape(w[n].shape) for n in WEIGHTS]
    return tuple(res)
```

```python
import functools
import math

import jax
import jax.numpy as jnp
from jax import lax
from jax.experimental import pallas as pl
from jax.experimental.pallas import tpu as pltpu

F32 = jnp.float32
BF16 = jnp.bfloat16
MESH = pl.DeviceIdType.MESH

EPS = 1e-6
S5_GROUP = 16
S5_STATE = 64
S5_BLOCK_GROUPS = 8
HEAD = 128
CHUNK = 64
CONV_W = 3
LANES = 128
SUBLANES = 8
GATE_BLOCK = 512
VMEM_LIMIT_BYTES = 56 * 1024 * 1024

ADAM_LR = 0.001
ADAM_B1 = 0.9
ADAM_B2 = 0.999
ADAM_EPS = 1e-08
ADAM_WD = 0.01
ADAM_STEP = 10

N_CHIPS = 4
N_DEV = 8


def _params(*sem):
    return pltpu.CompilerParams(dimension_semantics=sem, vmem_limit_bytes=VMEM_LIMIT_BYTES)


def _row_tile(rows, cap):
    if rows <= cap:
        return rows
    for t in range(cap - cap % 8, 7, -8):
        if rows % t == 0:
            return t
    raise ValueError(f"no row tile for {rows}")


def _dot(a, b):
    return jnp.dot(a.astype(BF16), b.astype(BF16), preferred_element_type=F32)


def _dot_nt(a, b):
    return lax.dot_general(a.astype(BF16), b.astype(BF16), (((1,), (1,)), ((), ())), preferred_element_type=F32)


def _dot_tn(a, b):
    return lax.dot_general(a.astype(BF16), b.astype(BF16), (((0,), (0,)), ((), ())), preferred_element_type=F32)


def _sigmoid(x):
    return 1.0 / (1.0 + jnp.exp(-x))


_GELU_C = math.sqrt(2.0 / math.pi)


def _gelu(x):
    return 0.5 * x * (1.0 + jnp.tanh(_GELU_C * (x + 0.044715 * x * x * x)))


def _gelu_grad(x):
    th = jnp.tanh(_GELU_C * (x + 0.044715 * x * x * x))
    return 0.5 * (1.0 + th) + 0.5 * x * (1.0 - th * th) * _GELU_C * (1.0 + 3.0 * 0.044715 * x * x)


def _rowwise(name, fn, ins, outs, accs=(), *, rows, tm, ncol=1):
    n_in, n_out = len(ins), len(outs)

    def body(*refs):
        res = fn(*[r[...] for r in refs[:n_in]])
        for r, v in zip(refs[n_in:n_in + n_out], res[:n_out]):
            r[...] = v.astype(r.dtype)
        first = pl.program_id(1) == 0
        for r, v in zip(refs[n_in + n_out:], res[n_out:]):
            @pl.when(first)
            def _():
                r[...] = v

            @pl.when(jnp.logical_not(first))
            def _():
                r[...] += v

    in_specs = []
    for _, width, base, kind in ins:
        if kind == "row":
            in_specs.append(pl.BlockSpec((tm, width), lambda j, i, b=base: (i, b + j)))
        else:
            in_specs.append(pl.BlockSpec((1, width), lambda j, i, b=base: (0, b + j)))
    out_specs = [pl.BlockSpec((tm, width), lambda j, i: (i, j)) for _, width, _ in outs]
    out_specs += [pl.BlockSpec((1, width), lambda j, i: (0, j)) for _, width in accs]
    out_shape = [jax.ShapeDtypeStruct((rows, total), dt) for total, _, dt in outs]
    out_shape += [jax.ShapeDtypeStruct((1, total), F32) for total, _ in accs]
    return pl.pallas_call(
        body, name=name, grid=(ncol, rows // tm), in_specs=in_specs, out_specs=out_specs, out_shape=out_shape,
        compiler_params=_params("arbitrary", "arbitrary"),
    )(*[a for a, _, _, _ in ins])


def _mm(name, a, b, *, mode, grid, a_spec, b_spec, o_spec, out_shape, acc_shape, res=None, res_spec=None):
    nk = grid[2]
    dot = {"nn": _dot, "nt": _dot_nt, "tn": _dot_tn}[mode]

    def body(*refs):
        if res is None:
            a_ref, b_ref, o_ref, acc_ref = refs
        else:
            a_ref, b_ref, r_ref, o_ref, acc_ref = refs
        k = pl.program_id(2)

        @pl.when(k == 0)
        def _():
            acc_ref[...] = jnp.zeros_like(acc_ref)

        acc_ref[...] += dot(a_ref[...], b_ref[...])

        @pl.when(k == nk - 1)
        def _():
            v = acc_ref[...]
            if res is not None:
                v = v + r_ref[...]
            o_ref[...] = v.astype(o_ref.dtype)

    operands = [a, b] + ([] if res is None else [res])
    in_specs = [a_spec, b_spec] + ([] if res is None else [res_spec])
    return pl.pallas_call(
        body, name=name, grid=grid, in_specs=in_specs, out_specs=o_spec, out_shape=out_shape,
        scratch_shapes=[pltpu.VMEM(acc_shape, F32)],
        compiler_params=_params("arbitrary", "arbitrary", "arbitrary"),
    )(*operands)


def _mm_fwd_cols(name, a, w3, out_dtype=F32, tm_cap=512):
    t, k = a.shape
    ns = w3.shape[2]
    tm = _row_tile(t, tm_cap)
    return _mm(name, a, w3, mode="nn", grid=(t // tm, N_CHIPS, 1),
               a_spec=pl.BlockSpec((tm, k), lambda i, j, kk: (i, 0)),
               b_spec=pl.BlockSpec((None, k, ns), lambda i, j, kk: (j, 0, 0)),
               o_spec=pl.BlockSpec((tm, ns), lambda i, j, kk: (i, j)),
               out_shape=jax.ShapeDtypeStruct((t, N_CHIPS * ns), out_dtype), acc_shape=(tm, ns))


def _mm_bwd_cols(name, d, w3, out_dtype=F32, tm_cap=512):
    t = d.shape[0]
    k, ns = w3.shape[1], w3.shape[2]
    tm = _row_tile(t, tm_cap)
    return _mm(name, d, w3, mode="nt", grid=(t // tm, 1, N_CHIPS),
               a_spec=pl.BlockSpec((tm, ns), lambda i, j, kk: (i, kk)),
               b_spec=pl.BlockSpec((None, k, ns), lambda i, j, kk: (kk, 0, 0)),
               o_spec=pl.BlockSpec((tm, k), lambda i, j, kk: (i, 0)),
               out_shape=jax.ShapeDtypeStruct((t, k), out_dtype), acc_shape=(tm, k))


def _mm_wgrad_cols(name, a, d, tk_cap=512):
    t, k = a.shape
    ns = d.shape[1] // N_CHIPS
    tk = _row_tile(t, tk_cap)
    return _mm(name, a, d, mode="tn", grid=(N_CHIPS, 1, t // tk),
               a_spec=pl.BlockSpec((tk, k), lambda j, i, kk: (kk, 0)),
               b_spec=pl.BlockSpec((tk, ns), lambda j, i, kk: (kk, j)),
               o_spec=pl.BlockSpec((None, k, ns), lambda j, i, kk: (j, 0, 0)),
               out_shape=jax.ShapeDtypeStruct((N_CHIPS, k, ns), BF16), acc_shape=(k, ns))


def _mm_fwd_rows(name, a, w, res=None, out_dtype=F32, tm_cap=512, tk_cap=1408):
    t, k = a.shape
    n = w.shape[1]
    tm = _row_tile(t, tm_cap)
    tk = k if k <= tk_cap else tk_cap
    assert k % tk == 0
    return _mm(name, a, w, mode="nn", grid=(t // tm, 1, k // tk),
               a_spec=pl.BlockSpec((tm, tk), lambda i, j, kk: (i, kk)),
               b_spec=pl.BlockSpec((tk, n), lambda i, j, kk: (kk, 0)),
               o_spec=pl.BlockSpec((tm, n), lambda i, j, kk: (i, 0)),
               out_shape=jax.ShapeDtypeStruct((t, n), out_dtype), acc_shape=(tm, n),
               res=res, res_spec=None if res is None else pl.BlockSpec((tm, n), lambda i, j, kk: (i, 0)))


def _mm_bwd_rows(name, d, w, out_dtype=F32, tm_cap=512, tn_cap=1408):
    t, n = d.shape
    k = w.shape[0]
    tm = _row_tile(t, tm_cap)
    tn = k if k <= tn_cap else tn_cap
    assert k % tn == 0
    return _mm(name, d, w, mode="nt", grid=(t // tm, k // tn, 1),
               a_spec=pl.BlockSpec((tm, n), lambda i, j, kk: (i, 0)),
               b_spec=pl.BlockSpec((tn, n), lambda i, j, kk: (j, 0)),
               o_spec=pl.BlockSpec((tm, tn), lambda i, j, kk: (i, j)),
               out_shape=jax.ShapeDtypeStruct((t, k), out_dtype), acc_shape=(tm, tn))


def _mm_wgrad_rows(name, a, d, tk_cap=512):
    t, k = a.shape
    n = d.shape[1]
    nblk = next(b for b in (4, 2, 1) if (k // b) % LANES == 0)
    ks = k // nblk
    tk = _row_tile(t, tk_cap)
    return _mm(name, a, d, mode="tn", grid=(nblk, 1, t // tk),
               a_spec=pl.BlockSpec((tk, ks), lambda j, i, kk: (kk, j)),
               b_spec=pl.BlockSpec((tk, n), lambda j, i, kk: (kk, 0)),
               o_spec=pl.BlockSpec((ks, n), lambda j, i, kk: (j, 0)),
               out_shape=jax.ShapeDtypeStruct((k, n), BF16), acc_shape=(ks, n))


def _s5_discretize(a_re, a_im, log_dt, b_re, b_im):
    dt = jnp.exp(log_dt)[:, None]
    mag = jnp.exp(a_re * dt)
    ang = a_im * dt
    lb_re = mag * jnp.cos(ang)
    lb_im = mag * jnp.sin(ang)
    den = a_re * a_re + a_im * a_im
    n_re = lb_re - 1.0
    n_im = lb_im
    co_re = ((n_re * a_re + n_im * a_im) / den)[..., None]
    co_im = ((n_im * a_re - n_re * a_im) / den)[..., None]
    bb_re = co_re * b_re - co_im * b_im
    bb_im = co_re * b_im + co_im * b_re
    return lb_re, lb_im, bb_re, bb_im


def _s5_in_blocks(bb):
    g = bb.shape[0]
    nb = g // S5_BLOCK_GROUPS
    t = bb.reshape(nb, S5_BLOCK_GROUPS, S5_STATE, S5_GROUP).transpose(0, 1, 3, 2)
    eye = jnp.eye(S5_BLOCK_GROUPS, dtype=bb.dtype)
    full = t[:, :, :, None, :] * eye[None, :, None, :, None]
    return full.reshape(nb, S5_BLOCK_GROUPS * S5_GROUP, S5_BLOCK_GROUPS * S5_STATE)


def _s5_in_blocks_diag(blocks):
    nb = blocks.shape[0]
    t = blocks.reshape(nb, S5_BLOCK_GROUPS, S5_GROUP, S5_BLOCK_GROUPS, S5_STATE)
    d = jnp.einsum("bghgp->bghp", t)
    return d.transpose(0, 1, 3, 2).reshape(nb * S5_BLOCK_GROUPS, S5_STATE, S5_GROUP)


def _s5_out_blocks(c):
    g = c.shape[0]
    nb = g // S5_BLOCK_GROUPS
    t = c.reshape(nb, S5_BLOCK_GROUPS, S5_GROUP, S5_STATE).transpose(0, 1, 3, 2)
    eye = jnp.eye(S5_BLOCK_GROUPS, dtype=c.dtype)
    full = t[:, :, :, None, :] * eye[None, :, None, :, None]
    return full.reshape(nb, S5_BLOCK_GROUPS * S5_STATE, S5_BLOCK_GROUPS * S5_GROUP)


def _s5_out_blocks_diag(blocks):
    nb = blocks.shape[0]
    t = blocks.reshape(nb, S5_BLOCK_GROUPS, S5_STATE, S5_BLOCK_GROUPS, S5_GROUP)
    d = jnp.einsum("bgpgh->bgph", t)
    return d.transpose(0, 1, 3, 2).reshape(nb * S5_BLOCK_GROUPS, S5_GROUP, S5_STATE)


def _s5_scan_tables(lr, li, reverse):
    def cmul(a, b):
        return a[0] * b[0] - a[1] * b[1], a[0] * b[1] + a[1] * b[0]

    lam = (lr, -li) if reverse else (lr, li)
    pw = [lam]
    for _ in range(SUBLANES - 1):
        pw.append(cmul(pw[-1], lam))
    sub = jnp.arange(SUBLANES)[:, None]
    rows = []
    for s in (1, 2, 4):
        keep = (sub <= SUBLANES - 1 - s) if reverse else (sub >= s)
        rows.append(jnp.where(keep, pw[s - 1][0][None, :], 0.0))
        rows.append(jnp.where(keep, pw[s - 1][1][None, :], 0.0))
    order = list(range(SUBLANES - 1, -1, -1)) if reverse else list(range(SUBLANES))
    rows.append(jnp.stack([pw[i][0] for i in order]))
    rows.append(jnp.stack([pw[i][1] for i in order]))
    return jnp.concatenate(rows, axis=0)


def _s5_scan(vre_ref, vim_ref, coef_ref, seq, width, reverse, xre_ref=None, xim_ref=None):
    nt = seq // SUBLANES
    sub = lax.broadcasted_iota(jnp.int32, (SUBLANES, LANES), 0)
    sums = []
    for j in range(width // LANES):
        lanes = slice(j * LANES, (j + 1) * LANES)
        co = [coef_ref[SUBLANES * q:SUBLANES * (q + 1), lanes] for q in range(8)]

        def step(k, carry, lanes=lanes, co=co):
            cr, ci = carry[0], carry[1]
            kk = (nt - 1 - k) if reverse else k
            rows = pl.ds(pl.multiple_of(kk * SUBLANES, SUBLANES), SUBLANES)
            vr = vre_ref[rows, lanes]
            vi = vim_ref[rows, lanes]
            for q, s in enumerate((1, 2, 4)):
                sh = SUBLANES - s if reverse else s
                rr = pltpu.roll(vr, sh, 0)
                ri = pltpu.roll(vi, sh, 0)
                ar, ai = co[2 * q], co[2 * q + 1]
                vr, vi = vr + ar * rr - ai * ri, vi + ar * ri + ai * rr
            edge = 0 if reverse else SUBLANES - 1
            cbr = jnp.broadcast_to(cr[edge:edge + 1, :], (SUBLANES, LANES))
            cbi = jnp.broadcast_to(ci[edge:edge + 1, :], (SUBLANES, LANES))
            pr, pi = co[6], co[7]
            vr, vi = vr + pr * cbr - pi * cbi, vi + pr * cbi + pi * cbr
            vre_ref[rows, lanes] = vr
            vim_ref[rows, lanes] = vi
            if xre_ref is None:
                return vr, vi
            nr = jnp.where(sub == SUBLANES - 1, cbr, pltpu.roll(vr, SUBLANES - 1, 0))
            ni = jnp.where(sub == SUBLANES - 1, cbi, pltpu.roll(vi, SUBLANES - 1, 0))
            xr = xre_ref[rows, lanes]
            xi = xim_ref[rows, lanes]
            return vr, vi, carry[2] + nr * xr + ni * xi, carry[3] + ni * xr - nr * xi

        zero = jnp.zeros((SUBLANES, LANES), F32)
        init = (zero, zero) if xre_ref is None else (zero, zero, zero, zero)
        out = lax.fori_loop(0, nt, step, init)
        if xre_ref is not None:
            sums.append((jnp.sum(out[2], axis=0, keepdims=True), jnp.sum(out[3], axis=0, keepdims=True)))
    if xre_ref is None:
        return None
    return jnp.concatenate([jnp.concatenate([s[0] for s in sums], axis=1),
                            jnp.concatenate([s[1] for s in sums], axis=1)], axis=0)


def _s5_fwd(z, bre3, bim3, cre3, cim3, coef, dskip, *, nseq, seq):
    nb = bre3.shape[0]
    ch, ns = bre3.shape[1], bre3.shape[2]

    def body(za_ref, bre_ref, bim_ref, cre_ref, cim_ref, coef_ref, d_ref, y_ref, xre_ref, xim_ref):
        za = za_ref[...]
        xre_ref[...] = _dot(za, bre_ref[...])
        xim_ref[...] = _dot(za, bim_ref[...])
        _s5_scan(xre_ref, xim_ref, coef_ref, seq, ns, False)
        y_ref[...] = _dot(xre_ref[...], cre_ref[...]) - _dot(xim_ref[...], cim_ref[...]) + d_ref[...] * za

    blk3 = lambda r, c: pl.BlockSpec((None, r, c), lambda b, j: (j, 0, 0))
    return pl.pallas_call(
        body, name="s5_fwd", grid=(nseq, nb),
        in_specs=[pl.BlockSpec((seq, ch), lambda b, j: (b, j)), blk3(ch, ns), blk3(ch, ns), blk3(ns, ch), blk3(ns, ch),
                  pl.BlockSpec((8 * SUBLANES, ns), lambda b, j: (0, j)), pl.BlockSpec((1, ch), lambda b, j: (0, j))],
        out_specs=[pl.BlockSpec((seq, ch), lambda b, j: (b, j)), pl.BlockSpec((seq, ns), lambda b, j: (b, j)),
                   pl.BlockSpec((seq, ns), lambda b, j: (b, j))],
        out_shape=[jax.ShapeDtypeStruct((nseq * seq, nb * ch), F32), jax.ShapeDtypeStruct((nseq * seq, nb * ns), F32),
                   jax.ShapeDtypeStruct((nseq * seq, nb * ns), F32)],
        compiler_params=_params("arbitrary", "arbitrary"),
    )(z, bre3, bim3, cre3, cim3, coef, dskip)


def _s5_bwd(dy, z, xre, xim, bre3, bim3, cre3, cim3, coef_rev, dskip, *, nseq, seq):
    nb = bre3.shape[0]
    ch, ns = bre3.shape[1], bre3.shape[2]

    def body(dy_ref, za_ref, xre_ref, xim_ref, bre_ref, bim_ref, cre_ref, cim_ref, coef_ref, d_ref,
             dza_ref, dbre_ref, dbim_ref, dcre_ref, dcim_ref, dlam_ref, dd_ref, are_ref, aim_ref):
        dy = dy_ref[...]
        za = za_ref[...]
        are_ref[...] = _dot_nt(dy, cre_ref[...])
        aim_ref[...] = -_dot_nt(dy, cim_ref[...])
        dlam = _s5_scan(are_ref, aim_ref, coef_ref, seq, ns, True, xre_ref, xim_ref)
        are = are_ref[...]
        aim = aim_ref[...]
        dza_ref[...] = (_dot_nt(are, bre_ref[...]) + _dot_nt(aim, bim_ref[...]) + d_ref[...] * dy).astype(dza_ref.dtype)
        parts = (_dot_tn(za, are), _dot_tn(za, aim), _dot_tn(xre_ref[...], dy), -_dot_tn(xim_ref[...], dy),
                 dlam, jnp.sum(dy * za, axis=0, keepdims=True))
        first = pl.program_id(1) == 0
        for r, v in zip((dbre_ref, dbim_ref, dcre_ref, dcim_ref, dlam_ref, dd_ref), parts):
            @pl.when(first)
            def _():
                r[...] = v

            @pl.when(jnp.logical_not(first))
            def _():
                r[...] += v

    blk3 = lambda r, c: pl.BlockSpec((None, r, c), lambda j, b: (j, 0, 0))
    tok = lambda c: pl.BlockSpec((seq, c), lambda j, b: (b, j))
    return pl.pallas_call(
        body, name="s5_bwd", grid=(nb, nseq),
        in_specs=[tok(ch), tok(ch), tok(ns), tok(ns), blk3(ch, ns), blk3(ch, ns), blk3(ns, ch), blk3(ns, ch),
                  pl.BlockSpec((8 * SUBLANES, ns), lambda j, b: (0, j)), pl.BlockSpec((1, ch), lambda j, b: (0, j))],
        out_specs=[tok(ch), blk3(ch, ns), blk3(ch, ns), blk3(ns, ch), blk3(ns, ch),
                   pl.BlockSpec((None, 2, ns), lambda j, b: (j, 0, 0)), pl.BlockSpec((1, ch), lambda j, b: (0, j))],
        out_shape=[jax.ShapeDtypeStruct((nseq * seq, nb * ch), BF16),
                   jax.ShapeDtypeStruct((nb, ch, ns), F32), jax.ShapeDtypeStruct((nb, ch, ns), F32),
                   jax.ShapeDtypeStruct((nb, ns, ch), F32), jax.ShapeDtypeStruct((nb, ns, ch), F32),
                   jax.ShapeDtypeStruct((nb, 2, ns), F32), jax.ShapeDtypeStruct((1, nb * ch), F32)],
        scratch_shapes=[pltpu.VMEM((seq, ns), F32), pltpu.VMEM((seq, ns), F32)],
        compiler_params=_params("arbitrary", "arbitrary"),
    )(dy, z, xre, xim, bre3, bim3, cre3, cim3, coef_rev, dskip)


def _cumsum_rows(x, reverse=False):
    n = x.shape[0]
    row = lax.broadcasted_iota(jnp.int32, x.shape, 0)
    s = 1
    while s < n:
        if reverse:
            x = x + jnp.where(row < n - s, pltpu.roll(x, n - s, 0), 0.0)
        else:
            x = x + jnp.where(row >= s, pltpu.roll(x, s, 0), 0.0)
        s *= 2
    return x


def _hg_gates(zq, zf, lb):
    sg = _sigmoid(zf)
    f = lb + (1.0 - lb) * sg
    sq = _sigmoid(zq)
    qa = zq * sq * (HEAD ** -0.5)
    b = _cumsum_rows(jnp.log(f))
    return sg, f, sq, qa, 1.0 - f, b


def _hg_scores(qa, kk, b):
    c = qa.shape[0]
    row = lax.broadcasted_iota(jnp.int32, qa.shape, 0)
    dmat = lax.broadcasted_iota(jnp.int32, (c, c), 0) - lax.broadcasted_iota(jnp.int32, (c, c), 1)
    p = jnp.zeros((c, c), F32)
    for d in range(c):
        if d == 0:
            fd = qa * kk
        else:
            e = jnp.exp(jnp.minimum(b - pltpu.roll(b, d, 0), 0.0))
            fd = jnp.where(row >= d, qa * pltpu.roll(kk, d, 0) * e, 0.0)
        p = jnp.where(dmat == d, jnp.sum(fd, axis=1, keepdims=True), p)
    return p


def _hg_scores_bwd(dp, qa, kk, b):
    c = qa.shape[0]
    row = lax.broadcasted_iota(jnp.int32, qa.shape, 0)
    dmat = lax.broadcasted_iota(jnp.int32, (c, c), 0) - lax.broadcasted_iota(jnp.int32, (c, c), 1)
    dqa = jnp.zeros_like(qa)
    dkk = jnp.zeros_like(qa)
    db = jnp.zeros_like(qa)
    for d in range(c):
        dcol = jnp.sum(jnp.where(dmat == d, dp, 0.0), axis=1, keepdims=True)
        if d == 0:
            dqa = dqa + dcol * kk
            dkk = dkk + dcol * qa
        else:
            e = jnp.exp(jnp.minimum(b - pltpu.roll(b, d, 0), 0.0))
            w = jnp.where(row >= d, dcol * e, 0.0)
            kr = pltpu.roll(kk, d, 0)
            dqa = dqa + w * kr
            tmp = w * qa
            dkk = dkk + pltpu.roll(tmp, c - d, 0)
            x = tmp * kr
            db = db + x - pltpu.roll(x, c - d, 0)
    return dqa, dkk, db


def _hg_specs(nseq, seq, heads, qoff):
    nc = seq // CHUNK
    zspec = lambda off: pl.BlockSpec((CHUNK, HEAD), lambda h, b, n, off=off: (b * nc + n, off + h))
    return nc, [zspec(qoff), zspec(qoff + heads), zspec(qoff + 2 * heads), zspec(qoff + 3 * heads)]


def _hg_fwd(z, lbrow, gain, *, nseq, seq, heads, qoff):
    nc, zspecs = _hg_specs(nseq, seq, heads, qoff)

    def body(zq_ref, zf_ref, zi_ref, zg_ref, lb_ref, gn_ref, o_ref, yb_ref, st_ref, state):
        @pl.when(pl.program_id(2) == 0)
        def _():
            state[...] = jnp.zeros_like(state)

        st = state[...]
        st_ref[...] = st
        zi = zi_ref[...]
        zg = zg_ref[...]
        _, _, _, qa, kk, b = _hg_gates(zq_ref[...], zf_ref[...], lb_ref[...])
        o = _dot_nt(qa * jnp.exp(b), st) + _dot(_hg_scores(qa, kk, b), zi)
        bl = b[CHUNK - 1:CHUNK, :]
        state[...] = st * jnp.exp(bl) + _dot_tn(zi, kk * jnp.exp(bl - b))
        o_ref[...] = o
        r = lax.rsqrt(jnp.mean(o * o, axis=1, keepdims=True) + EPS)
        yb_ref[...] = (o * r * gn_ref[...] * zg * _sigmoid(zg)).astype(yb_ref.dtype)

    tok = pl.BlockSpec((CHUNK, HEAD), lambda h, b, n: (b * nc + n, h))
    vec = pl.BlockSpec((1, HEAD), lambda h, b, n: (0, h))
    rows = nseq * seq
    return pl.pallas_call(
        body, name="hgrn2_fwd", grid=(heads, nseq, nc),
        in_specs=zspecs + [vec, vec],
        out_specs=[tok, tok, pl.BlockSpec((None, None, None, HEAD, HEAD), lambda h, b, n: (h, b, n, 0, 0))],
        out_shape=[jax.ShapeDtypeStruct((rows, heads * HEAD), F32), jax.ShapeDtypeStruct((rows, heads * HEAD), BF16),
                   jax.ShapeDtypeStruct((heads, nseq, nc, HEAD, HEAD), F32)],
        scratch_shapes=[pltpu.VMEM((HEAD, HEAD), F32)],
        compiler_params=_params("arbitrary", "arbitrary", "arbitrary"),
    )(z, z, z, z, lbrow, gain)


def _hg_bwd(dyb, z, o, states, lbrow, gain, *, nseq, seq, heads, qoff):
    nc = seq // CHUNK
    rev = lambda n: nc - 1 - n
    zspec = lambda off: pl.BlockSpec((CHUNK, HEAD), lambda h, b, n, off=off: (b * nc + rev(n), off + h))

    def body(dyb_ref, zq_ref, zf_ref, zi_ref, zg_ref, o_ref, st_ref, lb_ref, gn_ref,
             dzq_ref, dzf_ref, dzi_ref, dzg_ref, dlb_ref, dgn_ref, dstate):
        @pl.when(pl.program_id(2) == 0)
        def _():
            dstate[...] = jnp.zeros_like(dstate)

        lb = lb_ref[...]
        gain_v = gn_ref[...]
        zq = zq_ref[...]
        zi = zi_ref[...]
        zg = zg_ref[...]
        sg, f, sq, qa, kk, b = _hg_gates(zq, zf_ref[...], lb)
        eb = jnp.exp(b)
        qt = qa * eb
        bl = b[CHUNK - 1:CHUNK, :]
        ebl = jnp.exp(bl)
        ekb = jnp.exp(bl - b)
        kh = kk * ekb
        st = st_ref[...]
        dst = dstate[...]
        o = o_ref[...]
        r = lax.rsqrt(jnp.mean(o * o, axis=1, keepdims=True) + EPS)
        oh = o * r
        szg = _sigmoid(zg)
        dyb = dyb_ref[...]
        don = dyb * zg * szg
        dzg_ref[...] = (dyb * oh * gain_v * szg * (1.0 + zg * (1.0 - szg))).astype(dzg_ref.dtype)
        doh = don * gain_v
        do = r * (doh - oh * jnp.mean(doh * oh, axis=1, keepdims=True))
        c = CHUNK
        causal = lax.broadcasted_iota(jnp.int32, (c, c), 0) >= lax.broadcasted_iota(jnp.int32, (c, c), 1)
        dqt = _dot(do, st)
        dp = jnp.where(causal, _dot_nt(do, zi), 0.0)
        p = _hg_scores(qa, kk, b)
        dzi_ref[...] = (_dot_tn(p, do) + _dot_nt(kh, dst)).astype(dzi_ref.dtype)
        dkh = _dot(zi, dst)
        dbl = jnp.sum(dkh * kh, axis=0, keepdims=True) + jnp.sum(dst * st, axis=0, keepdims=True) * ebl
        dstate[...] = _dot_tn(do, qt) + dst * ebl
        dqa_s, dkk_s, db_s = _hg_scores_bwd(dp, qa, kk, b)
        dqa = dqt * eb + dqa_s
        dkk = dkh * ekb + dkk_s
        db = dqt * qt - dkh * kh + db_s
        row = lax.broadcasted_iota(jnp.int32, db.shape, 0)
        db = db + jnp.where(row == c - 1, dbl, 0.0)
        df = _cumsum_rows(db, reverse=True) / f - dkk
        dzf_ref[...] = (df * (1.0 - lb) * sg * (1.0 - sg)).astype(dzf_ref.dtype)
        dzq_ref[...] = (dqa * (HEAD ** -0.5) * sq * (1.0 + zq * (1.0 - sq))).astype(dzq_ref.dtype)
        parts = (jnp.sum(df * (1.0 - sg), axis=0, keepdims=True), jnp.sum(don * oh, axis=0, keepdims=True))
        first = jnp.logical_and(pl.program_id(1) == 0, pl.program_id(2) == 0)
        for rf, v in zip((dlb_ref, dgn_ref), parts):
            @pl.when(first)
            def _():
                rf[...] = v

            @pl.when(jnp.logical_not(first))
            def _():
                rf[...] += v

    tok = pl.BlockSpec((CHUNK, HEAD), lambda h, b, n: (b * nc + rev(n), h))
    vec = pl.BlockSpec((1, HEAD), lambda h, b, n: (0, h))
    rows = nseq * seq
    return pl.pallas_call(
        body, name="hgrn2_bwd", grid=(heads, nseq, nc),
        in_specs=[tok, zspec(qoff), zspec(qoff + heads), zspec(qoff + 2 * heads), zspec(qoff + 3 * heads), tok,
                  pl.BlockSpec((None, None, None, HEAD, HEAD), lambda h, b, n: (h, b, rev(n), 0, 0)), vec, vec],
        out_specs=[tok, tok, tok, tok, vec, vec],
        out_shape=[jax.ShapeDtypeStruct((rows, heads * HEAD), BF16)] * 4
        + [jax.ShapeDtypeStruct((1, heads * HEAD), F32)] * 2,
        scratch_shapes=[pltpu.VMEM((HEAD, HEAD), F32)],
        compiler_params=_params("arbitrary", "arbitrary", "arbitrary"),
    )(dyb, z, z, z, z, o, states, lbrow, gain)


def _conv_taps(h, w, bias):
    row = lax.broadcasted_iota(jnp.int32, h.shape, 0)
    h1 = jnp.where(row >= 1, pltpu.roll(h, 1, 0), 0.0)
    h2 = jnp.where(row >= 2, pltpu.roll(h, 2, 0), 0.0)
    return h2 * w[0:1, :] + h1 * w[1:2, :] + h * w[2:3, :] + bias, h1, h2


def _conv_fwd(h, wconv, bconv, *, nseq, seq):
    ff2 = h.shape[1]
    ncol = ff2 // 2 // LANES

    def body(hg_ref, hv_ref, wg_ref, wv_ref, bg_ref, bv_ref, a_ref):
        g, _, _ = _conv_taps(hg_ref[...], wg_ref[...], bg_ref[...])
        v, _, _ = _conv_taps(hv_ref[...], wv_ref[...], bv_ref[...])
        a_ref[...] = (g * _sigmoid(g) * v).astype(a_ref.dtype)

    tok = lambda off: pl.BlockSpec((seq, LANES), lambda j, b, off=off: (b, off + j))
    wsp = lambda off: pl.BlockSpec((CONV_W, LANES), lambda j, b, off=off: (0, off + j))
    bsp = lambda off: pl.BlockSpec((1, LANES), lambda j, b, off=off: (0, off + j))
    return pl.pallas_call(
        body, name="conv_fwd", grid=(ncol, nseq),
        in_specs=[tok(0), tok(ncol), wsp(0), wsp(ncol), bsp(0), bsp(ncol)],
        out_specs=tok(0), out_shape=jax.ShapeDtypeStruct((nseq * seq, ff2 // 2), BF16),
        compiler_params=_params("arbitrary", "arbitrary"),
    )(h, h, wconv, wconv, bconv, bconv)


def _conv_bwd(da, h, wconv, bconv, *, nseq, seq):
    ff2 = h.shape[1]
    ncol = ff2 // 2 // LANES

    def half_bwd(d, hcur, h1, h2, w):
        n = d.shape[0]
        row = lax.broadcasted_iota(jnp.int32, d.shape, 0)
        d1 = jnp.where(row < n - 1, pltpu.roll(d, n - 1, 0), 0.0)
        d2 = jnp.where(row < n - 2, pltpu.roll(d, n - 2, 0), 0.0)
        dh = d * w[2:3, :] + d1 * w[1:2, :] + d2 * w[0:1, :]
        stats = jnp.concatenate(
            [jnp.sum(h2 * d, axis=0, keepdims=True), jnp.sum(h1 * d, axis=0, keepdims=True),
             jnp.sum(hcur * d, axis=0, keepdims=True), jnp.sum(d, axis=0, keepdims=True),
             jnp.zeros((SUBLANES - 4, d.shape[1]), F32)], axis=0)
        return dh, stats

    def body(da_ref, hg_ref, hv_ref, wg_ref, wv_ref, bg_ref, bv_ref, dhg_ref, dhv_ref, sg_ref, sv_ref):
        hg = hg_ref[...]
        hv = hv_ref[...]
        wg = wg_ref[...]
        wv = wv_ref[...]
        g, g1, g2 = _conv_taps(hg, wg, bg_ref[...])
        v, v1, v2 = _conv_taps(hv, wv, bv_ref[...])
        da = da_ref[...]
        s = _sigmoid(g)
        dhg, stg = half_bwd(da * v * s * (1.0 + g * (1.0 - s)), hg, g1, g2, wg)
        dhv, stv = half_bwd(da * g * s, hv, v1, v2, wv)
        dhg_ref[...] = dhg.astype(dhg_ref.dtype)
        dhv_ref[...] = dhv.astype(dhv_ref.dtype)
        first = pl.program_id(1) == 0
        for r, val in ((sg_ref, stg), (sv_ref, stv)):
            @pl.when(first)
            def _():
                r[...] = val

            @pl.when(jnp.logical_not(first))
            def _():
                r[...] += val

    tok = lambda off: pl.BlockSpec((seq, LANES), lambda j, b, off=off: (b, off + j))
    wsp = lambda off: pl.BlockSpec((CONV_W, LANES), lambda j, b, off=off: (0, off + j))
    bsp = lambda off: pl.BlockSpec((1, LANES), lambda j, b, off=off: (0, off + j))
    ssp = pl.BlockSpec((SUBLANES, LANES), lambda j, b: (0, j))
    dhg, dhv, stg, stv = pl.pallas_call(
        body, name="conv_bwd", grid=(ncol, nseq),
        in_specs=[tok(0), tok(0), tok(ncol), wsp(0), wsp(ncol), bsp(0), bsp(ncol)],
        out_specs=[tok(0), tok(0), ssp, ssp],
        out_shape=[jax.ShapeDtypeStruct((nseq * seq, ff2 // 2), BF16)] * 2
        + [jax.ShapeDtypeStruct((SUBLANES, ff2 // 2), F32)] * 2,
        compiler_params=_params("arbitrary", "arbitrary"),
    )(da, h, h, wconv, wconv, bconv, bconv)
    return jnp.concatenate([dhg, dhv], axis=1), jnp.concatenate([stg, stv], axis=1)


def _rms_fwd(xv, g):
    r = lax.rsqrt(jnp.mean(xv * xv, axis=1, keepdims=True) + EPS)
    return (xv * r * g,)


def _rms_bwd(xv, g, dy, res):
    r = lax.rsqrt(jnp.mean(xv * xv, axis=1, keepdims=True) + EPS)
    xh = xv * r
    dxh = dy * g
    dx = r * (dxh - xh * jnp.mean(dxh * xh, axis=1, keepdims=True)) + res
    return dx, jnp.sum(dy * xh, axis=0, keepdims=True)


def _loss_head(x2, tgt, g):
    d = x2.shape[1]
    r = lax.rsqrt(jnp.mean(x2 * x2, axis=1, keepdims=True) + EPS)
    xh = x2 * r
    err = xh * g - tgt
    dy = err * (1.0 / d)
    dxh = dy * g
    dx = r * (dxh - xh * jnp.mean(dxh * xh, axis=1, keepdims=True))
    loss = 0.5 * jnp.sum(jnp.mean(err * err, axis=1, keepdims=True), axis=0, keepdims=True)
    return dx, jnp.sum(dy * xh, axis=0, keepdims=True), jnp.broadcast_to(loss, (1, LANES))


def _local_step(x, tgt, p, *, nseq, seq):
    t, d = x.shape
    s5w = p["s5_d"].shape[1]
    hgw = p["gain"].shape[1]
    heads = hgw // HEAD
    qoff = s5w // LANES
    gblk = (s5w + 4 * hgw) // GATE_BLOCK
    ngb = d // GATE_BLOCK
    tm = _row_tile(t, 256)
    row = lambda a, w=None, base=0: (a, a.shape[1] if w is None else w, base, "row")
    vec = lambda a, w=None, base=0: (a, a.shape[1] if w is None else w, base, "vec")
    rw = functools.partial(_rowwise, rows=t, tm=tm)

    (u,) = rw("rms_mix", _rms_fwd, [row(x), vec(p["g_mix"])], [(d, d, BF16)])
    z = _mm_fwd_cols("in_proj", u, p["w_in"])

    lam_re, lam_im, bb_re, bb_im = _s5_discretize(p["s5_a_re"], p["s5_a_im"], p["s5_log_dt"], p["s5_b_re"], p["s5_b_im"])
    bre3 = _s5_in_blocks(bb_re).astype(BF16)
    bim3 = _s5_in_blocks(bb_im).astype(BF16)
    cre3 = _s5_out_blocks(p["s5_c_re"]).astype(BF16)
    cim3 = _s5_out_blocks(p["s5_c_im"]).astype(BF16)
    coef_f = _s5_scan_tables(lam_re.reshape(-1), lam_im.reshape(-1), False)
    coef_r = _s5_scan_tables(lam_re.reshape(-1), lam_im.reshape(-1), True)
    y5, xre, xim = _s5_fwd(z, bre3, bim3, cre3, cim3, coef_f, p["s5_d"], nseq=nseq, seq=seq)
    (ya0,) = rw("s5_gelu", lambda y: (_gelu(y),), [row(y5)], [(s5w, s5w, BF16)])
    gl = _mm_fwd_rows("glu_proj", ya0, p["w_glu"])
    (ya,) = rw("s5_glu", lambda y, g, b: (_gelu(y) * _sigmoid(g + b),), [row(y5), row(gl), vec(p["b_glu"])],
               [(s5w, s5w, BF16)])

    o, yb, states = _hg_fwd(z, p["lbrow"], p["gain"], nseq=nseq, seq=seq, heads=heads, qoff=qoff)

    pa = _mm_fwd_cols("proj_a", ya, p["w_pa"])
    pb = _mm_fwd_cols("proj_b", yb, p["w_pb"])
    gb = GATE_BLOCK
    (m,) = rw("merge", lambda ga, gbv, a, b: (_sigmoid(ga) * a + _sigmoid(gbv) * b,),
              [row(z, gb, gblk), row(z, gb, gblk + ngb), row(pa, gb), row(pb, gb)], [(d, gb, BF16)], ncol=ngb)
    x1 = _mm_fwd_rows("out_proj", m, p["w_out"], res=x)

    (u2,) = rw("rms_ffn", _rms_fwd, [row(x1), vec(p["g_ffn"])], [(d, d, BF16)])
    h = _mm_fwd_cols("up_proj", u2, p["w_up"])
    a = _conv_fwd(h, p["w_conv"], p["b_conv"], nseq=nseq, seq=seq)
    x2 = _mm_fwd_rows("down_proj", a, p["w_down"], res=x1)

    dx2, dg_final, lossv = rw("loss_head", _loss_head, [row(x2), row(tgt), vec(p["g_final"])], [(d, d, F32)],
                              accs=[(d, d), (LANES, LANES)])

    da = _mm_bwd_rows("down_bwd", dx2, p["w_down"])
    g_wdown = _mm_wgrad_rows("down_wgrad", a, dx2)
    dh, cstats = _conv_bwd(da, h, p["w_conv"], p["b_conv"], nseq=nseq, seq=seq)
    du2 = _mm_bwd_cols("up_bwd", dh, p["w_up"])
    g_wup = _mm_wgrad_cols("up_wgrad", u2, dh)
    dx1, dg_ffn = rw("rms_ffn_bwd", _rms_bwd, [row(x1), vec(p["g_ffn"]), row(du2), row(dx2)], [(d, d, F32)],
                     accs=[(d, d)])

    dm = _mm_bwd_rows("out_bwd", dx1, p["w_out"])
    g_wout = _mm_wgrad_rows("out_wgrad", m, dx1)

    def merge_bwd(ga, gbv, av, bv, dmv):
        sa = _sigmoid(ga)
        sb = _sigmoid(gbv)
        return dmv * sa, dmv * sb, dmv * av * sa * (1.0 - sa), dmv * bv * sb * (1.0 - sb)

    dpa, dpb, dzga, dzgb = rw("merge_bwd", merge_bwd,
                              [row(z, gb, gblk), row(z, gb, gblk + ngb), row(pa, gb), row(pb, gb), row(dm, gb)],
                              [(d, gb, BF16)] * 4, ncol=ngb)
    dya = _mm_bwd_cols("proj_a_bwd", dpa, p["w_pa"])
    g_wpa = _mm_wgrad_cols("proj_a_wgrad", ya, dpa)
    dyb = _mm_bwd_cols("proj_b_bwd", dpb, p["w_pb"])
    g_wpb = _mm_wgrad_cols("proj_b_wgrad", yb, dpb)

    def glu_bwd1(y, g, b, dyv):
        s = _sigmoid(g + b)
        dgl = dyv * _gelu(y) * s * (1.0 - s)
        return dgl, jnp.sum(dgl, axis=0, keepdims=True)

    dgl, db_glu = rw("s5_glu_bwd", glu_bwd1, [row(y5), row(gl), vec(p["b_glu"]), row(dya)], [(s5w, s5w, BF16)],
                     accs=[(s5w, s5w)])
    dgl_in = _mm_bwd_rows("glu_bwd", dgl, p["w_glu"])
    g_wglu = _mm_wgrad_rows("glu_wgrad", ya0, dgl)
    (dy5,) = rw("s5_gelu_bwd", lambda y, g, b, dyv, tv: ((dyv * _sigmoid(g + b) + tv) * _gelu_grad(y),),
                [row(y5), row(gl), vec(p["b_glu"]), row(dya), row(dgl_in)], [(s5w, s5w, F32)])
    dza, dbre3, dbim3, dcre3, dcim3, dlam, dd = _s5_bwd(dy5, z, xre, xim, bre3, bim3, cre3, cim3, coef_r, p["s5_d"],
                                                        nseq=nseq, seq=seq)

    dzq, dzf, dzi, dzg, dlb, dgain = _hg_bwd(dyb, z, o, states, p["lbrow"], p["gain"], nseq=nseq, seq=seq,
                                             heads=heads, qoff=qoff)

    dz = jnp.concatenate([dza, dzq, dzf, dzi, dzg, dzga, dzgb], axis=1)
    du = _mm_bwd_cols("in_bwd", dz, p["w_in"])
    g_win = _mm_wgrad_cols("in_wgrad", u, dz)
    dx, dg_mix = rw("rms_mix_bwd", _rms_bwd, [row(x), vec(p["g_mix"]), row(du), row(dx1)], [(d, d, F32)],
                    accs=[(d, d)])

    gshape = lam_re.shape
    big = {"w_in": g_win, "w_glu": g_wglu, "w_pa": g_wpa, "w_pb": g_wpb, "w_out": g_wout, "w_up": g_wup,
           "w_down": g_wdown}
    small = {
        "loss": lossv, "g_mix": dg_mix, "g_ffn": dg_ffn, "g_final": dg_final, "b_glu": db_glu, "gain": dgain,
        "lbrow": dlb, "s5_d": dd, "w_conv": cstats[0:CONV_W], "b_conv": cstats[CONV_W:CONV_W + 1],
        "lam_re": dlam[:, 0, :].reshape(gshape), "lam_im": dlam[:, 1, :].reshape(gshape),
        "bb_re": _s5_in_blocks_diag(dbre3), "bb_im": _s5_in_blocks_diag(dbim3),
        "s5_c_re": _s5_out_blocks_diag(dcre3), "s5_c_im": _s5_out_blocks_diag(dcim3),
    }
    return dx, big, small


ANY = pl.BlockSpec(memory_space=pl.ANY)


def _place():
    x, y, c = lax.axis_index("x"), lax.axis_index("y"), lax.axis_index("c")
    chips = [(1 - x, y), (x, 1 - y), (1 - x, 1 - y)]
    return x, y, c, chips


def _remote(src, dst, send_sems, recv_sems, k, to):
    return pltpu.make_async_remote_copy(src_ref=src, dst_ref=dst, send_sem=send_sems.at[k], recv_sem=recv_sems.at[k],
                                        device_id=to, device_id_type=MESH)


def _half(rows, which):
    return pl.ds(pl.multiple_of(which * (rows // 2), 16), rows // 2)


def _gather_weights(shards, whole):
    n, nw = len(shards), len(whole)
    arrays = list(shards) + list(whole)

    def body(*refs):
        in_refs, out_refs = refs[:n + nw], refs[n + nw:2 * (n + nw)]
        send_sems, recv_sems = refs[2 * (n + nw):]
        x, y, c, chips = _place()
        me = 2 * x + y
        copy = functools.partial(_remote, send_sems=send_sems, recv_sems=recv_sems)
        sends = []
        for a in range(n):
            mine_half = _half(arrays[a].shape[0], c)
            for j, (cx, cy) in enumerate(chips):
                sends.append(copy(in_refs[a].at[mine_half], out_refs[a].at[me, mine_half], k=6 * a + j, to=(cx, cy, c)))
        for a in range(n, n + nw):
            for j, (cx, cy) in enumerate(chips):
                sends.append(copy(in_refs[a], out_refs[a].at[me], k=6 * n + 3 * (a - n) + j, to=(cx, cy, c)))
        for cp in sends:
            cp.start()
        for a in range(n):
            mine_half = _half(arrays[a].shape[0], c)
            for j, (cx, cy) in enumerate(chips):
                landed = out_refs[a].at[2 * cx + cy, mine_half]
                copy(landed, landed, k=6 * a + j, to=(x, y, c)).wait_recv()
                fwd = copy(landed, landed, k=6 * a + 3 + j, to=(x, y, 1 - c))
                fwd.start()
                sends.append(fwd)
        for a in range(n):
            other_half = _half(arrays[a].shape[0], 1 - c)
            for j, (cx, cy) in enumerate(chips):
                landed = out_refs[a].at[2 * cx + cy, other_half]
                copy(landed, landed, k=6 * a + 3 + j, to=(x, y, c)).wait_recv()
        for a in range(n, n + nw):
            for j, (cx, cy) in enumerate(chips):
                landed = out_refs[a].at[2 * cx + cy]
                copy(landed, landed, k=6 * n + 3 * (a - n) + j, to=(x, y, c)).wait_recv()
        for cp in sends:
            cp.wait_send()

    nsem = 6 * n + 3 * nw
    return pl.pallas_call(
        body, name="gather_weights", out_shape=[jax.ShapeDtypeStruct((N_CHIPS,) + a.shape, a.dtype) for a in arrays],
        in_specs=[ANY] * (n + nw), out_specs=[ANY] * (n + nw),
        scratch_shapes=[pltpu.SemaphoreType.DMA((nsem,)), pltpu.SemaphoreType.DMA((nsem,))],
    )(*arrays)


def _swap_halves(parts):
    n = len(parts)

    def body(*refs):
        in_refs, out_refs, send_sems, recv_sems = refs[:n], refs[n:2 * n], refs[2 * n], refs[2 * n + 1]
        x, y, c, _ = _place()
        copies = [_remote(in_refs[a].at[:, _half(parts[a].shape[1], 1 - c), :], out_refs[a], send_sems, recv_sems, a,
                          (x, y, 1 - c)) for a in range(n)]
        for cp in copies:
            cp.start()
        for cp in copies:
            cp.wait()

    return pl.pallas_call(
        body, name="grad_swap_halves",
        out_shape=[jax.ShapeDtypeStruct((g.shape[0], g.shape[1] // 2, g.shape[2]), g.dtype) for g in parts],
        in_specs=[ANY] * n, out_specs=[ANY] * n,
        scratch_shapes=[pltpu.SemaphoreType.DMA((n,)), pltpu.SemaphoreType.DMA((n,))],
    )(*parts)


def _scatter_to_chips(parts):
    n = len(parts)

    def body(*refs):
        in_refs, out_refs, send_sems, recv_sems = refs[:n], refs[n:2 * n], refs[2 * n], refs[2 * n + 1]
        x, y, c, chips = _place()
        copies = [_remote(in_refs[a].at[2 * cx + cy], out_refs[a].at[j], send_sems, recv_sems, 3 * a + j, (cx, cy, c))
                  for a in range(n) for j, (cx, cy) in enumerate(chips)]
        for cp in copies:
            cp.start()
        for cp in copies:
            cp.wait()

    return pl.pallas_call(
        body, name="grad_scatter_chips",
        out_shape=[jax.ShapeDtypeStruct((N_CHIPS - 1,) + h.shape[1:], h.dtype) for h in parts],
        in_specs=[ANY] * n, out_specs=[ANY] * n,
        scratch_shapes=[pltpu.SemaphoreType.DMA((3 * n,)), pltpu.SemaphoreType.DMA((3 * n,))],
    )(*parts)


def _swap_sums(parts):
    n = len(parts)

    def body(*refs):
        in_refs, out_refs, send_sems, recv_sems = refs[:n], refs[n:2 * n], refs[2 * n], refs[2 * n + 1]
        x, y, c, _ = _place()
        copies = [_remote(in_refs[a], out_refs[a], send_sems, recv_sems, a, (x, y, 1 - c)) for a in range(n)]
        for cp in copies:
            cp.start()
        for cp in copies:
            cp.wait()

    return pl.pallas_call(
        body, name="grad_swap_sums", out_shape=[jax.ShapeDtypeStruct(g.shape, g.dtype) for g in parts],
        in_specs=[ANY] * n, out_specs=[ANY] * n,
        scratch_shapes=[pltpu.SemaphoreType.DMA((n,)), pltpu.SemaphoreType.DMA((n,))],
    )(*parts)


def _gather_all(v):
    m_per, n = v.shape

    def body(x_ref, out_ref, send_sems, recv_sems):
        x, y, c, chips = _place()
        me, sibling = (x, y, c), (x, y, 1 - c)

        def rows(px, py, pc):
            return out_ref.at[pl.ds(pl.multiple_of((4 * px + 2 * py + pc) * m_per, 8), m_per), :]

        def copy(k, block, to, src=None):
            return pltpu.make_async_remote_copy(src_ref=rows(*block) if src is None else src, dst_ref=rows(*block),
                                                send_sem=send_sems.at[k], recv_sem=recv_sems.at[k], device_id=to,
                                                device_id_type=MESH)

        out_ref[pl.ds(pl.multiple_of((4 * x + 2 * y + c) * m_per, 8), m_per), :] = x_ref[...]
        first = [copy(0, me, sibling, src=x_ref)]
        first += [copy(1 + j, me, (*chip, c), src=x_ref) for j, chip in enumerate(chips)]
        for cp in first:
            cp.start()
        passed = [copy(4 + j, (*chip, c), sibling) for j, chip in enumerate(chips)]
        for j, chip in enumerate(chips):
            copy(1 + j, (*chip, c), me).wait_recv()
            passed[j].start()
        copy(0, sibling, me).wait_recv()
        for j, chip in enumerate(chips):
            copy(4 + j, (*chip, 1 - c), me).wait_recv()
        for cp in first + passed:
            cp.wait_send()

    return pl.pallas_call(
        body, name="gather_small_grads", out_shape=jax.ShapeDtypeStruct((N_DEV * m_per, n), v.dtype),
        in_specs=[pl.BlockSpec(memory_space=pltpu.VMEM)], out_specs=pl.BlockSpec(memory_space=pltpu.VMEM),
        scratch_shapes=[pltpu.SemaphoreType.DMA((7,)), pltpu.SemaphoreType.DMA((7,))],
        compiler_params=pltpu.CompilerParams(vmem_limit_bytes=VMEM_LIMIT_BYTES),
    )(v)


def _sum_blocks(name, parts, out_dtype):
    rows, cols = parts[0].shape
    tm = _row_tile(rows, 512)

    def body(*refs):
        acc = refs[0][...].astype(F32)
        for r in refs[1:-1]:
            acc = acc + r[...].astype(F32)
        refs[-1][...] = acc.astype(refs[-1].dtype)

    spec = pl.BlockSpec((tm, cols), lambda i: (i, 0))
    return pl.pallas_call(
        body, name=name, grid=(rows // tm,), in_specs=[spec] * len(parts), out_specs=spec,
        out_shape=jax.ShapeDtypeStruct((rows, cols), out_dtype), compiler_params=_params("arbitrary"),
    )(*parts)


def _adamw_math(wv, gv, mv, vv):
    m2 = ADAM_B1 * mv + (1.0 - ADAM_B1) * gv
    v2 = ADAM_B2 * vv + (1.0 - ADAM_B2) * (gv * gv)
    delta = -ADAM_LR * ((m2 / (1.0 - ADAM_B1 ** ADAM_STEP)) / (jnp.sqrt(v2 / (1.0 - ADAM_B2 ** ADAM_STEP)) + ADAM_EPS)
                        + ADAM_WD * wv)
    return delta, m2, v2


def _adamw_small(ws, gs, ms, vs):
    n = len(ws)

    def body(*refs):
        for i in range(n):
            res = _adamw_math(refs[i][...], refs[n + i][...], refs[2 * n + i][...], refs[3 * n + i][...])
            for k in range(3):
                refs[(4 + k) * n + i][...] = res[k]

    vm = pl.BlockSpec(memory_space=pltpu.VMEM)
    outs = pl.pallas_call(
        body, name="adamw_small", in_specs=[vm] * (4 * n), out_specs=[vm] * (3 * n),
        out_shape=[jax.ShapeDtypeStruct(a.shape, F32) for a in ws] * 3,
        compiler_params=pltpu.CompilerParams(vmem_limit_bytes=VMEM_LIMIT_BYTES),
    )(*ws, *gs, *ms, *vs)
    return outs[:n], outs[n:2 * n], outs[2 * n:]


def _adamw(name, w, g, m, v):
    rows, cols = w.shape
    ins = [(a, cols, 0, "row") for a in (w, g, m, v)]
    return _rowwise(name, _adamw_math, ins, [(cols, cols, F32)] * 3, rows=rows, tm=_row_tile(rows, 256))


PACK_ROWS = 256


def _pack(flat_parts, dtype, lead=()):
    parts = [a.astype(dtype).reshape(lead + (-1,)) for a in flat_parts]
    n = sum(a.shape[-1] for a in parts)
    chunk = PACK_ROWS * LANES
    total = -(-n // chunk) * chunk
    if total > n:
        parts.append(jnp.zeros(lead + (total - n,), dtype))
    return jnp.concatenate(parts, axis=-1).reshape(lead + (total // LANES, LANES))


def _unpack(buf, shapes, lead=()):
    flat = buf.reshape(lead + (-1,))
    out, off = [], 0
    for shp in shapes:
        n = math.prod(shp)
        out.append(lax.slice_in_dim(flat, off, off + n, axis=len(lead)).reshape(lead + tuple(shp)))
        off += n
    return out


BIG = ("w_in", "w_glu", "w_pa", "w_pb", "w_out", "w_up", "w_down")
WEIGHTS = ("g_mix", "w_in", "s5_a_re", "s5_a_im", "s5_log_dt", "s5_b_re", "s5_b_im", "s5_c_re", "s5_c_im", "s5_d",
           "w_glu", "b_glu", "hg_lb_logits", "hg_norm_gain", "w_pa", "w_pb", "w_out", "g_ffn", "w_up", "w_conv",
           "b_conv", "w_down", "g_final")
SMALL = tuple(n for n in WEIGHTS if n not in BIG)
SMALL_PARTS = ("loss", "g_mix", "g_ffn", "g_final", "b_glu", "gain", "lbrow", "s5_d", "w_conv", "b_conv", "lam_re",
               "lam_im", "bb_re", "bb_im", "s5_c_re", "s5_c_im")


def _lower_bound(logits):
    return jnp.cumsum(jax.nn.softmax(logits, axis=0), axis=0)[0:1]


def kernel(x, g_mix, w_in, s5_a_re, s5_a_im, s5_log_dt, s5_b_re, s5_b_im, s5_c_re, s5_c_im, s5_d, w_glu, b_glu, hg_lb_logits, hg_norm_gain, w_pa, w_pb, w_out, g_ffn, w_up, w_conv, b_conv, w_down, g_final, loss_target, m_g_mix, m_w_in, m_s5_a_re, m_s5_a_im, m_s5_log_dt, m_s5_b_re, m_s5_b_im, m_s5_c_re, m_s5_c_im, m_s5_d, m_w_glu, m_b_glu, m_hg_lb_logits, m_hg_norm_gain, m_w_pa, m_w_pb, m_w_out, m_g_ffn, m_w_up, m_w_conv, m_b_conv, m_w_down, m_g_final, v_g_mix, v_w_in, v_s5_a_re, v_s5_a_im, v_s5_log_dt, v_s5_b_re, v_s5_b_im, v_s5_c_re, v_s5_c_im, v_s5_d, v_w_glu, v_b_glu, v_hg_lb_logits, v_hg_norm_gain, v_w_pa, v_w_pb, v_w_out, v_g_ffn, v_w_up, v_w_conv, v_b_conv, v_w_down, v_g_final):
    args = dict(locals())
    w = {n: args[n] for n in WEIGHTS}
    mom = {n: args["m_" + n] for n in WEIGHTS}
    var = {n: args["v_" + n] for n in WEIGHTS}
    nseq, seq, d = x.shape
    xi, yi = lax.axis_index("x"), lax.axis_index("y")
    chip = 2 * xi + yi

    shard = {n: w[n][0] for n in BIG}
    shard16 = [shard[n].astype(BF16) for n in BIG]
    got = _gather_weights(shard16, [w_conv[0]])
    full = {n: lax.dynamic_update_index_in_dim(g, s, chip, 0) for n, g, s in zip(BIG, got, shard16)}
    for n in ("w_glu", "w_out", "w_down"):
        full[n] = full[n].reshape(-1, full[n].shape[-1])
    conv_all = lax.dynamic_update_index_in_dim(got[-1], w_conv[0], chip, 0)
    conv_full = conv_all.transpose(1, 0, 2).reshape(CONV_W, -1)

    p = dict(full)
    p.update(g_mix=g_mix, g_ffn=g_ffn, g_final=g_final.reshape(1, -1), b_glu=b_glu, gain=hg_norm_gain, s5_d=s5_d,
             b_conv=b_conv, w_conv=conv_full, lbrow=_lower_bound(hg_lb_logits),
             s5_a_re=s5_a_re[0], s5_a_im=s5_a_im[0], s5_log_dt=s5_log_dt[0], s5_b_re=s5_b_re[0], s5_b_im=s5_b_im[0],
             s5_c_re=s5_c_re[0], s5_c_im=s5_c_im[0])

    dx, gbig, gsmall = _local_step(x.reshape(nseq * seq, d), loss_target.reshape(nseq * seq, d), p, nseq=nseq, seq=seq)

    ci = lax.axis_index("c")
    parts = [gbig[n].reshape((N_CHIPS, -1, gbig[n].shape[-1])) for n in BIG]
    pair = []
    for n, g, s in zip(BIG, parts, _swap_halves(parts)):
        rh, cols = s.shape[1], s.shape[2]
        own = lax.dynamic_slice_in_dim(g, ci * rh, rh, axis=1)
        both = _sum_blocks("grad_pair_sum_" + n, [own.reshape(-1, cols), s.reshape(-1, cols)], BF16)
        pair.append(both.reshape(N_CHIPS, rh, cols))
    halves = []
    for n, h, o in zip(BIG, pair, _scatter_to_chips(pair)):
        mine = lax.dynamic_index_in_dim(h, chip, axis=0, keepdims=False)
        halves.append(_sum_blocks("grad_chip_sum_" + n, [mine, o[0], o[1], o[2]], F32))
    grads = {}
    for n, own, s in zip(BIG, halves, _swap_sums(halves)):
        both = jnp.concatenate([own, own], axis=0)
        grads[n] = lax.dynamic_update_slice_in_dim(both, s, (1 - ci) * own.shape[0], axis=0)

    small_shapes = [gsmall[n].shape for n in SMALL_PARTS]
    vec = _pack([gsmall[n] for n in SMALL_PARTS], F32)
    gathered = _gather_all(vec)
    mrows = vec.shape[0]
    vsum = _sum_blocks("small_grad_sum", [gathered[i * mrows:(i + 1) * mrows] for i in range(N_DEV)], F32)
    sm = dict(zip(SMALL_PARTS, _unpack(vsum, small_shapes)))
    loss = sm["loss"][0, 0]

    _, disc_vjp = jax.vjp(_s5_discretize, p["s5_a_re"], p["s5_a_im"], p["s5_log_dt"], p["s5_b_re"], p["s5_b_im"])
    da_re, da_im, dlog_dt, db_re, db_im = disc_vjp((sm["lam_re"], sm["lam_im"], sm["bb_re"], sm["bb_im"]))
    _, lb_vjp = jax.vjp(_lower_bound, hg_lb_logits)
    (dlogits,) = lb_vjp(sm["lbrow"])
    fcols = w_conv.shape[-1]
    grads.update(
        g_mix=sm["g_mix"], g_ffn=sm["g_ffn"], g_final=sm["g_final"].reshape(-1), b_glu=sm["b_glu"],
        hg_norm_gain=sm["gain"], hg_lb_logits=dlogits, s5_d=sm["s5_d"], b_conv=sm["b_conv"],
        w_conv=lax.dynamic_slice_in_dim(sm["w_conv"], chip * fcols, fcols, axis=1),
        s5_a_re=da_re, s5_a_im=da_im, s5_log_dt=dlog_dt, s5_b_re=db_re, s5_b_im=db_im,
        s5_c_re=sm["s5_c_re"], s5_c_im=sm["s5_c_im"])
    grads = {n: grads[n].reshape(w[n].shape) for n in WEIGHTS}

    delta, new_m, new_v = {}, {}, {}
    for n in BIG:
        shp = shard[n].shape
        dl, m2, v2 = _adamw("adamw_" + n, shard[n], grads[n].reshape(shp), mom[n].reshape(shp), var[n].reshape(shp))
        delta[n], new_m[n], new_v[n] = dl, m2, v2
    def natural(a):
        return a.reshape(1, -1) if a.ndim == 1 else (a[0] if a.ndim > 2 else a)

    outs = _adamw_small(*[[natural(src[n]) for n in SMALL] for src in (w, grads, mom, var)])
    for dst, group in zip((delta, new_m, new_v), outs):
        dst.update(zip(SMALL, group))
    res = [loss, dx.reshape(x.shape)]
    for group in (grads, delta, new_m, new_v):
        res += [group[n].reshape(w[n].shape) for n in WEIGHTS]
    return tuple(res)
```

```python
import functools
import math

import jax
import jax.numpy as jnp
from jax import lax
from jax.experimental import pallas as pl
from jax.experimental.pallas import tpu as pltpu

F32 = jnp.float32
BF16 = jnp.bfloat16
MESH = pl.DeviceIdType.MESH

EPS = 1e-6
S5_GROUP = 16
S5_STATE = 64
S5_BLOCK_GROUPS = 8
HEAD = 128
CHUNK = 64
CONV_W = 3
LANES = 128
SUBLANES = 8
GATE_BLOCK = 512
VMEM_LIMIT_BYTES = 56 * 1024 * 1024

ADAM_LR = 0.001
ADAM_B1 = 0.9
ADAM_B2 = 0.999
ADAM_EPS = 1e-08
ADAM_WD = 0.01
ADAM_STEP = 10

N_CHIPS = 4
N_DEV = 8


def _params(*sem):
    return pltpu.CompilerParams(dimension_semantics=sem, vmem_limit_bytes=VMEM_LIMIT_BYTES)


def _row_tile(rows, cap):
    if rows <= cap:
        return rows
    for t in range(cap - cap % 8, 7, -8):
        if rows % t == 0:
            return t
    raise ValueError(f"no row tile for {rows}")


def _dot(a, b):
    return jnp.dot(a.astype(BF16), b.astype(BF16), preferred_element_type=F32)


def _dot_nt(a, b):
    return lax.dot_general(a.astype(BF16), b.astype(BF16), (((1,), (1,)), ((), ())), preferred_element_type=F32)


def _dot_tn(a, b):
    return lax.dot_general(a.astype(BF16), b.astype(BF16), (((0,), (0,)), ((), ())), preferred_element_type=F32)


def _sigmoid(x):
    return 1.0 / (1.0 + jnp.exp(-x))


_GELU_C = math.sqrt(2.0 / math.pi)


def _gelu(x):
    return 0.5 * x * (1.0 + jnp.tanh(_GELU_C * (x + 0.044715 * x * x * x)))


def _gelu_grad(x):
    th = jnp.tanh(_GELU_C * (x + 0.044715 * x * x * x))
    return 0.5 * (1.0 + th) + 0.5 * x * (1.0 - th * th) * _GELU_C * (1.0 + 3.0 * 0.044715 * x * x)


def _rowwise(name, fn, ins, outs, accs=(), *, rows, tm, ncol=1):
    n_in, n_out = len(ins), len(outs)

    def body(*refs):
        res = fn(*[r[...] for r in refs[:n_in]])
        for r, v in zip(refs[n_in:n_in + n_out], res[:n_out]):
            r[...] = v.astype(r.dtype)
        first = pl.program_id(1) == 0
        for r, v in zip(refs[n_in + n_out:], res[n_out:]):
            @pl.when(first)
            def _():
                r[...] = v

            @pl.when(jnp.logical_not(first))
            def _():
                r[...] += v

    in_specs = []
    for _, width, base, kind in ins:
        if kind == "row":
            in_specs.append(pl.BlockSpec((tm, width), lambda j, i, b=base: (i, b + j)))
        else:
            in_specs.append(pl.BlockSpec((1, width), lambda j, i, b=base: (0, b + j)))
    out_specs = [pl.BlockSpec((tm, width), lambda j, i: (i, j)) for _, width, _ in outs]
    out_specs += [pl.BlockSpec((1, width), lambda j, i: (0, j)) for _, width in accs]
    out_shape = [jax.ShapeDtypeStruct((rows, total), dt) for total, _, dt in outs]
    out_shape += [jax.ShapeDtypeStruct((1, total), F32) for total, _ in accs]
    return pl.pallas_call(
        body, name=name, grid=(ncol, rows // tm), in_specs=in_specs, out_specs=out_specs, out_shape=out_shape,
        compiler_params=_params("arbitrary", "arbitrary"),
    )(*[a for a, _, _, _ in ins])


def _mm(name, a, b, *, mode, grid, a_spec, b_spec, o_spec, out_shape, acc_shape, res=None, res_spec=None):
    nk = grid[2]
    dot = {"nn": _dot, "nt": _dot_nt, "tn": _dot_tn}[mode]

    def body(*refs):
        if res is None:
            a_ref, b_ref, o_ref, acc_ref = refs
        else:
            a_ref, b_ref, r_ref, o_ref, acc_ref = refs
        k = pl.program_id(2)

        @pl.when(k == 0)
        def _():
            acc_ref[...] = jnp.zeros_like(acc_ref)

        acc_ref[...] += dot(a_ref[...], b_ref[...])

        @pl.when(k == nk - 1)
        def _():
            v = acc_ref[...]
            if res is not None:
                v = v + r_ref[...]
            o_ref[...] = v.astype(o_ref.dtype)

    operands = [a, b] + ([] if res is None else [res])
    in_specs = [a_spec, b_spec] + ([] if res is None else [res_spec])
    return pl.pallas_call(
        body, name=name, grid=grid, in_specs=in_specs, out_specs=o_spec, out_shape=out_shape,
        scratch_shapes=[pltpu.VMEM(acc_shape, F32)],
        compiler_params=_params("arbitrary", "arbitrary", "arbitrary"),
    )(*operands)


def _mm_fwd_cols(name, a, w3, out_dtype=F32, tm_cap=512):
    t, k = a.shape
    ns = w3.shape[2]
    tm = _row_tile(t, tm_cap)
    return _mm(name, a, w3, mode="nn", grid=(t // tm, N_CHIPS, 1),
               a_spec=pl.BlockSpec((tm, k), lambda i, j, kk: (i, 0)),
               b_spec=pl.BlockSpec((None, k, ns), lambda i, j, kk: (j, 0, 0)),
               o_spec=pl.BlockSpec((tm, ns), lambda i, j, kk: (i, j)),
               out_shape=jax.ShapeDtypeStruct((t, N_CHIPS * ns), out_dtype), acc_shape=(tm, ns))


def _mm_bwd_cols(name, d, w3, out_dtype=F32, tm_cap=512):
    t = d.shape[0]
    k, ns = w3.shape[1], w3.shape[2]
    tm = _row_tile(t, tm_cap)
    return _mm(name, d, w3, mode="nt", grid=(t // tm, 1, N_CHIPS),
               a_spec=pl.BlockSpec((tm, ns), lambda i, j, kk: (i, kk)),
               b_spec=pl.BlockSpec((None, k, ns), lambda i, j, kk: (kk, 0, 0)),
               o_spec=pl.BlockSpec((tm, k), lambda i, j, kk: (i, 0)),
               out_shape=jax.ShapeDtypeStruct((t, k), out_dtype), acc_shape=(tm, k))


def _mm_wgrad_cols(name, a, d, tk_cap=512):
    t, k = a.shape
    ns = d.shape[1] // N_CHIPS
    tk = _row_tile(t, tk_cap)
    return _mm(name, a, d, mode="tn", grid=(N_CHIPS, 1, t // tk),
               a_spec=pl.BlockSpec((tk, k), lambda j, i, kk: (kk, 0)),
               b_spec=pl.BlockSpec((tk, ns), lambda j, i, kk: (kk, j)),
               o_spec=pl.BlockSpec((None, k, ns), lambda j, i, kk: (j, 0, 0)),
               out_shape=jax.ShapeDtypeStruct((N_CHIPS, k, ns), BF16), acc_shape=(k, ns))


def _mm_fwd_rows(name, a, w, res=None, out_dtype=F32, tm_cap=512, tk_cap=1408):
    t, k = a.shape
    n = w.shape[1]
    tm = _row_tile(t, tm_cap)
    tk = k if k <= tk_cap else tk_cap
    assert k % tk == 0
    return _mm(name, a, w, mode="nn", grid=(t // tm, 1, k // tk),
               a_spec=pl.BlockSpec((tm, tk), lambda i, j, kk: (i, kk)),
               b_spec=pl.BlockSpec((tk, n), lambda i, j, kk: (kk, 0)),
               o_spec=pl.BlockSpec((tm, n), lambda i, j, kk: (i, 0)),
               out_shape=jax.ShapeDtypeStruct((t, n), out_dtype), acc_shape=(tm, n),
               res=res, res_spec=None if res is None else pl.BlockSpec((tm, n), lambda i, j, kk: (i, 0)))


def _mm_bwd_rows(name, d, w, out_dtype=F32, tm_cap=512, tn_cap=1408):
    t, n = d.shape
    k = w.shape[0]
    tm = _row_tile(t, tm_cap)
    tn = k if k <= tn_cap else tn_cap
    assert k % tn == 0
    return _mm(name, d, w, mode="nt", grid=(t // tm, k // tn, 1),
               a_spec=pl.BlockSpec((tm, n), lambda i, j, kk: (i, 0)),
               b_spec=pl.BlockSpec((tn, n), lambda i, j, kk: (j, 0)),
               o_spec=pl.BlockSpec((tm, tn), lambda i, j, kk: (i, j)),
               out_shape=jax.ShapeDtypeStruct((t, k), out_dtype), acc_shape=(tm, tn))


def _mm_wgrad_rows(name, a, d, tk_cap=512):
    t, k = a.shape
    n = d.shape[1]
    nblk = next(b for b in (4, 2, 1) if (k // b) % LANES == 0)
    ks = k // nblk
    tk = _row_tile(t, tk_cap)
    return _mm(name, a, d, mode="tn", grid=(nblk, 1, t // tk),
               a_spec=pl.BlockSpec((tk, ks), lambda j, i, kk: (kk, j)),
               b_spec=pl.BlockSpec((tk, n), lambda j, i, kk: (kk, 0)),
               o_spec=pl.BlockSpec((ks, n), lambda j, i, kk: (j, 0)),
               out_shape=jax.ShapeDtypeStruct((k, n), BF16), acc_shape=(ks, n))


def _s5_discretize(a_re, a_im, log_dt, b_re, b_im):
    dt = jnp.exp(log_dt)[:, None]
    mag = jnp.exp(a_re * dt)
    ang = a_im * dt
    lb_re = mag * jnp.cos(ang)
    lb_im = mag * jnp.sin(ang)
    den = a_re * a_re + a_im * a_im
    n_re = lb_re - 1.0
    n_im = lb_im
    co_re = ((n_re * a_re + n_im * a_im) / den)[..., None]
    co_im = ((n_im * a_re - n_re * a_im) / den)[..., None]
    bb_re = co_re * b_re - co_im * b_im
    bb_im = co_re * b_im + co_im * b_re
    return lb_re, lb_im, bb_re, bb_im


def _s5_in_blocks(bb):
    g = bb.shape[0]
    nb = g // S5_BLOCK_GROUPS
    t = bb.reshape(nb, S5_BLOCK_GROUPS, S5_STATE, S5_GROUP).transpose(0, 1, 3, 2)
    eye = jnp.eye(S5_BLOCK_GROUPS, dtype=bb.dtype)
    full = t[:, :, :, None, :] * eye[None, :, None, :, None]
    return full.reshape(nb, S5_BLOCK_GROUPS * S5_GROUP, S5_BLOCK_GROUPS * S5_STATE)


def _s5_in_blocks_diag(blocks):
    nb = blocks.shape[0]
    t = blocks.reshape(nb, S5_BLOCK_GROUPS, S5_GROUP, S5_BLOCK_GROUPS, S5_STATE)
    d = jnp.einsum("bghgp->bghp", t)
    return d.transpose(0, 1, 3, 2).reshape(nb * S5_BLOCK_GROUPS, S5_STATE, S5_GROUP)


def _s5_out_blocks(c):
    g = c.shape[0]
    nb = g // S5_BLOCK_GROUPS
    t = c.reshape(nb, S5_BLOCK_GROUPS, S5_GROUP, S5_STATE).transpose(0, 1, 3, 2)
    eye = jnp.eye(S5_BLOCK_GROUPS, dtype=c.dtype)
    full = t[:, :, :, None, :] * eye[None, :, None, :, None]
    return full.reshape(nb, S5_BLOCK_GROUPS * S5_STATE, S5_BLOCK_GROUPS * S5_GROUP)


def _s5_out_blocks_diag(blocks):
    nb = blocks.shape[0]
    t = blocks.reshape(nb, S5_BLOCK_GROUPS, S5_STATE, S5_BLOCK_GROUPS, S5_GROUP)
    d = jnp.einsum("bgpgh->bgph", t)
    return d.transpose(0, 1, 3, 2).reshape(nb * S5_BLOCK_GROUPS, S5_GROUP, S5_STATE)


def _s5_scan_tables(lr, li, reverse):
    def cmul(a, b):
        return a[0] * b[0] - a[1] * b[1], a[0] * b[1] + a[1] * b[0]

    lam = (lr, -li) if reverse else (lr, li)
    pw = [lam]
    for _ in range(SUBLANES - 1):
        pw.append(cmul(pw[-1], lam))
    sub = jnp.arange(SUBLANES)[:, None]
    rows = []
    for s in (1, 2, 4):
        keep = (sub <= SUBLANES - 1 - s) if reverse else (sub >= s)
        rows.append(jnp.where(keep, pw[s - 1][0][None, :], 0.0))
        rows.append(jnp.where(keep, pw[s - 1][1][None, :], 0.0))
    order = list(range(SUBLANES - 1, -1, -1)) if reverse else list(range(SUBLANES))
    rows.append(jnp.stack([pw[i][0] for i in order]))
    rows.append(jnp.stack([pw[i][1] for i in order]))
    return jnp.concatenate(rows, axis=0)


def _s5_scan(vre_ref, vim_ref, coef_ref, seq, width, reverse, xre_ref=None, xim_ref=None):
    nt = seq // SUBLANES
    sub = lax.broadcasted_iota(jnp.int32, (SUBLANES, LANES), 0)
    sums = []
    for j in range(width // LANES):
        lanes = slice(j * LANES, (j + 1) * LANES)
        co = [coef_ref[SUBLANES * q:SUBLANES * (q + 1), lanes] for q in range(8)]

        def step(k, carry, lanes=lanes, co=co):
            cr, ci = carry[0], carry[1]
            kk = (nt - 1 - k) if reverse else k
            rows = pl.ds(pl.multiple_of(kk * SUBLANES, SUBLANES), SUBLANES)
            vr = vre_ref[rows, lanes]
            vi = vim_ref[rows, lanes]
            for q, s in enumerate((1, 2, 4)):
                sh = SUBLANES - s if reverse else s
                rr = pltpu.roll(vr, sh, 0)
                ri = pltpu.roll(vi, sh, 0)
                ar, ai = co[2 * q], co[2 * q + 1]
                vr, vi = vr + ar * rr - ai * ri, vi + ar * ri + ai * rr
            edge = 0 if reverse else SUBLANES - 1
            cbr = jnp.broadcast_to(cr[edge:edge + 1, :], (SUBLANES, LANES))
            cbi = jnp.broadcast_to(ci[edge:edge + 1, :], (SUBLANES, LANES))
            pr, pi = co[6], co[7]
            vr, vi = vr + pr * cbr - pi * cbi, vi + pr * cbi + pi * cbr
            vre_ref[rows, lanes] = vr
            vim_ref[rows, lanes] = vi
            if xre_ref is None:
                return vr, vi
            nr = jnp.where(sub == SUBLANES - 1, cbr, pltpu.roll(vr, SUBLANES - 1, 0))
            ni = jnp.where(sub == SUBLANES - 1, cbi, pltpu.roll(vi, SUBLANES - 1, 0))
            xr = xre_ref[rows, lanes]
            xi = xim_ref[rows, lanes]
            return vr, vi, carry[2] + nr * xr + ni * xi, carry[3] + ni * xr - nr * xi

        zero = jnp.zeros((SUBLANES, LANES), F32)
        init = (zero, zero) if xre_ref is None else (zero, zero, zero, zero)
        out = lax.fori_loop(0, nt, step, init)
        if xre_ref is not None:
            sums.append((jnp.sum(out[2], axis=0, keepdims=True), jnp.sum(out[3], axis=0, keepdims=True)))
    if xre_ref is None:
        return None
    return jnp.concatenate([jnp.concatenate([s[0] for s in sums], axis=1),
                            jnp.concatenate([s[1] for s in sums], axis=1)], axis=0)


def _s5_fwd(z, bre3, bim3, cre3, cim3, coef, dskip, *, nseq, seq):
    nb = bre3.shape[0]
    ch, ns = bre3.shape[1], bre3.shape[2]

    def body(za_ref, bre_ref, bim_ref, cre_ref, cim_ref, coef_ref, d_ref, y_ref, xre_ref, xim_ref):
        za = za_ref[...]
        xre_ref[...] = _dot(za, bre_ref[...])
        xim_ref[...] = _dot(za, bim_ref[...])
        _s5_scan(xre_ref, xim_ref, coef_ref, seq, ns, False)
        y_ref[...] = _dot(xre_ref[...], cre_ref[...]) - _dot(xim_ref[...], cim_ref[...]) + d_ref[...] * za

    blk3 = lambda r, c: pl.BlockSpec((None, r, c), lambda b, j: (j, 0, 0))
    return pl.pallas_call(
        body, name="s5_fwd", grid=(nseq, nb),
        in_specs=[pl.BlockSpec((seq, ch), lambda b, j: (b, j)), blk3(ch, ns), blk3(ch, ns), blk3(ns, ch), blk3(ns, ch),
                  pl.BlockSpec((8 * SUBLANES, ns), lambda b, j: (0, j)), pl.BlockSpec((1, ch), lambda b, j: (0, j))],
        out_specs=[pl.BlockSpec((seq, ch), lambda b, j: (b, j)), pl.BlockSpec((seq, ns), lambda b, j: (b, j)),
                   pl.BlockSpec((seq, ns), lambda b, j: (b, j))],
        out_shape=[jax.ShapeDtypeStruct((nseq * seq, nb * ch), F32), jax.ShapeDtypeStruct((nseq * seq, nb * ns), F32),
                   jax.ShapeDtypeStruct((nseq * seq, nb * ns), F32)],
        compiler_params=_params("arbitrary", "arbitrary"),
    )(z, bre3, bim3, cre3, cim3, coef, dskip)


def _s5_bwd(dy, z, xre, xim, bre3, bim3, cre3, cim3, coef_rev, dskip, *, nseq, seq):
    nb = bre3.shape[0]
    ch, ns = bre3.shape[1], bre3.shape[2]

    def body(dy_ref, za_ref, xre_ref, xim_ref, bre_ref, bim_ref, cre_ref, cim_ref, coef_ref, d_ref,
             dza_ref, dbre_ref, dbim_ref, dcre_ref, dcim_ref, dlam_ref, dd_ref, are_ref, aim_ref):
        dy = dy_ref[...]
        za = za_ref[...]
        are_ref[...] = _dot_nt(dy, cre_ref[...])
        aim_ref[...] = -_dot_nt(dy, cim_ref[...])
        dlam = _s5_scan(are_ref, aim_ref, coef_ref, seq, ns, True, xre_ref, xim_ref)
        are = are_ref[...]
        aim = aim_ref[...]
        dza_ref[...] = (_dot_nt(are, bre_ref[...]) + _dot_nt(aim, bim_ref[...]) + d_ref[...] * dy).astype(dza_ref.dtype)
        parts = (_dot_tn(za, are), _dot_tn(za, aim), _dot_tn(xre_ref[...], dy), -_dot_tn(xim_ref[...], dy),
                 dlam, jnp.sum(dy * za, axis=0, keepdims=True))
        first = pl.program_id(1) == 0
        for r, v in zip((dbre_ref, dbim_ref, dcre_ref, dcim_ref, dlam_ref, dd_ref), parts):
            @pl.when(first)
            def _():
                r[...] = v

            @pl.when(jnp.logical_not(first))
            def _():
                r[...] += v

    blk3 = lambda r, c: pl.BlockSpec((None, r, c), lambda j, b: (j, 0, 0))
    tok = lambda c: pl.BlockSpec((seq, c), lambda j, b: (b, j))
    return pl.pallas_call(
        body, name="s5_bwd", grid=(nb, nseq),
        in_specs=[tok(ch), tok(ch), tok(ns), tok(ns), blk3(ch, ns), blk3(ch, ns), blk3(ns, ch), blk3(ns, ch),
                  pl.BlockSpec((8 * SUBLANES, ns), lambda j, b: (0, j)), pl.BlockSpec((1, ch), lambda j, b: (0, j))],
        out_specs=[tok(ch), blk3(ch, ns), blk3(ch, ns), blk3(ns, ch), blk3(ns, ch),
                   pl.BlockSpec((None, 2, ns), lambda j, b: (j, 0, 0)), pl.BlockSpec((1, ch), lambda j, b: (0, j))],
        out_shape=[jax.ShapeDtypeStruct((nseq * seq, nb * ch), BF16),
                   jax.ShapeDtypeStruct((nb, ch, ns), F32), jax.ShapeDtypeStruct((nb, ch, ns), F32),
                   jax.ShapeDtypeStruct((nb, ns, ch), F32), jax.ShapeDtypeStruct((nb, ns, ch), F32),
                   jax.ShapeDtypeStruct((nb, 2, ns), F32), jax.ShapeDtypeStruct((1, nb * ch), F32)],
        scratch_shapes=[pltpu.VMEM((seq, ns), F32), pltpu.VMEM((seq, ns), F32)],
        compiler_params=_params("arbitrary", "arbitrary"),
    )(dy, z, xre, xim, bre3, bim3, cre3, cim3, coef_rev, dskip)


def _cumsum_rows(x, reverse=False):
    n = x.shape[0]
    row = lax.broadcasted_iota(jnp.int32, x.shape, 0)
    s = 1
    while s < n:
        if reverse:
            x = x + jnp.where(row < n - s, pltpu.roll(x, n - s, 0), 0.0)
        else:
            x = x + jnp.where(row >= s, pltpu.roll(x, s, 0), 0.0)
        s *= 2
    return x


def _hg_gates(zq, zf, lb):
    sg = _sigmoid(zf)
    f = lb + (1.0 - lb) * sg
    sq = _sigmoid(zq)
    qa = zq * sq * (HEAD ** -0.5)
    b = _cumsum_rows(jnp.log(f))
    return sg, f, sq, qa, 1.0 - f, b


SUB = 16


def _hg_scores(qa, kk, b):
    c = qa.shape[0]
    row = lax.broadcasted_iota(jnp.int32, qa.shape, 0)
    pos = jnp.bitwise_and(row, SUB - 1)
    dmat = lax.broadcasted_iota(jnp.int32, (c, c), 0) - lax.broadcasted_iota(jnp.int32, (c, c), 1)
    p = jnp.zeros((c, c), F32)
    for d in range(SUB):
        if d == 0:
            fd = qa * kk
        else:
            e = jnp.exp(jnp.minimum(b - pltpu.roll(b, d, 0), 0.0))
            fd = jnp.where(pos >= d, qa * pltpu.roll(kk, d, 0) * e, 0.0)
        p = jnp.where(dmat == d, jnp.sum(fd, axis=1, keepdims=True), p)
    col = lax.broadcasted_iota(jnp.int32, (SUB, c), 1)
    blocks = [jnp.zeros((SUB, c), F32)]
    for r0 in range(SUB, c, SUB):
        beta = b[r0 - 1:r0, :]
        qt = qa[r0:r0 + SUB] * jnp.exp(b[r0:r0 + SUB] - beta)
        kt = kk * jnp.exp(jnp.minimum(beta - b, 0.0))
        blocks.append(jnp.where(col < r0, _dot_nt(qt, kt), 0.0))
    return p + jnp.concatenate(blocks, axis=0)


def _hg_scores_bwd(dp, qa, kk, b):
    c = qa.shape[0]
    row = lax.broadcasted_iota(jnp.int32, qa.shape, 0)
    pos = jnp.bitwise_and(row, SUB - 1)
    dmat = lax.broadcasted_iota(jnp.int32, (c, c), 0) - lax.broadcasted_iota(jnp.int32, (c, c), 1)
    dqa = jnp.zeros_like(qa)
    dkk = jnp.zeros_like(qa)
    db = jnp.zeros_like(qa)
    for d in range(SUB):
        dcol = jnp.sum(jnp.where(dmat == d, dp, 0.0), axis=1, keepdims=True)
        if d == 0:
            dqa = dqa + dcol * kk
            dkk = dkk + dcol * qa
        else:
            e = jnp.exp(jnp.minimum(b - pltpu.roll(b, d, 0), 0.0))
            w = jnp.where(pos >= d, dcol * e, 0.0)
            kr = pltpu.roll(kk, d, 0)
            dqa = dqa + w * kr
            tmp = w * qa
            dkk = dkk + pltpu.roll(tmp, c - d, 0)
            x = tmp * kr
            db = db + x - pltpu.roll(x, c - d, 0)
    col = lax.broadcasted_iota(jnp.int32, (SUB, c), 1)
    dq_blocks = [jnp.zeros((SUB, qa.shape[1]), F32)]
    db_blocks = [jnp.zeros((SUB, qa.shape[1]), F32)]
    for r0 in range(SUB, c, SUB):
        beta = b[r0 - 1:r0, :]
        eq = jnp.exp(b[r0:r0 + SUB] - beta)
        ek = jnp.exp(jnp.minimum(beta - b, 0.0))
        qt = qa[r0:r0 + SUB] * eq
        kt = kk * ek
        dpi = jnp.where(col < r0, dp[r0:r0 + SUB, :], 0.0)
        dqt = _dot(dpi, kt)
        dkt = _dot_tn(dpi, qt)
        dq_blocks.append(dqt * eq)
        db_blocks.append(dqt * qt)
        dkk = dkk + dkt * ek
        db = db - dkt * kt
    return dqa + jnp.concatenate(dq_blocks, axis=0), dkk, db + jnp.concatenate(db_blocks, axis=0)


def _hg_chunks_per_step(seq):
    nc = seq // CHUNK
    cps = next(k for k in (4, 2, 1) if nc % k == 0)
    return nc, cps, nc // cps


def _hg_fwd(z, lbrow, gain, *, nseq, seq, heads, qoff):
    nc, cps, nblk = _hg_chunks_per_step(seq)
    blk = cps * CHUNK
    zspec = lambda off: pl.BlockSpec((blk, HEAD), lambda h, b, n, off=off: (b * nblk + n, off + h))

    def body(zq_ref, zf_ref, zi_ref, zg_ref, lb_ref, gn_ref, o_ref, yb_ref, st_ref, state):
        @pl.when(pl.program_id(2) == 0)
        def _():
            state[...] = jnp.zeros_like(state)

        lb = lb_ref[...]
        gain_v = gn_ref[...]

        def chunk(ci, carry):
            rows = pl.ds(pl.multiple_of(ci * CHUNK, CHUNK), CHUNK)
            st = state[...]
            st_ref[ci] = st
            zi = zi_ref[rows, :]
            zg = zg_ref[rows, :]
            _, _, _, qa, kk, b = _hg_gates(zq_ref[rows, :], zf_ref[rows, :], lb)
            o = _dot_nt(qa * jnp.exp(b), st) + _dot(_hg_scores(qa, kk, b), zi)
            bl = b[CHUNK - 1:CHUNK, :]
            state[...] = st * jnp.exp(bl) + _dot_tn(zi, kk * jnp.exp(bl - b))
            o_ref[rows, :] = o
            r = lax.rsqrt(jnp.mean(o * o, axis=1, keepdims=True) + EPS)
            yb_ref[rows, :] = (o * r * gain_v * zg * _sigmoid(zg)).astype(yb_ref.dtype)
            return carry

        lax.fori_loop(0, cps, chunk, 0)

    tok = pl.BlockSpec((blk, HEAD), lambda h, b, n: (b * nblk + n, h))
    vec = pl.BlockSpec((1, HEAD), lambda h, b, n: (0, h))
    rows = nseq * seq
    return pl.pallas_call(
        body, name="hgrn2_fwd", grid=(heads, nseq, nblk),
        in_specs=[zspec(qoff), zspec(qoff + heads), zspec(qoff + 2 * heads), zspec(qoff + 3 * heads), vec, vec],
        out_specs=[tok, tok, pl.BlockSpec((None, None, cps, HEAD, HEAD), lambda h, b, n: (h, b, n, 0, 0))],
        out_shape=[jax.ShapeDtypeStruct((rows, heads * HEAD), F32), jax.ShapeDtypeStruct((rows, heads * HEAD), BF16),
                   jax.ShapeDtypeStruct((heads, nseq, nc, HEAD, HEAD), F32)],
        scratch_shapes=[pltpu.VMEM((HEAD, HEAD), F32)],
        compiler_params=_params("arbitrary", "arbitrary", "arbitrary"),
    )(z, z, z, z, lbrow, gain)


def _hg_bwd(dyb, z, o, states, lbrow, gain, *, nseq, seq, heads, qoff):
    nc, cps, nblk = _hg_chunks_per_step(seq)
    blk = cps * CHUNK
    rev = lambda n: nblk - 1 - n
    zspec = lambda off: pl.BlockSpec((blk, HEAD), lambda h, b, n, off=off: (b * nblk + rev(n), off + h))

    def body(dyb_ref, zq_ref, zf_ref, zi_ref, zg_ref, o_ref, st_ref, lb_ref, gn_ref,
             dzq_ref, dzf_ref, dzi_ref, dzg_ref, dlb_ref, dgn_ref, dstate):
        @pl.when(pl.program_id(2) == 0)
        def _():
            dstate[...] = jnp.zeros_like(dstate)

        @pl.when(jnp.logical_and(pl.program_id(1) == 0, pl.program_id(2) == 0))
        def _():
            dlb_ref[...] = jnp.zeros_like(dlb_ref)
            dgn_ref[...] = jnp.zeros_like(dgn_ref)

        lb = lb_ref[...]
        gain_v = gn_ref[...]
        c = CHUNK
        causal = lax.broadcasted_iota(jnp.int32, (c, c), 0) >= lax.broadcasted_iota(jnp.int32, (c, c), 1)

        def chunk(step, carry):
            ci = cps - 1 - step
            rows = pl.ds(pl.multiple_of(ci * CHUNK, CHUNK), CHUNK)
            zq = zq_ref[rows, :]
            zi = zi_ref[rows, :]
            zg = zg_ref[rows, :]
            sg, f, sq, qa, kk, b = _hg_gates(zq, zf_ref[rows, :], lb)
            eb = jnp.exp(b)
            qt = qa * eb
            bl = b[c - 1:c, :]
            ebl = jnp.exp(bl)
            ekb = jnp.exp(bl - b)
            kh = kk * ekb
            st = st_ref[ci]
            dst = dstate[...]
            o = o_ref[rows, :]
            r = lax.rsqrt(jnp.mean(o * o, axis=1, keepdims=True) + EPS)
            oh = o * r
            szg = _sigmoid(zg)
            dyb = dyb_ref[rows, :]
            don = dyb * zg * szg
            dzg_ref[rows, :] = (dyb * oh * gain_v * szg * (1.0 + zg * (1.0 - szg))).astype(dzg_ref.dtype)
            doh = don * gain_v
            do = r * (doh - oh * jnp.mean(doh * oh, axis=1, keepdims=True))
            dqt = _dot(do, st)
            dp = jnp.where(causal, _dot_nt(do, zi), 0.0)
            p = _hg_scores(qa, kk, b)
            dzi_ref[rows, :] = (_dot_tn(p, do) + _dot_nt(kh, dst)).astype(dzi_ref.dtype)
            dkh = _dot(zi, dst)
            dbl = jnp.sum(dkh * kh, axis=0, keepdims=True) + jnp.sum(dst * st, axis=0, keepdims=True) * ebl
            dstate[...] = _dot_tn(do, qt) + dst * ebl
            dqa_s, dkk_s, db_s = _hg_scores_bwd(dp, qa, kk, b)
            dqa = dqt * eb + dqa_s
            dkk = dkh * ekb + dkk_s
            db = dqt * qt - dkh * kh + db_s
            row = lax.broadcasted_iota(jnp.int32, db.shape, 0)
            db = db + jnp.where(row == c - 1, dbl, 0.0)
            df = _cumsum_rows(db, reverse=True) / f - dkk
            dzf_ref[rows, :] = (df * (1.0 - lb) * sg * (1.0 - sg)).astype(dzf_ref.dtype)
            dzq_ref[rows, :] = (dqa * (HEAD ** -0.5) * sq * (1.0 + zq * (1.0 - sq))).astype(dzq_ref.dtype)
            dlb_ref[...] += jnp.sum(df * (1.0 - sg), axis=0, keepdims=True)
            dgn_ref[...] += jnp.sum(don * oh, axis=0, keepdims=True)
            return carry

        lax.fori_loop(0, cps, chunk, 0)

    tok = pl.BlockSpec((blk, HEAD), lambda h, b, n: (b * nblk + rev(n), h))
    vec = pl.BlockSpec((1, HEAD), lambda h, b, n: (0, h))
    rows = nseq * seq
    return pl.pallas_call(
        body, name="hgrn2_bwd", grid=(heads, nseq, nblk),
        in_specs=[tok, zspec(qoff), zspec(qoff + heads), zspec(qoff + 2 * heads), zspec(qoff + 3 * heads), tok,
                  pl.BlockSpec((None, None, cps, HEAD, HEAD), lambda h, b, n: (h, b, rev(n), 0, 0)), vec, vec],
        out_specs=[tok, tok, tok, tok, vec, vec],
        out_shape=[jax.ShapeDtypeStruct((rows, heads * HEAD), BF16)] * 4
        + [jax.ShapeDtypeStruct((1, heads * HEAD), F32)] * 2,
        scratch_shapes=[pltpu.VMEM((HEAD, HEAD), F32)],
        compiler_params=_params("arbitrary", "arbitrary", "arbitrary"),
    )(dyb, z, z, z, z, o, states, lbrow, gain)


def _conv_taps(h, w, bias):
    row = lax.broadcasted_iota(jnp.int32, h.shape, 0)
    h1 = jnp.where(row >= 1, pltpu.roll(h, 1, 0), 0.0)
    h2 = jnp.where(row >= 2, pltpu.roll(h, 2, 0), 0.0)
    return h2 * w[0:1, :] + h1 * w[1:2, :] + h * w[2:3, :] + bias, h1, h2


def _conv_fwd(h, wconv, bconv, *, nseq, seq):
    ff2 = h.shape[1]
    ncol = ff2 // 2 // LANES

    def body(hg_ref, hv_ref, wg_ref, wv_ref, bg_ref, bv_ref, a_ref):
        g, _, _ = _conv_taps(hg_ref[...], wg_ref[...], bg_ref[...])
        v, _, _ = _conv_taps(hv_ref[...], wv_ref[...], bv_ref[...])
        a_ref[...] = (g * _sigmoid(g) * v).astype(a_ref.dtype)

    tok = lambda off: pl.BlockSpec((seq, LANES), lambda j, b, off=off: (b, off + j))
    wsp = lambda off: pl.BlockSpec((CONV_W, LANES), lambda j, b, off=off: (0, off + j))
    bsp = lambda off: pl.BlockSpec((1, LANES), lambda j, b, off=off: (0, off + j))
    return pl.pallas_call(
        body, name="conv_fwd", grid=(ncol, nseq),
        in_specs=[tok(0), tok(ncol), wsp(0), wsp(ncol), bsp(0), bsp(ncol)],
        out_specs=tok(0), out_shape=jax.ShapeDtypeStruct((nseq * seq, ff2 // 2), BF16),
        compiler_params=_params("arbitrary", "arbitrary"),
    )(h, h, wconv, wconv, bconv, bconv)


def _conv_bwd(da, h, wconv, bconv, *, nseq, seq):
    ff2 = h.shape[1]
    ncol = ff2 // 2 // LANES

    def half_bwd(d, hcur, h1, h2, w):
        n = d.shape[0]
        row = lax.broadcasted_iota(jnp.int32, d.shape, 0)
        d1 = jnp.where(row < n - 1, pltpu.roll(d, n - 1, 0), 0.0)
        d2 = jnp.where(row < n - 2, pltpu.roll(d, n - 2, 0), 0.0)
        dh = d * w[2:3, :] + d1 * w[1:2, :] + d2 * w[0:1, :]
        stats = jnp.concatenate(
            [jnp.sum(h2 * d, axis=0, keepdims=True), jnp.sum(h1 * d, axis=0, keepdims=True),
             jnp.sum(hcur * d, axis=0, keepdims=True), jnp.sum(d, axis=0, keepdims=True),
             jnp.zeros((SUBLANES - 4, d.shape[1]), F32)], axis=0)
        return dh, stats

    def body(da_ref, hg_ref, hv_ref, wg_ref, wv_ref, bg_ref, bv_ref, dhg_ref, dhv_ref, sg_ref, sv_ref):
        hg = hg_ref[...]
        hv = hv_ref[...]
        wg = wg_ref[...]
        wv = wv_ref[...]
        g, g1, g2 = _conv_taps(hg, wg, bg_ref[...])
        v, v1, v2 = _conv_taps(hv, wv, bv_ref[...])
        da = da_ref[...]
        s = _sigmoid(g)
        dhg, stg = half_bwd(da * v * s * (1.0 + g * (1.0 - s)), hg, g1, g2, wg)
        dhv, stv = half_bwd(da * g * s, hv, v1, v2, wv)
        dhg_ref[...] = dhg.astype(dhg_ref.dtype)
        dhv_ref[...] = dhv.astype(dhv_ref.dtype)
        first = pl.program_id(1) == 0
        for r, val in ((sg_ref, stg), (sv_ref, stv)):
            @pl.when(first)
            def _():
                r[...] = val

            @pl.when(jnp.logical_not(first))
            def _():
                r[...] += val

    tok = lambda off: pl.BlockSpec((seq, LANES), lambda j, b, off=off: (b, off + j))
    wsp = lambda off: pl.BlockSpec((CONV_W, LANES), lambda j, b, off=off: (0, off + j))
    bsp = lambda off: pl.BlockSpec((1, LANES), lambda j, b, off=off: (0, off + j))
    ssp = pl.BlockSpec((SUBLANES, LANES), lambda j, b: (0, j))
    dhg, dhv, stg, stv = pl.pallas_call(
        body, name="conv_bwd", grid=(ncol, nseq),
        in_specs=[tok(0), tok(0), tok(ncol), wsp(0), wsp(ncol), bsp(0), bsp(ncol)],
        out_specs=[tok(0), tok(0), ssp, ssp],
        out_shape=[jax.ShapeDtypeStruct((nseq * seq, ff2 // 2), BF16)] * 2
        + [jax.ShapeDtypeStruct((SUBLANES, ff2 // 2), F32)] * 2,
        compiler_params=_params("arbitrary", "arbitrary"),
    )(da, h, h, wconv, wconv, bconv, bconv)
    return jnp.concatenate([dhg, dhv], axis=1), jnp.concatenate([stg, stv], axis=1)


def _rms_fwd(xv, g):
    r = lax.rsqrt(jnp.mean(xv * xv, axis=1, keepdims=True) + EPS)
    return (xv * r * g,)


def _rms_bwd(xv, g, dy, res):
    r = lax.rsqrt(jnp.mean(xv * xv, axis=1, keepdims=True) + EPS)
    xh = xv * r
    dxh = dy * g
    dx = r * (dxh - xh * jnp.mean(dxh * xh, axis=1, keepdims=True)) + res
    return dx, jnp.sum(dy * xh, axis=0, keepdims=True)


def _loss_head(x2, tgt, g):
    d = x2.shape[1]
    r = lax.rsqrt(jnp.mean(x2 * x2, axis=1, keepdims=True) + EPS)
    xh = x2 * r
    err = xh * g - tgt
    dy = err * (1.0 / d)
    dxh = dy * g
    dx = r * (dxh - xh * jnp.mean(dxh * xh, axis=1, keepdims=True))
    loss = 0.5 * jnp.sum(jnp.mean(err * err, axis=1, keepdims=True), axis=0, keepdims=True)
    return dx, jnp.sum(dy * xh, axis=0, keepdims=True), jnp.broadcast_to(loss, (1, LANES))


def _local_step(x, tgt, p, *, nseq, seq):
    t, d = x.shape
    s5w = p["s5_d"].shape[1]
    hgw = p["gain"].shape[1]
    heads = hgw // HEAD
    qoff = s5w // LANES
    gblk = (s5w + 4 * hgw) // GATE_BLOCK
    ngb = d // GATE_BLOCK
    tm = _row_tile(t, 256)
    row = lambda a, w=None, base=0: (a, a.shape[1] if w is None else w, base, "row")
    vec = lambda a, w=None, base=0: (a, a.shape[1] if w is None else w, base, "vec")
    rw = functools.partial(_rowwise, rows=t, tm=tm)

    (u,) = rw("rms_mix", _rms_fwd, [row(x), vec(p["g_mix"])], [(d, d, BF16)])
    z = _mm_fwd_cols("in_proj", u, p["w_in"])

    lam_re, lam_im, bb_re, bb_im = _s5_discretize(p["s5_a_re"], p["s5_a_im"], p["s5_log_dt"], p["s5_b_re"], p["s5_b_im"])
    bre3 = _s5_in_blocks(bb_re).astype(BF16)
    bim3 = _s5_in_blocks(bb_im).astype(BF16)
    cre3 = _s5_out_blocks(p["s5_c_re"]).astype(BF16)
    cim3 = _s5_out_blocks(p["s5_c_im"]).astype(BF16)
    coef_f = _s5_scan_tables(lam_re.reshape(-1), lam_im.reshape(-1), False)
    coef_r = _s5_scan_tables(lam_re.reshape(-1), lam_im.reshape(-1), True)
    y5, xre, xim = _s5_fwd(z, bre3, bim3, cre3, cim3, coef_f, p["s5_d"], nseq=nseq, seq=seq)
    (ya0,) = rw("s5_gelu", lambda y: (_gelu(y),), [row(y5)], [(s5w, s5w, BF16)])
    gl = _mm_fwd_rows("glu_proj", ya0, p["w_glu"])
    (ya,) = rw("s5_glu", lambda y, g, b: (_gelu(y) * _sigmoid(g + b),), [row(y5), row(gl), vec(p["b_glu"])],
               [(s5w, s5w, BF16)])

    o, yb, states = _hg_fwd(z, p["lbrow"], p["gain"], nseq=nseq, seq=seq, heads=heads, qoff=qoff)

    pa = _mm_fwd_cols("proj_a", ya, p["w_pa"])
    pb = _mm_fwd_cols("proj_b", yb, p["w_pb"])
    gb = GATE_BLOCK
    (m,) = rw("merge", lambda ga, gbv, a, b: (_sigmoid(ga) * a + _sigmoid(gbv) * b,),
              [row(z, gb, gblk), row(z, gb, gblk + ngb), row(pa, gb), row(pb, gb)], [(d, gb, BF16)], ncol=ngb)
    x1 = _mm_fwd_rows("out_proj", m, p["w_out"], res=x)

    (u2,) = rw("rms_ffn", _rms_fwd, [row(x1), vec(p["g_ffn"])], [(d, d, BF16)])
    h = _mm_fwd_cols("up_proj", u2, p["w_up"])
    a = _conv_fwd(h, p["w_conv"], p["b_conv"], nseq=nseq, seq=seq)
    x2 = _mm_fwd_rows("down_proj", a, p["w_down"], res=x1)

    dx2, dg_final, lossv = rw("loss_head", _loss_head, [row(x2), row(tgt), vec(p["g_final"])], [(d, d, F32)],
                              accs=[(d, d), (LANES, LANES)])

    da = _mm_bwd_rows("down_bwd", dx2, p["w_down"])
    g_wdown = _mm_wgrad_rows("down_wgrad", a, dx2)
    dh, cstats = _conv_bwd(da, h, p["w_conv"], p["b_conv"], nseq=nseq, seq=seq)
    du2 = _mm_bwd_cols("up_bwd", dh, p["w_up"])
    g_wup = _mm_wgrad_cols("up_wgrad", u2, dh)
    dx1, dg_ffn = rw("rms_ffn_bwd", _rms_bwd, [row(x1), vec(p["g_ffn"]), row(du2), row(dx2)], [(d, d, F32)],
                     accs=[(d, d)])

    dm = _mm_bwd_rows("out_bwd", dx1, p["w_out"])
    g_wout = _mm_wgrad_rows("out_wgrad", m, dx1)

    def merge_bwd(ga, gbv, av, bv, dmv):
        sa = _sigmoid(ga)
        sb = _sigmoid(gbv)
        return dmv * sa, dmv * sb, dmv * av * sa * (1.0 - sa), dmv * bv * sb * (1.0 - sb)

    dpa, dpb, dzga, dzgb = rw("merge_bwd", merge_bwd,
                              [row(z, gb, gblk), row(z, gb, gblk + ngb), row(pa, gb), row(pb, gb), row(dm, gb)],
                              [(d, gb, BF16)] * 4, ncol=ngb)
    dya = _mm_bwd_cols("proj_a_bwd", dpa, p["w_pa"])
    g_wpa = _mm_wgrad_cols("proj_a_wgrad", ya, dpa)
    dyb = _mm_bwd_cols("proj_b_bwd", dpb, p["w_pb"])
    g_wpb = _mm_wgrad_cols("proj_b_wgrad", yb, dpb)

    def glu_bwd1(y, g, b, dyv):
        s = _sigmoid(g + b)
        dgl = dyv * _gelu(y) * s * (1.0 - s)
        return dgl, jnp.sum(dgl, axis=0, keepdims=True)

    dgl, db_glu = rw("s5_glu_bwd", glu_bwd1, [row(y5), row(gl), vec(p["b_glu"]), row(dya)], [(s5w, s5w, BF16)],
                     accs=[(s5w, s5w)])
    dgl_in = _mm_bwd_rows("glu_bwd", dgl, p["w_glu"])
    g_wglu = _mm_wgrad_rows("glu_wgrad", ya0, dgl)
    (dy5,) = rw("s5_gelu_bwd", lambda y, g, b, dyv, tv: ((dyv * _sigmoid(g + b) + tv) * _gelu_grad(y),),
                [row(y5), row(gl), vec(p["b_glu"]), row(dya), row(dgl_in)], [(s5w, s5w, F32)])
    dza, dbre3, dbim3, dcre3, dcim3, dlam, dd = _s5_bwd(dy5, z, xre, xim, bre3, bim3, cre3, cim3, coef_r, p["s5_d"],
                                                        nseq=nseq, seq=seq)

    dzq, dzf, dzi, dzg, dlb, dgain = _hg_bwd(dyb, z, o, states, p["lbrow"], p["gain"], nseq=nseq, seq=seq,
                                             heads=heads, qoff=qoff)

    dz = jnp.concatenate([dza, dzq, dzf, dzi, dzg, dzga, dzgb], axis=1)
    du = _mm_bwd_cols("in_bwd", dz, p["w_in"])
    g_win = _mm_wgrad_cols("in_wgrad", u, dz)
    dx, dg_mix = rw("rms_mix_bwd", _rms_bwd, [row(x), vec(p["g_mix"]), row(du), row(dx1)], [(d, d, F32)],
                    accs=[(d, d)])

    gshape = lam_re.shape
    big = {"w_in": g_win, "w_glu": g_wglu, "w_pa": g_wpa, "w_pb": g_wpb, "w_out": g_wout, "w_up": g_wup,
           "w_down": g_wdown}
    small = {
        "loss": lossv, "g_mix": dg_mix, "g_ffn": dg_ffn, "g_final": dg_final, "b_glu": db_glu, "gain": dgain,
        "lbrow": dlb, "s5_d": dd, "w_conv": cstats[0:CONV_W], "b_conv": cstats[CONV_W:CONV_W + 1],
        "lam_re": dlam[:, 0, :].reshape(gshape), "lam_im": dlam[:, 1, :].reshape(gshape),
        "bb_re": _s5_in_blocks_diag(dbre3), "bb_im": _s5_in_blocks_diag(dbim3),
        "s5_c_re": _s5_out_blocks_diag(dcre3), "s5_c_im": _s5_out_blocks_diag(dcim3),
    }
    return dx, big, small


ANY = pl.BlockSpec(memory_space=pl.ANY)


def _place():
    x, y, c = lax.axis_index("x"), lax.axis_index("y"), lax.axis_index("c")
    chips = [(1 - x, y), (x, 1 - y), (1 - x, 1 - y)]
    return x, y, c, chips


def _remote(src, dst, send_sems, recv_sems, k, to):
    return pltpu.make_async_remote_copy(src_ref=src, dst_ref=dst, send_sem=send_sems.at[k], recv_sem=recv_sems.at[k],
                                        device_id=to, device_id_type=MESH)


def _half(rows, which):
    return pl.ds(pl.multiple_of(which * (rows // 2), 16), rows // 2)


def _gather_weights(shards, whole):
    n, nw = len(shards), len(whole)
    arrays = list(shards) + list(whole)

    def body(*refs):
        in_refs, out_refs = refs[:n + nw], refs[n + nw:2 * (n + nw)]
        send_sems, recv_sems = refs[2 * (n + nw):]
        x, y, c, chips = _place()
        me = 2 * x + y
        copy = functools.partial(_remote, send_sems=send_sems, recv_sems=recv_sems)
        sends = []
        for a in range(n):
            mine_half = _half(arrays[a].shape[0], c)
            for j, (cx, cy) in enumerate(chips):
                sends.append(copy(in_refs[a].at[mine_half], out_refs[a].at[me, mine_half], k=6 * a + j, to=(cx, cy, c)))
        for a in range(n, n + nw):
            for j, (cx, cy) in enumerate(chips):
                sends.append(copy(in_refs[a], out_refs[a].at[me], k=6 * n + 3 * (a - n) + j, to=(cx, cy, c)))
        for cp in sends:
            cp.start()
        for a in range(n):
            mine_half = _half(arrays[a].shape[0], c)
            for j, (cx, cy) in enumerate(chips):
                landed = out_refs[a].at[2 * cx + cy, mine_half]
                copy(landed, landed, k=6 * a + j, to=(x, y, c)).wait_recv()
                fwd = copy(landed, landed, k=6 * a + 3 + j, to=(x, y, 1 - c))
                fwd.start()
                sends.append(fwd)
        for a in range(n):
            other_half = _half(arrays[a].shape[0], 1 - c)
            for j, (cx, cy) in enumerate(chips):
                landed = out_refs[a].at[2 * cx + cy, other_half]
                copy(landed, landed, k=6 * a + 3 + j, to=(x, y, c)).wait_recv()
        for a in range(n, n + nw):
            for j, (cx, cy) in enumerate(chips):
                landed = out_refs[a].at[2 * cx + cy]
                copy(landed, landed, k=6 * n + 3 * (a - n) + j, to=(x, y, c)).wait_recv()
        for cp in sends:
            cp.wait_send()

    nsem = 6 * n + 3 * nw
    return pl.pallas_call(
        body, name="gather_weights", out_shape=[jax.ShapeDtypeStruct((N_CHIPS,) + a.shape, a.dtype) for a in arrays],
        in_specs=[ANY] * (n + nw), out_specs=[ANY] * (n + nw),
        scratch_shapes=[pltpu.SemaphoreType.DMA((nsem,)), pltpu.SemaphoreType.DMA((nsem,))],
    )(*arrays)


def _swap_halves(parts):
    n = len(parts)

    def body(*refs):
        in_refs, out_refs, send_sems, recv_sems = refs[:n], refs[n:2 * n], refs[2 * n], refs[2 * n + 1]
        x, y, c, _ = _place()
        copies = [_remote(in_refs[a].at[:, _half(parts[a].shape[1], 1 - c), :], out_refs[a], send_sems, recv_sems, a,
                          (x, y, 1 - c)) for a in range(n)]
        for cp in copies:
            cp.start()
        for cp in copies:
            cp.wait()

    return pl.pallas_call(
        body, name="grad_swap_halves",
        out_shape=[jax.ShapeDtypeStruct((g.shape[0], g.shape[1] // 2, g.shape[2]), g.dtype) for g in parts],
        in_specs=[ANY] * n, out_specs=[ANY] * n,
        scratch_shapes=[pltpu.SemaphoreType.DMA((n,)), pltpu.SemaphoreType.DMA((n,))],
    )(*parts)


def _scatter_to_chips(parts):
    n = len(parts)

    def body(*refs):
        in_refs, out_refs, send_sems, recv_sems = refs[:n], refs[n:2 * n], refs[2 * n], refs[2 * n + 1]
        x, y, c, chips = _place()
        copies = [_remote(in_refs[a].at[2 * cx + cy], out_refs[a].at[j], send_sems, recv_sems, 3 * a + j, (cx, cy, c))
                  for a in range(n) for j, (cx, cy) in enumerate(chips)]
        for cp in copies:
            cp.start()
        for cp in copies:
            cp.wait()

    return pl.pallas_call(
        body, name="grad_scatter_chips",
        out_shape=[jax.ShapeDtypeStruct((N_CHIPS - 1,) + h.shape[1:], h.dtype) for h in parts],
        in_specs=[ANY] * n, out_specs=[ANY] * n,
        scratch_shapes=[pltpu.SemaphoreType.DMA((3 * n,)), pltpu.SemaphoreType.DMA((3 * n,))],
    )(*parts)


def _swap_sums(parts):
    n = len(parts)

    def body(*refs):
        in_refs, out_refs, send_sems, recv_sems = refs[:n], refs[n:2 * n], refs[2 * n], refs[2 * n + 1]
        x, y, c, _ = _place()
        copies = [_remote(in_refs[a], out_refs[a], send_sems, recv_sems, a, (x, y, 1 - c)) for a in range(n)]
        for cp in copies:
            cp.start()
        for cp in copies:
            cp.wait()

    return pl.pallas_call(
        body, name="grad_swap_sums", out_shape=[jax.ShapeDtypeStruct(g.shape, g.dtype) for g in parts],
        in_specs=[ANY] * n, out_specs=[ANY] * n,
        scratch_shapes=[pltpu.SemaphoreType.DMA((n,)), pltpu.SemaphoreType.DMA((n,))],
    )(*parts)


def _gather_all(v):
    m_per, n = v.shape

    def body(x_ref, out_ref, send_sems, recv_sems):
        x, y, c, chips = _place()
        me, sibling = (x, y, c), (x, y, 1 - c)

        def rows(px, py, pc):
            return out_ref.at[pl.ds(pl.multiple_of((4 * px + 2 * py + pc) * m_per, 8), m_per), :]

        def copy(k, block, to, src=None):
            return pltpu.make_async_remote_copy(src_ref=rows(*block) if src is None else src, dst_ref=rows(*block),
                                                send_sem=send_sems.at[k], recv_sem=recv_sems.at[k], device_id=to,
                                                device_id_type=MESH)

        out_ref[pl.ds(pl.multiple_of((4 * x + 2 * y + c) * m_per, 8), m_per), :] = x_ref[...]
        first = [copy(0, me, sibling, src=x_ref)]
        first += [copy(1 + j, me, (*chip, c), src=x_ref) for j, chip in enumerate(chips)]
        for cp in first:
            cp.start()
        passed = [copy(4 + j, (*chip, c), sibling) for j, chip in enumerate(chips)]
        for j, chip in enumerate(chips):
            copy(1 + j, (*chip, c), me).wait_recv()
            passed[j].start()
        copy(0, sibling, me).wait_recv()
        for j, chip in enumerate(chips):
            copy(4 + j, (*chip, 1 - c), me).wait_recv()
        for cp in first + passed:
            cp.wait_send()

    return pl.pallas_call(
        body, name="gather_small_grads", out_shape=jax.ShapeDtypeStruct((N_DEV * m_per, n), v.dtype),
        in_specs=[pl.BlockSpec(memory_space=pltpu.VMEM)], out_specs=pl.BlockSpec(memory_space=pltpu.VMEM),
        scratch_shapes=[pltpu.SemaphoreType.DMA((7,)), pltpu.SemaphoreType.DMA((7,))],
        compiler_params=pltpu.CompilerParams(vmem_limit_bytes=VMEM_LIMIT_BYTES),
    )(v)


def _sum_blocks(name, parts, out_dtype):
    rows, cols = parts[0].shape
    tm = _row_tile(rows, 512)

    def body(*refs):
        acc = refs[0][...].astype(F32)
        for r in refs[1:-1]:
            acc = acc + r[...].astype(F32)
        refs[-1][...] = acc.astype(refs[-1].dtype)

    spec = pl.BlockSpec((tm, cols), lambda i: (i, 0))
    return pl.pallas_call(
        body, name=name, grid=(rows // tm,), in_specs=[spec] * len(parts), out_specs=spec,
        out_shape=jax.ShapeDtypeStruct((rows, cols), out_dtype), compiler_params=_params("arbitrary"),
    )(*parts)


def _adamw_math(wv, gv, mv, vv):
    m2 = ADAM_B1 * mv + (1.0 - ADAM_B1) * gv
    v2 = ADAM_B2 * vv + (1.0 - ADAM_B2) * (gv * gv)
    delta = -ADAM_LR * ((m2 / (1.0 - ADAM_B1 ** ADAM_STEP)) / (jnp.sqrt(v2 / (1.0 - ADAM_B2 ** ADAM_STEP)) + ADAM_EPS)
                        + ADAM_WD * wv)
    return delta, m2, v2


def _adamw_small(ws, gs, ms, vs):
    n = len(ws)

    def body(*refs):
        for i in range(n):
            res = _adamw_math(refs[i][...], refs[n + i][...], refs[2 * n + i][...], refs[3 * n + i][...])
            for k in range(3):
                refs[(4 + k) * n + i][...] = res[k]

    vm = pl.BlockSpec(memory_space=pltpu.VMEM)
    outs = pl.pallas_call(
        body, name="adamw_small", in_specs=[vm] * (4 * n), out_specs=[vm] * (3 * n),
        out_shape=[jax.ShapeDtypeStruct(a.shape, F32) for a in ws] * 3,
        compiler_params=pltpu.CompilerParams(vmem_limit_bytes=VMEM_LIMIT_BYTES),
    )(*ws, *gs, *ms, *vs)
    return outs[:n], outs[n:2 * n], outs[2 * n:]


def _adamw(name, w, g, m, v):
    rows, cols = w.shape
    ins = [(a, cols, 0, "row") for a in (w, g, m, v)]
    return _rowwise(name, _adamw_math, ins, [(cols, cols, F32)] * 3, rows=rows, tm=_row_tile(rows, 256))


PACK_ROWS = 256


def _pack(flat_parts, dtype, lead=()):
    parts = [a.astype(dtype).reshape(lead + (-1,)) for a in flat_parts]
    n = sum(a.shape[-1] for a in parts)
    chunk = PACK_ROWS * LANES
    total = -(-n // chunk) * chunk
    if total > n:
        parts.append(jnp.zeros(lead + (total - n,), dtype))
    return jnp.concatenate(parts, axis=-1).reshape(lead + (total // LANES, LANES))


def _unpack(buf, shapes, lead=()):
    flat = buf.reshape(lead + (-1,))
    out, off = [], 0
    for shp in shapes:
        n = math.prod(shp)
        out.append(lax.slice_in_dim(flat, off, off + n, axis=len(lead)).reshape(lead + tuple(shp)))
        off += n
    return out


BIG = ("w_in", "w_glu", "w_pa", "w_pb", "w_out", "w_up", "w_down")
WEIGHTS = ("g_mix", "w_in", "s5_a_re", "s5_a_im", "s5_log_dt", "s5_b_re", "s5_b_im", "s5_c_re", "s5_c_im", "s5_d",
           "w_glu", "b_glu", "hg_lb_logits", "hg_norm_gain", "w_pa", "w_pb", "w_out", "g_ffn", "w_up", "w_conv",
           "b_conv", "w_down", "g_final")
SMALL = tuple(n for n in WEIGHTS if n not in BIG)
SMALL_PARTS = ("loss", "g_mix", "g_ffn", "g_final", "b_glu", "gain", "lbrow", "s5_d", "w_conv", "b_conv", "lam_re",
               "lam_im", "bb_re", "bb_im", "s5_c_re", "s5_c_im")


def _lower_bound(logits):
    return jnp.cumsum(jax.nn.softmax(logits, axis=0), axis=0)[0:1]


def kernel(x, g_mix, w_in, s5_a_re, s5_a_im, s5_log_dt, s5_b_re, s5_b_im, s5_c_re, s5_c_im, s5_d, w_glu, b_glu, hg_lb_logits, hg_norm_gain, w_pa, w_pb, w_out, g_ffn, w_up, w_conv, b_conv, w_down, g_final, loss_target, m_g_mix, m_w_in, m_s5_a_re, m_s5_a_im, m_s5_log_dt, m_s5_b_re, m_s5_b_im, m_s5_c_re, m_s5_c_im, m_s5_d, m_w_glu, m_b_glu, m_hg_lb_logits, m_hg_norm_gain, m_w_pa, m_w_pb, m_w_out, m_g_ffn, m_w_up, m_w_conv, m_b_conv, m_w_down, m_g_final, v_g_mix, v_w_in, v_s5_a_re, v_s5_a_im, v_s5_log_dt, v_s5_b_re, v_s5_b_im, v_s5_c_re, v_s5_c_im, v_s5_d, v_w_glu, v_b_glu, v_hg_lb_logits, v_hg_norm_gain, v_w_pa, v_w_pb, v_w_out, v_g_ffn, v_w_up, v_w_conv, v_b_conv, v_w_down, v_g_final):
    args = dict(locals())
    w = {n: args[n] for n in WEIGHTS}
    mom = {n: args["m_" + n] for n in WEIGHTS}
    var = {n: args["v_" + n] for n in WEIGHTS}
    nseq, seq, d = x.shape
    xi, yi = lax.axis_index("x"), lax.axis_index("y")
    chip = 2 * xi + yi

    shard = {n: w[n][0] for n in BIG}
    shard16 = [shard[n].astype(BF16) for n in BIG]
    got = _gather_weights(shard16, [w_conv[0]])
    full = {n: lax.dynamic_update_index_in_dim(g, s, chip, 0) for n, g, s in zip(BIG, got, shard16)}
    for n in ("w_glu", "w_out", "w_down"):
        full[n] = full[n].reshape(-1, full[n].shape[-1])
    conv_all = lax.dynamic_update_index_in_dim(got[-1], w_conv[0], chip, 0)
    conv_full = conv_all.transpose(1, 0, 2).reshape(CONV_W, -1)

    p = dict(full)
    p.update(g_mix=g_mix, g_ffn=g_ffn, g_final=g_final.reshape(1, -1), b_glu=b_glu, gain=hg_norm_gain, s5_d=s5_d,
             b_conv=b_conv, w_conv=conv_full, lbrow=_lower_bound(hg_lb_logits),
             s5_a_re=s5_a_re[0], s5_a_im=s5_a_im[0], s5_log_dt=s5_log_dt[0], s5_b_re=s5_b_re[0], s5_b_im=s5_b_im[0],
             s5_c_re=s5_c_re[0], s5_c_im=s5_c_im[0])

    dx, gbig, gsmall = _local_step(x.reshape(nseq * seq, d), loss_target.reshape(nseq * seq, d), p, nseq=nseq, seq=seq)

    ci = lax.axis_index("c")
    parts = [gbig[n].reshape((N_CHIPS, -1, gbig[n].shape[-1])) for n in BIG]
    pair = []
    for n, g, s in zip(BIG, parts, _swap_halves(parts)):
        rh, cols = s.shape[1], s.shape[2]
        own = lax.dynamic_slice_in_dim(g, ci * rh, rh, axis=1)
        both = _sum_blocks("grad_pair_sum_" + n, [own.reshape(-1, cols), s.reshape(-1, cols)], BF16)
        pair.append(both.reshape(N_CHIPS, rh, cols))
    halves = []
    for n, h, o in zip(BIG, pair, _scatter_to_chips(pair)):
        mine = lax.dynamic_index_in_dim(h, chip, axis=0, keepdims=False)
        halves.append(_sum_blocks("grad_chip_sum_" + n, [mine, o[0], o[1], o[2]], F32))
    grads = {}
    for n, own, s in zip(BIG, halves, _swap_sums(halves)):
        both = jnp.concatenate([own, own], axis=0)
        grads[n] = lax.dynamic_update_slice_in_dim(both, s, (1 - ci) * own.shape[0], axis=0)

    small_shapes = [gsmall[n].shape for n in SMALL_PARTS]
    vec = _pack([gsmall[n] for n in SMALL_PARTS], F32)
    gathered = _gather_all(vec)
    mrows = vec.shape[0]
    vsum = _sum_blocks("small_grad_sum", [gathered[i * mrows:(i + 1) * mrows] for i in range(N_DEV)], F32)
    sm = dict(zip(SMALL_PARTS, _unpack(vsum, small_shapes)))
    loss = sm["loss"][0, 0]

    _, disc_vjp = jax.vjp(_s5_discretize, p["s5_a_re"], p["s5_a_im"], p["s5_log_dt"], p["s5_b_re"], p["s5_b_im"])
    da_re, da_im, dlog_dt, db_re, db_im = disc_vjp((sm["lam_re"], sm["lam_im"], sm["bb_re"], sm["bb_im"]))
    _, lb_vjp = jax.vjp(_lower_bound, hg_lb_logits)
    (dlogits,) = lb_vjp(sm["lbrow"])
    fcols = w_conv.shape[-1]
    grads.update(
        g_mix=sm["g_mix"], g_ffn=sm["g_ffn"], g_final=sm["g_final"].reshape(-1), b_glu=sm["b_glu"],
        hg_norm_gain=sm["gain"], hg_lb_logits=dlogits, s5_d=sm["s5_d"], b_conv=sm["b_conv"],
        w_conv=lax.dynamic_slice_in_dim(sm["w_conv"], chip * fcols, fcols, axis=1),
        s5_a_re=da_re, s5_a_im=da_im, s5_log_dt=dlog_dt, s5_b_re=db_re, s5_b_im=db_im,
        s5_c_re=sm["s5_c_re"], s5_c_im=sm["s5_c_im"])
    grads = {n: grads[n].reshape(w[n].shape) for n in WEIGHTS}

    delta, new_m, new_v = {}, {}, {}
    for n in BIG:
        shp = shard[n].shape
        dl, m2, v2 = _adamw("adamw_" + n, shard[n], grads[n].reshape(shp), mom[n].reshape(shp), var[n].reshape(shp))
        delta[n], new_m[n], new_v[n] = dl, m2, v2
    def natural(a):
        return a.reshape(1, -1) if a.ndim == 1 else (a[0] if a.ndim > 2 else a)

    outs = _adamw_small(*[[natural(src[n]) for n in SMALL] for src in (w, grads, mom, var)])
    for dst, group in zip((delta, new_m, new_v), outs):
        dst.update(zip(SMALL, group))
    res = [loss, dx.reshape(x.shape)]
    for group in (grads, delta, new_m, new_v):
        res += [group[n].reshape(w[n].shape) for n in WEIGHTS]
    return tuple(res)
```

```python
import functools
import math

import jax
import jax.numpy as jnp
from jax import lax
from jax.experimental import pallas as pl
from jax.experimental.pallas import tpu as pltpu

F32 = jnp.float32
BF16 = jnp.bfloat16
MESH = pl.DeviceIdType.MESH

EPS = 1e-6
S5_GROUP = 16
S5_STATE = 64
S5_BLOCK_GROUPS = 8
HEAD = 128
CHUNK = 64
CONV_W = 3
LANES = 128
SUBLANES = 8
GATE_BLOCK = 512
VMEM_LIMIT_BYTES = 56 * 1024 * 1024

ADAM_LR = 0.001
ADAM_B1 = 0.9
ADAM_B2 = 0.999
ADAM_EPS = 1e-08
ADAM_WD = 0.01
ADAM_STEP = 10

N_CHIPS = 4
N_DEV = 8


def _params(*sem):
    return pltpu.CompilerParams(dimension_semantics=sem, vmem_limit_bytes=VMEM_LIMIT_BYTES)


def _row_tile(rows, cap):
    if rows <= cap:
        return rows
    for t in range(cap - cap % 8, 7, -8):
        if rows % t == 0:
            return t
    raise ValueError(f"no row tile for {rows}")


def _dot(a, b):
    return jnp.dot(a.astype(BF16), b.astype(BF16), preferred_element_type=F32)


def _dot_nt(a, b):
    return lax.dot_general(a.astype(BF16), b.astype(BF16), (((1,), (1,)), ((), ())), preferred_element_type=F32)


def _dot_tn(a, b):
    return lax.dot_general(a.astype(BF16), b.astype(BF16), (((0,), (0,)), ((), ())), preferred_element_type=F32)


def _sigmoid(x):
    return 1.0 / (1.0 + jnp.exp(-x))


_GELU_C = math.sqrt(2.0 / math.pi)


def _gelu(x):
    return 0.5 * x * (1.0 + jnp.tanh(_GELU_C * (x + 0.044715 * x * x * x)))


def _gelu_grad(x):
    th = jnp.tanh(_GELU_C * (x + 0.044715 * x * x * x))
    return 0.5 * (1.0 + th) + 0.5 * x * (1.0 - th * th) * _GELU_C * (1.0 + 3.0 * 0.044715 * x * x)


def _rowwise(name, fn, ins, outs, accs=(), *, rows, tm, ncol=1):
    n_in, n_out = len(ins), len(outs)

    def body(*refs):
        res = fn(*[r[...] for r in refs[:n_in]])
        for r, v in zip(refs[n_in:n_in + n_out], res[:n_out]):
            r[...] = v.astype(r.dtype)
        first = pl.program_id(1) == 0
        for r, v in zip(refs[n_in + n_out:], res[n_out:]):
            @pl.when(first)
            def _():
                r[...] = v

            @pl.when(jnp.logical_not(first))
            def _():
                r[...] += v

    in_specs = []
    for _, width, base, kind in ins:
        if kind == "row":
            in_specs.append(pl.BlockSpec((tm, width), lambda j, i, b=base: (i, b + j)))
        else:
            in_specs.append(pl.BlockSpec((1, width), lambda j, i, b=base: (0, b + j)))
    out_specs = [pl.BlockSpec((tm, width), lambda j, i: (i, j)) for _, width, _ in outs]
    out_specs += [pl.BlockSpec((1, width), lambda j, i: (0, j)) for _, width in accs]
    out_shape = [jax.ShapeDtypeStruct((rows, total), dt) for total, _, dt in outs]
    out_shape += [jax.ShapeDtypeStruct((1, total), F32) for total, _ in accs]
    return pl.pallas_call(
        body, name=name, grid=(ncol, rows // tm), in_specs=in_specs, out_specs=out_specs, out_shape=out_shape,
        compiler_params=_params("arbitrary", "arbitrary"),
    )(*[a for a, _, _, _ in ins])


def _mm(name, a, b, *, mode, grid, a_spec, b_spec, o_spec, out_shape, acc_shape, res=None, res_spec=None):
    nk = grid[2]
    dot = {"nn": _dot, "nt": _dot_nt, "tn": _dot_tn}[mode]

    def body(*refs):
        a_ref, b_ref = refs[0], refs[1]
        r_ref = None if res is None else refs[2]
        o_ref = refs[2 if res is None else 3]

        def finish(v):
            if res is not None:
                v = v + r_ref[...]
            o_ref[...] = v.astype(o_ref.dtype)

        if nk == 1:
            finish(dot(a_ref[...], b_ref[...]))
            return
        acc_ref = refs[-1]
        k = pl.program_id(2)

        @pl.when(k == 0)
        def _():
            acc_ref[...] = jnp.zeros_like(acc_ref)

        acc_ref[...] += dot(a_ref[...], b_ref[...])

        @pl.when(k == nk - 1)
        def _():
            finish(acc_ref[...])

    operands = [a, b] + ([] if res is None else [res])
    in_specs = [a_spec, b_spec] + ([] if res is None else [res_spec])
    return pl.pallas_call(
        body, name=name, grid=grid, in_specs=in_specs, out_specs=o_spec, out_shape=out_shape,
        scratch_shapes=[] if nk == 1 else [pltpu.VMEM(acc_shape, F32)],
        compiler_params=_params("arbitrary", "arbitrary", "arbitrary"),
    )(*operands)


MM_TILE_BUDGET_BYTES = 36 * 1024 * 1024
MM_TILE_CAP = 1024


def _mm_tile(t, row_bytes, fixed_bytes):
    cap = max(16, min(MM_TILE_CAP, (MM_TILE_BUDGET_BYTES - fixed_bytes) // row_bytes))
    return _row_tile(t, cap - cap % 16)


def _size(a):
    return jnp.dtype(a.dtype).itemsize


def _mm_fwd_cols(name, a, w3, out_dtype=F32):
    t, k = a.shape
    ns = w3.shape[2]
    tm = _mm_tile(t, 2 * k * _size(a) + 2 * ns * jnp.dtype(out_dtype).itemsize, 2 * k * ns * _size(w3))
    return _mm(name, a, w3, mode="nn", grid=(N_CHIPS, t // tm, 1),
               a_spec=pl.BlockSpec((tm, k), lambda j, i, kk: (i, 0)),
               b_spec=pl.BlockSpec((None, k, ns), lambda j, i, kk: (j, 0, 0)),
               o_spec=pl.BlockSpec((tm, ns), lambda j, i, kk: (i, j)),
               out_shape=jax.ShapeDtypeStruct((t, N_CHIPS * ns), out_dtype), acc_shape=(tm, ns))


def _mm_bwd_cols(name, d, w3, out_dtype=F32):
    t = d.shape[0]
    k, ns = w3.shape[1], w3.shape[2]
    tm = _mm_tile(t, 2 * ns * _size(d) + 2 * k * jnp.dtype(out_dtype).itemsize + 4 * k, 2 * k * ns * _size(w3))
    return _mm(name, d, w3, mode="nt", grid=(t // tm, 1, N_CHIPS),
               a_spec=pl.BlockSpec((tm, ns), lambda i, j, kk: (i, kk)),
               b_spec=pl.BlockSpec((None, k, ns), lambda i, j, kk: (kk, 0, 0)),
               o_spec=pl.BlockSpec((tm, k), lambda i, j, kk: (i, 0)),
               out_shape=jax.ShapeDtypeStruct((t, k), out_dtype), acc_shape=(tm, k))


def _mm_wgrad_cols(name, a, d):
    t, k = a.shape
    ns = d.shape[1] // N_CHIPS
    tk = _mm_tile(t, 2 * k * _size(a) + 2 * ns * _size(d), k * ns * (4 + 2 * 2))
    return _mm(name, a, d, mode="tn", grid=(N_CHIPS, 1, t // tk),
               a_spec=pl.BlockSpec((tk, k), lambda j, i, kk: (kk, 0)),
               b_spec=pl.BlockSpec((tk, ns), lambda j, i, kk: (kk, j)),
               o_spec=pl.BlockSpec((None, k, ns), lambda j, i, kk: (j, 0, 0)),
               out_shape=jax.ShapeDtypeStruct((N_CHIPS, k, ns), BF16), acc_shape=(k, ns))


MM_BLOCK_CAP = 1408


def _mm_fwd_rows(name, a, w, res=None, out_dtype=F32):
    t, k = a.shape
    n = w.shape[1]
    tk = k if k <= MM_BLOCK_CAP else MM_BLOCK_CAP
    assert k % tk == 0
    row_bytes = 2 * tk * _size(a) + 2 * n * jnp.dtype(out_dtype).itemsize + (0 if res is None else 2 * n * 4) + 4 * n
    tm = _mm_tile(t, row_bytes, 2 * tk * n * _size(w))
    return _mm(name, a, w, mode="nn", grid=(t // tm, 1, k // tk),
               a_spec=pl.BlockSpec((tm, tk), lambda i, j, kk: (i, kk)),
               b_spec=pl.BlockSpec((tk, n), lambda i, j, kk: (kk, 0)),
               o_spec=pl.BlockSpec((tm, n), lambda i, j, kk: (i, 0)),
               out_shape=jax.ShapeDtypeStruct((t, n), out_dtype), acc_shape=(tm, n),
               res=res, res_spec=None if res is None else pl.BlockSpec((tm, n), lambda i, j, kk: (i, 0)))


def _mm_bwd_rows(name, d, w, out_dtype=F32):
    t, n = d.shape
    k = w.shape[0]
    tn = k if k <= MM_BLOCK_CAP else MM_BLOCK_CAP
    assert k % tn == 0
    tm = _mm_tile(t, 2 * n * _size(d) + 2 * tn * jnp.dtype(out_dtype).itemsize, 2 * tn * n * _size(w))
    return _mm(name, d, w, mode="nt", grid=(t // tm, k // tn, 1),
               a_spec=pl.BlockSpec((tm, n), lambda i, j, kk: (i, 0)),
               b_spec=pl.BlockSpec((tn, n), lambda i, j, kk: (j, 0)),
               o_spec=pl.BlockSpec((tm, tn), lambda i, j, kk: (i, j)),
               out_shape=jax.ShapeDtypeStruct((t, k), out_dtype), acc_shape=(tm, tn))


def _mm_wgrad_rows(name, a, d):
    t, k = a.shape
    n = d.shape[1]
    nblk = next(b for b in (1, 2, 4) if (k // b) % LANES == 0 and k // b <= MM_BLOCK_CAP)
    ks = k // nblk
    tk = _mm_tile(t, 2 * ks * _size(a) + 2 * n * _size(d), ks * n * (4 + 2 * 2))
    return _mm(name, a, d, mode="tn", grid=(nblk, 1, t // tk),
               a_spec=pl.BlockSpec((tk, ks), lambda j, i, kk: (kk, j)),
               b_spec=pl.BlockSpec((tk, n), lambda j, i, kk: (kk, 0)),
               o_spec=pl.BlockSpec((ks, n), lambda j, i, kk: (j, 0)),
               out_shape=jax.ShapeDtypeStruct((k, n), BF16), acc_shape=(ks, n))


def _s5_discretize(a_re, a_im, log_dt, b_re, b_im):
    dt = jnp.exp(log_dt)[:, None]
    mag = jnp.exp(a_re * dt)
    ang = a_im * dt
    lb_re = mag * jnp.cos(ang)
    lb_im = mag * jnp.sin(ang)
    den = a_re * a_re + a_im * a_im
    n_re = lb_re - 1.0
    n_im = lb_im
    co_re = ((n_re * a_re + n_im * a_im) / den)[..., None]
    co_im = ((n_im * a_re - n_re * a_im) / den)[..., None]
    bb_re = co_re * b_re - co_im * b_im
    bb_im = co_re * b_im + co_im * b_re
    return lb_re, lb_im, bb_re, bb_im


def _s5_in_blocks(bb):
    g = bb.shape[0]
    nb = g // S5_BLOCK_GROUPS
    t = bb.reshape(nb, S5_BLOCK_GROUPS, S5_STATE, S5_GROUP).transpose(0, 1, 3, 2)
    eye = jnp.eye(S5_BLOCK_GROUPS, dtype=bb.dtype)
    full = t[:, :, :, None, :] * eye[None, :, None, :, None]
    return full.reshape(nb, S5_BLOCK_GROUPS * S5_GROUP, S5_BLOCK_GROUPS * S5_STATE)


def _s5_in_blocks_diag(blocks):
    nb = blocks.shape[0]
    t = blocks.reshape(nb, S5_BLOCK_GROUPS, S5_GROUP, S5_BLOCK_GROUPS, S5_STATE)
    d = jnp.einsum("bghgp->bghp", t)
    return d.transpose(0, 1, 3, 2).reshape(nb * S5_BLOCK_GROUPS, S5_STATE, S5_GROUP)


def _s5_out_blocks(c):
    g = c.shape[0]
    nb = g // S5_BLOCK_GROUPS
    t = c.reshape(nb, S5_BLOCK_GROUPS, S5_GROUP, S5_STATE).transpose(0, 1, 3, 2)
    eye = jnp.eye(S5_BLOCK_GROUPS, dtype=c.dtype)
    full = t[:, :, :, None, :] * eye[None, :, None, :, None]
    return full.reshape(nb, S5_BLOCK_GROUPS * S5_STATE, S5_BLOCK_GROUPS * S5_GROUP)


def _s5_out_blocks_diag(blocks):
    nb = blocks.shape[0]
    t = blocks.reshape(nb, S5_BLOCK_GROUPS, S5_STATE, S5_BLOCK_GROUPS, S5_GROUP)
    d = jnp.einsum("bgpgh->bgph", t)
    return d.transpose(0, 1, 3, 2).reshape(nb * S5_BLOCK_GROUPS, S5_GROUP, S5_STATE)


def _s5_scan_tables(lr, li, reverse):
    def cmul(a, b):
        return a[0] * b[0] - a[1] * b[1], a[0] * b[1] + a[1] * b[0]

    lam = (lr, -li) if reverse else (lr, li)
    pw = [lam]
    for _ in range(SUBLANES - 1):
        pw.append(cmul(pw[-1], lam))
    sub = jnp.arange(SUBLANES)[:, None]
    rows = []
    for s in (1, 2, 4):
        keep = (sub <= SUBLANES - 1 - s) if reverse else (sub >= s)
        rows.append(jnp.where(keep, pw[s - 1][0][None, :], 0.0))
        rows.append(jnp.where(keep, pw[s - 1][1][None, :], 0.0))
    order = list(range(SUBLANES - 1, -1, -1)) if reverse else list(range(SUBLANES))
    rows.append(jnp.stack([pw[i][0] for i in order]))
    rows.append(jnp.stack([pw[i][1] for i in order]))
    return jnp.concatenate(rows, axis=0)


def _s5_scan(vre_ref, vim_ref, coef_ref, seq, width, reverse, xre_ref=None, xim_ref=None):
    nt = seq // SUBLANES
    nl = width // LANES
    per = 2 if xre_ref is None else 4
    sub = lax.broadcasted_iota(jnp.int32, (SUBLANES, LANES), 0)

    def step(k, carry):
        kk = (nt - 1 - k) if reverse else k
        rows = pl.ds(pl.multiple_of(kk * SUBLANES, SUBLANES), SUBLANES)
        out = []
        for j in range(nl):
            lanes = slice(j * LANES, (j + 1) * LANES)
            co = [coef_ref[SUBLANES * q:SUBLANES * (q + 1), lanes] for q in range(8)]
            cr, ci = carry[per * j], carry[per * j + 1]
            vr = vre_ref[rows, lanes]
            vi = vim_ref[rows, lanes]
            for q, s in enumerate((1, 2, 4)):
                sh = SUBLANES - s if reverse else s
                rr = pltpu.roll(vr, sh, 0)
                ri = pltpu.roll(vi, sh, 0)
                ar, ai = co[2 * q], co[2 * q + 1]
                vr, vi = vr + ar * rr - ai * ri, vi + ar * ri + ai * rr
            edge = 0 if reverse else SUBLANES - 1
            cbr = jnp.broadcast_to(cr[edge:edge + 1, :], (SUBLANES, LANES))
            cbi = jnp.broadcast_to(ci[edge:edge + 1, :], (SUBLANES, LANES))
            pr, pi = co[6], co[7]
            vr, vi = vr + pr * cbr - pi * cbi, vi + pr * cbi + pi * cbr
            vre_ref[rows, lanes] = vr
            vim_ref[rows, lanes] = vi
            out += [vr, vi]
            if xre_ref is not None:
                nr = jnp.where(sub == SUBLANES - 1, cbr, pltpu.roll(vr, SUBLANES - 1, 0))
                ni = jnp.where(sub == SUBLANES - 1, cbi, pltpu.roll(vi, SUBLANES - 1, 0))
                xr = xre_ref[rows, lanes]
                xi = xim_ref[rows, lanes]
                out += [carry[per * j + 2] + nr * xr + ni * xi, carry[per * j + 3] + ni * xr - nr * xi]
        return tuple(out)

    zero = jnp.zeros((SUBLANES, LANES), F32)
    res = lax.fori_loop(0, nt, step, (zero,) * (per * nl))
    if xre_ref is None:
        return None
    return jnp.concatenate(
        [jnp.concatenate([jnp.sum(res[per * j + 2], axis=0, keepdims=True) for j in range(nl)], axis=1),
         jnp.concatenate([jnp.sum(res[per * j + 3], axis=0, keepdims=True) for j in range(nl)], axis=1)], axis=0)


def _s5_fwd(z, bre3, bim3, cre3, cim3, coef, dskip, *, nseq, seq):
    nb = bre3.shape[0]
    ch, ns = bre3.shape[1], bre3.shape[2]

    def body(za_ref, bre_ref, bim_ref, cre_ref, cim_ref, coef_ref, d_ref, y_ref, xre_ref, xim_ref):
        za = za_ref[...]
        xre_ref[...] = _dot(za, bre_ref[...])
        xim_ref[...] = _dot(za, bim_ref[...])
        _s5_scan(xre_ref, xim_ref, coef_ref, seq, ns, False)
        y_ref[...] = _dot(xre_ref[...], cre_ref[...]) - _dot(xim_ref[...], cim_ref[...]) + d_ref[...] * za

    blk3 = lambda r, c: pl.BlockSpec((None, r, c), lambda b, j: (j, 0, 0))
    return pl.pallas_call(
        body, name="s5_fwd", grid=(nseq, nb),
        in_specs=[pl.BlockSpec((seq, ch), lambda b, j: (b, j)), blk3(ch, ns), blk3(ch, ns), blk3(ns, ch), blk3(ns, ch),
                  pl.BlockSpec((8 * SUBLANES, ns), lambda b, j: (0, j)), pl.BlockSpec((1, ch), lambda b, j: (0, j))],
        out_specs=[pl.BlockSpec((seq, ch), lambda b, j: (b, j)), pl.BlockSpec((seq, ns), lambda b, j: (b, j)),
                   pl.BlockSpec((seq, ns), lambda b, j: (b, j))],
        out_shape=[jax.ShapeDtypeStruct((nseq * seq, nb * ch), F32), jax.ShapeDtypeStruct((nseq * seq, nb * ns), F32),
                   jax.ShapeDtypeStruct((nseq * seq, nb * ns), F32)],
        compiler_params=_params("arbitrary", "arbitrary"),
    )(z, bre3, bim3, cre3, cim3, coef, dskip)


def _s5_bwd(dy, z, xre, xim, bre3, bim3, cre3, cim3, coef_rev, dskip, *, nseq, seq):
    nb = bre3.shape[0]
    ch, ns = bre3.shape[1], bre3.shape[2]

    def body(dy_ref, za_ref, xre_ref, xim_ref, bre_ref, bim_ref, cre_ref, cim_ref, coef_ref, d_ref,
             dza_ref, dbre_ref, dbim_ref, dcre_ref, dcim_ref, dlam_ref, dd_ref, are_ref, aim_ref):
        dy = dy_ref[...]
        za = za_ref[...]
        are_ref[...] = _dot_nt(dy, cre_ref[...])
        aim_ref[...] = -_dot_nt(dy, cim_ref[...])
        dlam = _s5_scan(are_ref, aim_ref, coef_ref, seq, ns, True, xre_ref, xim_ref)
        are = are_ref[...]
        aim = aim_ref[...]
        dza_ref[...] = (_dot_nt(are, bre_ref[...]) + _dot_nt(aim, bim_ref[...]) + d_ref[...] * dy).astype(dza_ref.dtype)
        parts = (_dot_tn(za, are), _dot_tn(za, aim), _dot_tn(xre_ref[...], dy), -_dot_tn(xim_ref[...], dy),
                 dlam, jnp.sum(dy * za, axis=0, keepdims=True))
        first = pl.program_id(1) == 0
        for r, v in zip((dbre_ref, dbim_ref, dcre_ref, dcim_ref, dlam_ref, dd_ref), parts):
            @pl.when(first)
            def _():
                r[...] = v

            @pl.when(jnp.logical_not(first))
            def _():
                r[...] += v

    blk3 = lambda r, c: pl.BlockSpec((None, r, c), lambda j, b: (j, 0, 0))
    tok = lambda c: pl.BlockSpec((seq, c), lambda j, b: (b, j))
    return pl.pallas_call(
        body, name="s5_bwd", grid=(nb, nseq),
        in_specs=[tok(ch), tok(ch), tok(ns), tok(ns), blk3(ch, ns), blk3(ch, ns), blk3(ns, ch), blk3(ns, ch),
                  pl.BlockSpec((8 * SUBLANES, ns), lambda j, b: (0, j)), pl.BlockSpec((1, ch), lambda j, b: (0, j))],
        out_specs=[tok(ch), blk3(ch, ns), blk3(ch, ns), blk3(ns, ch), blk3(ns, ch),
                   pl.BlockSpec((None, 2, ns), lambda j, b: (j, 0, 0)), pl.BlockSpec((1, ch), lambda j, b: (0, j))],
        out_shape=[jax.ShapeDtypeStruct((nseq * seq, nb * ch), BF16),
                   jax.ShapeDtypeStruct((nb, ch, ns), F32), jax.ShapeDtypeStruct((nb, ch, ns), F32),
                   jax.ShapeDtypeStruct((nb, ns, ch), F32), jax.ShapeDtypeStruct((nb, ns, ch), F32),
                   jax.ShapeDtypeStruct((nb, 2, ns), F32), jax.ShapeDtypeStruct((1, nb * ch), F32)],
        scratch_shapes=[pltpu.VMEM((seq, ns), F32), pltpu.VMEM((seq, ns), F32)],
        compiler_params=_params("arbitrary", "arbitrary"),
    )(dy, z, xre, xim, bre3, bim3, cre3, cim3, coef_rev, dskip)


def _cumsum_rows(x, reverse=False):
    n = x.shape[0]
    row = lax.broadcasted_iota(jnp.int32, x.shape, 0)
    s = 1
    while s < n:
        if reverse:
            x = x + jnp.where(row < n - s, pltpu.roll(x, n - s, 0), 0.0)
        else:
            x = x + jnp.where(row >= s, pltpu.roll(x, s, 0), 0.0)
        s *= 2
    return x


def _hg_gates(zq, zf, lb):
    sg = _sigmoid(zf)
    f = lb + (1.0 - lb) * sg
    sq = _sigmoid(zq)
    qa = zq * sq * (HEAD ** -0.5)
    b = _cumsum_rows(jnp.log(f))
    return sg, f, sq, qa, 1.0 - f, b


SUB = 16


def _hg_scores(qa, kk, b):
    c = qa.shape[0]
    row = lax.broadcasted_iota(jnp.int32, qa.shape, 0)
    pos = jnp.bitwise_and(row, SUB - 1)
    dmat = lax.broadcasted_iota(jnp.int32, (c, c), 0) - lax.broadcasted_iota(jnp.int32, (c, c), 1)
    p = jnp.zeros((c, c), F32)
    for d in range(SUB):
        if d == 0:
            fd = qa * kk
        else:
            e = jnp.exp(jnp.minimum(b - pltpu.roll(b, d, 0), 0.0))
            fd = jnp.where(pos >= d, qa * pltpu.roll(kk, d, 0) * e, 0.0)
        p = jnp.where(dmat == d, jnp.sum(fd, axis=1, keepdims=True), p)
    col = lax.broadcasted_iota(jnp.int32, (SUB, c), 1)
    blocks = [jnp.zeros((SUB, c), F32)]
    for r0 in range(SUB, c, SUB):
        beta = b[r0 - 1:r0, :]
        qt = qa[r0:r0 + SUB] * jnp.exp(b[r0:r0 + SUB] - beta)
        kt = kk * jnp.exp(jnp.minimum(beta - b, 0.0))
        blocks.append(jnp.where(col < r0, _dot_nt(qt, kt), 0.0))
    return p + jnp.concatenate(blocks, axis=0)


def _hg_scores_bwd(dp, qa, kk, b):
    c = qa.shape[0]
    row = lax.broadcasted_iota(jnp.int32, qa.shape, 0)
    pos = jnp.bitwise_and(row, SUB - 1)
    dmat = lax.broadcasted_iota(jnp.int32, (c, c), 0) - lax.broadcasted_iota(jnp.int32, (c, c), 1)
    dqa = jnp.zeros_like(qa)
    dkk = jnp.zeros_like(qa)
    db = jnp.zeros_like(qa)
    for d in range(SUB):
        dcol = jnp.sum(jnp.where(dmat == d, dp, 0.0), axis=1, keepdims=True)
        if d == 0:
            dqa = dqa + dcol * kk
            dkk = dkk + dcol * qa
        else:
            e = jnp.exp(jnp.minimum(b - pltpu.roll(b, d, 0), 0.0))
            w = jnp.where(pos >= d, dcol * e, 0.0)
            kr = pltpu.roll(kk, d, 0)
            dqa = dqa + w * kr
            tmp = w * qa
            dkk = dkk + pltpu.roll(tmp, c - d, 0)
            x = tmp * kr
            db = db + x - pltpu.roll(x, c - d, 0)
    col = lax.broadcasted_iota(jnp.int32, (SUB, c), 1)
    dq_blocks = [jnp.zeros((SUB, qa.shape[1]), F32)]
    db_blocks = [jnp.zeros((SUB, qa.shape[1]), F32)]
    for r0 in range(SUB, c, SUB):
        beta = b[r0 - 1:r0, :]
        eq = jnp.exp(b[r0:r0 + SUB] - beta)
        ek = jnp.exp(jnp.minimum(beta - b, 0.0))
        qt = qa[r0:r0 + SUB] * eq
        kt = kk * ek
        dpi = jnp.where(col < r0, dp[r0:r0 + SUB, :], 0.0)
        dqt = _dot(dpi, kt)
        dkt = _dot_tn(dpi, qt)
        dq_blocks.append(dqt * eq)
        db_blocks.append(dqt * qt)
        dkk = dkk + dkt * ek
        db = db - dkt * kt
    return dqa + jnp.concatenate(dq_blocks, axis=0), dkk, db + jnp.concatenate(db_blocks, axis=0)


def _hg_chunks_per_step(seq):
    nc = seq // CHUNK
    cps = next(k for k in (4, 2, 1) if nc % k == 0)
    return nc, cps, nc // cps


def _hg_fwd(z, lbrow, gain, *, nseq, seq, heads, qoff):
    nc, cps, nblk = _hg_chunks_per_step(seq)
    blk = cps * CHUNK
    zspec = lambda off: pl.BlockSpec((blk, HEAD), lambda h, b, n, off=off: (b * nblk + n, off + h))

    def body(zq_ref, zf_ref, zi_ref, zg_ref, lb_ref, gn_ref, o_ref, yb_ref, st_ref, state):
        @pl.when(pl.program_id(2) == 0)
        def _():
            state[...] = jnp.zeros_like(state)

        lb = lb_ref[...]
        gain_v = gn_ref[...]

        def chunk(ci, carry):
            rows = pl.ds(pl.multiple_of(ci * CHUNK, CHUNK), CHUNK)
            st = state[...]
            st_ref[ci] = st
            zi = zi_ref[rows, :]
            zg = zg_ref[rows, :]
            _, _, _, qa, kk, b = _hg_gates(zq_ref[rows, :], zf_ref[rows, :], lb)
            o = _dot_nt(qa * jnp.exp(b), st) + _dot(_hg_scores(qa, kk, b), zi)
            bl = b[CHUNK - 1:CHUNK, :]
            state[...] = st * jnp.exp(bl) + _dot_tn(zi, kk * jnp.exp(bl - b))
            o_ref[rows, :] = o
            r = lax.rsqrt(jnp.mean(o * o, axis=1, keepdims=True) + EPS)
            yb_ref[rows, :] = (o * r * gain_v * zg * _sigmoid(zg)).astype(yb_ref.dtype)
            return carry

        lax.fori_loop(0, cps, chunk, 0)

    tok = pl.BlockSpec((blk, HEAD), lambda h, b, n: (b * nblk + n, h))
    vec = pl.BlockSpec((1, HEAD), lambda h, b, n: (0, h))
    rows = nseq * seq
    return pl.pallas_call(
        body, name="hgrn2_fwd", grid=(heads, nseq, nblk),
        in_specs=[zspec(qoff), zspec(qoff + heads), zspec(qoff + 2 * heads), zspec(qoff + 3 * heads), vec, vec],
        out_specs=[tok, tok, pl.BlockSpec((None, None, cps, HEAD, HEAD), lambda h, b, n: (h, b, n, 0, 0))],
        out_shape=[jax.ShapeDtypeStruct((rows, heads * HEAD), F32), jax.ShapeDtypeStruct((rows, heads * HEAD), BF16),
                   jax.ShapeDtypeStruct((heads, nseq, nc, HEAD, HEAD), F32)],
        scratch_shapes=[pltpu.VMEM((HEAD, HEAD), F32)],
        compiler_params=_params("arbitrary", "arbitrary", "arbitrary"),
    )(z, z, z, z, lbrow, gain)


def _hg_bwd(dyb, z, o, states, lbrow, gain, *, nseq, seq, heads, qoff):
    nc, cps, nblk = _hg_chunks_per_step(seq)
    blk = cps * CHUNK
    rev = lambda n: nblk - 1 - n
    zspec = lambda off: pl.BlockSpec((blk, HEAD), lambda h, b, n, off=off: (b * nblk + rev(n), off + h))

    def body(dyb_ref, zq_ref, zf_ref, zi_ref, zg_ref, o_ref, st_ref, lb_ref, gn_ref,
             dzq_ref, dzf_ref, dzi_ref, dzg_ref, dlb_ref, dgn_ref, dstate):
        @pl.when(pl.program_id(2) == 0)
        def _():
            dstate[...] = jnp.zeros_like(dstate)

        @pl.when(jnp.logical_and(pl.program_id(1) == 0, pl.program_id(2) == 0))
        def _():
            dlb_ref[...] = jnp.zeros_like(dlb_ref)
            dgn_ref[...] = jnp.zeros_like(dgn_ref)

        lb = lb_ref[...]
        gain_v = gn_ref[...]
        c = CHUNK
        causal = lax.broadcasted_iota(jnp.int32, (c, c), 0) >= lax.broadcasted_iota(jnp.int32, (c, c), 1)

        def chunk(step, carry):
            ci = cps - 1 - step
            rows = pl.ds(pl.multiple_of(ci * CHUNK, CHUNK), CHUNK)
            zq = zq_ref[rows, :]
            zi = zi_ref[rows, :]
            zg = zg_ref[rows, :]
            sg, f, sq, qa, kk, b = _hg_gates(zq, zf_ref[rows, :], lb)
            eb = jnp.exp(b)
            qt = qa * eb
            bl = b[c - 1:c, :]
            ebl = jnp.exp(bl)
            ekb = jnp.exp(bl - b)
            kh = kk * ekb
            st = st_ref[ci]
            dst = dstate[...]
            o = o_ref[rows, :]
            r = lax.rsqrt(jnp.mean(o * o, axis=1, keepdims=True) + EPS)
            oh = o * r
            szg = _sigmoid(zg)
            dyb = dyb_ref[rows, :]
            don = dyb * zg * szg
            dzg_ref[rows, :] = (dyb * oh * gain_v * szg * (1.0 + zg * (1.0 - szg))).astype(dzg_ref.dtype)
            doh = don * gain_v
            do = r * (doh - oh * jnp.mean(doh * oh, axis=1, keepdims=True))
            dqt = _dot(do, st)
            dp = jnp.where(causal, _dot_nt(do, zi), 0.0)
            p = _hg_scores(qa, kk, b)
            dzi_ref[rows, :] = (_dot_tn(p, do) + _dot_nt(kh, dst)).astype(dzi_ref.dtype)
            dkh = _dot(zi, dst)
            dbl = jnp.sum(dkh * kh, axis=0, keepdims=True) + jnp.sum(dst * st, axis=0, keepdims=True) * ebl
            dstate[...] = _dot_tn(do, qt) + dst * ebl
            dqa_s, dkk_s, db_s = _hg_scores_bwd(dp, qa, kk, b)
            dqa = dqt * eb + dqa_s
            dkk = dkh * ekb + dkk_s
            db = dqt * qt - dkh * kh + db_s
            row = lax.broadcasted_iota(jnp.int32, db.shape, 0)
            db = db + jnp.where(row == c - 1, dbl, 0.0)
            df = _cumsum_rows(db, reverse=True) / f - dkk
            dzf_ref[rows, :] = (df * (1.0 - lb) * sg * (1.0 - sg)).astype(dzf_ref.dtype)
            dzq_ref[rows, :] = (dqa * (HEAD ** -0.5) * sq * (1.0 + zq * (1.0 - sq))).astype(dzq_ref.dtype)
            dlb_ref[...] += jnp.sum(df * (1.0 - sg), axis=0, keepdims=True)
            dgn_ref[...] += jnp.sum(don * oh, axis=0, keepdims=True)
            return carry

        lax.fori_loop(0, cps, chunk, 0)

    tok = pl.BlockSpec((blk, HEAD), lambda h, b, n: (b * nblk + rev(n), h))
    vec = pl.BlockSpec((1, HEAD), lambda h, b, n: (0, h))
    rows = nseq * seq
    return pl.pallas_call(
        body, name="hgrn2_bwd", grid=(heads, nseq, nblk),
        in_specs=[tok, zspec(qoff), zspec(qoff + heads), zspec(qoff + 2 * heads), zspec(qoff + 3 * heads), tok,
                  pl.BlockSpec((None, None, cps, HEAD, HEAD), lambda h, b, n: (h, b, rev(n), 0, 0)), vec, vec],
        out_specs=[tok, tok, tok, tok, vec, vec],
        out_shape=[jax.ShapeDtypeStruct((rows, heads * HEAD), BF16)] * 4
        + [jax.ShapeDtypeStruct((1, heads * HEAD), F32)] * 2,
        scratch_shapes=[pltpu.VMEM((HEAD, HEAD), F32)],
        compiler_params=_params("arbitrary", "arbitrary", "arbitrary"),
    )(dyb, z, z, z, z, o, states, lbrow, gain)


def _conv_taps(h, w, bias):
    row = lax.broadcasted_iota(jnp.int32, h.shape, 0)
    h1 = jnp.where(row >= 1, pltpu.roll(h, 1, 0), 0.0)
    h2 = jnp.where(row >= 2, pltpu.roll(h, 2, 0), 0.0)
    return h2 * w[0:1, :] + h1 * w[1:2, :] + h * w[2:3, :] + bias, h1, h2


def _conv_fwd(h, wconv, bconv, *, nseq, seq):
    ff2 = h.shape[1]
    ncol = ff2 // 2 // LANES

    def body(hg_ref, hv_ref, wg_ref, wv_ref, bg_ref, bv_ref, a_ref):
        g, _, _ = _conv_taps(hg_ref[...], wg_ref[...], bg_ref[...])
        v, _, _ = _conv_taps(hv_ref[...], wv_ref[...], bv_ref[...])
        a_ref[...] = (g * _sigmoid(g) * v).astype(a_ref.dtype)

    tok = lambda off: pl.BlockSpec((seq, LANES), lambda j, b, off=off: (b, off + j))
    wsp = lambda off: pl.BlockSpec((CONV_W, LANES), lambda j, b, off=off: (0, off + j))
    bsp = lambda off: pl.BlockSpec((1, LANES), lambda j, b, off=off: (0, off + j))
    return pl.pallas_call(
        body, name="conv_fwd", grid=(ncol, nseq),
        in_specs=[tok(0), tok(ncol), wsp(0), wsp(ncol), bsp(0), bsp(ncol)],
        out_specs=tok(0), out_shape=jax.ShapeDtypeStruct((nseq * seq, ff2 // 2), BF16),
        compiler_params=_params("arbitrary", "arbitrary"),
    )(h, h, wconv, wconv, bconv, bconv)


def _conv_bwd(da, h, wconv, bconv, *, nseq, seq):
    ff2 = h.shape[1]
    ncol = ff2 // 2 // LANES

    def half_bwd(d, hcur, h1, h2, w):
        n = d.shape[0]
        row = lax.broadcasted_iota(jnp.int32, d.shape, 0)
        d1 = jnp.where(row < n - 1, pltpu.roll(d, n - 1, 0), 0.0)
        d2 = jnp.where(row < n - 2, pltpu.roll(d, n - 2, 0), 0.0)
        dh = d * w[2:3, :] + d1 * w[1:2, :] + d2 * w[0:1, :]
        stats = jnp.concatenate(
            [jnp.sum(h2 * d, axis=0, keepdims=True), jnp.sum(h1 * d, axis=0, keepdims=True),
             jnp.sum(hcur * d, axis=0, keepdims=True), jnp.sum(d, axis=0, keepdims=True),
             jnp.zeros((SUBLANES - 4, d.shape[1]), F32)], axis=0)
        return dh, stats

    def body(da_ref, hg_ref, hv_ref, wg_ref, wv_ref, bg_ref, bv_ref, dh_ref, st_ref, dhv_keep, stv_keep):
        half = pl.program_id(2)

        @pl.when(half == 0)
        def _():
            hg = hg_ref[...]
            hv = hv_ref[...]
            wg = wg_ref[...]
            wv = wv_ref[...]
            g, g1, g2 = _conv_taps(hg, wg, bg_ref[...])
            v, v1, v2 = _conv_taps(hv, wv, bv_ref[...])
            da = da_ref[...]
            s = _sigmoid(g)
            dhg, stg = half_bwd(da * v * s * (1.0 + g * (1.0 - s)), hg, g1, g2, wg)
            dhv, stv = half_bwd(da * g * s, hv, v1, v2, wv)
            dh_ref[...] = dhg.astype(dh_ref.dtype)
            st_ref[...] = stg
            dhv_keep[...] = dhv.astype(dhv_keep.dtype)
            stv_keep[...] = stv

        @pl.when(half == 1)
        def _():
            dh_ref[...] = dhv_keep[...]
            st_ref[...] = stv_keep[...]

    tok = lambda off: pl.BlockSpec((seq, LANES), lambda j, b, hf, off=off: (b, off + j))
    wsp = lambda off: pl.BlockSpec((CONV_W, LANES), lambda j, b, hf, off=off: (0, off + j))
    bsp = lambda off: pl.BlockSpec((1, LANES), lambda j, b, hf, off=off: (0, off + j))
    dh, stats = pl.pallas_call(
        body, name="conv_bwd", grid=(ncol, nseq, 2),
        in_specs=[tok(0), tok(0), tok(ncol), wsp(0), wsp(ncol), bsp(0), bsp(ncol)],
        out_specs=[pl.BlockSpec((seq, LANES), lambda j, b, hf: (b, j + hf * ncol)),
                   pl.BlockSpec((None, SUBLANES, LANES), lambda j, b, hf: (b, 0, j + hf * ncol))],
        out_shape=[jax.ShapeDtypeStruct((nseq * seq, ff2), BF16), jax.ShapeDtypeStruct((nseq, SUBLANES, ff2), F32)],
        scratch_shapes=[pltpu.VMEM((seq, LANES), BF16), pltpu.VMEM((SUBLANES, LANES), F32)],
        compiler_params=_params("arbitrary", "arbitrary", "arbitrary"),
    )(da, h, h, wconv, wconv, bconv, bconv)
    return dh, stats


def _rms_fwd(xv, g):
    r = lax.rsqrt(jnp.mean(xv * xv, axis=1, keepdims=True) + EPS)
    return (xv * r * g,)


def _rms_bwd(xv, g, dy, res):
    r = lax.rsqrt(jnp.mean(xv * xv, axis=1, keepdims=True) + EPS)
    xh = xv * r
    dxh = dy * g
    dx = r * (dxh - xh * jnp.mean(dxh * xh, axis=1, keepdims=True)) + res
    return dx, jnp.sum(dy * xh, axis=0, keepdims=True)


def _loss_head(x2, tgt, g):
    d = x2.shape[1]
    r = lax.rsqrt(jnp.mean(x2 * x2, axis=1, keepdims=True) + EPS)
    xh = x2 * r
    err = xh * g - tgt
    dy = err * (1.0 / d)
    dxh = dy * g
    dx = r * (dxh - xh * jnp.mean(dxh * xh, axis=1, keepdims=True))
    loss = 0.5 * jnp.sum(jnp.mean(err * err, axis=1, keepdims=True), axis=0, keepdims=True)
    return dx, jnp.sum(dy * xh, axis=0, keepdims=True), jnp.broadcast_to(loss, (1, LANES))


def _local_step(x, tgt, p, *, nseq, seq):
    t, d = x.shape
    s5w = p["s5_d"].shape[1]
    hgw = p["gain"].shape[1]
    heads = hgw // HEAD
    qoff = s5w // LANES
    gblk = (s5w + 4 * hgw) // GATE_BLOCK
    ngb = d // GATE_BLOCK
    tm = _row_tile(t, 256)
    row = lambda a, w=None, base=0: (a, a.shape[1] if w is None else w, base, "row")
    vec = lambda a, w=None, base=0: (a, a.shape[1] if w is None else w, base, "vec")
    rw = functools.partial(_rowwise, rows=t, tm=tm)

    (u,) = rw("rms_mix", _rms_fwd, [row(x), vec(p["g_mix"])], [(d, d, BF16)])
    z = _mm_fwd_cols("in_proj", u, p["w_in"])

    lam_re, lam_im, bb_re, bb_im = _s5_discretize(p["s5_a_re"], p["s5_a_im"], p["s5_log_dt"], p["s5_b_re"], p["s5_b_im"])
    bre3 = _s5_in_blocks(bb_re).astype(BF16)
    bim3 = _s5_in_blocks(bb_im).astype(BF16)
    cre3 = _s5_out_blocks(p["s5_c_re"]).astype(BF16)
    cim3 = _s5_out_blocks(p["s5_c_im"]).astype(BF16)
    coef_f = _s5_scan_tables(lam_re.reshape(-1), lam_im.reshape(-1), False)
    coef_r = _s5_scan_tables(lam_re.reshape(-1), lam_im.reshape(-1), True)
    y5, xre, xim = _s5_fwd(z, bre3, bim3, cre3, cim3, coef_f, p["s5_d"], nseq=nseq, seq=seq)
    (ya0,) = rw("s5_gelu", lambda y: (_gelu(y),), [row(y5)], [(s5w, s5w, BF16)])
    gl = _mm_fwd_rows("glu_proj", ya0, p["w_glu"])
    (ya,) = rw("s5_glu", lambda y, g, b: (_gelu(y) * _sigmoid(g + b),), [row(y5), row(gl), vec(p["b_glu"])],
               [(s5w, s5w, BF16)])

    o, yb, states = _hg_fwd(z, p["lbrow"], p["gain"], nseq=nseq, seq=seq, heads=heads, qoff=qoff)

    pa = _mm_fwd_cols("proj_a", ya, p["w_pa"], out_dtype=BF16)
    pb = _mm_fwd_cols("proj_b", yb, p["w_pb"], out_dtype=BF16)
    gb = GATE_BLOCK
    (m,) = rw("merge", lambda ga, gbv, a, b: (_sigmoid(ga) * a + _sigmoid(gbv) * b,),
              [row(z, gb, gblk), row(z, gb, gblk + ngb), row(pa, gb), row(pb, gb)], [(d, gb, BF16)], ncol=ngb)
    x1 = _mm_fwd_rows("out_proj", m, p["w_out"], res=x)

    (u2,) = rw("rms_ffn", _rms_fwd, [row(x1), vec(p["g_ffn"])], [(d, d, BF16)])
    h = _mm_fwd_cols("up_proj", u2, p["w_up"])
    a = _conv_fwd(h, p["w_conv"], p["b_conv"], nseq=nseq, seq=seq)
    x2 = _mm_fwd_rows("down_proj", a, p["w_down"], res=x1)

    dx2, dg_final, lossv = rw("loss_head", _loss_head, [row(x2), row(tgt), vec(p["g_final"])], [(d, d, F32)],
                              accs=[(d, d), (LANES, LANES)])

    da = _mm_bwd_rows("down_bwd", dx2, p["w_down"])
    g_wdown = _mm_wgrad_rows("down_wgrad", a, dx2)
    dh, cstats = _conv_bwd(da, h, p["w_conv"], p["b_conv"], nseq=nseq, seq=seq)
    cstats = _sum_blocks("conv_stat_sum", [cstats[i] for i in range(nseq)], F32)
    du2 = _mm_bwd_cols("up_bwd", dh, p["w_up"])
    g_wup = _mm_wgrad_cols("up_wgrad", u2, dh)
    dx1, dg_ffn = rw("rms_ffn_bwd", _rms_bwd, [row(x1), vec(p["g_ffn"]), row(du2), row(dx2)], [(d, d, F32)],
                     accs=[(d, d)])

    dm = _mm_bwd_rows("out_bwd", dx1, p["w_out"])
    g_wout = _mm_wgrad_rows("out_wgrad", m, dx1)

    def merge_bwd(ga, gbv, av, bv, dmv):
        sa = _sigmoid(ga)
        sb = _sigmoid(gbv)
        return dmv * sa, dmv * sb, dmv * av * sa * (1.0 - sa), dmv * bv * sb * (1.0 - sb)

    dpa, dpb, dzga, dzgb = rw("merge_bwd", merge_bwd,
                              [row(z, gb, gblk), row(z, gb, gblk + ngb), row(pa, gb), row(pb, gb), row(dm, gb)],
                              [(d, gb, BF16)] * 4, ncol=ngb)
    dya = _mm_bwd_cols("proj_a_bwd", dpa, p["w_pa"])
    g_wpa = _mm_wgrad_cols("proj_a_wgrad", ya, dpa)
    dyb = _mm_bwd_cols("proj_b_bwd", dpb, p["w_pb"])
    g_wpb = _mm_wgrad_cols("proj_b_wgrad", yb, dpb)

    def glu_bwd1(y, g, b, dyv):
        s = _sigmoid(g + b)
        dgl = dyv * _gelu(y) * s * (1.0 - s)
        return dgl, jnp.sum(dgl, axis=0, keepdims=True)

    dgl, db_glu = rw("s5_glu_bwd", glu_bwd1, [row(y5), row(gl), vec(p["b_glu"]), row(dya)], [(s5w, s5w, BF16)],
                     accs=[(s5w, s5w)])
    dgl_in = _mm_bwd_rows("glu_bwd", dgl, p["w_glu"])
    g_wglu = _mm_wgrad_rows("glu_wgrad", ya0, dgl)
    (dy5,) = rw("s5_gelu_bwd", lambda y, g, b, dyv, tv: ((dyv * _sigmoid(g + b) + tv) * _gelu_grad(y),),
                [row(y5), row(gl), vec(p["b_glu"]), row(dya), row(dgl_in)], [(s5w, s5w, F32)])
    dza, dbre3, dbim3, dcre3, dcim3, dlam, dd = _s5_bwd(dy5, z, xre, xim, bre3, bim3, cre3, cim3, coef_r, p["s5_d"],
                                                        nseq=nseq, seq=seq)

    dzq, dzf, dzi, dzg, dlb, dgain = _hg_bwd(dyb, z, o, states, p["lbrow"], p["gain"], nseq=nseq, seq=seq,
                                             heads=heads, qoff=qoff)

    dz = jnp.concatenate([dza, dzq, dzf, dzi, dzg, dzga, dzgb], axis=1)
    du = _mm_bwd_cols("in_bwd", dz, p["w_in"])
    g_win = _mm_wgrad_cols("in_wgrad", u, dz)
    dx, dg_mix = rw("rms_mix_bwd", _rms_bwd, [row(x), vec(p["g_mix"]), row(du), row(dx1)], [(d, d, F32)],
                    accs=[(d, d)])

    gshape = lam_re.shape
    big = {"w_in": g_win, "w_glu": g_wglu, "w_pa": g_wpa, "w_pb": g_wpb, "w_out": g_wout, "w_up": g_wup,
           "w_down": g_wdown}
    small = {
        "loss": lossv, "g_mix": dg_mix, "g_ffn": dg_ffn, "g_final": dg_final, "b_glu": db_glu, "gain": dgain,
        "lbrow": dlb, "s5_d": dd, "w_conv": cstats[0:CONV_W], "b_conv": cstats[CONV_W:CONV_W + 1],
        "lam_re": dlam[:, 0, :].reshape(gshape), "lam_im": dlam[:, 1, :].reshape(gshape),
        "bb_re": _s5_in_blocks_diag(dbre3), "bb_im": _s5_in_blocks_diag(dbim3),
        "s5_c_re": _s5_out_blocks_diag(dcre3), "s5_c_im": _s5_out_blocks_diag(dcim3),
    }
    return dx, big, small


ANY = pl.BlockSpec(memory_space=pl.ANY)


def _place():
    x, y, c = lax.axis_index("x"), lax.axis_index("y"), lax.axis_index("c")
    chips = [(1 - x, y), (x, 1 - y), (1 - x, 1 - y)]
    return x, y, c, chips


def _remote(src, dst, send_sems, recv_sems, k, to):
    return pltpu.make_async_remote_copy(src_ref=src, dst_ref=dst, send_sem=send_sems.at[k], recv_sem=recv_sems.at[k],
                                        device_id=to, device_id_type=MESH)


def _half(rows, which):
    return pl.ds(pl.multiple_of(which * (rows // 2), 16), rows // 2)


def _gather_weights(shards, whole):
    n, nw = len(shards), len(whole)
    arrays = list(shards) + list(whole)

    def body(*refs):
        in_refs, out_refs = refs[:n + nw], refs[n + nw:2 * (n + nw)]
        send_sems, recv_sems = refs[2 * (n + nw):]
        x, y, c, chips = _place()
        me = 2 * x + y
        copy = functools.partial(_remote, send_sems=send_sems, recv_sems=recv_sems)
        sends = []
        for a in range(n):
            mine_half = _half(arrays[a].shape[0], c)
            for j, (cx, cy) in enumerate(chips):
                sends.append(copy(in_refs[a].at[mine_half], out_refs[a].at[me, mine_half], k=6 * a + j, to=(cx, cy, c)))
        for a in range(n, n + nw):
            for j, (cx, cy) in enumerate(chips):
                sends.append(copy(in_refs[a], out_refs[a].at[me], k=6 * n + 3 * (a - n) + j, to=(cx, cy, c)))
        for cp in sends:
            cp.start()
        for a in range(n):
            mine_half = _half(arrays[a].shape[0], c)
            for j, (cx, cy) in enumerate(chips):
                landed = out_refs[a].at[2 * cx + cy, mine_half]
                copy(landed, landed, k=6 * a + j, to=(x, y, c)).wait_recv()
                fwd = copy(landed, landed, k=6 * a + 3 + j, to=(x, y, 1 - c))
                fwd.start()
                sends.append(fwd)
        for a in range(n):
            other_half = _half(arrays[a].shape[0], 1 - c)
            for j, (cx, cy) in enumerate(chips):
                landed = out_refs[a].at[2 * cx + cy, other_half]
                copy(landed, landed, k=6 * a + 3 + j, to=(x, y, c)).wait_recv()
        for a in range(n, n + nw):
            for j, (cx, cy) in enumerate(chips):
                landed = out_refs[a].at[2 * cx + cy]
                copy(landed, landed, k=6 * n + 3 * (a - n) + j, to=(x, y, c)).wait_recv()
        for cp in sends:
            cp.wait_send()

    nsem = 6 * n + 3 * nw
    return pl.pallas_call(
        body, name="gather_weights", out_shape=[jax.ShapeDtypeStruct((N_CHIPS,) + a.shape, a.dtype) for a in arrays],
        in_specs=[ANY] * (n + nw), out_specs=[ANY] * (n + nw),
        scratch_shapes=[pltpu.SemaphoreType.DMA((nsem,)), pltpu.SemaphoreType.DMA((nsem,))],
    )(*arrays)


def _swap_halves(parts):
    n = len(parts)

    def body(*refs):
        in_refs, out_refs, send_sems, recv_sems = refs[:n], refs[n:2 * n], refs[2 * n], refs[2 * n + 1]
        x, y, c, _ = _place()
        copies = [_remote(in_refs[a].at[:, _half(parts[a].shape[1], 1 - c), :], out_refs[a], send_sems, recv_sems, a,
                          (x, y, 1 - c)) for a in range(n)]
        for cp in copies:
            cp.start()
        for cp in copies:
            cp.wait()

    return pl.pallas_call(
        body, name="grad_swap_halves",
        out_shape=[jax.ShapeDtypeStruct((g.shape[0], g.shape[1] // 2, g.shape[2]), g.dtype) for g in parts],
        in_specs=[ANY] * n, out_specs=[ANY] * n,
        scratch_shapes=[pltpu.SemaphoreType.DMA((n,)), pltpu.SemaphoreType.DMA((n,))],
    )(*parts)


def _scatter_to_chips(parts):
    n = len(parts)

    def body(*refs):
        in_refs, out_refs, send_sems, recv_sems = refs[:n], refs[n:2 * n], refs[2 * n], refs[2 * n + 1]
        x, y, c, chips = _place()
        copies = [_remote(in_refs[a].at[2 * cx + cy], out_refs[a].at[j], send_sems, recv_sems, 3 * a + j, (cx, cy, c))
                  for a in range(n) for j, (cx, cy) in enumerate(chips)]
        for cp in copies:
            cp.start()
        for cp in copies:
            cp.wait()

    return pl.pallas_call(
        body, name="grad_scatter_chips",
        out_shape=[jax.ShapeDtypeStruct((N_CHIPS - 1,) + h.shape[1:], h.dtype) for h in parts],
        in_specs=[ANY] * n, out_specs=[ANY] * n,
        scratch_shapes=[pltpu.SemaphoreType.DMA((3 * n,)), pltpu.SemaphoreType.DMA((3 * n,))],
    )(*parts)


def _swap_sums(parts):
    n = len(parts)

    def body(*refs):
        in_refs, out_refs, send_sems, recv_sems = refs[:n], refs[n:2 * n], refs[2 * n], refs[2 * n + 1]
        x, y, c, _ = _place()
        copies = [_remote(in_refs[a], out_refs[a], send_sems, recv_sems, a, (x, y, 1 - c)) for a in range(n)]
        for cp in copies:
            cp.start()
        for cp in copies:
            cp.wait()

    return pl.pallas_call(
        body, name="grad_swap_sums", out_shape=[jax.ShapeDtypeStruct(g.shape, g.dtype) for g in parts],
        in_specs=[ANY] * n, out_specs=[ANY] * n,
        scratch_shapes=[pltpu.SemaphoreType.DMA((n,)), pltpu.SemaphoreType.DMA((n,))],
    )(*parts)


def _gather_all(v):
    m_per, n = v.shape

    def body(x_ref, out_ref, send_sems, recv_sems):
        x, y, c, chips = _place()
        me, sibling = (x, y, c), (x, y, 1 - c)

        def rows(px, py, pc):
            return out_ref.at[pl.ds(pl.multiple_of((4 * px + 2 * py + pc) * m_per, 8), m_per), :]

        def copy(k, block, to, src=None):
            return pltpu.make_async_remote_copy(src_ref=rows(*block) if src is None else src, dst_ref=rows(*block),
                                                send_sem=send_sems.at[k], recv_sem=recv_sems.at[k], device_id=to,
                                                device_id_type=MESH)

        out_ref[pl.ds(pl.multiple_of((4 * x + 2 * y + c) * m_per, 8), m_per), :] = x_ref[...]
        first = [copy(0, me, sibling, src=x_ref)]
        first += [copy(1 + j, me, (*chip, c), src=x_ref) for j, chip in enumerate(chips)]
        for cp in first:
            cp.start()
        passed = [copy(4 + j, (*chip, c), sibling) for j, chip in enumerate(chips)]
        for j, chip in enumerate(chips):
            copy(1 + j, (*chip, c), me).wait_recv()
            passed[j].start()
        copy(0, sibling, me).wait_recv()
        for j, chip in enumerate(chips):
            copy(4 + j, (*chip, 1 - c), me).wait_recv()
        for cp in first + passed:
            cp.wait_send()

    return pl.pallas_call(
        body, name="gather_small_grads", out_shape=jax.ShapeDtypeStruct((N_DEV * m_per, n), v.dtype),
        in_specs=[pl.BlockSpec(memory_space=pltpu.VMEM)], out_specs=pl.BlockSpec(memory_space=pltpu.VMEM),
        scratch_shapes=[pltpu.SemaphoreType.DMA((7,)), pltpu.SemaphoreType.DMA((7,))],
        compiler_params=pltpu.CompilerParams(vmem_limit_bytes=VMEM_LIMIT_BYTES),
    )(v)


def _sum_blocks(name, parts, out_dtype):
    rows, cols = parts[0].shape
    tm = _row_tile(rows, 512)

    def body(*refs):
        acc = refs[0][...].astype(F32)
        for r in refs[1:-1]:
            acc = acc + r[...].astype(F32)
        refs[-1][...] = acc.astype(refs[-1].dtype)

    spec = pl.BlockSpec((tm, cols), lambda i: (i, 0))
    return pl.pallas_call(
        body, name=name, grid=(rows // tm,), in_specs=[spec] * len(parts), out_specs=spec,
        out_shape=jax.ShapeDtypeStruct((rows, cols), out_dtype), compiler_params=_params("arbitrary"),
    )(*parts)


def _adamw_math(wv, gv, mv, vv):
    m2 = ADAM_B1 * mv + (1.0 - ADAM_B1) * gv
    v2 = ADAM_B2 * vv + (1.0 - ADAM_B2) * (gv * gv)
    delta = -ADAM_LR * ((m2 / (1.0 - ADAM_B1 ** ADAM_STEP)) / (jnp.sqrt(v2 / (1.0 - ADAM_B2 ** ADAM_STEP)) + ADAM_EPS)
                        + ADAM_WD * wv)
    return delta, m2, v2


def _adamw_small(ws, gs, ms, vs):
    n = len(ws)

    def body(*refs):
        for i in range(n):
            res = _adamw_math(refs[i][...], refs[n + i][...], refs[2 * n + i][...], refs[3 * n + i][...])
            for k in range(3):
                refs[(4 + k) * n + i][...] = res[k]

    vm = pl.BlockSpec(memory_space=pltpu.VMEM)
    outs = pl.pallas_call(
        body, name="adamw_small", in_specs=[vm] * (4 * n), out_specs=[vm] * (3 * n),
        out_shape=[jax.ShapeDtypeStruct(a.shape, F32) for a in ws] * 3,
        compiler_params=pltpu.CompilerParams(vmem_limit_bytes=VMEM_LIMIT_BYTES),
    )(*ws, *gs, *ms, *vs)
    return outs[:n], outs[n:2 * n], outs[2 * n:]


def _adamw(name, w, g, m, v):
    rows, cols = w.shape
    ins = [(a, cols, 0, "row") for a in (w, g, m, v)]
    return _rowwise(name, _adamw_math, ins, [(cols, cols, F32)] * 3, rows=rows, tm=_row_tile(rows, 256))


PACK_ROWS = 256


def _pack(flat_parts, dtype, lead=()):
    parts = [a.astype(dtype).reshape(lead + (-1,)) for a in flat_parts]
    n = sum(a.shape[-1] for a in parts)
    chunk = PACK_ROWS * LANES
    total = -(-n // chunk) * chunk
    if total > n:
        parts.append(jnp.zeros(lead + (total - n,), dtype))
    return jnp.concatenate(parts, axis=-1).reshape(lead + (total // LANES, LANES))


def _unpack(buf, shapes, lead=()):
    flat = buf.reshape(lead + (-1,))
    out, off = [], 0
    for shp in shapes:
        n = math.prod(shp)
        out.append(lax.slice_in_dim(flat, off, off + n, axis=len(lead)).reshape(lead + tuple(shp)))
        off += n
    return out


BIG = ("w_in", "w_glu", "w_pa", "w_pb", "w_out", "w_up", "w_down")
WEIGHTS = ("g_mix", "w_in", "s5_a_re", "s5_a_im", "s5_log_dt", "s5_b_re", "s5_b_im", "s5_c_re", "s5_c_im", "s5_d",
           "w_glu", "b_glu", "hg_lb_logits", "hg_norm_gain", "w_pa", "w_pb", "w_out", "g_ffn", "w_up", "w_conv",
           "b_conv", "w_down", "g_final")
SMALL = tuple(n for n in WEIGHTS if n not in BIG)
SMALL_PARTS = ("loss", "g_mix", "g_ffn", "g_final", "b_glu", "gain", "lbrow", "s5_d", "w_conv", "b_conv", "lam_re",
               "lam_im", "bb_re", "bb_im", "s5_c_re", "s5_c_im")


def _lower_bound(logits):
    return jnp.cumsum(jax.nn.softmax(logits, axis=0), axis=0)[0:1]


def kernel(x, g_mix, w_in, s5_a_re, s5_a_im, s5_log_dt, s5_b_re, s5_b_im, s5_c_re, s5_c_im, s5_d, w_glu, b_glu, hg_lb_logits, hg_norm_gain, w_pa, w_pb, w_out, g_ffn, w_up, w_conv, b_conv, w_down, g_final, loss_target, m_g_mix, m_w_in, m_s5_a_re, m_s5_a_im, m_s5_log_dt, m_s5_b_re, m_s5_b_im, m_s5_c_re, m_s5_c_im, m_s5_d, m_w_glu, m_b_glu, m_hg_lb_logits, m_hg_norm_gain, m_w_pa, m_w_pb, m_w_out, m_g_ffn, m_w_up, m_w_conv, m_b_conv, m_w_down, m_g_final, v_g_mix, v_w_in, v_s5_a_re, v_s5_a_im, v_s5_log_dt, v_s5_b_re, v_s5_b_im, v_s5_c_re, v_s5_c_im, v_s5_d, v_w_glu, v_b_glu, v_hg_lb_logits, v_hg_norm_gain, v_w_pa, v_w_pb, v_w_out, v_g_ffn, v_w_up, v_w_conv, v_b_conv, v_w_down, v_g_final):
    args = dict(locals())
    w = {n: args[n] for n in WEIGHTS}
    mom = {n: args["m_" + n] for n in WEIGHTS}
    var = {n: args["v_" + n] for n in WEIGHTS}
    nseq, seq, d = x.shape
    xi, yi = lax.axis_index("x"), lax.axis_index("y")
    chip = 2 * xi + yi

    shard = {n: w[n][0] for n in BIG}
    shard16 = [shard[n].astype(BF16) for n in BIG]
    got = _gather_weights(shard16, [w_conv[0]])
    full = {n: lax.dynamic_update_index_in_dim(g, s, chip, 0) for n, g, s in zip(BIG, got, shard16)}
    for n in ("w_glu", "w_out", "w_down"):
        full[n] = full[n].reshape(-1, full[n].shape[-1])
    conv_all = lax.dynamic_update_index_in_dim(got[-1], w_conv[0], chip, 0)
    conv_full = conv_all.transpose(1, 0, 2).reshape(CONV_W, -1)

    p = dict(full)
    p.update(g_mix=g_mix, g_ffn=g_ffn, g_final=g_final.reshape(1, -1), b_glu=b_glu, gain=hg_norm_gain, s5_d=s5_d,
             b_conv=b_conv, w_conv=conv_full, lbrow=_lower_bound(hg_lb_logits),
             s5_a_re=s5_a_re[0], s5_a_im=s5_a_im[0], s5_log_dt=s5_log_dt[0], s5_b_re=s5_b_re[0], s5_b_im=s5_b_im[0],
             s5_c_re=s5_c_re[0], s5_c_im=s5_c_im[0])

    dx, gbig, gsmall = _local_step(x.reshape(nseq * seq, d), loss_target.reshape(nseq * seq, d), p, nseq=nseq, seq=seq)

    ci = lax.axis_index("c")
    parts = [gbig[n].reshape((N_CHIPS, -1, gbig[n].shape[-1])) for n in BIG]
    pair = []
    for n, g, s in zip(BIG, parts, _swap_halves(parts)):
        rh, cols = s.shape[1], s.shape[2]
        own = lax.dynamic_slice_in_dim(g, ci * rh, rh, axis=1)
        both = _sum_blocks("grad_pair_sum_" + n, [own.reshape(-1, cols), s.reshape(-1, cols)], BF16)
        pair.append(both.reshape(N_CHIPS, rh, cols))
    halves = []
    for n, h, o in zip(BIG, pair, _scatter_to_chips(pair)):
        mine = lax.dynamic_index_in_dim(h, chip, axis=0, keepdims=False)
        halves.append(_sum_blocks("grad_chip_sum_" + n, [mine, o[0], o[1], o[2]], F32))
    grads = {}
    for n, own, s in zip(BIG, halves, _swap_sums(halves)):
        both = jnp.concatenate([own, own], axis=0)
        grads[n] = lax.dynamic_update_slice_in_dim(both, s, (1 - ci) * own.shape[0], axis=0)

    small_shapes = [gsmall[n].shape for n in SMALL_PARTS]
    vec = _pack([gsmall[n] for n in SMALL_PARTS], F32)
    gathered = _gather_all(vec)
    mrows = vec.shape[0]
    vsum = _sum_blocks("small_grad_sum", [gathered[i * mrows:(i + 1) * mrows] for i in range(N_DEV)], F32)
    sm = dict(zip(SMALL_PARTS, _unpack(vsum, small_shapes)))
    loss = sm["loss"][0, 0]

    _, disc_vjp = jax.vjp(_s5_discretize, p["s5_a_re"], p["s5_a_im"], p["s5_log_dt"], p["s5_b_re"], p["s5_b_im"])
    da_re, da_im, dlog_dt, db_re, db_im = disc_vjp((sm["lam_re"], sm["lam_im"], sm["bb_re"], sm["bb_im"]))
    _, lb_vjp = jax.vjp(_lower_bound, hg_lb_logits)
    (dlogits,) = lb_vjp(sm["lbrow"])
    fcols = w_conv.shape[-1]
    grads.update(
        g_mix=sm["g_mix"], g_ffn=sm["g_ffn"], g_final=sm["g_final"].reshape(-1), b_glu=sm["b_glu"],
        hg_norm_gain=sm["gain"], hg_lb_logits=dlogits, s5_d=sm["s5_d"], b_conv=sm["b_conv"],
        w_conv=lax.dynamic_slice_in_dim(sm["w_conv"], chip * fcols, fcols, axis=1),
        s5_a_re=da_re, s5_a_im=da_im, s5_log_dt=dlog_dt, s5_b_re=db_re, s5_b_im=db_im,
        s5_c_re=sm["s5_c_re"], s5_c_im=sm["s5_c_im"])
    grads = {n: grads[n].reshape(w[n].shape) for n in WEIGHTS}

    delta, new_m, new_v = {}, {}, {}
    for n in BIG:
        shp = shard[n].shape
        dl, m2, v2 = _adamw("adamw_" + n, shard[n], grads[n].reshape(shp), mom[n].reshape(shp), var[n].reshape(shp))
        delta[n], new_m[n], new_v[n] = dl, m2, v2
    def natural(a):
        return a.reshape(1, -1) if a.ndim == 1 else (a[0] if a.ndim > 2 else a)

    outs = _adamw_small(*[[natural(src[n]) for n in SMALL] for src in (w, grads, mom, var)])
    for dst, group in zip((delta, new_m, new_v), outs):
        dst.update(zip(SMALL, group))
    res = [loss, dx.reshape(x.shape)]
    for group in (grads, delta, new_m, new_v):
        res += [group[n].reshape(w[n].shape) for n in WEIGHTS]
    return tuple(res)
```

```python
import functools
import math

import jax
import jax.numpy as jnp
from jax import lax
from jax.experimental import pallas as pl
from jax.experimental.pallas import tpu as pltpu

F32 = jnp.float32
BF16 = jnp.bfloat16
MESH = pl.DeviceIdType.MESH

EPS = 1e-6
S5_GROUP = 16
S5_STATE = 64
S5_BLOCK_GROUPS = 8
HEAD = 128
CHUNK = 64
CONV_W = 3
LANES = 128
SUBLANES = 8
GATE_BLOCK = 512
VMEM_LIMIT_BYTES = 56 * 1024 * 1024

ADAM_LR = 0.001
ADAM_B1 = 0.9
ADAM_B2 = 0.999
ADAM_EPS = 1e-08
ADAM_WD = 0.01
ADAM_STEP = 10

N_CHIPS = 4
N_DEV = 8


def _params(*sem):
    return pltpu.CompilerParams(dimension_semantics=sem, vmem_limit_bytes=VMEM_LIMIT_BYTES)


def _row_tile(rows, cap):
    if rows <= cap:
        return rows
    for t in range(cap - cap % 8, 7, -8):
        if rows % t == 0:
            return t
    raise ValueError(f"no row tile for {rows}")


def _dot(a, b):
    return jnp.dot(a.astype(BF16), b.astype(BF16), preferred_element_type=F32)


def _dot_nt(a, b):
    return lax.dot_general(a.astype(BF16), b.astype(BF16), (((1,), (1,)), ((), ())), preferred_element_type=F32)


def _dot_tn(a, b):
    return lax.dot_general(a.astype(BF16), b.astype(BF16), (((0,), (0,)), ((), ())), preferred_element_type=F32)


def _sigmoid(x):
    return 1.0 / (1.0 + jnp.exp(-x))


_GELU_C = math.sqrt(2.0 / math.pi)


def _gelu(x):
    return 0.5 * x * (1.0 + jnp.tanh(_GELU_C * (x + 0.044715 * x * x * x)))


def _gelu_grad(x):
    th = jnp.tanh(_GELU_C * (x + 0.044715 * x * x * x))
    return 0.5 * (1.0 + th) + 0.5 * x * (1.0 - th * th) * _GELU_C * (1.0 + 3.0 * 0.044715 * x * x)


def _rowwise(name, fn, ins, outs, accs=(), *, rows, tm, ncol=1):
    n_in, n_out = len(ins), len(outs)

    def body(*refs):
        res = fn(*[r[...] for r in refs[:n_in]])
        for r, v in zip(refs[n_in:n_in + n_out], res[:n_out]):
            r[...] = v.astype(r.dtype)
        first = pl.program_id(1) == 0
        for r, v in zip(refs[n_in + n_out:], res[n_out:]):
            @pl.when(first)
            def _():
                r[...] = v

            @pl.when(jnp.logical_not(first))
            def _():
                r[...] += v

    in_specs = []
    for _, width, base, kind in ins:
        if kind == "row":
            in_specs.append(pl.BlockSpec((tm, width), lambda j, i, b=base: (i, b + j)))
        else:
            in_specs.append(pl.BlockSpec((1, width), lambda j, i, b=base: (0, b + j)))
    out_specs = [pl.BlockSpec((tm, width), lambda j, i: (i, j)) for _, width, _ in outs]
    out_specs += [pl.BlockSpec((1, width), lambda j, i: (0, j)) for _, width in accs]
    out_shape = [jax.ShapeDtypeStruct((rows, total), dt) for total, _, dt in outs]
    out_shape += [jax.ShapeDtypeStruct((1, total), F32) for total, _ in accs]
    return pl.pallas_call(
        body, name=name, grid=(ncol, rows // tm), in_specs=in_specs, out_specs=out_specs, out_shape=out_shape,
        compiler_params=_params("arbitrary", "arbitrary"),
    )(*[a for a, _, _, _ in ins])


def _mm(name, a, b, *, mode, grid, a_spec, b_spec, o_spec, out_shape, acc_shape, res=None, res_spec=None,
        pair_axis=None):
    nk = grid[2]
    dot = {"nn": _dot, "nt": _dot_nt, "tn": _dot_tn}[mode]
    a_list = list(a) if isinstance(a, tuple) else [a]
    b_list = list(b) if isinstance(b, tuple) else [b]
    na, nb = len(a_list), len(b_list)
    assert (pair_axis is None) == (na + nb == 2)
    direct = nk == 1 and pair_axis is None

    def body(*refs):
        a_refs, b_refs = refs[:na], refs[na:na + nb]
        r_ref = None if res is None else refs[na + nb]
        o_ref = refs[na + nb + (0 if res is None else 1)]

        def finish(v):
            if res is not None:
                v = v + r_ref[...]
            o_ref[...] = v.astype(o_ref.dtype)

        if direct:
            finish(dot(a_refs[0][...], b_refs[0][...]))
            return
        acc_ref = refs[-1]
        k = pl.program_id(2)

        @pl.when(k == 0)
        def _():
            acc_ref[...] = jnp.zeros_like(acc_ref)

        if pair_axis is None:
            acc_ref[...] += dot(a_refs[0][...], b_refs[0][...])
        else:
            lower = pl.program_id(pair_axis) < grid[pair_axis] // 2

            @pl.when(lower)
            def _():
                acc_ref[...] += dot(a_refs[0][...], b_refs[0][...])

            @pl.when(jnp.logical_not(lower))
            def _():
                acc_ref[...] += dot(a_refs[-1][...], b_refs[-1][...])

        @pl.when(k == nk - 1)
        def _():
            finish(acc_ref[...])

    operands = a_list + b_list + ([] if res is None else [res])
    in_specs = (list(a_spec) if na == 2 else [a_spec]) + (list(b_spec) if nb == 2 else [b_spec])
    in_specs += [] if res is None else [res_spec]
    return pl.pallas_call(
        body, name=name, grid=grid, in_specs=in_specs, out_specs=o_spec, out_shape=out_shape,
        scratch_shapes=[] if direct else [pltpu.VMEM(acc_shape, F32)],
        compiler_params=_params("arbitrary", "arbitrary", "arbitrary"),
    )(*operands)


MM_TILE_BUDGET_BYTES = 36 * 1024 * 1024
MM_TILE_CAP = 1024


def _mm_tile(t, row_bytes, fixed_bytes):
    cap = max(16, min(MM_TILE_CAP, (MM_TILE_BUDGET_BYTES - fixed_bytes) // row_bytes))
    return _row_tile(t, cap - cap % 16)


def _size(a):
    return jnp.dtype(a.dtype).itemsize


def _mm_fwd_cols(name, a, w3, out_dtype=F32):
    t, k = a.shape
    ns = w3.shape[2]
    tm = _mm_tile(t, 2 * k * _size(a) + 2 * ns * jnp.dtype(out_dtype).itemsize, 2 * k * ns * _size(w3))
    return _mm(name, a, w3, mode="nn", grid=(N_CHIPS, t // tm, 1),
               a_spec=pl.BlockSpec((tm, k), lambda j, i, kk: (i, 0)),
               b_spec=pl.BlockSpec((None, k, ns), lambda j, i, kk: (j, 0, 0)),
               o_spec=pl.BlockSpec((tm, ns), lambda j, i, kk: (i, j)),
               out_shape=jax.ShapeDtypeStruct((t, N_CHIPS * ns), out_dtype), acc_shape=(tm, ns))


def _mm_bwd_cols(name, d, w3, out_dtype=F32):
    pair = isinstance(d, tuple)
    t = d[0].shape[0] if pair else d.shape[0]
    k, ns = w3.shape[1], w3.shape[2]
    dsize = _size(d[0] if pair else d)
    tm = _mm_tile(t, (4 if pair else 2) * ns * dsize + 2 * k * jnp.dtype(out_dtype).itemsize + 4 * k,
                  2 * k * ns * _size(w3))
    half = N_CHIPS // 2
    if pair:
        a_spec = (pl.BlockSpec((tm, ns), lambda i, j, kk: (i, jnp.minimum(kk, half - 1))),
                  pl.BlockSpec((tm, ns), lambda i, j, kk: (i, jnp.maximum(kk - half, 0))))
    else:
        a_spec = pl.BlockSpec((tm, ns), lambda i, j, kk: (i, kk))
    return _mm(name, d, w3, mode="nt", grid=(t // tm, 1, N_CHIPS), a_spec=a_spec,
               b_spec=pl.BlockSpec((None, k, ns), lambda i, j, kk: (kk, 0, 0)),
               o_spec=pl.BlockSpec((tm, k), lambda i, j, kk: (i, 0)),
               out_shape=jax.ShapeDtypeStruct((t, k), out_dtype), acc_shape=(tm, k), pair_axis=2 if pair else None)


def _mm_wgrad_cols(name, a, d):
    pair = isinstance(d, tuple)
    t, k = a.shape
    ns = (2 * d[0].shape[1] if pair else d.shape[1]) // N_CHIPS
    dsize = _size(d[0] if pair else d)
    tk = _mm_tile(t, 2 * k * _size(a) + (4 if pair else 2) * ns * dsize, k * ns * (4 + 2 * 2))
    half = N_CHIPS // 2
    if pair:
        b_spec = (pl.BlockSpec((tk, ns), lambda j, i, kk: (jnp.where(j < half, kk, 0), jnp.minimum(j, half - 1))),
                  pl.BlockSpec((tk, ns), lambda j, i, kk: (jnp.where(j < half, 0, kk), jnp.maximum(j - half, 0))))
    else:
        b_spec = pl.BlockSpec((tk, ns), lambda j, i, kk: (kk, j))
    return _mm(name, a, d, mode="tn", grid=(N_CHIPS, 1, t // tk),
               a_spec=pl.BlockSpec((tk, k), lambda j, i, kk: (kk, 0)), b_spec=b_spec,
               o_spec=pl.BlockSpec((None, k, ns), lambda j, i, kk: (j, 0, 0)),
               out_shape=jax.ShapeDtypeStruct((N_CHIPS, k, ns), BF16), acc_shape=(k, ns),
               pair_axis=0 if pair else None)


MM_BLOCK_CAP = 1408


def _mm_fwd_rows(name, a, w, res=None, out_dtype=F32):
    t, k = a.shape
    n = w.shape[1]
    tk = k if k <= MM_BLOCK_CAP else MM_BLOCK_CAP
    assert k % tk == 0
    row_bytes = 2 * tk * _size(a) + 2 * n * jnp.dtype(out_dtype).itemsize + (0 if res is None else 2 * n * 4) + 4 * n
    tm = _mm_tile(t, row_bytes, 2 * tk * n * _size(w))
    return _mm(name, a, w, mode="nn", grid=(t // tm, 1, k // tk),
               a_spec=pl.BlockSpec((tm, tk), lambda i, j, kk: (i, kk)),
               b_spec=pl.BlockSpec((tk, n), lambda i, j, kk: (kk, 0)),
               o_spec=pl.BlockSpec((tm, n), lambda i, j, kk: (i, 0)),
               out_shape=jax.ShapeDtypeStruct((t, n), out_dtype), acc_shape=(tm, n),
               res=res, res_spec=None if res is None else pl.BlockSpec((tm, n), lambda i, j, kk: (i, 0)))


def _mm_bwd_rows(name, d, w, out_dtype=F32):
    t, n = d.shape
    k = w.shape[0]
    tn = k if k <= MM_BLOCK_CAP else MM_BLOCK_CAP
    assert k % tn == 0
    tm = _mm_tile(t, 2 * n * _size(d) + 2 * tn * jnp.dtype(out_dtype).itemsize, 2 * tn * n * _size(w))
    return _mm(name, d, w, mode="nt", grid=(t // tm, k // tn, 1),
               a_spec=pl.BlockSpec((tm, n), lambda i, j, kk: (i, 0)),
               b_spec=pl.BlockSpec((tn, n), lambda i, j, kk: (j, 0)),
               o_spec=pl.BlockSpec((tm, tn), lambda i, j, kk: (i, j)),
               out_shape=jax.ShapeDtypeStruct((t, k), out_dtype), acc_shape=(tm, tn))


def _mm_wgrad_rows(name, a, d):
    t, k = a.shape
    n = d.shape[1]
    nblk = next(b for b in (1, 2, 4) if (k // b) % LANES == 0 and k // b <= MM_BLOCK_CAP)
    ks = k // nblk
    tk = _mm_tile(t, 2 * ks * _size(a) + 2 * n * _size(d), ks * n * (4 + 2 * 2))
    return _mm(name, a, d, mode="tn", grid=(nblk, 1, t // tk),
               a_spec=pl.BlockSpec((tk, ks), lambda j, i, kk: (kk, j)),
               b_spec=pl.BlockSpec((tk, n), lambda j, i, kk: (kk, 0)),
               o_spec=pl.BlockSpec((ks, n), lambda j, i, kk: (j, 0)),
               out_shape=jax.ShapeDtypeStruct((k, n), BF16), acc_shape=(ks, n))


def _s5_discretize(a_re, a_im, log_dt, b_re, b_im):
    dt = jnp.exp(log_dt)[:, None]
    mag = jnp.exp(a_re * dt)
    ang = a_im * dt
    lb_re = mag * jnp.cos(ang)
    lb_im = mag * jnp.sin(ang)
    den = a_re * a_re + a_im * a_im
    n_re = lb_re - 1.0
    n_im = lb_im
    co_re = ((n_re * a_re + n_im * a_im) / den)[..., None]
    co_im = ((n_im * a_re - n_re * a_im) / den)[..., None]
    bb_re = co_re * b_re - co_im * b_im
    bb_im = co_re * b_im + co_im * b_re
    return lb_re, lb_im, bb_re, bb_im


def _s5_in_blocks(bb):
    g = bb.shape[0]
    nb = g // S5_BLOCK_GROUPS
    t = bb.reshape(nb, S5_BLOCK_GROUPS, S5_STATE, S5_GROUP).transpose(0, 1, 3, 2)
    eye = jnp.eye(S5_BLOCK_GROUPS, dtype=bb.dtype)
    full = t[:, :, :, None, :] * eye[None, :, None, :, None]
    return full.reshape(nb, S5_BLOCK_GROUPS * S5_GROUP, S5_BLOCK_GROUPS * S5_STATE)


def _s5_in_blocks_diag(blocks):
    nb = blocks.shape[0]
    t = blocks.reshape(nb, S5_BLOCK_GROUPS, S5_GROUP, S5_BLOCK_GROUPS, S5_STATE)
    d = jnp.einsum("bghgp->bghp", t)
    return d.transpose(0, 1, 3, 2).reshape(nb * S5_BLOCK_GROUPS, S5_STATE, S5_GROUP)


def _s5_out_blocks(c):
    g = c.shape[0]
    nb = g // S5_BLOCK_GROUPS
    t = c.reshape(nb, S5_BLOCK_GROUPS, S5_GROUP, S5_STATE).transpose(0, 1, 3, 2)
    eye = jnp.eye(S5_BLOCK_GROUPS, dtype=c.dtype)
    full = t[:, :, :, None, :] * eye[None, :, None, :, None]
    return full.reshape(nb, S5_BLOCK_GROUPS * S5_STATE, S5_BLOCK_GROUPS * S5_GROUP)


def _s5_out_blocks_diag(blocks):
    nb = blocks.shape[0]
    t = blocks.reshape(nb, S5_BLOCK_GROUPS, S5_STATE, S5_BLOCK_GROUPS, S5_GROUP)
    d = jnp.einsum("bgpgh->bgph", t)
    return d.transpose(0, 1, 3, 2).reshape(nb * S5_BLOCK_GROUPS, S5_GROUP, S5_STATE)


def _s5_scan_tables(lr, li, reverse):
    def cmul(a, b):
        return a[0] * b[0] - a[1] * b[1], a[0] * b[1] + a[1] * b[0]

    lam = (lr, -li) if reverse else (lr, li)
    pw = [lam]
    for _ in range(SUBLANES - 1):
        pw.append(cmul(pw[-1], lam))
    sub = jnp.arange(SUBLANES)[:, None]
    rows = []
    for s in (1, 2, 4):
        keep = (sub <= SUBLANES - 1 - s) if reverse else (sub >= s)
        rows.append(jnp.where(keep, pw[s - 1][0][None, :], 0.0))
        rows.append(jnp.where(keep, pw[s - 1][1][None, :], 0.0))
    order = list(range(SUBLANES - 1, -1, -1)) if reverse else list(range(SUBLANES))
    rows.append(jnp.stack([pw[i][0] for i in order]))
    rows.append(jnp.stack([pw[i][1] for i in order]))
    return jnp.concatenate(rows, axis=0)


def _s5_scan(vre_ref, vim_ref, coef_ref, seq, width, reverse, xre_ref=None, xim_ref=None):
    nt = seq // SUBLANES
    nl = width // LANES
    per = 2 if xre_ref is None else 4
    sub = lax.broadcasted_iota(jnp.int32, (SUBLANES, LANES), 0)

    def step(k, carry):
        kk = (nt - 1 - k) if reverse else k
        rows = pl.ds(pl.multiple_of(kk * SUBLANES, SUBLANES), SUBLANES)
        out = []
        for j in range(nl):
            lanes = slice(j * LANES, (j + 1) * LANES)
            co = [coef_ref[SUBLANES * q:SUBLANES * (q + 1), lanes] for q in range(8)]
            cr, ci = carry[per * j], carry[per * j + 1]
            vr = vre_ref[rows, lanes]
            vi = vim_ref[rows, lanes]
            for q, s in enumerate((1, 2, 4)):
                sh = SUBLANES - s if reverse else s
                rr = pltpu.roll(vr, sh, 0)
                ri = pltpu.roll(vi, sh, 0)
                ar, ai = co[2 * q], co[2 * q + 1]
                vr, vi = vr + ar * rr - ai * ri, vi + ar * ri + ai * rr
            edge = 0 if reverse else SUBLANES - 1
            cbr = jnp.broadcast_to(cr[edge:edge + 1, :], (SUBLANES, LANES))
            cbi = jnp.broadcast_to(ci[edge:edge + 1, :], (SUBLANES, LANES))
            pr, pi = co[6], co[7]
            vr, vi = vr + pr * cbr - pi * cbi, vi + pr * cbi + pi * cbr
            vre_ref[rows, lanes] = vr
            vim_ref[rows, lanes] = vi
            out += [vr, vi]
            if xre_ref is not None:
                nr = jnp.where(sub == SUBLANES - 1, cbr, pltpu.roll(vr, SUBLANES - 1, 0))
                ni = jnp.where(sub == SUBLANES - 1, cbi, pltpu.roll(vi, SUBLANES - 1, 0))
                xr = xre_ref[rows, lanes]
                xi = xim_ref[rows, lanes]
                out += [carry[per * j + 2] + nr * xr + ni * xi, carry[per * j + 3] + ni * xr - nr * xi]
        return tuple(out)

    zero = jnp.zeros((SUBLANES, LANES), F32)
    res = lax.fori_loop(0, nt, step, (zero,) * (per * nl))
    if xre_ref is None:
        return None
    return jnp.concatenate(
        [jnp.concatenate([jnp.sum(res[per * j + 2], axis=0, keepdims=True) for j in range(nl)], axis=1),
         jnp.concatenate([jnp.sum(res[per * j + 3], axis=0, keepdims=True) for j in range(nl)], axis=1)], axis=0)


def _s5_fwd(z, bre3, bim3, cre3, cim3, coef, dskip, *, nseq, seq):
    nb = bre3.shape[0]
    ch, ns = bre3.shape[1], bre3.shape[2]

    def body(za_ref, bre_ref, bim_ref, cre_ref, cim_ref, coef_ref, d_ref, y_ref, xre_ref, xim_ref):
        za = za_ref[...]
        xre_ref[...] = _dot(za, bre_ref[...])
        xim_ref[...] = _dot(za, bim_ref[...])
        _s5_scan(xre_ref, xim_ref, coef_ref, seq, ns, False)
        y_ref[...] = _dot(xre_ref[...], cre_ref[...]) - _dot(xim_ref[...], cim_ref[...]) + d_ref[...] * za

    blk3 = lambda r, c: pl.BlockSpec((None, r, c), lambda b, j: (j, 0, 0))
    return pl.pallas_call(
        body, name="s5_fwd", grid=(nseq, nb),
        in_specs=[pl.BlockSpec((seq, ch), lambda b, j: (b, j)), blk3(ch, ns), blk3(ch, ns), blk3(ns, ch), blk3(ns, ch),
                  pl.BlockSpec((8 * SUBLANES, ns), lambda b, j: (0, j)), pl.BlockSpec((1, ch), lambda b, j: (0, j))],
        out_specs=[pl.BlockSpec((seq, ch), lambda b, j: (b, j)), pl.BlockSpec((seq, ns), lambda b, j: (b, j)),
                   pl.BlockSpec((seq, ns), lambda b, j: (b, j))],
        out_shape=[jax.ShapeDtypeStruct((nseq * seq, nb * ch), F32), jax.ShapeDtypeStruct((nseq * seq, nb * ns), F32),
                   jax.ShapeDtypeStruct((nseq * seq, nb * ns), F32)],
        compiler_params=_params("arbitrary", "arbitrary"),
    )(z, bre3, bim3, cre3, cim3, coef, dskip)


def _s5_bwd(dy, z, xre, xim, bre3, bim3, cre3, cim3, coef_rev, dskip, *, nseq, seq):
    nb = bre3.shape[0]
    ch, ns = bre3.shape[1], bre3.shape[2]

    def body(dy_ref, za_ref, xre_ref, xim_ref, bre_ref, bim_ref, cre_ref, cim_ref, coef_ref, d_ref,
             dza_ref, dbre_ref, dbim_ref, dcre_ref, dcim_ref, dlam_ref, dd_ref, are_ref, aim_ref):
        dy = dy_ref[...]
        za = za_ref[...]
        are_ref[...] = _dot_nt(dy, cre_ref[...])
        aim_ref[...] = -_dot_nt(dy, cim_ref[...])
        dlam = _s5_scan(are_ref, aim_ref, coef_ref, seq, ns, True, xre_ref, xim_ref)
        are = are_ref[...]
        aim = aim_ref[...]
        dza_ref[...] = (_dot_nt(are, bre_ref[...]) + _dot_nt(aim, bim_ref[...]) + d_ref[...] * dy).astype(dza_ref.dtype)
        parts = (_dot_tn(za, are), _dot_tn(za, aim), _dot_tn(xre_ref[...], dy), -_dot_tn(xim_ref[...], dy),
                 dlam, jnp.sum(dy * za, axis=0, keepdims=True))
        first = pl.program_id(1) == 0
        for r, v in zip((dbre_ref, dbim_ref, dcre_ref, dcim_ref, dlam_ref, dd_ref), parts):
            @pl.when(first)
            def _():
                r[...] = v

            @pl.when(jnp.logical_not(first))
            def _():
                r[...] += v

    blk3 = lambda r, c: pl.BlockSpec((None, r, c), lambda j, b: (j, 0, 0))
    tok = lambda c: pl.BlockSpec((seq, c), lambda j, b: (b, j))
    return pl.pallas_call(
        body, name="s5_bwd", grid=(nb, nseq),
        in_specs=[tok(ch), tok(ch), tok(ns), tok(ns), blk3(ch, ns), blk3(ch, ns), blk3(ns, ch), blk3(ns, ch),
                  pl.BlockSpec((8 * SUBLANES, ns), lambda j, b: (0, j)), pl.BlockSpec((1, ch), lambda j, b: (0, j))],
        out_specs=[tok(ch), blk3(ch, ns), blk3(ch, ns), blk3(ns, ch), blk3(ns, ch),
                   pl.BlockSpec((None, 2, ns), lambda j, b: (j, 0, 0)), pl.BlockSpec((1, ch), lambda j, b: (0, j))],
        out_shape=[jax.ShapeDtypeStruct((nseq * seq, nb * ch), BF16),
                   jax.ShapeDtypeStruct((nb, ch, ns), F32), jax.ShapeDtypeStruct((nb, ch, ns), F32),
                   jax.ShapeDtypeStruct((nb, ns, ch), F32), jax.ShapeDtypeStruct((nb, ns, ch), F32),
                   jax.ShapeDtypeStruct((nb, 2, ns), F32), jax.ShapeDtypeStruct((1, nb * ch), F32)],
        scratch_shapes=[pltpu.VMEM((seq, ns), F32), pltpu.VMEM((seq, ns), F32)],
        compiler_params=_params("arbitrary", "arbitrary"),
    )(dy, z, xre, xim, bre3, bim3, cre3, cim3, coef_rev, dskip)


def _cumsum_rows(x, reverse=False):
    n = x.shape[0]
    row = lax.broadcasted_iota(jnp.int32, x.shape, 0)
    s = 1
    while s < n:
        if reverse:
            x = x + jnp.where(row < n - s, pltpu.roll(x, n - s, 0), 0.0)
        else:
            x = x + jnp.where(row >= s, pltpu.roll(x, s, 0), 0.0)
        s *= 2
    return x


def _hg_gates(zq, zf, lb):
    sg = _sigmoid(zf)
    f = lb + (1.0 - lb) * sg
    sq = _sigmoid(zq)
    qa = zq * sq * (HEAD ** -0.5)
    b = _cumsum_rows(jnp.log(f))
    return sg, f, sq, qa, 1.0 - f, b


SUB = 16


def _hg_scores(qa, kk, b):
    c = qa.shape[0]
    row = lax.broadcasted_iota(jnp.int32, qa.shape, 0)
    pos = jnp.bitwise_and(row, SUB - 1)
    dmat = lax.broadcasted_iota(jnp.int32, (c, c), 0) - lax.broadcasted_iota(jnp.int32, (c, c), 1)
    p = jnp.zeros((c, c), F32)
    for d in range(SUB):
        if d == 0:
            fd = qa * kk
        else:
            e = jnp.exp(jnp.minimum(b - pltpu.roll(b, d, 0), 0.0))
            fd = jnp.where(pos >= d, qa * pltpu.roll(kk, d, 0) * e, 0.0)
        p = jnp.where(dmat == d, jnp.sum(fd, axis=1, keepdims=True), p)
    col = lax.broadcasted_iota(jnp.int32, (SUB, c), 1)
    blocks = [jnp.zeros((SUB, c), F32)]
    for r0 in range(SUB, c, SUB):
        beta = b[r0 - 1:r0, :]
        qt = qa[r0:r0 + SUB] * jnp.exp(b[r0:r0 + SUB] - beta)
        kt = kk * jnp.exp(jnp.minimum(beta - b, 0.0))
        blocks.append(jnp.where(col < r0, _dot_nt(qt, kt), 0.0))
    return p + jnp.concatenate(blocks, axis=0)


def _hg_scores_bwd(dp, qa, kk, b):
    c = qa.shape[0]
    row = lax.broadcasted_iota(jnp.int32, qa.shape, 0)
    pos = jnp.bitwise_and(row, SUB - 1)
    dmat = lax.broadcasted_iota(jnp.int32, (c, c), 0) - lax.broadcasted_iota(jnp.int32, (c, c), 1)
    dqa = jnp.zeros_like(qa)
    dkk = jnp.zeros_like(qa)
    db = jnp.zeros_like(qa)
    for d in range(SUB):
        dcol = jnp.sum(jnp.where(dmat == d, dp, 0.0), axis=1, keepdims=True)
        if d == 0:
            dqa = dqa + dcol * kk
            dkk = dkk + dcol * qa
        else:
            e = jnp.exp(jnp.minimum(b - pltpu.roll(b, d, 0), 0.0))
            w = jnp.where(pos >= d, dcol * e, 0.0)
            kr = pltpu.roll(kk, d, 0)
            dqa = dqa + w * kr
            tmp = w * qa
            dkk = dkk + pltpu.roll(tmp, c - d, 0)
            x = tmp * kr
            db = db + x - pltpu.roll(x, c - d, 0)
    col = lax.broadcasted_iota(jnp.int32, (SUB, c), 1)
    dq_blocks = [jnp.zeros((SUB, qa.shape[1]), F32)]
    db_blocks = [jnp.zeros((SUB, qa.shape[1]), F32)]
    for r0 in range(SUB, c, SUB):
        beta = b[r0 - 1:r0, :]
        eq = jnp.exp(b[r0:r0 + SUB] - beta)
        ek = jnp.exp(jnp.minimum(beta - b, 0.0))
        qt = qa[r0:r0 + SUB] * eq
        kt = kk * ek
        dpi = jnp.where(col < r0, dp[r0:r0 + SUB, :], 0.0)
        dqt = _dot(dpi, kt)
        dkt = _dot_tn(dpi, qt)
        dq_blocks.append(dqt * eq)
        db_blocks.append(dqt * qt)
        dkk = dkk + dkt * ek
        db = db - dkt * kt
    return dqa + jnp.concatenate(dq_blocks, axis=0), dkk, db + jnp.concatenate(db_blocks, axis=0)


def _hg_chunks_per_step(seq):
    nc = seq // CHUNK
    cps = next(k for k in (4, 2, 1) if nc % k == 0)
    return nc, cps, nc // cps


def _hg_fwd(z, lbrow, gain, *, nseq, seq, heads, qoff):
    nc, cps, nblk = _hg_chunks_per_step(seq)
    blk = cps * CHUNK
    zspec = lambda off: pl.BlockSpec((blk, HEAD), lambda h, b, n, off=off: (b * nblk + n, off + h))

    def body(zq_ref, zf_ref, zi_ref, zg_ref, lb_ref, gn_ref, o_ref, yb_ref, st_ref, state):
        @pl.when(pl.program_id(2) == 0)
        def _():
            state[...] = jnp.zeros_like(state)

        lb = lb_ref[...]
        gain_v = gn_ref[...]

        def chunk(ci, carry):
            rows = pl.ds(pl.multiple_of(ci * CHUNK, CHUNK), CHUNK)
            st = state[...]
            st_ref[ci] = st
            zi = zi_ref[rows, :]
            zg = zg_ref[rows, :]
            _, _, _, qa, kk, b = _hg_gates(zq_ref[rows, :], zf_ref[rows, :], lb)
            o = _dot_nt(qa * jnp.exp(b), st) + _dot(_hg_scores(qa, kk, b), zi)
            bl = b[CHUNK - 1:CHUNK, :]
            state[...] = st * jnp.exp(bl) + _dot_tn(zi, kk * jnp.exp(bl - b))
            o_ref[rows, :] = o
            r = lax.rsqrt(jnp.mean(o * o, axis=1, keepdims=True) + EPS)
            yb_ref[rows, :] = (o * r * gain_v * zg * _sigmoid(zg)).astype(yb_ref.dtype)
            return carry

        lax.fori_loop(0, cps, chunk, 0)

    tok = pl.BlockSpec((blk, HEAD), lambda h, b, n: (b * nblk + n, h))
    vec = pl.BlockSpec((1, HEAD), lambda h, b, n: (0, h))
    rows = nseq * seq
    return pl.pallas_call(
        body, name="hgrn2_fwd", grid=(heads, nseq, nblk),
        in_specs=[zspec(qoff), zspec(qoff + heads), zspec(qoff + 2 * heads), zspec(qoff + 3 * heads), vec, vec],
        out_specs=[tok, tok, pl.BlockSpec((None, None, cps, HEAD, HEAD), lambda h, b, n: (h, b, n, 0, 0))],
        out_shape=[jax.ShapeDtypeStruct((rows, heads * HEAD), F32), jax.ShapeDtypeStruct((rows, heads * HEAD), BF16),
                   jax.ShapeDtypeStruct((heads, nseq, nc, HEAD, HEAD), F32)],
        scratch_shapes=[pltpu.VMEM((HEAD, HEAD), F32)],
        compiler_params=_params("arbitrary", "arbitrary", "arbitrary"),
    )(z, z, z, z, lbrow, gain)


def _hg_bwd(dyb, z, o, states, lbrow, gain, *, nseq, seq, heads, qoff):
    nc, cps, nblk = _hg_chunks_per_step(seq)
    blk = cps * CHUNK
    rev = lambda n: nblk - 1 - n
    zspec = lambda off: pl.BlockSpec((blk, HEAD), lambda h, b, n, off=off: (b * nblk + rev(n), off + h))

    def body(dyb_ref, zq_ref, zf_ref, zi_ref, zg_ref, o_ref, st_ref, lb_ref, gn_ref,
             dzq_ref, dzf_ref, dzi_ref, dzg_ref, dlb_ref, dgn_ref, dstate):
        @pl.when(pl.program_id(2) == 0)
        def _():
            dstate[...] = jnp.zeros_like(dstate)

        @pl.when(jnp.logical_and(pl.program_id(1) == 0, pl.program_id(2) == 0))
        def _():
            dlb_ref[...] = jnp.zeros_like(dlb_ref)
            dgn_ref[...] = jnp.zeros_like(dgn_ref)

        lb = lb_ref[...]
        gain_v = gn_ref[...]
        c = CHUNK
        causal = lax.broadcasted_iota(jnp.int32, (c, c), 0) >= lax.broadcasted_iota(jnp.int32, (c, c), 1)

        def chunk(step, carry):
            ci = cps - 1 - step
            rows = pl.ds(pl.multiple_of(ci * CHUNK, CHUNK), CHUNK)
            zq = zq_ref[rows, :]
            zi = zi_ref[rows, :]
            zg = zg_ref[rows, :]
            sg, f, sq, qa, kk, b = _hg_gates(zq, zf_ref[rows, :], lb)
            eb = jnp.exp(b)
            qt = qa * eb
            bl = b[c - 1:c, :]
            ebl = jnp.exp(bl)
            ekb = jnp.exp(bl - b)
            kh = kk * ekb
            st = st_ref[ci]
            dst = dstate[...]
            o = o_ref[rows, :]
            r = lax.rsqrt(jnp.mean(o * o, axis=1, keepdims=True) + EPS)
            oh = o * r
            szg = _sigmoid(zg)
            dyb = dyb_ref[rows, :]
            don = dyb * zg * szg
            dzg_ref[rows, :] = (dyb * oh * gain_v * szg * (1.0 + zg * (1.0 - szg))).astype(dzg_ref.dtype)
            doh = don * gain_v
            do = r * (doh - oh * jnp.mean(doh * oh, axis=1, keepdims=True))
            dqt = _dot(do, st)
            dp = jnp.where(causal, _dot_nt(do, zi), 0.0)
            p = _hg_scores(qa, kk, b)
            dzi_ref[rows, :] = (_dot_tn(p, do) + _dot_nt(kh, dst)).astype(dzi_ref.dtype)
            dkh = _dot(zi, dst)
            dbl = jnp.sum(dkh * kh, axis=0, keepdims=True) + jnp.sum(dst * st, axis=0, keepdims=True) * ebl
            dstate[...] = _dot_tn(do, qt) + dst * ebl
            dqa_s, dkk_s, db_s = _hg_scores_bwd(dp, qa, kk, b)
            dqa = dqt * eb + dqa_s
            dkk = dkh * ekb + dkk_s
            db = dqt * qt - dkh * kh + db_s
            row = lax.broadcasted_iota(jnp.int32, db.shape, 0)
            db = db + jnp.where(row == c - 1, dbl, 0.0)
            df = _cumsum_rows(db, reverse=True) / f - dkk
            dzf_ref[rows, :] = (df * (1.0 - lb) * sg * (1.0 - sg)).astype(dzf_ref.dtype)
            dzq_ref[rows, :] = (dqa * (HEAD ** -0.5) * sq * (1.0 + zq * (1.0 - sq))).astype(dzq_ref.dtype)
            dlb_ref[...] += jnp.sum(df * (1.0 - sg), axis=0, keepdims=True)
            dgn_ref[...] += jnp.sum(don * oh, axis=0, keepdims=True)
            return carry

        lax.fori_loop(0, cps, chunk, 0)

    tok = pl.BlockSpec((blk, HEAD), lambda h, b, n: (b * nblk + rev(n), h))
    vec = pl.BlockSpec((1, HEAD), lambda h, b, n: (0, h))
    rows = nseq * seq
    return pl.pallas_call(
        body, name="hgrn2_bwd", grid=(heads, nseq, nblk),
        in_specs=[tok, zspec(qoff), zspec(qoff + heads), zspec(qoff + 2 * heads), zspec(qoff + 3 * heads), tok,
                  pl.BlockSpec((None, None, cps, HEAD, HEAD), lambda h, b, n: (h, b, rev(n), 0, 0)), vec, vec],
        out_specs=[tok, tok, tok, tok, vec, vec],
        out_shape=[jax.ShapeDtypeStruct((rows, heads * HEAD), BF16)] * 4
        + [jax.ShapeDtypeStruct((1, heads * HEAD), F32)] * 2,
        scratch_shapes=[pltpu.VMEM((HEAD, HEAD), F32)],
        compiler_params=_params("arbitrary", "arbitrary", "arbitrary"),
    )(dyb, z, z, z, z, o, states, lbrow, gain)


def _conv_taps(h, w, bias):
    row = lax.broadcasted_iota(jnp.int32, h.shape, 0)
    h1 = jnp.where(row >= 1, pltpu.roll(h, 1, 0), 0.0)
    h2 = jnp.where(row >= 2, pltpu.roll(h, 2, 0), 0.0)
    return h2 * w[0:1, :] + h1 * w[1:2, :] + h * w[2:3, :] + bias, h1, h2


def _conv_fwd(h, wconv, bconv, *, nseq, seq):
    ff2 = h.shape[1]
    ncol = ff2 // 2 // LANES

    def body(hg_ref, hv_ref, wg_ref, wv_ref, bg_ref, bv_ref, a_ref):
        g, _, _ = _conv_taps(hg_ref[...], wg_ref[...], bg_ref[...])
        v, _, _ = _conv_taps(hv_ref[...], wv_ref[...], bv_ref[...])
        a_ref[...] = (g * _sigmoid(g) * v).astype(a_ref.dtype)

    tok = lambda off: pl.BlockSpec((seq, LANES), lambda j, b, off=off: (b, off + j))
    wsp = lambda off: pl.BlockSpec((CONV_W, LANES), lambda j, b, off=off: (0, off + j))
    bsp = lambda off: pl.BlockSpec((1, LANES), lambda j, b, off=off: (0, off + j))
    return pl.pallas_call(
        body, name="conv_fwd", grid=(ncol, nseq),
        in_specs=[tok(0), tok(ncol), wsp(0), wsp(ncol), bsp(0), bsp(ncol)],
        out_specs=tok(0), out_shape=jax.ShapeDtypeStruct((nseq * seq, ff2 // 2), BF16),
        compiler_params=_params("arbitrary", "arbitrary"),
    )(h, h, wconv, wconv, bconv, bconv)


def _conv_bwd(da, h, wconv, bconv, *, nseq, seq):
    ff2 = h.shape[1]
    ncol = ff2 // 2 // LANES

    def half_bwd(d, hcur, h1, h2, w):
        n = d.shape[0]
        row = lax.broadcasted_iota(jnp.int32, d.shape, 0)
        d1 = jnp.where(row < n - 1, pltpu.roll(d, n - 1, 0), 0.0)
        d2 = jnp.where(row < n - 2, pltpu.roll(d, n - 2, 0), 0.0)
        dh = d * w[2:3, :] + d1 * w[1:2, :] + d2 * w[0:1, :]
        stats = jnp.concatenate(
            [jnp.sum(h2 * d, axis=0, keepdims=True), jnp.sum(h1 * d, axis=0, keepdims=True),
             jnp.sum(hcur * d, axis=0, keepdims=True), jnp.sum(d, axis=0, keepdims=True),
             jnp.zeros((SUBLANES - 4, d.shape[1]), F32)], axis=0)
        return dh, stats

    def body(da_ref, hg_ref, hv_ref, wg_ref, wv_ref, bg_ref, bv_ref, dhg_ref, dhv_ref, sg_ref, sv_ref):
        hg = hg_ref[...]
        hv = hv_ref[...]
        wg = wg_ref[...]
        wv = wv_ref[...]
        g, g1, g2 = _conv_taps(hg, wg, bg_ref[...])
        v, v1, v2 = _conv_taps(hv, wv, bv_ref[...])
        da = da_ref[...]
        s = _sigmoid(g)
        dhg, stg = half_bwd(da * v * s * (1.0 + g * (1.0 - s)), hg, g1, g2, wg)
        dhv, stv = half_bwd(da * g * s, hv, v1, v2, wv)
        dhg_ref[...] = dhg.astype(dhg_ref.dtype)
        dhv_ref[...] = dhv.astype(dhv_ref.dtype)
        first = pl.program_id(1) == 0
        for r, val in ((sg_ref, stg), (sv_ref, stv)):
            @pl.when(first)
            def _():
                r[...] = val

            @pl.when(jnp.logical_not(first))
            def _():
                r[...] += val

    tok = lambda off: pl.BlockSpec((seq, LANES), lambda j, b, off=off: (b, off + j))
    wsp = lambda off: pl.BlockSpec((CONV_W, LANES), lambda j, b, off=off: (0, off + j))
    bsp = lambda off: pl.BlockSpec((1, LANES), lambda j, b, off=off: (0, off + j))
    ssp = pl.BlockSpec((SUBLANES, LANES), lambda j, b: (0, j))
    dhg, dhv, stg, stv = pl.pallas_call(
        body, name="conv_bwd", grid=(ncol, nseq),
        in_specs=[tok(0), tok(0), tok(ncol), wsp(0), wsp(ncol), bsp(0), bsp(ncol)],
        out_specs=[tok(0), tok(0), ssp, ssp],
        out_shape=[jax.ShapeDtypeStruct((nseq * seq, ff2 // 2), BF16)] * 2
        + [jax.ShapeDtypeStruct((SUBLANES, ff2 // 2), F32)] * 2,
        compiler_params=_params("arbitrary", "arbitrary"),
    )(da, h, h, wconv, wconv, bconv, bconv)
    return (dhg, dhv), jnp.concatenate([stg, stv], axis=1)


def _rms_fwd(xv, g):
    r = lax.rsqrt(jnp.mean(xv * xv, axis=1, keepdims=True) + EPS)
    return (xv * r * g,)


def _rms_bwd(xv, g, dy, res):
    r = lax.rsqrt(jnp.mean(xv * xv, axis=1, keepdims=True) + EPS)
    xh = xv * r
    dxh = dy * g
    dx = r * (dxh - xh * jnp.mean(dxh * xh, axis=1, keepdims=True)) + res
    return dx, jnp.sum(dy * xh, axis=0, keepdims=True)


def _loss_head(x2, tgt, g):
    d = x2.shape[1]
    r = lax.rsqrt(jnp.mean(x2 * x2, axis=1, keepdims=True) + EPS)
    xh = x2 * r
    err = xh * g - tgt
    dy = err * (1.0 / d)
    dxh = dy * g
    dx = r * (dxh - xh * jnp.mean(dxh * xh, axis=1, keepdims=True))
    loss = 0.5 * jnp.sum(jnp.mean(err * err, axis=1, keepdims=True), axis=0, keepdims=True)
    return dx, jnp.sum(dy * xh, axis=0, keepdims=True), jnp.broadcast_to(loss, (1, LANES))


def _local_step(x, tgt, p, *, nseq, seq):
    t, d = x.shape
    s5w = p["s5_d"].shape[1]
    hgw = p["gain"].shape[1]
    heads = hgw // HEAD
    qoff = s5w // LANES
    gblk = (s5w + 4 * hgw) // GATE_BLOCK
    ngb = d // GATE_BLOCK
    tm = _row_tile(t, 256)
    row = lambda a, w=None, base=0: (a, a.shape[1] if w is None else w, base, "row")
    vec = lambda a, w=None, base=0: (a, a.shape[1] if w is None else w, base, "vec")
    rw = functools.partial(_rowwise, rows=t, tm=tm)

    (u,) = rw("rms_mix", _rms_fwd, [row(x), vec(p["g_mix"])], [(d, d, BF16)])
    z = _mm_fwd_cols("in_proj", u, p["w_in"])

    lam_re, lam_im, bb_re, bb_im = _s5_discretize(p["s5_a_re"], p["s5_a_im"], p["s5_log_dt"], p["s5_b_re"], p["s5_b_im"])
    bre3 = _s5_in_blocks(bb_re).astype(BF16)
    bim3 = _s5_in_blocks(bb_im).astype(BF16)
    cre3 = _s5_out_blocks(p["s5_c_re"]).astype(BF16)
    cim3 = _s5_out_blocks(p["s5_c_im"]).astype(BF16)
    coef_f = _s5_scan_tables(lam_re.reshape(-1), lam_im.reshape(-1), False)
    coef_r = _s5_scan_tables(lam_re.reshape(-1), lam_im.reshape(-1), True)
    y5, xre, xim = _s5_fwd(z, bre3, bim3, cre3, cim3, coef_f, p["s5_d"], nseq=nseq, seq=seq)
    (ya0,) = rw("s5_gelu", lambda y: (_gelu(y),), [row(y5)], [(s5w, s5w, BF16)])
    gl = _mm_fwd_rows("glu_proj", ya0, p["w_glu"])
    (ya,) = rw("s5_glu", lambda y, g, b: (_gelu(y) * _sigmoid(g + b),), [row(y5), row(gl), vec(p["b_glu"])],
               [(s5w, s5w, BF16)])

    o, yb, states = _hg_fwd(z, p["lbrow"], p["gain"], nseq=nseq, seq=seq, heads=heads, qoff=qoff)

    pa = _mm_fwd_cols("proj_a", ya, p["w_pa"], out_dtype=BF16)
    pb = _mm_fwd_cols("proj_b", yb, p["w_pb"], out_dtype=BF16)
    gb = GATE_BLOCK
    (m,) = rw("merge", lambda ga, gbv, a, b: (_sigmoid(ga) * a + _sigmoid(gbv) * b,),
              [row(z, gb, gblk), row(z, gb, gblk + ngb), row(pa, gb), row(pb, gb)], [(d, gb, BF16)], ncol=ngb)
    x1 = _mm_fwd_rows("out_proj", m, p["w_out"], res=x)

    (u2,) = rw("rms_ffn", _rms_fwd, [row(x1), vec(p["g_ffn"])], [(d, d, BF16)])
    h = _mm_fwd_cols("up_proj", u2, p["w_up"])
    a = _conv_fwd(h, p["w_conv"], p["b_conv"], nseq=nseq, seq=seq)
    x2 = _mm_fwd_rows("down_proj", a, p["w_down"], res=x1)

    dx2, dg_final, lossv = rw("loss_head", _loss_head, [row(x2), row(tgt), vec(p["g_final"])], [(d, d, F32)],
                              accs=[(d, d), (LANES, LANES)])

    da = _mm_bwd_rows("down_bwd", dx2, p["w_down"])
    g_wdown = _mm_wgrad_rows("down_wgrad", a, dx2)
    dh, cstats = _conv_bwd(da, h, p["w_conv"], p["b_conv"], nseq=nseq, seq=seq)
    du2 = _mm_bwd_cols("up_bwd", dh, p["w_up"])
    g_wup = _mm_wgrad_cols("up_wgrad", u2, dh)
    dx1, dg_ffn = rw("rms_ffn_bwd", _rms_bwd, [row(x1), vec(p["g_ffn"]), row(du2), row(dx2)], [(d, d, F32)],
                     accs=[(d, d)])

    dm = _mm_bwd_rows("out_bwd", dx1, p["w_out"])
    g_wout = _mm_wgrad_rows("out_wgrad", m, dx1)

    def merge_bwd(ga, gbv, av, bv, dmv):
        sa = _sigmoid(ga)
        sb = _sigmoid(gbv)
        return dmv * sa, dmv * sb, dmv * av * sa * (1.0 - sa), dmv * bv * sb * (1.0 - sb)

    dpa, dpb, dzga, dzgb = rw("merge_bwd", merge_bwd,
                              [row(z, gb, gblk), row(z, gb, gblk + ngb), row(pa, gb), row(pb, gb), row(dm, gb)],
                              [(d, gb, BF16)] * 4, ncol=ngb)
    dya = _mm_bwd_cols("proj_a_bwd", dpa, p["w_pa"])
    g_wpa = _mm_wgrad_cols("proj_a_wgrad", ya, dpa)
    dyb = _mm_bwd_cols("proj_b_bwd", dpb, p["w_pb"])
    g_wpb = _mm_wgrad_cols("proj_b_wgrad", yb, dpb)

    def glu_bwd1(y, g, b, dyv):
        s = _sigmoid(g + b)
        dgl = dyv * _gelu(y) * s * (1.0 - s)
        return dgl, jnp.sum(dgl, axis=0, keepdims=True)

    dgl, db_glu = rw("s5_glu_bwd", glu_bwd1, [row(y5), row(gl), vec(p["b_glu"]), row(dya)], [(s5w, s5w, BF16)],
                     accs=[(s5w, s5w)])
    dgl_in = _mm_bwd_rows("glu_bwd", dgl, p["w_glu"])
    g_wglu = _mm_wgrad_rows("glu_wgrad", ya0, dgl)
    (dy5,) = rw("s5_gelu_bwd", lambda y, g, b, dyv, tv: ((dyv * _sigmoid(g + b) + tv) * _gelu_grad(y),),
                [row(y5), row(gl), vec(p["b_glu"]), row(dya), row(dgl_in)], [(s5w, s5w, F32)])
    dza, dbre3, dbim3, dcre3, dcim3, dlam, dd = _s5_bwd(dy5, z, xre, xim, bre3, bim3, cre3, cim3, coef_r, p["s5_d"],
                                                        nseq=nseq, seq=seq)

    dzq, dzf, dzi, dzg, dlb, dgain = _hg_bwd(dyb, z, o, states, p["lbrow"], p["gain"], nseq=nseq, seq=seq,
                                             heads=heads, qoff=qoff)

    dz = jnp.concatenate([dza, dzq, dzf, dzi, dzg, dzga, dzgb], axis=1)
    du = _mm_bwd_cols("in_bwd", dz, p["w_in"])
    g_win = _mm_wgrad_cols("in_wgrad", u, dz)
    dx, dg_mix = rw("rms_mix_bwd", _rms_bwd, [row(x), vec(p["g_mix"]), row(du), row(dx1)], [(d, d, F32)],
                    accs=[(d, d)])

    gshape = lam_re.shape
    big = {"w_in": g_win, "w_glu": g_wglu, "w_pa": g_wpa, "w_pb": g_wpb, "w_out": g_wout, "w_up": g_wup,
           "w_down": g_wdown}
    small = {
        "loss": lossv, "g_mix": dg_mix, "g_ffn": dg_ffn, "g_final": dg_final, "b_glu": db_glu, "gain": dgain,
        "lbrow": dlb, "s5_d": dd, "w_conv": cstats[0:CONV_W], "b_conv": cstats[CONV_W:CONV_W + 1],
        "lam_re": dlam[:, 0, :].reshape(gshape), "lam_im": dlam[:, 1, :].reshape(gshape),
        "bb_re": _s5_in_blocks_diag(dbre3), "bb_im": _s5_in_blocks_diag(dbim3),
        "s5_c_re": _s5_out_blocks_diag(dcre3), "s5_c_im": _s5_out_blocks_diag(dcim3),
    }
    return dx, big, small


ANY = pl.BlockSpec(memory_space=pl.ANY)


def _place():
    x, y, c = lax.axis_index("x"), lax.axis_index("y"), lax.axis_index("c")
    chips = [(1 - x, y), (x, 1 - y), (1 - x, 1 - y)]
    return x, y, c, chips


def _remote(src, dst, send_sems, recv_sems, k, to):
    return pltpu.make_async_remote_copy(src_ref=src, dst_ref=dst, send_sem=send_sems.at[k], recv_sem=recv_sems.at[k],
                                        device_id=to, device_id_type=MESH)


def _half(rows, which):
    return pl.ds(pl.multiple_of(which * (rows // 2), 16), rows // 2)


def _gather_weights(shards, whole):
    n, nw = len(shards), len(whole)
    arrays = list(shards) + list(whole)

    def body(*refs):
        in_refs, out_refs = refs[:n + nw], refs[n + nw:2 * (n + nw)]
        send_sems, recv_sems = refs[2 * (n + nw):]
        x, y, c, chips = _place()
        me = 2 * x + y
        copy = functools.partial(_remote, send_sems=send_sems, recv_sems=recv_sems)
        sends = []
        for a in range(n):
            mine_half = _half(arrays[a].shape[0], c)
            for j, (cx, cy) in enumerate(chips):
                sends.append(copy(in_refs[a].at[mine_half], out_refs[a].at[me, mine_half], k=6 * a + j, to=(cx, cy, c)))
        for a in range(n, n + nw):
            for j, (cx, cy) in enumerate(chips):
                sends.append(copy(in_refs[a], out_refs[a].at[me], k=6 * n + 3 * (a - n) + j, to=(cx, cy, c)))
        for cp in sends:
            cp.start()
        for a in range(n):
            mine_half = _half(arrays[a].shape[0], c)
            for j, (cx, cy) in enumerate(chips):
                landed = out_refs[a].at[2 * cx + cy, mine_half]
                copy(landed, landed, k=6 * a + j, to=(x, y, c)).wait_recv()
                fwd = copy(landed, landed, k=6 * a + 3 + j, to=(x, y, 1 - c))
                fwd.start()
                sends.append(fwd)
        for a in range(n):
            other_half = _half(arrays[a].shape[0], 1 - c)
            for j, (cx, cy) in enumerate(chips):
                landed = out_refs[a].at[2 * cx + cy, other_half]
                copy(landed, landed, k=6 * a + 3 + j, to=(x, y, c)).wait_recv()
        for a in range(n, n + nw):
            for j, (cx, cy) in enumerate(chips):
                landed = out_refs[a].at[2 * cx + cy]
                copy(landed, landed, k=6 * n + 3 * (a - n) + j, to=(x, y, c)).wait_recv()
        for cp in sends:
            cp.wait_send()

    nsem = 6 * n + 3 * nw
    return pl.pallas_call(
        body, name="gather_weights", out_shape=[jax.ShapeDtypeStruct((N_CHIPS,) + a.shape, a.dtype) for a in arrays],
        in_specs=[ANY] * (n + nw), out_specs=[ANY] * (n + nw),
        scratch_shapes=[pltpu.SemaphoreType.DMA((nsem,)), pltpu.SemaphoreType.DMA((nsem,))],
    )(*arrays)


def _swap_halves(parts):
    n = len(parts)

    def body(*refs):
        in_refs, out_refs, send_sems, recv_sems = refs[:n], refs[n:2 * n], refs[2 * n], refs[2 * n + 1]
        x, y, c, _ = _place()
        copies = [_remote(in_refs[a].at[:, _half(parts[a].shape[1], 1 - c), :], out_refs[a], send_sems, recv_sems, a,
                          (x, y, 1 - c)) for a in range(n)]
        for cp in copies:
            cp.start()
        for cp in copies:
            cp.wait()

    return pl.pallas_call(
        body, name="grad_swap_halves",
        out_shape=[jax.ShapeDtypeStruct((g.shape[0], g.shape[1] // 2, g.shape[2]), g.dtype) for g in parts],
        in_specs=[ANY] * n, out_specs=[ANY] * n,
        scratch_shapes=[pltpu.SemaphoreType.DMA((n,)), pltpu.SemaphoreType.DMA((n,))],
    )(*parts)


def _scatter_to_chips(parts):
    n = len(parts)

    def body(*refs):
        in_refs, out_refs, send_sems, recv_sems = refs[:n], refs[n:2 * n], refs[2 * n], refs[2 * n + 1]
        x, y, c, chips = _place()
        copies = [_remote(in_refs[a].at[2 * cx + cy], out_refs[a].at[j], send_sems, recv_sems, 3 * a + j, (cx, cy, c))
                  for a in range(n) for j, (cx, cy) in enumerate(chips)]
        for cp in copies:
            cp.start()
        for cp in copies:
            cp.wait()

    return pl.pallas_call(
        body, name="grad_scatter_chips",
        out_shape=[jax.ShapeDtypeStruct((N_CHIPS - 1,) + h.shape[1:], h.dtype) for h in parts],
        in_specs=[ANY] * n, out_specs=[ANY] * n,
        scratch_shapes=[pltpu.SemaphoreType.DMA((3 * n,)), pltpu.SemaphoreType.DMA((3 * n,))],
    )(*parts)


def _swap_sums(parts):
    n = len(parts)

    def body(*refs):
        in_refs, out_refs, send_sems, recv_sems = refs[:n], refs[n:2 * n], refs[2 * n], refs[2 * n + 1]
        x, y, c, _ = _place()
        copies = [_remote(in_refs[a], out_refs[a], send_sems, recv_sems, a, (x, y, 1 - c)) for a in range(n)]
        for cp in copies:
            cp.start()
        for cp in copies:
            cp.wait()

    return pl.pallas_call(
        body, name="grad_swap_sums", out_shape=[jax.ShapeDtypeStruct(g.shape, g.dtype) for g in parts],
        in_specs=[ANY] * n, out_specs=[ANY] * n,
        scratch_shapes=[pltpu.SemaphoreType.DMA((n,)), pltpu.SemaphoreType.DMA((n,))],
    )(*parts)


def _gather_all(v):
    m_per, n = v.shape

    def body(x_ref, out_ref, send_sems, recv_sems):
        x, y, c, chips = _place()
        me, sibling = (x, y, c), (x, y, 1 - c)

        def rows(px, py, pc):
            return out_ref.at[pl.ds(pl.multiple_of((4 * px + 2 * py + pc) * m_per, 8), m_per), :]

        def copy(k, block, to, src=None):
            return pltpu.make_async_remote_copy(src_ref=rows(*block) if src is None else src, dst_ref=rows(*block),
                                                send_sem=send_sems.at[k], recv_sem=recv_sems.at[k], device_id=to,
                                                device_id_type=MESH)

        out_ref[pl.ds(pl.multiple_of((4 * x + 2 * y + c) * m_per, 8), m_per), :] = x_ref[...]
        first = [copy(0, me, sibling, src=x_ref)]
        first += [copy(1 + j, me, (*chip, c), src=x_ref) for j, chip in enumerate(chips)]
        for cp in first:
            cp.start()
        passed = [copy(4 + j, (*chip, c), sibling) for j, chip in enumerate(chips)]
        for j, chip in enumerate(chips):
            copy(1 + j, (*chip, c), me).wait_recv()
            passed[j].start()
        copy(0, sibling, me).wait_recv()
        for j, chip in enumerate(chips):
            copy(4 + j, (*chip, 1 - c), me).wait_recv()
        for cp in first + passed:
            cp.wait_send()

    return pl.pallas_call(
        body, name="gather_small_grads", out_shape=jax.ShapeDtypeStruct((N_DEV * m_per, n), v.dtype),
        in_specs=[pl.BlockSpec(memory_space=pltpu.VMEM)], out_specs=pl.BlockSpec(memory_space=pltpu.VMEM),
        scratch_shapes=[pltpu.SemaphoreType.DMA((7,)), pltpu.SemaphoreType.DMA((7,))],
        compiler_params=pltpu.CompilerParams(vmem_limit_bytes=VMEM_LIMIT_BYTES),
    )(v)


def _sum_blocks(name, parts, out_dtype):
    rows, cols = parts[0].shape
    tm = _row_tile(rows, 512)

    def body(*refs):
        acc = refs[0][...].astype(F32)
        for r in refs[1:-1]:
            acc = acc + r[...].astype(F32)
        refs[-1][...] = acc.astype(refs[-1].dtype)

    spec = pl.BlockSpec((tm, cols), lambda i: (i, 0))
    return pl.pallas_call(
        body, name=name, grid=(rows // tm,), in_specs=[spec] * len(parts), out_specs=spec,
        out_shape=jax.ShapeDtypeStruct((rows, cols), out_dtype), compiler_params=_params("arbitrary"),
    )(*parts)


def _adamw_math(wv, gv, mv, vv):
    m2 = ADAM_B1 * mv + (1.0 - ADAM_B1) * gv
    v2 = ADAM_B2 * vv + (1.0 - ADAM_B2) * (gv * gv)
    delta = -ADAM_LR * ((m2 / (1.0 - ADAM_B1 ** ADAM_STEP)) / (jnp.sqrt(v2 / (1.0 - ADAM_B2 ** ADAM_STEP)) + ADAM_EPS)
                        + ADAM_WD * wv)
    return delta, m2, v2


def _adamw_small(ws, gs, ms, vs):
    n = len(ws)

    def body(*refs):
        for i in range(n):
            res = _adamw_math(refs[i][...], refs[n + i][...], refs[2 * n + i][...], refs[3 * n + i][...])
            for k in range(3):
                refs[(4 + k) * n + i][...] = res[k]

    vm = pl.BlockSpec(memory_space=pltpu.VMEM)
    outs = pl.pallas_call(
        body, name="adamw_small", in_specs=[vm] * (4 * n), out_specs=[vm] * (3 * n),
        out_shape=[jax.ShapeDtypeStruct(a.shape, F32) for a in ws] * 3,
        compiler_params=pltpu.CompilerParams(vmem_limit_bytes=VMEM_LIMIT_BYTES),
    )(*ws, *gs, *ms, *vs)
    return outs[:n], outs[n:2 * n], outs[2 * n:]


def _adamw(name, w, g, m, v):
    rows, cols = w.shape
    ins = [(a, cols, 0, "row") for a in (w, g, m, v)]
    return _rowwise(name, _adamw_math, ins, [(cols, cols, F32)] * 3, rows=rows, tm=_row_tile(rows, 256))


PACK_ROWS = 256


def _pack(flat_parts, dtype, lead=()):
    parts = [a.astype(dtype).reshape(lead + (-1,)) for a in flat_parts]
    n = sum(a.shape[-1] for a in parts)
    chunk = PACK_ROWS * LANES
    total = -(-n // chunk) * chunk
    if total > n:
        parts.append(jnp.zeros(lead + (total - n,), dtype))
    return jnp.concatenate(parts, axis=-1).reshape(lead + (total // LANES, LANES))


def _unpack(buf, shapes, lead=()):
    flat = buf.reshape(lead + (-1,))
    out, off = [], 0
    for shp in shapes:
        n = math.prod(shp)
        out.append(lax.slice_in_dim(flat, off, off + n, axis=len(lead)).reshape(lead + tuple(shp)))
        off += n
    return out


BIG = ("w_in", "w_glu", "w_pa", "w_pb", "w_out", "w_up", "w_down")
WEIGHTS = ("g_mix", "w_in", "s5_a_re", "s5_a_im", "s5_log_dt", "s5_b_re", "s5_b_im", "s5_c_re", "s5_c_im", "s5_d",
           "w_glu", "b_glu", "hg_lb_logits", "hg_norm_gain", "w_pa", "w_pb", "w_out", "g_ffn", "w_up", "w_conv",
           "b_conv", "w_down", "g_final")
SMALL = tuple(n for n in WEIGHTS if n not in BIG)
SMALL_PARTS = ("loss", "g_mix", "g_ffn", "g_final", "b_glu", "gain", "lbrow", "s5_d", "w_conv", "b_conv", "lam_re",
               "lam_im", "bb_re", "bb_im", "s5_c_re", "s5_c_im")


def _lower_bound(logits):
    return jnp.cumsum(jax.nn.softmax(logits, axis=0), axis=0)[0:1]


def kernel(x, g_mix, w_in, s5_a_re, s5_a_im, s5_log_dt, s5_b_re, s5_b_im, s5_c_re, s5_c_im, s5_d, w_glu, b_glu, hg_lb_logits, hg_norm_gain, w_pa, w_pb, w_out, g_ffn, w_up, w_conv, b_conv, w_down, g_final, loss_target, m_g_mix, m_w_in, m_s5_a_re, m_s5_a_im, m_s5_log_dt, m_s5_b_re, m_s5_b_im, m_s5_c_re, m_s5_c_im, m_s5_d, m_w_glu, m_b_glu, m_hg_lb_logits, m_hg_norm_gain, m_w_pa, m_w_pb, m_w_out, m_g_ffn, m_w_up, m_w_conv, m_b_conv, m_w_down, m_g_final, v_g_mix, v_w_in, v_s5_a_re, v_s5_a_im, v_s5_log_dt, v_s5_b_re, v_s5_b_im, v_s5_c_re, v_s5_c_im, v_s5_d, v_w_glu, v_b_glu, v_hg_lb_logits, v_hg_norm_gain, v_w_pa, v_w_pb, v_w_out, v_g_ffn, v_w_up, v_w_conv, v_b_conv, v_w_down, v_g_final):
    args = dict(locals())
    w = {n: args[n] for n in WEIGHTS}
    mom = {n: args["m_" + n] for n in WEIGHTS}
    var = {n: args["v_" + n] for n in WEIGHTS}
    nseq, seq, d = x.shape
    xi, yi = lax.axis_index("x"), lax.axis_index("y")
    chip = 2 * xi + yi

    shard = {n: w[n][0] for n in BIG}
    shard16 = [shard[n].astype(BF16) for n in BIG]
    got = _gather_weights(shard16, [w_conv[0]])
    full = {n: lax.dynamic_update_index_in_dim(g, s, chip, 0) for n, g, s in zip(BIG, got, shard16)}
    for n in ("w_glu", "w_out", "w_down"):
        full[n] = full[n].reshape(-1, full[n].shape[-1])
    conv_all = lax.dynamic_update_index_in_dim(got[-1], w_conv[0], chip, 0)
    conv_full = conv_all.transpose(1, 0, 2).reshape(CONV_W, -1)

    p = dict(full)
    p.update(g_mix=g_mix, g_ffn=g_ffn, g_final=g_final.reshape(1, -1), b_glu=b_glu, gain=hg_norm_gain, s5_d=s5_d,
             b_conv=b_conv, w_conv=conv_full, lbrow=_lower_bound(hg_lb_logits),
             s5_a_re=s5_a_re[0], s5_a_im=s5_a_im[0], s5_log_dt=s5_log_dt[0], s5_b_re=s5_b_re[0], s5_b_im=s5_b_im[0],
             s5_c_re=s5_c_re[0], s5_c_im=s5_c_im[0])

    dx, gbig, gsmall = _local_step(x.reshape(nseq * seq, d), loss_target.reshape(nseq * seq, d), p, nseq=nseq, seq=seq)

    ci = lax.axis_index("c")
    parts = [gbig[n].reshape((N_CHIPS, -1, gbig[n].shape[-1])) for n in BIG]
    pair = []
    for n, g, s in zip(BIG, parts, _swap_halves(parts)):
        rh, cols = s.shape[1], s.shape[2]
        own = lax.dynamic_slice_in_dim(g, ci * rh, rh, axis=1)
        both = _sum_blocks("grad_pair_sum_" + n, [own.reshape(-1, cols), s.reshape(-1, cols)], BF16)
        pair.append(both.reshape(N_CHIPS, rh, cols))
    halves = []
    for n, h, o in zip(BIG, pair, _scatter_to_chips(pair)):
        mine = lax.dynamic_index_in_dim(h, chip, axis=0, keepdims=False)
        halves.append(_sum_blocks("grad_chip_sum_" + n, [mine, o[0], o[1], o[2]], F32))
    grads = {}
    for n, own, s in zip(BIG, halves, _swap_sums(halves)):
        both = jnp.concatenate([own, own], axis=0)
        grads[n] = lax.dynamic_update_slice_in_dim(both, s, (1 - ci) * own.shape[0], axis=0)

    small_shapes = [gsmall[n].shape for n in SMALL_PARTS]
    vec = _pack([gsmall[n] for n in SMALL_PARTS], F32)
    gathered = _gather_all(vec)
    mrows = vec.shape[0]
    vsum = _sum_blocks("small_grad_sum", [gathered[i * mrows:(i + 1) * mrows] for i in range(N_DEV)], F32)
    sm = dict(zip(SMALL_PARTS, _unpack(vsum, small_shapes)))
    loss = sm["loss"][0, 0]

    _, disc_vjp = jax.vjp(_s5_discretize, p["s5_a_re"], p["s5_a_im"], p["s5_log_dt"], p["s5_b_re"], p["s5_b_im"])
    da_re, da_im, dlog_dt, db_re, db_im = disc_vjp((sm["lam_re"], sm["lam_im"], sm["bb_re"], sm["bb_im"]))
    _, lb_vjp = jax.vjp(_lower_bound, hg_lb_logits)
    (dlogits,) = lb_vjp(sm["lbrow"])
    fcols = w_conv.shape[-1]
    grads.update(
        g_mix=sm["g_mix"], g_ffn=sm["g_ffn"], g_final=sm["g_final"].reshape(-1), b_glu=sm["b_glu"],
        hg_norm_gain=sm["gain"], hg_lb_logits=dlogits, s5_d=sm["s5_d"], b_conv=sm["b_conv"],
        w_conv=lax.dynamic_slice_in_dim(sm["w_conv"], chip * fcols, fcols, axis=1),
        s5_a_re=da_re, s5_a_im=da_im, s5_log_dt=dlog_dt, s5_b_re=db_re, s5_b_im=db_im,
        s5_c_re=sm["s5_c_re"], s5_c_im=sm["s5_c_im"])
    grads = {n: grads[n].reshape(w[n].shape) for n in WEIGHTS}

    delta, new_m, new_v = {}, {}, {}
    for n in BIG:
        shp = shard[n].shape
        dl, m2, v2 = _adamw("adamw_" + n, shard[n], grads[n].reshape(shp), mom[n].reshape(shp), var[n].reshape(shp))
        delta[n], new_m[n], new_v[n] = dl, m2, v2
    def natural(a):
        return a.reshape(1, -1) if a.ndim == 1 else (a[0] if a.ndim > 2 else a)

    outs = _adamw_small(*[[natural(src[n]) for n in SMALL] for src in (w, grads, mom, var)])
    for dst, group in zip((delta, new_m, new_v), outs):
        dst.update(zip(SMALL, group))
    res = [loss, dx.reshape(x.shape)]
    for group in (grads, delta, new_m, new_v):
        res += [group[n].reshape(w[n].shape) for n in WEIGHTS]
    return tuple(res)
```

```python
import functools
import math

import jax
import jax.numpy as jnp
from jax import lax
from jax.experimental import pallas as pl
from jax.experimental.pallas import tpu as pltpu

F32 = jnp.float32
BF16 = jnp.bfloat16
MESH = pl.DeviceIdType.MESH

EPS = 1e-6
S5_GROUP = 16
S5_STATE = 64
S5_BLOCK_GROUPS = 8
HEAD = 128
CHUNK = 64
CONV_W = 3
LANES = 128
SUBLANES = 8
GATE_BLOCK = 512
VMEM_LIMIT_BYTES = 56 * 1024 * 1024

ADAM_LR = 0.001
ADAM_B1 = 0.9
ADAM_B2 = 0.999
ADAM_EPS = 1e-08
ADAM_WD = 0.01
ADAM_STEP = 10

N_CHIPS = 4
N_DEV = 8


def _params(*sem):
    return pltpu.CompilerParams(dimension_semantics=sem, vmem_limit_bytes=VMEM_LIMIT_BYTES)


class _Rider:
    def __init__(self, arrays, out_shapes, nsem, start, finish, aliases=None):
        self.arrays, self.out_shapes, self.nsem = list(arrays), list(out_shapes), nsem
        self.start, self.finish, self.aliases = start, finish, dict(aliases or {})


def _hosted_call(name, body, *, grid, in_specs, out_specs, out_shape, operands, scratch_shapes=(), rider=None):
    in_specs, out_specs, out_shape, scratch_shapes = list(in_specs), list(out_specs), list(out_shape), list(scratch_shapes)
    cparams = _params(*(["arbitrary"] * len(grid)))
    if rider is None:
        return pl.pallas_call(body, name=name, grid=grid, in_specs=in_specs, out_specs=out_specs, out_shape=out_shape,
                              scratch_shapes=scratch_shapes, compiler_params=cparams)(*operands)
    n_in, n_out, n_sc = len(in_specs), len(out_specs), len(scratch_shapes)
    r_in, r_out = len(rider.arrays), len(rider.out_shapes)

    def hosted(*refs):
        ins, rins = refs[:n_in], refs[n_in:n_in + r_in]
        outs = refs[n_in + r_in:n_in + r_in + n_out]
        routs = refs[n_in + r_in + n_out:n_in + r_in + n_out + r_out]
        rest = refs[n_in + r_in + n_out + r_out:]
        send_sems, recv_sems = rest[n_sc], rest[n_sc + 1]
        first = functools.reduce(jnp.logical_and, [pl.program_id(i) == 0 for i in range(len(grid))])
        last = functools.reduce(jnp.logical_and, [pl.program_id(i) == grid[i] - 1 for i in range(len(grid))])

        @pl.when(first)
        def _():
            rider.start(rins, routs, send_sems, recv_sems)

        body(*ins, *outs, *rest[:n_sc])

        @pl.when(last)
        def _():
            rider.finish(rins, routs, send_sems, recv_sems)

    res = pl.pallas_call(
        hosted, name=name, grid=grid, in_specs=in_specs + [ANY] * r_in, out_specs=out_specs + [ANY] * r_out,
        out_shape=out_shape + rider.out_shapes,
        scratch_shapes=scratch_shapes + [pltpu.SemaphoreType.DMA((rider.nsem,)), pltpu.SemaphoreType.DMA((rider.nsem,))],
        input_output_aliases={n_in + i: n_out + o for i, o in rider.aliases.items()}, compiler_params=cparams,
    )(*operands, *rider.arrays)
    return res[:n_out], res[n_out:]


def _run_rider(name, rider):
    r_in, r_out = len(rider.arrays), len(rider.out_shapes)

    def body(*refs):
        rins, routs, send_sems, recv_sems = refs[:r_in], refs[r_in:r_in + r_out], refs[-2], refs[-1]
        rider.start(rins, routs, send_sems, recv_sems)
        rider.finish(rins, routs, send_sems, recv_sems)

    return pl.pallas_call(
        body, name=name, in_specs=[ANY] * r_in, out_specs=[ANY] * r_out, out_shape=rider.out_shapes,
        scratch_shapes=[pltpu.SemaphoreType.DMA((rider.nsem,)), pltpu.SemaphoreType.DMA((rider.nsem,))],
        input_output_aliases=rider.aliases,
    )(*rider.arrays)


def _row_tile(rows, cap):
    if rows <= cap:
        return rows
    for t in range(cap - cap % 8, 7, -8):
        if rows % t == 0:
            return t
    raise ValueError(f"no row tile for {rows}")


def _dot(a, b):
    return jnp.dot(a.astype(BF16), b.astype(BF16), preferred_element_type=F32)


def _dot_nt(a, b):
    return lax.dot_general(a.astype(BF16), b.astype(BF16), (((1,), (1,)), ((), ())), preferred_element_type=F32)


def _dot_tn(a, b):
    return lax.dot_general(a.astype(BF16), b.astype(BF16), (((0,), (0,)), ((), ())), preferred_element_type=F32)


def _sigmoid(x):
    return 1.0 / (1.0 + jnp.exp(-x))


_GELU_C = math.sqrt(2.0 / math.pi)


def _gelu(x):
    return 0.5 * x * (1.0 + jnp.tanh(_GELU_C * (x + 0.044715 * x * x * x)))


def _gelu_grad(x):
    th = jnp.tanh(_GELU_C * (x + 0.044715 * x * x * x))
    return 0.5 * (1.0 + th) + 0.5 * x * (1.0 - th * th) * _GELU_C * (1.0 + 3.0 * 0.044715 * x * x)


def _rowwise(name, fn, ins, outs, accs=(), *, rows, tm, ncol=1):
    n_in, n_out = len(ins), len(outs)

    def body(*refs):
        res = fn(*[r[...] for r in refs[:n_in]])
        for r, v in zip(refs[n_in:n_in + n_out], res[:n_out]):
            r[...] = v.astype(r.dtype)
        first = pl.program_id(1) == 0
        for r, v in zip(refs[n_in + n_out:], res[n_out:]):
            @pl.when(first)
            def _():
                r[...] = v

            @pl.when(jnp.logical_not(first))
            def _():
                r[...] += v

    in_specs = []
    for _, width, base, kind in ins:
        if kind == "row":
            in_specs.append(pl.BlockSpec((tm, width), lambda j, i, b=base: (i, b + j)))
        else:
            in_specs.append(pl.BlockSpec((1, width), lambda j, i, b=base: (0, b + j)))
    out_specs = [pl.BlockSpec((tm, width), lambda j, i: (i, j)) for _, width, _ in outs]
    out_specs += [pl.BlockSpec((1, width), lambda j, i: (0, j)) for _, width in accs]
    out_shape = [jax.ShapeDtypeStruct((rows, total), dt) for total, _, dt in outs]
    out_shape += [jax.ShapeDtypeStruct((1, total), F32) for total, _ in accs]
    return pl.pallas_call(
        body, name=name, grid=(ncol, rows // tm), in_specs=in_specs, out_specs=out_specs, out_shape=out_shape,
        compiler_params=_params("arbitrary", "arbitrary"),
    )(*[a for a, _, _, _ in ins])


def _mm(name, a, b, *, mode, grid, a_spec, b_spec, o_spec, out_shape, acc_shape, res=None, res_spec=None,
        pair_axis=None, rider=None):
    nk = grid[2]
    dot = {"nn": _dot, "nt": _dot_nt, "tn": _dot_tn}[mode]
    a_list = list(a) if isinstance(a, tuple) else [a]
    b_list = list(b) if isinstance(b, tuple) else [b]
    na, nb = len(a_list), len(b_list)
    assert (pair_axis is None) == (na + nb == 2)
    direct = nk == 1 and pair_axis is None

    def body(*refs):
        a_refs, b_refs = refs[:na], refs[na:na + nb]
        r_ref = None if res is None else refs[na + nb]
        o_ref = refs[na + nb + (0 if res is None else 1)]

        def finish(v):
            if res is not None:
                v = v + r_ref[...]
            o_ref[...] = v.astype(o_ref.dtype)

        if direct:
            finish(dot(a_refs[0][...], b_refs[0][...]))
            return
        acc_ref = refs[-1]
        k = pl.program_id(2)

        @pl.when(k == 0)
        def _():
            acc_ref[...] = jnp.zeros_like(acc_ref)

        if pair_axis is None:
            acc_ref[...] += dot(a_refs[0][...], b_refs[0][...])
        else:
            lower = pl.program_id(pair_axis) < grid[pair_axis] // 2

            @pl.when(lower)
            def _():
                acc_ref[...] += dot(a_refs[0][...], b_refs[0][...])

            @pl.when(jnp.logical_not(lower))
            def _():
                acc_ref[...] += dot(a_refs[-1][...], b_refs[-1][...])

        @pl.when(k == nk - 1)
        def _():
            finish(acc_ref[...])

    operands = a_list + b_list + ([] if res is None else [res])
    in_specs = (list(a_spec) if na == 2 else [a_spec]) + (list(b_spec) if nb == 2 else [b_spec])
    in_specs += [] if res is None else [res_spec]
    got = _hosted_call(name, body, grid=grid, in_specs=in_specs, out_specs=[o_spec], out_shape=[out_shape],
                       scratch_shapes=[] if direct else [pltpu.VMEM(acc_shape, F32)], operands=operands, rider=rider)
    return got[0] if rider is None else (got[0][0], got[1])


MM_TILE_BUDGET_BYTES = 36 * 1024 * 1024
MM_TILE_CAP = 1024


def _mm_tile(t, row_bytes, fixed_bytes):
    cap = max(16, min(MM_TILE_CAP, (MM_TILE_BUDGET_BYTES - fixed_bytes) // row_bytes))
    return _row_tile(t, cap - cap % 16)


def _size(a):
    return jnp.dtype(a.dtype).itemsize


def _mm_fwd_cols(name, a, w3, out_dtype=F32):
    t, k = a.shape
    ns = w3.shape[2]
    tm = _mm_tile(t, 2 * k * _size(a) + 2 * ns * jnp.dtype(out_dtype).itemsize, 2 * k * ns * _size(w3))
    return _mm(name, a, w3, mode="nn", grid=(N_CHIPS, t // tm, 1),
               a_spec=pl.BlockSpec((tm, k), lambda j, i, kk: (i, 0)),
               b_spec=pl.BlockSpec((None, k, ns), lambda j, i, kk: (j, 0, 0)),
               o_spec=pl.BlockSpec((tm, ns), lambda j, i, kk: (i, j)),
               out_shape=jax.ShapeDtypeStruct((t, N_CHIPS * ns), out_dtype), acc_shape=(tm, ns))


def _mm_bwd_cols(name, d, w3, out_dtype=F32, rider=None):
    pair = isinstance(d, tuple)
    t = d[0].shape[0] if pair else d.shape[0]
    k, ns = w3.shape[1], w3.shape[2]
    dsize = _size(d[0] if pair else d)
    tm = _mm_tile(t, (4 if pair else 2) * ns * dsize + 2 * k * jnp.dtype(out_dtype).itemsize + 4 * k,
                  2 * k * ns * _size(w3))
    half = N_CHIPS // 2
    if pair:
        a_spec = (pl.BlockSpec((tm, ns), lambda i, j, kk: (i, jnp.minimum(kk, half - 1))),
                  pl.BlockSpec((tm, ns), lambda i, j, kk: (i, jnp.maximum(kk - half, 0))))
    else:
        a_spec = pl.BlockSpec((tm, ns), lambda i, j, kk: (i, kk))
    return _mm(name, d, w3, mode="nt", grid=(t // tm, 1, N_CHIPS), a_spec=a_spec,
               b_spec=pl.BlockSpec((None, k, ns), lambda i, j, kk: (kk, 0, 0)),
               o_spec=pl.BlockSpec((tm, k), lambda i, j, kk: (i, 0)),
               out_shape=jax.ShapeDtypeStruct((t, k), out_dtype), acc_shape=(tm, k), pair_axis=2 if pair else None,
               rider=rider)


def _mm_wgrad_cols(name, a, d):
    pair = isinstance(d, tuple)
    t, k = a.shape
    ns = (2 * d[0].shape[1] if pair else d.shape[1]) // N_CHIPS
    dsize = _size(d[0] if pair else d)
    tk = _mm_tile(t, 2 * k * _size(a) + (4 if pair else 2) * ns * dsize, k * ns * (4 + 2 * 2))
    half = N_CHIPS // 2
    if pair:
        b_spec = (pl.BlockSpec((tk, ns), lambda j, i, kk: (jnp.where(j < half, kk, 0), jnp.minimum(j, half - 1))),
                  pl.BlockSpec((tk, ns), lambda j, i, kk: (jnp.where(j < half, 0, kk), jnp.maximum(j - half, 0))))
    else:
        b_spec = pl.BlockSpec((tk, ns), lambda j, i, kk: (kk, j))
    return _mm(name, a, d, mode="tn", grid=(N_CHIPS, 1, t // tk),
               a_spec=pl.BlockSpec((tk, k), lambda j, i, kk: (kk, 0)), b_spec=b_spec,
               o_spec=pl.BlockSpec((None, k, ns), lambda j, i, kk: (j, 0, 0)),
               out_shape=jax.ShapeDtypeStruct((N_CHIPS, k, ns), BF16), acc_shape=(k, ns),
               pair_axis=0 if pair else None)


MM_BLOCK_CAP = 1408


def _mm_fwd_rows(name, a, w, res=None, out_dtype=F32):
    t, k = a.shape
    n = w.shape[1]
    tk = k if k <= MM_BLOCK_CAP else MM_BLOCK_CAP
    assert k % tk == 0
    row_bytes = 2 * tk * _size(a) + 2 * n * jnp.dtype(out_dtype).itemsize + (0 if res is None else 2 * n * 4) + 4 * n
    tm = _mm_tile(t, row_bytes, 2 * tk * n * _size(w))
    return _mm(name, a, w, mode="nn", grid=(t // tm, 1, k // tk),
               a_spec=pl.BlockSpec((tm, tk), lambda i, j, kk: (i, kk)),
               b_spec=pl.BlockSpec((tk, n), lambda i, j, kk: (kk, 0)),
               o_spec=pl.BlockSpec((tm, n), lambda i, j, kk: (i, 0)),
               out_shape=jax.ShapeDtypeStruct((t, n), out_dtype), acc_shape=(tm, n),
               res=res, res_spec=None if res is None else pl.BlockSpec((tm, n), lambda i, j, kk: (i, 0)))


def _mm_bwd_rows(name, d, w, out_dtype=F32):
    t, n = d.shape
    k = w.shape[0]
    tn = k if k <= MM_BLOCK_CAP else MM_BLOCK_CAP
    assert k % tn == 0
    tm = _mm_tile(t, 2 * n * _size(d) + 2 * tn * jnp.dtype(out_dtype).itemsize, 2 * tn * n * _size(w))
    return _mm(name, d, w, mode="nt", grid=(t // tm, k // tn, 1),
               a_spec=pl.BlockSpec((tm, n), lambda i, j, kk: (i, 0)),
               b_spec=pl.BlockSpec((tn, n), lambda i, j, kk: (j, 0)),
               o_spec=pl.BlockSpec((tm, tn), lambda i, j, kk: (i, j)),
               out_shape=jax.ShapeDtypeStruct((t, k), out_dtype), acc_shape=(tm, tn))


def _mm_wgrad_rows(name, a, d):
    t, k = a.shape
    n = d.shape[1]
    nblk = next(b for b in (1, 2, 4) if (k // b) % LANES == 0 and k // b <= MM_BLOCK_CAP)
    ks = k // nblk
    tk = _mm_tile(t, 2 * ks * _size(a) + 2 * n * _size(d), ks * n * (4 + 2 * 2))
    return _mm(name, a, d, mode="tn", grid=(nblk, 1, t // tk),
               a_spec=pl.BlockSpec((tk, ks), lambda j, i, kk: (kk, j)),
               b_spec=pl.BlockSpec((tk, n), lambda j, i, kk: (kk, 0)),
               o_spec=pl.BlockSpec((ks, n), lambda j, i, kk: (j, 0)),
               out_shape=jax.ShapeDtypeStruct((k, n), BF16), acc_shape=(ks, n))


def _s5_discretize(a_re, a_im, log_dt, b_re, b_im):
    dt = jnp.exp(log_dt)[:, None]
    mag = jnp.exp(a_re * dt)
    ang = a_im * dt
    lb_re = mag * jnp.cos(ang)
    lb_im = mag * jnp.sin(ang)
    den = a_re * a_re + a_im * a_im
    n_re = lb_re - 1.0
    n_im = lb_im
    co_re = ((n_re * a_re + n_im * a_im) / den)[..., None]
    co_im = ((n_im * a_re - n_re * a_im) / den)[..., None]
    bb_re = co_re * b_re - co_im * b_im
    bb_im = co_re * b_im + co_im * b_re
    return lb_re, lb_im, bb_re, bb_im


def _s5_in_blocks(bb):
    g = bb.shape[0]
    nb = g // S5_BLOCK_GROUPS
    t = bb.reshape(nb, S5_BLOCK_GROUPS, S5_STATE, S5_GROUP).transpose(0, 1, 3, 2)
    eye = jnp.eye(S5_BLOCK_GROUPS, dtype=bb.dtype)
    full = t[:, :, :, None, :] * eye[None, :, None, :, None]
    return full.reshape(nb, S5_BLOCK_GROUPS * S5_GROUP, S5_BLOCK_GROUPS * S5_STATE)


def _s5_in_blocks_diag(blocks):
    nb = blocks.shape[0]
    t = blocks.reshape(nb, S5_BLOCK_GROUPS, S5_GROUP, S5_BLOCK_GROUPS, S5_STATE)
    d = jnp.einsum("bghgp->bghp", t)
    return d.transpose(0, 1, 3, 2).reshape(nb * S5_BLOCK_GROUPS, S5_STATE, S5_GROUP)


def _s5_out_blocks(c):
    g = c.shape[0]
    nb = g // S5_BLOCK_GROUPS
    t = c.reshape(nb, S5_BLOCK_GROUPS, S5_GROUP, S5_STATE).transpose(0, 1, 3, 2)
    eye = jnp.eye(S5_BLOCK_GROUPS, dtype=c.dtype)
    full = t[:, :, :, None, :] * eye[None, :, None, :, None]
    return full.reshape(nb, S5_BLOCK_GROUPS * S5_STATE, S5_BLOCK_GROUPS * S5_GROUP)


def _s5_out_blocks_diag(blocks):
    nb = blocks.shape[0]
    t = blocks.reshape(nb, S5_BLOCK_GROUPS, S5_STATE, S5_BLOCK_GROUPS, S5_GROUP)
    d = jnp.einsum("bgpgh->bgph", t)
    return d.transpose(0, 1, 3, 2).reshape(nb * S5_BLOCK_GROUPS, S5_GROUP, S5_STATE)


def _s5_scan_tables(lr, li, reverse):
    def cmul(a, b):
        return a[0] * b[0] - a[1] * b[1], a[0] * b[1] + a[1] * b[0]

    lam = (lr, -li) if reverse else (lr, li)
    pw = [lam]
    for _ in range(SUBLANES - 1):
        pw.append(cmul(pw[-1], lam))
    sub = jnp.arange(SUBLANES)[:, None]
    rows = []
    for s in (1, 2, 4):
        keep = (sub <= SUBLANES - 1 - s) if reverse else (sub >= s)
        rows.append(jnp.where(keep, pw[s - 1][0][None, :], 0.0))
        rows.append(jnp.where(keep, pw[s - 1][1][None, :], 0.0))
    order = list(range(SUBLANES - 1, -1, -1)) if reverse else list(range(SUBLANES))
    rows.append(jnp.stack([pw[i][0] for i in order]))
    rows.append(jnp.stack([pw[i][1] for i in order]))
    return jnp.concatenate(rows, axis=0)


def _s5_scan(vre_ref, vim_ref, coef_ref, seq, width, reverse, xre_ref=None, xim_ref=None):
    nt = seq // SUBLANES
    nl = width // LANES
    per = 2 if xre_ref is None else 4
    sub = lax.broadcasted_iota(jnp.int32, (SUBLANES, LANES), 0)

    def step(k, carry):
        kk = (nt - 1 - k) if reverse else k
        rows = pl.ds(pl.multiple_of(kk * SUBLANES, SUBLANES), SUBLANES)
        out = []
        for j in range(nl):
            lanes = slice(j * LANES, (j + 1) * LANES)
            co = [coef_ref[SUBLANES * q:SUBLANES * (q + 1), lanes] for q in range(8)]
            cr, ci = carry[per * j], carry[per * j + 1]
            vr = vre_ref[rows, lanes]
            vi = vim_ref[rows, lanes]
            for q, s in enumerate((1, 2, 4)):
                sh = SUBLANES - s if reverse else s
                rr = pltpu.roll(vr, sh, 0)
                ri = pltpu.roll(vi, sh, 0)
                ar, ai = co[2 * q], co[2 * q + 1]
                vr, vi = vr + ar * rr - ai * ri, vi + ar * ri + ai * rr
            edge = 0 if reverse else SUBLANES - 1
            cbr = jnp.broadcast_to(cr[edge:edge + 1, :], (SUBLANES, LANES))
            cbi = jnp.broadcast_to(ci[edge:edge + 1, :], (SUBLANES, LANES))
            pr, pi = co[6], co[7]
            vr, vi = vr + pr * cbr - pi * cbi, vi + pr * cbi + pi * cbr
            vre_ref[rows, lanes] = vr
            vim_ref[rows, lanes] = vi
            out += [vr, vi]
            if xre_ref is not None:
                nr = jnp.where(sub == SUBLANES - 1, cbr, pltpu.roll(vr, SUBLANES - 1, 0))
                ni = jnp.where(sub == SUBLANES - 1, cbi, pltpu.roll(vi, SUBLANES - 1, 0))
                xr = xre_ref[rows, lanes]
                xi = xim_ref[rows, lanes]
                out += [carry[per * j + 2] + nr * xr + ni * xi, carry[per * j + 3] + ni * xr - nr * xi]
        return tuple(out)

    zero = jnp.zeros((SUBLANES, LANES), F32)
    res = lax.fori_loop(0, nt, step, (zero,) * (per * nl))
    if xre_ref is None:
        return None
    return jnp.concatenate(
        [jnp.concatenate([jnp.sum(res[per * j + 2], axis=0, keepdims=True) for j in range(nl)], axis=1),
         jnp.concatenate([jnp.sum(res[per * j + 3], axis=0, keepdims=True) for j in range(nl)], axis=1)], axis=0)


def _s5_fwd(z, bre3, bim3, cre3, cim3, coef, dskip, *, nseq, seq, rider=None):
    nb = bre3.shape[0]
    ch, ns = bre3.shape[1], bre3.shape[2]

    def body(za_ref, bre_ref, bim_ref, cre_ref, cim_ref, coef_ref, d_ref, y_ref, xre_ref, xim_ref):
        za = za_ref[...]
        xre_ref[...] = _dot(za, bre_ref[...])
        xim_ref[...] = _dot(za, bim_ref[...])
        _s5_scan(xre_ref, xim_ref, coef_ref, seq, ns, False)
        y_ref[...] = _dot(xre_ref[...], cre_ref[...]) - _dot(xim_ref[...], cim_ref[...]) + d_ref[...] * za

    blk3 = lambda r, c: pl.BlockSpec((None, r, c), lambda b, j: (j, 0, 0))
    return _hosted_call(
        "s5_fwd", body, grid=(nseq, nb),
        in_specs=[pl.BlockSpec((seq, ch), lambda b, j: (b, j)), blk3(ch, ns), blk3(ch, ns), blk3(ns, ch), blk3(ns, ch),
                  pl.BlockSpec((8 * SUBLANES, ns), lambda b, j: (0, j)), pl.BlockSpec((1, ch), lambda b, j: (0, j))],
        out_specs=[pl.BlockSpec((seq, ch), lambda b, j: (b, j)), pl.BlockSpec((seq, ns), lambda b, j: (b, j)),
                   pl.BlockSpec((seq, ns), lambda b, j: (b, j))],
        out_shape=[jax.ShapeDtypeStruct((nseq * seq, nb * ch), F32), jax.ShapeDtypeStruct((nseq * seq, nb * ns), F32),
                   jax.ShapeDtypeStruct((nseq * seq, nb * ns), F32)],
        operands=(z, bre3, bim3, cre3, cim3, coef, dskip), rider=rider)


def _s5_bwd(dy, z, xre, xim, bre3, bim3, cre3, cim3, coef_rev, dskip, *, nseq, seq, rider=None):
    nb = bre3.shape[0]
    ch, ns = bre3.shape[1], bre3.shape[2]

    def body(dy_ref, za_ref, xre_ref, xim_ref, bre_ref, bim_ref, cre_ref, cim_ref, coef_ref, d_ref,
             dza_ref, dbre_ref, dbim_ref, dcre_ref, dcim_ref, dlam_ref, dd_ref, are_ref, aim_ref):
        dy = dy_ref[...]
        za = za_ref[...]
        are_ref[...] = _dot_nt(dy, cre_ref[...])
        aim_ref[...] = -_dot_nt(dy, cim_ref[...])
        dlam = _s5_scan(are_ref, aim_ref, coef_ref, seq, ns, True, xre_ref, xim_ref)
        are = are_ref[...]
        aim = aim_ref[...]
        dza_ref[...] = (_dot_nt(are, bre_ref[...]) + _dot_nt(aim, bim_ref[...]) + d_ref[...] * dy).astype(dza_ref.dtype)
        parts = (_dot_tn(za, are), _dot_tn(za, aim), _dot_tn(xre_ref[...], dy), -_dot_tn(xim_ref[...], dy),
                 dlam, jnp.sum(dy * za, axis=0, keepdims=True))
        first = pl.program_id(1) == 0
        for r, v in zip((dbre_ref, dbim_ref, dcre_ref, dcim_ref, dlam_ref, dd_ref), parts):
            @pl.when(first)
            def _():
                r[...] = v

            @pl.when(jnp.logical_not(first))
            def _():
                r[...] += v

    blk3 = lambda r, c: pl.BlockSpec((None, r, c), lambda j, b: (j, 0, 0))
    tok = lambda c: pl.BlockSpec((seq, c), lambda j, b: (b, j))
    return _hosted_call(
        "s5_bwd", body, grid=(nb, nseq),
        in_specs=[tok(ch), tok(ch), tok(ns), tok(ns), blk3(ch, ns), blk3(ch, ns), blk3(ns, ch), blk3(ns, ch),
                  pl.BlockSpec((8 * SUBLANES, ns), lambda j, b: (0, j)), pl.BlockSpec((1, ch), lambda j, b: (0, j))],
        out_specs=[tok(ch), blk3(ch, ns), blk3(ch, ns), blk3(ns, ch), blk3(ns, ch),
                   pl.BlockSpec((None, 2, ns), lambda j, b: (j, 0, 0)), pl.BlockSpec((1, ch), lambda j, b: (0, j))],
        out_shape=[jax.ShapeDtypeStruct((nseq * seq, nb * ch), BF16),
                   jax.ShapeDtypeStruct((nb, ch, ns), F32), jax.ShapeDtypeStruct((nb, ch, ns), F32),
                   jax.ShapeDtypeStruct((nb, ns, ch), F32), jax.ShapeDtypeStruct((nb, ns, ch), F32),
                   jax.ShapeDtypeStruct((nb, 2, ns), F32), jax.ShapeDtypeStruct((1, nb * ch), F32)],
        scratch_shapes=[pltpu.VMEM((seq, ns), F32), pltpu.VMEM((seq, ns), F32)],
        operands=(dy, z, xre, xim, bre3, bim3, cre3, cim3, coef_rev, dskip), rider=rider)


def _cumsum_rows(x, reverse=False):
    n = x.shape[0]
    row = lax.broadcasted_iota(jnp.int32, x.shape, 0)
    s = 1
    while s < n:
        if reverse:
            x = x + jnp.where(row < n - s, pltpu.roll(x, n - s, 0), 0.0)
        else:
            x = x + jnp.where(row >= s, pltpu.roll(x, s, 0), 0.0)
        s *= 2
    return x


def _hg_gates(zq, zf, lb):
    sg = _sigmoid(zf)
    f = lb + (1.0 - lb) * sg
    sq = _sigmoid(zq)
    qa = zq * sq * (HEAD ** -0.5)
    b = _cumsum_rows(jnp.log(f))
    return sg, f, sq, qa, 1.0 - f, b


SUB = 16


def _hg_scores(qa, kk, b):
    c = qa.shape[0]
    row = lax.broadcasted_iota(jnp.int32, qa.shape, 0)
    pos = jnp.bitwise_and(row, SUB - 1)
    dmat = lax.broadcasted_iota(jnp.int32, (c, c), 0) - lax.broadcasted_iota(jnp.int32, (c, c), 1)
    p = jnp.zeros((c, c), F32)
    for d in range(SUB):
        if d == 0:
            fd = qa * kk
        else:
            e = jnp.exp(jnp.minimum(b - pltpu.roll(b, d, 0), 0.0))
            fd = jnp.where(pos >= d, qa * pltpu.roll(kk, d, 0) * e, 0.0)
        p = jnp.where(dmat == d, jnp.sum(fd, axis=1, keepdims=True), p)
    col = lax.broadcasted_iota(jnp.int32, (SUB, c), 1)
    blocks = [jnp.zeros((SUB, c), F32)]
    for r0 in range(SUB, c, SUB):
        beta = b[r0 - 1:r0, :]
        qt = qa[r0:r0 + SUB] * jnp.exp(b[r0:r0 + SUB] - beta)
        kt = kk * jnp.exp(jnp.minimum(beta - b, 0.0))
        blocks.append(jnp.where(col < r0, _dot_nt(qt, kt), 0.0))
    return p + jnp.concatenate(blocks, axis=0)


def _hg_scores_bwd(dp, qa, kk, b):
    c = qa.shape[0]
    row = lax.broadcasted_iota(jnp.int32, qa.shape, 0)
    pos = jnp.bitwise_and(row, SUB - 1)
    dmat = lax.broadcasted_iota(jnp.int32, (c, c), 0) - lax.broadcasted_iota(jnp.int32, (c, c), 1)
    dqa = jnp.zeros_like(qa)
    dkk = jnp.zeros_like(qa)
    db = jnp.zeros_like(qa)
    for d in range(SUB):
        dcol = jnp.sum(jnp.where(dmat == d, dp, 0.0), axis=1, keepdims=True)
        if d == 0:
            dqa = dqa + dcol * kk
            dkk = dkk + dcol * qa
        else:
            e = jnp.exp(jnp.minimum(b - pltpu.roll(b, d, 0), 0.0))
            w = jnp.where(pos >= d, dcol * e, 0.0)
            kr = pltpu.roll(kk, d, 0)
            dqa = dqa + w * kr
            tmp = w * qa
            dkk = dkk + pltpu.roll(tmp, c - d, 0)
            x = tmp * kr
            db = db + x - pltpu.roll(x, c - d, 0)
    col = lax.broadcasted_iota(jnp.int32, (SUB, c), 1)
    dq_blocks = [jnp.zeros((SUB, qa.shape[1]), F32)]
    db_blocks = [jnp.zeros((SUB, qa.shape[1]), F32)]
    for r0 in range(SUB, c, SUB):
        beta = b[r0 - 1:r0, :]
        eq = jnp.exp(b[r0:r0 + SUB] - beta)
        ek = jnp.exp(jnp.minimum(beta - b, 0.0))
        qt = qa[r0:r0 + SUB] * eq
        kt = kk * ek
        dpi = jnp.where(col < r0, dp[r0:r0 + SUB, :], 0.0)
        dqt = _dot(dpi, kt)
        dkt = _dot_tn(dpi, qt)
        dq_blocks.append(dqt * eq)
        db_blocks.append(dqt * qt)
        dkk = dkk + dkt * ek
        db = db - dkt * kt
    return dqa + jnp.concatenate(dq_blocks, axis=0), dkk, db + jnp.concatenate(db_blocks, axis=0)


def _hg_chunks_per_step(seq):
    nc = seq // CHUNK
    cps = next(k for k in (4, 2, 1) if nc % k == 0)
    return nc, cps, nc // cps


def _hg_fwd(z, lbrow, gain, *, nseq, seq, heads, qoff, rider=None):
    nc, cps, nblk = _hg_chunks_per_step(seq)
    blk = cps * CHUNK
    zspec = lambda off: pl.BlockSpec((blk, HEAD), lambda h, b, n, off=off: (b * nblk + n, off + h))

    def body(zq_ref, zf_ref, zi_ref, zg_ref, lb_ref, gn_ref, o_ref, yb_ref, st_ref, state):
        @pl.when(pl.program_id(2) == 0)
        def _():
            state[...] = jnp.zeros_like(state)

        lb = lb_ref[...]
        gain_v = gn_ref[...]

        def chunk(ci, carry):
            rows = pl.ds(pl.multiple_of(ci * CHUNK, CHUNK), CHUNK)
            st = state[...]
            st_ref[ci] = st
            zi = zi_ref[rows, :]
            zg = zg_ref[rows, :]
            _, _, _, qa, kk, b = _hg_gates(zq_ref[rows, :], zf_ref[rows, :], lb)
            o = _dot_nt(qa * jnp.exp(b), st) + _dot(_hg_scores(qa, kk, b), zi)
            bl = b[CHUNK - 1:CHUNK, :]
            state[...] = st * jnp.exp(bl) + _dot_tn(zi, kk * jnp.exp(bl - b))
            o_ref[rows, :] = o
            r = lax.rsqrt(jnp.mean(o * o, axis=1, keepdims=True) + EPS)
            yb_ref[rows, :] = (o * r * gain_v * zg * _sigmoid(zg)).astype(yb_ref.dtype)
            return carry

        lax.fori_loop(0, cps, chunk, 0)

    tok = pl.BlockSpec((blk, HEAD), lambda h, b, n: (b * nblk + n, h))
    vec = pl.BlockSpec((1, HEAD), lambda h, b, n: (0, h))
    rows = nseq * seq
    return _hosted_call(
        "hgrn2_fwd", body, grid=(heads, nseq, nblk),
        in_specs=[zspec(qoff), zspec(qoff + heads), zspec(qoff + 2 * heads), zspec(qoff + 3 * heads), vec, vec],
        out_specs=[tok, tok, pl.BlockSpec((None, None, cps, HEAD, HEAD), lambda h, b, n: (h, b, n, 0, 0))],
        out_shape=[jax.ShapeDtypeStruct((rows, heads * HEAD), F32), jax.ShapeDtypeStruct((rows, heads * HEAD), BF16),
                   jax.ShapeDtypeStruct((heads, nseq, nc, HEAD, HEAD), F32)],
        scratch_shapes=[pltpu.VMEM((HEAD, HEAD), F32)], operands=(z, z, z, z, lbrow, gain), rider=rider)


def _hg_bwd(dyb, z, o, states, lbrow, gain, *, nseq, seq, heads, qoff, rider=None):
    nc, cps, nblk = _hg_chunks_per_step(seq)
    blk = cps * CHUNK
    rev = lambda n: nblk - 1 - n
    zspec = lambda off: pl.BlockSpec((blk, HEAD), lambda h, b, n, off=off: (b * nblk + rev(n), off + h))

    def body(dyb_ref, zq_ref, zf_ref, zi_ref, zg_ref, o_ref, st_ref, lb_ref, gn_ref,
             dzq_ref, dzf_ref, dzi_ref, dzg_ref, dlb_ref, dgn_ref, dstate):
        @pl.when(pl.program_id(2) == 0)
        def _():
            dstate[...] = jnp.zeros_like(dstate)

        @pl.when(jnp.logical_and(pl.program_id(1) == 0, pl.program_id(2) == 0))
        def _():
            dlb_ref[...] = jnp.zeros_like(dlb_ref)
            dgn_ref[...] = jnp.zeros_like(dgn_ref)

        lb = lb_ref[...]
        gain_v = gn_ref[...]
        c = CHUNK
        causal = lax.broadcasted_iota(jnp.int32, (c, c), 0) >= lax.broadcasted_iota(jnp.int32, (c, c), 1)

        def chunk(step, carry):
            ci = cps - 1 - step
            rows = pl.ds(pl.multiple_of(ci * CHUNK, CHUNK), CHUNK)
            zq = zq_ref[rows, :]
            zi = zi_ref[rows, :]
            zg = zg_ref[rows, :]
            sg, f, sq, qa, kk, b = _hg_gates(zq, zf_ref[rows, :], lb)
            eb = jnp.exp(b)
            qt = qa * eb
            bl = b[c - 1:c, :]
            ebl = jnp.exp(bl)
            ekb = jnp.exp(bl - b)
            kh = kk * ekb
            st = st_ref[ci]
            dst = dstate[...]
            o = o_ref[rows, :]
            r = lax.rsqrt(jnp.mean(o * o, axis=1, keepdims=True) + EPS)
            oh = o * r
            szg = _sigmoid(zg)
            dyb = dyb_ref[rows, :]
            don = dyb * zg * szg
            dzg_ref[rows, :] = (dyb * oh * gain_v * szg * (1.0 + zg * (1.0 - szg))).astype(dzg_ref.dtype)
            doh = don * gain_v
            do = r * (doh - oh * jnp.mean(doh * oh, axis=1, keepdims=True))
            dqt = _dot(do, st)
            dp = jnp.where(causal, _dot_nt(do, zi), 0.0)
            p = _hg_scores(qa, kk, b)
            dzi_ref[rows, :] = (_dot_tn(p, do) + _dot_nt(kh, dst)).astype(dzi_ref.dtype)
            dkh = _dot(zi, dst)
            dbl = jnp.sum(dkh * kh, axis=0, keepdims=True) + jnp.sum(dst * st, axis=0, keepdims=True) * ebl
            dstate[...] = _dot_tn(do, qt) + dst * ebl
            dqa_s, dkk_s, db_s = _hg_scores_bwd(dp, qa, kk, b)
            dqa = dqt * eb + dqa_s
            dkk = dkh * ekb + dkk_s
            db = dqt * qt - dkh * kh + db_s
            row = lax.broadcasted_iota(jnp.int32, db.shape, 0)
            db = db + jnp.where(row == c - 1, dbl, 0.0)
            df = _cumsum_rows(db, reverse=True) / f - dkk
            dzf_ref[rows, :] = (df * (1.0 - lb) * sg * (1.0 - sg)).astype(dzf_ref.dtype)
            dzq_ref[rows, :] = (dqa * (HEAD ** -0.5) * sq * (1.0 + zq * (1.0 - sq))).astype(dzq_ref.dtype)
            dlb_ref[...] += jnp.sum(df * (1.0 - sg), axis=0, keepdims=True)
            dgn_ref[...] += jnp.sum(don * oh, axis=0, keepdims=True)
            return carry

        lax.fori_loop(0, cps, chunk, 0)

    tok = pl.BlockSpec((blk, HEAD), lambda h, b, n: (b * nblk + rev(n), h))
    vec = pl.BlockSpec((1, HEAD), lambda h, b, n: (0, h))
    rows = nseq * seq
    return _hosted_call(
        "hgrn2_bwd", body, grid=(heads, nseq, nblk),
        in_specs=[tok, zspec(qoff), zspec(qoff + heads), zspec(qoff + 2 * heads), zspec(qoff + 3 * heads), tok,
                  pl.BlockSpec((None, None, cps, HEAD, HEAD), lambda h, b, n: (h, b, rev(n), 0, 0)), vec, vec],
        out_specs=[tok, tok, tok, tok, vec, vec],
        out_shape=[jax.ShapeDtypeStruct((rows, heads * HEAD), BF16)] * 4
        + [jax.ShapeDtypeStruct((1, heads * HEAD), F32)] * 2,
        scratch_shapes=[pltpu.VMEM((HEAD, HEAD), F32)], operands=(dyb, z, z, z, z, o, states, lbrow, gain), rider=rider)


def _conv_taps(h, w, bias):
    row = lax.broadcasted_iota(jnp.int32, h.shape, 0)
    h1 = jnp.where(row >= 1, pltpu.roll(h, 1, 0), 0.0)
    h2 = jnp.where(row >= 2, pltpu.roll(h, 2, 0), 0.0)
    return h2 * w[0:1, :] + h1 * w[1:2, :] + h * w[2:3, :] + bias, h1, h2


def _conv_fwd(h, wconv, bconv, *, nseq, seq):
    ff2 = h.shape[1]
    ncol = ff2 // 2 // LANES

    def body(hg_ref, hv_ref, wg_ref, wv_ref, bg_ref, bv_ref, a_ref):
        g, _, _ = _conv_taps(hg_ref[...], wg_ref[...], bg_ref[...])
        v, _, _ = _conv_taps(hv_ref[...], wv_ref[...], bv_ref[...])
        a_ref[...] = (g * _sigmoid(g) * v).astype(a_ref.dtype)

    tok = lambda off: pl.BlockSpec((seq, LANES), lambda j, b, off=off: (b, off + j))
    wsp = lambda off: pl.BlockSpec((CONV_W, LANES), lambda j, b, off=off: (0, off + j))
    bsp = lambda off: pl.BlockSpec((1, LANES), lambda j, b, off=off: (0, off + j))
    return pl.pallas_call(
        body, name="conv_fwd", grid=(ncol, nseq),
        in_specs=[tok(0), tok(ncol), wsp(0), wsp(ncol), bsp(0), bsp(ncol)],
        out_specs=tok(0), out_shape=jax.ShapeDtypeStruct((nseq * seq, ff2 // 2), BF16),
        compiler_params=_params("arbitrary", "arbitrary"),
    )(h, h, wconv, wconv, bconv, bconv)


def _conv_bwd(da, h, wconv, bconv, *, nseq, seq):
    ff2 = h.shape[1]
    ncol = ff2 // 2 // LANES

    def half_bwd(d, hcur, h1, h2, w):
        n = d.shape[0]
        row = lax.broadcasted_iota(jnp.int32, d.shape, 0)
        d1 = jnp.where(row < n - 1, pltpu.roll(d, n - 1, 0), 0.0)
        d2 = jnp.where(row < n - 2, pltpu.roll(d, n - 2, 0), 0.0)
        dh = d * w[2:3, :] + d1 * w[1:2, :] + d2 * w[0:1, :]
        stats = jnp.concatenate(
            [jnp.sum(h2 * d, axis=0, keepdims=True), jnp.sum(h1 * d, axis=0, keepdims=True),
             jnp.sum(hcur * d, axis=0, keepdims=True), jnp.sum(d, axis=0, keepdims=True),
             jnp.zeros((SUBLANES - 4, d.shape[1]), F32)], axis=0)
        return dh, stats

    def body(da_ref, hg_ref, hv_ref, wg_ref, wv_ref, bg_ref, bv_ref, dhg_ref, dhv_ref, sg_ref, sv_ref):
        hg = hg_ref[...]
        hv = hv_ref[...]
        wg = wg_ref[...]
        wv = wv_ref[...]
        g, g1, g2 = _conv_taps(hg, wg, bg_ref[...])
        v, v1, v2 = _conv_taps(hv, wv, bv_ref[...])
        da = da_ref[...]
        s = _sigmoid(g)
        dhg, stg = half_bwd(da * v * s * (1.0 + g * (1.0 - s)), hg, g1, g2, wg)
        dhv, stv = half_bwd(da * g * s, hv, v1, v2, wv)
        dhg_ref[...] = dhg.astype(dhg_ref.dtype)
        dhv_ref[...] = dhv.astype(dhv_ref.dtype)
        first = pl.program_id(1) == 0
        for r, val in ((sg_ref, stg), (sv_ref, stv)):
            @pl.when(first)
            def _():
                r[...] = val

            @pl.when(jnp.logical_not(first))
            def _():
                r[...] += val

    tok = lambda off: pl.BlockSpec((seq, LANES), lambda j, b, off=off: (b, off + j))
    wsp = lambda off: pl.BlockSpec((CONV_W, LANES), lambda j, b, off=off: (0, off + j))
    bsp = lambda off: pl.BlockSpec((1, LANES), lambda j, b, off=off: (0, off + j))
    ssp = pl.BlockSpec((SUBLANES, LANES), lambda j, b: (0, j))
    dhg, dhv, stg, stv = pl.pallas_call(
        body, name="conv_bwd", grid=(ncol, nseq),
        in_specs=[tok(0), tok(0), tok(ncol), wsp(0), wsp(ncol), bsp(0), bsp(ncol)],
        out_specs=[tok(0), tok(0), ssp, ssp],
        out_shape=[jax.ShapeDtypeStruct((nseq * seq, ff2 // 2), BF16)] * 2
        + [jax.ShapeDtypeStruct((SUBLANES, ff2 // 2), F32)] * 2,
        compiler_params=_params("arbitrary", "arbitrary"),
    )(da, h, h, wconv, wconv, bconv, bconv)
    return (dhg, dhv), jnp.concatenate([stg, stv], axis=1)


def _rms_fwd(xv, g):
    r = lax.rsqrt(jnp.mean(xv * xv, axis=1, keepdims=True) + EPS)
    return (xv * r * g,)


def _rms_bwd(xv, g, dy, res):
    r = lax.rsqrt(jnp.mean(xv * xv, axis=1, keepdims=True) + EPS)
    xh = xv * r
    dxh = dy * g
    dx = r * (dxh - xh * jnp.mean(dxh * xh, axis=1, keepdims=True)) + res
    return dx, jnp.sum(dy * xh, axis=0, keepdims=True)


def _loss_head(x2, tgt, g):
    d = x2.shape[1]
    r = lax.rsqrt(jnp.mean(x2 * x2, axis=1, keepdims=True) + EPS)
    xh = x2 * r
    err = xh * g - tgt
    dy = err * (1.0 / d)
    dxh = dy * g
    dx = r * (dxh - xh * jnp.mean(dxh * xh, axis=1, keepdims=True))
    loss = 0.5 * jnp.sum(jnp.mean(err * err, axis=1, keepdims=True), axis=0, keepdims=True)
    return dx, jnp.sum(dy * xh, axis=0, keepdims=True), jnp.broadcast_to(loss, (1, LANES))


LATE = ("w_pa", "w_pb", "w_out", "w_up", "w_down")
EARLY_GRADS = ("w_down", "w_up", "w_out", "w_pa", "w_pb", "w_glu")
ROW_SHARDED = ("w_glu", "w_out", "w_down")


def _local_step(x, tgt, p, late, *, nseq, seq):
    p = dict(p)
    chip = 2 * lax.axis_index("x") + lax.axis_index("y")
    t, d = x.shape
    s5w = p["s5_d"].shape[1]
    hgw = p["gain"].shape[1]
    heads = hgw // HEAD
    qoff = s5w // LANES
    gblk = (s5w + 4 * hgw) // GATE_BLOCK
    ngb = d // GATE_BLOCK
    tm = _row_tile(t, 256)
    row = lambda a, w=None, base=0: (a, a.shape[1] if w is None else w, base, "row")
    vec = lambda a, w=None, base=0: (a, a.shape[1] if w is None else w, base, "vec")
    rw = functools.partial(_rowwise, rows=t, tm=tm)

    (u,) = rw("rms_mix", _rms_fwd, [row(x), vec(p["g_mix"])], [(d, d, BF16)])
    z = _mm_fwd_cols("in_proj", u, p["w_in"])

    lam_re, lam_im, bb_re, bb_im = _s5_discretize(p["s5_a_re"], p["s5_a_im"], p["s5_log_dt"], p["s5_b_re"], p["s5_b_im"])
    bre3 = _s5_in_blocks(bb_re).astype(BF16)
    bim3 = _s5_in_blocks(bb_im).astype(BF16)
    cre3 = _s5_out_blocks(p["s5_c_re"]).astype(BF16)
    cim3 = _s5_out_blocks(p["s5_c_im"]).astype(BF16)
    coef_f = _s5_scan_tables(lam_re.reshape(-1), lam_im.reshape(-1), False)
    coef_r = _s5_scan_tables(lam_re.reshape(-1), lam_im.reshape(-1), True)
    (o, yb, states), landed = _hg_fwd(z, p["lbrow"], p["gain"], nseq=nseq, seq=seq, heads=heads, qoff=qoff,
                                      rider=_gather_ici_rider([late[n] for n in LATE]))
    (y5, xre, xim), gathered = _s5_fwd(z, bre3, bim3, cre3, cim3, coef_f, p["s5_d"], nseq=nseq, seq=seq,
                                       rider=_gather_pass_rider(landed))
    for n, g in zip(LATE, gathered):
        full = lax.dynamic_update_index_in_dim(g, late[n], chip, 0)
        p[n] = full.reshape(-1, full.shape[-1]) if n in ROW_SHARDED else full
    (ya0,) = rw("s5_gelu", lambda y: (_gelu(y),), [row(y5)], [(s5w, s5w, BF16)])
    gl = _mm_fwd_rows("glu_proj", ya0, p["w_glu"])
    (ya,) = rw("s5_glu", lambda y, g, b: (_gelu(y) * _sigmoid(g + b),), [row(y5), row(gl), vec(p["b_glu"])],
               [(s5w, s5w, BF16)])

    pa = _mm_fwd_cols("proj_a", ya, p["w_pa"], out_dtype=BF16)
    pb = _mm_fwd_cols("proj_b", yb, p["w_pb"], out_dtype=BF16)
    gb = GATE_BLOCK
    (m,) = rw("merge", lambda ga, gbv, a, b: (_sigmoid(ga) * a + _sigmoid(gbv) * b,),
              [row(z, gb, gblk), row(z, gb, gblk + ngb), row(pa, gb), row(pb, gb)], [(d, gb, BF16)], ncol=ngb)
    x1 = _mm_fwd_rows("out_proj", m, p["w_out"], res=x)

    (u2,) = rw("rms_ffn", _rms_fwd, [row(x1), vec(p["g_ffn"])], [(d, d, BF16)])
    h = _mm_fwd_cols("up_proj", u2, p["w_up"])
    a = _conv_fwd(h, p["w_conv"], p["b_conv"], nseq=nseq, seq=seq)
    x2 = _mm_fwd_rows("down_proj", a, p["w_down"], res=x1)

    dx2, dg_final, lossv = rw("loss_head", _loss_head, [row(x2), row(tgt), vec(p["g_final"])], [(d, d, F32)],
                              accs=[(d, d), (LANES, LANES)])

    da = _mm_bwd_rows("down_bwd", dx2, p["w_down"])
    g_wdown = _mm_wgrad_rows("down_wgrad", a, dx2)
    dh, cstats = _conv_bwd(da, h, p["w_conv"], p["b_conv"], nseq=nseq, seq=seq)
    du2 = _mm_bwd_cols("up_bwd", dh, p["w_up"])
    g_wup = _mm_wgrad_cols("up_wgrad", u2, dh)
    dx1, dg_ffn = rw("rms_ffn_bwd", _rms_bwd, [row(x1), vec(p["g_ffn"]), row(du2), row(dx2)], [(d, d, F32)],
                     accs=[(d, d)])

    dm = _mm_bwd_rows("out_bwd", dx1, p["w_out"])
    g_wout = _mm_wgrad_rows("out_wgrad", m, dx1)

    def merge_bwd(ga, gbv, av, bv, dmv):
        sa = _sigmoid(ga)
        sb = _sigmoid(gbv)
        return dmv * sa, dmv * sb, dmv * av * sa * (1.0 - sa), dmv * bv * sb * (1.0 - sb)

    dpa, dpb, dzga, dzgb = rw("merge_bwd", merge_bwd,
                              [row(z, gb, gblk), row(z, gb, gblk + ngb), row(pa, gb), row(pb, gb), row(dm, gb)],
                              [(d, gb, BF16)] * 4, ncol=ngb)
    dya = _mm_bwd_cols("proj_a_bwd", dpa, p["w_pa"])
    g_wpa = _mm_wgrad_cols("proj_a_wgrad", ya, dpa)
    dyb = _mm_bwd_cols("proj_b_bwd", dpb, p["w_pb"])
    g_wpb = _mm_wgrad_cols("proj_b_wgrad", yb, dpb)

    def glu_bwd1(y, g, b, dyv):
        s = _sigmoid(g + b)
        dgl = dyv * _gelu(y) * s * (1.0 - s)
        return dgl, jnp.sum(dgl, axis=0, keepdims=True)

    dgl, db_glu = rw("s5_glu_bwd", glu_bwd1, [row(y5), row(gl), vec(p["b_glu"]), row(dya)], [(s5w, s5w, BF16)],
                     accs=[(s5w, s5w)])
    dgl_in = _mm_bwd_rows("glu_bwd", dgl, p["w_glu"])
    g_wglu = _mm_wgrad_rows("glu_wgrad", ya0, dgl)
    (dy5,) = rw("s5_gelu_bwd", lambda y, g, b, dyv, tv: ((dyv * _sigmoid(g + b) + tv) * _gelu_grad(y),),
                [row(y5), row(gl), vec(p["b_glu"]), row(dya), row(dgl_in)], [(s5w, s5w, F32)])
    partial = dict(w_down=g_wdown, w_up=g_wup, w_out=g_wout, w_pa=g_wpa, w_pb=g_wpb, w_glu=g_wglu)
    parts = [_grad_parts(partial[n]) for n in EARLY_GRADS]
    (dza, dbre3, dbim3, dcre3, dcim3, dlam, dd), sib = _s5_bwd(
        dy5, z, xre, xim, bre3, bim3, cre3, cim3, coef_r, p["s5_d"], nseq=nseq, seq=seq, rider=_swap_halves_rider(parts))
    pair = _pair_sums(EARLY_GRADS, parts, sib)
    (dzq, dzf, dzi, dzg, dlb, dgain), others = _hg_bwd(
        dyb, z, o, states, p["lbrow"], p["gain"], nseq=nseq, seq=seq, heads=heads, qoff=qoff, rider=_scatter_rider(pair))
    halves = _chip_sums(EARLY_GRADS, pair, others)

    dz = jnp.concatenate([dza, dzq, dzf, dzi, dzg, dzga, dzgb], axis=1)
    du, sibs = _mm_bwd_cols("in_bwd", dz, p["w_in"], rider=_swap_sums_rider(halves))
    big = dict(zip(EARLY_GRADS, _join_halves(halves, sibs)))

    g_win = _mm_wgrad_cols("in_wgrad", u, dz)
    last = [_grad_parts(g_win)]
    pair = _pair_sums(("w_in",), last, _run_rider("grad_swap_halves", _swap_halves_rider(last)))
    (half,) = _chip_sums(("w_in",), pair, _run_rider("grad_scatter_chips", _scatter_rider(pair)))
    mid = half.shape[0] // 2
    sib_half = jnp.concatenate(_run_rider("grad_swap_sums", _swap_sums_rider([half[:mid], half[mid:]])), axis=0)
    big["w_in"] = _join_halves([half], [sib_half])[0]

    dx, dg_mix = rw("rms_mix_bwd", _rms_bwd, [row(x), vec(p["g_mix"]), row(du), row(dx1)], [(d, d, F32)],
                    accs=[(d, d)])

    gshape = lam_re.shape
    small = {
        "loss": lossv, "g_mix": dg_mix, "g_ffn": dg_ffn, "g_final": dg_final, "b_glu": db_glu, "gain": dgain,
        "lbrow": dlb, "s5_d": dd, "w_conv": cstats[0:CONV_W], "b_conv": cstats[CONV_W:CONV_W + 1],
        "lam_re": dlam[:, 0, :].reshape(gshape), "lam_im": dlam[:, 1, :].reshape(gshape),
        "bb_re": _s5_in_blocks_diag(dbre3), "bb_im": _s5_in_blocks_diag(dbim3),
        "s5_c_re": _s5_out_blocks_diag(dcre3), "s5_c_im": _s5_out_blocks_diag(dcim3),
    }
    return dx, big, small


ANY = pl.BlockSpec(memory_space=pl.ANY)


def _place():
    x, y, c = lax.axis_index("x"), lax.axis_index("y"), lax.axis_index("c")
    chips = [(1 - x, y), (x, 1 - y), (1 - x, 1 - y)]
    return x, y, c, chips


def _remote(src, dst, send_sems, recv_sems, k, to):
    return pltpu.make_async_remote_copy(src_ref=src, dst_ref=dst, send_sem=send_sems.at[k], recv_sem=recv_sems.at[k],
                                        device_id=to, device_id_type=MESH)


def _half(rows, which):
    return pl.ds(pl.multiple_of(which * (rows // 2), 16), rows // 2)


def _gather_weights(shards, whole):
    n, nw = len(shards), len(whole)
    arrays = list(shards) + list(whole)

    def body(*refs):
        in_refs, out_refs = refs[:n + nw], refs[n + nw:2 * (n + nw)]
        send_sems, recv_sems = refs[2 * (n + nw):]
        x, y, c, chips = _place()
        me = 2 * x + y
        copy = functools.partial(_remote, send_sems=send_sems, recv_sems=recv_sems)
        sends = []
        for a in range(n):
            mine_half = _half(arrays[a].shape[0], c)
            for j, (cx, cy) in enumerate(chips):
                sends.append(copy(in_refs[a].at[mine_half], out_refs[a].at[me, mine_half], k=6 * a + j, to=(cx, cy, c)))
        for a in range(n, n + nw):
            for j, (cx, cy) in enumerate(chips):
                sends.append(copy(in_refs[a], out_refs[a].at[me], k=6 * n + 3 * (a - n) + j, to=(cx, cy, c)))
        for cp in sends:
            cp.start()
        for a in range(n):
            mine_half = _half(arrays[a].shape[0], c)
            for j, (cx, cy) in enumerate(chips):
                landed = out_refs[a].at[2 * cx + cy, mine_half]
                copy(landed, landed, k=6 * a + j, to=(x, y, c)).wait_recv()
                fwd = copy(landed, landed, k=6 * a + 3 + j, to=(x, y, 1 - c))
                fwd.start()
                sends.append(fwd)
        for a in range(n):
            other_half = _half(arrays[a].shape[0], 1 - c)
            for j, (cx, cy) in enumerate(chips):
                landed = out_refs[a].at[2 * cx + cy, other_half]
                copy(landed, landed, k=6 * a + 3 + j, to=(x, y, c)).wait_recv()
        for a in range(n, n + nw):
            for j, (cx, cy) in enumerate(chips):
                landed = out_refs[a].at[2 * cx + cy]
                copy(landed, landed, k=6 * n + 3 * (a - n) + j, to=(x, y, c)).wait_recv()
        for cp in sends:
            cp.wait_send()

    nsem = 6 * n + 3 * nw
    return pl.pallas_call(
        body, name="gather_weights", out_shape=[jax.ShapeDtypeStruct((N_CHIPS,) + a.shape, a.dtype) for a in arrays],
        in_specs=[ANY] * (n + nw), out_specs=[ANY] * (n + nw),
        scratch_shapes=[pltpu.SemaphoreType.DMA((nsem,)), pltpu.SemaphoreType.DMA((nsem,))],
    )(*arrays)


def _symmetric_rider(arrays, out_shapes, copies_of, nsem):
    def start(ins, outs, send_sems, recv_sems):
        for cp in copies_of(ins, outs, send_sems, recv_sems):
            cp.start()

    def finish(ins, outs, send_sems, recv_sems):
        for cp in copies_of(ins, outs, send_sems, recv_sems):
            cp.wait()

    return _Rider(arrays, out_shapes, nsem, start, finish)


def _swap_halves_rider(parts):
    def copies_of(ins, outs, send_sems, recv_sems):
        x, y, c, _ = _place()
        return [_remote(ins[a].at[:, _half(g.shape[1], 1 - c), :], outs[a], send_sems, recv_sems, a, (x, y, 1 - c))
                for a, g in enumerate(parts)]

    shapes = [jax.ShapeDtypeStruct((g.shape[0], g.shape[1] // 2, g.shape[2]), g.dtype) for g in parts]
    return _symmetric_rider(parts, shapes, copies_of, len(parts))


def _scatter_rider(parts):
    def copies_of(ins, outs, send_sems, recv_sems):
        x, y, c, chips = _place()
        return [_remote(ins[a].at[2 * cx + cy], outs[a].at[j], send_sems, recv_sems, 3 * a + j, (cx, cy, c))
                for a in range(len(parts)) for j, (cx, cy) in enumerate(chips)]

    shapes = [jax.ShapeDtypeStruct((N_CHIPS - 1,) + h.shape[1:], h.dtype) for h in parts]
    return _symmetric_rider(parts, shapes, copies_of, 3 * len(parts))


def _swap_sums_rider(parts):
    def copies_of(ins, outs, send_sems, recv_sems):
        x, y, c, _ = _place()
        return [_remote(ins[a], outs[a], send_sems, recv_sems, a, (x, y, 1 - c)) for a in range(len(parts))]

    shapes = [jax.ShapeDtypeStruct(g.shape, g.dtype) for g in parts]
    return _symmetric_rider(parts, shapes, copies_of, len(parts))


def _gather_ici_rider(shards):
    def sends(ins, outs, send_sems, recv_sems):
        x, y, c, chips = _place()
        return [_remote(ins[a].at[_half(s.shape[0], c)], outs[a].at[2 * x + y, _half(s.shape[0], c)], send_sems,
                        recv_sems, 3 * a + j, (cx, cy, c)) for a, s in enumerate(shards) for j, (cx, cy) in enumerate(chips)]

    def start(ins, outs, send_sems, recv_sems):
        for cp in sends(ins, outs, send_sems, recv_sems):
            cp.start()

    def finish(ins, outs, send_sems, recv_sems):
        x, y, c, chips = _place()
        for a, s in enumerate(shards):
            for j, (cx, cy) in enumerate(chips):
                landed = outs[a].at[2 * cx + cy, _half(s.shape[0], c)]
                _remote(landed, landed, send_sems, recv_sems, 3 * a + j, (x, y, c)).wait_recv()
        for cp in sends(ins, outs, send_sems, recv_sems):
            cp.wait_send()

    shapes = [jax.ShapeDtypeStruct((N_CHIPS,) + s.shape, s.dtype) for s in shards]
    return _Rider(shards, shapes, 3 * len(shards), start, finish)


def _gather_pass_rider(landed):
    def sends(ins, outs, send_sems, recv_sems):
        x, y, c, chips = _place()
        return [_remote(ins[a].at[2 * cx + cy, _half(g.shape[1], c)], outs[a].at[2 * cx + cy, _half(g.shape[1], c)],
                        send_sems, recv_sems, 3 * a + j, (x, y, 1 - c))
                for a, g in enumerate(landed) for j, (cx, cy) in enumerate(chips)]

    def start(ins, outs, send_sems, recv_sems):
        for cp in sends(ins, outs, send_sems, recv_sems):
            cp.start()

    def finish(ins, outs, send_sems, recv_sems):
        x, y, c, chips = _place()
        for a, g in enumerate(landed):
            for j, (cx, cy) in enumerate(chips):
                other = outs[a].at[2 * cx + cy, _half(g.shape[1], 1 - c)]
                _remote(other, other, send_sems, recv_sems, 3 * a + j, (x, y, c)).wait_recv()
        for cp in sends(ins, outs, send_sems, recv_sems):
            cp.wait_send()

    shapes = [jax.ShapeDtypeStruct(g.shape, g.dtype) for g in landed]
    return _Rider(landed, shapes, 3 * len(landed), start, finish, aliases={a: a for a in range(len(landed))})


def _grad_parts(g):
    return g.reshape((N_CHIPS, -1, g.shape[-1]))


def _pair_sums(names, parts, sib):
    ci = lax.axis_index("c")
    out = []
    for n, g, s in zip(names, parts, sib):
        rh, cols = s.shape[1], s.shape[2]
        own = lax.dynamic_slice_in_dim(g, ci * rh, rh, axis=1)
        both = _sum_blocks("grad_pair_sum_" + n, [own.reshape(-1, cols), s.reshape(-1, cols)], BF16)
        out.append(both.reshape(N_CHIPS, rh, cols))
    return out


def _chip_sums(names, pair, others):
    chip = 2 * lax.axis_index("x") + lax.axis_index("y")
    return [_sum_blocks("grad_chip_sum_" + n, [lax.dynamic_index_in_dim(h, chip, axis=0, keepdims=False), o[0], o[1], o[2]],
                        F32) for n, h, o in zip(names, pair, others)]


def _join_halves(halves, sibs):
    ci = lax.axis_index("c")
    return [lax.dynamic_update_slice_in_dim(jnp.concatenate([own, own], axis=0), s, (1 - ci) * own.shape[0], axis=0)
            for own, s in zip(halves, sibs)]


def _gather_all(v):
    m_per, n = v.shape

    def body(x_ref, out_ref, send_sems, recv_sems):
        x, y, c, chips = _place()
        me, sibling = (x, y, c), (x, y, 1 - c)

        def rows(px, py, pc):
            return out_ref.at[pl.ds(pl.multiple_of((4 * px + 2 * py + pc) * m_per, 8), m_per), :]

        def copy(k, block, to, src=None):
            return pltpu.make_async_remote_copy(src_ref=rows(*block) if src is None else src, dst_ref=rows(*block),
                                                send_sem=send_sems.at[k], recv_sem=recv_sems.at[k], device_id=to,
                                                device_id_type=MESH)

        out_ref[pl.ds(pl.multiple_of((4 * x + 2 * y + c) * m_per, 8), m_per), :] = x_ref[...]
        first = [copy(0, me, sibling, src=x_ref)]
        first += [copy(1 + j, me, (*chip, c), src=x_ref) for j, chip in enumerate(chips)]
        for cp in first:
            cp.start()
        passed = [copy(4 + j, (*chip, c), sibling) for j, chip in enumerate(chips)]
        for j, chip in enumerate(chips):
            copy(1 + j, (*chip, c), me).wait_recv()
            passed[j].start()
        copy(0, sibling, me).wait_recv()
        for j, chip in enumerate(chips):
            copy(4 + j, (*chip, 1 - c), me).wait_recv()
        for cp in first + passed:
            cp.wait_send()

    return pl.pallas_call(
        body, name="gather_small_grads", out_shape=jax.ShapeDtypeStruct((N_DEV * m_per, n), v.dtype),
        in_specs=[pl.BlockSpec(memory_space=pltpu.VMEM)], out_specs=pl.BlockSpec(memory_space=pltpu.VMEM),
        scratch_shapes=[pltpu.SemaphoreType.DMA((7,)), pltpu.SemaphoreType.DMA((7,))],
        compiler_params=pltpu.CompilerParams(vmem_limit_bytes=VMEM_LIMIT_BYTES),
    )(v)


def _sum_blocks(name, parts, out_dtype):
    rows, cols = parts[0].shape
    tm = _row_tile(rows, 512)

    def body(*refs):
        acc = refs[0][...].astype(F32)
        for r in refs[1:-1]:
            acc = acc + r[...].astype(F32)
        refs[-1][...] = acc.astype(refs[-1].dtype)

    spec = pl.BlockSpec((tm, cols), lambda i: (i, 0))
    return pl.pallas_call(
        body, name=name, grid=(rows // tm,), in_specs=[spec] * len(parts), out_specs=spec,
        out_shape=jax.ShapeDtypeStruct((rows, cols), out_dtype), compiler_params=_params("arbitrary"),
    )(*parts)


def _adamw_math(wv, gv, mv, vv):
    m2 = ADAM_B1 * mv + (1.0 - ADAM_B1) * gv
    v2 = ADAM_B2 * vv + (1.0 - ADAM_B2) * (gv * gv)
    delta = -ADAM_LR * ((m2 / (1.0 - ADAM_B1 ** ADAM_STEP)) / (jnp.sqrt(v2 / (1.0 - ADAM_B2 ** ADAM_STEP)) + ADAM_EPS)
                        + ADAM_WD * wv)
    return delta, m2, v2


def _adamw_small(ws, gs, ms, vs):
    n = len(ws)

    def body(*refs):
        for i in range(n):
            res = _adamw_math(refs[i][...], refs[n + i][...], refs[2 * n + i][...], refs[3 * n + i][...])
            for k in range(3):
                refs[(4 + k) * n + i][...] = res[k]

    vm = pl.BlockSpec(memory_space=pltpu.VMEM)
    outs = pl.pallas_call(
        body, name="adamw_small", in_specs=[vm] * (4 * n), out_specs=[vm] * (3 * n),
        out_shape=[jax.ShapeDtypeStruct(a.shape, F32) for a in ws] * 3,
        compiler_params=pltpu.CompilerParams(vmem_limit_bytes=VMEM_LIMIT_BYTES),
    )(*ws, *gs, *ms, *vs)
    return outs[:n], outs[n:2 * n], outs[2 * n:]


def _adamw(name, w, g, m, v):
    rows, cols = w.shape
    ins = [(a, cols, 0, "row") for a in (w, g, m, v)]
    return _rowwise(name, _adamw_math, ins, [(cols, cols, F32)] * 3, rows=rows, tm=_row_tile(rows, 256))


PACK_ROWS = 256


def _pack(flat_parts, dtype, lead=()):
    parts = [a.astype(dtype).reshape(lead + (-1,)) for a in flat_parts]
    n = sum(a.shape[-1] for a in parts)
    chunk = PACK_ROWS * LANES
    total = -(-n // chunk) * chunk
    if total > n:
        parts.append(jnp.zeros(lead + (total - n,), dtype))
    return jnp.concatenate(parts, axis=-1).reshape(lead + (total // LANES, LANES))


def _unpack(buf, shapes, lead=()):
    flat = buf.reshape(lead + (-1,))
    out, off = [], 0
    for shp in shapes:
        n = math.prod(shp)
        out.append(lax.slice_in_dim(flat, off, off + n, axis=len(lead)).reshape(lead + tuple(shp)))
        off += n
    return out


BIG = ("w_in", "w_glu", "w_pa", "w_pb", "w_out", "w_up", "w_down")
WEIGHTS = ("g_mix", "w_in", "s5_a_re", "s5_a_im", "s5_log_dt", "s5_b_re", "s5_b_im", "s5_c_re", "s5_c_im", "s5_d",
           "w_glu", "b_glu", "hg_lb_logits", "hg_norm_gain", "w_pa", "w_pb", "w_out", "g_ffn", "w_up", "w_conv",
           "b_conv", "w_down", "g_final")
SMALL = tuple(n for n in WEIGHTS if n not in BIG)
SMALL_PARTS = ("loss", "g_mix", "g_ffn", "g_final", "b_glu", "gain", "lbrow", "s5_d", "w_conv", "b_conv", "lam_re",
               "lam_im", "bb_re", "bb_im", "s5_c_re", "s5_c_im")


def _lower_bound(logits):
    return jnp.cumsum(jax.nn.softmax(logits, axis=0), axis=0)[0:1]


def kernel(x, g_mix, w_in, s5_a_re, s5_a_im, s5_log_dt, s5_b_re, s5_b_im, s5_c_re, s5_c_im, s5_d, w_glu, b_glu, hg_lb_logits, hg_norm_gain, w_pa, w_pb, w_out, g_ffn, w_up, w_conv, b_conv, w_down, g_final, loss_target, m_g_mix, m_w_in, m_s5_a_re, m_s5_a_im, m_s5_log_dt, m_s5_b_re, m_s5_b_im, m_s5_c_re, m_s5_c_im, m_s5_d, m_w_glu, m_b_glu, m_hg_lb_logits, m_hg_norm_gain, m_w_pa, m_w_pb, m_w_out, m_g_ffn, m_w_up, m_w_conv, m_b_conv, m_w_down, m_g_final, v_g_mix, v_w_in, v_s5_a_re, v_s5_a_im, v_s5_log_dt, v_s5_b_re, v_s5_b_im, v_s5_c_re, v_s5_c_im, v_s5_d, v_w_glu, v_b_glu, v_hg_lb_logits, v_hg_norm_gain, v_w_pa, v_w_pb, v_w_out, v_g_ffn, v_w_up, v_w_conv, v_b_conv, v_w_down, v_g_final):
    args = dict(locals())
    w = {n: args[n] for n in WEIGHTS}
    mom = {n: args["m_" + n] for n in WEIGHTS}
    var = {n: args["v_" + n] for n in WEIGHTS}
    nseq, seq, d = x.shape
    xi, yi = lax.axis_index("x"), lax.axis_index("y")
    chip = 2 * xi + yi

    shard = {n: w[n][0] for n in BIG}
    shard16 = {n: shard[n].astype(BF16) for n in BIG}
    first = ("w_in", "w_glu")
    got = _gather_weights([shard16[n] for n in first], [w_conv[0]])
    p = {n: lax.dynamic_update_index_in_dim(g, shard16[n], chip, 0) for n, g in zip(first, got)}
    p["w_glu"] = p["w_glu"].reshape(-1, p["w_glu"].shape[-1])
    conv_all = lax.dynamic_update_index_in_dim(got[-1], w_conv[0], chip, 0)
    p.update(g_mix=g_mix, g_ffn=g_ffn, g_final=g_final.reshape(1, -1), b_glu=b_glu, gain=hg_norm_gain, s5_d=s5_d,
             b_conv=b_conv, w_conv=conv_all.transpose(1, 0, 2).reshape(CONV_W, -1), lbrow=_lower_bound(hg_lb_logits),
             s5_a_re=s5_a_re[0], s5_a_im=s5_a_im[0], s5_log_dt=s5_log_dt[0], s5_b_re=s5_b_re[0], s5_b_im=s5_b_im[0],
             s5_c_re=s5_c_re[0], s5_c_im=s5_c_im[0])

    dx, grads, gsmall = _local_step(x.reshape(nseq * seq, d), loss_target.reshape(nseq * seq, d), p,
                                    {n: shard16[n] for n in LATE}, nseq=nseq, seq=seq)

    small_shapes = [gsmall[n].shape for n in SMALL_PARTS]
    vec = _pack([gsmall[n] for n in SMALL_PARTS], F32)
    gathered = _gather_all(vec)
    mrows = vec.shape[0]
    vsum = _sum_blocks("small_grad_sum", [gathered[i * mrows:(i + 1) * mrows] for i in range(N_DEV)], F32)
    sm = dict(zip(SMALL_PARTS, _unpack(vsum, small_shapes)))
    loss = sm["loss"][0, 0]

    _, disc_vjp = jax.vjp(_s5_discretize, p["s5_a_re"], p["s5_a_im"], p["s5_log_dt"], p["s5_b_re"], p["s5_b_im"])
    da_re, da_im, dlog_dt, db_re, db_im = disc_vjp((sm["lam_re"], sm["lam_im"], sm["bb_re"], sm["bb_im"]))
    _, lb_vjp = jax.vjp(_lower_bound, hg_lb_logits)
    (dlogits,) = lb_vjp(sm["lbrow"])
    fcols = w_conv.shape[-1]
    grads.update(
        g_mix=sm["g_mix"], g_ffn=sm["g_ffn"], g_final=sm["g_final"].reshape(-1), b_glu=sm["b_glu"],
        hg_norm_gain=sm["gain"], hg_lb_logits=dlogits, s5_d=sm["s5_d"], b_conv=sm["b_conv"],
        w_conv=lax.dynamic_slice_in_dim(sm["w_conv"], chip * fcols, fcols, axis=1),
        s5_a_re=da_re, s5_a_im=da_im, s5_log_dt=dlog_dt, s5_b_re=db_re, s5_b_im=db_im,
        s5_c_re=sm["s5_c_re"], s5_c_im=sm["s5_c_im"])
    grads = {n: grads[n].reshape(w[n].shape) for n in WEIGHTS}

    delta, new_m, new_v = {}, {}, {}
    for n in BIG:
        shp = shard[n].shape
        dl, m2, v2 = _adamw("adamw_" + n, shard[n], grads[n].reshape(shp), mom[n].reshape(shp), var[n].reshape(shp))
        delta[n], new_m[n], new_v[n] = dl, m2, v2
    def natural(a):
        return a.reshape(1, -1) if a.ndim == 1 else (a[0] if a.ndim > 2 else a)

    outs = _adamw_small(*[[natural(src[n]) for n in SMALL] for src in (w, grads, mom, var)])
    for dst, group in zip((delta, new_m, new_v), outs):
        dst.update(zip(SMALL, group))
    res = [loss, dx.reshape(x.shape)]
    for group in (grads, delta, new_m, new_v):
        res += [group[n].reshape(w[n].shape) for n in WEIGHTS]
    return tuple(res)
```

```python
import functools
import math

import jax
import jax.numpy as jnp
from jax import lax
from jax.experimental import pallas as pl
from jax.experimental.pallas import tpu as pltpu

F32 = jnp.float32
BF16 = jnp.bfloat16
MESH = pl.DeviceIdType.MESH

EPS = 1e-6
S5_GROUP = 16
S5_STATE = 64
S5_BLOCK_GROUPS = 8
HEAD = 128
CHUNK = 64
CONV_W = 3
LANES = 128
SUBLANES = 8
GATE_BLOCK = 512
VMEM_LIMIT_BYTES = 56 * 1024 * 1024

ADAM_LR = 0.001
ADAM_B1 = 0.9
ADAM_B2 = 0.999
ADAM_EPS = 1e-08
ADAM_WD = 0.01
ADAM_STEP = 10

N_CHIPS = 4
N_DEV = 8


def _params(*sem):
    return pltpu.CompilerParams(dimension_semantics=sem, vmem_limit_bytes=VMEM_LIMIT_BYTES)


class _Rider:
    def __init__(self, arrays, out_shapes, nsem, start, finish, aliases=None):
        self.arrays, self.out_shapes, self.nsem = list(arrays), list(out_shapes), nsem
        self.start, self.finish, self.aliases = start, finish, dict(aliases or {})


def _hosted_call(name, body, *, grid, in_specs, out_specs, out_shape, operands, scratch_shapes=(), rider=None):
    in_specs, out_specs, out_shape, scratch_shapes = list(in_specs), list(out_specs), list(out_shape), list(scratch_shapes)
    cparams = _params(*(["arbitrary"] * len(grid)))
    if rider is None:
        return pl.pallas_call(body, name=name, grid=grid, in_specs=in_specs, out_specs=out_specs, out_shape=out_shape,
                              scratch_shapes=scratch_shapes, compiler_params=cparams)(*operands)
    n_in, n_out, n_sc = len(in_specs), len(out_specs), len(scratch_shapes)
    r_in, r_out = len(rider.arrays), len(rider.out_shapes)

    def hosted(*refs):
        ins, rins = refs[:n_in], refs[n_in:n_in + r_in]
        outs = refs[n_in + r_in:n_in + r_in + n_out]
        routs = refs[n_in + r_in + n_out:n_in + r_in + n_out + r_out]
        rest = refs[n_in + r_in + n_out + r_out:]
        send_sems, recv_sems = rest[n_sc], rest[n_sc + 1]
        first = functools.reduce(jnp.logical_and, [pl.program_id(i) == 0 for i in range(len(grid))])
        last = functools.reduce(jnp.logical_and, [pl.program_id(i) == grid[i] - 1 for i in range(len(grid))])

        @pl.when(first)
        def _():
            rider.start(rins, routs, send_sems, recv_sems)

        body(*ins, *outs, *rest[:n_sc])

        @pl.when(last)
        def _():
            rider.finish(rins, routs, send_sems, recv_sems)

    res = pl.pallas_call(
        hosted, name=name, grid=grid, in_specs=in_specs + [ANY] * r_in, out_specs=out_specs + [ANY] * r_out,
        out_shape=out_shape + rider.out_shapes,
        scratch_shapes=scratch_shapes + [pltpu.SemaphoreType.DMA((rider.nsem,)), pltpu.SemaphoreType.DMA((rider.nsem,))],
        input_output_aliases={n_in + i: n_out + o for i, o in rider.aliases.items()}, compiler_params=cparams,
    )(*operands, *rider.arrays)
    return res[:n_out], res[n_out:]


def _run_rider(name, rider):
    r_in, r_out = len(rider.arrays), len(rider.out_shapes)

    def body(*refs):
        rins, routs, send_sems, recv_sems = refs[:r_in], refs[r_in:r_in + r_out], refs[-2], refs[-1]
        rider.start(rins, routs, send_sems, recv_sems)
        rider.finish(rins, routs, send_sems, recv_sems)

    return pl.pallas_call(
        body, name=name, in_specs=[ANY] * r_in, out_specs=[ANY] * r_out, out_shape=rider.out_shapes,
        scratch_shapes=[pltpu.SemaphoreType.DMA((rider.nsem,)), pltpu.SemaphoreType.DMA((rider.nsem,))],
        input_output_aliases=rider.aliases,
    )(*rider.arrays)


def _row_tile(rows, cap):
    if rows <= cap:
        return rows
    for t in range(cap - cap % 8, 7, -8):
        if rows % t == 0:
            return t
    raise ValueError(f"no row tile for {rows}")


def _dot(a, b):
    return jnp.dot(a.astype(BF16), b.astype(BF16), preferred_element_type=F32)


def _dot_nt(a, b):
    return lax.dot_general(a.astype(BF16), b.astype(BF16), (((1,), (1,)), ((), ())), preferred_element_type=F32)


def _dot_tn(a, b):
    return lax.dot_general(a.astype(BF16), b.astype(BF16), (((0,), (0,)), ((), ())), preferred_element_type=F32)


def _sigmoid(x):
    return 1.0 / (1.0 + jnp.exp(-x))


_GELU_C = math.sqrt(2.0 / math.pi)


def _gelu(x):
    return 0.5 * x * (1.0 + jnp.tanh(_GELU_C * (x + 0.044715 * x * x * x)))


def _gelu_grad(x):
    th = jnp.tanh(_GELU_C * (x + 0.044715 * x * x * x))
    return 0.5 * (1.0 + th) + 0.5 * x * (1.0 - th * th) * _GELU_C * (1.0 + 3.0 * 0.044715 * x * x)


def _rowwise(name, fn, ins, outs, accs=(), *, rows, tm, ncol=1):
    n_in, n_out = len(ins), len(outs)

    def body(*refs):
        res = fn(*[r[...] for r in refs[:n_in]])
        for r, v in zip(refs[n_in:n_in + n_out], res[:n_out]):
            r[...] = v.astype(r.dtype)
        first = pl.program_id(1) == 0
        for r, v in zip(refs[n_in + n_out:], res[n_out:]):
            @pl.when(first)
            def _():
                r[...] = v

            @pl.when(jnp.logical_not(first))
            def _():
                r[...] += v

    in_specs = []
    for _, width, base, kind in ins:
        if kind == "row":
            in_specs.append(pl.BlockSpec((tm, width), lambda j, i, b=base: (i, b + j)))
        else:
            in_specs.append(pl.BlockSpec((1, width), lambda j, i, b=base: (0, b + j)))
    out_specs = [pl.BlockSpec((tm, width), lambda j, i: (i, j)) for _, width, _ in outs]
    out_specs += [pl.BlockSpec((1, width), lambda j, i: (0, j)) for _, width in accs]
    out_shape = [jax.ShapeDtypeStruct((rows, total), dt) for total, _, dt in outs]
    out_shape += [jax.ShapeDtypeStruct((1, total), F32) for total, _ in accs]
    return pl.pallas_call(
        body, name=name, grid=(ncol, rows // tm), in_specs=in_specs, out_specs=out_specs, out_shape=out_shape,
        compiler_params=_params("arbitrary", "arbitrary"),
    )(*[a for a, _, _, _ in ins])


def _mm(name, a, b, *, mode, grid, a_spec, b_spec, o_spec, out_shape, acc_shape, res=None, res_spec=None,
        pair_axis=None, rider=None):
    nk = grid[2]
    dot = {"nn": _dot, "nt": _dot_nt, "tn": _dot_tn}[mode]
    a_list = list(a) if isinstance(a, tuple) else [a]
    b_list = list(b) if isinstance(b, tuple) else [b]
    na, nb = len(a_list), len(b_list)
    assert (pair_axis is None) == (na + nb == 2)
    direct = nk == 1 and pair_axis is None

    def body(*refs):
        a_refs, b_refs = refs[:na], refs[na:na + nb]
        r_ref = None if res is None else refs[na + nb]
        o_ref = refs[na + nb + (0 if res is None else 1)]

        def finish(v):
            if res is not None:
                v = v + r_ref[...]
            o_ref[...] = v.astype(o_ref.dtype)

        if direct:
            finish(dot(a_refs[0][...], b_refs[0][...]))
            return
        acc_ref = refs[-1]
        k = pl.program_id(2)

        @pl.when(k == 0)
        def _():
            acc_ref[...] = jnp.zeros_like(acc_ref)

        if pair_axis is None:
            acc_ref[...] += dot(a_refs[0][...], b_refs[0][...])
        else:
            lower = pl.program_id(pair_axis) < grid[pair_axis] // 2

            @pl.when(lower)
            def _():
                acc_ref[...] += dot(a_refs[0][...], b_refs[0][...])

            @pl.when(jnp.logical_not(lower))
            def _():
                acc_ref[...] += dot(a_refs[-1][...], b_refs[-1][...])

        @pl.when(k == nk - 1)
        def _():
            finish(acc_ref[...])

    operands = a_list + b_list + ([] if res is None else [res])
    in_specs = (list(a_spec) if na == 2 else [a_spec]) + (list(b_spec) if nb == 2 else [b_spec])
    in_specs += [] if res is None else [res_spec]
    got = _hosted_call(name, body, grid=grid, in_specs=in_specs, out_specs=[o_spec], out_shape=[out_shape],
                       scratch_shapes=[] if direct else [pltpu.VMEM(acc_shape, F32)], operands=operands, rider=rider)
    return got[0] if rider is None else (got[0][0], got[1])


MM_TILE_BUDGET_BYTES = 36 * 1024 * 1024
MM_TILE_CAP = 1024


def _mm_tile(t, row_bytes, fixed_bytes):
    cap = max(16, min(MM_TILE_CAP, (MM_TILE_BUDGET_BYTES - fixed_bytes) // row_bytes))
    return _row_tile(t, cap - cap % 16)


def _size(a):
    return jnp.dtype(a.dtype).itemsize


def _mm_fwd_cols(name, a, w3, out_dtype=F32):
    t, k = a.shape
    ns = w3.shape[2]
    tm = _mm_tile(t, 2 * k * _size(a) + 2 * ns * jnp.dtype(out_dtype).itemsize, 2 * k * ns * _size(w3))
    return _mm(name, a, w3, mode="nn", grid=(N_CHIPS, t // tm, 1),
               a_spec=pl.BlockSpec((tm, k), lambda j, i, kk: (i, 0)),
               b_spec=pl.BlockSpec((None, k, ns), lambda j, i, kk: (j, 0, 0)),
               o_spec=pl.BlockSpec((tm, ns), lambda j, i, kk: (i, j)),
               out_shape=jax.ShapeDtypeStruct((t, N_CHIPS * ns), out_dtype), acc_shape=(tm, ns))


def _mm_bwd_cols(name, d, w3, out_dtype=F32, rider=None):
    pair = isinstance(d, tuple)
    t = d[0].shape[0] if pair else d.shape[0]
    k, ns = w3.shape[1], w3.shape[2]
    dsize = _size(d[0] if pair else d)
    tm = _mm_tile(t, (4 if pair else 2) * ns * dsize + 2 * k * jnp.dtype(out_dtype).itemsize + 4 * k,
                  2 * k * ns * _size(w3))
    half = N_CHIPS // 2
    if pair:
        a_spec = (pl.BlockSpec((tm, ns), lambda i, j, kk: (i, jnp.minimum(kk, half - 1))),
                  pl.BlockSpec((tm, ns), lambda i, j, kk: (i, jnp.maximum(kk - half, 0))))
    else:
        a_spec = pl.BlockSpec((tm, ns), lambda i, j, kk: (i, kk))
    return _mm(name, d, w3, mode="nt", grid=(t // tm, 1, N_CHIPS), a_spec=a_spec,
               b_spec=pl.BlockSpec((None, k, ns), lambda i, j, kk: (kk, 0, 0)),
               o_spec=pl.BlockSpec((tm, k), lambda i, j, kk: (i, 0)),
               out_shape=jax.ShapeDtypeStruct((t, k), out_dtype), acc_shape=(tm, k), pair_axis=2 if pair else None,
               rider=rider)


def _mm_wgrad_cols(name, a, d):
    pair = isinstance(d, tuple)
    t, k = a.shape
    ns = (2 * d[0].shape[1] if pair else d.shape[1]) // N_CHIPS
    dsize = _size(d[0] if pair else d)
    tk = _mm_tile(t, 2 * k * _size(a) + (4 if pair else 2) * ns * dsize, k * ns * (4 + 2 * 2))
    half = N_CHIPS // 2
    if pair:
        b_spec = (pl.BlockSpec((tk, ns), lambda j, i, kk: (jnp.where(j < half, kk, 0), jnp.minimum(j, half - 1))),
                  pl.BlockSpec((tk, ns), lambda j, i, kk: (jnp.where(j < half, 0, kk), jnp.maximum(j - half, 0))))
    else:
        b_spec = pl.BlockSpec((tk, ns), lambda j, i, kk: (kk, j))
    return _mm(name, a, d, mode="tn", grid=(N_CHIPS, 1, t // tk),
               a_spec=pl.BlockSpec((tk, k), lambda j, i, kk: (kk, 0)), b_spec=b_spec,
               o_spec=pl.BlockSpec((None, k, ns), lambda j, i, kk: (j, 0, 0)),
               out_shape=jax.ShapeDtypeStruct((N_CHIPS, k, ns), BF16), acc_shape=(k, ns),
               pair_axis=0 if pair else None)


MM_BLOCK_CAP = 1408


def _mm_fwd_rows(name, a, w, res=None, out_dtype=F32):
    t, k = a.shape
    n = w.shape[1]
    tk = k if k <= MM_BLOCK_CAP else MM_BLOCK_CAP
    assert k % tk == 0
    row_bytes = 2 * tk * _size(a) + 2 * n * jnp.dtype(out_dtype).itemsize + (0 if res is None else 2 * n * 4) + 4 * n
    tm = _mm_tile(t, row_bytes, 2 * tk * n * _size(w))
    return _mm(name, a, w, mode="nn", grid=(t // tm, 1, k // tk),
               a_spec=pl.BlockSpec((tm, tk), lambda i, j, kk: (i, kk)),
               b_spec=pl.BlockSpec((tk, n), lambda i, j, kk: (kk, 0)),
               o_spec=pl.BlockSpec((tm, n), lambda i, j, kk: (i, 0)),
               out_shape=jax.ShapeDtypeStruct((t, n), out_dtype), acc_shape=(tm, n),
               res=res, res_spec=None if res is None else pl.BlockSpec((tm, n), lambda i, j, kk: (i, 0)))


def _mm_bwd_rows(name, d, w, out_dtype=F32):
    t, n = d.shape
    k = w.shape[0]
    tn = k if k <= MM_BLOCK_CAP else MM_BLOCK_CAP
    assert k % tn == 0
    tm = _mm_tile(t, 2 * n * _size(d) + 2 * tn * jnp.dtype(out_dtype).itemsize, 2 * tn * n * _size(w))
    return _mm(name, d, w, mode="nt", grid=(t // tm, k // tn, 1),
               a_spec=pl.BlockSpec((tm, n), lambda i, j, kk: (i, 0)),
               b_spec=pl.BlockSpec((tn, n), lambda i, j, kk: (j, 0)),
               o_spec=pl.BlockSpec((tm, tn), lambda i, j, kk: (i, j)),
               out_shape=jax.ShapeDtypeStruct((t, k), out_dtype), acc_shape=(tm, tn))


def _mm_wgrad_rows(name, a, d):
    t, k = a.shape
    n = d.shape[1]
    nblk = next(b for b in (1, 2, 4) if (k // b) % LANES == 0 and k // b <= MM_BLOCK_CAP)
    ks = k // nblk
    tk = _mm_tile(t, 2 * ks * _size(a) + 2 * n * _size(d), ks * n * (4 + 2 * 2))
    return _mm(name, a, d, mode="tn", grid=(nblk, 1, t // tk),
               a_spec=pl.BlockSpec((tk, ks), lambda j, i, kk: (kk, j)),
               b_spec=pl.BlockSpec((tk, n), lambda j, i, kk: (kk, 0)),
               o_spec=pl.BlockSpec((ks, n), lambda j, i, kk: (j, 0)),
               out_shape=jax.ShapeDtypeStruct((k, n), BF16), acc_shape=(ks, n))


def _s5_discretize(a_re, a_im, log_dt, b_re, b_im):
    dt = jnp.exp(log_dt)[:, None]
    mag = jnp.exp(a_re * dt)
    ang = a_im * dt
    lb_re = mag * jnp.cos(ang)
    lb_im = mag * jnp.sin(ang)
    den = a_re * a_re + a_im * a_im
    n_re = lb_re - 1.0
    n_im = lb_im
    co_re = ((n_re * a_re + n_im * a_im) / den)[..., None]
    co_im = ((n_im * a_re - n_re * a_im) / den)[..., None]
    bb_re = co_re * b_re - co_im * b_im
    bb_im = co_re * b_im + co_im * b_re
    return lb_re, lb_im, bb_re, bb_im


def _s5_in_blocks(bb):
    g = bb.shape[0]
    nb = g // S5_BLOCK_GROUPS
    t = bb.reshape(nb, S5_BLOCK_GROUPS, S5_STATE, S5_GROUP).transpose(0, 1, 3, 2)
    eye = jnp.eye(S5_BLOCK_GROUPS, dtype=bb.dtype)
    full = t[:, :, :, None, :] * eye[None, :, None, :, None]
    return full.reshape(nb, S5_BLOCK_GROUPS * S5_GROUP, S5_BLOCK_GROUPS * S5_STATE)


def _s5_in_blocks_diag(blocks):
    nb = blocks.shape[0]
    t = blocks.reshape(nb, S5_BLOCK_GROUPS, S5_GROUP, S5_BLOCK_GROUPS, S5_STATE)
    d = jnp.einsum("bghgp->bghp", t)
    return d.transpose(0, 1, 3, 2).reshape(nb * S5_BLOCK_GROUPS, S5_STATE, S5_GROUP)


def _s5_out_blocks(c):
    g = c.shape[0]
    nb = g // S5_BLOCK_GROUPS
    t = c.reshape(nb, S5_BLOCK_GROUPS, S5_GROUP, S5_STATE).transpose(0, 1, 3, 2)
    eye = jnp.eye(S5_BLOCK_GROUPS, dtype=c.dtype)
    full = t[:, :, :, None, :] * eye[None, :, None, :, None]
    return full.reshape(nb, S5_BLOCK_GROUPS * S5_STATE, S5_BLOCK_GROUPS * S5_GROUP)


def _s5_out_blocks_diag(blocks):
    nb = blocks.shape[0]
    t = blocks.reshape(nb, S5_BLOCK_GROUPS, S5_STATE, S5_BLOCK_GROUPS, S5_GROUP)
    d = jnp.einsum("bgpgh->bgph", t)
    return d.transpose(0, 1, 3, 2).reshape(nb * S5_BLOCK_GROUPS, S5_GROUP, S5_STATE)


def _s5_scan_tables(lr, li, reverse):
    def cmul(a, b):
        return a[0] * b[0] - a[1] * b[1], a[0] * b[1] + a[1] * b[0]

    lam = (lr, -li) if reverse else (lr, li)
    pw = [lam]
    for _ in range(SUBLANES - 1):
        pw.append(cmul(pw[-1], lam))
    sub = jnp.arange(SUBLANES)[:, None]
    rows = []
    for s in (1, 2, 4):
        keep = (sub <= SUBLANES - 1 - s) if reverse else (sub >= s)
        rows.append(jnp.where(keep, pw[s - 1][0][None, :], 0.0))
        rows.append(jnp.where(keep, pw[s - 1][1][None, :], 0.0))
    order = list(range(SUBLANES - 1, -1, -1)) if reverse else list(range(SUBLANES))
    rows.append(jnp.stack([pw[i][0] for i in order]))
    rows.append(jnp.stack([pw[i][1] for i in order]))
    return jnp.concatenate(rows, axis=0)


def _s5_scan(vre_ref, vim_ref, coef_ref, seq, width, reverse, xre_ref=None, xim_ref=None):
    nt = seq // SUBLANES
    nl = width // LANES
    per = 2 if xre_ref is None else 4
    sub = lax.broadcasted_iota(jnp.int32, (SUBLANES, LANES), 0)

    def step(k, carry):
        kk = (nt - 1 - k) if reverse else k
        rows = pl.ds(pl.multiple_of(kk * SUBLANES, SUBLANES), SUBLANES)
        out = []
        for j in range(nl):
            lanes = slice(j * LANES, (j + 1) * LANES)
            co = [coef_ref[SUBLANES * q:SUBLANES * (q + 1), lanes] for q in range(8)]
            cr, ci = carry[per * j], carry[per * j + 1]
            vr = vre_ref[rows, lanes]
            vi = vim_ref[rows, lanes]
            for q, s in enumerate((1, 2, 4)):
                sh = SUBLANES - s if reverse else s
                rr = pltpu.roll(vr, sh, 0)
                ri = pltpu.roll(vi, sh, 0)
                ar, ai = co[2 * q], co[2 * q + 1]
                vr, vi = vr + ar * rr - ai * ri, vi + ar * ri + ai * rr
            edge = 0 if reverse else SUBLANES - 1
            cbr = jnp.broadcast_to(cr[edge:edge + 1, :], (SUBLANES, LANES))
            cbi = jnp.broadcast_to(ci[edge:edge + 1, :], (SUBLANES, LANES))
            pr, pi = co[6], co[7]
            vr, vi = vr + pr * cbr - pi * cbi, vi + pr * cbi + pi * cbr
            vre_ref[rows, lanes] = vr
            vim_ref[rows, lanes] = vi
            out += [vr, vi]
            if xre_ref is not None:
                nr = jnp.where(sub == SUBLANES - 1, cbr, pltpu.roll(vr, SUBLANES - 1, 0))
                ni = jnp.where(sub == SUBLANES - 1, cbi, pltpu.roll(vi, SUBLANES - 1, 0))
                xr = xre_ref[rows, lanes]
                xi = xim_ref[rows, lanes]
                out += [carry[per * j + 2] + nr * xr + ni * xi, carry[per * j + 3] + ni * xr - nr * xi]
        return tuple(out)

    zero = jnp.zeros((SUBLANES, LANES), F32)
    res = lax.fori_loop(0, nt, step, (zero,) * (per * nl))
    if xre_ref is None:
        return None
    return jnp.concatenate(
        [jnp.concatenate([jnp.sum(res[per * j + 2], axis=0, keepdims=True) for j in range(nl)], axis=1),
         jnp.concatenate([jnp.sum(res[per * j + 3], axis=0, keepdims=True) for j in range(nl)], axis=1)], axis=0)


def _s5_fwd(z, bre3, bim3, cre3, cim3, coef, dskip, *, nseq, seq, rider=None):
    nb = bre3.shape[0]
    ch, ns = bre3.shape[1], bre3.shape[2]

    def body(za_ref, bre_ref, bim_ref, cre_ref, cim_ref, coef_ref, d_ref, y_ref, xre_ref, xim_ref):
        za = za_ref[...]
        xre_ref[...] = _dot(za, bre_ref[...])
        xim_ref[...] = _dot(za, bim_ref[...])
        _s5_scan(xre_ref, xim_ref, coef_ref, seq, ns, False)
        y_ref[...] = _dot(xre_ref[...], cre_ref[...]) - _dot(xim_ref[...], cim_ref[...]) + d_ref[...] * za

    blk3 = lambda r, c: pl.BlockSpec((None, r, c), lambda b, j: (j, 0, 0))
    return _hosted_call(
        "s5_fwd", body, grid=(nseq, nb),
        in_specs=[pl.BlockSpec((seq, ch), lambda b, j: (b, j)), blk3(ch, ns), blk3(ch, ns), blk3(ns, ch), blk3(ns, ch),
                  pl.BlockSpec((8 * SUBLANES, ns), lambda b, j: (0, j)), pl.BlockSpec((1, ch), lambda b, j: (0, j))],
        out_specs=[pl.BlockSpec((seq, ch), lambda b, j: (b, j)), pl.BlockSpec((seq, ns), lambda b, j: (b, j)),
                   pl.BlockSpec((seq, ns), lambda b, j: (b, j))],
        out_shape=[jax.ShapeDtypeStruct((nseq * seq, nb * ch), F32), jax.ShapeDtypeStruct((nseq * seq, nb * ns), F32),
                   jax.ShapeDtypeStruct((nseq * seq, nb * ns), F32)],
        operands=(z, bre3, bim3, cre3, cim3, coef, dskip), rider=rider)


def _s5_bwd(dy, z, xre, xim, bre3, bim3, cre3, cim3, coef_rev, dskip, *, nseq, seq, rider=None):
    nb = bre3.shape[0]
    ch, ns = bre3.shape[1], bre3.shape[2]

    def body(dy_ref, za_ref, xre_ref, xim_ref, bre_ref, bim_ref, cre_ref, cim_ref, coef_ref, d_ref,
             dza_ref, dbre_ref, dbim_ref, dcre_ref, dcim_ref, dlam_ref, dd_ref, are_ref, aim_ref):
        dy = dy_ref[...]
        za = za_ref[...]
        are_ref[...] = _dot_nt(dy, cre_ref[...])
        aim_ref[...] = -_dot_nt(dy, cim_ref[...])
        dlam = _s5_scan(are_ref, aim_ref, coef_ref, seq, ns, True, xre_ref, xim_ref)
        are = are_ref[...]
        aim = aim_ref[...]
        dza_ref[...] = (_dot_nt(are, bre_ref[...]) + _dot_nt(aim, bim_ref[...]) + d_ref[...] * dy).astype(dza_ref.dtype)
        parts = (_dot_tn(za, are), _dot_tn(za, aim), _dot_tn(xre_ref[...], dy), -_dot_tn(xim_ref[...], dy),
                 dlam, jnp.sum(dy * za, axis=0, keepdims=True))
        first = pl.program_id(1) == 0
        for r, v in zip((dbre_ref, dbim_ref, dcre_ref, dcim_ref, dlam_ref, dd_ref), parts):
            @pl.when(first)
            def _():
                r[...] = v

            @pl.when(jnp.logical_not(first))
            def _():
                r[...] += v

    blk3 = lambda r, c: pl.BlockSpec((None, r, c), lambda j, b: (j, 0, 0))
    tok = lambda c: pl.BlockSpec((seq, c), lambda j, b: (b, j))
    return _hosted_call(
        "s5_bwd", body, grid=(nb, nseq),
        in_specs=[tok(ch), tok(ch), tok(ns), tok(ns), blk3(ch, ns), blk3(ch, ns), blk3(ns, ch), blk3(ns, ch),
                  pl.BlockSpec((8 * SUBLANES, ns), lambda j, b: (0, j)), pl.BlockSpec((1, ch), lambda j, b: (0, j))],
        out_specs=[tok(ch), blk3(ch, ns), blk3(ch, ns), blk3(ns, ch), blk3(ns, ch),
                   pl.BlockSpec((None, 2, ns), lambda j, b: (j, 0, 0)), pl.BlockSpec((1, ch), lambda j, b: (0, j))],
        out_shape=[jax.ShapeDtypeStruct((nseq * seq, nb * ch), BF16),
                   jax.ShapeDtypeStruct((nb, ch, ns), F32), jax.ShapeDtypeStruct((nb, ch, ns), F32),
                   jax.ShapeDtypeStruct((nb, ns, ch), F32), jax.ShapeDtypeStruct((nb, ns, ch), F32),
                   jax.ShapeDtypeStruct((nb, 2, ns), F32), jax.ShapeDtypeStruct((1, nb * ch), F32)],
        scratch_shapes=[pltpu.VMEM((seq, ns), F32), pltpu.VMEM((seq, ns), F32)],
        operands=(dy, z, xre, xim, bre3, bim3, cre3, cim3, coef_rev, dskip), rider=rider)


def _cumsum_rows(x, reverse=False):
    n = x.shape[0]
    row = lax.broadcasted_iota(jnp.int32, x.shape, 0)
    s = 1
    while s < n:
        if reverse:
            x = x + jnp.where(row < n - s, pltpu.roll(x, n - s, 0), 0.0)
        else:
            x = x + jnp.where(row >= s, pltpu.roll(x, s, 0), 0.0)
        s *= 2
    return x


def _hg_gates(zq, zf, lb):
    sg = _sigmoid(zf)
    f = lb + (1.0 - lb) * sg
    sq = _sigmoid(zq)
    qa = zq * sq * (HEAD ** -0.5)
    b = _cumsum_rows(jnp.log(f))
    return sg, f, sq, qa, 1.0 - f, b


SUB = 16


def _hg_scores(qa, kk, b):
    c = qa.shape[0]
    row = lax.broadcasted_iota(jnp.int32, qa.shape, 0)
    pos = jnp.bitwise_and(row, SUB - 1)
    dmat = lax.broadcasted_iota(jnp.int32, (c, c), 0) - lax.broadcasted_iota(jnp.int32, (c, c), 1)
    p = jnp.zeros((c, c), F32)
    for d in range(SUB):
        if d == 0:
            fd = qa * kk
        else:
            e = jnp.exp(jnp.minimum(b - pltpu.roll(b, d, 0), 0.0))
            fd = jnp.where(pos >= d, qa * pltpu.roll(kk, d, 0) * e, 0.0)
        p = jnp.where(dmat == d, jnp.sum(fd, axis=1, keepdims=True), p)
    col = lax.broadcasted_iota(jnp.int32, (SUB, c), 1)
    blocks = [jnp.zeros((SUB, c), F32)]
    for r0 in range(SUB, c, SUB):
        beta = b[r0 - 1:r0, :]
        qt = qa[r0:r0 + SUB] * jnp.exp(b[r0:r0 + SUB] - beta)
        kt = kk * jnp.exp(jnp.minimum(beta - b, 0.0))
        blocks.append(jnp.where(col < r0, _dot_nt(qt, kt), 0.0))
    return p + jnp.concatenate(blocks, axis=0)


def _hg_scores_bwd(dp, qa, kk, b):
    c = qa.shape[0]
    row = lax.broadcasted_iota(jnp.int32, qa.shape, 0)
    pos = jnp.bitwise_and(row, SUB - 1)
    dmat = lax.broadcasted_iota(jnp.int32, (c, c), 0) - lax.broadcasted_iota(jnp.int32, (c, c), 1)
    dqa = jnp.zeros_like(qa)
    dkk = jnp.zeros_like(qa)
    db = jnp.zeros_like(qa)
    for d in range(SUB):
        dcol = jnp.sum(jnp.where(dmat == d, dp, 0.0), axis=1, keepdims=True)
        if d == 0:
            dqa = dqa + dcol * kk
            dkk = dkk + dcol * qa
        else:
            e = jnp.exp(jnp.minimum(b - pltpu.roll(b, d, 0), 0.0))
            w = jnp.where(pos >= d, dcol * e, 0.0)
            kr = pltpu.roll(kk, d, 0)
            dqa = dqa + w * kr
            tmp = w * qa
            dkk = dkk + pltpu.roll(tmp, c - d, 0)
            x = tmp * kr
            db = db + x - pltpu.roll(x, c - d, 0)
    col = lax.broadcasted_iota(jnp.int32, (SUB, c), 1)
    dq_blocks = [jnp.zeros((SUB, qa.shape[1]), F32)]
    db_blocks = [jnp.zeros((SUB, qa.shape[1]), F32)]
    for r0 in range(SUB, c, SUB):
        beta = b[r0 - 1:r0, :]
        eq = jnp.exp(b[r0:r0 + SUB] - beta)
        ek = jnp.exp(jnp.minimum(beta - b, 0.0))
        qt = qa[r0:r0 + SUB] * eq
        kt = kk * ek
        dpi = jnp.where(col < r0, dp[r0:r0 + SUB, :], 0.0)
        dqt = _dot(dpi, kt)
        dkt = _dot_tn(dpi, qt)
        dq_blocks.append(dqt * eq)
        db_blocks.append(dqt * qt)
        dkk = dkk + dkt * ek
        db = db - dkt * kt
    return dqa + jnp.concatenate(dq_blocks, axis=0), dkk, db + jnp.concatenate(db_blocks, axis=0)


def _hg_chunks_per_step(seq):
    nc = seq // CHUNK
    cps = next(k for k in (4, 2, 1) if nc % k == 0)
    return nc, cps, nc // cps


def _hg_fwd(z, lbrow, gain, *, nseq, seq, heads, qoff, rider=None):
    nc, cps, nblk = _hg_chunks_per_step(seq)
    blk = cps * CHUNK
    zspec = lambda off: pl.BlockSpec((blk, HEAD), lambda h, b, n, off=off: (b * nblk + n, off + h))

    def body(zq_ref, zf_ref, zi_ref, zg_ref, lb_ref, gn_ref, o_ref, yb_ref, st_ref, state):
        @pl.when(pl.program_id(2) == 0)
        def _():
            state[...] = jnp.zeros_like(state)

        lb = lb_ref[...]
        gain_v = gn_ref[...]

        def chunk(ci, carry):
            rows = pl.ds(pl.multiple_of(ci * CHUNK, CHUNK), CHUNK)
            st = state[...]
            st_ref[ci] = st
            zi = zi_ref[rows, :]
            zg = zg_ref[rows, :]
            _, _, _, qa, kk, b = _hg_gates(zq_ref[rows, :], zf_ref[rows, :], lb)
            o = _dot_nt(qa * jnp.exp(b), st) + _dot(_hg_scores(qa, kk, b), zi)
            bl = b[CHUNK - 1:CHUNK, :]
            state[...] = st * jnp.exp(bl) + _dot_tn(zi, kk * jnp.exp(bl - b))
            o_ref[rows, :] = o
            r = lax.rsqrt(jnp.mean(o * o, axis=1, keepdims=True) + EPS)
            yb_ref[rows, :] = (o * r * gain_v * zg * _sigmoid(zg)).astype(yb_ref.dtype)
            return carry

        lax.fori_loop(0, cps, chunk, 0)

    tok = pl.BlockSpec((blk, HEAD), lambda h, b, n: (b * nblk + n, h))
    vec = pl.BlockSpec((1, HEAD), lambda h, b, n: (0, h))
    rows = nseq * seq
    return _hosted_call(
        "hgrn2_fwd", body, grid=(heads, nseq, nblk),
        in_specs=[zspec(qoff), zspec(qoff + heads), zspec(qoff + 2 * heads), zspec(qoff + 3 * heads), vec, vec],
        out_specs=[tok, tok, pl.BlockSpec((None, None, cps, HEAD, HEAD), lambda h, b, n: (h, b, n, 0, 0))],
        out_shape=[jax.ShapeDtypeStruct((rows, heads * HEAD), F32), jax.ShapeDtypeStruct((rows, heads * HEAD), BF16),
                   jax.ShapeDtypeStruct((heads, nseq, nc, HEAD, HEAD), F32)],
        scratch_shapes=[pltpu.VMEM((HEAD, HEAD), F32)], operands=(z, z, z, z, lbrow, gain), rider=rider)


def _hg_bwd(dyb, z, o, states, lbrow, gain, *, nseq, seq, heads, qoff, rider=None):
    nc, cps, nblk = _hg_chunks_per_step(seq)
    blk = cps * CHUNK
    rev = lambda n: nblk - 1 - n
    zspec = lambda off: pl.BlockSpec((blk, HEAD), lambda h, b, n, off=off: (b * nblk + rev(n), off + h))

    def body(dyb_ref, zq_ref, zf_ref, zi_ref, zg_ref, o_ref, st_ref, lb_ref, gn_ref,
             dzq_ref, dzf_ref, dzi_ref, dzg_ref, dlb_ref, dgn_ref, dstate):
        @pl.when(pl.program_id(2) == 0)
        def _():
            dstate[...] = jnp.zeros_like(dstate)

        @pl.when(jnp.logical_and(pl.program_id(1) == 0, pl.program_id(2) == 0))
        def _():
            dlb_ref[...] = jnp.zeros_like(dlb_ref)
            dgn_ref[...] = jnp.zeros_like(dgn_ref)

        lb = lb_ref[...]
        gain_v = gn_ref[...]
        c = CHUNK
        causal = lax.broadcasted_iota(jnp.int32, (c, c), 0) >= lax.broadcasted_iota(jnp.int32, (c, c), 1)

        def chunk(step, carry):
            ci = cps - 1 - step
            rows = pl.ds(pl.multiple_of(ci * CHUNK, CHUNK), CHUNK)
            zq = zq_ref[rows, :]
            zi = zi_ref[rows, :]
            zg = zg_ref[rows, :]
            sg, f, sq, qa, kk, b = _hg_gates(zq, zf_ref[rows, :], lb)
            eb = jnp.exp(b)
            qt = qa * eb
            bl = b[c - 1:c, :]
            ebl = jnp.exp(bl)
            ekb = jnp.exp(bl - b)
            kh = kk * ekb
            st = st_ref[ci]
            dst = dstate[...]
            o = o_ref[rows, :]
            r = lax.rsqrt(jnp.mean(o * o, axis=1, keepdims=True) + EPS)
            oh = o * r
            szg = _sigmoid(zg)
            dyb = dyb_ref[rows, :]
            don = dyb * zg * szg
            dzg_ref[rows, :] = (dyb * oh * gain_v * szg * (1.0 + zg * (1.0 - szg))).astype(dzg_ref.dtype)
            doh = don * gain_v
            do = r * (doh - oh * jnp.mean(doh * oh, axis=1, keepdims=True))
            dqt = _dot(do, st)
            dp = jnp.where(causal, _dot_nt(do, zi), 0.0)
            p = _hg_scores(qa, kk, b)
            dzi_ref[rows, :] = (_dot_tn(p, do) + _dot_nt(kh, dst)).astype(dzi_ref.dtype)
            dkh = _dot(zi, dst)
            dbl = jnp.sum(dkh * kh, axis=0, keepdims=True) + jnp.sum(dst * st, axis=0, keepdims=True) * ebl
            dstate[...] = _dot_tn(do, qt) + dst * ebl
            dqa_s, dkk_s, db_s = _hg_scores_bwd(dp, qa, kk, b)
            dqa = dqt * eb + dqa_s
            dkk = dkh * ekb + dkk_s
            db = dqt * qt - dkh * kh + db_s
            row = lax.broadcasted_iota(jnp.int32, db.shape, 0)
            db = db + jnp.where(row == c - 1, dbl, 0.0)
            df = _cumsum_rows(db, reverse=True) / f - dkk
            dzf_ref[rows, :] = (df * (1.0 - lb) * sg * (1.0 - sg)).astype(dzf_ref.dtype)
            dzq_ref[rows, :] = (dqa * (HEAD ** -0.5) * sq * (1.0 + zq * (1.0 - sq))).astype(dzq_ref.dtype)
            dlb_ref[...] += jnp.sum(df * (1.0 - sg), axis=0, keepdims=True)
            dgn_ref[...] += jnp.sum(don * oh, axis=0, keepdims=True)
            return carry

        lax.fori_loop(0, cps, chunk, 0)

    tok = pl.BlockSpec((blk, HEAD), lambda h, b, n: (b * nblk + rev(n), h))
    vec = pl.BlockSpec((1, HEAD), lambda h, b, n: (0, h))
    rows = nseq * seq
    return _hosted_call(
        "hgrn2_bwd", body, grid=(heads, nseq, nblk),
        in_specs=[tok, zspec(qoff), zspec(qoff + heads), zspec(qoff + 2 * heads), zspec(qoff + 3 * heads), tok,
                  pl.BlockSpec((None, None, cps, HEAD, HEAD), lambda h, b, n: (h, b, rev(n), 0, 0)), vec, vec],
        out_specs=[tok, tok, tok, tok, vec, vec],
        out_shape=[jax.ShapeDtypeStruct((rows, heads * HEAD), BF16)] * 4
        + [jax.ShapeDtypeStruct((1, heads * HEAD), F32)] * 2,
        scratch_shapes=[pltpu.VMEM((HEAD, HEAD), F32)], operands=(dyb, z, z, z, z, o, states, lbrow, gain), rider=rider)


def _conv_taps(h, w, bias):
    row = lax.broadcasted_iota(jnp.int32, h.shape, 0)
    h1 = jnp.where(row >= 1, pltpu.roll(h, 1, 0), 0.0)
    h2 = jnp.where(row >= 2, pltpu.roll(h, 2, 0), 0.0)
    return h2 * w[0:1, :] + h1 * w[1:2, :] + h * w[2:3, :] + bias, h1, h2


def _conv_fwd(h, wconv, bconv, *, nseq, seq):
    ff2 = h.shape[1]
    ncol = ff2 // 2 // LANES

    def body(hg_ref, hv_ref, wg_ref, wv_ref, bg_ref, bv_ref, a_ref):
        g, _, _ = _conv_taps(hg_ref[...].astype(F32), wg_ref[...], bg_ref[...])
        v, _, _ = _conv_taps(hv_ref[...].astype(F32), wv_ref[...], bv_ref[...])
        a_ref[...] = (g * _sigmoid(g) * v).astype(a_ref.dtype)

    tok = lambda off: pl.BlockSpec((seq, LANES), lambda j, b, off=off: (b, off + j))
    wsp = lambda off: pl.BlockSpec((CONV_W, LANES), lambda j, b, off=off: (0, off + j))
    bsp = lambda off: pl.BlockSpec((1, LANES), lambda j, b, off=off: (0, off + j))
    return pl.pallas_call(
        body, name="conv_fwd", grid=(ncol, nseq),
        in_specs=[tok(0), tok(ncol), wsp(0), wsp(ncol), bsp(0), bsp(ncol)],
        out_specs=tok(0), out_shape=jax.ShapeDtypeStruct((nseq * seq, ff2 // 2), BF16),
        compiler_params=_params("arbitrary", "arbitrary"),
    )(h, h, wconv, wconv, bconv, bconv)


def _conv_bwd(da, h, wconv, bconv, *, nseq, seq):
    ff2 = h.shape[1]
    ncol = ff2 // 2 // LANES

    def half_bwd(d, hcur, h1, h2, w):
        n = d.shape[0]
        row = lax.broadcasted_iota(jnp.int32, d.shape, 0)
        d1 = jnp.where(row < n - 1, pltpu.roll(d, n - 1, 0), 0.0)
        d2 = jnp.where(row < n - 2, pltpu.roll(d, n - 2, 0), 0.0)
        dh = d * w[2:3, :] + d1 * w[1:2, :] + d2 * w[0:1, :]
        stats = jnp.concatenate(
            [jnp.sum(h2 * d, axis=0, keepdims=True), jnp.sum(h1 * d, axis=0, keepdims=True),
             jnp.sum(hcur * d, axis=0, keepdims=True), jnp.sum(d, axis=0, keepdims=True),
             jnp.zeros((SUBLANES - 4, d.shape[1]), F32)], axis=0)
        return dh, stats

    def body(da_ref, hg_ref, hv_ref, wg_ref, wv_ref, bg_ref, bv_ref, dhg_ref, dhv_ref, sg_ref, sv_ref):
        hg = hg_ref[...].astype(F32)
        hv = hv_ref[...].astype(F32)
        wg = wg_ref[...]
        wv = wv_ref[...]
        g, g1, g2 = _conv_taps(hg, wg, bg_ref[...])
        v, v1, v2 = _conv_taps(hv, wv, bv_ref[...])
        da = da_ref[...].astype(F32)
        s = _sigmoid(g)
        dhg, stg = half_bwd(da * v * s * (1.0 + g * (1.0 - s)), hg, g1, g2, wg)
        dhv, stv = half_bwd(da * g * s, hv, v1, v2, wv)
        dhg_ref[...] = dhg.astype(dhg_ref.dtype)
        dhv_ref[...] = dhv.astype(dhv_ref.dtype)
        first = pl.program_id(1) == 0
        for r, val in ((sg_ref, stg), (sv_ref, stv)):
            @pl.when(first)
            def _():
                r[...] = val

            @pl.when(jnp.logical_not(first))
            def _():
                r[...] += val

    tok = lambda off: pl.BlockSpec((seq, LANES), lambda j, b, off=off: (b, off + j))
    wsp = lambda off: pl.BlockSpec((CONV_W, LANES), lambda j, b, off=off: (0, off + j))
    bsp = lambda off: pl.BlockSpec((1, LANES), lambda j, b, off=off: (0, off + j))
    ssp = pl.BlockSpec((SUBLANES, LANES), lambda j, b: (0, j))
    dhg, dhv, stg, stv = pl.pallas_call(
        body, name="conv_bwd", grid=(ncol, nseq),
        in_specs=[tok(0), tok(0), tok(ncol), wsp(0), wsp(ncol), bsp(0), bsp(ncol)],
        out_specs=[tok(0), tok(0), ssp, ssp],
        out_shape=[jax.ShapeDtypeStruct((nseq * seq, ff2 // 2), BF16)] * 2
        + [jax.ShapeDtypeStruct((SUBLANES, ff2 // 2), F32)] * 2,
        compiler_params=_params("arbitrary", "arbitrary"),
    )(da, h, h, wconv, wconv, bconv, bconv)
    return (dhg, dhv), jnp.concatenate([stg, stv], axis=1)


def _rms_fwd(xv, g):
    r = lax.rsqrt(jnp.mean(xv * xv, axis=1, keepdims=True) + EPS)
    return (xv * r * g,)


def _rms_bwd(xv, g, dy, res):
    r = lax.rsqrt(jnp.mean(xv * xv, axis=1, keepdims=True) + EPS)
    xh = xv * r
    dxh = dy * g
    dx = r * (dxh - xh * jnp.mean(dxh * xh, axis=1, keepdims=True)) + res
    return dx, jnp.sum(dy * xh, axis=0, keepdims=True)


def _loss_head(x2, tgt, g):
    d = x2.shape[1]
    r = lax.rsqrt(jnp.mean(x2 * x2, axis=1, keepdims=True) + EPS)
    xh = x2 * r
    err = xh * g - tgt
    dy = err * (1.0 / d)
    dxh = dy * g
    dx = r * (dxh - xh * jnp.mean(dxh * xh, axis=1, keepdims=True))
    loss = 0.5 * jnp.sum(jnp.mean(err * err, axis=1, keepdims=True), axis=0, keepdims=True)
    return dx, jnp.sum(dy * xh, axis=0, keepdims=True), jnp.broadcast_to(loss, (1, LANES))


LATE = ("w_pa", "w_pb", "w_out", "w_up", "w_down")
EARLY_GRADS = ("w_down", "w_up", "w_out", "w_pa", "w_pb", "w_glu")
ROW_SHARDED = ("w_glu", "w_out", "w_down")


def _local_step(x, tgt, p, late, *, nseq, seq):
    p = dict(p)
    chip = 2 * lax.axis_index("x") + lax.axis_index("y")
    t, d = x.shape
    s5w = p["s5_d"].shape[1]
    hgw = p["gain"].shape[1]
    heads = hgw // HEAD
    qoff = s5w // LANES
    gblk = (s5w + 4 * hgw) // GATE_BLOCK
    ngb = d // GATE_BLOCK
    tm = _row_tile(t, 256)
    row = lambda a, w=None, base=0: (a, a.shape[1] if w is None else w, base, "row")
    vec = lambda a, w=None, base=0: (a, a.shape[1] if w is None else w, base, "vec")
    rw = functools.partial(_rowwise, rows=t, tm=tm)

    (u,) = rw("rms_mix", _rms_fwd, [row(x), vec(p["g_mix"])], [(d, d, BF16)])
    z = _mm_fwd_cols("in_proj", u, p["w_in"])

    lam_re, lam_im, bb_re, bb_im = _s5_discretize(p["s5_a_re"], p["s5_a_im"], p["s5_log_dt"], p["s5_b_re"], p["s5_b_im"])
    bre3 = _s5_in_blocks(bb_re).astype(BF16)
    bim3 = _s5_in_blocks(bb_im).astype(BF16)
    cre3 = _s5_out_blocks(p["s5_c_re"]).astype(BF16)
    cim3 = _s5_out_blocks(p["s5_c_im"]).astype(BF16)
    coef_f = _s5_scan_tables(lam_re.reshape(-1), lam_im.reshape(-1), False)
    coef_r = _s5_scan_tables(lam_re.reshape(-1), lam_im.reshape(-1), True)
    (o, yb, states), landed = _hg_fwd(z, p["lbrow"], p["gain"], nseq=nseq, seq=seq, heads=heads, qoff=qoff,
                                      rider=_gather_ici_rider([late[n] for n in LATE]))
    (y5, xre, xim), gathered = _s5_fwd(z, bre3, bim3, cre3, cim3, coef_f, p["s5_d"], nseq=nseq, seq=seq,
                                       rider=_gather_pass_rider(landed))
    for n, g in zip(LATE, gathered):
        full = lax.dynamic_update_index_in_dim(g, late[n], chip, 0)
        p[n] = full.reshape(-1, full.shape[-1]) if n in ROW_SHARDED else full
    (ya0,) = rw("s5_gelu", lambda y: (_gelu(y),), [row(y5)], [(s5w, s5w, BF16)])
    gl = _mm_fwd_rows("glu_proj", ya0, p["w_glu"])
    (ya,) = rw("s5_glu", lambda y, g, b: (_gelu(y) * _sigmoid(g + b),), [row(y5), row(gl), vec(p["b_glu"])],
               [(s5w, s5w, BF16)])

    joined = lambda w3: w3.transpose(1, 0, 2).reshape(w3.shape[1], -1)
    split = lambda g: g.reshape(g.shape[0], N_CHIPS, -1).transpose(1, 0, 2)
    wpa, wpb = joined(p["w_pa"]), joined(p["w_pb"])
    pa = _mm_fwd_rows("proj_a", ya, wpa, out_dtype=BF16)
    pb = _mm_fwd_rows("proj_b", yb, wpb, out_dtype=BF16)
    gb = GATE_BLOCK
    (m,) = rw("merge", lambda ga, gbv, a, b: (_sigmoid(ga) * a + _sigmoid(gbv) * b,),
              [row(z, gb, gblk), row(z, gb, gblk + ngb), row(pa, gb), row(pb, gb)], [(d, gb, BF16)], ncol=ngb)
    x1 = _mm_fwd_rows("out_proj", m, p["w_out"], res=x)

    (u2,) = rw("rms_ffn", _rms_fwd, [row(x1), vec(p["g_ffn"])], [(d, d, BF16)])
    h = _mm_fwd_cols("up_proj", u2, p["w_up"], out_dtype=BF16)
    a = _conv_fwd(h, p["w_conv"], p["b_conv"], nseq=nseq, seq=seq)
    x2 = _mm_fwd_rows("down_proj", a, p["w_down"], res=x1)

    dx2, dg_final, lossv = rw("loss_head", _loss_head, [row(x2), row(tgt), vec(p["g_final"])], [(d, d, F32)],
                              accs=[(d, d), (LANES, LANES)])

    da = _mm_bwd_rows("down_bwd", dx2, p["w_down"], out_dtype=BF16)
    g_wdown = _mm_wgrad_rows("down_wgrad", a, dx2)
    dh, cstats = _conv_bwd(da, h, p["w_conv"], p["b_conv"], nseq=nseq, seq=seq)
    du2 = _mm_bwd_cols("up_bwd", dh, p["w_up"])
    g_wup = _mm_wgrad_cols("up_wgrad", u2, dh)
    dx1, dg_ffn = rw("rms_ffn_bwd", _rms_bwd, [row(x1), vec(p["g_ffn"]), row(du2), row(dx2)], [(d, d, F32)],
                     accs=[(d, d)])

    dm = _mm_bwd_rows("out_bwd", dx1, p["w_out"], out_dtype=BF16)
    g_wout = _mm_wgrad_rows("out_wgrad", m, dx1)

    def merge_bwd(ga, gbv, av, bv, dmv):
        sa = _sigmoid(ga)
        sb = _sigmoid(gbv)
        return dmv * sa, dmv * sb, dmv * av * sa * (1.0 - sa), dmv * bv * sb * (1.0 - sb)

    dpa, dpb, dzga, dzgb = rw("merge_bwd", merge_bwd,
                              [row(z, gb, gblk), row(z, gb, gblk + ngb), row(pa, gb), row(pb, gb), row(dm, gb)],
                              [(d, gb, BF16)] * 4, ncol=ngb)
    dya = _mm_bwd_rows("proj_a_bwd", dpa, wpa)
    g_wpa = split(_mm_wgrad_rows("proj_a_wgrad", ya, dpa))
    dyb = _mm_bwd_rows("proj_b_bwd", dpb, wpb)
    g_wpb = split(_mm_wgrad_rows("proj_b_wgrad", yb, dpb))

    def glu_bwd1(y, g, b, dyv):
        s = _sigmoid(g + b)
        dgl = dyv * _gelu(y) * s * (1.0 - s)
        return dgl, jnp.sum(dgl, axis=0, keepdims=True)

    dgl, db_glu = rw("s5_glu_bwd", glu_bwd1, [row(y5), row(gl), vec(p["b_glu"]), row(dya)], [(s5w, s5w, BF16)],
                     accs=[(s5w, s5w)])
    dgl_in = _mm_bwd_rows("glu_bwd", dgl, p["w_glu"])
    g_wglu = _mm_wgrad_rows("glu_wgrad", ya0, dgl)
    (dy5,) = rw("s5_gelu_bwd", lambda y, g, b, dyv, tv: ((dyv * _sigmoid(g + b) + tv) * _gelu_grad(y),),
                [row(y5), row(gl), vec(p["b_glu"]), row(dya), row(dgl_in)], [(s5w, s5w, F32)])
    partial = dict(w_down=g_wdown, w_up=g_wup, w_out=g_wout, w_pa=g_wpa, w_pb=g_wpb, w_glu=g_wglu)
    parts = [_grad_parts(partial[n]) for n in EARLY_GRADS]
    (dza, dbre3, dbim3, dcre3, dcim3, dlam, dd), sib = _s5_bwd(
        dy5, z, xre, xim, bre3, bim3, cre3, cim3, coef_r, p["s5_d"], nseq=nseq, seq=seq, rider=_swap_halves_rider(parts))
    pair = _pair_sums(EARLY_GRADS, parts, sib)
    (dzq, dzf, dzi, dzg, dlb, dgain), others = _hg_bwd(
        dyb, z, o, states, p["lbrow"], p["gain"], nseq=nseq, seq=seq, heads=heads, qoff=qoff, rider=_scatter_rider(pair))
    halves = _chip_sums(EARLY_GRADS, pair, others)

    dz = jnp.concatenate([dza, dzq, dzf, dzi, dzg, dzga, dzgb], axis=1)
    du, sibs = _mm_bwd_cols("in_bwd", dz, p["w_in"], rider=_swap_sums_rider(halves))
    big = dict(zip(EARLY_GRADS, _join_halves(halves, sibs)))

    g_win = _mm_wgrad_cols("in_wgrad", u, dz)
    last = [_grad_parts(g_win)]
    pair = _pair_sums(("w_in",), last, _run_rider("grad_swap_halves", _swap_halves_rider(last)))
    (half,) = _chip_sums(("w_in",), pair, _run_rider("grad_scatter_chips", _scatter_rider(pair)))
    mid = half.shape[0] // 2
    sib_half = jnp.concatenate(_run_rider("grad_swap_sums", _swap_sums_rider([half[:mid], half[mid:]])), axis=0)
    big["w_in"] = _join_halves([half], [sib_half])[0]

    dx, dg_mix = rw("rms_mix_bwd", _rms_bwd, [row(x), vec(p["g_mix"]), row(du), row(dx1)], [(d, d, F32)],
                    accs=[(d, d)])

    gshape = lam_re.shape
    small = {
        "loss": lossv, "g_mix": dg_mix, "g_ffn": dg_ffn, "g_final": dg_final, "b_glu": db_glu, "gain": dgain,
        "lbrow": dlb, "s5_d": dd, "w_conv": cstats[0:CONV_W], "b_conv": cstats[CONV_W:CONV_W + 1],
        "lam_re": dlam[:, 0, :].reshape(gshape), "lam_im": dlam[:, 1, :].reshape(gshape),
        "bb_re": _s5_in_blocks_diag(dbre3), "bb_im": _s5_in_blocks_diag(dbim3),
        "s5_c_re": _s5_out_blocks_diag(dcre3), "s5_c_im": _s5_out_blocks_diag(dcim3),
    }
    return dx, big, small


ANY = pl.BlockSpec(memory_space=pl.ANY)


def _place():
    x, y, c = lax.axis_index("x"), lax.axis_index("y"), lax.axis_index("c")
    chips = [(1 - x, y), (x, 1 - y), (1 - x, 1 - y)]
    return x, y, c, chips


def _remote(src, dst, send_sems, recv_sems, k, to):
    return pltpu.make_async_remote_copy(src_ref=src, dst_ref=dst, send_sem=send_sems.at[k], recv_sem=recv_sems.at[k],
                                        device_id=to, device_id_type=MESH)


def _half(rows, which):
    return pl.ds(pl.multiple_of(which * (rows // 2), 16), rows // 2)


def _gather_weights(shards, whole):
    n, nw = len(shards), len(whole)
    arrays = list(shards) + list(whole)

    def body(*refs):
        in_refs, out_refs = refs[:n + nw], refs[n + nw:2 * (n + nw)]
        send_sems, recv_sems = refs[2 * (n + nw):]
        x, y, c, chips = _place()
        me = 2 * x + y
        copy = functools.partial(_remote, send_sems=send_sems, recv_sems=recv_sems)
        sends = []
        for a in range(n):
            mine_half = _half(arrays[a].shape[0], c)
            for j, (cx, cy) in enumerate(chips):
                sends.append(copy(in_refs[a].at[mine_half], out_refs[a].at[me, mine_half], k=6 * a + j, to=(cx, cy, c)))
        for a in range(n, n + nw):
            for j, (cx, cy) in enumerate(chips):
                sends.append(copy(in_refs[a], out_refs[a].at[me], k=6 * n + 3 * (a - n) + j, to=(cx, cy, c)))
        for cp in sends:
            cp.start()
        for a in range(n):
            mine_half = _half(arrays[a].shape[0], c)
            for j, (cx, cy) in enumerate(chips):
                landed = out_refs[a].at[2 * cx + cy, mine_half]
                copy(landed, landed, k=6 * a + j, to=(x, y, c)).wait_recv()
                fwd = copy(landed, landed, k=6 * a + 3 + j, to=(x, y, 1 - c))
                fwd.start()
                sends.append(fwd)
        for a in range(n):
            other_half = _half(arrays[a].shape[0], 1 - c)
            for j, (cx, cy) in enumerate(chips):
                landed = out_refs[a].at[2 * cx + cy, other_half]
                copy(landed, landed, k=6 * a + 3 + j, to=(x, y, c)).wait_recv()
        for a in range(n, n + nw):
            for j, (cx, cy) in enumerate(chips):
                landed = out_refs[a].at[2 * cx + cy]
                copy(landed, landed, k=6 * n + 3 * (a - n) + j, to=(x, y, c)).wait_recv()
        for cp in sends:
            cp.wait_send()

    nsem = 6 * n + 3 * nw
    return pl.pallas_call(
        body, name="gather_weights", out_shape=[jax.ShapeDtypeStruct((N_CHIPS,) + a.shape, a.dtype) for a in arrays],
        in_specs=[ANY] * (n + nw), out_specs=[ANY] * (n + nw),
        scratch_shapes=[pltpu.SemaphoreType.DMA((nsem,)), pltpu.SemaphoreType.DMA((nsem,))],
    )(*arrays)


def _symmetric_rider(arrays, out_shapes, copies_of, nsem):
    def start(ins, outs, send_sems, recv_sems):
        for cp in copies_of(ins, outs, send_sems, recv_sems):
            cp.start()

    def finish(ins, outs, send_sems, recv_sems):
        for cp in copies_of(ins, outs, send_sems, recv_sems):
            cp.wait()

    return _Rider(arrays, out_shapes, nsem, start, finish)


def _swap_halves_rider(parts):
    def copies_of(ins, outs, send_sems, recv_sems):
        x, y, c, _ = _place()
        return [_remote(ins[a].at[:, _half(g.shape[1], 1 - c), :], outs[a], send_sems, recv_sems, a, (x, y, 1 - c))
                for a, g in enumerate(parts)]

    shapes = [jax.ShapeDtypeStruct((g.shape[0], g.shape[1] // 2, g.shape[2]), g.dtype) for g in parts]
    return _symmetric_rider(parts, shapes, copies_of, len(parts))


def _scatter_rider(parts):
    def copies_of(ins, outs, send_sems, recv_sems):
        x, y, c, chips = _place()
        return [_remote(ins[a].at[2 * cx + cy], outs[a].at[j], send_sems, recv_sems, 3 * a + j, (cx, cy, c))
                for a in range(len(parts)) for j, (cx, cy) in enumerate(chips)]

    shapes = [jax.ShapeDtypeStruct((N_CHIPS - 1,) + h.shape[1:], h.dtype) for h in parts]
    return _symmetric_rider(parts, shapes, copies_of, 3 * len(parts))


def _swap_sums_rider(parts):
    def copies_of(ins, outs, send_sems, recv_sems):
        x, y, c, _ = _place()
        return [_remote(ins[a], outs[a], send_sems, recv_sems, a, (x, y, 1 - c)) for a in range(len(parts))]

    shapes = [jax.ShapeDtypeStruct(g.shape, g.dtype) for g in parts]
    return _symmetric_rider(parts, shapes, copies_of, len(parts))


def _gather_ici_rider(shards):
    def sends(ins, outs, send_sems, recv_sems):
        x, y, c, chips = _place()
        return [_remote(ins[a].at[_half(s.shape[0], c)], outs[a].at[2 * x + y, _half(s.shape[0], c)], send_sems,
                        recv_sems, 3 * a + j, (cx, cy, c)) for a, s in enumerate(shards) for j, (cx, cy) in enumerate(chips)]

    def start(ins, outs, send_sems, recv_sems):
        for cp in sends(ins, outs, send_sems, recv_sems):
            cp.start()

    def finish(ins, outs, send_sems, recv_sems):
        x, y, c, chips = _place()
        for a, s in enumerate(shards):
            for j, (cx, cy) in enumerate(chips):
                landed = outs[a].at[2 * cx + cy, _half(s.shape[0], c)]
                _remote(landed, landed, send_sems, recv_sems, 3 * a + j, (x, y, c)).wait_recv()
        for cp in sends(ins, outs, send_sems, recv_sems):
            cp.wait_send()

    shapes = [jax.ShapeDtypeStruct((N_CHIPS,) + s.shape, s.dtype) for s in shards]
    return _Rider(shards, shapes, 3 * len(shards), start, finish)


def _gather_pass_rider(landed):
    def sends(ins, outs, send_sems, recv_sems):
        x, y, c, chips = _place()
        return [_remote(ins[a].at[2 * cx + cy, _half(g.shape[1], c)], outs[a].at[2 * cx + cy, _half(g.shape[1], c)],
                        send_sems, recv_sems, 3 * a + j, (x, y, 1 - c))
                for a, g in enumerate(landed) for j, (cx, cy) in enumerate(chips)]

    def start(ins, outs, send_sems, recv_sems):
        for cp in sends(ins, outs, send_sems, recv_sems):
            cp.start()

    def finish(ins, outs, send_sems, recv_sems):
        x, y, c, chips = _place()
        for a, g in enumerate(landed):
            for j, (cx, cy) in enumerate(chips):
                other = outs[a].at[2 * cx + cy, _half(g.shape[1], 1 - c)]
                _remote(other, other, send_sems, recv_sems, 3 * a + j, (x, y, c)).wait_recv()
        for cp in sends(ins, outs, send_sems, recv_sems):
            cp.wait_send()

    shapes = [jax.ShapeDtypeStruct(g.shape, g.dtype) for g in landed]
    return _Rider(landed, shapes, 3 * len(landed), start, finish, aliases={a: a for a in range(len(landed))})


def _grad_parts(g):
    return g.reshape((N_CHIPS, -1, g.shape[-1]))


def _pair_sums(names, parts, sib):
    ci = lax.axis_index("c")
    out = []
    for n, g, s in zip(names, parts, sib):
        rh, cols = s.shape[1], s.shape[2]
        own = lax.dynamic_slice_in_dim(g, ci * rh, rh, axis=1)
        both = _sum_blocks("grad_pair_sum_" + n, [own.reshape(-1, cols), s.reshape(-1, cols)], BF16)
        out.append(both.reshape(N_CHIPS, rh, cols))
    return out


def _chip_sums(names, pair, others):
    chip = 2 * lax.axis_index("x") + lax.axis_index("y")
    return [_sum_blocks("grad_chip_sum_" + n, [lax.dynamic_index_in_dim(h, chip, axis=0, keepdims=False), o[0], o[1], o[2]],
                        F32) for n, h, o in zip(names, pair, others)]


def _join_halves(halves, sibs):
    ci = lax.axis_index("c")
    return [lax.dynamic_update_slice_in_dim(jnp.concatenate([own, own], axis=0), s, (1 - ci) * own.shape[0], axis=0)
            for own, s in zip(halves, sibs)]


def _gather_all(v):
    m_per, n = v.shape

    def body(x_ref, out_ref, send_sems, recv_sems):
        x, y, c, chips = _place()
        me, sibling = (x, y, c), (x, y, 1 - c)

        def rows(px, py, pc):
            return out_ref.at[pl.ds(pl.multiple_of((4 * px + 2 * py + pc) * m_per, 8), m_per), :]

        def copy(k, block, to, src=None):
            return pltpu.make_async_remote_copy(src_ref=rows(*block) if src is None else src, dst_ref=rows(*block),
                                                send_sem=send_sems.at[k], recv_sem=recv_sems.at[k], device_id=to,
                                                device_id_type=MESH)

        out_ref[pl.ds(pl.multiple_of((4 * x + 2 * y + c) * m_per, 8), m_per), :] = x_ref[...]
        first = [copy(0, me, sibling, src=x_ref)]
        first += [copy(1 + j, me, (*chip, c), src=x_ref) for j, chip in enumerate(chips)]
        for cp in first:
            cp.start()
        passed = [copy(4 + j, (*chip, c), sibling) for j, chip in enumerate(chips)]
        for j, chip in enumerate(chips):
            copy(1 + j, (*chip, c), me).wait_recv()
            passed[j].start()
        copy(0, sibling, me).wait_recv()
        for j, chip in enumerate(chips):
            copy(4 + j, (*chip, 1 - c), me).wait_recv()
        for cp in first + passed:
            cp.wait_send()

    return pl.pallas_call(
        body, name="gather_small_grads", out_shape=jax.ShapeDtypeStruct((N_DEV * m_per, n), v.dtype),
        in_specs=[pl.BlockSpec(memory_space=pltpu.VMEM)], out_specs=pl.BlockSpec(memory_space=pltpu.VMEM),
        scratch_shapes=[pltpu.SemaphoreType.DMA((7,)), pltpu.SemaphoreType.DMA((7,))],
        compiler_params=pltpu.CompilerParams(vmem_limit_bytes=VMEM_LIMIT_BYTES),
    )(v)


def _sum_blocks(name, parts, out_dtype):
    rows, cols = parts[0].shape
    tm = _row_tile(rows, 512)

    def body(*refs):
        acc = refs[0][...].astype(F32)
        for r in refs[1:-1]:
            acc = acc + r[...].astype(F32)
        refs[-1][...] = acc.astype(refs[-1].dtype)

    spec = pl.BlockSpec((tm, cols), lambda i: (i, 0))
    return pl.pallas_call(
        body, name=name, grid=(rows // tm,), in_specs=[spec] * len(parts), out_specs=spec,
        out_shape=jax.ShapeDtypeStruct((rows, cols), out_dtype), compiler_params=_params("arbitrary"),
    )(*parts)


def _adamw_math(wv, gv, mv, vv):
    m2 = ADAM_B1 * mv + (1.0 - ADAM_B1) * gv
    v2 = ADAM_B2 * vv + (1.0 - ADAM_B2) * (gv * gv)
    delta = -ADAM_LR * ((m2 / (1.0 - ADAM_B1 ** ADAM_STEP)) / (jnp.sqrt(v2 / (1.0 - ADAM_B2 ** ADAM_STEP)) + ADAM_EPS)
                        + ADAM_WD * wv)
    return delta, m2, v2


def _adamw_small(ws, gs, ms, vs):
    n = len(ws)

    def body(*refs):
        for i in range(n):
            res = _adamw_math(refs[i][...], refs[n + i][...], refs[2 * n + i][...], refs[3 * n + i][...])
            for k in range(3):
                refs[(4 + k) * n + i][...] = res[k]

    vm = pl.BlockSpec(memory_space=pltpu.VMEM)
    outs = pl.pallas_call(
        body, name="adamw_small", in_specs=[vm] * (4 * n), out_specs=[vm] * (3 * n),
        out_shape=[jax.ShapeDtypeStruct(a.shape, F32) for a in ws] * 3,
        compiler_params=pltpu.CompilerParams(vmem_limit_bytes=VMEM_LIMIT_BYTES),
    )(*ws, *gs, *ms, *vs)
    return outs[:n], outs[n:2 * n], outs[2 * n:]


def _adamw(name, w, g, m, v):
    rows, cols = w.shape
    ins = [(a, cols, 0, "row") for a in (w, g, m, v)]
    return _rowwise(name, _adamw_math, ins, [(cols, cols, F32)] * 3, rows=rows, tm=_row_tile(rows, 256))


PACK_ROWS = 256


def _pack(flat_parts, dtype, lead=()):
    parts = [a.astype(dtype).reshape(lead + (-1,)) for a in flat_parts]
    n = sum(a.shape[-1] for a in parts)
    chunk = PACK_ROWS * LANES
    total = -(-n // chunk) * chunk
    if total > n:
        parts.append(jnp.zeros(lead + (total - n,), dtype))
    return jnp.concatenate(parts, axis=-1).reshape(lead + (total // LANES, LANES))


def _unpack(buf, shapes, lead=()):
    flat = buf.reshape(lead + (-1,))
    out, off = [], 0
    for shp in shapes:
        n = math.prod(shp)
        out.append(lax.slice_in_dim(flat, off, off + n, axis=len(lead)).reshape(lead + tuple(shp)))
        off += n
    return out


BIG = ("w_in", "w_glu", "w_pa", "w_pb", "w_out", "w_up", "w_down")
WEIGHTS = ("g_mix", "w_in", "s5_a_re", "s5_a_im", "s5_log_dt", "s5_b_re", "s5_b_im", "s5_c_re", "s5_c_im", "s5_d",
           "w_glu", "b_glu", "hg_lb_logits", "hg_norm_gain", "w_pa", "w_pb", "w_out", "g_ffn", "w_up", "w_conv",
           "b_conv", "w_down", "g_final")
SMALL = tuple(n for n in WEIGHTS if n not in BIG)
SMALL_PARTS = ("loss", "g_mix", "g_ffn", "g_final", "b_glu", "gain", "lbrow", "s5_d", "w_conv", "b_conv", "lam_re",
               "lam_im", "bb_re", "bb_im", "s5_c_re", "s5_c_im")


def _lower_bound(logits):
    return jnp.cumsum(jax.nn.softmax(logits, axis=0), axis=0)[0:1]


def kernel(x, g_mix, w_in, s5_a_re, s5_a_im, s5_log_dt, s5_b_re, s5_b_im, s5_c_re, s5_c_im, s5_d, w_glu, b_glu, hg_lb_logits, hg_norm_gain, w_pa, w_pb, w_out, g_ffn, w_up, w_conv, b_conv, w_down, g_final, loss_target, m_g_mix, m_w_in, m_s5_a_re, m_s5_a_im, m_s5_log_dt, m_s5_b_re, m_s5_b_im, m_s5_c_re, m_s5_c_im, m_s5_d, m_w_glu, m_b_glu, m_hg_lb_logits, m_hg_norm_gain, m_w_pa, m_w_pb, m_w_out, m_g_ffn, m_w_up, m_w_conv, m_b_conv, m_w_down, m_g_final, v_g_mix, v_w_in, v_s5_a_re, v_s5_a_im, v_s5_log_dt, v_s5_b_re, v_s5_b_im, v_s5_c_re, v_s5_c_im, v_s5_d, v_w_glu, v_b_glu, v_hg_lb_logits, v_hg_norm_gain, v_w_pa, v_w_pb, v_w_out, v_g_ffn, v_w_up, v_w_conv, v_b_conv, v_w_down, v_g_final):
    args = dict(locals())
    w = {n: args[n] for n in WEIGHTS}
    mom = {n: args["m_" + n] for n in WEIGHTS}
    var = {n: args["v_" + n] for n in WEIGHTS}
    nseq, seq, d = x.shape
    xi, yi = lax.axis_index("x"), lax.axis_index("y")
    chip = 2 * xi + yi

    shard = {n: w[n][0] for n in BIG}
    shard16 = {n: shard[n].astype(BF16) for n in BIG}
    first = ("w_in", "w_glu")
    got = _gather_weights([shard16[n] for n in first], [w_conv[0]])
    p = {n: lax.dynamic_update_index_in_dim(g, shard16[n], chip, 0) for n, g in zip(first, got)}
    p["w_glu"] = p["w_glu"].reshape(-1, p["w_glu"].shape[-1])
    conv_all = lax.dynamic_update_index_in_dim(got[-1], w_conv[0], chip, 0)
    p.update(g_mix=g_mix, g_ffn=g_ffn, g_final=g_final.reshape(1, -1), b_glu=b_glu, gain=hg_norm_gain, s5_d=s5_d,
             b_conv=b_conv, w_conv=conv_all.transpose(1, 0, 2).reshape(CONV_W, -1), lbrow=_lower_bound(hg_lb_logits),
             s5_a_re=s5_a_re[0], s5_a_im=s5_a_im[0], s5_log_dt=s5_log_dt[0], s5_b_re=s5_b_re[0], s5_b_im=s5_b_im[0],
             s5_c_re=s5_c_re[0], s5_c_im=s5_c_im[0])

    dx, grads, gsmall = _local_step(x.reshape(nseq * seq, d), loss_target.reshape(nseq * seq, d), p,
                                    {n: shard16[n] for n in LATE}, nseq=nseq, seq=seq)

    small_shapes = [gsmall[n].shape for n in SMALL_PARTS]
    vec = _pack([gsmall[n] for n in SMALL_PARTS], F32)
    gathered = _gather_all(vec)
    mrows = vec.shape[0]
    vsum = _sum_blocks("small_grad_sum", [gathered[i * mrows:(i + 1) * mrows] for i in range(N_DEV)], F32)
    sm = dict(zip(SMALL_PARTS, _unpack(vsum, small_shapes)))
    loss = sm["loss"][0, 0]

    _, disc_vjp = jax.vjp(_s5_discretize, p["s5_a_re"], p["s5_a_im"], p["s5_log_dt"], p["s5_b_re"], p["s5_b_im"])
    da_re, da_im, dlog_dt, db_re, db_im = disc_vjp((sm["lam_re"], sm["lam_im"], sm["bb_re"], sm["bb_im"]))
    _, lb_vjp = jax.vjp(_lower_bound, hg_lb_logits)
    (dlogits,) = lb_vjp(sm["lbrow"])
    fcols = w_conv.shape[-1]
    grads.update(
        g_mix=sm["g_mix"], g_ffn=sm["g_ffn"], g_final=sm["g_final"].reshape(-1), b_glu=sm["b_glu"],
        hg_norm_gain=sm["gain"], hg_lb_logits=dlogits, s5_d=sm["s5_d"], b_conv=sm["b_conv"],
        w_conv=lax.dynamic_slice_in_dim(sm["w_conv"], chip * fcols, fcols, axis=1),
        s5_a_re=da_re, s5_a_im=da_im, s5_log_dt=dlog_dt, s5_b_re=db_re, s5_b_im=db_im,
        s5_c_re=sm["s5_c_re"], s5_c_im=sm["s5_c_im"])
    grads = {n: grads[n].reshape(w[n].shape) for n in WEIGHTS}

    delta, new_m, new_v = {}, {}, {}
    for n in BIG:
        shp = shard[n].shape
        dl, m2, v2 = _adamw("adamw_" + n, shard[n], grads[n].reshape(shp), mom[n].reshape(shp), var[n].reshape(shp))
        delta[n], new_m[n], new_v[n] = dl, m2, v2
    def natural(a):
        return a.reshape(1, -1) if a.ndim == 1 else (a[0] if a.ndim > 2 else a)

    outs = _adamw_small(*[[natural(src[n]) for n in SMALL] for src in (w, grads, mom, var)])
    for dst, group in zip((delta, new_m, new_v), outs):
        dst.update(zip(SMALL, group))
    res = [loss, dx.reshape(x.shape)]
    for group in (grads, delta, new_m, new_v):
        res += [group[n].reshape(w[n].shape) for n in WEIGHTS]
    return tuple(res)
```

```python
import functools
import math

import jax
import jax.numpy as jnp
from jax import lax
from jax.experimental import pallas as pl
from jax.experimental.pallas import tpu as pltpu

F32 = jnp.float32
BF16 = jnp.bfloat16
MESH = pl.DeviceIdType.MESH

EPS = 1e-6
S5_GROUP = 16
S5_STATE = 64
S5_BLOCK_GROUPS = 8
HEAD = 128
CHUNK = 64
CONV_W = 3
LANES = 128
SUBLANES = 8
GATE_BLOCK = 512
VMEM_LIMIT_BYTES = 56 * 1024 * 1024

ADAM_LR = 0.001
ADAM_B1 = 0.9
ADAM_B2 = 0.999
ADAM_EPS = 1e-08
ADAM_WD = 0.01
ADAM_STEP = 10

N_CHIPS = 4
N_DEV = 8


def _params(*sem):
    return pltpu.CompilerParams(dimension_semantics=sem, vmem_limit_bytes=VMEM_LIMIT_BYTES)


class _Rider:
    def __init__(self, arrays, out_shapes, nsem, start, finish, aliases=None):
        self.arrays, self.out_shapes, self.nsem = list(arrays), list(out_shapes), nsem
        self.start, self.finish, self.aliases = start, finish, dict(aliases or {})


def _hosted_call(name, body, *, grid, in_specs, out_specs, out_shape, operands, scratch_shapes=(), rider=None):
    in_specs, out_specs, out_shape, scratch_shapes = list(in_specs), list(out_specs), list(out_shape), list(scratch_shapes)
    cparams = _params(*(["arbitrary"] * len(grid)))
    if rider is None:
        return pl.pallas_call(body, name=name, grid=grid, in_specs=in_specs, out_specs=out_specs, out_shape=out_shape,
                              scratch_shapes=scratch_shapes, compiler_params=cparams)(*operands)
    n_in, n_out, n_sc = len(in_specs), len(out_specs), len(scratch_shapes)
    r_in, r_out = len(rider.arrays), len(rider.out_shapes)

    def hosted(*refs):
        ins, rins = refs[:n_in], refs[n_in:n_in + r_in]
        outs = refs[n_in + r_in:n_in + r_in + n_out]
        routs = refs[n_in + r_in + n_out:n_in + r_in + n_out + r_out]
        rest = refs[n_in + r_in + n_out + r_out:]
        send_sems, recv_sems = rest[n_sc], rest[n_sc + 1]
        first = functools.reduce(jnp.logical_and, [pl.program_id(i) == 0 for i in range(len(grid))])
        last = functools.reduce(jnp.logical_and, [pl.program_id(i) == grid[i] - 1 for i in range(len(grid))])

        @pl.when(first)
        def _():
            rider.start(rins, routs, send_sems, recv_sems)

        body(*ins, *outs, *rest[:n_sc])

        @pl.when(last)
        def _():
            rider.finish(rins, routs, send_sems, recv_sems)

    res = pl.pallas_call(
        hosted, name=name, grid=grid, in_specs=in_specs + [ANY] * r_in, out_specs=out_specs + [ANY] * r_out,
        out_shape=out_shape + rider.out_shapes,
        scratch_shapes=scratch_shapes + [pltpu.SemaphoreType.DMA((rider.nsem,)), pltpu.SemaphoreType.DMA((rider.nsem,))],
        input_output_aliases={n_in + i: n_out + o for i, o in rider.aliases.items()}, compiler_params=cparams,
    )(*operands, *rider.arrays)
    return res[:n_out], res[n_out:]


def _run_rider(name, rider):
    r_in, r_out = len(rider.arrays), len(rider.out_shapes)

    def body(*refs):
        rins, routs, send_sems, recv_sems = refs[:r_in], refs[r_in:r_in + r_out], refs[-2], refs[-1]
        rider.start(rins, routs, send_sems, recv_sems)
        rider.finish(rins, routs, send_sems, recv_sems)

    return pl.pallas_call(
        body, name=name, in_specs=[ANY] * r_in, out_specs=[ANY] * r_out, out_shape=rider.out_shapes,
        scratch_shapes=[pltpu.SemaphoreType.DMA((rider.nsem,)), pltpu.SemaphoreType.DMA((rider.nsem,))],
        input_output_aliases=rider.aliases,
    )(*rider.arrays)


def _row_tile(rows, cap):
    if rows <= cap:
        return rows
    for t in range(cap - cap % 8, 7, -8):
        if rows % t == 0:
            return t
    raise ValueError(f"no row tile for {rows}")


def _dot(a, b):
    return jnp.dot(a.astype(BF16), b.astype(BF16), preferred_element_type=F32)


def _dot_nt(a, b):
    return lax.dot_general(a.astype(BF16), b.astype(BF16), (((1,), (1,)), ((), ())), preferred_element_type=F32)


def _dot_tn(a, b):
    return lax.dot_general(a.astype(BF16), b.astype(BF16), (((0,), (0,)), ((), ())), preferred_element_type=F32)


def _sigmoid(x):
    return 0.5 * jnp.tanh(0.5 * x) + 0.5


_GELU_C = math.sqrt(2.0 / math.pi)


def _gelu(x):
    return 0.5 * x * (1.0 + jnp.tanh(_GELU_C * (x + 0.044715 * x * x * x)))


def _gelu_grad(x):
    th = jnp.tanh(_GELU_C * (x + 0.044715 * x * x * x))
    return 0.5 * (1.0 + th) + 0.5 * x * (1.0 - th * th) * _GELU_C * (1.0 + 3.0 * 0.044715 * x * x)


def _rowwise(name, fn, ins, outs, accs=(), *, rows, tm, ncol=1):
    n_in, n_out = len(ins), len(outs)

    def body(*refs):
        res = fn(*[r[...] for r in refs[:n_in]])
        for r, v in zip(refs[n_in:n_in + n_out], res[:n_out]):
            r[...] = v.astype(r.dtype)
        first = pl.program_id(1) == 0
        for r, v in zip(refs[n_in + n_out:], res[n_out:]):
            @pl.when(first)
            def _():
                r[...] = v

            @pl.when(jnp.logical_not(first))
            def _():
                r[...] += v

    in_specs = []
    for _, width, base, kind in ins:
        if kind == "row":
            in_specs.append(pl.BlockSpec((tm, width), lambda j, i, b=base: (i, b + j)))
        else:
            in_specs.append(pl.BlockSpec((1, width), lambda j, i, b=base: (0, b + j)))
    out_specs = [pl.BlockSpec((tm, width), lambda j, i: (i, j)) for _, width, _ in outs]
    out_specs += [pl.BlockSpec((1, width), lambda j, i: (0, j)) for _, width in accs]
    out_shape = [jax.ShapeDtypeStruct((rows, total), dt) for total, _, dt in outs]
    out_shape += [jax.ShapeDtypeStruct((1, total), F32) for total, _ in accs]
    return pl.pallas_call(
        body, name=name, grid=(ncol, rows // tm), in_specs=in_specs, out_specs=out_specs, out_shape=out_shape,
        compiler_params=_params("arbitrary", "arbitrary"),
    )(*[a for a, _, _, _ in ins])


def _mm(name, a, b, *, mode, grid, a_spec, b_spec, o_spec, out_shape, acc_shape, res=None, res_spec=None,
        pair_axis=None, rider=None):
    nk = grid[2]
    dot = {"nn": _dot, "nt": _dot_nt, "tn": _dot_tn}[mode]
    a_list = list(a) if isinstance(a, tuple) else [a]
    b_list = list(b) if isinstance(b, tuple) else [b]
    na, nb = len(a_list), len(b_list)
    assert (pair_axis is None) == (na + nb == 2)
    direct = nk == 1 and pair_axis is None

    def body(*refs):
        a_refs, b_refs = refs[:na], refs[na:na + nb]
        r_ref = None if res is None else refs[na + nb]
        o_ref = refs[na + nb + (0 if res is None else 1)]

        def finish(v):
            if res is not None:
                v = v + r_ref[...]
            o_ref[...] = v.astype(o_ref.dtype)

        if direct:
            finish(dot(a_refs[0][...], b_refs[0][...]))
            return
        acc_ref = refs[-1]
        k = pl.program_id(2)

        @pl.when(k == 0)
        def _():
            acc_ref[...] = jnp.zeros_like(acc_ref)

        if pair_axis is None:
            acc_ref[...] += dot(a_refs[0][...], b_refs[0][...])
        else:
            lower = pl.program_id(pair_axis) < grid[pair_axis] // 2

            @pl.when(lower)
            def _():
                acc_ref[...] += dot(a_refs[0][...], b_refs[0][...])

            @pl.when(jnp.logical_not(lower))
            def _():
                acc_ref[...] += dot(a_refs[-1][...], b_refs[-1][...])

        @pl.when(k == nk - 1)
        def _():
            finish(acc_ref[...])

    operands = a_list + b_list + ([] if res is None else [res])
    in_specs = (list(a_spec) if na == 2 else [a_spec]) + (list(b_spec) if nb == 2 else [b_spec])
    in_specs += [] if res is None else [res_spec]
    got = _hosted_call(name, body, grid=grid, in_specs=in_specs, out_specs=[o_spec], out_shape=[out_shape],
                       scratch_shapes=[] if direct else [pltpu.VMEM(acc_shape, F32)], operands=operands, rider=rider)
    return got[0] if rider is None else (got[0][0], got[1])


MM_TILE_BUDGET_BYTES = 36 * 1024 * 1024
MM_TILE_CAP = 1024


def _mm_tile(t, row_bytes, fixed_bytes):
    cap = max(16, min(MM_TILE_CAP, (MM_TILE_BUDGET_BYTES - fixed_bytes) // row_bytes))
    return _row_tile(t, cap - cap % 16)


def _size(a):
    return jnp.dtype(a.dtype).itemsize


def _mm_fwd_cols(name, a, w3, out_dtype=F32):
    t, k = a.shape
    ns = w3.shape[2]
    tm = _mm_tile(t, 2 * k * _size(a) + 2 * ns * jnp.dtype(out_dtype).itemsize, 2 * k * ns * _size(w3))
    return _mm(name, a, w3, mode="nn", grid=(N_CHIPS, t // tm, 1),
               a_spec=pl.BlockSpec((tm, k), lambda j, i, kk: (i, 0)),
               b_spec=pl.BlockSpec((None, k, ns), lambda j, i, kk: (j, 0, 0)),
               o_spec=pl.BlockSpec((tm, ns), lambda j, i, kk: (i, j)),
               out_shape=jax.ShapeDtypeStruct((t, N_CHIPS * ns), out_dtype), acc_shape=(tm, ns))


def _mm_bwd_cols(name, d, w3, out_dtype=F32, rider=None):
    pair = isinstance(d, tuple)
    t = d[0].shape[0] if pair else d.shape[0]
    k, ns = w3.shape[1], w3.shape[2]
    dsize = _size(d[0] if pair else d)
    tm = _mm_tile(t, (4 if pair else 2) * ns * dsize + 2 * k * jnp.dtype(out_dtype).itemsize + 4 * k,
                  2 * k * ns * _size(w3))
    half = N_CHIPS // 2
    if pair:
        a_spec = (pl.BlockSpec((tm, ns), lambda i, j, kk: (i, jnp.minimum(kk, half - 1))),
                  pl.BlockSpec((tm, ns), lambda i, j, kk: (i, jnp.maximum(kk - half, 0))))
    else:
        a_spec = pl.BlockSpec((tm, ns), lambda i, j, kk: (i, kk))
    return _mm(name, d, w3, mode="nt", grid=(t // tm, 1, N_CHIPS), a_spec=a_spec,
               b_spec=pl.BlockSpec((None, k, ns), lambda i, j, kk: (kk, 0, 0)),
               o_spec=pl.BlockSpec((tm, k), lambda i, j, kk: (i, 0)),
               out_shape=jax.ShapeDtypeStruct((t, k), out_dtype), acc_shape=(tm, k), pair_axis=2 if pair else None,
               rider=rider)


def _mm_wgrad_cols(name, a, d):
    pair = isinstance(d, tuple)
    t, k = a.shape
    ns = (2 * d[0].shape[1] if pair else d.shape[1]) // N_CHIPS
    dsize = _size(d[0] if pair else d)
    tk = _mm_tile(t, 2 * k * _size(a) + (4 if pair else 2) * ns * dsize, k * ns * (4 + 2 * 2))
    half = N_CHIPS // 2
    if pair:
        b_spec = (pl.BlockSpec((tk, ns), lambda j, i, kk: (jnp.where(j < half, kk, 0), jnp.minimum(j, half - 1))),
                  pl.BlockSpec((tk, ns), lambda j, i, kk: (jnp.where(j < half, 0, kk), jnp.maximum(j - half, 0))))
    else:
        b_spec = pl.BlockSpec((tk, ns), lambda j, i, kk: (kk, j))
    return _mm(name, a, d, mode="tn", grid=(N_CHIPS, 1, t // tk),
               a_spec=pl.BlockSpec((tk, k), lambda j, i, kk: (kk, 0)), b_spec=b_spec,
               o_spec=pl.BlockSpec((None, k, ns), lambda j, i, kk: (j, 0, 0)),
               out_shape=jax.ShapeDtypeStruct((N_CHIPS, k, ns), BF16), acc_shape=(k, ns),
               pair_axis=0 if pair else None)


MM_BLOCK_CAP = 1408


def _mm_fwd_rows(name, a, w, res=None, out_dtype=F32):
    t, k = a.shape
    n = w.shape[1]
    tk = k if k <= MM_BLOCK_CAP else MM_BLOCK_CAP
    assert k % tk == 0
    row_bytes = 2 * tk * _size(a) + 2 * n * jnp.dtype(out_dtype).itemsize + (0 if res is None else 2 * n * 4) + 4 * n
    tm = _mm_tile(t, row_bytes, 2 * tk * n * _size(w))
    return _mm(name, a, w, mode="nn", grid=(t // tm, 1, k // tk),
               a_spec=pl.BlockSpec((tm, tk), lambda i, j, kk: (i, kk)),
               b_spec=pl.BlockSpec((tk, n), lambda i, j, kk: (kk, 0)),
               o_spec=pl.BlockSpec((tm, n), lambda i, j, kk: (i, 0)),
               out_shape=jax.ShapeDtypeStruct((t, n), out_dtype), acc_shape=(tm, n),
               res=res, res_spec=None if res is None else pl.BlockSpec((tm, n), lambda i, j, kk: (i, 0)))


def _mm_bwd_rows(name, d, w, out_dtype=F32):
    t, n = d.shape
    k = w.shape[0]
    tn = k if k <= MM_BLOCK_CAP else MM_BLOCK_CAP
    assert k % tn == 0
    tm = _mm_tile(t, 2 * n * _size(d) + 2 * tn * jnp.dtype(out_dtype).itemsize, 2 * tn * n * _size(w))
    return _mm(name, d, w, mode="nt", grid=(t // tm, k // tn, 1),
               a_spec=pl.BlockSpec((tm, n), lambda i, j, kk: (i, 0)),
               b_spec=pl.BlockSpec((tn, n), lambda i, j, kk: (j, 0)),
               o_spec=pl.BlockSpec((tm, tn), lambda i, j, kk: (i, j)),
               out_shape=jax.ShapeDtypeStruct((t, k), out_dtype), acc_shape=(tm, tn))


def _mm_wgrad_rows(name, a, d):
    t, k = a.shape
    n = d.shape[1]
    nblk = next(b for b in (1, 2, 4) if (k // b) % LANES == 0 and k // b <= MM_BLOCK_CAP)
    ks = k // nblk
    tk = _mm_tile(t, 2 * ks * _size(a) + 2 * n * _size(d), ks * n * (4 + 2 * 2))
    return _mm(name, a, d, mode="tn", grid=(nblk, 1, t // tk),
               a_spec=pl.BlockSpec((tk, ks), lambda j, i, kk: (kk, j)),
               b_spec=pl.BlockSpec((tk, n), lambda j, i, kk: (kk, 0)),
               o_spec=pl.BlockSpec((ks, n), lambda j, i, kk: (j, 0)),
               out_shape=jax.ShapeDtypeStruct((k, n), BF16), acc_shape=(ks, n))


def _s5_discretize(a_re, a_im, log_dt, b_re, b_im):
    dt = jnp.exp(log_dt)[:, None]
    mag = jnp.exp(a_re * dt)
    ang = a_im * dt
    lb_re = mag * jnp.cos(ang)
    lb_im = mag * jnp.sin(ang)
    den = a_re * a_re + a_im * a_im
    n_re = lb_re - 1.0
    n_im = lb_im
    co_re = ((n_re * a_re + n_im * a_im) / den)[..., None]
    co_im = ((n_im * a_re - n_re * a_im) / den)[..., None]
    bb_re = co_re * b_re - co_im * b_im
    bb_im = co_re * b_im + co_im * b_re
    return lb_re, lb_im, bb_re, bb_im


def _s5_in_blocks(bb):
    g = bb.shape[0]
    nb = g // S5_BLOCK_GROUPS
    t = bb.reshape(nb, S5_BLOCK_GROUPS, S5_STATE, S5_GROUP).transpose(0, 1, 3, 2)
    eye = jnp.eye(S5_BLOCK_GROUPS, dtype=bb.dtype)
    full = t[:, :, :, None, :] * eye[None, :, None, :, None]
    return full.reshape(nb, S5_BLOCK_GROUPS * S5_GROUP, S5_BLOCK_GROUPS * S5_STATE)


def _s5_in_blocks_diag(blocks):
    nb = blocks.shape[0]
    t = blocks.reshape(nb, S5_BLOCK_GROUPS, S5_GROUP, S5_BLOCK_GROUPS, S5_STATE)
    d = jnp.einsum("bghgp->bghp", t)
    return d.transpose(0, 1, 3, 2).reshape(nb * S5_BLOCK_GROUPS, S5_STATE, S5_GROUP)


def _s5_out_blocks(c):
    g = c.shape[0]
    nb = g // S5_BLOCK_GROUPS
    t = c.reshape(nb, S5_BLOCK_GROUPS, S5_GROUP, S5_STATE).transpose(0, 1, 3, 2)
    eye = jnp.eye(S5_BLOCK_GROUPS, dtype=c.dtype)
    full = t[:, :, :, None, :] * eye[None, :, None, :, None]
    return full.reshape(nb, S5_BLOCK_GROUPS * S5_STATE, S5_BLOCK_GROUPS * S5_GROUP)


def _s5_out_blocks_diag(blocks):
    nb = blocks.shape[0]
    t = blocks.reshape(nb, S5_BLOCK_GROUPS, S5_STATE, S5_BLOCK_GROUPS, S5_GROUP)
    d = jnp.einsum("bgpgh->bgph", t)
    return d.transpose(0, 1, 3, 2).reshape(nb * S5_BLOCK_GROUPS, S5_GROUP, S5_STATE)


def _s5_scan_tables(lr, li, reverse):
    def cmul(a, b):
        return a[0] * b[0] - a[1] * b[1], a[0] * b[1] + a[1] * b[0]

    lam = (lr, -li) if reverse else (lr, li)
    pw = [lam]
    for _ in range(SUBLANES - 1):
        pw.append(cmul(pw[-1], lam))
    sub = jnp.arange(SUBLANES)[:, None]
    rows = []
    for s in (1, 2, 4):
        keep = (sub <= SUBLANES - 1 - s) if reverse else (sub >= s)
        rows.append(jnp.where(keep, pw[s - 1][0][None, :], 0.0))
        rows.append(jnp.where(keep, pw[s - 1][1][None, :], 0.0))
    order = list(range(SUBLANES - 1, -1, -1)) if reverse else list(range(SUBLANES))
    rows.append(jnp.stack([pw[i][0] for i in order]))
    rows.append(jnp.stack([pw[i][1] for i in order]))
    return jnp.concatenate(rows, axis=0)


def _s5_scan(vre_ref, vim_ref, coef_ref, seq, width, reverse, xre_ref=None, xim_ref=None):
    nt = seq // SUBLANES
    nl = width // LANES
    per = 2 if xre_ref is None else 4
    sub = lax.broadcasted_iota(jnp.int32, (SUBLANES, LANES), 0)

    def step(k, carry):
        kk = (nt - 1 - k) if reverse else k
        rows = pl.ds(pl.multiple_of(kk * SUBLANES, SUBLANES), SUBLANES)
        out = []
        for j in range(nl):
            lanes = slice(j * LANES, (j + 1) * LANES)
            co = [coef_ref[SUBLANES * q:SUBLANES * (q + 1), lanes] for q in range(8)]
            cr, ci = carry[per * j], carry[per * j + 1]
            vr = vre_ref[rows, lanes]
            vi = vim_ref[rows, lanes]
            for q, s in enumerate((1, 2, 4)):
                sh = SUBLANES - s if reverse else s
                rr = pltpu.roll(vr, sh, 0)
                ri = pltpu.roll(vi, sh, 0)
                ar, ai = co[2 * q], co[2 * q + 1]
                vr, vi = vr + ar * rr - ai * ri, vi + ar * ri + ai * rr
            edge = 0 if reverse else SUBLANES - 1
            cbr = jnp.broadcast_to(cr[edge:edge + 1, :], (SUBLANES, LANES))
            cbi = jnp.broadcast_to(ci[edge:edge + 1, :], (SUBLANES, LANES))
            pr, pi = co[6], co[7]
            vr, vi = vr + pr * cbr - pi * cbi, vi + pr * cbi + pi * cbr
            vre_ref[rows, lanes] = vr
            vim_ref[rows, lanes] = vi
            out += [vr, vi]
            if xre_ref is not None:
                nr = jnp.where(sub == SUBLANES - 1, cbr, pltpu.roll(vr, SUBLANES - 1, 0))
                ni = jnp.where(sub == SUBLANES - 1, cbi, pltpu.roll(vi, SUBLANES - 1, 0))
                xr = xre_ref[rows, lanes]
                xi = xim_ref[rows, lanes]
                out += [carry[per * j + 2] + nr * xr + ni * xi, carry[per * j + 3] + ni * xr - nr * xi]
        return tuple(out)

    zero = jnp.zeros((SUBLANES, LANES), F32)
    res = lax.fori_loop(0, nt, step, (zero,) * (per * nl))
    if xre_ref is None:
        return None
    return jnp.concatenate(
        [jnp.concatenate([jnp.sum(res[per * j + 2], axis=0, keepdims=True) for j in range(nl)], axis=1),
         jnp.concatenate([jnp.sum(res[per * j + 3], axis=0, keepdims=True) for j in range(nl)], axis=1)], axis=0)


def _s5_fwd(z, bre3, bim3, cre3, cim3, coef, dskip, *, nseq, seq, rider=None):
    nb = bre3.shape[0]
    ch, ns = bre3.shape[1], bre3.shape[2]

    def body(za_ref, bre_ref, bim_ref, cre_ref, cim_ref, coef_ref, d_ref, y_ref, xre_ref, xim_ref):
        za = za_ref[...]
        xre_ref[...] = _dot(za, bre_ref[...])
        xim_ref[...] = _dot(za, bim_ref[...])
        _s5_scan(xre_ref, xim_ref, coef_ref, seq, ns, False)
        y_ref[...] = _dot(xre_ref[...], cre_ref[...]) - _dot(xim_ref[...], cim_ref[...]) + d_ref[...] * za

    blk3 = lambda r, c: pl.BlockSpec((None, r, c), lambda b, j: (j, 0, 0))
    return _hosted_call(
        "s5_fwd", body, grid=(nseq, nb),
        in_specs=[pl.BlockSpec((seq, ch), lambda b, j: (b, j)), blk3(ch, ns), blk3(ch, ns), blk3(ns, ch), blk3(ns, ch),
                  pl.BlockSpec((8 * SUBLANES, ns), lambda b, j: (0, j)), pl.BlockSpec((1, ch), lambda b, j: (0, j))],
        out_specs=[pl.BlockSpec((seq, ch), lambda b, j: (b, j)), pl.BlockSpec((seq, ns), lambda b, j: (b, j)),
                   pl.BlockSpec((seq, ns), lambda b, j: (b, j))],
        out_shape=[jax.ShapeDtypeStruct((nseq * seq, nb * ch), F32), jax.ShapeDtypeStruct((nseq * seq, nb * ns), F32),
                   jax.ShapeDtypeStruct((nseq * seq, nb * ns), F32)],
        operands=(z, bre3, bim3, cre3, cim3, coef, dskip), rider=rider)


def _s5_bwd(dy, z, xre, xim, bre3, bim3, cre3, cim3, coef_rev, dskip, *, nseq, seq, rider=None):
    nb = bre3.shape[0]
    ch, ns = bre3.shape[1], bre3.shape[2]

    def body(dy_ref, za_ref, xre_ref, xim_ref, bre_ref, bim_ref, cre_ref, cim_ref, coef_ref, d_ref,
             dza_ref, dbre_ref, dbim_ref, dcre_ref, dcim_ref, dlam_ref, dd_ref, are_ref, aim_ref):
        dy = dy_ref[...]
        za = za_ref[...]
        are_ref[...] = _dot_nt(dy, cre_ref[...])
        aim_ref[...] = -_dot_nt(dy, cim_ref[...])
        dlam = _s5_scan(are_ref, aim_ref, coef_ref, seq, ns, True, xre_ref, xim_ref)
        are = are_ref[...]
        aim = aim_ref[...]
        dza_ref[...] = (_dot_nt(are, bre_ref[...]) + _dot_nt(aim, bim_ref[...]) + d_ref[...] * dy).astype(dza_ref.dtype)
        parts = (_dot_tn(za, are), _dot_tn(za, aim), _dot_tn(xre_ref[...], dy), -_dot_tn(xim_ref[...], dy),
                 dlam, jnp.sum(dy * za, axis=0, keepdims=True))
        first = pl.program_id(1) == 0
        for r, v in zip((dbre_ref, dbim_ref, dcre_ref, dcim_ref, dlam_ref, dd_ref), parts):
            @pl.when(first)
            def _():
                r[...] = v

            @pl.when(jnp.logical_not(first))
            def _():
                r[...] += v

    blk3 = lambda r, c: pl.BlockSpec((None, r, c), lambda j, b: (j, 0, 0))
    tok = lambda c: pl.BlockSpec((seq, c), lambda j, b: (b, j))
    return _hosted_call(
        "s5_bwd", body, grid=(nb, nseq),
        in_specs=[tok(ch), tok(ch), tok(ns), tok(ns), blk3(ch, ns), blk3(ch, ns), blk3(ns, ch), blk3(ns, ch),
                  pl.BlockSpec((8 * SUBLANES, ns), lambda j, b: (0, j)), pl.BlockSpec((1, ch), lambda j, b: (0, j))],
        out_specs=[tok(ch), blk3(ch, ns), blk3(ch, ns), blk3(ns, ch), blk3(ns, ch),
                   pl.BlockSpec((None, 2, ns), lambda j, b: (j, 0, 0)), pl.BlockSpec((1, ch), lambda j, b: (0, j))],
        out_shape=[jax.ShapeDtypeStruct((nseq * seq, nb * ch), BF16),
                   jax.ShapeDtypeStruct((nb, ch, ns), F32), jax.ShapeDtypeStruct((nb, ch, ns), F32),
                   jax.ShapeDtypeStruct((nb, ns, ch), F32), jax.ShapeDtypeStruct((nb, ns, ch), F32),
                   jax.ShapeDtypeStruct((nb, 2, ns), F32), jax.ShapeDtypeStruct((1, nb * ch), F32)],
        scratch_shapes=[pltpu.VMEM((seq, ns), F32), pltpu.VMEM((seq, ns), F32)],
        operands=(dy, z, xre, xim, bre3, bim3, cre3, cim3, coef_rev, dskip), rider=rider)


def _cumsum_rows(x, reverse=False):
    n = x.shape[0]
    row = lax.broadcasted_iota(jnp.int32, x.shape, 0)
    s = 1
    while s < n:
        if reverse:
            x = x + jnp.where(row < n - s, pltpu.roll(x, n - s, 0), 0.0)
        else:
            x = x + jnp.where(row >= s, pltpu.roll(x, s, 0), 0.0)
        s *= 2
    return x


def _hg_gates(zq, zf, lb):
    sg = _sigmoid(zf)
    f = lb + (1.0 - lb) * sg
    sq = _sigmoid(zq)
    qa = zq * sq * (HEAD ** -0.5)
    b = _cumsum_rows(jnp.log(f))
    return sg, f, sq, qa, 1.0 - f, b


SUB = 16


def _hg_scores(qa, kk, b):
    c = qa.shape[0]
    row = lax.broadcasted_iota(jnp.int32, qa.shape, 0)
    pos = jnp.bitwise_and(row, SUB - 1)
    dmat = lax.broadcasted_iota(jnp.int32, (c, c), 0) - lax.broadcasted_iota(jnp.int32, (c, c), 1)
    p = jnp.zeros((c, c), F32)
    for d in range(SUB):
        if d == 0:
            fd = qa * kk
        else:
            e = jnp.exp(jnp.minimum(b - pltpu.roll(b, d, 0), 0.0))
            fd = jnp.where(pos >= d, qa * pltpu.roll(kk, d, 0) * e, 0.0)
        p = jnp.where(dmat == d, jnp.sum(fd, axis=1, keepdims=True), p)
    col = lax.broadcasted_iota(jnp.int32, (SUB, c), 1)
    blocks = [jnp.zeros((SUB, c), F32)]
    for r0 in range(SUB, c, SUB):
        beta = b[r0 - 1:r0, :]
        qt = qa[r0:r0 + SUB] * jnp.exp(b[r0:r0 + SUB] - beta)
        kt = kk * jnp.exp(jnp.minimum(beta - b, 0.0))
        blocks.append(jnp.where(col < r0, _dot_nt(qt, kt), 0.0))
    return p + jnp.concatenate(blocks, axis=0)


def _hg_scores_bwd(dp, qa, kk, b):
    c = qa.shape[0]
    row = lax.broadcasted_iota(jnp.int32, qa.shape, 0)
    pos = jnp.bitwise_and(row, SUB - 1)
    dmat = lax.broadcasted_iota(jnp.int32, (c, c), 0) - lax.broadcasted_iota(jnp.int32, (c, c), 1)
    dqa = jnp.zeros_like(qa)
    dkk = jnp.zeros_like(qa)
    db = jnp.zeros_like(qa)
    for d in range(SUB):
        dcol = jnp.sum(jnp.where(dmat == d, dp, 0.0), axis=1, keepdims=True)
        if d == 0:
            dqa = dqa + dcol * kk
            dkk = dkk + dcol * qa
        else:
            e = jnp.exp(jnp.minimum(b - pltpu.roll(b, d, 0), 0.0))
            w = jnp.where(pos >= d, dcol * e, 0.0)
            kr = pltpu.roll(kk, d, 0)
            dqa = dqa + w * kr
            tmp = w * qa
            dkk = dkk + pltpu.roll(tmp, c - d, 0)
            x = tmp * kr
            db = db + x - pltpu.roll(x, c - d, 0)
    col = lax.broadcasted_iota(jnp.int32, (SUB, c), 1)
    dq_blocks = [jnp.zeros((SUB, qa.shape[1]), F32)]
    db_blocks = [jnp.zeros((SUB, qa.shape[1]), F32)]
    for r0 in range(SUB, c, SUB):
        beta = b[r0 - 1:r0, :]
        eq = jnp.exp(b[r0:r0 + SUB] - beta)
        ek = jnp.exp(jnp.minimum(beta - b, 0.0))
        qt = qa[r0:r0 + SUB] * eq
        kt = kk * ek
        dpi = jnp.where(col < r0, dp[r0:r0 + SUB, :], 0.0)
        dqt = _dot(dpi, kt)
        dkt = _dot_tn(dpi, qt)
        dq_blocks.append(dqt * eq)
        db_blocks.append(dqt * qt)
        dkk = dkk + dkt * ek
        db = db - dkt * kt
    return dqa + jnp.concatenate(dq_blocks, axis=0), dkk, db + jnp.concatenate(db_blocks, axis=0)


def _hg_chunks_per_step(seq):
    nc = seq // CHUNK
    cps = next(k for k in (4, 2, 1) if nc % k == 0)
    return nc, cps, nc // cps


def _hg_fwd(z, lbrow, gain, *, nseq, seq, heads, qoff, rider=None):
    nc, cps, nblk = _hg_chunks_per_step(seq)
    blk = cps * CHUNK
    zspec = lambda off: pl.BlockSpec((blk, HEAD), lambda h, b, n, off=off: (b * nblk + n, off + h))

    def body(zq_ref, zf_ref, zi_ref, zg_ref, lb_ref, gn_ref, o_ref, yb_ref, st_ref, sc_ref, state):
        @pl.when(pl.program_id(2) == 0)
        def _():
            state[...] = jnp.zeros_like(state)

        lb = lb_ref[...]
        gain_v = gn_ref[...]

        def chunk(ci, carry):
            rows = pl.ds(pl.multiple_of(ci * CHUNK, CHUNK), CHUNK)
            st = state[...]
            st_ref[ci] = st
            zi = zi_ref[rows, :]
            zg = zg_ref[rows, :]
            _, _, _, qa, kk, b = _hg_gates(zq_ref[rows, :], zf_ref[rows, :], lb)
            scores = _hg_scores(qa, kk, b).astype(BF16)
            sc_ref[rows, :] = scores
            o = _dot_nt(qa * jnp.exp(b), st) + _dot(scores, zi)
            bl = b[CHUNK - 1:CHUNK, :]
            state[...] = st * jnp.exp(bl) + _dot_tn(zi, kk * jnp.exp(bl - b))
            o_ref[rows, :] = o
            r = lax.rsqrt(jnp.mean(o * o, axis=1, keepdims=True) + EPS)
            yb_ref[rows, :] = (o * r * gain_v * zg * _sigmoid(zg)).astype(yb_ref.dtype)
            return carry

        lax.fori_loop(0, cps, chunk, 0, unroll=True)

    tok = pl.BlockSpec((blk, HEAD), lambda h, b, n: (b * nblk + n, h))
    vec = pl.BlockSpec((1, HEAD), lambda h, b, n: (0, h))
    rows = nseq * seq
    return _hosted_call(
        "hgrn2_fwd", body, grid=(heads, nseq, nblk),
        in_specs=[zspec(qoff), zspec(qoff + heads), zspec(qoff + 2 * heads), zspec(qoff + 3 * heads), vec, vec],
        out_specs=[tok, tok, pl.BlockSpec((None, None, cps, HEAD, HEAD), lambda h, b, n: (h, b, n, 0, 0)),
                   pl.BlockSpec((None, blk, CHUNK), lambda h, b, n: (h, b * nblk + n, 0))],
        out_shape=[jax.ShapeDtypeStruct((rows, heads * HEAD), F32), jax.ShapeDtypeStruct((rows, heads * HEAD), BF16),
                   jax.ShapeDtypeStruct((heads, nseq, nc, HEAD, HEAD), F32),
                   jax.ShapeDtypeStruct((heads, rows, CHUNK), BF16)],
        scratch_shapes=[pltpu.VMEM((HEAD, HEAD), F32)], operands=(z, z, z, z, lbrow, gain), rider=rider)


def _hg_bwd(dyb, z, o, states, scores, lbrow, gain, *, nseq, seq, heads, qoff, rider=None):
    nc, cps, nblk = _hg_chunks_per_step(seq)
    blk = cps * CHUNK
    rev = lambda n: nblk - 1 - n
    zspec = lambda off: pl.BlockSpec((blk, HEAD), lambda h, b, n, off=off: (b * nblk + rev(n), off + h))

    def body(dyb_ref, zq_ref, zf_ref, zi_ref, zg_ref, o_ref, st_ref, sc_ref, lb_ref, gn_ref,
             dzq_ref, dzf_ref, dzi_ref, dzg_ref, dlb_ref, dgn_ref, dstate):
        @pl.when(pl.program_id(2) == 0)
        def _():
            dstate[...] = jnp.zeros_like(dstate)

        @pl.when(jnp.logical_and(pl.program_id(1) == 0, pl.program_id(2) == 0))
        def _():
            dlb_ref[...] = jnp.zeros_like(dlb_ref)
            dgn_ref[...] = jnp.zeros_like(dgn_ref)

        lb = lb_ref[...]
        gain_v = gn_ref[...]
        c = CHUNK
        causal = lax.broadcasted_iota(jnp.int32, (c, c), 0) >= lax.broadcasted_iota(jnp.int32, (c, c), 1)

        def chunk(step, carry):
            ci = cps - 1 - step
            rows = pl.ds(pl.multiple_of(ci * CHUNK, CHUNK), CHUNK)
            zq = zq_ref[rows, :]
            zi = zi_ref[rows, :]
            zg = zg_ref[rows, :]
            sg, f, sq, qa, kk, b = _hg_gates(zq, zf_ref[rows, :], lb)
            eb = jnp.exp(b)
            qt = qa * eb
            bl = b[c - 1:c, :]
            ebl = jnp.exp(bl)
            ekb = jnp.exp(bl - b)
            kh = kk * ekb
            st = st_ref[ci]
            dst = dstate[...]
            o = o_ref[rows, :]
            r = lax.rsqrt(jnp.mean(o * o, axis=1, keepdims=True) + EPS)
            oh = o * r
            szg = _sigmoid(zg)
            dyb = dyb_ref[rows, :]
            don = dyb * zg * szg
            dzg_ref[rows, :] = (dyb * oh * gain_v * szg * (1.0 + zg * (1.0 - szg))).astype(dzg_ref.dtype)
            doh = don * gain_v
            do = r * (doh - oh * jnp.mean(doh * oh, axis=1, keepdims=True))
            dqt = _dot(do, st)
            dp = jnp.where(causal, _dot_nt(do, zi), 0.0)
            dzi_ref[rows, :] = (_dot_tn(sc_ref[rows, :], do) + _dot_nt(kh, dst)).astype(dzi_ref.dtype)
            dkh = _dot(zi, dst)
            dbl = jnp.sum(dkh * kh, axis=0, keepdims=True) + jnp.sum(dst * st, axis=0, keepdims=True) * ebl
            dstate[...] = _dot_tn(do, qt) + dst * ebl
            dqa_s, dkk_s, db_s = _hg_scores_bwd(dp, qa, kk, b)
            dqa = dqt * eb + dqa_s
            dkk = dkh * ekb + dkk_s
            db = dqt * qt - dkh * kh + db_s
            row = lax.broadcasted_iota(jnp.int32, db.shape, 0)
            db = db + jnp.where(row == c - 1, dbl, 0.0)
            df = _cumsum_rows(db, reverse=True) / f - dkk
            dzf_ref[rows, :] = (df * (1.0 - lb) * sg * (1.0 - sg)).astype(dzf_ref.dtype)
            dzq_ref[rows, :] = (dqa * (HEAD ** -0.5) * sq * (1.0 + zq * (1.0 - sq))).astype(dzq_ref.dtype)
            dlb_ref[...] += jnp.sum(df * (1.0 - sg), axis=0, keepdims=True)
            dgn_ref[...] += jnp.sum(don * oh, axis=0, keepdims=True)
            return carry

        lax.fori_loop(0, cps, chunk, 0, unroll=2)

    tok = pl.BlockSpec((blk, HEAD), lambda h, b, n: (b * nblk + rev(n), h))
    vec = pl.BlockSpec((1, HEAD), lambda h, b, n: (0, h))
    rows = nseq * seq
    return _hosted_call(
        "hgrn2_bwd", body, grid=(heads, nseq, nblk),
        in_specs=[tok, zspec(qoff), zspec(qoff + heads), zspec(qoff + 2 * heads), zspec(qoff + 3 * heads), tok,
                  pl.BlockSpec((None, None, cps, HEAD, HEAD), lambda h, b, n: (h, b, rev(n), 0, 0)),
                  pl.BlockSpec((None, blk, CHUNK), lambda h, b, n: (h, b * nblk + rev(n), 0)), vec, vec],
        out_specs=[tok, tok, tok, tok, vec, vec],
        out_shape=[jax.ShapeDtypeStruct((rows, heads * HEAD), BF16)] * 4
        + [jax.ShapeDtypeStruct((1, heads * HEAD), F32)] * 2,
        scratch_shapes=[pltpu.VMEM((HEAD, HEAD), F32)],
        operands=(dyb, z, z, z, z, o, states, scores, lbrow, gain), rider=rider)


def _conv_taps(h, w, bias):
    row = lax.broadcasted_iota(jnp.int32, h.shape, 0)
    h1 = jnp.where(row >= 1, pltpu.roll(h, 1, 0), 0.0)
    h2 = jnp.where(row >= 2, pltpu.roll(h, 2, 0), 0.0)
    return h2 * w[0:1, :] + h1 * w[1:2, :] + h * w[2:3, :] + bias, h1, h2


def _conv_fwd(h, wconv, bconv, *, nseq, seq):
    ff2 = h.shape[1]
    ncol = ff2 // 2 // LANES

    def body(hg_ref, hv_ref, wg_ref, wv_ref, bg_ref, bv_ref, a_ref):
        g, _, _ = _conv_taps(hg_ref[...].astype(F32), wg_ref[...], bg_ref[...])
        v, _, _ = _conv_taps(hv_ref[...].astype(F32), wv_ref[...], bv_ref[...])
        a_ref[...] = (g * _sigmoid(g) * v).astype(a_ref.dtype)

    tok = lambda off: pl.BlockSpec((seq, LANES), lambda j, b, off=off: (b, off + j))
    wsp = lambda off: pl.BlockSpec((CONV_W, LANES), lambda j, b, off=off: (0, off + j))
    bsp = lambda off: pl.BlockSpec((1, LANES), lambda j, b, off=off: (0, off + j))
    return pl.pallas_call(
        body, name="conv_fwd", grid=(ncol, nseq),
        in_specs=[tok(0), tok(ncol), wsp(0), wsp(ncol), bsp(0), bsp(ncol)],
        out_specs=tok(0), out_shape=jax.ShapeDtypeStruct((nseq * seq, ff2 // 2), BF16),
        compiler_params=_params("arbitrary", "arbitrary"),
    )(h, h, wconv, wconv, bconv, bconv)


def _conv_bwd(da, h, wconv, bconv, *, nseq, seq):
    ff2 = h.shape[1]
    ncol = ff2 // 2 // LANES

    def half_bwd(d, hcur, h1, h2, w):
        n = d.shape[0]
        row = lax.broadcasted_iota(jnp.int32, d.shape, 0)
        d1 = jnp.where(row < n - 1, pltpu.roll(d, n - 1, 0), 0.0)
        d2 = jnp.where(row < n - 2, pltpu.roll(d, n - 2, 0), 0.0)
        dh = d * w[2:3, :] + d1 * w[1:2, :] + d2 * w[0:1, :]
        stats = jnp.concatenate(
            [jnp.sum(h2 * d, axis=0, keepdims=True), jnp.sum(h1 * d, axis=0, keepdims=True),
             jnp.sum(hcur * d, axis=0, keepdims=True), jnp.sum(d, axis=0, keepdims=True),
             jnp.zeros((SUBLANES - 4, d.shape[1]), F32)], axis=0)
        return dh, stats

    def body(da_ref, hg_ref, hv_ref, wg_ref, wv_ref, bg_ref, bv_ref, dhg_ref, dhv_ref, sg_ref, sv_ref):
        hg = hg_ref[...].astype(F32)
        hv = hv_ref[...].astype(F32)
        wg = wg_ref[...]
        wv = wv_ref[...]
        g, g1, g2 = _conv_taps(hg, wg, bg_ref[...])
        v, v1, v2 = _conv_taps(hv, wv, bv_ref[...])
        da = da_ref[...].astype(F32)
        s = _sigmoid(g)
        dhg, stg = half_bwd(da * v * s * (1.0 + g * (1.0 - s)), hg, g1, g2, wg)
        dhv, stv = half_bwd(da * g * s, hv, v1, v2, wv)
        dhg_ref[...] = dhg.astype(dhg_ref.dtype)
        dhv_ref[...] = dhv.astype(dhv_ref.dtype)
        first = pl.program_id(1) == 0
        for r, val in ((sg_ref, stg), (sv_ref, stv)):
            @pl.when(first)
            def _():
                r[...] = val

            @pl.when(jnp.logical_not(first))
            def _():
                r[...] += val

    tok = lambda off: pl.BlockSpec((seq, LANES), lambda j, b, off=off: (b, off + j))
    wsp = lambda off: pl.BlockSpec((CONV_W, LANES), lambda j, b, off=off: (0, off + j))
    bsp = lambda off: pl.BlockSpec((1, LANES), lambda j, b, off=off: (0, off + j))
    ssp = pl.BlockSpec((SUBLANES, LANES), lambda j, b: (0, j))
    dhg, dhv, stg, stv = pl.pallas_call(
        body, name="conv_bwd", grid=(ncol, nseq),
        in_specs=[tok(0), tok(0), tok(ncol), wsp(0), wsp(ncol), bsp(0), bsp(ncol)],
        out_specs=[tok(0), tok(0), ssp, ssp],
        out_shape=[jax.ShapeDtypeStruct((nseq * seq, ff2 // 2), BF16)] * 2
        + [jax.ShapeDtypeStruct((SUBLANES, ff2 // 2), F32)] * 2,
        compiler_params=_params("arbitrary", "arbitrary"),
    )(da, h, h, wconv, wconv, bconv, bconv)
    return (dhg, dhv), jnp.concatenate([stg, stv], axis=1)


def _rms_fwd(xv, g):
    r = lax.rsqrt(jnp.mean(xv * xv, axis=1, keepdims=True) + EPS)
    return (xv * r * g,)


def _rms_bwd(xv, g, dy, res):
    r = lax.rsqrt(jnp.mean(xv * xv, axis=1, keepdims=True) + EPS)
    xh = xv * r
    dxh = dy * g
    dx = r * (dxh - xh * jnp.mean(dxh * xh, axis=1, keepdims=True)) + res
    return dx, jnp.sum(dy * xh, axis=0, keepdims=True)


def _loss_head(x2, tgt, g):
    d = x2.shape[1]
    r = lax.rsqrt(jnp.mean(x2 * x2, axis=1, keepdims=True) + EPS)
    xh = x2 * r
    err = xh * g - tgt
    dy = err * (1.0 / d)
    dxh = dy * g
    dx = r * (dxh - xh * jnp.mean(dxh * xh, axis=1, keepdims=True))
    loss = 0.5 * jnp.sum(jnp.mean(err * err, axis=1, keepdims=True), axis=0, keepdims=True)
    return dx, jnp.sum(dy * xh, axis=0, keepdims=True), jnp.broadcast_to(loss, (1, LANES))


LATE = ("w_pa", "w_pb", "w_out", "w_up", "w_down")
EARLY_GRADS = ("w_down", "w_up", "w_out", "w_pa", "w_pb", "w_glu")
ROW_SHARDED = ("w_glu", "w_out", "w_down")


def _local_step(x, tgt, p, late, *, nseq, seq):
    p = dict(p)
    chip = 2 * lax.axis_index("x") + lax.axis_index("y")
    t, d = x.shape
    s5w = p["s5_d"].shape[1]
    hgw = p["gain"].shape[1]
    heads = hgw // HEAD
    qoff = s5w // LANES
    gblk = (s5w + 4 * hgw) // GATE_BLOCK
    ngb = d // GATE_BLOCK
    tm = _row_tile(t, 256)
    row = lambda a, w=None, base=0: (a, a.shape[1] if w is None else w, base, "row")
    vec = lambda a, w=None, base=0: (a, a.shape[1] if w is None else w, base, "vec")
    rw = functools.partial(_rowwise, rows=t, tm=tm)

    (u,) = rw("rms_mix", _rms_fwd, [row(x), vec(p["g_mix"])], [(d, d, BF16)])
    z = _mm_fwd_cols("in_proj", u, p["w_in"])

    lam_re, lam_im, bb_re, bb_im = _s5_discretize(p["s5_a_re"], p["s5_a_im"], p["s5_log_dt"], p["s5_b_re"], p["s5_b_im"])
    bre3 = _s5_in_blocks(bb_re).astype(BF16)
    bim3 = _s5_in_blocks(bb_im).astype(BF16)
    cre3 = _s5_out_blocks(p["s5_c_re"]).astype(BF16)
    cim3 = _s5_out_blocks(p["s5_c_im"]).astype(BF16)
    coef_f = _s5_scan_tables(lam_re.reshape(-1), lam_im.reshape(-1), False)
    coef_r = _s5_scan_tables(lam_re.reshape(-1), lam_im.reshape(-1), True)
    (o, yb, states, scores), landed = _hg_fwd(z, p["lbrow"], p["gain"], nseq=nseq, seq=seq, heads=heads, qoff=qoff,
                                      rider=_gather_ici_rider([late[n] for n in LATE]))
    (y5, xre, xim), gathered = _s5_fwd(z, bre3, bim3, cre3, cim3, coef_f, p["s5_d"], nseq=nseq, seq=seq,
                                       rider=_gather_pass_rider(landed))
    for n, g in zip(LATE, gathered):
        full = lax.dynamic_update_index_in_dim(g, late[n], chip, 0)
        p[n] = full.reshape(-1, full.shape[-1]) if n in ROW_SHARDED else full
    (ya0,) = rw("s5_gelu", lambda y: (_gelu(y),), [row(y5)], [(s5w, s5w, BF16)])
    gl = _mm_fwd_rows("glu_proj", ya0, p["w_glu"])
    (ya,) = rw("s5_glu", lambda y, g, b: (_gelu(y) * _sigmoid(g + b),), [row(y5), row(gl), vec(p["b_glu"])],
               [(s5w, s5w, BF16)])

    joined = lambda w3: w3.transpose(1, 0, 2).reshape(w3.shape[1], -1)
    split = lambda g: g.reshape(g.shape[0], N_CHIPS, -1).transpose(1, 0, 2)
    wpa, wpb = joined(p["w_pa"]), joined(p["w_pb"])
    pa = _mm_fwd_rows("proj_a", ya, wpa, out_dtype=BF16)
    pb = _mm_fwd_rows("proj_b", yb, wpb, out_dtype=BF16)
    gb = GATE_BLOCK
    (m,) = rw("merge", lambda ga, gbv, a, b: (_sigmoid(ga) * a + _sigmoid(gbv) * b,),
              [row(z, gb, gblk), row(z, gb, gblk + ngb), row(pa, gb), row(pb, gb)], [(d, gb, BF16)], ncol=ngb)
    x1 = _mm_fwd_rows("out_proj", m, p["w_out"], res=x)

    (u2,) = rw("rms_ffn", _rms_fwd, [row(x1), vec(p["g_ffn"])], [(d, d, BF16)])
    h = _mm_fwd_cols("up_proj", u2, p["w_up"], out_dtype=BF16)
    a = _conv_fwd(h, p["w_conv"], p["b_conv"], nseq=nseq, seq=seq)
    x2 = _mm_fwd_rows("down_proj", a, p["w_down"], res=x1)

    dx2, dg_final, lossv = rw("loss_head", _loss_head, [row(x2), row(tgt), vec(p["g_final"])], [(d, d, F32)],
                              accs=[(d, d), (LANES, LANES)])

    da = _mm_bwd_rows("down_bwd", dx2, p["w_down"], out_dtype=BF16)
    g_wdown = _mm_wgrad_rows("down_wgrad", a, dx2)
    dh, cstats = _conv_bwd(da, h, p["w_conv"], p["b_conv"], nseq=nseq, seq=seq)
    du2 = _mm_bwd_cols("up_bwd", dh, p["w_up"])
    g_wup = _mm_wgrad_cols("up_wgrad", u2, dh)
    dx1, dg_ffn = rw("rms_ffn_bwd", _rms_bwd, [row(x1), vec(p["g_ffn"]), row(du2), row(dx2)], [(d, d, F32)],
                     accs=[(d, d)])

    dm = _mm_bwd_rows("out_bwd", dx1, p["w_out"], out_dtype=BF16)
    g_wout = _mm_wgrad_rows("out_wgrad", m, dx1)

    def merge_bwd(ga, gbv, av, bv, dmv):
        sa = _sigmoid(ga)
        sb = _sigmoid(gbv)
        return dmv * sa, dmv * sb, dmv * av * sa * (1.0 - sa), dmv * bv * sb * (1.0 - sb)

    dpa, dpb, dzga, dzgb = rw("merge_bwd", merge_bwd,
                              [row(z, gb, gblk), row(z, gb, gblk + ngb), row(pa, gb), row(pb, gb), row(dm, gb)],
                              [(d, gb, BF16)] * 4, ncol=ngb)
    dya = _mm_bwd_rows("proj_a_bwd", dpa, wpa)
    g_wpa = split(_mm_wgrad_rows("proj_a_wgrad", ya, dpa))
    dyb = _mm_bwd_rows("proj_b_bwd", dpb, wpb)
    g_wpb = split(_mm_wgrad_rows("proj_b_wgrad", yb, dpb))

    def glu_bwd1(y, g, b, dyv):
        s = _sigmoid(g + b)
        dgl = dyv * _gelu(y) * s * (1.0 - s)
        return dgl, jnp.sum(dgl, axis=0, keepdims=True)

    dgl, db_glu = rw("s5_glu_bwd", glu_bwd1, [row(y5), row(gl), vec(p["b_glu"]), row(dya)], [(s5w, s5w, BF16)],
                     accs=[(s5w, s5w)])
    dgl_in = _mm_bwd_rows("glu_bwd", dgl, p["w_glu"])
    g_wglu = _mm_wgrad_rows("glu_wgrad", ya0, dgl)
    (dy5,) = rw("s5_gelu_bwd", lambda y, g, b, dyv, tv: ((dyv * _sigmoid(g + b) + tv) * _gelu_grad(y),),
                [row(y5), row(gl), vec(p["b_glu"]), row(dya), row(dgl_in)], [(s5w, s5w, F32)])
    partial = dict(w_down=g_wdown, w_up=g_wup, w_out=g_wout, w_pa=g_wpa, w_pb=g_wpb, w_glu=g_wglu)
    parts = [_grad_parts(partial[n]) for n in EARLY_GRADS]
    (dza, dbre3, dbim3, dcre3, dcim3, dlam, dd), sib = _s5_bwd(
        dy5, z, xre, xim, bre3, bim3, cre3, cim3, coef_r, p["s5_d"], nseq=nseq, seq=seq, rider=_swap_halves_rider(parts))
    pair = _pair_sums(EARLY_GRADS, parts, sib)
    (dzq, dzf, dzi, dzg, dlb, dgain), others = _hg_bwd(
        dyb, z, o, states, scores, p["lbrow"], p["gain"], nseq=nseq, seq=seq, heads=heads, qoff=qoff,
        rider=_scatter_rider(pair))
    halves = _chip_sums(EARLY_GRADS, pair, others)

    dz = jnp.concatenate([dza, dzq, dzf, dzi, dzg, dzga, dzgb], axis=1)
    du, sibs = _mm_bwd_cols("in_bwd", dz, p["w_in"], rider=_swap_sums_rider(halves))
    big = dict(zip(EARLY_GRADS, _join_halves(halves, sibs)))

    g_win = _mm_wgrad_cols("in_wgrad", u, dz)
    last = [_grad_parts(g_win)]
    pair = _pair_sums(("w_in",), last, _run_rider("grad_swap_halves", _swap_halves_rider(last)))
    (half,) = _chip_sums(("w_in",), pair, _run_rider("grad_scatter_chips", _scatter_rider(pair)))
    mid = half.shape[0] // 2
    sib_half = jnp.concatenate(_run_rider("grad_swap_sums", _swap_sums_rider([half[:mid], half[mid:]])), axis=0)
    big["w_in"] = _join_halves([half], [sib_half])[0]

    dx, dg_mix = rw("rms_mix_bwd", _rms_bwd, [row(x), vec(p["g_mix"]), row(du), row(dx1)], [(d, d, F32)],
                    accs=[(d, d)])

    gshape = lam_re.shape
    small = {
        "loss": lossv, "g_mix": dg_mix, "g_ffn": dg_ffn, "g_final": dg_final, "b_glu": db_glu, "gain": dgain,
        "lbrow": dlb, "s5_d": dd, "w_conv": cstats[0:CONV_W], "b_conv": cstats[CONV_W:CONV_W + 1],
        "lam_re": dlam[:, 0, :].reshape(gshape), "lam_im": dlam[:, 1, :].reshape(gshape),
        "bb_re": _s5_in_blocks_diag(dbre3), "bb_im": _s5_in_blocks_diag(dbim3),
        "s5_c_re": _s5_out_blocks_diag(dcre3), "s5_c_im": _s5_out_blocks_diag(dcim3),
    }
    return dx, big, small


ANY = pl.BlockSpec(memory_space=pl.ANY)


def _place():
    x, y, c = lax.axis_index("x"), lax.axis_index("y"), lax.axis_index("c")
    chips = [(1 - x, y), (x, 1 - y), (1 - x, 1 - y)]
    return x, y, c, chips


def _remote(src, dst, send_sems, recv_sems, k, to):
    return pltpu.make_async_remote_copy(src_ref=src, dst_ref=dst, send_sem=send_sems.at[k], recv_sem=recv_sems.at[k],
                                        device_id=to, device_id_type=MESH)


def _half(rows, which):
    return pl.ds(pl.multiple_of(which * (rows // 2), 16), rows // 2)


def _gather_weights(shards, whole):
    n, nw = len(shards), len(whole)
    arrays = list(shards) + list(whole)

    def body(*refs):
        in_refs, out_refs = refs[:n + nw], refs[n + nw:2 * (n + nw)]
        send_sems, recv_sems = refs[2 * (n + nw):]
        x, y, c, chips = _place()
        me = 2 * x + y
        copy = functools.partial(_remote, send_sems=send_sems, recv_sems=recv_sems)
        sends = []
        for a in range(n):
            mine_half = _half(arrays[a].shape[0], c)
            for j, (cx, cy) in enumerate(chips):
                sends.append(copy(in_refs[a].at[mine_half], out_refs[a].at[me, mine_half], k=6 * a + j, to=(cx, cy, c)))
        for a in range(n, n + nw):
            for j, (cx, cy) in enumerate(chips):
                sends.append(copy(in_refs[a], out_refs[a].at[me], k=6 * n + 3 * (a - n) + j, to=(cx, cy, c)))
        for cp in sends:
            cp.start()
        for a in range(n):
            mine_half = _half(arrays[a].shape[0], c)
            for j, (cx, cy) in enumerate(chips):
                landed = out_refs[a].at[2 * cx + cy, mine_half]
                copy(landed, landed, k=6 * a + j, to=(x, y, c)).wait_recv()
                fwd = copy(landed, landed, k=6 * a + 3 + j, to=(x, y, 1 - c))
                fwd.start()
                sends.append(fwd)
        for a in range(n):
            other_half = _half(arrays[a].shape[0], 1 - c)
            for j, (cx, cy) in enumerate(chips):
                landed = out_refs[a].at[2 * cx + cy, other_half]
                copy(landed, landed, k=6 * a + 3 + j, to=(x, y, c)).wait_recv()
        for a in range(n, n + nw):
            for j, (cx, cy) in enumerate(chips):
                landed = out_refs[a].at[2 * cx + cy]
                copy(landed, landed, k=6 * n + 3 * (a - n) + j, to=(x, y, c)).wait_recv()
        for cp in sends:
            cp.wait_send()

    nsem = 6 * n + 3 * nw
    return pl.pallas_call(
        body, name="gather_weights", out_shape=[jax.ShapeDtypeStruct((N_CHIPS,) + a.shape, a.dtype) for a in arrays],
        in_specs=[ANY] * (n + nw), out_specs=[ANY] * (n + nw),
        scratch_shapes=[pltpu.SemaphoreType.DMA((nsem,)), pltpu.SemaphoreType.DMA((nsem,))],
    )(*arrays)


def _symmetric_rider(arrays, out_shapes, copies_of, nsem):
    def start(ins, outs, send_sems, recv_sems):
        for cp in copies_of(ins, outs, send_sems, recv_sems):
            cp.start()

    def finish(ins, outs, send_sems, recv_sems):
        for cp in copies_of(ins, outs, send_sems, recv_sems):
            cp.wait()

    return _Rider(arrays, out_shapes, nsem, start, finish)


def _swap_halves_rider(parts):
    def copies_of(ins, outs, send_sems, recv_sems):
        x, y, c, _ = _place()
        return [_remote(ins[a].at[:, _half(g.shape[1], 1 - c), :], outs[a], send_sems, recv_sems, a, (x, y, 1 - c))
                for a, g in enumerate(parts)]

    shapes = [jax.ShapeDtypeStruct((g.shape[0], g.shape[1] // 2, g.shape[2]), g.dtype) for g in parts]
    return _symmetric_rider(parts, shapes, copies_of, len(parts))


def _scatter_rider(parts):
    def copies_of(ins, outs, send_sems, recv_sems):
        x, y, c, chips = _place()
        return [_remote(ins[a].at[2 * cx + cy], outs[a].at[j], send_sems, recv_sems, 3 * a + j, (cx, cy, c))
                for a in range(len(parts)) for j, (cx, cy) in enumerate(chips)]

    shapes = [jax.ShapeDtypeStruct((N_CHIPS - 1,) + h.shape[1:], h.dtype) for h in parts]
    return _symmetric_rider(parts, shapes, copies_of, 3 * len(parts))


def _swap_sums_rider(parts):
    def copies_of(ins, outs, send_sems, recv_sems):
        x, y, c, _ = _place()
        return [_remote(ins[a], outs[a], send_sems, recv_sems, a, (x, y, 1 - c)) for a in range(len(parts))]

    shapes = [jax.ShapeDtypeStruct(g.shape, g.dtype) for g in parts]
    return _symmetric_rider(parts, shapes, copies_of, len(parts))


def _gather_ici_rider(shards):
    def sends(ins, outs, send_sems, recv_sems):
        x, y, c, chips = _place()
        return [_remote(ins[a].at[_half(s.shape[0], c)], outs[a].at[2 * x + y, _half(s.shape[0], c)], send_sems,
                        recv_sems, 3 * a + j, (cx, cy, c)) for a, s in enumerate(shards) for j, (cx, cy) in enumerate(chips)]

    def start(ins, outs, send_sems, recv_sems):
        for cp in sends(ins, outs, send_sems, recv_sems):
            cp.start()

    def finish(ins, outs, send_sems, recv_sems):
        x, y, c, chips = _place()
        for a, s in enumerate(shards):
            for j, (cx, cy) in enumerate(chips):
                landed = outs[a].at[2 * cx + cy, _half(s.shape[0], c)]
                _remote(landed, landed, send_sems, recv_sems, 3 * a + j, (x, y, c)).wait_recv()
        for cp in sends(ins, outs, send_sems, recv_sems):
            cp.wait_send()

    shapes = [jax.ShapeDtypeStruct((N_CHIPS,) + s.shape, s.dtype) for s in shards]
    return _Rider(shards, shapes, 3 * len(shards), start, finish)


def _gather_pass_rider(landed):
    def sends(ins, outs, send_sems, recv_sems):
        x, y, c, chips = _place()
        return [_remote(ins[a].at[2 * cx + cy, _half(g.shape[1], c)], outs[a].at[2 * cx + cy, _half(g.shape[1], c)],
                        send_sems, recv_sems, 3 * a + j, (x, y, 1 - c))
                for a, g in enumerate(landed) for j, (cx, cy) in enumerate(chips)]

    def start(ins, outs, send_sems, recv_sems):
        for cp in sends(ins, outs, send_sems, recv_sems):
            cp.start()

    def finish(ins, outs, send_sems, recv_sems):
        x, y, c, chips = _place()
        for a, g in enumerate(landed):
            for j, (cx, cy) in enumerate(chips):
                other = outs[a].at[2 * cx + cy, _half(g.shape[1], 1 - c)]
                _remote(other, other, send_sems, recv_sems, 3 * a + j, (x, y, c)).wait_recv()
        for cp in sends(ins, outs, send_sems, recv_sems):
            cp.wait_send()

    shapes = [jax.ShapeDtypeStruct(g.shape, g.dtype) for g in landed]
    return _Rider(landed, shapes, 3 * len(landed), start, finish, aliases={a: a for a in range(len(landed))})


def _grad_parts(g):
    return g.reshape((N_CHIPS, -1, g.shape[-1]))


def _pair_sums(names, parts, sib):
    ci = lax.axis_index("c")
    out = []
    for n, g, s in zip(names, parts, sib):
        rh, cols = s.shape[1], s.shape[2]
        own = lax.dynamic_slice_in_dim(g, ci * rh, rh, axis=1)
        both = _sum_blocks("grad_pair_sum_" + n, [own.reshape(-1, cols), s.reshape(-1, cols)], BF16)
        out.append(both.reshape(N_CHIPS, rh, cols))
    return out


def _chip_sums(names, pair, others):
    chip = 2 * lax.axis_index("x") + lax.axis_index("y")
    return [_sum_blocks("grad_chip_sum_" + n, [lax.dynamic_index_in_dim(h, chip, axis=0, keepdims=False), o[0], o[1], o[2]],
                        F32) for n, h, o in zip(names, pair, others)]


def _join_halves(halves, sibs):
    ci = lax.axis_index("c")
    return [lax.dynamic_update_slice_in_dim(jnp.concatenate([own, own], axis=0), s, (1 - ci) * own.shape[0], axis=0)
            for own, s in zip(halves, sibs)]


def _gather_all(v):
    m_per, n = v.shape

    def body(x_ref, out_ref, send_sems, recv_sems):
        x, y, c, chips = _place()
        me, sibling = (x, y, c), (x, y, 1 - c)

        def rows(px, py, pc):
            return out_ref.at[pl.ds(pl.multiple_of((4 * px + 2 * py + pc) * m_per, 8), m_per), :]

        def copy(k, block, to, src=None):
            return pltpu.make_async_remote_copy(src_ref=rows(*block) if src is None else src, dst_ref=rows(*block),
                                                send_sem=send_sems.at[k], recv_sem=recv_sems.at[k], device_id=to,
                                                device_id_type=MESH)

        out_ref[pl.ds(pl.multiple_of((4 * x + 2 * y + c) * m_per, 8), m_per), :] = x_ref[...]
        first = [copy(0, me, sibling, src=x_ref)]
        first += [copy(1 + j, me, (*chip, c), src=x_ref) for j, chip in enumerate(chips)]
        for cp in first:
            cp.start()
        passed = [copy(4 + j, (*chip, c), sibling) for j, chip in enumerate(chips)]
        for j, chip in enumerate(chips):
            copy(1 + j, (*chip, c), me).wait_recv()
            passed[j].start()
        copy(0, sibling, me).wait_recv()
        for j, chip in enumerate(chips):
            copy(4 + j, (*chip, 1 - c), me).wait_recv()
        for cp in first + passed:
            cp.wait_send()

    return pl.pallas_call(
        body, name="gather_small_grads", out_shape=jax.ShapeDtypeStruct((N_DEV * m_per, n), v.dtype),
        in_specs=[pl.BlockSpec(memory_space=pltpu.VMEM)], out_specs=pl.BlockSpec(memory_space=pltpu.VMEM),
        scratch_shapes=[pltpu.SemaphoreType.DMA((7,)), pltpu.SemaphoreType.DMA((7,))],
        compiler_params=pltpu.CompilerParams(vmem_limit_bytes=VMEM_LIMIT_BYTES),
    )(v)


def _sum_blocks(name, parts, out_dtype):
    rows, cols = parts[0].shape
    tm = _row_tile(rows, 512)

    def body(*refs):
        acc = refs[0][...].astype(F32)
        for r in refs[1:-1]:
            acc = acc + r[...].astype(F32)
        refs[-1][...] = acc.astype(refs[-1].dtype)

    spec = pl.BlockSpec((tm, cols), lambda i: (i, 0))
    return pl.pallas_call(
        body, name=name, grid=(rows // tm,), in_specs=[spec] * len(parts), out_specs=spec,
        out_shape=jax.ShapeDtypeStruct((rows, cols), out_dtype), compiler_params=_params("arbitrary"),
    )(*parts)


def _adamw_math(wv, gv, mv, vv):
    m2 = ADAM_B1 * mv + (1.0 - ADAM_B1) * gv
    v2 = ADAM_B2 * vv + (1.0 - ADAM_B2) * (gv * gv)
    delta = -ADAM_LR * ((m2 / (1.0 - ADAM_B1 ** ADAM_STEP)) / (jnp.sqrt(v2 / (1.0 - ADAM_B2 ** ADAM_STEP)) + ADAM_EPS)
                        + ADAM_WD * wv)
    return delta, m2, v2


def _adamw_small(ws, gs, ms, vs):
    n = len(ws)

    def body(*refs):
        for i in range(n):
            res = _adamw_math(refs[i][...], refs[n + i][...], refs[2 * n + i][...], refs[3 * n + i][...])
            for k in range(3):
                refs[(4 + k) * n + i][...] = res[k]

    vm = pl.BlockSpec(memory_space=pltpu.VMEM)
    outs = pl.pallas_call(
        body, name="adamw_small", in_specs=[vm] * (4 * n), out_specs=[vm] * (3 * n),
        out_shape=[jax.ShapeDtypeStruct(a.shape, F32) for a in ws] * 3,
        compiler_params=pltpu.CompilerParams(vmem_limit_bytes=VMEM_LIMIT_BYTES),
    )(*ws, *gs, *ms, *vs)
    return outs[:n], outs[n:2 * n], outs[2 * n:]


def _adamw(name, w, g, m, v):
    rows, cols = w.shape
    ins = [(a, cols, 0, "row") for a in (w, g, m, v)]
    return _rowwise(name, _adamw_math, ins, [(cols, cols, F32)] * 3, rows=rows, tm=_row_tile(rows, 256))


PACK_ROWS = 256


def _pack(flat_parts, dtype, lead=()):
    parts = [a.astype(dtype).reshape(lead + (-1,)) for a in flat_parts]
    n = sum(a.shape[-1] for a in parts)
    chunk = PACK_ROWS * LANES
    total = -(-n // chunk) * chunk
    if total > n:
        parts.append(jnp.zeros(lead + (total - n,), dtype))
    return jnp.concatenate(parts, axis=-1).reshape(lead + (total // LANES, LANES))


def _unpack(buf, shapes, lead=()):
    flat = buf.reshape(lead + (-1,))
    out, off = [], 0
    for shp in shapes:
        n = math.prod(shp)
        out.append(lax.slice_in_dim(flat, off, off + n, axis=len(lead)).reshape(lead + tuple(shp)))
        off += n
    return out


BIG = ("w_in", "w_glu", "w_pa", "w_pb", "w_out", "w_up", "w_down")
WEIGHTS = ("g_mix", "w_in", "s5_a_re", "s5_a_im", "s5_log_dt", "s5_b_re", "s5_b_im", "s5_c_re", "s5_c_im", "s5_d",
           "w_glu", "b_glu", "hg_lb_logits", "hg_norm_gain", "w_pa", "w_pb", "w_out", "g_ffn", "w_up", "w_conv",
           "b_conv", "w_down", "g_final")
SMALL = tuple(n for n in WEIGHTS if n not in BIG)
SMALL_PARTS = ("loss", "g_mix", "g_ffn", "g_final", "b_glu", "gain", "lbrow", "s5_d", "w_conv", "b_conv", "lam_re",
               "lam_im", "bb_re", "bb_im", "s5_c_re", "s5_c_im")


def _lower_bound(logits):
    return jnp.cumsum(jax.nn.softmax(logits, axis=0), axis=0)[0:1]


def kernel(x, g_mix, w_in, s5_a_re, s5_a_im, s5_log_dt, s5_b_re, s5_b_im, s5_c_re, s5_c_im, s5_d, w_glu, b_glu, hg_lb_logits, hg_norm_gain, w_pa, w_pb, w_out, g_ffn, w_up, w_conv, b_conv, w_down, g_final, loss_target, m_g_mix, m_w_in, m_s5_a_re, m_s5_a_im, m_s5_log_dt, m_s5_b_re, m_s5_b_im, m_s5_c_re, m_s5_c_im, m_s5_d, m_w_glu, m_b_glu, m_hg_lb_logits, m_hg_norm_gain, m_w_pa, m_w_pb, m_w_out, m_g_ffn, m_w_up, m_w_conv, m_b_conv, m_w_down, m_g_final, v_g_mix, v_w_in, v_s5_a_re, v_s5_a_im, v_s5_log_dt, v_s5_b_re, v_s5_b_im, v_s5_c_re, v_s5_c_im, v_s5_d, v_w_glu, v_b_glu, v_hg_lb_logits, v_hg_norm_gain, v_w_pa, v_w_pb, v_w_out, v_g_ffn, v_w_up, v_w_conv, v_b_conv, v_w_down, v_g_final):
    args = dict(locals())
    w = {n: args[n] for n in WEIGHTS}
    mom = {n: args["m_" + n] for n in WEIGHTS}
    var = {n: args["v_" + n] for n in WEIGHTS}
    nseq, seq, d = x.shape
    xi, yi = lax.axis_index("x"), lax.axis_index("y")
    chip = 2 * xi + yi

    shard = {n: w[n][0] for n in BIG}
    shard16 = {n: shard[n].astype(BF16) for n in BIG}
    first = ("w_in", "w_glu")
    got = _gather_weights([shard16[n] for n in first], [w_conv[0]])
    p = {n: lax.dynamic_update_index_in_dim(g, shard16[n], chip, 0) for n, g in zip(first, got)}
    p["w_glu"] = p["w_glu"].reshape(-1, p["w_glu"].shape[-1])
    conv_all = lax.dynamic_update_index_in_dim(got[-1], w_conv[0], chip, 0)
    p.update(g_mix=g_mix, g_ffn=g_ffn, g_final=g_final.reshape(1, -1), b_glu=b_glu, gain=hg_norm_gain, s5_d=s5_d,
             b_conv=b_conv, w_conv=conv_all.transpose(1, 0, 2).reshape(CONV_W, -1), lbrow=_lower_bound(hg_lb_logits),
             s5_a_re=s5_a_re[0], s5_a_im=s5_a_im[0], s5_log_dt=s5_log_dt[0], s5_b_re=s5_b_re[0], s5_b_im=s5_b_im[0],
             s5_c_re=s5_c_re[0], s5_c_im=s5_c_im[0])

    dx, grads, gsmall = _local_step(x.reshape(nseq * seq, d), loss_target.reshape(nseq * seq, d), p,
                                    {n: shard16[n] for n in LATE}, nseq=nseq, seq=seq)

    small_shapes = [gsmall[n].shape for n in SMALL_PARTS]
    vec = _pack([gsmall[n] for n in SMALL_PARTS], F32)
    gathered = _gather_all(vec)
    mrows = vec.shape[0]
    vsum = _sum_blocks("small_grad_sum", [gathered[i * mrows:(i + 1) * mrows] for i in range(N_DEV)], F32)
    sm = dict(zip(SMALL_PARTS, _unpack(vsum, small_shapes)))
    loss = sm["loss"][0, 0]

    _, disc_vjp = jax.vjp(_s5_discretize, p["s5_a_re"], p["s5_a_im"], p["s5_log_dt"], p["s5_b_re"], p["s5_b_im"])
    da_re, da_im, dlog_dt, db_re, db_im = disc_vjp((sm["lam_re"], sm["lam_im"], sm["bb_re"], sm["bb_im"]))
    _, lb_vjp = jax.vjp(_lower_bound, hg_lb_logits)
    (dlogits,) = lb_vjp(sm["lbrow"])
    fcols = w_conv.shape[-1]
    grads.update(
        g_mix=sm["g_mix"], g_ffn=sm["g_ffn"], g_final=sm["g_final"].reshape(-1), b_glu=sm["b_glu"],
        hg_norm_gain=sm["gain"], hg_lb_logits=dlogits, s5_d=sm["s5_d"], b_conv=sm["b_conv"],
        w_conv=lax.dynamic_slice_in_dim(sm["w_conv"], chip * fcols, fcols, axis=1),
        s5_a_re=da_re, s5_a_im=da_im, s5_log_dt=dlog_dt, s5_b_re=db_re, s5_b_im=db_im,
        s5_c_re=sm["s5_c_re"], s5_c_im=sm["s5_c_im"])
    grads = {n: grads[n].reshape(w[n].shape) for n in WEIGHTS}

    delta, new_m, new_v = {}, {}, {}
    for n in BIG:
        shp = shard[n].shape
        dl, m2, v2 = _adamw("adamw_" + n, shard[n], grads[n].reshape(shp), mom[n].reshape(shp), var[n].reshape(shp))
        delta[n], new_m[n], new_v[n] = dl, m2, v2
    def natural(a):
        return a.reshape(1, -1) if a.ndim == 1 else (a[0] if a.ndim > 2 else a)

    outs = _adamw_small(*[[natural(src[n]) for n in SMALL] for src in (w, grads, mom, var)])
    for dst, group in zip((delta, new_m, new_v), outs):
        dst.update(zip(SMALL, group))
    res = [loss, dx.reshape(x.shape)]
    for group in (grads, delta, new_m, new_v):
        res += [group[n].reshape(w[n].shape) for n in WEIGHTS]
    return tuple(res)
```

```python
import functools
import math

import jax
import jax.numpy as jnp
from jax import lax
from jax.experimental import pallas as pl
from jax.experimental.pallas import tpu as pltpu

F32 = jnp.float32
BF16 = jnp.bfloat16
MESH = pl.DeviceIdType.MESH

EPS = 1e-6
S5_GROUP = 16
S5_STATE = 64
S5_BLOCK_GROUPS = 8
HEAD = 128
CHUNK = 64
CONV_W = 3
LANES = 128
SUBLANES = 8
GATE_BLOCK = 512
VMEM_LIMIT_BYTES = 56 * 1024 * 1024

ADAM_LR = 0.001
ADAM_B1 = 0.9
ADAM_B2 = 0.999
ADAM_EPS = 1e-08
ADAM_WD = 0.01
ADAM_STEP = 10

N_CHIPS = 4
N_DEV = 8


def _params(*sem):
    return pltpu.CompilerParams(dimension_semantics=sem, vmem_limit_bytes=VMEM_LIMIT_BYTES)


class _Rider:
    def __init__(self, arrays, out_shapes, nsem, start, finish, aliases=None):
        self.arrays, self.out_shapes, self.nsem = list(arrays), list(out_shapes), nsem
        self.start, self.finish, self.aliases = start, finish, dict(aliases or {})


def _hosted_call(name, body, *, grid, in_specs, out_specs, out_shape, operands, scratch_shapes=(), rider=None):
    in_specs, out_specs, out_shape, scratch_shapes = list(in_specs), list(out_specs), list(out_shape), list(scratch_shapes)
    cparams = _params(*(["arbitrary"] * len(grid)))
    if rider is None:
        return pl.pallas_call(body, name=name, grid=grid, in_specs=in_specs, out_specs=out_specs, out_shape=out_shape,
                              scratch_shapes=scratch_shapes, compiler_params=cparams)(*operands)
    n_in, n_out, n_sc = len(in_specs), len(out_specs), len(scratch_shapes)
    r_in, r_out = len(rider.arrays), len(rider.out_shapes)

    def hosted(*refs):
        ins, rins = refs[:n_in], refs[n_in:n_in + r_in]
        outs = refs[n_in + r_in:n_in + r_in + n_out]
        routs = refs[n_in + r_in + n_out:n_in + r_in + n_out + r_out]
        rest = refs[n_in + r_in + n_out + r_out:]
        send_sems, recv_sems = rest[n_sc], rest[n_sc + 1]
        first = functools.reduce(jnp.logical_and, [pl.program_id(i) == 0 for i in range(len(grid))])
        last = functools.reduce(jnp.logical_and, [pl.program_id(i) == grid[i] - 1 for i in range(len(grid))])

        @pl.when(first)
        def _():
            rider.start(rins, routs, send_sems, recv_sems)

        body(*ins, *outs, *rest[:n_sc])

        @pl.when(last)
        def _():
            rider.finish(rins, routs, send_sems, recv_sems)

    res = pl.pallas_call(
        hosted, name=name, grid=grid, in_specs=in_specs + [ANY] * r_in, out_specs=out_specs + [ANY] * r_out,
        out_shape=out_shape + rider.out_shapes,
        scratch_shapes=scratch_shapes + [pltpu.SemaphoreType.DMA((rider.nsem,)), pltpu.SemaphoreType.DMA((rider.nsem,))],
        input_output_aliases={n_in + i: n_out + o for i, o in rider.aliases.items()}, compiler_params=cparams,
    )(*operands, *rider.arrays)
    return res[:n_out], res[n_out:]


def _run_rider(name, rider):
    r_in, r_out = len(rider.arrays), len(rider.out_shapes)

    def body(*refs):
        rins, routs, send_sems, recv_sems = refs[:r_in], refs[r_in:r_in + r_out], refs[-2], refs[-1]
        rider.start(rins, routs, send_sems, recv_sems)
        rider.finish(rins, routs, send_sems, recv_sems)

    return pl.pallas_call(
        body, name=name, in_specs=[ANY] * r_in, out_specs=[ANY] * r_out, out_shape=rider.out_shapes,
        scratch_shapes=[pltpu.SemaphoreType.DMA((rider.nsem,)), pltpu.SemaphoreType.DMA((rider.nsem,))],
        input_output_aliases=rider.aliases,
    )(*rider.arrays)


def _row_tile(rows, cap):
    if rows <= cap:
        return rows
    for t in range(cap - cap % 8, 7, -8):
        if rows % t == 0:
            return t
    raise ValueError(f"no row tile for {rows}")


def _dot(a, b):
    return jnp.dot(a.astype(BF16), b.astype(BF16), preferred_element_type=F32)


def _dot_nt(a, b):
    return lax.dot_general(a.astype(BF16), b.astype(BF16), (((1,), (1,)), ((), ())), preferred_element_type=F32)


def _dot_tn(a, b):
    return lax.dot_general(a.astype(BF16), b.astype(BF16), (((0,), (0,)), ((), ())), preferred_element_type=F32)


def _sigmoid(x):
    return 0.5 * jnp.tanh(0.5 * x) + 0.5


_GELU_C = math.sqrt(2.0 / math.pi)


def _gelu(x):
    return 0.5 * x * (1.0 + jnp.tanh(_GELU_C * (x + 0.044715 * x * x * x)))


def _gelu_grad(x):
    th = jnp.tanh(_GELU_C * (x + 0.044715 * x * x * x))
    return 0.5 * (1.0 + th) + 0.5 * x * (1.0 - th * th) * _GELU_C * (1.0 + 3.0 * 0.044715 * x * x)


def _rowwise(name, fn, ins, outs, accs=(), *, rows, tm, ncol=1):
    n_in, n_out = len(ins), len(outs)

    def body(*refs):
        res = fn(*[r[...] for r in refs[:n_in]])
        for r, v in zip(refs[n_in:n_in + n_out], res[:n_out]):
            r[...] = v.astype(r.dtype)
        first = pl.program_id(1) == 0
        for r, v in zip(refs[n_in + n_out:], res[n_out:]):
            @pl.when(first)
            def _():
                r[...] = v

            @pl.when(jnp.logical_not(first))
            def _():
                r[...] += v

    in_specs = []
    for _, width, base, kind in ins:
        if kind == "row":
            in_specs.append(pl.BlockSpec((tm, width), lambda j, i, b=base: (i, b + j)))
        else:
            in_specs.append(pl.BlockSpec((1, width), lambda j, i, b=base: (0, b + j)))
    out_specs = [pl.BlockSpec((tm, width), lambda j, i: (i, j)) for _, width, _ in outs]
    out_specs += [pl.BlockSpec((1, width), lambda j, i: (0, j)) for _, width in accs]
    out_shape = [jax.ShapeDtypeStruct((rows, total), dt) for total, _, dt in outs]
    out_shape += [jax.ShapeDtypeStruct((1, total), F32) for total, _ in accs]
    return pl.pallas_call(
        body, name=name, grid=(ncol, rows // tm), in_specs=in_specs, out_specs=out_specs, out_shape=out_shape,
        compiler_params=_params("arbitrary", "arbitrary"),
    )(*[a for a, _, _, _ in ins])


def _mm(name, a, b, *, mode, grid, a_spec, b_spec, o_spec, out_shape, acc_shape, res=None, res_spec=None,
        pair_axis=None, rider=None):
    nk = grid[2]
    dot = {"nn": _dot, "nt": _dot_nt, "tn": _dot_tn}[mode]
    a_list = list(a) if isinstance(a, tuple) else [a]
    b_list = list(b) if isinstance(b, tuple) else [b]
    na, nb = len(a_list), len(b_list)
    assert (pair_axis is None) == (na + nb == 2)
    direct = nk == 1 and pair_axis is None

    def body(*refs):
        a_refs, b_refs = refs[:na], refs[na:na + nb]
        r_ref = None if res is None else refs[na + nb]
        o_ref = refs[na + nb + (0 if res is None else 1)]

        def finish(v):
            if res is not None:
                v = v + r_ref[...]
            o_ref[...] = v.astype(o_ref.dtype)

        if direct:
            finish(dot(a_refs[0][...], b_refs[0][...]))
            return
        acc_ref = refs[-1]
        k = pl.program_id(2)

        @pl.when(k == 0)
        def _():
            acc_ref[...] = jnp.zeros_like(acc_ref)

        if pair_axis is None:
            acc_ref[...] += dot(a_refs[0][...], b_refs[0][...])
        else:
            lower = pl.program_id(pair_axis) < grid[pair_axis] // 2

            @pl.when(lower)
            def _():
                acc_ref[...] += dot(a_refs[0][...], b_refs[0][...])

            @pl.when(jnp.logical_not(lower))
            def _():
                acc_ref[...] += dot(a_refs[-1][...], b_refs[-1][...])

        @pl.when(k == nk - 1)
        def _():
            finish(acc_ref[...])

    operands = a_list + b_list + ([] if res is None else [res])
    in_specs = (list(a_spec) if na == 2 else [a_spec]) + (list(b_spec) if nb == 2 else [b_spec])
    in_specs += [] if res is None else [res_spec]
    got = _hosted_call(name, body, grid=grid, in_specs=in_specs, out_specs=[o_spec], out_shape=[out_shape],
                       scratch_shapes=[] if direct else [pltpu.VMEM(acc_shape, F32)], operands=operands, rider=rider)
    return got[0] if rider is None else (got[0][0], got[1])


MM_TILE_BUDGET_BYTES = 36 * 1024 * 1024
MM_TILE_CAP = 1024


def _mm_tile(t, row_bytes, fixed_bytes):
    cap = max(16, min(MM_TILE_CAP, (MM_TILE_BUDGET_BYTES - fixed_bytes) // row_bytes))
    return _row_tile(t, cap - cap % 16)


def _size(a):
    return jnp.dtype(a.dtype).itemsize


def _mm_fwd_cols(name, a, w3, out_dtype=F32, rider=None):
    t, k = a.shape
    ns = w3.shape[2]
    tm = _mm_tile(t, 2 * k * _size(a) + 2 * ns * jnp.dtype(out_dtype).itemsize, 2 * k * ns * _size(w3))
    return _mm(name, a, w3, mode="nn", grid=(N_CHIPS, t // tm, 1),
               a_spec=pl.BlockSpec((tm, k), lambda j, i, kk: (i, 0)),
               b_spec=pl.BlockSpec((None, k, ns), lambda j, i, kk: (j, 0, 0)),
               o_spec=pl.BlockSpec((tm, ns), lambda j, i, kk: (i, j)),
               out_shape=jax.ShapeDtypeStruct((t, N_CHIPS * ns), out_dtype), acc_shape=(tm, ns), rider=rider)


def _mm_bwd_cols(name, d, w3, out_dtype=F32, rider=None):
    pair = isinstance(d, tuple)
    t = d[0].shape[0] if pair else d.shape[0]
    k, ns = w3.shape[1], w3.shape[2]
    dsize = _size(d[0] if pair else d)
    tm = _mm_tile(t, (4 if pair else 2) * ns * dsize + 2 * k * jnp.dtype(out_dtype).itemsize + 4 * k,
                  2 * k * ns * _size(w3))
    half = N_CHIPS // 2
    if pair:
        a_spec = (pl.BlockSpec((tm, ns), lambda i, j, kk: (i, jnp.minimum(kk, half - 1))),
                  pl.BlockSpec((tm, ns), lambda i, j, kk: (i, jnp.maximum(kk - half, 0))))
    else:
        a_spec = pl.BlockSpec((tm, ns), lambda i, j, kk: (i, kk))
    return _mm(name, d, w3, mode="nt", grid=(t // tm, 1, N_CHIPS), a_spec=a_spec,
               b_spec=pl.BlockSpec((None, k, ns), lambda i, j, kk: (kk, 0, 0)),
               o_spec=pl.BlockSpec((tm, k), lambda i, j, kk: (i, 0)),
               out_shape=jax.ShapeDtypeStruct((t, k), out_dtype), acc_shape=(tm, k), pair_axis=2 if pair else None,
               rider=rider)


def _mm_wgrad_cols(name, a, d, rider=None):
    pair = isinstance(d, tuple)
    t, k = a.shape
    ns = (2 * d[0].shape[1] if pair else d.shape[1]) // N_CHIPS
    dsize = _size(d[0] if pair else d)
    tk = _mm_tile(t, 2 * k * _size(a) + (4 if pair else 2) * ns * dsize, k * ns * (4 + 2 * 2))
    half = N_CHIPS // 2
    if pair:
        b_spec = (pl.BlockSpec((tk, ns), lambda j, i, kk: (jnp.where(j < half, kk, 0), jnp.minimum(j, half - 1))),
                  pl.BlockSpec((tk, ns), lambda j, i, kk: (jnp.where(j < half, 0, kk), jnp.maximum(j - half, 0))))
    else:
        b_spec = pl.BlockSpec((tk, ns), lambda j, i, kk: (kk, j))
    return _mm(name, a, d, mode="tn", grid=(N_CHIPS, 1, t // tk),
               a_spec=pl.BlockSpec((tk, k), lambda j, i, kk: (kk, 0)), b_spec=b_spec,
               o_spec=pl.BlockSpec((None, k, ns), lambda j, i, kk: (j, 0, 0)),
               out_shape=jax.ShapeDtypeStruct((N_CHIPS, k, ns), BF16), acc_shape=(k, ns),
               pair_axis=0 if pair else None, rider=rider)


MM_BLOCK_CAP = 1408


def _mm_fwd_rows(name, a, w, res=None, out_dtype=F32):
    t, k = a.shape
    n = w.shape[1]
    tk = k if k <= MM_BLOCK_CAP else MM_BLOCK_CAP
    assert k % tk == 0
    row_bytes = 2 * tk * _size(a) + 2 * n * jnp.dtype(out_dtype).itemsize + (0 if res is None else 2 * n * 4) + 4 * n
    tm = _mm_tile(t, row_bytes, 2 * tk * n * _size(w))
    return _mm(name, a, w, mode="nn", grid=(t // tm, 1, k // tk),
               a_spec=pl.BlockSpec((tm, tk), lambda i, j, kk: (i, kk)),
               b_spec=pl.BlockSpec((tk, n), lambda i, j, kk: (kk, 0)),
               o_spec=pl.BlockSpec((tm, n), lambda i, j, kk: (i, 0)),
               out_shape=jax.ShapeDtypeStruct((t, n), out_dtype), acc_shape=(tm, n),
               res=res, res_spec=None if res is None else pl.BlockSpec((tm, n), lambda i, j, kk: (i, 0)))


def _mm_bwd_rows(name, d, w, out_dtype=F32):
    t, n = d.shape
    k = w.shape[0]
    tn = k if k <= MM_BLOCK_CAP else MM_BLOCK_CAP
    assert k % tn == 0
    tm = _mm_tile(t, 2 * n * _size(d) + 2 * tn * jnp.dtype(out_dtype).itemsize, 2 * tn * n * _size(w))
    return _mm(name, d, w, mode="nt", grid=(t // tm, k // tn, 1),
               a_spec=pl.BlockSpec((tm, n), lambda i, j, kk: (i, 0)),
               b_spec=pl.BlockSpec((tn, n), lambda i, j, kk: (j, 0)),
               o_spec=pl.BlockSpec((tm, tn), lambda i, j, kk: (i, j)),
               out_shape=jax.ShapeDtypeStruct((t, k), out_dtype), acc_shape=(tm, tn))


def _mm_wgrad_rows(name, a, d):
    t, k = a.shape
    n = d.shape[1]
    nblk = next(b for b in (1, 2, 4) if (k // b) % LANES == 0 and k // b <= MM_BLOCK_CAP)
    ks = k // nblk
    tk = _mm_tile(t, 2 * ks * _size(a) + 2 * n * _size(d), ks * n * (4 + 2 * 2))
    return _mm(name, a, d, mode="tn", grid=(nblk, 1, t // tk),
               a_spec=pl.BlockSpec((tk, ks), lambda j, i, kk: (kk, j)),
               b_spec=pl.BlockSpec((tk, n), lambda j, i, kk: (kk, 0)),
               o_spec=pl.BlockSpec((ks, n), lambda j, i, kk: (j, 0)),
               out_shape=jax.ShapeDtypeStruct((k, n), BF16), acc_shape=(ks, n))


def _s5_discretize(a_re, a_im, log_dt, b_re, b_im):
    dt = jnp.exp(log_dt)[:, None]
    mag = jnp.exp(a_re * dt)
    ang = a_im * dt
    lb_re = mag * jnp.cos(ang)
    lb_im = mag * jnp.sin(ang)
    den = a_re * a_re + a_im * a_im
    n_re = lb_re - 1.0
    n_im = lb_im
    co_re = ((n_re * a_re + n_im * a_im) / den)[..., None]
    co_im = ((n_im * a_re - n_re * a_im) / den)[..., None]
    bb_re = co_re * b_re - co_im * b_im
    bb_im = co_re * b_im + co_im * b_re
    return lb_re, lb_im, bb_re, bb_im


def _s5_in_blocks(bb):
    g = bb.shape[0]
    nb = g // S5_BLOCK_GROUPS
    t = bb.reshape(nb, S5_BLOCK_GROUPS, S5_STATE, S5_GROUP).transpose(0, 1, 3, 2)
    eye = jnp.eye(S5_BLOCK_GROUPS, dtype=bb.dtype)
    full = t[:, :, :, None, :] * eye[None, :, None, :, None]
    return full.reshape(nb, S5_BLOCK_GROUPS * S5_GROUP, S5_BLOCK_GROUPS * S5_STATE)


def _s5_in_blocks_diag(blocks):
    nb = blocks.shape[0]
    t = blocks.reshape(nb, S5_BLOCK_GROUPS, S5_GROUP, S5_BLOCK_GROUPS, S5_STATE)
    d = jnp.einsum("bghgp->bghp", t)
    return d.transpose(0, 1, 3, 2).reshape(nb * S5_BLOCK_GROUPS, S5_STATE, S5_GROUP)


def _s5_out_blocks(c):
    g = c.shape[0]
    nb = g // S5_BLOCK_GROUPS
    t = c.reshape(nb, S5_BLOCK_GROUPS, S5_GROUP, S5_STATE).transpose(0, 1, 3, 2)
    eye = jnp.eye(S5_BLOCK_GROUPS, dtype=c.dtype)
    full = t[:, :, :, None, :] * eye[None, :, None, :, None]
    return full.reshape(nb, S5_BLOCK_GROUPS * S5_STATE, S5_BLOCK_GROUPS * S5_GROUP)


def _s5_out_blocks_diag(blocks):
    nb = blocks.shape[0]
    t = blocks.reshape(nb, S5_BLOCK_GROUPS, S5_STATE, S5_BLOCK_GROUPS, S5_GROUP)
    d = jnp.einsum("bgpgh->bgph", t)
    return d.transpose(0, 1, 3, 2).reshape(nb * S5_BLOCK_GROUPS, S5_GROUP, S5_STATE)


def _s5_scan_tables(lr, li, reverse):
    def cmul(a, b):
        return a[0] * b[0] - a[1] * b[1], a[0] * b[1] + a[1] * b[0]

    lam = (lr, -li) if reverse else (lr, li)
    pw = [lam]
    for _ in range(SUBLANES - 1):
        pw.append(cmul(pw[-1], lam))
    sub = jnp.arange(SUBLANES)[:, None]
    rows = []
    for s in (1, 2, 4):
        keep = (sub <= SUBLANES - 1 - s) if reverse else (sub >= s)
        rows.append(jnp.where(keep, pw[s - 1][0][None, :], 0.0))
        rows.append(jnp.where(keep, pw[s - 1][1][None, :], 0.0))
    order = list(range(SUBLANES - 1, -1, -1)) if reverse else list(range(SUBLANES))
    rows.append(jnp.stack([pw[i][0] for i in order]))
    rows.append(jnp.stack([pw[i][1] for i in order]))
    return jnp.concatenate(rows, axis=0)


def _s5_scan(vre_ref, vim_ref, coef_ref, seq, width, reverse, xre_ref=None, xim_ref=None):
    nt = seq // SUBLANES
    nl = width // LANES
    per = 2 if xre_ref is None else 4
    sub = lax.broadcasted_iota(jnp.int32, (SUBLANES, LANES), 0)

    def step(k, carry):
        kk = (nt - 1 - k) if reverse else k
        rows = pl.ds(pl.multiple_of(kk * SUBLANES, SUBLANES), SUBLANES)
        out = []
        for j in range(nl):
            lanes = slice(j * LANES, (j + 1) * LANES)
            co = [coef_ref[SUBLANES * q:SUBLANES * (q + 1), lanes] for q in range(8)]
            cr, ci = carry[per * j], carry[per * j + 1]
            vr = vre_ref[rows, lanes]
            vi = vim_ref[rows, lanes]
            for q, s in enumerate((1, 2, 4)):
                sh = SUBLANES - s if reverse else s
                rr = pltpu.roll(vr, sh, 0)
                ri = pltpu.roll(vi, sh, 0)
                ar, ai = co[2 * q], co[2 * q + 1]
                vr, vi = vr + ar * rr - ai * ri, vi + ar * ri + ai * rr
            edge = 0 if reverse else SUBLANES - 1
            cbr = jnp.broadcast_to(cr[edge:edge + 1, :], (SUBLANES, LANES))
            cbi = jnp.broadcast_to(ci[edge:edge + 1, :], (SUBLANES, LANES))
            pr, pi = co[6], co[7]
            vr, vi = vr + pr * cbr - pi * cbi, vi + pr * cbi + pi * cbr
            vre_ref[rows, lanes] = vr
            vim_ref[rows, lanes] = vi
            out += [vr, vi]
            if xre_ref is not None:
                nr = jnp.where(sub == SUBLANES - 1, cbr, pltpu.roll(vr, SUBLANES - 1, 0))
                ni = jnp.where(sub == SUBLANES - 1, cbi, pltpu.roll(vi, SUBLANES - 1, 0))
                xr = xre_ref[rows, lanes]
                xi = xim_ref[rows, lanes]
                out += [carry[per * j + 2] + nr * xr + ni * xi, carry[per * j + 3] + ni * xr - nr * xi]
        return tuple(out)

    zero = jnp.zeros((SUBLANES, LANES), F32)
    res = lax.fori_loop(0, nt, step, (zero,) * (per * nl))
    if xre_ref is None:
        return None
    return jnp.concatenate(
        [jnp.concatenate([jnp.sum(res[per * j + 2], axis=0, keepdims=True) for j in range(nl)], axis=1),
         jnp.concatenate([jnp.sum(res[per * j + 3], axis=0, keepdims=True) for j in range(nl)], axis=1)], axis=0)


def _s5_fwd(z, bre3, bim3, cre3, cim3, coef, dskip, *, nseq, seq, rider=None):
    nb = bre3.shape[0]
    ch, ns = bre3.shape[1], bre3.shape[2]

    def body(za_ref, bre_ref, bim_ref, cre_ref, cim_ref, coef_ref, d_ref, y_ref, xre_ref, xim_ref):
        za = za_ref[...]
        xre_ref[...] = _dot(za, bre_ref[...])
        xim_ref[...] = _dot(za, bim_ref[...])
        _s5_scan(xre_ref, xim_ref, coef_ref, seq, ns, False)
        y_ref[...] = _dot(xre_ref[...], cre_ref[...]) - _dot(xim_ref[...], cim_ref[...]) + d_ref[...] * za

    blk3 = lambda r, c: pl.BlockSpec((None, r, c), lambda b, j: (j, 0, 0))
    return _hosted_call(
        "s5_fwd", body, grid=(nseq, nb),
        in_specs=[pl.BlockSpec((seq, ch), lambda b, j: (b, j)), blk3(ch, ns), blk3(ch, ns), blk3(ns, ch), blk3(ns, ch),
                  pl.BlockSpec((8 * SUBLANES, ns), lambda b, j: (0, j)), pl.BlockSpec((1, ch), lambda b, j: (0, j))],
        out_specs=[pl.BlockSpec((seq, ch), lambda b, j: (b, j)), pl.BlockSpec((seq, ns), lambda b, j: (b, j)),
                   pl.BlockSpec((seq, ns), lambda b, j: (b, j))],
        out_shape=[jax.ShapeDtypeStruct((nseq * seq, nb * ch), F32), jax.ShapeDtypeStruct((nseq * seq, nb * ns), F32),
                   jax.ShapeDtypeStruct((nseq * seq, nb * ns), F32)],
        operands=(z, bre3, bim3, cre3, cim3, coef, dskip), rider=rider)


def _s5_bwd(dy, z, xre, xim, bre3, bim3, cre3, cim3, coef_rev, dskip, *, nseq, seq, rider=None):
    nb = bre3.shape[0]
    ch, ns = bre3.shape[1], bre3.shape[2]

    def body(dy_ref, za_ref, xre_ref, xim_ref, bre_ref, bim_ref, cre_ref, cim_ref, coef_ref, d_ref,
             dza_ref, dbre_ref, dbim_ref, dcre_ref, dcim_ref, dlam_ref, dd_ref, are_ref, aim_ref):
        dy = dy_ref[...]
        za = za_ref[...]
        are_ref[...] = _dot_nt(dy, cre_ref[...])
        aim_ref[...] = -_dot_nt(dy, cim_ref[...])
        dlam = _s5_scan(are_ref, aim_ref, coef_ref, seq, ns, True, xre_ref, xim_ref)
        are = are_ref[...]
        aim = aim_ref[...]
        dza_ref[...] = (_dot_nt(are, bre_ref[...]) + _dot_nt(aim, bim_ref[...]) + d_ref[...] * dy).astype(dza_ref.dtype)
        parts = (_dot_tn(za, are), _dot_tn(za, aim), _dot_tn(xre_ref[...], dy), -_dot_tn(xim_ref[...], dy),
                 dlam, jnp.sum(dy * za, axis=0, keepdims=True))
        first = pl.program_id(1) == 0
        for r, v in zip((dbre_ref, dbim_ref, dcre_ref, dcim_ref, dlam_ref, dd_ref), parts):
            @pl.when(first)
            def _():
                r[...] = v

            @pl.when(jnp.logical_not(first))
            def _():
                r[...] += v

    blk3 = lambda r, c: pl.BlockSpec((None, r, c), lambda j, b: (j, 0, 0))
    tok = lambda c: pl.BlockSpec((seq, c), lambda j, b: (b, j))
    return _hosted_call(
        "s5_bwd", body, grid=(nb, nseq),
        in_specs=[tok(ch), tok(ch), tok(ns), tok(ns), blk3(ch, ns), blk3(ch, ns), blk3(ns, ch), blk3(ns, ch),
                  pl.BlockSpec((8 * SUBLANES, ns), lambda j, b: (0, j)), pl.BlockSpec((1, ch), lambda j, b: (0, j))],
        out_specs=[tok(ch), blk3(ch, ns), blk3(ch, ns), blk3(ns, ch), blk3(ns, ch),
                   pl.BlockSpec((None, 2, ns), lambda j, b: (j, 0, 0)), pl.BlockSpec((1, ch), lambda j, b: (0, j))],
        out_shape=[jax.ShapeDtypeStruct((nseq * seq, nb * ch), BF16),
                   jax.ShapeDtypeStruct((nb, ch, ns), F32), jax.ShapeDtypeStruct((nb, ch, ns), F32),
                   jax.ShapeDtypeStruct((nb, ns, ch), F32), jax.ShapeDtypeStruct((nb, ns, ch), F32),
                   jax.ShapeDtypeStruct((nb, 2, ns), F32), jax.ShapeDtypeStruct((1, nb * ch), F32)],
        scratch_shapes=[pltpu.VMEM((seq, ns), F32), pltpu.VMEM((seq, ns), F32)],
        operands=(dy, z, xre, xim, bre3, bim3, cre3, cim3, coef_rev, dskip), rider=rider)


def _cumsum_rows(x, reverse=False):
    n = x.shape[0]
    row = lax.broadcasted_iota(jnp.int32, x.shape, 0)
    s = 1
    while s < n:
        if reverse:
            x = x + jnp.where(row < n - s, pltpu.roll(x, n - s, 0), 0.0)
        else:
            x = x + jnp.where(row >= s, pltpu.roll(x, s, 0), 0.0)
        s *= 2
    return x


def _hg_gates(zq, zf, lb):
    sg = _sigmoid(zf)
    f = lb + (1.0 - lb) * sg
    sq = _sigmoid(zq)
    qa = zq * sq * (HEAD ** -0.5)
    b = _cumsum_rows(jnp.log(f))
    return sg, f, sq, qa, 1.0 - f, b


SUB = 16


def _hg_scores(qa, kk, b):
    c = qa.shape[0]
    row = lax.broadcasted_iota(jnp.int32, qa.shape, 0)
    pos = jnp.bitwise_and(row, SUB - 1)
    dmat = lax.broadcasted_iota(jnp.int32, (c, c), 0) - lax.broadcasted_iota(jnp.int32, (c, c), 1)
    p = jnp.zeros((c, c), F32)
    for d in range(SUB):
        if d == 0:
            fd = qa * kk
        else:
            e = jnp.exp(jnp.minimum(b - pltpu.roll(b, d, 0), 0.0))
            fd = jnp.where(pos >= d, qa * pltpu.roll(kk, d, 0) * e, 0.0)
        p = jnp.where(dmat == d, jnp.sum(fd, axis=1, keepdims=True), p)
    col = lax.broadcasted_iota(jnp.int32, (SUB, c), 1)
    blocks = [jnp.zeros((SUB, c), F32)]
    for r0 in range(SUB, c, SUB):
        beta = b[r0 - 1:r0, :]
        qt = qa[r0:r0 + SUB] * jnp.exp(b[r0:r0 + SUB] - beta)
        kt = kk * jnp.exp(jnp.minimum(beta - b, 0.0))
        blocks.append(jnp.where(col < r0, _dot_nt(qt, kt), 0.0))
    return p + jnp.concatenate(blocks, axis=0)


def _hg_scores_bwd(dp, qa, kk, b):
    c = qa.shape[0]
    row = lax.broadcasted_iota(jnp.int32, qa.shape, 0)
    pos = jnp.bitwise_and(row, SUB - 1)
    dmat = lax.broadcasted_iota(jnp.int32, (c, c), 0) - lax.broadcasted_iota(jnp.int32, (c, c), 1)
    dqa = jnp.zeros_like(qa)
    dkk = jnp.zeros_like(qa)
    db = jnp.zeros_like(qa)
    for d in range(SUB):
        dcol = jnp.sum(jnp.where(dmat == d, dp, 0.0), axis=1, keepdims=True)
        if d == 0:
            dqa = dqa + dcol * kk
            dkk = dkk + dcol * qa
        else:
            e = jnp.exp(jnp.minimum(b - pltpu.roll(b, d, 0), 0.0))
            w = jnp.where(pos >= d, dcol * e, 0.0)
            kr = pltpu.roll(kk, d, 0)
            dqa = dqa + w * kr
            tmp = w * qa
            dkk = dkk + pltpu.roll(tmp, c - d, 0)
            x = tmp * kr
            db = db + x - pltpu.roll(x, c - d, 0)
    col = lax.broadcasted_iota(jnp.int32, (SUB, c), 1)
    dq_blocks = [jnp.zeros((SUB, qa.shape[1]), F32)]
    db_blocks = [jnp.zeros((SUB, qa.shape[1]), F32)]
    for r0 in range(SUB, c, SUB):
        beta = b[r0 - 1:r0, :]
        eq = jnp.exp(b[r0:r0 + SUB] - beta)
        ek = jnp.exp(jnp.minimum(beta - b, 0.0))
        qt = qa[r0:r0 + SUB] * eq
        kt = kk * ek
        dpi = jnp.where(col < r0, dp[r0:r0 + SUB, :], 0.0)
        dqt = _dot(dpi, kt)
        dkt = _dot_tn(dpi, qt)
        dq_blocks.append(dqt * eq)
        db_blocks.append(dqt * qt)
        dkk = dkk + dkt * ek
        db = db - dkt * kt
    return dqa + jnp.concatenate(dq_blocks, axis=0), dkk, db + jnp.concatenate(db_blocks, axis=0)


def _hg_chunks_per_step(seq):
    nc = seq // CHUNK
    cps = next(k for k in (4, 2, 1) if nc % k == 0)
    return nc, cps, nc // cps


def _hg_fwd(z, lbrow, gain, *, nseq, seq, heads, qoff, rider=None):
    nc, cps, nblk = _hg_chunks_per_step(seq)
    blk = cps * CHUNK
    zspec = lambda off: pl.BlockSpec((blk, HEAD), lambda h, b, n, off=off: (b * nblk + n, off + h))

    def body(zq_ref, zf_ref, zi_ref, zg_ref, lb_ref, gn_ref, o_ref, yb_ref, st_ref, sc_ref, state):
        @pl.when(pl.program_id(2) == 0)
        def _():
            state[...] = jnp.zeros_like(state)

        lb = lb_ref[...]
        gain_v = gn_ref[...]

        def chunk(ci, carry):
            rows = pl.ds(pl.multiple_of(ci * CHUNK, CHUNK), CHUNK)
            st = state[...]
            st_ref[ci] = st
            zi = zi_ref[rows, :]
            zg = zg_ref[rows, :]
            _, _, _, qa, kk, b = _hg_gates(zq_ref[rows, :], zf_ref[rows, :], lb)
            scores = _hg_scores(qa, kk, b).astype(BF16)
            sc_ref[rows, :] = scores
            o = _dot_nt(qa * jnp.exp(b), st) + _dot(scores, zi)
            bl = b[CHUNK - 1:CHUNK, :]
            state[...] = st * jnp.exp(bl) + _dot_tn(zi, kk * jnp.exp(bl - b))
            o_ref[rows, :] = o
            r = lax.rsqrt(jnp.mean(o * o, axis=1, keepdims=True) + EPS)
            yb_ref[rows, :] = (o * r * gain_v * zg * _sigmoid(zg)).astype(yb_ref.dtype)
            return carry

        lax.fori_loop(0, cps, chunk, 0, unroll=True)

    tok = pl.BlockSpec((blk, HEAD), lambda h, b, n: (b * nblk + n, h))
    vec = pl.BlockSpec((1, HEAD), lambda h, b, n: (0, h))
    rows = nseq * seq
    return _hosted_call(
        "hgrn2_fwd", body, grid=(heads, nseq, nblk),
        in_specs=[zspec(qoff), zspec(qoff + heads), zspec(qoff + 2 * heads), zspec(qoff + 3 * heads), vec, vec],
        out_specs=[tok, tok, pl.BlockSpec((None, None, cps, HEAD, HEAD), lambda h, b, n: (h, b, n, 0, 0)),
                   pl.BlockSpec((None, blk, CHUNK), lambda h, b, n: (h, b * nblk + n, 0))],
        out_shape=[jax.ShapeDtypeStruct((rows, heads * HEAD), F32), jax.ShapeDtypeStruct((rows, heads * HEAD), BF16),
                   jax.ShapeDtypeStruct((heads, nseq, nc, HEAD, HEAD), F32),
                   jax.ShapeDtypeStruct((heads, rows, CHUNK), BF16)],
        scratch_shapes=[pltpu.VMEM((HEAD, HEAD), F32)], operands=(z, z, z, z, lbrow, gain), rider=rider)


def _hg_bwd(dyb, z, o, states, scores, lbrow, gain, *, nseq, seq, heads, qoff, rider=None):
    nc, cps, nblk = _hg_chunks_per_step(seq)
    blk = cps * CHUNK
    rev = lambda n: nblk - 1 - n
    zspec = lambda off: pl.BlockSpec((blk, HEAD), lambda h, b, n, off=off: (b * nblk + rev(n), off + h))

    def body(dyb_ref, zq_ref, zf_ref, zi_ref, zg_ref, o_ref, st_ref, sc_ref, lb_ref, gn_ref,
             dzq_ref, dzf_ref, dzi_ref, dzg_ref, dlb_ref, dgn_ref, dstate):
        @pl.when(pl.program_id(2) == 0)
        def _():
            dstate[...] = jnp.zeros_like(dstate)

        @pl.when(jnp.logical_and(pl.program_id(1) == 0, pl.program_id(2) == 0))
        def _():
            dlb_ref[...] = jnp.zeros_like(dlb_ref)
            dgn_ref[...] = jnp.zeros_like(dgn_ref)

        lb = lb_ref[...]
        gain_v = gn_ref[...]
        c = CHUNK
        causal = lax.broadcasted_iota(jnp.int32, (c, c), 0) >= lax.broadcasted_iota(jnp.int32, (c, c), 1)

        def chunk(step, carry):
            ci = cps - 1 - step
            rows = pl.ds(pl.multiple_of(ci * CHUNK, CHUNK), CHUNK)
            zq = zq_ref[rows, :]
            zi = zi_ref[rows, :]
            zg = zg_ref[rows, :]
            sg, f, sq, qa, kk, b = _hg_gates(zq, zf_ref[rows, :], lb)
            eb = jnp.exp(b)
            qt = qa * eb
            bl = b[c - 1:c, :]
            ebl = jnp.exp(bl)
            ekb = jnp.exp(bl - b)
            kh = kk * ekb
            st = st_ref[ci]
            dst = dstate[...]
            o = o_ref[rows, :]
            r = lax.rsqrt(jnp.mean(o * o, axis=1, keepdims=True) + EPS)
            oh = o * r
            szg = _sigmoid(zg)
            dyb = dyb_ref[rows, :]
            don = dyb * zg * szg
            dzg_ref[rows, :] = (dyb * oh * gain_v * szg * (1.0 + zg * (1.0 - szg))).astype(dzg_ref.dtype)
            doh = don * gain_v
            do = r * (doh - oh * jnp.mean(doh * oh, axis=1, keepdims=True))
            dqt = _dot(do, st)
            dp = jnp.where(causal, _dot_nt(do, zi), 0.0)
            dzi_ref[rows, :] = (_dot_tn(sc_ref[rows, :], do) + _dot_nt(kh, dst)).astype(dzi_ref.dtype)
            dkh = _dot(zi, dst)
            dbl = jnp.sum(dkh * kh, axis=0, keepdims=True) + jnp.sum(dst * st, axis=0, keepdims=True) * ebl
            dstate[...] = _dot_tn(do, qt) + dst * ebl
            dqa_s, dkk_s, db_s = _hg_scores_bwd(dp, qa, kk, b)
            dqa = dqt * eb + dqa_s
            dkk = dkh * ekb + dkk_s
            db = dqt * qt - dkh * kh + db_s
            row = lax.broadcasted_iota(jnp.int32, db.shape, 0)
            db = db + jnp.where(row == c - 1, dbl, 0.0)
            df = _cumsum_rows(db, reverse=True) / f - dkk
            dzf_ref[rows, :] = (df * (1.0 - lb) * sg * (1.0 - sg)).astype(dzf_ref.dtype)
            dzq_ref[rows, :] = (dqa * (HEAD ** -0.5) * sq * (1.0 + zq * (1.0 - sq))).astype(dzq_ref.dtype)
            dlb_ref[...] += jnp.sum(df * (1.0 - sg), axis=0, keepdims=True)
            dgn_ref[...] += jnp.sum(don * oh, axis=0, keepdims=True)
            return carry

        lax.fori_loop(0, cps, chunk, 0, unroll=2)

    tok = pl.BlockSpec((blk, HEAD), lambda h, b, n: (b * nblk + rev(n), h))
    vec = pl.BlockSpec((1, HEAD), lambda h, b, n: (0, h))
    rows = nseq * seq
    return _hosted_call(
        "hgrn2_bwd", body, grid=(heads, nseq, nblk),
        in_specs=[tok, zspec(qoff), zspec(qoff + heads), zspec(qoff + 2 * heads), zspec(qoff + 3 * heads), tok,
                  pl.BlockSpec((None, None, cps, HEAD, HEAD), lambda h, b, n: (h, b, rev(n), 0, 0)),
                  pl.BlockSpec((None, blk, CHUNK), lambda h, b, n: (h, b * nblk + rev(n), 0)), vec, vec],
        out_specs=[tok, tok, tok, tok, vec, vec],
        out_shape=[jax.ShapeDtypeStruct((rows, heads * HEAD), BF16)] * 4
        + [jax.ShapeDtypeStruct((1, heads * HEAD), F32)] * 2,
        scratch_shapes=[pltpu.VMEM((HEAD, HEAD), F32)],
        operands=(dyb, z, z, z, z, o, states, scores, lbrow, gain), rider=rider)


def _conv_taps(h, w, bias):
    row = lax.broadcasted_iota(jnp.int32, h.shape, 0)
    h1 = jnp.where(row >= 1, pltpu.roll(h, 1, 0), 0.0)
    h2 = jnp.where(row >= 2, pltpu.roll(h, 2, 0), 0.0)
    return h2 * w[0:1, :] + h1 * w[1:2, :] + h * w[2:3, :] + bias, h1, h2


def _conv_fwd(h, wconv, bconv, *, nseq, seq):
    ff2 = h.shape[1]
    ncol = ff2 // 2 // LANES

    def body(hg_ref, hv_ref, wg_ref, wv_ref, bg_ref, bv_ref, a_ref):
        g, _, _ = _conv_taps(hg_ref[...].astype(F32), wg_ref[...], bg_ref[...])
        v, _, _ = _conv_taps(hv_ref[...].astype(F32), wv_ref[...], bv_ref[...])
        a_ref[...] = (g * _sigmoid(g) * v).astype(a_ref.dtype)

    tok = lambda off: pl.BlockSpec((seq, LANES), lambda j, b, off=off: (b, off + j))
    wsp = lambda off: pl.BlockSpec((CONV_W, LANES), lambda j, b, off=off: (0, off + j))
    bsp = lambda off: pl.BlockSpec((1, LANES), lambda j, b, off=off: (0, off + j))
    return pl.pallas_call(
        body, name="conv_fwd", grid=(ncol, nseq),
        in_specs=[tok(0), tok(ncol), wsp(0), wsp(ncol), bsp(0), bsp(ncol)],
        out_specs=tok(0), out_shape=jax.ShapeDtypeStruct((nseq * seq, ff2 // 2), BF16),
        compiler_params=_params("arbitrary", "arbitrary"),
    )(h, h, wconv, wconv, bconv, bconv)


def _conv_bwd(da, h, wconv, bconv, *, nseq, seq):
    ff2 = h.shape[1]
    ncol = ff2 // 2 // LANES

    def half_bwd(d, hcur, h1, h2, w):
        n = d.shape[0]
        row = lax.broadcasted_iota(jnp.int32, d.shape, 0)
        d1 = jnp.where(row < n - 1, pltpu.roll(d, n - 1, 0), 0.0)
        d2 = jnp.where(row < n - 2, pltpu.roll(d, n - 2, 0), 0.0)
        dh = d * w[2:3, :] + d1 * w[1:2, :] + d2 * w[0:1, :]
        stats = jnp.concatenate(
            [jnp.sum(h2 * d, axis=0, keepdims=True), jnp.sum(h1 * d, axis=0, keepdims=True),
             jnp.sum(hcur * d, axis=0, keepdims=True), jnp.sum(d, axis=0, keepdims=True),
             jnp.zeros((SUBLANES - 4, d.shape[1]), F32)], axis=0)
        return dh, stats

    def body(da_ref, hg_ref, hv_ref, wg_ref, wv_ref, bg_ref, bv_ref, dhg_ref, dhv_ref, sg_ref, sv_ref):
        hg = hg_ref[...].astype(F32)
        hv = hv_ref[...].astype(F32)
        wg = wg_ref[...]
        wv = wv_ref[...]
        g, g1, g2 = _conv_taps(hg, wg, bg_ref[...])
        v, v1, v2 = _conv_taps(hv, wv, bv_ref[...])
        da = da_ref[...].astype(F32)
        s = _sigmoid(g)
        dhg, stg = half_bwd(da * v * s * (1.0 + g * (1.0 - s)), hg, g1, g2, wg)
        dhv, stv = half_bwd(da * g * s, hv, v1, v2, wv)
        dhg_ref[...] = dhg.astype(dhg_ref.dtype)
        dhv_ref[...] = dhv.astype(dhv_ref.dtype)
        first = pl.program_id(1) == 0
        for r, val in ((sg_ref, stg), (sv_ref, stv)):
            @pl.when(first)
            def _():
                r[...] = val

            @pl.when(jnp.logical_not(first))
            def _():
                r[...] += val

    tok = lambda off: pl.BlockSpec((seq, LANES), lambda j, b, off=off: (b, off + j))
    wsp = lambda off: pl.BlockSpec((CONV_W, LANES), lambda j, b, off=off: (0, off + j))
    bsp = lambda off: pl.BlockSpec((1, LANES), lambda j, b, off=off: (0, off + j))
    ssp = pl.BlockSpec((SUBLANES, LANES), lambda j, b: (0, j))
    dhg, dhv, stg, stv = pl.pallas_call(
        body, name="conv_bwd", grid=(ncol, nseq),
        in_specs=[tok(0), tok(0), tok(ncol), wsp(0), wsp(ncol), bsp(0), bsp(ncol)],
        out_specs=[tok(0), tok(0), ssp, ssp],
        out_shape=[jax.ShapeDtypeStruct((nseq * seq, ff2 // 2), BF16)] * 2
        + [jax.ShapeDtypeStruct((SUBLANES, ff2 // 2), F32)] * 2,
        compiler_params=_params("arbitrary", "arbitrary"),
    )(da, h, h, wconv, wconv, bconv, bconv)
    return (dhg, dhv), jnp.concatenate([stg, stv], axis=1)


def _rms_fwd(xv, g):
    r = lax.rsqrt(jnp.mean(xv * xv, axis=1, keepdims=True) + EPS)
    return (xv * r * g,)


def _rms_bwd(xv, g, dy, res):
    r = lax.rsqrt(jnp.mean(xv * xv, axis=1, keepdims=True) + EPS)
    xh = xv * r
    dxh = dy * g
    dx = r * (dxh - xh * jnp.mean(dxh * xh, axis=1, keepdims=True)) + res
    return dx, jnp.sum(dy * xh, axis=0, keepdims=True)


def _loss_head(x2, tgt, g):
    d = x2.shape[1]
    r = lax.rsqrt(jnp.mean(x2 * x2, axis=1, keepdims=True) + EPS)
    xh = x2 * r
    err = xh * g - tgt
    dy = err * (1.0 / d)
    dxh = dy * g
    dx = r * (dxh - xh * jnp.mean(dxh * xh, axis=1, keepdims=True))
    loss = 0.5 * jnp.sum(jnp.mean(err * err, axis=1, keepdims=True), axis=0, keepdims=True)
    return dx, jnp.sum(dy * xh, axis=0, keepdims=True), jnp.broadcast_to(loss, (1, LANES))


LATE_A = ("w_down", "w_out")
LATE_B = ("w_up", "w_pa", "w_pb")
LATE = LATE_A + LATE_B
EARLY_GRADS = ("w_down", "w_up", "w_out", "w_pa", "w_pb", "w_glu")
ROW_SHARDED = ("w_glu", "w_out", "w_down")


def _local_step(x, tgt, p, late, *, nseq, seq):
    p = dict(p)
    chip = 2 * lax.axis_index("x") + lax.axis_index("y")
    t, d = x.shape
    s5w = p["s5_d"].shape[1]
    hgw = p["gain"].shape[1]
    heads = hgw // HEAD
    qoff = s5w // LANES
    gblk = (s5w + 4 * hgw) // GATE_BLOCK
    ngb = d // GATE_BLOCK
    tm = _row_tile(t, 256)
    row = lambda a, w=None, base=0: (a, a.shape[1] if w is None else w, base, "row")
    vec = lambda a, w=None, base=0: (a, a.shape[1] if w is None else w, base, "vec")
    rw = functools.partial(_rowwise, rows=t, tm=tm)

    (u,) = rw("rms_mix", _rms_fwd, [row(x), vec(p["g_mix"])], [(d, d, BF16)])
    z, landed_a = _mm_fwd_cols("in_proj", u, p["w_in"], rider=_gather_ici_rider([late[n] for n in LATE_A]))

    lam_re, lam_im, bb_re, bb_im = _s5_discretize(p["s5_a_re"], p["s5_a_im"], p["s5_log_dt"], p["s5_b_re"], p["s5_b_im"])
    bre3 = _s5_in_blocks(bb_re).astype(BF16)
    bim3 = _s5_in_blocks(bb_im).astype(BF16)
    cre3 = _s5_out_blocks(p["s5_c_re"]).astype(BF16)
    cim3 = _s5_out_blocks(p["s5_c_im"]).astype(BF16)
    coef_f = _s5_scan_tables(lam_re.reshape(-1), lam_im.reshape(-1), False)
    coef_r = _s5_scan_tables(lam_re.reshape(-1), lam_im.reshape(-1), True)
    (o, yb, states, scores), landed_b = _hg_fwd(z, p["lbrow"], p["gain"], nseq=nseq, seq=seq, heads=heads, qoff=qoff,
                                                rider=_gather_ici_rider([late[n] for n in LATE_B]))
    (y5, xre, xim), gathered = _s5_fwd(z, bre3, bim3, cre3, cim3, coef_f, p["s5_d"], nseq=nseq, seq=seq,
                                       rider=_gather_pass_rider(list(landed_a) + list(landed_b)))
    for n, g in zip(LATE, gathered):
        full = lax.dynamic_update_index_in_dim(g, late[n], chip, 0)
        p[n] = full.reshape(-1, full.shape[-1]) if n in ROW_SHARDED else full
    (ya0,) = rw("s5_gelu", lambda y: (_gelu(y),), [row(y5)], [(s5w, s5w, BF16)])
    gl = _mm_fwd_rows("glu_proj", ya0, p["w_glu"])
    (ya,) = rw("s5_glu", lambda y, g, b: (_gelu(y) * _sigmoid(g + b),), [row(y5), row(gl), vec(p["b_glu"])],
               [(s5w, s5w, BF16)])

    joined = lambda w3: w3.transpose(1, 0, 2).reshape(w3.shape[1], -1)
    split = lambda g: g.reshape(g.shape[0], N_CHIPS, -1).transpose(1, 0, 2)
    wpa, wpb = joined(p["w_pa"]), joined(p["w_pb"])
    pa = _mm_fwd_rows("proj_a", ya, wpa, out_dtype=BF16)
    pb = _mm_fwd_rows("proj_b", yb, wpb, out_dtype=BF16)
    gb = GATE_BLOCK
    (m,) = rw("merge", lambda ga, gbv, a, b: (_sigmoid(ga) * a + _sigmoid(gbv) * b,),
              [row(z, gb, gblk), row(z, gb, gblk + ngb), row(pa, gb), row(pb, gb)], [(d, gb, BF16)], ncol=ngb)
    x1 = _mm_fwd_rows("out_proj", m, p["w_out"], res=x)

    (u2,) = rw("rms_ffn", _rms_fwd, [row(x1), vec(p["g_ffn"])], [(d, d, BF16)])
    h = _mm_fwd_cols("up_proj", u2, p["w_up"], out_dtype=BF16)
    a = _conv_fwd(h, p["w_conv"], p["b_conv"], nseq=nseq, seq=seq)
    x2 = _mm_fwd_rows("down_proj", a, p["w_down"], res=x1)

    dx2, dg_final, lossv = rw("loss_head", _loss_head, [row(x2), row(tgt), vec(p["g_final"])], [(d, d, F32)],
                              accs=[(d, d), (LANES, LANES)])

    da = _mm_bwd_rows("down_bwd", dx2, p["w_down"], out_dtype=BF16)
    g_wdown = _mm_wgrad_rows("down_wgrad", a, dx2)
    dh, cstats = _conv_bwd(da, h, p["w_conv"], p["b_conv"], nseq=nseq, seq=seq)
    du2 = _mm_bwd_cols("up_bwd", dh, p["w_up"])
    g_wup = _mm_wgrad_cols("up_wgrad", u2, dh)
    dx1, dg_ffn = rw("rms_ffn_bwd", _rms_bwd, [row(x1), vec(p["g_ffn"]), row(du2), row(dx2)], [(d, d, F32)],
                     accs=[(d, d)])

    dm = _mm_bwd_rows("out_bwd", dx1, p["w_out"], out_dtype=BF16)
    g_wout = _mm_wgrad_rows("out_wgrad", m, dx1)

    def merge_bwd(ga, gbv, av, bv, dmv):
        sa = _sigmoid(ga)
        sb = _sigmoid(gbv)
        return dmv * sa, dmv * sb, dmv * av * sa * (1.0 - sa), dmv * bv * sb * (1.0 - sb)

    dpa, dpb, dzga, dzgb = rw("merge_bwd", merge_bwd,
                              [row(z, gb, gblk), row(z, gb, gblk + ngb), row(pa, gb), row(pb, gb), row(dm, gb)],
                              [(d, gb, BF16)] * 4, ncol=ngb)
    dya = _mm_bwd_rows("proj_a_bwd", dpa, wpa)
    g_wpa = split(_mm_wgrad_rows("proj_a_wgrad", ya, dpa))
    dyb = _mm_bwd_rows("proj_b_bwd", dpb, wpb)
    g_wpb = split(_mm_wgrad_rows("proj_b_wgrad", yb, dpb))

    def glu_bwd1(y, g, b, dyv):
        s = _sigmoid(g + b)
        dgl = dyv * _gelu(y) * s * (1.0 - s)
        return dgl, jnp.sum(dgl, axis=0, keepdims=True)

    dgl, db_glu = rw("s5_glu_bwd", glu_bwd1, [row(y5), row(gl), vec(p["b_glu"]), row(dya)], [(s5w, s5w, BF16)],
                     accs=[(s5w, s5w)])
    dgl_in = _mm_bwd_rows("glu_bwd", dgl, p["w_glu"])
    g_wglu = _mm_wgrad_rows("glu_wgrad", ya0, dgl)
    (dy5,) = rw("s5_gelu_bwd", lambda y, g, b, dyv, tv: ((dyv * _sigmoid(g + b) + tv) * _gelu_grad(y),),
                [row(y5), row(gl), vec(p["b_glu"]), row(dya), row(dgl_in)], [(s5w, s5w, F32)])
    partial = dict(w_down=g_wdown, w_up=g_wup, w_out=g_wout, w_pa=g_wpa, w_pb=g_wpb, w_glu=g_wglu)
    parts = [_grad_parts(partial[n]) for n in EARLY_GRADS]
    (dza, dbre3, dbim3, dcre3, dcim3, dlam, dd), sib = _s5_bwd(
        dy5, z, xre, xim, bre3, bim3, cre3, cim3, coef_r, p["s5_d"], nseq=nseq, seq=seq, rider=_swap_halves_rider(parts))
    pair = _pair_sums(EARLY_GRADS, parts, sib)
    (dzq, dzf, dzi, dzg, dlb, dgain), others = _hg_bwd(
        dyb, z, o, states, scores, p["lbrow"], p["gain"], nseq=nseq, seq=seq, heads=heads, qoff=qoff,
        rider=_scatter_rider(pair))
    halves = _chip_sums(EARLY_GRADS, pair, others)

    dz = jnp.concatenate([dza, dzq, dzf, dzi, dzg, dzga, dzgb], axis=1)
    du, sibs = _mm_bwd_cols("in_bwd", dz, p["w_in"], rider=_swap_sums_rider(halves))
    big = dict(zip(EARLY_GRADS, _join_halves(halves, sibs)))

    gshape = lam_re.shape
    small = {
        "loss": lossv, "g_ffn": dg_ffn, "g_final": dg_final, "b_glu": db_glu, "gain": dgain,
        "lbrow": dlb, "s5_d": dd, "w_conv": cstats[0:CONV_W], "b_conv": cstats[CONV_W:CONV_W + 1],
        "lam_re": dlam[:, 0, :].reshape(gshape), "lam_im": dlam[:, 1, :].reshape(gshape),
        "bb_re": _s5_in_blocks_diag(dbre3), "bb_im": _s5_in_blocks_diag(dbim3),
        "s5_c_re": _s5_out_blocks_diag(dcre3), "s5_c_im": _s5_out_blocks_diag(dcim3),
    }
    small_vec = _pack([small[n] for n in SMALL_PARTS], F32)
    g_win, (small_all,) = _mm_wgrad_cols("in_wgrad", u, dz, rider=_gather_all_rider(small_vec))
    small_sum = _sum_over_devices("small_grad_sum", small_vec, small_all)
    sm = dict(zip(SMALL_PARTS, _unpack(small_sum, [small[n].shape for n in SMALL_PARTS])))
    last = [_grad_parts(g_win)]
    pair = _pair_sums(("w_in",), last, _run_rider("grad_swap_halves", _swap_halves_rider(last)))
    (half,) = _chip_sums(("w_in",), pair, _run_rider("grad_scatter_chips", _scatter_rider(pair)))
    mid = half.shape[0] // 2
    sib_half = jnp.concatenate(_run_rider("grad_swap_sums", _swap_sums_rider([half[:mid], half[mid:]])), axis=0)
    big["w_in"] = _join_halves([half], [sib_half])[0]

    dx, dg_mix = rw("rms_mix_bwd", _rms_bwd, [row(x), vec(p["g_mix"]), row(du), row(dx1)], [(d, d, F32)],
                    accs=[(d, d)])

    mix_vec = dg_mix.reshape(SUBLANES, -1)
    (mix_all,) = _run_rider("gather_g_mix", _gather_all_rider(mix_vec))
    sm["g_mix"] = _sum_over_devices("g_mix_sum", mix_vec, mix_all).reshape(dg_mix.shape)
    return dx, big, sm


ANY = pl.BlockSpec(memory_space=pl.ANY)


def _place():
    x, y, c = lax.axis_index("x"), lax.axis_index("y"), lax.axis_index("c")
    chips = [(1 - x, y), (x, 1 - y), (1 - x, 1 - y)]
    return x, y, c, chips


def _remote(src, dst, send_sems, recv_sems, k, to):
    return pltpu.make_async_remote_copy(src_ref=src, dst_ref=dst, send_sem=send_sems.at[k], recv_sem=recv_sems.at[k],
                                        device_id=to, device_id_type=MESH)


def _half(rows, which):
    return pl.ds(pl.multiple_of(which * (rows // 2), 16), rows // 2)


def _gather_weights(shards, whole):
    n, nw = len(shards), len(whole)
    arrays = list(shards) + list(whole)

    def body(*refs):
        in_refs, out_refs = refs[:n + nw], refs[n + nw:2 * (n + nw)]
        send_sems, recv_sems = refs[2 * (n + nw):]
        x, y, c, chips = _place()
        me = 2 * x + y
        copy = functools.partial(_remote, send_sems=send_sems, recv_sems=recv_sems)
        sends = []
        for a in range(n):
            mine_half = _half(arrays[a].shape[0], c)
            for j, (cx, cy) in enumerate(chips):
                sends.append(copy(in_refs[a].at[mine_half], out_refs[a].at[me, mine_half], k=6 * a + j, to=(cx, cy, c)))
        for a in range(n, n + nw):
            for j, (cx, cy) in enumerate(chips):
                sends.append(copy(in_refs[a], out_refs[a].at[me], k=6 * n + 3 * (a - n) + j, to=(cx, cy, c)))
        for cp in sends:
            cp.start()
        for a in range(n):
            mine_half = _half(arrays[a].shape[0], c)
            for j, (cx, cy) in enumerate(chips):
                landed = out_refs[a].at[2 * cx + cy, mine_half]
                copy(landed, landed, k=6 * a + j, to=(x, y, c)).wait_recv()
                fwd = copy(landed, landed, k=6 * a + 3 + j, to=(x, y, 1 - c))
                fwd.start()
                sends.append(fwd)
        for a in range(n):
            other_half = _half(arrays[a].shape[0], 1 - c)
            for j, (cx, cy) in enumerate(chips):
                landed = out_refs[a].at[2 * cx + cy, other_half]
                copy(landed, landed, k=6 * a + 3 + j, to=(x, y, c)).wait_recv()
        for a in range(n, n + nw):
            for j, (cx, cy) in enumerate(chips):
                landed = out_refs[a].at[2 * cx + cy]
                copy(landed, landed, k=6 * n + 3 * (a - n) + j, to=(x, y, c)).wait_recv()
        for cp in sends:
            cp.wait_send()

    nsem = 6 * n + 3 * nw
    return pl.pallas_call(
        body, name="gather_weights", out_shape=[jax.ShapeDtypeStruct((N_CHIPS,) + a.shape, a.dtype) for a in arrays],
        in_specs=[ANY] * (n + nw), out_specs=[ANY] * (n + nw),
        scratch_shapes=[pltpu.SemaphoreType.DMA((nsem,)), pltpu.SemaphoreType.DMA((nsem,))],
    )(*arrays)


def _symmetric_rider(arrays, out_shapes, copies_of, nsem):
    def start(ins, outs, send_sems, recv_sems):
        for cp in copies_of(ins, outs, send_sems, recv_sems):
            cp.start()

    def finish(ins, outs, send_sems, recv_sems):
        for cp in copies_of(ins, outs, send_sems, recv_sems):
            cp.wait()

    return _Rider(arrays, out_shapes, nsem, start, finish)


def _swap_halves_rider(parts):
    def copies_of(ins, outs, send_sems, recv_sems):
        x, y, c, _ = _place()
        return [_remote(ins[a].at[:, _half(g.shape[1], 1 - c), :], outs[a], send_sems, recv_sems, a, (x, y, 1 - c))
                for a, g in enumerate(parts)]

    shapes = [jax.ShapeDtypeStruct((g.shape[0], g.shape[1] // 2, g.shape[2]), g.dtype) for g in parts]
    return _symmetric_rider(parts, shapes, copies_of, len(parts))


def _scatter_rider(parts):
    def copies_of(ins, outs, send_sems, recv_sems):
        x, y, c, chips = _place()
        return [_remote(ins[a].at[2 * cx + cy], outs[a].at[j], send_sems, recv_sems, 3 * a + j, (cx, cy, c))
                for a in range(len(parts)) for j, (cx, cy) in enumerate(chips)]

    shapes = [jax.ShapeDtypeStruct((N_CHIPS - 1,) + h.shape[1:], h.dtype) for h in parts]
    return _symmetric_rider(parts, shapes, copies_of, 3 * len(parts))


def _swap_sums_rider(parts):
    def copies_of(ins, outs, send_sems, recv_sems):
        x, y, c, _ = _place()
        return [_remote(ins[a], outs[a], send_sems, recv_sems, a, (x, y, 1 - c)) for a in range(len(parts))]

    shapes = [jax.ShapeDtypeStruct(g.shape, g.dtype) for g in parts]
    return _symmetric_rider(parts, shapes, copies_of, len(parts))


def _gather_ici_rider(shards):
    def sends(ins, outs, send_sems, recv_sems):
        x, y, c, chips = _place()
        return [_remote(ins[a].at[_half(s.shape[0], c)], outs[a].at[2 * x + y, _half(s.shape[0], c)], send_sems,
                        recv_sems, 3 * a + j, (cx, cy, c)) for a, s in enumerate(shards) for j, (cx, cy) in enumerate(chips)]

    def start(ins, outs, send_sems, recv_sems):
        for cp in sends(ins, outs, send_sems, recv_sems):
            cp.start()

    def finish(ins, outs, send_sems, recv_sems):
        x, y, c, chips = _place()
        for a, s in enumerate(shards):
            for j, (cx, cy) in enumerate(chips):
                landed = outs[a].at[2 * cx + cy, _half(s.shape[0], c)]
                _remote(landed, landed, send_sems, recv_sems, 3 * a + j, (x, y, c)).wait_recv()
        for cp in sends(ins, outs, send_sems, recv_sems):
            cp.wait_send()

    shapes = [jax.ShapeDtypeStruct((N_CHIPS,) + s.shape, s.dtype) for s in shards]
    return _Rider(shards, shapes, 3 * len(shards), start, finish)


def _gather_pass_rider(landed):
    def sends(ins, outs, send_sems, recv_sems):
        x, y, c, chips = _place()
        return [_remote(ins[a].at[2 * cx + cy, _half(g.shape[1], c)], outs[a].at[2 * cx + cy, _half(g.shape[1], c)],
                        send_sems, recv_sems, 3 * a + j, (x, y, 1 - c))
                for a, g in enumerate(landed) for j, (cx, cy) in enumerate(chips)]

    def start(ins, outs, send_sems, recv_sems):
        for cp in sends(ins, outs, send_sems, recv_sems):
            cp.start()

    def finish(ins, outs, send_sems, recv_sems):
        x, y, c, chips = _place()
        for a, g in enumerate(landed):
            for j, (cx, cy) in enumerate(chips):
                other = outs[a].at[2 * cx + cy, _half(g.shape[1], 1 - c)]
                _remote(other, other, send_sems, recv_sems, 3 * a + j, (x, y, c)).wait_recv()
        for cp in sends(ins, outs, send_sems, recv_sems):
            cp.wait_send()

    shapes = [jax.ShapeDtypeStruct(g.shape, g.dtype) for g in landed]
    return _Rider(landed, shapes, 3 * len(landed), start, finish, aliases={a: a for a in range(len(landed))})


def _grad_parts(g):
    return g.reshape((N_CHIPS, -1, g.shape[-1]))


def _pair_sums(names, parts, sib):
    ci = lax.axis_index("c")
    out = []
    for n, g, s in zip(names, parts, sib):
        rh, cols = s.shape[1], s.shape[2]
        own = lax.dynamic_slice_in_dim(g, ci * rh, rh, axis=1)
        both = _sum_blocks("grad_pair_sum_" + n, [own.reshape(-1, cols), s.reshape(-1, cols)], BF16)
        out.append(both.reshape(N_CHIPS, rh, cols))
    return out


def _chip_sums(names, pair, others):
    chip = 2 * lax.axis_index("x") + lax.axis_index("y")
    return [_sum_blocks("grad_chip_sum_" + n, [lax.dynamic_index_in_dim(h, chip, axis=0, keepdims=False), o[0], o[1], o[2]],
                        F32) for n, h, o in zip(names, pair, others)]


def _join_halves(halves, sibs):
    ci = lax.axis_index("c")
    return [lax.dynamic_update_slice_in_dim(jnp.concatenate([own, own], axis=0), s, (1 - ci) * own.shape[0], axis=0)
            for own, s in zip(halves, sibs)]


def _gather_all_rider(v):
    m_per = v.shape[0]

    def rows(ref, px, py, pc):
        return ref.at[pl.ds(pl.multiple_of((4 * px + 2 * py + pc) * m_per, 8), m_per)]

    def first(ins, outs, send_sems, recv_sems):
        x, y, c, chips = _place()
        mine = rows(outs[0], x, y, c)
        return [_remote(ins[0], mine, send_sems, recv_sems, 0, (x, y, 1 - c))] + [
            _remote(ins[0], mine, send_sems, recv_sems, 1 + j, (cx, cy, c)) for j, (cx, cy) in enumerate(chips)]

    def start(ins, outs, send_sems, recv_sems):
        for cp in first(ins, outs, send_sems, recv_sems):
            cp.start()

    def finish(ins, outs, send_sems, recv_sems):
        x, y, c, chips = _place()
        passed = []
        for j, (cx, cy) in enumerate(chips):
            blk = rows(outs[0], cx, cy, c)
            _remote(blk, blk, send_sems, recv_sems, 1 + j, (x, y, c)).wait_recv()
            passed.append(_remote(blk, blk, send_sems, recv_sems, 4 + j, (x, y, 1 - c)))
            passed[j].start()
        sib = rows(outs[0], x, y, 1 - c)
        _remote(sib, sib, send_sems, recv_sems, 0, (x, y, c)).wait_recv()
        for j, (cx, cy) in enumerate(chips):
            blk = rows(outs[0], cx, cy, 1 - c)
            _remote(blk, blk, send_sems, recv_sems, 4 + j, (x, y, c)).wait_recv()
        for cp in first(ins, outs, send_sems, recv_sems) + passed:
            cp.wait_send()

    return _Rider([v], [jax.ShapeDtypeStruct((N_DEV * m_per,) + v.shape[1:], v.dtype)], 7, start, finish)


def _sum_over_devices(name, v, gathered):
    m_per = v.shape[0]
    dev = 4 * lax.axis_index("x") + 2 * lax.axis_index("y") + lax.axis_index("c")
    full = lax.dynamic_update_slice_in_dim(gathered, v, dev * m_per, axis=0)
    return _sum_blocks(name, [full[i * m_per:(i + 1) * m_per] for i in range(N_DEV)], F32)


def _sum_blocks(name, parts, out_dtype):
    rows, cols = parts[0].shape
    tm = _row_tile(rows, 512)

    def body(*refs):
        acc = refs[0][...].astype(F32)
        for r in refs[1:-1]:
            acc = acc + r[...].astype(F32)
        refs[-1][...] = acc.astype(refs[-1].dtype)

    spec = pl.BlockSpec((tm, cols), lambda i: (i, 0))
    return pl.pallas_call(
        body, name=name, grid=(rows // tm,), in_specs=[spec] * len(parts), out_specs=spec,
        out_shape=jax.ShapeDtypeStruct((rows, cols), out_dtype), compiler_params=_params("arbitrary"),
    )(*parts)


def _adamw_math(wv, gv, mv, vv):
    m2 = ADAM_B1 * mv + (1.0 - ADAM_B1) * gv
    v2 = ADAM_B2 * vv + (1.0 - ADAM_B2) * (gv * gv)
    delta = -ADAM_LR * ((m2 / (1.0 - ADAM_B1 ** ADAM_STEP)) / (jnp.sqrt(v2 / (1.0 - ADAM_B2 ** ADAM_STEP)) + ADAM_EPS)
                        + ADAM_WD * wv)
    return delta, m2, v2


def _adamw_small(ws, gs, ms, vs):
    n = len(ws)

    def body(*refs):
        for i in range(n):
            res = _adamw_math(refs[i][...], refs[n + i][...], refs[2 * n + i][...], refs[3 * n + i][...])
            for k in range(3):
                refs[(4 + k) * n + i][...] = res[k]

    vm = pl.BlockSpec(memory_space=pltpu.VMEM)
    outs = pl.pallas_call(
        body, name="adamw_small", in_specs=[vm] * (4 * n), out_specs=[vm] * (3 * n),
        out_shape=[jax.ShapeDtypeStruct(a.shape, F32) for a in ws] * 3,
        compiler_params=pltpu.CompilerParams(vmem_limit_bytes=VMEM_LIMIT_BYTES),
    )(*ws, *gs, *ms, *vs)
    return outs[:n], outs[n:2 * n], outs[2 * n:]


def _adamw(name, w, g, m, v):
    rows, cols = w.shape
    ins = [(a, cols, 0, "row") for a in (w, g, m, v)]
    return _rowwise(name, _adamw_math, ins, [(cols, cols, F32)] * 3, rows=rows, tm=_row_tile(rows, 256))


PACK_ROWS = 256


def _pack(flat_parts, dtype, lead=()):
    parts = [a.astype(dtype).reshape(lead + (-1,)) for a in flat_parts]
    n = sum(a.shape[-1] for a in parts)
    chunk = PACK_ROWS * LANES
    total = -(-n // chunk) * chunk
    if total > n:
        parts.append(jnp.zeros(lead + (total - n,), dtype))
    return jnp.concatenate(parts, axis=-1).reshape(lead + (total // LANES, LANES))


def _unpack(buf, shapes, lead=()):
    flat = buf.reshape(lead + (-1,))
    out, off = [], 0
    for shp in shapes:
        n = math.prod(shp)
        out.append(lax.slice_in_dim(flat, off, off + n, axis=len(lead)).reshape(lead + tuple(shp)))
        off += n
    return out


BIG = ("w_in", "w_glu", "w_pa", "w_pb", "w_out", "w_up", "w_down")
WEIGHTS = ("g_mix", "w_in", "s5_a_re", "s5_a_im", "s5_log_dt", "s5_b_re", "s5_b_im", "s5_c_re", "s5_c_im", "s5_d",
           "w_glu", "b_glu", "hg_lb_logits", "hg_norm_gain", "w_pa", "w_pb", "w_out", "g_ffn", "w_up", "w_conv",
           "b_conv", "w_down", "g_final")
SMALL = tuple(n for n in WEIGHTS if n not in BIG)
SMALL_PARTS = ("loss", "g_ffn", "g_final", "b_glu", "gain", "lbrow", "s5_d", "w_conv", "b_conv", "lam_re", "lam_im",
               "bb_re", "bb_im", "s5_c_re", "s5_c_im")


def _lower_bound(logits):
    return jnp.cumsum(jax.nn.softmax(logits, axis=0), axis=0)[0:1]


def kernel(x, g_mix, w_in, s5_a_re, s5_a_im, s5_log_dt, s5_b_re, s5_b_im, s5_c_re, s5_c_im, s5_d, w_glu, b_glu, hg_lb_logits, hg_norm_gain, w_pa, w_pb, w_out, g_ffn, w_up, w_conv, b_conv, w_down, g_final, loss_target, m_g_mix, m_w_in, m_s5_a_re, m_s5_a_im, m_s5_log_dt, m_s5_b_re, m_s5_b_im, m_s5_c_re, m_s5_c_im, m_s5_d, m_w_glu, m_b_glu, m_hg_lb_logits, m_hg_norm_gain, m_w_pa, m_w_pb, m_w_out, m_g_ffn, m_w_up, m_w_conv, m_b_conv, m_w_down, m_g_final, v_g_mix, v_w_in, v_s5_a_re, v_s5_a_im, v_s5_log_dt, v_s5_b_re, v_s5_b_im, v_s5_c_re, v_s5_c_im, v_s5_d, v_w_glu, v_b_glu, v_hg_lb_logits, v_hg_norm_gain, v_w_pa, v_w_pb, v_w_out, v_g_ffn, v_w_up, v_w_conv, v_b_conv, v_w_down, v_g_final):
    args = dict(locals())
    w = {n: args[n] for n in WEIGHTS}
    mom = {n: args["m_" + n] for n in WEIGHTS}
    var = {n: args["v_" + n] for n in WEIGHTS}
    nseq, seq, d = x.shape
    xi, yi = lax.axis_index("x"), lax.axis_index("y")
    chip = 2 * xi + yi

    shard = {n: w[n][0] for n in BIG}
    shard16 = {n: shard[n].astype(BF16) for n in BIG}
    first = ("w_in", "w_glu")
    got = _gather_weights([shard16[n] for n in first], [w_conv[0]])
    p = {n: lax.dynamic_update_index_in_dim(g, shard16[n], chip, 0) for n, g in zip(first, got)}
    p["w_glu"] = p["w_glu"].reshape(-1, p["w_glu"].shape[-1])
    conv_all = lax.dynamic_update_index_in_dim(got[-1], w_conv[0], chip, 0)
    p.update(g_mix=g_mix, g_ffn=g_ffn, g_final=g_final.reshape(1, -1), b_glu=b_glu, gain=hg_norm_gain, s5_d=s5_d,
             b_conv=b_conv, w_conv=conv_all.transpose(1, 0, 2).reshape(CONV_W, -1), lbrow=_lower_bound(hg_lb_logits),
             s5_a_re=s5_a_re[0], s5_a_im=s5_a_im[0], s5_log_dt=s5_log_dt[0], s5_b_re=s5_b_re[0], s5_b_im=s5_b_im[0],
             s5_c_re=s5_c_re[0], s5_c_im=s5_c_im[0])

    dx, grads, sm = _local_step(x.reshape(nseq * seq, d), loss_target.reshape(nseq * seq, d), p,
                                {n: shard16[n] for n in LATE}, nseq=nseq, seq=seq)
    loss = sm["loss"][0, 0]

    _, disc_vjp = jax.vjp(_s5_discretize, p["s5_a_re"], p["s5_a_im"], p["s5_log_dt"], p["s5_b_re"], p["s5_b_im"])
    da_re, da_im, dlog_dt, db_re, db_im = disc_vjp((sm["lam_re"], sm["lam_im"], sm["bb_re"], sm["bb_im"]))
    _, lb_vjp = jax.vjp(_lower_bound, hg_lb_logits)
    (dlogits,) = lb_vjp(sm["lbrow"])
    fcols = w_conv.shape[-1]
    grads.update(
        g_mix=sm["g_mix"], g_ffn=sm["g_ffn"], g_final=sm["g_final"].reshape(-1), b_glu=sm["b_glu"],
        hg_norm_gain=sm["gain"], hg_lb_logits=dlogits, s5_d=sm["s5_d"], b_conv=sm["b_conv"],
        w_conv=lax.dynamic_slice_in_dim(sm["w_conv"], chip * fcols, fcols, axis=1),
        s5_a_re=da_re, s5_a_im=da_im, s5_log_dt=dlog_dt, s5_b_re=db_re, s5_b_im=db_im,
        s5_c_re=sm["s5_c_re"], s5_c_im=sm["s5_c_im"])
    grads = {n: grads[n].reshape(w[n].shape) for n in WEIGHTS}

    delta, new_m, new_v = {}, {}, {}
    for n in BIG:
        shp = shard[n].shape
        dl, m2, v2 = _adamw("adamw_" + n, shard[n], grads[n].reshape(shp), mom[n].reshape(shp), var[n].reshape(shp))
        delta[n], new_m[n], new_v[n] = dl, m2, v2
    def natural(a):
        return a.reshape(1, -1) if a.ndim == 1 else (a[0] if a.ndim > 2 else a)

    outs = _adamw_small(*[[natural(src[n]) for n in SMALL] for src in (w, grads, mom, var)])
    for dst, group in zip((delta, new_m, new_v), outs):
        dst.update(zip(SMALL, group))
    res = [loss, dx.reshape(x.shape)]
    for group in (grads, delta, new_m, new_v):
        res += [group[n].reshape(w[n].shape) for n in WEIGHTS]
    return tuple(res)
```

```python
import functools
import math

import jax
import jax.numpy as jnp
from jax import lax
from jax.experimental import pallas as pl
from jax.experimental.pallas import tpu as pltpu

F32 = jnp.float32
BF16 = jnp.bfloat16
MESH = pl.DeviceIdType.MESH

EPS = 1e-6
S5_GROUP = 16
S5_STATE = 64
S5_BLOCK_GROUPS = 8
HEAD = 128
CHUNK = 64
CONV_W = 3
LANES = 128
SUBLANES = 8
GATE_BLOCK = 512
VMEM_LIMIT_BYTES = 56 * 1024 * 1024

ADAM_LR = 0.001
ADAM_B1 = 0.9
ADAM_B2 = 0.999
ADAM_EPS = 1e-08
ADAM_WD = 0.01
ADAM_STEP = 10

N_CHIPS = 4
N_DEV = 8


def _params(*sem):
    return pltpu.CompilerParams(dimension_semantics=sem, vmem_limit_bytes=VMEM_LIMIT_BYTES)


class _Rider:
    def __init__(self, arrays, out_shapes, nsem, start, finish, aliases=None):
        self.arrays, self.out_shapes, self.nsem = list(arrays), list(out_shapes), nsem
        self.start, self.finish, self.aliases = start, finish, dict(aliases or {})


def _hosted_call(name, body, *, grid, in_specs, out_specs, out_shape, operands, scratch_shapes=(), rider=None):
    in_specs, out_specs, out_shape, scratch_shapes = list(in_specs), list(out_specs), list(out_shape), list(scratch_shapes)
    cparams = _params(*(["arbitrary"] * len(grid)))
    if rider is None:
        return pl.pallas_call(body, name=name, grid=grid, in_specs=in_specs, out_specs=out_specs, out_shape=out_shape,
                              scratch_shapes=scratch_shapes, compiler_params=cparams)(*operands)
    n_in, n_out, n_sc = len(in_specs), len(out_specs), len(scratch_shapes)
    r_in, r_out = len(rider.arrays), len(rider.out_shapes)

    def hosted(*refs):
        ins, rins = refs[:n_in], refs[n_in:n_in + r_in]
        outs = refs[n_in + r_in:n_in + r_in + n_out]
        routs = refs[n_in + r_in + n_out:n_in + r_in + n_out + r_out]
        rest = refs[n_in + r_in + n_out + r_out:]
        send_sems, recv_sems = rest[n_sc], rest[n_sc + 1]
        first = functools.reduce(jnp.logical_and, [pl.program_id(i) == 0 for i in range(len(grid))])
        last = functools.reduce(jnp.logical_and, [pl.program_id(i) == grid[i] - 1 for i in range(len(grid))])

        @pl.when(first)
        def _():
            rider.start(rins, routs, send_sems, recv_sems)

        body(*ins, *outs, *rest[:n_sc])

        @pl.when(last)
        def _():
            rider.finish(rins, routs, send_sems, recv_sems)

    res = pl.pallas_call(
        hosted, name=name, grid=grid, in_specs=in_specs + [ANY] * r_in, out_specs=out_specs + [ANY] * r_out,
        out_shape=out_shape + rider.out_shapes,
        scratch_shapes=scratch_shapes + [pltpu.SemaphoreType.DMA((rider.nsem,)), pltpu.SemaphoreType.DMA((rider.nsem,))],
        input_output_aliases={n_in + i: n_out + o for i, o in rider.aliases.items()}, compiler_params=cparams,
    )(*operands, *rider.arrays)
    return res[:n_out], res[n_out:]


def _run_rider(name, rider):
    r_in, r_out = len(rider.arrays), len(rider.out_shapes)

    def body(*refs):
        rins, routs, send_sems, recv_sems = refs[:r_in], refs[r_in:r_in + r_out], refs[-2], refs[-1]
        rider.start(rins, routs, send_sems, recv_sems)
        rider.finish(rins, routs, send_sems, recv_sems)

    return pl.pallas_call(
        body, name=name, in_specs=[ANY] * r_in, out_specs=[ANY] * r_out, out_shape=rider.out_shapes,
        scratch_shapes=[pltpu.SemaphoreType.DMA((rider.nsem,)), pltpu.SemaphoreType.DMA((rider.nsem,))],
        input_output_aliases=rider.aliases,
    )(*rider.arrays)


def _row_tile(rows, cap):
    if rows <= cap:
        return rows
    for t in range(cap - cap % 8, 7, -8):
        if rows % t == 0:
            return t
    raise ValueError(f"no row tile for {rows}")


def _dot(a, b):
    return jnp.dot(a.astype(BF16), b.astype(BF16), preferred_element_type=F32)


def _dot_nt(a, b):
    return lax.dot_general(a.astype(BF16), b.astype(BF16), (((1,), (1,)), ((), ())), preferred_element_type=F32)


def _dot_tn(a, b):
    return lax.dot_general(a.astype(BF16), b.astype(BF16), (((0,), (0,)), ((), ())), preferred_element_type=F32)


def _sigmoid(x):
    return 0.5 * jnp.tanh(0.5 * x) + 0.5


_GELU_C = math.sqrt(2.0 / math.pi)


def _gelu(x):
    return 0.5 * x * (1.0 + jnp.tanh(_GELU_C * (x + 0.044715 * x * x * x)))


def _gelu_grad(x):
    th = jnp.tanh(_GELU_C * (x + 0.044715 * x * x * x))
    return 0.5 * (1.0 + th) + 0.5 * x * (1.0 - th * th) * _GELU_C * (1.0 + 3.0 * 0.044715 * x * x)


def _rowwise(name, fn, ins, outs, accs=(), *, rows, tm, ncol=1):
    n_in, n_out = len(ins), len(outs)

    def body(*refs):
        res = fn(*[r[...] for r in refs[:n_in]])
        for r, v in zip(refs[n_in:n_in + n_out], res[:n_out]):
            r[...] = v.astype(r.dtype)
        first = pl.program_id(1) == 0
        for r, v in zip(refs[n_in + n_out:], res[n_out:]):
            @pl.when(first)
            def _():
                r[...] = v

            @pl.when(jnp.logical_not(first))
            def _():
                r[...] += v

    in_specs = []
    for _, width, base, kind in ins:
        if kind == "row":
            in_specs.append(pl.BlockSpec((tm, width), lambda j, i, b=base: (i, b + j)))
        else:
            in_specs.append(pl.BlockSpec((1, width), lambda j, i, b=base: (0, b + j)))
    out_specs = [pl.BlockSpec((tm, width), lambda j, i: (i, j)) for _, width, _ in outs]
    out_specs += [pl.BlockSpec((1, width), lambda j, i: (0, j)) for _, width in accs]
    out_shape = [jax.ShapeDtypeStruct((rows, total), dt) for total, _, dt in outs]
    out_shape += [jax.ShapeDtypeStruct((1, total), F32) for total, _ in accs]
    return pl.pallas_call(
        body, name=name, grid=(ncol, rows // tm), in_specs=in_specs, out_specs=out_specs, out_shape=out_shape,
        compiler_params=_params("arbitrary", "arbitrary"),
    )(*[a for a, _, _, _ in ins])


def _mm(name, a, b, *, mode, grid, a_spec, b_spec, o_spec, out_shape, acc_shape, res=None, res_spec=None,
        pair_axis=None, rider=None):
    nk = grid[2]
    dot = {"nn": _dot, "nt": _dot_nt, "tn": _dot_tn}[mode]
    a_list = list(a) if isinstance(a, tuple) else [a]
    b_list = list(b) if isinstance(b, tuple) else [b]
    na, nb = len(a_list), len(b_list)
    assert (pair_axis is None) == (na + nb == 2)
    direct = nk == 1 and pair_axis is None

    def body(*refs):
        a_refs, b_refs = refs[:na], refs[na:na + nb]
        r_ref = None if res is None else refs[na + nb]
        o_ref = refs[na + nb + (0 if res is None else 1)]

        def finish(v):
            if res is not None:
                v = v + r_ref[...]
            o_ref[...] = v.astype(o_ref.dtype)

        if direct:
            finish(dot(a_refs[0][...], b_refs[0][...]))
            return
        acc_ref = refs[-1]
        k = pl.program_id(2)

        @pl.when(k == 0)
        def _():
            acc_ref[...] = jnp.zeros_like(acc_ref)

        if pair_axis is None:
            acc_ref[...] += dot(a_refs[0][...], b_refs[0][...])
        else:
            lower = pl.program_id(pair_axis) < grid[pair_axis] // 2

            @pl.when(lower)
            def _():
                acc_ref[...] += dot(a_refs[0][...], b_refs[0][...])

            @pl.when(jnp.logical_not(lower))
            def _():
                acc_ref[...] += dot(a_refs[-1][...], b_refs[-1][...])

        @pl.when(k == nk - 1)
        def _():
            finish(acc_ref[...])

    operands = a_list + b_list + ([] if res is None else [res])
    in_specs = (list(a_spec) if na == 2 else [a_spec]) + (list(b_spec) if nb == 2 else [b_spec])
    in_specs += [] if res is None else [res_spec]
    got = _hosted_call(name, body, grid=grid, in_specs=in_specs, out_specs=[o_spec], out_shape=[out_shape],
                       scratch_shapes=[] if direct else [pltpu.VMEM(acc_shape, F32)], operands=operands, rider=rider)
    return got[0] if rider is None else (got[0][0], got[1])


MM_TILE_BUDGET_BYTES = 36 * 1024 * 1024
MM_TILE_CAP = 1024


def _mm_tile(t, row_bytes, fixed_bytes):
    cap = max(16, min(MM_TILE_CAP, (MM_TILE_BUDGET_BYTES - fixed_bytes) // row_bytes))
    return _row_tile(t, cap - cap % 16)


def _size(a):
    return jnp.dtype(a.dtype).itemsize


def _mm_fwd_cols(name, a, w3, out_dtype=F32, rider=None):
    t, k = a.shape
    ns = w3.shape[2]
    tm = _mm_tile(t, 2 * k * _size(a) + 2 * ns * jnp.dtype(out_dtype).itemsize, 2 * k * ns * _size(w3))
    return _mm(name, a, w3, mode="nn", grid=(N_CHIPS, t // tm, 1),
               a_spec=pl.BlockSpec((tm, k), lambda j, i, kk: (i, 0)),
               b_spec=pl.BlockSpec((None, k, ns), lambda j, i, kk: (j, 0, 0)),
               o_spec=pl.BlockSpec((tm, ns), lambda j, i, kk: (i, j)),
               out_shape=jax.ShapeDtypeStruct((t, N_CHIPS * ns), out_dtype), acc_shape=(tm, ns), rider=rider)


def _mm_bwd_cols(name, d, w3, out_dtype=F32, rider=None):
    pair = isinstance(d, tuple)
    t = d[0].shape[0] if pair else d.shape[0]
    k, ns = w3.shape[1], w3.shape[2]
    dsize = _size(d[0] if pair else d)
    tm = _mm_tile(t, (4 if pair else 2) * ns * dsize + 2 * k * jnp.dtype(out_dtype).itemsize + 4 * k,
                  2 * k * ns * _size(w3))
    half = N_CHIPS // 2
    if pair:
        a_spec = (pl.BlockSpec((tm, ns), lambda i, j, kk: (i, jnp.minimum(kk, half - 1))),
                  pl.BlockSpec((tm, ns), lambda i, j, kk: (i, jnp.maximum(kk - half, 0))))
    else:
        a_spec = pl.BlockSpec((tm, ns), lambda i, j, kk: (i, kk))
    return _mm(name, d, w3, mode="nt", grid=(t // tm, 1, N_CHIPS), a_spec=a_spec,
               b_spec=pl.BlockSpec((None, k, ns), lambda i, j, kk: (kk, 0, 0)),
               o_spec=pl.BlockSpec((tm, k), lambda i, j, kk: (i, 0)),
               out_shape=jax.ShapeDtypeStruct((t, k), out_dtype), acc_shape=(tm, k), pair_axis=2 if pair else None,
               rider=rider)


def _mm_wgrad_cols(name, a, d, rider=None):
    pair = isinstance(d, tuple)
    t, k = a.shape
    ns = (2 * d[0].shape[1] if pair else d.shape[1]) // N_CHIPS
    dsize = _size(d[0] if pair else d)
    tk = _mm_tile(t, 2 * k * _size(a) + (4 if pair else 2) * ns * dsize, k * ns * (4 + 2 * 2))
    half = N_CHIPS // 2
    if pair:
        b_spec = (pl.BlockSpec((tk, ns), lambda j, i, kk: (jnp.where(j < half, kk, 0), jnp.minimum(j, half - 1))),
                  pl.BlockSpec((tk, ns), lambda j, i, kk: (jnp.where(j < half, 0, kk), jnp.maximum(j - half, 0))))
    else:
        b_spec = pl.BlockSpec((tk, ns), lambda j, i, kk: (kk, j))
    return _mm(name, a, d, mode="tn", grid=(N_CHIPS, 1, t // tk),
               a_spec=pl.BlockSpec((tk, k), lambda j, i, kk: (kk, 0)), b_spec=b_spec,
               o_spec=pl.BlockSpec((None, k, ns), lambda j, i, kk: (j, 0, 0)),
               out_shape=jax.ShapeDtypeStruct((N_CHIPS, k, ns), BF16), acc_shape=(k, ns),
               pair_axis=0 if pair else None, rider=rider)


MM_BLOCK_CAP = 1408


def _mm_fwd_rows(name, a, w, res=None, out_dtype=F32):
    t, k = a.shape
    n = w.shape[1]
    tk = k if k <= MM_BLOCK_CAP else MM_BLOCK_CAP
    assert k % tk == 0
    row_bytes = 2 * tk * _size(a) + 2 * n * jnp.dtype(out_dtype).itemsize + (0 if res is None else 2 * n * 4) + 4 * n
    tm = _mm_tile(t, row_bytes, 2 * tk * n * _size(w))
    return _mm(name, a, w, mode="nn", grid=(t // tm, 1, k // tk),
               a_spec=pl.BlockSpec((tm, tk), lambda i, j, kk: (i, kk)),
               b_spec=pl.BlockSpec((tk, n), lambda i, j, kk: (kk, 0)),
               o_spec=pl.BlockSpec((tm, n), lambda i, j, kk: (i, 0)),
               out_shape=jax.ShapeDtypeStruct((t, n), out_dtype), acc_shape=(tm, n),
               res=res, res_spec=None if res is None else pl.BlockSpec((tm, n), lambda i, j, kk: (i, 0)))


def _mm_bwd_rows(name, d, w, out_dtype=F32):
    t, n = d.shape
    k = w.shape[0]
    tn = k if k <= MM_BLOCK_CAP else MM_BLOCK_CAP
    assert k % tn == 0
    tm = _mm_tile(t, 2 * n * _size(d) + 2 * tn * jnp.dtype(out_dtype).itemsize, 2 * tn * n * _size(w))
    return _mm(name, d, w, mode="nt", grid=(t // tm, k // tn, 1),
               a_spec=pl.BlockSpec((tm, n), lambda i, j, kk: (i, 0)),
               b_spec=pl.BlockSpec((tn, n), lambda i, j, kk: (j, 0)),
               o_spec=pl.BlockSpec((tm, tn), lambda i, j, kk: (i, j)),
               out_shape=jax.ShapeDtypeStruct((t, k), out_dtype), acc_shape=(tm, tn))


def _mm_wgrad_rows(name, a, d):
    t, k = a.shape
    n = d.shape[1]
    nblk = next(b for b in (1, 2, 4) if (k // b) % LANES == 0 and k // b <= MM_BLOCK_CAP)
    ks = k // nblk
    tk = _mm_tile(t, 2 * ks * _size(a) + 2 * n * _size(d), ks * n * (4 + 2 * 2))
    return _mm(name, a, d, mode="tn", grid=(nblk, 1, t // tk),
               a_spec=pl.BlockSpec((tk, ks), lambda j, i, kk: (kk, j)),
               b_spec=pl.BlockSpec((tk, n), lambda j, i, kk: (kk, 0)),
               o_spec=pl.BlockSpec((ks, n), lambda j, i, kk: (j, 0)),
               out_shape=jax.ShapeDtypeStruct((k, n), BF16), acc_shape=(ks, n))


def _s5_discretize(a_re, a_im, log_dt, b_re, b_im):
    dt = jnp.exp(log_dt)[:, None]
    mag = jnp.exp(a_re * dt)
    ang = a_im * dt
    lb_re = mag * jnp.cos(ang)
    lb_im = mag * jnp.sin(ang)
    den = a_re * a_re + a_im * a_im
    n_re = lb_re - 1.0
    n_im = lb_im
    co_re = ((n_re * a_re + n_im * a_im) / den)[..., None]
    co_im = ((n_im * a_re - n_re * a_im) / den)[..., None]
    bb_re = co_re * b_re - co_im * b_im
    bb_im = co_re * b_im + co_im * b_re
    return lb_re, lb_im, bb_re, bb_im


def _s5_in_blocks(bb):
    g = bb.shape[0]
    nb = g // S5_BLOCK_GROUPS
    t = bb.reshape(nb, S5_BLOCK_GROUPS, S5_STATE, S5_GROUP).transpose(0, 1, 3, 2)
    eye = jnp.eye(S5_BLOCK_GROUPS, dtype=bb.dtype)
    full = t[:, :, :, None, :] * eye[None, :, None, :, None]
    return full.reshape(nb, S5_BLOCK_GROUPS * S5_GROUP, S5_BLOCK_GROUPS * S5_STATE)


def _s5_in_blocks_diag(blocks):
    nb = blocks.shape[0]
    t = blocks.reshape(nb, S5_BLOCK_GROUPS, S5_GROUP, S5_BLOCK_GROUPS, S5_STATE)
    d = jnp.einsum("bghgp->bghp", t)
    return d.transpose(0, 1, 3, 2).reshape(nb * S5_BLOCK_GROUPS, S5_STATE, S5_GROUP)


def _s5_out_blocks(c):
    g = c.shape[0]
    nb = g // S5_BLOCK_GROUPS
    t = c.reshape(nb, S5_BLOCK_GROUPS, S5_GROUP, S5_STATE).transpose(0, 1, 3, 2)
    eye = jnp.eye(S5_BLOCK_GROUPS, dtype=c.dtype)
    full = t[:, :, :, None, :] * eye[None, :, None, :, None]
    return full.reshape(nb, S5_BLOCK_GROUPS * S5_STATE, S5_BLOCK_GROUPS * S5_GROUP)


def _s5_out_blocks_diag(blocks):
    nb = blocks.shape[0]
    t = blocks.reshape(nb, S5_BLOCK_GROUPS, S5_STATE, S5_BLOCK_GROUPS, S5_GROUP)
    d = jnp.einsum("bgpgh->bgph", t)
    return d.transpose(0, 1, 3, 2).reshape(nb * S5_BLOCK_GROUPS, S5_GROUP, S5_STATE)


def _s5_scan_tables(lr, li, reverse):
    def cmul(a, b):
        return a[0] * b[0] - a[1] * b[1], a[0] * b[1] + a[1] * b[0]

    lam = (lr, -li) if reverse else (lr, li)
    pw = [lam]
    for _ in range(SUBLANES - 1):
        pw.append(cmul(pw[-1], lam))
    sub = jnp.arange(SUBLANES)[:, None]
    rows = []
    for s in (1, 2, 4):
        keep = (sub <= SUBLANES - 1 - s) if reverse else (sub >= s)
        rows.append(jnp.where(keep, pw[s - 1][0][None, :], 0.0))
        rows.append(jnp.where(keep, pw[s - 1][1][None, :], 0.0))
    order = list(range(SUBLANES - 1, -1, -1)) if reverse else list(range(SUBLANES))
    rows.append(jnp.stack([pw[i][0] for i in order]))
    rows.append(jnp.stack([pw[i][1] for i in order]))
    return jnp.concatenate(rows, axis=0)


def _s5_scan(vre_ref, vim_ref, coef_ref, seq, width, reverse, xre_ref=None, xim_ref=None):
    nt = seq // SUBLANES
    nl = width // LANES
    per = 2 if xre_ref is None else 4
    sub = lax.broadcasted_iota(jnp.int32, (SUBLANES, LANES), 0)

    def step(k, carry):
        kk = (nt - 1 - k) if reverse else k
        rows = pl.ds(pl.multiple_of(kk * SUBLANES, SUBLANES), SUBLANES)
        out = []
        for j in range(nl):
            lanes = slice(j * LANES, (j + 1) * LANES)
            co = [coef_ref[SUBLANES * q:SUBLANES * (q + 1), lanes] for q in range(8)]
            cr, ci = carry[per * j], carry[per * j + 1]
            vr = vre_ref[rows, lanes]
            vi = vim_ref[rows, lanes]
            for q, s in enumerate((1, 2, 4)):
                sh = SUBLANES - s if reverse else s
                rr = pltpu.roll(vr, sh, 0)
                ri = pltpu.roll(vi, sh, 0)
                ar, ai = co[2 * q], co[2 * q + 1]
                vr, vi = vr + ar * rr - ai * ri, vi + ar * ri + ai * rr
            edge = 0 if reverse else SUBLANES - 1
            cbr = jnp.broadcast_to(cr[edge:edge + 1, :], (SUBLANES, LANES))
            cbi = jnp.broadcast_to(ci[edge:edge + 1, :], (SUBLANES, LANES))
            pr, pi = co[6], co[7]
            vr, vi = vr + pr * cbr - pi * cbi, vi + pr * cbi + pi * cbr
            vre_ref[rows, lanes] = vr
            vim_ref[rows, lanes] = vi
            out += [vr, vi]
            if xre_ref is not None:
                nr = jnp.where(sub == SUBLANES - 1, cbr, pltpu.roll(vr, SUBLANES - 1, 0))
                ni = jnp.where(sub == SUBLANES - 1, cbi, pltpu.roll(vi, SUBLANES - 1, 0))
                xr = xre_ref[rows, lanes]
                xi = xim_ref[rows, lanes]
                out += [carry[per * j + 2] + nr * xr + ni * xi, carry[per * j + 3] + ni * xr - nr * xi]
        return tuple(out)

    zero = jnp.zeros((SUBLANES, LANES), F32)
    res = lax.fori_loop(0, nt, step, (zero,) * (per * nl))
    if xre_ref is None:
        return None
    return jnp.concatenate(
        [jnp.concatenate([jnp.sum(res[per * j + 2], axis=0, keepdims=True) for j in range(nl)], axis=1),
         jnp.concatenate([jnp.sum(res[per * j + 3], axis=0, keepdims=True) for j in range(nl)], axis=1)], axis=0)


def _s5_fwd(z, bre3, bim3, cre3, cim3, coef, dskip, *, nseq, seq, rider=None):
    nb = bre3.shape[0]
    ch, ns = bre3.shape[1], bre3.shape[2]

    def body(za_ref, bre_ref, bim_ref, cre_ref, cim_ref, coef_ref, d_ref, y_ref, xre_ref, xim_ref):
        za = za_ref[...]
        xre_ref[...] = _dot(za, bre_ref[...])
        xim_ref[...] = _dot(za, bim_ref[...])
        _s5_scan(xre_ref, xim_ref, coef_ref, seq, ns, False)
        y_ref[...] = _dot(xre_ref[...], cre_ref[...]) - _dot(xim_ref[...], cim_ref[...]) + d_ref[...] * za

    blk3 = lambda r, c: pl.BlockSpec((None, r, c), lambda b, j: (j, 0, 0))
    return _hosted_call(
        "s5_fwd", body, grid=(nseq, nb),
        in_specs=[pl.BlockSpec((seq, ch), lambda b, j: (b, j)), blk3(ch, ns), blk3(ch, ns), blk3(ns, ch), blk3(ns, ch),
                  pl.BlockSpec((8 * SUBLANES, ns), lambda b, j: (0, j)), pl.BlockSpec((1, ch), lambda b, j: (0, j))],
        out_specs=[pl.BlockSpec((seq, ch), lambda b, j: (b, j)), pl.BlockSpec((seq, ns), lambda b, j: (b, j)),
                   pl.BlockSpec((seq, ns), lambda b, j: (b, j))],
        out_shape=[jax.ShapeDtypeStruct((nseq * seq, nb * ch), F32), jax.ShapeDtypeStruct((nseq * seq, nb * ns), F32),
                   jax.ShapeDtypeStruct((nseq * seq, nb * ns), F32)],
        operands=(z, bre3, bim3, cre3, cim3, coef, dskip), rider=rider)


def _s5_bwd(dy, z, xre, xim, bre3, bim3, cre3, cim3, coef_rev, dskip, *, nseq, seq, rider=None):
    nb = bre3.shape[0]
    ch, ns = bre3.shape[1], bre3.shape[2]

    def body(dy_ref, za_ref, xre_ref, xim_ref, bre_ref, bim_ref, cre_ref, cim_ref, coef_ref, d_ref,
             dza_ref, dbre_ref, dbim_ref, dcre_ref, dcim_ref, dlam_ref, dd_ref, are_ref, aim_ref):
        dy = dy_ref[...]
        za = za_ref[...]
        are_ref[...] = _dot_nt(dy, cre_ref[...])
        aim_ref[...] = -_dot_nt(dy, cim_ref[...])
        dlam = _s5_scan(are_ref, aim_ref, coef_ref, seq, ns, True, xre_ref, xim_ref)
        are = are_ref[...]
        aim = aim_ref[...]
        dza_ref[...] = (_dot_nt(are, bre_ref[...]) + _dot_nt(aim, bim_ref[...]) + d_ref[...] * dy).astype(dza_ref.dtype)
        parts = (_dot_tn(za, are), _dot_tn(za, aim), _dot_tn(xre_ref[...], dy), -_dot_tn(xim_ref[...], dy),
                 dlam, jnp.sum(dy * za, axis=0, keepdims=True))
        first = pl.program_id(1) == 0
        for r, v in zip((dbre_ref, dbim_ref, dcre_ref, dcim_ref, dlam_ref, dd_ref), parts):
            @pl.when(first)
            def _():
                r[...] = v

            @pl.when(jnp.logical_not(first))
            def _():
                r[...] += v

    blk3 = lambda r, c: pl.BlockSpec((None, r, c), lambda j, b: (j, 0, 0))
    tok = lambda c: pl.BlockSpec((seq, c), lambda j, b: (b, j))
    return _hosted_call(
        "s5_bwd", body, grid=(nb, nseq),
        in_specs=[tok(ch), tok(ch), tok(ns), tok(ns), blk3(ch, ns), blk3(ch, ns), blk3(ns, ch), blk3(ns, ch),
                  pl.BlockSpec((8 * SUBLANES, ns), lambda j, b: (0, j)), pl.BlockSpec((1, ch), lambda j, b: (0, j))],
        out_specs=[tok(ch), blk3(ch, ns), blk3(ch, ns), blk3(ns, ch), blk3(ns, ch),
                   pl.BlockSpec((None, 2, ns), lambda j, b: (j, 0, 0)), pl.BlockSpec((1, ch), lambda j, b: (0, j))],
        out_shape=[jax.ShapeDtypeStruct((nseq * seq, nb * ch), BF16),
                   jax.ShapeDtypeStruct((nb, ch, ns), F32), jax.ShapeDtypeStruct((nb, ch, ns), F32),
                   jax.ShapeDtypeStruct((nb, ns, ch), F32), jax.ShapeDtypeStruct((nb, ns, ch), F32),
                   jax.ShapeDtypeStruct((nb, 2, ns), F32), jax.ShapeDtypeStruct((1, nb * ch), F32)],
        scratch_shapes=[pltpu.VMEM((seq, ns), F32), pltpu.VMEM((seq, ns), F32)],
        operands=(dy, z, xre, xim, bre3, bim3, cre3, cim3, coef_rev, dskip), rider=rider)


def _cumsum_rows(x, reverse=False):
    n = x.shape[0]
    row = lax.broadcasted_iota(jnp.int32, x.shape, 0)
    s = 1
    while s < n:
        if reverse:
            x = x + jnp.where(row < n - s, pltpu.roll(x, n - s, 0), 0.0)
        else:
            x = x + jnp.where(row >= s, pltpu.roll(x, s, 0), 0.0)
        s *= 2
    return x


def _hg_gates(zq, zf, lb):
    sg = _sigmoid(zf)
    f = lb + (1.0 - lb) * sg
    sq = _sigmoid(zq)
    qa = zq * sq * (HEAD ** -0.5)
    b = _cumsum_rows(jnp.log(f))
    return sg, f, sq, qa, 1.0 - f, b


SUB = 16


def _hg_scores(qa, kk, b):
    c = qa.shape[0]
    row = lax.broadcasted_iota(jnp.int32, qa.shape, 0)
    pos = jnp.bitwise_and(row, SUB - 1)
    dmat = lax.broadcasted_iota(jnp.int32, (c, c), 0) - lax.broadcasted_iota(jnp.int32, (c, c), 1)
    p = jnp.zeros((c, c), F32)
    for d in range(SUB):
        if d == 0:
            fd = qa * kk
        else:
            e = jnp.exp(jnp.minimum(b - pltpu.roll(b, d, 0), 0.0))
            fd = jnp.where(pos >= d, qa * pltpu.roll(kk, d, 0) * e, 0.0)
        p = jnp.where(dmat == d, jnp.sum(fd, axis=1, keepdims=True), p)
    col = lax.broadcasted_iota(jnp.int32, (SUB, c), 1)
    blocks = [jnp.zeros((SUB, c), F32)]
    for r0 in range(SUB, c, SUB):
        beta = b[r0 - 1:r0, :]
        qt = qa[r0:r0 + SUB] * jnp.exp(b[r0:r0 + SUB] - beta)
        kt = kk * jnp.exp(jnp.minimum(beta - b, 0.0))
        blocks.append(jnp.where(col < r0, _dot_nt(qt, kt), 0.0))
    return p + jnp.concatenate(blocks, axis=0)


def _hg_scores_bwd(dp, qa, kk, b):
    c = qa.shape[0]
    row = lax.broadcasted_iota(jnp.int32, qa.shape, 0)
    pos = jnp.bitwise_and(row, SUB - 1)
    dmat = lax.broadcasted_iota(jnp.int32, (c, c), 0) - lax.broadcasted_iota(jnp.int32, (c, c), 1)
    dqa = jnp.zeros_like(qa)
    dkk = jnp.zeros_like(qa)
    db = jnp.zeros_like(qa)
    for d in range(SUB):
        dcol = jnp.sum(jnp.where(dmat == d, dp, 0.0), axis=1, keepdims=True)
        if d == 0:
            dqa = dqa + dcol * kk
            dkk = dkk + dcol * qa
        else:
            e = jnp.exp(jnp.minimum(b - pltpu.roll(b, d, 0), 0.0))
            w = jnp.where(pos >= d, dcol * e, 0.0)
            kr = pltpu.roll(kk, d, 0)
            dqa = dqa + w * kr
            tmp = w * qa
            dkk = dkk + pltpu.roll(tmp, c - d, 0)
            x = tmp * kr
            db = db + x - pltpu.roll(x, c - d, 0)
    col = lax.broadcasted_iota(jnp.int32, (SUB, c), 1)
    dq_blocks = [jnp.zeros((SUB, qa.shape[1]), F32)]
    db_blocks = [jnp.zeros((SUB, qa.shape[1]), F32)]
    for r0 in range(SUB, c, SUB):
        beta = b[r0 - 1:r0, :]
        eq = jnp.exp(b[r0:r0 + SUB] - beta)
        ek = jnp.exp(jnp.minimum(beta - b, 0.0))
        qt = qa[r0:r0 + SUB] * eq
        kt = kk * ek
        dpi = jnp.where(col < r0, dp[r0:r0 + SUB, :], 0.0)
        dqt = _dot(dpi, kt)
        dkt = _dot_tn(dpi, qt)
        dq_blocks.append(dqt * eq)
        db_blocks.append(dqt * qt)
        dkk = dkk + dkt * ek
        db = db - dkt * kt
    return dqa + jnp.concatenate(dq_blocks, axis=0), dkk, db + jnp.concatenate(db_blocks, axis=0)


def _hg_chunks_per_step(seq):
    nc = seq // CHUNK
    cps = next(k for k in (4, 2, 1) if nc % k == 0)
    return nc, cps, nc // cps


def _hg_fwd(z, lbrow, gain, *, nseq, seq, heads, qoff, rider=None):
    nc, cps, nblk = _hg_chunks_per_step(seq)
    blk = cps * CHUNK
    zspec = lambda off: pl.BlockSpec((blk, HEAD), lambda h, b, n, off=off: (b * nblk + n, off + h))

    def body(zq_ref, zf_ref, zi_ref, zg_ref, lb_ref, gn_ref, o_ref, yb_ref, st_ref, sc_ref, state):
        @pl.when(pl.program_id(2) == 0)
        def _():
            state[...] = jnp.zeros_like(state)

        lb = lb_ref[...]
        gain_v = gn_ref[...]

        def chunk(ci, carry):
            rows = pl.ds(pl.multiple_of(ci * CHUNK, CHUNK), CHUNK)
            st = state[...]
            st_ref[ci] = st
            zi = zi_ref[rows, :]
            zg = zg_ref[rows, :]
            _, _, _, qa, kk, b = _hg_gates(zq_ref[rows, :], zf_ref[rows, :], lb)
            scores = _hg_scores(qa, kk, b).astype(BF16)
            sc_ref[rows, :] = scores
            o = _dot_nt(qa * jnp.exp(b), st) + _dot(scores, zi)
            bl = b[CHUNK - 1:CHUNK, :]
            state[...] = st * jnp.exp(bl) + _dot_tn(zi, kk * jnp.exp(bl - b))
            o_ref[rows, :] = o
            r = lax.rsqrt(jnp.mean(o * o, axis=1, keepdims=True) + EPS)
            yb_ref[rows, :] = (o * r * gain_v * zg * _sigmoid(zg)).astype(yb_ref.dtype)
            return carry

        lax.fori_loop(0, cps, chunk, 0, unroll=True)

    tok = pl.BlockSpec((blk, HEAD), lambda h, b, n: (b * nblk + n, h))
    vec = pl.BlockSpec((1, HEAD), lambda h, b, n: (0, h))
    rows = nseq * seq
    return _hosted_call(
        "hgrn2_fwd", body, grid=(heads, nseq, nblk),
        in_specs=[zspec(qoff), zspec(qoff + heads), zspec(qoff + 2 * heads), zspec(qoff + 3 * heads), vec, vec],
        out_specs=[tok, tok, pl.BlockSpec((None, None, cps, HEAD, HEAD), lambda h, b, n: (h, b, n, 0, 0)),
                   pl.BlockSpec((None, blk, CHUNK), lambda h, b, n: (h, b * nblk + n, 0))],
        out_shape=[jax.ShapeDtypeStruct((rows, heads * HEAD), F32), jax.ShapeDtypeStruct((rows, heads * HEAD), BF16),
                   jax.ShapeDtypeStruct((heads, nseq, nc, HEAD, HEAD), F32),
                   jax.ShapeDtypeStruct((heads, rows, CHUNK), BF16)],
        scratch_shapes=[pltpu.VMEM((HEAD, HEAD), F32)], operands=(z, z, z, z, lbrow, gain), rider=rider)


def _hg_bwd(dyb, z, o, states, scores, lbrow, gain, *, nseq, seq, heads, qoff, rider=None):
    nc, cps, nblk = _hg_chunks_per_step(seq)
    blk = cps * CHUNK
    rev = lambda n: nblk - 1 - n
    zspec = lambda off: pl.BlockSpec((blk, HEAD), lambda h, b, n, off=off: (b * nblk + rev(n), off + h))

    def body(dyb_ref, zq_ref, zf_ref, zi_ref, zg_ref, o_ref, st_ref, sc_ref, lb_ref, gn_ref,
             dzq_ref, dzf_ref, dzi_ref, dzg_ref, dlb_ref, dgn_ref, dstate):
        @pl.when(pl.program_id(2) == 0)
        def _():
            dstate[...] = jnp.zeros_like(dstate)

        @pl.when(jnp.logical_and(pl.program_id(1) == 0, pl.program_id(2) == 0))
        def _():
            dlb_ref[...] = jnp.zeros_like(dlb_ref)
            dgn_ref[...] = jnp.zeros_like(dgn_ref)

        lb = lb_ref[...]
        gain_v = gn_ref[...]
        c = CHUNK
        causal = lax.broadcasted_iota(jnp.int32, (c, c), 0) >= lax.broadcasted_iota(jnp.int32, (c, c), 1)

        def chunk(step, carry):
            ci = cps - 1 - step
            rows = pl.ds(pl.multiple_of(ci * CHUNK, CHUNK), CHUNK)
            zq = zq_ref[rows, :]
            zi = zi_ref[rows, :]
            zg = zg_ref[rows, :]
            sg, f, sq, qa, kk, b = _hg_gates(zq, zf_ref[rows, :], lb)
            eb = jnp.exp(b)
            qt = qa * eb
            bl = b[c - 1:c, :]
            ebl = jnp.exp(bl)
            ekb = jnp.exp(bl - b)
            kh = kk * ekb
            st = st_ref[ci]
            dst = dstate[...]
            o = o_ref[rows, :]
            r = lax.rsqrt(jnp.mean(o * o, axis=1, keepdims=True) + EPS)
            oh = o * r
            szg = _sigmoid(zg)
            dyb = dyb_ref[rows, :]
            don = dyb * zg * szg
            dzg_ref[rows, :] = (dyb * oh * gain_v * szg * (1.0 + zg * (1.0 - szg))).astype(dzg_ref.dtype)
            doh = don * gain_v
            do = r * (doh - oh * jnp.mean(doh * oh, axis=1, keepdims=True))
            dqt = _dot(do, st)
            dp = jnp.where(causal, _dot_nt(do, zi), 0.0)
            dzi_ref[rows, :] = (_dot_tn(sc_ref[rows, :], do) + _dot_nt(kh, dst)).astype(dzi_ref.dtype)
            dkh = _dot(zi, dst)
            dbl = jnp.sum(dkh * kh, axis=0, keepdims=True) + jnp.sum(dst * st, axis=0, keepdims=True) * ebl
            dstate[...] = _dot_tn(do, qt) + dst * ebl
            dqa_s, dkk_s, db_s = _hg_scores_bwd(dp, qa, kk, b)
            dqa = dqt * eb + dqa_s
            dkk = dkh * ekb + dkk_s
            db = dqt * qt - dkh * kh + db_s
            row = lax.broadcasted_iota(jnp.int32, db.shape, 0)
            db = db + jnp.where(row == c - 1, dbl, 0.0)
            df = _cumsum_rows(db, reverse=True) / f - dkk
            dzf_ref[rows, :] = (df * (1.0 - lb) * sg * (1.0 - sg)).astype(dzf_ref.dtype)
            dzq_ref[rows, :] = (dqa * (HEAD ** -0.5) * sq * (1.0 + zq * (1.0 - sq))).astype(dzq_ref.dtype)
            dlb_ref[...] += jnp.sum(df * (1.0 - sg), axis=0, keepdims=True)
            dgn_ref[...] += jnp.sum(don * oh, axis=0, keepdims=True)
            return carry

        lax.fori_loop(0, cps, chunk, 0, unroll=2)

    tok = pl.BlockSpec((blk, HEAD), lambda h, b, n: (b * nblk + rev(n), h))
    vec = pl.BlockSpec((1, HEAD), lambda h, b, n: (0, h))
    rows = nseq * seq
    return _hosted_call(
        "hgrn2_bwd", body, grid=(heads, nseq, nblk),
        in_specs=[tok, zspec(qoff), zspec(qoff + heads), zspec(qoff + 2 * heads), zspec(qoff + 3 * heads), tok,
                  pl.BlockSpec((None, None, cps, HEAD, HEAD), lambda h, b, n: (h, b, rev(n), 0, 0)),
                  pl.BlockSpec((None, blk, CHUNK), lambda h, b, n: (h, b * nblk + rev(n), 0)), vec, vec],
        out_specs=[tok, tok, tok, tok, vec, vec],
        out_shape=[jax.ShapeDtypeStruct((rows, heads * HEAD), BF16)] * 4
        + [jax.ShapeDtypeStruct((1, heads * HEAD), F32)] * 2,
        scratch_shapes=[pltpu.VMEM((HEAD, HEAD), F32)],
        operands=(dyb, z, z, z, z, o, states, scores, lbrow, gain), rider=rider)


def _conv_taps(h, w, bias):
    row = lax.broadcasted_iota(jnp.int32, h.shape, 0)
    h1 = jnp.where(row >= 1, pltpu.roll(h, 1, 0), 0.0)
    h2 = jnp.where(row >= 2, pltpu.roll(h, 2, 0), 0.0)
    return h2 * w[0:1, :] + h1 * w[1:2, :] + h * w[2:3, :] + bias, h1, h2


def _conv_fwd(h, wconv, bconv, *, nseq, seq):
    ff2 = h.shape[1]
    ncol = ff2 // 2 // LANES

    def body(hg_ref, hv_ref, wg_ref, wv_ref, bg_ref, bv_ref, a_ref):
        g, _, _ = _conv_taps(hg_ref[...].astype(F32), wg_ref[...], bg_ref[...])
        v, _, _ = _conv_taps(hv_ref[...].astype(F32), wv_ref[...], bv_ref[...])
        a_ref[...] = (g * _sigmoid(g) * v).astype(a_ref.dtype)

    tok = lambda off: pl.BlockSpec((seq, LANES), lambda j, b, off=off: (b, off + j))
    wsp = lambda off: pl.BlockSpec((CONV_W, LANES), lambda j, b, off=off: (0, off + j))
    bsp = lambda off: pl.BlockSpec((1, LANES), lambda j, b, off=off: (0, off + j))
    return pl.pallas_call(
        body, name="conv_fwd", grid=(ncol, nseq),
        in_specs=[tok(0), tok(ncol), wsp(0), wsp(ncol), bsp(0), bsp(ncol)],
        out_specs=tok(0), out_shape=jax.ShapeDtypeStruct((nseq * seq, ff2 // 2), BF16),
        compiler_params=_params("arbitrary", "arbitrary"),
    )(h, h, wconv, wconv, bconv, bconv)


def _conv_bwd(da, h, wconv, bconv, *, nseq, seq):
    ff2 = h.shape[1]
    ncol = ff2 // 2 // LANES

    def half_bwd(d, hcur, h1, h2, w):
        n = d.shape[0]
        row = lax.broadcasted_iota(jnp.int32, d.shape, 0)
        d1 = jnp.where(row < n - 1, pltpu.roll(d, n - 1, 0), 0.0)
        d2 = jnp.where(row < n - 2, pltpu.roll(d, n - 2, 0), 0.0)
        dh = d * w[2:3, :] + d1 * w[1:2, :] + d2 * w[0:1, :]
        stats = jnp.concatenate(
            [jnp.sum(h2 * d, axis=0, keepdims=True), jnp.sum(h1 * d, axis=0, keepdims=True),
             jnp.sum(hcur * d, axis=0, keepdims=True), jnp.sum(d, axis=0, keepdims=True),
             jnp.zeros((SUBLANES - 4, d.shape[1]), F32)], axis=0)
        return dh, stats

    def body(da_ref, hg_ref, hv_ref, wg_ref, wv_ref, bg_ref, bv_ref, dhg_ref, dhv_ref, sg_ref, sv_ref):
        hg = hg_ref[...].astype(F32)
        hv = hv_ref[...].astype(F32)
        wg = wg_ref[...]
        wv = wv_ref[...]
        g, g1, g2 = _conv_taps(hg, wg, bg_ref[...])
        v, v1, v2 = _conv_taps(hv, wv, bv_ref[...])
        da = da_ref[...].astype(F32)
        s = _sigmoid(g)
        dhg, stg = half_bwd(da * v * s * (1.0 + g * (1.0 - s)), hg, g1, g2, wg)
        dhv, stv = half_bwd(da * g * s, hv, v1, v2, wv)
        dhg_ref[...] = dhg.astype(dhg_ref.dtype)
        dhv_ref[...] = dhv.astype(dhv_ref.dtype)
        first = pl.program_id(1) == 0
        for r, val in ((sg_ref, stg), (sv_ref, stv)):
            @pl.when(first)
            def _():
                r[...] = val

            @pl.when(jnp.logical_not(first))
            def _():
                r[...] += val

    tok = lambda off: pl.BlockSpec((seq, LANES), lambda j, b, off=off: (b, off + j))
    wsp = lambda off: pl.BlockSpec((CONV_W, LANES), lambda j, b, off=off: (0, off + j))
    bsp = lambda off: pl.BlockSpec((1, LANES), lambda j, b, off=off: (0, off + j))
    ssp = pl.BlockSpec((SUBLANES, LANES), lambda j, b: (0, j))
    dhg, dhv, stg, stv = pl.pallas_call(
        body, name="conv_bwd", grid=(ncol, nseq),
        in_specs=[tok(0), tok(0), tok(ncol), wsp(0), wsp(ncol), bsp(0), bsp(ncol)],
        out_specs=[tok(0), tok(0), ssp, ssp],
        out_shape=[jax.ShapeDtypeStruct((nseq * seq, ff2 // 2), BF16)] * 2
        + [jax.ShapeDtypeStruct((SUBLANES, ff2 // 2), F32)] * 2,
        compiler_params=_params("arbitrary", "arbitrary"),
    )(da, h, h, wconv, wconv, bconv, bconv)
    return (dhg, dhv), jnp.concatenate([stg, stv], axis=1)


def _rms_fwd(xv, g):
    r = lax.rsqrt(jnp.mean(xv * xv, axis=1, keepdims=True) + EPS)
    return (xv * r * g,)


def _rms_bwd(xv, g, dy, res):
    r = lax.rsqrt(jnp.mean(xv * xv, axis=1, keepdims=True) + EPS)
    xh = xv * r
    dxh = dy * g
    dx = r * (dxh - xh * jnp.mean(dxh * xh, axis=1, keepdims=True)) + res
    return dx, jnp.sum(dy * xh, axis=0, keepdims=True)


def _loss_head(x2, tgt, g):
    d = x2.shape[1]
    r = lax.rsqrt(jnp.mean(x2 * x2, axis=1, keepdims=True) + EPS)
    xh = x2 * r
    err = xh * g - tgt
    dy = err * (1.0 / d)
    dxh = dy * g
    dx = r * (dxh - xh * jnp.mean(dxh * xh, axis=1, keepdims=True))
    loss = 0.5 * jnp.sum(jnp.mean(err * err, axis=1, keepdims=True), axis=0, keepdims=True)
    return dx, jnp.sum(dy * xh, axis=0, keepdims=True), jnp.broadcast_to(loss, (1, LANES))


LATE_A = ("w_down", "w_out")
LATE_B = ("w_up", "w_pa", "w_pb")
LATE = LATE_A + LATE_B
EARLY_GRADS = ("w_down", "w_up", "w_out", "w_pa", "w_pb", "w_glu")
ROW_SHARDED = ("w_glu", "w_out", "w_down")


def _local_step(x, tgt, p, late, *, nseq, seq):
    p = dict(p)
    chip = 2 * lax.axis_index("x") + lax.axis_index("y")
    t, d = x.shape
    s5w = p["s5_d"].shape[1]
    hgw = p["gain"].shape[1]
    heads = hgw // HEAD
    qoff = s5w // LANES
    gblk = (s5w + 4 * hgw) // GATE_BLOCK
    ngb = d // GATE_BLOCK
    tm = _row_tile(t, 256)
    row = lambda a, w=None, base=0: (a, a.shape[1] if w is None else w, base, "row")
    vec = lambda a, w=None, base=0: (a, a.shape[1] if w is None else w, base, "vec")
    rw = functools.partial(_rowwise, rows=t, tm=tm)

    (u,) = rw("rms_mix", _rms_fwd, [row(x), vec(p["g_mix"])], [(d, d, BF16)])
    z, landed_a = _mm_fwd_cols("in_proj", u, p["w_in"], rider=_gather_ici_rider([late[n] for n in LATE_A]))

    lam_re, lam_im, bb_re, bb_im = _s5_discretize(p["s5_a_re"], p["s5_a_im"], p["s5_log_dt"], p["s5_b_re"], p["s5_b_im"])
    bre3 = _s5_in_blocks(bb_re).astype(BF16)
    bim3 = _s5_in_blocks(bb_im).astype(BF16)
    cre3 = _s5_out_blocks(p["s5_c_re"]).astype(BF16)
    cim3 = _s5_out_blocks(p["s5_c_im"]).astype(BF16)
    coef_f = _s5_scan_tables(lam_re.reshape(-1), lam_im.reshape(-1), False)
    coef_r = _s5_scan_tables(lam_re.reshape(-1), lam_im.reshape(-1), True)
    (o, yb, states, scores), landed_b = _hg_fwd(z, p["lbrow"], p["gain"], nseq=nseq, seq=seq, heads=heads, qoff=qoff,
                                                rider=_gather_ici_rider([late[n] for n in LATE_B]))
    (y5, xre, xim), gathered = _s5_fwd(z, bre3, bim3, cre3, cim3, coef_f, p["s5_d"], nseq=nseq, seq=seq,
                                       rider=_gather_pass_rider(list(landed_a) + list(landed_b)))
    for n, g in zip(LATE, gathered):
        full = lax.dynamic_update_index_in_dim(g, late[n], chip, 0)
        p[n] = full.reshape(-1, full.shape[-1]) if n in ROW_SHARDED else full
    (ya0,) = rw("s5_gelu", lambda y: (_gelu(y),), [row(y5)], [(s5w, s5w, BF16)])
    gl = _mm_fwd_rows("glu_proj", ya0, p["w_glu"])
    (ya,) = rw("s5_glu", lambda y, g, b: (_gelu(y) * _sigmoid(g + b),), [row(y5), row(gl), vec(p["b_glu"])],
               [(s5w, s5w, BF16)])

    joined = lambda w3: w3.transpose(1, 0, 2).reshape(w3.shape[1], -1)
    split = lambda g: g.reshape(g.shape[0], N_CHIPS, -1).transpose(1, 0, 2)
    wpa, wpb = joined(p["w_pa"]), joined(p["w_pb"])
    pa = _mm_fwd_rows("proj_a", ya, wpa, out_dtype=BF16)
    pb = _mm_fwd_rows("proj_b", yb, wpb, out_dtype=BF16)
    gb = GATE_BLOCK
    (m,) = rw("merge", lambda ga, gbv, a, b: (_sigmoid(ga) * a + _sigmoid(gbv) * b,),
              [row(z, gb, gblk), row(z, gb, gblk + ngb), row(pa, gb), row(pb, gb)], [(d, gb, BF16)], ncol=ngb)
    x1 = _mm_fwd_rows("out_proj", m, p["w_out"], res=x)

    (u2,) = rw("rms_ffn", _rms_fwd, [row(x1), vec(p["g_ffn"])], [(d, d, BF16)])
    h = _mm_fwd_cols("up_proj", u2, p["w_up"], out_dtype=BF16)
    a = _conv_fwd(h, p["w_conv"], p["b_conv"], nseq=nseq, seq=seq)
    x2 = _mm_fwd_rows("down_proj", a, p["w_down"], res=x1)

    dx2, dg_final, lossv = rw("loss_head", _loss_head, [row(x2), row(tgt), vec(p["g_final"])], [(d, d, F32)],
                              accs=[(d, d), (LANES, LANES)])

    da = _mm_bwd_rows("down_bwd", dx2, p["w_down"], out_dtype=BF16)
    g_wdown = _mm_wgrad_rows("down_wgrad", a, dx2)
    dh, cstats = _conv_bwd(da, h, p["w_conv"], p["b_conv"], nseq=nseq, seq=seq)
    du2 = _mm_bwd_cols("up_bwd", dh, p["w_up"])
    g_wup = _mm_wgrad_cols("up_wgrad", u2, dh)
    dx1, dg_ffn = rw("rms_ffn_bwd", _rms_bwd, [row(x1), vec(p["g_ffn"]), row(du2), row(dx2)], [(d, d, F32)],
                     accs=[(d, d)])

    dm = _mm_bwd_rows("out_bwd", dx1, p["w_out"], out_dtype=BF16)
    g_wout = _mm_wgrad_rows("out_wgrad", m, dx1)

    def merge_bwd(ga, gbv, av, bv, dmv):
        sa = _sigmoid(ga)
        sb = _sigmoid(gbv)
        return dmv * sa, dmv * sb, dmv * av * sa * (1.0 - sa), dmv * bv * sb * (1.0 - sb)

    dpa, dpb, dzga, dzgb = rw("merge_bwd", merge_bwd,
                              [row(z, gb, gblk), row(z, gb, gblk + ngb), row(pa, gb), row(pb, gb), row(dm, gb)],
                              [(d, gb, BF16)] * 4, ncol=ngb)
    dya = _mm_bwd_rows("proj_a_bwd", dpa, wpa)
    g_wpa = split(_mm_wgrad_rows("proj_a_wgrad", ya, dpa))
    dyb = _mm_bwd_rows("proj_b_bwd", dpb, wpb)
    g_wpb = split(_mm_wgrad_rows("proj_b_wgrad", yb, dpb))

    def glu_bwd1(y, g, b, dyv):
        s = _sigmoid(g + b)
        dgl = dyv * _gelu(y) * s * (1.0 - s)
        return dgl, jnp.sum(dgl, axis=0, keepdims=True)

    dgl, db_glu = rw("s5_glu_bwd", glu_bwd1, [row(y5), row(gl), vec(p["b_glu"]), row(dya)], [(s5w, s5w, BF16)],
                     accs=[(s5w, s5w)])
    dgl_in = _mm_bwd_rows("glu_bwd", dgl, p["w_glu"])
    g_wglu = _mm_wgrad_rows("glu_wgrad", ya0, dgl)
    (dy5,) = rw("s5_gelu_bwd", lambda y, g, b, dyv, tv: ((dyv * _sigmoid(g + b) + tv) * _gelu_grad(y),),
                [row(y5), row(gl), vec(p["b_glu"]), row(dya), row(dgl_in)], [(s5w, s5w, F32)])
    partial = dict(w_down=g_wdown, w_up=g_wup, w_out=g_wout, w_pa=g_wpa, w_pb=g_wpb, w_glu=g_wglu)
    parts = [_grad_parts(partial[n]) for n in EARLY_GRADS]
    (dza, dbre3, dbim3, dcre3, dcim3, dlam, dd), sib = _s5_bwd(
        dy5, z, xre, xim, bre3, bim3, cre3, cim3, coef_r, p["s5_d"], nseq=nseq, seq=seq, rider=_swap_halves_rider(parts))
    pair = _pair_sums(EARLY_GRADS, parts, sib)
    (dzq, dzf, dzi, dzg, dlb, dgain), others = _hg_bwd(
        dyb, z, o, states, scores, p["lbrow"], p["gain"], nseq=nseq, seq=seq, heads=heads, qoff=qoff,
        rider=_scatter_rider(pair))
    halves = _chip_sums(EARLY_GRADS, pair, others)

    dz = jnp.concatenate([dza, dzq, dzf, dzi, dzg, dzga, dzgb], axis=1)
    du, sibs = _mm_bwd_cols("in_bwd", dz, p["w_in"], rider=_swap_sums_rider(halves))
    big = dict(zip(EARLY_GRADS, zip(halves, sibs)))

    gshape = lam_re.shape
    small = {
        "loss": lossv, "g_ffn": dg_ffn, "g_final": dg_final, "b_glu": db_glu, "gain": dgain,
        "lbrow": dlb, "s5_d": dd, "w_conv": cstats[0:CONV_W], "b_conv": cstats[CONV_W:CONV_W + 1],
        "lam_re": dlam[:, 0, :].reshape(gshape), "lam_im": dlam[:, 1, :].reshape(gshape),
        "bb_re": _s5_in_blocks_diag(dbre3), "bb_im": _s5_in_blocks_diag(dbim3),
        "s5_c_re": _s5_out_blocks_diag(dcre3), "s5_c_im": _s5_out_blocks_diag(dcim3),
    }
    small_vec = _pack([small[n] for n in SMALL_PARTS], F32)
    g_win, (small_all,) = _mm_wgrad_cols("in_wgrad", u, dz, rider=_gather_all_rider(small_vec))
    small_sum = _sum_over_devices("small_grad_sum", small_vec, small_all)
    sm = dict(zip(SMALL_PARTS, _unpack(small_sum, [small[n].shape for n in SMALL_PARTS])))
    last = [_grad_parts(g_win)]
    pair = _pair_sums(("w_in",), last, _run_rider("grad_swap_halves", _swap_halves_rider(last)))
    (half,) = _chip_sums(("w_in",), pair, _run_rider("grad_scatter_chips", _scatter_rider(pair)))
    mid = half.shape[0] // 2
    sib_half = jnp.concatenate(_run_rider("grad_swap_sums", _swap_sums_rider([half[:mid], half[mid:]])), axis=0)
    big["w_in"] = (half, sib_half)

    dx, dg_mix = rw("rms_mix_bwd", _rms_bwd, [row(x), vec(p["g_mix"]), row(du), row(dx1)], [(d, d, F32)],
                    accs=[(d, d)])

    mix_vec = dg_mix.reshape(SUBLANES, -1)
    (mix_all,) = _run_rider("gather_g_mix", _gather_all_rider(mix_vec))
    sm["g_mix"] = _sum_over_devices("g_mix_sum", mix_vec, mix_all).reshape(dg_mix.shape)
    return dx, big, sm


ANY = pl.BlockSpec(memory_space=pl.ANY)


def _place():
    x, y, c = lax.axis_index("x"), lax.axis_index("y"), lax.axis_index("c")
    chips = [(1 - x, y), (x, 1 - y), (1 - x, 1 - y)]
    return x, y, c, chips


def _remote(src, dst, send_sems, recv_sems, k, to):
    return pltpu.make_async_remote_copy(src_ref=src, dst_ref=dst, send_sem=send_sems.at[k], recv_sem=recv_sems.at[k],
                                        device_id=to, device_id_type=MESH)


def _half(rows, which):
    return pl.ds(pl.multiple_of(which * (rows // 2), 16), rows // 2)


def _gather_weights(shards, whole):
    n, nw = len(shards), len(whole)
    arrays = list(shards) + list(whole)

    def body(*refs):
        in_refs, out_refs = refs[:n + nw], refs[n + nw:2 * (n + nw)]
        send_sems, recv_sems = refs[2 * (n + nw):]
        x, y, c, chips = _place()
        me = 2 * x + y
        copy = functools.partial(_remote, send_sems=send_sems, recv_sems=recv_sems)
        sends = []
        for a in range(n):
            mine_half = _half(arrays[a].shape[0], c)
            for j, (cx, cy) in enumerate(chips):
                sends.append(copy(in_refs[a].at[mine_half], out_refs[a].at[me, mine_half], k=6 * a + j, to=(cx, cy, c)))
        for a in range(n, n + nw):
            for j, (cx, cy) in enumerate(chips):
                sends.append(copy(in_refs[a], out_refs[a].at[me], k=6 * n + 3 * (a - n) + j, to=(cx, cy, c)))
        for cp in sends:
            cp.start()
        for a in range(n):
            mine_half = _half(arrays[a].shape[0], c)
            for j, (cx, cy) in enumerate(chips):
                landed = out_refs[a].at[2 * cx + cy, mine_half]
                copy(landed, landed, k=6 * a + j, to=(x, y, c)).wait_recv()
                fwd = copy(landed, landed, k=6 * a + 3 + j, to=(x, y, 1 - c))
                fwd.start()
                sends.append(fwd)
        for a in range(n):
            other_half = _half(arrays[a].shape[0], 1 - c)
            for j, (cx, cy) in enumerate(chips):
                landed = out_refs[a].at[2 * cx + cy, other_half]
                copy(landed, landed, k=6 * a + 3 + j, to=(x, y, c)).wait_recv()
        for a in range(n, n + nw):
            for j, (cx, cy) in enumerate(chips):
                landed = out_refs[a].at[2 * cx + cy]
                copy(landed, landed, k=6 * n + 3 * (a - n) + j, to=(x, y, c)).wait_recv()
        for cp in sends:
            cp.wait_send()

    nsem = 6 * n + 3 * nw
    return pl.pallas_call(
        body, name="gather_weights", out_shape=[jax.ShapeDtypeStruct((N_CHIPS,) + a.shape, a.dtype) for a in arrays],
        in_specs=[ANY] * (n + nw), out_specs=[ANY] * (n + nw),
        scratch_shapes=[pltpu.SemaphoreType.DMA((nsem,)), pltpu.SemaphoreType.DMA((nsem,))],
    )(*arrays)


def _symmetric_rider(arrays, out_shapes, copies_of, nsem):
    def start(ins, outs, send_sems, recv_sems):
        for cp in copies_of(ins, outs, send_sems, recv_sems):
            cp.start()

    def finish(ins, outs, send_sems, recv_sems):
        for cp in copies_of(ins, outs, send_sems, recv_sems):
            cp.wait()

    return _Rider(arrays, out_shapes, nsem, start, finish)


def _swap_halves_rider(parts):
    def copies_of(ins, outs, send_sems, recv_sems):
        x, y, c, _ = _place()
        return [_remote(ins[a].at[:, _half(g.shape[1], 1 - c), :], outs[a], send_sems, recv_sems, a, (x, y, 1 - c))
                for a, g in enumerate(parts)]

    shapes = [jax.ShapeDtypeStruct((g.shape[0], g.shape[1] // 2, g.shape[2]), g.dtype) for g in parts]
    return _symmetric_rider(parts, shapes, copies_of, len(parts))


def _scatter_rider(parts):
    def copies_of(ins, outs, send_sems, recv_sems):
        x, y, c, chips = _place()
        return [_remote(ins[a].at[2 * cx + cy], outs[a].at[j], send_sems, recv_sems, 3 * a + j, (cx, cy, c))
                for a in range(len(parts)) for j, (cx, cy) in enumerate(chips)]

    shapes = [jax.ShapeDtypeStruct((N_CHIPS - 1,) + h.shape[1:], h.dtype) for h in parts]
    return _symmetric_rider(parts, shapes, copies_of, 3 * len(parts))


def _swap_sums_rider(parts):
    def copies_of(ins, outs, send_sems, recv_sems):
        x, y, c, _ = _place()
        return [_remote(ins[a], outs[a], send_sems, recv_sems, a, (x, y, 1 - c)) for a in range(len(parts))]

    shapes = [jax.ShapeDtypeStruct(g.shape, g.dtype) for g in parts]
    return _symmetric_rider(parts, shapes, copies_of, len(parts))


def _gather_ici_rider(shards):
    def sends(ins, outs, send_sems, recv_sems):
        x, y, c, chips = _place()
        return [_remote(ins[a].at[_half(s.shape[0], c)], outs[a].at[2 * x + y, _half(s.shape[0], c)], send_sems,
                        recv_sems, 3 * a + j, (cx, cy, c)) for a, s in enumerate(shards) for j, (cx, cy) in enumerate(chips)]

    def start(ins, outs, send_sems, recv_sems):
        for cp in sends(ins, outs, send_sems, recv_sems):
            cp.start()

    def finish(ins, outs, send_sems, recv_sems):
        x, y, c, chips = _place()
        for a, s in enumerate(shards):
            for j, (cx, cy) in enumerate(chips):
                landed = outs[a].at[2 * cx + cy, _half(s.shape[0], c)]
                _remote(landed, landed, send_sems, recv_sems, 3 * a + j, (x, y, c)).wait_recv()
        for cp in sends(ins, outs, send_sems, recv_sems):
            cp.wait_send()

    shapes = [jax.ShapeDtypeStruct((N_CHIPS,) + s.shape, s.dtype) for s in shards]
    return _Rider(shards, shapes, 3 * len(shards), start, finish)


def _gather_pass_rider(landed):
    def sends(ins, outs, send_sems, recv_sems):
        x, y, c, chips = _place()
        return [_remote(ins[a].at[2 * cx + cy, _half(g.shape[1], c)], outs[a].at[2 * cx + cy, _half(g.shape[1], c)],
                        send_sems, recv_sems, 3 * a + j, (x, y, 1 - c))
                for a, g in enumerate(landed) for j, (cx, cy) in enumerate(chips)]

    def start(ins, outs, send_sems, recv_sems):
        for cp in sends(ins, outs, send_sems, recv_sems):
            cp.start()

    def finish(ins, outs, send_sems, recv_sems):
        x, y, c, chips = _place()
        for a, g in enumerate(landed):
            for j, (cx, cy) in enumerate(chips):
                other = outs[a].at[2 * cx + cy, _half(g.shape[1], 1 - c)]
                _remote(other, other, send_sems, recv_sems, 3 * a + j, (x, y, c)).wait_recv()
        for cp in sends(ins, outs, send_sems, recv_sems):
            cp.wait_send()

    shapes = [jax.ShapeDtypeStruct(g.shape, g.dtype) for g in landed]
    return _Rider(landed, shapes, 3 * len(landed), start, finish, aliases={a: a for a in range(len(landed))})


def _grad_parts(g):
    return g.reshape((N_CHIPS, -1, g.shape[-1]))


def _place_scalars():
    return jnp.stack([lax.axis_index("c"), 2 * lax.axis_index("x") + lax.axis_index("y")]).astype(jnp.int32)


def _scalar_call(body, name, grid, in_specs, out_specs, out_shape, operands):
    spec = pltpu.PrefetchScalarGridSpec(num_scalar_prefetch=1, grid=grid, in_specs=in_specs, out_specs=out_specs)
    return pl.pallas_call(body, name=name, grid_spec=spec, out_shape=out_shape,
                          compiler_params=_params(*(["arbitrary"] * len(grid))))(_place_scalars(), *operands)


def _pair_sums(names, parts, sib):
    out = []
    for n, g, s in zip(names, parts, sib):
        rh, cols = s.shape[1], s.shape[2]
        tm = _row_tile(rh, 512)
        nblk = rh // tm

        def body(place, g_ref, s_ref, o_ref):
            o_ref[...] = (g_ref[...].astype(F32) + s_ref[...].astype(F32)).astype(o_ref.dtype)

        blk = pl.BlockSpec((None, tm, cols), lambda j, i, place: (j, i, 0))
        own = pl.BlockSpec((None, tm, cols), lambda j, i, place, nblk=nblk: (j, place[0] * nblk + i, 0))
        out.append(_scalar_call(body, "grad_pair_sum_" + n, (N_CHIPS, nblk), [own, blk], blk,
                                jax.ShapeDtypeStruct(s.shape, BF16), (g, s)))
    return out


def _chip_sums(names, pair, others):
    out = []
    for n, h, o in zip(names, pair, others):
        rh, cols = h.shape[1], h.shape[2]
        tm = _row_tile(rh, 512)

        def body(place, h_ref, a_ref, b_ref, c_ref, o_ref):
            o_ref[...] = (h_ref[...].astype(F32) + a_ref[...].astype(F32)) + b_ref[...].astype(F32) + c_ref[...].astype(F32)

        mine = pl.BlockSpec((None, tm, cols), lambda i, place: (place[1], i, 0))
        other = lambda k: pl.BlockSpec((None, tm, cols), lambda i, place, k=k: (k, i, 0))
        out.append(_scalar_call(body, "grad_chip_sum_" + n, (rh // tm,), [mine, other(0), other(1), other(2)],
                                pl.BlockSpec((tm, cols), lambda i, place: (i, 0)), jax.ShapeDtypeStruct((rh, cols), F32),
                                (h, o, o, o)))
    return out


def _adamw_halves(name, w, m, v, own, sib):
    rh, cols = own.shape
    tm = _row_tile(rh, 256)
    nblk = rh // tm

    def body(place, w_ref, m_ref, v_ref, own_ref, sib_ref, g_ref, d_ref, m2_ref, v2_ref):
        mine = pl.program_id(0) // nblk == place[0]

        def run(gv):
            g_ref[...] = gv
            d_ref[...], m2_ref[...], v2_ref[...] = _adamw_math(w_ref[...], gv, m_ref[...], v_ref[...])

        @pl.when(mine)
        def _():
            run(own_ref[...])

        @pl.when(jnp.logical_not(mine))
        def _():
            run(sib_ref[...])

    full = pl.BlockSpec((tm, cols), lambda i, place: (i, 0))
    own_spec = pl.BlockSpec((tm, cols), lambda i, place: (jnp.where(i // nblk == place[0], i % nblk, 0), 0))
    sib_spec = pl.BlockSpec((tm, cols), lambda i, place: (jnp.where(i // nblk == place[0], 0, i % nblk), 0))
    return _scalar_call(body, name, (2 * nblk,), [full, full, full, own_spec, sib_spec], [full] * 4,
                        [jax.ShapeDtypeStruct((2 * rh, cols), F32)] * 4, (w, m, v, own, sib))


def _gather_all_rider(v):
    m_per = v.shape[0]

    def rows(ref, px, py, pc):
        return ref.at[pl.ds(pl.multiple_of((4 * px + 2 * py + pc) * m_per, 8), m_per)]

    def first(ins, outs, send_sems, recv_sems):
        x, y, c, chips = _place()
        mine = rows(outs[0], x, y, c)
        return [_remote(ins[0], mine, send_sems, recv_sems, 0, (x, y, 1 - c))] + [
            _remote(ins[0], mine, send_sems, recv_sems, 1 + j, (cx, cy, c)) for j, (cx, cy) in enumerate(chips)]

    def start(ins, outs, send_sems, recv_sems):
        for cp in first(ins, outs, send_sems, recv_sems):
            cp.start()

    def finish(ins, outs, send_sems, recv_sems):
        x, y, c, chips = _place()
        passed = []
        for j, (cx, cy) in enumerate(chips):
            blk = rows(outs[0], cx, cy, c)
            _remote(blk, blk, send_sems, recv_sems, 1 + j, (x, y, c)).wait_recv()
            passed.append(_remote(blk, blk, send_sems, recv_sems, 4 + j, (x, y, 1 - c)))
            passed[j].start()
        sib = rows(outs[0], x, y, 1 - c)
        _remote(sib, sib, send_sems, recv_sems, 0, (x, y, c)).wait_recv()
        for j, (cx, cy) in enumerate(chips):
            blk = rows(outs[0], cx, cy, 1 - c)
            _remote(blk, blk, send_sems, recv_sems, 4 + j, (x, y, c)).wait_recv()
        for cp in first(ins, outs, send_sems, recv_sems) + passed:
            cp.wait_send()

    return _Rider([v], [jax.ShapeDtypeStruct((N_DEV * m_per,) + v.shape[1:], v.dtype)], 7, start, finish)


def _sum_over_devices(name, v, gathered):
    m_per = v.shape[0]
    dev = 4 * lax.axis_index("x") + 2 * lax.axis_index("y") + lax.axis_index("c")
    full = lax.dynamic_update_slice_in_dim(gathered, v, dev * m_per, axis=0)
    return _sum_blocks(name, [full[i * m_per:(i + 1) * m_per] for i in range(N_DEV)], F32)


def _sum_blocks(name, parts, out_dtype):
    rows, cols = parts[0].shape
    tm = _row_tile(rows, 512)

    def body(*refs):
        acc = refs[0][...].astype(F32)
        for r in refs[1:-1]:
            acc = acc + r[...].astype(F32)
        refs[-1][...] = acc.astype(refs[-1].dtype)

    spec = pl.BlockSpec((tm, cols), lambda i: (i, 0))
    return pl.pallas_call(
        body, name=name, grid=(rows // tm,), in_specs=[spec] * len(parts), out_specs=spec,
        out_shape=jax.ShapeDtypeStruct((rows, cols), out_dtype), compiler_params=_params("arbitrary"),
    )(*parts)


def _adamw_math(wv, gv, mv, vv):
    m2 = ADAM_B1 * mv + (1.0 - ADAM_B1) * gv
    v2 = ADAM_B2 * vv + (1.0 - ADAM_B2) * (gv * gv)
    delta = -ADAM_LR * ((m2 / (1.0 - ADAM_B1 ** ADAM_STEP)) / (jnp.sqrt(v2 / (1.0 - ADAM_B2 ** ADAM_STEP)) + ADAM_EPS)
                        + ADAM_WD * wv)
    return delta, m2, v2


def _adamw_small(ws, gs, ms, vs):
    n = len(ws)

    def body(*refs):
        for i in range(n):
            res = _adamw_math(refs[i][...], refs[n + i][...], refs[2 * n + i][...], refs[3 * n + i][...])
            for k in range(3):
                refs[(4 + k) * n + i][...] = res[k]

    vm = pl.BlockSpec(memory_space=pltpu.VMEM)
    outs = pl.pallas_call(
        body, name="adamw_small", in_specs=[vm] * (4 * n), out_specs=[vm] * (3 * n),
        out_shape=[jax.ShapeDtypeStruct(a.shape, F32) for a in ws] * 3,
        compiler_params=pltpu.CompilerParams(vmem_limit_bytes=VMEM_LIMIT_BYTES),
    )(*ws, *gs, *ms, *vs)
    return outs[:n], outs[n:2 * n], outs[2 * n:]


PACK_ROWS = 256


def _pack(flat_parts, dtype, lead=()):
    parts = [a.astype(dtype).reshape(lead + (-1,)) for a in flat_parts]
    n = sum(a.shape[-1] for a in parts)
    chunk = PACK_ROWS * LANES
    total = -(-n // chunk) * chunk
    if total > n:
        parts.append(jnp.zeros(lead + (total - n,), dtype))
    return jnp.concatenate(parts, axis=-1).reshape(lead + (total // LANES, LANES))


def _unpack(buf, shapes, lead=()):
    flat = buf.reshape(lead + (-1,))
    out, off = [], 0
    for shp in shapes:
        n = math.prod(shp)
        out.append(lax.slice_in_dim(flat, off, off + n, axis=len(lead)).reshape(lead + tuple(shp)))
        off += n
    return out


BIG = ("w_in", "w_glu", "w_pa", "w_pb", "w_out", "w_up", "w_down")
WEIGHTS = ("g_mix", "w_in", "s5_a_re", "s5_a_im", "s5_log_dt", "s5_b_re", "s5_b_im", "s5_c_re", "s5_c_im", "s5_d",
           "w_glu", "b_glu", "hg_lb_logits", "hg_norm_gain", "w_pa", "w_pb", "w_out", "g_ffn", "w_up", "w_conv",
           "b_conv", "w_down", "g_final")
SMALL = tuple(n for n in WEIGHTS if n not in BIG)
SMALL_PARTS = ("loss", "g_ffn", "g_final", "b_glu", "gain", "lbrow", "s5_d", "w_conv", "b_conv", "lam_re", "lam_im",
               "bb_re", "bb_im", "s5_c_re", "s5_c_im")


def _lower_bound(logits):
    return jnp.cumsum(jax.nn.softmax(logits, axis=0), axis=0)[0:1]


def kernel(x, g_mix, w_in, s5_a_re, s5_a_im, s5_log_dt, s5_b_re, s5_b_im, s5_c_re, s5_c_im, s5_d, w_glu, b_glu, hg_lb_logits, hg_norm_gain, w_pa, w_pb, w_out, g_ffn, w_up, w_conv, b_conv, w_down, g_final, loss_target, m_g_mix, m_w_in, m_s5_a_re, m_s5_a_im, m_s5_log_dt, m_s5_b_re, m_s5_b_im, m_s5_c_re, m_s5_c_im, m_s5_d, m_w_glu, m_b_glu, m_hg_lb_logits, m_hg_norm_gain, m_w_pa, m_w_pb, m_w_out, m_g_ffn, m_w_up, m_w_conv, m_b_conv, m_w_down, m_g_final, v_g_mix, v_w_in, v_s5_a_re, v_s5_a_im, v_s5_log_dt, v_s5_b_re, v_s5_b_im, v_s5_c_re, v_s5_c_im, v_s5_d, v_w_glu, v_b_glu, v_hg_lb_logits, v_hg_norm_gain, v_w_pa, v_w_pb, v_w_out, v_g_ffn, v_w_up, v_w_conv, v_b_conv, v_w_down, v_g_final):
    args = dict(locals())
    w = {n: args[n] for n in WEIGHTS}
    mom = {n: args["m_" + n] for n in WEIGHTS}
    var = {n: args["v_" + n] for n in WEIGHTS}
    nseq, seq, d = x.shape
    xi, yi = lax.axis_index("x"), lax.axis_index("y")
    chip = 2 * xi + yi

    shard = {n: w[n][0] for n in BIG}
    shard16 = {n: shard[n].astype(BF16) for n in BIG}
    first = ("w_in", "w_glu")
    got = _gather_weights([shard16[n] for n in first], [w_conv[0]])
    p = {n: lax.dynamic_update_index_in_dim(g, shard16[n], chip, 0) for n, g in zip(first, got)}
    p["w_glu"] = p["w_glu"].reshape(-1, p["w_glu"].shape[-1])
    conv_all = lax.dynamic_update_index_in_dim(got[-1], w_conv[0], chip, 0)
    p.update(g_mix=g_mix, g_ffn=g_ffn, g_final=g_final.reshape(1, -1), b_glu=b_glu, gain=hg_norm_gain, s5_d=s5_d,
             b_conv=b_conv, w_conv=conv_all.transpose(1, 0, 2).reshape(CONV_W, -1), lbrow=_lower_bound(hg_lb_logits),
             s5_a_re=s5_a_re[0], s5_a_im=s5_a_im[0], s5_log_dt=s5_log_dt[0], s5_b_re=s5_b_re[0], s5_b_im=s5_b_im[0],
             s5_c_re=s5_c_re[0], s5_c_im=s5_c_im[0])

    dx, halves, sm = _local_step(x.reshape(nseq * seq, d), loss_target.reshape(nseq * seq, d), p,
                                 {n: shard16[n] for n in LATE}, nseq=nseq, seq=seq)
    loss = sm["loss"][0, 0]

    grads, delta, new_m, new_v = {}, {}, {}, {}
    for n in BIG:
        shp = shard[n].shape
        grads[n], delta[n], new_m[n], new_v[n] = _adamw_halves("adamw_" + n, shard[n], mom[n].reshape(shp),
                                                               var[n].reshape(shp), *halves[n])

    _, disc_vjp = jax.vjp(_s5_discretize, p["s5_a_re"], p["s5_a_im"], p["s5_log_dt"], p["s5_b_re"], p["s5_b_im"])
    da_re, da_im, dlog_dt, db_re, db_im = disc_vjp((sm["lam_re"], sm["lam_im"], sm["bb_re"], sm["bb_im"]))
    _, lb_vjp = jax.vjp(_lower_bound, hg_lb_logits)
    (dlogits,) = lb_vjp(sm["lbrow"])
    fcols = w_conv.shape[-1]
    grads.update(
        g_mix=sm["g_mix"], g_ffn=sm["g_ffn"], g_final=sm["g_final"].reshape(-1), b_glu=sm["b_glu"],
        hg_norm_gain=sm["gain"], hg_lb_logits=dlogits, s5_d=sm["s5_d"], b_conv=sm["b_conv"],
        w_conv=lax.dynamic_slice_in_dim(sm["w_conv"], chip * fcols, fcols, axis=1),
        s5_a_re=da_re, s5_a_im=da_im, s5_log_dt=dlog_dt, s5_b_re=db_re, s5_b_im=db_im,
        s5_c_re=sm["s5_c_re"], s5_c_im=sm["s5_c_im"])
    grads = {n: grads[n].reshape(w[n].shape) for n in WEIGHTS}

    def natural(a):
        return a.reshape(1, -1) if a.ndim == 1 else (a[0] if a.ndim > 2 else a)

    outs = _adamw_small(*[[natural(src[n]) for n in SMALL] for src in (w, grads, mom, var)])
    for dst, group in zip((delta, new_m, new_v), outs):
        dst.update(zip(SMALL, group))
    res = [loss, dx.reshape(x.shape)]
    for group in (grads, delta, new_m, new_v):
        res += [group[n].reshape(w[n].shape) for n in WEIGHTS]
    return tuple(res)
```

```python
import functools
import math

import jax
import jax.numpy as jnp
from jax import lax
from jax.experimental import pallas as pl
from jax.experimental.pallas import tpu as pltpu

F32 = jnp.float32
BF16 = jnp.bfloat16
MESH = pl.DeviceIdType.MESH

EPS = 1e-6
S5_GROUP = 16
S5_STATE = 64
S5_BLOCK_GROUPS = 8
HEAD = 128
CHUNK = 64
CONV_W = 3
LANES = 128
SUBLANES = 8
GATE_BLOCK = 512
VMEM_LIMIT_BYTES = 56 * 1024 * 1024

ADAM_LR = 0.001
ADAM_B1 = 0.9
ADAM_B2 = 0.999
ADAM_EPS = 1e-08
ADAM_WD = 0.01
ADAM_STEP = 10

N_CHIPS = 4
N_DEV = 8


def _params(*sem):
    return pltpu.CompilerParams(dimension_semantics=sem, vmem_limit_bytes=VMEM_LIMIT_BYTES)


class _Rider:
    def __init__(self, arrays, out_shapes, nsem, start, finish, aliases=None):
        self.arrays, self.out_shapes, self.nsem = list(arrays), list(out_shapes), nsem
        self.start, self.finish, self.aliases = start, finish, dict(aliases or {})


def _hosted_call(name, body, *, grid, in_specs, out_specs, out_shape, operands, scratch_shapes=(), rider=None):
    in_specs, out_specs, out_shape, scratch_shapes = list(in_specs), list(out_specs), list(out_shape), list(scratch_shapes)
    cparams = _params(*(["arbitrary"] * len(grid)))
    if rider is None:
        return pl.pallas_call(body, name=name, grid=grid, in_specs=in_specs, out_specs=out_specs, out_shape=out_shape,
                              scratch_shapes=scratch_shapes, compiler_params=cparams)(*operands)
    n_in, n_out, n_sc = len(in_specs), len(out_specs), len(scratch_shapes)
    r_in, r_out = len(rider.arrays), len(rider.out_shapes)

    def hosted(*refs):
        ins, rins = refs[:n_in], refs[n_in:n_in + r_in]
        outs = refs[n_in + r_in:n_in + r_in + n_out]
        routs = refs[n_in + r_in + n_out:n_in + r_in + n_out + r_out]
        rest = refs[n_in + r_in + n_out + r_out:]
        send_sems, recv_sems = rest[n_sc], rest[n_sc + 1]
        first = functools.reduce(jnp.logical_and, [pl.program_id(i) == 0 for i in range(len(grid))])
        last = functools.reduce(jnp.logical_and, [pl.program_id(i) == grid[i] - 1 for i in range(len(grid))])

        @pl.when(first)
        def _():
            rider.start(rins, routs, send_sems, recv_sems)

        body(*ins, *outs, *rest[:n_sc])

        @pl.when(last)
        def _():
            rider.finish(rins, routs, send_sems, recv_sems)

    res = pl.pallas_call(
        hosted, name=name, grid=grid, in_specs=in_specs + [ANY] * r_in, out_specs=out_specs + [ANY] * r_out,
        out_shape=out_shape + rider.out_shapes,
        scratch_shapes=scratch_shapes + [pltpu.SemaphoreType.DMA((rider.nsem,)), pltpu.SemaphoreType.DMA((rider.nsem,))],
        input_output_aliases={n_in + i: n_out + o for i, o in rider.aliases.items()}, compiler_params=cparams,
    )(*operands, *rider.arrays)
    return res[:n_out], res[n_out:]


def _run_rider(name, rider):
    r_in, r_out = len(rider.arrays), len(rider.out_shapes)

    def body(*refs):
        rins, routs, send_sems, recv_sems = refs[:r_in], refs[r_in:r_in + r_out], refs[-2], refs[-1]
        rider.start(rins, routs, send_sems, recv_sems)
        rider.finish(rins, routs, send_sems, recv_sems)

    return pl.pallas_call(
        body, name=name, in_specs=[ANY] * r_in, out_specs=[ANY] * r_out, out_shape=rider.out_shapes,
        scratch_shapes=[pltpu.SemaphoreType.DMA((rider.nsem,)), pltpu.SemaphoreType.DMA((rider.nsem,))],
        input_output_aliases=rider.aliases,
    )(*rider.arrays)


def _row_tile(rows, cap):
    if rows <= cap:
        return rows
    for t in range(cap - cap % 8, 7, -8):
        if rows % t == 0:
            return t
    raise ValueError(f"no row tile for {rows}")


def _dot(a, b):
    return jnp.dot(a.astype(BF16), b.astype(BF16), preferred_element_type=F32)


def _dot_nt(a, b):
    return lax.dot_general(a.astype(BF16), b.astype(BF16), (((1,), (1,)), ((), ())), preferred_element_type=F32)


def _dot_tn(a, b):
    return lax.dot_general(a.astype(BF16), b.astype(BF16), (((0,), (0,)), ((), ())), preferred_element_type=F32)


def _sigmoid(x):
    return 0.5 * jnp.tanh(0.5 * x) + 0.5


_GELU_C = math.sqrt(2.0 / math.pi)


def _gelu(x):
    return 0.5 * x * (1.0 + jnp.tanh(_GELU_C * (x + 0.044715 * x * x * x)))


def _gelu_grad(x):
    th = jnp.tanh(_GELU_C * (x + 0.044715 * x * x * x))
    return 0.5 * (1.0 + th) + 0.5 * x * (1.0 - th * th) * _GELU_C * (1.0 + 3.0 * 0.044715 * x * x)


def _rowwise(name, fn, ins, outs, accs=(), *, rows, tm, ncol=1):
    n_in, n_out = len(ins), len(outs)

    def body(*refs):
        res = fn(*[r[...] for r in refs[:n_in]])
        for r, v in zip(refs[n_in:n_in + n_out], res[:n_out]):
            r[...] = v.astype(r.dtype)
        first = pl.program_id(1) == 0
        for r, v in zip(refs[n_in + n_out:], res[n_out:]):
            @pl.when(first)
            def _():
                r[...] = v

            @pl.when(jnp.logical_not(first))
            def _():
                r[...] += v

    in_specs = []
    for _, width, base, kind in ins:
        if kind == "row":
            in_specs.append(pl.BlockSpec((tm, width), lambda j, i, b=base: (i, b + j)))
        else:
            in_specs.append(pl.BlockSpec((1, width), lambda j, i, b=base: (0, b + j)))
    out_specs = [pl.BlockSpec((tm, width), lambda j, i: (i, j)) for _, width, _ in outs]
    out_specs += [pl.BlockSpec((1, width), lambda j, i: (0, j)) for _, width in accs]
    out_shape = [jax.ShapeDtypeStruct((rows, total), dt) for total, _, dt in outs]
    out_shape += [jax.ShapeDtypeStruct((1, total), F32) for total, _ in accs]
    return pl.pallas_call(
        body, name=name, grid=(ncol, rows // tm), in_specs=in_specs, out_specs=out_specs, out_shape=out_shape,
        compiler_params=_params("arbitrary", "arbitrary"),
    )(*[a for a, _, _, _ in ins])


def _mm(name, a, b, *, mode, grid, a_spec, b_spec, o_spec, out_shape, acc_shape, res=None, res_spec=None,
        pair_axis=None, rider=None, epilogue=None):
    nk = grid[2]
    dot = {"nn": _dot, "nt": _dot_nt, "tn": _dot_tn}[mode]
    a_list = list(a) if isinstance(a, tuple) else [a]
    b_list = list(b) if isinstance(b, tuple) else [b]
    na, nb = len(a_list), len(b_list)
    assert (pair_axis is None) == (na + nb == 2)
    direct = nk == 1 and pair_axis is None
    epi_fn, epi_ins, epi_sums = epilogue if epilogue is not None else (None, [], [])
    n_res = 0 if res is None else 1
    n_epi = len(epi_ins)

    def body(*refs):
        a_refs, b_refs = refs[:na], refs[na:na + nb]
        r_ref = None if res is None else refs[na + nb]
        e_refs = refs[na + nb + n_res:na + nb + n_res + n_epi]
        o_ref = refs[na + nb + n_res + n_epi]
        s_refs = refs[na + nb + n_res + n_epi + 1:na + nb + n_res + n_epi + 1 + len(epi_sums)]
        first_rows = pl.program_id(0) == 0

        def finish(v):
            if res is not None:
                v = v + r_ref[...]
            if epi_fn is None:
                o_ref[...] = v.astype(o_ref.dtype)
                return
            outs = epi_fn(v, *[r[...] for r in e_refs])
            o_ref[...] = outs[0].astype(o_ref.dtype)
            for s_ref, part in zip(s_refs, outs[1:]):
                @pl.when(first_rows)
                def _():
                    s_ref[...] = part

                @pl.when(jnp.logical_not(first_rows))
                def _():
                    s_ref[...] += part

        if direct:
            finish(dot(a_refs[0][...], b_refs[0][...]))
            return
        acc_ref = refs[-1]
        k = pl.program_id(2)

        @pl.when(k == 0)
        def _():
            acc_ref[...] = jnp.zeros_like(acc_ref)

        if pair_axis is None:
            acc_ref[...] += dot(a_refs[0][...], b_refs[0][...])
        else:
            lower = pl.program_id(pair_axis) < grid[pair_axis] // 2

            @pl.when(lower)
            def _():
                acc_ref[...] += dot(a_refs[0][...], b_refs[0][...])

            @pl.when(jnp.logical_not(lower))
            def _():
                acc_ref[...] += dot(a_refs[-1][...], b_refs[-1][...])

        @pl.when(k == nk - 1)
        def _():
            finish(acc_ref[...])

    operands = a_list + b_list + ([] if res is None else [res]) + [arr for arr, _ in epi_ins]
    in_specs = (list(a_spec) if na == 2 else [a_spec]) + (list(b_spec) if nb == 2 else [b_spec])
    in_specs += ([] if res is None else [res_spec]) + [spec for _, spec in epi_ins]
    out_specs = [o_spec] + [pl.BlockSpec((1, c), lambda *_: (0, 0)) for c in epi_sums]
    out_shapes = [out_shape] + [jax.ShapeDtypeStruct((1, c), F32) for c in epi_sums]
    got = _hosted_call(name, body, grid=grid, in_specs=in_specs, out_specs=out_specs, out_shape=out_shapes,
                       scratch_shapes=[] if direct else [pltpu.VMEM(acc_shape, F32)], operands=operands, rider=rider)
    mine, rider_outs = (got, None) if rider is None else got
    mine = mine[0] if epilogue is None else tuple(mine)
    return mine if rider is None else (mine, rider_outs)


MM_TILE_BUDGET_BYTES = 36 * 1024 * 1024
MM_TILE_CAP = 1024


def _mm_tile(t, row_bytes, fixed_bytes):
    cap = max(16, min(MM_TILE_CAP, (MM_TILE_BUDGET_BYTES - fixed_bytes) // row_bytes))
    return _row_tile(t, cap - cap % 16)


def _size(a):
    return jnp.dtype(a.dtype).itemsize


def _mm_fwd_cols(name, a, w3, out_dtype=F32, rider=None):
    t, k = a.shape
    ns = w3.shape[2]
    tm = _mm_tile(t, 2 * k * _size(a) + 2 * ns * jnp.dtype(out_dtype).itemsize, 2 * k * ns * _size(w3))
    return _mm(name, a, w3, mode="nn", grid=(N_CHIPS, t // tm, 1),
               a_spec=pl.BlockSpec((tm, k), lambda j, i, kk: (i, 0)),
               b_spec=pl.BlockSpec((None, k, ns), lambda j, i, kk: (j, 0, 0)),
               o_spec=pl.BlockSpec((tm, ns), lambda j, i, kk: (i, j)),
               out_shape=jax.ShapeDtypeStruct((t, N_CHIPS * ns), out_dtype), acc_shape=(tm, ns), rider=rider)


def _mm_bwd_cols(name, d, w3, out_dtype=F32, rider=None, epilogue=None):
    pair = isinstance(d, tuple)
    t = d[0].shape[0] if pair else d.shape[0]
    k, ns = w3.shape[1], w3.shape[2]
    dsize = _size(d[0] if pair else d)
    tm = _mm_tile(t, (4 if pair else 2) * ns * dsize + 2 * k * jnp.dtype(out_dtype).itemsize + 4 * k
                  + _row_epilogue(epilogue, 8)[1], 2 * k * ns * _size(w3))
    half = N_CHIPS // 2
    if pair:
        a_spec = (pl.BlockSpec((tm, ns), lambda i, j, kk: (i, jnp.minimum(kk, half - 1))),
                  pl.BlockSpec((tm, ns), lambda i, j, kk: (i, jnp.maximum(kk - half, 0))))
    else:
        a_spec = pl.BlockSpec((tm, ns), lambda i, j, kk: (i, kk))
    return _mm(name, d, w3, mode="nt", grid=(t // tm, 1, N_CHIPS), a_spec=a_spec,
               b_spec=pl.BlockSpec((None, k, ns), lambda i, j, kk: (kk, 0, 0)),
               o_spec=pl.BlockSpec((tm, k), lambda i, j, kk: (i, 0)),
               out_shape=jax.ShapeDtypeStruct((t, k), out_dtype), acc_shape=(tm, k), pair_axis=2 if pair else None,
               rider=rider, epilogue=_row_epilogue(epilogue, tm)[0])


def _mm_wgrad_cols(name, a, d, rider=None):
    pair = isinstance(d, tuple)
    t, k = a.shape
    ns = (2 * d[0].shape[1] if pair else d.shape[1]) // N_CHIPS
    dsize = _size(d[0] if pair else d)
    tk = _mm_tile(t, 2 * k * _size(a) + (4 if pair else 2) * ns * dsize, k * ns * (4 + 2 * 2))
    half = N_CHIPS // 2
    if pair:
        b_spec = (pl.BlockSpec((tk, ns), lambda j, i, kk: (jnp.where(j < half, kk, 0), jnp.minimum(j, half - 1))),
                  pl.BlockSpec((tk, ns), lambda j, i, kk: (jnp.where(j < half, 0, kk), jnp.maximum(j - half, 0))))
    else:
        b_spec = pl.BlockSpec((tk, ns), lambda j, i, kk: (kk, j))
    return _mm(name, a, d, mode="tn", grid=(N_CHIPS, 1, t // tk),
               a_spec=pl.BlockSpec((tk, k), lambda j, i, kk: (kk, 0)), b_spec=b_spec,
               o_spec=pl.BlockSpec((None, k, ns), lambda j, i, kk: (j, 0, 0)),
               out_shape=jax.ShapeDtypeStruct((N_CHIPS, k, ns), BF16), acc_shape=(k, ns),
               pair_axis=0 if pair else None, rider=rider)


MM_BLOCK_CAP = 1408


def _row_epilogue(epilogue, tm):
    if epilogue is None:
        return None, 0
    fn, arrays, sums = epilogue
    specs = [pl.BlockSpec((1, x.shape[1]), lambda i, j, kk: (0, 0)) if x.shape[0] == 1 else
             pl.BlockSpec((tm, x.shape[1]), lambda i, j, kk: (i, 0)) for x in arrays]
    return (fn, list(zip(arrays, specs)), list(sums)), sum(2 * x.shape[1] * _size(x) for x in arrays if x.shape[0] > 1)


def _mm_fwd_rows(name, a, w, res=None, out_dtype=F32, epilogue=None):
    t, k = a.shape
    n = w.shape[1]
    tk = k if k <= MM_BLOCK_CAP else MM_BLOCK_CAP
    assert k % tk == 0
    row_bytes = 2 * tk * _size(a) + 2 * n * jnp.dtype(out_dtype).itemsize + (0 if res is None else 2 * n * 4) + 4 * n
    row_bytes += _row_epilogue(epilogue, 8)[1]
    tm = _mm_tile(t, row_bytes, 2 * tk * n * _size(w))
    return _mm(name, a, w, mode="nn", grid=(t // tm, 1, k // tk),
               a_spec=pl.BlockSpec((tm, tk), lambda i, j, kk: (i, kk)),
               b_spec=pl.BlockSpec((tk, n), lambda i, j, kk: (kk, 0)),
               o_spec=pl.BlockSpec((tm, n), lambda i, j, kk: (i, 0)),
               out_shape=jax.ShapeDtypeStruct((t, n), out_dtype), acc_shape=(tm, n),
               res=res, res_spec=None if res is None else pl.BlockSpec((tm, n), lambda i, j, kk: (i, 0)),
               epilogue=_row_epilogue(epilogue, tm)[0])


def _mm_bwd_rows(name, d, w, out_dtype=F32):
    t, n = d.shape
    k = w.shape[0]
    tn = k if k <= MM_BLOCK_CAP else MM_BLOCK_CAP
    assert k % tn == 0
    tm = _mm_tile(t, 2 * n * _size(d) + 2 * tn * jnp.dtype(out_dtype).itemsize, 2 * tn * n * _size(w))
    return _mm(name, d, w, mode="nt", grid=(t // tm, k // tn, 1),
               a_spec=pl.BlockSpec((tm, n), lambda i, j, kk: (i, 0)),
               b_spec=pl.BlockSpec((tn, n), lambda i, j, kk: (j, 0)),
               o_spec=pl.BlockSpec((tm, tn), lambda i, j, kk: (i, j)),
               out_shape=jax.ShapeDtypeStruct((t, k), out_dtype), acc_shape=(tm, tn))


def _mm_wgrad_rows(name, a, d):
    t, k = a.shape
    n = d.shape[1]
    nblk = next(b for b in (1, 2, 4) if (k // b) % LANES == 0 and k // b <= MM_BLOCK_CAP)
    ks = k // nblk
    tk = _mm_tile(t, 2 * ks * _size(a) + 2 * n * _size(d), ks * n * (4 + 2 * 2))
    return _mm(name, a, d, mode="tn", grid=(nblk, 1, t // tk),
               a_spec=pl.BlockSpec((tk, ks), lambda j, i, kk: (kk, j)),
               b_spec=pl.BlockSpec((tk, n), lambda j, i, kk: (kk, 0)),
               o_spec=pl.BlockSpec((ks, n), lambda j, i, kk: (j, 0)),
               out_shape=jax.ShapeDtypeStruct((k, n), BF16), acc_shape=(ks, n))


def _s5_discretize(a_re, a_im, log_dt, b_re, b_im):
    dt = jnp.exp(log_dt)[:, None]
    mag = jnp.exp(a_re * dt)
    ang = a_im * dt
    lb_re = mag * jnp.cos(ang)
    lb_im = mag * jnp.sin(ang)
    den = a_re * a_re + a_im * a_im
    n_re = lb_re - 1.0
    n_im = lb_im
    co_re = ((n_re * a_re + n_im * a_im) / den)[..., None]
    co_im = ((n_im * a_re - n_re * a_im) / den)[..., None]
    bb_re = co_re * b_re - co_im * b_im
    bb_im = co_re * b_im + co_im * b_re
    return lb_re, lb_im, bb_re, bb_im


def _s5_in_blocks(bb):
    g = bb.shape[0]
    nb = g // S5_BLOCK_GROUPS
    t = bb.reshape(nb, S5_BLOCK_GROUPS, S5_STATE, S5_GROUP).transpose(0, 1, 3, 2)
    eye = jnp.eye(S5_BLOCK_GROUPS, dtype=bb.dtype)
    full = t[:, :, :, None, :] * eye[None, :, None, :, None]
    return full.reshape(nb, S5_BLOCK_GROUPS * S5_GROUP, S5_BLOCK_GROUPS * S5_STATE)


def _s5_in_blocks_diag(blocks):
    nb = blocks.shape[0]
    t = blocks.reshape(nb, S5_BLOCK_GROUPS, S5_GROUP, S5_BLOCK_GROUPS, S5_STATE)
    d = jnp.einsum("bghgp->bghp", t)
    return d.transpose(0, 1, 3, 2).reshape(nb * S5_BLOCK_GROUPS, S5_STATE, S5_GROUP)


def _s5_out_blocks(c):
    g = c.shape[0]
    nb = g // S5_BLOCK_GROUPS
    t = c.reshape(nb, S5_BLOCK_GROUPS, S5_GROUP, S5_STATE).transpose(0, 1, 3, 2)
    eye = jnp.eye(S5_BLOCK_GROUPS, dtype=c.dtype)
    full = t[:, :, :, None, :] * eye[None, :, None, :, None]
    return full.reshape(nb, S5_BLOCK_GROUPS * S5_STATE, S5_BLOCK_GROUPS * S5_GROUP)


def _s5_out_blocks_diag(blocks):
    nb = blocks.shape[0]
    t = blocks.reshape(nb, S5_BLOCK_GROUPS, S5_STATE, S5_BLOCK_GROUPS, S5_GROUP)
    d = jnp.einsum("bgpgh->bgph", t)
    return d.transpose(0, 1, 3, 2).reshape(nb * S5_BLOCK_GROUPS, S5_GROUP, S5_STATE)


def _s5_scan_tables(lr, li, reverse):
    def cmul(a, b):
        return a[0] * b[0] - a[1] * b[1], a[0] * b[1] + a[1] * b[0]

    lam = (lr, -li) if reverse else (lr, li)
    pw = [lam]
    for _ in range(SUBLANES - 1):
        pw.append(cmul(pw[-1], lam))
    sub = jnp.arange(SUBLANES)[:, None]
    rows = []
    for s in (1, 2, 4):
        keep = (sub <= SUBLANES - 1 - s) if reverse else (sub >= s)
        rows.append(jnp.where(keep, pw[s - 1][0][None, :], 0.0))
        rows.append(jnp.where(keep, pw[s - 1][1][None, :], 0.0))
    order = list(range(SUBLANES - 1, -1, -1)) if reverse else list(range(SUBLANES))
    rows.append(jnp.stack([pw[i][0] for i in order]))
    rows.append(jnp.stack([pw[i][1] for i in order]))
    return jnp.concatenate(rows, axis=0)


def _s5_scan(vre_ref, vim_ref, coef_ref, seq, width, reverse, xre_ref=None, xim_ref=None):
    nt = seq // SUBLANES
    nl = width // LANES
    per = 2 if xre_ref is None else 4
    sub = lax.broadcasted_iota(jnp.int32, (SUBLANES, LANES), 0)

    def step(k, carry):
        kk = (nt - 1 - k) if reverse else k
        rows = pl.ds(pl.multiple_of(kk * SUBLANES, SUBLANES), SUBLANES)
        out = []
        for j in range(nl):
            lanes = slice(j * LANES, (j + 1) * LANES)
            co = [coef_ref[SUBLANES * q:SUBLANES * (q + 1), lanes] for q in range(8)]
            cr, ci = carry[per * j], carry[per * j + 1]
            vr = vre_ref[rows, lanes]
            vi = vim_ref[rows, lanes]
            for q, s in enumerate((1, 2, 4)):
                sh = SUBLANES - s if reverse else s
                rr = pltpu.roll(vr, sh, 0)
                ri = pltpu.roll(vi, sh, 0)
                ar, ai = co[2 * q], co[2 * q + 1]
                vr, vi = vr + ar * rr - ai * ri, vi + ar * ri + ai * rr
            edge = 0 if reverse else SUBLANES - 1
            cbr = jnp.broadcast_to(cr[edge:edge + 1, :], (SUBLANES, LANES))
            cbi = jnp.broadcast_to(ci[edge:edge + 1, :], (SUBLANES, LANES))
            pr, pi = co[6], co[7]
            vr, vi = vr + pr * cbr - pi * cbi, vi + pr * cbi + pi * cbr
            vre_ref[rows, lanes] = vr
            vim_ref[rows, lanes] = vi
            out += [vr, vi]
            if xre_ref is not None:
                nr = jnp.where(sub == SUBLANES - 1, cbr, pltpu.roll(vr, SUBLANES - 1, 0))
                ni = jnp.where(sub == SUBLANES - 1, cbi, pltpu.roll(vi, SUBLANES - 1, 0))
                xr = xre_ref[rows, lanes]
                xi = xim_ref[rows, lanes]
                out += [carry[per * j + 2] + nr * xr + ni * xi, carry[per * j + 3] + ni * xr - nr * xi]
        return tuple(out)

    zero = jnp.zeros((SUBLANES, LANES), F32)
    res = lax.fori_loop(0, nt, step, (zero,) * (per * nl))
    if xre_ref is None:
        return None
    return jnp.concatenate(
        [jnp.concatenate([jnp.sum(res[per * j + 2], axis=0, keepdims=True) for j in range(nl)], axis=1),
         jnp.concatenate([jnp.sum(res[per * j + 3], axis=0, keepdims=True) for j in range(nl)], axis=1)], axis=0)


def _s5_fwd(z, bre3, bim3, cre3, cim3, coef, dskip, *, nseq, seq, rider=None):
    nb = bre3.shape[0]
    ch, ns = bre3.shape[1], bre3.shape[2]

    def body(za_ref, bre_ref, bim_ref, cre_ref, cim_ref, coef_ref, d_ref, y_ref, xre_ref, xim_ref):
        za = za_ref[...]
        xre_ref[...] = _dot(za, bre_ref[...])
        xim_ref[...] = _dot(za, bim_ref[...])
        _s5_scan(xre_ref, xim_ref, coef_ref, seq, ns, False)
        y_ref[...] = _dot(xre_ref[...], cre_ref[...]) - _dot(xim_ref[...], cim_ref[...]) + d_ref[...] * za

    blk3 = lambda r, c: pl.BlockSpec((None, r, c), lambda b, j: (j, 0, 0))
    return _hosted_call(
        "s5_fwd", body, grid=(nseq, nb),
        in_specs=[pl.BlockSpec((seq, ch), lambda b, j: (b, j)), blk3(ch, ns), blk3(ch, ns), blk3(ns, ch), blk3(ns, ch),
                  pl.BlockSpec((8 * SUBLANES, ns), lambda b, j: (0, j)), pl.BlockSpec((1, ch), lambda b, j: (0, j))],
        out_specs=[pl.BlockSpec((seq, ch), lambda b, j: (b, j)), pl.BlockSpec((seq, ns), lambda b, j: (b, j)),
                   pl.BlockSpec((seq, ns), lambda b, j: (b, j))],
        out_shape=[jax.ShapeDtypeStruct((nseq * seq, nb * ch), F32), jax.ShapeDtypeStruct((nseq * seq, nb * ns), F32),
                   jax.ShapeDtypeStruct((nseq * seq, nb * ns), F32)],
        operands=(z, bre3, bim3, cre3, cim3, coef, dskip), rider=rider)


def _s5_bwd(dy, z, xre, xim, bre3, bim3, cre3, cim3, coef_rev, dskip, *, nseq, seq, rider=None):
    nb = bre3.shape[0]
    ch, ns = bre3.shape[1], bre3.shape[2]

    def body(dy_ref, za_ref, xre_ref, xim_ref, bre_ref, bim_ref, cre_ref, cim_ref, coef_ref, d_ref,
             dza_ref, dbre_ref, dbim_ref, dcre_ref, dcim_ref, dlam_ref, dd_ref, are_ref, aim_ref):
        dy = dy_ref[...]
        za = za_ref[...]
        are_ref[...] = _dot_nt(dy, cre_ref[...])
        aim_ref[...] = -_dot_nt(dy, cim_ref[...])
        dlam = _s5_scan(are_ref, aim_ref, coef_ref, seq, ns, True, xre_ref, xim_ref)
        are = are_ref[...]
        aim = aim_ref[...]
        dza_ref[...] = (_dot_nt(are, bre_ref[...]) + _dot_nt(aim, bim_ref[...]) + d_ref[...] * dy).astype(dza_ref.dtype)
        parts = (_dot_tn(za, are), _dot_tn(za, aim), _dot_tn(xre_ref[...], dy), -_dot_tn(xim_ref[...], dy),
                 dlam, jnp.sum(dy * za, axis=0, keepdims=True))
        first = pl.program_id(1) == 0
        for r, v in zip((dbre_ref, dbim_ref, dcre_ref, dcim_ref, dlam_ref, dd_ref), parts):
            @pl.when(first)
            def _():
                r[...] = v

            @pl.when(jnp.logical_not(first))
            def _():
                r[...] += v

    blk3 = lambda r, c: pl.BlockSpec((None, r, c), lambda j, b: (j, 0, 0))
    tok = lambda c: pl.BlockSpec((seq, c), lambda j, b: (b, j))
    return _hosted_call(
        "s5_bwd", body, grid=(nb, nseq),
        in_specs=[tok(ch), tok(ch), tok(ns), tok(ns), blk3(ch, ns), blk3(ch, ns), blk3(ns, ch), blk3(ns, ch),
                  pl.BlockSpec((8 * SUBLANES, ns), lambda j, b: (0, j)), pl.BlockSpec((1, ch), lambda j, b: (0, j))],
        out_specs=[tok(ch), blk3(ch, ns), blk3(ch, ns), blk3(ns, ch), blk3(ns, ch),
                   pl.BlockSpec((None, 2, ns), lambda j, b: (j, 0, 0)), pl.BlockSpec((1, ch), lambda j, b: (0, j))],
        out_shape=[jax.ShapeDtypeStruct((nseq * seq, nb * ch), BF16),
                   jax.ShapeDtypeStruct((nb, ch, ns), F32), jax.ShapeDtypeStruct((nb, ch, ns), F32),
                   jax.ShapeDtypeStruct((nb, ns, ch), F32), jax.ShapeDtypeStruct((nb, ns, ch), F32),
                   jax.ShapeDtypeStruct((nb, 2, ns), F32), jax.ShapeDtypeStruct((1, nb * ch), F32)],
        scratch_shapes=[pltpu.VMEM((seq, ns), F32), pltpu.VMEM((seq, ns), F32)],
        operands=(dy, z, xre, xim, bre3, bim3, cre3, cim3, coef_rev, dskip), rider=rider)


def _glu_fwd(y, wglu, bglu):
    t, w = y.shape
    tm = _row_tile(t, 1024)

    def body(y_ref, w_ref, b_ref, a0_ref, gl_ref, a_ref):
        a0 = _gelu(y_ref[...])
        gl = _dot(a0, w_ref[...])
        a0_ref[...] = a0.astype(a0_ref.dtype)
        gl_ref[...] = gl
        a_ref[...] = (a0 * _sigmoid(gl + b_ref[...])).astype(a_ref.dtype)

    tok = pl.BlockSpec((tm, w), lambda i: (i, 0))
    return pl.pallas_call(
        body, name="s5_glu", grid=(t // tm,),
        in_specs=[tok, pl.BlockSpec((w, w), lambda i: (0, 0)), pl.BlockSpec((1, w), lambda i: (0, 0))],
        out_specs=[tok, tok, tok],
        out_shape=[jax.ShapeDtypeStruct((t, w), BF16), jax.ShapeDtypeStruct((t, w), F32), jax.ShapeDtypeStruct((t, w), BF16)],
        compiler_params=_params("arbitrary"),
    )(y, wglu, bglu)


def _glu_bwd(y, gl, bglu, da, wglu):
    t, w = y.shape
    tm = _row_tile(t, 1024)

    def body(y_ref, gl_ref, b_ref, da_ref, w_ref, dgl_ref, dy_ref, db_ref):
        yv = y_ref[...]
        dav = da_ref[...]
        s = _sigmoid(gl_ref[...] + b_ref[...])
        dgl = dav * _gelu(yv) * s * (1.0 - s)
        dgl_ref[...] = dgl.astype(dgl_ref.dtype)
        dy_ref[...] = (dav * s + _dot_nt(dgl, w_ref[...])) * _gelu_grad(yv)
        part = jnp.sum(dgl, axis=0, keepdims=True)
        first = pl.program_id(0) == 0

        @pl.when(first)
        def _():
            db_ref[...] = part

        @pl.when(jnp.logical_not(first))
        def _():
            db_ref[...] += part

    tok = pl.BlockSpec((tm, w), lambda i: (i, 0))
    vec = pl.BlockSpec((1, w), lambda i: (0, 0))
    return pl.pallas_call(
        body, name="s5_glu_bwd", grid=(t // tm,),
        in_specs=[tok, tok, vec, tok, pl.BlockSpec((w, w), lambda i: (0, 0))], out_specs=[tok, tok, vec],
        out_shape=[jax.ShapeDtypeStruct((t, w), BF16), jax.ShapeDtypeStruct((t, w), F32), jax.ShapeDtypeStruct((1, w), F32)],
        compiler_params=_params("arbitrary"),
    )(y, gl, bglu, da, wglu)


def _cumsum_rows(x, reverse=False):
    n = x.shape[0]
    row = lax.broadcasted_iota(jnp.int32, x.shape, 0)
    s = 1
    while s < n:
        if reverse:
            x = x + jnp.where(row < n - s, pltpu.roll(x, n - s, 0), 0.0)
        else:
            x = x + jnp.where(row >= s, pltpu.roll(x, s, 0), 0.0)
        s *= 2
    return x


def _hg_gates(zq, zf, lb):
    sg = _sigmoid(zf)
    f = lb + (1.0 - lb) * sg
    sq = _sigmoid(zq)
    qa = zq * sq * (HEAD ** -0.5)
    b = _cumsum_rows(jnp.log(f))
    return sg, f, sq, qa, 1.0 - f, b


SUB = 16


def _hg_scores(qa, kk, b):
    c = qa.shape[0]
    row = lax.broadcasted_iota(jnp.int32, qa.shape, 0)
    pos = jnp.bitwise_and(row, SUB - 1)
    dmat = lax.broadcasted_iota(jnp.int32, (c, c), 0) - lax.broadcasted_iota(jnp.int32, (c, c), 1)
    p = jnp.zeros((c, c), F32)
    for d in range(SUB):
        if d == 0:
            fd = qa * kk
        else:
            e = jnp.exp(jnp.minimum(b - pltpu.roll(b, d, 0), 0.0))
            fd = jnp.where(pos >= d, qa * pltpu.roll(kk, d, 0) * e, 0.0)
        p = jnp.where(dmat == d, jnp.sum(fd, axis=1, keepdims=True), p)
    col = lax.broadcasted_iota(jnp.int32, (SUB, c), 1)
    blocks = [jnp.zeros((SUB, c), F32)]
    for r0 in range(SUB, c, SUB):
        beta = b[r0 - 1:r0, :]
        qt = qa[r0:r0 + SUB] * jnp.exp(b[r0:r0 + SUB] - beta)
        kt = kk * jnp.exp(jnp.minimum(beta - b, 0.0))
        blocks.append(jnp.where(col < r0, _dot_nt(qt, kt), 0.0))
    return p + jnp.concatenate(blocks, axis=0)


def _hg_scores_bwd(dp, qa, kk, b):
    c = qa.shape[0]
    row = lax.broadcasted_iota(jnp.int32, qa.shape, 0)
    pos = jnp.bitwise_and(row, SUB - 1)
    dmat = lax.broadcasted_iota(jnp.int32, (c, c), 0) - lax.broadcasted_iota(jnp.int32, (c, c), 1)
    dqa = jnp.zeros_like(qa)
    dkk = jnp.zeros_like(qa)
    db = jnp.zeros_like(qa)
    for d in range(SUB):
        dcol = jnp.sum(jnp.where(dmat == d, dp, 0.0), axis=1, keepdims=True)
        if d == 0:
            dqa = dqa + dcol * kk
            dkk = dkk + dcol * qa
        else:
            e = jnp.exp(jnp.minimum(b - pltpu.roll(b, d, 0), 0.0))
            w = jnp.where(pos >= d, dcol * e, 0.0)
            kr = pltpu.roll(kk, d, 0)
            dqa = dqa + w * kr
            tmp = w * qa
            dkk = dkk + pltpu.roll(tmp, c - d, 0)
            x = tmp * kr
            db = db + x - pltpu.roll(x, c - d, 0)
    col = lax.broadcasted_iota(jnp.int32, (SUB, c), 1)
    dq_blocks = [jnp.zeros((SUB, qa.shape[1]), F32)]
    db_blocks = [jnp.zeros((SUB, qa.shape[1]), F32)]
    for r0 in range(SUB, c, SUB):
        beta = b[r0 - 1:r0, :]
        eq = jnp.exp(b[r0:r0 + SUB] - beta)
        ek = jnp.exp(jnp.minimum(beta - b, 0.0))
        qt = qa[r0:r0 + SUB] * eq
        kt = kk * ek
        dpi = jnp.where(col < r0, dp[r0:r0 + SUB, :], 0.0)
        dqt = _dot(dpi, kt)
        dkt = _dot_tn(dpi, qt)
        dq_blocks.append(dqt * eq)
        db_blocks.append(dqt * qt)
        dkk = dkk + dkt * ek
        db = db - dkt * kt
    return dqa + jnp.concatenate(dq_blocks, axis=0), dkk, db + jnp.concatenate(db_blocks, axis=0)


def _hg_chunks_per_step(seq):
    nc = seq // CHUNK
    cps = next(k for k in (4, 2, 1) if nc % k == 0)
    return nc, cps, nc // cps


def _hg_fwd(z, lbrow, gain, *, nseq, seq, heads, qoff, rider=None):
    nc, cps, nblk = _hg_chunks_per_step(seq)
    blk = cps * CHUNK
    zspec = lambda off: pl.BlockSpec((blk, HEAD), lambda h, b, n, off=off: (b * nblk + n, off + h))

    def body(zq_ref, zf_ref, zi_ref, zg_ref, lb_ref, gn_ref, o_ref, yb_ref, st_ref, sc_ref, state):
        @pl.when(pl.program_id(2) == 0)
        def _():
            state[...] = jnp.zeros_like(state)

        lb = lb_ref[...]
        gain_v = gn_ref[...]

        def chunk(ci, carry):
            rows = pl.ds(pl.multiple_of(ci * CHUNK, CHUNK), CHUNK)
            st = state[...]
            st_ref[ci] = st
            zi = zi_ref[rows, :]
            zg = zg_ref[rows, :]
            _, _, _, qa, kk, b = _hg_gates(zq_ref[rows, :], zf_ref[rows, :], lb)
            scores = _hg_scores(qa, kk, b).astype(BF16)
            sc_ref[rows, :] = scores
            o = _dot_nt(qa * jnp.exp(b), st) + _dot(scores, zi)
            bl = b[CHUNK - 1:CHUNK, :]
            state[...] = st * jnp.exp(bl) + _dot_tn(zi, kk * jnp.exp(bl - b))
            o_ref[rows, :] = o
            r = lax.rsqrt(jnp.mean(o * o, axis=1, keepdims=True) + EPS)
            yb_ref[rows, :] = (o * r * gain_v * zg * _sigmoid(zg)).astype(yb_ref.dtype)
            return carry

        lax.fori_loop(0, cps, chunk, 0, unroll=True)

    tok = pl.BlockSpec((blk, HEAD), lambda h, b, n: (b * nblk + n, h))
    vec = pl.BlockSpec((1, HEAD), lambda h, b, n: (0, h))
    rows = nseq * seq
    return _hosted_call(
        "hgrn2_fwd", body, grid=(heads, nseq, nblk),
        in_specs=[zspec(qoff), zspec(qoff + heads), zspec(qoff + 2 * heads), zspec(qoff + 3 * heads), vec, vec],
        out_specs=[tok, tok, pl.BlockSpec((None, None, cps, HEAD, HEAD), lambda h, b, n: (h, b, n, 0, 0)),
                   pl.BlockSpec((None, blk, CHUNK), lambda h, b, n: (h, b * nblk + n, 0))],
        out_shape=[jax.ShapeDtypeStruct((rows, heads * HEAD), F32), jax.ShapeDtypeStruct((rows, heads * HEAD), BF16),
                   jax.ShapeDtypeStruct((heads, nseq, nc, HEAD, HEAD), F32),
                   jax.ShapeDtypeStruct((heads, rows, CHUNK), BF16)],
        scratch_shapes=[pltpu.VMEM((HEAD, HEAD), F32)], operands=(z, z, z, z, lbrow, gain), rider=rider)


def _hg_bwd(dyb, z, o, states, scores, lbrow, gain, *, nseq, seq, heads, qoff, rider=None):
    nc, cps, nblk = _hg_chunks_per_step(seq)
    blk = cps * CHUNK
    rev = lambda n: nblk - 1 - n
    zspec = lambda off: pl.BlockSpec((blk, HEAD), lambda h, b, n, off=off: (b * nblk + rev(n), off + h))

    def body(dyb_ref, zq_ref, zf_ref, zi_ref, zg_ref, o_ref, st_ref, sc_ref, lb_ref, gn_ref,
             dzq_ref, dzf_ref, dzi_ref, dzg_ref, dlb_ref, dgn_ref, dstate):
        @pl.when(pl.program_id(2) == 0)
        def _():
            dstate[...] = jnp.zeros_like(dstate)

        @pl.when(jnp.logical_and(pl.program_id(1) == 0, pl.program_id(2) == 0))
        def _():
            dlb_ref[...] = jnp.zeros_like(dlb_ref)
            dgn_ref[...] = jnp.zeros_like(dgn_ref)

        lb = lb_ref[...]
        gain_v = gn_ref[...]
        c = CHUNK
        causal = lax.broadcasted_iota(jnp.int32, (c, c), 0) >= lax.broadcasted_iota(jnp.int32, (c, c), 1)

        def chunk(step, carry):
            ci = cps - 1 - step
            rows = pl.ds(pl.multiple_of(ci * CHUNK, CHUNK), CHUNK)
            zq = zq_ref[rows, :]
            zi = zi_ref[rows, :]
            zg = zg_ref[rows, :]
            sg, f, sq, qa, kk, b = _hg_gates(zq, zf_ref[rows, :], lb)
            eb = jnp.exp(b)
            qt = qa * eb
            bl = b[c - 1:c, :]
            ebl = jnp.exp(bl)
            ekb = jnp.exp(bl - b)
            kh = kk * ekb
            st = st_ref[ci]
            dst = dstate[...]
            o = o_ref[rows, :]
            r = lax.rsqrt(jnp.mean(o * o, axis=1, keepdims=True) + EPS)
            oh = o * r
            szg = _sigmoid(zg)
            dyb = dyb_ref[rows, :]
            don = dyb * zg * szg
            dzg_ref[rows, :] = (dyb * oh * gain_v * szg * (1.0 + zg * (1.0 - szg))).astype(dzg_ref.dtype)
            doh = don * gain_v
            do = r * (doh - oh * jnp.mean(doh * oh, axis=1, keepdims=True))
            dqt = _dot(do, st)
            dp = jnp.where(causal, _dot_nt(do, zi), 0.0)
            dzi_ref[rows, :] = (_dot_tn(sc_ref[rows, :], do) + _dot_nt(kh, dst)).astype(dzi_ref.dtype)
            dkh = _dot(zi, dst)
            dbl = jnp.sum(dkh * kh, axis=0, keepdims=True) + jnp.sum(dst * st, axis=0, keepdims=True) * ebl
            dstate[...] = _dot_tn(do, qt) + dst * ebl
            dqa_s, dkk_s, db_s = _hg_scores_bwd(dp, qa, kk, b)
            dqa = dqt * eb + dqa_s
            dkk = dkh * ekb + dkk_s
            db = dqt * qt - dkh * kh + db_s
            row = lax.broadcasted_iota(jnp.int32, db.shape, 0)
            db = db + jnp.where(row == c - 1, dbl, 0.0)
            df = _cumsum_rows(db, reverse=True) / f - dkk
            dzf_ref[rows, :] = (df * (1.0 - lb) * sg * (1.0 - sg)).astype(dzf_ref.dtype)
            dzq_ref[rows, :] = (dqa * (HEAD ** -0.5) * sq * (1.0 + zq * (1.0 - sq))).astype(dzq_ref.dtype)
            dlb_ref[...] += jnp.sum(df * (1.0 - sg), axis=0, keepdims=True)
            dgn_ref[...] += jnp.sum(don * oh, axis=0, keepdims=True)
            return carry

        lax.fori_loop(0, cps, chunk, 0, unroll=2)

    tok = pl.BlockSpec((blk, HEAD), lambda h, b, n: (b * nblk + rev(n), h))
    vec = pl.BlockSpec((1, HEAD), lambda h, b, n: (0, h))
    rows = nseq * seq
    return _hosted_call(
        "hgrn2_bwd", body, grid=(heads, nseq, nblk),
        in_specs=[tok, zspec(qoff), zspec(qoff + heads), zspec(qoff + 2 * heads), zspec(qoff + 3 * heads), tok,
                  pl.BlockSpec((None, None, cps, HEAD, HEAD), lambda h, b, n: (h, b, rev(n), 0, 0)),
                  pl.BlockSpec((None, blk, CHUNK), lambda h, b, n: (h, b * nblk + rev(n), 0)), vec, vec],
        out_specs=[tok, tok, tok, tok, vec, vec],
        out_shape=[jax.ShapeDtypeStruct((rows, heads * HEAD), BF16)] * 4
        + [jax.ShapeDtypeStruct((1, heads * HEAD), F32)] * 2,
        scratch_shapes=[pltpu.VMEM((HEAD, HEAD), F32)],
        operands=(dyb, z, z, z, z, o, states, scores, lbrow, gain), rider=rider)


def _conv_taps(h, w, bias):
    row = lax.broadcasted_iota(jnp.int32, h.shape, 0)
    h1 = jnp.where(row >= 1, pltpu.roll(h, 1, 0), 0.0)
    h2 = jnp.where(row >= 2, pltpu.roll(h, 2, 0), 0.0)
    return h2 * w[0:1, :] + h1 * w[1:2, :] + h * w[2:3, :] + bias, h1, h2


def _conv_fwd(h, wconv, bconv, *, nseq, seq):
    ff2 = h.shape[1]
    ncol = ff2 // 2 // LANES

    def body(hg_ref, hv_ref, wg_ref, wv_ref, bg_ref, bv_ref, a_ref):
        g, _, _ = _conv_taps(hg_ref[...].astype(F32), wg_ref[...], bg_ref[...])
        v, _, _ = _conv_taps(hv_ref[...].astype(F32), wv_ref[...], bv_ref[...])
        a_ref[...] = (g * _sigmoid(g) * v).astype(a_ref.dtype)

    tok = lambda off: pl.BlockSpec((seq, LANES), lambda j, b, off=off: (b, off + j))
    wsp = lambda off: pl.BlockSpec((CONV_W, LANES), lambda j, b, off=off: (0, off + j))
    bsp = lambda off: pl.BlockSpec((1, LANES), lambda j, b, off=off: (0, off + j))
    return pl.pallas_call(
        body, name="conv_fwd", grid=(ncol, nseq),
        in_specs=[tok(0), tok(ncol), wsp(0), wsp(ncol), bsp(0), bsp(ncol)],
        out_specs=tok(0), out_shape=jax.ShapeDtypeStruct((nseq * seq, ff2 // 2), BF16),
        compiler_params=_params("arbitrary", "arbitrary"),
    )(h, h, wconv, wconv, bconv, bconv)


def _conv_bwd(da, h, wconv, bconv, *, nseq, seq):
    ff2 = h.shape[1]
    ncol = ff2 // 2 // LANES

    def half_bwd(d, hcur, h1, h2, w):
        n = d.shape[0]
        row = lax.broadcasted_iota(jnp.int32, d.shape, 0)
        d1 = jnp.where(row < n - 1, pltpu.roll(d, n - 1, 0), 0.0)
        d2 = jnp.where(row < n - 2, pltpu.roll(d, n - 2, 0), 0.0)
        dh = d * w[2:3, :] + d1 * w[1:2, :] + d2 * w[0:1, :]
        stats = jnp.concatenate(
            [jnp.sum(h2 * d, axis=0, keepdims=True), jnp.sum(h1 * d, axis=0, keepdims=True),
             jnp.sum(hcur * d, axis=0, keepdims=True), jnp.sum(d, axis=0, keepdims=True),
             jnp.zeros((SUBLANES - 4, d.shape[1]), F32)], axis=0)
        return dh, stats

    def body(da_ref, hg_ref, hv_ref, wg_ref, wv_ref, bg_ref, bv_ref, dhg_ref, dhv_ref, sg_ref, sv_ref):
        hg = hg_ref[...].astype(F32)
        hv = hv_ref[...].astype(F32)
        wg = wg_ref[...]
        wv = wv_ref[...]
        g, g1, g2 = _conv_taps(hg, wg, bg_ref[...])
        v, v1, v2 = _conv_taps(hv, wv, bv_ref[...])
        da = da_ref[...].astype(F32)
        s = _sigmoid(g)
        dhg, stg = half_bwd(da * v * s * (1.0 + g * (1.0 - s)), hg, g1, g2, wg)
        dhv, stv = half_bwd(da * g * s, hv, v1, v2, wv)
        dhg_ref[...] = dhg.astype(dhg_ref.dtype)
        dhv_ref[...] = dhv.astype(dhv_ref.dtype)
        first = pl.program_id(1) == 0
        for r, val in ((sg_ref, stg), (sv_ref, stv)):
            @pl.when(first)
            def _():
                r[...] = val

            @pl.when(jnp.logical_not(first))
            def _():
                r[...] += val

    tok = lambda off: pl.BlockSpec((seq, LANES), lambda j, b, off=off: (b, off + j))
    wsp = lambda off: pl.BlockSpec((CONV_W, LANES), lambda j, b, off=off: (0, off + j))
    bsp = lambda off: pl.BlockSpec((1, LANES), lambda j, b, off=off: (0, off + j))
    ssp = pl.BlockSpec((SUBLANES, LANES), lambda j, b: (0, j))
    dhg, dhv, stg, stv = pl.pallas_call(
        body, name="conv_bwd", grid=(ncol, nseq),
        in_specs=[tok(0), tok(0), tok(ncol), wsp(0), wsp(ncol), bsp(0), bsp(ncol)],
        out_specs=[tok(0), tok(0), ssp, ssp],
        out_shape=[jax.ShapeDtypeStruct((nseq * seq, ff2 // 2), BF16)] * 2
        + [jax.ShapeDtypeStruct((SUBLANES, ff2 // 2), F32)] * 2,
        compiler_params=_params("arbitrary", "arbitrary"),
    )(da, h, h, wconv, wconv, bconv, bconv)
    return (dhg, dhv), jnp.concatenate([stg, stv], axis=1)


def _rms_fwd(xv, g):
    r = lax.rsqrt(jnp.mean(xv * xv, axis=1, keepdims=True) + EPS)
    return (xv * r * g,)


def _rms_bwd(xv, g, dy, res):
    r = lax.rsqrt(jnp.mean(xv * xv, axis=1, keepdims=True) + EPS)
    xh = xv * r
    dxh = dy * g
    dx = r * (dxh - xh * jnp.mean(dxh * xh, axis=1, keepdims=True)) + res
    return dx, jnp.sum(dy * xh, axis=0, keepdims=True)


def _loss_head(x2, tgt, g):
    d = x2.shape[1]
    r = lax.rsqrt(jnp.mean(x2 * x2, axis=1, keepdims=True) + EPS)
    xh = x2 * r
    err = xh * g - tgt
    dy = err * (1.0 / d)
    dxh = dy * g
    dx = r * (dxh - xh * jnp.mean(dxh * xh, axis=1, keepdims=True))
    loss = 0.5 * jnp.sum(jnp.mean(err * err, axis=1, keepdims=True), axis=0, keepdims=True)
    return dx, jnp.sum(dy * xh, axis=0, keepdims=True), jnp.broadcast_to(loss, (1, LANES))


LATE_A = ("w_down", "w_out")
LATE_B = ("w_up", "w_pa", "w_pb")
LATE = LATE_A + LATE_B
EARLY_GRADS = ("w_down", "w_up", "w_out", "w_pa", "w_pb", "w_glu")
ROW_SHARDED = ("w_glu", "w_out", "w_down")


def _local_step(x, tgt, p, late, *, nseq, seq):
    p = dict(p)
    chip = 2 * lax.axis_index("x") + lax.axis_index("y")
    t, d = x.shape
    s5w = p["s5_d"].shape[1]
    hgw = p["gain"].shape[1]
    heads = hgw // HEAD
    qoff = s5w // LANES
    gblk = (s5w + 4 * hgw) // GATE_BLOCK
    ngb = d // GATE_BLOCK
    tm = _row_tile(t, 256)
    row = lambda a, w=None, base=0: (a, a.shape[1] if w is None else w, base, "row")
    vec = lambda a, w=None, base=0: (a, a.shape[1] if w is None else w, base, "vec")
    rw = functools.partial(_rowwise, rows=t, tm=tm)

    (u,) = rw("rms_mix", _rms_fwd, [row(x), vec(p["g_mix"])], [(d, d, BF16)])
    z, landed_a = _mm_fwd_cols("in_proj", u, p["w_in"], rider=_gather_ici_rider([late[n] for n in LATE_A]))

    lam_re, lam_im, bb_re, bb_im = _s5_discretize(p["s5_a_re"], p["s5_a_im"], p["s5_log_dt"], p["s5_b_re"], p["s5_b_im"])
    bre3 = _s5_in_blocks(bb_re).astype(BF16)
    bim3 = _s5_in_blocks(bb_im).astype(BF16)
    cre3 = _s5_out_blocks(p["s5_c_re"]).astype(BF16)
    cim3 = _s5_out_blocks(p["s5_c_im"]).astype(BF16)
    coef_f = _s5_scan_tables(lam_re.reshape(-1), lam_im.reshape(-1), False)
    coef_r = _s5_scan_tables(lam_re.reshape(-1), lam_im.reshape(-1), True)
    (o, yb, states, scores), landed_b = _hg_fwd(z, p["lbrow"], p["gain"], nseq=nseq, seq=seq, heads=heads, qoff=qoff,
                                                rider=_gather_ici_rider([late[n] for n in LATE_B]))
    (y5, xre, xim), gathered = _s5_fwd(z, bre3, bim3, cre3, cim3, coef_f, p["s5_d"], nseq=nseq, seq=seq,
                                       rider=_gather_pass_rider(list(landed_a) + list(landed_b)))
    for n, g in zip(LATE, gathered):
        full = lax.dynamic_update_index_in_dim(g, late[n], chip, 0)
        p[n] = full.reshape(-1, full.shape[-1]) if n in ROW_SHARDED else full
    ya0, gl, ya = _glu_fwd(y5, p["w_glu"], p["b_glu"])

    joined = lambda w3: w3.transpose(1, 0, 2).reshape(w3.shape[1], -1)
    split = lambda g: g.reshape(g.shape[0], N_CHIPS, -1).transpose(1, 0, 2)
    wpa, wpb = joined(p["w_pa"]), joined(p["w_pb"])
    pa = _mm_fwd_rows("proj_a", ya, wpa, out_dtype=BF16)
    pb = _mm_fwd_rows("proj_b", yb, wpb, out_dtype=BF16)
    gb = GATE_BLOCK
    (m,) = rw("merge", lambda ga, gbv, a, b: (_sigmoid(ga) * a + _sigmoid(gbv) * b,),
              [row(z, gb, gblk), row(z, gb, gblk + ngb), row(pa, gb), row(pb, gb)], [(d, gb, BF16)], ncol=ngb)
    x1 = _mm_fwd_rows("out_proj", m, p["w_out"], res=x)

    (u2,) = rw("rms_ffn", _rms_fwd, [row(x1), vec(p["g_ffn"])], [(d, d, BF16)])
    h = _mm_fwd_cols("up_proj", u2, p["w_up"], out_dtype=BF16)
    a = _conv_fwd(h, p["w_conv"], p["b_conv"], nseq=nseq, seq=seq)
    dx2, dg_final, lossv = _mm_fwd_rows("down_proj", a, p["w_down"], res=x1,
                                        epilogue=(_loss_head, [tgt, p["g_final"]], [d, LANES]))

    norm_bwd = lambda dyv, xv, g, resv: _rms_bwd(xv, g, dyv, resv)
    da = _mm_bwd_rows("down_bwd", dx2, p["w_down"], out_dtype=BF16)
    g_wdown = _mm_wgrad_rows("down_wgrad", a, dx2)
    dh, cstats = _conv_bwd(da, h, p["w_conv"], p["b_conv"], nseq=nseq, seq=seq)
    dx1, dg_ffn = _mm_bwd_cols("up_bwd", dh, p["w_up"], epilogue=(norm_bwd, [x1, p["g_ffn"], dx2], [d]))
    g_wup = _mm_wgrad_cols("up_wgrad", u2, dh)

    dm = _mm_bwd_rows("out_bwd", dx1, p["w_out"], out_dtype=BF16)
    g_wout = _mm_wgrad_rows("out_wgrad", m, dx1)

    def merge_bwd(ga, gbv, av, bv, dmv):
        sa = _sigmoid(ga)
        sb = _sigmoid(gbv)
        return dmv * sa, dmv * sb, dmv * av * sa * (1.0 - sa), dmv * bv * sb * (1.0 - sb)

    dpa, dpb, dzga, dzgb = rw("merge_bwd", merge_bwd,
                              [row(z, gb, gblk), row(z, gb, gblk + ngb), row(pa, gb), row(pb, gb), row(dm, gb)],
                              [(d, gb, BF16)] * 4, ncol=ngb)
    dya = _mm_bwd_rows("proj_a_bwd", dpa, wpa)
    g_wpa = split(_mm_wgrad_rows("proj_a_wgrad", ya, dpa))
    dyb = _mm_bwd_rows("proj_b_bwd", dpb, wpb)
    g_wpb = split(_mm_wgrad_rows("proj_b_wgrad", yb, dpb))

    dgl, dy5, db_glu = _glu_bwd(y5, gl, p["b_glu"], dya, p["w_glu"])
    g_wglu = _mm_wgrad_rows("glu_wgrad", ya0, dgl)
    partial = dict(w_down=g_wdown, w_up=g_wup, w_out=g_wout, w_pa=g_wpa, w_pb=g_wpb, w_glu=g_wglu)
    parts = [_grad_parts(partial[n]) for n in EARLY_GRADS]
    (dza, dbre3, dbim3, dcre3, dcim3, dlam, dd), sib = _s5_bwd(
        dy5, z, xre, xim, bre3, bim3, cre3, cim3, coef_r, p["s5_d"], nseq=nseq, seq=seq, rider=_swap_halves_rider(parts))
    pair = _pair_sums(EARLY_GRADS, parts, sib)
    (dzq, dzf, dzi, dzg, dlb, dgain), others = _hg_bwd(
        dyb, z, o, states, scores, p["lbrow"], p["gain"], nseq=nseq, seq=seq, heads=heads, qoff=qoff,
        rider=_scatter_rider(pair))
    halves = _chip_sums(EARLY_GRADS, pair, others)

    dz = jnp.concatenate([dza, dzq, dzf, dzi, dzg, dzga, dzgb], axis=1)
    (dx, dg_mix), sibs = _mm_bwd_cols("in_bwd", dz, p["w_in"], rider=_swap_sums_rider(halves),
                                      epilogue=(norm_bwd, [x, p["g_mix"], dx1], [d]))
    big = dict(zip(EARLY_GRADS, zip(halves, sibs)))

    gshape = lam_re.shape
    small = {
        "loss": lossv, "g_mix": dg_mix, "g_ffn": dg_ffn, "g_final": dg_final, "b_glu": db_glu, "gain": dgain,
        "lbrow": dlb, "s5_d": dd, "w_conv": cstats[0:CONV_W], "b_conv": cstats[CONV_W:CONV_W + 1],
        "lam_re": dlam[:, 0, :].reshape(gshape), "lam_im": dlam[:, 1, :].reshape(gshape),
        "bb_re": _s5_in_blocks_diag(dbre3), "bb_im": _s5_in_blocks_diag(dbim3),
        "s5_c_re": _s5_out_blocks_diag(dcre3), "s5_c_im": _s5_out_blocks_diag(dcim3),
    }
    small_vec = _pack([small[n] for n in SMALL_PARTS], F32)
    g_win, (small_all,) = _mm_wgrad_cols("in_wgrad", u, dz, rider=_gather_all_rider(small_vec))
    small_sum = _sum_over_devices("small_grad_sum", small_vec, small_all)
    sm = dict(zip(SMALL_PARTS, _unpack(small_sum, [small[n].shape for n in SMALL_PARTS])))
    last = [_grad_parts(g_win)]
    pair = _pair_sums(("w_in",), last, _run_rider("grad_swap_halves", _swap_halves_rider(last)))
    (half,) = _chip_sums(("w_in",), pair, _run_rider("grad_scatter_chips", _scatter_rider(pair)))
    mid = half.shape[0] // 2
    sib_half = jnp.concatenate(_run_rider("grad_swap_sums", _swap_sums_rider([half[:mid], half[mid:]])), axis=0)
    big["w_in"] = (half, sib_half)
    return dx, big, sm


ANY = pl.BlockSpec(memory_space=pl.ANY)


def _place():
    x, y, c = lax.axis_index("x"), lax.axis_index("y"), lax.axis_index("c")
    chips = [(1 - x, y), (x, 1 - y), (1 - x, 1 - y)]
    return x, y, c, chips


def _remote(src, dst, send_sems, recv_sems, k, to):
    return pltpu.make_async_remote_copy(src_ref=src, dst_ref=dst, send_sem=send_sems.at[k], recv_sem=recv_sems.at[k],
                                        device_id=to, device_id_type=MESH)


def _half(rows, which):
    return pl.ds(pl.multiple_of(which * (rows // 2), 16), rows // 2)


def _gather_weights(shards, whole):
    n, nw = len(shards), len(whole)
    arrays = list(shards) + list(whole)

    def body(*refs):
        in_refs, out_refs = refs[:n + nw], refs[n + nw:2 * (n + nw)]
        send_sems, recv_sems = refs[2 * (n + nw):]
        x, y, c, chips = _place()
        me = 2 * x + y
        copy = functools.partial(_remote, send_sems=send_sems, recv_sems=recv_sems)
        sends = []
        for a in range(n):
            mine_half = _half(arrays[a].shape[0], c)
            for j, (cx, cy) in enumerate(chips):
                sends.append(copy(in_refs[a].at[mine_half], out_refs[a].at[me, mine_half], k=6 * a + j, to=(cx, cy, c)))
        for a in range(n, n + nw):
            for j, (cx, cy) in enumerate(chips):
                sends.append(copy(in_refs[a], out_refs[a].at[me], k=6 * n + 3 * (a - n) + j, to=(cx, cy, c)))
        for cp in sends:
            cp.start()
        for a in range(n):
            mine_half = _half(arrays[a].shape[0], c)
            for j, (cx, cy) in enumerate(chips):
                landed = out_refs[a].at[2 * cx + cy, mine_half]
                copy(landed, landed, k=6 * a + j, to=(x, y, c)).wait_recv()
                fwd = copy(landed, landed, k=6 * a + 3 + j, to=(x, y, 1 - c))
                fwd.start()
                sends.append(fwd)
        for a in range(n):
            other_half = _half(arrays[a].shape[0], 1 - c)
            for j, (cx, cy) in enumerate(chips):
                landed = out_refs[a].at[2 * cx + cy, other_half]
                copy(landed, landed, k=6 * a + 3 + j, to=(x, y, c)).wait_recv()
        for a in range(n, n + nw):
            for j, (cx, cy) in enumerate(chips):
                landed = out_refs[a].at[2 * cx + cy]
                copy(landed, landed, k=6 * n + 3 * (a - n) + j, to=(x, y, c)).wait_recv()
        for cp in sends:
            cp.wait_send()

    nsem = 6 * n + 3 * nw
    return pl.pallas_call(
        body, name="gather_weights", out_shape=[jax.ShapeDtypeStruct((N_CHIPS,) + a.shape, a.dtype) for a in arrays],
        in_specs=[ANY] * (n + nw), out_specs=[ANY] * (n + nw),
        scratch_shapes=[pltpu.SemaphoreType.DMA((nsem,)), pltpu.SemaphoreType.DMA((nsem,))],
    )(*arrays)


def _symmetric_rider(arrays, out_shapes, copies_of, nsem):
    def start(ins, outs, send_sems, recv_sems):
        for cp in copies_of(ins, outs, send_sems, recv_sems):
            cp.start()

    def finish(ins, outs, send_sems, recv_sems):
        for cp in copies_of(ins, outs, send_sems, recv_sems):
            cp.wait()

    return _Rider(arrays, out_shapes, nsem, start, finish)


def _swap_halves_rider(parts):
    def copies_of(ins, outs, send_sems, recv_sems):
        x, y, c, _ = _place()
        return [_remote(ins[a].at[:, _half(g.shape[1], 1 - c), :], outs[a], send_sems, recv_sems, a, (x, y, 1 - c))
                for a, g in enumerate(parts)]

    shapes = [jax.ShapeDtypeStruct((g.shape[0], g.shape[1] // 2, g.shape[2]), g.dtype) for g in parts]
    return _symmetric_rider(parts, shapes, copies_of, len(parts))


def _scatter_rider(parts):
    def copies_of(ins, outs, send_sems, recv_sems):
        x, y, c, chips = _place()
        return [_remote(ins[a].at[2 * cx + cy], outs[a].at[j], send_sems, recv_sems, 3 * a + j, (cx, cy, c))
                for a in range(len(parts)) for j, (cx, cy) in enumerate(chips)]

    shapes = [jax.ShapeDtypeStruct((N_CHIPS - 1,) + h.shape[1:], h.dtype) for h in parts]
    return _symmetric_rider(parts, shapes, copies_of, 3 * len(parts))


def _swap_sums_rider(parts):
    def copies_of(ins, outs, send_sems, recv_sems):
        x, y, c, _ = _place()
        return [_remote(ins[a], outs[a], send_sems, recv_sems, a, (x, y, 1 - c)) for a in range(len(parts))]

    shapes = [jax.ShapeDtypeStruct(g.shape, g.dtype) for g in parts]
    return _symmetric_rider(parts, shapes, copies_of, len(parts))


def _gather_ici_rider(shards):
    def sends(ins, outs, send_sems, recv_sems):
        x, y, c, chips = _place()
        return [_remote(ins[a].at[_half(s.shape[0], c)], outs[a].at[2 * x + y, _half(s.shape[0], c)], send_sems,
                        recv_sems, 3 * a + j, (cx, cy, c)) for a, s in enumerate(shards) for j, (cx, cy) in enumerate(chips)]

    def start(ins, outs, send_sems, recv_sems):
        for cp in sends(ins, outs, send_sems, recv_sems):
            cp.start()

    def finish(ins, outs, send_sems, recv_sems):
        x, y, c, chips = _place()
        for a, s in enumerate(shards):
            for j, (cx, cy) in enumerate(chips):
                landed = outs[a].at[2 * cx + cy, _half(s.shape[0], c)]
                _remote(landed, landed, send_sems, recv_sems, 3 * a + j, (x, y, c)).wait_recv()
        for cp in sends(ins, outs, send_sems, recv_sems):
            cp.wait_send()

    shapes = [jax.ShapeDtypeStruct((N_CHIPS,) + s.shape, s.dtype) for s in shards]
    return _Rider(shards, shapes, 3 * len(shards), start, finish)


def _gather_pass_rider(landed):
    def sends(ins, outs, send_sems, recv_sems):
        x, y, c, chips = _place()
        return [_remote(ins[a].at[2 * cx + cy, _half(g.shape[1], c)], outs[a].at[2 * cx + cy, _half(g.shape[1], c)],
                        send_sems, recv_sems, 3 * a + j, (x, y, 1 - c))
                for a, g in enumerate(landed) for j, (cx, cy) in enumerate(chips)]

    def start(ins, outs, send_sems, recv_sems):
        for cp in sends(ins, outs, send_sems, recv_sems):
            cp.start()

    def finish(ins, outs, send_sems, recv_sems):
        x, y, c, chips = _place()
        for a, g in enumerate(landed):
            for j, (cx, cy) in enumerate(chips):
                other = outs[a].at[2 * cx + cy, _half(g.shape[1], 1 - c)]
                _remote(other, other, send_sems, recv_sems, 3 * a + j, (x, y, c)).wait_recv()
        for cp in sends(ins, outs, send_sems, recv_sems):
            cp.wait_send()

    shapes = [jax.ShapeDtypeStruct(g.shape, g.dtype) for g in landed]
    return _Rider(landed, shapes, 3 * len(landed), start, finish, aliases={a: a for a in range(len(landed))})


def _grad_parts(g):
    return g.reshape((N_CHIPS, -1, g.shape[-1]))


def _place_scalars():
    return jnp.stack([lax.axis_index("c"), 2 * lax.axis_index("x") + lax.axis_index("y")]).astype(jnp.int32)


def _scalar_call(body, name, grid, in_specs, out_specs, out_shape, operands):
    spec = pltpu.PrefetchScalarGridSpec(num_scalar_prefetch=1, grid=grid, in_specs=in_specs, out_specs=out_specs)
    return pl.pallas_call(body, name=name, grid_spec=spec, out_shape=out_shape,
                          compiler_params=_params(*(["arbitrary"] * len(grid))))(_place_scalars(), *operands)


def _pair_sums(names, parts, sib):
    out = []
    for n, g, s in zip(names, parts, sib):
        rh, cols = s.shape[1], s.shape[2]
        tm = _row_tile(rh, 512)
        nblk = rh // tm

        def body(place, g_ref, s_ref, o_ref):
            o_ref[...] = (g_ref[...].astype(F32) + s_ref[...].astype(F32)).astype(o_ref.dtype)

        blk = pl.BlockSpec((None, tm, cols), lambda j, i, place: (j, i, 0))
        own = pl.BlockSpec((None, tm, cols), lambda j, i, place, nblk=nblk: (j, place[0] * nblk + i, 0))
        out.append(_scalar_call(body, "grad_pair_sum_" + n, (N_CHIPS, nblk), [own, blk], blk,
                                jax.ShapeDtypeStruct(s.shape, BF16), (g, s)))
    return out


def _chip_sums(names, pair, others):
    out = []
    for n, h, o in zip(names, pair, others):
        rh, cols = h.shape[1], h.shape[2]
        tm = _row_tile(rh, 512)

        def body(place, h_ref, a_ref, b_ref, c_ref, o_ref):
            o_ref[...] = (h_ref[...].astype(F32) + a_ref[...].astype(F32)) + b_ref[...].astype(F32) + c_ref[...].astype(F32)

        mine = pl.BlockSpec((None, tm, cols), lambda i, place: (place[1], i, 0))
        other = lambda k: pl.BlockSpec((None, tm, cols), lambda i, place, k=k: (k, i, 0))
        out.append(_scalar_call(body, "grad_chip_sum_" + n, (rh // tm,), [mine, other(0), other(1), other(2)],
                                pl.BlockSpec((tm, cols), lambda i, place: (i, 0)), jax.ShapeDtypeStruct((rh, cols), F32),
                                (h, o, o, o)))
    return out


def _adamw_halves(name, w, m, v, own, sib):
    rh, cols = own.shape
    tm = _row_tile(rh, 256)
    nblk = rh // tm

    def body(place, w_ref, m_ref, v_ref, own_ref, sib_ref, g_ref, d_ref, m2_ref, v2_ref):
        mine = pl.program_id(0) // nblk == place[0]

        def run(gv):
            g_ref[...] = gv
            d_ref[...], m2_ref[...], v2_ref[...] = _adamw_math(w_ref[...], gv, m_ref[...], v_ref[...])

        @pl.when(mine)
        def _():
            run(own_ref[...])

        @pl.when(jnp.logical_not(mine))
        def _():
            run(sib_ref[...])

    full = pl.BlockSpec((tm, cols), lambda i, place: (i, 0))
    own_spec = pl.BlockSpec((tm, cols), lambda i, place: (jnp.where(i // nblk == place[0], i % nblk, 0), 0))
    sib_spec = pl.BlockSpec((tm, cols), lambda i, place: (jnp.where(i // nblk == place[0], 0, i % nblk), 0))
    return _scalar_call(body, name, (2 * nblk,), [full, full, full, own_spec, sib_spec], [full] * 4,
                        [jax.ShapeDtypeStruct((2 * rh, cols), F32)] * 4, (w, m, v, own, sib))


def _gather_all_rider(v):
    m_per = v.shape[0]

    def rows(ref, px, py, pc):
        return ref.at[pl.ds(pl.multiple_of((4 * px + 2 * py + pc) * m_per, 8), m_per)]

    def first(ins, outs, send_sems, recv_sems):
        x, y, c, chips = _place()
        mine = rows(outs[0], x, y, c)
        return [_remote(ins[0], mine, send_sems, recv_sems, 0, (x, y, 1 - c))] + [
            _remote(ins[0], mine, send_sems, recv_sems, 1 + j, (cx, cy, c)) for j, (cx, cy) in enumerate(chips)]

    def start(ins, outs, send_sems, recv_sems):
        for cp in first(ins, outs, send_sems, recv_sems):
            cp.start()

    def finish(ins, outs, send_sems, recv_sems):
        x, y, c, chips = _place()
        passed = []
        for j, (cx, cy) in enumerate(chips):
            blk = rows(outs[0], cx, cy, c)
            _remote(blk, blk, send_sems, recv_sems, 1 + j, (x, y, c)).wait_recv()
            passed.append(_remote(blk, blk, send_sems, recv_sems, 4 + j, (x, y, 1 - c)))
            passed[j].start()
        sib = rows(outs[0], x, y, 1 - c)
        _remote(sib, sib, send_sems, recv_sems, 0, (x, y, c)).wait_recv()
        for j, (cx, cy) in enumerate(chips):
            blk = rows(outs[0], cx, cy, 1 - c)
            _remote(blk, blk, send_sems, recv_sems, 4 + j, (x, y, c)).wait_recv()
        for cp in first(ins, outs, send_sems, recv_sems) + passed:
            cp.wait_send()

    return _Rider([v], [jax.ShapeDtypeStruct((N_DEV * m_per,) + v.shape[1:], v.dtype)], 7, start, finish)


def _sum_over_devices(name, v, gathered):
    m_per = v.shape[0]
    dev = 4 * lax.axis_index("x") + 2 * lax.axis_index("y") + lax.axis_index("c")
    full = lax.dynamic_update_slice_in_dim(gathered, v, dev * m_per, axis=0)
    return _sum_blocks(name, [full[i * m_per:(i + 1) * m_per] for i in range(N_DEV)], F32)


def _sum_blocks(name, parts, out_dtype):
    rows, cols = parts[0].shape
    tm = _row_tile(rows, 512)

    def body(*refs):
        acc = refs[0][...].astype(F32)
        for r in refs[1:-1]:
            acc = acc + r[...].astype(F32)
        refs[-1][...] = acc.astype(refs[-1].dtype)

    spec = pl.BlockSpec((tm, cols), lambda i: (i, 0))
    return pl.pallas_call(
        body, name=name, grid=(rows // tm,), in_specs=[spec] * len(parts), out_specs=spec,
        out_shape=jax.ShapeDtypeStruct((rows, cols), out_dtype), compiler_params=_params("arbitrary"),
    )(*parts)


def _adamw_math(wv, gv, mv, vv):
    m2 = ADAM_B1 * mv + (1.0 - ADAM_B1) * gv
    v2 = ADAM_B2 * vv + (1.0 - ADAM_B2) * (gv * gv)
    delta = -ADAM_LR * ((m2 / (1.0 - ADAM_B1 ** ADAM_STEP)) / (jnp.sqrt(v2 / (1.0 - ADAM_B2 ** ADAM_STEP)) + ADAM_EPS)
                        + ADAM_WD * wv)
    return delta, m2, v2


def _adamw_small(ws, gs, ms, vs):
    n = len(ws)

    def body(*refs):
        for i in range(n):
            res = _adamw_math(refs[i][...], refs[n + i][...], refs[2 * n + i][...], refs[3 * n + i][...])
            for k in range(3):
                refs[(4 + k) * n + i][...] = res[k]

    vm = pl.BlockSpec(memory_space=pltpu.VMEM)
    outs = pl.pallas_call(
        body, name="adamw_small", in_specs=[vm] * (4 * n), out_specs=[vm] * (3 * n),
        out_shape=[jax.ShapeDtypeStruct(a.shape, F32) for a in ws] * 3,
        compiler_params=pltpu.CompilerParams(vmem_limit_bytes=VMEM_LIMIT_BYTES),
    )(*ws, *gs, *ms, *vs)
    return outs[:n], outs[n:2 * n], outs[2 * n:]


PACK_ROWS = 256


def _pack(flat_parts, dtype, lead=()):
    parts = [a.astype(dtype).reshape(lead + (-1,)) for a in flat_parts]
    n = sum(a.shape[-1] for a in parts)
    chunk = PACK_ROWS * LANES
    total = -(-n // chunk) * chunk
    if total > n:
        parts.append(jnp.zeros(lead + (total - n,), dtype))
    return jnp.concatenate(parts, axis=-1).reshape(lead + (total // LANES, LANES))


def _unpack(buf, shapes, lead=()):
    flat = buf.reshape(lead + (-1,))
    out, off = [], 0
    for shp in shapes:
        n = math.prod(shp)
        out.append(lax.slice_in_dim(flat, off, off + n, axis=len(lead)).reshape(lead + tuple(shp)))
        off += n
    return out


BIG = ("w_in", "w_glu", "w_pa", "w_pb", "w_out", "w_up", "w_down")
WEIGHTS = ("g_mix", "w_in", "s5_a_re", "s5_a_im", "s5_log_dt", "s5_b_re", "s5_b_im", "s5_c_re", "s5_c_im", "s5_d",
           "w_glu", "b_glu", "hg_lb_logits", "hg_norm_gain", "w_pa", "w_pb", "w_out", "g_ffn", "w_up", "w_conv",
           "b_conv", "w_down", "g_final")
SMALL = tuple(n for n in WEIGHTS if n not in BIG)
SMALL_PARTS = ("loss", "g_mix", "g_ffn", "g_final", "b_glu", "gain", "lbrow", "s5_d", "w_conv", "b_conv", "lam_re",
               "lam_im", "bb_re", "bb_im", "s5_c_re", "s5_c_im")


def _lower_bound(logits):
    return jnp.cumsum(jax.nn.softmax(logits, axis=0), axis=0)[0:1]


def kernel(x, g_mix, w_in, s5_a_re, s5_a_im, s5_log_dt, s5_b_re, s5_b_im, s5_c_re, s5_c_im, s5_d, w_glu, b_glu, hg_lb_logits, hg_norm_gain, w_pa, w_pb, w_out, g_ffn, w_up, w_conv, b_conv, w_down, g_final, loss_target, m_g_mix, m_w_in, m_s5_a_re, m_s5_a_im, m_s5_log_dt, m_s5_b_re, m_s5_b_im, m_s5_c_re, m_s5_c_im, m_s5_d, m_w_glu, m_b_glu, m_hg_lb_logits, m_hg_norm_gain, m_w_pa, m_w_pb, m_w_out, m_g_ffn, m_w_up, m_w_conv, m_b_conv, m_w_down, m_g_final, v_g_mix, v_w_in, v_s5_a_re, v_s5_a_im, v_s5_log_dt, v_s5_b_re, v_s5_b_im, v_s5_c_re, v_s5_c_im, v_s5_d, v_w_glu, v_b_glu, v_hg_lb_logits, v_hg_norm_gain, v_w_pa, v_w_pb, v_w_out, v_g_ffn, v_w_up, v_w_conv, v_b_conv, v_w_down, v_g_final):
    args = dict(locals())
    w = {n: args[n] for n in WEIGHTS}
    mom = {n: args["m_" + n] for n in WEIGHTS}
    var = {n: args["v_" + n] for n in WEIGHTS}
    nseq, seq, d = x.shape
    xi, yi = lax.axis_index("x"), lax.axis_index("y")
    chip = 2 * xi + yi

    shard = {n: w[n][0] for n in BIG}
    shard16 = {n: shard[n].astype(BF16) for n in BIG}
    first = ("w_in", "w_glu")
    got = _gather_weights([shard16[n] for n in first], [w_conv[0]])
    p = {n: lax.dynamic_update_index_in_dim(g, shard16[n], chip, 0) for n, g in zip(first, got)}
    p["w_glu"] = p["w_glu"].reshape(-1, p["w_glu"].shape[-1])
    conv_all = lax.dynamic_update_index_in_dim(got[-1], w_conv[0], chip, 0)
    p.update(g_mix=g_mix, g_ffn=g_ffn, g_final=g_final.reshape(1, -1), b_glu=b_glu, gain=hg_norm_gain, s5_d=s5_d,
             b_conv=b_conv, w_conv=conv_all.transpose(1, 0, 2).reshape(CONV_W, -1), lbrow=_lower_bound(hg_lb_logits),
             s5_a_re=s5_a_re[0], s5_a_im=s5_a_im[0], s5_log_dt=s5_log_dt[0], s5_b_re=s5_b_re[0], s5_b_im=s5_b_im[0],
             s5_c_re=s5_c_re[0], s5_c_im=s5_c_im[0])

    dx, halves, sm = _local_step(x.reshape(nseq * seq, d), loss_target.reshape(nseq * seq, d), p,
                                 {n: shard16[n] for n in LATE}, nseq=nseq, seq=seq)
    loss = sm["loss"][0, 0]

    grads, delta, new_m, new_v = {}, {}, {}, {}
    for n in BIG:
        shp = shard[n].shape
        grads[n], delta[n], new_m[n], new_v[n] = _adamw_halves("adamw_" + n, shard[n], mom[n].reshape(shp),
                                                               var[n].reshape(shp), *halves[n])

    _, disc_vjp = jax.vjp(_s5_discretize, p["s5_a_re"], p["s5_a_im"], p["s5_log_dt"], p["s5_b_re"], p["s5_b_im"])
    da_re, da_im, dlog_dt, db_re, db_im = disc_vjp((sm["lam_re"], sm["lam_im"], sm["bb_re"], sm["bb_im"]))
    _, lb_vjp = jax.vjp(_lower_bound, hg_lb_logits)
    (dlogits,) = lb_vjp(sm["lbrow"])
    fcols = w_conv.shape[-1]
    grads.update(
        g_mix=sm["g_mix"], g_ffn=sm["g_ffn"], g_final=sm["g_final"].reshape(-1), b_glu=sm["b_glu"],
        hg_norm_gain=sm["gain"], hg_lb_logits=dlogits, s5_d=sm["s5_d"], b_conv=sm["b_conv"],
        w_conv=lax.dynamic_slice_in_dim(sm["w_conv"], chip * fcols, fcols, axis=1),
        s5_a_re=da_re, s5_a_im=da_im, s5_log_dt=dlog_dt, s5_b_re=db_re, s5_b_im=db_im,
        s5_c_re=sm["s5_c_re"], s5_c_im=sm["s5_c_im"])
    grads = {n: grads[n].reshape(w[n].shape) for n in WEIGHTS}

    def natural(a):
        return a.reshape(1, -1) if a.ndim == 1 else (a[0] if a.ndim > 2 else a)

    outs = _adamw_small(*[[natural(src[n]) for n in SMALL] for src in (w, grads, mom, var)])
    for dst, group in zip((delta, new_m, new_v), outs):
        dst.update(zip(SMALL, group))
    res = [loss, dx.reshape(x.shape)]
    for group in (grads, delta, new_m, new_v):
        res += [group[n].reshape(w[n].shape) for n in WEIGHTS]
    return tuple(res)
```

```python
import functools
import math

import jax
import jax.numpy as jnp
from jax import lax
from jax.experimental import pallas as pl
from jax.experimental.pallas import tpu as pltpu

F32 = jnp.float32
BF16 = jnp.bfloat16
MESH = pl.DeviceIdType.MESH

EPS = 1e-6
S5_GROUP = 16
S5_STATE = 64
S5_BLOCK_GROUPS = 8
HEAD = 128
CHUNK = 64
CONV_W = 3
LANES = 128
SUBLANES = 8
GATE_BLOCK = 512
VMEM_LIMIT_BYTES = 56 * 1024 * 1024

ADAM_LR = 0.001
ADAM_B1 = 0.9
ADAM_B2 = 0.999
ADAM_EPS = 1e-08
ADAM_WD = 0.01
ADAM_STEP = 10

N_CHIPS = 4
N_DEV = 8


def _params(*sem):
    return pltpu.CompilerParams(dimension_semantics=sem, vmem_limit_bytes=VMEM_LIMIT_BYTES)


class _Rider:
    def __init__(self, arrays, out_shapes, nsem, start, finish, aliases=None):
        self.arrays, self.out_shapes, self.nsem = list(arrays), list(out_shapes), nsem
        self.start, self.finish, self.aliases = start, finish, dict(aliases or {})


def _hosted_call(name, body, *, grid, in_specs, out_specs, out_shape, operands, scratch_shapes=(), rider=None):
    in_specs, out_specs, out_shape, scratch_shapes = list(in_specs), list(out_specs), list(out_shape), list(scratch_shapes)
    cparams = _params(*(["arbitrary"] * len(grid)))
    if rider is None:
        return pl.pallas_call(body, name=name, grid=grid, in_specs=in_specs, out_specs=out_specs, out_shape=out_shape,
                              scratch_shapes=scratch_shapes, compiler_params=cparams)(*operands)
    n_in, n_out, n_sc = len(in_specs), len(out_specs), len(scratch_shapes)
    r_in, r_out = len(rider.arrays), len(rider.out_shapes)

    def hosted(*refs):
        ins, rins = refs[:n_in], refs[n_in:n_in + r_in]
        outs = refs[n_in + r_in:n_in + r_in + n_out]
        routs = refs[n_in + r_in + n_out:n_in + r_in + n_out + r_out]
        rest = refs[n_in + r_in + n_out + r_out:]
        send_sems, recv_sems = rest[n_sc], rest[n_sc + 1]
        first = functools.reduce(jnp.logical_and, [pl.program_id(i) == 0 for i in range(len(grid))])
        last = functools.reduce(jnp.logical_and, [pl.program_id(i) == grid[i] - 1 for i in range(len(grid))])

        @pl.when(first)
        def _():
            rider.start(rins, routs, send_sems, recv_sems)

        body(*ins, *outs, *rest[:n_sc])

        @pl.when(last)
        def _():
            rider.finish(rins, routs, send_sems, recv_sems)

    res = pl.pallas_call(
        hosted, name=name, grid=grid, in_specs=in_specs + [ANY] * r_in, out_specs=out_specs + [ANY] * r_out,
        out_shape=out_shape + rider.out_shapes,
        scratch_shapes=scratch_shapes + [pltpu.SemaphoreType.DMA((rider.nsem,)), pltpu.SemaphoreType.DMA((rider.nsem,))],
        input_output_aliases={n_in + i: n_out + o for i, o in rider.aliases.items()}, compiler_params=cparams,
    )(*operands, *rider.arrays)
    return res[:n_out], res[n_out:]


def _run_rider(name, rider):
    r_in, r_out = len(rider.arrays), len(rider.out_shapes)

    def body(*refs):
        rins, routs, send_sems, recv_sems = refs[:r_in], refs[r_in:r_in + r_out], refs[-2], refs[-1]
        rider.start(rins, routs, send_sems, recv_sems)
        rider.finish(rins, routs, send_sems, recv_sems)

    return pl.pallas_call(
        body, name=name, in_specs=[ANY] * r_in, out_specs=[ANY] * r_out, out_shape=rider.out_shapes,
        scratch_shapes=[pltpu.SemaphoreType.DMA((rider.nsem,)), pltpu.SemaphoreType.DMA((rider.nsem,))],
        input_output_aliases=rider.aliases,
    )(*rider.arrays)


def _row_tile(rows, cap):
    if rows <= cap:
        return rows
    for t in range(cap - cap % 8, 7, -8):
        if rows % t == 0:
            return t
    raise ValueError(f"no row tile for {rows}")


def _dot(a, b):
    return jnp.dot(a.astype(BF16), b.astype(BF16), preferred_element_type=F32)


def _dot_nt(a, b):
    return lax.dot_general(a.astype(BF16), b.astype(BF16), (((1,), (1,)), ((), ())), preferred_element_type=F32)


def _dot_tn(a, b):
    return lax.dot_general(a.astype(BF16), b.astype(BF16), (((0,), (0,)), ((), ())), preferred_element_type=F32)


def _sigmoid(x):
    return 0.5 * jnp.tanh(0.5 * x) + 0.5


_GELU_C = math.sqrt(2.0 / math.pi)


def _gelu(x):
    return 0.5 * x * (1.0 + jnp.tanh(_GELU_C * (x + 0.044715 * x * x * x)))


def _gelu_grad(x):
    th = jnp.tanh(_GELU_C * (x + 0.044715 * x * x * x))
    return 0.5 * (1.0 + th) + 0.5 * x * (1.0 - th * th) * _GELU_C * (1.0 + 3.0 * 0.044715 * x * x)


def _rowwise(name, fn, ins, outs, accs=(), *, rows, tm, ncol=1, rider=None):
    n_in, n_out = len(ins), len(outs)

    def body(*refs):
        res = fn(*[r[...] for r in refs[:n_in]])
        for r, v in zip(refs[n_in:n_in + n_out], res[:n_out]):
            r[...] = v.astype(r.dtype)
        first = pl.program_id(1) == 0
        for r, v in zip(refs[n_in + n_out:], res[n_out:]):
            @pl.when(first)
            def _():
                r[...] = v

            @pl.when(jnp.logical_not(first))
            def _():
                r[...] += v

    in_specs = []
    for _, width, base, kind in ins:
        if kind == "row":
            in_specs.append(pl.BlockSpec((tm, width), lambda j, i, b=base: (i, b + j)))
        else:
            in_specs.append(pl.BlockSpec((1, width), lambda j, i, b=base: (0, b + j)))
    out_specs = [pl.BlockSpec((tm, width), lambda j, i: (i, j)) for _, width, _ in outs]
    out_specs += [pl.BlockSpec((1, width), lambda j, i: (0, j)) for _, width in accs]
    out_shape = [jax.ShapeDtypeStruct((rows, total), dt) for total, _, dt in outs]
    out_shape += [jax.ShapeDtypeStruct((1, total), F32) for total, _ in accs]
    return _hosted_call(name, body, grid=(ncol, rows // tm), in_specs=in_specs, out_specs=out_specs, out_shape=out_shape,
                        operands=[a for a, _, _, _ in ins], rider=rider)


def _mm(name, a, b, *, mode, grid, a_spec, b_spec, o_spec, out_shape, acc_shape, res=None, res_spec=None,
        pair_axis=None, rider=None, epilogue=None):
    nk = grid[2]
    dot = {"nn": _dot, "nt": _dot_nt, "tn": _dot_tn}[mode]
    a_list = list(a) if isinstance(a, tuple) else [a]
    b_list = list(b) if isinstance(b, tuple) else [b]
    na, nb = len(a_list), len(b_list)
    assert (pair_axis is None) == (na + nb == 2)
    direct = nk == 1 and pair_axis is None
    epi_fn, epi_ins, epi_sums = epilogue if epilogue is not None else (None, [], [])
    n_res = 0 if res is None else 1
    n_epi = len(epi_ins)

    def body(*refs):
        a_refs, b_refs = refs[:na], refs[na:na + nb]
        r_ref = None if res is None else refs[na + nb]
        e_refs = refs[na + nb + n_res:na + nb + n_res + n_epi]
        o_ref = refs[na + nb + n_res + n_epi]
        s_refs = refs[na + nb + n_res + n_epi + 1:na + nb + n_res + n_epi + 1 + len(epi_sums)]
        first_rows = pl.program_id(0) == 0

        def finish(v):
            if res is not None:
                v = v + r_ref[...]
            if epi_fn is None:
                o_ref[...] = v.astype(o_ref.dtype)
                return
            outs = epi_fn(v, *[r[...] for r in e_refs])
            o_ref[...] = outs[0].astype(o_ref.dtype)
            for s_ref, part in zip(s_refs, outs[1:]):
                @pl.when(first_rows)
                def _():
                    s_ref[...] = part

                @pl.when(jnp.logical_not(first_rows))
                def _():
                    s_ref[...] += part

        if direct:
            finish(dot(a_refs[0][...], b_refs[0][...]))
            return
        acc_ref = refs[-1]
        k = pl.program_id(2)

        @pl.when(k == 0)
        def _():
            acc_ref[...] = jnp.zeros_like(acc_ref)

        if pair_axis is None:
            acc_ref[...] += dot(a_refs[0][...], b_refs[0][...])
        else:
            lower = pl.program_id(pair_axis) < grid[pair_axis] // 2

            @pl.when(lower)
            def _():
                acc_ref[...] += dot(a_refs[0][...], b_refs[0][...])

            @pl.when(jnp.logical_not(lower))
            def _():
                acc_ref[...] += dot(a_refs[-1][...], b_refs[-1][...])

        @pl.when(k == nk - 1)
        def _():
            finish(acc_ref[...])

    operands = a_list + b_list + ([] if res is None else [res]) + [arr for arr, _ in epi_ins]
    in_specs = (list(a_spec) if na == 2 else [a_spec]) + (list(b_spec) if nb == 2 else [b_spec])
    in_specs += ([] if res is None else [res_spec]) + [spec for _, spec in epi_ins]
    out_specs = [o_spec] + [pl.BlockSpec((1, c), lambda *_: (0, 0)) for c in epi_sums]
    out_shapes = [out_shape] + [jax.ShapeDtypeStruct((1, c), F32) for c in epi_sums]
    got = _hosted_call(name, body, grid=grid, in_specs=in_specs, out_specs=out_specs, out_shape=out_shapes,
                       scratch_shapes=[] if direct else [pltpu.VMEM(acc_shape, F32)], operands=operands, rider=rider)
    mine, rider_outs = (got, None) if rider is None else got
    mine = mine[0] if epilogue is None else tuple(mine)
    return mine if rider is None else (mine, rider_outs)


MM_TILE_BUDGET_BYTES = 36 * 1024 * 1024
MM_TILE_CAP = 1024


def _mm_tile(t, row_bytes, fixed_bytes):
    cap = max(16, min(MM_TILE_CAP, (MM_TILE_BUDGET_BYTES - fixed_bytes) // row_bytes))
    return _row_tile(t, cap - cap % 16)


def _size(a):
    return jnp.dtype(a.dtype).itemsize


def _mm_fwd_cols(name, a, w3, out_dtype=F32, rider=None):
    t, k = a.shape
    ns = w3.shape[2]
    tm = _mm_tile(t, 2 * k * _size(a) + 2 * ns * jnp.dtype(out_dtype).itemsize, 2 * k * ns * _size(w3))
    return _mm(name, a, w3, mode="nn", grid=(N_CHIPS, t // tm, 1),
               a_spec=pl.BlockSpec((tm, k), lambda j, i, kk: (i, 0)),
               b_spec=pl.BlockSpec((None, k, ns), lambda j, i, kk: (j, 0, 0)),
               o_spec=pl.BlockSpec((tm, ns), lambda j, i, kk: (i, j)),
               out_shape=jax.ShapeDtypeStruct((t, N_CHIPS * ns), out_dtype), acc_shape=(tm, ns), rider=rider)


def _mm_bwd_cols(name, d, w3, out_dtype=F32, rider=None, epilogue=None):
    pair = isinstance(d, tuple)
    t = d[0].shape[0] if pair else d.shape[0]
    k, ns = w3.shape[1], w3.shape[2]
    dsize = _size(d[0] if pair else d)
    tm = _mm_tile(t, (4 if pair else 2) * ns * dsize + 2 * k * jnp.dtype(out_dtype).itemsize + 4 * k
                  + _row_epilogue(epilogue, 8)[1], 2 * k * ns * _size(w3))
    half = N_CHIPS // 2
    if pair:
        a_spec = (pl.BlockSpec((tm, ns), lambda i, j, kk: (i, jnp.minimum(kk, half - 1))),
                  pl.BlockSpec((tm, ns), lambda i, j, kk: (i, jnp.maximum(kk - half, 0))))
    else:
        a_spec = pl.BlockSpec((tm, ns), lambda i, j, kk: (i, kk))
    return _mm(name, d, w3, mode="nt", grid=(t // tm, 1, N_CHIPS), a_spec=a_spec,
               b_spec=pl.BlockSpec((None, k, ns), lambda i, j, kk: (kk, 0, 0)),
               o_spec=pl.BlockSpec((tm, k), lambda i, j, kk: (i, 0)),
               out_shape=jax.ShapeDtypeStruct((t, k), out_dtype), acc_shape=(tm, k), pair_axis=2 if pair else None,
               rider=rider, epilogue=_row_epilogue(epilogue, tm)[0])


def _mm_wgrad_cols(name, a, d, rider=None):
    pair = isinstance(d, tuple)
    t, k = a.shape
    ns = (2 * d[0].shape[1] if pair else d.shape[1]) // N_CHIPS
    dsize = _size(d[0] if pair else d)
    tk = _mm_tile(t, 2 * k * _size(a) + (4 if pair else 2) * ns * dsize, k * ns * (4 + 2 * 2))
    half = N_CHIPS // 2
    if pair:
        b_spec = (pl.BlockSpec((tk, ns), lambda j, i, kk: (jnp.where(j < half, kk, 0), jnp.minimum(j, half - 1))),
                  pl.BlockSpec((tk, ns), lambda j, i, kk: (jnp.where(j < half, 0, kk), jnp.maximum(j - half, 0))))
    else:
        b_spec = pl.BlockSpec((tk, ns), lambda j, i, kk: (kk, j))
    return _mm(name, a, d, mode="tn", grid=(N_CHIPS, 1, t // tk),
               a_spec=pl.BlockSpec((tk, k), lambda j, i, kk: (kk, 0)), b_spec=b_spec,
               o_spec=pl.BlockSpec((None, k, ns), lambda j, i, kk: (j, 0, 0)),
               out_shape=jax.ShapeDtypeStruct((N_CHIPS, k, ns), BF16), acc_shape=(k, ns),
               pair_axis=0 if pair else None, rider=rider)


MM_BLOCK_CAP = 1408


def _row_epilogue(epilogue, tm):
    if epilogue is None:
        return None, 0
    fn, arrays, sums = epilogue
    specs = [pl.BlockSpec((1, x.shape[1]), lambda i, j, kk: (0, 0)) if x.shape[0] == 1 else
             pl.BlockSpec((tm, x.shape[1]), lambda i, j, kk: (i, 0)) for x in arrays]
    return (fn, list(zip(arrays, specs)), list(sums)), sum(2 * x.shape[1] * _size(x) for x in arrays if x.shape[0] > 1)


def _mm_fwd_rows(name, a, w, res=None, out_dtype=F32, epilogue=None):
    t, k = a.shape
    n = w.shape[1]
    tk = k if k <= MM_BLOCK_CAP else MM_BLOCK_CAP
    assert k % tk == 0
    row_bytes = 2 * tk * _size(a) + 2 * n * jnp.dtype(out_dtype).itemsize + (0 if res is None else 2 * n * 4) + 4 * n
    row_bytes += _row_epilogue(epilogue, 8)[1]
    tm = _mm_tile(t, row_bytes, 2 * tk * n * _size(w))
    return _mm(name, a, w, mode="nn", grid=(t // tm, 1, k // tk),
               a_spec=pl.BlockSpec((tm, tk), lambda i, j, kk: (i, kk)),
               b_spec=pl.BlockSpec((tk, n), lambda i, j, kk: (kk, 0)),
               o_spec=pl.BlockSpec((tm, n), lambda i, j, kk: (i, 0)),
               out_shape=jax.ShapeDtypeStruct((t, n), out_dtype), acc_shape=(tm, n),
               res=res, res_spec=None if res is None else pl.BlockSpec((tm, n), lambda i, j, kk: (i, 0)),
               epilogue=_row_epilogue(epilogue, tm)[0])


def _mm_bwd_rows(name, d, w, out_dtype=F32):
    t, n = d.shape
    k = w.shape[0]
    tn = k if k <= MM_BLOCK_CAP else MM_BLOCK_CAP
    assert k % tn == 0
    tm = _mm_tile(t, 2 * n * _size(d) + 2 * tn * jnp.dtype(out_dtype).itemsize, 2 * tn * n * _size(w))
    return _mm(name, d, w, mode="nt", grid=(t // tm, k // tn, 1),
               a_spec=pl.BlockSpec((tm, n), lambda i, j, kk: (i, 0)),
               b_spec=pl.BlockSpec((tn, n), lambda i, j, kk: (j, 0)),
               o_spec=pl.BlockSpec((tm, tn), lambda i, j, kk: (i, j)),
               out_shape=jax.ShapeDtypeStruct((t, k), out_dtype), acc_shape=(tm, tn))


def _mm_wgrad_rows(name, a, d):
    t, k = a.shape
    n = d.shape[1]
    nblk = next(b for b in (1, 2, 4) if (k // b) % LANES == 0 and k // b <= MM_BLOCK_CAP)
    ks = k // nblk
    tk = _mm_tile(t, 2 * ks * _size(a) + 2 * n * _size(d), ks * n * (4 + 2 * 2))
    return _mm(name, a, d, mode="tn", grid=(nblk, 1, t // tk),
               a_spec=pl.BlockSpec((tk, ks), lambda j, i, kk: (kk, j)),
               b_spec=pl.BlockSpec((tk, n), lambda j, i, kk: (kk, 0)),
               o_spec=pl.BlockSpec((ks, n), lambda j, i, kk: (j, 0)),
               out_shape=jax.ShapeDtypeStruct((k, n), BF16), acc_shape=(ks, n))


def _s5_discretize(a_re, a_im, log_dt, b_re, b_im):
    dt = jnp.exp(log_dt)[:, None]
    mag = jnp.exp(a_re * dt)
    ang = a_im * dt
    lb_re = mag * jnp.cos(ang)
    lb_im = mag * jnp.sin(ang)
    den = a_re * a_re + a_im * a_im
    n_re = lb_re - 1.0
    n_im = lb_im
    co_re = ((n_re * a_re + n_im * a_im) / den)[..., None]
    co_im = ((n_im * a_re - n_re * a_im) / den)[..., None]
    bb_re = co_re * b_re - co_im * b_im
    bb_im = co_re * b_im + co_im * b_re
    return lb_re, lb_im, bb_re, bb_im


def _s5_in_blocks(bb):
    g = bb.shape[0]
    nb = g // S5_BLOCK_GROUPS
    t = bb.reshape(nb, S5_BLOCK_GROUPS, S5_STATE, S5_GROUP).transpose(0, 1, 3, 2)
    eye = jnp.eye(S5_BLOCK_GROUPS, dtype=bb.dtype)
    full = t[:, :, :, None, :] * eye[None, :, None, :, None]
    return full.reshape(nb, S5_BLOCK_GROUPS * S5_GROUP, S5_BLOCK_GROUPS * S5_STATE)


def _s5_in_blocks_diag(blocks):
    nb = blocks.shape[0]
    t = blocks.reshape(nb, S5_BLOCK_GROUPS, S5_GROUP, S5_BLOCK_GROUPS, S5_STATE)
    d = jnp.einsum("bghgp->bghp", t)
    return d.transpose(0, 1, 3, 2).reshape(nb * S5_BLOCK_GROUPS, S5_STATE, S5_GROUP)


def _s5_out_blocks(c):
    g = c.shape[0]
    nb = g // S5_BLOCK_GROUPS
    t = c.reshape(nb, S5_BLOCK_GROUPS, S5_GROUP, S5_STATE).transpose(0, 1, 3, 2)
    eye = jnp.eye(S5_BLOCK_GROUPS, dtype=c.dtype)
    full = t[:, :, :, None, :] * eye[None, :, None, :, None]
    return full.reshape(nb, S5_BLOCK_GROUPS * S5_STATE, S5_BLOCK_GROUPS * S5_GROUP)


def _s5_out_blocks_diag(blocks):
    nb = blocks.shape[0]
    t = blocks.reshape(nb, S5_BLOCK_GROUPS, S5_STATE, S5_BLOCK_GROUPS, S5_GROUP)
    d = jnp.einsum("bgpgh->bgph", t)
    return d.transpose(0, 1, 3, 2).reshape(nb * S5_BLOCK_GROUPS, S5_GROUP, S5_STATE)


def _s5_scan_tables(lr, li, reverse):
    def cmul(a, b):
        return a[0] * b[0] - a[1] * b[1], a[0] * b[1] + a[1] * b[0]

    lam = (lr, -li) if reverse else (lr, li)
    pw = [lam]
    for _ in range(SUBLANES - 1):
        pw.append(cmul(pw[-1], lam))
    sub = jnp.arange(SUBLANES)[:, None]
    rows = []
    for s in (1, 2, 4):
        keep = (sub <= SUBLANES - 1 - s) if reverse else (sub >= s)
        rows.append(jnp.where(keep, pw[s - 1][0][None, :], 0.0))
        rows.append(jnp.where(keep, pw[s - 1][1][None, :], 0.0))
    order = list(range(SUBLANES - 1, -1, -1)) if reverse else list(range(SUBLANES))
    rows.append(jnp.stack([pw[i][0] for i in order]))
    rows.append(jnp.stack([pw[i][1] for i in order]))
    return jnp.concatenate(rows, axis=0)


def _s5_scan(vre_ref, vim_ref, coef_ref, seq, width, reverse, xre_ref=None, xim_ref=None):
    nt = seq // SUBLANES
    nl = width // LANES
    per = 2 if xre_ref is None else 4
    sub = lax.broadcasted_iota(jnp.int32, (SUBLANES, LANES), 0)

    def step(k, carry):
        kk = (nt - 1 - k) if reverse else k
        rows = pl.ds(pl.multiple_of(kk * SUBLANES, SUBLANES), SUBLANES)
        out = []
        for j in range(nl):
            lanes = slice(j * LANES, (j + 1) * LANES)
            co = [coef_ref[SUBLANES * q:SUBLANES * (q + 1), lanes] for q in range(8)]
            cr, ci = carry[per * j], carry[per * j + 1]
            vr = vre_ref[rows, lanes]
            vi = vim_ref[rows, lanes]
            for q, s in enumerate((1, 2, 4)):
                sh = SUBLANES - s if reverse else s
                rr = pltpu.roll(vr, sh, 0)
                ri = pltpu.roll(vi, sh, 0)
                ar, ai = co[2 * q], co[2 * q + 1]
                vr, vi = vr + ar * rr - ai * ri, vi + ar * ri + ai * rr
            edge = 0 if reverse else SUBLANES - 1
            cbr = jnp.broadcast_to(cr[edge:edge + 1, :], (SUBLANES, LANES))
            cbi = jnp.broadcast_to(ci[edge:edge + 1, :], (SUBLANES, LANES))
            pr, pi = co[6], co[7]
            vr, vi = vr + pr * cbr - pi * cbi, vi + pr * cbi + pi * cbr
            vre_ref[rows, lanes] = vr
            vim_ref[rows, lanes] = vi
            out += [vr, vi]
            if xre_ref is not None:
                nr = jnp.where(sub == SUBLANES - 1, cbr, pltpu.roll(vr, SUBLANES - 1, 0))
                ni = jnp.where(sub == SUBLANES - 1, cbi, pltpu.roll(vi, SUBLANES - 1, 0))
                xr = xre_ref[rows, lanes]
                xi = xim_ref[rows, lanes]
                out += [carry[per * j + 2] + nr * xr + ni * xi, carry[per * j + 3] + ni * xr - nr * xi]
        return tuple(out)

    zero = jnp.zeros((SUBLANES, LANES), F32)
    res = lax.fori_loop(0, nt, step, (zero,) * (per * nl))
    if xre_ref is None:
        return None
    return jnp.concatenate(
        [jnp.concatenate([jnp.sum(res[per * j + 2], axis=0, keepdims=True) for j in range(nl)], axis=1),
         jnp.concatenate([jnp.sum(res[per * j + 3], axis=0, keepdims=True) for j in range(nl)], axis=1)], axis=0)


def _s5_fwd(z, bre3, bim3, cre3, cim3, coef, dskip, *, nseq, seq, rider=None):
    nb = bre3.shape[0]
    ch, ns = bre3.shape[1], bre3.shape[2]

    def body(za_ref, bre_ref, bim_ref, cre_ref, cim_ref, coef_ref, d_ref, y_ref, xre_ref, xim_ref):
        za = za_ref[...]
        xre_ref[...] = _dot(za, bre_ref[...])
        xim_ref[...] = _dot(za, bim_ref[...])
        _s5_scan(xre_ref, xim_ref, coef_ref, seq, ns, False)
        y_ref[...] = _dot(xre_ref[...], cre_ref[...]) - _dot(xim_ref[...], cim_ref[...]) + d_ref[...] * za

    blk3 = lambda r, c: pl.BlockSpec((None, r, c), lambda b, j: (j, 0, 0))
    return _hosted_call(
        "s5_fwd", body, grid=(nseq, nb),
        in_specs=[pl.BlockSpec((seq, ch), lambda b, j: (b, j)), blk3(ch, ns), blk3(ch, ns), blk3(ns, ch), blk3(ns, ch),
                  pl.BlockSpec((8 * SUBLANES, ns), lambda b, j: (0, j)), pl.BlockSpec((1, ch), lambda b, j: (0, j))],
        out_specs=[pl.BlockSpec((seq, ch), lambda b, j: (b, j)), pl.BlockSpec((seq, ns), lambda b, j: (b, j)),
                   pl.BlockSpec((seq, ns), lambda b, j: (b, j))],
        out_shape=[jax.ShapeDtypeStruct((nseq * seq, nb * ch), F32), jax.ShapeDtypeStruct((nseq * seq, nb * ns), F32),
                   jax.ShapeDtypeStruct((nseq * seq, nb * ns), F32)],
        operands=(z, bre3, bim3, cre3, cim3, coef, dskip), rider=rider)


def _s5_bwd(dy, z, xre, xim, bre3, bim3, cre3, cim3, coef_rev, dskip, *, nseq, seq, rider=None):
    nb = bre3.shape[0]
    ch, ns = bre3.shape[1], bre3.shape[2]

    def body(dy_ref, za_ref, xre_ref, xim_ref, bre_ref, bim_ref, cre_ref, cim_ref, coef_ref, d_ref,
             dza_ref, dbre_ref, dbim_ref, dcre_ref, dcim_ref, dlam_ref, dd_ref, are_ref, aim_ref):
        dy = dy_ref[...]
        za = za_ref[...]
        are_ref[...] = _dot_nt(dy, cre_ref[...])
        aim_ref[...] = -_dot_nt(dy, cim_ref[...])
        dlam = _s5_scan(are_ref, aim_ref, coef_ref, seq, ns, True, xre_ref, xim_ref)
        are = are_ref[...]
        aim = aim_ref[...]
        dza_ref[...] = (_dot_nt(are, bre_ref[...]) + _dot_nt(aim, bim_ref[...]) + d_ref[...] * dy).astype(dza_ref.dtype)
        parts = (_dot_tn(za, are), _dot_tn(za, aim), _dot_tn(xre_ref[...], dy), -_dot_tn(xim_ref[...], dy),
                 dlam, jnp.sum(dy * za, axis=0, keepdims=True))
        first = pl.program_id(1) == 0
        for r, v in zip((dbre_ref, dbim_ref, dcre_ref, dcim_ref, dlam_ref, dd_ref), parts):
            @pl.when(first)
            def _():
                r[...] = v

            @pl.when(jnp.logical_not(first))
            def _():
                r[...] += v

    blk3 = lambda r, c: pl.BlockSpec((None, r, c), lambda j, b: (j, 0, 0))
    tok = lambda c: pl.BlockSpec((seq, c), lambda j, b: (b, j))
    return _hosted_call(
        "s5_bwd", body, grid=(nb, nseq),
        in_specs=[tok(ch), tok(ch), tok(ns), tok(ns), blk3(ch, ns), blk3(ch, ns), blk3(ns, ch), blk3(ns, ch),
                  pl.BlockSpec((8 * SUBLANES, ns), lambda j, b: (0, j)), pl.BlockSpec((1, ch), lambda j, b: (0, j))],
        out_specs=[tok(ch), blk3(ch, ns), blk3(ch, ns), blk3(ns, ch), blk3(ns, ch),
                   pl.BlockSpec((None, 2, ns), lambda j, b: (j, 0, 0)), pl.BlockSpec((1, ch), lambda j, b: (0, j))],
        out_shape=[jax.ShapeDtypeStruct((nseq * seq, nb * ch), BF16),
                   jax.ShapeDtypeStruct((nb, ch, ns), F32), jax.ShapeDtypeStruct((nb, ch, ns), F32),
                   jax.ShapeDtypeStruct((nb, ns, ch), F32), jax.ShapeDtypeStruct((nb, ns, ch), F32),
                   jax.ShapeDtypeStruct((nb, 2, ns), F32), jax.ShapeDtypeStruct((1, nb * ch), F32)],
        scratch_shapes=[pltpu.VMEM((seq, ns), F32), pltpu.VMEM((seq, ns), F32)],
        operands=(dy, z, xre, xim, bre3, bim3, cre3, cim3, coef_rev, dskip), rider=rider)


def _glu_fwd(y, wglu, bglu):
    t, w = y.shape
    tm = _row_tile(t, 1024)

    def body(y_ref, w_ref, b_ref, a0_ref, gl_ref, a_ref):
        a0 = _gelu(y_ref[...])
        gl = _dot(a0, w_ref[...])
        a0_ref[...] = a0.astype(a0_ref.dtype)
        gl_ref[...] = gl
        a_ref[...] = (a0 * _sigmoid(gl + b_ref[...])).astype(a_ref.dtype)

    tok = pl.BlockSpec((tm, w), lambda i: (i, 0))
    return pl.pallas_call(
        body, name="s5_glu", grid=(t // tm,),
        in_specs=[tok, pl.BlockSpec((w, w), lambda i: (0, 0)), pl.BlockSpec((1, w), lambda i: (0, 0))],
        out_specs=[tok, tok, tok],
        out_shape=[jax.ShapeDtypeStruct((t, w), BF16), jax.ShapeDtypeStruct((t, w), F32), jax.ShapeDtypeStruct((t, w), BF16)],
        compiler_params=_params("arbitrary"),
    )(y, wglu, bglu)


def _glu_bwd(y, gl, bglu, da, wglu):
    t, w = y.shape
    tm = _row_tile(t, 1024)

    def body(y_ref, gl_ref, b_ref, da_ref, w_ref, dgl_ref, dy_ref, db_ref):
        yv = y_ref[...]
        dav = da_ref[...]
        s = _sigmoid(gl_ref[...] + b_ref[...])
        dgl = dav * _gelu(yv) * s * (1.0 - s)
        dgl_ref[...] = dgl.astype(dgl_ref.dtype)
        dy_ref[...] = (dav * s + _dot_nt(dgl, w_ref[...])) * _gelu_grad(yv)
        part = jnp.sum(dgl, axis=0, keepdims=True)
        first = pl.program_id(0) == 0

        @pl.when(first)
        def _():
            db_ref[...] = part

        @pl.when(jnp.logical_not(first))
        def _():
            db_ref[...] += part

    tok = pl.BlockSpec((tm, w), lambda i: (i, 0))
    vec = pl.BlockSpec((1, w), lambda i: (0, 0))
    return pl.pallas_call(
        body, name="s5_glu_bwd", grid=(t // tm,),
        in_specs=[tok, tok, vec, tok, pl.BlockSpec((w, w), lambda i: (0, 0))], out_specs=[tok, tok, vec],
        out_shape=[jax.ShapeDtypeStruct((t, w), BF16), jax.ShapeDtypeStruct((t, w), F32), jax.ShapeDtypeStruct((1, w), F32)],
        compiler_params=_params("arbitrary"),
    )(y, gl, bglu, da, wglu)


def _cumsum_rows(x, reverse=False):
    n = x.shape[0]
    row = lax.broadcasted_iota(jnp.int32, x.shape, 0)
    s = 1
    while s < n:
        if reverse:
            x = x + jnp.where(row < n - s, pltpu.roll(x, n - s, 0), 0.0)
        else:
            x = x + jnp.where(row >= s, pltpu.roll(x, s, 0), 0.0)
        s *= 2
    return x


def _hg_gates(zq, zf, lb):
    sg = _sigmoid(zf)
    f = lb + (1.0 - lb) * sg
    sq = _sigmoid(zq)
    qa = zq * sq * (HEAD ** -0.5)
    b = _cumsum_rows(jnp.log(f))
    return sg, f, sq, qa, 1.0 - f, b


SUB = 16


def _hg_scores(qa, kk, b):
    c = qa.shape[0]
    row = lax.broadcasted_iota(jnp.int32, qa.shape, 0)
    pos = jnp.bitwise_and(row, SUB - 1)
    dmat = lax.broadcasted_iota(jnp.int32, (c, c), 0) - lax.broadcasted_iota(jnp.int32, (c, c), 1)
    p = jnp.zeros((c, c), F32)
    for d in range(SUB):
        if d == 0:
            fd = qa * kk
        else:
            e = jnp.exp(jnp.minimum(b - pltpu.roll(b, d, 0), 0.0))
            fd = jnp.where(pos >= d, qa * pltpu.roll(kk, d, 0) * e, 0.0)
        p = jnp.where(dmat == d, jnp.sum(fd, axis=1, keepdims=True), p)
    col = lax.broadcasted_iota(jnp.int32, (SUB, c), 1)
    blocks = [jnp.zeros((SUB, c), F32)]
    for r0 in range(SUB, c, SUB):
        beta = b[r0 - 1:r0, :]
        qt = qa[r0:r0 + SUB] * jnp.exp(b[r0:r0 + SUB] - beta)
        kt = kk * jnp.exp(jnp.minimum(beta - b, 0.0))
        blocks.append(jnp.where(col < r0, _dot_nt(qt, kt), 0.0))
    return p + jnp.concatenate(blocks, axis=0)


def _hg_scores_bwd(dp, qa, kk, b):
    c = qa.shape[0]
    row = lax.broadcasted_iota(jnp.int32, qa.shape, 0)
    pos = jnp.bitwise_and(row, SUB - 1)
    dmat = lax.broadcasted_iota(jnp.int32, (c, c), 0) - lax.broadcasted_iota(jnp.int32, (c, c), 1)
    dqa = jnp.zeros_like(qa)
    dkk = jnp.zeros_like(qa)
    db = jnp.zeros_like(qa)
    for d in range(SUB):
        dcol = jnp.sum(jnp.where(dmat == d, dp, 0.0), axis=1, keepdims=True)
        if d == 0:
            dqa = dqa + dcol * kk
            dkk = dkk + dcol * qa
        else:
            e = jnp.exp(jnp.minimum(b - pltpu.roll(b, d, 0), 0.0))
            w = jnp.where(pos >= d, dcol * e, 0.0)
            kr = pltpu.roll(kk, d, 0)
            dqa = dqa + w * kr
            tmp = w * qa
            dkk = dkk + pltpu.roll(tmp, c - d, 0)
            x = tmp * kr
            db = db + x - pltpu.roll(x, c - d, 0)
    col = lax.broadcasted_iota(jnp.int32, (SUB, c), 1)
    dq_blocks = [jnp.zeros((SUB, qa.shape[1]), F32)]
    db_blocks = [jnp.zeros((SUB, qa.shape[1]), F32)]
    for r0 in range(SUB, c, SUB):
        beta = b[r0 - 1:r0, :]
        eq = jnp.exp(b[r0:r0 + SUB] - beta)
        ek = jnp.exp(jnp.minimum(beta - b, 0.0))
        qt = qa[r0:r0 + SUB] * eq
        kt = kk * ek
        dpi = jnp.where(col < r0, dp[r0:r0 + SUB, :], 0.0)
        dqt = _dot(dpi, kt)
        dkt = _dot_tn(dpi, qt)
        dq_blocks.append(dqt * eq)
        db_blocks.append(dqt * qt)
        dkk = dkk + dkt * ek
        db = db - dkt * kt
    return dqa + jnp.concatenate(dq_blocks, axis=0), dkk, db + jnp.concatenate(db_blocks, axis=0)


def _hg_chunks_per_step(seq):
    nc = seq // CHUNK
    cps = next(k for k in (4, 2, 1) if nc % k == 0)
    return nc, cps, nc // cps


def _hg_fwd(z, lbrow, gain, *, nseq, seq, heads, qoff, rider=None):
    nc, cps, nblk = _hg_chunks_per_step(seq)
    blk = cps * CHUNK
    zspec = lambda off: pl.BlockSpec((blk, HEAD), lambda h, b, n, off=off: (b * nblk + n, off + h))

    def body(zq_ref, zf_ref, zi_ref, zg_ref, lb_ref, gn_ref, o_ref, yb_ref, st_ref, sc_ref, state):
        @pl.when(pl.program_id(2) == 0)
        def _():
            state[...] = jnp.zeros_like(state)

        lb = lb_ref[...]
        gain_v = gn_ref[...]

        def chunk(ci, carry):
            rows = pl.ds(pl.multiple_of(ci * CHUNK, CHUNK), CHUNK)
            st = state[...]
            st_ref[ci] = st
            zi = zi_ref[rows, :]
            zg = zg_ref[rows, :]
            _, _, _, qa, kk, b = _hg_gates(zq_ref[rows, :], zf_ref[rows, :], lb)
            scores = _hg_scores(qa, kk, b).astype(BF16)
            sc_ref[rows, :] = scores
            o = _dot_nt(qa * jnp.exp(b), st) + _dot(scores, zi)
            bl = b[CHUNK - 1:CHUNK, :]
            state[...] = st * jnp.exp(bl) + _dot_tn(zi, kk * jnp.exp(bl - b))
            o_ref[rows, :] = o
            r = lax.rsqrt(jnp.mean(o * o, axis=1, keepdims=True) + EPS)
            yb_ref[rows, :] = (o * r * gain_v * zg * _sigmoid(zg)).astype(yb_ref.dtype)
            return carry

        lax.fori_loop(0, cps, chunk, 0, unroll=True)

    tok = pl.BlockSpec((blk, HEAD), lambda h, b, n: (b * nblk + n, h))
    vec = pl.BlockSpec((1, HEAD), lambda h, b, n: (0, h))
    rows = nseq * seq
    return _hosted_call(
        "hgrn2_fwd", body, grid=(heads, nseq, nblk),
        in_specs=[zspec(qoff), zspec(qoff + heads), zspec(qoff + 2 * heads), zspec(qoff + 3 * heads), vec, vec],
        out_specs=[tok, tok, pl.BlockSpec((None, None, cps, HEAD, HEAD), lambda h, b, n: (h, b, n, 0, 0)),
                   pl.BlockSpec((None, blk, CHUNK), lambda h, b, n: (h, b * nblk + n, 0))],
        out_shape=[jax.ShapeDtypeStruct((rows, heads * HEAD), F32), jax.ShapeDtypeStruct((rows, heads * HEAD), BF16),
                   jax.ShapeDtypeStruct((heads, nseq, nc, HEAD, HEAD), F32),
                   jax.ShapeDtypeStruct((heads, rows, CHUNK), BF16)],
        scratch_shapes=[pltpu.VMEM((HEAD, HEAD), F32)], operands=(z, z, z, z, lbrow, gain), rider=rider)


def _hg_bwd(dyb, z, o, states, scores, lbrow, gain, *, nseq, seq, heads, qoff, rider=None):
    nc, cps, nblk = _hg_chunks_per_step(seq)
    blk = cps * CHUNK
    rev = lambda n: nblk - 1 - n
    zspec = lambda off: pl.BlockSpec((blk, HEAD), lambda h, b, n, off=off: (b * nblk + rev(n), off + h))

    def body(dyb_ref, zq_ref, zf_ref, zi_ref, zg_ref, o_ref, st_ref, sc_ref, lb_ref, gn_ref,
             dzq_ref, dzf_ref, dzi_ref, dzg_ref, dlb_ref, dgn_ref, dstate):
        @pl.when(pl.program_id(2) == 0)
        def _():
            dstate[...] = jnp.zeros_like(dstate)

        @pl.when(jnp.logical_and(pl.program_id(1) == 0, pl.program_id(2) == 0))
        def _():
            dlb_ref[...] = jnp.zeros_like(dlb_ref)
            dgn_ref[...] = jnp.zeros_like(dgn_ref)

        lb = lb_ref[...]
        gain_v = gn_ref[...]
        c = CHUNK
        causal = lax.broadcasted_iota(jnp.int32, (c, c), 0) >= lax.broadcasted_iota(jnp.int32, (c, c), 1)

        def chunk(step, carry):
            ci = cps - 1 - step
            rows = pl.ds(pl.multiple_of(ci * CHUNK, CHUNK), CHUNK)
            zq = zq_ref[rows, :]
            zi = zi_ref[rows, :]
            zg = zg_ref[rows, :]
            sg, f, sq, qa, kk, b = _hg_gates(zq, zf_ref[rows, :], lb)
            eb = jnp.exp(b)
            qt = qa * eb
            bl = b[c - 1:c, :]
            ebl = jnp.exp(bl)
            ekb = jnp.exp(bl - b)
            kh = kk * ekb
            st = st_ref[ci]
            dst = dstate[...]
            o = o_ref[rows, :]
            r = lax.rsqrt(jnp.mean(o * o, axis=1, keepdims=True) + EPS)
            oh = o * r
            szg = _sigmoid(zg)
            dyb = dyb_ref[rows, :]
            don = dyb * zg * szg
            dzg_ref[rows, :] = (dyb * oh * gain_v * szg * (1.0 + zg * (1.0 - szg))).astype(dzg_ref.dtype)
            doh = don * gain_v
            do = r * (doh - oh * jnp.mean(doh * oh, axis=1, keepdims=True))
            dqt = _dot(do, st)
            dp = jnp.where(causal, _dot_nt(do, zi), 0.0)
            dzi_ref[rows, :] = (_dot_tn(sc_ref[rows, :], do) + _dot_nt(kh, dst)).astype(dzi_ref.dtype)
            dkh = _dot(zi, dst)
            dbl = jnp.sum(dkh * kh, axis=0, keepdims=True) + jnp.sum(dst * st, axis=0, keepdims=True) * ebl
            dstate[...] = _dot_tn(do, qt) + dst * ebl
            dqa_s, dkk_s, db_s = _hg_scores_bwd(dp, qa, kk, b)
            dqa = dqt * eb + dqa_s
            dkk = dkh * ekb + dkk_s
            db = dqt * qt - dkh * kh + db_s
            row = lax.broadcasted_iota(jnp.int32, db.shape, 0)
            db = db + jnp.where(row == c - 1, dbl, 0.0)
            df = _cumsum_rows(db, reverse=True) / f - dkk
            dzf_ref[rows, :] = (df * (1.0 - lb) * sg * (1.0 - sg)).astype(dzf_ref.dtype)
            dzq_ref[rows, :] = (dqa * (HEAD ** -0.5) * sq * (1.0 + zq * (1.0 - sq))).astype(dzq_ref.dtype)
            dlb_ref[...] += jnp.sum(df * (1.0 - sg), axis=0, keepdims=True)
            dgn_ref[...] += jnp.sum(don * oh, axis=0, keepdims=True)
            return carry

        lax.fori_loop(0, cps, chunk, 0, unroll=2)

    tok = pl.BlockSpec((blk, HEAD), lambda h, b, n: (b * nblk + rev(n), h))
    vec = pl.BlockSpec((1, HEAD), lambda h, b, n: (0, h))
    rows = nseq * seq
    return _hosted_call(
        "hgrn2_bwd", body, grid=(heads, nseq, nblk),
        in_specs=[tok, zspec(qoff), zspec(qoff + heads), zspec(qoff + 2 * heads), zspec(qoff + 3 * heads), tok,
                  pl.BlockSpec((None, None, cps, HEAD, HEAD), lambda h, b, n: (h, b, rev(n), 0, 0)),
                  pl.BlockSpec((None, blk, CHUNK), lambda h, b, n: (h, b * nblk + rev(n), 0)), vec, vec],
        out_specs=[tok, tok, tok, tok, vec, vec],
        out_shape=[jax.ShapeDtypeStruct((rows, heads * HEAD), BF16)] * 4
        + [jax.ShapeDtypeStruct((1, heads * HEAD), F32)] * 2,
        scratch_shapes=[pltpu.VMEM((HEAD, HEAD), F32)],
        operands=(dyb, z, z, z, z, o, states, scores, lbrow, gain), rider=rider)


def _conv_taps(h, w, bias):
    row = lax.broadcasted_iota(jnp.int32, h.shape, 0)
    h1 = jnp.where(row >= 1, pltpu.roll(h, 1, 0), 0.0)
    h2 = jnp.where(row >= 2, pltpu.roll(h, 2, 0), 0.0)
    return h2 * w[0:1, :] + h1 * w[1:2, :] + h * w[2:3, :] + bias, h1, h2


def _conv_fwd(h, wconv, bconv, *, nseq, seq):
    ff2 = h.shape[1]
    ncol = ff2 // 2 // LANES

    def body(hg_ref, hv_ref, wg_ref, wv_ref, bg_ref, bv_ref, a_ref):
        g, _, _ = _conv_taps(hg_ref[...].astype(F32), wg_ref[...], bg_ref[...])
        v, _, _ = _conv_taps(hv_ref[...].astype(F32), wv_ref[...], bv_ref[...])
        a_ref[...] = (g * _sigmoid(g) * v).astype(a_ref.dtype)

    tok = lambda off: pl.BlockSpec((seq, LANES), lambda j, b, off=off: (b, off + j))
    wsp = lambda off: pl.BlockSpec((CONV_W, LANES), lambda j, b, off=off: (0, off + j))
    bsp = lambda off: pl.BlockSpec((1, LANES), lambda j, b, off=off: (0, off + j))
    return pl.pallas_call(
        body, name="conv_fwd", grid=(ncol, nseq),
        in_specs=[tok(0), tok(ncol), wsp(0), wsp(ncol), bsp(0), bsp(ncol)],
        out_specs=tok(0), out_shape=jax.ShapeDtypeStruct((nseq * seq, ff2 // 2), BF16),
        compiler_params=_params("arbitrary", "arbitrary"),
    )(h, h, wconv, wconv, bconv, bconv)


def _conv_bwd(da, h, wconv, bconv, *, nseq, seq):
    ff2 = h.shape[1]
    ncol = ff2 // 2 // LANES

    def half_bwd(d, hcur, h1, h2, w):
        n = d.shape[0]
        row = lax.broadcasted_iota(jnp.int32, d.shape, 0)
        d1 = jnp.where(row < n - 1, pltpu.roll(d, n - 1, 0), 0.0)
        d2 = jnp.where(row < n - 2, pltpu.roll(d, n - 2, 0), 0.0)
        dh = d * w[2:3, :] + d1 * w[1:2, :] + d2 * w[0:1, :]
        stats = jnp.concatenate(
            [jnp.sum(h2 * d, axis=0, keepdims=True), jnp.sum(h1 * d, axis=0, keepdims=True),
             jnp.sum(hcur * d, axis=0, keepdims=True), jnp.sum(d, axis=0, keepdims=True),
             jnp.zeros((SUBLANES - 4, d.shape[1]), F32)], axis=0)
        return dh, stats

    def body(da_ref, hg_ref, hv_ref, wg_ref, wv_ref, bg_ref, bv_ref, dhg_ref, dhv_ref, sg_ref, sv_ref):
        hg = hg_ref[...].astype(F32)
        hv = hv_ref[...].astype(F32)
        wg = wg_ref[...]
        wv = wv_ref[...]
        g, g1, g2 = _conv_taps(hg, wg, bg_ref[...])
        v, v1, v2 = _conv_taps(hv, wv, bv_ref[...])
        da = da_ref[...].astype(F32)
        s = _sigmoid(g)
        dhg, stg = half_bwd(da * v * s * (1.0 + g * (1.0 - s)), hg, g1, g2, wg)
        dhv, stv = half_bwd(da * g * s, hv, v1, v2, wv)
        dhg_ref[...] = dhg.astype(dhg_ref.dtype)
        dhv_ref[...] = dhv.astype(dhv_ref.dtype)
        first = pl.program_id(1) == 0
        for r, val in ((sg_ref, stg), (sv_ref, stv)):
            @pl.when(first)
            def _():
                r[...] = val

            @pl.when(jnp.logical_not(first))
            def _():
                r[...] += val

    tok = lambda off: pl.BlockSpec((seq, LANES), lambda j, b, off=off: (b, off + j))
    wsp = lambda off: pl.BlockSpec((CONV_W, LANES), lambda j, b, off=off: (0, off + j))
    bsp = lambda off: pl.BlockSpec((1, LANES), lambda j, b, off=off: (0, off + j))
    ssp = pl.BlockSpec((SUBLANES, LANES), lambda j, b: (0, j))
    dhg, dhv, stg, stv = pl.pallas_call(
        body, name="conv_bwd", grid=(ncol, nseq),
        in_specs=[tok(0), tok(0), tok(ncol), wsp(0), wsp(ncol), bsp(0), bsp(ncol)],
        out_specs=[tok(0), tok(0), ssp, ssp],
        out_shape=[jax.ShapeDtypeStruct((nseq * seq, ff2 // 2), BF16)] * 2
        + [jax.ShapeDtypeStruct((SUBLANES, ff2 // 2), F32)] * 2,
        compiler_params=_params("arbitrary", "arbitrary"),
    )(da, h, h, wconv, wconv, bconv, bconv)
    return (dhg, dhv), jnp.concatenate([stg, stv], axis=1)


def _rms_fwd(xv, g):
    r = lax.rsqrt(jnp.mean(xv * xv, axis=1, keepdims=True) + EPS)
    return (xv * r * g,)


def _rms_bwd(xv, g, dy, res):
    r = lax.rsqrt(jnp.mean(xv * xv, axis=1, keepdims=True) + EPS)
    xh = xv * r
    dxh = dy * g
    dx = r * (dxh - xh * jnp.mean(dxh * xh, axis=1, keepdims=True)) + res
    return dx, jnp.sum(dy * xh, axis=0, keepdims=True)


def _loss_head(x2, tgt, g):
    d = x2.shape[1]
    r = lax.rsqrt(jnp.mean(x2 * x2, axis=1, keepdims=True) + EPS)
    xh = x2 * r
    err = xh * g - tgt
    dy = err * (1.0 / d)
    dxh = dy * g
    dx = r * (dxh - xh * jnp.mean(dxh * xh, axis=1, keepdims=True))
    loss = 0.5 * jnp.sum(jnp.mean(err * err, axis=1, keepdims=True), axis=0, keepdims=True)
    return dx, jnp.sum(dy * xh, axis=0, keepdims=True), jnp.broadcast_to(loss, (1, LANES))


LATE_A = ("w_down", "w_out")
LATE_B = ("w_up", "w_pa", "w_pb")
LATE = LATE_A + LATE_B
EARLY_GRADS = ("w_down", "w_up", "w_out", "w_pa", "w_pb", "w_glu")
ROW_SHARDED = ("w_glu", "w_out", "w_down")


def _local_step(x, tgt, p, first, landed_first, late, *, nseq, seq):
    p = dict(p)
    chip = 2 * lax.axis_index("x") + lax.axis_index("y")
    t, d = x.shape
    s5w = p["s5_d"].shape[1]
    hgw = p["gain"].shape[1]
    heads = hgw // HEAD
    qoff = s5w // LANES
    gblk = (s5w + 4 * hgw) // GATE_BLOCK
    ngb = d // GATE_BLOCK
    tm = _row_tile(t, 256)
    row = lambda a, w=None, base=0: (a, a.shape[1] if w is None else w, base, "row")
    vec = lambda a, w=None, base=0: (a, a.shape[1] if w is None else w, base, "vec")
    rw = functools.partial(_rowwise, rows=t, tm=tm)

    (u,), got = rw("rms_mix", _rms_fwd, [row(x), vec(p["g_mix"])], [(d, d, BF16)], rider=_gather_pass_rider(landed_first))
    w_in_all, w_glu_all, conv_all = [lax.dynamic_update_index_in_dim(g, s, chip, 0) for g, s in zip(got, first)]
    p["w_in"] = w_in_all
    p["w_glu"] = w_glu_all.reshape(-1, w_glu_all.shape[-1])
    p["w_conv"] = conv_all[:, :CONV_W].transpose(1, 0, 2).reshape(CONV_W, -1)
    z, landed_a = _mm_fwd_cols("in_proj", u, p["w_in"], rider=_gather_ici_rider([late[n] for n in LATE_A]))

    lam_re, lam_im, bb_re, bb_im = _s5_discretize(p["s5_a_re"], p["s5_a_im"], p["s5_log_dt"], p["s5_b_re"], p["s5_b_im"])
    bre3 = _s5_in_blocks(bb_re).astype(BF16)
    bim3 = _s5_in_blocks(bb_im).astype(BF16)
    cre3 = _s5_out_blocks(p["s5_c_re"]).astype(BF16)
    cim3 = _s5_out_blocks(p["s5_c_im"]).astype(BF16)
    coef_f = _s5_scan_tables(lam_re.reshape(-1), lam_im.reshape(-1), False)
    coef_r = _s5_scan_tables(lam_re.reshape(-1), lam_im.reshape(-1), True)
    (o, yb, states, scores), landed_b = _hg_fwd(z, p["lbrow"], p["gain"], nseq=nseq, seq=seq, heads=heads, qoff=qoff,
                                                rider=_gather_ici_rider([late[n] for n in LATE_B]))
    (y5, xre, xim), gathered = _s5_fwd(z, bre3, bim3, cre3, cim3, coef_f, p["s5_d"], nseq=nseq, seq=seq,
                                       rider=_gather_pass_rider(list(landed_a) + list(landed_b)))
    for n, g in zip(LATE, gathered):
        full = lax.dynamic_update_index_in_dim(g, late[n], chip, 0)
        p[n] = full.reshape(-1, full.shape[-1]) if n in ROW_SHARDED else full
    ya0, gl, ya = _glu_fwd(y5, p["w_glu"], p["b_glu"])

    joined = lambda w3: w3.transpose(1, 0, 2).reshape(w3.shape[1], -1)
    split = lambda g: g.reshape(g.shape[0], N_CHIPS, -1).transpose(1, 0, 2)
    wpa, wpb = joined(p["w_pa"]), joined(p["w_pb"])
    pa = _mm_fwd_rows("proj_a", ya, wpa, out_dtype=BF16)
    pb = _mm_fwd_rows("proj_b", yb, wpb, out_dtype=BF16)
    gb = GATE_BLOCK
    (m,) = rw("merge", lambda ga, gbv, a, b: (_sigmoid(ga) * a + _sigmoid(gbv) * b,),
              [row(z, gb, gblk), row(z, gb, gblk + ngb), row(pa, gb), row(pb, gb)], [(d, gb, BF16)], ncol=ngb)
    x1 = _mm_fwd_rows("out_proj", m, p["w_out"], res=x)

    (u2,) = rw("rms_ffn", _rms_fwd, [row(x1), vec(p["g_ffn"])], [(d, d, BF16)])
    h = _mm_fwd_cols("up_proj", u2, p["w_up"], out_dtype=BF16)
    a = _conv_fwd(h, p["w_conv"], p["b_conv"], nseq=nseq, seq=seq)
    dx2, dg_final, lossv = _mm_fwd_rows("down_proj", a, p["w_down"], res=x1,
                                        epilogue=(_loss_head, [tgt, p["g_final"]], [d, LANES]))

    norm_bwd = lambda dyv, xv, g, resv: _rms_bwd(xv, g, dyv, resv)
    da = _mm_bwd_rows("down_bwd", dx2, p["w_down"], out_dtype=BF16)
    g_wdown = _mm_wgrad_rows("down_wgrad", a, dx2)
    dh, cstats = _conv_bwd(da, h, p["w_conv"], p["b_conv"], nseq=nseq, seq=seq)
    dx1, dg_ffn = _mm_bwd_cols("up_bwd", dh, p["w_up"], epilogue=(norm_bwd, [x1, p["g_ffn"], dx2], [d]))
    g_wup = _mm_wgrad_cols("up_wgrad", u2, dh)

    dm = _mm_bwd_rows("out_bwd", dx1, p["w_out"], out_dtype=BF16)
    g_wout = _mm_wgrad_rows("out_wgrad", m, dx1)

    def merge_bwd(ga, gbv, av, bv, dmv):
        sa = _sigmoid(ga)
        sb = _sigmoid(gbv)
        return dmv * sa, dmv * sb, dmv * av * sa * (1.0 - sa), dmv * bv * sb * (1.0 - sb)

    dpa, dpb, dzga, dzgb = rw("merge_bwd", merge_bwd,
                              [row(z, gb, gblk), row(z, gb, gblk + ngb), row(pa, gb), row(pb, gb), row(dm, gb)],
                              [(d, gb, BF16)] * 4, ncol=ngb)
    dya = _mm_bwd_rows("proj_a_bwd", dpa, wpa)
    g_wpa = split(_mm_wgrad_rows("proj_a_wgrad", ya, dpa))
    dyb = _mm_bwd_rows("proj_b_bwd", dpb, wpb)
    g_wpb = split(_mm_wgrad_rows("proj_b_wgrad", yb, dpb))

    dgl, dy5, db_glu = _glu_bwd(y5, gl, p["b_glu"], dya, p["w_glu"])
    g_wglu = _mm_wgrad_rows("glu_wgrad", ya0, dgl)
    partial = dict(w_down=g_wdown, w_up=g_wup, w_out=g_wout, w_pa=g_wpa, w_pb=g_wpb, w_glu=g_wglu)
    parts = [_grad_parts(partial[n]) for n in EARLY_GRADS]
    (dza, dbre3, dbim3, dcre3, dcim3, dlam, dd), sib = _s5_bwd(
        dy5, z, xre, xim, bre3, bim3, cre3, cim3, coef_r, p["s5_d"], nseq=nseq, seq=seq, rider=_swap_halves_rider(parts))
    pair = _pair_sums(EARLY_GRADS, parts, sib)
    (dzq, dzf, dzi, dzg, dlb, dgain), others = _hg_bwd(
        dyb, z, o, states, scores, p["lbrow"], p["gain"], nseq=nseq, seq=seq, heads=heads, qoff=qoff,
        rider=_scatter_rider(pair))
    halves = _chip_sums(EARLY_GRADS, pair, others)

    dz = jnp.concatenate([dza, dzq, dzf, dzi, dzg, dzga, dzgb], axis=1)
    dx, dg_mix = _mm_bwd_cols("in_bwd", dz, p["w_in"], epilogue=(norm_bwd, [x, p["g_mix"], dx1], [d]))

    gshape = lam_re.shape
    small = {
        "loss": lossv, "g_mix": dg_mix, "g_ffn": dg_ffn, "g_final": dg_final, "b_glu": db_glu, "gain": dgain,
        "lbrow": dlb, "s5_d": dd, "w_conv": cstats[0:CONV_W], "b_conv": cstats[CONV_W:CONV_W + 1],
        "lam_re": dlam[:, 0, :].reshape(gshape), "lam_im": dlam[:, 1, :].reshape(gshape),
        "bb_re": _s5_in_blocks_diag(dbre3), "bb_im": _s5_in_blocks_diag(dbim3),
        "s5_c_re": _s5_out_blocks_diag(dcre3), "s5_c_im": _s5_out_blocks_diag(dcim3),
    }
    small_vec = _pack([small[n] for n in SMALL_PARTS], F32)
    g_win, (small_all, *sibs) = _mm_wgrad_cols(
        "in_wgrad", u, dz, rider=_merge_riders(_gather_all_rider(small_vec), _swap_sums_rider(halves)))
    big = dict(zip(EARLY_GRADS, zip(halves, sibs)))
    small_sum = _sum_over_devices("small_grad_sum", small_vec, small_all)
    sm = dict(zip(SMALL_PARTS, _unpack(small_sum, [small[n].shape for n in SMALL_PARTS])))
    last = [_grad_parts(g_win)]
    pair = _pair_sums(("w_in",), last, _run_rider("grad_swap_halves", _swap_halves_rider(last)))
    (half,) = _chip_sums(("w_in",), pair, _run_rider("grad_scatter_chips", _scatter_rider(pair)))
    mid = half.shape[0] // 2
    sib_half = jnp.concatenate(_run_rider("grad_swap_sums", _swap_sums_rider([half[:mid], half[mid:]])), axis=0)
    big["w_in"] = (half, sib_half)
    return dx, big, sm


ANY = pl.BlockSpec(memory_space=pl.ANY)


def _place():
    x, y, c = lax.axis_index("x"), lax.axis_index("y"), lax.axis_index("c")
    chips = [(1 - x, y), (x, 1 - y), (1 - x, 1 - y)]
    return x, y, c, chips


def _remote(src, dst, send_sems, recv_sems, k, to):
    return pltpu.make_async_remote_copy(src_ref=src, dst_ref=dst, send_sem=send_sems.at[k], recv_sem=recv_sems.at[k],
                                        device_id=to, device_id_type=MESH)


def _half(rows, which):
    return pl.ds(pl.multiple_of(which * (rows // 2), SUBLANES), rows // 2)


class _SemView:
    def __init__(self, base, offset):
        self.base, self.offset = base, offset

    @property
    def at(self):
        return self

    def __getitem__(self, k):
        return self.base.at[self.offset + k]


def _merge_riders(first, second):
    na, no, ns = len(first.arrays), len(first.out_shapes), first.nsem

    def split(fn_a, fn_b):
        def run(ins, outs, send_sems, recv_sems):
            fn_a(ins[:na], outs[:no], send_sems, recv_sems)
            fn_b(ins[na:], outs[no:], _SemView(send_sems, ns), _SemView(recv_sems, ns))
        return run

    aliases = dict(first.aliases)
    aliases.update({na + i: no + o for i, o in second.aliases.items()})
    return _Rider(first.arrays + second.arrays, first.out_shapes + second.out_shapes, ns + second.nsem,
                  split(first.start, second.start), split(first.finish, second.finish), aliases)


def _cast_bf16(arrays, rider=None):
    n = len(arrays)

    def body(*refs):
        for i in range(n):
            refs[n + i][...] = refs[i][...].astype(BF16)

    vm = pl.BlockSpec(memory_space=pltpu.VMEM)
    return _hosted_call("cast_weights", body, grid=(1,), in_specs=[vm] * n, out_specs=[vm] * n,
                        out_shape=[jax.ShapeDtypeStruct(a.shape, BF16) for a in arrays], operands=arrays, rider=rider)


def _symmetric_rider(arrays, out_shapes, copies_of, nsem):
    def start(ins, outs, send_sems, recv_sems):
        for cp in copies_of(ins, outs, send_sems, recv_sems):
            cp.start()

    def finish(ins, outs, send_sems, recv_sems):
        for cp in copies_of(ins, outs, send_sems, recv_sems):
            cp.wait()

    return _Rider(arrays, out_shapes, nsem, start, finish)


def _swap_halves_rider(parts):
    def copies_of(ins, outs, send_sems, recv_sems):
        x, y, c, _ = _place()
        return [_remote(ins[a].at[:, _half(g.shape[1], 1 - c), :], outs[a], send_sems, recv_sems, a, (x, y, 1 - c))
                for a, g in enumerate(parts)]

    shapes = [jax.ShapeDtypeStruct((g.shape[0], g.shape[1] // 2, g.shape[2]), g.dtype) for g in parts]
    return _symmetric_rider(parts, shapes, copies_of, len(parts))


def _scatter_rider(parts):
    def copies_of(ins, outs, send_sems, recv_sems):
        x, y, c, chips = _place()
        return [_remote(ins[a].at[2 * cx + cy], outs[a].at[j], send_sems, recv_sems, 3 * a + j, (cx, cy, c))
                for a in range(len(parts)) for j, (cx, cy) in enumerate(chips)]

    shapes = [jax.ShapeDtypeStruct((N_CHIPS - 1,) + h.shape[1:], h.dtype) for h in parts]
    return _symmetric_rider(parts, shapes, copies_of, 3 * len(parts))


def _swap_sums_rider(parts):
    def copies_of(ins, outs, send_sems, recv_sems):
        x, y, c, _ = _place()
        return [_remote(ins[a], outs[a], send_sems, recv_sems, a, (x, y, 1 - c)) for a in range(len(parts))]

    shapes = [jax.ShapeDtypeStruct(g.shape, g.dtype) for g in parts]
    return _symmetric_rider(parts, shapes, copies_of, len(parts))


def _gather_ici_rider(shards):
    def sends(ins, outs, send_sems, recv_sems):
        x, y, c, chips = _place()
        return [_remote(ins[a].at[_half(s.shape[0], c)], outs[a].at[2 * x + y, _half(s.shape[0], c)], send_sems,
                        recv_sems, 3 * a + j, (cx, cy, c)) for a, s in enumerate(shards) for j, (cx, cy) in enumerate(chips)]

    def start(ins, outs, send_sems, recv_sems):
        for cp in sends(ins, outs, send_sems, recv_sems):
            cp.start()

    def finish(ins, outs, send_sems, recv_sems):
        x, y, c, chips = _place()
        for a, s in enumerate(shards):
            for j, (cx, cy) in enumerate(chips):
                landed = outs[a].at[2 * cx + cy, _half(s.shape[0], c)]
                _remote(landed, landed, send_sems, recv_sems, 3 * a + j, (x, y, c)).wait_recv()
        for cp in sends(ins, outs, send_sems, recv_sems):
            cp.wait_send()

    shapes = [jax.ShapeDtypeStruct((N_CHIPS,) + s.shape, s.dtype) for s in shards]
    return _Rider(shards, shapes, 3 * len(shards), start, finish)


def _gather_pass_rider(landed):
    def sends(ins, outs, send_sems, recv_sems):
        x, y, c, chips = _place()
        return [_remote(ins[a].at[2 * cx + cy, _half(g.shape[1], c)], outs[a].at[2 * cx + cy, _half(g.shape[1], c)],
                        send_sems, recv_sems, 3 * a + j, (x, y, 1 - c))
                for a, g in enumerate(landed) for j, (cx, cy) in enumerate(chips)]

    def start(ins, outs, send_sems, recv_sems):
        for cp in sends(ins, outs, send_sems, recv_sems):
            cp.start()

    def finish(ins, outs, send_sems, recv_sems):
        x, y, c, chips = _place()
        for a, g in enumerate(landed):
            for j, (cx, cy) in enumerate(chips):
                other = outs[a].at[2 * cx + cy, _half(g.shape[1], 1 - c)]
                _remote(other, other, send_sems, recv_sems, 3 * a + j, (x, y, c)).wait_recv()
        for cp in sends(ins, outs, send_sems, recv_sems):
            cp.wait_send()

    shapes = [jax.ShapeDtypeStruct(g.shape, g.dtype) for g in landed]
    return _Rider(landed, shapes, 3 * len(landed), start, finish, aliases={a: a for a in range(len(landed))})


def _grad_parts(g):
    return g.reshape((N_CHIPS, -1, g.shape[-1]))


def _place_scalars():
    return jnp.stack([lax.axis_index("c"), 2 * lax.axis_index("x") + lax.axis_index("y")]).astype(jnp.int32)


def _scalar_call(body, name, grid, in_specs, out_specs, out_shape, operands):
    spec = pltpu.PrefetchScalarGridSpec(num_scalar_prefetch=1, grid=grid, in_specs=in_specs, out_specs=out_specs)
    return pl.pallas_call(body, name=name, grid_spec=spec, out_shape=out_shape,
                          compiler_params=_params(*(["arbitrary"] * len(grid))))(_place_scalars(), *operands)


def _pair_sums(names, parts, sib):
    out = []
    for n, g, s in zip(names, parts, sib):
        rh, cols = s.shape[1], s.shape[2]
        tm = _row_tile(rh, 512)
        nblk = rh // tm

        def body(place, g_ref, s_ref, o_ref):
            o_ref[...] = (g_ref[...].astype(F32) + s_ref[...].astype(F32)).astype(o_ref.dtype)

        blk = pl.BlockSpec((None, tm, cols), lambda j, i, place: (j, i, 0))
        own = pl.BlockSpec((None, tm, cols), lambda j, i, place, nblk=nblk: (j, place[0] * nblk + i, 0))
        out.append(_scalar_call(body, "grad_pair_sum_" + n, (N_CHIPS, nblk), [own, blk], blk,
                                jax.ShapeDtypeStruct(s.shape, BF16), (g, s)))
    return out


def _chip_sums(names, pair, others):
    out = []
    for n, h, o in zip(names, pair, others):
        rh, cols = h.shape[1], h.shape[2]
        tm = _row_tile(rh, 512)

        def body(place, h_ref, a_ref, b_ref, c_ref, o_ref):
            o_ref[...] = (h_ref[...].astype(F32) + a_ref[...].astype(F32)) + b_ref[...].astype(F32) + c_ref[...].astype(F32)

        mine = pl.BlockSpec((None, tm, cols), lambda i, place: (place[1], i, 0))
        other = lambda k: pl.BlockSpec((None, tm, cols), lambda i, place, k=k: (k, i, 0))
        out.append(_scalar_call(body, "grad_chip_sum_" + n, (rh // tm,), [mine, other(0), other(1), other(2)],
                                pl.BlockSpec((tm, cols), lambda i, place: (i, 0)), jax.ShapeDtypeStruct((rh, cols), F32),
                                (h, o, o, o)))
    return out


def _adamw_halves(name, w, m, v, own, sib):
    rh, cols = own.shape
    tm = _row_tile(rh, 256)
    nblk = rh // tm

    def body(place, w_ref, m_ref, v_ref, own_ref, sib_ref, g_ref, d_ref, m2_ref, v2_ref):
        mine = pl.program_id(0) // nblk == place[0]

        def run(gv):
            g_ref[...] = gv
            d_ref[...], m2_ref[...], v2_ref[...] = _adamw_math(w_ref[...], gv, m_ref[...], v_ref[...])

        @pl.when(mine)
        def _():
            run(own_ref[...])

        @pl.when(jnp.logical_not(mine))
        def _():
            run(sib_ref[...])

    full = pl.BlockSpec((tm, cols), lambda i, place: (i, 0))
    own_spec = pl.BlockSpec((tm, cols), lambda i, place: (jnp.where(i // nblk == place[0], i % nblk, 0), 0))
    sib_spec = pl.BlockSpec((tm, cols), lambda i, place: (jnp.where(i // nblk == place[0], 0, i % nblk), 0))
    return _scalar_call(body, name, (2 * nblk,), [full, full, full, own_spec, sib_spec], [full] * 4,
                        [jax.ShapeDtypeStruct((2 * rh, cols), F32)] * 4, (w, m, v, own, sib))


def _gather_all_rider(v):
    m_per = v.shape[0]

    def rows(ref, px, py, pc):
        return ref.at[pl.ds(pl.multiple_of((4 * px + 2 * py + pc) * m_per, 8), m_per)]

    def first(ins, outs, send_sems, recv_sems):
        x, y, c, chips = _place()
        mine = rows(outs[0], x, y, c)
        return [_remote(ins[0], mine, send_sems, recv_sems, 0, (x, y, 1 - c))] + [
            _remote(ins[0], mine, send_sems, recv_sems, 1 + j, (cx, cy, c)) for j, (cx, cy) in enumerate(chips)]

    def start(ins, outs, send_sems, recv_sems):
        for cp in first(ins, outs, send_sems, recv_sems):
            cp.start()

    def finish(ins, outs, send_sems, recv_sems):
        x, y, c, chips = _place()
        passed = []
        for j, (cx, cy) in enumerate(chips):
            blk = rows(outs[0], cx, cy, c)
            _remote(blk, blk, send_sems, recv_sems, 1 + j, (x, y, c)).wait_recv()
            passed.append(_remote(blk, blk, send_sems, recv_sems, 4 + j, (x, y, 1 - c)))
            passed[j].start()
        sib = rows(outs[0], x, y, 1 - c)
        _remote(sib, sib, send_sems, recv_sems, 0, (x, y, c)).wait_recv()
        for j, (cx, cy) in enumerate(chips):
            blk = rows(outs[0], cx, cy, 1 - c)
            _remote(blk, blk, send_sems, recv_sems, 4 + j, (x, y, c)).wait_recv()
        for cp in first(ins, outs, send_sems, recv_sems) + passed:
            cp.wait_send()

    return _Rider([v], [jax.ShapeDtypeStruct((N_DEV * m_per,) + v.shape[1:], v.dtype)], 7, start, finish)


def _sum_over_devices(name, v, gathered):
    m_per = v.shape[0]
    dev = 4 * lax.axis_index("x") + 2 * lax.axis_index("y") + lax.axis_index("c")
    full = lax.dynamic_update_slice_in_dim(gathered, v, dev * m_per, axis=0)
    return _sum_blocks(name, [full[i * m_per:(i + 1) * m_per] for i in range(N_DEV)], F32)


def _sum_blocks(name, parts, out_dtype):
    rows, cols = parts[0].shape
    tm = _row_tile(rows, 512)

    def body(*refs):
        acc = refs[0][...].astype(F32)
        for r in refs[1:-1]:
            acc = acc + r[...].astype(F32)
        refs[-1][...] = acc.astype(refs[-1].dtype)

    spec = pl.BlockSpec((tm, cols), lambda i: (i, 0))
    return pl.pallas_call(
        body, name=name, grid=(rows // tm,), in_specs=[spec] * len(parts), out_specs=spec,
        out_shape=jax.ShapeDtypeStruct((rows, cols), out_dtype), compiler_params=_params("arbitrary"),
    )(*parts)


def _adamw_math(wv, gv, mv, vv):
    m2 = ADAM_B1 * mv + (1.0 - ADAM_B1) * gv
    v2 = ADAM_B2 * vv + (1.0 - ADAM_B2) * (gv * gv)
    delta = -ADAM_LR * ((m2 / (1.0 - ADAM_B1 ** ADAM_STEP)) / (jnp.sqrt(v2 / (1.0 - ADAM_B2 ** ADAM_STEP)) + ADAM_EPS)
                        + ADAM_WD * wv)
    return delta, m2, v2


def _adamw_small(ws, gs, ms, vs):
    n = len(ws)

    def body(*refs):
        for i in range(n):
            res = _adamw_math(refs[i][...], refs[n + i][...], refs[2 * n + i][...], refs[3 * n + i][...])
            for k in range(3):
                refs[(4 + k) * n + i][...] = res[k]

    vm = pl.BlockSpec(memory_space=pltpu.VMEM)
    outs = pl.pallas_call(
        body, name="adamw_small", in_specs=[vm] * (4 * n), out_specs=[vm] * (3 * n),
        out_shape=[jax.ShapeDtypeStruct(a.shape, F32) for a in ws] * 3,
        compiler_params=pltpu.CompilerParams(vmem_limit_bytes=VMEM_LIMIT_BYTES),
    )(*ws, *gs, *ms, *vs)
    return outs[:n], outs[n:2 * n], outs[2 * n:]


PACK_ROWS = 256


def _pack(flat_parts, dtype, lead=()):
    parts = [a.astype(dtype).reshape(lead + (-1,)) for a in flat_parts]
    n = sum(a.shape[-1] for a in parts)
    chunk = PACK_ROWS * LANES
    total = -(-n // chunk) * chunk
    if total > n:
        parts.append(jnp.zeros(lead + (total - n,), dtype))
    return jnp.concatenate(parts, axis=-1).reshape(lead + (total // LANES, LANES))


def _unpack(buf, shapes, lead=()):
    flat = buf.reshape(lead + (-1,))
    out, off = [], 0
    for shp in shapes:
        n = math.prod(shp)
        out.append(lax.slice_in_dim(flat, off, off + n, axis=len(lead)).reshape(lead + tuple(shp)))
        off += n
    return out


BIG = ("w_in", "w_glu", "w_pa", "w_pb", "w_out", "w_up", "w_down")
WEIGHTS = ("g_mix", "w_in", "s5_a_re", "s5_a_im", "s5_log_dt", "s5_b_re", "s5_b_im", "s5_c_re", "s5_c_im", "s5_d",
           "w_glu", "b_glu", "hg_lb_logits", "hg_norm_gain", "w_pa", "w_pb", "w_out", "g_ffn", "w_up", "w_conv",
           "b_conv", "w_down", "g_final")
SMALL = tuple(n for n in WEIGHTS if n not in BIG)
SMALL_PARTS = ("loss", "g_mix", "g_ffn", "g_final", "b_glu", "gain", "lbrow", "s5_d", "w_conv", "b_conv", "lam_re",
               "lam_im", "bb_re", "bb_im", "s5_c_re", "s5_c_im")


def _lower_bound(logits):
    return jnp.cumsum(jax.nn.softmax(logits, axis=0), axis=0)[0:1]


def kernel(x, g_mix, w_in, s5_a_re, s5_a_im, s5_log_dt, s5_b_re, s5_b_im, s5_c_re, s5_c_im, s5_d, w_glu, b_glu, hg_lb_logits, hg_norm_gain, w_pa, w_pb, w_out, g_ffn, w_up, w_conv, b_conv, w_down, g_final, loss_target, m_g_mix, m_w_in, m_s5_a_re, m_s5_a_im, m_s5_log_dt, m_s5_b_re, m_s5_b_im, m_s5_c_re, m_s5_c_im, m_s5_d, m_w_glu, m_b_glu, m_hg_lb_logits, m_hg_norm_gain, m_w_pa, m_w_pb, m_w_out, m_g_ffn, m_w_up, m_w_conv, m_b_conv, m_w_down, m_g_final, v_g_mix, v_w_in, v_s5_a_re, v_s5_a_im, v_s5_log_dt, v_s5_b_re, v_s5_b_im, v_s5_c_re, v_s5_c_im, v_s5_d, v_w_glu, v_b_glu, v_hg_lb_logits, v_hg_norm_gain, v_w_pa, v_w_pb, v_w_out, v_g_ffn, v_w_up, v_w_conv, v_b_conv, v_w_down, v_g_final):
    args = dict(locals())
    w = {n: args[n] for n in WEIGHTS}
    mom = {n: args["m_" + n] for n in WEIGHTS}
    var = {n: args["v_" + n] for n in WEIGHTS}
    nseq, seq, d = x.shape
    xi, yi = lax.axis_index("x"), lax.axis_index("y")
    chip = 2 * xi + yi

    shard = {n: w[n][0] for n in BIG}
    first = [shard["w_in"].astype(BF16), shard["w_glu"].astype(BF16),
             jnp.pad(w_conv[0], ((0, 2 * SUBLANES - CONV_W), (0, 0)))]
    late16, landed_first = _cast_bf16([shard[n] for n in LATE], rider=_gather_ici_rider(first))
    p = dict(g_mix=g_mix, g_ffn=g_ffn, g_final=g_final.reshape(1, -1), b_glu=b_glu, gain=hg_norm_gain, s5_d=s5_d,
             b_conv=b_conv, lbrow=_lower_bound(hg_lb_logits),
             s5_a_re=s5_a_re[0], s5_a_im=s5_a_im[0], s5_log_dt=s5_log_dt[0], s5_b_re=s5_b_re[0], s5_b_im=s5_b_im[0],
             s5_c_re=s5_c_re[0], s5_c_im=s5_c_im[0])

    dx, halves, sm = _local_step(x.reshape(nseq * seq, d), loss_target.reshape(nseq * seq, d), p, first, landed_first,
                                 dict(zip(LATE, late16)), nseq=nseq, seq=seq)
    loss = sm["loss"][0, 0]

    grads, delta, new_m, new_v = {}, {}, {}, {}
    for n in BIG:
        shp = shard[n].shape
        grads[n], delta[n], new_m[n], new_v[n] = _adamw_halves("adamw_" + n, shard[n], mom[n].reshape(shp),
                                                               var[n].reshape(shp), *halves[n])

    _, disc_vjp = jax.vjp(_s5_discretize, p["s5_a_re"], p["s5_a_im"], p["s5_log_dt"], p["s5_b_re"], p["s5_b_im"])
    da_re, da_im, dlog_dt, db_re, db_im = disc_vjp((sm["lam_re"], sm["lam_im"], sm["bb_re"], sm["bb_im"]))
    _, lb_vjp = jax.vjp(_lower_bound, hg_lb_logits)
    (dlogits,) = lb_vjp(sm["lbrow"])
    fcols = w_conv.shape[-1]
    grads.update(
        g_mix=sm["g_mix"], g_ffn=sm["g_ffn"], g_final=sm["g_final"].reshape(-1), b_glu=sm["b_glu"],
        hg_norm_gain=sm["gain"], hg_lb_logits=dlogits, s5_d=sm["s5_d"], b_conv=sm["b_conv"],
        w_conv=lax.dynamic_slice_in_dim(sm["w_conv"], chip * fcols, fcols, axis=1),
        s5_a_re=da_re, s5_a_im=da_im, s5_log_dt=dlog_dt, s5_b_re=db_re, s5_b_im=db_im,
        s5_c_re=sm["s5_c_re"], s5_c_im=sm["s5_c_im"])
    grads = {n: grads[n].reshape(w[n].shape) for n in WEIGHTS}

    def natural(a):
        return a.reshape(1, -1) if a.ndim == 1 else (a[0] if a.ndim > 2 else a)

    outs = _adamw_small(*[[natural(src[n]) for n in SMALL] for src in (w, grads, mom, var)])
    for dst, group in zip((delta, new_m, new_v), outs):
        dst.update(zip(SMALL, group))
    res = [loss, dx.reshape(x.shape)]
    for group in (grads, delta, new_m, new_v):
        res += [group[n].reshape(w[n].shape) for n in WEIGHTS]
    return tuple(res)
```

```python
import functools
import math

import jax
import jax.numpy as jnp
from jax import lax
from jax.experimental import pallas as pl
from jax.experimental.pallas import tpu as pltpu

F32 = jnp.float32
BF16 = jnp.bfloat16
MESH = pl.DeviceIdType.MESH

EPS = 1e-6
S5_GROUP = 16
S5_STATE = 64
S5_BLOCK_GROUPS = 8
HEAD = 128
CHUNK = 64
CONV_W = 3
LANES = 128
SUBLANES = 8
GATE_BLOCK = 512
VMEM_LIMIT_BYTES = 56 * 1024 * 1024

ADAM_LR = 0.001
ADAM_B1 = 0.9
ADAM_B2 = 0.999
ADAM_EPS = 1e-08
ADAM_WD = 0.01
ADAM_STEP = 10

N_CHIPS = 4
N_DEV = 8


def _params(*sem):
    return pltpu.CompilerParams(dimension_semantics=sem, vmem_limit_bytes=VMEM_LIMIT_BYTES)


class _Rider:
    def __init__(self, arrays, out_shapes, nsem, start, finish, aliases=None):
        self.arrays, self.out_shapes, self.nsem = list(arrays), list(out_shapes), nsem
        self.start, self.finish, self.aliases = start, finish, dict(aliases or {})


def _hosted_call(name, body, *, grid, in_specs, out_specs, out_shape, operands, scratch_shapes=(), rider=None):
    in_specs, out_specs, out_shape, scratch_shapes = list(in_specs), list(out_specs), list(out_shape), list(scratch_shapes)
    cparams = _params(*(["arbitrary"] * len(grid)))
    if rider is None:
        return pl.pallas_call(body, name=name, grid=grid, in_specs=in_specs, out_specs=out_specs, out_shape=out_shape,
                              scratch_shapes=scratch_shapes, compiler_params=cparams)(*operands)
    n_in, n_out, n_sc = len(in_specs), len(out_specs), len(scratch_shapes)
    r_in, r_out = len(rider.arrays), len(rider.out_shapes)

    def hosted(*refs):
        ins, rins = refs[:n_in], refs[n_in:n_in + r_in]
        outs = refs[n_in + r_in:n_in + r_in + n_out]
        routs = refs[n_in + r_in + n_out:n_in + r_in + n_out + r_out]
        rest = refs[n_in + r_in + n_out + r_out:]
        send_sems, recv_sems = rest[n_sc], rest[n_sc + 1]
        first = functools.reduce(jnp.logical_and, [pl.program_id(i) == 0 for i in range(len(grid))])
        last = functools.reduce(jnp.logical_and, [pl.program_id(i) == grid[i] - 1 for i in range(len(grid))])

        @pl.when(first)
        def _():
            rider.start(rins, routs, send_sems, recv_sems)

        body(*ins, *outs, *rest[:n_sc])

        @pl.when(last)
        def _():
            rider.finish(rins, routs, send_sems, recv_sems)

    res = pl.pallas_call(
        hosted, name=name, grid=grid, in_specs=in_specs + [ANY] * r_in, out_specs=out_specs + [ANY] * r_out,
        out_shape=out_shape + rider.out_shapes,
        scratch_shapes=scratch_shapes + [pltpu.SemaphoreType.DMA((rider.nsem,)), pltpu.SemaphoreType.DMA((rider.nsem,))],
        input_output_aliases={n_in + i: n_out + o for i, o in rider.aliases.items()}, compiler_params=cparams,
    )(*operands, *rider.arrays)
    return res[:n_out], res[n_out:]


def _run_rider(name, rider):
    r_in, r_out = len(rider.arrays), len(rider.out_shapes)

    def body(*refs):
        rins, routs, send_sems, recv_sems = refs[:r_in], refs[r_in:r_in + r_out], refs[-2], refs[-1]
        rider.start(rins, routs, send_sems, recv_sems)
        rider.finish(rins, routs, send_sems, recv_sems)

    return pl.pallas_call(
        body, name=name, in_specs=[ANY] * r_in, out_specs=[ANY] * r_out, out_shape=rider.out_shapes,
        scratch_shapes=[pltpu.SemaphoreType.DMA((rider.nsem,)), pltpu.SemaphoreType.DMA((rider.nsem,))],
        input_output_aliases=rider.aliases,
    )(*rider.arrays)


def _row_tile(rows, cap):
    if rows <= cap:
        return rows
    for t in range(cap - cap % 8, 7, -8):
        if rows % t == 0:
            return t
    raise ValueError(f"no row tile for {rows}")


def _dot(a, b):
    return jnp.dot(a.astype(BF16), b.astype(BF16), preferred_element_type=F32)


def _dot_nt(a, b):
    return lax.dot_general(a.astype(BF16), b.astype(BF16), (((1,), (1,)), ((), ())), preferred_element_type=F32)


def _dot_tn(a, b):
    return lax.dot_general(a.astype(BF16), b.astype(BF16), (((0,), (0,)), ((), ())), preferred_element_type=F32)


def _sigmoid(x):
    return 0.5 * jnp.tanh(0.5 * x) + 0.5


_GELU_C = math.sqrt(2.0 / math.pi)


def _gelu(x):
    return 0.5 * x * (1.0 + jnp.tanh(_GELU_C * (x + 0.044715 * x * x * x)))


def _gelu_grad(x):
    th = jnp.tanh(_GELU_C * (x + 0.044715 * x * x * x))
    return 0.5 * (1.0 + th) + 0.5 * x * (1.0 - th * th) * _GELU_C * (1.0 + 3.0 * 0.044715 * x * x)


def _rowwise(name, fn, ins, outs, accs=(), *, rows, tm, ncol=1, rider=None):
    n_in, n_out = len(ins), len(outs)

    def body(*refs):
        res = fn(*[r[...] for r in refs[:n_in]])
        for r, v in zip(refs[n_in:n_in + n_out], res[:n_out]):
            r[...] = v.astype(r.dtype)
        first = pl.program_id(1) == 0
        for r, v in zip(refs[n_in + n_out:], res[n_out:]):
            @pl.when(first)
            def _():
                r[...] = v

            @pl.when(jnp.logical_not(first))
            def _():
                r[...] += v

    in_specs = []
    for _, width, base, kind in ins:
        if kind == "row":
            in_specs.append(pl.BlockSpec((tm, width), lambda j, i, b=base: (i, b + j)))
        else:
            in_specs.append(pl.BlockSpec((1, width), lambda j, i, b=base: (0, b + j)))
    out_specs = [pl.BlockSpec((tm, width), lambda j, i: (i, j)) for _, width, _ in outs]
    out_specs += [pl.BlockSpec((1, width), lambda j, i: (0, j)) for _, width in accs]
    out_shape = [jax.ShapeDtypeStruct((rows, total), dt) for total, _, dt in outs]
    out_shape += [jax.ShapeDtypeStruct((1, total), F32) for total, _ in accs]
    return _hosted_call(name, body, grid=(ncol, rows // tm), in_specs=in_specs, out_specs=out_specs, out_shape=out_shape,
                        operands=[a for a, _, _, _ in ins], rider=rider)


def _mm(name, a, b, *, mode, grid, a_spec, b_spec, o_spec, out_shape, acc_shape, res=None, res_spec=None,
        pair_axis=None, rider=None, epilogue=None):
    nk = grid[2]
    dot = {"nn": _dot, "nt": _dot_nt, "tn": _dot_tn}[mode]
    a_list = list(a) if isinstance(a, tuple) else [a]
    b_list = list(b) if isinstance(b, tuple) else [b]
    na, nb = len(a_list), len(b_list)
    assert (pair_axis is None) == (na + nb == 2)
    direct = nk == 1 and pair_axis is None
    epi_fn, epi_ins, epi_sums = epilogue if epilogue is not None else (None, [], [])
    n_res = 0 if res is None else 1
    n_epi = len(epi_ins)

    def body(*refs):
        a_refs, b_refs = refs[:na], refs[na:na + nb]
        r_ref = None if res is None else refs[na + nb]
        e_refs = refs[na + nb + n_res:na + nb + n_res + n_epi]
        o_ref = refs[na + nb + n_res + n_epi]
        s_refs = refs[na + nb + n_res + n_epi + 1:na + nb + n_res + n_epi + 1 + len(epi_sums)]
        first_rows = pl.program_id(0) == 0

        def finish(v):
            if res is not None:
                v = v + r_ref[...]
            if epi_fn is None:
                o_ref[...] = v.astype(o_ref.dtype)
                return
            outs = epi_fn(v, *[r[...] for r in e_refs])
            o_ref[...] = outs[0].astype(o_ref.dtype)
            for s_ref, part in zip(s_refs, outs[1:]):
                @pl.when(first_rows)
                def _():
                    s_ref[...] = part

                @pl.when(jnp.logical_not(first_rows))
                def _():
                    s_ref[...] += part

        if direct:
            finish(dot(a_refs[0][...], b_refs[0][...]))
            return
        acc_ref = refs[-1]
        k = pl.program_id(2)

        @pl.when(k == 0)
        def _():
            acc_ref[...] = jnp.zeros_like(acc_ref)

        if pair_axis is None:
            acc_ref[...] += dot(a_refs[0][...], b_refs[0][...])
        else:
            lower = pl.program_id(pair_axis) < grid[pair_axis] // 2

            @pl.when(lower)
            def _():
                acc_ref[...] += dot(a_refs[0][...], b_refs[0][...])

            @pl.when(jnp.logical_not(lower))
            def _():
                acc_ref[...] += dot(a_refs[-1][...], b_refs[-1][...])

        @pl.when(k == nk - 1)
        def _():
            finish(acc_ref[...])

    operands = a_list + b_list + ([] if res is None else [res]) + [arr for arr, _ in epi_ins]
    in_specs = (list(a_spec) if na == 2 else [a_spec]) + (list(b_spec) if nb == 2 else [b_spec])
    in_specs += ([] if res is None else [res_spec]) + [spec for _, spec in epi_ins]
    out_specs = [o_spec] + [pl.BlockSpec((1, c), lambda *_: (0, 0)) for c in epi_sums]
    out_shapes = [out_shape] + [jax.ShapeDtypeStruct((1, c), F32) for c in epi_sums]
    got = _hosted_call(name, body, grid=grid, in_specs=in_specs, out_specs=out_specs, out_shape=out_shapes,
                       scratch_shapes=[] if direct else [pltpu.VMEM(acc_shape, F32)], operands=operands, rider=rider)
    mine, rider_outs = (got, None) if rider is None else got
    mine = mine[0] if epilogue is None else tuple(mine)
    return mine if rider is None else (mine, rider_outs)


MM_TILE_BUDGET_BYTES = 36 * 1024 * 1024
MM_TILE_CAP = 1024
ROW_TILE = 512
GLU_TILE = 1024
ADAMW_TILE = 256


def _mm_tile(t, row_bytes, fixed_bytes):
    cap = max(16, min(MM_TILE_CAP, (MM_TILE_BUDGET_BYTES - fixed_bytes) // row_bytes))
    return _row_tile(t, cap - cap % 16)


def _size(a):
    return jnp.dtype(a.dtype).itemsize


def _mm_fwd_cols(name, a, w3, out_dtype=F32, rider=None):
    t, k = a.shape
    ns = w3.shape[2]
    tm = _mm_tile(t, 2 * k * _size(a) + 2 * ns * jnp.dtype(out_dtype).itemsize, 2 * k * ns * _size(w3))
    return _mm(name, a, w3, mode="nn", grid=(N_CHIPS, t // tm, 1),
               a_spec=pl.BlockSpec((tm, k), lambda j, i, kk: (i, 0)),
               b_spec=pl.BlockSpec((None, k, ns), lambda j, i, kk: (j, 0, 0)),
               o_spec=pl.BlockSpec((tm, ns), lambda j, i, kk: (i, j)),
               out_shape=jax.ShapeDtypeStruct((t, N_CHIPS * ns), out_dtype), acc_shape=(tm, ns), rider=rider)


def _mm_bwd_cols(name, d, w3, out_dtype=F32, rider=None, epilogue=None):
    pair = isinstance(d, tuple)
    t = d[0].shape[0] if pair else d.shape[0]
    k, ns = w3.shape[1], w3.shape[2]
    dsize = _size(d[0] if pair else d)
    tm = _mm_tile(t, (4 if pair else 2) * ns * dsize + 2 * k * jnp.dtype(out_dtype).itemsize + 4 * k
                  + _row_epilogue(epilogue, 8)[1], 2 * k * ns * _size(w3))
    half = N_CHIPS // 2
    if pair:
        a_spec = (pl.BlockSpec((tm, ns), lambda i, j, kk: (i, jnp.minimum(kk, half - 1))),
                  pl.BlockSpec((tm, ns), lambda i, j, kk: (i, jnp.maximum(kk - half, 0))))
    else:
        a_spec = pl.BlockSpec((tm, ns), lambda i, j, kk: (i, kk))
    return _mm(name, d, w3, mode="nt", grid=(t // tm, 1, N_CHIPS), a_spec=a_spec,
               b_spec=pl.BlockSpec((None, k, ns), lambda i, j, kk: (kk, 0, 0)),
               o_spec=pl.BlockSpec((tm, k), lambda i, j, kk: (i, 0)),
               out_shape=jax.ShapeDtypeStruct((t, k), out_dtype), acc_shape=(tm, k), pair_axis=2 if pair else None,
               rider=rider, epilogue=_row_epilogue(epilogue, tm)[0])


def _mm_wgrad_cols(name, a, d, rider=None):
    pair = isinstance(d, tuple)
    t, k = a.shape
    ns = (2 * d[0].shape[1] if pair else d.shape[1]) // N_CHIPS
    dsize = _size(d[0] if pair else d)
    tk = _mm_tile(t, 2 * k * _size(a) + (4 if pair else 2) * ns * dsize, k * ns * (4 + 2 * 2))
    half = N_CHIPS // 2
    if pair:
        b_spec = (pl.BlockSpec((tk, ns), lambda j, i, kk: (jnp.where(j < half, kk, 0), jnp.minimum(j, half - 1))),
                  pl.BlockSpec((tk, ns), lambda j, i, kk: (jnp.where(j < half, 0, kk), jnp.maximum(j - half, 0))))
    else:
        b_spec = pl.BlockSpec((tk, ns), lambda j, i, kk: (kk, j))
    return _mm(name, a, d, mode="tn", grid=(N_CHIPS, 1, t // tk),
               a_spec=pl.BlockSpec((tk, k), lambda j, i, kk: (kk, 0)), b_spec=b_spec,
               o_spec=pl.BlockSpec((None, k, ns), lambda j, i, kk: (j, 0, 0)),
               out_shape=jax.ShapeDtypeStruct((N_CHIPS, k, ns), BF16), acc_shape=(k, ns),
               pair_axis=0 if pair else None, rider=rider)


MM_BLOCK_CAP = 1408


def _row_epilogue(epilogue, tm):
    if epilogue is None:
        return None, 0
    fn, arrays, sums = epilogue
    specs = [pl.BlockSpec((1, x.shape[1]), lambda i, j, kk: (0, 0)) if x.shape[0] == 1 else
             pl.BlockSpec((tm, x.shape[1]), lambda i, j, kk: (i, 0)) for x in arrays]
    return (fn, list(zip(arrays, specs)), list(sums)), sum(2 * x.shape[1] * _size(x) for x in arrays if x.shape[0] > 1)


def _mm_fwd_rows(name, a, w, res=None, out_dtype=F32, epilogue=None):
    t, k = a.shape
    n = w.shape[1]
    tk = k if k <= MM_BLOCK_CAP else MM_BLOCK_CAP
    assert k % tk == 0
    row_bytes = 2 * tk * _size(a) + 2 * n * jnp.dtype(out_dtype).itemsize + (0 if res is None else 2 * n * 4) + 4 * n
    row_bytes += _row_epilogue(epilogue, 8)[1]
    tm = _mm_tile(t, row_bytes, 2 * tk * n * _size(w))
    return _mm(name, a, w, mode="nn", grid=(t // tm, 1, k // tk),
               a_spec=pl.BlockSpec((tm, tk), lambda i, j, kk: (i, kk)),
               b_spec=pl.BlockSpec((tk, n), lambda i, j, kk: (kk, 0)),
               o_spec=pl.BlockSpec((tm, n), lambda i, j, kk: (i, 0)),
               out_shape=jax.ShapeDtypeStruct((t, n), out_dtype), acc_shape=(tm, n),
               res=res, res_spec=None if res is None else pl.BlockSpec((tm, n), lambda i, j, kk: (i, 0)),
               epilogue=_row_epilogue(epilogue, tm)[0])


def _mm_bwd_rows(name, d, w, out_dtype=F32):
    t, n = d.shape
    k = w.shape[0]
    tn = k if k <= MM_BLOCK_CAP else MM_BLOCK_CAP
    assert k % tn == 0
    tm = _mm_tile(t, 2 * n * _size(d) + 2 * tn * jnp.dtype(out_dtype).itemsize, 2 * tn * n * _size(w))
    return _mm(name, d, w, mode="nt", grid=(t // tm, k // tn, 1),
               a_spec=pl.BlockSpec((tm, n), lambda i, j, kk: (i, 0)),
               b_spec=pl.BlockSpec((tn, n), lambda i, j, kk: (j, 0)),
               o_spec=pl.BlockSpec((tm, tn), lambda i, j, kk: (i, j)),
               out_shape=jax.ShapeDtypeStruct((t, k), out_dtype), acc_shape=(tm, tn))


def _mm_wgrad_rows(name, a, d):
    t, k = a.shape
    n = d.shape[1]
    nblk = next(b for b in (1, 2, 4) if (k // b) % LANES == 0 and k // b <= MM_BLOCK_CAP)
    ks = k // nblk
    tk = _mm_tile(t, 2 * ks * _size(a) + 2 * n * _size(d), ks * n * (4 + 2 * 2))
    return _mm(name, a, d, mode="tn", grid=(nblk, 1, t // tk),
               a_spec=pl.BlockSpec((tk, ks), lambda j, i, kk: (kk, j)),
               b_spec=pl.BlockSpec((tk, n), lambda j, i, kk: (kk, 0)),
               o_spec=pl.BlockSpec((ks, n), lambda j, i, kk: (j, 0)),
               out_shape=jax.ShapeDtypeStruct((k, n), BF16), acc_shape=(ks, n))


def _s5_discretize(a_re, a_im, log_dt, b_re, b_im):
    dt = jnp.exp(log_dt)[:, None]
    mag = jnp.exp(a_re * dt)
    ang = a_im * dt
    lb_re = mag * jnp.cos(ang)
    lb_im = mag * jnp.sin(ang)
    den = a_re * a_re + a_im * a_im
    n_re = lb_re - 1.0
    n_im = lb_im
    co_re = ((n_re * a_re + n_im * a_im) / den)[..., None]
    co_im = ((n_im * a_re - n_re * a_im) / den)[..., None]
    bb_re = co_re * b_re - co_im * b_im
    bb_im = co_re * b_im + co_im * b_re
    return lb_re, lb_im, bb_re, bb_im


def _s5_in_blocks(bb):
    g = bb.shape[0]
    nb = g // S5_BLOCK_GROUPS
    t = bb.reshape(nb, S5_BLOCK_GROUPS, S5_STATE, S5_GROUP).transpose(0, 1, 3, 2)
    eye = jnp.eye(S5_BLOCK_GROUPS, dtype=bb.dtype)
    full = t[:, :, :, None, :] * eye[None, :, None, :, None]
    return full.reshape(nb, S5_BLOCK_GROUPS * S5_GROUP, S5_BLOCK_GROUPS * S5_STATE)


def _s5_in_blocks_diag(blocks):
    nb = blocks.shape[0]
    t = blocks.reshape(nb, S5_BLOCK_GROUPS, S5_GROUP, S5_BLOCK_GROUPS, S5_STATE)
    d = jnp.einsum("bghgp->bghp", t)
    return d.transpose(0, 1, 3, 2).reshape(nb * S5_BLOCK_GROUPS, S5_STATE, S5_GROUP)


def _s5_out_blocks(c):
    g = c.shape[0]
    nb = g // S5_BLOCK_GROUPS
    t = c.reshape(nb, S5_BLOCK_GROUPS, S5_GROUP, S5_STATE).transpose(0, 1, 3, 2)
    eye = jnp.eye(S5_BLOCK_GROUPS, dtype=c.dtype)
    full = t[:, :, :, None, :] * eye[None, :, None, :, None]
    return full.reshape(nb, S5_BLOCK_GROUPS * S5_STATE, S5_BLOCK_GROUPS * S5_GROUP)


def _s5_out_blocks_diag(blocks):
    nb = blocks.shape[0]
    t = blocks.reshape(nb, S5_BLOCK_GROUPS, S5_STATE, S5_BLOCK_GROUPS, S5_GROUP)
    d = jnp.einsum("bgpgh->bgph", t)
    return d.transpose(0, 1, 3, 2).reshape(nb * S5_BLOCK_GROUPS, S5_GROUP, S5_STATE)


def _s5_scan_tables(lr, li, reverse):
    def cmul(a, b):
        return a[0] * b[0] - a[1] * b[1], a[0] * b[1] + a[1] * b[0]

    lam = (lr, -li) if reverse else (lr, li)
    pw = [lam]
    for _ in range(SUBLANES - 1):
        pw.append(cmul(pw[-1], lam))
    sub = jnp.arange(SUBLANES)[:, None]
    rows = []
    for s in (1, 2, 4):
        keep = (sub <= SUBLANES - 1 - s) if reverse else (sub >= s)
        rows.append(jnp.where(keep, pw[s - 1][0][None, :], 0.0))
        rows.append(jnp.where(keep, pw[s - 1][1][None, :], 0.0))
    order = list(range(SUBLANES - 1, -1, -1)) if reverse else list(range(SUBLANES))
    rows.append(jnp.stack([pw[i][0] for i in order]))
    rows.append(jnp.stack([pw[i][1] for i in order]))
    return jnp.concatenate(rows, axis=0)


def _s5_scan(vre_ref, vim_ref, coef_ref, seq, width, reverse, xre_ref=None, xim_ref=None):
    nt = seq // SUBLANES
    nl = width // LANES
    per = 2 if xre_ref is None else 4
    sub = lax.broadcasted_iota(jnp.int32, (SUBLANES, LANES), 0)

    def step(k, carry):
        kk = (nt - 1 - k) if reverse else k
        rows = pl.ds(pl.multiple_of(kk * SUBLANES, SUBLANES), SUBLANES)
        out = []
        for j in range(nl):
            lanes = slice(j * LANES, (j + 1) * LANES)
            co = [coef_ref[SUBLANES * q:SUBLANES * (q + 1), lanes] for q in range(8)]
            cr, ci = carry[per * j], carry[per * j + 1]
            vr = vre_ref[rows, lanes]
            vi = vim_ref[rows, lanes]
            for q, s in enumerate((1, 2, 4)):
                sh = SUBLANES - s if reverse else s
                rr = pltpu.roll(vr, sh, 0)
                ri = pltpu.roll(vi, sh, 0)
                ar, ai = co[2 * q], co[2 * q + 1]
                vr, vi = vr + ar * rr - ai * ri, vi + ar * ri + ai * rr
            edge = 0 if reverse else SUBLANES - 1
            cbr = jnp.broadcast_to(cr[edge:edge + 1, :], (SUBLANES, LANES))
            cbi = jnp.broadcast_to(ci[edge:edge + 1, :], (SUBLANES, LANES))
            pr, pi = co[6], co[7]
            vr, vi = vr + pr * cbr - pi * cbi, vi + pr * cbi + pi * cbr
            vre_ref[rows, lanes] = vr
            vim_ref[rows, lanes] = vi
            out += [vr, vi]
            if xre_ref is not None:
                nr = jnp.where(sub == SUBLANES - 1, cbr, pltpu.roll(vr, SUBLANES - 1, 0))
                ni = jnp.where(sub == SUBLANES - 1, cbi, pltpu.roll(vi, SUBLANES - 1, 0))
                xr = xre_ref[rows, lanes]
                xi = xim_ref[rows, lanes]
                out += [carry[per * j + 2] + nr * xr + ni * xi, carry[per * j + 3] + ni * xr - nr * xi]
        return tuple(out)

    zero = jnp.zeros((SUBLANES, LANES), F32)
    res = lax.fori_loop(0, nt, step, (zero,) * (per * nl))
    if xre_ref is None:
        return None
    return jnp.concatenate(
        [jnp.concatenate([jnp.sum(res[per * j + 2], axis=0, keepdims=True) for j in range(nl)], axis=1),
         jnp.concatenate([jnp.sum(res[per * j + 3], axis=0, keepdims=True) for j in range(nl)], axis=1)], axis=0)


def _s5_fwd(z, bre3, bim3, cre3, cim3, coef, dskip, *, nseq, seq, rider=None):
    nb = bre3.shape[0]
    ch, ns = bre3.shape[1], bre3.shape[2]

    def body(za_ref, bre_ref, bim_ref, cre_ref, cim_ref, coef_ref, d_ref, y_ref, xre_ref, xim_ref):
        za = za_ref[...]
        xre_ref[...] = _dot(za, bre_ref[...])
        xim_ref[...] = _dot(za, bim_ref[...])
        _s5_scan(xre_ref, xim_ref, coef_ref, seq, ns, False)
        y_ref[...] = _dot(xre_ref[...], cre_ref[...]) - _dot(xim_ref[...], cim_ref[...]) + d_ref[...] * za

    blk3 = lambda r, c: pl.BlockSpec((None, r, c), lambda b, j: (j, 0, 0))
    return _hosted_call(
        "s5_fwd", body, grid=(nseq, nb),
        in_specs=[pl.BlockSpec((seq, ch), lambda b, j: (b, j)), blk3(ch, ns), blk3(ch, ns), blk3(ns, ch), blk3(ns, ch),
                  pl.BlockSpec((8 * SUBLANES, ns), lambda b, j: (0, j)), pl.BlockSpec((1, ch), lambda b, j: (0, j))],
        out_specs=[pl.BlockSpec((seq, ch), lambda b, j: (b, j)), pl.BlockSpec((seq, ns), lambda b, j: (b, j)),
                   pl.BlockSpec((seq, ns), lambda b, j: (b, j))],
        out_shape=[jax.ShapeDtypeStruct((nseq * seq, nb * ch), F32), jax.ShapeDtypeStruct((nseq * seq, nb * ns), F32),
                   jax.ShapeDtypeStruct((nseq * seq, nb * ns), F32)],
        operands=(z, bre3, bim3, cre3, cim3, coef, dskip), rider=rider)


def _s5_bwd(dy, z, xre, xim, bre3, bim3, cre3, cim3, coef_rev, dskip, *, nseq, seq, rider=None):
    nb = bre3.shape[0]
    ch, ns = bre3.shape[1], bre3.shape[2]

    def body(dy_ref, za_ref, xre_ref, xim_ref, bre_ref, bim_ref, cre_ref, cim_ref, coef_ref, d_ref,
             dza_ref, dbre_ref, dbim_ref, dcre_ref, dcim_ref, dlam_ref, dd_ref, are_ref, aim_ref):
        dy = dy_ref[...]
        za = za_ref[...]
        are_ref[...] = _dot_nt(dy, cre_ref[...])
        aim_ref[...] = -_dot_nt(dy, cim_ref[...])
        dlam = _s5_scan(are_ref, aim_ref, coef_ref, seq, ns, True, xre_ref, xim_ref)
        are = are_ref[...]
        aim = aim_ref[...]
        dza_ref[...] = (_dot_nt(are, bre_ref[...]) + _dot_nt(aim, bim_ref[...]) + d_ref[...] * dy).astype(dza_ref.dtype)
        parts = (_dot_tn(za, are), _dot_tn(za, aim), _dot_tn(xre_ref[...], dy), -_dot_tn(xim_ref[...], dy),
                 dlam, jnp.sum(dy * za, axis=0, keepdims=True))
        first = pl.program_id(1) == 0
        for r, v in zip((dbre_ref, dbim_ref, dcre_ref, dcim_ref, dlam_ref, dd_ref), parts):
            @pl.when(first)
            def _():
                r[...] = v

            @pl.when(jnp.logical_not(first))
            def _():
                r[...] += v

    blk3 = lambda r, c: pl.BlockSpec((None, r, c), lambda j, b: (j, 0, 0))
    tok = lambda c: pl.BlockSpec((seq, c), lambda j, b: (b, j))
    return _hosted_call(
        "s5_bwd", body, grid=(nb, nseq),
        in_specs=[tok(ch), tok(ch), tok(ns), tok(ns), blk3(ch, ns), blk3(ch, ns), blk3(ns, ch), blk3(ns, ch),
                  pl.BlockSpec((8 * SUBLANES, ns), lambda j, b: (0, j)), pl.BlockSpec((1, ch), lambda j, b: (0, j))],
        out_specs=[tok(ch), blk3(ch, ns), blk3(ch, ns), blk3(ns, ch), blk3(ns, ch),
                   pl.BlockSpec((None, 2, ns), lambda j, b: (j, 0, 0)), pl.BlockSpec((1, ch), lambda j, b: (0, j))],
        out_shape=[jax.ShapeDtypeStruct((nseq * seq, nb * ch), BF16),
                   jax.ShapeDtypeStruct((nb, ch, ns), F32), jax.ShapeDtypeStruct((nb, ch, ns), F32),
                   jax.ShapeDtypeStruct((nb, ns, ch), F32), jax.ShapeDtypeStruct((nb, ns, ch), F32),
                   jax.ShapeDtypeStruct((nb, 2, ns), F32), jax.ShapeDtypeStruct((1, nb * ch), F32)],
        scratch_shapes=[pltpu.VMEM((seq, ns), F32), pltpu.VMEM((seq, ns), F32)],
        operands=(dy, z, xre, xim, bre3, bim3, cre3, cim3, coef_rev, dskip), rider=rider)


def _glu_fwd(y, wglu, bglu):
    t, w = y.shape
    tm = _row_tile(t, GLU_TILE)

    def body(y_ref, w_ref, b_ref, a0_ref, gl_ref, a_ref):
        a0 = _gelu(y_ref[...])
        gl = _dot(a0, w_ref[...])
        a0_ref[...] = a0.astype(a0_ref.dtype)
        gl_ref[...] = gl
        a_ref[...] = (a0 * _sigmoid(gl + b_ref[...])).astype(a_ref.dtype)

    tok = pl.BlockSpec((tm, w), lambda i: (i, 0))
    return pl.pallas_call(
        body, name="s5_glu", grid=(t // tm,),
        in_specs=[tok, pl.BlockSpec((w, w), lambda i: (0, 0)), pl.BlockSpec((1, w), lambda i: (0, 0))],
        out_specs=[tok, tok, tok],
        out_shape=[jax.ShapeDtypeStruct((t, w), BF16), jax.ShapeDtypeStruct((t, w), F32), jax.ShapeDtypeStruct((t, w), BF16)],
        compiler_params=_params("arbitrary"),
    )(y, wglu, bglu)


def _glu_bwd(y, gl, bglu, da, wglu):
    t, w = y.shape
    tm = _row_tile(t, GLU_TILE)

    def body(y_ref, gl_ref, b_ref, da_ref, w_ref, dgl_ref, dy_ref, db_ref):
        yv = y_ref[...]
        dav = da_ref[...]
        s = _sigmoid(gl_ref[...] + b_ref[...])
        dgl = dav * _gelu(yv) * s * (1.0 - s)
        dgl_ref[...] = dgl.astype(dgl_ref.dtype)
        dy_ref[...] = (dav * s + _dot_nt(dgl, w_ref[...])) * _gelu_grad(yv)
        part = jnp.sum(dgl, axis=0, keepdims=True)
        first = pl.program_id(0) == 0

        @pl.when(first)
        def _():
            db_ref[...] = part

        @pl.when(jnp.logical_not(first))
        def _():
            db_ref[...] += part

    tok = pl.BlockSpec((tm, w), lambda i: (i, 0))
    vec = pl.BlockSpec((1, w), lambda i: (0, 0))
    return pl.pallas_call(
        body, name="s5_glu_bwd", grid=(t // tm,),
        in_specs=[tok, tok, vec, tok, pl.BlockSpec((w, w), lambda i: (0, 0))], out_specs=[tok, tok, vec],
        out_shape=[jax.ShapeDtypeStruct((t, w), BF16), jax.ShapeDtypeStruct((t, w), F32), jax.ShapeDtypeStruct((1, w), F32)],
        compiler_params=_params("arbitrary"),
    )(y, gl, bglu, da, wglu)


def _cumsum_rows(x, reverse=False):
    n = x.shape[0]
    row = lax.broadcasted_iota(jnp.int32, x.shape, 0)
    s = 1
    while s < n:
        if reverse:
            x = x + jnp.where(row < n - s, pltpu.roll(x, n - s, 0), 0.0)
        else:
            x = x + jnp.where(row >= s, pltpu.roll(x, s, 0), 0.0)
        s *= 2
    return x


def _hg_gates(zq, zf, lb):
    sg = _sigmoid(zf)
    f = lb + (1.0 - lb) * sg
    sq = _sigmoid(zq)
    qa = zq * sq * (HEAD ** -0.5)
    b = _cumsum_rows(jnp.log(f))
    return sg, f, sq, qa, 1.0 - f, b


SUB = 16


def _hg_scores(qa, kk, b):
    c = qa.shape[0]
    row = lax.broadcasted_iota(jnp.int32, qa.shape, 0)
    pos = jnp.bitwise_and(row, SUB - 1)
    dmat = lax.broadcasted_iota(jnp.int32, (c, c), 0) - lax.broadcasted_iota(jnp.int32, (c, c), 1)
    p = jnp.zeros((c, c), F32)
    for d in range(SUB):
        if d == 0:
            fd = qa * kk
        else:
            e = jnp.exp(jnp.minimum(b - pltpu.roll(b, d, 0), 0.0))
            fd = jnp.where(pos >= d, qa * pltpu.roll(kk, d, 0) * e, 0.0)
        p = jnp.where(dmat == d, jnp.sum(fd, axis=1, keepdims=True), p)
    col = lax.broadcasted_iota(jnp.int32, (SUB, c), 1)
    blocks = [jnp.zeros((SUB, c), F32)]
    for r0 in range(SUB, c, SUB):
        beta = b[r0 - 1:r0, :]
        qt = qa[r0:r0 + SUB] * jnp.exp(b[r0:r0 + SUB] - beta)
        kt = kk * jnp.exp(jnp.minimum(beta - b, 0.0))
        blocks.append(jnp.where(col < r0, _dot_nt(qt, kt), 0.0))
    return p + jnp.concatenate(blocks, axis=0)


def _hg_scores_bwd(dp, qa, kk, b):
    c = qa.shape[0]
    row = lax.broadcasted_iota(jnp.int32, qa.shape, 0)
    pos = jnp.bitwise_and(row, SUB - 1)
    dmat = lax.broadcasted_iota(jnp.int32, (c, c), 0) - lax.broadcasted_iota(jnp.int32, (c, c), 1)
    dqa = jnp.zeros_like(qa)
    dkk = jnp.zeros_like(qa)
    db = jnp.zeros_like(qa)
    for d in range(SUB):
        dcol = jnp.sum(jnp.where(dmat == d, dp, 0.0), axis=1, keepdims=True)
        if d == 0:
            dqa = dqa + dcol * kk
            dkk = dkk + dcol * qa
        else:
            e = jnp.exp(jnp.minimum(b - pltpu.roll(b, d, 0), 0.0))
            w = jnp.where(pos >= d, dcol * e, 0.0)
            kr = pltpu.roll(kk, d, 0)
            dqa = dqa + w * kr
            tmp = w * qa
            dkk = dkk + pltpu.roll(tmp, c - d, 0)
            x = tmp * kr
            db = db + x - pltpu.roll(x, c - d, 0)
    col = lax.broadcasted_iota(jnp.int32, (SUB, c), 1)
    dq_blocks = [jnp.zeros((SUB, qa.shape[1]), F32)]
    db_blocks = [jnp.zeros((SUB, qa.shape[1]), F32)]
    for r0 in range(SUB, c, SUB):
        beta = b[r0 - 1:r0, :]
        eq = jnp.exp(b[r0:r0 + SUB] - beta)
        ek = jnp.exp(jnp.minimum(beta - b, 0.0))
        qt = qa[r0:r0 + SUB] * eq
        kt = kk * ek
        dpi = jnp.where(col < r0, dp[r0:r0 + SUB, :], 0.0)
        dqt = _dot(dpi, kt)
        dkt = _dot_tn(dpi, qt)
        dq_blocks.append(dqt * eq)
        db_blocks.append(dqt * qt)
        dkk = dkk + dkt * ek
        db = db - dkt * kt
    return dqa + jnp.concatenate(dq_blocks, axis=0), dkk, db + jnp.concatenate(db_blocks, axis=0)


def _hg_chunks_per_step(seq):
    nc = seq // CHUNK
    cps = next(k for k in (4, 2, 1) if nc % k == 0)
    return nc, cps, nc // cps


def _hg_fwd(z, lbrow, gain, *, nseq, seq, heads, qoff, rider=None):
    nc, cps, nblk = _hg_chunks_per_step(seq)
    blk = cps * CHUNK
    zspec = lambda off: pl.BlockSpec((blk, HEAD), lambda h, b, n, off=off: (b * nblk + n, off + h))

    def body(zq_ref, zf_ref, zi_ref, zg_ref, lb_ref, gn_ref, o_ref, yb_ref, st_ref, sc_ref, state):
        @pl.when(pl.program_id(2) == 0)
        def _():
            state[...] = jnp.zeros_like(state)

        lb = lb_ref[...]
        gain_v = gn_ref[...]

        def chunk(ci, carry):
            rows = pl.ds(pl.multiple_of(ci * CHUNK, CHUNK), CHUNK)
            st = state[...]
            st_ref[ci] = st
            zi = zi_ref[rows, :]
            zg = zg_ref[rows, :]
            _, _, _, qa, kk, b = _hg_gates(zq_ref[rows, :], zf_ref[rows, :], lb)
            scores = _hg_scores(qa, kk, b).astype(BF16)
            sc_ref[rows, :] = scores
            o = _dot_nt(qa * jnp.exp(b), st) + _dot(scores, zi)
            bl = b[CHUNK - 1:CHUNK, :]
            state[...] = st * jnp.exp(bl) + _dot_tn(zi, kk * jnp.exp(bl - b))
            o_ref[rows, :] = o
            r = lax.rsqrt(jnp.mean(o * o, axis=1, keepdims=True) + EPS)
            yb_ref[rows, :] = (o * r * gain_v * zg * _sigmoid(zg)).astype(yb_ref.dtype)
            return carry

        lax.fori_loop(0, cps, chunk, 0, unroll=True)

    tok = pl.BlockSpec((blk, HEAD), lambda h, b, n: (b * nblk + n, h))
    vec = pl.BlockSpec((1, HEAD), lambda h, b, n: (0, h))
    rows = nseq * seq
    return _hosted_call(
        "hgrn2_fwd", body, grid=(heads, nseq, nblk),
        in_specs=[zspec(qoff), zspec(qoff + heads), zspec(qoff + 2 * heads), zspec(qoff + 3 * heads), vec, vec],
        out_specs=[tok, tok, pl.BlockSpec((None, None, cps, HEAD, HEAD), lambda h, b, n: (h, b, n, 0, 0)),
                   pl.BlockSpec((None, blk, CHUNK), lambda h, b, n: (h, b * nblk + n, 0))],
        out_shape=[jax.ShapeDtypeStruct((rows, heads * HEAD), F32), jax.ShapeDtypeStruct((rows, heads * HEAD), BF16),
                   jax.ShapeDtypeStruct((heads, nseq, nc, HEAD, HEAD), F32),
                   jax.ShapeDtypeStruct((heads, rows, CHUNK), BF16)],
        scratch_shapes=[pltpu.VMEM((HEAD, HEAD), F32)], operands=(z, z, z, z, lbrow, gain), rider=rider)


def _hg_bwd(dyb, z, o, states, scores, lbrow, gain, *, nseq, seq, heads, qoff, rider=None):
    nc, cps, nblk = _hg_chunks_per_step(seq)
    blk = cps * CHUNK
    rev = lambda n: nblk - 1 - n
    zspec = lambda off: pl.BlockSpec((blk, HEAD), lambda h, b, n, off=off: (b * nblk + rev(n), off + h))

    def body(dyb_ref, zq_ref, zf_ref, zi_ref, zg_ref, o_ref, st_ref, sc_ref, lb_ref, gn_ref,
             dzq_ref, dzf_ref, dzi_ref, dzg_ref, dlb_ref, dgn_ref, dstate):
        @pl.when(pl.program_id(2) == 0)
        def _():
            dstate[...] = jnp.zeros_like(dstate)

        @pl.when(jnp.logical_and(pl.program_id(1) == 0, pl.program_id(2) == 0))
        def _():
            dlb_ref[...] = jnp.zeros_like(dlb_ref)
            dgn_ref[...] = jnp.zeros_like(dgn_ref)

        lb = lb_ref[...]
        gain_v = gn_ref[...]
        c = CHUNK
        causal = lax.broadcasted_iota(jnp.int32, (c, c), 0) >= lax.broadcasted_iota(jnp.int32, (c, c), 1)

        def chunk(step, carry):
            ci = cps - 1 - step
            rows = pl.ds(pl.multiple_of(ci * CHUNK, CHUNK), CHUNK)
            zq = zq_ref[rows, :]
            zi = zi_ref[rows, :]
            zg = zg_ref[rows, :]
            sg, f, sq, qa, kk, b = _hg_gates(zq, zf_ref[rows, :], lb)
            eb = jnp.exp(b)
            qt = qa * eb
            bl = b[c - 1:c, :]
            ebl = jnp.exp(bl)
            ekb = jnp.exp(bl - b)
            kh = kk * ekb
            st = st_ref[ci]
            dst = dstate[...]
            o = o_ref[rows, :]
            r = lax.rsqrt(jnp.mean(o * o, axis=1, keepdims=True) + EPS)
            oh = o * r
            szg = _sigmoid(zg)
            dyb = dyb_ref[rows, :]
            don = dyb * zg * szg
            dzg_ref[rows, :] = (dyb * oh * gain_v * szg * (1.0 + zg * (1.0 - szg))).astype(dzg_ref.dtype)
            doh = don * gain_v
            do = r * (doh - oh * jnp.mean(doh * oh, axis=1, keepdims=True))
            dqt = _dot(do, st)
            dp = jnp.where(causal, _dot_nt(do, zi), 0.0)
            dzi_ref[rows, :] = (_dot_tn(sc_ref[rows, :], do) + _dot_nt(kh, dst)).astype(dzi_ref.dtype)
            dkh = _dot(zi, dst)
            dbl = jnp.sum(dkh * kh, axis=0, keepdims=True) + jnp.sum(dst * st, axis=0, keepdims=True) * ebl
            dstate[...] = _dot_tn(do, qt) + dst * ebl
            dqa_s, dkk_s, db_s = _hg_scores_bwd(dp, qa, kk, b)
            dqa = dqt * eb + dqa_s
            dkk = dkh * ekb + dkk_s
            db = dqt * qt - dkh * kh + db_s
            row = lax.broadcasted_iota(jnp.int32, db.shape, 0)
            db = db + jnp.where(row == c - 1, dbl, 0.0)
            df = _cumsum_rows(db, reverse=True) / f - dkk
            dzf_ref[rows, :] = (df * (1.0 - lb) * sg * (1.0 - sg)).astype(dzf_ref.dtype)
            dzq_ref[rows, :] = (dqa * (HEAD ** -0.5) * sq * (1.0 + zq * (1.0 - sq))).astype(dzq_ref.dtype)
            dlb_ref[...] += jnp.sum(df * (1.0 - sg), axis=0, keepdims=True)
            dgn_ref[...] += jnp.sum(don * oh, axis=0, keepdims=True)
            return carry

        lax.fori_loop(0, cps, chunk, 0, unroll=2)

    tok = pl.BlockSpec((blk, HEAD), lambda h, b, n: (b * nblk + rev(n), h))
    vec = pl.BlockSpec((1, HEAD), lambda h, b, n: (0, h))
    rows = nseq * seq
    return _hosted_call(
        "hgrn2_bwd", body, grid=(heads, nseq, nblk),
        in_specs=[tok, zspec(qoff), zspec(qoff + heads), zspec(qoff + 2 * heads), zspec(qoff + 3 * heads), tok,
                  pl.BlockSpec((None, None, cps, HEAD, HEAD), lambda h, b, n: (h, b, rev(n), 0, 0)),
                  pl.BlockSpec((None, blk, CHUNK), lambda h, b, n: (h, b * nblk + rev(n), 0)), vec, vec],
        out_specs=[tok, tok, tok, tok, vec, vec],
        out_shape=[jax.ShapeDtypeStruct((rows, heads * HEAD), BF16)] * 4
        + [jax.ShapeDtypeStruct((1, heads * HEAD), F32)] * 2,
        scratch_shapes=[pltpu.VMEM((HEAD, HEAD), F32)],
        operands=(dyb, z, z, z, z, o, states, scores, lbrow, gain), rider=rider)


def _shift_rows(x, k):
    n = x.shape[0]
    r = pltpu.roll(x, k % n, 0)
    sub = lax.broadcasted_iota(jnp.int32, (SUBLANES, x.shape[1]), 0)
    if k > 0:
        return jnp.concatenate([jnp.where(sub >= k, r[0:SUBLANES], 0.0), r[SUBLANES:]], axis=0)
    return jnp.concatenate([r[:n - SUBLANES], jnp.where(sub < SUBLANES + k, r[n - SUBLANES:], 0.0)], axis=0)


def _conv_taps(h, w, bias):
    h1 = _shift_rows(h, 1)
    h2 = _shift_rows(h, 2)
    return h2 * w[0:1, :] + h1 * w[1:2, :] + h * w[2:3, :] + bias, h1, h2


def _conv_fwd(h, wconv, bconv, *, nseq, seq):
    ff2 = h.shape[1]
    ncol = ff2 // 2 // LANES

    def body(hg_ref, hv_ref, wg_ref, wv_ref, bg_ref, bv_ref, a_ref):
        g, _, _ = _conv_taps(hg_ref[...].astype(F32), wg_ref[...], bg_ref[...])
        v, _, _ = _conv_taps(hv_ref[...].astype(F32), wv_ref[...], bv_ref[...])
        a_ref[...] = (g * _sigmoid(g) * v).astype(a_ref.dtype)

    tok = lambda off: pl.BlockSpec((seq, LANES), lambda j, b, off=off: (b, off + j))
    wsp = lambda off: pl.BlockSpec((CONV_W, LANES), lambda j, b, off=off: (0, off + j))
    bsp = lambda off: pl.BlockSpec((1, LANES), lambda j, b, off=off: (0, off + j))
    return pl.pallas_call(
        body, name="conv_fwd", grid=(ncol, nseq),
        in_specs=[tok(0), tok(ncol), wsp(0), wsp(ncol), bsp(0), bsp(ncol)],
        out_specs=tok(0), out_shape=jax.ShapeDtypeStruct((nseq * seq, ff2 // 2), BF16),
        compiler_params=_params("arbitrary", "arbitrary"),
    )(h, h, wconv, wconv, bconv, bconv)


def _conv_bwd(da, h, wconv, bconv, *, nseq, seq):
    ff2 = h.shape[1]
    ncol = ff2 // 2 // LANES

    def half_bwd(d, hcur, h1, h2, w):
        d1 = _shift_rows(d, -1)
        d2 = _shift_rows(d, -2)
        dh = d * w[2:3, :] + d1 * w[1:2, :] + d2 * w[0:1, :]
        stats = jnp.concatenate(
            [jnp.sum(h2 * d, axis=0, keepdims=True), jnp.sum(h1 * d, axis=0, keepdims=True),
             jnp.sum(hcur * d, axis=0, keepdims=True), jnp.sum(d, axis=0, keepdims=True),
             jnp.zeros((SUBLANES - 4, d.shape[1]), F32)], axis=0)
        return dh, stats

    def body(da_ref, hg_ref, hv_ref, wg_ref, wv_ref, bg_ref, bv_ref, dhg_ref, dhv_ref, sg_ref, sv_ref):
        hg = hg_ref[...].astype(F32)
        hv = hv_ref[...].astype(F32)
        wg = wg_ref[...]
        wv = wv_ref[...]
        g, g1, g2 = _conv_taps(hg, wg, bg_ref[...])
        v, v1, v2 = _conv_taps(hv, wv, bv_ref[...])
        da = da_ref[...].astype(F32)
        s = _sigmoid(g)
        dhg, stg = half_bwd(da * v * s * (1.0 + g * (1.0 - s)), hg, g1, g2, wg)
        dhv, stv = half_bwd(da * g * s, hv, v1, v2, wv)
        dhg_ref[...] = dhg.astype(dhg_ref.dtype)
        dhv_ref[...] = dhv.astype(dhv_ref.dtype)
        first = pl.program_id(1) == 0
        for r, val in ((sg_ref, stg), (sv_ref, stv)):
            @pl.when(first)
            def _():
                r[...] = val

            @pl.when(jnp.logical_not(first))
            def _():
                r[...] += val

    tok = lambda off: pl.BlockSpec((seq, LANES), lambda j, b, off=off: (b, off + j))
    wsp = lambda off: pl.BlockSpec((CONV_W, LANES), lambda j, b, off=off: (0, off + j))
    bsp = lambda off: pl.BlockSpec((1, LANES), lambda j, b, off=off: (0, off + j))
    ssp = pl.BlockSpec((SUBLANES, LANES), lambda j, b: (0, j))
    dhg, dhv, stg, stv = pl.pallas_call(
        body, name="conv_bwd", grid=(ncol, nseq),
        in_specs=[tok(0), tok(0), tok(ncol), wsp(0), wsp(ncol), bsp(0), bsp(ncol)],
        out_specs=[tok(0), tok(0), ssp, ssp],
        out_shape=[jax.ShapeDtypeStruct((nseq * seq, ff2 // 2), BF16)] * 2
        + [jax.ShapeDtypeStruct((SUBLANES, ff2 // 2), F32)] * 2,
        compiler_params=_params("arbitrary", "arbitrary"),
    )(da, h, h, wconv, wconv, bconv, bconv)
    return (dhg, dhv), jnp.concatenate([stg, stv], axis=1)


def _rms_fwd(xv, g):
    r = lax.rsqrt(jnp.mean(xv * xv, axis=1, keepdims=True) + EPS)
    return (xv * r * g,)


def _rms_bwd(xv, g, dy, res):
    r = lax.rsqrt(jnp.mean(xv * xv, axis=1, keepdims=True) + EPS)
    xh = xv * r
    dxh = dy * g
    dx = r * (dxh - xh * jnp.mean(dxh * xh, axis=1, keepdims=True)) + res
    return dx, jnp.sum(dy * xh, axis=0, keepdims=True)


def _loss_head(x2, tgt, g):
    d = x2.shape[1]
    r = lax.rsqrt(jnp.mean(x2 * x2, axis=1, keepdims=True) + EPS)
    xh = x2 * r
    err = xh * g - tgt
    dy = err * (1.0 / d)
    dxh = dy * g
    dx = r * (dxh - xh * jnp.mean(dxh * xh, axis=1, keepdims=True))
    loss = 0.5 * jnp.sum(jnp.mean(err * err, axis=1, keepdims=True), axis=0, keepdims=True)
    return dx, jnp.sum(dy * xh, axis=0, keepdims=True), jnp.broadcast_to(loss, (1, LANES))


LATE_A = ("w_down", "w_out")
LATE_B = ("w_up", "w_pa", "w_pb")
LATE = LATE_A + LATE_B
EARLY_GRADS = ("w_down", "w_up", "w_out", "w_pa", "w_pb", "w_glu")
ROW_SHARDED = ("w_glu", "w_out", "w_down")


def _local_step(x, tgt, u, p, late, *, nseq, seq):
    p = dict(p)
    chip = 2 * lax.axis_index("x") + lax.axis_index("y")
    t, d = x.shape
    s5w = p["s5_d"].shape[1]
    hgw = p["gain"].shape[1]
    heads = hgw // HEAD
    qoff = s5w // LANES
    gblk = (s5w + 4 * hgw) // GATE_BLOCK
    ngb = d // GATE_BLOCK
    tm = _row_tile(t, ROW_TILE)
    row = lambda a, w=None, base=0: (a, a.shape[1] if w is None else w, base, "row")
    vec = lambda a, w=None, base=0: (a, a.shape[1] if w is None else w, base, "vec")
    rw = functools.partial(_rowwise, rows=t, tm=tm)

    z, landed_a = _mm_fwd_cols("in_proj", u, p["w_in"], rider=_gather_ici_rider([late[n] for n in LATE_A]))

    lam_re, lam_im, bb_re, bb_im = _s5_discretize(p["s5_a_re"], p["s5_a_im"], p["s5_log_dt"], p["s5_b_re"], p["s5_b_im"])
    bre3 = _s5_in_blocks(bb_re).astype(BF16)
    bim3 = _s5_in_blocks(bb_im).astype(BF16)
    cre3 = _s5_out_blocks(p["s5_c_re"]).astype(BF16)
    cim3 = _s5_out_blocks(p["s5_c_im"]).astype(BF16)
    coef_f = _s5_scan_tables(lam_re.reshape(-1), lam_im.reshape(-1), False)
    coef_r = _s5_scan_tables(lam_re.reshape(-1), lam_im.reshape(-1), True)
    (o, yb, states, scores), landed_b = _hg_fwd(z, p["lbrow"], p["gain"], nseq=nseq, seq=seq, heads=heads, qoff=qoff,
                                                rider=_gather_ici_rider([late[n] for n in LATE_B]))
    (y5, xre, xim), gathered = _s5_fwd(z, bre3, bim3, cre3, cim3, coef_f, p["s5_d"], nseq=nseq, seq=seq,
                                       rider=_gather_pass_rider(list(landed_a) + list(landed_b)))
    for n, g in zip(LATE, gathered):
        full = lax.dynamic_update_index_in_dim(g, late[n], chip, 0)
        p[n] = full.reshape(-1, full.shape[-1]) if n in ROW_SHARDED else full
    ya0, gl, ya = _glu_fwd(y5, p["w_glu"], p["b_glu"])

    joined = lambda w3: w3.transpose(1, 0, 2).reshape(w3.shape[1], -1)
    split = lambda g: g.reshape(g.shape[0], N_CHIPS, -1).transpose(1, 0, 2)
    wpa, wpb = joined(p["w_pa"]), joined(p["w_pb"])
    pa = _mm_fwd_rows("proj_a", ya, wpa, out_dtype=BF16)
    pb = _mm_fwd_rows("proj_b", yb, wpb, out_dtype=BF16)
    gb = GATE_BLOCK
    (m,) = rw("merge", lambda ga, gbv, a, b: (_sigmoid(ga) * a + _sigmoid(gbv) * b,),
              [row(z, gb, gblk), row(z, gb, gblk + ngb), row(pa, gb), row(pb, gb)], [(d, gb, BF16)], ncol=ngb)
    x1 = _mm_fwd_rows("out_proj", m, p["w_out"], res=x)

    (u2,) = rw("rms_ffn", _rms_fwd, [row(x1), vec(p["g_ffn"])], [(d, d, BF16)])
    h = _mm_fwd_cols("up_proj", u2, p["w_up"], out_dtype=BF16)
    a = _conv_fwd(h, p["w_conv"], p["b_conv"], nseq=nseq, seq=seq)
    dx2, dg_final, lossv = _mm_fwd_rows("down_proj", a, p["w_down"], res=x1,
                                        epilogue=(_loss_head, [tgt, p["g_final"]], [d, LANES]))

    norm_bwd = lambda dyv, xv, g, resv: _rms_bwd(xv, g, dyv, resv)
    da = _mm_bwd_rows("down_bwd", dx2, p["w_down"], out_dtype=BF16)
    g_wdown = _mm_wgrad_rows("down_wgrad", a, dx2)
    dh, cstats = _conv_bwd(da, h, p["w_conv"], p["b_conv"], nseq=nseq, seq=seq)
    dx1, dg_ffn = _mm_bwd_cols("up_bwd", dh, p["w_up"], epilogue=(norm_bwd, [x1, p["g_ffn"], dx2], [d]))
    g_wup = _mm_wgrad_cols("up_wgrad", u2, dh)

    dm = _mm_bwd_rows("out_bwd", dx1, p["w_out"], out_dtype=BF16)
    g_wout = _mm_wgrad_rows("out_wgrad", m, dx1)

    def merge_bwd(ga, gbv, av, bv, dmv):
        sa = _sigmoid(ga)
        sb = _sigmoid(gbv)
        return dmv * sa, dmv * sb, dmv * av * sa * (1.0 - sa), dmv * bv * sb * (1.0 - sb)

    dpa, dpb, dzga, dzgb = rw("merge_bwd", merge_bwd,
                              [row(z, gb, gblk), row(z, gb, gblk + ngb), row(pa, gb), row(pb, gb), row(dm, gb)],
                              [(d, gb, BF16)] * 4, ncol=ngb)
    dya = _mm_bwd_rows("proj_a_bwd", dpa, wpa)
    g_wpa = split(_mm_wgrad_rows("proj_a_wgrad", ya, dpa))
    dyb = _mm_bwd_rows("proj_b_bwd", dpb, wpb)
    g_wpb = split(_mm_wgrad_rows("proj_b_wgrad", yb, dpb))

    dgl, dy5, db_glu = _glu_bwd(y5, gl, p["b_glu"], dya, p["w_glu"])
    g_wglu = _mm_wgrad_rows("glu_wgrad", ya0, dgl)
    partial = dict(w_down=g_wdown, w_up=g_wup, w_out=g_wout, w_pa=g_wpa, w_pb=g_wpb, w_glu=g_wglu)
    parts = [_grad_parts(partial[n]) for n in EARLY_GRADS]
    (dza, dbre3, dbim3, dcre3, dcim3, dlam, dd), sib = _s5_bwd(
        dy5, z, xre, xim, bre3, bim3, cre3, cim3, coef_r, p["s5_d"], nseq=nseq, seq=seq, rider=_swap_halves_rider(parts))
    pair = _pair_sums(EARLY_GRADS, parts, sib)
    (dzq, dzf, dzi, dzg, dlb, dgain), others = _hg_bwd(
        dyb, z, o, states, scores, p["lbrow"], p["gain"], nseq=nseq, seq=seq, heads=heads, qoff=qoff,
        rider=_scatter_rider(pair))
    halves = _chip_sums(EARLY_GRADS, pair, others)

    dz = jnp.concatenate([dza, dzq, dzf, dzi, dzg, dzga, dzgb], axis=1)
    dx, dg_mix = _mm_bwd_cols("in_bwd", dz, p["w_in"], epilogue=(norm_bwd, [x, p["g_mix"], dx1], [d]))

    gshape = lam_re.shape
    small = {
        "loss": lossv, "g_mix": dg_mix, "g_ffn": dg_ffn, "g_final": dg_final, "b_glu": db_glu, "gain": dgain,
        "lbrow": dlb, "s5_d": dd, "w_conv": cstats[0:CONV_W], "b_conv": cstats[CONV_W:CONV_W + 1],
        "lam_re": dlam[:, 0, :].reshape(gshape), "lam_im": dlam[:, 1, :].reshape(gshape),
        "bb_re": _s5_in_blocks_diag(dbre3), "bb_im": _s5_in_blocks_diag(dbim3),
        "s5_c_re": _s5_out_blocks_diag(dcre3), "s5_c_im": _s5_out_blocks_diag(dcim3),
    }
    small_vec = _pack([small[n] for n in SMALL_PARTS], F32)
    g_win, (small_all, *sibs) = _mm_wgrad_cols(
        "in_wgrad", u, dz, rider=_merge_riders(_gather_all_rider(small_vec), _swap_sums_rider(halves)))
    big = dict(zip(EARLY_GRADS, zip(halves, sibs)))
    small_sum = _sum_over_devices("small_grad_sum", small_vec, small_all)
    sm = dict(zip(SMALL_PARTS, _unpack(small_sum, [small[n].shape for n in SMALL_PARTS])))
    last = [_grad_parts(g_win)]
    pair = _pair_sums(("w_in",), last, _run_rider("grad_swap_halves", _swap_halves_rider(last)))
    (half,) = _chip_sums(("w_in",), pair, _run_rider("grad_scatter_chips", _scatter_rider(pair)))
    mid = half.shape[0] // 2
    sib_half = jnp.concatenate(_run_rider("grad_swap_sums", _swap_sums_rider([half[:mid], half[mid:]])), axis=0)
    big["w_in"] = (half, sib_half)
    return dx, big, sm


ANY = pl.BlockSpec(memory_space=pl.ANY)


def _place():
    x, y, c = lax.axis_index("x"), lax.axis_index("y"), lax.axis_index("c")
    chips = [(1 - x, y), (x, 1 - y), (1 - x, 1 - y)]
    return x, y, c, chips


def _remote(src, dst, send_sems, recv_sems, k, to):
    return pltpu.make_async_remote_copy(src_ref=src, dst_ref=dst, send_sem=send_sems.at[k], recv_sem=recv_sems.at[k],
                                        device_id=to, device_id_type=MESH)


def _half(rows, which):
    return pl.ds(pl.multiple_of(which * (rows // 2), SUBLANES), rows // 2)


class _SemView:
    def __init__(self, base, offset):
        self.base, self.offset = base, offset

    @property
    def at(self):
        return self

    def __getitem__(self, k):
        return self.base.at[self.offset + k]


def _merge_riders(first, second):
    na, no, ns = len(first.arrays), len(first.out_shapes), first.nsem

    def split(fn_a, fn_b):
        def run(ins, outs, send_sems, recv_sems):
            fn_a(ins[:na], outs[:no], send_sems, recv_sems)
            fn_b(ins[na:], outs[no:], _SemView(send_sems, ns), _SemView(recv_sems, ns))
        return run

    aliases = dict(first.aliases)
    aliases.update({na + i: no + o for i, o in second.aliases.items()})
    return _Rider(first.arrays + second.arrays, first.out_shapes + second.out_shapes, ns + second.nsem,
                  split(first.start, second.start), split(first.finish, second.finish), aliases)


def _prepare(x, gain, arrays, rider):
    t, d = x.shape
    n = len(arrays)
    tm = _row_tile(t, ROW_TILE)

    def body(x_ref, g_ref, *refs):
        (u,) = _rms_fwd(x_ref[...], g_ref[...])
        refs[n][...] = u.astype(BF16)

        @pl.when(pl.program_id(0) == 0)
        def _():
            for i in range(n):
                refs[n + 1 + i][...] = refs[i][...].astype(BF16)

    vm = pl.BlockSpec(memory_space=pltpu.VMEM)
    tok = pl.BlockSpec((tm, d), lambda i: (i, 0))
    outs, gathered = _hosted_call(
        "prepare", body, grid=(t // tm,), in_specs=[tok, pl.BlockSpec((1, d), lambda i: (0, 0))] + [vm] * n,
        out_specs=[tok] + [vm] * n,
        out_shape=[jax.ShapeDtypeStruct((t, d), BF16)] + [jax.ShapeDtypeStruct(a.shape, BF16) for a in arrays],
        operands=[x, gain] + list(arrays), rider=rider)
    return outs[0], outs[1:], gathered


def _symmetric_rider(arrays, out_shapes, copies_of, nsem):
    def start(ins, outs, send_sems, recv_sems):
        for cp in copies_of(ins, outs, send_sems, recv_sems):
            cp.start()

    def finish(ins, outs, send_sems, recv_sems):
        for cp in copies_of(ins, outs, send_sems, recv_sems):
            cp.wait()

    return _Rider(arrays, out_shapes, nsem, start, finish)


def _swap_halves_rider(parts):
    def copies_of(ins, outs, send_sems, recv_sems):
        x, y, c, _ = _place()
        return [_remote(ins[a].at[:, _half(g.shape[1], 1 - c), :], outs[a], send_sems, recv_sems, a, (x, y, 1 - c))
                for a, g in enumerate(parts)]

    shapes = [jax.ShapeDtypeStruct((g.shape[0], g.shape[1] // 2, g.shape[2]), g.dtype) for g in parts]
    return _symmetric_rider(parts, shapes, copies_of, len(parts))


def _scatter_rider(parts):
    def copies_of(ins, outs, send_sems, recv_sems):
        x, y, c, chips = _place()
        return [_remote(ins[a].at[2 * cx + cy], outs[a].at[j], send_sems, recv_sems, 3 * a + j, (cx, cy, c))
                for a in range(len(parts)) for j, (cx, cy) in enumerate(chips)]

    shapes = [jax.ShapeDtypeStruct((N_CHIPS - 1,) + h.shape[1:], h.dtype) for h in parts]
    return _symmetric_rider(parts, shapes, copies_of, 3 * len(parts))


def _swap_sums_rider(parts):
    def copies_of(ins, outs, send_sems, recv_sems):
        x, y, c, _ = _place()
        return [_remote(ins[a], outs[a], send_sems, recv_sems, a, (x, y, 1 - c)) for a in range(len(parts))]

    shapes = [jax.ShapeDtypeStruct(g.shape, g.dtype) for g in parts]
    return _symmetric_rider(parts, shapes, copies_of, len(parts))


def _gather_ici_rider(shards):
    def sends(ins, outs, send_sems, recv_sems):
        x, y, c, chips = _place()
        return [_remote(ins[a].at[_half(s.shape[0], c)], outs[a].at[2 * x + y, _half(s.shape[0], c)], send_sems,
                        recv_sems, 3 * a + j, (cx, cy, c)) for a, s in enumerate(shards) for j, (cx, cy) in enumerate(chips)]

    def start(ins, outs, send_sems, recv_sems):
        for cp in sends(ins, outs, send_sems, recv_sems):
            cp.start()

    def finish(ins, outs, send_sems, recv_sems):
        x, y, c, chips = _place()
        for a, s in enumerate(shards):
            for j, (cx, cy) in enumerate(chips):
                landed = outs[a].at[2 * cx + cy, _half(s.shape[0], c)]
                _remote(landed, landed, send_sems, recv_sems, 3 * a + j, (x, y, c)).wait_recv()
        for cp in sends(ins, outs, send_sems, recv_sems):
            cp.wait_send()

    shapes = [jax.ShapeDtypeStruct((N_CHIPS,) + s.shape, s.dtype) for s in shards]
    return _Rider(shards, shapes, 3 * len(shards), start, finish)


def _gather_full_rider(shards):
    n = len(shards)

    def sends(ins, outs, send_sems, recv_sems):
        x, y, c, chips = _place()
        return [_remote(ins[a].at[_half(s.shape[0], c)], outs[a].at[2 * x + y, _half(s.shape[0], c)], send_sems,
                        recv_sems, 6 * a + j, (cx, cy, c)) for a, s in enumerate(shards) for j, (cx, cy) in enumerate(chips)]

    def start(ins, outs, send_sems, recv_sems):
        for cp in sends(ins, outs, send_sems, recv_sems):
            cp.start()

    def finish(ins, outs, send_sems, recv_sems):
        x, y, c, chips = _place()
        passed = []
        for a, s in enumerate(shards):
            for j, (cx, cy) in enumerate(chips):
                landed = outs[a].at[2 * cx + cy, _half(s.shape[0], c)]
                _remote(landed, landed, send_sems, recv_sems, 6 * a + j, (x, y, c)).wait_recv()
                passed.append(_remote(landed, landed, send_sems, recv_sems, 6 * a + 3 + j, (x, y, 1 - c)))
                passed[-1].start()
        for a, s in enumerate(shards):
            for j, (cx, cy) in enumerate(chips):
                other = outs[a].at[2 * cx + cy, _half(s.shape[0], 1 - c)]
                _remote(other, other, send_sems, recv_sems, 6 * a + 3 + j, (x, y, c)).wait_recv()
        for cp in sends(ins, outs, send_sems, recv_sems) + passed:
            cp.wait_send()

    shapes = [jax.ShapeDtypeStruct((N_CHIPS,) + s.shape, s.dtype) for s in shards]
    return _Rider(shards, shapes, 6 * n, start, finish)


def _gather_pass_rider(landed):
    def sends(ins, outs, send_sems, recv_sems):
        x, y, c, chips = _place()
        return [_remote(ins[a].at[2 * cx + cy, _half(g.shape[1], c)], outs[a].at[2 * cx + cy, _half(g.shape[1], c)],
                        send_sems, recv_sems, 3 * a + j, (x, y, 1 - c))
                for a, g in enumerate(landed) for j, (cx, cy) in enumerate(chips)]

    def start(ins, outs, send_sems, recv_sems):
        for cp in sends(ins, outs, send_sems, recv_sems):
            cp.start()

    def finish(ins, outs, send_sems, recv_sems):
        x, y, c, chips = _place()
        for a, g in enumerate(landed):
            for j, (cx, cy) in enumerate(chips):
                other = outs[a].at[2 * cx + cy, _half(g.shape[1], 1 - c)]
                _remote(other, other, send_sems, recv_sems, 3 * a + j, (x, y, c)).wait_recv()
        for cp in sends(ins, outs, send_sems, recv_sems):
            cp.wait_send()

    shapes = [jax.ShapeDtypeStruct(g.shape, g.dtype) for g in landed]
    return _Rider(landed, shapes, 3 * len(landed), start, finish, aliases={a: a for a in range(len(landed))})


def _grad_parts(g):
    return g.reshape((N_CHIPS, -1, g.shape[-1]))


def _place_scalars():
    return jnp.stack([lax.axis_index("c"), 2 * lax.axis_index("x") + lax.axis_index("y")]).astype(jnp.int32)


def _scalar_call(body, name, grid, in_specs, out_specs, out_shape, operands):
    spec = pltpu.PrefetchScalarGridSpec(num_scalar_prefetch=1, grid=grid, in_specs=in_specs, out_specs=out_specs)
    return pl.pallas_call(body, name=name, grid_spec=spec, out_shape=out_shape,
                          compiler_params=_params(*(["arbitrary"] * len(grid))))(_place_scalars(), *operands)


def _pair_sums(names, parts, sib):
    out = []
    for n, g, s in zip(names, parts, sib):
        rh, cols = s.shape[1], s.shape[2]
        tm = _row_tile(rh, ROW_TILE)
        nblk = rh // tm

        def body(place, g_ref, s_ref, o_ref):
            o_ref[...] = (g_ref[...].astype(F32) + s_ref[...].astype(F32)).astype(o_ref.dtype)

        blk = pl.BlockSpec((None, tm, cols), lambda j, i, place: (j, i, 0))
        own = pl.BlockSpec((None, tm, cols), lambda j, i, place, nblk=nblk: (j, place[0] * nblk + i, 0))
        out.append(_scalar_call(body, "grad_pair_sum_" + n, (N_CHIPS, nblk), [own, blk], blk,
                                jax.ShapeDtypeStruct(s.shape, BF16), (g, s)))
    return out


def _chip_sums(names, pair, others):
    out = []
    for n, h, o in zip(names, pair, others):
        rh, cols = h.shape[1], h.shape[2]
        tm = _row_tile(rh, ROW_TILE)

        def body(place, h_ref, a_ref, b_ref, c_ref, o_ref):
            o_ref[...] = (h_ref[...].astype(F32) + a_ref[...].astype(F32)) + b_ref[...].astype(F32) + c_ref[...].astype(F32)

        mine = pl.BlockSpec((None, tm, cols), lambda i, place: (place[1], i, 0))
        other = lambda k: pl.BlockSpec((None, tm, cols), lambda i, place, k=k: (k, i, 0))
        out.append(_scalar_call(body, "grad_chip_sum_" + n, (rh // tm,), [mine, other(0), other(1), other(2)],
                                pl.BlockSpec((tm, cols), lambda i, place: (i, 0)), jax.ShapeDtypeStruct((rh, cols), F32),
                                (h, o, o, o)))
    return out


def _adamw_halves(name, w, m, v, own, sib):
    rh, cols = own.shape
    tm = _row_tile(rh, ADAMW_TILE)
    nblk = rh // tm

    def body(place, w_ref, m_ref, v_ref, own_ref, sib_ref, g_ref, d_ref, m2_ref, v2_ref):
        mine = pl.program_id(0) // nblk == place[0]

        def run(gv):
            g_ref[...] = gv
            d_ref[...], m2_ref[...], v2_ref[...] = _adamw_math(w_ref[...], gv, m_ref[...], v_ref[...])

        @pl.when(mine)
        def _():
            run(own_ref[...])

        @pl.when(jnp.logical_not(mine))
        def _():
            run(sib_ref[...])

    full = pl.BlockSpec((tm, cols), lambda i, place: (i, 0))
    own_spec = pl.BlockSpec((tm, cols), lambda i, place: (jnp.where(i // nblk == place[0], i % nblk, 0), 0))
    sib_spec = pl.BlockSpec((tm, cols), lambda i, place: (jnp.where(i // nblk == place[0], 0, i % nblk), 0))
    return _scalar_call(body, name, (2 * nblk,), [full, full, full, own_spec, sib_spec], [full] * 4,
                        [jax.ShapeDtypeStruct((2 * rh, cols), F32)] * 4, (w, m, v, own, sib))


def _gather_all_rider(v):
    m_per = v.shape[0]

    def rows(ref, px, py, pc):
        return ref.at[pl.ds(pl.multiple_of((4 * px + 2 * py + pc) * m_per, 8), m_per)]

    def first(ins, outs, send_sems, recv_sems):
        x, y, c, chips = _place()
        mine = rows(outs[0], x, y, c)
        return [_remote(ins[0], mine, send_sems, recv_sems, 0, (x, y, 1 - c))] + [
            _remote(ins[0], mine, send_sems, recv_sems, 1 + j, (cx, cy, c)) for j, (cx, cy) in enumerate(chips)]

    def start(ins, outs, send_sems, recv_sems):
        for cp in first(ins, outs, send_sems, recv_sems):
            cp.start()

    def finish(ins, outs, send_sems, recv_sems):
        x, y, c, chips = _place()
        passed = []
        for j, (cx, cy) in enumerate(chips):
            blk = rows(outs[0], cx, cy, c)
            _remote(blk, blk, send_sems, recv_sems, 1 + j, (x, y, c)).wait_recv()
            passed.append(_remote(blk, blk, send_sems, recv_sems, 4 + j, (x, y, 1 - c)))
            passed[j].start()
        sib = rows(outs[0], x, y, 1 - c)
        _remote(sib, sib, send_sems, recv_sems, 0, (x, y, c)).wait_recv()
        for j, (cx, cy) in enumerate(chips):
            blk = rows(outs[0], cx, cy, 1 - c)
            _remote(blk, blk, send_sems, recv_sems, 4 + j, (x, y, c)).wait_recv()
        for cp in first(ins, outs, send_sems, recv_sems) + passed:
            cp.wait_send()

    return _Rider([v], [jax.ShapeDtypeStruct((N_DEV * m_per,) + v.shape[1:], v.dtype)], 7, start, finish)


def _sum_over_devices(name, v, gathered):
    m_per = v.shape[0]
    dev = 4 * lax.axis_index("x") + 2 * lax.axis_index("y") + lax.axis_index("c")
    full = lax.dynamic_update_slice_in_dim(gathered, v, dev * m_per, axis=0)
    return _sum_blocks(name, [full[i * m_per:(i + 1) * m_per] for i in range(N_DEV)], F32)


def _sum_blocks(name, parts, out_dtype):
    rows, cols = parts[0].shape
    tm = _row_tile(rows, ROW_TILE)

    def body(*refs):
        acc = refs[0][...].astype(F32)
        for r in refs[1:-1]:
            acc = acc + r[...].astype(F32)
        refs[-1][...] = acc.astype(refs[-1].dtype)

    spec = pl.BlockSpec((tm, cols), lambda i: (i, 0))
    return pl.pallas_call(
        body, name=name, grid=(rows // tm,), in_specs=[spec] * len(parts), out_specs=spec,
        out_shape=jax.ShapeDtypeStruct((rows, cols), out_dtype), compiler_params=_params("arbitrary"),
    )(*parts)


def _adamw_math(wv, gv, mv, vv):
    m2 = ADAM_B1 * mv + (1.0 - ADAM_B1) * gv
    v2 = ADAM_B2 * vv + (1.0 - ADAM_B2) * (gv * gv)
    delta = -ADAM_LR * ((m2 / (1.0 - ADAM_B1 ** ADAM_STEP)) / (jnp.sqrt(v2 / (1.0 - ADAM_B2 ** ADAM_STEP)) + ADAM_EPS)
                        + ADAM_WD * wv)
    return delta, m2, v2


def _adamw_small(ws, gs, ms, vs):
    n = len(ws)

    def body(*refs):
        for i in range(n):
            res = _adamw_math(refs[i][...], refs[n + i][...], refs[2 * n + i][...], refs[3 * n + i][...])
            for k in range(3):
                refs[(4 + k) * n + i][...] = res[k]

    vm = pl.BlockSpec(memory_space=pltpu.VMEM)
    outs = pl.pallas_call(
        body, name="adamw_small", in_specs=[vm] * (4 * n), out_specs=[vm] * (3 * n),
        out_shape=[jax.ShapeDtypeStruct(a.shape, F32) for a in ws] * 3,
        compiler_params=pltpu.CompilerParams(vmem_limit_bytes=VMEM_LIMIT_BYTES),
    )(*ws, *gs, *ms, *vs)
    return outs[:n], outs[n:2 * n], outs[2 * n:]


PACK_ROWS = 256


def _pack(flat_parts, dtype, lead=()):
    parts = [a.astype(dtype).reshape(lead + (-1,)) for a in flat_parts]
    n = sum(a.shape[-1] for a in parts)
    chunk = PACK_ROWS * LANES
    total = -(-n // chunk) * chunk
    if total > n:
        parts.append(jnp.zeros(lead + (total - n,), dtype))
    return jnp.concatenate(parts, axis=-1).reshape(lead + (total // LANES, LANES))


def _unpack(buf, shapes, lead=()):
    flat = buf.reshape(lead + (-1,))
    out, off = [], 0
    for shp in shapes:
        n = math.prod(shp)
        out.append(lax.slice_in_dim(flat, off, off + n, axis=len(lead)).reshape(lead + tuple(shp)))
        off += n
    return out


BIG = ("w_in", "w_glu", "w_pa", "w_pb", "w_out", "w_up", "w_down")
WEIGHTS = ("g_mix", "w_in", "s5_a_re", "s5_a_im", "s5_log_dt", "s5_b_re", "s5_b_im", "s5_c_re", "s5_c_im", "s5_d",
           "w_glu", "b_glu", "hg_lb_logits", "hg_norm_gain", "w_pa", "w_pb", "w_out", "g_ffn", "w_up", "w_conv",
           "b_conv", "w_down", "g_final")
SMALL = tuple(n for n in WEIGHTS if n not in BIG)
SMALL_PARTS = ("loss", "g_mix", "g_ffn", "g_final", "b_glu", "gain", "lbrow", "s5_d", "w_conv", "b_conv", "lam_re",
               "lam_im", "bb_re", "bb_im", "s5_c_re", "s5_c_im")


def _lower_bound(logits):
    return jnp.cumsum(jax.nn.softmax(logits, axis=0), axis=0)[0:1]


def kernel(x, g_mix, w_in, s5_a_re, s5_a_im, s5_log_dt, s5_b_re, s5_b_im, s5_c_re, s5_c_im, s5_d, w_glu, b_glu, hg_lb_logits, hg_norm_gain, w_pa, w_pb, w_out, g_ffn, w_up, w_conv, b_conv, w_down, g_final, loss_target, m_g_mix, m_w_in, m_s5_a_re, m_s5_a_im, m_s5_log_dt, m_s5_b_re, m_s5_b_im, m_s5_c_re, m_s5_c_im, m_s5_d, m_w_glu, m_b_glu, m_hg_lb_logits, m_hg_norm_gain, m_w_pa, m_w_pb, m_w_out, m_g_ffn, m_w_up, m_w_conv, m_b_conv, m_w_down, m_g_final, v_g_mix, v_w_in, v_s5_a_re, v_s5_a_im, v_s5_log_dt, v_s5_b_re, v_s5_b_im, v_s5_c_re, v_s5_c_im, v_s5_d, v_w_glu, v_b_glu, v_hg_lb_logits, v_hg_norm_gain, v_w_pa, v_w_pb, v_w_out, v_g_ffn, v_w_up, v_w_conv, v_b_conv, v_w_down, v_g_final):
    args = dict(locals())
    w = {n: args[n] for n in WEIGHTS}
    mom = {n: args["m_" + n] for n in WEIGHTS}
    var = {n: args["v_" + n] for n in WEIGHTS}
    nseq, seq, d = x.shape
    xi, yi = lax.axis_index("x"), lax.axis_index("y")
    chip = 2 * xi + yi

    shard = {n: w[n][0] for n in BIG}
    first = [shard["w_in"].astype(BF16), shard["w_glu"].astype(BF16),
             jnp.pad(w_conv[0], ((0, 2 * SUBLANES - CONV_W), (0, 0)))]
    x2 = x.reshape(nseq * seq, d)
    u, late16, got = _prepare(x2, g_mix, [shard[n] for n in LATE], _gather_full_rider(first))
    w_in_all, w_glu_all, conv_all = [lax.dynamic_update_index_in_dim(g, s, chip, 0) for g, s in zip(got, first)]
    p = dict(g_mix=g_mix, g_ffn=g_ffn, g_final=g_final.reshape(1, -1), b_glu=b_glu, gain=hg_norm_gain, s5_d=s5_d,
             b_conv=b_conv, lbrow=_lower_bound(hg_lb_logits),
             s5_a_re=s5_a_re[0], s5_a_im=s5_a_im[0], s5_log_dt=s5_log_dt[0], s5_b_re=s5_b_re[0], s5_b_im=s5_b_im[0],
             s5_c_re=s5_c_re[0], s5_c_im=s5_c_im[0], w_in=w_in_all, w_glu=w_glu_all.reshape(-1, w_glu_all.shape[-1]),
             w_conv=conv_all[:, :CONV_W].transpose(1, 0, 2).reshape(CONV_W, -1))

    dx, halves, sm = _local_step(x2, loss_target.reshape(nseq * seq, d), u, p, dict(zip(LATE, late16)), nseq=nseq, seq=seq)
    loss = sm["loss"][0, 0]

    grads, delta, new_m, new_v = {}, {}, {}, {}
    for n in BIG:
        shp = shard[n].shape
        grads[n], delta[n], new_m[n], new_v[n] = _adamw_halves("adamw_" + n, shard[n], mom[n].reshape(shp),
                                                               var[n].reshape(shp), *halves[n])

    _, disc_vjp = jax.vjp(_s5_discretize, p["s5_a_re"], p["s5_a_im"], p["s5_log_dt"], p["s5_b_re"], p["s5_b_im"])
    da_re, da_im, dlog_dt, db_re, db_im = disc_vjp((sm["lam_re"], sm["lam_im"], sm["bb_re"], sm["bb_im"]))
    _, lb_vjp = jax.vjp(_lower_bound, hg_lb_logits)
    (dlogits,) = lb_vjp(sm["lbrow"])
    fcols = w_conv.shape[-1]
    grads.update(
        g_mix=sm["g_mix"], g_ffn=sm["g_ffn"], g_final=sm["g_final"].reshape(-1), b_glu=sm["b_glu"],
        hg_norm_gain=sm["gain"], hg_lb_logits=dlogits, s5_d=sm["s5_d"], b_conv=sm["b_conv"],
        w_conv=lax.dynamic_slice_in_dim(sm["w_conv"], chip * fcols, fcols, axis=1),
        s5_a_re=da_re, s5_a_im=da_im, s5_log_dt=dlog_dt, s5_b_re=db_re, s5_b_im=db_im,
        s5_c_re=sm["s5_c_re"], s5_c_im=sm["s5_c_im"])
    grads = {n: grads[n].reshape(w[n].shape) for n in WEIGHTS}

    def natural(a):
        return a.reshape(1, -1) if a.ndim == 1 else (a[0] if a.ndim > 2 else a)

    outs = _adamw_small(*[[natural(src[n]) for n in SMALL] for src in (w, grads, mom, var)])
    for dst, group in zip((delta, new_m, new_v), outs):
        dst.update(zip(SMALL, group))
    res = [loss, dx.reshape(x.shape)]
    for group in (grads, delta, new_m, new_v):
        res += [group[n].reshape(w[n].shape) for n in WEIGHTS]
    return tuple(res)
```

```python
import functools
import math

import jax
import jax.numpy as jnp
from jax import lax
from jax.experimental import pallas as pl
from jax.experimental.pallas import tpu as pltpu

F32 = jnp.float32
BF16 = jnp.bfloat16
MESH = pl.DeviceIdType.MESH

EPS = 1e-6
S5_GROUP = 16
S5_STATE = 64
S5_BLOCK_GROUPS = 8
HEAD = 128
CHUNK = 64
CONV_W = 3
LANES = 128
SUBLANES = 8
GATE_BLOCK = 512
VMEM_LIMIT_BYTES = 56 * 1024 * 1024

ADAM_LR = 0.001
ADAM_B1 = 0.9
ADAM_B2 = 0.999
ADAM_EPS = 1e-08
ADAM_WD = 0.01
ADAM_STEP = 10

N_CHIPS = 4
N_DEV = 8


def _params(*sem):
    return pltpu.CompilerParams(dimension_semantics=sem, vmem_limit_bytes=VMEM_LIMIT_BYTES)


class _Rider:
    def __init__(self, arrays, out_shapes, nsem, start, finish, aliases=None):
        self.arrays, self.out_shapes, self.nsem = list(arrays), list(out_shapes), nsem
        self.start, self.finish, self.aliases = start, finish, dict(aliases or {})


def _hosted_call(name, body, *, grid, in_specs, out_specs, out_shape, operands, scratch_shapes=(), rider=None):
    in_specs, out_specs, out_shape, scratch_shapes = list(in_specs), list(out_specs), list(out_shape), list(scratch_shapes)
    cparams = _params(*(["arbitrary"] * len(grid)))
    if rider is None:
        return pl.pallas_call(body, name=name, grid=grid, in_specs=in_specs, out_specs=out_specs, out_shape=out_shape,
                              scratch_shapes=scratch_shapes, compiler_params=cparams)(*operands)
    n_in, n_out, n_sc = len(in_specs), len(out_specs), len(scratch_shapes)
    r_in, r_out = len(rider.arrays), len(rider.out_shapes)

    def hosted(*refs):
        ins, rins = refs[:n_in], refs[n_in:n_in + r_in]
        outs = refs[n_in + r_in:n_in + r_in + n_out]
        routs = refs[n_in + r_in + n_out:n_in + r_in + n_out + r_out]
        rest = refs[n_in + r_in + n_out + r_out:]
        send_sems, recv_sems = rest[n_sc], rest[n_sc + 1]
        first = functools.reduce(jnp.logical_and, [pl.program_id(i) == 0 for i in range(len(grid))])
        last = functools.reduce(jnp.logical_and, [pl.program_id(i) == grid[i] - 1 for i in range(len(grid))])

        @pl.when(first)
        def _():
            rider.start(rins, routs, send_sems, recv_sems)

        body(*ins, *outs, *rest[:n_sc])

        @pl.when(last)
        def _():
            rider.finish(rins, routs, send_sems, recv_sems)

    res = pl.pallas_call(
        hosted, name=name, grid=grid, in_specs=in_specs + [ANY] * r_in, out_specs=out_specs + [ANY] * r_out,
        out_shape=out_shape + rider.out_shapes,
        scratch_shapes=scratch_shapes + [pltpu.SemaphoreType.DMA((rider.nsem,)), pltpu.SemaphoreType.DMA((rider.nsem,))],
        input_output_aliases={n_in + i: n_out + o for i, o in rider.aliases.items()}, compiler_params=cparams,
    )(*operands, *rider.arrays)
    return res[:n_out], res[n_out:]


def _run_rider(name, rider):
    r_in, r_out = len(rider.arrays), len(rider.out_shapes)

    def body(*refs):
        rins, routs, send_sems, recv_sems = refs[:r_in], refs[r_in:r_in + r_out], refs[-2], refs[-1]
        rider.start(rins, routs, send_sems, recv_sems)
        rider.finish(rins, routs, send_sems, recv_sems)

    return pl.pallas_call(
        body, name=name, in_specs=[ANY] * r_in, out_specs=[ANY] * r_out, out_shape=rider.out_shapes,
        scratch_shapes=[pltpu.SemaphoreType.DMA((rider.nsem,)), pltpu.SemaphoreType.DMA((rider.nsem,))],
        input_output_aliases=rider.aliases,
    )(*rider.arrays)


def _row_tile(rows, cap):
    if rows <= cap:
        return rows
    for t in range(cap - cap % 8, 7, -8):
        if rows % t == 0:
            return t
    raise ValueError(f"no row tile for {rows}")


def _dot(a, b):
    return jnp.dot(a.astype(BF16), b.astype(BF16), preferred_element_type=F32)


def _dot_nt(a, b):
    return lax.dot_general(a.astype(BF16), b.astype(BF16), (((1,), (1,)), ((), ())), preferred_element_type=F32)


def _dot_tn(a, b):
    return lax.dot_general(a.astype(BF16), b.astype(BF16), (((0,), (0,)), ((), ())), preferred_element_type=F32)


def _sigmoid(x):
    return 0.5 * jnp.tanh(0.5 * x) + 0.5


_GELU_C = math.sqrt(2.0 / math.pi)


def _gelu(x):
    return 0.5 * x * (1.0 + jnp.tanh(_GELU_C * (x + 0.044715 * x * x * x)))


def _gelu_grad(x):
    th = jnp.tanh(_GELU_C * (x + 0.044715 * x * x * x))
    return 0.5 * (1.0 + th) + 0.5 * x * (1.0 - th * th) * _GELU_C * (1.0 + 3.0 * 0.044715 * x * x)


def _rowwise(name, fn, ins, outs, accs=(), *, rows, tm, ncol=1, rider=None):
    n_in, n_out = len(ins), len(outs)

    def body(*refs):
        res = fn(*[r[...] for r in refs[:n_in]])
        for r, v in zip(refs[n_in:n_in + n_out], res[:n_out]):
            r[...] = v.astype(r.dtype)
        first = pl.program_id(1) == 0
        for r, v in zip(refs[n_in + n_out:], res[n_out:]):
            @pl.when(first)
            def _():
                r[...] = v

            @pl.when(jnp.logical_not(first))
            def _():
                r[...] += v

    in_specs = []
    for _, width, base, kind in ins:
        if kind == "row":
            in_specs.append(pl.BlockSpec((tm, width), lambda j, i, b=base: (i, b + j)))
        else:
            in_specs.append(pl.BlockSpec((1, width), lambda j, i, b=base: (0, b + j)))
    out_specs = [pl.BlockSpec((tm, width), lambda j, i: (i, j)) for _, width, _ in outs]
    out_specs += [pl.BlockSpec((1, width), lambda j, i: (0, j)) for _, width in accs]
    out_shape = [jax.ShapeDtypeStruct((rows, total), dt) for total, _, dt in outs]
    out_shape += [jax.ShapeDtypeStruct((1, total), F32) for total, _ in accs]
    return _hosted_call(name, body, grid=(ncol, rows // tm), in_specs=in_specs, out_specs=out_specs, out_shape=out_shape,
                        operands=[a for a, _, _, _ in ins], rider=rider)


def _mm(name, a, b, *, mode, grid, a_spec, b_spec, o_spec, out_shape, acc_shape, res=None, res_spec=None,
        pair_axis=None, rider=None, epilogue=None):
    nk = grid[2]
    dot = {"nn": _dot, "nt": _dot_nt, "tn": _dot_tn}[mode]
    a_list = list(a) if isinstance(a, tuple) else [a]
    b_list = list(b) if isinstance(b, tuple) else [b]
    na, nb = len(a_list), len(b_list)
    assert (pair_axis is None) == (na + nb == 2)
    direct = nk == 1 and pair_axis is None
    epi_fn, epi_ins, epi_sums = epilogue if epilogue is not None else (None, [], [])
    n_res = 0 if res is None else 1
    n_epi = len(epi_ins)

    def body(*refs):
        a_refs, b_refs = refs[:na], refs[na:na + nb]
        r_ref = None if res is None else refs[na + nb]
        e_refs = refs[na + nb + n_res:na + nb + n_res + n_epi]
        o_ref = refs[na + nb + n_res + n_epi]
        s_refs = refs[na + nb + n_res + n_epi + 1:na + nb + n_res + n_epi + 1 + len(epi_sums)]
        first_rows = pl.program_id(0) == 0

        def finish(v):
            if res is not None:
                v = v + r_ref[...]
            if epi_fn is None:
                o_ref[...] = v.astype(o_ref.dtype)
                return
            outs = epi_fn(v, *[r[...] for r in e_refs])
            o_ref[...] = outs[0].astype(o_ref.dtype)
            for s_ref, part in zip(s_refs, outs[1:]):
                @pl.when(first_rows)
                def _():
                    s_ref[...] = part

                @pl.when(jnp.logical_not(first_rows))
                def _():
                    s_ref[...] += part

        if direct:
            finish(dot(a_refs[0][...], b_refs[0][...]))
            return
        acc_ref = refs[-1]
        k = pl.program_id(2)

        @pl.when(k == 0)
        def _():
            acc_ref[...] = jnp.zeros_like(acc_ref)

        if pair_axis is None:
            acc_ref[...] += dot(a_refs[0][...], b_refs[0][...])
        else:
            lower = pl.program_id(pair_axis) < grid[pair_axis] // 2

            @pl.when(lower)
            def _():
                acc_ref[...] += dot(a_refs[0][...], b_refs[0][...])

            @pl.when(jnp.logical_not(lower))
            def _():
                acc_ref[...] += dot(a_refs[-1][...], b_refs[-1][...])

        @pl.when(k == nk - 1)
        def _():
            finish(acc_ref[...])

    operands = a_list + b_list + ([] if res is None else [res]) + [arr for arr, _ in epi_ins]
    in_specs = (list(a_spec) if na == 2 else [a_spec]) + (list(b_spec) if nb == 2 else [b_spec])
    in_specs += ([] if res is None else [res_spec]) + [spec for _, spec in epi_ins]
    out_specs = [o_spec] + [pl.BlockSpec((1, c), lambda *_: (0, 0)) for c in epi_sums]
    out_shapes = [out_shape] + [jax.ShapeDtypeStruct((1, c), F32) for c in epi_sums]
    got = _hosted_call(name, body, grid=grid, in_specs=in_specs, out_specs=out_specs, out_shape=out_shapes,
                       scratch_shapes=[] if direct else [pltpu.VMEM(acc_shape, F32)], operands=operands, rider=rider)
    mine, rider_outs = (got, None) if rider is None else got
    mine = mine[0] if epilogue is None else tuple(mine)
    return mine if rider is None else (mine, rider_outs)


MM_TILE_BUDGET_BYTES = 36 * 1024 * 1024
MM_TILE_CAP = 2048
ROW_TILE = 1024
GLU_TILE = 1024
ADAMW_TILE = 256


def _mm_tile(t, row_bytes, fixed_bytes):
    cap = max(16, min(MM_TILE_CAP, (MM_TILE_BUDGET_BYTES - fixed_bytes) // row_bytes))
    return _row_tile(t, cap - cap % 16)


def _size(a):
    return jnp.dtype(a.dtype).itemsize


def _mm_fwd_cols(name, a, w3, out_dtype=F32, rider=None):
    t, k = a.shape
    ns = w3.shape[2]
    tm = _mm_tile(t, 2 * k * _size(a) + 2 * ns * jnp.dtype(out_dtype).itemsize, 2 * k * ns * _size(w3))
    return _mm(name, a, w3, mode="nn", grid=(N_CHIPS, t // tm, 1),
               a_spec=pl.BlockSpec((tm, k), lambda j, i, kk: (i, 0)),
               b_spec=pl.BlockSpec((None, k, ns), lambda j, i, kk: (j, 0, 0)),
               o_spec=pl.BlockSpec((tm, ns), lambda j, i, kk: (i, j)),
               out_shape=jax.ShapeDtypeStruct((t, N_CHIPS * ns), out_dtype), acc_shape=(tm, ns), rider=rider)


def _mm_bwd_cols(name, d, w3, out_dtype=F32, rider=None, epilogue=None):
    pair = isinstance(d, tuple)
    t = d[0].shape[0] if pair else d.shape[0]
    k, ns = w3.shape[1], w3.shape[2]
    dsize = _size(d[0] if pair else d)
    tm = _mm_tile(t, (4 if pair else 2) * ns * dsize + 2 * k * jnp.dtype(out_dtype).itemsize + 4 * k
                  + _row_epilogue(epilogue, 8)[1], 2 * k * ns * _size(w3))
    half = N_CHIPS // 2
    if pair:
        a_spec = (pl.BlockSpec((tm, ns), lambda i, j, kk: (i, jnp.minimum(kk, half - 1))),
                  pl.BlockSpec((tm, ns), lambda i, j, kk: (i, jnp.maximum(kk - half, 0))))
    else:
        a_spec = pl.BlockSpec((tm, ns), lambda i, j, kk: (i, kk))
    return _mm(name, d, w3, mode="nt", grid=(t // tm, 1, N_CHIPS), a_spec=a_spec,
               b_spec=pl.BlockSpec((None, k, ns), lambda i, j, kk: (kk, 0, 0)),
               o_spec=pl.BlockSpec((tm, k), lambda i, j, kk: (i, 0)),
               out_shape=jax.ShapeDtypeStruct((t, k), out_dtype), acc_shape=(tm, k), pair_axis=2 if pair else None,
               rider=rider, epilogue=_row_epilogue(epilogue, tm)[0])


def _mm_wgrad_cols(name, a, d, rider=None):
    pair = isinstance(d, tuple)
    t, k = a.shape
    ns = (2 * d[0].shape[1] if pair else d.shape[1]) // N_CHIPS
    dsize = _size(d[0] if pair else d)
    tk = _mm_tile(t, 2 * k * _size(a) + (4 if pair else 2) * ns * dsize, k * ns * (4 + 2 * 2))
    half = N_CHIPS // 2
    if pair:
        b_spec = (pl.BlockSpec((tk, ns), lambda j, i, kk: (jnp.where(j < half, kk, 0), jnp.minimum(j, half - 1))),
                  pl.BlockSpec((tk, ns), lambda j, i, kk: (jnp.where(j < half, 0, kk), jnp.maximum(j - half, 0))))
    else:
        b_spec = pl.BlockSpec((tk, ns), lambda j, i, kk: (kk, j))
    return _mm(name, a, d, mode="tn", grid=(N_CHIPS, 1, t // tk),
               a_spec=pl.BlockSpec((tk, k), lambda j, i, kk: (kk, 0)), b_spec=b_spec,
               o_spec=pl.BlockSpec((None, k, ns), lambda j, i, kk: (j, 0, 0)),
               out_shape=jax.ShapeDtypeStruct((N_CHIPS, k, ns), BF16), acc_shape=(k, ns),
               pair_axis=0 if pair else None, rider=rider)


MM_BLOCK_CAP = 1408


def _row_epilogue(epilogue, tm):
    if epilogue is None:
        return None, 0
    fn, arrays, sums = epilogue
    specs = [pl.BlockSpec((1, x.shape[1]), lambda i, j, kk: (0, 0)) if x.shape[0] == 1 else
             pl.BlockSpec((tm, x.shape[1]), lambda i, j, kk: (i, 0)) for x in arrays]
    return (fn, list(zip(arrays, specs)), list(sums)), sum(2 * x.shape[1] * _size(x) for x in arrays if x.shape[0] > 1)


def _mm_fwd_rows(name, a, w, res=None, out_dtype=F32, epilogue=None):
    t, k = a.shape
    n = w.shape[1]
    tk = k if k <= MM_BLOCK_CAP else MM_BLOCK_CAP
    assert k % tk == 0
    row_bytes = 2 * tk * _size(a) + 2 * n * jnp.dtype(out_dtype).itemsize + (0 if res is None else 2 * n * 4) + 4 * n
    row_bytes += _row_epilogue(epilogue, 8)[1]
    tm = _mm_tile(t, row_bytes, 2 * tk * n * _size(w))
    return _mm(name, a, w, mode="nn", grid=(t // tm, 1, k // tk),
               a_spec=pl.BlockSpec((tm, tk), lambda i, j, kk: (i, kk)),
               b_spec=pl.BlockSpec((tk, n), lambda i, j, kk: (kk, 0)),
               o_spec=pl.BlockSpec((tm, n), lambda i, j, kk: (i, 0)),
               out_shape=jax.ShapeDtypeStruct((t, n), out_dtype), acc_shape=(tm, n),
               res=res, res_spec=None if res is None else pl.BlockSpec((tm, n), lambda i, j, kk: (i, 0)),
               epilogue=_row_epilogue(epilogue, tm)[0])


def _mm_bwd_rows(name, d, w, out_dtype=F32):
    t, n = d.shape
    k = w.shape[0]
    tn = k if k <= MM_BLOCK_CAP else MM_BLOCK_CAP
    assert k % tn == 0
    tm = _mm_tile(t, 2 * n * _size(d) + 2 * tn * jnp.dtype(out_dtype).itemsize, 2 * tn * n * _size(w))
    return _mm(name, d, w, mode="nt", grid=(t // tm, k // tn, 1),
               a_spec=pl.BlockSpec((tm, n), lambda i, j, kk: (i, 0)),
               b_spec=pl.BlockSpec((tn, n), lambda i, j, kk: (j, 0)),
               o_spec=pl.BlockSpec((tm, tn), lambda i, j, kk: (i, j)),
               out_shape=jax.ShapeDtypeStruct((t, k), out_dtype), acc_shape=(tm, tn))


def _mm_wgrad_rows(name, a, d):
    t, k = a.shape
    n = d.shape[1]
    nblk = next(b for b in (1, 2, 4) if (k // b) % LANES == 0 and k // b <= MM_BLOCK_CAP)
    ks = k // nblk
    tk = _mm_tile(t, 2 * ks * _size(a) + 2 * n * _size(d), ks * n * (4 + 2 * 2))
    return _mm(name, a, d, mode="tn", grid=(nblk, 1, t // tk),
               a_spec=pl.BlockSpec((tk, ks), lambda j, i, kk: (kk, j)),
               b_spec=pl.BlockSpec((tk, n), lambda j, i, kk: (kk, 0)),
               o_spec=pl.BlockSpec((ks, n), lambda j, i, kk: (j, 0)),
               out_shape=jax.ShapeDtypeStruct((k, n), BF16), acc_shape=(ks, n))


def _s5_discretize(a_re, a_im, log_dt, b_re, b_im):
    dt = jnp.exp(log_dt)[:, None]
    mag = jnp.exp(a_re * dt)
    ang = a_im * dt
    lb_re = mag * jnp.cos(ang)
    lb_im = mag * jnp.sin(ang)
    den = a_re * a_re + a_im * a_im
    n_re = lb_re - 1.0
    n_im = lb_im
    co_re = ((n_re * a_re + n_im * a_im) / den)[..., None]
    co_im = ((n_im * a_re - n_re * a_im) / den)[..., None]
    bb_re = co_re * b_re - co_im * b_im
    bb_im = co_re * b_im + co_im * b_re
    return lb_re, lb_im, bb_re, bb_im


def _s5_in_blocks(bb):
    g = bb.shape[0]
    nb = g // S5_BLOCK_GROUPS
    t = bb.reshape(nb, S5_BLOCK_GROUPS, S5_STATE, S5_GROUP).transpose(0, 1, 3, 2)
    eye = jnp.eye(S5_BLOCK_GROUPS, dtype=bb.dtype)
    full = t[:, :, :, None, :] * eye[None, :, None, :, None]
    return full.reshape(nb, S5_BLOCK_GROUPS * S5_GROUP, S5_BLOCK_GROUPS * S5_STATE)


def _s5_in_blocks_diag(blocks):
    nb = blocks.shape[0]
    t = blocks.reshape(nb, S5_BLOCK_GROUPS, S5_GROUP, S5_BLOCK_GROUPS, S5_STATE)
    d = jnp.einsum("bghgp->bghp", t)
    return d.transpose(0, 1, 3, 2).reshape(nb * S5_BLOCK_GROUPS, S5_STATE, S5_GROUP)


def _s5_out_blocks(c):
    g = c.shape[0]
    nb = g // S5_BLOCK_GROUPS
    t = c.reshape(nb, S5_BLOCK_GROUPS, S5_GROUP, S5_STATE).transpose(0, 1, 3, 2)
    eye = jnp.eye(S5_BLOCK_GROUPS, dtype=c.dtype)
    full = t[:, :, :, None, :] * eye[None, :, None, :, None]
    return full.reshape(nb, S5_BLOCK_GROUPS * S5_STATE, S5_BLOCK_GROUPS * S5_GROUP)


def _s5_out_blocks_diag(blocks):
    nb = blocks.shape[0]
    t = blocks.reshape(nb, S5_BLOCK_GROUPS, S5_STATE, S5_BLOCK_GROUPS, S5_GROUP)
    d = jnp.einsum("bgpgh->bgph", t)
    return d.transpose(0, 1, 3, 2).reshape(nb * S5_BLOCK_GROUPS, S5_GROUP, S5_STATE)


def _s5_scan_tables(lr, li, reverse):
    def cmul(a, b):
        return a[0] * b[0] - a[1] * b[1], a[0] * b[1] + a[1] * b[0]

    lam = (lr, -li) if reverse else (lr, li)
    pw = [lam]
    for _ in range(SUBLANES - 1):
        pw.append(cmul(pw[-1], lam))
    sub = jnp.arange(SUBLANES)[:, None]
    rows = []
    for s in (1, 2, 4):
        keep = (sub <= SUBLANES - 1 - s) if reverse else (sub >= s)
        rows.append(jnp.where(keep, pw[s - 1][0][None, :], 0.0))
        rows.append(jnp.where(keep, pw[s - 1][1][None, :], 0.0))
    order = list(range(SUBLANES - 1, -1, -1)) if reverse else list(range(SUBLANES))
    rows.append(jnp.stack([pw[i][0] for i in order]))
    rows.append(jnp.stack([pw[i][1] for i in order]))
    return jnp.concatenate(rows, axis=0)


def _s5_scan(vre_ref, vim_ref, coef_ref, seq, width, reverse, xre_ref=None, xim_ref=None):
    nt = seq // SUBLANES
    nl = width // LANES
    per = 2 if xre_ref is None else 4
    sub = lax.broadcasted_iota(jnp.int32, (SUBLANES, LANES), 0)

    def step(k, carry):
        kk = (nt - 1 - k) if reverse else k
        rows = pl.ds(pl.multiple_of(kk * SUBLANES, SUBLANES), SUBLANES)
        out = []
        for j in range(nl):
            lanes = slice(j * LANES, (j + 1) * LANES)
            co = [coef_ref[SUBLANES * q:SUBLANES * (q + 1), lanes] for q in range(8)]
            cr, ci = carry[per * j], carry[per * j + 1]
            vr = vre_ref[rows, lanes]
            vi = vim_ref[rows, lanes]
            for q, s in enumerate((1, 2, 4)):
                sh = SUBLANES - s if reverse else s
                rr = pltpu.roll(vr, sh, 0)
                ri = pltpu.roll(vi, sh, 0)
                ar, ai = co[2 * q], co[2 * q + 1]
                vr, vi = vr + ar * rr - ai * ri, vi + ar * ri + ai * rr
            edge = 0 if reverse else SUBLANES - 1
            cbr = jnp.broadcast_to(cr[edge:edge + 1, :], (SUBLANES, LANES))
            cbi = jnp.broadcast_to(ci[edge:edge + 1, :], (SUBLANES, LANES))
            pr, pi = co[6], co[7]
            vr, vi = vr + pr * cbr - pi * cbi, vi + pr * cbi + pi * cbr
            vre_ref[rows, lanes] = vr
            vim_ref[rows, lanes] = vi
            out += [vr, vi]
            if xre_ref is not None:
                nr = jnp.where(sub == SUBLANES - 1, cbr, pltpu.roll(vr, SUBLANES - 1, 0))
                ni = jnp.where(sub == SUBLANES - 1, cbi, pltpu.roll(vi, SUBLANES - 1, 0))
                xr = xre_ref[rows, lanes]
                xi = xim_ref[rows, lanes]
                out += [carry[per * j + 2] + nr * xr + ni * xi, carry[per * j + 3] + ni * xr - nr * xi]
        return tuple(out)

    zero = jnp.zeros((SUBLANES, LANES), F32)
    res = lax.fori_loop(0, nt, step, (zero,) * (per * nl))
    if xre_ref is None:
        return None
    return jnp.concatenate(
        [jnp.concatenate([jnp.sum(res[per * j + 2], axis=0, keepdims=True) for j in range(nl)], axis=1),
         jnp.concatenate([jnp.sum(res[per * j + 3], axis=0, keepdims=True) for j in range(nl)], axis=1)], axis=0)


def _s5_fwd(z, bre3, bim3, cre3, cim3, coef, dskip, *, nseq, seq, rider=None):
    nb = bre3.shape[0]
    ch, ns = bre3.shape[1], bre3.shape[2]

    def body(za_ref, bre_ref, bim_ref, cre_ref, cim_ref, coef_ref, d_ref, y_ref, xre_ref, xim_ref):
        za = za_ref[...]
        xre_ref[...] = _dot(za, bre_ref[...])
        xim_ref[...] = _dot(za, bim_ref[...])
        _s5_scan(xre_ref, xim_ref, coef_ref, seq, ns, False)
        y_ref[...] = _dot(xre_ref[...], cre_ref[...]) - _dot(xim_ref[...], cim_ref[...]) + d_ref[...] * za

    blk3 = lambda r, c: pl.BlockSpec((None, r, c), lambda b, j: (j, 0, 0))
    return _hosted_call(
        "s5_fwd", body, grid=(nseq, nb),
        in_specs=[pl.BlockSpec((seq, ch), lambda b, j: (b, j)), blk3(ch, ns), blk3(ch, ns), blk3(ns, ch), blk3(ns, ch),
                  pl.BlockSpec((8 * SUBLANES, ns), lambda b, j: (0, j)), pl.BlockSpec((1, ch), lambda b, j: (0, j))],
        out_specs=[pl.BlockSpec((seq, ch), lambda b, j: (b, j)), pl.BlockSpec((seq, ns), lambda b, j: (b, j)),
                   pl.BlockSpec((seq, ns), lambda b, j: (b, j))],
        out_shape=[jax.ShapeDtypeStruct((nseq * seq, nb * ch), F32), jax.ShapeDtypeStruct((nseq * seq, nb * ns), F32),
                   jax.ShapeDtypeStruct((nseq * seq, nb * ns), F32)],
        operands=(z, bre3, bim3, cre3, cim3, coef, dskip), rider=rider)


def _s5_bwd(dy, z, xre, xim, bre3, bim3, cre3, cim3, coef_rev, dskip, *, nseq, seq, rider=None):
    nb = bre3.shape[0]
    ch, ns = bre3.shape[1], bre3.shape[2]

    def body(dy_ref, za_ref, xre_ref, xim_ref, bre_ref, bim_ref, cre_ref, cim_ref, coef_ref, d_ref,
             dza_ref, dbre_ref, dbim_ref, dcre_ref, dcim_ref, dlam_ref, dd_ref, are_ref, aim_ref):
        dy = dy_ref[...]
        za = za_ref[...]
        are_ref[...] = _dot_nt(dy, cre_ref[...])
        aim_ref[...] = -_dot_nt(dy, cim_ref[...])
        dlam = _s5_scan(are_ref, aim_ref, coef_ref, seq, ns, True, xre_ref, xim_ref)
        are = are_ref[...]
        aim = aim_ref[...]
        dza_ref[...] = (_dot_nt(are, bre_ref[...]) + _dot_nt(aim, bim_ref[...]) + d_ref[...] * dy).astype(dza_ref.dtype)
        parts = (_dot_tn(za, are), _dot_tn(za, aim), _dot_tn(xre_ref[...], dy), -_dot_tn(xim_ref[...], dy),
                 dlam, jnp.sum(dy * za, axis=0, keepdims=True))
        first = pl.program_id(1) == 0
        for r, v in zip((dbre_ref, dbim_ref, dcre_ref, dcim_ref, dlam_ref, dd_ref), parts):
            @pl.when(first)
            def _():
                r[...] = v

            @pl.when(jnp.logical_not(first))
            def _():
                r[...] += v

    blk3 = lambda r, c: pl.BlockSpec((None, r, c), lambda j, b: (j, 0, 0))
    tok = lambda c: pl.BlockSpec((seq, c), lambda j, b: (b, j))
    return _hosted_call(
        "s5_bwd", body, grid=(nb, nseq),
        in_specs=[tok(ch), tok(ch), tok(ns), tok(ns), blk3(ch, ns), blk3(ch, ns), blk3(ns, ch), blk3(ns, ch),
                  pl.BlockSpec((8 * SUBLANES, ns), lambda j, b: (0, j)), pl.BlockSpec((1, ch), lambda j, b: (0, j))],
        out_specs=[tok(ch), blk3(ch, ns), blk3(ch, ns), blk3(ns, ch), blk3(ns, ch),
                   pl.BlockSpec((None, 2, ns), lambda j, b: (j, 0, 0)), pl.BlockSpec((1, ch), lambda j, b: (0, j))],
        out_shape=[jax.ShapeDtypeStruct((nseq * seq, nb * ch), BF16),
                   jax.ShapeDtypeStruct((nb, ch, ns), F32), jax.ShapeDtypeStruct((nb, ch, ns), F32),
                   jax.ShapeDtypeStruct((nb, ns, ch), F32), jax.ShapeDtypeStruct((nb, ns, ch), F32),
                   jax.ShapeDtypeStruct((nb, 2, ns), F32), jax.ShapeDtypeStruct((1, nb * ch), F32)],
        scratch_shapes=[pltpu.VMEM((seq, ns), F32), pltpu.VMEM((seq, ns), F32)],
        operands=(dy, z, xre, xim, bre3, bim3, cre3, cim3, coef_rev, dskip), rider=rider)


def _glu_fwd(y, wglu, bglu):
    t, w = y.shape
    tm = _row_tile(t, GLU_TILE)

    def body(y_ref, w_ref, b_ref, a0_ref, gl_ref, a_ref):
        a0 = _gelu(y_ref[...])
        gl = _dot(a0, w_ref[...])
        a0_ref[...] = a0.astype(a0_ref.dtype)
        gl_ref[...] = gl
        a_ref[...] = (a0 * _sigmoid(gl + b_ref[...])).astype(a_ref.dtype)

    tok = pl.BlockSpec((tm, w), lambda i: (i, 0))
    return pl.pallas_call(
        body, name="s5_glu", grid=(t // tm,),
        in_specs=[tok, pl.BlockSpec((w, w), lambda i: (0, 0)), pl.BlockSpec((1, w), lambda i: (0, 0))],
        out_specs=[tok, tok, tok],
        out_shape=[jax.ShapeDtypeStruct((t, w), BF16), jax.ShapeDtypeStruct((t, w), F32), jax.ShapeDtypeStruct((t, w), BF16)],
        compiler_params=_params("arbitrary"),
    )(y, wglu, bglu)


def _glu_bwd(y, gl, bglu, da, wglu):
    t, w = y.shape
    tm = _row_tile(t, GLU_TILE)

    def body(y_ref, gl_ref, b_ref, da_ref, w_ref, dgl_ref, dy_ref, db_ref):
        yv = y_ref[...]
        dav = da_ref[...]
        s = _sigmoid(gl_ref[...] + b_ref[...])
        dgl = dav * _gelu(yv) * s * (1.0 - s)
        dgl_ref[...] = dgl.astype(dgl_ref.dtype)
        dy_ref[...] = (dav * s + _dot_nt(dgl, w_ref[...])) * _gelu_grad(yv)
        part = jnp.sum(dgl, axis=0, keepdims=True)
        first = pl.program_id(0) == 0

        @pl.when(first)
        def _():
            db_ref[...] = part

        @pl.when(jnp.logical_not(first))
        def _():
            db_ref[...] += part

    tok = pl.BlockSpec((tm, w), lambda i: (i, 0))
    vec = pl.BlockSpec((1, w), lambda i: (0, 0))
    return pl.pallas_call(
        body, name="s5_glu_bwd", grid=(t // tm,),
        in_specs=[tok, tok, vec, tok, pl.BlockSpec((w, w), lambda i: (0, 0))], out_specs=[tok, tok, vec],
        out_shape=[jax.ShapeDtypeStruct((t, w), BF16), jax.ShapeDtypeStruct((t, w), F32), jax.ShapeDtypeStruct((1, w), F32)],
        compiler_params=_params("arbitrary"),
    )(y, gl, bglu, da, wglu)


def _cumsum_rows(x, reverse=False):
    n = x.shape[0]
    row = lax.broadcasted_iota(jnp.int32, x.shape, 0)
    s = 1
    while s < n:
        if reverse:
            x = x + jnp.where(row < n - s, pltpu.roll(x, n - s, 0), 0.0)
        else:
            x = x + jnp.where(row >= s, pltpu.roll(x, s, 0), 0.0)
        s *= 2
    return x


def _hg_gates(zq, zf, lb):
    sg = _sigmoid(zf)
    f = lb + (1.0 - lb) * sg
    sq = _sigmoid(zq)
    qa = zq * sq * (HEAD ** -0.5)
    b = _cumsum_rows(jnp.log(f))
    return sg, f, sq, qa, 1.0 - f, b


SUB = 16


def _hg_scores(qa, kk, b):
    c = qa.shape[0]
    row = lax.broadcasted_iota(jnp.int32, qa.shape, 0)
    pos = jnp.bitwise_and(row, SUB - 1)
    dmat = lax.broadcasted_iota(jnp.int32, (c, c), 0) - lax.broadcasted_iota(jnp.int32, (c, c), 1)
    p = jnp.zeros((c, c), F32)
    for d in range(SUB):
        if d == 0:
            fd = qa * kk
        else:
            e = jnp.exp(jnp.minimum(b - pltpu.roll(b, d, 0), 0.0))
            fd = jnp.where(pos >= d, qa * pltpu.roll(kk, d, 0) * e, 0.0)
        p = jnp.where(dmat == d, jnp.sum(fd, axis=1, keepdims=True), p)
    col = lax.broadcasted_iota(jnp.int32, (SUB, c), 1)
    blocks = [jnp.zeros((SUB, c), F32)]
    for r0 in range(SUB, c, SUB):
        beta = b[r0 - 1:r0, :]
        qt = qa[r0:r0 + SUB] * jnp.exp(b[r0:r0 + SUB] - beta)
        kt = kk * jnp.exp(jnp.minimum(beta - b, 0.0))
        blocks.append(jnp.where(col < r0, _dot_nt(qt, kt), 0.0))
    return p + jnp.concatenate(blocks, axis=0)


def _hg_scores_bwd(dp, qa, kk, b):
    c = qa.shape[0]
    row = lax.broadcasted_iota(jnp.int32, qa.shape, 0)
    pos = jnp.bitwise_and(row, SUB - 1)
    dmat = lax.broadcasted_iota(jnp.int32, (c, c), 0) - lax.broadcasted_iota(jnp.int32, (c, c), 1)
    dqa = jnp.zeros_like(qa)
    dkk = jnp.zeros_like(qa)
    db = jnp.zeros_like(qa)
    for d in range(SUB):
        dcol = jnp.sum(jnp.where(dmat == d, dp, 0.0), axis=1, keepdims=True)
        if d == 0:
            dqa = dqa + dcol * kk
            dkk = dkk + dcol * qa
        else:
            e = jnp.exp(jnp.minimum(b - pltpu.roll(b, d, 0), 0.0))
            w = jnp.where(pos >= d, dcol * e, 0.0)
            kr = pltpu.roll(kk, d, 0)
            dqa = dqa + w * kr
            tmp = w * qa
            dkk = dkk + pltpu.roll(tmp, c - d, 0)
            x = tmp * kr
            db = db + x - pltpu.roll(x, c - d, 0)
    col = lax.broadcasted_iota(jnp.int32, (SUB, c), 1)
    dq_blocks = [jnp.zeros((SUB, qa.shape[1]), F32)]
    db_blocks = [jnp.zeros((SUB, qa.shape[1]), F32)]
    for r0 in range(SUB, c, SUB):
        beta = b[r0 - 1:r0, :]
        eq = jnp.exp(b[r0:r0 + SUB] - beta)
        ek = jnp.exp(jnp.minimum(beta - b, 0.0))
        qt = qa[r0:r0 + SUB] * eq
        kt = kk * ek
        dpi = jnp.where(col < r0, dp[r0:r0 + SUB, :], 0.0)
        dqt = _dot(dpi, kt)
        dkt = _dot_tn(dpi, qt)
        dq_blocks.append(dqt * eq)
        db_blocks.append(dqt * qt)
        dkk = dkk + dkt * ek
        db = db - dkt * kt
    return dqa + jnp.concatenate(dq_blocks, axis=0), dkk, db + jnp.concatenate(db_blocks, axis=0)


def _hg_chunks_per_step(seq):
    nc = seq // CHUNK
    cps = next(k for k in (4, 2, 1) if nc % k == 0)
    return nc, cps, nc // cps


def _hg_fwd(z, lbrow, gain, *, nseq, seq, heads, qoff, rider=None):
    nc, cps, nblk = _hg_chunks_per_step(seq)
    blk = cps * CHUNK
    zspec = lambda off: pl.BlockSpec((blk, HEAD), lambda h, b, n, off=off: (b * nblk + n, off + h))

    def body(zq_ref, zf_ref, zi_ref, zg_ref, lb_ref, gn_ref, o_ref, yb_ref, st_ref, sc_ref, state):
        @pl.when(pl.program_id(2) == 0)
        def _():
            state[...] = jnp.zeros_like(state)

        lb = lb_ref[...]
        gain_v = gn_ref[...]

        def chunk(ci, carry):
            rows = pl.ds(pl.multiple_of(ci * CHUNK, CHUNK), CHUNK)
            st = state[...]
            st_ref[ci] = st
            zi = zi_ref[rows, :]
            zg = zg_ref[rows, :]
            _, _, _, qa, kk, b = _hg_gates(zq_ref[rows, :], zf_ref[rows, :], lb)
            scores = _hg_scores(qa, kk, b).astype(BF16)
            sc_ref[rows, :] = scores
            o = _dot_nt(qa * jnp.exp(b), st) + _dot(scores, zi)
            bl = b[CHUNK - 1:CHUNK, :]
            state[...] = st * jnp.exp(bl) + _dot_tn(zi, kk * jnp.exp(bl - b))
            o_ref[rows, :] = o
            r = lax.rsqrt(jnp.mean(o * o, axis=1, keepdims=True) + EPS)
            yb_ref[rows, :] = (o * r * gain_v * zg * _sigmoid(zg)).astype(yb_ref.dtype)
            return carry

        lax.fori_loop(0, cps, chunk, 0, unroll=True)

    tok = pl.BlockSpec((blk, HEAD), lambda h, b, n: (b * nblk + n, h))
    vec = pl.BlockSpec((1, HEAD), lambda h, b, n: (0, h))
    rows = nseq * seq
    return _hosted_call(
        "hgrn2_fwd", body, grid=(heads, nseq, nblk),
        in_specs=[zspec(qoff), zspec(qoff + heads), zspec(qoff + 2 * heads), zspec(qoff + 3 * heads), vec, vec],
        out_specs=[tok, tok, pl.BlockSpec((None, None, cps, HEAD, HEAD), lambda h, b, n: (h, b, n, 0, 0)),
                   pl.BlockSpec((None, blk, CHUNK), lambda h, b, n: (h, b * nblk + n, 0))],
        out_shape=[jax.ShapeDtypeStruct((rows, heads * HEAD), F32), jax.ShapeDtypeStruct((rows, heads * HEAD), BF16),
                   jax.ShapeDtypeStruct((heads, nseq, nc, HEAD, HEAD), F32),
                   jax.ShapeDtypeStruct((heads, rows, CHUNK), BF16)],
        scratch_shapes=[pltpu.VMEM((HEAD, HEAD), F32)], operands=(z, z, z, z, lbrow, gain), rider=rider)


def _hg_bwd(dyb, z, o, states, scores, lbrow, gain, *, nseq, seq, heads, qoff, rider=None):
    nc, cps, nblk = _hg_chunks_per_step(seq)
    blk = cps * CHUNK
    rev = lambda n: nblk - 1 - n
    zspec = lambda off: pl.BlockSpec((blk, HEAD), lambda h, b, n, off=off: (b * nblk + rev(n), off + h))

    def body(dyb_ref, zq_ref, zf_ref, zi_ref, zg_ref, o_ref, st_ref, sc_ref, lb_ref, gn_ref,
             dzq_ref, dzf_ref, dzi_ref, dzg_ref, dlb_ref, dgn_ref, dstate):
        @pl.when(pl.program_id(2) == 0)
        def _():
            dstate[...] = jnp.zeros_like(dstate)

        @pl.when(jnp.logical_and(pl.program_id(1) == 0, pl.program_id(2) == 0))
        def _():
            dlb_ref[...] = jnp.zeros_like(dlb_ref)
            dgn_ref[...] = jnp.zeros_like(dgn_ref)

        lb = lb_ref[...]
        gain_v = gn_ref[...]
        c = CHUNK
        causal = lax.broadcasted_iota(jnp.int32, (c, c), 0) >= lax.broadcasted_iota(jnp.int32, (c, c), 1)

        def chunk(step, carry):
            ci = cps - 1 - step
            rows = pl.ds(pl.multiple_of(ci * CHUNK, CHUNK), CHUNK)
            zq = zq_ref[rows, :]
            zi = zi_ref[rows, :]
            zg = zg_ref[rows, :]
            sg, f, sq, qa, kk, b = _hg_gates(zq, zf_ref[rows, :], lb)
            eb = jnp.exp(b)
            qt = qa * eb
            bl = b[c - 1:c, :]
            ebl = jnp.exp(bl)
            ekb = jnp.exp(bl - b)
            kh = kk * ekb
            st = st_ref[ci]
            dst = dstate[...]
            o = o_ref[rows, :]
            r = lax.rsqrt(jnp.mean(o * o, axis=1, keepdims=True) + EPS)
            oh = o * r
            szg = _sigmoid(zg)
            dyb = dyb_ref[rows, :]
            don = dyb * zg * szg
            dzg_ref[rows, :] = (dyb * oh * gain_v * szg * (1.0 + zg * (1.0 - szg))).astype(dzg_ref.dtype)
            doh = don * gain_v
            do = r * (doh - oh * jnp.mean(doh * oh, axis=1, keepdims=True))
            dqt = _dot(do, st)
            dp = jnp.where(causal, _dot_nt(do, zi), 0.0)
            dzi_ref[rows, :] = (_dot_tn(sc_ref[rows, :], do) + _dot_nt(kh, dst)).astype(dzi_ref.dtype)
            dkh = _dot(zi, dst)
            dbl = jnp.sum(dkh * kh, axis=0, keepdims=True) + jnp.sum(dst * st, axis=0, keepdims=True) * ebl
            dstate[...] = _dot_tn(do, qt) + dst * ebl
            dqa_s, dkk_s, db_s = _hg_scores_bwd(dp, qa, kk, b)
            dqa = dqt * eb + dqa_s
            dkk = dkh * ekb + dkk_s
            db = dqt * qt - dkh * kh + db_s
            row = lax.broadcasted_iota(jnp.int32, db.shape, 0)
            db = db + jnp.where(row == c - 1, dbl, 0.0)
            df = _cumsum_rows(db, reverse=True) / f - dkk
            dzf_ref[rows, :] = (df * (1.0 - lb) * sg * (1.0 - sg)).astype(dzf_ref.dtype)
            dzq_ref[rows, :] = (dqa * (HEAD ** -0.5) * sq * (1.0 + zq * (1.0 - sq))).astype(dzq_ref.dtype)
            dlb_ref[...] += jnp.sum(df * (1.0 - sg), axis=0, keepdims=True)
            dgn_ref[...] += jnp.sum(don * oh, axis=0, keepdims=True)
            return carry

        lax.fori_loop(0, cps, chunk, 0, unroll=True)

    tok = pl.BlockSpec((blk, HEAD), lambda h, b, n: (b * nblk + rev(n), h))
    vec = pl.BlockSpec((1, HEAD), lambda h, b, n: (0, h))
    rows = nseq * seq
    return _hosted_call(
        "hgrn2_bwd", body, grid=(heads, nseq, nblk),
        in_specs=[tok, zspec(qoff), zspec(qoff + heads), zspec(qoff + 2 * heads), zspec(qoff + 3 * heads), tok,
                  pl.BlockSpec((None, None, cps, HEAD, HEAD), lambda h, b, n: (h, b, rev(n), 0, 0)),
                  pl.BlockSpec((None, blk, CHUNK), lambda h, b, n: (h, b * nblk + rev(n), 0)), vec, vec],
        out_specs=[tok, tok, tok, tok, vec, vec],
        out_shape=[jax.ShapeDtypeStruct((rows, heads * HEAD), BF16)] * 4
        + [jax.ShapeDtypeStruct((1, heads * HEAD), F32)] * 2,
        scratch_shapes=[pltpu.VMEM((HEAD, HEAD), F32)],
        operands=(dyb, z, z, z, z, o, states, scores, lbrow, gain), rider=rider)


def _shift_rows(x, k):
    n = x.shape[0]
    r = pltpu.roll(x, k % n, 0)
    sub = lax.broadcasted_iota(jnp.int32, (SUBLANES, x.shape[1]), 0)
    if k > 0:
        return jnp.concatenate([jnp.where(sub >= k, r[0:SUBLANES], 0.0), r[SUBLANES:]], axis=0)
    return jnp.concatenate([r[:n - SUBLANES], jnp.where(sub < SUBLANES + k, r[n - SUBLANES:], 0.0)], axis=0)


def _conv_taps(h, w, bias):
    h1 = _shift_rows(h, 1)
    h2 = _shift_rows(h, 2)
    return h2 * w[0:1, :] + h1 * w[1:2, :] + h * w[2:3, :] + bias, h1, h2


def _conv_fwd(h, wconv, bconv, *, nseq, seq):
    ff2 = h.shape[1]
    ncol = ff2 // 2 // LANES

    def body(hg_ref, hv_ref, wg_ref, wv_ref, bg_ref, bv_ref, a_ref):
        g, _, _ = _conv_taps(hg_ref[...].astype(F32), wg_ref[...], bg_ref[...])
        v, _, _ = _conv_taps(hv_ref[...].astype(F32), wv_ref[...], bv_ref[...])
        a_ref[...] = (g * _sigmoid(g) * v).astype(a_ref.dtype)

    tok = lambda off: pl.BlockSpec((seq, LANES), lambda j, b, off=off: (b, off + j))
    wsp = lambda off: pl.BlockSpec((CONV_W, LANES), lambda j, b, off=off: (0, off + j))
    bsp = lambda off: pl.BlockSpec((1, LANES), lambda j, b, off=off: (0, off + j))
    return pl.pallas_call(
        body, name="conv_fwd", grid=(ncol, nseq),
        in_specs=[tok(0), tok(ncol), wsp(0), wsp(ncol), bsp(0), bsp(ncol)],
        out_specs=tok(0), out_shape=jax.ShapeDtypeStruct((nseq * seq, ff2 // 2), BF16),
        compiler_params=_params("arbitrary", "arbitrary"),
    )(h, h, wconv, wconv, bconv, bconv)


def _conv_bwd(da, h, wconv, bconv, *, nseq, seq):
    ff2 = h.shape[1]
    ncol = ff2 // 2 // LANES

    def half_bwd(d, hcur, h1, h2, w):
        d1 = _shift_rows(d, -1)
        d2 = _shift_rows(d, -2)
        dh = d * w[2:3, :] + d1 * w[1:2, :] + d2 * w[0:1, :]
        stats = jnp.concatenate(
            [jnp.sum(h2 * d, axis=0, keepdims=True), jnp.sum(h1 * d, axis=0, keepdims=True),
             jnp.sum(hcur * d, axis=0, keepdims=True), jnp.sum(d, axis=0, keepdims=True),
             jnp.zeros((SUBLANES - 4, d.shape[1]), F32)], axis=0)
        return dh, stats

    def body(da_ref, hg_ref, hv_ref, wg_ref, wv_ref, bg_ref, bv_ref, dhg_ref, dhv_ref, sg_ref, sv_ref):
        hg = hg_ref[...].astype(F32)
        hv = hv_ref[...].astype(F32)
        wg = wg_ref[...]
        wv = wv_ref[...]
        g, g1, g2 = _conv_taps(hg, wg, bg_ref[...])
        v, v1, v2 = _conv_taps(hv, wv, bv_ref[...])
        da = da_ref[...].astype(F32)
        s = _sigmoid(g)
        dhg, stg = half_bwd(da * v * s * (1.0 + g * (1.0 - s)), hg, g1, g2, wg)
        dhv, stv = half_bwd(da * g * s, hv, v1, v2, wv)
        dhg_ref[...] = dhg.astype(dhg_ref.dtype)
        dhv_ref[...] = dhv.astype(dhv_ref.dtype)
        first = pl.program_id(1) == 0
        for r, val in ((sg_ref, stg), (sv_ref, stv)):
            @pl.when(first)
            def _():
                r[...] = val

            @pl.when(jnp.logical_not(first))
            def _():
                r[...] += val

    tok = lambda off: pl.BlockSpec((seq, LANES), lambda j, b, off=off: (b, off + j))
    wsp = lambda off: pl.BlockSpec((CONV_W, LANES), lambda j, b, off=off: (0, off + j))
    bsp = lambda off: pl.BlockSpec((1, LANES), lambda j, b, off=off: (0, off + j))
    ssp = pl.BlockSpec((SUBLANES, LANES), lambda j, b: (0, j))
    dhg, dhv, stg, stv = pl.pallas_call(
        body, name="conv_bwd", grid=(ncol, nseq),
        in_specs=[tok(0), tok(0), tok(ncol), wsp(0), wsp(ncol), bsp(0), bsp(ncol)],
        out_specs=[tok(0), tok(0), ssp, ssp],
        out_shape=[jax.ShapeDtypeStruct((nseq * seq, ff2 // 2), BF16)] * 2
        + [jax.ShapeDtypeStruct((SUBLANES, ff2 // 2), F32)] * 2,
        compiler_params=_params("arbitrary", "arbitrary"),
    )(da, h, h, wconv, wconv, bconv, bconv)
    return (dhg, dhv), jnp.concatenate([stg, stv], axis=1)


def _rms_fwd(xv, g):
    r = lax.rsqrt(jnp.mean(xv * xv, axis=1, keepdims=True) + EPS)
    return (xv * r * g,)


def _rms_bwd(xv, g, dy, res):
    r = lax.rsqrt(jnp.mean(xv * xv, axis=1, keepdims=True) + EPS)
    xh = xv * r
    dxh = dy * g
    dx = r * (dxh - xh * jnp.mean(dxh * xh, axis=1, keepdims=True)) + res
    return dx, jnp.sum(dy * xh, axis=0, keepdims=True)


def _loss_head(x2, tgt, g):
    d = x2.shape[1]
    r = lax.rsqrt(jnp.mean(x2 * x2, axis=1, keepdims=True) + EPS)
    xh = x2 * r
    err = xh * g - tgt
    dy = err * (1.0 / d)
    dxh = dy * g
    dx = r * (dxh - xh * jnp.mean(dxh * xh, axis=1, keepdims=True))
    loss = 0.5 * jnp.sum(jnp.mean(err * err, axis=1, keepdims=True), axis=0, keepdims=True)
    return dx, jnp.sum(dy * xh, axis=0, keepdims=True), jnp.broadcast_to(loss, (1, LANES))


LATE_A = ("w_down", "w_out")
LATE_B = ("w_up", "w_pa", "w_pb")
LATE = LATE_A + LATE_B
EARLY_GRADS = ("w_down", "w_up", "w_out", "w_pa", "w_pb", "w_glu")
ROW_SHARDED = ("w_glu", "w_out", "w_down")


def _local_step(x, tgt, u, p, late, *, nseq, seq):
    p = dict(p)
    chip = 2 * lax.axis_index("x") + lax.axis_index("y")
    t, d = x.shape
    s5w = p["s5_d"].shape[1]
    hgw = p["gain"].shape[1]
    heads = hgw // HEAD
    qoff = s5w // LANES
    gblk = (s5w + 4 * hgw) // GATE_BLOCK
    ngb = d // GATE_BLOCK
    tm = _row_tile(t, ROW_TILE)
    row = lambda a, w=None, base=0: (a, a.shape[1] if w is None else w, base, "row")
    vec = lambda a, w=None, base=0: (a, a.shape[1] if w is None else w, base, "vec")
    rw = functools.partial(_rowwise, rows=t, tm=tm)

    z, landed_a = _mm_fwd_cols("in_proj", u, p["w_in"], rider=_gather_ici_rider([late[n] for n in LATE_A]))

    lam_re, lam_im, bb_re, bb_im = _s5_discretize(p["s5_a_re"], p["s5_a_im"], p["s5_log_dt"], p["s5_b_re"], p["s5_b_im"])
    bre3 = _s5_in_blocks(bb_re).astype(BF16)
    bim3 = _s5_in_blocks(bb_im).astype(BF16)
    cre3 = _s5_out_blocks(p["s5_c_re"]).astype(BF16)
    cim3 = _s5_out_blocks(p["s5_c_im"]).astype(BF16)
    coef_f = _s5_scan_tables(lam_re.reshape(-1), lam_im.reshape(-1), False)
    coef_r = _s5_scan_tables(lam_re.reshape(-1), lam_im.reshape(-1), True)
    (o, yb, states, scores), landed_b = _hg_fwd(z, p["lbrow"], p["gain"], nseq=nseq, seq=seq, heads=heads, qoff=qoff,
                                                rider=_gather_ici_rider([late[n] for n in LATE_B]))
    (y5, xre, xim), gathered = _s5_fwd(z, bre3, bim3, cre3, cim3, coef_f, p["s5_d"], nseq=nseq, seq=seq,
                                       rider=_gather_pass_rider(list(landed_a) + list(landed_b)))
    for n, g in zip(LATE, gathered):
        full = lax.dynamic_update_index_in_dim(g, late[n], chip, 0)
        p[n] = full.reshape(-1, full.shape[-1]) if n in ROW_SHARDED else full
    ya0, gl, ya = _glu_fwd(y5, p["w_glu"], p["b_glu"])

    joined = lambda w3: w3.transpose(1, 0, 2).reshape(w3.shape[1], -1)
    split = lambda g: g.reshape(g.shape[0], N_CHIPS, -1).transpose(1, 0, 2)
    wpa, wpb = joined(p["w_pa"]), joined(p["w_pb"])
    pa = _mm_fwd_rows("proj_a", ya, wpa, out_dtype=BF16)
    pb = _mm_fwd_rows("proj_b", yb, wpb, out_dtype=BF16)
    gb = GATE_BLOCK
    (m,) = rw("merge", lambda ga, gbv, a, b: (_sigmoid(ga) * a + _sigmoid(gbv) * b,),
              [row(z, gb, gblk), row(z, gb, gblk + ngb), row(pa, gb), row(pb, gb)], [(d, gb, BF16)], ncol=ngb)
    x1 = _mm_fwd_rows("out_proj", m, p["w_out"], res=x)

    (u2,) = rw("rms_ffn", _rms_fwd, [row(x1), vec(p["g_ffn"])], [(d, d, BF16)])
    h = _mm_fwd_cols("up_proj", u2, p["w_up"], out_dtype=BF16)
    a = _conv_fwd(h, p["w_conv"], p["b_conv"], nseq=nseq, seq=seq)
    dx2, dg_final, lossv = _mm_fwd_rows("down_proj", a, p["w_down"], res=x1,
                                        epilogue=(_loss_head, [tgt, p["g_final"]], [d, LANES]))

    norm_bwd = lambda dyv, xv, g, resv: _rms_bwd(xv, g, dyv, resv)
    da = _mm_bwd_rows("down_bwd", dx2, p["w_down"], out_dtype=BF16)
    g_wdown = _mm_wgrad_rows("down_wgrad", a, dx2)
    dh, cstats = _conv_bwd(da, h, p["w_conv"], p["b_conv"], nseq=nseq, seq=seq)
    dx1, dg_ffn = _mm_bwd_cols("up_bwd", dh, p["w_up"], epilogue=(norm_bwd, [x1, p["g_ffn"], dx2], [d]))
    g_wup = _mm_wgrad_cols("up_wgrad", u2, dh)

    dm = _mm_bwd_rows("out_bwd", dx1, p["w_out"], out_dtype=BF16)
    g_wout = _mm_wgrad_rows("out_wgrad", m, dx1)

    def merge_bwd(ga, gbv, av, bv, dmv):
        sa = _sigmoid(ga)
        sb = _sigmoid(gbv)
        return dmv * sa, dmv * sb, dmv * av * sa * (1.0 - sa), dmv * bv * sb * (1.0 - sb)

    dpa, dpb, dzga, dzgb = rw("merge_bwd", merge_bwd,
                              [row(z, gb, gblk), row(z, gb, gblk + ngb), row(pa, gb), row(pb, gb), row(dm, gb)],
                              [(d, gb, BF16)] * 4, ncol=ngb)
    dya = _mm_bwd_rows("proj_a_bwd", dpa, wpa)
    g_wpa = split(_mm_wgrad_rows("proj_a_wgrad", ya, dpa))
    dyb = _mm_bwd_rows("proj_b_bwd", dpb, wpb)
    g_wpb = split(_mm_wgrad_rows("proj_b_wgrad", yb, dpb))

    dgl, dy5, db_glu = _glu_bwd(y5, gl, p["b_glu"], dya, p["w_glu"])
    g_wglu = _mm_wgrad_rows("glu_wgrad", ya0, dgl)
    partial = dict(w_down=g_wdown, w_up=g_wup, w_out=g_wout, w_pa=g_wpa, w_pb=g_wpb, w_glu=g_wglu)
    parts = [_grad_parts(partial[n]) for n in EARLY_GRADS]
    (dza, dbre3, dbim3, dcre3, dcim3, dlam, dd), sib = _s5_bwd(
        dy5, z, xre, xim, bre3, bim3, cre3, cim3, coef_r, p["s5_d"], nseq=nseq, seq=seq, rider=_swap_halves_rider(parts))
    pair = _pair_sums(EARLY_GRADS, parts, sib)
    (dzq, dzf, dzi, dzg, dlb, dgain), others = _hg_bwd(
        dyb, z, o, states, scores, p["lbrow"], p["gain"], nseq=nseq, seq=seq, heads=heads, qoff=qoff,
        rider=_scatter_rider(pair))
    halves = _chip_sums(EARLY_GRADS, pair, others)

    dz = jnp.concatenate([dza, dzq, dzf, dzi, dzg, dzga, dzgb], axis=1)
    dx, dg_mix = _mm_bwd_cols("in_bwd", dz, p["w_in"], epilogue=(norm_bwd, [x, p["g_mix"], dx1], [d]))

    gshape = lam_re.shape
    small = {
        "loss": lossv, "g_mix": dg_mix, "g_ffn": dg_ffn, "g_final": dg_final, "b_glu": db_glu, "gain": dgain,
        "lbrow": dlb, "s5_d": dd, "w_conv": cstats[0:CONV_W], "b_conv": cstats[CONV_W:CONV_W + 1],
        "lam_re": dlam[:, 0, :].reshape(gshape), "lam_im": dlam[:, 1, :].reshape(gshape),
        "bb_re": _s5_in_blocks_diag(dbre3), "bb_im": _s5_in_blocks_diag(dbim3),
        "s5_c_re": _s5_out_blocks_diag(dcre3), "s5_c_im": _s5_out_blocks_diag(dcim3),
    }
    small_vec = _pack([small[n] for n in SMALL_PARTS], F32)
    g_win, (small_all, *sibs) = _mm_wgrad_cols(
        "in_wgrad", u, dz, rider=_merge_riders(_gather_all_rider(small_vec), _swap_sums_rider(halves)))
    big = dict(zip(EARLY_GRADS, zip(halves, sibs)))
    small_sum = _sum_over_devices("small_grad_sum", small_vec, small_all)
    sm = dict(zip(SMALL_PARTS, _unpack(small_sum, [small[n].shape for n in SMALL_PARTS])))
    last = [_grad_parts(g_win)]
    pair = _pair_sums(("w_in",), last, _run_rider("grad_swap_halves", _swap_halves_rider(last)))
    (half,) = _chip_sums(("w_in",), pair, _run_rider("grad_scatter_chips", _scatter_rider(pair)))
    mid = half.shape[0] // 2
    sib_half = jnp.concatenate(_run_rider("grad_swap_sums", _swap_sums_rider([half[:mid], half[mid:]])), axis=0)
    big["w_in"] = (half, sib_half)
    return dx, big, sm


ANY = pl.BlockSpec(memory_space=pl.ANY)


def _place():
    x, y, c = lax.axis_index("x"), lax.axis_index("y"), lax.axis_index("c")
    chips = [(1 - x, y), (x, 1 - y), (1 - x, 1 - y)]
    return x, y, c, chips


def _remote(src, dst, send_sems, recv_sems, k, to):
    return pltpu.make_async_remote_copy(src_ref=src, dst_ref=dst, send_sem=send_sems.at[k], recv_sem=recv_sems.at[k],
                                        device_id=to, device_id_type=MESH)


def _half(rows, which):
    return pl.ds(pl.multiple_of(which * (rows // 2), SUBLANES), rows // 2)


class _SemView:
    def __init__(self, base, offset):
        self.base, self.offset = base, offset

    @property
    def at(self):
        return self

    def __getitem__(self, k):
        return self.base.at[self.offset + k]


def _merge_riders(first, second):
    na, no, ns = len(first.arrays), len(first.out_shapes), first.nsem

    def split(fn_a, fn_b):
        def run(ins, outs, send_sems, recv_sems):
            fn_a(ins[:na], outs[:no], send_sems, recv_sems)
            fn_b(ins[na:], outs[no:], _SemView(send_sems, ns), _SemView(recv_sems, ns))
        return run

    aliases = dict(first.aliases)
    aliases.update({na + i: no + o for i, o in second.aliases.items()})
    return _Rider(first.arrays + second.arrays, first.out_shapes + second.out_shapes, ns + second.nsem,
                  split(first.start, second.start), split(first.finish, second.finish), aliases)


def _prepare(x, gain, arrays, rider):
    t, d = x.shape
    n = len(arrays)
    tm = _row_tile(t, ROW_TILE)

    def body(x_ref, g_ref, *refs):
        (u,) = _rms_fwd(x_ref[...], g_ref[...])
        refs[n][...] = u.astype(BF16)

        @pl.when(pl.program_id(0) == 0)
        def _():
            for i in range(n):
                refs[n + 1 + i][...] = refs[i][...].astype(BF16)

    vm = pl.BlockSpec(memory_space=pltpu.VMEM)
    tok = pl.BlockSpec((tm, d), lambda i: (i, 0))
    outs, gathered = _hosted_call(
        "prepare", body, grid=(t // tm,), in_specs=[tok, pl.BlockSpec((1, d), lambda i: (0, 0))] + [vm] * n,
        out_specs=[tok] + [vm] * n,
        out_shape=[jax.ShapeDtypeStruct((t, d), BF16)] + [jax.ShapeDtypeStruct(a.shape, BF16) for a in arrays],
        operands=[x, gain] + list(arrays), rider=rider)
    return outs[0], outs[1:], gathered


def _symmetric_rider(arrays, out_shapes, copies_of, nsem):
    def start(ins, outs, send_sems, recv_sems):
        for cp in copies_of(ins, outs, send_sems, recv_sems):
            cp.start()

    def finish(ins, outs, send_sems, recv_sems):
        for cp in copies_of(ins, outs, send_sems, recv_sems):
            cp.wait()

    return _Rider(arrays, out_shapes, nsem, start, finish)


def _swap_halves_rider(parts):
    def copies_of(ins, outs, send_sems, recv_sems):
        x, y, c, _ = _place()
        return [_remote(ins[a].at[:, _half(g.shape[1], 1 - c), :], outs[a], send_sems, recv_sems, a, (x, y, 1 - c))
                for a, g in enumerate(parts)]

    shapes = [jax.ShapeDtypeStruct((g.shape[0], g.shape[1] // 2, g.shape[2]), g.dtype) for g in parts]
    return _symmetric_rider(parts, shapes, copies_of, len(parts))


def _scatter_rider(parts):
    def copies_of(ins, outs, send_sems, recv_sems):
        x, y, c, chips = _place()
        return [_remote(ins[a].at[2 * cx + cy], outs[a].at[j], send_sems, recv_sems, 3 * a + j, (cx, cy, c))
                for a in range(len(parts)) for j, (cx, cy) in enumerate(chips)]

    shapes = [jax.ShapeDtypeStruct((N_CHIPS - 1,) + h.shape[1:], h.dtype) for h in parts]
    return _symmetric_rider(parts, shapes, copies_of, 3 * len(parts))


def _swap_sums_rider(parts):
    def copies_of(ins, outs, send_sems, recv_sems):
        x, y, c, _ = _place()
        return [_remote(ins[a], outs[a], send_sems, recv_sems, a, (x, y, 1 - c)) for a in range(len(parts))]

    shapes = [jax.ShapeDtypeStruct(g.shape, g.dtype) for g in parts]
    return _symmetric_rider(parts, shapes, copies_of, len(parts))


def _gather_ici_rider(shards):
    def sends(ins, outs, send_sems, recv_sems):
        x, y, c, chips = _place()
        return [_remote(ins[a].at[_half(s.shape[0], c)], outs[a].at[2 * x + y, _half(s.shape[0], c)], send_sems,
                        recv_sems, 3 * a + j, (cx, cy, c)) for a, s in enumerate(shards) for j, (cx, cy) in enumerate(chips)]

    def start(ins, outs, send_sems, recv_sems):
        for cp in sends(ins, outs, send_sems, recv_sems):
            cp.start()

    def finish(ins, outs, send_sems, recv_sems):
        x, y, c, chips = _place()
        for a, s in enumerate(shards):
            for j, (cx, cy) in enumerate(chips):
                landed = outs[a].at[2 * cx + cy, _half(s.shape[0], c)]
                _remote(landed, landed, send_sems, recv_sems, 3 * a + j, (x, y, c)).wait_recv()
        for cp in sends(ins, outs, send_sems, recv_sems):
            cp.wait_send()

    shapes = [jax.ShapeDtypeStruct((N_CHIPS,) + s.shape, s.dtype) for s in shards]
    return _Rider(shards, shapes, 3 * len(shards), start, finish)


def _gather_full_rider(shards):
    n = len(shards)

    def sends(ins, outs, send_sems, recv_sems):
        x, y, c, chips = _place()
        return [_remote(ins[a].at[_half(s.shape[0], c)], outs[a].at[2 * x + y, _half(s.shape[0], c)], send_sems,
                        recv_sems, 6 * a + j, (cx, cy, c)) for a, s in enumerate(shards) for j, (cx, cy) in enumerate(chips)]

    def start(ins, outs, send_sems, recv_sems):
        for cp in sends(ins, outs, send_sems, recv_sems):
            cp.start()

    def finish(ins, outs, send_sems, recv_sems):
        x, y, c, chips = _place()
        passed = []
        for a, s in enumerate(shards):
            for j, (cx, cy) in enumerate(chips):
                landed = outs[a].at[2 * cx + cy, _half(s.shape[0], c)]
                _remote(landed, landed, send_sems, recv_sems, 6 * a + j, (x, y, c)).wait_recv()
                passed.append(_remote(landed, landed, send_sems, recv_sems, 6 * a + 3 + j, (x, y, 1 - c)))
                passed[-1].start()
        for a, s in enumerate(shards):
            for j, (cx, cy) in enumerate(chips):
                other = outs[a].at[2 * cx + cy, _half(s.shape[0], 1 - c)]
                _remote(other, other, send_sems, recv_sems, 6 * a + 3 + j, (x, y, c)).wait_recv()
        for cp in sends(ins, outs, send_sems, recv_sems) + passed:
            cp.wait_send()

    shapes = [jax.ShapeDtypeStruct((N_CHIPS,) + s.shape, s.dtype) for s in shards]
    return _Rider(shards, shapes, 6 * n, start, finish)


def _gather_pass_rider(landed):
    def sends(ins, outs, send_sems, recv_sems):
        x, y, c, chips = _place()
        return [_remote(ins[a].at[2 * cx + cy, _half(g.shape[1], c)], outs[a].at[2 * cx + cy, _half(g.shape[1], c)],
                        send_sems, recv_sems, 3 * a + j, (x, y, 1 - c))
                for a, g in enumerate(landed) for j, (cx, cy) in enumerate(chips)]

    def start(ins, outs, send_sems, recv_sems):
        for cp in sends(ins, outs, send_sems, recv_sems):
            cp.start()

    def finish(ins, outs, send_sems, recv_sems):
        x, y, c, chips = _place()
        for a, g in enumerate(landed):
            for j, (cx, cy) in enumerate(chips):
                other = outs[a].at[2 * cx + cy, _half(g.shape[1], 1 - c)]
                _remote(other, other, send_sems, recv_sems, 3 * a + j, (x, y, c)).wait_recv()
        for cp in sends(ins, outs, send_sems, recv_sems):
            cp.wait_send()

    shapes = [jax.ShapeDtypeStruct(g.shape, g.dtype) for g in landed]
    return _Rider(landed, shapes, 3 * len(landed), start, finish, aliases={a: a for a in range(len(landed))})


def _grad_parts(g):
    return g.reshape((N_CHIPS, -1, g.shape[-1]))


def _place_scalars():
    return jnp.stack([lax.axis_index("c"), 2 * lax.axis_index("x") + lax.axis_index("y")]).astype(jnp.int32)


def _scalar_call(body, name, grid, in_specs, out_specs, out_shape, operands):
    spec = pltpu.PrefetchScalarGridSpec(num_scalar_prefetch=1, grid=grid, in_specs=in_specs, out_specs=out_specs)
    return pl.pallas_call(body, name=name, grid_spec=spec, out_shape=out_shape,
                          compiler_params=_params(*(["arbitrary"] * len(grid))))(_place_scalars(), *operands)


def _pair_sums(names, parts, sib):
    out = []
    for n, g, s in zip(names, parts, sib):
        rh, cols = s.shape[1], s.shape[2]
        tm = _row_tile(rh, ROW_TILE)
        nblk = rh // tm

        def body(place, g_ref, s_ref, o_ref):
            o_ref[...] = (g_ref[...].astype(F32) + s_ref[...].astype(F32)).astype(o_ref.dtype)

        blk = pl.BlockSpec((None, tm, cols), lambda j, i, place: (j, i, 0))
        own = pl.BlockSpec((None, tm, cols), lambda j, i, place, nblk=nblk: (j, place[0] * nblk + i, 0))
        out.append(_scalar_call(body, "grad_pair_sum_" + n, (N_CHIPS, nblk), [own, blk], blk,
                                jax.ShapeDtypeStruct(s.shape, BF16), (g, s)))
    return out


def _chip_sums(names, pair, others):
    out = []
    for n, h, o in zip(names, pair, others):
        rh, cols = h.shape[1], h.shape[2]
        tm = _row_tile(rh, ROW_TILE)

        def body(place, h_ref, a_ref, b_ref, c_ref, o_ref):
            o_ref[...] = (h_ref[...].astype(F32) + a_ref[...].astype(F32)) + b_ref[...].astype(F32) + c_ref[...].astype(F32)

        mine = pl.BlockSpec((None, tm, cols), lambda i, place: (place[1], i, 0))
        other = lambda k: pl.BlockSpec((None, tm, cols), lambda i, place, k=k: (k, i, 0))
        out.append(_scalar_call(body, "grad_chip_sum_" + n, (rh // tm,), [mine, other(0), other(1), other(2)],
                                pl.BlockSpec((tm, cols), lambda i, place: (i, 0)), jax.ShapeDtypeStruct((rh, cols), F32),
                                (h, o, o, o)))
    return out


def _adamw_halves(name, w, m, v, own, sib):
    rh, cols = own.shape
    tm = _row_tile(rh, ADAMW_TILE)
    nblk = rh // tm

    def body(place, w_ref, m_ref, v_ref, own_ref, sib_ref, g_ref, d_ref, m2_ref, v2_ref):
        mine = pl.program_id(0) // nblk == place[0]

        def run(gv):
            g_ref[...] = gv
            d_ref[...], m2_ref[...], v2_ref[...] = _adamw_math(w_ref[...], gv, m_ref[...], v_ref[...])

        @pl.when(mine)
        def _():
            run(own_ref[...])

        @pl.when(jnp.logical_not(mine))
        def _():
            run(sib_ref[...])

    full = pl.BlockSpec((tm, cols), lambda i, place: (i, 0))
    own_spec = pl.BlockSpec((tm, cols), lambda i, place: (jnp.where(i // nblk == place[0], i % nblk, 0), 0))
    sib_spec = pl.BlockSpec((tm, cols), lambda i, place: (jnp.where(i // nblk == place[0], 0, i % nblk), 0))
    return _scalar_call(body, name, (2 * nblk,), [full, full, full, own_spec, sib_spec], [full] * 4,
                        [jax.ShapeDtypeStruct((2 * rh, cols), F32)] * 4, (w, m, v, own, sib))


def _gather_all_rider(v):
    m_per = v.shape[0]

    def rows(ref, px, py, pc):
        return ref.at[pl.ds(pl.multiple_of((4 * px + 2 * py + pc) * m_per, 8), m_per)]

    def first(ins, outs, send_sems, recv_sems):
        x, y, c, chips = _place()
        mine = rows(outs[0], x, y, c)
        return [_remote(ins[0], mine, send_sems, recv_sems, 0, (x, y, 1 - c))] + [
            _remote(ins[0], mine, send_sems, recv_sems, 1 + j, (cx, cy, c)) for j, (cx, cy) in enumerate(chips)]

    def start(ins, outs, send_sems, recv_sems):
        for cp in first(ins, outs, send_sems, recv_sems):
            cp.start()

    def finish(ins, outs, send_sems, recv_sems):
        x, y, c, chips = _place()
        passed = []
        for j, (cx, cy) in enumerate(chips):
            blk = rows(outs[0], cx, cy, c)
            _remote(blk, blk, send_sems, recv_sems, 1 + j, (x, y, c)).wait_recv()
            passed.append(_remote(blk, blk, send_sems, recv_sems, 4 + j, (x, y, 1 - c)))
            passed[j].start()
        sib = rows(outs[0], x, y, 1 - c)
        _remote(sib, sib, send_sems, recv_sems, 0, (x, y, c)).wait_recv()
        for j, (cx, cy) in enumerate(chips):
            blk = rows(outs[0], cx, cy, 1 - c)
            _remote(blk, blk, send_sems, recv_sems, 4 + j, (x, y, c)).wait_recv()
        for cp in first(ins, outs, send_sems, recv_sems) + passed:
            cp.wait_send()

    return _Rider([v], [jax.ShapeDtypeStruct((N_DEV * m_per,) + v.shape[1:], v.dtype)], 7, start, finish)


def _sum_over_devices(name, v, gathered):
    m_per = v.shape[0]
    dev = 4 * lax.axis_index("x") + 2 * lax.axis_index("y") + lax.axis_index("c")
    full = lax.dynamic_update_slice_in_dim(gathered, v, dev * m_per, axis=0)
    return _sum_blocks(name, [full[i * m_per:(i + 1) * m_per] for i in range(N_DEV)], F32)


def _sum_blocks(name, parts, out_dtype):
    rows, cols = parts[0].shape
    tm = _row_tile(rows, ROW_TILE)

    def body(*refs):
        acc = refs[0][...].astype(F32)
        for r in refs[1:-1]:
            acc = acc + r[...].astype(F32)
        refs[-1][...] = acc.astype(refs[-1].dtype)

    spec = pl.BlockSpec((tm, cols), lambda i: (i, 0))
    return pl.pallas_call(
        body, name=name, grid=(rows // tm,), in_specs=[spec] * len(parts), out_specs=spec,
        out_shape=jax.ShapeDtypeStruct((rows, cols), out_dtype), compiler_params=_params("arbitrary"),
    )(*parts)


def _adamw_math(wv, gv, mv, vv):
    m2 = ADAM_B1 * mv + (1.0 - ADAM_B1) * gv
    v2 = ADAM_B2 * vv + (1.0 - ADAM_B2) * (gv * gv)
    delta = -ADAM_LR * ((m2 / (1.0 - ADAM_B1 ** ADAM_STEP)) / (jnp.sqrt(v2 / (1.0 - ADAM_B2 ** ADAM_STEP)) + ADAM_EPS)
                        + ADAM_WD * wv)
    return delta, m2, v2


def _adamw_small(ws, gs, ms, vs):
    n = len(ws)

    def body(*refs):
        for i in range(n):
            res = _adamw_math(refs[i][...], refs[n + i][...], refs[2 * n + i][...], refs[3 * n + i][...])
            for k in range(3):
                refs[(4 + k) * n + i][...] = res[k]

    vm = pl.BlockSpec(memory_space=pltpu.VMEM)
    outs = pl.pallas_call(
        body, name="adamw_small", in_specs=[vm] * (4 * n), out_specs=[vm] * (3 * n),
        out_shape=[jax.ShapeDtypeStruct(a.shape, F32) for a in ws] * 3,
        compiler_params=pltpu.CompilerParams(vmem_limit_bytes=VMEM_LIMIT_BYTES),
    )(*ws, *gs, *ms, *vs)
    return outs[:n], outs[n:2 * n], outs[2 * n:]


PACK_ROWS = 256


def _pack(flat_parts, dtype, lead=()):
    parts = [a.astype(dtype).reshape(lead + (-1,)) for a in flat_parts]
    n = sum(a.shape[-1] for a in parts)
    chunk = PACK_ROWS * LANES
    total = -(-n // chunk) * chunk
    if total > n:
        parts.append(jnp.zeros(lead + (total - n,), dtype))
    return jnp.concatenate(parts, axis=-1).reshape(lead + (total // LANES, LANES))


def _unpack(buf, shapes, lead=()):
    flat = buf.reshape(lead + (-1,))
    out, off = [], 0
    for shp in shapes:
        n = math.prod(shp)
        out.append(lax.slice_in_dim(flat, off, off + n, axis=len(lead)).reshape(lead + tuple(shp)))
        off += n
    return out


BIG = ("w_in", "w_glu", "w_pa", "w_pb", "w_out", "w_up", "w_down")
WEIGHTS = ("g_mix", "w_in", "s5_a_re", "s5_a_im", "s5_log_dt", "s5_b_re", "s5_b_im", "s5_c_re", "s5_c_im", "s5_d",
           "w_glu", "b_glu", "hg_lb_logits", "hg_norm_gain", "w_pa", "w_pb", "w_out", "g_ffn", "w_up", "w_conv",
           "b_conv", "w_down", "g_final")
SMALL = tuple(n for n in WEIGHTS if n not in BIG)
SMALL_PARTS = ("loss", "g_mix", "g_ffn", "g_final", "b_glu", "gain", "lbrow", "s5_d", "w_conv", "b_conv", "lam_re",
               "lam_im", "bb_re", "bb_im", "s5_c_re", "s5_c_im")


def _lower_bound(logits):
    return jnp.cumsum(jax.nn.softmax(logits, axis=0), axis=0)[0:1]


def kernel(x, g_mix, w_in, s5_a_re, s5_a_im, s5_log_dt, s5_b_re, s5_b_im, s5_c_re, s5_c_im, s5_d, w_glu, b_glu, hg_lb_logits, hg_norm_gain, w_pa, w_pb, w_out, g_ffn, w_up, w_conv, b_conv, w_down, g_final, loss_target, m_g_mix, m_w_in, m_s5_a_re, m_s5_a_im, m_s5_log_dt, m_s5_b_re, m_s5_b_im, m_s5_c_re, m_s5_c_im, m_s5_d, m_w_glu, m_b_glu, m_hg_lb_logits, m_hg_norm_gain, m_w_pa, m_w_pb, m_w_out, m_g_ffn, m_w_up, m_w_conv, m_b_conv, m_w_down, m_g_final, v_g_mix, v_w_in, v_s5_a_re, v_s5_a_im, v_s5_log_dt, v_s5_b_re, v_s5_b_im, v_s5_c_re, v_s5_c_im, v_s5_d, v_w_glu, v_b_glu, v_hg_lb_logits, v_hg_norm_gain, v_w_pa, v_w_pb, v_w_out, v_g_ffn, v_w_up, v_w_conv, v_b_conv, v_w_down, v_g_final):
    args = dict(locals())
    w = {n: args[n] for n in WEIGHTS}
    mom = {n: args["m_" + n] for n in WEIGHTS}
    var = {n: args["v_" + n] for n in WEIGHTS}
    nseq, seq, d = x.shape
    xi, yi = lax.axis_index("x"), lax.axis_index("y")
    chip = 2 * xi + yi

    shard = {n: w[n][0] for n in BIG}
    first = [shard["w_in"].astype(BF16), shard["w_glu"].astype(BF16),
             jnp.pad(w_conv[0], ((0, 2 * SUBLANES - CONV_W), (0, 0)))]
    x2 = x.reshape(nseq * seq, d)
    u, late16, got = _prepare(x2, g_mix, [shard[n] for n in LATE], _gather_full_rider(first))
    w_in_all, w_glu_all, conv_all = [lax.dynamic_update_index_in_dim(g, s, chip, 0) for g, s in zip(got, first)]
    p = dict(g_mix=g_mix, g_ffn=g_ffn, g_final=g_final.reshape(1, -1), b_glu=b_glu, gain=hg_norm_gain, s5_d=s5_d,
             b_conv=b_conv, lbrow=_lower_bound(hg_lb_logits),
             s5_a_re=s5_a_re[0], s5_a_im=s5_a_im[0], s5_log_dt=s5_log_dt[0], s5_b_re=s5_b_re[0], s5_b_im=s5_b_im[0],
             s5_c_re=s5_c_re[0], s5_c_im=s5_c_im[0], w_in=w_in_all, w_glu=w_glu_all.reshape(-1, w_glu_all.shape[-1]),
             w_conv=conv_all[:, :CONV_W].transpose(1, 0, 2).reshape(CONV_W, -1))

    dx, halves, sm = _local_step(x2, loss_target.reshape(nseq * seq, d), u, p, dict(zip(LATE, late16)), nseq=nseq, seq=seq)
    loss = sm["loss"][0, 0]

    grads, delta, new_m, new_v = {}, {}, {}, {}
    for n in BIG:
        shp = shard[n].shape
        grads[n], delta[n], new_m[n], new_v[n] = _adamw_halves("adamw_" + n, shard[n], mom[n].reshape(shp),
                                                               var[n].reshape(shp), *halves[n])

    _, disc_vjp = jax.vjp(_s5_discretize, p["s5_a_re"], p["s5_a_im"], p["s5_log_dt"], p["s5_b_re"], p["s5_b_im"])
    da_re, da_im, dlog_dt, db_re, db_im = disc_vjp((sm["lam_re"], sm["lam_im"], sm["bb_re"], sm["bb_im"]))
    _, lb_vjp = jax.vjp(_lower_bound, hg_lb_logits)
    (dlogits,) = lb_vjp(sm["lbrow"])
    fcols = w_conv.shape[-1]
    grads.update(
        g_mix=sm["g_mix"], g_ffn=sm["g_ffn"], g_final=sm["g_final"].reshape(-1), b_glu=sm["b_glu"],
        hg_norm_gain=sm["gain"], hg_lb_logits=dlogits, s5_d=sm["s5_d"], b_conv=sm["b_conv"],
        w_conv=lax.dynamic_slice_in_dim(sm["w_conv"], chip * fcols, fcols, axis=1),
        s5_a_re=da_re, s5_a_im=da_im, s5_log_dt=dlog_dt, s5_b_re=db_re, s5_b_im=db_im,
        s5_c_re=sm["s5_c_re"], s5_c_im=sm["s5_c_im"])
    grads = {n: grads[n].reshape(w[n].shape) for n in WEIGHTS}

    def natural(a):
        return a.reshape(1, -1) if a.ndim == 1 else (a[0] if a.ndim > 2 else a)

    outs = _adamw_small(*[[natural(src[n]) for n in SMALL] for src in (w, grads, mom, var)])
    for dst, group in zip((delta, new_m, new_v), outs):
        dst.update(zip(SMALL, group))
    res = [loss, dx.reshape(x.shape)]
    for group in (grads, delta, new_m, new_v):
        res += [group[n].reshape(w[n].shape) for n in WEIGHTS]
    return tuple(res)
```

```python
import functools
import math

import jax
import jax.numpy as jnp
from jax import lax
from jax.experimental import pallas as pl
from jax.experimental.pallas import tpu as pltpu

F32 = jnp.float32
BF16 = jnp.bfloat16
MESH = pl.DeviceIdType.MESH

EPS = 1e-6
S5_GROUP = 16
S5_STATE = 64
S5_BLOCK_GROUPS = 8
HEAD = 128
CHUNK = 64
CONV_W = 3
LANES = 128
SUBLANES = 8
GATE_BLOCK = 512
VMEM_LIMIT_BYTES = 56 * 1024 * 1024

ADAM_LR = 0.001
ADAM_B1 = 0.9
ADAM_B2 = 0.999
ADAM_EPS = 1e-08
ADAM_WD = 0.01
ADAM_STEP = 10

N_CHIPS = 4
N_DEV = 8


def _params(*sem):
    return pltpu.CompilerParams(dimension_semantics=sem, vmem_limit_bytes=VMEM_LIMIT_BYTES)


class _Rider:
    def __init__(self, arrays, out_shapes, nsem, start, finish, aliases=None):
        self.arrays, self.out_shapes, self.nsem = list(arrays), list(out_shapes), nsem
        self.start, self.finish, self.aliases = start, finish, dict(aliases or {})


def _hosted_call(name, body, *, grid, in_specs, out_specs, out_shape, operands, scratch_shapes=(), rider=None):
    in_specs, out_specs, out_shape, scratch_shapes = list(in_specs), list(out_specs), list(out_shape), list(scratch_shapes)
    cparams = _params(*(["arbitrary"] * len(grid)))
    if rider is None:
        return pl.pallas_call(body, name=name, grid=grid, in_specs=in_specs, out_specs=out_specs, out_shape=out_shape,
                              scratch_shapes=scratch_shapes, compiler_params=cparams)(*operands)
    n_in, n_out, n_sc = len(in_specs), len(out_specs), len(scratch_shapes)
    r_in, r_out = len(rider.arrays), len(rider.out_shapes)

    def hosted(*refs):
        ins, rins = refs[:n_in], refs[n_in:n_in + r_in]
        outs = refs[n_in + r_in:n_in + r_in + n_out]
        routs = refs[n_in + r_in + n_out:n_in + r_in + n_out + r_out]
        rest = refs[n_in + r_in + n_out + r_out:]
        send_sems, recv_sems = rest[n_sc], rest[n_sc + 1]
        first = functools.reduce(jnp.logical_and, [pl.program_id(i) == 0 for i in range(len(grid))])
        last = functools.reduce(jnp.logical_and, [pl.program_id(i) == grid[i] - 1 for i in range(len(grid))])

        @pl.when(first)
        def _():
            rider.start(rins, routs, send_sems, recv_sems)

        body(*ins, *outs, *rest[:n_sc])

        @pl.when(last)
        def _():
            rider.finish(rins, routs, send_sems, recv_sems)

    res = pl.pallas_call(
        hosted, name=name, grid=grid, in_specs=in_specs + [ANY] * r_in, out_specs=out_specs + [ANY] * r_out,
        out_shape=out_shape + rider.out_shapes,
        scratch_shapes=scratch_shapes + [pltpu.SemaphoreType.DMA((rider.nsem,)), pltpu.SemaphoreType.DMA((rider.nsem,))],
        input_output_aliases={n_in + i: n_out + o for i, o in rider.aliases.items()}, compiler_params=cparams,
    )(*operands, *rider.arrays)
    return res[:n_out], res[n_out:]


def _run_rider(name, rider):
    r_in, r_out = len(rider.arrays), len(rider.out_shapes)

    def body(*refs):
        rins, routs, send_sems, recv_sems = refs[:r_in], refs[r_in:r_in + r_out], refs[-2], refs[-1]
        rider.start(rins, routs, send_sems, recv_sems)
        rider.finish(rins, routs, send_sems, recv_sems)

    return pl.pallas_call(
        body, name=name, in_specs=[ANY] * r_in, out_specs=[ANY] * r_out, out_shape=rider.out_shapes,
        scratch_shapes=[pltpu.SemaphoreType.DMA((rider.nsem,)), pltpu.SemaphoreType.DMA((rider.nsem,))],
        input_output_aliases=rider.aliases,
    )(*rider.arrays)


def _row_tile(rows, cap):
    if rows <= cap:
        return rows
    for t in range(cap - cap % 8, 7, -8):
        if rows % t == 0:
            return t
    raise ValueError(f"no row tile for {rows}")


def _dot(a, b):
    return jnp.dot(a.astype(BF16), b.astype(BF16), preferred_element_type=F32)


def _dot_nt(a, b):
    return lax.dot_general(a.astype(BF16), b.astype(BF16), (((1,), (1,)), ((), ())), preferred_element_type=F32)


def _dot_tn(a, b):
    return lax.dot_general(a.astype(BF16), b.astype(BF16), (((0,), (0,)), ((), ())), preferred_element_type=F32)


def _sigmoid(x):
    return 0.5 * jnp.tanh(0.5 * x) + 0.5


_GELU_C = math.sqrt(2.0 / math.pi)


def _gelu(x):
    return 0.5 * x * (1.0 + jnp.tanh(_GELU_C * (x + 0.044715 * x * x * x)))


def _gelu_grad(x):
    th = jnp.tanh(_GELU_C * (x + 0.044715 * x * x * x))
    return 0.5 * (1.0 + th) + 0.5 * x * (1.0 - th * th) * _GELU_C * (1.0 + 3.0 * 0.044715 * x * x)


def _rowwise(name, fn, ins, outs, accs=(), *, rows, tm, ncol=1, rider=None):
    n_in, n_out = len(ins), len(outs)

    def body(*refs):
        res = fn(*[r[...] for r in refs[:n_in]])
        for r, v in zip(refs[n_in:n_in + n_out], res[:n_out]):
            r[...] = v.astype(r.dtype)
        first = pl.program_id(1) == 0
        for r, v in zip(refs[n_in + n_out:], res[n_out:]):
            @pl.when(first)
            def _():
                r[...] = v

            @pl.when(jnp.logical_not(first))
            def _():
                r[...] += v

    in_specs = []
    for _, width, base, kind in ins:
        if kind == "row":
            in_specs.append(pl.BlockSpec((tm, width), lambda j, i, b=base: (i, b + j)))
        else:
            in_specs.append(pl.BlockSpec((1, width), lambda j, i, b=base: (0, b + j)))
    out_specs = [pl.BlockSpec((tm, width), lambda j, i: (i, j)) for _, width, _ in outs]
    out_specs += [pl.BlockSpec((1, width), lambda j, i: (0, j)) for _, width in accs]
    out_shape = [jax.ShapeDtypeStruct((rows, total), dt) for total, _, dt in outs]
    out_shape += [jax.ShapeDtypeStruct((1, total), F32) for total, _ in accs]
    return _hosted_call(name, body, grid=(ncol, rows // tm), in_specs=in_specs, out_specs=out_specs, out_shape=out_shape,
                        operands=[a for a, _, _, _ in ins], rider=rider)


def _mm(name, a, b, *, mode, grid, a_spec, b_spec, o_spec, out_shape, acc_shape, res=None, res_spec=None,
        pair_axis=None, rider=None, epilogue=None):
    nk = grid[2]
    dot = {"nn": _dot, "nt": _dot_nt, "tn": _dot_tn}[mode]
    a_list = list(a) if isinstance(a, tuple) else [a]
    b_list = list(b) if isinstance(b, tuple) else [b]
    na, nb = len(a_list), len(b_list)
    assert (pair_axis is None) == (na + nb == 2)
    direct = nk == 1 and pair_axis is None
    epi_fn, epi_ins, epi_sums = epilogue if epilogue is not None else (None, [], [])
    n_res = 0 if res is None else 1
    n_epi = len(epi_ins)

    def body(*refs):
        a_refs, b_refs = refs[:na], refs[na:na + nb]
        r_ref = None if res is None else refs[na + nb]
        e_refs = refs[na + nb + n_res:na + nb + n_res + n_epi]
        o_ref = refs[na + nb + n_res + n_epi]
        s_refs = refs[na + nb + n_res + n_epi + 1:na + nb + n_res + n_epi + 1 + len(epi_sums)]
        first_rows = pl.program_id(0) == 0

        def finish(v):
            if res is not None:
                v = v + r_ref[...]
            if epi_fn is None:
                o_ref[...] = v.astype(o_ref.dtype)
                return
            outs = epi_fn(v, *[r[...] for r in e_refs])
            o_ref[...] = outs[0].astype(o_ref.dtype)
            for s_ref, part in zip(s_refs, outs[1:]):
                @pl.when(first_rows)
                def _():
                    s_ref[...] = part

                @pl.when(jnp.logical_not(first_rows))
                def _():
                    s_ref[...] += part

        if direct:
            finish(dot(a_refs[0][...], b_refs[0][...]))
            return
        acc_ref = refs[-1]
        k = pl.program_id(2)

        @pl.when(k == 0)
        def _():
            acc_ref[...] = jnp.zeros_like(acc_ref)

        if pair_axis is None:
            acc_ref[...] += dot(a_refs[0][...], b_refs[0][...])
        else:
            lower = pl.program_id(pair_axis) < grid[pair_axis] // 2

            @pl.when(lower)
            def _():
                acc_ref[...] += dot(a_refs[0][...], b_refs[0][...])

            @pl.when(jnp.logical_not(lower))
            def _():
                acc_ref[...] += dot(a_refs[-1][...], b_refs[-1][...])

        @pl.when(k == nk - 1)
        def _():
            finish(acc_ref[...])

    operands = a_list + b_list + ([] if res is None else [res]) + [arr for arr, _ in epi_ins]
    in_specs = (list(a_spec) if na == 2 else [a_spec]) + (list(b_spec) if nb == 2 else [b_spec])
    in_specs += ([] if res is None else [res_spec]) + [spec for _, spec in epi_ins]
    out_specs = [o_spec] + [pl.BlockSpec((1, c), lambda *_: (0, 0)) for c in epi_sums]
    out_shapes = [out_shape] + [jax.ShapeDtypeStruct((1, c), F32) for c in epi_sums]
    got = _hosted_call(name, body, grid=grid, in_specs=in_specs, out_specs=out_specs, out_shape=out_shapes,
                       scratch_shapes=[] if direct else [pltpu.VMEM(acc_shape, F32)], operands=operands, rider=rider)
    mine, rider_outs = (got, None) if rider is None else got
    mine = mine[0] if epilogue is None else tuple(mine)
    return mine if rider is None else (mine, rider_outs)


MM_TILE_BUDGET_BYTES = 36 * 1024 * 1024
MM_TILE_CAP = 2048
ROW_TILE = 1024
GLU_TILE = 1024
ADAMW_TILE = 256


def _mm_tile(t, row_bytes, fixed_bytes):
    cap = max(16, min(MM_TILE_CAP, (MM_TILE_BUDGET_BYTES - fixed_bytes) // row_bytes))
    return _row_tile(t, cap - cap % 16)


def _size(a):
    return jnp.dtype(a.dtype).itemsize


def _mm_fwd_cols(name, a, w3, out_dtype=F32, rider=None):
    t, k = a.shape
    ns = w3.shape[2]
    tm = _mm_tile(t, 2 * k * _size(a) + 2 * ns * jnp.dtype(out_dtype).itemsize, 2 * k * ns * _size(w3))
    return _mm(name, a, w3, mode="nn", grid=(N_CHIPS, t // tm, 1),
               a_spec=pl.BlockSpec((tm, k), lambda j, i, kk: (i, 0)),
               b_spec=pl.BlockSpec((None, k, ns), lambda j, i, kk: (j, 0, 0)),
               o_spec=pl.BlockSpec((tm, ns), lambda j, i, kk: (i, j)),
               out_shape=jax.ShapeDtypeStruct((t, N_CHIPS * ns), out_dtype), acc_shape=(tm, ns), rider=rider)


def _mm_bwd_cols(name, d, w3, out_dtype=F32, rider=None, epilogue=None):
    pair = isinstance(d, tuple)
    t = d[0].shape[0] if pair else d.shape[0]
    k, ns = w3.shape[1], w3.shape[2]
    dsize = _size(d[0] if pair else d)
    tm = _mm_tile(t, (4 if pair else 2) * ns * dsize + 2 * k * jnp.dtype(out_dtype).itemsize + 4 * k
                  + _row_epilogue(epilogue, 8)[1], 2 * k * ns * _size(w3))
    half = N_CHIPS // 2
    if pair:
        a_spec = (pl.BlockSpec((tm, ns), lambda i, j, kk: (i, jnp.minimum(kk, half - 1))),
                  pl.BlockSpec((tm, ns), lambda i, j, kk: (i, jnp.maximum(kk - half, 0))))
    else:
        a_spec = pl.BlockSpec((tm, ns), lambda i, j, kk: (i, kk))
    return _mm(name, d, w3, mode="nt", grid=(t // tm, 1, N_CHIPS), a_spec=a_spec,
               b_spec=pl.BlockSpec((None, k, ns), lambda i, j, kk: (kk, 0, 0)),
               o_spec=pl.BlockSpec((tm, k), lambda i, j, kk: (i, 0)),
               out_shape=jax.ShapeDtypeStruct((t, k), out_dtype), acc_shape=(tm, k), pair_axis=2 if pair else None,
               rider=rider, epilogue=_row_epilogue(epilogue, tm)[0])


def _mm_wgrad_cols(name, a, d, rider=None):
    pair = isinstance(d, tuple)
    t, k = a.shape
    ns = (2 * d[0].shape[1] if pair else d.shape[1]) // N_CHIPS
    dsize = _size(d[0] if pair else d)
    tk = _mm_tile(t, 2 * k * _size(a) + (4 if pair else 2) * ns * dsize, k * ns * (4 + 2 * 2))
    half = N_CHIPS // 2
    if pair:
        b_spec = (pl.BlockSpec((tk, ns), lambda j, i, kk: (jnp.where(j < half, kk, 0), jnp.minimum(j, half - 1))),
                  pl.BlockSpec((tk, ns), lambda j, i, kk: (jnp.where(j < half, 0, kk), jnp.maximum(j - half, 0))))
    else:
        b_spec = pl.BlockSpec((tk, ns), lambda j, i, kk: (kk, j))
    return _mm(name, a, d, mode="tn", grid=(N_CHIPS, 1, t // tk),
               a_spec=pl.BlockSpec((tk, k), lambda j, i, kk: (kk, 0)), b_spec=b_spec,
               o_spec=pl.BlockSpec((None, k, ns), lambda j, i, kk: (j, 0, 0)),
               out_shape=jax.ShapeDtypeStruct((N_CHIPS, k, ns), BF16), acc_shape=(k, ns),
               pair_axis=0 if pair else None, rider=rider)


MM_BLOCK_CAP = 1408


def _row_epilogue(epilogue, tm):
    if epilogue is None:
        return None, 0
    fn, arrays, sums = epilogue
    specs = [pl.BlockSpec((1, x.shape[1]), lambda i, j, kk: (0, 0)) if x.shape[0] == 1 else
             pl.BlockSpec((tm, x.shape[1]), lambda i, j, kk: (i, 0)) for x in arrays]
    return (fn, list(zip(arrays, specs)), list(sums)), sum(2 * x.shape[1] * _size(x) for x in arrays if x.shape[0] > 1)


def _mm_fwd_rows(name, a, w, res=None, out_dtype=F32, epilogue=None):
    t, k = a.shape
    n = w.shape[1]
    tk = k if k <= MM_BLOCK_CAP else MM_BLOCK_CAP
    assert k % tk == 0
    row_bytes = 2 * tk * _size(a) + 2 * n * jnp.dtype(out_dtype).itemsize + (0 if res is None else 2 * n * 4) + 4 * n
    row_bytes += _row_epilogue(epilogue, 8)[1]
    tm = _mm_tile(t, row_bytes, 2 * tk * n * _size(w))
    return _mm(name, a, w, mode="nn", grid=(t // tm, 1, k // tk),
               a_spec=pl.BlockSpec((tm, tk), lambda i, j, kk: (i, kk)),
               b_spec=pl.BlockSpec((tk, n), lambda i, j, kk: (kk, 0)),
               o_spec=pl.BlockSpec((tm, n), lambda i, j, kk: (i, 0)),
               out_shape=jax.ShapeDtypeStruct((t, n), out_dtype), acc_shape=(tm, n),
               res=res, res_spec=None if res is None else pl.BlockSpec((tm, n), lambda i, j, kk: (i, 0)),
               epilogue=_row_epilogue(epilogue, tm)[0])


def _mm_bwd_rows(name, d, w, out_dtype=F32):
    t, n = d.shape
    k = w.shape[0]
    tn = k if k <= MM_BLOCK_CAP else MM_BLOCK_CAP
    assert k % tn == 0
    tm = _mm_tile(t, 2 * n * _size(d) + 2 * tn * jnp.dtype(out_dtype).itemsize, 2 * tn * n * _size(w))
    return _mm(name, d, w, mode="nt", grid=(t // tm, k // tn, 1),
               a_spec=pl.BlockSpec((tm, n), lambda i, j, kk: (i, 0)),
               b_spec=pl.BlockSpec((tn, n), lambda i, j, kk: (j, 0)),
               o_spec=pl.BlockSpec((tm, tn), lambda i, j, kk: (i, j)),
               out_shape=jax.ShapeDtypeStruct((t, k), out_dtype), acc_shape=(tm, tn))


def _mm_wgrad_rows(name, a, d):
    t, k = a.shape
    n = d.shape[1]
    nblk = next(b for b in (1, 2, 4) if (k // b) % LANES == 0 and k // b <= MM_BLOCK_CAP)
    ks = k // nblk
    tk = _mm_tile(t, 2 * ks * _size(a) + 2 * n * _size(d), ks * n * (4 + 2 * 2))
    return _mm(name, a, d, mode="tn", grid=(nblk, 1, t // tk),
               a_spec=pl.BlockSpec((tk, ks), lambda j, i, kk: (kk, j)),
               b_spec=pl.BlockSpec((tk, n), lambda j, i, kk: (kk, 0)),
               o_spec=pl.BlockSpec((ks, n), lambda j, i, kk: (j, 0)),
               out_shape=jax.ShapeDtypeStruct((k, n), BF16), acc_shape=(ks, n))


def _s5_discretize(a_re, a_im, log_dt, b_re, b_im):
    dt = jnp.exp(log_dt)[:, None]
    mag = jnp.exp(a_re * dt)
    ang = a_im * dt
    lb_re = mag * jnp.cos(ang)
    lb_im = mag * jnp.sin(ang)
    den = a_re * a_re + a_im * a_im
    n_re = lb_re - 1.0
    n_im = lb_im
    co_re = ((n_re * a_re + n_im * a_im) / den)[..., None]
    co_im = ((n_im * a_re - n_re * a_im) / den)[..., None]
    bb_re = co_re * b_re - co_im * b_im
    bb_im = co_re * b_im + co_im * b_re
    return lb_re, lb_im, bb_re, bb_im


def _s5_in_blocks(bb):
    g = bb.shape[0]
    nb = g // S5_BLOCK_GROUPS
    t = bb.reshape(nb, S5_BLOCK_GROUPS, S5_STATE, S5_GROUP).transpose(0, 1, 3, 2)
    eye = jnp.eye(S5_BLOCK_GROUPS, dtype=bb.dtype)
    full = t[:, :, :, None, :] * eye[None, :, None, :, None]
    return full.reshape(nb, S5_BLOCK_GROUPS * S5_GROUP, S5_BLOCK_GROUPS * S5_STATE)


def _s5_in_blocks_diag(blocks):
    nb = blocks.shape[0]
    t = blocks.reshape(nb, S5_BLOCK_GROUPS, S5_GROUP, S5_BLOCK_GROUPS, S5_STATE)
    d = jnp.einsum("bghgp->bghp", t)
    return d.transpose(0, 1, 3, 2).reshape(nb * S5_BLOCK_GROUPS, S5_STATE, S5_GROUP)


def _s5_out_blocks(c):
    g = c.shape[0]
    nb = g // S5_BLOCK_GROUPS
    t = c.reshape(nb, S5_BLOCK_GROUPS, S5_GROUP, S5_STATE).transpose(0, 1, 3, 2)
    eye = jnp.eye(S5_BLOCK_GROUPS, dtype=c.dtype)
    full = t[:, :, :, None, :] * eye[None, :, None, :, None]
    return full.reshape(nb, S5_BLOCK_GROUPS * S5_STATE, S5_BLOCK_GROUPS * S5_GROUP)


def _s5_out_blocks_diag(blocks):
    nb = blocks.shape[0]
    t = blocks.reshape(nb, S5_BLOCK_GROUPS, S5_STATE, S5_BLOCK_GROUPS, S5_GROUP)
    d = jnp.einsum("bgpgh->bgph", t)
    return d.transpose(0, 1, 3, 2).reshape(nb * S5_BLOCK_GROUPS, S5_GROUP, S5_STATE)


def _s5_scan_tables(lr, li, reverse):
    def cmul(a, b):
        return a[0] * b[0] - a[1] * b[1], a[0] * b[1] + a[1] * b[0]

    lam = (lr, -li) if reverse else (lr, li)
    pw = [lam]
    for _ in range(SUBLANES - 1):
        pw.append(cmul(pw[-1], lam))
    sub = jnp.arange(SUBLANES)[:, None]
    rows = []
    for s in (1, 2, 4):
        keep = (sub <= SUBLANES - 1 - s) if reverse else (sub >= s)
        rows.append(jnp.where(keep, pw[s - 1][0][None, :], 0.0))
        rows.append(jnp.where(keep, pw[s - 1][1][None, :], 0.0))
    order = list(range(SUBLANES - 1, -1, -1)) if reverse else list(range(SUBLANES))
    rows.append(jnp.stack([pw[i][0] for i in order]))
    rows.append(jnp.stack([pw[i][1] for i in order]))
    return jnp.concatenate(rows, axis=0)


def _s5_scan(vre_ref, vim_ref, coef_ref, seq, width, reverse, xre_ref=None, xim_ref=None):
    nt = seq // SUBLANES
    nl = width // LANES
    per = 2 if xre_ref is None else 4
    sub = lax.broadcasted_iota(jnp.int32, (SUBLANES, LANES), 0)

    def step(k, carry):
        kk = (nt - 1 - k) if reverse else k
        rows = pl.ds(pl.multiple_of(kk * SUBLANES, SUBLANES), SUBLANES)
        out = []
        for j in range(nl):
            lanes = slice(j * LANES, (j + 1) * LANES)
            co = [coef_ref[SUBLANES * q:SUBLANES * (q + 1), lanes] for q in range(8)]
            cr, ci = carry[per * j], carry[per * j + 1]
            vr = vre_ref[rows, lanes]
            vi = vim_ref[rows, lanes]
            for q, s in enumerate((1, 2, 4)):
                sh = SUBLANES - s if reverse else s
                rr = pltpu.roll(vr, sh, 0)
                ri = pltpu.roll(vi, sh, 0)
                ar, ai = co[2 * q], co[2 * q + 1]
                vr, vi = vr + ar * rr - ai * ri, vi + ar * ri + ai * rr
            edge = 0 if reverse else SUBLANES - 1
            cbr = jnp.broadcast_to(cr[edge:edge + 1, :], (SUBLANES, LANES))
            cbi = jnp.broadcast_to(ci[edge:edge + 1, :], (SUBLANES, LANES))
            pr, pi = co[6], co[7]
            vr, vi = vr + pr * cbr - pi * cbi, vi + pr * cbi + pi * cbr
            vre_ref[rows, lanes] = vr
            vim_ref[rows, lanes] = vi
            out += [vr, vi]
            if xre_ref is not None:
                nr = jnp.where(sub == SUBLANES - 1, cbr, pltpu.roll(vr, SUBLANES - 1, 0))
                ni = jnp.where(sub == SUBLANES - 1, cbi, pltpu.roll(vi, SUBLANES - 1, 0))
                xr = xre_ref[rows, lanes]
                xi = xim_ref[rows, lanes]
                out += [carry[per * j + 2] + nr * xr + ni * xi, carry[per * j + 3] + ni * xr - nr * xi]
        return tuple(out)

    zero = jnp.zeros((SUBLANES, LANES), F32)
    res = lax.fori_loop(0, nt, step, (zero,) * (per * nl))
    if xre_ref is None:
        return None
    return jnp.concatenate(
        [jnp.concatenate([jnp.sum(res[per * j + 2], axis=0, keepdims=True) for j in range(nl)], axis=1),
         jnp.concatenate([jnp.sum(res[per * j + 3], axis=0, keepdims=True) for j in range(nl)], axis=1)], axis=0)


def _s5_fwd(z, bre3, bim3, cre3, cim3, coef, dskip, *, nseq, seq, rider=None):
    nb = bre3.shape[0]
    ch, ns = bre3.shape[1], bre3.shape[2]

    def body(za_ref, bre_ref, bim_ref, cre_ref, cim_ref, coef_ref, d_ref, y_ref, xre_ref, xim_ref):
        za = za_ref[...]
        xre_ref[...] = _dot(za, bre_ref[...])
        xim_ref[...] = _dot(za, bim_ref[...])
        _s5_scan(xre_ref, xim_ref, coef_ref, seq, ns, False)
        y_ref[...] = _dot(xre_ref[...], cre_ref[...]) - _dot(xim_ref[...], cim_ref[...]) + d_ref[...] * za

    blk3 = lambda r, c: pl.BlockSpec((None, r, c), lambda b, j: (j, 0, 0))
    return _hosted_call(
        "s5_fwd", body, grid=(nseq, nb),
        in_specs=[pl.BlockSpec((seq, ch), lambda b, j: (b, j)), blk3(ch, ns), blk3(ch, ns), blk3(ns, ch), blk3(ns, ch),
                  pl.BlockSpec((8 * SUBLANES, ns), lambda b, j: (0, j)), pl.BlockSpec((1, ch), lambda b, j: (0, j))],
        out_specs=[pl.BlockSpec((seq, ch), lambda b, j: (b, j)), pl.BlockSpec((seq, ns), lambda b, j: (b, j)),
                   pl.BlockSpec((seq, ns), lambda b, j: (b, j))],
        out_shape=[jax.ShapeDtypeStruct((nseq * seq, nb * ch), F32), jax.ShapeDtypeStruct((nseq * seq, nb * ns), F32),
                   jax.ShapeDtypeStruct((nseq * seq, nb * ns), F32)],
        operands=(z, bre3, bim3, cre3, cim3, coef, dskip), rider=rider)


def _s5_bwd(dy, z, xre, xim, bre3, bim3, cre3, cim3, coef_rev, dskip, *, nseq, seq, rider=None):
    nb = bre3.shape[0]
    ch, ns = bre3.shape[1], bre3.shape[2]

    def body(dy_ref, za_ref, xre_ref, xim_ref, bre_ref, bim_ref, cre_ref, cim_ref, coef_ref, d_ref,
             dza_ref, dbre_ref, dbim_ref, dcre_ref, dcim_ref, dlam_ref, dd_ref, are_ref, aim_ref):
        dy = dy_ref[...]
        za = za_ref[...]
        are_ref[...] = _dot_nt(dy, cre_ref[...])
        aim_ref[...] = -_dot_nt(dy, cim_ref[...])
        dlam = _s5_scan(are_ref, aim_ref, coef_ref, seq, ns, True, xre_ref, xim_ref)
        are = are_ref[...]
        aim = aim_ref[...]
        dza_ref[...] = (_dot_nt(are, bre_ref[...]) + _dot_nt(aim, bim_ref[...]) + d_ref[...] * dy).astype(dza_ref.dtype)
        parts = (_dot_tn(za, are), _dot_tn(za, aim), _dot_tn(xre_ref[...], dy), -_dot_tn(xim_ref[...], dy),
                 dlam, jnp.sum(dy * za, axis=0, keepdims=True))
        first = pl.program_id(1) == 0
        for r, v in zip((dbre_ref, dbim_ref, dcre_ref, dcim_ref, dlam_ref, dd_ref), parts):
            @pl.when(first)
            def _():
                r[...] = v

            @pl.when(jnp.logical_not(first))
            def _():
                r[...] += v

    blk3 = lambda r, c: pl.BlockSpec((None, r, c), lambda j, b: (j, 0, 0))
    tok = lambda c: pl.BlockSpec((seq, c), lambda j, b: (b, j))
    return _hosted_call(
        "s5_bwd", body, grid=(nb, nseq),
        in_specs=[tok(ch), tok(ch), tok(ns), tok(ns), blk3(ch, ns), blk3(ch, ns), blk3(ns, ch), blk3(ns, ch),
                  pl.BlockSpec((8 * SUBLANES, ns), lambda j, b: (0, j)), pl.BlockSpec((1, ch), lambda j, b: (0, j))],
        out_specs=[tok(ch), blk3(ch, ns), blk3(ch, ns), blk3(ns, ch), blk3(ns, ch),
                   pl.BlockSpec((None, 2, ns), lambda j, b: (j, 0, 0)), pl.BlockSpec((1, ch), lambda j, b: (0, j))],
        out_shape=[jax.ShapeDtypeStruct((nseq * seq, nb * ch), BF16),
                   jax.ShapeDtypeStruct((nb, ch, ns), F32), jax.ShapeDtypeStruct((nb, ch, ns), F32),
                   jax.ShapeDtypeStruct((nb, ns, ch), F32), jax.ShapeDtypeStruct((nb, ns, ch), F32),
                   jax.ShapeDtypeStruct((nb, 2, ns), F32), jax.ShapeDtypeStruct((1, nb * ch), F32)],
        scratch_shapes=[pltpu.VMEM((seq, ns), F32), pltpu.VMEM((seq, ns), F32)],
        operands=(dy, z, xre, xim, bre3, bim3, cre3, cim3, coef_rev, dskip), rider=rider)


def _glu_fwd(y, wglu, bglu):
    t, w = y.shape
    tm = _row_tile(t, GLU_TILE)

    def body(y_ref, w_ref, b_ref, a0_ref, gl_ref, a_ref):
        a0 = _gelu(y_ref[...])
        gl = _dot(a0, w_ref[...])
        a0_ref[...] = a0.astype(a0_ref.dtype)
        gl_ref[...] = gl
        a_ref[...] = (a0 * _sigmoid(gl + b_ref[...])).astype(a_ref.dtype)

    tok = pl.BlockSpec((tm, w), lambda i: (i, 0))
    return pl.pallas_call(
        body, name="s5_glu", grid=(t // tm,),
        in_specs=[tok, pl.BlockSpec((w, w), lambda i: (0, 0)), pl.BlockSpec((1, w), lambda i: (0, 0))],
        out_specs=[tok, tok, tok],
        out_shape=[jax.ShapeDtypeStruct((t, w), BF16), jax.ShapeDtypeStruct((t, w), F32), jax.ShapeDtypeStruct((t, w), BF16)],
        compiler_params=_params("arbitrary"),
    )(y, wglu, bglu)


def _glu_bwd(y, gl, bglu, da, wglu):
    t, w = y.shape
    tm = _row_tile(t, GLU_TILE)

    def body(y_ref, gl_ref, b_ref, da_ref, w_ref, dgl_ref, dy_ref, db_ref):
        yv = y_ref[...]
        dav = da_ref[...]
        s = _sigmoid(gl_ref[...] + b_ref[...])
        dgl = dav * _gelu(yv) * s * (1.0 - s)
        dgl_ref[...] = dgl.astype(dgl_ref.dtype)
        dy_ref[...] = (dav * s + _dot_nt(dgl, w_ref[...])) * _gelu_grad(yv)
        part = jnp.sum(dgl, axis=0, keepdims=True)
        first = pl.program_id(0) == 0

        @pl.when(first)
        def _():
            db_ref[...] = part

        @pl.when(jnp.logical_not(first))
        def _():
            db_ref[...] += part

    tok = pl.BlockSpec((tm, w), lambda i: (i, 0))
    vec = pl.BlockSpec((1, w), lambda i: (0, 0))
    return pl.pallas_call(
        body, name="s5_glu_bwd", grid=(t // tm,),
        in_specs=[tok, tok, vec, tok, pl.BlockSpec((w, w), lambda i: (0, 0))], out_specs=[tok, tok, vec],
        out_shape=[jax.ShapeDtypeStruct((t, w), BF16), jax.ShapeDtypeStruct((t, w), F32), jax.ShapeDtypeStruct((1, w), F32)],
        compiler_params=_params("arbitrary"),
    )(y, gl, bglu, da, wglu)


def _cumsum_rows(x, reverse=False):
    n = x.shape[0]
    row = lax.broadcasted_iota(jnp.int32, x.shape, 0)
    s = 1
    while s < n:
        if reverse:
            x = x + jnp.where(row < n - s, pltpu.roll(x, n - s, 0), 0.0)
        else:
            x = x + jnp.where(row >= s, pltpu.roll(x, s, 0), 0.0)
        s *= 2
    return x


def _hg_gates(zq, zf, lb):
    sg = _sigmoid(zf)
    f = lb + (1.0 - lb) * sg
    sq = _sigmoid(zq)
    qa = zq * sq * (HEAD ** -0.5)
    b = _cumsum_rows(jnp.log(f))
    return sg, f, sq, qa, 1.0 - f, b


SUB = 16


def _hg_scores(qa, kk, b):
    c = qa.shape[0]
    row = lax.broadcasted_iota(jnp.int32, qa.shape, 0)
    pos = jnp.bitwise_and(row, SUB - 1)
    dmat = lax.broadcasted_iota(jnp.int32, (c, c), 0) - lax.broadcasted_iota(jnp.int32, (c, c), 1)
    p = jnp.zeros((c, c), F32)
    for d in range(SUB):
        if d == 0:
            fd = qa * kk
        else:
            e = jnp.exp(jnp.minimum(b - pltpu.roll(b, d, 0), 0.0))
            fd = jnp.where(pos >= d, qa * pltpu.roll(kk, d, 0) * e, 0.0)
        p = jnp.where(dmat == d, jnp.sum(fd, axis=1, keepdims=True), p)
    col = lax.broadcasted_iota(jnp.int32, (SUB, c), 1)
    blocks = [jnp.zeros((SUB, c), F32)]
    for r0 in range(SUB, c, SUB):
        beta = b[r0 - 1:r0, :]
        qt = qa[r0:r0 + SUB] * jnp.exp(b[r0:r0 + SUB] - beta)
        kt = kk * jnp.exp(jnp.minimum(beta - b, 0.0))
        blocks.append(jnp.where(col < r0, _dot_nt(qt, kt), 0.0))
    return p + jnp.concatenate(blocks, axis=0)


def _hg_scores_bwd(dp, qa, kk, b):
    c = qa.shape[0]
    row = lax.broadcasted_iota(jnp.int32, qa.shape, 0)
    pos = jnp.bitwise_and(row, SUB - 1)
    dmat = lax.broadcasted_iota(jnp.int32, (c, c), 0) - lax.broadcasted_iota(jnp.int32, (c, c), 1)
    dqa = jnp.zeros_like(qa)
    dkk = jnp.zeros_like(qa)
    db = jnp.zeros_like(qa)
    for d in range(SUB):
        dcol = jnp.sum(jnp.where(dmat == d, dp, 0.0), axis=1, keepdims=True)
        if d == 0:
            dqa = dqa + dcol * kk
            dkk = dkk + dcol * qa
        else:
            e = jnp.exp(jnp.minimum(b - pltpu.roll(b, d, 0), 0.0))
            w = jnp.where(pos >= d, dcol * e, 0.0)
            kr = pltpu.roll(kk, d, 0)
            dqa = dqa + w * kr
            tmp = w * qa
            dkk = dkk + pltpu.roll(tmp, c - d, 0)
            x = tmp * kr
            db = db + x - pltpu.roll(x, c - d, 0)
    col = lax.broadcasted_iota(jnp.int32, (SUB, c), 1)
    dq_blocks = [jnp.zeros((SUB, qa.shape[1]), F32)]
    db_blocks = [jnp.zeros((SUB, qa.shape[1]), F32)]
    for r0 in range(SUB, c, SUB):
        beta = b[r0 - 1:r0, :]
        eq = jnp.exp(b[r0:r0 + SUB] - beta)
        ek = jnp.exp(jnp.minimum(beta - b, 0.0))
        qt = qa[r0:r0 + SUB] * eq
        kt = kk * ek
        dpi = jnp.where(col < r0, dp[r0:r0 + SUB, :], 0.0)
        dqt = _dot(dpi, kt)
        dkt = _dot_tn(dpi, qt)
        dq_blocks.append(dqt * eq)
        db_blocks.append(dqt * qt)
        dkk = dkk + dkt * ek
        db = db - dkt * kt
    return dqa + jnp.concatenate(dq_blocks, axis=0), dkk, db + jnp.concatenate(db_blocks, axis=0)


def _hg_chunks_per_step(seq):
    nc = seq // CHUNK
    cps = next(k for k in (4, 2, 1) if nc % k == 0)
    return nc, cps, nc // cps


def _hg_fwd(z, lbrow, gain, *, nseq, seq, heads, qoff, rider=None):
    nc, cps, nblk = _hg_chunks_per_step(seq)
    blk = cps * CHUNK
    zspec = lambda off: pl.BlockSpec((blk, HEAD), lambda h, b, n, off=off: (b * nblk + n, off + h))

    def body(zq_ref, zf_ref, zi_ref, zg_ref, lb_ref, gn_ref, o_ref, yb_ref, st_ref, sc_ref, state):
        @pl.when(pl.program_id(2) == 0)
        def _():
            state[...] = jnp.zeros_like(state)

        lb = lb_ref[...]
        gain_v = gn_ref[...]

        def chunk(ci, carry):
            rows = pl.ds(pl.multiple_of(ci * CHUNK, CHUNK), CHUNK)
            st = state[...]
            st_ref[ci] = st
            zi = zi_ref[rows, :]
            zg = zg_ref[rows, :]
            _, _, _, qa, kk, b = _hg_gates(zq_ref[rows, :], zf_ref[rows, :], lb)
            scores = _hg_scores(qa, kk, b).astype(BF16)
            sc_ref[rows, :] = scores
            o = _dot_nt(qa * jnp.exp(b), st) + _dot(scores, zi)
            bl = b[CHUNK - 1:CHUNK, :]
            state[...] = st * jnp.exp(bl) + _dot_tn(zi, kk * jnp.exp(bl - b))
            o_ref[rows, :] = o
            r = lax.rsqrt(jnp.mean(o * o, axis=1, keepdims=True) + EPS)
            yb_ref[rows, :] = (o * r * gain_v * zg * _sigmoid(zg)).astype(yb_ref.dtype)
            return carry

        lax.fori_loop(0, cps, chunk, 0, unroll=True)

    tok = pl.BlockSpec((blk, HEAD), lambda h, b, n: (b * nblk + n, h))
    vec = pl.BlockSpec((1, HEAD), lambda h, b, n: (0, h))
    rows = nseq * seq
    return _hosted_call(
        "hgrn2_fwd", body, grid=(heads, nseq, nblk),
        in_specs=[zspec(qoff), zspec(qoff + heads), zspec(qoff + 2 * heads), zspec(qoff + 3 * heads), vec, vec],
        out_specs=[tok, tok, pl.BlockSpec((None, None, cps, HEAD, HEAD), lambda h, b, n: (h, b, n, 0, 0)),
                   pl.BlockSpec((None, blk, CHUNK), lambda h, b, n: (h, b * nblk + n, 0))],
        out_shape=[jax.ShapeDtypeStruct((rows, heads * HEAD), F32), jax.ShapeDtypeStruct((rows, heads * HEAD), BF16),
                   jax.ShapeDtypeStruct((heads, nseq, nc, HEAD, HEAD), F32),
                   jax.ShapeDtypeStruct((heads, rows, CHUNK), BF16)],
        scratch_shapes=[pltpu.VMEM((HEAD, HEAD), F32)], operands=(z, z, z, z, lbrow, gain), rider=rider)


def _hg_bwd(dyb, z, o, states, scores, lbrow, gain, *, nseq, seq, heads, qoff, rider=None):
    nc, cps, nblk = _hg_chunks_per_step(seq)
    blk = cps * CHUNK
    rev = lambda n: nblk - 1 - n
    zspec = lambda off: pl.BlockSpec((blk, HEAD), lambda h, b, n, off=off: (b * nblk + rev(n), off + h))

    def body(dyb_ref, zq_ref, zf_ref, zi_ref, zg_ref, o_ref, st_ref, sc_ref, lb_ref, gn_ref,
             dzq_ref, dzf_ref, dzi_ref, dzg_ref, dlb_ref, dgn_ref, dstate):
        @pl.when(pl.program_id(2) == 0)
        def _():
            dstate[...] = jnp.zeros_like(dstate)

        @pl.when(jnp.logical_and(pl.program_id(1) == 0, pl.program_id(2) == 0))
        def _():
            dlb_ref[...] = jnp.zeros_like(dlb_ref)
            dgn_ref[...] = jnp.zeros_like(dgn_ref)

        lb = lb_ref[...]
        gain_v = gn_ref[...]
        c = CHUNK
        causal = lax.broadcasted_iota(jnp.int32, (c, c), 0) >= lax.broadcasted_iota(jnp.int32, (c, c), 1)

        def chunk(step, carry):
            ci = cps - 1 - step
            rows = pl.ds(pl.multiple_of(ci * CHUNK, CHUNK), CHUNK)
            zq = zq_ref[rows, :]
            zi = zi_ref[rows, :]
            zg = zg_ref[rows, :]
            sg, f, sq, qa, kk, b = _hg_gates(zq, zf_ref[rows, :], lb)
            eb = jnp.exp(b)
            qt = qa * eb
            bl = b[c - 1:c, :]
            ebl = jnp.exp(bl)
            ekb = jnp.exp(bl - b)
            kh = kk * ekb
            st = st_ref[ci]
            dst = dstate[...]
            o = o_ref[rows, :]
            r = lax.rsqrt(jnp.mean(o * o, axis=1, keepdims=True) + EPS)
            oh = o * r
            szg = _sigmoid(zg)
            dyb = dyb_ref[rows, :]
            don = dyb * zg * szg
            dzg_ref[rows, :] = (dyb * oh * gain_v * szg * (1.0 + zg * (1.0 - szg))).astype(dzg_ref.dtype)
            doh = don * gain_v
            do = r * (doh - oh * jnp.mean(doh * oh, axis=1, keepdims=True))
            dqt = _dot(do, st)
            dp = jnp.where(causal, _dot_nt(do, zi), 0.0)
            dzi_ref[rows, :] = (_dot_tn(sc_ref[rows, :], do) + _dot_nt(kh, dst)).astype(dzi_ref.dtype)
            dkh = _dot(zi, dst)
            dbl = jnp.sum(dkh * kh, axis=0, keepdims=True) + jnp.sum(dst * st, axis=0, keepdims=True) * ebl
            dstate[...] = _dot_tn(do, qt) + dst * ebl
            dqa_s, dkk_s, db_s = _hg_scores_bwd(dp, qa, kk, b)
            dqa = dqt * eb + dqa_s
            dkk = dkh * ekb + dkk_s
            db = dqt * qt - dkh * kh + db_s
            row = lax.broadcasted_iota(jnp.int32, db.shape, 0)
            db = db + jnp.where(row == c - 1, dbl, 0.0)
            df = _cumsum_rows(db, reverse=True) / f - dkk
            dzf_ref[rows, :] = (df * (1.0 - lb) * sg * (1.0 - sg)).astype(dzf_ref.dtype)
            dzq_ref[rows, :] = (dqa * (HEAD ** -0.5) * sq * (1.0 + zq * (1.0 - sq))).astype(dzq_ref.dtype)
            dlb_ref[...] += jnp.sum(df * (1.0 - sg), axis=0, keepdims=True)
            dgn_ref[...] += jnp.sum(don * oh, axis=0, keepdims=True)
            return carry

        lax.fori_loop(0, cps, chunk, 0, unroll=True)

    tok = pl.BlockSpec((blk, HEAD), lambda h, b, n: (b * nblk + rev(n), h))
    vec = pl.BlockSpec((1, HEAD), lambda h, b, n: (0, h))
    rows = nseq * seq
    return _hosted_call(
        "hgrn2_bwd", body, grid=(heads, nseq, nblk),
        in_specs=[tok, zspec(qoff), zspec(qoff + heads), zspec(qoff + 2 * heads), zspec(qoff + 3 * heads), tok,
                  pl.BlockSpec((None, None, cps, HEAD, HEAD), lambda h, b, n: (h, b, rev(n), 0, 0)),
                  pl.BlockSpec((None, blk, CHUNK), lambda h, b, n: (h, b * nblk + rev(n), 0)), vec, vec],
        out_specs=[tok, tok, tok, tok, vec, vec],
        out_shape=[jax.ShapeDtypeStruct((rows, heads * HEAD), BF16)] * 4
        + [jax.ShapeDtypeStruct((1, heads * HEAD), F32)] * 2,
        scratch_shapes=[pltpu.VMEM((HEAD, HEAD), F32)],
        operands=(dyb, z, z, z, z, o, states, scores, lbrow, gain), rider=rider)


def _shift_rows(x, k):
    n = x.shape[0]
    r = pltpu.roll(x, k % n, 0)
    sub = lax.broadcasted_iota(jnp.int32, (SUBLANES, x.shape[1]), 0)
    if k > 0:
        return jnp.concatenate([jnp.where(sub >= k, r[0:SUBLANES], 0.0), r[SUBLANES:]], axis=0)
    return jnp.concatenate([r[:n - SUBLANES], jnp.where(sub < SUBLANES + k, r[n - SUBLANES:], 0.0)], axis=0)


def _conv_taps(h, w, bias):
    h1 = _shift_rows(h, 1)
    h2 = _shift_rows(h, 2)
    return h2 * w[0:1, :] + h1 * w[1:2, :] + h * w[2:3, :] + bias, h1, h2


def _conv_fwd(h, wconv, bconv, *, nseq, seq):
    ff2 = h.shape[1]
    ncol = ff2 // 2 // LANES

    def body(hg_ref, hv_ref, wg_ref, wv_ref, bg_ref, bv_ref, a_ref):
        g, _, _ = _conv_taps(hg_ref[...].astype(F32), wg_ref[...], bg_ref[...])
        v, _, _ = _conv_taps(hv_ref[...].astype(F32), wv_ref[...], bv_ref[...])
        a_ref[...] = (g * _sigmoid(g) * v).astype(a_ref.dtype)

    tok = lambda off: pl.BlockSpec((seq, LANES), lambda j, b, off=off: (b, off + j))
    wsp = lambda off: pl.BlockSpec((CONV_W, LANES), lambda j, b, off=off: (0, off + j))
    bsp = lambda off: pl.BlockSpec((1, LANES), lambda j, b, off=off: (0, off + j))
    return pl.pallas_call(
        body, name="conv_fwd", grid=(ncol, nseq),
        in_specs=[tok(0), tok(ncol), wsp(0), wsp(ncol), bsp(0), bsp(ncol)],
        out_specs=tok(0), out_shape=jax.ShapeDtypeStruct((nseq * seq, ff2 // 2), BF16),
        compiler_params=_params("arbitrary", "arbitrary"),
    )(h, h, wconv, wconv, bconv, bconv)


def _conv_bwd(da, h, wconv, bconv, *, nseq, seq):
    ff2 = h.shape[1]
    ncol = ff2 // 2 // LANES

    def half_bwd(d, hcur, h1, h2, w):
        d1 = _shift_rows(d, -1)
        d2 = _shift_rows(d, -2)
        dh = d * w[2:3, :] + d1 * w[1:2, :] + d2 * w[0:1, :]
        stats = jnp.concatenate(
            [jnp.sum(h2 * d, axis=0, keepdims=True), jnp.sum(h1 * d, axis=0, keepdims=True),
             jnp.sum(hcur * d, axis=0, keepdims=True), jnp.sum(d, axis=0, keepdims=True),
             jnp.zeros((SUBLANES - 4, d.shape[1]), F32)], axis=0)
        return dh, stats

    def body(da_ref, hg_ref, hv_ref, wg_ref, wv_ref, bg_ref, bv_ref, dhg_ref, dhv_ref, sg_ref, sv_ref):
        hg = hg_ref[...].astype(F32)
        hv = hv_ref[...].astype(F32)
        wg = wg_ref[...]
        wv = wv_ref[...]
        g, g1, g2 = _conv_taps(hg, wg, bg_ref[...])
        v, v1, v2 = _conv_taps(hv, wv, bv_ref[...])
        da = da_ref[...].astype(F32)
        s = _sigmoid(g)
        dhg, stg = half_bwd(da * v * s * (1.0 + g * (1.0 - s)), hg, g1, g2, wg)
        dhv, stv = half_bwd(da * g * s, hv, v1, v2, wv)
        dhg_ref[...] = dhg.astype(dhg_ref.dtype)
        dhv_ref[...] = dhv.astype(dhv_ref.dtype)
        first = pl.program_id(1) == 0
        for r, val in ((sg_ref, stg), (sv_ref, stv)):
            @pl.when(first)
            def _():
                r[...] = val

            @pl.when(jnp.logical_not(first))
            def _():
                r[...] += val

    tok = lambda off: pl.BlockSpec((seq, LANES), lambda j, b, off=off: (b, off + j))
    wsp = lambda off: pl.BlockSpec((CONV_W, LANES), lambda j, b, off=off: (0, off + j))
    bsp = lambda off: pl.BlockSpec((1, LANES), lambda j, b, off=off: (0, off + j))
    ssp = pl.BlockSpec((SUBLANES, LANES), lambda j, b: (0, j))
    dhg, dhv, stg, stv = pl.pallas_call(
        body, name="conv_bwd", grid=(ncol, nseq),
        in_specs=[tok(0), tok(0), tok(ncol), wsp(0), wsp(ncol), bsp(0), bsp(ncol)],
        out_specs=[tok(0), tok(0), ssp, ssp],
        out_shape=[jax.ShapeDtypeStruct((nseq * seq, ff2 // 2), BF16)] * 2
        + [jax.ShapeDtypeStruct((SUBLANES, ff2 // 2), F32)] * 2,
        compiler_params=_params("arbitrary", "arbitrary"),
    )(da, h, h, wconv, wconv, bconv, bconv)
    return (dhg, dhv), jnp.concatenate([stg, stv], axis=1)


def _rms_fwd(xv, g):
    r = lax.rsqrt(jnp.mean(xv * xv, axis=1, keepdims=True) + EPS)
    return (xv * r * g,)


def _rms_bwd(xv, g, dy, res):
    r = lax.rsqrt(jnp.mean(xv * xv, axis=1, keepdims=True) + EPS)
    xh = xv * r
    dxh = dy * g
    dx = r * (dxh - xh * jnp.mean(dxh * xh, axis=1, keepdims=True)) + res
    return dx, jnp.sum(dy * xh, axis=0, keepdims=True)


def _loss_head(x2, tgt, g):
    d = x2.shape[1]
    r = lax.rsqrt(jnp.mean(x2 * x2, axis=1, keepdims=True) + EPS)
    xh = x2 * r
    err = xh * g - tgt
    dy = err * (1.0 / d)
    dxh = dy * g
    dx = r * (dxh - xh * jnp.mean(dxh * xh, axis=1, keepdims=True))
    loss = 0.5 * jnp.sum(jnp.mean(err * err, axis=1, keepdims=True), axis=0, keepdims=True)
    return dx, jnp.sum(dy * xh, axis=0, keepdims=True), jnp.broadcast_to(loss, (1, LANES))


LATE_A = ("w_down", "w_out")
LATE_B = ("w_up", "w_pa", "w_pb")
LATE = LATE_A + LATE_B
EARLY_GRADS = ("w_down", "w_up", "w_out", "w_pa", "w_pb", "w_glu")
ROW_SHARDED = ("w_glu", "w_out", "w_down")


def _local_step(x, tgt, u, p, late, *, nseq, seq):
    p = dict(p)
    chip = 2 * lax.axis_index("x") + lax.axis_index("y")
    t, d = x.shape
    s5w = p["s5_d"].shape[1]
    hgw = p["gain"].shape[1]
    heads = hgw // HEAD
    qoff = s5w // LANES
    gblk = (s5w + 4 * hgw) // GATE_BLOCK
    ngb = d // GATE_BLOCK
    tm = _row_tile(t, ROW_TILE)
    row = lambda a, w=None, base=0: (a, a.shape[1] if w is None else w, base, "row")
    vec = lambda a, w=None, base=0: (a, a.shape[1] if w is None else w, base, "vec")
    rw = functools.partial(_rowwise, rows=t, tm=tm)

    z, landed_a = _mm_fwd_cols("in_proj", u, p["w_in"], rider=_gather_ici_rider([late[n] for n in LATE_A]))

    lam_re, lam_im, bb_re, bb_im = _s5_discretize(p["s5_a_re"], p["s5_a_im"], p["s5_log_dt"], p["s5_b_re"], p["s5_b_im"])
    bre3 = _s5_in_blocks(bb_re).astype(BF16)
    bim3 = _s5_in_blocks(bb_im).astype(BF16)
    cre3 = _s5_out_blocks(p["s5_c_re"]).astype(BF16)
    cim3 = _s5_out_blocks(p["s5_c_im"]).astype(BF16)
    coef_f = _s5_scan_tables(lam_re.reshape(-1), lam_im.reshape(-1), False)
    coef_r = _s5_scan_tables(lam_re.reshape(-1), lam_im.reshape(-1), True)
    (o, yb, states, scores), landed_b = _hg_fwd(z, p["lbrow"], p["gain"], nseq=nseq, seq=seq, heads=heads, qoff=qoff,
                                                rider=_gather_ici_rider([late[n] for n in LATE_B]))
    (y5, xre, xim), gathered = _s5_fwd(z, bre3, bim3, cre3, cim3, coef_f, p["s5_d"], nseq=nseq, seq=seq,
                                       rider=_gather_pass_rider(list(landed_a) + list(landed_b)))
    for n, g in zip(LATE, gathered):
        full = lax.dynamic_update_index_in_dim(g, late[n], chip, 0)
        p[n] = full.reshape(-1, full.shape[-1]) if n in ROW_SHARDED else full
    ya0, gl, ya = _glu_fwd(y5, p["w_glu"], p["b_glu"])

    joined = lambda w3: w3.transpose(1, 0, 2).reshape(w3.shape[1], -1)
    split = lambda g: g.reshape(g.shape[0], N_CHIPS, -1).transpose(1, 0, 2)
    wpa, wpb = joined(p["w_pa"]), joined(p["w_pb"])
    pa = _mm_fwd_rows("proj_a", ya, wpa, out_dtype=BF16)
    pb = _mm_fwd_rows("proj_b", yb, wpb, out_dtype=BF16)
    gb = GATE_BLOCK
    (m,) = rw("merge", lambda ga, gbv, a, b: (_sigmoid(ga) * a + _sigmoid(gbv) * b,),
              [row(z, gb, gblk), row(z, gb, gblk + ngb), row(pa, gb), row(pb, gb)], [(d, gb, BF16)], ncol=ngb)
    x1 = _mm_fwd_rows("out_proj", m, p["w_out"], res=x)

    (u2,) = rw("rms_ffn", _rms_fwd, [row(x1), vec(p["g_ffn"])], [(d, d, BF16)])
    h = _mm_fwd_cols("up_proj", u2, p["w_up"], out_dtype=BF16)
    a = _conv_fwd(h, p["w_conv"], p["b_conv"], nseq=nseq, seq=seq)
    dx2, dg_final, lossv = _mm_fwd_rows("down_proj", a, p["w_down"], res=x1,
                                        epilogue=(_loss_head, [tgt, p["g_final"]], [d, LANES]))

    norm_bwd = lambda dyv, xv, g, resv: _rms_bwd(xv, g, dyv, resv)
    da = _mm_bwd_rows("down_bwd", dx2, p["w_down"], out_dtype=BF16)
    g_wdown = _mm_wgrad_rows("down_wgrad", a, dx2)
    dh, cstats = _conv_bwd(da, h, p["w_conv"], p["b_conv"], nseq=nseq, seq=seq)
    dx1, dg_ffn = _mm_bwd_cols("up_bwd", dh, p["w_up"], epilogue=(norm_bwd, [x1, p["g_ffn"], dx2], [d]))
    g_wup = _mm_wgrad_cols("up_wgrad", u2, dh)

    dm = _mm_bwd_rows("out_bwd", dx1, p["w_out"], out_dtype=BF16)
    g_wout = _mm_wgrad_rows("out_wgrad", m, dx1)

    def merge_bwd(ga, gbv, av, bv, dmv):
        sa = _sigmoid(ga)
        sb = _sigmoid(gbv)
        return dmv * sa, dmv * sb, dmv * av * sa * (1.0 - sa), dmv * bv * sb * (1.0 - sb)

    dpa, dpb, dzga, dzgb = rw("merge_bwd", merge_bwd,
                              [row(z, gb, gblk), row(z, gb, gblk + ngb), row(pa, gb), row(pb, gb), row(dm, gb)],
                              [(d, gb, BF16)] * 4, ncol=ngb)
    dya = _mm_bwd_rows("proj_a_bwd", dpa, wpa)
    g_wpa = split(_mm_wgrad_rows("proj_a_wgrad", ya, dpa))
    dyb = _mm_bwd_rows("proj_b_bwd", dpb, wpb)
    g_wpb = split(_mm_wgrad_rows("proj_b_wgrad", yb, dpb))

    dgl, dy5, db_glu = _glu_bwd(y5, gl, p["b_glu"], dya, p["w_glu"])
    g_wglu = _mm_wgrad_rows("glu_wgrad", ya0, dgl)
    partial = dict(w_down=g_wdown, w_up=g_wup, w_out=g_wout, w_pa=g_wpa, w_pb=g_wpb, w_glu=g_wglu)
    parts = [_grad_parts(partial[n]) for n in EARLY_GRADS]
    (dza, dbre3, dbim3, dcre3, dcim3, dlam, dd), sib = _s5_bwd(
        dy5, z, xre, xim, bre3, bim3, cre3, cim3, coef_r, p["s5_d"], nseq=nseq, seq=seq, rider=_swap_halves_rider(parts))
    pair = _pair_sums(EARLY_GRADS, parts, sib)
    (dzq, dzf, dzi, dzg, dlb, dgain), others = _hg_bwd(
        dyb, z, o, states, scores, p["lbrow"], p["gain"], nseq=nseq, seq=seq, heads=heads, qoff=qoff,
        rider=_scatter_rider(pair))
    halves = _chip_sums(EARLY_GRADS, pair, others)

    dz = jnp.concatenate([dza, dzq, dzf, dzi, dzg, dzga, dzgb], axis=1)
    gshape = lam_re.shape
    small = {
        "loss": lossv, "g_ffn": dg_ffn, "g_final": dg_final, "b_glu": db_glu, "gain": dgain,
        "lbrow": dlb, "s5_d": dd, "w_conv": cstats[0:CONV_W], "b_conv": cstats[CONV_W:CONV_W + 1],
        "lam_re": dlam[:, 0, :].reshape(gshape), "lam_im": dlam[:, 1, :].reshape(gshape),
        "bb_re": _s5_in_blocks_diag(dbre3), "bb_im": _s5_in_blocks_diag(dbim3),
        "s5_c_re": _s5_out_blocks_diag(dcre3), "s5_c_im": _s5_out_blocks_diag(dcim3),
    }
    small_vec = _pack([small[n] for n in SMALL_PARTS], F32)
    g_win, (small_all, *sibs) = _mm_wgrad_cols(
        "in_wgrad", u, dz, rider=_merge_riders(_gather_all_rider(small_vec), _swap_sums_rider(halves)))
    big = dict(zip(EARLY_GRADS, zip(halves, sibs)))
    small_sum = _sum_over_devices("small_grad_sum", small_vec, small_all)
    sm = dict(zip(SMALL_PARTS, _unpack(small_sum, [small[n].shape for n in SMALL_PARTS])))
    last = [_grad_parts(g_win)]
    pair = _pair_sums(("w_in",), last, _run_rider("grad_swap_halves", _swap_halves_rider(last)))
    (dx, dg_mix), others = _mm_bwd_cols("in_bwd", dz, p["w_in"], epilogue=(norm_bwd, [x, p["g_mix"], dx1], [d]),
                                        rider=_scatter_rider(pair))
    (half,) = _chip_sums(("w_in",), pair, others)
    mid = half.shape[0] // 2
    mix_vec = dg_mix.reshape(SUBLANES, -1)
    top, bottom, mix_all = _run_rider("grad_swap_sums", _merge_riders(_swap_sums_rider([half[:mid], half[mid:]]),
                                                                      _gather_all_rider(mix_vec)))
    big["w_in"] = (half, jnp.concatenate([top, bottom], axis=0))
    sm["g_mix"] = _sum_over_devices("g_mix_sum", mix_vec, mix_all).reshape(dg_mix.shape)
    return dx, big, sm


ANY = pl.BlockSpec(memory_space=pl.ANY)


def _place():
    x, y, c = lax.axis_index("x"), lax.axis_index("y"), lax.axis_index("c")
    chips = [(1 - x, y), (x, 1 - y), (1 - x, 1 - y)]
    return x, y, c, chips


def _remote(src, dst, send_sems, recv_sems, k, to):
    return pltpu.make_async_remote_copy(src_ref=src, dst_ref=dst, send_sem=send_sems.at[k], recv_sem=recv_sems.at[k],
                                        device_id=to, device_id_type=MESH)


def _half(rows, which):
    return pl.ds(pl.multiple_of(which * (rows // 2), SUBLANES), rows // 2)


class _SemView:
    def __init__(self, base, offset):
        self.base, self.offset = base, offset

    @property
    def at(self):
        return self

    def __getitem__(self, k):
        return self.base.at[self.offset + k]


def _merge_riders(first, second):
    na, no, ns = len(first.arrays), len(first.out_shapes), first.nsem

    def split(fn_a, fn_b):
        def run(ins, outs, send_sems, recv_sems):
            fn_a(ins[:na], outs[:no], send_sems, recv_sems)
            fn_b(ins[na:], outs[no:], _SemView(send_sems, ns), _SemView(recv_sems, ns))
        return run

    aliases = dict(first.aliases)
    aliases.update({na + i: no + o for i, o in second.aliases.items()})
    return _Rider(first.arrays + second.arrays, first.out_shapes + second.out_shapes, ns + second.nsem,
                  split(first.start, second.start), split(first.finish, second.finish), aliases)


def _prepare(x, gain, arrays, rider):
    t, d = x.shape
    n = len(arrays)
    tm = _row_tile(t, ROW_TILE)

    def body(x_ref, g_ref, *refs):
        (u,) = _rms_fwd(x_ref[...], g_ref[...])
        refs[n][...] = u.astype(BF16)

        @pl.when(pl.program_id(0) == 0)
        def _():
            for i in range(n):
                refs[n + 1 + i][...] = refs[i][...].astype(BF16)

    vm = pl.BlockSpec(memory_space=pltpu.VMEM)
    tok = pl.BlockSpec((tm, d), lambda i: (i, 0))
    outs, gathered = _hosted_call(
        "prepare", body, grid=(t // tm,), in_specs=[tok, pl.BlockSpec((1, d), lambda i: (0, 0))] + [vm] * n,
        out_specs=[tok] + [vm] * n,
        out_shape=[jax.ShapeDtypeStruct((t, d), BF16)] + [jax.ShapeDtypeStruct(a.shape, BF16) for a in arrays],
        operands=[x, gain] + list(arrays), rider=rider)
    return outs[0], outs[1:], gathered


def _symmetric_rider(arrays, out_shapes, copies_of, nsem):
    def start(ins, outs, send_sems, recv_sems):
        for cp in copies_of(ins, outs, send_sems, recv_sems):
            cp.start()

    def finish(ins, outs, send_sems, recv_sems):
        for cp in copies_of(ins, outs, send_sems, recv_sems):
            cp.wait()

    return _Rider(arrays, out_shapes, nsem, start, finish)


def _swap_halves_rider(parts):
    def copies_of(ins, outs, send_sems, recv_sems):
        x, y, c, _ = _place()
        return [_remote(ins[a].at[:, _half(g.shape[1], 1 - c), :], outs[a], send_sems, recv_sems, a, (x, y, 1 - c))
                for a, g in enumerate(parts)]

    shapes = [jax.ShapeDtypeStruct((g.shape[0], g.shape[1] // 2, g.shape[2]), g.dtype) for g in parts]
    return _symmetric_rider(parts, shapes, copies_of, len(parts))


def _scatter_rider(parts):
    def copies_of(ins, outs, send_sems, recv_sems):
        x, y, c, chips = _place()
        return [_remote(ins[a].at[2 * cx + cy], outs[a].at[j], send_sems, recv_sems, 3 * a + j, (cx, cy, c))
                for a in range(len(parts)) for j, (cx, cy) in enumerate(chips)]

    shapes = [jax.ShapeDtypeStruct((N_CHIPS - 1,) + h.shape[1:], h.dtype) for h in parts]
    return _symmetric_rider(parts, shapes, copies_of, 3 * len(parts))


def _swap_sums_rider(parts):
    def copies_of(ins, outs, send_sems, recv_sems):
        x, y, c, _ = _place()
        return [_remote(ins[a], outs[a], send_sems, recv_sems, a, (x, y, 1 - c)) for a in range(len(parts))]

    shapes = [jax.ShapeDtypeStruct(g.shape, g.dtype) for g in parts]
    return _symmetric_rider(parts, shapes, copies_of, len(parts))


def _gather_ici_rider(shards):
    def sends(ins, outs, send_sems, recv_sems):
        x, y, c, chips = _place()
        return [_remote(ins[a].at[_half(s.shape[0], c)], outs[a].at[2 * x + y, _half(s.shape[0], c)], send_sems,
                        recv_sems, 3 * a + j, (cx, cy, c)) for a, s in enumerate(shards) for j, (cx, cy) in enumerate(chips)]

    def start(ins, outs, send_sems, recv_sems):
        for cp in sends(ins, outs, send_sems, recv_sems):
            cp.start()

    def finish(ins, outs, send_sems, recv_sems):
        x, y, c, chips = _place()
        for a, s in enumerate(shards):
            for j, (cx, cy) in enumerate(chips):
                landed = outs[a].at[2 * cx + cy, _half(s.shape[0], c)]
                _remote(landed, landed, send_sems, recv_sems, 3 * a + j, (x, y, c)).wait_recv()
        for cp in sends(ins, outs, send_sems, recv_sems):
            cp.wait_send()

    shapes = [jax.ShapeDtypeStruct((N_CHIPS,) + s.shape, s.dtype) for s in shards]
    return _Rider(shards, shapes, 3 * len(shards), start, finish)


def _gather_full_rider(shards):
    n = len(shards)

    def sends(ins, outs, send_sems, recv_sems):
        x, y, c, chips = _place()
        return [_remote(ins[a].at[_half(s.shape[0], c)], outs[a].at[2 * x + y, _half(s.shape[0], c)], send_sems,
                        recv_sems, 6 * a + j, (cx, cy, c)) for a, s in enumerate(shards) for j, (cx, cy) in enumerate(chips)]

    def start(ins, outs, send_sems, recv_sems):
        for cp in sends(ins, outs, send_sems, recv_sems):
            cp.start()

    def finish(ins, outs, send_sems, recv_sems):
        x, y, c, chips = _place()
        passed = []
        for a, s in enumerate(shards):
            for j, (cx, cy) in enumerate(chips):
                landed = outs[a].at[2 * cx + cy, _half(s.shape[0], c)]
                _remote(landed, landed, send_sems, recv_sems, 6 * a + j, (x, y, c)).wait_recv()
                passed.append(_remote(landed, landed, send_sems, recv_sems, 6 * a + 3 + j, (x, y, 1 - c)))
                passed[-1].start()
        for a, s in enumerate(shards):
            for j, (cx, cy) in enumerate(chips):
                other = outs[a].at[2 * cx + cy, _half(s.shape[0], 1 - c)]
                _remote(other, other, send_sems, recv_sems, 6 * a + 3 + j, (x, y, c)).wait_recv()
        for cp in sends(ins, outs, send_sems, recv_sems) + passed:
            cp.wait_send()

    shapes = [jax.ShapeDtypeStruct((N_CHIPS,) + s.shape, s.dtype) for s in shards]
    return _Rider(shards, shapes, 6 * n, start, finish)


def _gather_pass_rider(landed):
    def sends(ins, outs, send_sems, recv_sems):
        x, y, c, chips = _place()
        return [_remote(ins[a].at[2 * cx + cy, _half(g.shape[1], c)], outs[a].at[2 * cx + cy, _half(g.shape[1], c)],
                        send_sems, recv_sems, 3 * a + j, (x, y, 1 - c))
                for a, g in enumerate(landed) for j, (cx, cy) in enumerate(chips)]

    def start(ins, outs, send_sems, recv_sems):
        for cp in sends(ins, outs, send_sems, recv_sems):
            cp.start()

    def finish(ins, outs, send_sems, recv_sems):
        x, y, c, chips = _place()
        for a, g in enumerate(landed):
            for j, (cx, cy) in enumerate(chips):
                other = outs[a].at[2 * cx + cy, _half(g.shape[1], 1 - c)]
                _remote(other, other, send_sems, recv_sems, 3 * a + j, (x, y, c)).wait_recv()
        for cp in sends(ins, outs, send_sems, recv_sems):
            cp.wait_send()

    shapes = [jax.ShapeDtypeStruct(g.shape, g.dtype) for g in landed]
    return _Rider(landed, shapes, 3 * len(landed), start, finish, aliases={a: a for a in range(len(landed))})


def _grad_parts(g):
    return g.reshape((N_CHIPS, -1, g.shape[-1]))


def _place_scalars():
    return jnp.stack([lax.axis_index("c"), 2 * lax.axis_index("x") + lax.axis_index("y")]).astype(jnp.int32)


def _scalar_call(body, name, grid, in_specs, out_specs, out_shape, operands):
    spec = pltpu.PrefetchScalarGridSpec(num_scalar_prefetch=1, grid=grid, in_specs=in_specs, out_specs=out_specs)
    return pl.pallas_call(body, name=name, grid_spec=spec, out_shape=out_shape,
                          compiler_params=_params(*(["arbitrary"] * len(grid))))(_place_scalars(), *operands)


def _pair_sums(names, parts, sib):
    out = []
    for n, g, s in zip(names, parts, sib):
        rh, cols = s.shape[1], s.shape[2]
        tm = _row_tile(rh, ROW_TILE)
        nblk = rh // tm

        def body(place, g_ref, s_ref, o_ref):
            o_ref[...] = (g_ref[...].astype(F32) + s_ref[...].astype(F32)).astype(o_ref.dtype)

        blk = pl.BlockSpec((None, tm, cols), lambda j, i, place: (j, i, 0))
        own = pl.BlockSpec((None, tm, cols), lambda j, i, place, nblk=nblk: (j, place[0] * nblk + i, 0))
        out.append(_scalar_call(body, "grad_pair_sum_" + n, (N_CHIPS, nblk), [own, blk], blk,
                                jax.ShapeDtypeStruct(s.shape, BF16), (g, s)))
    return out


def _chip_sums(names, pair, others):
    out = []
    for n, h, o in zip(names, pair, others):
        rh, cols = h.shape[1], h.shape[2]
        tm = _row_tile(rh, ROW_TILE)

        def body(place, h_ref, a_ref, b_ref, c_ref, o_ref):
            o_ref[...] = (h_ref[...].astype(F32) + a_ref[...].astype(F32)) + b_ref[...].astype(F32) + c_ref[...].astype(F32)

        mine = pl.BlockSpec((None, tm, cols), lambda i, place: (place[1], i, 0))
        other = lambda k: pl.BlockSpec((None, tm, cols), lambda i, place, k=k: (k, i, 0))
        out.append(_scalar_call(body, "grad_chip_sum_" + n, (rh // tm,), [mine, other(0), other(1), other(2)],
                                pl.BlockSpec((tm, cols), lambda i, place: (i, 0)), jax.ShapeDtypeStruct((rh, cols), F32),
                                (h, o, o, o)))
    return out


def _adamw_halves(name, w, m, v, own, sib):
    rh, cols = own.shape
    tm = _row_tile(rh, ADAMW_TILE)
    nblk = rh // tm

    def body(place, w_ref, m_ref, v_ref, own_ref, sib_ref, g_ref, d_ref, m2_ref, v2_ref):
        mine = pl.program_id(0) // nblk == place[0]

        def run(gv):
            g_ref[...] = gv
            d_ref[...], m2_ref[...], v2_ref[...] = _adamw_math(w_ref[...], gv, m_ref[...], v_ref[...])

        @pl.when(mine)
        def _():
            run(own_ref[...])

        @pl.when(jnp.logical_not(mine))
        def _():
            run(sib_ref[...])

    full = pl.BlockSpec((tm, cols), lambda i, place: (i, 0))
    own_spec = pl.BlockSpec((tm, cols), lambda i, place: (jnp.where(i // nblk == place[0], i % nblk, 0), 0))
    sib_spec = pl.BlockSpec((tm, cols), lambda i, place: (jnp.where(i // nblk == place[0], 0, i % nblk), 0))
    return _scalar_call(body, name, (2 * nblk,), [full, full, full, own_spec, sib_spec], [full] * 4,
                        [jax.ShapeDtypeStruct((2 * rh, cols), F32)] * 4, (w, m, v, own, sib))


def _gather_all_rider(v):
    m_per = v.shape[0]

    def rows(ref, px, py, pc):
        return ref.at[pl.ds(pl.multiple_of((4 * px + 2 * py + pc) * m_per, 8), m_per)]

    def first(ins, outs, send_sems, recv_sems):
        x, y, c, chips = _place()
        mine = rows(outs[0], x, y, c)
        return [_remote(ins[0], mine, send_sems, recv_sems, 0, (x, y, 1 - c))] + [
            _remote(ins[0], mine, send_sems, recv_sems, 1 + j, (cx, cy, c)) for j, (cx, cy) in enumerate(chips)]

    def start(ins, outs, send_sems, recv_sems):
        for cp in first(ins, outs, send_sems, recv_sems):
            cp.start()

    def finish(ins, outs, send_sems, recv_sems):
        x, y, c, chips = _place()
        passed = []
        for j, (cx, cy) in enumerate(chips):
            blk = rows(outs[0], cx, cy, c)
            _remote(blk, blk, send_sems, recv_sems, 1 + j, (x, y, c)).wait_recv()
            passed.append(_remote(blk, blk, send_sems, recv_sems, 4 + j, (x, y, 1 - c)))
            passed[j].start()
        sib = rows(outs[0], x, y, 1 - c)
        _remote(sib, sib, send_sems, recv_sems, 0, (x, y, c)).wait_recv()
        for j, (cx, cy) in enumerate(chips):
            blk = rows(outs[0], cx, cy, 1 - c)
            _remote(blk, blk, send_sems, recv_sems, 4 + j, (x, y, c)).wait_recv()
        for cp in first(ins, outs, send_sems, recv_sems) + passed:
            cp.wait_send()

    return _Rider([v], [jax.ShapeDtypeStruct((N_DEV * m_per,) + v.shape[1:], v.dtype)], 7, start, finish)


def _sum_over_devices(name, v, gathered):
    m_per = v.shape[0]
    dev = 4 * lax.axis_index("x") + 2 * lax.axis_index("y") + lax.axis_index("c")
    full = lax.dynamic_update_slice_in_dim(gathered, v, dev * m_per, axis=0)
    return _sum_blocks(name, [full[i * m_per:(i + 1) * m_per] for i in range(N_DEV)], F32)


def _sum_blocks(name, parts, out_dtype):
    rows, cols = parts[0].shape
    tm = _row_tile(rows, ROW_TILE)

    def body(*refs):
        acc = refs[0][...].astype(F32)
        for r in refs[1:-1]:
            acc = acc + r[...].astype(F32)
        refs[-1][...] = acc.astype(refs[-1].dtype)

    spec = pl.BlockSpec((tm, cols), lambda i: (i, 0))
    return pl.pallas_call(
        body, name=name, grid=(rows // tm,), in_specs=[spec] * len(parts), out_specs=spec,
        out_shape=jax.ShapeDtypeStruct((rows, cols), out_dtype), compiler_params=_params("arbitrary"),
    )(*parts)


def _adamw_math(wv, gv, mv, vv):
    m2 = ADAM_B1 * mv + (1.0 - ADAM_B1) * gv
    v2 = ADAM_B2 * vv + (1.0 - ADAM_B2) * (gv * gv)
    delta = -ADAM_LR * ((m2 / (1.0 - ADAM_B1 ** ADAM_STEP)) / (jnp.sqrt(v2 / (1.0 - ADAM_B2 ** ADAM_STEP)) + ADAM_EPS)
                        + ADAM_WD * wv)
    return delta, m2, v2


def _adamw_small(ws, gs, ms, vs):
    n = len(ws)

    def body(*refs):
        for i in range(n):
            res = _adamw_math(refs[i][...], refs[n + i][...], refs[2 * n + i][...], refs[3 * n + i][...])
            for k in range(3):
                refs[(4 + k) * n + i][...] = res[k]

    vm = pl.BlockSpec(memory_space=pltpu.VMEM)
    outs = pl.pallas_call(
        body, name="adamw_small", in_specs=[vm] * (4 * n), out_specs=[vm] * (3 * n),
        out_shape=[jax.ShapeDtypeStruct(a.shape, F32) for a in ws] * 3,
        compiler_params=pltpu.CompilerParams(vmem_limit_bytes=VMEM_LIMIT_BYTES),
    )(*ws, *gs, *ms, *vs)
    return outs[:n], outs[n:2 * n], outs[2 * n:]


PACK_ROWS = 256


def _pack(flat_parts, dtype, lead=()):
    parts = [a.astype(dtype).reshape(lead + (-1,)) for a in flat_parts]
    n = sum(a.shape[-1] for a in parts)
    chunk = PACK_ROWS * LANES
    total = -(-n // chunk) * chunk
    if total > n:
        parts.append(jnp.zeros(lead + (total - n,), dtype))
    return jnp.concatenate(parts, axis=-1).reshape(lead + (total // LANES, LANES))


def _unpack(buf, shapes, lead=()):
    flat = buf.reshape(lead + (-1,))
    out, off = [], 0
    for shp in shapes:
        n = math.prod(shp)
        out.append(lax.slice_in_dim(flat, off, off + n, axis=len(lead)).reshape(lead + tuple(shp)))
        off += n
    return out


BIG = ("w_in", "w_glu", "w_pa", "w_pb", "w_out", "w_up", "w_down")
WEIGHTS = ("g_mix", "w_in", "s5_a_re", "s5_a_im", "s5_log_dt", "s5_b_re", "s5_b_im", "s5_c_re", "s5_c_im", "s5_d",
           "w_glu", "b_glu", "hg_lb_logits", "hg_norm_gain", "w_pa", "w_pb", "w_out", "g_ffn", "w_up", "w_conv",
           "b_conv", "w_down", "g_final")
SMALL = tuple(n for n in WEIGHTS if n not in BIG)
SMALL_PARTS = ("loss", "g_ffn", "g_final", "b_glu", "gain", "lbrow", "s5_d", "w_conv", "b_conv", "lam_re", "lam_im",
               "bb_re", "bb_im", "s5_c_re", "s5_c_im")


def _lower_bound(logits):
    return jnp.cumsum(jax.nn.softmax(logits, axis=0), axis=0)[0:1]


def kernel(x, g_mix, w_in, s5_a_re, s5_a_im, s5_log_dt, s5_b_re, s5_b_im, s5_c_re, s5_c_im, s5_d, w_glu, b_glu, hg_lb_logits, hg_norm_gain, w_pa, w_pb, w_out, g_ffn, w_up, w_conv, b_conv, w_down, g_final, loss_target, m_g_mix, m_w_in, m_s5_a_re, m_s5_a_im, m_s5_log_dt, m_s5_b_re, m_s5_b_im, m_s5_c_re, m_s5_c_im, m_s5_d, m_w_glu, m_b_glu, m_hg_lb_logits, m_hg_norm_gain, m_w_pa, m_w_pb, m_w_out, m_g_ffn, m_w_up, m_w_conv, m_b_conv, m_w_down, m_g_final, v_g_mix, v_w_in, v_s5_a_re, v_s5_a_im, v_s5_log_dt, v_s5_b_re, v_s5_b_im, v_s5_c_re, v_s5_c_im, v_s5_d, v_w_glu, v_b_glu, v_hg_lb_logits, v_hg_norm_gain, v_w_pa, v_w_pb, v_w_out, v_g_ffn, v_w_up, v_w_conv, v_b_conv, v_w_down, v_g_final):
    args = dict(locals())
    w = {n: args[n] for n in WEIGHTS}
    mom = {n: args["m_" + n] for n in WEIGHTS}
    var = {n: args["v_" + n] for n in WEIGHTS}
    nseq, seq, d = x.shape
    xi, yi = lax.axis_index("x"), lax.axis_index("y")
    chip = 2 * xi + yi

    shard = {n: w[n][0] for n in BIG}
    first = [shard["w_in"].astype(BF16), shard["w_glu"].astype(BF16),
             jnp.pad(w_conv[0], ((0, 2 * SUBLANES - CONV_W), (0, 0)))]
    x2 = x.reshape(nseq * seq, d)
    u, late16, got = _prepare(x2, g_mix, [shard[n] for n in LATE], _gather_full_rider(first))
    w_in_all, w_glu_all, conv_all = [lax.dynamic_update_index_in_dim(g, s, chip, 0) for g, s in zip(got, first)]
    p = dict(g_mix=g_mix, g_ffn=g_ffn, g_final=g_final.reshape(1, -1), b_glu=b_glu, gain=hg_norm_gain, s5_d=s5_d,
             b_conv=b_conv, lbrow=_lower_bound(hg_lb_logits),
             s5_a_re=s5_a_re[0], s5_a_im=s5_a_im[0], s5_log_dt=s5_log_dt[0], s5_b_re=s5_b_re[0], s5_b_im=s5_b_im[0],
             s5_c_re=s5_c_re[0], s5_c_im=s5_c_im[0], w_in=w_in_all, w_glu=w_glu_all.reshape(-1, w_glu_all.shape[-1]),
             w_conv=conv_all[:, :CONV_W].transpose(1, 0, 2).reshape(CONV_W, -1))

    dx, halves, sm = _local_step(x2, loss_target.reshape(nseq * seq, d), u, p, dict(zip(LATE, late16)), nseq=nseq, seq=seq)
    loss = sm["loss"][0, 0]

    grads, delta, new_m, new_v = {}, {}, {}, {}
    for n in BIG:
        shp = shard[n].shape
        grads[n], delta[n], new_m[n], new_v[n] = _adamw_halves("adamw_" + n, shard[n], mom[n].reshape(shp),
                                                               var[n].reshape(shp), *halves[n])

    _, disc_vjp = jax.vjp(_s5_discretize, p["s5_a_re"], p["s5_a_im"], p["s5_log_dt"], p["s5_b_re"], p["s5_b_im"])
    da_re, da_im, dlog_dt, db_re, db_im = disc_vjp((sm["lam_re"], sm["lam_im"], sm["bb_re"], sm["bb_im"]))
    _, lb_vjp = jax.vjp(_lower_bound, hg_lb_logits)
    (dlogits,) = lb_vjp(sm["lbrow"])
    fcols = w_conv.shape[-1]
    grads.update(
        g_mix=sm["g_mix"], g_ffn=sm["g_ffn"], g_final=sm["g_final"].reshape(-1), b_glu=sm["b_glu"],
        hg_norm_gain=sm["gain"], hg_lb_logits=dlogits, s5_d=sm["s5_d"], b_conv=sm["b_conv"],
        w_conv=lax.dynamic_slice_in_dim(sm["w_conv"], chip * fcols, fcols, axis=1),
        s5_a_re=da_re, s5_a_im=da_im, s5_log_dt=dlog_dt, s5_b_re=db_re, s5_b_im=db_im,
        s5_c_re=sm["s5_c_re"], s5_c_im=sm["s5_c_im"])
    grads = {n: grads[n].reshape(w[n].shape) for n in WEIGHTS}

    def natural(a):
        return a.reshape(1, -1) if a.ndim == 1 else (a[0] if a.ndim > 2 else a)

    outs = _adamw_small(*[[natural(src[n]) for n in SMALL] for src in (w, grads, mom, var)])
    for dst, group in zip((delta, new_m, new_v), outs):
        dst.update(zip(SMALL, group))
    res = [loss, dx.reshape(x.shape)]
    for group in (grads, delta, new_m, new_v):
        res += [group[n].reshape(w[n].shape) for n in WEIGHTS]
    return tuple(res)
```

```python
import functools
import math

import jax
import jax.numpy as jnp
from jax import lax
from jax.experimental import pallas as pl
from jax.experimental.pallas import tpu as pltpu

F32 = jnp.float32
BF16 = jnp.bfloat16
MESH = pl.DeviceIdType.MESH

EPS = 1e-6
S5_GROUP = 16
S5_STATE = 64
S5_BLOCK_GROUPS = 8
HEAD = 128
CHUNK = 64
CONV_W = 3
LANES = 128
SUBLANES = 8
GATE_BLOCK = 512
VMEM_LIMIT_BYTES = 56 * 1024 * 1024

ADAM_LR = 0.001
ADAM_B1 = 0.9
ADAM_B2 = 0.999
ADAM_EPS = 1e-08
ADAM_WD = 0.01
ADAM_STEP = 10

N_CHIPS = 4
N_DEV = 8


def _params(*sem):
    return pltpu.CompilerParams(dimension_semantics=sem, vmem_limit_bytes=VMEM_LIMIT_BYTES)


class _Rider:
    def __init__(self, arrays, out_shapes, nsem, start, finish, aliases=None):
        self.arrays, self.out_shapes, self.nsem = list(arrays), list(out_shapes), nsem
        self.start, self.finish, self.aliases = start, finish, dict(aliases or {})


def _hosted_call(name, body, *, grid, in_specs, out_specs, out_shape, operands, scratch_shapes=(), rider=None):
    in_specs, out_specs, out_shape, scratch_shapes = list(in_specs), list(out_specs), list(out_shape), list(scratch_shapes)
    cparams = _params(*(["arbitrary"] * len(grid)))
    if rider is None:
        return pl.pallas_call(body, name=name, grid=grid, in_specs=in_specs, out_specs=out_specs, out_shape=out_shape,
                              scratch_shapes=scratch_shapes, compiler_params=cparams)(*operands)
    n_in, n_out, n_sc = len(in_specs), len(out_specs), len(scratch_shapes)
    r_in, r_out = len(rider.arrays), len(rider.out_shapes)

    def hosted(*refs):
        ins, rins = refs[:n_in], refs[n_in:n_in + r_in]
        outs = refs[n_in + r_in:n_in + r_in + n_out]
        routs = refs[n_in + r_in + n_out:n_in + r_in + n_out + r_out]
        rest = refs[n_in + r_in + n_out + r_out:]
        send_sems, recv_sems = rest[n_sc], rest[n_sc + 1]
        first = functools.reduce(jnp.logical_and, [pl.program_id(i) == 0 for i in range(len(grid))])
        last = functools.reduce(jnp.logical_and, [pl.program_id(i) == grid[i] - 1 for i in range(len(grid))])

        @pl.when(first)
        def _():
            rider.start(rins, routs, send_sems, recv_sems)

        body(*ins, *outs, *rest[:n_sc])

        @pl.when(last)
        def _():
            rider.finish(rins, routs, send_sems, recv_sems)

    res = pl.pallas_call(
        hosted, name=name, grid=grid, in_specs=in_specs + [ANY] * r_in, out_specs=out_specs + [ANY] * r_out,
        out_shape=out_shape + rider.out_shapes,
        scratch_shapes=scratch_shapes + [pltpu.SemaphoreType.DMA((rider.nsem,)), pltpu.SemaphoreType.DMA((rider.nsem,))],
        input_output_aliases={n_in + i: n_out + o for i, o in rider.aliases.items()}, compiler_params=cparams,
    )(*operands, *rider.arrays)
    return res[:n_out], res[n_out:]


def _hosted_scalar_call(name, body, *, grid, in_specs, out_specs, out_shape, operands, rider, aliases=None):
    in_specs, out_specs, out_shape = list(in_specs), list(out_specs), list(out_shape)
    n_in, n_out = len(in_specs), len(out_specs)
    r_in, r_out = len(rider.arrays), len(rider.out_shapes)

    def hosted(place, *refs):
        ins, rins = refs[:n_in], refs[n_in:n_in + r_in]
        outs = refs[n_in + r_in:n_in + r_in + n_out]
        routs = refs[n_in + r_in + n_out:n_in + r_in + n_out + r_out]
        send_sems, recv_sems = refs[-2], refs[-1]
        first = functools.reduce(jnp.logical_and, [pl.program_id(i) == 0 for i in range(len(grid))])
        last = functools.reduce(jnp.logical_and, [pl.program_id(i) == grid[i] - 1 for i in range(len(grid))])

        @pl.when(first)
        def _():
            rider.start(rins, routs, send_sems, recv_sems)

        body(place, *ins, *outs)

        @pl.when(last)
        def _():
            rider.finish(rins, routs, send_sems, recv_sems)

    spec = pltpu.PrefetchScalarGridSpec(
        num_scalar_prefetch=1, grid=grid, in_specs=in_specs + [ANY] * r_in, out_specs=out_specs + [ANY] * r_out,
        scratch_shapes=[pltpu.SemaphoreType.DMA((rider.nsem,)), pltpu.SemaphoreType.DMA((rider.nsem,))])
    alias = {1 + i: o for i, o in (aliases or {}).items()}
    alias.update({1 + n_in + i: n_out + o for i, o in rider.aliases.items()})
    res = pl.pallas_call(hosted, name=name, grid_spec=spec, out_shape=out_shape + rider.out_shapes,
                         input_output_aliases=alias, compiler_params=_params(*(["arbitrary"] * len(grid))),
                         )(_place_scalars(), *operands, *rider.arrays)
    return res[:n_out], res[n_out:]


def _run_rider(name, rider):
    r_in, r_out = len(rider.arrays), len(rider.out_shapes)

    def body(*refs):
        rins, routs, send_sems, recv_sems = refs[:r_in], refs[r_in:r_in + r_out], refs[-2], refs[-1]
        rider.start(rins, routs, send_sems, recv_sems)
        rider.finish(rins, routs, send_sems, recv_sems)

    return pl.pallas_call(
        body, name=name, in_specs=[ANY] * r_in, out_specs=[ANY] * r_out, out_shape=rider.out_shapes,
        scratch_shapes=[pltpu.SemaphoreType.DMA((rider.nsem,)), pltpu.SemaphoreType.DMA((rider.nsem,))],
        input_output_aliases=rider.aliases,
    )(*rider.arrays)


def _row_tile(rows, cap):
    if rows <= cap:
        return rows
    for t in range(cap - cap % 8, 7, -8):
        if rows % t == 0:
            return t
    raise ValueError(f"no row tile for {rows}")


def _dot(a, b):
    return jnp.dot(a.astype(BF16), b.astype(BF16), preferred_element_type=F32)


def _dot_nt(a, b):
    return lax.dot_general(a.astype(BF16), b.astype(BF16), (((1,), (1,)), ((), ())), preferred_element_type=F32)


def _dot_tn(a, b):
    return lax.dot_general(a.astype(BF16), b.astype(BF16), (((0,), (0,)), ((), ())), preferred_element_type=F32)


def _sigmoid(x):
    return 0.5 * jnp.tanh(0.5 * x) + 0.5


_GELU_C = math.sqrt(2.0 / math.pi)


def _gelu(x):
    return 0.5 * x * (1.0 + jnp.tanh(_GELU_C * (x + 0.044715 * x * x * x)))


def _gelu_grad(x):
    th = jnp.tanh(_GELU_C * (x + 0.044715 * x * x * x))
    return 0.5 * (1.0 + th) + 0.5 * x * (1.0 - th * th) * _GELU_C * (1.0 + 3.0 * 0.044715 * x * x)


def _rowwise(name, fn, ins, outs, accs=(), *, rows, tm, ncol=1, rider=None):
    n_in, n_out = len(ins), len(outs)

    def body(*refs):
        res = fn(*[r[...] for r in refs[:n_in]])
        for r, v in zip(refs[n_in:n_in + n_out], res[:n_out]):
            r[...] = v.astype(r.dtype)
        first = pl.program_id(1) == 0
        for r, v in zip(refs[n_in + n_out:], res[n_out:]):
            @pl.when(first)
            def _():
                r[...] = v

            @pl.when(jnp.logical_not(first))
            def _():
                r[...] += v

    in_specs = []
    for _, width, base, kind in ins:
        if kind == "row":
            in_specs.append(pl.BlockSpec((tm, width), lambda j, i, b=base: (i, b + j)))
        else:
            in_specs.append(pl.BlockSpec((1, width), lambda j, i, b=base: (0, b + j)))
    out_specs = [pl.BlockSpec((tm, width), lambda j, i: (i, j)) for _, width, _ in outs]
    out_specs += [pl.BlockSpec((1, width), lambda j, i: (0, j)) for _, width in accs]
    out_shape = [jax.ShapeDtypeStruct((rows, total), dt) for total, _, dt in outs]
    out_shape += [jax.ShapeDtypeStruct((1, total), F32) for total, _ in accs]
    return _hosted_call(name, body, grid=(ncol, rows // tm), in_specs=in_specs, out_specs=out_specs, out_shape=out_shape,
                        operands=[a for a, _, _, _ in ins], rider=rider)


def _mm(name, a, b, *, mode, grid, a_spec, b_spec, o_spec, out_shape, acc_shape, res=None, res_spec=None,
        pair_axis=None, rider=None, epilogue=None):
    nk = grid[2]
    dot = {"nn": _dot, "nt": _dot_nt, "tn": _dot_tn}[mode]
    a_list = list(a) if isinstance(a, tuple) else [a]
    b_list = list(b) if isinstance(b, tuple) else [b]
    na, nb = len(a_list), len(b_list)
    assert (pair_axis is None) == (na + nb == 2)
    direct = nk == 1 and pair_axis is None
    epi_fn, epi_ins, epi_sums = epilogue if epilogue is not None else (None, [], [])
    n_res = 0 if res is None else 1
    n_epi = len(epi_ins)

    def body(*refs):
        a_refs, b_refs = refs[:na], refs[na:na + nb]
        r_ref = None if res is None else refs[na + nb]
        e_refs = refs[na + nb + n_res:na + nb + n_res + n_epi]
        o_ref = refs[na + nb + n_res + n_epi]
        s_refs = refs[na + nb + n_res + n_epi + 1:na + nb + n_res + n_epi + 1 + len(epi_sums)]
        first_rows = pl.program_id(0) == 0

        def finish(v):
            if res is not None:
                v = v + r_ref[...]
            if epi_fn is None:
                o_ref[...] = v.astype(o_ref.dtype)
                return
            outs = epi_fn(v, *[r[...] for r in e_refs])
            o_ref[...] = outs[0].astype(o_ref.dtype)
            for s_ref, part in zip(s_refs, outs[1:]):
                @pl.when(first_rows)
                def _():
                    s_ref[...] = part

                @pl.when(jnp.logical_not(first_rows))
                def _():
                    s_ref[...] += part

        if direct:
            finish(dot(a_refs[0][...], b_refs[0][...]))
            return
        acc_ref = refs[-1]
        k = pl.program_id(2)

        @pl.when(k == 0)
        def _():
            acc_ref[...] = jnp.zeros_like(acc_ref)

        if pair_axis is None:
            acc_ref[...] += dot(a_refs[0][...], b_refs[0][...])
        else:
            lower = pl.program_id(pair_axis) < grid[pair_axis] // 2

            @pl.when(lower)
            def _():
                acc_ref[...] += dot(a_refs[0][...], b_refs[0][...])

            @pl.when(jnp.logical_not(lower))
            def _():
                acc_ref[...] += dot(a_refs[-1][...], b_refs[-1][...])

        @pl.when(k == nk - 1)
        def _():
            finish(acc_ref[...])

    operands = a_list + b_list + ([] if res is None else [res]) + [arr for arr, _ in epi_ins]
    in_specs = (list(a_spec) if na == 2 else [a_spec]) + (list(b_spec) if nb == 2 else [b_spec])
    in_specs += ([] if res is None else [res_spec]) + [spec for _, spec in epi_ins]
    out_specs = [o_spec] + [pl.BlockSpec((1, c), lambda *_: (0, 0)) for c in epi_sums]
    out_shapes = [out_shape] + [jax.ShapeDtypeStruct((1, c), F32) for c in epi_sums]
    got = _hosted_call(name, body, grid=grid, in_specs=in_specs, out_specs=out_specs, out_shape=out_shapes,
                       scratch_shapes=[] if direct else [pltpu.VMEM(acc_shape, F32)], operands=operands, rider=rider)
    mine, rider_outs = (got, None) if rider is None else got
    mine = mine[0] if epilogue is None else tuple(mine)
    return mine if rider is None else (mine, rider_outs)


MM_TILE_BUDGET_BYTES = 36 * 1024 * 1024
MM_TILE_CAP = 2048
ROW_TILE = 1024
GLU_TILE = 1024
ADAMW_TILE = 256


def _mm_tile(t, row_bytes, fixed_bytes):
    cap = max(16, min(MM_TILE_CAP, (MM_TILE_BUDGET_BYTES - fixed_bytes) // row_bytes))
    return _row_tile(t, cap - cap % 16)


def _size(a):
    return jnp.dtype(a.dtype).itemsize


def _mm_fwd_cols(name, a, w3, out_dtype=F32, rider=None):
    t, k = a.shape
    ns = w3.shape[2]
    tm = _mm_tile(t, 2 * k * _size(a) + 2 * ns * jnp.dtype(out_dtype).itemsize, 2 * k * ns * _size(w3))
    return _mm(name, a, w3, mode="nn", grid=(N_CHIPS, t // tm, 1),
               a_spec=pl.BlockSpec((tm, k), lambda j, i, kk: (i, 0)),
               b_spec=pl.BlockSpec((None, k, ns), lambda j, i, kk: (j, 0, 0)),
               o_spec=pl.BlockSpec((tm, ns), lambda j, i, kk: (i, j)),
               out_shape=jax.ShapeDtypeStruct((t, N_CHIPS * ns), out_dtype), acc_shape=(tm, ns), rider=rider)


def _mm_bwd_cols(name, d, w3, out_dtype=F32, rider=None, epilogue=None):
    pair = isinstance(d, tuple)
    t = d[0].shape[0] if pair else d.shape[0]
    k, ns = w3.shape[1], w3.shape[2]
    dsize = _size(d[0] if pair else d)
    tm = _mm_tile(t, (4 if pair else 2) * ns * dsize + 2 * k * jnp.dtype(out_dtype).itemsize + 4 * k
                  + _row_epilogue(epilogue, 8)[1], 2 * k * ns * _size(w3))
    half = N_CHIPS // 2
    if pair:
        a_spec = (pl.BlockSpec((tm, ns), lambda i, j, kk: (i, jnp.minimum(kk, half - 1))),
                  pl.BlockSpec((tm, ns), lambda i, j, kk: (i, jnp.maximum(kk - half, 0))))
    else:
        a_spec = pl.BlockSpec((tm, ns), lambda i, j, kk: (i, kk))
    return _mm(name, d, w3, mode="nt", grid=(t // tm, 1, N_CHIPS), a_spec=a_spec,
               b_spec=pl.BlockSpec((None, k, ns), lambda i, j, kk: (kk, 0, 0)),
               o_spec=pl.BlockSpec((tm, k), lambda i, j, kk: (i, 0)),
               out_shape=jax.ShapeDtypeStruct((t, k), out_dtype), acc_shape=(tm, k), pair_axis=2 if pair else None,
               rider=rider, epilogue=_row_epilogue(epilogue, tm)[0])


def _mm_wgrad_cols(name, a, d, rider=None):
    pair = isinstance(d, tuple)
    t, k = a.shape
    ns = (2 * d[0].shape[1] if pair else d.shape[1]) // N_CHIPS
    dsize = _size(d[0] if pair else d)
    tk = _mm_tile(t, 2 * k * _size(a) + (4 if pair else 2) * ns * dsize, k * ns * (4 + 2 * 2))
    half = N_CHIPS // 2
    if pair:
        b_spec = (pl.BlockSpec((tk, ns), lambda j, i, kk: (jnp.where(j < half, kk, 0), jnp.minimum(j, half - 1))),
                  pl.BlockSpec((tk, ns), lambda j, i, kk: (jnp.where(j < half, 0, kk), jnp.maximum(j - half, 0))))
    else:
        b_spec = pl.BlockSpec((tk, ns), lambda j, i, kk: (kk, j))
    return _mm(name, a, d, mode="tn", grid=(N_CHIPS, 1, t // tk),
               a_spec=pl.BlockSpec((tk, k), lambda j, i, kk: (kk, 0)), b_spec=b_spec,
               o_spec=pl.BlockSpec((None, k, ns), lambda j, i, kk: (j, 0, 0)),
               out_shape=jax.ShapeDtypeStruct((N_CHIPS, k, ns), BF16), acc_shape=(k, ns),
               pair_axis=0 if pair else None, rider=rider)


MM_BLOCK_CAP = 1408


def _row_epilogue(epilogue, tm):
    if epilogue is None:
        return None, 0
    fn, arrays, sums = epilogue
    specs = [pl.BlockSpec((1, x.shape[1]), lambda i, j, kk: (0, 0)) if x.shape[0] == 1 else
             pl.BlockSpec((tm, x.shape[1]), lambda i, j, kk: (i, 0)) for x in arrays]
    return (fn, list(zip(arrays, specs)), list(sums)), sum(2 * x.shape[1] * _size(x) for x in arrays if x.shape[0] > 1)


def _mm_fwd_rows(name, a, w, res=None, out_dtype=F32, epilogue=None):
    t, k = a.shape
    n = w.shape[1]
    tk = k if k <= MM_BLOCK_CAP else MM_BLOCK_CAP
    assert k % tk == 0
    row_bytes = 2 * tk * _size(a) + 2 * n * jnp.dtype(out_dtype).itemsize + (0 if res is None else 2 * n * 4) + 4 * n
    row_bytes += _row_epilogue(epilogue, 8)[1]
    tm = _mm_tile(t, row_bytes, 2 * tk * n * _size(w))
    return _mm(name, a, w, mode="nn", grid=(t // tm, 1, k // tk),
               a_spec=pl.BlockSpec((tm, tk), lambda i, j, kk: (i, kk)),
               b_spec=pl.BlockSpec((tk, n), lambda i, j, kk: (kk, 0)),
               o_spec=pl.BlockSpec((tm, n), lambda i, j, kk: (i, 0)),
               out_shape=jax.ShapeDtypeStruct((t, n), out_dtype), acc_shape=(tm, n),
               res=res, res_spec=None if res is None else pl.BlockSpec((tm, n), lambda i, j, kk: (i, 0)),
               epilogue=_row_epilogue(epilogue, tm)[0])


def _mm_bwd_rows(name, d, w, out_dtype=F32):
    t, n = d.shape
    k = w.shape[0]
    tn = k if k <= MM_BLOCK_CAP else MM_BLOCK_CAP
    assert k % tn == 0
    tm = _mm_tile(t, 2 * n * _size(d) + 2 * tn * jnp.dtype(out_dtype).itemsize, 2 * tn * n * _size(w))
    return _mm(name, d, w, mode="nt", grid=(t // tm, k // tn, 1),
               a_spec=pl.BlockSpec((tm, n), lambda i, j, kk: (i, 0)),
               b_spec=pl.BlockSpec((tn, n), lambda i, j, kk: (j, 0)),
               o_spec=pl.BlockSpec((tm, tn), lambda i, j, kk: (i, j)),
               out_shape=jax.ShapeDtypeStruct((t, k), out_dtype), acc_shape=(tm, tn))


def _mm_wgrad_rows(name, a, d):
    t, k = a.shape
    n = d.shape[1]
    nblk = next(b for b in (1, 2, 4) if (k // b) % LANES == 0 and k // b <= MM_BLOCK_CAP)
    ks = k // nblk
    tk = _mm_tile(t, 2 * ks * _size(a) + 2 * n * _size(d), ks * n * (4 + 2 * 2))
    return _mm(name, a, d, mode="tn", grid=(nblk, 1, t // tk),
               a_spec=pl.BlockSpec((tk, ks), lambda j, i, kk: (kk, j)),
               b_spec=pl.BlockSpec((tk, n), lambda j, i, kk: (kk, 0)),
               o_spec=pl.BlockSpec((ks, n), lambda j, i, kk: (j, 0)),
               out_shape=jax.ShapeDtypeStruct((k, n), BF16), acc_shape=(ks, n))


def _s5_discretize(a_re, a_im, log_dt, b_re, b_im):
    dt = jnp.exp(log_dt)[:, None]
    mag = jnp.exp(a_re * dt)
    ang = a_im * dt
    lb_re = mag * jnp.cos(ang)
    lb_im = mag * jnp.sin(ang)
    den = a_re * a_re + a_im * a_im
    n_re = lb_re - 1.0
    n_im = lb_im
    co_re = ((n_re * a_re + n_im * a_im) / den)[..., None]
    co_im = ((n_im * a_re - n_re * a_im) / den)[..., None]
    bb_re = co_re * b_re - co_im * b_im
    bb_im = co_re * b_im + co_im * b_re
    return lb_re, lb_im, bb_re, bb_im


def _s5_in_blocks(bb):
    g = bb.shape[0]
    nb = g // S5_BLOCK_GROUPS
    t = bb.reshape(nb, S5_BLOCK_GROUPS, S5_STATE, S5_GROUP).transpose(0, 1, 3, 2)
    eye = jnp.eye(S5_BLOCK_GROUPS, dtype=bb.dtype)
    full = t[:, :, :, None, :] * eye[None, :, None, :, None]
    return full.reshape(nb, S5_BLOCK_GROUPS * S5_GROUP, S5_BLOCK_GROUPS * S5_STATE)


def _s5_in_blocks_diag(blocks):
    nb = blocks.shape[0]
    t = blocks.reshape(nb, S5_BLOCK_GROUPS, S5_GROUP, S5_BLOCK_GROUPS, S5_STATE)
    d = jnp.einsum("bghgp->bghp", t)
    return d.transpose(0, 1, 3, 2).reshape(nb * S5_BLOCK_GROUPS, S5_STATE, S5_GROUP)


def _s5_out_blocks(c):
    g = c.shape[0]
    nb = g // S5_BLOCK_GROUPS
    t = c.reshape(nb, S5_BLOCK_GROUPS, S5_GROUP, S5_STATE).transpose(0, 1, 3, 2)
    eye = jnp.eye(S5_BLOCK_GROUPS, dtype=c.dtype)
    full = t[:, :, :, None, :] * eye[None, :, None, :, None]
    return full.reshape(nb, S5_BLOCK_GROUPS * S5_STATE, S5_BLOCK_GROUPS * S5_GROUP)


def _s5_out_blocks_diag(blocks):
    nb = blocks.shape[0]
    t = blocks.reshape(nb, S5_BLOCK_GROUPS, S5_STATE, S5_BLOCK_GROUPS, S5_GROUP)
    d = jnp.einsum("bgpgh->bgph", t)
    return d.transpose(0, 1, 3, 2).reshape(nb * S5_BLOCK_GROUPS, S5_GROUP, S5_STATE)


def _s5_scan_tables(lr, li, reverse):
    def cmul(a, b):
        return a[0] * b[0] - a[1] * b[1], a[0] * b[1] + a[1] * b[0]

    lam = (lr, -li) if reverse else (lr, li)
    pw = [lam]
    for _ in range(SUBLANES - 1):
        pw.append(cmul(pw[-1], lam))
    sub = jnp.arange(SUBLANES)[:, None]
    rows = []
    for s in (1, 2, 4):
        keep = (sub <= SUBLANES - 1 - s) if reverse else (sub >= s)
        rows.append(jnp.where(keep, pw[s - 1][0][None, :], 0.0))
        rows.append(jnp.where(keep, pw[s - 1][1][None, :], 0.0))
    order = list(range(SUBLANES - 1, -1, -1)) if reverse else list(range(SUBLANES))
    rows.append(jnp.stack([pw[i][0] for i in order]))
    rows.append(jnp.stack([pw[i][1] for i in order]))
    return jnp.concatenate(rows, axis=0)


def _s5_scan(vre_ref, vim_ref, coef_ref, seq, width, reverse, xre_ref=None, xim_ref=None):
    nt = seq // SUBLANES
    nl = width // LANES
    per = 2 if xre_ref is None else 4
    sub = lax.broadcasted_iota(jnp.int32, (SUBLANES, LANES), 0)

    def step(k, carry):
        kk = (nt - 1 - k) if reverse else k
        rows = pl.ds(pl.multiple_of(kk * SUBLANES, SUBLANES), SUBLANES)
        out = []
        for j in range(nl):
            lanes = slice(j * LANES, (j + 1) * LANES)
            co = [coef_ref[SUBLANES * q:SUBLANES * (q + 1), lanes] for q in range(8)]
            cr, ci = carry[per * j], carry[per * j + 1]
            vr = vre_ref[rows, lanes]
            vi = vim_ref[rows, lanes]
            for q, s in enumerate((1, 2, 4)):
                sh = SUBLANES - s if reverse else s
                rr = pltpu.roll(vr, sh, 0)
                ri = pltpu.roll(vi, sh, 0)
                ar, ai = co[2 * q], co[2 * q + 1]
                vr, vi = vr + ar * rr - ai * ri, vi + ar * ri + ai * rr
            edge = 0 if reverse else SUBLANES - 1
            cbr = jnp.broadcast_to(cr[edge:edge + 1, :], (SUBLANES, LANES))
            cbi = jnp.broadcast_to(ci[edge:edge + 1, :], (SUBLANES, LANES))
            pr, pi = co[6], co[7]
            vr, vi = vr + pr * cbr - pi * cbi, vi + pr * cbi + pi * cbr
            vre_ref[rows, lanes] = vr
            vim_ref[rows, lanes] = vi
            out += [vr, vi]
            if xre_ref is not None:
                nr = jnp.where(sub == SUBLANES - 1, cbr, pltpu.roll(vr, SUBLANES - 1, 0))
                ni = jnp.where(sub == SUBLANES - 1, cbi, pltpu.roll(vi, SUBLANES - 1, 0))
                xr = xre_ref[rows, lanes]
                xi = xim_ref[rows, lanes]
                out += [carry[per * j + 2] + nr * xr + ni * xi, carry[per * j + 3] + ni * xr - nr * xi]
        return tuple(out)

    zero = jnp.zeros((SUBLANES, LANES), F32)
    res = lax.fori_loop(0, nt, step, (zero,) * (per * nl))
    if xre_ref is None:
        return None
    return jnp.concatenate(
        [jnp.concatenate([jnp.sum(res[per * j + 2], axis=0, keepdims=True) for j in range(nl)], axis=1),
         jnp.concatenate([jnp.sum(res[per * j + 3], axis=0, keepdims=True) for j in range(nl)], axis=1)], axis=0)


def _s5_fwd(z, bre3, bim3, cre3, cim3, coef, dskip, *, nseq, seq, rider=None):
    nb = bre3.shape[0]
    ch, ns = bre3.shape[1], bre3.shape[2]

    def body(za_ref, bre_ref, bim_ref, cre_ref, cim_ref, coef_ref, d_ref, y_ref, xre_ref, xim_ref):
        za = za_ref[...]
        xre_ref[...] = _dot(za, bre_ref[...])
        xim_ref[...] = _dot(za, bim_ref[...])
        _s5_scan(xre_ref, xim_ref, coef_ref, seq, ns, False)
        y_ref[...] = _dot(xre_ref[...], cre_ref[...]) - _dot(xim_ref[...], cim_ref[...]) + d_ref[...] * za

    blk3 = lambda r, c: pl.BlockSpec((None, r, c), lambda b, j: (j, 0, 0))
    return _hosted_call(
        "s5_fwd", body, grid=(nseq, nb),
        in_specs=[pl.BlockSpec((seq, ch), lambda b, j: (b, j)), blk3(ch, ns), blk3(ch, ns), blk3(ns, ch), blk3(ns, ch),
                  pl.BlockSpec((8 * SUBLANES, ns), lambda b, j: (0, j)), pl.BlockSpec((1, ch), lambda b, j: (0, j))],
        out_specs=[pl.BlockSpec((seq, ch), lambda b, j: (b, j)), pl.BlockSpec((seq, ns), lambda b, j: (b, j)),
                   pl.BlockSpec((seq, ns), lambda b, j: (b, j))],
        out_shape=[jax.ShapeDtypeStruct((nseq * seq, nb * ch), F32), jax.ShapeDtypeStruct((nseq * seq, nb * ns), F32),
                   jax.ShapeDtypeStruct((nseq * seq, nb * ns), F32)],
        operands=(z, bre3, bim3, cre3, cim3, coef, dskip), rider=rider)


def _s5_bwd(dy, z, xre, xim, bre3, bim3, cre3, cim3, coef_rev, dskip, *, nseq, seq, rider=None):
    nb = bre3.shape[0]
    ch, ns = bre3.shape[1], bre3.shape[2]

    def body(dy_ref, za_ref, xre_ref, xim_ref, bre_ref, bim_ref, cre_ref, cim_ref, coef_ref, d_ref,
             dza_ref, dbre_ref, dbim_ref, dcre_ref, dcim_ref, dlam_ref, dd_ref, are_ref, aim_ref):
        dy = dy_ref[...]
        za = za_ref[...]
        are_ref[...] = _dot_nt(dy, cre_ref[...])
        aim_ref[...] = -_dot_nt(dy, cim_ref[...])
        dlam = _s5_scan(are_ref, aim_ref, coef_ref, seq, ns, True, xre_ref, xim_ref)
        are = are_ref[...]
        aim = aim_ref[...]
        dza_ref[...] = (_dot_nt(are, bre_ref[...]) + _dot_nt(aim, bim_ref[...]) + d_ref[...] * dy).astype(dza_ref.dtype)
        parts = (_dot_tn(za, are), _dot_tn(za, aim), _dot_tn(xre_ref[...], dy), -_dot_tn(xim_ref[...], dy),
                 dlam, jnp.sum(dy * za, axis=0, keepdims=True))
        first = pl.program_id(1) == 0
        for r, v in zip((dbre_ref, dbim_ref, dcre_ref, dcim_ref, dlam_ref, dd_ref), parts):
            @pl.when(first)
            def _():
                r[...] = v

            @pl.when(jnp.logical_not(first))
            def _():
                r[...] += v

    blk3 = lambda r, c: pl.BlockSpec((None, r, c), lambda j, b: (j, 0, 0))
    tok = lambda c: pl.BlockSpec((seq, c), lambda j, b: (b, j))
    return _hosted_call(
        "s5_bwd", body, grid=(nb, nseq),
        in_specs=[tok(ch), tok(ch), tok(ns), tok(ns), blk3(ch, ns), blk3(ch, ns), blk3(ns, ch), blk3(ns, ch),
                  pl.BlockSpec((8 * SUBLANES, ns), lambda j, b: (0, j)), pl.BlockSpec((1, ch), lambda j, b: (0, j))],
        out_specs=[tok(ch), blk3(ch, ns), blk3(ch, ns), blk3(ns, ch), blk3(ns, ch),
                   pl.BlockSpec((None, 2, ns), lambda j, b: (j, 0, 0)), pl.BlockSpec((1, ch), lambda j, b: (0, j))],
        out_shape=[jax.ShapeDtypeStruct((nseq * seq, nb * ch), BF16),
                   jax.ShapeDtypeStruct((nb, ch, ns), F32), jax.ShapeDtypeStruct((nb, ch, ns), F32),
                   jax.ShapeDtypeStruct((nb, ns, ch), F32), jax.ShapeDtypeStruct((nb, ns, ch), F32),
                   jax.ShapeDtypeStruct((nb, 2, ns), F32), jax.ShapeDtypeStruct((1, nb * ch), F32)],
        scratch_shapes=[pltpu.VMEM((seq, ns), F32), pltpu.VMEM((seq, ns), F32)],
        operands=(dy, z, xre, xim, bre3, bim3, cre3, cim3, coef_rev, dskip), rider=rider)


def _glu_fwd(y, wglu, bglu):
    t, w = y.shape
    tm = _row_tile(t, GLU_TILE)

    def body(y_ref, w_ref, b_ref, a0_ref, gl_ref, a_ref):
        a0 = _gelu(y_ref[...])
        gl = _dot(a0, w_ref[...])
        a0_ref[...] = a0.astype(a0_ref.dtype)
        gl_ref[...] = gl
        a_ref[...] = (a0 * _sigmoid(gl + b_ref[...])).astype(a_ref.dtype)

    tok = pl.BlockSpec((tm, w), lambda i: (i, 0))
    return pl.pallas_call(
        body, name="s5_glu", grid=(t // tm,),
        in_specs=[tok, pl.BlockSpec((w, w), lambda i: (0, 0)), pl.BlockSpec((1, w), lambda i: (0, 0))],
        out_specs=[tok, tok, tok],
        out_shape=[jax.ShapeDtypeStruct((t, w), BF16), jax.ShapeDtypeStruct((t, w), F32), jax.ShapeDtypeStruct((t, w), BF16)],
        compiler_params=_params("arbitrary"),
    )(y, wglu, bglu)


def _glu_bwd(y, gl, bglu, da, wglu):
    t, w = y.shape
    tm = _row_tile(t, GLU_TILE)

    def body(y_ref, gl_ref, b_ref, da_ref, w_ref, dgl_ref, dy_ref, db_ref):
        yv = y_ref[...]
        dav = da_ref[...]
        s = _sigmoid(gl_ref[...] + b_ref[...])
        dgl = dav * _gelu(yv) * s * (1.0 - s)
        dgl_ref[...] = dgl.astype(dgl_ref.dtype)
        dy_ref[...] = (dav * s + _dot_nt(dgl, w_ref[...])) * _gelu_grad(yv)
        part = jnp.sum(dgl, axis=0, keepdims=True)
        first = pl.program_id(0) == 0

        @pl.when(first)
        def _():
            db_ref[...] = part

        @pl.when(jnp.logical_not(first))
        def _():
            db_ref[...] += part

    tok = pl.BlockSpec((tm, w), lambda i: (i, 0))
    vec = pl.BlockSpec((1, w), lambda i: (0, 0))
    return pl.pallas_call(
        body, name="s5_glu_bwd", grid=(t // tm,),
        in_specs=[tok, tok, vec, tok, pl.BlockSpec((w, w), lambda i: (0, 0))], out_specs=[tok, tok, vec],
        out_shape=[jax.ShapeDtypeStruct((t, w), BF16), jax.ShapeDtypeStruct((t, w), F32), jax.ShapeDtypeStruct((1, w), F32)],
        compiler_params=_params("arbitrary"),
    )(y, gl, bglu, da, wglu)


def _cumsum_rows(x, reverse=False):
    n = x.shape[0]
    row = lax.broadcasted_iota(jnp.int32, x.shape, 0)
    s = 1
    while s < n:
        if reverse:
            x = x + jnp.where(row < n - s, pltpu.roll(x, n - s, 0), 0.0)
        else:
            x = x + jnp.where(row >= s, pltpu.roll(x, s, 0), 0.0)
        s *= 2
    return x


def _hg_gates(zq, zf, lb):
    sg = _sigmoid(zf)
    f = lb + (1.0 - lb) * sg
    sq = _sigmoid(zq)
    qa = zq * sq * (HEAD ** -0.5)
    b = _cumsum_rows(jnp.log(f))
    return sg, f, sq, qa, 1.0 - f, b


SUB = 16


def _hg_scores(qa, kk, b):
    c = qa.shape[0]
    row = lax.broadcasted_iota(jnp.int32, qa.shape, 0)
    pos = jnp.bitwise_and(row, SUB - 1)
    dmat = lax.broadcasted_iota(jnp.int32, (c, c), 0) - lax.broadcasted_iota(jnp.int32, (c, c), 1)
    p = jnp.zeros((c, c), F32)
    for d in range(SUB):
        if d == 0:
            fd = qa * kk
        else:
            e = jnp.exp(jnp.minimum(b - pltpu.roll(b, d, 0), 0.0))
            fd = jnp.where(pos >= d, qa * pltpu.roll(kk, d, 0) * e, 0.0)
        p = jnp.where(dmat == d, jnp.sum(fd, axis=1, keepdims=True), p)
    col = lax.broadcasted_iota(jnp.int32, (SUB, c), 1)
    blocks = [jnp.zeros((SUB, c), F32)]
    for r0 in range(SUB, c, SUB):
        beta = b[r0 - 1:r0, :]
        qt = qa[r0:r0 + SUB] * jnp.exp(b[r0:r0 + SUB] - beta)
        kt = kk * jnp.exp(jnp.minimum(beta - b, 0.0))
        blocks.append(jnp.where(col < r0, _dot_nt(qt, kt), 0.0))
    return p + jnp.concatenate(blocks, axis=0)


def _hg_scores_bwd(dp, qa, kk, b):
    c = qa.shape[0]
    row = lax.broadcasted_iota(jnp.int32, qa.shape, 0)
    pos = jnp.bitwise_and(row, SUB - 1)
    dmat = lax.broadcasted_iota(jnp.int32, (c, c), 0) - lax.broadcasted_iota(jnp.int32, (c, c), 1)
    dqa = jnp.zeros_like(qa)
    dkk = jnp.zeros_like(qa)
    db = jnp.zeros_like(qa)
    for d in range(SUB):
        dcol = jnp.sum(jnp.where(dmat == d, dp, 0.0), axis=1, keepdims=True)
        if d == 0:
            dqa = dqa + dcol * kk
            dkk = dkk + dcol * qa
        else:
            e = jnp.exp(jnp.minimum(b - pltpu.roll(b, d, 0), 0.0))
            w = jnp.where(pos >= d, dcol * e, 0.0)
            kr = pltpu.roll(kk, d, 0)
            dqa = dqa + w * kr
            tmp = w * qa
            dkk = dkk + pltpu.roll(tmp, c - d, 0)
            x = tmp * kr
            db = db + x - pltpu.roll(x, c - d, 0)
    col = lax.broadcasted_iota(jnp.int32, (SUB, c), 1)
    dq_blocks = [jnp.zeros((SUB, qa.shape[1]), F32)]
    db_blocks = [jnp.zeros((SUB, qa.shape[1]), F32)]
    for r0 in range(SUB, c, SUB):
        beta = b[r0 - 1:r0, :]
        eq = jnp.exp(b[r0:r0 + SUB] - beta)
        ek = jnp.exp(jnp.minimum(beta - b, 0.0))
        qt = qa[r0:r0 + SUB] * eq
        kt = kk * ek
        dpi = jnp.where(col < r0, dp[r0:r0 + SUB, :], 0.0)
        dqt = _dot(dpi, kt)
        dkt = _dot_tn(dpi, qt)
        dq_blocks.append(dqt * eq)
        db_blocks.append(dqt * qt)
        dkk = dkk + dkt * ek
        db = db - dkt * kt
    return dqa + jnp.concatenate(dq_blocks, axis=0), dkk, db + jnp.concatenate(db_blocks, axis=0)


def _hg_chunks_per_step(seq):
    nc = seq // CHUNK
    cps = next(k for k in (4, 2, 1) if nc % k == 0)
    return nc, cps, nc // cps


def _hg_fwd(z, lbrow, gain, *, nseq, seq, heads, qoff, rider=None):
    nc, cps, nblk = _hg_chunks_per_step(seq)
    blk = cps * CHUNK
    zspec = lambda off: pl.BlockSpec((blk, HEAD), lambda h, b, n, off=off: (b * nblk + n, off + h))

    def body(zq_ref, zf_ref, zi_ref, zg_ref, lb_ref, gn_ref, o_ref, yb_ref, st_ref, sc_ref, state):
        @pl.when(pl.program_id(2) == 0)
        def _():
            state[...] = jnp.zeros_like(state)

        lb = lb_ref[...]
        gain_v = gn_ref[...]

        def chunk(ci, carry):
            rows = pl.ds(pl.multiple_of(ci * CHUNK, CHUNK), CHUNK)
            st = state[...]
            st_ref[ci] = st
            zi = zi_ref[rows, :]
            zg = zg_ref[rows, :]
            _, _, _, qa, kk, b = _hg_gates(zq_ref[rows, :], zf_ref[rows, :], lb)
            scores = _hg_scores(qa, kk, b).astype(BF16)
            sc_ref[rows, :] = scores
            o = _dot_nt(qa * jnp.exp(b), st) + _dot(scores, zi)
            bl = b[CHUNK - 1:CHUNK, :]
            state[...] = st * jnp.exp(bl) + _dot_tn(zi, kk * jnp.exp(bl - b))
            o_ref[rows, :] = o
            r = lax.rsqrt(jnp.mean(o * o, axis=1, keepdims=True) + EPS)
            yb_ref[rows, :] = (o * r * gain_v * zg * _sigmoid(zg)).astype(yb_ref.dtype)
            return carry

        lax.fori_loop(0, cps, chunk, 0, unroll=True)

    tok = pl.BlockSpec((blk, HEAD), lambda h, b, n: (b * nblk + n, h))
    vec = pl.BlockSpec((1, HEAD), lambda h, b, n: (0, h))
    rows = nseq * seq
    return _hosted_call(
        "hgrn2_fwd", body, grid=(heads, nseq, nblk),
        in_specs=[zspec(qoff), zspec(qoff + heads), zspec(qoff + 2 * heads), zspec(qoff + 3 * heads), vec, vec],
        out_specs=[tok, tok, pl.BlockSpec((None, None, cps, HEAD, HEAD), lambda h, b, n: (h, b, n, 0, 0)),
                   pl.BlockSpec((None, blk, CHUNK), lambda h, b, n: (h, b * nblk + n, 0))],
        out_shape=[jax.ShapeDtypeStruct((rows, heads * HEAD), F32), jax.ShapeDtypeStruct((rows, heads * HEAD), BF16),
                   jax.ShapeDtypeStruct((heads, nseq, nc, HEAD, HEAD), F32),
                   jax.ShapeDtypeStruct((heads, rows, CHUNK), BF16)],
        scratch_shapes=[pltpu.VMEM((HEAD, HEAD), F32)], operands=(z, z, z, z, lbrow, gain), rider=rider)


def _hg_bwd(dyb, z, o, states, scores, lbrow, gain, *, nseq, seq, heads, qoff, rider=None):
    nc, cps, nblk = _hg_chunks_per_step(seq)
    blk = cps * CHUNK
    rev = lambda n: nblk - 1 - n
    zspec = lambda off: pl.BlockSpec((blk, HEAD), lambda h, b, n, off=off: (b * nblk + rev(n), off + h))

    def body(dyb_ref, zq_ref, zf_ref, zi_ref, zg_ref, o_ref, st_ref, sc_ref, lb_ref, gn_ref,
             dzq_ref, dzf_ref, dzi_ref, dzg_ref, dlb_ref, dgn_ref, dstate):
        @pl.when(pl.program_id(2) == 0)
        def _():
            dstate[...] = jnp.zeros_like(dstate)

        @pl.when(jnp.logical_and(pl.program_id(1) == 0, pl.program_id(2) == 0))
        def _():
            dlb_ref[...] = jnp.zeros_like(dlb_ref)
            dgn_ref[...] = jnp.zeros_like(dgn_ref)

        lb = lb_ref[...]
        gain_v = gn_ref[...]
        c = CHUNK
        causal = lax.broadcasted_iota(jnp.int32, (c, c), 0) >= lax.broadcasted_iota(jnp.int32, (c, c), 1)

        def chunk(step, carry):
            ci = cps - 1 - step
            rows = pl.ds(pl.multiple_of(ci * CHUNK, CHUNK), CHUNK)
            zq = zq_ref[rows, :]
            zi = zi_ref[rows, :]
            zg = zg_ref[rows, :]
            sg, f, sq, qa, kk, b = _hg_gates(zq, zf_ref[rows, :], lb)
            eb = jnp.exp(b)
            qt = qa * eb
            bl = b[c - 1:c, :]
            ebl = jnp.exp(bl)
            ekb = jnp.exp(bl - b)
            kh = kk * ekb
            st = st_ref[ci]
            dst = dstate[...]
            o = o_ref[rows, :]
            r = lax.rsqrt(jnp.mean(o * o, axis=1, keepdims=True) + EPS)
            oh = o * r
            szg = _sigmoid(zg)
            dyb = dyb_ref[rows, :]
            don = dyb * zg * szg
            dzg_ref[rows, :] = (dyb * oh * gain_v * szg * (1.0 + zg * (1.0 - szg))).astype(dzg_ref.dtype)
            doh = don * gain_v
            do = r * (doh - oh * jnp.mean(doh * oh, axis=1, keepdims=True))
            dqt = _dot(do, st)
            dp = jnp.where(causal, _dot_nt(do, zi), 0.0)
            dzi_ref[rows, :] = (_dot_tn(sc_ref[rows, :], do) + _dot_nt(kh, dst)).astype(dzi_ref.dtype)
            dkh = _dot(zi, dst)
            dbl = jnp.sum(dkh * kh, axis=0, keepdims=True) + jnp.sum(dst * st, axis=0, keepdims=True) * ebl
            dstate[...] = _dot_tn(do, qt) + dst * ebl
            dqa_s, dkk_s, db_s = _hg_scores_bwd(dp, qa, kk, b)
            dqa = dqt * eb + dqa_s
            dkk = dkh * ekb + dkk_s
            db = dqt * qt - dkh * kh + db_s
            row = lax.broadcasted_iota(jnp.int32, db.shape, 0)
            db = db + jnp.where(row == c - 1, dbl, 0.0)
            df = _cumsum_rows(db, reverse=True) / f - dkk
            dzf_ref[rows, :] = (df * (1.0 - lb) * sg * (1.0 - sg)).astype(dzf_ref.dtype)
            dzq_ref[rows, :] = (dqa * (HEAD ** -0.5) * sq * (1.0 + zq * (1.0 - sq))).astype(dzq_ref.dtype)
            dlb_ref[...] += jnp.sum(df * (1.0 - sg), axis=0, keepdims=True)
            dgn_ref[...] += jnp.sum(don * oh, axis=0, keepdims=True)
            return carry

        lax.fori_loop(0, cps, chunk, 0, unroll=True)

    tok = pl.BlockSpec((blk, HEAD), lambda h, b, n: (b * nblk + rev(n), h))
    vec = pl.BlockSpec((1, HEAD), lambda h, b, n: (0, h))
    rows = nseq * seq
    return _hosted_call(
        "hgrn2_bwd", body, grid=(heads, nseq, nblk),
        in_specs=[tok, zspec(qoff), zspec(qoff + heads), zspec(qoff + 2 * heads), zspec(qoff + 3 * heads), tok,
                  pl.BlockSpec((None, None, cps, HEAD, HEAD), lambda h, b, n: (h, b, rev(n), 0, 0)),
                  pl.BlockSpec((None, blk, CHUNK), lambda h, b, n: (h, b * nblk + rev(n), 0)), vec, vec],
        out_specs=[tok, tok, tok, tok, vec, vec],
        out_shape=[jax.ShapeDtypeStruct((rows, heads * HEAD), BF16)] * 4
        + [jax.ShapeDtypeStruct((1, heads * HEAD), F32)] * 2,
        scratch_shapes=[pltpu.VMEM((HEAD, HEAD), F32)],
        operands=(dyb, z, z, z, z, o, states, scores, lbrow, gain), rider=rider)


def _shift_rows(x, k):
    n = x.shape[0]
    r = pltpu.roll(x, k % n, 0)
    sub = lax.broadcasted_iota(jnp.int32, (SUBLANES, x.shape[1]), 0)
    if k > 0:
        return jnp.concatenate([jnp.where(sub >= k, r[0:SUBLANES], 0.0), r[SUBLANES:]], axis=0)
    return jnp.concatenate([r[:n - SUBLANES], jnp.where(sub < SUBLANES + k, r[n - SUBLANES:], 0.0)], axis=0)


def _conv_taps(h, w, bias):
    h1 = _shift_rows(h, 1)
    h2 = _shift_rows(h, 2)
    return h2 * w[0:1, :] + h1 * w[1:2, :] + h * w[2:3, :] + bias, h1, h2


def _conv_fwd(h, wconv, bconv, *, nseq, seq):
    ff2 = h.shape[1]
    ncol = ff2 // 2 // LANES

    def body(hg_ref, hv_ref, wg_ref, wv_ref, bg_ref, bv_ref, a_ref):
        g, _, _ = _conv_taps(hg_ref[...].astype(F32), wg_ref[...], bg_ref[...])
        v, _, _ = _conv_taps(hv_ref[...].astype(F32), wv_ref[...], bv_ref[...])
        a_ref[...] = (g * _sigmoid(g) * v).astype(a_ref.dtype)

    tok = lambda off: pl.BlockSpec((seq, LANES), lambda j, b, off=off: (b, off + j))
    wsp = lambda off: pl.BlockSpec((CONV_W, LANES), lambda j, b, off=off: (0, off + j))
    bsp = lambda off: pl.BlockSpec((1, LANES), lambda j, b, off=off: (0, off + j))
    return pl.pallas_call(
        body, name="conv_fwd", grid=(ncol, nseq),
        in_specs=[tok(0), tok(ncol), wsp(0), wsp(ncol), bsp(0), bsp(ncol)],
        out_specs=tok(0), out_shape=jax.ShapeDtypeStruct((nseq * seq, ff2 // 2), BF16),
        compiler_params=_params("arbitrary", "arbitrary"),
    )(h, h, wconv, wconv, bconv, bconv)


def _conv_bwd(da, h, wconv, bconv, *, nseq, seq):
    ff2 = h.shape[1]
    ncol = ff2 // 2 // LANES

    def half_bwd(d, hcur, h1, h2, w):
        d1 = _shift_rows(d, -1)
        d2 = _shift_rows(d, -2)
        dh = d * w[2:3, :] + d1 * w[1:2, :] + d2 * w[0:1, :]
        stats = jnp.concatenate(
            [jnp.sum(h2 * d, axis=0, keepdims=True), jnp.sum(h1 * d, axis=0, keepdims=True),
             jnp.sum(hcur * d, axis=0, keepdims=True), jnp.sum(d, axis=0, keepdims=True),
             jnp.zeros((SUBLANES - 4, d.shape[1]), F32)], axis=0)
        return dh, stats

    def body(da_ref, hg_ref, hv_ref, wg_ref, wv_ref, bg_ref, bv_ref, dhg_ref, dhv_ref, sg_ref, sv_ref):
        hg = hg_ref[...].astype(F32)
        hv = hv_ref[...].astype(F32)
        wg = wg_ref[...]
        wv = wv_ref[...]
        g, g1, g2 = _conv_taps(hg, wg, bg_ref[...])
        v, v1, v2 = _conv_taps(hv, wv, bv_ref[...])
        da = da_ref[...].astype(F32)
        s = _sigmoid(g)
        dhg, stg = half_bwd(da * v * s * (1.0 + g * (1.0 - s)), hg, g1, g2, wg)
        dhv, stv = half_bwd(da * g * s, hv, v1, v2, wv)
        dhg_ref[...] = dhg.astype(dhg_ref.dtype)
        dhv_ref[...] = dhv.astype(dhv_ref.dtype)
        first = pl.program_id(1) == 0
        for r, val in ((sg_ref, stg), (sv_ref, stv)):
            @pl.when(first)
            def _():
                r[...] = val

            @pl.when(jnp.logical_not(first))
            def _():
                r[...] += val

    tok = lambda off: pl.BlockSpec((seq, LANES), lambda j, b, off=off: (b, off + j))
    wsp = lambda off: pl.BlockSpec((CONV_W, LANES), lambda j, b, off=off: (0, off + j))
    bsp = lambda off: pl.BlockSpec((1, LANES), lambda j, b, off=off: (0, off + j))
    ssp = pl.BlockSpec((SUBLANES, LANES), lambda j, b: (0, j))
    dhg, dhv, stg, stv = pl.pallas_call(
        body, name="conv_bwd", grid=(ncol, nseq),
        in_specs=[tok(0), tok(0), tok(ncol), wsp(0), wsp(ncol), bsp(0), bsp(ncol)],
        out_specs=[tok(0), tok(0), ssp, ssp],
        out_shape=[jax.ShapeDtypeStruct((nseq * seq, ff2 // 2), BF16)] * 2
        + [jax.ShapeDtypeStruct((SUBLANES, ff2 // 2), F32)] * 2,
        compiler_params=_params("arbitrary", "arbitrary"),
    )(da, h, h, wconv, wconv, bconv, bconv)
    return (dhg, dhv), jnp.concatenate([stg, stv], axis=1)


def _rms_fwd(xv, g):
    r = lax.rsqrt(jnp.mean(xv * xv, axis=1, keepdims=True) + EPS)
    return (xv * r * g,)


def _rms_bwd(xv, g, dy, res):
    r = lax.rsqrt(jnp.mean(xv * xv, axis=1, keepdims=True) + EPS)
    xh = xv * r
    dxh = dy * g
    dx = r * (dxh - xh * jnp.mean(dxh * xh, axis=1, keepdims=True)) + res
    return dx, jnp.sum(dy * xh, axis=0, keepdims=True)


def _loss_head(x2, tgt, g):
    d = x2.shape[1]
    r = lax.rsqrt(jnp.mean(x2 * x2, axis=1, keepdims=True) + EPS)
    xh = x2 * r
    err = xh * g - tgt
    dy = err * (1.0 / d)
    dxh = dy * g
    dx = r * (dxh - xh * jnp.mean(dxh * xh, axis=1, keepdims=True))
    loss = 0.5 * jnp.sum(jnp.mean(err * err, axis=1, keepdims=True), axis=0, keepdims=True)
    return dx, jnp.sum(dy * xh, axis=0, keepdims=True), jnp.broadcast_to(loss, (1, LANES))


LATE_A = ("w_down", "w_out")
LATE_B = ("w_up", "w_pa", "w_pb")
LATE = LATE_A + LATE_B
EARLY_GRADS = ("w_down", "w_up", "w_out", "w_pa", "w_pb", "w_glu")
ROW_SHARDED = ("w_glu", "w_out", "w_down")


def _local_step(x, tgt, u, z_own, p, late, *, nseq, seq):
    p = dict(p)
    chip = 2 * lax.axis_index("x") + lax.axis_index("y")
    t, d = x.shape
    s5w = p["s5_d"].shape[1]
    hgw = p["gain"].shape[1]
    heads = hgw // HEAD
    qoff = s5w // LANES
    gblk = (s5w + 4 * hgw) // GATE_BLOCK
    ngb = d // GATE_BLOCK
    tm = _row_tile(t, ROW_TILE)
    row = lambda a, w=None, base=0: (a, a.shape[1] if w is None else w, base, "row")
    vec = lambda a, w=None, base=0: (a, a.shape[1] if w is None else w, base, "vec")
    rw = functools.partial(_rowwise, rows=t, tm=tm)

    z, landed_a = _in_proj_rest(u, p["w_in"], z_own, _gather_ici_rider([late[n] for n in LATE_A]))

    lam_re, lam_im, bb_re, bb_im = _s5_discretize(p["s5_a_re"], p["s5_a_im"], p["s5_log_dt"], p["s5_b_re"], p["s5_b_im"])
    bre3 = _s5_in_blocks(bb_re).astype(BF16)
    bim3 = _s5_in_blocks(bb_im).astype(BF16)
    cre3 = _s5_out_blocks(p["s5_c_re"]).astype(BF16)
    cim3 = _s5_out_blocks(p["s5_c_im"]).astype(BF16)
    coef_f = _s5_scan_tables(lam_re.reshape(-1), lam_im.reshape(-1), False)
    coef_r = _s5_scan_tables(lam_re.reshape(-1), lam_im.reshape(-1), True)
    (o, yb, states, scores), landed_b = _hg_fwd(z, p["lbrow"], p["gain"], nseq=nseq, seq=seq, heads=heads, qoff=qoff,
                                                rider=_gather_ici_rider([late[n] for n in LATE_B]))
    (y5, xre, xim), gathered = _s5_fwd(z, bre3, bim3, cre3, cim3, coef_f, p["s5_d"], nseq=nseq, seq=seq,
                                       rider=_gather_pass_rider(list(landed_a) + list(landed_b)))
    for n, g in zip(LATE, gathered):
        full = lax.dynamic_update_index_in_dim(g, late[n], chip, 0)
        p[n] = full.reshape(-1, full.shape[-1]) if n in ROW_SHARDED else full
    ya0, gl, ya = _glu_fwd(y5, p["w_glu"], p["b_glu"])

    joined = lambda w3: w3.transpose(1, 0, 2).reshape(w3.shape[1], -1)
    split = lambda g: g.reshape(g.shape[0], N_CHIPS, -1).transpose(1, 0, 2)
    wpa, wpb = joined(p["w_pa"]), joined(p["w_pb"])
    pa = _mm_fwd_rows("proj_a", ya, wpa, out_dtype=BF16)
    pb = _mm_fwd_rows("proj_b", yb, wpb, out_dtype=BF16)
    gb = GATE_BLOCK
    (m,) = rw("merge", lambda ga, gbv, a, b: (_sigmoid(ga) * a + _sigmoid(gbv) * b,),
              [row(z, gb, gblk), row(z, gb, gblk + ngb), row(pa, gb), row(pb, gb)], [(d, gb, BF16)], ncol=ngb)
    x1 = _mm_fwd_rows("out_proj", m, p["w_out"], res=x)

    (u2,) = rw("rms_ffn", _rms_fwd, [row(x1), vec(p["g_ffn"])], [(d, d, BF16)])
    h = _mm_fwd_cols("up_proj", u2, p["w_up"], out_dtype=BF16)
    a = _conv_fwd(h, p["w_conv"], p["b_conv"], nseq=nseq, seq=seq)
    dx2, dg_final, lossv = _mm_fwd_rows("down_proj", a, p["w_down"], res=x1,
                                        epilogue=(_loss_head, [tgt, p["g_final"]], [d, LANES]))

    norm_bwd = lambda dyv, xv, g, resv: _rms_bwd(xv, g, dyv, resv)
    da = _mm_bwd_rows("down_bwd", dx2, p["w_down"], out_dtype=BF16)
    g_wdown = _mm_wgrad_rows("down_wgrad", a, dx2)
    dh, cstats = _conv_bwd(da, h, p["w_conv"], p["b_conv"], nseq=nseq, seq=seq)
    dx1, dg_ffn = _mm_bwd_cols("up_bwd", dh, p["w_up"], epilogue=(norm_bwd, [x1, p["g_ffn"], dx2], [d]))
    g_wup = _mm_wgrad_cols("up_wgrad", u2, dh)

    dm = _mm_bwd_rows("out_bwd", dx1, p["w_out"], out_dtype=BF16)
    g_wout = _mm_wgrad_rows("out_wgrad", m, dx1)

    def merge_bwd(ga, gbv, av, bv, dmv):
        sa = _sigmoid(ga)
        sb = _sigmoid(gbv)
        return dmv * sa, dmv * sb, dmv * av * sa * (1.0 - sa), dmv * bv * sb * (1.0 - sb)

    dpa, dpb, dzga, dzgb = rw("merge_bwd", merge_bwd,
                              [row(z, gb, gblk), row(z, gb, gblk + ngb), row(pa, gb), row(pb, gb), row(dm, gb)],
                              [(d, gb, BF16)] * 4, ncol=ngb)
    dya = _mm_bwd_rows("proj_a_bwd", dpa, wpa)
    g_wpa = split(_mm_wgrad_rows("proj_a_wgrad", ya, dpa))
    dyb = _mm_bwd_rows("proj_b_bwd", dpb, wpb)
    g_wpb = split(_mm_wgrad_rows("proj_b_wgrad", yb, dpb))

    dgl, dy5, db_glu = _glu_bwd(y5, gl, p["b_glu"], dya, p["w_glu"])
    g_wglu = _mm_wgrad_rows("glu_wgrad", ya0, dgl)
    partial = dict(w_down=g_wdown, w_up=g_wup, w_out=g_wout, w_pa=g_wpa, w_pb=g_wpb, w_glu=g_wglu)
    parts = [_grad_parts(partial[n]) for n in EARLY_GRADS]
    (dza, dbre3, dbim3, dcre3, dcim3, dlam, dd), sib = _s5_bwd(
        dy5, z, xre, xim, bre3, bim3, cre3, cim3, coef_r, p["s5_d"], nseq=nseq, seq=seq, rider=_swap_halves_rider(parts))
    pair = _pair_sums(EARLY_GRADS, parts, sib)
    (dzq, dzf, dzi, dzg, dlb, dgain), others = _hg_bwd(
        dyb, z, o, states, scores, p["lbrow"], p["gain"], nseq=nseq, seq=seq, heads=heads, qoff=qoff,
        rider=_scatter_rider(pair))
    halves = _chip_sums(EARLY_GRADS, pair, others)

    dz = jnp.concatenate([dza, dzq, dzf, dzi, dzg, dzga, dzgb], axis=1)
    gshape = lam_re.shape
    small = {
        "loss": lossv, "g_ffn": dg_ffn, "g_final": dg_final, "b_glu": db_glu, "gain": dgain,
        "lbrow": dlb, "s5_d": dd, "w_conv": cstats[0:CONV_W], "b_conv": cstats[CONV_W:CONV_W + 1],
        "lam_re": dlam[:, 0, :].reshape(gshape), "lam_im": dlam[:, 1, :].reshape(gshape),
        "bb_re": _s5_in_blocks_diag(dbre3), "bb_im": _s5_in_blocks_diag(dbim3),
        "s5_c_re": _s5_out_blocks_diag(dcre3), "s5_c_im": _s5_out_blocks_diag(dcim3),
    }
    small_vec = _pack([small[n] for n in SMALL_PARTS], F32)
    g_win, (small_all, *sibs) = _mm_wgrad_cols(
        "in_wgrad", u, dz, rider=_merge_riders(_gather_all_rider(small_vec), _swap_sums_rider(halves)))
    big = dict(zip(EARLY_GRADS, zip(halves, sibs)))
    small_sum = _sum_over_devices("small_grad_sum", small_vec, small_all)
    sm = dict(zip(SMALL_PARTS, _unpack(small_sum, [small[n].shape for n in SMALL_PARTS])))
    last = [_grad_parts(g_win)]
    pair = _pair_sums(("w_in",), last, _run_rider("grad_swap_halves", _swap_halves_rider(last)))
    (dx, dg_mix), others = _mm_bwd_cols("in_bwd", dz, p["w_in"], epilogue=(norm_bwd, [x, p["g_mix"], dx1], [d]),
                                        rider=_scatter_rider(pair))
    (half,) = _chip_sums(("w_in",), pair, others)
    mid = half.shape[0] // 2
    mix_vec = dg_mix.reshape(SUBLANES, -1)
    top, bottom, mix_all = _run_rider("grad_swap_sums", _merge_riders(_swap_sums_rider([half[:mid], half[mid:]]),
                                                                      _gather_all_rider(mix_vec)))
    big["w_in"] = (half, jnp.concatenate([top, bottom], axis=0))
    sm["g_mix"] = _sum_over_devices("g_mix_sum", mix_vec, mix_all).reshape(dg_mix.shape)
    return dx, big, sm


ANY = pl.BlockSpec(memory_space=pl.ANY)


def _place():
    x, y, c = lax.axis_index("x"), lax.axis_index("y"), lax.axis_index("c")
    chips = [(1 - x, y), (x, 1 - y), (1 - x, 1 - y)]
    return x, y, c, chips


def _remote(src, dst, send_sems, recv_sems, k, to):
    return pltpu.make_async_remote_copy(src_ref=src, dst_ref=dst, send_sem=send_sems.at[k], recv_sem=recv_sems.at[k],
                                        device_id=to, device_id_type=MESH)


def _half(rows, which):
    return pl.ds(pl.multiple_of(which * (rows // 2), SUBLANES), rows // 2)


class _SemView:
    def __init__(self, base, offset):
        self.base, self.offset = base, offset

    @property
    def at(self):
        return self

    def __getitem__(self, k):
        return self.base.at[self.offset + k]


def _merge_riders(first, second):
    na, no, ns = len(first.arrays), len(first.out_shapes), first.nsem

    def split(fn_a, fn_b):
        def run(ins, outs, send_sems, recv_sems):
            fn_a(ins[:na], outs[:no], send_sems, recv_sems)
            fn_b(ins[na:], outs[no:], _SemView(send_sems, ns), _SemView(recv_sems, ns))
        return run

    aliases = dict(first.aliases)
    aliases.update({na + i: no + o for i, o in second.aliases.items()})
    return _Rider(first.arrays + second.arrays, first.out_shapes + second.out_shapes, ns + second.nsem,
                  split(first.start, second.start), split(first.finish, second.finish), aliases)


PREPARE_TILE = 512


def _prepare(x, gain, w_own, arrays, rider):
    t, d = x.shape
    ns = w_own.shape[1]
    n = len(arrays)
    tm = _row_tile(t, PREPARE_TILE)

    def body(place, x_ref, g_ref, w_ref, *refs):
        (u,) = _rms_fwd(x_ref[...], g_ref[...])
        u = u.astype(BF16)
        refs[n][...] = u
        refs[n + 1][...] = _dot(u, w_ref[...])

        @pl.when(pl.program_id(0) == 0)
        def _():
            for i in range(n):
                refs[n + 2 + i][...] = refs[i][...].astype(BF16)

    vm = pl.BlockSpec(memory_space=pltpu.VMEM)
    tok = pl.BlockSpec((tm, d), lambda i, place: (i, 0))
    outs, gathered = _hosted_scalar_call(
        "prepare", body, grid=(t // tm,),
        in_specs=[tok, pl.BlockSpec((1, d), lambda i, place: (0, 0)), vm] + [vm] * n,
        out_specs=[tok, pl.BlockSpec((tm, ns), lambda i, place: (i, place[1]))] + [vm] * n,
        out_shape=[jax.ShapeDtypeStruct((t, d), BF16), jax.ShapeDtypeStruct((t, N_CHIPS * ns), F32)]
        + [jax.ShapeDtypeStruct(a.shape, BF16) for a in arrays],
        operands=[x, gain, w_own] + list(arrays), rider=rider)
    return outs[0], outs[1], outs[2:], gathered


def _in_proj_rest(u, w3, z, rider):
    t, k = u.shape
    ns = w3.shape[2]
    tm = _mm_tile(t, 2 * k * _size(u) + 2 * ns * 4, 2 * k * ns * _size(w3))
    other = lambda j, place: jnp.bitwise_xor(place[1], j + 1)

    def body(place, u_ref, w_ref, z_ref, o_ref):
        o_ref[...] = _dot(u_ref[...], w_ref[...])

    outs, ridden = _hosted_scalar_call(
        "in_proj", body, grid=(N_CHIPS - 1, t // tm),
        in_specs=[pl.BlockSpec((tm, k), lambda j, i, place: (i, 0)),
                  pl.BlockSpec((None, k, ns), lambda j, i, place: (other(j, place), 0, 0)), ANY],
        out_specs=[pl.BlockSpec((tm, ns), lambda j, i, place: (i, other(j, place)))],
        out_shape=[jax.ShapeDtypeStruct(z.shape, z.dtype)], operands=[u, w3, z], rider=rider, aliases={2: 0})
    return outs[0], ridden


def _symmetric_rider(arrays, out_shapes, copies_of, nsem):
    def start(ins, outs, send_sems, recv_sems):
        for cp in copies_of(ins, outs, send_sems, recv_sems):
            cp.start()

    def finish(ins, outs, send_sems, recv_sems):
        for cp in copies_of(ins, outs, send_sems, recv_sems):
            cp.wait()

    return _Rider(arrays, out_shapes, nsem, start, finish)


def _swap_halves_rider(parts):
    def copies_of(ins, outs, send_sems, recv_sems):
        x, y, c, _ = _place()
        return [_remote(ins[a].at[:, _half(g.shape[1], 1 - c), :], outs[a], send_sems, recv_sems, a, (x, y, 1 - c))
                for a, g in enumerate(parts)]

    shapes = [jax.ShapeDtypeStruct((g.shape[0], g.shape[1] // 2, g.shape[2]), g.dtype) for g in parts]
    return _symmetric_rider(parts, shapes, copies_of, len(parts))


def _scatter_rider(parts):
    def copies_of(ins, outs, send_sems, recv_sems):
        x, y, c, chips = _place()
        return [_remote(ins[a].at[2 * cx + cy], outs[a].at[j], send_sems, recv_sems, 3 * a + j, (cx, cy, c))
                for a in range(len(parts)) for j, (cx, cy) in enumerate(chips)]

    shapes = [jax.ShapeDtypeStruct((N_CHIPS - 1,) + h.shape[1:], h.dtype) for h in parts]
    return _symmetric_rider(parts, shapes, copies_of, 3 * len(parts))


def _swap_sums_rider(parts):
    def copies_of(ins, outs, send_sems, recv_sems):
        x, y, c, _ = _place()
        return [_remote(ins[a], outs[a], send_sems, recv_sems, a, (x, y, 1 - c)) for a in range(len(parts))]

    shapes = [jax.ShapeDtypeStruct(g.shape, g.dtype) for g in parts]
    return _symmetric_rider(parts, shapes, copies_of, len(parts))


def _gather_ici_rider(shards):
    def sends(ins, outs, send_sems, recv_sems):
        x, y, c, chips = _place()
        return [_remote(ins[a].at[_half(s.shape[0], c)], outs[a].at[2 * x + y, _half(s.shape[0], c)], send_sems,
                        recv_sems, 3 * a + j, (cx, cy, c)) for a, s in enumerate(shards) for j, (cx, cy) in enumerate(chips)]

    def start(ins, outs, send_sems, recv_sems):
        for cp in sends(ins, outs, send_sems, recv_sems):
            cp.start()

    def finish(ins, outs, send_sems, recv_sems):
        x, y, c, chips = _place()
        for a, s in enumerate(shards):
            for j, (cx, cy) in enumerate(chips):
                landed = outs[a].at[2 * cx + cy, _half(s.shape[0], c)]
                _remote(landed, landed, send_sems, recv_sems, 3 * a + j, (x, y, c)).wait_recv()
        for cp in sends(ins, outs, send_sems, recv_sems):
            cp.wait_send()

    shapes = [jax.ShapeDtypeStruct((N_CHIPS,) + s.shape, s.dtype) for s in shards]
    return _Rider(shards, shapes, 3 * len(shards), start, finish)


def _gather_full_rider(shards):
    n = len(shards)

    def sends(ins, outs, send_sems, recv_sems):
        x, y, c, chips = _place()
        return [_remote(ins[a].at[_half(s.shape[0], c)], outs[a].at[2 * x + y, _half(s.shape[0], c)], send_sems,
                        recv_sems, 6 * a + j, (cx, cy, c)) for a, s in enumerate(shards) for j, (cx, cy) in enumerate(chips)]

    def start(ins, outs, send_sems, recv_sems):
        for cp in sends(ins, outs, send_sems, recv_sems):
            cp.start()

    def finish(ins, outs, send_sems, recv_sems):
        x, y, c, chips = _place()
        passed = []
        for a, s in enumerate(shards):
            for j, (cx, cy) in enumerate(chips):
                landed = outs[a].at[2 * cx + cy, _half(s.shape[0], c)]
                _remote(landed, landed, send_sems, recv_sems, 6 * a + j, (x, y, c)).wait_recv()
                passed.append(_remote(landed, landed, send_sems, recv_sems, 6 * a + 3 + j, (x, y, 1 - c)))
                passed[-1].start()
        for a, s in enumerate(shards):
            for j, (cx, cy) in enumerate(chips):
                other = outs[a].at[2 * cx + cy, _half(s.shape[0], 1 - c)]
                _remote(other, other, send_sems, recv_sems, 6 * a + 3 + j, (x, y, c)).wait_recv()
        for cp in sends(ins, outs, send_sems, recv_sems) + passed:
            cp.wait_send()

    shapes = [jax.ShapeDtypeStruct((N_CHIPS,) + s.shape, s.dtype) for s in shards]
    return _Rider(shards, shapes, 6 * n, start, finish)


def _gather_pass_rider(landed):
    def sends(ins, outs, send_sems, recv_sems):
        x, y, c, chips = _place()
        return [_remote(ins[a].at[2 * cx + cy, _half(g.shape[1], c)], outs[a].at[2 * cx + cy, _half(g.shape[1], c)],
                        send_sems, recv_sems, 3 * a + j, (x, y, 1 - c))
                for a, g in enumerate(landed) for j, (cx, cy) in enumerate(chips)]

    def start(ins, outs, send_sems, recv_sems):
        for cp in sends(ins, outs, send_sems, recv_sems):
            cp.start()

    def finish(ins, outs, send_sems, recv_sems):
        x, y, c, chips = _place()
        for a, g in enumerate(landed):
            for j, (cx, cy) in enumerate(chips):
                other = outs[a].at[2 * cx + cy, _half(g.shape[1], 1 - c)]
                _remote(other, other, send_sems, recv_sems, 3 * a + j, (x, y, c)).wait_recv()
        for cp in sends(ins, outs, send_sems, recv_sems):
            cp.wait_send()

    shapes = [jax.ShapeDtypeStruct(g.shape, g.dtype) for g in landed]
    return _Rider(landed, shapes, 3 * len(landed), start, finish, aliases={a: a for a in range(len(landed))})


def _grad_parts(g):
    return g.reshape((N_CHIPS, -1, g.shape[-1]))


def _place_scalars():
    return jnp.stack([lax.axis_index("c"), 2 * lax.axis_index("x") + lax.axis_index("y")]).astype(jnp.int32)


def _scalar_call(body, name, grid, in_specs, out_specs, out_shape, operands):
    spec = pltpu.PrefetchScalarGridSpec(num_scalar_prefetch=1, grid=grid, in_specs=in_specs, out_specs=out_specs)
    return pl.pallas_call(body, name=name, grid_spec=spec, out_shape=out_shape,
                          compiler_params=_params(*(["arbitrary"] * len(grid))))(_place_scalars(), *operands)


def _pair_sums(names, parts, sib):
    out = []
    for n, g, s in zip(names, parts, sib):
        rh, cols = s.shape[1], s.shape[2]
        tm = _row_tile(rh, ROW_TILE)
        nblk = rh // tm

        def body(place, g_ref, s_ref, o_ref):
            o_ref[...] = (g_ref[...].astype(F32) + s_ref[...].astype(F32)).astype(o_ref.dtype)

        blk = pl.BlockSpec((None, tm, cols), lambda j, i, place: (j, i, 0))
        own = pl.BlockSpec((None, tm, cols), lambda j, i, place, nblk=nblk: (j, place[0] * nblk + i, 0))
        out.append(_scalar_call(body, "grad_pair_sum_" + n, (N_CHIPS, nblk), [own, blk], blk,
                                jax.ShapeDtypeStruct(s.shape, BF16), (g, s)))
    return out


def _chip_sums(names, pair, others):
    out = []
    for n, h, o in zip(names, pair, others):
        rh, cols = h.shape[1], h.shape[2]
        tm = _row_tile(rh, ROW_TILE)

        def body(place, h_ref, a_ref, b_ref, c_ref, o_ref):
            o_ref[...] = (h_ref[...].astype(F32) + a_ref[...].astype(F32)) + b_ref[...].astype(F32) + c_ref[...].astype(F32)

        mine = pl.BlockSpec((None, tm, cols), lambda i, place: (place[1], i, 0))
        other = lambda k: pl.BlockSpec((None, tm, cols), lambda i, place, k=k: (k, i, 0))
        out.append(_scalar_call(body, "grad_chip_sum_" + n, (rh // tm,), [mine, other(0), other(1), other(2)],
                                pl.BlockSpec((tm, cols), lambda i, place: (i, 0)), jax.ShapeDtypeStruct((rh, cols), F32),
                                (h, o, o, o)))
    return out


def _adamw_halves(name, w, m, v, own, sib):
    rh, cols = own.shape
    tm = _row_tile(rh, ADAMW_TILE)
    nblk = rh // tm

    def body(place, w_ref, m_ref, v_ref, own_ref, sib_ref, g_ref, d_ref, m2_ref, v2_ref):
        mine = pl.program_id(0) // nblk == place[0]

        def run(gv):
            g_ref[...] = gv
            d_ref[...], m2_ref[...], v2_ref[...] = _adamw_math(w_ref[...], gv, m_ref[...], v_ref[...])

        @pl.when(mine)
        def _():
            run(own_ref[...])

        @pl.when(jnp.logical_not(mine))
        def _():
            run(sib_ref[...])

    full = pl.BlockSpec((tm, cols), lambda i, place: (i, 0))
    own_spec = pl.BlockSpec((tm, cols), lambda i, place: (jnp.where(i // nblk == place[0], i % nblk, 0), 0))
    sib_spec = pl.BlockSpec((tm, cols), lambda i, place: (jnp.where(i // nblk == place[0], 0, i % nblk), 0))
    return _scalar_call(body, name, (2 * nblk,), [full, full, full, own_spec, sib_spec], [full] * 4,
                        [jax.ShapeDtypeStruct((2 * rh, cols), F32)] * 4, (w, m, v, own, sib))


def _gather_all_rider(v):
    m_per = v.shape[0]

    def rows(ref, px, py, pc):
        return ref.at[pl.ds(pl.multiple_of((4 * px + 2 * py + pc) * m_per, 8), m_per)]

    def first(ins, outs, send_sems, recv_sems):
        x, y, c, chips = _place()
        mine = rows(outs[0], x, y, c)
        return [_remote(ins[0], mine, send_sems, recv_sems, 0, (x, y, 1 - c))] + [
            _remote(ins[0], mine, send_sems, recv_sems, 1 + j, (cx, cy, c)) for j, (cx, cy) in enumerate(chips)]

    def start(ins, outs, send_sems, recv_sems):
        for cp in first(ins, outs, send_sems, recv_sems):
            cp.start()

    def finish(ins, outs, send_sems, recv_sems):
        x, y, c, chips = _place()
        passed = []
        for j, (cx, cy) in enumerate(chips):
            blk = rows(outs[0], cx, cy, c)
            _remote(blk, blk, send_sems, recv_sems, 1 + j, (x, y, c)).wait_recv()
            passed.append(_remote(blk, blk, send_sems, recv_sems, 4 + j, (x, y, 1 - c)))
            passed[j].start()
        sib = rows(outs[0], x, y, 1 - c)
        _remote(sib, sib, send_sems, recv_sems, 0, (x, y, c)).wait_recv()
        for j, (cx, cy) in enumerate(chips):
            blk = rows(outs[0], cx, cy, 1 - c)
            _remote(blk, blk, send_sems, recv_sems, 4 + j, (x, y, c)).wait_recv()
        for cp in first(ins, outs, send_sems, recv_sems) + passed:
            cp.wait_send()

    return _Rider([v], [jax.ShapeDtypeStruct((N_DEV * m_per,) + v.shape[1:], v.dtype)], 7, start, finish)


def _sum_over_devices(name, v, gathered):
    m_per = v.shape[0]
    dev = 4 * lax.axis_index("x") + 2 * lax.axis_index("y") + lax.axis_index("c")
    full = lax.dynamic_update_slice_in_dim(gathered, v, dev * m_per, axis=0)
    return _sum_blocks(name, [full[i * m_per:(i + 1) * m_per] for i in range(N_DEV)], F32)


def _sum_blocks(name, parts, out_dtype):
    rows, cols = parts[0].shape
    tm = _row_tile(rows, ROW_TILE)

    def body(*refs):
        acc = refs[0][...].astype(F32)
        for r in refs[1:-1]:
            acc = acc + r[...].astype(F32)
        refs[-1][...] = acc.astype(refs[-1].dtype)

    spec = pl.BlockSpec((tm, cols), lambda i: (i, 0))
    return pl.pallas_call(
        body, name=name, grid=(rows // tm,), in_specs=[spec] * len(parts), out_specs=spec,
        out_shape=jax.ShapeDtypeStruct((rows, cols), out_dtype), compiler_params=_params("arbitrary"),
    )(*parts)


def _adamw_math(wv, gv, mv, vv):
    m2 = ADAM_B1 * mv + (1.0 - ADAM_B1) * gv
    v2 = ADAM_B2 * vv + (1.0 - ADAM_B2) * (gv * gv)
    delta = -ADAM_LR * ((m2 / (1.0 - ADAM_B1 ** ADAM_STEP)) / (jnp.sqrt(v2 / (1.0 - ADAM_B2 ** ADAM_STEP)) + ADAM_EPS)
                        + ADAM_WD * wv)
    return delta, m2, v2


def _adamw_small(ws, gs, ms, vs):
    n = len(ws)

    def body(*refs):
        for i in range(n):
            res = _adamw_math(refs[i][...], refs[n + i][...], refs[2 * n + i][...], refs[3 * n + i][...])
            for k in range(3):
                refs[(4 + k) * n + i][...] = res[k]

    vm = pl.BlockSpec(memory_space=pltpu.VMEM)
    outs = pl.pallas_call(
        body, name="adamw_small", in_specs=[vm] * (4 * n), out_specs=[vm] * (3 * n),
        out_shape=[jax.ShapeDtypeStruct(a.shape, F32) for a in ws] * 3,
        compiler_params=pltpu.CompilerParams(vmem_limit_bytes=VMEM_LIMIT_BYTES),
    )(*ws, *gs, *ms, *vs)
    return outs[:n], outs[n:2 * n], outs[2 * n:]


PACK_ROWS = 256


def _pack(flat_parts, dtype, lead=()):
    parts = [a.astype(dtype).reshape(lead + (-1,)) for a in flat_parts]
    n = sum(a.shape[-1] for a in parts)
    chunk = PACK_ROWS * LANES
    total = -(-n // chunk) * chunk
    if total > n:
        parts.append(jnp.zeros(lead + (total - n,), dtype))
    return jnp.concatenate(parts, axis=-1).reshape(lead + (total // LANES, LANES))


def _unpack(buf, shapes, lead=()):
    flat = buf.reshape(lead + (-1,))
    out, off = [], 0
    for shp in shapes:
        n = math.prod(shp)
        out.append(lax.slice_in_dim(flat, off, off + n, axis=len(lead)).reshape(lead + tuple(shp)))
        off += n
    return out


BIG = ("w_in", "w_glu", "w_pa", "w_pb", "w_out", "w_up", "w_down")
WEIGHTS = ("g_mix", "w_in", "s5_a_re", "s5_a_im", "s5_log_dt", "s5_b_re", "s5_b_im", "s5_c_re", "s5_c_im", "s5_d",
           "w_glu", "b_glu", "hg_lb_logits", "hg_norm_gain", "w_pa", "w_pb", "w_out", "g_ffn", "w_up", "w_conv",
           "b_conv", "w_down", "g_final")
SMALL = tuple(n for n in WEIGHTS if n not in BIG)
SMALL_PARTS = ("loss", "g_ffn", "g_final", "b_glu", "gain", "lbrow", "s5_d", "w_conv", "b_conv", "lam_re", "lam_im",
               "bb_re", "bb_im", "s5_c_re", "s5_c_im")


def _lower_bound(logits):
    return jnp.cumsum(jax.nn.softmax(logits, axis=0), axis=0)[0:1]


def kernel(x, g_mix, w_in, s5_a_re, s5_a_im, s5_log_dt, s5_b_re, s5_b_im, s5_c_re, s5_c_im, s5_d, w_glu, b_glu, hg_lb_logits, hg_norm_gain, w_pa, w_pb, w_out, g_ffn, w_up, w_conv, b_conv, w_down, g_final, loss_target, m_g_mix, m_w_in, m_s5_a_re, m_s5_a_im, m_s5_log_dt, m_s5_b_re, m_s5_b_im, m_s5_c_re, m_s5_c_im, m_s5_d, m_w_glu, m_b_glu, m_hg_lb_logits, m_hg_norm_gain, m_w_pa, m_w_pb, m_w_out, m_g_ffn, m_w_up, m_w_conv, m_b_conv, m_w_down, m_g_final, v_g_mix, v_w_in, v_s5_a_re, v_s5_a_im, v_s5_log_dt, v_s5_b_re, v_s5_b_im, v_s5_c_re, v_s5_c_im, v_s5_d, v_w_glu, v_b_glu, v_hg_lb_logits, v_hg_norm_gain, v_w_pa, v_w_pb, v_w_out, v_g_ffn, v_w_up, v_w_conv, v_b_conv, v_w_down, v_g_final):
    args = dict(locals())
    w = {n: args[n] for n in WEIGHTS}
    mom = {n: args["m_" + n] for n in WEIGHTS}
    var = {n: args["v_" + n] for n in WEIGHTS}
    nseq, seq, d = x.shape
    xi, yi = lax.axis_index("x"), lax.axis_index("y")
    chip = 2 * xi + yi

    shard = {n: w[n][0] for n in BIG}
    first = [shard["w_in"].astype(BF16), shard["w_glu"].astype(BF16),
             jnp.pad(w_conv[0], ((0, 2 * SUBLANES - CONV_W), (0, 0)))]
    x2 = x.reshape(nseq * seq, d)
    u, z_own, late16, got = _prepare(x2, g_mix, first[0], [shard[n] for n in LATE], _gather_full_rider(first))
    w_in_all, w_glu_all, conv_all = [lax.dynamic_update_index_in_dim(g, s, chip, 0) for g, s in zip(got, first)]
    p = dict(g_mix=g_mix, g_ffn=g_ffn, g_final=g_final.reshape(1, -1), b_glu=b_glu, gain=hg_norm_gain, s5_d=s5_d,
             b_conv=b_conv, lbrow=_lower_bound(hg_lb_logits),
             s5_a_re=s5_a_re[0], s5_a_im=s5_a_im[0], s5_log_dt=s5_log_dt[0], s5_b_re=s5_b_re[0], s5_b_im=s5_b_im[0],
             s5_c_re=s5_c_re[0], s5_c_im=s5_c_im[0], w_in=w_in_all, w_glu=w_glu_all.reshape(-1, w_glu_all.shape[-1]),
             w_conv=conv_all[:, :CONV_W].transpose(1, 0, 2).reshape(CONV_W, -1))

    dx, halves, sm = _local_step(x2, loss_target.reshape(nseq * seq, d), u, z_own, p, dict(zip(LATE, late16)),
                                 nseq=nseq, seq=seq)
    loss = sm["loss"][0, 0]

    grads, delta, new_m, new_v = {}, {}, {}, {}
    for n in BIG:
        shp = shard[n].shape
        grads[n], delta[n], new_m[n], new_v[n] = _adamw_halves("adamw_" + n, shard[n], mom[n].reshape(shp),
                                                               var[n].reshape(shp), *halves[n])

    _, disc_vjp = jax.vjp(_s5_discretize, p["s5_a_re"], p["s5_a_im"], p["s5_log_dt"], p["s5_b_re"], p["s5_b_im"])
    da_re, da_im, dlog_dt, db_re, db_im = disc_vjp((sm["lam_re"], sm["lam_im"], sm["bb_re"], sm["bb_im"]))
    _, lb_vjp = jax.vjp(_lower_bound, hg_lb_logits)
    (dlogits,) = lb_vjp(sm["lbrow"])
    fcols = w_conv.shape[-1]
    grads.update(
        g_mix=sm["g_mix"], g_ffn=sm["g_ffn"], g_final=sm["g_final"].reshape(-1), b_glu=sm["b_glu"],
        hg_norm_gain=sm["gain"], hg_lb_logits=dlogits, s5_d=sm["s5_d"], b_conv=sm["b_conv"],
        w_conv=lax.dynamic_slice_in_dim(sm["w_conv"], chip * fcols, fcols, axis=1),
        s5_a_re=da_re, s5_a_im=da_im, s5_log_dt=dlog_dt, s5_b_re=db_re, s5_b_im=db_im,
        s5_c_re=sm["s5_c_re"], s5_c_im=sm["s5_c_im"])
    grads = {n: grads[n].reshape(w[n].shape) for n in WEIGHTS}

    def natural(a):
        return a.reshape(1, -1) if a.ndim == 1 else (a[0] if a.ndim > 2 else a)

    outs = _adamw_small(*[[natural(src[n]) for n in SMALL] for src in (w, grads, mom, var)])
    for dst, group in zip((delta, new_m, new_v), outs):
        dst.update(zip(SMALL, group))
    res = [loss, dx.reshape(x.shape)]
    for group in (grads, delta, new_m, new_v):
        res += [group[n].reshape(w[n].shape) for n in WEIGHTS]
    return tuple(res)
```

```python
import functools
import math

import jax
import jax.numpy as jnp
from jax import lax
from jax.experimental import pallas as pl
from jax.experimental.pallas import tpu as pltpu

F32 = jnp.float32
BF16 = jnp.bfloat16
MESH = pl.DeviceIdType.MESH

EPS = 1e-6
S5_GROUP = 16
S5_STATE = 64
S5_BLOCK_GROUPS = 8
HEAD = 128
CHUNK = 64
CONV_W = 3
LANES = 128
SUBLANES = 8
GATE_BLOCK = 512
VMEM_LIMIT_BYTES = 56 * 1024 * 1024

ADAM_LR = 0.001
ADAM_B1 = 0.9
ADAM_B2 = 0.999
ADAM_EPS = 1e-08
ADAM_WD = 0.01
ADAM_STEP = 10

N_CHIPS = 4
N_DEV = 8


def _params(*sem):
    return pltpu.CompilerParams(dimension_semantics=sem, vmem_limit_bytes=VMEM_LIMIT_BYTES)


class _Rider:
    def __init__(self, arrays, out_shapes, nsem, start, finish, aliases=None):
        self.arrays, self.out_shapes, self.nsem = list(arrays), list(out_shapes), nsem
        self.start, self.finish, self.aliases = start, finish, dict(aliases or {})


def _hosted_call(name, body, *, grid, in_specs, out_specs, out_shape, operands, scratch_shapes=(), rider=None):
    in_specs, out_specs, out_shape, scratch_shapes = list(in_specs), list(out_specs), list(out_shape), list(scratch_shapes)
    cparams = _params(*(["arbitrary"] * len(grid)))
    if rider is None:
        return pl.pallas_call(body, name=name, grid=grid, in_specs=in_specs, out_specs=out_specs, out_shape=out_shape,
                              scratch_shapes=scratch_shapes, compiler_params=cparams)(*operands)
    n_in, n_out, n_sc = len(in_specs), len(out_specs), len(scratch_shapes)
    r_in, r_out = len(rider.arrays), len(rider.out_shapes)

    def hosted(*refs):
        ins, rins = refs[:n_in], refs[n_in:n_in + r_in]
        outs = refs[n_in + r_in:n_in + r_in + n_out]
        routs = refs[n_in + r_in + n_out:n_in + r_in + n_out + r_out]
        rest = refs[n_in + r_in + n_out + r_out:]
        send_sems, recv_sems = rest[n_sc], rest[n_sc + 1]
        first = functools.reduce(jnp.logical_and, [pl.program_id(i) == 0 for i in range(len(grid))])
        last = functools.reduce(jnp.logical_and, [pl.program_id(i) == grid[i] - 1 for i in range(len(grid))])

        @pl.when(first)
        def _():
            rider.start(rins, routs, send_sems, recv_sems)

        body(*ins, *outs, *rest[:n_sc])

        @pl.when(last)
        def _():
            rider.finish(rins, routs, send_sems, recv_sems)

    res = pl.pallas_call(
        hosted, name=name, grid=grid, in_specs=in_specs + [ANY] * r_in, out_specs=out_specs + [ANY] * r_out,
        out_shape=out_shape + rider.out_shapes,
        scratch_shapes=scratch_shapes + [pltpu.SemaphoreType.DMA((rider.nsem,)), pltpu.SemaphoreType.DMA((rider.nsem,))],
        input_output_aliases={n_in + i: n_out + o for i, o in rider.aliases.items()}, compiler_params=cparams,
    )(*operands, *rider.arrays)
    return res[:n_out], res[n_out:]


def _hosted_scalar_call(name, body, *, grid, in_specs, out_specs, out_shape, operands, rider, aliases=None):
    in_specs, out_specs, out_shape = list(in_specs), list(out_specs), list(out_shape)
    n_in, n_out = len(in_specs), len(out_specs)
    if rider is None:
        spec = pltpu.PrefetchScalarGridSpec(num_scalar_prefetch=1, grid=grid, in_specs=in_specs, out_specs=out_specs)
        res = pl.pallas_call(body, name=name, grid_spec=spec, out_shape=out_shape,
                             input_output_aliases={1 + i: o for i, o in (aliases or {}).items()},
                             compiler_params=_params(*(["arbitrary"] * len(grid))))(_place_scalars(), *operands)
        return res, []
    r_in, r_out = len(rider.arrays), len(rider.out_shapes)

    def hosted(place, *refs):
        ins, rins = refs[:n_in], refs[n_in:n_in + r_in]
        outs = refs[n_in + r_in:n_in + r_in + n_out]
        routs = refs[n_in + r_in + n_out:n_in + r_in + n_out + r_out]
        send_sems, recv_sems = refs[-2], refs[-1]
        first = functools.reduce(jnp.logical_and, [pl.program_id(i) == 0 for i in range(len(grid))])
        last = functools.reduce(jnp.logical_and, [pl.program_id(i) == grid[i] - 1 for i in range(len(grid))])

        @pl.when(first)
        def _():
            rider.start(rins, routs, send_sems, recv_sems)

        body(place, *ins, *outs)

        @pl.when(last)
        def _():
            rider.finish(rins, routs, send_sems, recv_sems)

    spec = pltpu.PrefetchScalarGridSpec(
        num_scalar_prefetch=1, grid=grid, in_specs=in_specs + [ANY] * r_in, out_specs=out_specs + [ANY] * r_out,
        scratch_shapes=[pltpu.SemaphoreType.DMA((rider.nsem,)), pltpu.SemaphoreType.DMA((rider.nsem,))])
    alias = {1 + i: o for i, o in (aliases or {}).items()}
    alias.update({1 + n_in + i: n_out + o for i, o in rider.aliases.items()})
    res = pl.pallas_call(hosted, name=name, grid_spec=spec, out_shape=out_shape + rider.out_shapes,
                         input_output_aliases=alias, compiler_params=_params(*(["arbitrary"] * len(grid))),
                         )(_place_scalars(), *operands, *rider.arrays)
    return res[:n_out], res[n_out:]


def _run_rider(name, rider):
    r_in, r_out = len(rider.arrays), len(rider.out_shapes)

    def body(*refs):
        rins, routs, send_sems, recv_sems = refs[:r_in], refs[r_in:r_in + r_out], refs[-2], refs[-1]
        rider.start(rins, routs, send_sems, recv_sems)
        rider.finish(rins, routs, send_sems, recv_sems)

    return pl.pallas_call(
        body, name=name, in_specs=[ANY] * r_in, out_specs=[ANY] * r_out, out_shape=rider.out_shapes,
        scratch_shapes=[pltpu.SemaphoreType.DMA((rider.nsem,)), pltpu.SemaphoreType.DMA((rider.nsem,))],
        input_output_aliases=rider.aliases,
    )(*rider.arrays)


def _row_tile(rows, cap):
    if rows <= cap:
        return rows
    for t in range(cap - cap % 8, 7, -8):
        if rows % t == 0:
            return t
    raise ValueError(f"no row tile for {rows}")


def _dot(a, b):
    return jnp.dot(a.astype(BF16), b.astype(BF16), preferred_element_type=F32)


def _dot_nt(a, b):
    return lax.dot_general(a.astype(BF16), b.astype(BF16), (((1,), (1,)), ((), ())), preferred_element_type=F32)


def _dot_tn(a, b):
    return lax.dot_general(a.astype(BF16), b.astype(BF16), (((0,), (0,)), ((), ())), preferred_element_type=F32)


def _sigmoid(x):
    return 0.5 * jnp.tanh(0.5 * x) + 0.5


_GELU_C = math.sqrt(2.0 / math.pi)


def _gelu(x):
    return 0.5 * x * (1.0 + jnp.tanh(_GELU_C * (x + 0.044715 * x * x * x)))


def _gelu_grad(x):
    th = jnp.tanh(_GELU_C * (x + 0.044715 * x * x * x))
    return 0.5 * (1.0 + th) + 0.5 * x * (1.0 - th * th) * _GELU_C * (1.0 + 3.0 * 0.044715 * x * x)


def _rowwise(name, fn, ins, outs, accs=(), *, rows, tm, ncol=1, rider=None):
    n_in, n_out = len(ins), len(outs)

    def body(*refs):
        res = fn(*[r[...] for r in refs[:n_in]])
        for r, v in zip(refs[n_in:n_in + n_out], res[:n_out]):
            r[...] = v.astype(r.dtype)
        first = pl.program_id(1) == 0
        for r, v in zip(refs[n_in + n_out:], res[n_out:]):
            @pl.when(first)
            def _():
                r[...] = v

            @pl.when(jnp.logical_not(first))
            def _():
                r[...] += v

    in_specs = []
    for _, width, base, kind in ins:
        if kind == "row":
            in_specs.append(pl.BlockSpec((tm, width), lambda j, i, b=base: (i, b + j)))
        else:
            in_specs.append(pl.BlockSpec((1, width), lambda j, i, b=base: (0, b + j)))
    out_specs = [pl.BlockSpec((tm, width), lambda j, i: (i, j)) for _, width, _ in outs]
    out_specs += [pl.BlockSpec((1, width), lambda j, i: (0, j)) for _, width in accs]
    out_shape = [jax.ShapeDtypeStruct((rows, total), dt) for total, _, dt in outs]
    out_shape += [jax.ShapeDtypeStruct((1, total), F32) for total, _ in accs]
    return _hosted_call(name, body, grid=(ncol, rows // tm), in_specs=in_specs, out_specs=out_specs, out_shape=out_shape,
                        operands=[a for a, _, _, _ in ins], rider=rider)


def _mm(name, a, b, *, mode, grid, a_spec, b_spec, o_spec, out_shape, acc_shape, res=None, res_spec=None,
        pair_axis=None, rider=None, epilogue=None):
    nk = grid[2]
    dot = {"nn": _dot, "nt": _dot_nt, "tn": _dot_tn}[mode]
    a_list = list(a) if isinstance(a, tuple) else [a]
    b_list = list(b) if isinstance(b, tuple) else [b]
    na, nb = len(a_list), len(b_list)
    assert (pair_axis is None) == (na + nb == 2)
    direct = nk == 1 and pair_axis is None
    epi_fn, epi_ins, epi_sums = epilogue if epilogue is not None else (None, [], [])
    n_res = 0 if res is None else 1
    n_epi = len(epi_ins)

    def body(*refs):
        a_refs, b_refs = refs[:na], refs[na:na + nb]
        r_ref = None if res is None else refs[na + nb]
        e_refs = refs[na + nb + n_res:na + nb + n_res + n_epi]
        o_ref = refs[na + nb + n_res + n_epi]
        s_refs = refs[na + nb + n_res + n_epi + 1:na + nb + n_res + n_epi + 1 + len(epi_sums)]
        first_rows = pl.program_id(0) == 0

        def finish(v):
            if res is not None:
                v = v + r_ref[...]
            if epi_fn is None:
                o_ref[...] = v.astype(o_ref.dtype)
                return
            outs = epi_fn(v, *[r[...] for r in e_refs])
            o_ref[...] = outs[0].astype(o_ref.dtype)
            for s_ref, part in zip(s_refs, outs[1:]):
                @pl.when(first_rows)
                def _():
                    s_ref[...] = part

                @pl.when(jnp.logical_not(first_rows))
                def _():
                    s_ref[...] += part

        if direct:
            finish(dot(a_refs[0][...], b_refs[0][...]))
            return
        acc_ref = refs[-1]
        k = pl.program_id(2)

        @pl.when(k == 0)
        def _():
            acc_ref[...] = jnp.zeros_like(acc_ref)

        if pair_axis is None:
            acc_ref[...] += dot(a_refs[0][...], b_refs[0][...])
        else:
            lower = pl.program_id(pair_axis) < grid[pair_axis] // 2

            @pl.when(lower)
            def _():
                acc_ref[...] += dot(a_refs[0][...], b_refs[0][...])

            @pl.when(jnp.logical_not(lower))
            def _():
                acc_ref[...] += dot(a_refs[-1][...], b_refs[-1][...])

        @pl.when(k == nk - 1)
        def _():
            finish(acc_ref[...])

    operands = a_list + b_list + ([] if res is None else [res]) + [arr for arr, _ in epi_ins]
    in_specs = (list(a_spec) if na == 2 else [a_spec]) + (list(b_spec) if nb == 2 else [b_spec])
    in_specs += ([] if res is None else [res_spec]) + [spec for _, spec in epi_ins]
    out_specs = [o_spec] + [pl.BlockSpec((1, c), lambda *_: (0, 0)) for c in epi_sums]
    out_shapes = [out_shape] + [jax.ShapeDtypeStruct((1, c), F32) for c in epi_sums]
    got = _hosted_call(name, body, grid=grid, in_specs=in_specs, out_specs=out_specs, out_shape=out_shapes,
                       scratch_shapes=[] if direct else [pltpu.VMEM(acc_shape, F32)], operands=operands, rider=rider)
    mine, rider_outs = (got, None) if rider is None else got
    mine = mine[0] if epilogue is None else tuple(mine)
    return mine if rider is None else (mine, rider_outs)


MM_TILE_BUDGET_BYTES = 36 * 1024 * 1024
MM_TILE_CAP = 2048
ROW_TILE = 1024
GLU_TILE = 1024
ADAMW_TILE = 256


def _mm_tile(t, row_bytes, fixed_bytes):
    cap = max(16, min(MM_TILE_CAP, (MM_TILE_BUDGET_BYTES - fixed_bytes) // row_bytes))
    return _row_tile(t, cap - cap % 16)


def _size(a):
    return jnp.dtype(a.dtype).itemsize


def _mm_fwd_cols(name, a, w3, out_dtype=F32, rider=None):
    t, k = a.shape
    ns = w3.shape[2]
    tm = _mm_tile(t, 2 * k * _size(a) + 2 * ns * jnp.dtype(out_dtype).itemsize, 2 * k * ns * _size(w3))
    return _mm(name, a, w3, mode="nn", grid=(N_CHIPS, t // tm, 1),
               a_spec=pl.BlockSpec((tm, k), lambda j, i, kk: (i, 0)),
               b_spec=pl.BlockSpec((None, k, ns), lambda j, i, kk: (j, 0, 0)),
               o_spec=pl.BlockSpec((tm, ns), lambda j, i, kk: (i, j)),
               out_shape=jax.ShapeDtypeStruct((t, N_CHIPS * ns), out_dtype), acc_shape=(tm, ns), rider=rider)


def _mm_bwd_cols(name, d, w3, out_dtype=F32, rider=None, epilogue=None):
    pair = isinstance(d, tuple)
    t = d[0].shape[0] if pair else d.shape[0]
    k, ns = w3.shape[1], w3.shape[2]
    dsize = _size(d[0] if pair else d)
    tm = _mm_tile(t, (4 if pair else 2) * ns * dsize + 2 * k * jnp.dtype(out_dtype).itemsize + 4 * k
                  + _row_epilogue(epilogue, 8)[1], 2 * k * ns * _size(w3))
    half = N_CHIPS // 2
    if pair:
        a_spec = (pl.BlockSpec((tm, ns), lambda i, j, kk: (i, jnp.minimum(kk, half - 1))),
                  pl.BlockSpec((tm, ns), lambda i, j, kk: (i, jnp.maximum(kk - half, 0))))
    else:
        a_spec = pl.BlockSpec((tm, ns), lambda i, j, kk: (i, kk))
    return _mm(name, d, w3, mode="nt", grid=(t // tm, 1, N_CHIPS), a_spec=a_spec,
               b_spec=pl.BlockSpec((None, k, ns), lambda i, j, kk: (kk, 0, 0)),
               o_spec=pl.BlockSpec((tm, k), lambda i, j, kk: (i, 0)),
               out_shape=jax.ShapeDtypeStruct((t, k), out_dtype), acc_shape=(tm, k), pair_axis=2 if pair else None,
               rider=rider, epilogue=_row_epilogue(epilogue, tm)[0])


def _mm_wgrad_cols(name, a, d, rider=None):
    pair = isinstance(d, tuple)
    t, k = a.shape
    ns = (2 * d[0].shape[1] if pair else d.shape[1]) // N_CHIPS
    dsize = _size(d[0] if pair else d)
    tk = _mm_tile(t, 2 * k * _size(a) + (4 if pair else 2) * ns * dsize, k * ns * (4 + 2 * 2))
    half = N_CHIPS // 2
    if pair:
        b_spec = (pl.BlockSpec((tk, ns), lambda j, i, kk: (jnp.where(j < half, kk, 0), jnp.minimum(j, half - 1))),
                  pl.BlockSpec((tk, ns), lambda j, i, kk: (jnp.where(j < half, 0, kk), jnp.maximum(j - half, 0))))
    else:
        b_spec = pl.BlockSpec((tk, ns), lambda j, i, kk: (kk, j))
    return _mm(name, a, d, mode="tn", grid=(N_CHIPS, 1, t // tk),
               a_spec=pl.BlockSpec((tk, k), lambda j, i, kk: (kk, 0)), b_spec=b_spec,
               o_spec=pl.BlockSpec((None, k, ns), lambda j, i, kk: (j, 0, 0)),
               out_shape=jax.ShapeDtypeStruct((N_CHIPS, k, ns), BF16), acc_shape=(k, ns),
               pair_axis=0 if pair else None, rider=rider)


MM_BLOCK_CAP = 1408


def _row_epilogue(epilogue, tm):
    if epilogue is None:
        return None, 0
    fn, arrays, sums = epilogue
    specs = [pl.BlockSpec((1, x.shape[1]), lambda i, j, kk: (0, 0)) if x.shape[0] == 1 else
             pl.BlockSpec((tm, x.shape[1]), lambda i, j, kk: (i, 0)) for x in arrays]
    return (fn, list(zip(arrays, specs)), list(sums)), sum(2 * x.shape[1] * _size(x) for x in arrays if x.shape[0] > 1)


def _mm_fwd_rows(name, a, w, res=None, out_dtype=F32, epilogue=None, rider=None):
    t, k = a.shape
    n = w.shape[1]
    tk = k if k <= MM_BLOCK_CAP else MM_BLOCK_CAP
    assert k % tk == 0
    row_bytes = 2 * tk * _size(a) + 2 * n * jnp.dtype(out_dtype).itemsize + (0 if res is None else 2 * n * 4) + 4 * n
    row_bytes += _row_epilogue(epilogue, 8)[1]
    tm = _mm_tile(t, row_bytes, 2 * tk * n * _size(w))
    return _mm(name, a, w, mode="nn", grid=(t // tm, 1, k // tk),
               a_spec=pl.BlockSpec((tm, tk), lambda i, j, kk: (i, kk)),
               b_spec=pl.BlockSpec((tk, n), lambda i, j, kk: (kk, 0)),
               o_spec=pl.BlockSpec((tm, n), lambda i, j, kk: (i, 0)),
               out_shape=jax.ShapeDtypeStruct((t, n), out_dtype), acc_shape=(tm, n),
               res=res, res_spec=None if res is None else pl.BlockSpec((tm, n), lambda i, j, kk: (i, 0)),
               epilogue=_row_epilogue(epilogue, tm)[0], rider=rider)


def _mm_bwd_rows(name, d, w, out_dtype=F32):
    t, n = d.shape
    k = w.shape[0]
    tn = k if k <= MM_BLOCK_CAP else MM_BLOCK_CAP
    assert k % tn == 0
    tm = _mm_tile(t, 2 * n * _size(d) + 2 * tn * jnp.dtype(out_dtype).itemsize, 2 * tn * n * _size(w))
    return _mm(name, d, w, mode="nt", grid=(t // tm, k // tn, 1),
               a_spec=pl.BlockSpec((tm, n), lambda i, j, kk: (i, 0)),
               b_spec=pl.BlockSpec((tn, n), lambda i, j, kk: (j, 0)),
               o_spec=pl.BlockSpec((tm, tn), lambda i, j, kk: (i, j)),
               out_shape=jax.ShapeDtypeStruct((t, k), out_dtype), acc_shape=(tm, tn))


def _mm_wgrad_rows(name, a, d):
    t, k = a.shape
    n = d.shape[1]
    nblk = next(b for b in (1, 2, 4) if (k // b) % LANES == 0 and k // b <= MM_BLOCK_CAP)
    ks = k // nblk
    tk = _mm_tile(t, 2 * ks * _size(a) + 2 * n * _size(d), ks * n * (4 + 2 * 2))
    return _mm(name, a, d, mode="tn", grid=(nblk, 1, t // tk),
               a_spec=pl.BlockSpec((tk, ks), lambda j, i, kk: (kk, j)),
               b_spec=pl.BlockSpec((tk, n), lambda j, i, kk: (kk, 0)),
               o_spec=pl.BlockSpec((ks, n), lambda j, i, kk: (j, 0)),
               out_shape=jax.ShapeDtypeStruct((k, n), BF16), acc_shape=(ks, n))


def _s5_discretize(a_re, a_im, log_dt, b_re, b_im):
    dt = jnp.exp(log_dt)[:, None]
    mag = jnp.exp(a_re * dt)
    ang = a_im * dt
    lb_re = mag * jnp.cos(ang)
    lb_im = mag * jnp.sin(ang)
    den = a_re * a_re + a_im * a_im
    n_re = lb_re - 1.0
    n_im = lb_im
    co_re = ((n_re * a_re + n_im * a_im) / den)[..., None]
    co_im = ((n_im * a_re - n_re * a_im) / den)[..., None]
    bb_re = co_re * b_re - co_im * b_im
    bb_im = co_re * b_im + co_im * b_re
    return lb_re, lb_im, bb_re, bb_im


def _s5_in_blocks(bb):
    g = bb.shape[0]
    nb = g // S5_BLOCK_GROUPS
    t = bb.reshape(nb, S5_BLOCK_GROUPS, S5_STATE, S5_GROUP).transpose(0, 1, 3, 2)
    eye = jnp.eye(S5_BLOCK_GROUPS, dtype=bb.dtype)
    full = t[:, :, :, None, :] * eye[None, :, None, :, None]
    return full.reshape(nb, S5_BLOCK_GROUPS * S5_GROUP, S5_BLOCK_GROUPS * S5_STATE)


def _s5_in_blocks_diag(blocks):
    nb = blocks.shape[0]
    t = blocks.reshape(nb, S5_BLOCK_GROUPS, S5_GROUP, S5_BLOCK_GROUPS, S5_STATE)
    d = jnp.einsum("bghgp->bghp", t)
    return d.transpose(0, 1, 3, 2).reshape(nb * S5_BLOCK_GROUPS, S5_STATE, S5_GROUP)


def _s5_out_blocks(c):
    g = c.shape[0]
    nb = g // S5_BLOCK_GROUPS
    t = c.reshape(nb, S5_BLOCK_GROUPS, S5_GROUP, S5_STATE).transpose(0, 1, 3, 2)
    eye = jnp.eye(S5_BLOCK_GROUPS, dtype=c.dtype)
    full = t[:, :, :, None, :] * eye[None, :, None, :, None]
    return full.reshape(nb, S5_BLOCK_GROUPS * S5_STATE, S5_BLOCK_GROUPS * S5_GROUP)


def _s5_out_blocks_diag(blocks):
    nb = blocks.shape[0]
    t = blocks.reshape(nb, S5_BLOCK_GROUPS, S5_STATE, S5_BLOCK_GROUPS, S5_GROUP)
    d = jnp.einsum("bgpgh->bgph", t)
    return d.transpose(0, 1, 3, 2).reshape(nb * S5_BLOCK_GROUPS, S5_GROUP, S5_STATE)


def _s5_scan_tables(lr, li, reverse):
    def cmul(a, b):
        return a[0] * b[0] - a[1] * b[1], a[0] * b[1] + a[1] * b[0]

    lam = (lr, -li) if reverse else (lr, li)
    pw = [lam]
    for _ in range(SUBLANES - 1):
        pw.append(cmul(pw[-1], lam))
    sub = jnp.arange(SUBLANES)[:, None]
    rows = []
    for s in (1, 2, 4):
        keep = (sub <= SUBLANES - 1 - s) if reverse else (sub >= s)
        rows.append(jnp.where(keep, pw[s - 1][0][None, :], 0.0))
        rows.append(jnp.where(keep, pw[s - 1][1][None, :], 0.0))
    order = list(range(SUBLANES - 1, -1, -1)) if reverse else list(range(SUBLANES))
    rows.append(jnp.stack([pw[i][0] for i in order]))
    rows.append(jnp.stack([pw[i][1] for i in order]))
    return jnp.concatenate(rows, axis=0)


def _s5_scan(vre_ref, vim_ref, coef_ref, seq, width, reverse, xre_ref=None, xim_ref=None):
    nt = seq // SUBLANES
    nl = width // LANES
    per = 2 if xre_ref is None else 4
    sub = lax.broadcasted_iota(jnp.int32, (SUBLANES, LANES), 0)

    def step(k, carry):
        kk = (nt - 1 - k) if reverse else k
        rows = pl.ds(pl.multiple_of(kk * SUBLANES, SUBLANES), SUBLANES)
        out = []
        for j in range(nl):
            lanes = slice(j * LANES, (j + 1) * LANES)
            co = [coef_ref[SUBLANES * q:SUBLANES * (q + 1), lanes] for q in range(8)]
            cr, ci = carry[per * j], carry[per * j + 1]
            vr = vre_ref[rows, lanes]
            vi = vim_ref[rows, lanes]
            for q, s in enumerate((1, 2, 4)):
                sh = SUBLANES - s if reverse else s
                rr = pltpu.roll(vr, sh, 0)
                ri = pltpu.roll(vi, sh, 0)
                ar, ai = co[2 * q], co[2 * q + 1]
                vr, vi = vr + ar * rr - ai * ri, vi + ar * ri + ai * rr
            edge = 0 if reverse else SUBLANES - 1
            cbr = jnp.broadcast_to(cr[edge:edge + 1, :], (SUBLANES, LANES))
            cbi = jnp.broadcast_to(ci[edge:edge + 1, :], (SUBLANES, LANES))
            pr, pi = co[6], co[7]
            vr, vi = vr + pr * cbr - pi * cbi, vi + pr * cbi + pi * cbr
            vre_ref[rows, lanes] = vr
            vim_ref[rows, lanes] = vi
            out += [vr, vi]
            if xre_ref is not None:
                nr = jnp.where(sub == SUBLANES - 1, cbr, pltpu.roll(vr, SUBLANES - 1, 0))
                ni = jnp.where(sub == SUBLANES - 1, cbi, pltpu.roll(vi, SUBLANES - 1, 0))
                xr = xre_ref[rows, lanes]
                xi = xim_ref[rows, lanes]
                out += [carry[per * j + 2] + nr * xr + ni * xi, carry[per * j + 3] + ni * xr - nr * xi]
        return tuple(out)

    zero = jnp.zeros((SUBLANES, LANES), F32)
    res = lax.fori_loop(0, nt, step, (zero,) * (per * nl))
    if xre_ref is None:
        return None
    return jnp.concatenate(
        [jnp.concatenate([jnp.sum(res[per * j + 2], axis=0, keepdims=True) for j in range(nl)], axis=1),
         jnp.concatenate([jnp.sum(res[per * j + 3], axis=0, keepdims=True) for j in range(nl)], axis=1)], axis=0)


def _s5_fwd(z, bre3, bim3, cre3, cim3, coef, dskip, *, nseq, seq, rider=None):
    nb = bre3.shape[0]
    ch, ns = bre3.shape[1], bre3.shape[2]

    def body(za_ref, bre_ref, bim_ref, cre_ref, cim_ref, coef_ref, d_ref, y_ref, xre_ref, xim_ref):
        za = za_ref[...]
        xre_ref[...] = _dot(za, bre_ref[...])
        xim_ref[...] = _dot(za, bim_ref[...])
        _s5_scan(xre_ref, xim_ref, coef_ref, seq, ns, False)
        y_ref[...] = _dot(xre_ref[...], cre_ref[...]) - _dot(xim_ref[...], cim_ref[...]) + d_ref[...] * za

    blk3 = lambda r, c: pl.BlockSpec((None, r, c), lambda b, j: (j, 0, 0))
    return _hosted_call(
        "s5_fwd", body, grid=(nseq, nb),
        in_specs=[pl.BlockSpec((seq, ch), lambda b, j: (b, j)), blk3(ch, ns), blk3(ch, ns), blk3(ns, ch), blk3(ns, ch),
                  pl.BlockSpec((8 * SUBLANES, ns), lambda b, j: (0, j)), pl.BlockSpec((1, ch), lambda b, j: (0, j))],
        out_specs=[pl.BlockSpec((seq, ch), lambda b, j: (b, j)), pl.BlockSpec((seq, ns), lambda b, j: (b, j)),
                   pl.BlockSpec((seq, ns), lambda b, j: (b, j))],
        out_shape=[jax.ShapeDtypeStruct((nseq * seq, nb * ch), F32), jax.ShapeDtypeStruct((nseq * seq, nb * ns), F32),
                   jax.ShapeDtypeStruct((nseq * seq, nb * ns), F32)],
        operands=(z, bre3, bim3, cre3, cim3, coef, dskip), rider=rider)


def _s5_bwd(dy, z, xre, xim, bre3, bim3, cre3, cim3, coef_rev, dskip, *, nseq, seq, rider=None):
    nb = bre3.shape[0]
    ch, ns = bre3.shape[1], bre3.shape[2]

    def body(dy_ref, za_ref, xre_ref, xim_ref, bre_ref, bim_ref, cre_ref, cim_ref, coef_ref, d_ref,
             dza_ref, dbre_ref, dbim_ref, dcre_ref, dcim_ref, dlam_ref, dd_ref, are_ref, aim_ref):
        dy = dy_ref[...]
        za = za_ref[...]
        are_ref[...] = _dot_nt(dy, cre_ref[...])
        aim_ref[...] = -_dot_nt(dy, cim_ref[...])
        dlam = _s5_scan(are_ref, aim_ref, coef_ref, seq, ns, True, xre_ref, xim_ref)
        are = are_ref[...]
        aim = aim_ref[...]
        dza_ref[...] = (_dot_nt(are, bre_ref[...]) + _dot_nt(aim, bim_ref[...]) + d_ref[...] * dy).astype(dza_ref.dtype)
        parts = (_dot_tn(za, are), _dot_tn(za, aim), _dot_tn(xre_ref[...], dy), -_dot_tn(xim_ref[...], dy),
                 dlam, jnp.sum(dy * za, axis=0, keepdims=True))
        first = pl.program_id(1) == 0
        for r, v in zip((dbre_ref, dbim_ref, dcre_ref, dcim_ref, dlam_ref, dd_ref), parts):
            @pl.when(first)
            def _():
                r[...] = v

            @pl.when(jnp.logical_not(first))
            def _():
                r[...] += v

    blk3 = lambda r, c: pl.BlockSpec((None, r, c), lambda j, b: (j, 0, 0))
    tok = lambda c: pl.BlockSpec((seq, c), lambda j, b: (b, j))
    return _hosted_call(
        "s5_bwd", body, grid=(nb, nseq),
        in_specs=[tok(ch), tok(ch), tok(ns), tok(ns), blk3(ch, ns), blk3(ch, ns), blk3(ns, ch), blk3(ns, ch),
                  pl.BlockSpec((8 * SUBLANES, ns), lambda j, b: (0, j)), pl.BlockSpec((1, ch), lambda j, b: (0, j))],
        out_specs=[tok(ch), blk3(ch, ns), blk3(ch, ns), blk3(ns, ch), blk3(ns, ch),
                   pl.BlockSpec((None, 2, ns), lambda j, b: (j, 0, 0)), pl.BlockSpec((1, ch), lambda j, b: (0, j))],
        out_shape=[jax.ShapeDtypeStruct((nseq * seq, nb * ch), BF16),
                   jax.ShapeDtypeStruct((nb, ch, ns), F32), jax.ShapeDtypeStruct((nb, ch, ns), F32),
                   jax.ShapeDtypeStruct((nb, ns, ch), F32), jax.ShapeDtypeStruct((nb, ns, ch), F32),
                   jax.ShapeDtypeStruct((nb, 2, ns), F32), jax.ShapeDtypeStruct((1, nb * ch), F32)],
        scratch_shapes=[pltpu.VMEM((seq, ns), F32), pltpu.VMEM((seq, ns), F32)],
        operands=(dy, z, xre, xim, bre3, bim3, cre3, cim3, coef_rev, dskip), rider=rider)


def _glu_fwd(y, wglu, bglu):
    t, w = y.shape
    tm = _row_tile(t, GLU_TILE)

    def body(y_ref, w_ref, b_ref, a0_ref, gl_ref, a_ref):
        a0 = _gelu(y_ref[...])
        gl = _dot(a0, w_ref[...])
        a0_ref[...] = a0.astype(a0_ref.dtype)
        gl_ref[...] = gl
        a_ref[...] = (a0 * _sigmoid(gl + b_ref[...])).astype(a_ref.dtype)

    tok = pl.BlockSpec((tm, w), lambda i: (i, 0))
    return pl.pallas_call(
        body, name="s5_glu", grid=(t // tm,),
        in_specs=[tok, pl.BlockSpec((w, w), lambda i: (0, 0)), pl.BlockSpec((1, w), lambda i: (0, 0))],
        out_specs=[tok, tok, tok],
        out_shape=[jax.ShapeDtypeStruct((t, w), BF16), jax.ShapeDtypeStruct((t, w), F32), jax.ShapeDtypeStruct((t, w), BF16)],
        compiler_params=_params("arbitrary"),
    )(y, wglu, bglu)


def _glu_bwd(y, gl, bglu, da, wglu):
    t, w = y.shape
    tm = _row_tile(t, GLU_TILE)

    def body(y_ref, gl_ref, b_ref, da_ref, w_ref, dgl_ref, dy_ref, db_ref):
        yv = y_ref[...]
        dav = da_ref[...]
        s = _sigmoid(gl_ref[...] + b_ref[...])
        dgl = dav * _gelu(yv) * s * (1.0 - s)
        dgl_ref[...] = dgl.astype(dgl_ref.dtype)
        dy_ref[...] = (dav * s + _dot_nt(dgl, w_ref[...])) * _gelu_grad(yv)
        part = jnp.sum(dgl, axis=0, keepdims=True)
        first = pl.program_id(0) == 0

        @pl.when(first)
        def _():
            db_ref[...] = part

        @pl.when(jnp.logical_not(first))
        def _():
            db_ref[...] += part

    tok = pl.BlockSpec((tm, w), lambda i: (i, 0))
    vec = pl.BlockSpec((1, w), lambda i: (0, 0))
    return pl.pallas_call(
        body, name="s5_glu_bwd", grid=(t // tm,),
        in_specs=[tok, tok, vec, tok, pl.BlockSpec((w, w), lambda i: (0, 0))], out_specs=[tok, tok, vec],
        out_shape=[jax.ShapeDtypeStruct((t, w), BF16), jax.ShapeDtypeStruct((t, w), F32), jax.ShapeDtypeStruct((1, w), F32)],
        compiler_params=_params("arbitrary"),
    )(y, gl, bglu, da, wglu)


def _cumsum_rows(x, reverse=False):
    n = x.shape[0]
    row = lax.broadcasted_iota(jnp.int32, x.shape, 0)
    s = 1
    while s < n:
        if reverse:
            x = x + jnp.where(row < n - s, pltpu.roll(x, n - s, 0), 0.0)
        else:
            x = x + jnp.where(row >= s, pltpu.roll(x, s, 0), 0.0)
        s *= 2
    return x


def _hg_gates(zq, zf, lb):
    sg = _sigmoid(zf)
    f = lb + (1.0 - lb) * sg
    sq = _sigmoid(zq)
    qa = zq * sq * (HEAD ** -0.5)
    b = _cumsum_rows(jnp.log(f))
    return sg, f, sq, qa, 1.0 - f, b


SUB = 16


def _hg_scores(qa, kk, b):
    c = qa.shape[0]
    row = lax.broadcasted_iota(jnp.int32, qa.shape, 0)
    pos = jnp.bitwise_and(row, SUB - 1)
    dmat = lax.broadcasted_iota(jnp.int32, (c, c), 0) - lax.broadcasted_iota(jnp.int32, (c, c), 1)
    p = jnp.zeros((c, c), F32)
    for d in range(SUB):
        if d == 0:
            fd = qa * kk
        else:
            e = jnp.exp(jnp.minimum(b - pltpu.roll(b, d, 0), 0.0))
            fd = jnp.where(pos >= d, qa * pltpu.roll(kk, d, 0) * e, 0.0)
        p = jnp.where(dmat == d, jnp.sum(fd, axis=1, keepdims=True), p)
    col = lax.broadcasted_iota(jnp.int32, (SUB, c), 1)
    blocks = [jnp.zeros((SUB, c), F32)]
    for r0 in range(SUB, c, SUB):
        beta = b[r0 - 1:r0, :]
        qt = qa[r0:r0 + SUB] * jnp.exp(b[r0:r0 + SUB] - beta)
        kt = kk * jnp.exp(jnp.minimum(beta - b, 0.0))
        blocks.append(jnp.where(col < r0, _dot_nt(qt, kt), 0.0))
    return p + jnp.concatenate(blocks, axis=0)


def _hg_scores_bwd(dp, qa, kk, b):
    c = qa.shape[0]
    row = lax.broadcasted_iota(jnp.int32, qa.shape, 0)
    pos = jnp.bitwise_and(row, SUB - 1)
    dmat = lax.broadcasted_iota(jnp.int32, (c, c), 0) - lax.broadcasted_iota(jnp.int32, (c, c), 1)
    dqa = jnp.zeros_like(qa)
    dkk = jnp.zeros_like(qa)
    db = jnp.zeros_like(qa)
    for d in range(SUB):
        dcol = jnp.sum(jnp.where(dmat == d, dp, 0.0), axis=1, keepdims=True)
        if d == 0:
            dqa = dqa + dcol * kk
            dkk = dkk + dcol * qa
        else:
            e = jnp.exp(jnp.minimum(b - pltpu.roll(b, d, 0), 0.0))
            w = jnp.where(pos >= d, dcol * e, 0.0)
            kr = pltpu.roll(kk, d, 0)
            dqa = dqa + w * kr
            tmp = w * qa
            dkk = dkk + pltpu.roll(tmp, c - d, 0)
            x = tmp * kr
            db = db + x - pltpu.roll(x, c - d, 0)
    col = lax.broadcasted_iota(jnp.int32, (SUB, c), 1)
    dq_blocks = [jnp.zeros((SUB, qa.shape[1]), F32)]
    db_blocks = [jnp.zeros((SUB, qa.shape[1]), F32)]
    for r0 in range(SUB, c, SUB):
        beta = b[r0 - 1:r0, :]
        eq = jnp.exp(b[r0:r0 + SUB] - beta)
        ek = jnp.exp(jnp.minimum(beta - b, 0.0))
        qt = qa[r0:r0 + SUB] * eq
        kt = kk * ek
        dpi = jnp.where(col < r0, dp[r0:r0 + SUB, :], 0.0)
        dqt = _dot(dpi, kt)
        dkt = _dot_tn(dpi, qt)
        dq_blocks.append(dqt * eq)
        db_blocks.append(dqt * qt)
        dkk = dkk + dkt * ek
        db = db - dkt * kt
    return dqa + jnp.concatenate(dq_blocks, axis=0), dkk, db + jnp.concatenate(db_blocks, axis=0)


def _hg_chunks_per_step(seq):
    nc = seq // CHUNK
    cps = next(k for k in (4, 2, 1) if nc % k == 0)
    return nc, cps, nc // cps


def _hg_fwd(z, lbrow, gain, *, nseq, seq, heads, qoff, rider=None):
    nc, cps, nblk = _hg_chunks_per_step(seq)
    blk = cps * CHUNK
    zspec = lambda off: pl.BlockSpec((blk, HEAD), lambda h, b, n, off=off: (b * nblk + n, off + h))

    def body(zq_ref, zf_ref, zi_ref, zg_ref, lb_ref, gn_ref, o_ref, yb_ref, st_ref, sc_ref, state):
        @pl.when(pl.program_id(2) == 0)
        def _():
            state[...] = jnp.zeros_like(state)

        lb = lb_ref[...]
        gain_v = gn_ref[...]

        def chunk(ci, carry):
            rows = pl.ds(pl.multiple_of(ci * CHUNK, CHUNK), CHUNK)
            st = state[...]
            st_ref[ci] = st
            zi = zi_ref[rows, :]
            zg = zg_ref[rows, :]
            _, _, _, qa, kk, b = _hg_gates(zq_ref[rows, :], zf_ref[rows, :], lb)
            scores = _hg_scores(qa, kk, b).astype(BF16)
            sc_ref[rows, :] = scores
            o = _dot_nt(qa * jnp.exp(b), st) + _dot(scores, zi)
            bl = b[CHUNK - 1:CHUNK, :]
            state[...] = st * jnp.exp(bl) + _dot_tn(zi, kk * jnp.exp(bl - b))
            o_ref[rows, :] = o
            r = lax.rsqrt(jnp.mean(o * o, axis=1, keepdims=True) + EPS)
            yb_ref[rows, :] = (o * r * gain_v * zg * _sigmoid(zg)).astype(yb_ref.dtype)
            return carry

        lax.fori_loop(0, cps, chunk, 0, unroll=True)

    tok = pl.BlockSpec((blk, HEAD), lambda h, b, n: (b * nblk + n, h))
    vec = pl.BlockSpec((1, HEAD), lambda h, b, n: (0, h))
    rows = nseq * seq
    return _hosted_call(
        "hgrn2_fwd", body, grid=(heads, nseq, nblk),
        in_specs=[zspec(qoff), zspec(qoff + heads), zspec(qoff + 2 * heads), zspec(qoff + 3 * heads), vec, vec],
        out_specs=[tok, tok, pl.BlockSpec((None, None, cps, HEAD, HEAD), lambda h, b, n: (h, b, n, 0, 0)),
                   pl.BlockSpec((None, blk, CHUNK), lambda h, b, n: (h, b * nblk + n, 0))],
        out_shape=[jax.ShapeDtypeStruct((rows, heads * HEAD), F32), jax.ShapeDtypeStruct((rows, heads * HEAD), BF16),
                   jax.ShapeDtypeStruct((heads, nseq, nc, HEAD, HEAD), F32),
                   jax.ShapeDtypeStruct((heads, rows, CHUNK), BF16)],
        scratch_shapes=[pltpu.VMEM((HEAD, HEAD), F32)], operands=(z, z, z, z, lbrow, gain), rider=rider)


def _hg_bwd(dyb, z, o, states, scores, lbrow, gain, *, nseq, seq, heads, qoff, rider=None):
    nc, cps, nblk = _hg_chunks_per_step(seq)
    blk = cps * CHUNK
    rev = lambda n: nblk - 1 - n
    zspec = lambda off: pl.BlockSpec((blk, HEAD), lambda h, b, n, off=off: (b * nblk + rev(n), off + h))

    def body(dyb_ref, zq_ref, zf_ref, zi_ref, zg_ref, o_ref, st_ref, sc_ref, lb_ref, gn_ref,
             dzq_ref, dzf_ref, dzi_ref, dzg_ref, dlb_ref, dgn_ref, dstate):
        @pl.when(pl.program_id(2) == 0)
        def _():
            dstate[...] = jnp.zeros_like(dstate)

        @pl.when(jnp.logical_and(pl.program_id(1) == 0, pl.program_id(2) == 0))
        def _():
            dlb_ref[...] = jnp.zeros_like(dlb_ref)
            dgn_ref[...] = jnp.zeros_like(dgn_ref)

        lb = lb_ref[...]
        gain_v = gn_ref[...]
        c = CHUNK
        causal = lax.broadcasted_iota(jnp.int32, (c, c), 0) >= lax.broadcasted_iota(jnp.int32, (c, c), 1)

        def chunk(step, carry):
            ci = cps - 1 - step
            rows = pl.ds(pl.multiple_of(ci * CHUNK, CHUNK), CHUNK)
            zq = zq_ref[rows, :]
            zi = zi_ref[rows, :]
            zg = zg_ref[rows, :]
            sg, f, sq, qa, kk, b = _hg_gates(zq, zf_ref[rows, :], lb)
            eb = jnp.exp(b)
            qt = qa * eb
            bl = b[c - 1:c, :]
            ebl = jnp.exp(bl)
            ekb = jnp.exp(bl - b)
            kh = kk * ekb
            st = st_ref[ci]
            dst = dstate[...]
            o = o_ref[rows, :]
            r = lax.rsqrt(jnp.mean(o * o, axis=1, keepdims=True) + EPS)
            oh = o * r
            szg = _sigmoid(zg)
            dyb = dyb_ref[rows, :]
            don = dyb * zg * szg
            dzg_ref[rows, :] = (dyb * oh * gain_v * szg * (1.0 + zg * (1.0 - szg))).astype(dzg_ref.dtype)
            doh = don * gain_v
            do = r * (doh - oh * jnp.mean(doh * oh, axis=1, keepdims=True))
            dqt = _dot(do, st)
            dp = jnp.where(causal, _dot_nt(do, zi), 0.0)
            dzi_ref[rows, :] = (_dot_tn(sc_ref[rows, :], do) + _dot_nt(kh, dst)).astype(dzi_ref.dtype)
            dkh = _dot(zi, dst)
            dbl = jnp.sum(dkh * kh, axis=0, keepdims=True) + jnp.sum(dst * st, axis=0, keepdims=True) * ebl
            dstate[...] = _dot_tn(do, qt) + dst * ebl
            dqa_s, dkk_s, db_s = _hg_scores_bwd(dp, qa, kk, b)
            dqa = dqt * eb + dqa_s
            dkk = dkh * ekb + dkk_s
            db = dqt * qt - dkh * kh + db_s
            row = lax.broadcasted_iota(jnp.int32, db.shape, 0)
            db = db + jnp.where(row == c - 1, dbl, 0.0)
            df = _cumsum_rows(db, reverse=True) / f - dkk
            dzf_ref[rows, :] = (df * (1.0 - lb) * sg * (1.0 - sg)).astype(dzf_ref.dtype)
            dzq_ref[rows, :] = (dqa * (HEAD ** -0.5) * sq * (1.0 + zq * (1.0 - sq))).astype(dzq_ref.dtype)
            dlb_ref[...] += jnp.sum(df * (1.0 - sg), axis=0, keepdims=True)
            dgn_ref[...] += jnp.sum(don * oh, axis=0, keepdims=True)
            return carry

        lax.fori_loop(0, cps, chunk, 0, unroll=True)

    tok = pl.BlockSpec((blk, HEAD), lambda h, b, n: (b * nblk + rev(n), h))
    vec = pl.BlockSpec((1, HEAD), lambda h, b, n: (0, h))
    rows = nseq * seq
    return _hosted_call(
        "hgrn2_bwd", body, grid=(heads, nseq, nblk),
        in_specs=[tok, zspec(qoff), zspec(qoff + heads), zspec(qoff + 2 * heads), zspec(qoff + 3 * heads), tok,
                  pl.BlockSpec((None, None, cps, HEAD, HEAD), lambda h, b, n: (h, b, rev(n), 0, 0)),
                  pl.BlockSpec((None, blk, CHUNK), lambda h, b, n: (h, b * nblk + rev(n), 0)), vec, vec],
        out_specs=[tok, tok, tok, tok, vec, vec],
        out_shape=[jax.ShapeDtypeStruct((rows, heads * HEAD), BF16)] * 4
        + [jax.ShapeDtypeStruct((1, heads * HEAD), F32)] * 2,
        scratch_shapes=[pltpu.VMEM((HEAD, HEAD), F32)],
        operands=(dyb, z, z, z, z, o, states, scores, lbrow, gain), rider=rider)


def _shift_rows(x, k):
    n = x.shape[0]
    r = pltpu.roll(x, k % n, 0)
    sub = lax.broadcasted_iota(jnp.int32, (SUBLANES, x.shape[1]), 0)
    if k > 0:
        return jnp.concatenate([jnp.where(sub >= k, r[0:SUBLANES], 0.0), r[SUBLANES:]], axis=0)
    return jnp.concatenate([r[:n - SUBLANES], jnp.where(sub < SUBLANES + k, r[n - SUBLANES:], 0.0)], axis=0)


def _conv_taps(h, w, bias):
    h1 = _shift_rows(h, 1)
    h2 = _shift_rows(h, 2)
    return h2 * w[0:1, :] + h1 * w[1:2, :] + h * w[2:3, :] + bias, h1, h2


def _conv_fwd(h, wconv, bconv, *, nseq, seq):
    ff2 = h.shape[1]
    ncol = ff2 // 2 // LANES

    def body(hg_ref, hv_ref, wg_ref, wv_ref, bg_ref, bv_ref, a_ref):
        g, _, _ = _conv_taps(hg_ref[...].astype(F32), wg_ref[...], bg_ref[...])
        v, _, _ = _conv_taps(hv_ref[...].astype(F32), wv_ref[...], bv_ref[...])
        a_ref[...] = (g * _sigmoid(g) * v).astype(a_ref.dtype)

    tok = lambda off: pl.BlockSpec((seq, LANES), lambda j, b, off=off: (b, off + j))
    wsp = lambda off: pl.BlockSpec((CONV_W, LANES), lambda j, b, off=off: (0, off + j))
    bsp = lambda off: pl.BlockSpec((1, LANES), lambda j, b, off=off: (0, off + j))
    return pl.pallas_call(
        body, name="conv_fwd", grid=(ncol, nseq),
        in_specs=[tok(0), tok(ncol), wsp(0), wsp(ncol), bsp(0), bsp(ncol)],
        out_specs=tok(0), out_shape=jax.ShapeDtypeStruct((nseq * seq, ff2 // 2), BF16),
        compiler_params=_params("arbitrary", "arbitrary"),
    )(h, h, wconv, wconv, bconv, bconv)


def _conv_bwd(da, h, wconv, bconv, *, nseq, seq):
    ff2 = h.shape[1]
    ncol = ff2 // 2 // LANES

    def half_bwd(d, hcur, h1, h2, w):
        d1 = _shift_rows(d, -1)
        d2 = _shift_rows(d, -2)
        dh = d * w[2:3, :] + d1 * w[1:2, :] + d2 * w[0:1, :]
        stats = jnp.concatenate(
            [jnp.sum(h2 * d, axis=0, keepdims=True), jnp.sum(h1 * d, axis=0, keepdims=True),
             jnp.sum(hcur * d, axis=0, keepdims=True), jnp.sum(d, axis=0, keepdims=True),
             jnp.zeros((SUBLANES - 4, d.shape[1]), F32)], axis=0)
        return dh, stats

    def body(da_ref, hg_ref, hv_ref, wg_ref, wv_ref, bg_ref, bv_ref, dhg_ref, dhv_ref, sg_ref, sv_ref):
        hg = hg_ref[...].astype(F32)
        hv = hv_ref[...].astype(F32)
        wg = wg_ref[...]
        wv = wv_ref[...]
        g, g1, g2 = _conv_taps(hg, wg, bg_ref[...])
        v, v1, v2 = _conv_taps(hv, wv, bv_ref[...])
        da = da_ref[...].astype(F32)
        s = _sigmoid(g)
        dhg, stg = half_bwd(da * v * s * (1.0 + g * (1.0 - s)), hg, g1, g2, wg)
        dhv, stv = half_bwd(da * g * s, hv, v1, v2, wv)
        dhg_ref[...] = dhg.astype(dhg_ref.dtype)
        dhv_ref[...] = dhv.astype(dhv_ref.dtype)
        first = pl.program_id(1) == 0
        for r, val in ((sg_ref, stg), (sv_ref, stv)):
            @pl.when(first)
            def _():
                r[...] = val

            @pl.when(jnp.logical_not(first))
            def _():
                r[...] += val

    tok = lambda off: pl.BlockSpec((seq, LANES), lambda j, b, off=off: (b, off + j))
    wsp = lambda off: pl.BlockSpec((CONV_W, LANES), lambda j, b, off=off: (0, off + j))
    bsp = lambda off: pl.BlockSpec((1, LANES), lambda j, b, off=off: (0, off + j))
    ssp = pl.BlockSpec((SUBLANES, LANES), lambda j, b: (0, j))
    dhg, dhv, stg, stv = pl.pallas_call(
        body, name="conv_bwd", grid=(ncol, nseq),
        in_specs=[tok(0), tok(0), tok(ncol), wsp(0), wsp(ncol), bsp(0), bsp(ncol)],
        out_specs=[tok(0), tok(0), ssp, ssp],
        out_shape=[jax.ShapeDtypeStruct((nseq * seq, ff2 // 2), BF16)] * 2
        + [jax.ShapeDtypeStruct((SUBLANES, ff2 // 2), F32)] * 2,
        compiler_params=_params("arbitrary", "arbitrary"),
    )(da, h, h, wconv, wconv, bconv, bconv)
    return (dhg, dhv), jnp.concatenate([stg, stv], axis=1)


def _rms_fwd(xv, g):
    r = lax.rsqrt(jnp.mean(xv * xv, axis=1, keepdims=True) + EPS)
    return (xv * r * g,)


def _rms_bwd(xv, g, dy, res):
    r = lax.rsqrt(jnp.mean(xv * xv, axis=1, keepdims=True) + EPS)
    xh = xv * r
    dxh = dy * g
    dx = r * (dxh - xh * jnp.mean(dxh * xh, axis=1, keepdims=True)) + res
    return dx, jnp.sum(dy * xh, axis=0, keepdims=True)


def _loss_head(x2, tgt, g):
    d = x2.shape[1]
    r = lax.rsqrt(jnp.mean(x2 * x2, axis=1, keepdims=True) + EPS)
    xh = x2 * r
    err = xh * g - tgt
    dy = err * (1.0 / d)
    dxh = dy * g
    dx = r * (dxh - xh * jnp.mean(dxh * xh, axis=1, keepdims=True))
    loss = 0.5 * jnp.sum(jnp.mean(err * err, axis=1, keepdims=True), axis=0, keepdims=True)
    return dx, jnp.sum(dy * xh, axis=0, keepdims=True), jnp.broadcast_to(loss, (1, LANES))


LATE_A = ("w_down", "w_out")
LATE_B = ("w_up", "w_pa", "w_pb")
LATE = LATE_A + LATE_B
EARLY_GRADS = ("w_down", "w_up", "w_out", "w_pa", "w_pb", "w_glu")
ROW_SHARDED = ("w_glu", "w_out", "w_down")


def _local_step(x, tgt, u, z_own, p, late, *, nseq, seq):
    p = dict(p)
    chip = 2 * lax.axis_index("x") + lax.axis_index("y")
    t, d = x.shape
    s5w = p["s5_d"].shape[1]
    hgw = p["gain"].shape[1]
    heads = hgw // HEAD
    qoff = s5w // LANES
    gblk = (s5w + 4 * hgw) // GATE_BLOCK
    ngb = d // GATE_BLOCK
    tm = _row_tile(t, ROW_TILE)
    row = lambda a, w=None, base=0: (a, a.shape[1] if w is None else w, base, "row")
    vec = lambda a, w=None, base=0: (a, a.shape[1] if w is None else w, base, "vec")
    rw = functools.partial(_rowwise, rows=t, tm=tm)

    z, _ = _in_proj_rest(u, p["w_in"], z_own, None)

    lam_re, lam_im, bb_re, bb_im = _s5_discretize(p["s5_a_re"], p["s5_a_im"], p["s5_log_dt"], p["s5_b_re"], p["s5_b_im"])
    bre3 = _s5_in_blocks(bb_re).astype(BF16)
    bim3 = _s5_in_blocks(bb_im).astype(BF16)
    cre3 = _s5_out_blocks(p["s5_c_re"]).astype(BF16)
    cim3 = _s5_out_blocks(p["s5_c_im"]).astype(BF16)
    coef_f = _s5_scan_tables(lam_re.reshape(-1), lam_im.reshape(-1), False)
    coef_r = _s5_scan_tables(lam_re.reshape(-1), lam_im.reshape(-1), True)
    (o, yb, states, scores), landed_b = _hg_fwd(z, p["lbrow"], p["gain"], nseq=nseq, seq=seq, heads=heads, qoff=qoff,
                                                rider=_gather_ici_rider([late[n] for n in LATE_B]))
    def place_own(names, gathered):
        for n, g in zip(names, gathered):
            full = lax.dynamic_update_index_in_dim(g, late[n], chip, 0)
            p[n] = full.reshape(-1, full.shape[-1]) if n in ROW_SHARDED else full

    nb_late = len(LATE_B)
    (y5, xre, xim), got = _s5_fwd(z, bre3, bim3, cre3, cim3, coef_f, p["s5_d"], nseq=nseq, seq=seq,
                                  rider=_merge_riders(_gather_pass_rider(list(landed_b)),
                                                      _gather_ici_rider([late[n] for n in LATE_A])))
    place_own(LATE_B, got[:nb_late])
    ya0, gl, ya = _glu_fwd(y5, p["w_glu"], p["b_glu"])

    joined = lambda w3: w3.transpose(1, 0, 2).reshape(w3.shape[1], -1)
    split = lambda g: g.reshape(g.shape[0], N_CHIPS, -1).transpose(1, 0, 2)
    wpa, wpb = joined(p["w_pa"]), joined(p["w_pb"])
    pa, got_a = _mm_fwd_rows("proj_a", ya, wpa, out_dtype=BF16, rider=_gather_pass_rider(list(got[nb_late:])))
    place_own(LATE_A, got_a)
    pb = _mm_fwd_rows("proj_b", yb, wpb, out_dtype=BF16)
    gb = GATE_BLOCK
    (m,) = rw("merge", lambda ga, gbv, a, b: (_sigmoid(ga) * a + _sigmoid(gbv) * b,),
              [row(z, gb, gblk), row(z, gb, gblk + ngb), row(pa, gb), row(pb, gb)], [(d, gb, BF16)], ncol=ngb)
    x1 = _mm_fwd_rows("out_proj", m, p["w_out"], res=x)

    (u2,) = rw("rms_ffn", _rms_fwd, [row(x1), vec(p["g_ffn"])], [(d, d, BF16)])
    h = _mm_fwd_cols("up_proj", u2, p["w_up"], out_dtype=BF16)
    a = _conv_fwd(h, p["w_conv"], p["b_conv"], nseq=nseq, seq=seq)
    dx2, dg_final, lossv = _mm_fwd_rows("down_proj", a, p["w_down"], res=x1,
                                        epilogue=(_loss_head, [tgt, p["g_final"]], [d, LANES]))

    norm_bwd = lambda dyv, xv, g, resv: _rms_bwd(xv, g, dyv, resv)
    da = _mm_bwd_rows("down_bwd", dx2, p["w_down"], out_dtype=BF16)
    g_wdown = _mm_wgrad_rows("down_wgrad", a, dx2)
    dh, cstats = _conv_bwd(da, h, p["w_conv"], p["b_conv"], nseq=nseq, seq=seq)
    dx1, dg_ffn = _mm_bwd_cols("up_bwd", dh, p["w_up"], epilogue=(norm_bwd, [x1, p["g_ffn"], dx2], [d]))
    g_wup = _mm_wgrad_cols("up_wgrad", u2, dh)

    dm = _mm_bwd_rows("out_bwd", dx1, p["w_out"], out_dtype=BF16)
    g_wout = _mm_wgrad_rows("out_wgrad", m, dx1)

    def merge_bwd(ga, gbv, av, bv, dmv):
        sa = _sigmoid(ga)
        sb = _sigmoid(gbv)
        return dmv * sa, dmv * sb, dmv * av * sa * (1.0 - sa), dmv * bv * sb * (1.0 - sb)

    dpa, dpb, dzga, dzgb = rw("merge_bwd", merge_bwd,
                              [row(z, gb, gblk), row(z, gb, gblk + ngb), row(pa, gb), row(pb, gb), row(dm, gb)],
                              [(d, gb, BF16)] * 4, ncol=ngb)
    dya = _mm_bwd_rows("proj_a_bwd", dpa, wpa)
    g_wpa = split(_mm_wgrad_rows("proj_a_wgrad", ya, dpa))
    dyb = _mm_bwd_rows("proj_b_bwd", dpb, wpb)
    g_wpb = split(_mm_wgrad_rows("proj_b_wgrad", yb, dpb))

    dgl, dy5, db_glu = _glu_bwd(y5, gl, p["b_glu"], dya, p["w_glu"])
    g_wglu = _mm_wgrad_rows("glu_wgrad", ya0, dgl)
    partial = dict(w_down=g_wdown, w_up=g_wup, w_out=g_wout, w_pa=g_wpa, w_pb=g_wpb, w_glu=g_wglu)
    parts = [_grad_parts(partial[n]) for n in EARLY_GRADS]
    (dza, dbre3, dbim3, dcre3, dcim3, dlam, dd), sib = _s5_bwd(
        dy5, z, xre, xim, bre3, bim3, cre3, cim3, coef_r, p["s5_d"], nseq=nseq, seq=seq, rider=_swap_halves_rider(parts))
    pair = _pair_sums(EARLY_GRADS, parts, sib)
    (dzq, dzf, dzi, dzg, dlb, dgain), others = _hg_bwd(
        dyb, z, o, states, scores, p["lbrow"], p["gain"], nseq=nseq, seq=seq, heads=heads, qoff=qoff,
        rider=_scatter_rider(pair))
    halves = _chip_sums(EARLY_GRADS, pair, others)

    dz = jnp.concatenate([dza, dzq, dzf, dzi, dzg, dzga, dzgb], axis=1)
    gshape = lam_re.shape
    small = {
        "loss": lossv, "g_ffn": dg_ffn, "g_final": dg_final, "b_glu": db_glu, "gain": dgain,
        "lbrow": dlb, "s5_d": dd, "w_conv": cstats[0:CONV_W], "b_conv": cstats[CONV_W:CONV_W + 1],
        "lam_re": dlam[:, 0, :].reshape(gshape), "lam_im": dlam[:, 1, :].reshape(gshape),
        "bb_re": _s5_in_blocks_diag(dbre3), "bb_im": _s5_in_blocks_diag(dbim3),
        "s5_c_re": _s5_out_blocks_diag(dcre3), "s5_c_im": _s5_out_blocks_diag(dcim3),
    }
    small_vec = _pack([small[n] for n in SMALL_PARTS], F32)
    g_win, (small_all, *sibs) = _mm_wgrad_cols(
        "in_wgrad", u, dz, rider=_merge_riders(_gather_all_rider(small_vec), _swap_sums_rider(halves)))
    big = dict(zip(EARLY_GRADS, zip(halves, sibs)))
    small_sum = _sum_over_devices("small_grad_sum", small_vec, small_all)
    sm = dict(zip(SMALL_PARTS, _unpack(small_sum, [small[n].shape for n in SMALL_PARTS])))
    last = [_grad_parts(g_win)]
    pair = _pair_sums(("w_in",), last, _run_rider("grad_swap_halves", _swap_halves_rider(last)))
    (dx, dg_mix), others = _mm_bwd_cols("in_bwd", dz, p["w_in"], epilogue=(norm_bwd, [x, p["g_mix"], dx1], [d]),
                                        rider=_scatter_rider(pair))
    (half,) = _chip_sums(("w_in",), pair, others)
    mid = half.shape[0] // 2
    mix_vec = dg_mix.reshape(SUBLANES, -1)
    top, bottom, mix_all = _run_rider("grad_swap_sums", _merge_riders(_swap_sums_rider([half[:mid], half[mid:]]),
                                                                      _gather_all_rider(mix_vec)))
    big["w_in"] = (half, jnp.concatenate([top, bottom], axis=0))
    sm["g_mix"] = _sum_over_devices("g_mix_sum", mix_vec, mix_all).reshape(dg_mix.shape)
    return dx, big, sm


ANY = pl.BlockSpec(memory_space=pl.ANY)


def _place():
    x, y, c = lax.axis_index("x"), lax.axis_index("y"), lax.axis_index("c")
    chips = [(1 - x, y), (x, 1 - y), (1 - x, 1 - y)]
    return x, y, c, chips


def _remote(src, dst, send_sems, recv_sems, k, to):
    return pltpu.make_async_remote_copy(src_ref=src, dst_ref=dst, send_sem=send_sems.at[k], recv_sem=recv_sems.at[k],
                                        device_id=to, device_id_type=MESH)


def _half(rows, which):
    return pl.ds(pl.multiple_of(which * (rows // 2), SUBLANES), rows // 2)


class _SemView:
    def __init__(self, base, offset):
        self.base, self.offset = base, offset

    @property
    def at(self):
        return self

    def __getitem__(self, k):
        return self.base.at[self.offset + k]


def _merge_riders(first, second):
    na, no, ns = len(first.arrays), len(first.out_shapes), first.nsem

    def split(fn_a, fn_b):
        def run(ins, outs, send_sems, recv_sems):
            fn_a(ins[:na], outs[:no], send_sems, recv_sems)
            fn_b(ins[na:], outs[no:], _SemView(send_sems, ns), _SemView(recv_sems, ns))
        return run

    aliases = dict(first.aliases)
    aliases.update({na + i: no + o for i, o in second.aliases.items()})
    return _Rider(first.arrays + second.arrays, first.out_shapes + second.out_shapes, ns + second.nsem,
                  split(first.start, second.start), split(first.finish, second.finish), aliases)


PREPARE_TILE = 512


def _prepare(x, gain, w_own, arrays, rider):
    t, d = x.shape
    ns = w_own.shape[1]
    n = len(arrays)
    tm = _row_tile(t, PREPARE_TILE)

    def body(place, x_ref, g_ref, w_ref, *refs):
        (u,) = _rms_fwd(x_ref[...], g_ref[...])
        u = u.astype(BF16)
        refs[n][...] = u
        refs[n + 1][...] = _dot(u, w_ref[...])

        @pl.when(pl.program_id(0) == 0)
        def _():
            for i in range(n):
                refs[n + 2 + i][...] = refs[i][...].astype(BF16)

    vm = pl.BlockSpec(memory_space=pltpu.VMEM)
    tok = pl.BlockSpec((tm, d), lambda i, place: (i, 0))
    outs, gathered = _hosted_scalar_call(
        "prepare", body, grid=(t // tm,),
        in_specs=[tok, pl.BlockSpec((1, d), lambda i, place: (0, 0)), vm] + [vm] * n,
        out_specs=[tok, pl.BlockSpec((tm, ns), lambda i, place: (i, place[1]))] + [vm] * n,
        out_shape=[jax.ShapeDtypeStruct((t, d), BF16), jax.ShapeDtypeStruct((t, N_CHIPS * ns), F32)]
        + [jax.ShapeDtypeStruct(a.shape, BF16) for a in arrays],
        operands=[x, gain, w_own] + list(arrays), rider=rider)
    return outs[0], outs[1], outs[2:], gathered


def _in_proj_rest(u, w3, z, rider):
    t, k = u.shape
    ns = w3.shape[2]
    tm = _mm_tile(t, 2 * k * _size(u) + 2 * ns * 4, 2 * k * ns * _size(w3))
    other = lambda j, place: jnp.bitwise_xor(place[1], j + 1)

    def body(place, u_ref, w_ref, z_ref, o_ref):
        o_ref[...] = _dot(u_ref[...], w_ref[...])

    outs, ridden = _hosted_scalar_call(
        "in_proj", body, grid=(N_CHIPS - 1, t // tm),
        in_specs=[pl.BlockSpec((tm, k), lambda j, i, place: (i, 0)),
                  pl.BlockSpec((None, k, ns), lambda j, i, place: (other(j, place), 0, 0)), ANY],
        out_specs=[pl.BlockSpec((tm, ns), lambda j, i, place: (i, other(j, place)))],
        out_shape=[jax.ShapeDtypeStruct(z.shape, z.dtype)], operands=[u, w3, z], rider=rider, aliases={2: 0})
    return outs[0], ridden


def _symmetric_rider(arrays, out_shapes, copies_of, nsem):
    def start(ins, outs, send_sems, recv_sems):
        for cp in copies_of(ins, outs, send_sems, recv_sems):
            cp.start()

    def finish(ins, outs, send_sems, recv_sems):
        for cp in copies_of(ins, outs, send_sems, recv_sems):
            cp.wait()

    return _Rider(arrays, out_shapes, nsem, start, finish)


def _swap_halves_rider(parts):
    def copies_of(ins, outs, send_sems, recv_sems):
        x, y, c, _ = _place()
        return [_remote(ins[a].at[:, _half(g.shape[1], 1 - c), :], outs[a], send_sems, recv_sems, a, (x, y, 1 - c))
                for a, g in enumerate(parts)]

    shapes = [jax.ShapeDtypeStruct((g.shape[0], g.shape[1] // 2, g.shape[2]), g.dtype) for g in parts]
    return _symmetric_rider(parts, shapes, copies_of, len(parts))


def _scatter_rider(parts):
    def copies_of(ins, outs, send_sems, recv_sems):
        x, y, c, chips = _place()
        return [_remote(ins[a].at[2 * cx + cy], outs[a].at[j], send_sems, recv_sems, 3 * a + j, (cx, cy, c))
                for a in range(len(parts)) for j, (cx, cy) in enumerate(chips)]

    shapes = [jax.ShapeDtypeStruct((N_CHIPS - 1,) + h.shape[1:], h.dtype) for h in parts]
    return _symmetric_rider(parts, shapes, copies_of, 3 * len(parts))


def _swap_sums_rider(parts):
    def copies_of(ins, outs, send_sems, recv_sems):
        x, y, c, _ = _place()
        return [_remote(ins[a], outs[a], send_sems, recv_sems, a, (x, y, 1 - c)) for a in range(len(parts))]

    shapes = [jax.ShapeDtypeStruct(g.shape, g.dtype) for g in parts]
    return _symmetric_rider(parts, shapes, copies_of, len(parts))


def _gather_ici_rider(shards):
    def sends(ins, outs, send_sems, recv_sems):
        x, y, c, chips = _place()
        return [_remote(ins[a].at[_half(s.shape[0], c)], outs[a].at[2 * x + y, _half(s.shape[0], c)], send_sems,
                        recv_sems, 3 * a + j, (cx, cy, c)) for a, s in enumerate(shards) for j, (cx, cy) in enumerate(chips)]

    def start(ins, outs, send_sems, recv_sems):
        for cp in sends(ins, outs, send_sems, recv_sems):
            cp.start()

    def finish(ins, outs, send_sems, recv_sems):
        x, y, c, chips = _place()
        for a, s in enumerate(shards):
            for j, (cx, cy) in enumerate(chips):
                landed = outs[a].at[2 * cx + cy, _half(s.shape[0], c)]
                _remote(landed, landed, send_sems, recv_sems, 3 * a + j, (x, y, c)).wait_recv()
        for cp in sends(ins, outs, send_sems, recv_sems):
            cp.wait_send()

    shapes = [jax.ShapeDtypeStruct((N_CHIPS,) + s.shape, s.dtype) for s in shards]
    return _Rider(shards, shapes, 3 * len(shards), start, finish)


def _gather_full_rider(shards):
    n = len(shards)

    def sends(ins, outs, send_sems, recv_sems):
        x, y, c, chips = _place()
        return [_remote(ins[a].at[_half(s.shape[0], c)], outs[a].at[2 * x + y, _half(s.shape[0], c)], send_sems,
                        recv_sems, 6 * a + j, (cx, cy, c)) for a, s in enumerate(shards) for j, (cx, cy) in enumerate(chips)]

    def start(ins, outs, send_sems, recv_sems):
        for cp in sends(ins, outs, send_sems, recv_sems):
            cp.start()

    def finish(ins, outs, send_sems, recv_sems):
        x, y, c, chips = _place()
        passed = []
        for a, s in enumerate(shards):
            for j, (cx, cy) in enumerate(chips):
                landed = outs[a].at[2 * cx + cy, _half(s.shape[0], c)]
                _remote(landed, landed, send_sems, recv_sems, 6 * a + j, (x, y, c)).wait_recv()
                passed.append(_remote(landed, landed, send_sems, recv_sems, 6 * a + 3 + j, (x, y, 1 - c)))
                passed[-1].start()
        for a, s in enumerate(shards):
            for j, (cx, cy) in enumerate(chips):
                other = outs[a].at[2 * cx + cy, _half(s.shape[0], 1 - c)]
                _remote(other, other, send_sems, recv_sems, 6 * a + 3 + j, (x, y, c)).wait_recv()
        for cp in sends(ins, outs, send_sems, recv_sems) + passed:
            cp.wait_send()

    shapes = [jax.ShapeDtypeStruct((N_CHIPS,) + s.shape, s.dtype) for s in shards]
    return _Rider(shards, shapes, 6 * n, start, finish)


def _gather_pass_rider(landed):
    def sends(ins, outs, send_sems, recv_sems):
        x, y, c, chips = _place()
        return [_remote(ins[a].at[2 * cx + cy, _half(g.shape[1], c)], outs[a].at[2 * cx + cy, _half(g.shape[1], c)],
                        send_sems, recv_sems, 3 * a + j, (x, y, 1 - c))
                for a, g in enumerate(landed) for j, (cx, cy) in enumerate(chips)]

    def start(ins, outs, send_sems, recv_sems):
        for cp in sends(ins, outs, send_sems, recv_sems):
            cp.start()

    def finish(ins, outs, send_sems, recv_sems):
        x, y, c, chips = _place()
        for a, g in enumerate(landed):
            for j, (cx, cy) in enumerate(chips):
                other = outs[a].at[2 * cx + cy, _half(g.shape[1], 1 - c)]
                _remote(other, other, send_sems, recv_sems, 3 * a + j, (x, y, c)).wait_recv()
        for cp in sends(ins, outs, send_sems, recv_sems):
            cp.wait_send()

    shapes = [jax.ShapeDtypeStruct(g.shape, g.dtype) for g in landed]
    return _Rider(landed, shapes, 3 * len(landed), start, finish, aliases={a: a for a in range(len(landed))})


def _grad_parts(g):
    return g.reshape((N_CHIPS, -1, g.shape[-1]))


def _place_scalars():
    return jnp.stack([lax.axis_index("c"), 2 * lax.axis_index("x") + lax.axis_index("y")]).astype(jnp.int32)


def _scalar_call(body, name, grid, in_specs, out_specs, out_shape, operands):
    spec = pltpu.PrefetchScalarGridSpec(num_scalar_prefetch=1, grid=grid, in_specs=in_specs, out_specs=out_specs)
    return pl.pallas_call(body, name=name, grid_spec=spec, out_shape=out_shape,
                          compiler_params=_params(*(["arbitrary"] * len(grid))))(_place_scalars(), *operands)


def _pair_sums(names, parts, sib):
    out = []
    for n, g, s in zip(names, parts, sib):
        rh, cols = s.shape[1], s.shape[2]
        tm = _row_tile(rh, ROW_TILE)
        nblk = rh // tm

        def body(place, g_ref, s_ref, o_ref):
            o_ref[...] = (g_ref[...].astype(F32) + s_ref[...].astype(F32)).astype(o_ref.dtype)

        blk = pl.BlockSpec((None, tm, cols), lambda j, i, place: (j, i, 0))
        own = pl.BlockSpec((None, tm, cols), lambda j, i, place, nblk=nblk: (j, place[0] * nblk + i, 0))
        out.append(_scalar_call(body, "grad_pair_sum_" + n, (N_CHIPS, nblk), [own, blk], blk,
                                jax.ShapeDtypeStruct(s.shape, BF16), (g, s)))
    return out


def _chip_sums(names, pair, others):
    out = []
    for n, h, o in zip(names, pair, others):
        rh, cols = h.shape[1], h.shape[2]
        tm = _row_tile(rh, ROW_TILE)

        def body(place, h_ref, a_ref, b_ref, c_ref, o_ref):
            o_ref[...] = (h_ref[...].astype(F32) + a_ref[...].astype(F32)) + b_ref[...].astype(F32) + c_ref[...].astype(F32)

        mine = pl.BlockSpec((None, tm, cols), lambda i, place: (place[1], i, 0))
        other = lambda k: pl.BlockSpec((None, tm, cols), lambda i, place, k=k: (k, i, 0))
        out.append(_scalar_call(body, "grad_chip_sum_" + n, (rh // tm,), [mine, other(0), other(1), other(2)],
                                pl.BlockSpec((tm, cols), lambda i, place: (i, 0)), jax.ShapeDtypeStruct((rh, cols), F32),
                                (h, o, o, o)))
    return out


def _adamw_halves(name, w, m, v, own, sib):
    rh, cols = own.shape
    tm = _row_tile(rh, ADAMW_TILE)
    nblk = rh // tm

    def body(place, w_ref, m_ref, v_ref, own_ref, sib_ref, g_ref, d_ref, m2_ref, v2_ref):
        mine = pl.program_id(0) // nblk == place[0]

        def run(gv):
            g_ref[...] = gv
            d_ref[...], m2_ref[...], v2_ref[...] = _adamw_math(w_ref[...], gv, m_ref[...], v_ref[...])

        @pl.when(mine)
        def _():
            run(own_ref[...])

        @pl.when(jnp.logical_not(mine))
        def _():
            run(sib_ref[...])

    full = pl.BlockSpec((tm, cols), lambda i, place: (i, 0))
    own_spec = pl.BlockSpec((tm, cols), lambda i, place: (jnp.where(i // nblk == place[0], i % nblk, 0), 0))
    sib_spec = pl.BlockSpec((tm, cols), lambda i, place: (jnp.where(i // nblk == place[0], 0, i % nblk), 0))
    return _scalar_call(body, name, (2 * nblk,), [full, full, full, own_spec, sib_spec], [full] * 4,
                        [jax.ShapeDtypeStruct((2 * rh, cols), F32)] * 4, (w, m, v, own, sib))


def _gather_all_rider(v):
    m_per = v.shape[0]

    def rows(ref, px, py, pc):
        return ref.at[pl.ds(pl.multiple_of((4 * px + 2 * py + pc) * m_per, 8), m_per)]

    def first(ins, outs, send_sems, recv_sems):
        x, y, c, chips = _place()
        mine = rows(outs[0], x, y, c)
        return [_remote(ins[0], mine, send_sems, recv_sems, 0, (x, y, 1 - c))] + [
            _remote(ins[0], mine, send_sems, recv_sems, 1 + j, (cx, cy, c)) for j, (cx, cy) in enumerate(chips)]

    def start(ins, outs, send_sems, recv_sems):
        for cp in first(ins, outs, send_sems, recv_sems):
            cp.start()

    def finish(ins, outs, send_sems, recv_sems):
        x, y, c, chips = _place()
        passed = []
        for j, (cx, cy) in enumerate(chips):
            blk = rows(outs[0], cx, cy, c)
            _remote(blk, blk, send_sems, recv_sems, 1 + j, (x, y, c)).wait_recv()
            passed.append(_remote(blk, blk, send_sems, recv_sems, 4 + j, (x, y, 1 - c)))
            passed[j].start()
        sib = rows(outs[0], x, y, 1 - c)
        _remote(sib, sib, send_sems, recv_sems, 0, (x, y, c)).wait_recv()
        for j, (cx, cy) in enumerate(chips):
            blk = rows(outs[0], cx, cy, 1 - c)
            _remote(blk, blk, send_sems, recv_sems, 4 + j, (x, y, c)).wait_recv()
        for cp in first(ins, outs, send_sems, recv_sems) + passed:
            cp.wait_send()

    return _Rider([v], [jax.ShapeDtypeStruct((N_DEV * m_per,) + v.shape[1:], v.dtype)], 7, start, finish)


def _sum_over_devices(name, v, gathered):
    m_per = v.shape[0]
    dev = 4 * lax.axis_index("x") + 2 * lax.axis_index("y") + lax.axis_index("c")
    full = lax.dynamic_update_slice_in_dim(gathered, v, dev * m_per, axis=0)
    return _sum_blocks(name, [full[i * m_per:(i + 1) * m_per] for i in range(N_DEV)], F32)


def _sum_blocks(name, parts, out_dtype):
    rows, cols = parts[0].shape
    tm = _row_tile(rows, ROW_TILE)

    def body(*refs):
        acc = refs[0][...].astype(F32)
        for r in refs[1:-1]:
            acc = acc + r[...].astype(F32)
        refs[-1][...] = acc.astype(refs[-1].dtype)

    spec = pl.BlockSpec((tm, cols), lambda i: (i, 0))
    return pl.pallas_call(
        body, name=name, grid=(rows // tm,), in_specs=[spec] * len(parts), out_specs=spec,
        out_shape=jax.ShapeDtypeStruct((rows, cols), out_dtype), compiler_params=_params("arbitrary"),
    )(*parts)


def _adamw_math(wv, gv, mv, vv):
    m2 = ADAM_B1 * mv + (1.0 - ADAM_B1) * gv
    v2 = ADAM_B2 * vv + (1.0 - ADAM_B2) * (gv * gv)
    delta = -ADAM_LR * ((m2 / (1.0 - ADAM_B1 ** ADAM_STEP)) / (jnp.sqrt(v2 / (1.0 - ADAM_B2 ** ADAM_STEP)) + ADAM_EPS)
                        + ADAM_WD * wv)
    return delta, m2, v2


def _adamw_small(ws, gs, ms, vs):
    n = len(ws)

    def body(*refs):
        for i in range(n):
            res = _adamw_math(refs[i][...], refs[n + i][...], refs[2 * n + i][...], refs[3 * n + i][...])
            for k in range(3):
                refs[(4 + k) * n + i][...] = res[k]

    vm = pl.BlockSpec(memory_space=pltpu.VMEM)
    outs = pl.pallas_call(
        body, name="adamw_small", in_specs=[vm] * (4 * n), out_specs=[vm] * (3 * n),
        out_shape=[jax.ShapeDtypeStruct(a.shape, F32) for a in ws] * 3,
        compiler_params=pltpu.CompilerParams(vmem_limit_bytes=VMEM_LIMIT_BYTES),
    )(*ws, *gs, *ms, *vs)
    return outs[:n], outs[n:2 * n], outs[2 * n:]


PACK_ROWS = 256


def _pack(flat_parts, dtype, lead=()):
    parts = [a.astype(dtype).reshape(lead + (-1,)) for a in flat_parts]
    n = sum(a.shape[-1] for a in parts)
    chunk = PACK_ROWS * LANES
    total = -(-n // chunk) * chunk
    if total > n:
        parts.append(jnp.zeros(lead + (total - n,), dtype))
    return jnp.concatenate(parts, axis=-1).reshape(lead + (total // LANES, LANES))


def _unpack(buf, shapes, lead=()):
    flat = buf.reshape(lead + (-1,))
    out, off = [], 0
    for shp in shapes:
        n = math.prod(shp)
        out.append(lax.slice_in_dim(flat, off, off + n, axis=len(lead)).reshape(lead + tuple(shp)))
        off += n
    return out


BIG = ("w_in", "w_glu", "w_pa", "w_pb", "w_out", "w_up", "w_down")
WEIGHTS = ("g_mix", "w_in", "s5_a_re", "s5_a_im", "s5_log_dt", "s5_b_re", "s5_b_im", "s5_c_re", "s5_c_im", "s5_d",
           "w_glu", "b_glu", "hg_lb_logits", "hg_norm_gain", "w_pa", "w_pb", "w_out", "g_ffn", "w_up", "w_conv",
           "b_conv", "w_down", "g_final")
SMALL = tuple(n for n in WEIGHTS if n not in BIG)
SMALL_PARTS = ("loss", "g_ffn", "g_final", "b_glu", "gain", "lbrow", "s5_d", "w_conv", "b_conv", "lam_re", "lam_im",
               "bb_re", "bb_im", "s5_c_re", "s5_c_im")


def _lower_bound(logits):
    return jnp.cumsum(jax.nn.softmax(logits, axis=0), axis=0)[0:1]


def kernel(x, g_mix, w_in, s5_a_re, s5_a_im, s5_log_dt, s5_b_re, s5_b_im, s5_c_re, s5_c_im, s5_d, w_glu, b_glu, hg_lb_logits, hg_norm_gain, w_pa, w_pb, w_out, g_ffn, w_up, w_conv, b_conv, w_down, g_final, loss_target, m_g_mix, m_w_in, m_s5_a_re, m_s5_a_im, m_s5_log_dt, m_s5_b_re, m_s5_b_im, m_s5_c_re, m_s5_c_im, m_s5_d, m_w_glu, m_b_glu, m_hg_lb_logits, m_hg_norm_gain, m_w_pa, m_w_pb, m_w_out, m_g_ffn, m_w_up, m_w_conv, m_b_conv, m_w_down, m_g_final, v_g_mix, v_w_in, v_s5_a_re, v_s5_a_im, v_s5_log_dt, v_s5_b_re, v_s5_b_im, v_s5_c_re, v_s5_c_im, v_s5_d, v_w_glu, v_b_glu, v_hg_lb_logits, v_hg_norm_gain, v_w_pa, v_w_pb, v_w_out, v_g_ffn, v_w_up, v_w_conv, v_b_conv, v_w_down, v_g_final):
    args = dict(locals())
    w = {n: args[n] for n in WEIGHTS}
    mom = {n: args["m_" + n] for n in WEIGHTS}
    var = {n: args["v_" + n] for n in WEIGHTS}
    nseq, seq, d = x.shape
    xi, yi = lax.axis_index("x"), lax.axis_index("y")
    chip = 2 * xi + yi

    shard = {n: w[n][0] for n in BIG}
    first = [shard["w_in"].astype(BF16), shard["w_glu"].astype(BF16),
             jnp.pad(w_conv[0], ((0, 2 * SUBLANES - CONV_W), (0, 0)))]
    x2 = x.reshape(nseq * seq, d)
    u, z_own, late16, got = _prepare(x2, g_mix, first[0], [shard[n] for n in LATE], _gather_full_rider(first))
    w_in_all, w_glu_all, conv_all = [lax.dynamic_update_index_in_dim(g, s, chip, 0) for g, s in zip(got, first)]
    p = dict(g_mix=g_mix, g_ffn=g_ffn, g_final=g_final.reshape(1, -1), b_glu=b_glu, gain=hg_norm_gain, s5_d=s5_d,
             b_conv=b_conv, lbrow=_lower_bound(hg_lb_logits),
             s5_a_re=s5_a_re[0], s5_a_im=s5_a_im[0], s5_log_dt=s5_log_dt[0], s5_b_re=s5_b_re[0], s5_b_im=s5_b_im[0],
             s5_c_re=s5_c_re[0], s5_c_im=s5_c_im[0], w_in=w_in_all, w_glu=w_glu_all.reshape(-1, w_glu_all.shape[-1]),
             w_conv=conv_all[:, :CONV_W].transpose(1, 0, 2).reshape(CONV_W, -1))

    dx, halves, sm = _local_step(x2, loss_target.reshape(nseq * seq, d), u, z_own, p, dict(zip(LATE, late16)),
                                 nseq=nseq, seq=seq)
    loss = sm["loss"][0, 0]

    grads, delta, new_m, new_v = {}, {}, {}, {}
    for n in BIG:
        shp = shard[n].shape
        grads[n], delta[n], new_m[n], new_v[n] = _adamw_halves("adamw_" + n, shard[n], mom[n].reshape(shp),
                                                               var[n].reshape(shp), *halves[n])

    _, disc_vjp = jax.vjp(_s5_discretize, p["s5_a_re"], p["s5_a_im"], p["s5_log_dt"], p["s5_b_re"], p["s5_b_im"])
    da_re, da_im, dlog_dt, db_re, db_im = disc_vjp((sm["lam_re"], sm["lam_im"], sm["bb_re"], sm["bb_im"]))
    _, lb_vjp = jax.vjp(_lower_bound, hg_lb_logits)
    (dlogits,) = lb_vjp(sm["lbrow"])
    fcols = w_conv.shape[-1]
    grads.update(
        g_mix=sm["g_mix"], g_ffn=sm["g_ffn"], g_final=sm["g_final"].reshape(-1), b_glu=sm["b_glu"],
        hg_norm_gain=sm["gain"], hg_lb_logits=dlogits, s5_d=sm["s5_d"], b_conv=sm["b_conv"],
        w_conv=lax.dynamic_slice_in_dim(sm["w_conv"], chip * fcols, fcols, axis=1),
        s5_a_re=da_re, s5_a_im=da_im, s5_log_dt=dlog_dt, s5_b_re=db_re, s5_b_im=db_im,
        s5_c_re=sm["s5_c_re"], s5_c_im=sm["s5_c_im"])
    grads = {n: grads[n].reshape(w[n].shape) for n in WEIGHTS}

    def natural(a):
        return a.reshape(1, -1) if a.ndim == 1 else (a[0] if a.ndim > 2 else a)

    outs = _adamw_small(*[[natural(src[n]) for n in SMALL] for src in (w, grads, mom, var)])
    for dst, group in zip((delta, new_m, new_v), outs):
        dst.update(zip(SMALL, group))
    res = [loss, dx.reshape(x.shape)]
    for group in (grads, delta, new_m, new_v):
        res += [group[n].reshape(w[n].shape) for n in WEIGHTS]
    return tuple(res)
```

```python
import functools
import math

import jax
import jax.numpy as jnp
from jax import lax
from jax.experimental import pallas as pl
from jax.experimental.pallas import tpu as pltpu

F32 = jnp.float32
BF16 = jnp.bfloat16
MESH = pl.DeviceIdType.MESH

EPS = 1e-6
S5_GROUP = 16
S5_STATE = 64
S5_BLOCK_GROUPS = 8
HEAD = 128
CHUNK = 64
CONV_W = 3
LANES = 128
SUBLANES = 8
GATE_BLOCK = 512
VMEM_LIMIT_BYTES = 56 * 1024 * 1024

ADAM_LR = 0.001
ADAM_B1 = 0.9
ADAM_B2 = 0.999
ADAM_EPS = 1e-08
ADAM_WD = 0.01
ADAM_STEP = 10

N_CHIPS = 4
N_DEV = 8


def _params(*sem):
    return pltpu.CompilerParams(dimension_semantics=sem, vmem_limit_bytes=VMEM_LIMIT_BYTES)


class _Rider:
    def __init__(self, arrays, out_shapes, nsem, start, finish, aliases=None):
        self.arrays, self.out_shapes, self.nsem = list(arrays), list(out_shapes), nsem
        self.start, self.finish, self.aliases = start, finish, dict(aliases or {})


def _hosted_call(name, body, *, grid, in_specs, out_specs, out_shape, operands, scratch_shapes=(), rider=None):
    in_specs, out_specs, out_shape, scratch_shapes = list(in_specs), list(out_specs), list(out_shape), list(scratch_shapes)
    cparams = _params(*(["arbitrary"] * len(grid)))
    if rider is None:
        return pl.pallas_call(body, name=name, grid=grid, in_specs=in_specs, out_specs=out_specs, out_shape=out_shape,
                              scratch_shapes=scratch_shapes, compiler_params=cparams)(*operands)
    n_in, n_out, n_sc = len(in_specs), len(out_specs), len(scratch_shapes)
    r_in, r_out = len(rider.arrays), len(rider.out_shapes)

    def hosted(*refs):
        ins, rins = refs[:n_in], refs[n_in:n_in + r_in]
        outs = refs[n_in + r_in:n_in + r_in + n_out]
        routs = refs[n_in + r_in + n_out:n_in + r_in + n_out + r_out]
        rest = refs[n_in + r_in + n_out + r_out:]
        send_sems, recv_sems = rest[n_sc], rest[n_sc + 1]
        first = functools.reduce(jnp.logical_and, [pl.program_id(i) == 0 for i in range(len(grid))])
        last = functools.reduce(jnp.logical_and, [pl.program_id(i) == grid[i] - 1 for i in range(len(grid))])

        @pl.when(first)
        def _():
            rider.start(rins, routs, send_sems, recv_sems)

        body(*ins, *outs, *rest[:n_sc])

        @pl.when(last)
        def _():
            rider.finish(rins, routs, send_sems, recv_sems)

    res = pl.pallas_call(
        hosted, name=name, grid=grid, in_specs=in_specs + [ANY] * r_in, out_specs=out_specs + [ANY] * r_out,
        out_shape=out_shape + rider.out_shapes,
        scratch_shapes=scratch_shapes + [pltpu.SemaphoreType.DMA((rider.nsem,)), pltpu.SemaphoreType.DMA((rider.nsem,))],
        input_output_aliases={n_in + i: n_out + o for i, o in rider.aliases.items()}, compiler_params=cparams,
    )(*operands, *rider.arrays)
    return res[:n_out], res[n_out:]


def _hosted_scalar_call(name, body, *, grid, in_specs, out_specs, out_shape, operands, rider, aliases=None):
    in_specs, out_specs, out_shape = list(in_specs), list(out_specs), list(out_shape)
    n_in, n_out = len(in_specs), len(out_specs)
    if rider is None:
        spec = pltpu.PrefetchScalarGridSpec(num_scalar_prefetch=1, grid=grid, in_specs=in_specs, out_specs=out_specs)
        res = pl.pallas_call(body, name=name, grid_spec=spec, out_shape=out_shape,
                             input_output_aliases={1 + i: o for i, o in (aliases or {}).items()},
                             compiler_params=_params(*(["arbitrary"] * len(grid))))(_place_scalars(), *operands)
        return res, []
    r_in, r_out = len(rider.arrays), len(rider.out_shapes)

    def hosted(place, *refs):
        ins, rins = refs[:n_in], refs[n_in:n_in + r_in]
        outs = refs[n_in + r_in:n_in + r_in + n_out]
        routs = refs[n_in + r_in + n_out:n_in + r_in + n_out + r_out]
        send_sems, recv_sems = refs[-2], refs[-1]
        first = functools.reduce(jnp.logical_and, [pl.program_id(i) == 0 for i in range(len(grid))])
        last = functools.reduce(jnp.logical_and, [pl.program_id(i) == grid[i] - 1 for i in range(len(grid))])

        @pl.when(first)
        def _():
            rider.start(rins, routs, send_sems, recv_sems)

        body(place, *ins, *outs)

        @pl.when(last)
        def _():
            rider.finish(rins, routs, send_sems, recv_sems)

    spec = pltpu.PrefetchScalarGridSpec(
        num_scalar_prefetch=1, grid=grid, in_specs=in_specs + [ANY] * r_in, out_specs=out_specs + [ANY] * r_out,
        scratch_shapes=[pltpu.SemaphoreType.DMA((rider.nsem,)), pltpu.SemaphoreType.DMA((rider.nsem,))])
    alias = {1 + i: o for i, o in (aliases or {}).items()}
    alias.update({1 + n_in + i: n_out + o for i, o in rider.aliases.items()})
    res = pl.pallas_call(hosted, name=name, grid_spec=spec, out_shape=out_shape + rider.out_shapes,
                         input_output_aliases=alias, compiler_params=_params(*(["arbitrary"] * len(grid))),
                         )(_place_scalars(), *operands, *rider.arrays)
    return res[:n_out], res[n_out:]


def _run_rider(name, rider):
    r_in, r_out = len(rider.arrays), len(rider.out_shapes)

    def body(*refs):
        rins, routs, send_sems, recv_sems = refs[:r_in], refs[r_in:r_in + r_out], refs[-2], refs[-1]
        rider.start(rins, routs, send_sems, recv_sems)
        rider.finish(rins, routs, send_sems, recv_sems)

    return pl.pallas_call(
        body, name=name, in_specs=[ANY] * r_in, out_specs=[ANY] * r_out, out_shape=rider.out_shapes,
        scratch_shapes=[pltpu.SemaphoreType.DMA((rider.nsem,)), pltpu.SemaphoreType.DMA((rider.nsem,))],
        input_output_aliases=rider.aliases,
    )(*rider.arrays)


def _row_tile(rows, cap):
    if rows <= cap:
        return rows
    for t in range(cap - cap % 8, 7, -8):
        if rows % t == 0:
            return t
    raise ValueError(f"no row tile for {rows}")


def _dot(a, b):
    return jnp.dot(a.astype(BF16), b.astype(BF16), preferred_element_type=F32)


def _dot_nt(a, b):
    return lax.dot_general(a.astype(BF16), b.astype(BF16), (((1,), (1,)), ((), ())), preferred_element_type=F32)


def _dot_tn(a, b):
    return lax.dot_general(a.astype(BF16), b.astype(BF16), (((0,), (0,)), ((), ())), preferred_element_type=F32)


def _sigmoid(x):
    return 0.5 * jnp.tanh(0.5 * x) + 0.5


_GELU_C = math.sqrt(2.0 / math.pi)


def _gelu(x):
    return 0.5 * x * (1.0 + jnp.tanh(_GELU_C * (x + 0.044715 * x * x * x)))


def _gelu_grad(x):
    th = jnp.tanh(_GELU_C * (x + 0.044715 * x * x * x))
    return 0.5 * (1.0 + th) + 0.5 * x * (1.0 - th * th) * _GELU_C * (1.0 + 3.0 * 0.044715 * x * x)


def _rowwise(name, fn, ins, outs, accs=(), *, rows, tm, ncol=1, rider=None):
    n_in, n_out = len(ins), len(outs)

    def body(*refs):
        res = fn(*[r[...] for r in refs[:n_in]])
        for r, v in zip(refs[n_in:n_in + n_out], res[:n_out]):
            r[...] = v.astype(r.dtype)
        first = pl.program_id(1) == 0
        for r, v in zip(refs[n_in + n_out:], res[n_out:]):
            @pl.when(first)
            def _():
                r[...] = v

            @pl.when(jnp.logical_not(first))
            def _():
                r[...] += v

    in_specs = []
    for _, width, base, kind in ins:
        if kind == "row":
            in_specs.append(pl.BlockSpec((tm, width), lambda j, i, b=base: (i, b + j)))
        else:
            in_specs.append(pl.BlockSpec((1, width), lambda j, i, b=base: (0, b + j)))
    out_specs = [pl.BlockSpec((tm, width), lambda j, i: (i, j)) for _, width, _ in outs]
    out_specs += [pl.BlockSpec((1, width), lambda j, i: (0, j)) for _, width in accs]
    out_shape = [jax.ShapeDtypeStruct((rows, total), dt) for total, _, dt in outs]
    out_shape += [jax.ShapeDtypeStruct((1, total), F32) for total, _ in accs]
    return _hosted_call(name, body, grid=(ncol, rows // tm), in_specs=in_specs, out_specs=out_specs, out_shape=out_shape,
                        operands=[a for a, _, _, _ in ins], rider=rider)


def _mm(name, a, b, *, mode, grid, a_spec, b_spec, o_spec, out_shape, acc_shape, res=None, res_spec=None,
        pair_axis=None, rider=None, epilogue=None):
    nk = grid[2]
    dot = {"nn": _dot, "nt": _dot_nt, "tn": _dot_tn}[mode]
    a_list = list(a) if isinstance(a, tuple) else [a]
    b_list = list(b) if isinstance(b, tuple) else [b]
    na, nb = len(a_list), len(b_list)
    assert (pair_axis is None) == (na + nb == 2)
    direct = nk == 1 and pair_axis is None
    epi_fn, epi_ins, epi_sums = epilogue if epilogue is not None else (None, [], [])
    n_res = 0 if res is None else 1
    n_epi = len(epi_ins)

    def body(*refs):
        a_refs, b_refs = refs[:na], refs[na:na + nb]
        r_ref = None if res is None else refs[na + nb]
        e_refs = refs[na + nb + n_res:na + nb + n_res + n_epi]
        o_ref = refs[na + nb + n_res + n_epi]
        s_refs = refs[na + nb + n_res + n_epi + 1:na + nb + n_res + n_epi + 1 + len(epi_sums)]
        first_rows = pl.program_id(0) == 0

        def finish(v):
            if res is not None:
                v = v + r_ref[...]
            if epi_fn is None:
                o_ref[...] = v.astype(o_ref.dtype)
                return
            outs = epi_fn(v, *[r[...] for r in e_refs])
            o_ref[...] = outs[0].astype(o_ref.dtype)
            for s_ref, part in zip(s_refs, outs[1:]):
                @pl.when(first_rows)
                def _():
                    s_ref[...] = part

                @pl.when(jnp.logical_not(first_rows))
                def _():
                    s_ref[...] += part

        if direct:
            finish(dot(a_refs[0][...], b_refs[0][...]))
            return
        acc_ref = refs[-1]
        k = pl.program_id(2)

        @pl.when(k == 0)
        def _():
            acc_ref[...] = jnp.zeros_like(acc_ref)

        if pair_axis is None:
            acc_ref[...] += dot(a_refs[0][...], b_refs[0][...])
        else:
            lower = pl.program_id(pair_axis) < grid[pair_axis] // 2

            @pl.when(lower)
            def _():
                acc_ref[...] += dot(a_refs[0][...], b_refs[0][...])

            @pl.when(jnp.logical_not(lower))
            def _():
                acc_ref[...] += dot(a_refs[-1][...], b_refs[-1][...])

        @pl.when(k == nk - 1)
        def _():
            finish(acc_ref[...])

    operands = a_list + b_list + ([] if res is None else [res]) + [arr for arr, _ in epi_ins]
    in_specs = (list(a_spec) if na == 2 else [a_spec]) + (list(b_spec) if nb == 2 else [b_spec])
    in_specs += ([] if res is None else [res_spec]) + [spec for _, spec in epi_ins]
    out_specs = [o_spec] + [pl.BlockSpec((1, c), lambda *_: (0, 0)) for c in epi_sums]
    out_shapes = [out_shape] + [jax.ShapeDtypeStruct((1, c), F32) for c in epi_sums]
    got = _hosted_call(name, body, grid=grid, in_specs=in_specs, out_specs=out_specs, out_shape=out_shapes,
                       scratch_shapes=[] if direct else [pltpu.VMEM(acc_shape, F32)], operands=operands, rider=rider)
    mine, rider_outs = (got, None) if rider is None else got
    mine = mine[0] if epilogue is None else tuple(mine)
    return mine if rider is None else (mine, rider_outs)


MM_TILE_BUDGET_BYTES = 36 * 1024 * 1024
MM_TILE_CAP = 2048
ROW_TILE = 1024
GLU_TILE = 1024
ADAMW_TILE = 256


def _mm_tile(t, row_bytes, fixed_bytes):
    cap = max(16, min(MM_TILE_CAP, (MM_TILE_BUDGET_BYTES - fixed_bytes) // row_bytes))
    return _row_tile(t, cap - cap % 16)


def _size(a):
    return jnp.dtype(a.dtype).itemsize


def _mm_fwd_cols(name, a, w3, out_dtype=F32, rider=None):
    t, k = a.shape
    ns = w3.shape[2]
    tm = _mm_tile(t, 2 * k * _size(a) + 2 * ns * jnp.dtype(out_dtype).itemsize, 2 * k * ns * _size(w3))
    return _mm(name, a, w3, mode="nn", grid=(N_CHIPS, t // tm, 1),
               a_spec=pl.BlockSpec((tm, k), lambda j, i, kk: (i, 0)),
               b_spec=pl.BlockSpec((None, k, ns), lambda j, i, kk: (j, 0, 0)),
               o_spec=pl.BlockSpec((tm, ns), lambda j, i, kk: (i, j)),
               out_shape=jax.ShapeDtypeStruct((t, N_CHIPS * ns), out_dtype), acc_shape=(tm, ns), rider=rider)


def _mm_bwd_cols(name, d, w3, out_dtype=F32, rider=None, epilogue=None):
    pair = isinstance(d, tuple)
    t = d[0].shape[0] if pair else d.shape[0]
    k, ns = w3.shape[1], w3.shape[2]
    dsize = _size(d[0] if pair else d)
    tm = _mm_tile(t, (4 if pair else 2) * ns * dsize + 2 * k * jnp.dtype(out_dtype).itemsize + 4 * k
                  + _row_epilogue(epilogue, 8)[1], 2 * k * ns * _size(w3))
    half = N_CHIPS // 2
    if pair:
        a_spec = (pl.BlockSpec((tm, ns), lambda i, j, kk: (i, jnp.minimum(kk, half - 1))),
                  pl.BlockSpec((tm, ns), lambda i, j, kk: (i, jnp.maximum(kk - half, 0))))
    else:
        a_spec = pl.BlockSpec((tm, ns), lambda i, j, kk: (i, kk))
    return _mm(name, d, w3, mode="nt", grid=(t // tm, 1, N_CHIPS), a_spec=a_spec,
               b_spec=pl.BlockSpec((None, k, ns), lambda i, j, kk: (kk, 0, 0)),
               o_spec=pl.BlockSpec((tm, k), lambda i, j, kk: (i, 0)),
               out_shape=jax.ShapeDtypeStruct((t, k), out_dtype), acc_shape=(tm, k), pair_axis=2 if pair else None,
               rider=rider, epilogue=_row_epilogue(epilogue, tm)[0])


def _mm_wgrad_cols(name, a, d, rider=None):
    pair = isinstance(d, tuple)
    t, k = a.shape
    ns = (2 * d[0].shape[1] if pair else d.shape[1]) // N_CHIPS
    dsize = _size(d[0] if pair else d)
    tk = _mm_tile(t, 2 * k * _size(a) + (4 if pair else 2) * ns * dsize, k * ns * (4 + 2 * 2))
    half = N_CHIPS // 2
    if pair:
        b_spec = (pl.BlockSpec((tk, ns), lambda j, i, kk: (jnp.where(j < half, kk, 0), jnp.minimum(j, half - 1))),
                  pl.BlockSpec((tk, ns), lambda j, i, kk: (jnp.where(j < half, 0, kk), jnp.maximum(j - half, 0))))
    else:
        b_spec = pl.BlockSpec((tk, ns), lambda j, i, kk: (kk, j))
    return _mm(name, a, d, mode="tn", grid=(N_CHIPS, 1, t // tk),
               a_spec=pl.BlockSpec((tk, k), lambda j, i, kk: (kk, 0)), b_spec=b_spec,
               o_spec=pl.BlockSpec((None, k, ns), lambda j, i, kk: (j, 0, 0)),
               out_shape=jax.ShapeDtypeStruct((N_CHIPS, k, ns), BF16), acc_shape=(k, ns),
               pair_axis=0 if pair else None, rider=rider)


MM_BLOCK_CAP = 1408


def _row_epilogue(epilogue, tm):
    if epilogue is None:
        return None, 0
    fn, arrays, sums = epilogue
    specs = [pl.BlockSpec((1, x.shape[1]), lambda i, j, kk: (0, 0)) if x.shape[0] == 1 else
             pl.BlockSpec((tm, x.shape[1]), lambda i, j, kk: (i, 0)) for x in arrays]
    return (fn, list(zip(arrays, specs)), list(sums)), sum(2 * x.shape[1] * _size(x) for x in arrays if x.shape[0] > 1)


def _mm_fwd_rows(name, a, w, res=None, out_dtype=F32, epilogue=None, rider=None):
    t, k = a.shape
    n = w.shape[1]
    tk = k if k <= MM_BLOCK_CAP else MM_BLOCK_CAP
    assert k % tk == 0
    row_bytes = 2 * tk * _size(a) + 2 * n * jnp.dtype(out_dtype).itemsize + (0 if res is None else 2 * n * 4) + 4 * n
    row_bytes += _row_epilogue(epilogue, 8)[1]
    tm = _mm_tile(t, row_bytes, 2 * tk * n * _size(w))
    return _mm(name, a, w, mode="nn", grid=(t // tm, 1, k // tk),
               a_spec=pl.BlockSpec((tm, tk), lambda i, j, kk: (i, kk)),
               b_spec=pl.BlockSpec((tk, n), lambda i, j, kk: (kk, 0)),
               o_spec=pl.BlockSpec((tm, n), lambda i, j, kk: (i, 0)),
               out_shape=jax.ShapeDtypeStruct((t, n), out_dtype), acc_shape=(tm, n),
               res=res, res_spec=None if res is None else pl.BlockSpec((tm, n), lambda i, j, kk: (i, 0)),
               epilogue=_row_epilogue(epilogue, tm)[0], rider=rider)


def _mm_bwd_rows(name, d, w, out_dtype=F32):
    t, n = d.shape
    k = w.shape[0]
    tn = k if k <= MM_BLOCK_CAP else MM_BLOCK_CAP
    assert k % tn == 0
    tm = _mm_tile(t, 2 * n * _size(d) + 2 * tn * jnp.dtype(out_dtype).itemsize, 2 * tn * n * _size(w))
    return _mm(name, d, w, mode="nt", grid=(t // tm, k // tn, 1),
               a_spec=pl.BlockSpec((tm, n), lambda i, j, kk: (i, 0)),
               b_spec=pl.BlockSpec((tn, n), lambda i, j, kk: (j, 0)),
               o_spec=pl.BlockSpec((tm, tn), lambda i, j, kk: (i, j)),
               out_shape=jax.ShapeDtypeStruct((t, k), out_dtype), acc_shape=(tm, tn))


def _mm_wgrad_rows(name, a, d):
    t, k = a.shape
    n = d.shape[1]
    nblk = next(b for b in (1, 2, 4) if (k // b) % LANES == 0 and k // b <= MM_BLOCK_CAP)
    ks = k // nblk
    tk = _mm_tile(t, 2 * ks * _size(a) + 2 * n * _size(d), ks * n * (4 + 2 * 2))
    return _mm(name, a, d, mode="tn", grid=(nblk, 1, t // tk),
               a_spec=pl.BlockSpec((tk, ks), lambda j, i, kk: (kk, j)),
               b_spec=pl.BlockSpec((tk, n), lambda j, i, kk: (kk, 0)),
               o_spec=pl.BlockSpec((ks, n), lambda j, i, kk: (j, 0)),
               out_shape=jax.ShapeDtypeStruct((k, n), BF16), acc_shape=(ks, n))


def _s5_discretize(a_re, a_im, log_dt, b_re, b_im):
    dt = jnp.exp(log_dt)[:, None]
    mag = jnp.exp(a_re * dt)
    ang = a_im * dt
    lb_re = mag * jnp.cos(ang)
    lb_im = mag * jnp.sin(ang)
    den = a_re * a_re + a_im * a_im
    n_re = lb_re - 1.0
    n_im = lb_im
    co_re = ((n_re * a_re + n_im * a_im) / den)[..., None]
    co_im = ((n_im * a_re - n_re * a_im) / den)[..., None]
    bb_re = co_re * b_re - co_im * b_im
    bb_im = co_re * b_im + co_im * b_re
    return lb_re, lb_im, bb_re, bb_im


def _s5_in_blocks(bb):
    g = bb.shape[0]
    nb = g // S5_BLOCK_GROUPS
    t = bb.reshape(nb, S5_BLOCK_GROUPS, S5_STATE, S5_GROUP).transpose(0, 1, 3, 2)
    eye = jnp.eye(S5_BLOCK_GROUPS, dtype=bb.dtype)
    full = t[:, :, :, None, :] * eye[None, :, None, :, None]
    return full.reshape(nb, S5_BLOCK_GROUPS * S5_GROUP, S5_BLOCK_GROUPS * S5_STATE)


def _s5_in_blocks_diag(blocks):
    nb = blocks.shape[0]
    t = blocks.reshape(nb, S5_BLOCK_GROUPS, S5_GROUP, S5_BLOCK_GROUPS, S5_STATE)
    d = jnp.einsum("bghgp->bghp", t)
    return d.transpose(0, 1, 3, 2).reshape(nb * S5_BLOCK_GROUPS, S5_STATE, S5_GROUP)


def _s5_out_blocks(c):
    g = c.shape[0]
    nb = g // S5_BLOCK_GROUPS
    t = c.reshape(nb, S5_BLOCK_GROUPS, S5_GROUP, S5_STATE).transpose(0, 1, 3, 2)
    eye = jnp.eye(S5_BLOCK_GROUPS, dtype=c.dtype)
    full = t[:, :, :, None, :] * eye[None, :, None, :, None]
    return full.reshape(nb, S5_BLOCK_GROUPS * S5_STATE, S5_BLOCK_GROUPS * S5_GROUP)


def _s5_out_blocks_diag(blocks):
    nb = blocks.shape[0]
    t = blocks.reshape(nb, S5_BLOCK_GROUPS, S5_STATE, S5_BLOCK_GROUPS, S5_GROUP)
    d = jnp.einsum("bgpgh->bgph", t)
    return d.transpose(0, 1, 3, 2).reshape(nb * S5_BLOCK_GROUPS, S5_GROUP, S5_STATE)


def _s5_scan_tables(lr, li, reverse):
    def cmul(a, b):
        return a[0] * b[0] - a[1] * b[1], a[0] * b[1] + a[1] * b[0]

    lam = (lr, -li) if reverse else (lr, li)
    pw = [lam]
    for _ in range(SUBLANES - 1):
        pw.append(cmul(pw[-1], lam))
    sub = jnp.arange(SUBLANES)[:, None]
    rows = []
    for s in (1, 2, 4):
        keep = (sub <= SUBLANES - 1 - s) if reverse else (sub >= s)
        rows.append(jnp.where(keep, pw[s - 1][0][None, :], 0.0))
        rows.append(jnp.where(keep, pw[s - 1][1][None, :], 0.0))
    order = list(range(SUBLANES - 1, -1, -1)) if reverse else list(range(SUBLANES))
    rows.append(jnp.stack([pw[i][0] for i in order]))
    rows.append(jnp.stack([pw[i][1] for i in order]))
    return jnp.concatenate(rows, axis=0)


def _s5_scan(vre_ref, vim_ref, coef_ref, seq, width, reverse, xre_ref=None, xim_ref=None):
    nt = seq // SUBLANES
    nl = width // LANES
    per = 2 if xre_ref is None else 4
    sub = lax.broadcasted_iota(jnp.int32, (SUBLANES, LANES), 0)

    def step(k, carry):
        kk = (nt - 1 - k) if reverse else k
        rows = pl.ds(pl.multiple_of(kk * SUBLANES, SUBLANES), SUBLANES)
        out = []
        for j in range(nl):
            lanes = slice(j * LANES, (j + 1) * LANES)
            co = [coef_ref[SUBLANES * q:SUBLANES * (q + 1), lanes] for q in range(8)]
            cr, ci = carry[per * j], carry[per * j + 1]
            vr = vre_ref[rows, lanes]
            vi = vim_ref[rows, lanes]
            for q, s in enumerate((1, 2, 4)):
                sh = SUBLANES - s if reverse else s
                rr = pltpu.roll(vr, sh, 0)
                ri = pltpu.roll(vi, sh, 0)
                ar, ai = co[2 * q], co[2 * q + 1]
                vr, vi = vr + ar * rr - ai * ri, vi + ar * ri + ai * rr
            edge = 0 if reverse else SUBLANES - 1
            cbr = jnp.broadcast_to(cr[edge:edge + 1, :], (SUBLANES, LANES))
            cbi = jnp.broadcast_to(ci[edge:edge + 1, :], (SUBLANES, LANES))
            pr, pi = co[6], co[7]
            vr, vi = vr + pr * cbr - pi * cbi, vi + pr * cbi + pi * cbr
            vre_ref[rows, lanes] = vr
            vim_ref[rows, lanes] = vi
            out += [vr, vi]
            if xre_ref is not None:
                nr = jnp.where(sub == SUBLANES - 1, cbr, pltpu.roll(vr, SUBLANES - 1, 0))
                ni = jnp.where(sub == SUBLANES - 1, cbi, pltpu.roll(vi, SUBLANES - 1, 0))
                xr = xre_ref[rows, lanes]
                xi = xim_ref[rows, lanes]
                out += [carry[per * j + 2] + nr * xr + ni * xi, carry[per * j + 3] + ni * xr - nr * xi]
        return tuple(out)

    zero = jnp.zeros((SUBLANES, LANES), F32)
    res = lax.fori_loop(0, nt, step, (zero,) * (per * nl))
    if xre_ref is None:
        return None
    return jnp.concatenate(
        [jnp.concatenate([jnp.sum(res[per * j + 2], axis=0, keepdims=True) for j in range(nl)], axis=1),
         jnp.concatenate([jnp.sum(res[per * j + 3], axis=0, keepdims=True) for j in range(nl)], axis=1)], axis=0)


def _s5_fwd(z, bre3, bim3, cre3, cim3, coef, dskip, *, nseq, seq, rider=None):
    nb = bre3.shape[0]
    ch, ns = bre3.shape[1], bre3.shape[2]

    def body(za_ref, bre_ref, bim_ref, cre_ref, cim_ref, coef_ref, d_ref, y_ref, xre_ref, xim_ref):
        za = za_ref[...]
        xre_ref[...] = _dot(za, bre_ref[...])
        xim_ref[...] = _dot(za, bim_ref[...])
        _s5_scan(xre_ref, xim_ref, coef_ref, seq, ns, False)
        y_ref[...] = _dot(xre_ref[...], cre_ref[...]) - _dot(xim_ref[...], cim_ref[...]) + d_ref[...] * za

    blk3 = lambda r, c: pl.BlockSpec((None, r, c), lambda b, j: (j, 0, 0))
    return _hosted_call(
        "s5_fwd", body, grid=(nseq, nb),
        in_specs=[pl.BlockSpec((seq, ch), lambda b, j: (b, j)), blk3(ch, ns), blk3(ch, ns), blk3(ns, ch), blk3(ns, ch),
                  pl.BlockSpec((8 * SUBLANES, ns), lambda b, j: (0, j)), pl.BlockSpec((1, ch), lambda b, j: (0, j))],
        out_specs=[pl.BlockSpec((seq, ch), lambda b, j: (b, j)), pl.BlockSpec((seq, ns), lambda b, j: (b, j)),
                   pl.BlockSpec((seq, ns), lambda b, j: (b, j))],
        out_shape=[jax.ShapeDtypeStruct((nseq * seq, nb * ch), F32), jax.ShapeDtypeStruct((nseq * seq, nb * ns), F32),
                   jax.ShapeDtypeStruct((nseq * seq, nb * ns), F32)],
        operands=(z, bre3, bim3, cre3, cim3, coef, dskip), rider=rider)


def _s5_bwd(dy, z, xre, xim, bre3, bim3, cre3, cim3, coef_rev, dskip, *, nseq, seq, rider=None):
    nb = bre3.shape[0]
    ch, ns = bre3.shape[1], bre3.shape[2]

    def body(dy_ref, za_ref, xre_ref, xim_ref, bre_ref, bim_ref, cre_ref, cim_ref, coef_ref, d_ref,
             dza_ref, dbre_ref, dbim_ref, dcre_ref, dcim_ref, dlam_ref, dd_ref, are_ref, aim_ref):
        dy = dy_ref[...]
        za = za_ref[...]
        are_ref[...] = _dot_nt(dy, cre_ref[...])
        aim_ref[...] = -_dot_nt(dy, cim_ref[...])
        dlam = _s5_scan(are_ref, aim_ref, coef_ref, seq, ns, True, xre_ref, xim_ref)
        are = are_ref[...]
        aim = aim_ref[...]
        dza_ref[...] = (_dot_nt(are, bre_ref[...]) + _dot_nt(aim, bim_ref[...]) + d_ref[...] * dy).astype(dza_ref.dtype)
        parts = (_dot_tn(za, are), _dot_tn(za, aim), _dot_tn(xre_ref[...], dy), -_dot_tn(xim_ref[...], dy),
                 dlam, jnp.sum(dy * za, axis=0, keepdims=True))
        first = pl.program_id(1) == 0
        for r, v in zip((dbre_ref, dbim_ref, dcre_ref, dcim_ref, dlam_ref, dd_ref), parts):
            @pl.when(first)
            def _():
                r[...] = v

            @pl.when(jnp.logical_not(first))
            def _():
                r[...] += v

    blk3 = lambda r, c: pl.BlockSpec((None, r, c), lambda j, b: (j, 0, 0))
    tok = lambda c: pl.BlockSpec((seq, c), lambda j, b: (b, j))
    return _hosted_call(
        "s5_bwd", body, grid=(nb, nseq),
        in_specs=[tok(ch), tok(ch), tok(ns), tok(ns), blk3(ch, ns), blk3(ch, ns), blk3(ns, ch), blk3(ns, ch),
                  pl.BlockSpec((8 * SUBLANES, ns), lambda j, b: (0, j)), pl.BlockSpec((1, ch), lambda j, b: (0, j))],
        out_specs=[tok(ch), blk3(ch, ns), blk3(ch, ns), blk3(ns, ch), blk3(ns, ch),
                   pl.BlockSpec((None, 2, ns), lambda j, b: (j, 0, 0)), pl.BlockSpec((1, ch), lambda j, b: (0, j))],
        out_shape=[jax.ShapeDtypeStruct((nseq * seq, nb * ch), BF16),
                   jax.ShapeDtypeStruct((nb, ch, ns), F32), jax.ShapeDtypeStruct((nb, ch, ns), F32),
                   jax.ShapeDtypeStruct((nb, ns, ch), F32), jax.ShapeDtypeStruct((nb, ns, ch), F32),
                   jax.ShapeDtypeStruct((nb, 2, ns), F32), jax.ShapeDtypeStruct((1, nb * ch), F32)],
        scratch_shapes=[pltpu.VMEM((seq, ns), F32), pltpu.VMEM((seq, ns), F32)],
        operands=(dy, z, xre, xim, bre3, bim3, cre3, cim3, coef_rev, dskip), rider=rider)


def _glu_fwd(y, wglu, bglu):
    t, w = y.shape
    tm = _row_tile(t, GLU_TILE)

    def body(y_ref, w_ref, b_ref, a0_ref, gl_ref, a_ref):
        a0 = _gelu(y_ref[...])
        gl = _dot(a0, w_ref[...])
        a0_ref[...] = a0.astype(a0_ref.dtype)
        gl_ref[...] = gl
        a_ref[...] = (a0 * _sigmoid(gl + b_ref[...])).astype(a_ref.dtype)

    tok = pl.BlockSpec((tm, w), lambda i: (i, 0))
    return pl.pallas_call(
        body, name="s5_glu", grid=(t // tm,),
        in_specs=[tok, pl.BlockSpec((w, w), lambda i: (0, 0)), pl.BlockSpec((1, w), lambda i: (0, 0))],
        out_specs=[tok, tok, tok],
        out_shape=[jax.ShapeDtypeStruct((t, w), BF16), jax.ShapeDtypeStruct((t, w), F32), jax.ShapeDtypeStruct((t, w), BF16)],
        compiler_params=_params("arbitrary"),
    )(y, wglu, bglu)


def _glu_bwd(y, gl, bglu, da, wglu):
    t, w = y.shape
    tm = _row_tile(t, GLU_TILE)

    def body(y_ref, gl_ref, b_ref, da_ref, w_ref, dgl_ref, dy_ref, db_ref):
        yv = y_ref[...]
        dav = da_ref[...]
        s = _sigmoid(gl_ref[...] + b_ref[...])
        dgl = dav * _gelu(yv) * s * (1.0 - s)
        dgl_ref[...] = dgl.astype(dgl_ref.dtype)
        dy_ref[...] = (dav * s + _dot_nt(dgl, w_ref[...])) * _gelu_grad(yv)
        part = jnp.sum(dgl, axis=0, keepdims=True)
        first = pl.program_id(0) == 0

        @pl.when(first)
        def _():
            db_ref[...] = part

        @pl.when(jnp.logical_not(first))
        def _():
            db_ref[...] += part

    tok = pl.BlockSpec((tm, w), lambda i: (i, 0))
    vec = pl.BlockSpec((1, w), lambda i: (0, 0))
    return pl.pallas_call(
        body, name="s5_glu_bwd", grid=(t // tm,),
        in_specs=[tok, tok, vec, tok, pl.BlockSpec((w, w), lambda i: (0, 0))], out_specs=[tok, tok, vec],
        out_shape=[jax.ShapeDtypeStruct((t, w), BF16), jax.ShapeDtypeStruct((t, w), F32), jax.ShapeDtypeStruct((1, w), F32)],
        compiler_params=_params("arbitrary"),
    )(y, gl, bglu, da, wglu)


def _cumsum_rows(x, reverse=False):
    n = x.shape[0]
    row = lax.broadcasted_iota(jnp.int32, x.shape, 0)
    s = 1
    while s < n:
        if reverse:
            x = x + jnp.where(row < n - s, pltpu.roll(x, n - s, 0), 0.0)
        else:
            x = x + jnp.where(row >= s, pltpu.roll(x, s, 0), 0.0)
        s *= 2
    return x


def _hg_gates(zq, zf, lb):
    sg = _sigmoid(zf)
    f = lb + (1.0 - lb) * sg
    sq = _sigmoid(zq)
    qa = zq * sq * (HEAD ** -0.5)
    b = _cumsum_rows(jnp.log(f))
    return sg, f, sq, qa, 1.0 - f, b


SUB = 16


def _hg_scores(qa, kk, b):
    c = qa.shape[0]
    row = lax.broadcasted_iota(jnp.int32, qa.shape, 0)
    pos = jnp.bitwise_and(row, SUB - 1)
    dmat = lax.broadcasted_iota(jnp.int32, (c, c), 0) - lax.broadcasted_iota(jnp.int32, (c, c), 1)
    p = jnp.zeros((c, c), F32)
    for d in range(SUB):
        if d == 0:
            fd = qa * kk
        else:
            e = jnp.exp(jnp.minimum(b - pltpu.roll(b, d, 0), 0.0))
            fd = jnp.where(pos >= d, qa * pltpu.roll(kk, d, 0) * e, 0.0)
        p = jnp.where(dmat == d, jnp.sum(fd, axis=1, keepdims=True), p)
    col = lax.broadcasted_iota(jnp.int32, (SUB, c), 1)
    blocks = [jnp.zeros((SUB, c), F32)]
    for r0 in range(SUB, c, SUB):
        beta = b[r0 - 1:r0, :]
        qt = qa[r0:r0 + SUB] * jnp.exp(b[r0:r0 + SUB] - beta)
        kt = kk * jnp.exp(jnp.minimum(beta - b, 0.0))
        blocks.append(jnp.where(col < r0, _dot_nt(qt, kt), 0.0))
    return p + jnp.concatenate(blocks, axis=0)


def _hg_scores_bwd(dp, qa, kk, b):
    c = qa.shape[0]
    row = lax.broadcasted_iota(jnp.int32, qa.shape, 0)
    pos = jnp.bitwise_and(row, SUB - 1)
    dmat = lax.broadcasted_iota(jnp.int32, (c, c), 0) - lax.broadcasted_iota(jnp.int32, (c, c), 1)
    dqa = jnp.zeros_like(qa)
    dkk = jnp.zeros_like(qa)
    db = jnp.zeros_like(qa)
    for d in range(SUB):
        dcol = jnp.sum(jnp.where(dmat == d, dp, 0.0), axis=1, keepdims=True)
        if d == 0:
            dqa = dqa + dcol * kk
            dkk = dkk + dcol * qa
        else:
            e = jnp.exp(jnp.minimum(b - pltpu.roll(b, d, 0), 0.0))
            w = jnp.where(pos >= d, dcol * e, 0.0)
            kr = pltpu.roll(kk, d, 0)
            dqa = dqa + w * kr
            tmp = w * qa
            dkk = dkk + pltpu.roll(tmp, c - d, 0)
            x = tmp * kr
            db = db + x - pltpu.roll(x, c - d, 0)
    col = lax.broadcasted_iota(jnp.int32, (SUB, c), 1)
    dq_blocks = [jnp.zeros((SUB, qa.shape[1]), F32)]
    db_blocks = [jnp.zeros((SUB, qa.shape[1]), F32)]
    for r0 in range(SUB, c, SUB):
        beta = b[r0 - 1:r0, :]
        eq = jnp.exp(b[r0:r0 + SUB] - beta)
        ek = jnp.exp(jnp.minimum(beta - b, 0.0))
        qt = qa[r0:r0 + SUB] * eq
        kt = kk * ek
        dpi = jnp.where(col < r0, dp[r0:r0 + SUB, :], 0.0)
        dqt = _dot(dpi, kt)
        dkt = _dot_tn(dpi, qt)
        dq_blocks.append(dqt * eq)
        db_blocks.append(dqt * qt)
        dkk = dkk + dkt * ek
        db = db - dkt * kt
    return dqa + jnp.concatenate(dq_blocks, axis=0), dkk, db + jnp.concatenate(db_blocks, axis=0)


def _hg_chunks_per_step(seq):
    nc = seq // CHUNK
    cps = next(k for k in (4, 2, 1) if nc % k == 0)
    return nc, cps, nc // cps


def _hg_fwd(z, lbrow, gain, *, nseq, seq, heads, qoff, rider=None):
    nc, cps, nblk = _hg_chunks_per_step(seq)
    blk = cps * CHUNK
    zspec = lambda off: pl.BlockSpec((blk, HEAD), lambda h, b, n, off=off: (b * nblk + n, off + h))

    def body(zq_ref, zf_ref, zi_ref, zg_ref, lb_ref, gn_ref, o_ref, yb_ref, st_ref, sc_ref, state):
        @pl.when(pl.program_id(2) == 0)
        def _():
            state[...] = jnp.zeros_like(state)

        lb = lb_ref[...]
        gain_v = gn_ref[...]

        def chunk(ci, carry):
            rows = pl.ds(pl.multiple_of(ci * CHUNK, CHUNK), CHUNK)
            st = state[...]
            st_ref[ci] = st
            zi = zi_ref[rows, :]
            zg = zg_ref[rows, :]
            _, _, _, qa, kk, b = _hg_gates(zq_ref[rows, :], zf_ref[rows, :], lb)
            scores = _hg_scores(qa, kk, b).astype(BF16)
            sc_ref[rows, :] = scores
            o = _dot_nt(qa * jnp.exp(b), st) + _dot(scores, zi)
            bl = b[CHUNK - 1:CHUNK, :]
            state[...] = st * jnp.exp(bl) + _dot_tn(zi, kk * jnp.exp(bl - b))
            o_ref[rows, :] = o
            r = lax.rsqrt(jnp.mean(o * o, axis=1, keepdims=True) + EPS)
            yb_ref[rows, :] = (o * r * gain_v * zg * _sigmoid(zg)).astype(yb_ref.dtype)
            return carry

        lax.fori_loop(0, cps, chunk, 0, unroll=True)

    tok = pl.BlockSpec((blk, HEAD), lambda h, b, n: (b * nblk + n, h))
    vec = pl.BlockSpec((1, HEAD), lambda h, b, n: (0, h))
    rows = nseq * seq
    return _hosted_call(
        "hgrn2_fwd", body, grid=(heads, nseq, nblk),
        in_specs=[zspec(qoff), zspec(qoff + heads), zspec(qoff + 2 * heads), zspec(qoff + 3 * heads), vec, vec],
        out_specs=[tok, tok, pl.BlockSpec((None, None, cps, HEAD, HEAD), lambda h, b, n: (h, b, n, 0, 0)),
                   pl.BlockSpec((None, blk, CHUNK), lambda h, b, n: (h, b * nblk + n, 0))],
        out_shape=[jax.ShapeDtypeStruct((rows, heads * HEAD), F32), jax.ShapeDtypeStruct((rows, heads * HEAD), BF16),
                   jax.ShapeDtypeStruct((heads, nseq, nc, HEAD, HEAD), F32),
                   jax.ShapeDtypeStruct((heads, rows, CHUNK), BF16)],
        scratch_shapes=[pltpu.VMEM((HEAD, HEAD), F32)], operands=(z, z, z, z, lbrow, gain), rider=rider)


def _hg_bwd(dyb, z, o, states, scores, lbrow, gain, *, nseq, seq, heads, qoff, rider=None):
    nc, cps, nblk = _hg_chunks_per_step(seq)
    blk = cps * CHUNK
    rev = lambda n: nblk - 1 - n
    zspec = lambda off: pl.BlockSpec((blk, HEAD), lambda h, b, n, off=off: (b * nblk + rev(n), off + h))

    def body(dyb_ref, zq_ref, zf_ref, zi_ref, zg_ref, o_ref, st_ref, sc_ref, lb_ref, gn_ref,
             dzq_ref, dzf_ref, dzi_ref, dzg_ref, dlb_ref, dgn_ref, dstate):
        @pl.when(pl.program_id(2) == 0)
        def _():
            dstate[...] = jnp.zeros_like(dstate)

        @pl.when(jnp.logical_and(pl.program_id(1) == 0, pl.program_id(2) == 0))
        def _():
            dlb_ref[...] = jnp.zeros_like(dlb_ref)
            dgn_ref[...] = jnp.zeros_like(dgn_ref)

        lb = lb_ref[...]
        gain_v = gn_ref[...]
        c = CHUNK
        causal = lax.broadcasted_iota(jnp.int32, (c, c), 0) >= lax.broadcasted_iota(jnp.int32, (c, c), 1)

        def chunk(step, carry):
            ci = cps - 1 - step
            rows = pl.ds(pl.multiple_of(ci * CHUNK, CHUNK), CHUNK)
            zq = zq_ref[rows, :]
            zi = zi_ref[rows, :]
            zg = zg_ref[rows, :]
            sg, f, sq, qa, kk, b = _hg_gates(zq, zf_ref[rows, :], lb)
            eb = jnp.exp(b)
            qt = qa * eb
            bl = b[c - 1:c, :]
            ebl = jnp.exp(bl)
            ekb = jnp.exp(bl - b)
            kh = kk * ekb
            st = st_ref[ci]
            dst = dstate[...]
            o = o_ref[rows, :]
            r = lax.rsqrt(jnp.mean(o * o, axis=1, keepdims=True) + EPS)
            oh = o * r
            szg = _sigmoid(zg)
            dyb = dyb_ref[rows, :]
            don = dyb * zg * szg
            dzg_ref[rows, :] = (dyb * oh * gain_v * szg * (1.0 + zg * (1.0 - szg))).astype(dzg_ref.dtype)
            doh = don * gain_v
            do = r * (doh - oh * jnp.mean(doh * oh, axis=1, keepdims=True))
            dqt = _dot(do, st)
            dp = jnp.where(causal, _dot_nt(do, zi), 0.0)
            dzi_ref[rows, :] = (_dot_tn(sc_ref[rows, :], do) + _dot_nt(kh, dst)).astype(dzi_ref.dtype)
            dkh = _dot(zi, dst)
            dbl = jnp.sum(dkh * kh, axis=0, keepdims=True) + jnp.sum(dst * st, axis=0, keepdims=True) * ebl
            dstate[...] = _dot_tn(do, qt) + dst * ebl
            dqa_s, dkk_s, db_s = _hg_scores_bwd(dp, qa, kk, b)
            dqa = dqt * eb + dqa_s
            dkk = dkh * ekb + dkk_s
            db = dqt * qt - dkh * kh + db_s
            row = lax.broadcasted_iota(jnp.int32, db.shape, 0)
            db = db + jnp.where(row == c - 1, dbl, 0.0)
            df = _cumsum_rows(db, reverse=True) / f - dkk
            dzf_ref[rows, :] = (df * (1.0 - lb) * sg * (1.0 - sg)).astype(dzf_ref.dtype)
            dzq_ref[rows, :] = (dqa * (HEAD ** -0.5) * sq * (1.0 + zq * (1.0 - sq))).astype(dzq_ref.dtype)
            dlb_ref[...] += jnp.sum(df * (1.0 - sg), axis=0, keepdims=True)
            dgn_ref[...] += jnp.sum(don * oh, axis=0, keepdims=True)
            return carry

        lax.fori_loop(0, cps, chunk, 0, unroll=True)

    tok = pl.BlockSpec((blk, HEAD), lambda h, b, n: (b * nblk + rev(n), h))
    vec = pl.BlockSpec((1, HEAD), lambda h, b, n: (0, h))
    rows = nseq * seq
    return _hosted_call(
        "hgrn2_bwd", body, grid=(heads, nseq, nblk),
        in_specs=[tok, zspec(qoff), zspec(qoff + heads), zspec(qoff + 2 * heads), zspec(qoff + 3 * heads), tok,
                  pl.BlockSpec((None, None, cps, HEAD, HEAD), lambda h, b, n: (h, b, rev(n), 0, 0)),
                  pl.BlockSpec((None, blk, CHUNK), lambda h, b, n: (h, b * nblk + rev(n), 0)), vec, vec],
        out_specs=[tok, tok, tok, tok, vec, vec],
        out_shape=[jax.ShapeDtypeStruct((rows, heads * HEAD), BF16)] * 4
        + [jax.ShapeDtypeStruct((1, heads * HEAD), F32)] * 2,
        scratch_shapes=[pltpu.VMEM((HEAD, HEAD), F32)],
        operands=(dyb, z, z, z, z, o, states, scores, lbrow, gain), rider=rider)


def _shift_rows(x, k):
    n = x.shape[0]
    r = pltpu.roll(x, k % n, 0)
    sub = lax.broadcasted_iota(jnp.int32, (SUBLANES, x.shape[1]), 0)
    if k > 0:
        return jnp.concatenate([jnp.where(sub >= k, r[0:SUBLANES], 0.0), r[SUBLANES:]], axis=0)
    return jnp.concatenate([r[:n - SUBLANES], jnp.where(sub < SUBLANES + k, r[n - SUBLANES:], 0.0)], axis=0)


def _conv_taps(h, w, bias):
    h1 = _shift_rows(h, 1)
    h2 = _shift_rows(h, 2)
    return h2 * w[0:1, :] + h1 * w[1:2, :] + h * w[2:3, :] + bias, h1, h2


def _conv_fwd(h, wconv, bconv, *, nseq, seq):
    ff2 = h.shape[1]
    ncol = ff2 // 2 // LANES

    def body(hg_ref, hv_ref, wg_ref, wv_ref, bg_ref, bv_ref, a_ref):
        g, _, _ = _conv_taps(hg_ref[...].astype(F32), wg_ref[...], bg_ref[...])
        v, _, _ = _conv_taps(hv_ref[...].astype(F32), wv_ref[...], bv_ref[...])
        a_ref[...] = (g * _sigmoid(g) * v).astype(a_ref.dtype)

    tok = lambda off: pl.BlockSpec((seq, LANES), lambda j, b, off=off: (b, off + j))
    wsp = lambda off: pl.BlockSpec((CONV_W, LANES), lambda j, b, off=off: (0, off + j))
    bsp = lambda off: pl.BlockSpec((1, LANES), lambda j, b, off=off: (0, off + j))
    return pl.pallas_call(
        body, name="conv_fwd", grid=(ncol, nseq),
        in_specs=[tok(0), tok(ncol), wsp(0), wsp(ncol), bsp(0), bsp(ncol)],
        out_specs=tok(0), out_shape=jax.ShapeDtypeStruct((nseq * seq, ff2 // 2), BF16),
        compiler_params=_params("arbitrary", "arbitrary"),
    )(h, h, wconv, wconv, bconv, bconv)


def _conv_bwd(da, h, wconv, bconv, *, nseq, seq):
    ff2 = h.shape[1]
    ncol = ff2 // 2 // LANES

    def half_bwd(d, hcur, h1, h2, w):
        d1 = _shift_rows(d, -1)
        d2 = _shift_rows(d, -2)
        dh = d * w[2:3, :] + d1 * w[1:2, :] + d2 * w[0:1, :]
        stats = jnp.concatenate(
            [jnp.sum(h2 * d, axis=0, keepdims=True), jnp.sum(h1 * d, axis=0, keepdims=True),
             jnp.sum(hcur * d, axis=0, keepdims=True), jnp.sum(d, axis=0, keepdims=True),
             jnp.zeros((SUBLANES - 4, d.shape[1]), F32)], axis=0)
        return dh, stats

    def body(da_ref, hg_ref, hv_ref, wg_ref, wv_ref, bg_ref, bv_ref, dhg_ref, dhv_ref, sg_ref, sv_ref):
        hg = hg_ref[...].astype(F32)
        hv = hv_ref[...].astype(F32)
        wg = wg_ref[...]
        wv = wv_ref[...]
        g, g1, g2 = _conv_taps(hg, wg, bg_ref[...])
        v, v1, v2 = _conv_taps(hv, wv, bv_ref[...])
        da = da_ref[...].astype(F32)
        s = _sigmoid(g)
        dhg, stg = half_bwd(da * v * s * (1.0 + g * (1.0 - s)), hg, g1, g2, wg)
        dhv, stv = half_bwd(da * g * s, hv, v1, v2, wv)
        dhg_ref[...] = dhg.astype(dhg_ref.dtype)
        dhv_ref[...] = dhv.astype(dhv_ref.dtype)
        first = pl.program_id(1) == 0
        for r, val in ((sg_ref, stg), (sv_ref, stv)):
            @pl.when(first)
            def _():
                r[...] = val

            @pl.when(jnp.logical_not(first))
            def _():
                r[...] += val

    tok = lambda off: pl.BlockSpec((seq, LANES), lambda j, b, off=off: (b, off + j))
    wsp = lambda off: pl.BlockSpec((CONV_W, LANES), lambda j, b, off=off: (0, off + j))
    bsp = lambda off: pl.BlockSpec((1, LANES), lambda j, b, off=off: (0, off + j))
    ssp = pl.BlockSpec((SUBLANES, LANES), lambda j, b: (0, j))
    dhg, dhv, stg, stv = pl.pallas_call(
        body, name="conv_bwd", grid=(ncol, nseq),
        in_specs=[tok(0), tok(0), tok(ncol), wsp(0), wsp(ncol), bsp(0), bsp(ncol)],
        out_specs=[tok(0), tok(0), ssp, ssp],
        out_shape=[jax.ShapeDtypeStruct((nseq * seq, ff2 // 2), BF16)] * 2
        + [jax.ShapeDtypeStruct((SUBLANES, ff2 // 2), F32)] * 2,
        compiler_params=_params("arbitrary", "arbitrary"),
    )(da, h, h, wconv, wconv, bconv, bconv)
    return (dhg, dhv), jnp.concatenate([stg, stv], axis=1)


def _rms_fwd(xv, g):
    r = lax.rsqrt(jnp.mean(xv * xv, axis=1, keepdims=True) + EPS)
    return (xv * r * g,)


def _rms_bwd(xv, g, dy, res):
    r = lax.rsqrt(jnp.mean(xv * xv, axis=1, keepdims=True) + EPS)
    xh = xv * r
    dxh = dy * g
    dx = r * (dxh - xh * jnp.mean(dxh * xh, axis=1, keepdims=True)) + res
    return dx, jnp.sum(dy * xh, axis=0, keepdims=True)


def _loss_head(x2, tgt, g):
    d = x2.shape[1]
    r = lax.rsqrt(jnp.mean(x2 * x2, axis=1, keepdims=True) + EPS)
    xh = x2 * r
    err = xh * g - tgt
    dy = err * (1.0 / d)
    dxh = dy * g
    dx = r * (dxh - xh * jnp.mean(dxh * xh, axis=1, keepdims=True))
    loss = 0.5 * jnp.sum(jnp.mean(err * err, axis=1, keepdims=True), axis=0, keepdims=True)
    return dx, jnp.sum(dy * xh, axis=0, keepdims=True), jnp.broadcast_to(loss, (1, LANES))


LATE_A = ("w_down", "w_out")
LATE_B = ("w_up", "w_pa", "w_pb")
LATE = LATE_A + LATE_B
EARLY_GRADS = ("w_down", "w_up", "w_out", "w_pa", "w_pb", "w_glu")
ROW_SHARDED = ("w_glu", "w_out", "w_down")


def _local_step(x, tgt, u, z_own, p, first, near, late, *, nseq, seq):
    p = dict(p)
    chip = 2 * lax.axis_index("x") + lax.axis_index("y")
    t, d = x.shape
    s5w = p["s5_d"].shape[1]
    hgw = p["gain"].shape[1]
    heads = hgw // HEAD
    qoff = s5w // LANES
    gblk = (s5w + 4 * hgw) // GATE_BLOCK
    ngb = d // GATE_BLOCK
    tm = _row_tile(t, ROW_TILE)
    row = lambda a, w=None, base=0: (a, a.shape[1] if w is None else w, base, "row")
    vec = lambda a, w=None, base=0: (a, a.shape[1] if w is None else w, base, "vec")
    rw = functools.partial(_rowwise, rows=t, tm=tm)

    z, far = _in_proj_rest("in_proj", u, near[0], z_own, _gather_full_rider(first, relations=(2,)), flips=(2, 1))
    z, _ = _in_proj_rest("in_proj_far", u, far[0], z, None, flips=(3,))
    diag = jnp.bitwise_xor(chip, 3)
    w_in_all, w_glu_all, conv_all = [
        lax.dynamic_update_index_in_dim(full, lax.dynamic_index_in_dim(part, diag, 0, keepdims=False), diag, 0)
        for full, part in zip(near, far)]
    p["w_in"] = w_in_all
    p["w_glu"] = w_glu_all.reshape(-1, w_glu_all.shape[-1])
    p["w_conv"] = conv_all[:, :CONV_W].transpose(1, 0, 2).reshape(CONV_W, -1)

    lam_re, lam_im, bb_re, bb_im = _s5_discretize(p["s5_a_re"], p["s5_a_im"], p["s5_log_dt"], p["s5_b_re"], p["s5_b_im"])
    bre3 = _s5_in_blocks(bb_re).astype(BF16)
    bim3 = _s5_in_blocks(bb_im).astype(BF16)
    cre3 = _s5_out_blocks(p["s5_c_re"]).astype(BF16)
    cim3 = _s5_out_blocks(p["s5_c_im"]).astype(BF16)
    coef_f = _s5_scan_tables(lam_re.reshape(-1), lam_im.reshape(-1), False)
    coef_r = _s5_scan_tables(lam_re.reshape(-1), lam_im.reshape(-1), True)
    (o, yb, states, scores), landed_b = _hg_fwd(z, p["lbrow"], p["gain"], nseq=nseq, seq=seq, heads=heads, qoff=qoff,
                                                rider=_gather_ici_rider([late[n] for n in LATE_B]))
    def place_own(names, gathered):
        for n, g in zip(names, gathered):
            full = lax.dynamic_update_index_in_dim(g, late[n], chip, 0)
            p[n] = full.reshape(-1, full.shape[-1]) if n in ROW_SHARDED else full

    nb_late = len(LATE_B)
    (y5, xre, xim), got = _s5_fwd(z, bre3, bim3, cre3, cim3, coef_f, p["s5_d"], nseq=nseq, seq=seq,
                                  rider=_merge_riders(_gather_pass_rider(list(landed_b)),
                                                      _gather_ici_rider([late[n] for n in LATE_A])))
    place_own(LATE_B, got[:nb_late])
    ya0, gl, ya = _glu_fwd(y5, p["w_glu"], p["b_glu"])

    joined = lambda w3: w3.transpose(1, 0, 2).reshape(w3.shape[1], -1)
    split = lambda g: g.reshape(g.shape[0], N_CHIPS, -1).transpose(1, 0, 2)
    wpa, wpb = joined(p["w_pa"]), joined(p["w_pb"])
    pa, got_a = _mm_fwd_rows("proj_a", ya, wpa, out_dtype=BF16, rider=_gather_pass_rider(list(got[nb_late:])))
    place_own(LATE_A, got_a)
    pb = _mm_fwd_rows("proj_b", yb, wpb, out_dtype=BF16)
    gb = GATE_BLOCK
    (m,) = rw("merge", lambda ga, gbv, a, b: (_sigmoid(ga) * a + _sigmoid(gbv) * b,),
              [row(z, gb, gblk), row(z, gb, gblk + ngb), row(pa, gb), row(pb, gb)], [(d, gb, BF16)], ncol=ngb)
    x1 = _mm_fwd_rows("out_proj", m, p["w_out"], res=x)

    (u2,) = rw("rms_ffn", _rms_fwd, [row(x1), vec(p["g_ffn"])], [(d, d, BF16)])
    h = _mm_fwd_cols("up_proj", u2, p["w_up"], out_dtype=BF16)
    a = _conv_fwd(h, p["w_conv"], p["b_conv"], nseq=nseq, seq=seq)
    dx2, dg_final, lossv = _mm_fwd_rows("down_proj", a, p["w_down"], res=x1,
                                        epilogue=(_loss_head, [tgt, p["g_final"]], [d, LANES]))

    norm_bwd = lambda dyv, xv, g, resv: _rms_bwd(xv, g, dyv, resv)
    da = _mm_bwd_rows("down_bwd", dx2, p["w_down"], out_dtype=BF16)
    g_wdown = _mm_wgrad_rows("down_wgrad", a, dx2)
    dh, cstats = _conv_bwd(da, h, p["w_conv"], p["b_conv"], nseq=nseq, seq=seq)
    dx1, dg_ffn = _mm_bwd_cols("up_bwd", dh, p["w_up"], epilogue=(norm_bwd, [x1, p["g_ffn"], dx2], [d]))
    g_wup = _mm_wgrad_cols("up_wgrad", u2, dh)

    dm = _mm_bwd_rows("out_bwd", dx1, p["w_out"], out_dtype=BF16)
    g_wout = _mm_wgrad_rows("out_wgrad", m, dx1)

    def merge_bwd(ga, gbv, av, bv, dmv):
        sa = _sigmoid(ga)
        sb = _sigmoid(gbv)
        return dmv * sa, dmv * sb, dmv * av * sa * (1.0 - sa), dmv * bv * sb * (1.0 - sb)

    dpa, dpb, dzga, dzgb = rw("merge_bwd", merge_bwd,
                              [row(z, gb, gblk), row(z, gb, gblk + ngb), row(pa, gb), row(pb, gb), row(dm, gb)],
                              [(d, gb, BF16)] * 4, ncol=ngb)
    dya = _mm_bwd_rows("proj_a_bwd", dpa, wpa)
    g_wpa = split(_mm_wgrad_rows("proj_a_wgrad", ya, dpa))
    dyb = _mm_bwd_rows("proj_b_bwd", dpb, wpb)
    g_wpb = split(_mm_wgrad_rows("proj_b_wgrad", yb, dpb))

    dgl, dy5, db_glu = _glu_bwd(y5, gl, p["b_glu"], dya, p["w_glu"])
    g_wglu = _mm_wgrad_rows("glu_wgrad", ya0, dgl)
    partial = dict(w_down=g_wdown, w_up=g_wup, w_out=g_wout, w_pa=g_wpa, w_pb=g_wpb, w_glu=g_wglu)
    parts = [_grad_parts(partial[n]) for n in EARLY_GRADS]
    (dza, dbre3, dbim3, dcre3, dcim3, dlam, dd), sib = _s5_bwd(
        dy5, z, xre, xim, bre3, bim3, cre3, cim3, coef_r, p["s5_d"], nseq=nseq, seq=seq, rider=_swap_halves_rider(parts))
    pair = _pair_sums(EARLY_GRADS, parts, sib)
    (dzq, dzf, dzi, dzg, dlb, dgain), others = _hg_bwd(
        dyb, z, o, states, scores, p["lbrow"], p["gain"], nseq=nseq, seq=seq, heads=heads, qoff=qoff,
        rider=_scatter_rider(pair))
    halves = _chip_sums(EARLY_GRADS, pair, others)

    dz = jnp.concatenate([dza, dzq, dzf, dzi, dzg, dzga, dzgb], axis=1)
    gshape = lam_re.shape
    small = {
        "loss": lossv, "g_ffn": dg_ffn, "g_final": dg_final, "b_glu": db_glu, "gain": dgain,
        "lbrow": dlb, "s5_d": dd, "w_conv": cstats[0:CONV_W], "b_conv": cstats[CONV_W:CONV_W + 1],
        "lam_re": dlam[:, 0, :].reshape(gshape), "lam_im": dlam[:, 1, :].reshape(gshape),
        "bb_re": _s5_in_blocks_diag(dbre3), "bb_im": _s5_in_blocks_diag(dbim3),
        "s5_c_re": _s5_out_blocks_diag(dcre3), "s5_c_im": _s5_out_blocks_diag(dcim3),
    }
    small_vec = _pack([small[n] for n in SMALL_PARTS], F32)
    g_win, (small_all, *sibs) = _mm_wgrad_cols(
        "in_wgrad", u, dz, rider=_merge_riders(_gather_all_rider(small_vec), _swap_sums_rider(halves)))
    big = dict(zip(EARLY_GRADS, zip(halves, sibs)))
    small_sum = _sum_over_devices("small_grad_sum", small_vec, small_all)
    sm = dict(zip(SMALL_PARTS, _unpack(small_sum, [small[n].shape for n in SMALL_PARTS])))
    last = [_grad_parts(g_win)]
    pair = _pair_sums(("w_in",), last, _run_rider("grad_swap_halves", _swap_halves_rider(last)))
    (dx, dg_mix), others = _mm_bwd_cols("in_bwd", dz, p["w_in"], epilogue=(norm_bwd, [x, p["g_mix"], dx1], [d]),
                                        rider=_scatter_rider(pair))
    (half,) = _chip_sums(("w_in",), pair, others)
    mid = half.shape[0] // 2
    mix_vec = dg_mix.reshape(SUBLANES, -1)
    top, bottom, mix_all = _run_rider("grad_swap_sums", _merge_riders(_swap_sums_rider([half[:mid], half[mid:]]),
                                                                      _gather_all_rider(mix_vec)))
    big["w_in"] = (half, jnp.concatenate([top, bottom], axis=0))
    sm["g_mix"] = _sum_over_devices("g_mix_sum", mix_vec, mix_all).reshape(dg_mix.shape)
    return dx, big, sm


ANY = pl.BlockSpec(memory_space=pl.ANY)


def _place():
    x, y, c = lax.axis_index("x"), lax.axis_index("y"), lax.axis_index("c")
    chips = [(1 - x, y), (x, 1 - y), (1 - x, 1 - y)]
    return x, y, c, chips


def _remote(src, dst, send_sems, recv_sems, k, to):
    return pltpu.make_async_remote_copy(src_ref=src, dst_ref=dst, send_sem=send_sems.at[k], recv_sem=recv_sems.at[k],
                                        device_id=to, device_id_type=MESH)


def _half(rows, which):
    return pl.ds(pl.multiple_of(which * (rows // 2), SUBLANES), rows // 2)


class _SemView:
    def __init__(self, base, offset):
        self.base, self.offset = base, offset

    @property
    def at(self):
        return self

    def __getitem__(self, k):
        return self.base.at[self.offset + k]


def _merge_riders(first, second):
    na, no, ns = len(first.arrays), len(first.out_shapes), first.nsem

    def split(fn_a, fn_b):
        def run(ins, outs, send_sems, recv_sems):
            fn_a(ins[:na], outs[:no], send_sems, recv_sems)
            fn_b(ins[na:], outs[no:], _SemView(send_sems, ns), _SemView(recv_sems, ns))
        return run

    aliases = dict(first.aliases)
    aliases.update({na + i: no + o for i, o in second.aliases.items()})
    return _Rider(first.arrays + second.arrays, first.out_shapes + second.out_shapes, ns + second.nsem,
                  split(first.start, second.start), split(first.finish, second.finish), aliases)


PREPARE_TILE = 512


def _prepare(x, gain, w_own, arrays, rider):
    t, d = x.shape
    ns = w_own.shape[1]
    n = len(arrays)
    tm = _row_tile(t, PREPARE_TILE)

    def body(place, x_ref, g_ref, w_ref, *refs):
        (u,) = _rms_fwd(x_ref[...], g_ref[...])
        u = u.astype(BF16)
        refs[n][...] = u
        refs[n + 1][...] = _dot(u, w_ref[...])

        @pl.when(pl.program_id(0) == 0)
        def _():
            for i in range(n):
                refs[n + 2 + i][...] = refs[i][...].astype(BF16)

    vm = pl.BlockSpec(memory_space=pltpu.VMEM)
    tok = pl.BlockSpec((tm, d), lambda i, place: (i, 0))
    outs, gathered = _hosted_scalar_call(
        "prepare", body, grid=(t // tm,),
        in_specs=[tok, pl.BlockSpec((1, d), lambda i, place: (0, 0)), vm] + [vm] * n,
        out_specs=[tok, pl.BlockSpec((tm, ns), lambda i, place: (i, place[1]))] + [vm] * n,
        out_shape=[jax.ShapeDtypeStruct((t, d), BF16), jax.ShapeDtypeStruct((t, N_CHIPS * ns), F32)]
        + [jax.ShapeDtypeStruct(a.shape, BF16) for a in arrays],
        operands=[x, gain, w_own] + list(arrays), rider=rider)
    return outs[0], outs[1], outs[2:], gathered


def _in_proj_rest(name, u, w3, z, rider, flips):
    t, k = u.shape
    ns = w3.shape[2]
    tm = _mm_tile(t, 2 * k * _size(u) + 2 * ns * 4, 2 * k * ns * _size(w3))

    def other(j, place):
        flip = flips[0] if len(flips) == 1 else jnp.where(j == 0, flips[0], flips[1])
        return jnp.bitwise_xor(place[1], flip)

    def body(place, u_ref, w_ref, z_ref, o_ref):
        o_ref[...] = _dot(u_ref[...], w_ref[...])

    assert len(flips) <= 2
    outs, ridden = _hosted_scalar_call(
        name, body, grid=(len(flips), t // tm),
        in_specs=[pl.BlockSpec((tm, k), lambda j, i, place: (i, 0)),
                  pl.BlockSpec((None, k, ns), lambda j, i, place: (other(j, place), 0, 0)), ANY],
        out_specs=[pl.BlockSpec((tm, ns), lambda j, i, place: (i, other(j, place)))],
        out_shape=[jax.ShapeDtypeStruct(z.shape, z.dtype)], operands=[u, w3, z], rider=rider, aliases={2: 0})
    return outs[0], ridden


def _symmetric_rider(arrays, out_shapes, copies_of, nsem):
    def start(ins, outs, send_sems, recv_sems):
        for cp in copies_of(ins, outs, send_sems, recv_sems):
            cp.start()

    def finish(ins, outs, send_sems, recv_sems):
        for cp in copies_of(ins, outs, send_sems, recv_sems):
            cp.wait()

    return _Rider(arrays, out_shapes, nsem, start, finish)


def _swap_halves_rider(parts):
    def copies_of(ins, outs, send_sems, recv_sems):
        x, y, c, _ = _place()
        return [_remote(ins[a].at[:, _half(g.shape[1], 1 - c), :], outs[a], send_sems, recv_sems, a, (x, y, 1 - c))
                for a, g in enumerate(parts)]

    shapes = [jax.ShapeDtypeStruct((g.shape[0], g.shape[1] // 2, g.shape[2]), g.dtype) for g in parts]
    return _symmetric_rider(parts, shapes, copies_of, len(parts))


def _scatter_rider(parts):
    def copies_of(ins, outs, send_sems, recv_sems):
        x, y, c, chips = _place()
        return [_remote(ins[a].at[2 * cx + cy], outs[a].at[j], send_sems, recv_sems, 3 * a + j, (cx, cy, c))
                for a in range(len(parts)) for j, (cx, cy) in enumerate(chips)]

    shapes = [jax.ShapeDtypeStruct((N_CHIPS - 1,) + h.shape[1:], h.dtype) for h in parts]
    return _symmetric_rider(parts, shapes, copies_of, 3 * len(parts))


def _swap_sums_rider(parts):
    def copies_of(ins, outs, send_sems, recv_sems):
        x, y, c, _ = _place()
        return [_remote(ins[a], outs[a], send_sems, recv_sems, a, (x, y, 1 - c)) for a in range(len(parts))]

    shapes = [jax.ShapeDtypeStruct(g.shape, g.dtype) for g in parts]
    return _symmetric_rider(parts, shapes, copies_of, len(parts))


def _gather_ici_rider(shards):
    def sends(ins, outs, send_sems, recv_sems):
        x, y, c, chips = _place()
        return [_remote(ins[a].at[_half(s.shape[0], c)], outs[a].at[2 * x + y, _half(s.shape[0], c)], send_sems,
                        recv_sems, 3 * a + j, (cx, cy, c)) for a, s in enumerate(shards) for j, (cx, cy) in enumerate(chips)]

    def start(ins, outs, send_sems, recv_sems):
        for cp in sends(ins, outs, send_sems, recv_sems):
            cp.start()

    def finish(ins, outs, send_sems, recv_sems):
        x, y, c, chips = _place()
        for a, s in enumerate(shards):
            for j, (cx, cy) in enumerate(chips):
                landed = outs[a].at[2 * cx + cy, _half(s.shape[0], c)]
                _remote(landed, landed, send_sems, recv_sems, 3 * a + j, (x, y, c)).wait_recv()
        for cp in sends(ins, outs, send_sems, recv_sems):
            cp.wait_send()

    shapes = [jax.ShapeDtypeStruct((N_CHIPS,) + s.shape, s.dtype) for s in shards]
    return _Rider(shards, shapes, 3 * len(shards), start, finish)


def _gather_full_rider(shards, relations=(0, 1, 2)):
    n = len(shards)

    def peers():
        x, y, c, chips = _place()
        return x, y, c, [(j, chips[j]) for j in relations]

    def sends(ins, outs, send_sems, recv_sems):
        x, y, c, chips = peers()
        return [_remote(ins[a].at[_half(s.shape[0], c)], outs[a].at[2 * x + y, _half(s.shape[0], c)], send_sems,
                        recv_sems, 6 * a + j, (cx, cy, c)) for a, s in enumerate(shards) for j, (cx, cy) in chips]

    def start(ins, outs, send_sems, recv_sems):
        for cp in sends(ins, outs, send_sems, recv_sems):
            cp.start()

    def finish(ins, outs, send_sems, recv_sems):
        x, y, c, chips = peers()
        passed = []
        for a, s in enumerate(shards):
            for j, (cx, cy) in chips:
                landed = outs[a].at[2 * cx + cy, _half(s.shape[0], c)]
                _remote(landed, landed, send_sems, recv_sems, 6 * a + j, (x, y, c)).wait_recv()
                passed.append(_remote(landed, landed, send_sems, recv_sems, 6 * a + 3 + j, (x, y, 1 - c)))
                passed[-1].start()
        for a, s in enumerate(shards):
            for j, (cx, cy) in chips:
                other = outs[a].at[2 * cx + cy, _half(s.shape[0], 1 - c)]
                _remote(other, other, send_sems, recv_sems, 6 * a + 3 + j, (x, y, c)).wait_recv()
        for cp in sends(ins, outs, send_sems, recv_sems) + passed:
            cp.wait_send()

    shapes = [jax.ShapeDtypeStruct((N_CHIPS,) + s.shape, s.dtype) for s in shards]
    return _Rider(shards, shapes, 6 * n, start, finish)


def _gather_pass_rider(landed):
    def sends(ins, outs, send_sems, recv_sems):
        x, y, c, chips = _place()
        return [_remote(ins[a].at[2 * cx + cy, _half(g.shape[1], c)], outs[a].at[2 * cx + cy, _half(g.shape[1], c)],
                        send_sems, recv_sems, 3 * a + j, (x, y, 1 - c))
                for a, g in enumerate(landed) for j, (cx, cy) in enumerate(chips)]

    def start(ins, outs, send_sems, recv_sems):
        for cp in sends(ins, outs, send_sems, recv_sems):
            cp.start()

    def finish(ins, outs, send_sems, recv_sems):
        x, y, c, chips = _place()
        for a, g in enumerate(landed):
            for j, (cx, cy) in enumerate(chips):
                other = outs[a].at[2 * cx + cy, _half(g.shape[1], 1 - c)]
                _remote(other, other, send_sems, recv_sems, 3 * a + j, (x, y, c)).wait_recv()
        for cp in sends(ins, outs, send_sems, recv_sems):
            cp.wait_send()

    shapes = [jax.ShapeDtypeStruct(g.shape, g.dtype) for g in landed]
    return _Rider(landed, shapes, 3 * len(landed), start, finish, aliases={a: a for a in range(len(landed))})


def _grad_parts(g):
    return g.reshape((N_CHIPS, -1, g.shape[-1]))


def _place_scalars():
    return jnp.stack([lax.axis_index("c"), 2 * lax.axis_index("x") + lax.axis_index("y")]).astype(jnp.int32)


def _scalar_call(body, name, grid, in_specs, out_specs, out_shape, operands):
    spec = pltpu.PrefetchScalarGridSpec(num_scalar_prefetch=1, grid=grid, in_specs=in_specs, out_specs=out_specs)
    return pl.pallas_call(body, name=name, grid_spec=spec, out_shape=out_shape,
                          compiler_params=_params(*(["arbitrary"] * len(grid))))(_place_scalars(), *operands)


def _pair_sums(names, parts, sib):
    out = []
    for n, g, s in zip(names, parts, sib):
        rh, cols = s.shape[1], s.shape[2]
        tm = _row_tile(rh, ROW_TILE)
        nblk = rh // tm

        def body(place, g_ref, s_ref, o_ref):
            o_ref[...] = (g_ref[...].astype(F32) + s_ref[...].astype(F32)).astype(o_ref.dtype)

        blk = pl.BlockSpec((None, tm, cols), lambda j, i, place: (j, i, 0))
        own = pl.BlockSpec((None, tm, cols), lambda j, i, place, nblk=nblk: (j, place[0] * nblk + i, 0))
        out.append(_scalar_call(body, "grad_pair_sum_" + n, (N_CHIPS, nblk), [own, blk], blk,
                                jax.ShapeDtypeStruct(s.shape, BF16), (g, s)))
    return out


def _chip_sums(names, pair, others):
    out = []
    for n, h, o in zip(names, pair, others):
        rh, cols = h.shape[1], h.shape[2]
        tm = _row_tile(rh, ROW_TILE)

        def body(place, h_ref, a_ref, b_ref, c_ref, o_ref):
            o_ref[...] = (h_ref[...].astype(F32) + a_ref[...].astype(F32)) + b_ref[...].astype(F32) + c_ref[...].astype(F32)

        mine = pl.BlockSpec((None, tm, cols), lambda i, place: (place[1], i, 0))
        other = lambda k: pl.BlockSpec((None, tm, cols), lambda i, place, k=k: (k, i, 0))
        out.append(_scalar_call(body, "grad_chip_sum_" + n, (rh // tm,), [mine, other(0), other(1), other(2)],
                                pl.BlockSpec((tm, cols), lambda i, place: (i, 0)), jax.ShapeDtypeStruct((rh, cols), F32),
                                (h, o, o, o)))
    return out


def _adamw_halves(name, w, m, v, own, sib):
    rh, cols = own.shape
    tm = _row_tile(rh, ADAMW_TILE)
    nblk = rh // tm

    def body(place, w_ref, m_ref, v_ref, own_ref, sib_ref, g_ref, d_ref, m2_ref, v2_ref):
        mine = pl.program_id(0) // nblk == place[0]

        def run(gv):
            g_ref[...] = gv
            d_ref[...], m2_ref[...], v2_ref[...] = _adamw_math(w_ref[...], gv, m_ref[...], v_ref[...])

        @pl.when(mine)
        def _():
            run(own_ref[...])

        @pl.when(jnp.logical_not(mine))
        def _():
            run(sib_ref[...])

    full = pl.BlockSpec((tm, cols), lambda i, place: (i, 0))
    own_spec = pl.BlockSpec((tm, cols), lambda i, place: (jnp.where(i // nblk == place[0], i % nblk, 0), 0))
    sib_spec = pl.BlockSpec((tm, cols), lambda i, place: (jnp.where(i // nblk == place[0], 0, i % nblk), 0))
    return _scalar_call(body, name, (2 * nblk,), [full, full, full, own_spec, sib_spec], [full] * 4,
                        [jax.ShapeDtypeStruct((2 * rh, cols), F32)] * 4, (w, m, v, own, sib))


def _gather_all_rider(v):
    m_per = v.shape[0]

    def rows(ref, px, py, pc):
        return ref.at[pl.ds(pl.multiple_of((4 * px + 2 * py + pc) * m_per, 8), m_per)]

    def first(ins, outs, send_sems, recv_sems):
        x, y, c, chips = _place()
        mine = rows(outs[0], x, y, c)
        return [_remote(ins[0], mine, send_sems, recv_sems, 0, (x, y, 1 - c))] + [
            _remote(ins[0], mine, send_sems, recv_sems, 1 + j, (cx, cy, c)) for j, (cx, cy) in enumerate(chips)]

    def start(ins, outs, send_sems, recv_sems):
        for cp in first(ins, outs, send_sems, recv_sems):
            cp.start()

    def finish(ins, outs, send_sems, recv_sems):
        x, y, c, chips = _place()
        passed = []
        for j, (cx, cy) in enumerate(chips):
            blk = rows(outs[0], cx, cy, c)
            _remote(blk, blk, send_sems, recv_sems, 1 + j, (x, y, c)).wait_recv()
            passed.append(_remote(blk, blk, send_sems, recv_sems, 4 + j, (x, y, 1 - c)))
            passed[j].start()
        sib = rows(outs[0], x, y, 1 - c)
        _remote(sib, sib, send_sems, recv_sems, 0, (x, y, c)).wait_recv()
        for j, (cx, cy) in enumerate(chips):
            blk = rows(outs[0], cx, cy, 1 - c)
            _remote(blk, blk, send_sems, recv_sems, 4 + j, (x, y, c)).wait_recv()
        for cp in first(ins, outs, send_sems, recv_sems) + passed:
            cp.wait_send()

    return _Rider([v], [jax.ShapeDtypeStruct((N_DEV * m_per,) + v.shape[1:], v.dtype)], 7, start, finish)


def _sum_over_devices(name, v, gathered):
    m_per = v.shape[0]
    dev = 4 * lax.axis_index("x") + 2 * lax.axis_index("y") + lax.axis_index("c")
    full = lax.dynamic_update_slice_in_dim(gathered, v, dev * m_per, axis=0)
    return _sum_blocks(name, [full[i * m_per:(i + 1) * m_per] for i in range(N_DEV)], F32)


def _sum_blocks(name, parts, out_dtype):
    rows, cols = parts[0].shape
    tm = _row_tile(rows, ROW_TILE)

    def body(*refs):
        acc = refs[0][...].astype(F32)
        for r in refs[1:-1]:
            acc = acc + r[...].astype(F32)
        refs[-1][...] = acc.astype(refs[-1].dtype)

    spec = pl.BlockSpec((tm, cols), lambda i: (i, 0))
    return pl.pallas_call(
        body, name=name, grid=(rows // tm,), in_specs=[spec] * len(parts), out_specs=spec,
        out_shape=jax.ShapeDtypeStruct((rows, cols), out_dtype), compiler_params=_params("arbitrary"),
    )(*parts)


def _adamw_math(wv, gv, mv, vv):
    m2 = ADAM_B1 * mv + (1.0 - ADAM_B1) * gv
    v2 = ADAM_B2 * vv + (1.0 - ADAM_B2) * (gv * gv)
    delta = -ADAM_LR * ((m2 / (1.0 - ADAM_B1 ** ADAM_STEP)) / (jnp.sqrt(v2 / (1.0 - ADAM_B2 ** ADAM_STEP)) + ADAM_EPS)
                        + ADAM_WD * wv)
    return delta, m2, v2


def _adamw_small(ws, gs, ms, vs):
    n = len(ws)

    def body(*refs):
        for i in range(n):
            res = _adamw_math(refs[i][...], refs[n + i][...], refs[2 * n + i][...], refs[3 * n + i][...])
            for k in range(3):
                refs[(4 + k) * n + i][...] = res[k]

    vm = pl.BlockSpec(memory_space=pltpu.VMEM)
    outs = pl.pallas_call(
        body, name="adamw_small", in_specs=[vm] * (4 * n), out_specs=[vm] * (3 * n),
        out_shape=[jax.ShapeDtypeStruct(a.shape, F32) for a in ws] * 3,
        compiler_params=pltpu.CompilerParams(vmem_limit_bytes=VMEM_LIMIT_BYTES),
    )(*ws, *gs, *ms, *vs)
    return outs[:n], outs[n:2 * n], outs[2 * n:]


PACK_ROWS = 256


def _pack(flat_parts, dtype, lead=()):
    parts = [a.astype(dtype).reshape(lead + (-1,)) for a in flat_parts]
    n = sum(a.shape[-1] for a in parts)
    chunk = PACK_ROWS * LANES
    total = -(-n // chunk) * chunk
    if total > n:
        parts.append(jnp.zeros(lead + (total - n,), dtype))
    return jnp.concatenate(parts, axis=-1).reshape(lead + (total // LANES, LANES))


def _unpack(buf, shapes, lead=()):
    flat = buf.reshape(lead + (-1,))
    out, off = [], 0
    for shp in shapes:
        n = math.prod(shp)
        out.append(lax.slice_in_dim(flat, off, off + n, axis=len(lead)).reshape(lead + tuple(shp)))
        off += n
    return out


BIG = ("w_in", "w_glu", "w_pa", "w_pb", "w_out", "w_up", "w_down")
WEIGHTS = ("g_mix", "w_in", "s5_a_re", "s5_a_im", "s5_log_dt", "s5_b_re", "s5_b_im", "s5_c_re", "s5_c_im", "s5_d",
           "w_glu", "b_glu", "hg_lb_logits", "hg_norm_gain", "w_pa", "w_pb", "w_out", "g_ffn", "w_up", "w_conv",
           "b_conv", "w_down", "g_final")
SMALL = tuple(n for n in WEIGHTS if n not in BIG)
SMALL_PARTS = ("loss", "g_ffn", "g_final", "b_glu", "gain", "lbrow", "s5_d", "w_conv", "b_conv", "lam_re", "lam_im",
               "bb_re", "bb_im", "s5_c_re", "s5_c_im")


def _lower_bound(logits):
    return jnp.cumsum(jax.nn.softmax(logits, axis=0), axis=0)[0:1]


def kernel(x, g_mix, w_in, s5_a_re, s5_a_im, s5_log_dt, s5_b_re, s5_b_im, s5_c_re, s5_c_im, s5_d, w_glu, b_glu, hg_lb_logits, hg_norm_gain, w_pa, w_pb, w_out, g_ffn, w_up, w_conv, b_conv, w_down, g_final, loss_target, m_g_mix, m_w_in, m_s5_a_re, m_s5_a_im, m_s5_log_dt, m_s5_b_re, m_s5_b_im, m_s5_c_re, m_s5_c_im, m_s5_d, m_w_glu, m_b_glu, m_hg_lb_logits, m_hg_norm_gain, m_w_pa, m_w_pb, m_w_out, m_g_ffn, m_w_up, m_w_conv, m_b_conv, m_w_down, m_g_final, v_g_mix, v_w_in, v_s5_a_re, v_s5_a_im, v_s5_log_dt, v_s5_b_re, v_s5_b_im, v_s5_c_re, v_s5_c_im, v_s5_d, v_w_glu, v_b_glu, v_hg_lb_logits, v_hg_norm_gain, v_w_pa, v_w_pb, v_w_out, v_g_ffn, v_w_up, v_w_conv, v_b_conv, v_w_down, v_g_final):
    args = dict(locals())
    w = {n: args[n] for n in WEIGHTS}
    mom = {n: args["m_" + n] for n in WEIGHTS}
    var = {n: args["v_" + n] for n in WEIGHTS}
    nseq, seq, d = x.shape
    xi, yi = lax.axis_index("x"), lax.axis_index("y")
    chip = 2 * xi + yi

    shard = {n: w[n][0] for n in BIG}
    first = [shard["w_in"].astype(BF16), shard["w_glu"].astype(BF16),
             jnp.pad(w_conv[0], ((0, 2 * SUBLANES - CONV_W), (0, 0)))]
    x2 = x.reshape(nseq * seq, d)
    u, z_own, late16, got = _prepare(x2, g_mix, first[0], [shard[n] for n in LATE],
                                     _gather_full_rider(first, relations=(0, 1)))
    near = [lax.dynamic_update_index_in_dim(g, s, chip, 0) for g, s in zip(got, first)]
    p = dict(g_mix=g_mix, g_ffn=g_ffn, g_final=g_final.reshape(1, -1), b_glu=b_glu, gain=hg_norm_gain, s5_d=s5_d,
             b_conv=b_conv, lbrow=_lower_bound(hg_lb_logits),
             s5_a_re=s5_a_re[0], s5_a_im=s5_a_im[0], s5_log_dt=s5_log_dt[0], s5_b_re=s5_b_re[0], s5_b_im=s5_b_im[0],
             s5_c_re=s5_c_re[0], s5_c_im=s5_c_im[0])

    dx, halves, sm = _local_step(x2, loss_target.reshape(nseq * seq, d), u, z_own, p, first, near,
                                 dict(zip(LATE, late16)), nseq=nseq, seq=seq)
    loss = sm["loss"][0, 0]

    grads, delta, new_m, new_v = {}, {}, {}, {}
    for n in BIG:
        shp = shard[n].shape
        grads[n], delta[n], new_m[n], new_v[n] = _adamw_halves("adamw_" + n, shard[n], mom[n].reshape(shp),
                                                               var[n].reshape(shp), *halves[n])

    _, disc_vjp = jax.vjp(_s5_discretize, p["s5_a_re"], p["s5_a_im"], p["s5_log_dt"], p["s5_b_re"], p["s5_b_im"])
    da_re, da_im, dlog_dt, db_re, db_im = disc_vjp((sm["lam_re"], sm["lam_im"], sm["bb_re"], sm["bb_im"]))
    _, lb_vjp = jax.vjp(_lower_bound, hg_lb_logits)
    (dlogits,) = lb_vjp(sm["lbrow"])
    fcols = w_conv.shape[-1]
    grads.update(
        g_mix=sm["g_mix"], g_ffn=sm["g_ffn"], g_final=sm["g_final"].reshape(-1), b_glu=sm["b_glu"],
        hg_norm_gain=sm["gain"], hg_lb_logits=dlogits, s5_d=sm["s5_d"], b_conv=sm["b_conv"],
        w_conv=lax.dynamic_slice_in_dim(sm["w_conv"], chip * fcols, fcols, axis=1),
        s5_a_re=da_re, s5_a_im=da_im, s5_log_dt=dlog_dt, s5_b_re=db_re, s5_b_im=db_im,
        s5_c_re=sm["s5_c_re"], s5_c_im=sm["s5_c_im"])
    grads = {n: grads[n].reshape(w[n].shape) for n in WEIGHTS}

    def natural(a):
        return a.reshape(1, -1) if a.ndim == 1 else (a[0] if a.ndim > 2 else a)

    outs = _adamw_small(*[[natural(src[n]) for n in SMALL] for src in (w, grads, mom, var)])
    for dst, group in zip((delta, new_m, new_v), outs):
        dst.update(zip(SMALL, group))
    res = [loss, dx.reshape(x.shape)]
    for group in (grads, delta, new_m, new_v):
        res += [group[n].reshape(w[n].shape) for n in WEIGHTS]
    return tuple(res)
```

```python
import functools
import math

import jax
import jax.numpy as jnp
from jax import lax
from jax.experimental import pallas as pl
from jax.experimental.pallas import tpu as pltpu

F32 = jnp.float32
BF16 = jnp.bfloat16
MESH = pl.DeviceIdType.MESH

EPS = 1e-6
S5_GROUP = 16
S5_STATE = 64
S5_BLOCK_GROUPS = 8
HEAD = 128
CHUNK = 64
CONV_W = 3
LANES = 128
SUBLANES = 8
GATE_BLOCK = 512
VMEM_LIMIT_BYTES = 56 * 1024 * 1024

ADAM_LR = 0.001
ADAM_B1 = 0.9
ADAM_B2 = 0.999
ADAM_EPS = 1e-08
ADAM_WD = 0.01
ADAM_STEP = 10

N_CHIPS = 4
N_DEV = 8


def _params(*sem):
    return pltpu.CompilerParams(dimension_semantics=sem, vmem_limit_bytes=VMEM_LIMIT_BYTES)


class _Rider:
    def __init__(self, arrays, out_shapes, nsem, start, finish, aliases=None):
        self.arrays, self.out_shapes, self.nsem = list(arrays), list(out_shapes), nsem
        self.start, self.finish, self.aliases = start, finish, dict(aliases or {})


def _hosted_call(name, body, *, grid, in_specs, out_specs, out_shape, operands, scratch_shapes=(), rider=None):
    in_specs, out_specs, out_shape, scratch_shapes = list(in_specs), list(out_specs), list(out_shape), list(scratch_shapes)
    cparams = _params(*(["arbitrary"] * len(grid)))
    if rider is None:
        return pl.pallas_call(body, name=name, grid=grid, in_specs=in_specs, out_specs=out_specs, out_shape=out_shape,
                              scratch_shapes=scratch_shapes, compiler_params=cparams)(*operands)
    n_in, n_out, n_sc = len(in_specs), len(out_specs), len(scratch_shapes)
    r_in, r_out = len(rider.arrays), len(rider.out_shapes)

    def hosted(*refs):
        ins, rins = refs[:n_in], refs[n_in:n_in + r_in]
        outs = refs[n_in + r_in:n_in + r_in + n_out]
        routs = refs[n_in + r_in + n_out:n_in + r_in + n_out + r_out]
        rest = refs[n_in + r_in + n_out + r_out:]
        send_sems, recv_sems = rest[n_sc], rest[n_sc + 1]
        first = functools.reduce(jnp.logical_and, [pl.program_id(i) == 0 for i in range(len(grid))])
        last = functools.reduce(jnp.logical_and, [pl.program_id(i) == grid[i] - 1 for i in range(len(grid))])

        @pl.when(first)
        def _():
            rider.start(rins, routs, send_sems, recv_sems)

        body(*ins, *outs, *rest[:n_sc])

        @pl.when(last)
        def _():
            rider.finish(rins, routs, send_sems, recv_sems)

    res = pl.pallas_call(
        hosted, name=name, grid=grid, in_specs=in_specs + [ANY] * r_in, out_specs=out_specs + [ANY] * r_out,
        out_shape=out_shape + rider.out_shapes,
        scratch_shapes=scratch_shapes + [pltpu.SemaphoreType.DMA((rider.nsem,)), pltpu.SemaphoreType.DMA((rider.nsem,))],
        input_output_aliases={n_in + i: n_out + o for i, o in rider.aliases.items()}, compiler_params=cparams,
    )(*operands, *rider.arrays)
    return res[:n_out], res[n_out:]


def _hosted_scalar_call(name, body, *, grid, in_specs, out_specs, out_shape, operands, rider, aliases=None):
    in_specs, out_specs, out_shape = list(in_specs), list(out_specs), list(out_shape)
    n_in, n_out = len(in_specs), len(out_specs)
    if rider is None:
        spec = pltpu.PrefetchScalarGridSpec(num_scalar_prefetch=1, grid=grid, in_specs=in_specs, out_specs=out_specs)
        res = pl.pallas_call(body, name=name, grid_spec=spec, out_shape=out_shape,
                             input_output_aliases={1 + i: o for i, o in (aliases or {}).items()},
                             compiler_params=_params(*(["arbitrary"] * len(grid))))(_place_scalars(), *operands)
        return res, []
    r_in, r_out = len(rider.arrays), len(rider.out_shapes)

    def hosted(place, *refs):
        ins, rins = refs[:n_in], refs[n_in:n_in + r_in]
        outs = refs[n_in + r_in:n_in + r_in + n_out]
        routs = refs[n_in + r_in + n_out:n_in + r_in + n_out + r_out]
        send_sems, recv_sems = refs[-2], refs[-1]
        first = functools.reduce(jnp.logical_and, [pl.program_id(i) == 0 for i in range(len(grid))])
        last = functools.reduce(jnp.logical_and, [pl.program_id(i) == grid[i] - 1 for i in range(len(grid))])

        @pl.when(first)
        def _():
            rider.start(rins, routs, send_sems, recv_sems)

        body(place, *ins, *outs)

        @pl.when(last)
        def _():
            rider.finish(rins, routs, send_sems, recv_sems)

    spec = pltpu.PrefetchScalarGridSpec(
        num_scalar_prefetch=1, grid=grid, in_specs=in_specs + [ANY] * r_in, out_specs=out_specs + [ANY] * r_out,
        scratch_shapes=[pltpu.SemaphoreType.DMA((rider.nsem,)), pltpu.SemaphoreType.DMA((rider.nsem,))])
    alias = {1 + i: o for i, o in (aliases or {}).items()}
    alias.update({1 + n_in + i: n_out + o for i, o in rider.aliases.items()})
    res = pl.pallas_call(hosted, name=name, grid_spec=spec, out_shape=out_shape + rider.out_shapes,
                         input_output_aliases=alias, compiler_params=_params(*(["arbitrary"] * len(grid))),
                         )(_place_scalars(), *operands, *rider.arrays)
    return res[:n_out], res[n_out:]


def _run_rider(name, rider):
    r_in, r_out = len(rider.arrays), len(rider.out_shapes)

    def body(*refs):
        rins, routs, send_sems, recv_sems = refs[:r_in], refs[r_in:r_in + r_out], refs[-2], refs[-1]
        rider.start(rins, routs, send_sems, recv_sems)
        rider.finish(rins, routs, send_sems, recv_sems)

    return pl.pallas_call(
        body, name=name, in_specs=[ANY] * r_in, out_specs=[ANY] * r_out, out_shape=rider.out_shapes,
        scratch_shapes=[pltpu.SemaphoreType.DMA((rider.nsem,)), pltpu.SemaphoreType.DMA((rider.nsem,))],
        input_output_aliases=rider.aliases,
    )(*rider.arrays)


def _row_tile(rows, cap):
    if rows <= cap:
        return rows
    for t in range(cap - cap % 8, 7, -8):
        if rows % t == 0:
            return t
    raise ValueError(f"no row tile for {rows}")


def _dot(a, b):
    return jnp.dot(a.astype(BF16), b.astype(BF16), preferred_element_type=F32)


def _dot_nt(a, b):
    return lax.dot_general(a.astype(BF16), b.astype(BF16), (((1,), (1,)), ((), ())), preferred_element_type=F32)


def _dot_tn(a, b):
    return lax.dot_general(a.astype(BF16), b.astype(BF16), (((0,), (0,)), ((), ())), preferred_element_type=F32)


def _sigmoid(x):
    return 0.5 * jnp.tanh(0.5 * x) + 0.5


_GELU_C = math.sqrt(2.0 / math.pi)


def _gelu(x):
    return 0.5 * x * (1.0 + jnp.tanh(_GELU_C * (x + 0.044715 * x * x * x)))


def _gelu_grad(x):
    th = jnp.tanh(_GELU_C * (x + 0.044715 * x * x * x))
    return 0.5 * (1.0 + th) + 0.5 * x * (1.0 - th * th) * _GELU_C * (1.0 + 3.0 * 0.044715 * x * x)


def _rowwise(name, fn, ins, outs, accs=(), *, rows, tm, ncol=1, rider=None):
    n_in, n_out = len(ins), len(outs)

    def body(*refs):
        res = fn(*[r[...] for r in refs[:n_in]])
        for r, v in zip(refs[n_in:n_in + n_out], res[:n_out]):
            r[...] = v.astype(r.dtype)
        first = pl.program_id(1) == 0
        for r, v in zip(refs[n_in + n_out:], res[n_out:]):
            @pl.when(first)
            def _():
                r[...] = v

            @pl.when(jnp.logical_not(first))
            def _():
                r[...] += v

    in_specs = []
    for _, width, base, kind in ins:
        if kind == "row":
            in_specs.append(pl.BlockSpec((tm, width), lambda j, i, b=base: (i, b + j)))
        else:
            in_specs.append(pl.BlockSpec((1, width), lambda j, i, b=base: (0, b + j)))
    out_specs = [pl.BlockSpec((tm, width), lambda j, i: (i, j)) for _, width, _ in outs]
    out_specs += [pl.BlockSpec((1, width), lambda j, i: (0, j)) for _, width in accs]
    out_shape = [jax.ShapeDtypeStruct((rows, total), dt) for total, _, dt in outs]
    out_shape += [jax.ShapeDtypeStruct((1, total), F32) for total, _ in accs]
    return _hosted_call(name, body, grid=(ncol, rows // tm), in_specs=in_specs, out_specs=out_specs, out_shape=out_shape,
                        operands=[a for a, _, _, _ in ins], rider=rider)


def _mm(name, a, b, *, mode, grid, a_spec, b_spec, o_spec, out_shape, acc_shape, res=None, res_spec=None,
        pair_axis=None, rider=None, epilogue=None):
    nk = grid[2]
    dot = {"nn": _dot, "nt": _dot_nt, "tn": _dot_tn}[mode]
    a_list = list(a) if isinstance(a, tuple) else [a]
    b_list = list(b) if isinstance(b, tuple) else [b]
    na, nb = len(a_list), len(b_list)
    assert (pair_axis is None) == (na + nb == 2)
    direct = nk == 1 and pair_axis is None
    epi_fn, epi_ins, epi_sums = epilogue if epilogue is not None else (None, [], [])
    n_res = 0 if res is None else 1
    n_epi = len(epi_ins)

    def body(*refs):
        a_refs, b_refs = refs[:na], refs[na:na + nb]
        r_ref = None if res is None else refs[na + nb]
        e_refs = refs[na + nb + n_res:na + nb + n_res + n_epi]
        o_ref = refs[na + nb + n_res + n_epi]
        s_refs = refs[na + nb + n_res + n_epi + 1:na + nb + n_res + n_epi + 1 + len(epi_sums)]
        first_rows = pl.program_id(0) == 0

        def finish(v):
            if res is not None:
                v = v + r_ref[...]
            if epi_fn is None:
                o_ref[...] = v.astype(o_ref.dtype)
                return
            outs = epi_fn(v, *[r[...] for r in e_refs])
            o_ref[...] = outs[0].astype(o_ref.dtype)
            for s_ref, part in zip(s_refs, outs[1:]):
                @pl.when(first_rows)
                def _():
                    s_ref[...] = part

                @pl.when(jnp.logical_not(first_rows))
                def _():
                    s_ref[...] += part

        if direct:
            finish(dot(a_refs[0][...], b_refs[0][...]))
            return
        acc_ref = refs[-1]
        k = pl.program_id(2)

        @pl.when(k == 0)
        def _():
            acc_ref[...] = jnp.zeros_like(acc_ref)

        if pair_axis is None:
            acc_ref[...] += dot(a_refs[0][...], b_refs[0][...])
        else:
            lower = pl.program_id(pair_axis) < grid[pair_axis] // 2

            @pl.when(lower)
            def _():
                acc_ref[...] += dot(a_refs[0][...], b_refs[0][...])

            @pl.when(jnp.logical_not(lower))
            def _():
                acc_ref[...] += dot(a_refs[-1][...], b_refs[-1][...])

        @pl.when(k == nk - 1)
        def _():
            finish(acc_ref[...])

    operands = a_list + b_list + ([] if res is None else [res]) + [arr for arr, _ in epi_ins]
    in_specs = (list(a_spec) if na == 2 else [a_spec]) + (list(b_spec) if nb == 2 else [b_spec])
    in_specs += ([] if res is None else [res_spec]) + [spec for _, spec in epi_ins]
    out_specs = [o_spec] + [pl.BlockSpec((1, c), lambda *_: (0, 0)) for c in epi_sums]
    out_shapes = [out_shape] + [jax.ShapeDtypeStruct((1, c), F32) for c in epi_sums]
    got = _hosted_call(name, body, grid=grid, in_specs=in_specs, out_specs=out_specs, out_shape=out_shapes,
                       scratch_shapes=[] if direct else [pltpu.VMEM(acc_shape, F32)], operands=operands, rider=rider)
    mine, rider_outs = (got, None) if rider is None else got
    mine = mine[0] if epilogue is None else tuple(mine)
    return mine if rider is None else (mine, rider_outs)


MM_TILE_BUDGET_BYTES = 36 * 1024 * 1024
MM_TILE_CAP = 2048
ROW_TILE = 1024
GLU_TILE = 1024
ADAMW_TILE = 256


def _mm_tile(t, row_bytes, fixed_bytes):
    cap = max(16, min(MM_TILE_CAP, (MM_TILE_BUDGET_BYTES - fixed_bytes) // row_bytes))
    return _row_tile(t, cap - cap % 16)


def _size(a):
    return jnp.dtype(a.dtype).itemsize


def _mm_fwd_cols(name, a, w3, out_dtype=F32, rider=None):
    t, k = a.shape
    ns = w3.shape[2]
    tm = _mm_tile(t, 2 * k * _size(a) + 2 * ns * jnp.dtype(out_dtype).itemsize, 2 * k * ns * _size(w3))
    return _mm(name, a, w3, mode="nn", grid=(N_CHIPS, t // tm, 1),
               a_spec=pl.BlockSpec((tm, k), lambda j, i, kk: (i, 0)),
               b_spec=pl.BlockSpec((None, k, ns), lambda j, i, kk: (j, 0, 0)),
               o_spec=pl.BlockSpec((tm, ns), lambda j, i, kk: (i, j)),
               out_shape=jax.ShapeDtypeStruct((t, N_CHIPS * ns), out_dtype), acc_shape=(tm, ns), rider=rider)


def _mm_bwd_cols(name, d, w3, out_dtype=F32, rider=None, epilogue=None):
    pair = isinstance(d, tuple)
    t = d[0].shape[0] if pair else d.shape[0]
    k, ns = w3.shape[1], w3.shape[2]
    dsize = _size(d[0] if pair else d)
    tm = _mm_tile(t, (4 if pair else 2) * ns * dsize + 2 * k * jnp.dtype(out_dtype).itemsize + 4 * k
                  + _row_epilogue(epilogue, 8)[1], 2 * k * ns * _size(w3))
    half = N_CHIPS // 2
    if pair:
        a_spec = (pl.BlockSpec((tm, ns), lambda i, j, kk: (i, jnp.minimum(kk, half - 1))),
                  pl.BlockSpec((tm, ns), lambda i, j, kk: (i, jnp.maximum(kk - half, 0))))
    else:
        a_spec = pl.BlockSpec((tm, ns), lambda i, j, kk: (i, kk))
    return _mm(name, d, w3, mode="nt", grid=(t // tm, 1, N_CHIPS), a_spec=a_spec,
               b_spec=pl.BlockSpec((None, k, ns), lambda i, j, kk: (kk, 0, 0)),
               o_spec=pl.BlockSpec((tm, k), lambda i, j, kk: (i, 0)),
               out_shape=jax.ShapeDtypeStruct((t, k), out_dtype), acc_shape=(tm, k), pair_axis=2 if pair else None,
               rider=rider, epilogue=_row_epilogue(epilogue, tm)[0])


def _mm_wgrad_cols(name, a, d, rider=None):
    pair = isinstance(d, tuple)
    t, k = a.shape
    ns = (2 * d[0].shape[1] if pair else d.shape[1]) // N_CHIPS
    dsize = _size(d[0] if pair else d)
    tk = _mm_tile(t, 2 * k * _size(a) + (4 if pair else 2) * ns * dsize, k * ns * (4 + 2 * 2))
    half = N_CHIPS // 2
    if pair:
        b_spec = (pl.BlockSpec((tk, ns), lambda j, i, kk: (jnp.where(j < half, kk, 0), jnp.minimum(j, half - 1))),
                  pl.BlockSpec((tk, ns), lambda j, i, kk: (jnp.where(j < half, 0, kk), jnp.maximum(j - half, 0))))
    else:
        b_spec = pl.BlockSpec((tk, ns), lambda j, i, kk: (kk, j))
    return _mm(name, a, d, mode="tn", grid=(N_CHIPS, 1, t // tk),
               a_spec=pl.BlockSpec((tk, k), lambda j, i, kk: (kk, 0)), b_spec=b_spec,
               o_spec=pl.BlockSpec((None, k, ns), lambda j, i, kk: (j, 0, 0)),
               out_shape=jax.ShapeDtypeStruct((N_CHIPS, k, ns), BF16), acc_shape=(k, ns),
               pair_axis=0 if pair else None, rider=rider)


MM_BLOCK_CAP = 1408


def _row_epilogue(epilogue, tm):
    if epilogue is None:
        return None, 0
    fn, arrays, sums = epilogue
    specs = [pl.BlockSpec((1, x.shape[1]), lambda i, j, kk: (0, 0)) if x.shape[0] == 1 else
             pl.BlockSpec((tm, x.shape[1]), lambda i, j, kk: (i, 0)) for x in arrays]
    return (fn, list(zip(arrays, specs)), list(sums)), sum(2 * x.shape[1] * _size(x) for x in arrays if x.shape[0] > 1)


def _mm_fwd_rows(name, a, w, res=None, out_dtype=F32, epilogue=None, rider=None):
    t, k = a.shape
    n = w.shape[1]
    tk = k if k <= MM_BLOCK_CAP else MM_BLOCK_CAP
    assert k % tk == 0
    row_bytes = 2 * tk * _size(a) + 2 * n * jnp.dtype(out_dtype).itemsize + (0 if res is None else 2 * n * 4) + 4 * n
    row_bytes += _row_epilogue(epilogue, 8)[1]
    tm = _mm_tile(t, row_bytes, 2 * tk * n * _size(w))
    return _mm(name, a, w, mode="nn", grid=(t // tm, 1, k // tk),
               a_spec=pl.BlockSpec((tm, tk), lambda i, j, kk: (i, kk)),
               b_spec=pl.BlockSpec((tk, n), lambda i, j, kk: (kk, 0)),
               o_spec=pl.BlockSpec((tm, n), lambda i, j, kk: (i, 0)),
               out_shape=jax.ShapeDtypeStruct((t, n), out_dtype), acc_shape=(tm, n),
               res=res, res_spec=None if res is None else pl.BlockSpec((tm, n), lambda i, j, kk: (i, 0)),
               epilogue=_row_epilogue(epilogue, tm)[0], rider=rider)


def _mm_bwd_rows(name, d, w, out_dtype=F32):
    t, n = d.shape
    k = w.shape[0]
    tn = k if k <= MM_BLOCK_CAP else MM_BLOCK_CAP
    assert k % tn == 0
    tm = _mm_tile(t, 2 * n * _size(d) + 2 * tn * jnp.dtype(out_dtype).itemsize, 2 * tn * n * _size(w))
    return _mm(name, d, w, mode="nt", grid=(t // tm, k // tn, 1),
               a_spec=pl.BlockSpec((tm, n), lambda i, j, kk: (i, 0)),
               b_spec=pl.BlockSpec((tn, n), lambda i, j, kk: (j, 0)),
               o_spec=pl.BlockSpec((tm, tn), lambda i, j, kk: (i, j)),
               out_shape=jax.ShapeDtypeStruct((t, k), out_dtype), acc_shape=(tm, tn))


def _mm_wgrad_rows(name, a, d):
    t, k = a.shape
    n = d.shape[1]
    nblk = next(b for b in (1, 2, 4) if (k // b) % LANES == 0 and k // b <= MM_BLOCK_CAP)
    ks = k // nblk
    tk = _mm_tile(t, 2 * ks * _size(a) + 2 * n * _size(d), ks * n * (4 + 2 * 2))
    return _mm(name, a, d, mode="tn", grid=(nblk, 1, t // tk),
               a_spec=pl.BlockSpec((tk, ks), lambda j, i, kk: (kk, j)),
               b_spec=pl.BlockSpec((tk, n), lambda j, i, kk: (kk, 0)),
               o_spec=pl.BlockSpec((ks, n), lambda j, i, kk: (j, 0)),
               out_shape=jax.ShapeDtypeStruct((k, n), BF16), acc_shape=(ks, n))


def _s5_discretize(a_re, a_im, log_dt, b_re, b_im):
    dt = jnp.exp(log_dt)[:, None]
    mag = jnp.exp(a_re * dt)
    ang = a_im * dt
    lb_re = mag * jnp.cos(ang)
    lb_im = mag * jnp.sin(ang)
    den = a_re * a_re + a_im * a_im
    n_re = lb_re - 1.0
    n_im = lb_im
    co_re = ((n_re * a_re + n_im * a_im) / den)[..., None]
    co_im = ((n_im * a_re - n_re * a_im) / den)[..., None]
    bb_re = co_re * b_re - co_im * b_im
    bb_im = co_re * b_im + co_im * b_re
    return lb_re, lb_im, bb_re, bb_im


def _s5_in_blocks(bb):
    g = bb.shape[0]
    nb = g // S5_BLOCK_GROUPS
    t = bb.reshape(nb, S5_BLOCK_GROUPS, S5_STATE, S5_GROUP).transpose(0, 1, 3, 2)
    eye = jnp.eye(S5_BLOCK_GROUPS, dtype=bb.dtype)
    full = t[:, :, :, None, :] * eye[None, :, None, :, None]
    return full.reshape(nb, S5_BLOCK_GROUPS * S5_GROUP, S5_BLOCK_GROUPS * S5_STATE)


def _s5_in_blocks_diag(blocks):
    nb = blocks.shape[0]
    t = blocks.reshape(nb, S5_BLOCK_GROUPS, S5_GROUP, S5_BLOCK_GROUPS, S5_STATE)
    d = jnp.einsum("bghgp->bghp", t)
    return d.transpose(0, 1, 3, 2).reshape(nb * S5_BLOCK_GROUPS, S5_STATE, S5_GROUP)


def _s5_out_blocks(c):
    g = c.shape[0]
    nb = g // S5_BLOCK_GROUPS
    t = c.reshape(nb, S5_BLOCK_GROUPS, S5_GROUP, S5_STATE).transpose(0, 1, 3, 2)
    eye = jnp.eye(S5_BLOCK_GROUPS, dtype=c.dtype)
    full = t[:, :, :, None, :] * eye[None, :, None, :, None]
    return full.reshape(nb, S5_BLOCK_GROUPS * S5_STATE, S5_BLOCK_GROUPS * S5_GROUP)


def _s5_out_blocks_diag(blocks):
    nb = blocks.shape[0]
    t = blocks.reshape(nb, S5_BLOCK_GROUPS, S5_STATE, S5_BLOCK_GROUPS, S5_GROUP)
    d = jnp.einsum("bgpgh->bgph", t)
    return d.transpose(0, 1, 3, 2).reshape(nb * S5_BLOCK_GROUPS, S5_GROUP, S5_STATE)


def _s5_scan_tables(lr, li, reverse):
    def cmul(a, b):
        return a[0] * b[0] - a[1] * b[1], a[0] * b[1] + a[1] * b[0]

    lam = (lr, -li) if reverse else (lr, li)
    pw = [lam]
    for _ in range(SUBLANES - 1):
        pw.append(cmul(pw[-1], lam))
    sub = jnp.arange(SUBLANES)[:, None]
    rows = []
    for s in (1, 2, 4):
        keep = (sub <= SUBLANES - 1 - s) if reverse else (sub >= s)
        rows.append(jnp.where(keep, pw[s - 1][0][None, :], 0.0))
        rows.append(jnp.where(keep, pw[s - 1][1][None, :], 0.0))
    order = list(range(SUBLANES - 1, -1, -1)) if reverse else list(range(SUBLANES))
    rows.append(jnp.stack([pw[i][0] for i in order]))
    rows.append(jnp.stack([pw[i][1] for i in order]))
    return jnp.concatenate(rows, axis=0)


def _s5_scan(vre_ref, vim_ref, coef_ref, seq, width, reverse, xre_ref=None, xim_ref=None):
    nt = seq // SUBLANES
    nl = width // LANES
    per = 2 if xre_ref is None else 4
    sub = lax.broadcasted_iota(jnp.int32, (SUBLANES, LANES), 0)

    def step(k, carry):
        kk = (nt - 1 - k) if reverse else k
        rows = pl.ds(pl.multiple_of(kk * SUBLANES, SUBLANES), SUBLANES)
        out = []
        for j in range(nl):
            lanes = slice(j * LANES, (j + 1) * LANES)
            co = [coef_ref[SUBLANES * q:SUBLANES * (q + 1), lanes] for q in range(8)]
            cr, ci = carry[per * j], carry[per * j + 1]
            vr = vre_ref[rows, lanes]
            vi = vim_ref[rows, lanes]
            for q, s in enumerate((1, 2, 4)):
                sh = SUBLANES - s if reverse else s
                rr = pltpu.roll(vr, sh, 0)
                ri = pltpu.roll(vi, sh, 0)
                ar, ai = co[2 * q], co[2 * q + 1]
                vr, vi = vr + ar * rr - ai * ri, vi + ar * ri + ai * rr
            edge = 0 if reverse else SUBLANES - 1
            cbr = jnp.broadcast_to(cr[edge:edge + 1, :], (SUBLANES, LANES))
            cbi = jnp.broadcast_to(ci[edge:edge + 1, :], (SUBLANES, LANES))
            pr, pi = co[6], co[7]
            vr, vi = vr + pr * cbr - pi * cbi, vi + pr * cbi + pi * cbr
            vre_ref[rows, lanes] = vr
            vim_ref[rows, lanes] = vi
            out += [vr, vi]
            if xre_ref is not None:
                nr = jnp.where(sub == SUBLANES - 1, cbr, pltpu.roll(vr, SUBLANES - 1, 0))
                ni = jnp.where(sub == SUBLANES - 1, cbi, pltpu.roll(vi, SUBLANES - 1, 0))
                xr = xre_ref[rows, lanes]
                xi = xim_ref[rows, lanes]
                out += [carry[per * j + 2] + nr * xr + ni * xi, carry[per * j + 3] + ni * xr - nr * xi]
        return tuple(out)

    zero = jnp.zeros((SUBLANES, LANES), F32)
    res = lax.fori_loop(0, nt, step, (zero,) * (per * nl))
    if xre_ref is None:
        return None
    return jnp.concatenate(
        [jnp.concatenate([jnp.sum(res[per * j + 2], axis=0, keepdims=True) for j in range(nl)], axis=1),
         jnp.concatenate([jnp.sum(res[per * j + 3], axis=0, keepdims=True) for j in range(nl)], axis=1)], axis=0)


def _s5_fwd(z, bre3, bim3, cre3, cim3, coef, dskip, *, nseq, seq, rider=None):
    nb = bre3.shape[0]
    ch, ns = bre3.shape[1], bre3.shape[2]

    def body(za_ref, bre_ref, bim_ref, cre_ref, cim_ref, coef_ref, d_ref, y_ref, xre_ref, xim_ref):
        za = za_ref[...]
        xre_ref[...] = _dot(za, bre_ref[...])
        xim_ref[...] = _dot(za, bim_ref[...])
        _s5_scan(xre_ref, xim_ref, coef_ref, seq, ns, False)
        y_ref[...] = _dot(xre_ref[...], cre_ref[...]) - _dot(xim_ref[...], cim_ref[...]) + d_ref[...] * za

    blk3 = lambda r, c: pl.BlockSpec((None, r, c), lambda b, j: (j, 0, 0))
    return _hosted_call(
        "s5_fwd", body, grid=(nseq, nb),
        in_specs=[pl.BlockSpec((seq, ch), lambda b, j: (b, j)), blk3(ch, ns), blk3(ch, ns), blk3(ns, ch), blk3(ns, ch),
                  pl.BlockSpec((8 * SUBLANES, ns), lambda b, j: (0, j)), pl.BlockSpec((1, ch), lambda b, j: (0, j))],
        out_specs=[pl.BlockSpec((seq, ch), lambda b, j: (b, j)), pl.BlockSpec((seq, ns), lambda b, j: (b, j)),
                   pl.BlockSpec((seq, ns), lambda b, j: (b, j))],
        out_shape=[jax.ShapeDtypeStruct((nseq * seq, nb * ch), F32), jax.ShapeDtypeStruct((nseq * seq, nb * ns), F32),
                   jax.ShapeDtypeStruct((nseq * seq, nb * ns), F32)],
        operands=(z, bre3, bim3, cre3, cim3, coef, dskip), rider=rider)


def _s5_bwd(dy, z, xre, xim, bre3, bim3, cre3, cim3, coef_rev, dskip, *, nseq, seq, rider=None):
    nb = bre3.shape[0]
    ch, ns = bre3.shape[1], bre3.shape[2]

    def body(dy_ref, za_ref, xre_ref, xim_ref, bre_ref, bim_ref, cre_ref, cim_ref, coef_ref, d_ref,
             dza_ref, dbre_ref, dbim_ref, dcre_ref, dcim_ref, dlam_ref, dd_ref, are_ref, aim_ref):
        dy = dy_ref[...]
        za = za_ref[...]
        are_ref[...] = _dot_nt(dy, cre_ref[...])
        aim_ref[...] = -_dot_nt(dy, cim_ref[...])
        dlam = _s5_scan(are_ref, aim_ref, coef_ref, seq, ns, True, xre_ref, xim_ref)
        are = are_ref[...]
        aim = aim_ref[...]
        dza_ref[...] = (_dot_nt(are, bre_ref[...]) + _dot_nt(aim, bim_ref[...]) + d_ref[...] * dy).astype(dza_ref.dtype)
        parts = (_dot_tn(za, are), _dot_tn(za, aim), _dot_tn(xre_ref[...], dy), -_dot_tn(xim_ref[...], dy),
                 dlam, jnp.sum(dy * za, axis=0, keepdims=True))
        first = pl.program_id(1) == 0
        for r, v in zip((dbre_ref, dbim_ref, dcre_ref, dcim_ref, dlam_ref, dd_ref), parts):
            @pl.when(first)
            def _():
                r[...] = v

            @pl.when(jnp.logical_not(first))
            def _():
                r[...] += v

    blk3 = lambda r, c: pl.BlockSpec((None, r, c), lambda j, b: (j, 0, 0))
    tok = lambda c: pl.BlockSpec((seq, c), lambda j, b: (b, j))
    return _hosted_call(
        "s5_bwd", body, grid=(nb, nseq),
        in_specs=[tok(ch), tok(ch), tok(ns), tok(ns), blk3(ch, ns), blk3(ch, ns), blk3(ns, ch), blk3(ns, ch),
                  pl.BlockSpec((8 * SUBLANES, ns), lambda j, b: (0, j)), pl.BlockSpec((1, ch), lambda j, b: (0, j))],
        out_specs=[tok(ch), blk3(ch, ns), blk3(ch, ns), blk3(ns, ch), blk3(ns, ch),
                   pl.BlockSpec((None, 2, ns), lambda j, b: (j, 0, 0)), pl.BlockSpec((1, ch), lambda j, b: (0, j))],
        out_shape=[jax.ShapeDtypeStruct((nseq * seq, nb * ch), BF16),
                   jax.ShapeDtypeStruct((nb, ch, ns), F32), jax.ShapeDtypeStruct((nb, ch, ns), F32),
                   jax.ShapeDtypeStruct((nb, ns, ch), F32), jax.ShapeDtypeStruct((nb, ns, ch), F32),
                   jax.ShapeDtypeStruct((nb, 2, ns), F32), jax.ShapeDtypeStruct((1, nb * ch), F32)],
        scratch_shapes=[pltpu.VMEM((seq, ns), F32), pltpu.VMEM((seq, ns), F32)],
        operands=(dy, z, xre, xim, bre3, bim3, cre3, cim3, coef_rev, dskip), rider=rider)


def _glu_fwd(y, wglu, bglu):
    t, w = y.shape
    tm = _row_tile(t, GLU_TILE)

    def body(y_ref, w_ref, b_ref, a0_ref, gl_ref, a_ref):
        a0 = _gelu(y_ref[...])
        gl = _dot(a0, w_ref[...])
        a0_ref[...] = a0.astype(a0_ref.dtype)
        gl_ref[...] = gl
        a_ref[...] = (a0 * _sigmoid(gl + b_ref[...])).astype(a_ref.dtype)

    tok = pl.BlockSpec((tm, w), lambda i: (i, 0))
    return pl.pallas_call(
        body, name="s5_glu", grid=(t // tm,),
        in_specs=[tok, pl.BlockSpec((w, w), lambda i: (0, 0)), pl.BlockSpec((1, w), lambda i: (0, 0))],
        out_specs=[tok, tok, tok],
        out_shape=[jax.ShapeDtypeStruct((t, w), BF16), jax.ShapeDtypeStruct((t, w), F32), jax.ShapeDtypeStruct((t, w), BF16)],
        compiler_params=_params("arbitrary"),
    )(y, wglu, bglu)


def _glu_bwd(y, gl, bglu, da, wglu):
    t, w = y.shape
    tm = _row_tile(t, GLU_TILE)

    def body(y_ref, gl_ref, b_ref, da_ref, w_ref, dgl_ref, dy_ref, db_ref):
        yv = y_ref[...]
        dav = da_ref[...]
        s = _sigmoid(gl_ref[...] + b_ref[...])
        dgl = dav * _gelu(yv) * s * (1.0 - s)
        dgl_ref[...] = dgl.astype(dgl_ref.dtype)
        dy_ref[...] = (dav * s + _dot_nt(dgl, w_ref[...])) * _gelu_grad(yv)
        part = jnp.sum(dgl, axis=0, keepdims=True)
        first = pl.program_id(0) == 0

        @pl.when(first)
        def _():
            db_ref[...] = part

        @pl.when(jnp.logical_not(first))
        def _():
            db_ref[...] += part

    tok = pl.BlockSpec((tm, w), lambda i: (i, 0))
    vec = pl.BlockSpec((1, w), lambda i: (0, 0))
    return pl.pallas_call(
        body, name="s5_glu_bwd", grid=(t // tm,),
        in_specs=[tok, tok, vec, tok, pl.BlockSpec((w, w), lambda i: (0, 0))], out_specs=[tok, tok, vec],
        out_shape=[jax.ShapeDtypeStruct((t, w), BF16), jax.ShapeDtypeStruct((t, w), F32), jax.ShapeDtypeStruct((1, w), F32)],
        compiler_params=_params("arbitrary"),
    )(y, gl, bglu, da, wglu)


def _cumsum_rows(x, reverse=False):
    n = x.shape[0]
    row = lax.broadcasted_iota(jnp.int32, x.shape, 0)
    s = 1
    while s < n:
        if reverse:
            x = x + jnp.where(row < n - s, pltpu.roll(x, n - s, 0), 0.0)
        else:
            x = x + jnp.where(row >= s, pltpu.roll(x, s, 0), 0.0)
        s *= 2
    return x


def _hg_gates(zq, zf, lb):
    sg = _sigmoid(zf)
    f = lb + (1.0 - lb) * sg
    sq = _sigmoid(zq)
    qa = zq * sq * (HEAD ** -0.5)
    b = _cumsum_rows(jnp.log(f))
    return sg, f, sq, qa, 1.0 - f, b


SUB = 16


def _hg_scores(qa, kk, b):
    c = qa.shape[0]
    row = lax.broadcasted_iota(jnp.int32, qa.shape, 0)
    pos = jnp.bitwise_and(row, SUB - 1)
    dmat = lax.broadcasted_iota(jnp.int32, (c, c), 0) - lax.broadcasted_iota(jnp.int32, (c, c), 1)
    p = jnp.zeros((c, c), F32)
    for d in range(SUB):
        if d == 0:
            fd = qa * kk
        else:
            e = jnp.exp(jnp.minimum(b - pltpu.roll(b, d, 0), 0.0))
            fd = jnp.where(pos >= d, qa * pltpu.roll(kk, d, 0) * e, 0.0)
        p = jnp.where(dmat == d, jnp.sum(fd, axis=1, keepdims=True), p)
    col = lax.broadcasted_iota(jnp.int32, (SUB, c), 1)
    blocks = [jnp.zeros((SUB, c), F32)]
    for r0 in range(SUB, c, SUB):
        beta = b[r0 - 1:r0, :]
        qt = qa[r0:r0 + SUB] * jnp.exp(b[r0:r0 + SUB] - beta)
        kt = kk * jnp.exp(jnp.minimum(beta - b, 0.0))
        blocks.append(jnp.where(col < r0, _dot_nt(qt, kt), 0.0))
    return p + jnp.concatenate(blocks, axis=0)


def _hg_scores_bwd(dp, qa, kk, b):
    c = qa.shape[0]
    row = lax.broadcasted_iota(jnp.int32, qa.shape, 0)
    pos = jnp.bitwise_and(row, SUB - 1)
    dmat = lax.broadcasted_iota(jnp.int32, (c, c), 0) - lax.broadcasted_iota(jnp.int32, (c, c), 1)
    dqa = jnp.zeros_like(qa)
    dkk = jnp.zeros_like(qa)
    db = jnp.zeros_like(qa)
    for d in range(SUB):
        dcol = jnp.sum(jnp.where(dmat == d, dp, 0.0), axis=1, keepdims=True)
        if d == 0:
            dqa = dqa + dcol * kk
            dkk = dkk + dcol * qa
        else:
            e = jnp.exp(jnp.minimum(b - pltpu.roll(b, d, 0), 0.0))
            w = jnp.where(pos >= d, dcol * e, 0.0)
            kr = pltpu.roll(kk, d, 0)
            dqa = dqa + w * kr
            tmp = w * qa
            dkk = dkk + pltpu.roll(tmp, c - d, 0)
            x = tmp * kr
            db = db + x - pltpu.roll(x, c - d, 0)
    col = lax.broadcasted_iota(jnp.int32, (SUB, c), 1)
    dq_blocks = [jnp.zeros((SUB, qa.shape[1]), F32)]
    db_blocks = [jnp.zeros((SUB, qa.shape[1]), F32)]
    for r0 in range(SUB, c, SUB):
        beta = b[r0 - 1:r0, :]
        eq = jnp.exp(b[r0:r0 + SUB] - beta)
        ek = jnp.exp(jnp.minimum(beta - b, 0.0))
        qt = qa[r0:r0 + SUB] * eq
        kt = kk * ek
        dpi = jnp.where(col < r0, dp[r0:r0 + SUB, :], 0.0)
        dqt = _dot(dpi, kt)
        dkt = _dot_tn(dpi, qt)
        dq_blocks.append(dqt * eq)
        db_blocks.append(dqt * qt)
        dkk = dkk + dkt * ek
        db = db - dkt * kt
    return dqa + jnp.concatenate(dq_blocks, axis=0), dkk, db + jnp.concatenate(db_blocks, axis=0)


def _hg_chunks_per_step(seq):
    nc = seq // CHUNK
    cps = next(k for k in (8, 4, 2, 1) if nc % k == 0)
    return nc, cps, nc // cps


def _hg_fwd(z, lbrow, gain, *, nseq, seq, heads, qoff, rider=None):
    nc, cps, nblk = _hg_chunks_per_step(seq)
    blk = cps * CHUNK
    zspec = lambda off: pl.BlockSpec((blk, HEAD), lambda h, b, n, off=off: (b * nblk + n, off + h))

    def body(zq_ref, zf_ref, zi_ref, zg_ref, lb_ref, gn_ref, o_ref, yb_ref, st_ref, sc_ref, state):
        @pl.when(pl.program_id(2) == 0)
        def _():
            state[...] = jnp.zeros_like(state)

        lb = lb_ref[...]
        gain_v = gn_ref[...]

        def chunk(ci, carry):
            rows = pl.ds(pl.multiple_of(ci * CHUNK, CHUNK), CHUNK)
            st = state[...]
            st_ref[ci] = st
            zi = zi_ref[rows, :]
            zg = zg_ref[rows, :]
            _, _, _, qa, kk, b = _hg_gates(zq_ref[rows, :], zf_ref[rows, :], lb)
            scores = _hg_scores(qa, kk, b).astype(BF16)
            sc_ref[rows, :] = scores
            o = _dot_nt(qa * jnp.exp(b), st) + _dot(scores, zi)
            bl = b[CHUNK - 1:CHUNK, :]
            state[...] = st * jnp.exp(bl) + _dot_tn(zi, kk * jnp.exp(bl - b))
            o_ref[rows, :] = o
            r = lax.rsqrt(jnp.mean(o * o, axis=1, keepdims=True) + EPS)
            yb_ref[rows, :] = (o * r * gain_v * zg * _sigmoid(zg)).astype(yb_ref.dtype)
            return carry

        lax.fori_loop(0, cps, chunk, 0, unroll=True)

    tok = pl.BlockSpec((blk, HEAD), lambda h, b, n: (b * nblk + n, h))
    vec = pl.BlockSpec((1, HEAD), lambda h, b, n: (0, h))
    rows = nseq * seq
    return _hosted_call(
        "hgrn2_fwd", body, grid=(heads, nseq, nblk),
        in_specs=[zspec(qoff), zspec(qoff + heads), zspec(qoff + 2 * heads), zspec(qoff + 3 * heads), vec, vec],
        out_specs=[tok, tok, pl.BlockSpec((None, None, cps, HEAD, HEAD), lambda h, b, n: (h, b, n, 0, 0)),
                   pl.BlockSpec((None, blk, CHUNK), lambda h, b, n: (h, b * nblk + n, 0))],
        out_shape=[jax.ShapeDtypeStruct((rows, heads * HEAD), F32), jax.ShapeDtypeStruct((rows, heads * HEAD), BF16),
                   jax.ShapeDtypeStruct((heads, nseq, nc, HEAD, HEAD), F32),
                   jax.ShapeDtypeStruct((heads, rows, CHUNK), BF16)],
        scratch_shapes=[pltpu.VMEM((HEAD, HEAD), F32)], operands=(z, z, z, z, lbrow, gain), rider=rider)


def _hg_bwd(dyb, z, o, states, scores, lbrow, gain, *, nseq, seq, heads, qoff, rider=None):
    nc, cps, nblk = _hg_chunks_per_step(seq)
    blk = cps * CHUNK
    rev = lambda n: nblk - 1 - n
    zspec = lambda off: pl.BlockSpec((blk, HEAD), lambda h, b, n, off=off: (b * nblk + rev(n), off + h))

    def body(dyb_ref, zq_ref, zf_ref, zi_ref, zg_ref, o_ref, st_ref, sc_ref, lb_ref, gn_ref,
             dzq_ref, dzf_ref, dzi_ref, dzg_ref, dlb_ref, dgn_ref, dstate):
        @pl.when(pl.program_id(2) == 0)
        def _():
            dstate[...] = jnp.zeros_like(dstate)

        @pl.when(jnp.logical_and(pl.program_id(1) == 0, pl.program_id(2) == 0))
        def _():
            dlb_ref[...] = jnp.zeros_like(dlb_ref)
            dgn_ref[...] = jnp.zeros_like(dgn_ref)

        lb = lb_ref[...]
        gain_v = gn_ref[...]
        c = CHUNK
        causal = lax.broadcasted_iota(jnp.int32, (c, c), 0) >= lax.broadcasted_iota(jnp.int32, (c, c), 1)

        def chunk(step, carry):
            ci = cps - 1 - step
            rows = pl.ds(pl.multiple_of(ci * CHUNK, CHUNK), CHUNK)
            zq = zq_ref[rows, :]
            zi = zi_ref[rows, :]
            zg = zg_ref[rows, :]
            sg, f, sq, qa, kk, b = _hg_gates(zq, zf_ref[rows, :], lb)
            eb = jnp.exp(b)
            qt = qa * eb
            bl = b[c - 1:c, :]
            ebl = jnp.exp(bl)
            ekb = jnp.exp(bl - b)
            kh = kk * ekb
            st = st_ref[ci]
            dst = dstate[...]
            o = o_ref[rows, :]
            r = lax.rsqrt(jnp.mean(o * o, axis=1, keepdims=True) + EPS)
            oh = o * r
            szg = _sigmoid(zg)
            dyb = dyb_ref[rows, :]
            don = dyb * zg * szg
            dzg_ref[rows, :] = (dyb * oh * gain_v * szg * (1.0 + zg * (1.0 - szg))).astype(dzg_ref.dtype)
            doh = don * gain_v
            do = r * (doh - oh * jnp.mean(doh * oh, axis=1, keepdims=True))
            dqt = _dot(do, st)
            dp = jnp.where(causal, _dot_nt(do, zi), 0.0)
            dzi_ref[rows, :] = (_dot_tn(sc_ref[rows, :], do) + _dot_nt(kh, dst)).astype(dzi_ref.dtype)
            dkh = _dot(zi, dst)
            dbl = jnp.sum(dkh * kh, axis=0, keepdims=True) + jnp.sum(dst * st, axis=0, keepdims=True) * ebl
            dstate[...] = _dot_tn(do, qt) + dst * ebl
            dqa_s, dkk_s, db_s = _hg_scores_bwd(dp, qa, kk, b)
            dqa = dqt * eb + dqa_s
            dkk = dkh * ekb + dkk_s
            db = dqt * qt - dkh * kh + db_s
            row = lax.broadcasted_iota(jnp.int32, db.shape, 0)
            db = db + jnp.where(row == c - 1, dbl, 0.0)
            df = _cumsum_rows(db, reverse=True) / f - dkk
            dzf_ref[rows, :] = (df * (1.0 - lb) * sg * (1.0 - sg)).astype(dzf_ref.dtype)
            dzq_ref[rows, :] = (dqa * (HEAD ** -0.5) * sq * (1.0 + zq * (1.0 - sq))).astype(dzq_ref.dtype)
            dlb_ref[...] += jnp.sum(df * (1.0 - sg), axis=0, keepdims=True)
            dgn_ref[...] += jnp.sum(don * oh, axis=0, keepdims=True)
            return carry

        lax.fori_loop(0, cps, chunk, 0, unroll=True)

    tok = pl.BlockSpec((blk, HEAD), lambda h, b, n: (b * nblk + rev(n), h))
    vec = pl.BlockSpec((1, HEAD), lambda h, b, n: (0, h))
    rows = nseq * seq
    return _hosted_call(
        "hgrn2_bwd", body, grid=(heads, nseq, nblk),
        in_specs=[tok, zspec(qoff), zspec(qoff + heads), zspec(qoff + 2 * heads), zspec(qoff + 3 * heads), tok,
                  pl.BlockSpec((None, None, cps, HEAD, HEAD), lambda h, b, n: (h, b, rev(n), 0, 0)),
                  pl.BlockSpec((None, blk, CHUNK), lambda h, b, n: (h, b * nblk + rev(n), 0)), vec, vec],
        out_specs=[tok, tok, tok, tok, vec, vec],
        out_shape=[jax.ShapeDtypeStruct((rows, heads * HEAD), BF16)] * 4
        + [jax.ShapeDtypeStruct((1, heads * HEAD), F32)] * 2,
        scratch_shapes=[pltpu.VMEM((HEAD, HEAD), F32)],
        operands=(dyb, z, z, z, z, o, states, scores, lbrow, gain), rider=rider)


def _shift_rows(x, k):
    n = x.shape[0]
    r = pltpu.roll(x, k % n, 0)
    sub = lax.broadcasted_iota(jnp.int32, (SUBLANES, x.shape[1]), 0)
    if k > 0:
        return jnp.concatenate([jnp.where(sub >= k, r[0:SUBLANES], 0.0), r[SUBLANES:]], axis=0)
    return jnp.concatenate([r[:n - SUBLANES], jnp.where(sub < SUBLANES + k, r[n - SUBLANES:], 0.0)], axis=0)


def _conv_taps(h, w, bias):
    h1 = _shift_rows(h, 1)
    h2 = _shift_rows(h, 2)
    return h2 * w[0:1, :] + h1 * w[1:2, :] + h * w[2:3, :] + bias, h1, h2


def _conv_fwd(h, wconv, bconv, *, nseq, seq):
    ff2 = h.shape[1]
    ncol = ff2 // 2 // LANES

    def body(hg_ref, hv_ref, wg_ref, wv_ref, bg_ref, bv_ref, a_ref):
        g, _, _ = _conv_taps(hg_ref[...].astype(F32), wg_ref[...], bg_ref[...])
        v, _, _ = _conv_taps(hv_ref[...].astype(F32), wv_ref[...], bv_ref[...])
        a_ref[...] = (g * _sigmoid(g) * v).astype(a_ref.dtype)

    tok = lambda off: pl.BlockSpec((seq, LANES), lambda j, b, off=off: (b, off + j))
    wsp = lambda off: pl.BlockSpec((CONV_W, LANES), lambda j, b, off=off: (0, off + j))
    bsp = lambda off: pl.BlockSpec((1, LANES), lambda j, b, off=off: (0, off + j))
    return pl.pallas_call(
        body, name="conv_fwd", grid=(ncol, nseq),
        in_specs=[tok(0), tok(ncol), wsp(0), wsp(ncol), bsp(0), bsp(ncol)],
        out_specs=tok(0), out_shape=jax.ShapeDtypeStruct((nseq * seq, ff2 // 2), BF16),
        compiler_params=_params("arbitrary", "arbitrary"),
    )(h, h, wconv, wconv, bconv, bconv)


def _conv_bwd(da, h, wconv, bconv, *, nseq, seq):
    ff2 = h.shape[1]
    ncol = ff2 // 2 // LANES

    def half_bwd(d, hcur, h1, h2, w):
        d1 = _shift_rows(d, -1)
        d2 = _shift_rows(d, -2)
        dh = d * w[2:3, :] + d1 * w[1:2, :] + d2 * w[0:1, :]
        stats = jnp.concatenate(
            [jnp.sum(h2 * d, axis=0, keepdims=True), jnp.sum(h1 * d, axis=0, keepdims=True),
             jnp.sum(hcur * d, axis=0, keepdims=True), jnp.sum(d, axis=0, keepdims=True),
             jnp.zeros((SUBLANES - 4, d.shape[1]), F32)], axis=0)
        return dh, stats

    def body(da_ref, hg_ref, hv_ref, wg_ref, wv_ref, bg_ref, bv_ref, dhg_ref, dhv_ref, sg_ref, sv_ref):
        hg = hg_ref[...].astype(F32)
        hv = hv_ref[...].astype(F32)
        wg = wg_ref[...]
        wv = wv_ref[...]
        g, g1, g2 = _conv_taps(hg, wg, bg_ref[...])
        v, v1, v2 = _conv_taps(hv, wv, bv_ref[...])
        da = da_ref[...].astype(F32)
        s = _sigmoid(g)
        dhg, stg = half_bwd(da * v * s * (1.0 + g * (1.0 - s)), hg, g1, g2, wg)
        dhv, stv = half_bwd(da * g * s, hv, v1, v2, wv)
        dhg_ref[...] = dhg.astype(dhg_ref.dtype)
        dhv_ref[...] = dhv.astype(dhv_ref.dtype)
        first = pl.program_id(1) == 0
        for r, val in ((sg_ref, stg), (sv_ref, stv)):
            @pl.when(first)
            def _():
                r[...] = val

            @pl.when(jnp.logical_not(first))
            def _():
                r[...] += val

    tok = lambda off: pl.BlockSpec((seq, LANES), lambda j, b, off=off: (b, off + j))
    wsp = lambda off: pl.BlockSpec((CONV_W, LANES), lambda j, b, off=off: (0, off + j))
    bsp = lambda off: pl.BlockSpec((1, LANES), lambda j, b, off=off: (0, off + j))
    ssp = pl.BlockSpec((SUBLANES, LANES), lambda j, b: (0, j))
    dhg, dhv, stg, stv = pl.pallas_call(
        body, name="conv_bwd", grid=(ncol, nseq),
        in_specs=[tok(0), tok(0), tok(ncol), wsp(0), wsp(ncol), bsp(0), bsp(ncol)],
        out_specs=[tok(0), tok(0), ssp, ssp],
        out_shape=[jax.ShapeDtypeStruct((nseq * seq, ff2 // 2), BF16)] * 2
        + [jax.ShapeDtypeStruct((SUBLANES, ff2 // 2), F32)] * 2,
        compiler_params=_params("arbitrary", "arbitrary"),
    )(da, h, h, wconv, wconv, bconv, bconv)
    return (dhg, dhv), jnp.concatenate([stg, stv], axis=1)


def _rms_fwd(xv, g):
    r = lax.rsqrt(jnp.mean(xv * xv, axis=1, keepdims=True) + EPS)
    return (xv * r * g,)


def _rms_bwd(xv, g, dy, res):
    r = lax.rsqrt(jnp.mean(xv * xv, axis=1, keepdims=True) + EPS)
    xh = xv * r
    dxh = dy * g
    dx = r * (dxh - xh * jnp.mean(dxh * xh, axis=1, keepdims=True)) + res
    return dx, jnp.sum(dy * xh, axis=0, keepdims=True)


def _loss_head(x2, tgt, g):
    d = x2.shape[1]
    r = lax.rsqrt(jnp.mean(x2 * x2, axis=1, keepdims=True) + EPS)
    xh = x2 * r
    err = xh * g - tgt
    dy = err * (1.0 / d)
    dxh = dy * g
    dx = r * (dxh - xh * jnp.mean(dxh * xh, axis=1, keepdims=True))
    loss = 0.5 * jnp.sum(jnp.mean(err * err, axis=1, keepdims=True), axis=0, keepdims=True)
    return dx, jnp.sum(dy * xh, axis=0, keepdims=True), jnp.broadcast_to(loss, (1, LANES))


LATE_A = ("w_down", "w_out")
LATE_B = ("w_up", "w_pa", "w_pb")
LATE = LATE_A + LATE_B
EARLY_GRADS = ("w_down", "w_up", "w_out", "w_pa", "w_pb", "w_glu")
ROW_SHARDED = ("w_glu", "w_out", "w_down")


def _local_step(x, tgt, u, z_own, p, late, *, nseq, seq):
    p = dict(p)
    chip = 2 * lax.axis_index("x") + lax.axis_index("y")
    t, d = x.shape
    s5w = p["s5_d"].shape[1]
    hgw = p["gain"].shape[1]
    heads = hgw // HEAD
    qoff = s5w // LANES
    gblk = (s5w + 4 * hgw) // GATE_BLOCK
    ngb = d // GATE_BLOCK
    tm = _row_tile(t, ROW_TILE)
    row = lambda a, w=None, base=0: (a, a.shape[1] if w is None else w, base, "row")
    vec = lambda a, w=None, base=0: (a, a.shape[1] if w is None else w, base, "vec")
    rw = functools.partial(_rowwise, rows=t, tm=tm)

    z, _ = _in_proj_rest(u, p["w_in"], z_own, None)

    lam_re, lam_im, bb_re, bb_im = _s5_discretize(p["s5_a_re"], p["s5_a_im"], p["s5_log_dt"], p["s5_b_re"], p["s5_b_im"])
    bre3 = _s5_in_blocks(bb_re).astype(BF16)
    bim3 = _s5_in_blocks(bb_im).astype(BF16)
    cre3 = _s5_out_blocks(p["s5_c_re"]).astype(BF16)
    cim3 = _s5_out_blocks(p["s5_c_im"]).astype(BF16)
    coef_f = _s5_scan_tables(lam_re.reshape(-1), lam_im.reshape(-1), False)
    coef_r = _s5_scan_tables(lam_re.reshape(-1), lam_im.reshape(-1), True)
    (o, yb, states, scores), landed_b = _hg_fwd(z, p["lbrow"], p["gain"], nseq=nseq, seq=seq, heads=heads, qoff=qoff,
                                                rider=_gather_ici_rider([late[n] for n in LATE_B]))
    def place_own(names, gathered):
        for n, g in zip(names, gathered):
            full = lax.dynamic_update_index_in_dim(g, late[n], chip, 0)
            p[n] = full.reshape(-1, full.shape[-1]) if n in ROW_SHARDED else full

    nb_late = len(LATE_B)
    (y5, xre, xim), got = _s5_fwd(z, bre3, bim3, cre3, cim3, coef_f, p["s5_d"], nseq=nseq, seq=seq,
                                  rider=_merge_riders(_gather_pass_rider(list(landed_b)),
                                                      _gather_ici_rider([late[n] for n in LATE_A])))
    place_own(LATE_B, got[:nb_late])
    ya0, gl, ya = _glu_fwd(y5, p["w_glu"], p["b_glu"])

    joined = lambda w3: w3.transpose(1, 0, 2).reshape(w3.shape[1], -1)
    split = lambda g: g.reshape(g.shape[0], N_CHIPS, -1).transpose(1, 0, 2)
    wpa, wpb = joined(p["w_pa"]), joined(p["w_pb"])
    pa, got_a = _mm_fwd_rows("proj_a", ya, wpa, out_dtype=BF16, rider=_gather_pass_rider(list(got[nb_late:])))
    place_own(LATE_A, got_a)
    pb = _mm_fwd_rows("proj_b", yb, wpb, out_dtype=BF16)
    gb = GATE_BLOCK
    (m,) = rw("merge", lambda ga, gbv, a, b: (_sigmoid(ga) * a + _sigmoid(gbv) * b,),
              [row(z, gb, gblk), row(z, gb, gblk + ngb), row(pa, gb), row(pb, gb)], [(d, gb, BF16)], ncol=ngb)
    x1 = _mm_fwd_rows("out_proj", m, p["w_out"], res=x)

    (u2,) = rw("rms_ffn", _rms_fwd, [row(x1), vec(p["g_ffn"])], [(d, d, BF16)])
    h = _mm_fwd_cols("up_proj", u2, p["w_up"], out_dtype=BF16)
    a = _conv_fwd(h, p["w_conv"], p["b_conv"], nseq=nseq, seq=seq)
    dx2, dg_final, lossv = _mm_fwd_rows("down_proj", a, p["w_down"], res=x1,
                                        epilogue=(_loss_head, [tgt, p["g_final"]], [d, LANES]))

    norm_bwd = lambda dyv, xv, g, resv: _rms_bwd(xv, g, dyv, resv)
    da = _mm_bwd_rows("down_bwd", dx2, p["w_down"], out_dtype=BF16)
    g_wdown = _mm_wgrad_rows("down_wgrad", a, dx2)
    dh, cstats = _conv_bwd(da, h, p["w_conv"], p["b_conv"], nseq=nseq, seq=seq)
    dx1, dg_ffn = _mm_bwd_cols("up_bwd", dh, p["w_up"], epilogue=(norm_bwd, [x1, p["g_ffn"], dx2], [d]))
    g_wup = _mm_wgrad_cols("up_wgrad", u2, dh)

    dm = _mm_bwd_rows("out_bwd", dx1, p["w_out"], out_dtype=BF16)
    g_wout = _mm_wgrad_rows("out_wgrad", m, dx1)

    def merge_bwd(ga, gbv, av, bv, dmv):
        sa = _sigmoid(ga)
        sb = _sigmoid(gbv)
        return dmv * sa, dmv * sb, dmv * av * sa * (1.0 - sa), dmv * bv * sb * (1.0 - sb)

    dpa, dpb, dzga, dzgb = rw("merge_bwd", merge_bwd,
                              [row(z, gb, gblk), row(z, gb, gblk + ngb), row(pa, gb), row(pb, gb), row(dm, gb)],
                              [(d, gb, BF16)] * 4, ncol=ngb)
    dya = _mm_bwd_rows("proj_a_bwd", dpa, wpa)
    g_wpa = split(_mm_wgrad_rows("proj_a_wgrad", ya, dpa))
    dyb = _mm_bwd_rows("proj_b_bwd", dpb, wpb)
    g_wpb = split(_mm_wgrad_rows("proj_b_wgrad", yb, dpb))

    dgl, dy5, db_glu = _glu_bwd(y5, gl, p["b_glu"], dya, p["w_glu"])
    g_wglu = _mm_wgrad_rows("glu_wgrad", ya0, dgl)
    partial = dict(w_down=g_wdown, w_up=g_wup, w_out=g_wout, w_pa=g_wpa, w_pb=g_wpb, w_glu=g_wglu)
    parts = [_grad_parts(partial[n]) for n in EARLY_GRADS]
    (dza, dbre3, dbim3, dcre3, dcim3, dlam, dd), sib = _s5_bwd(
        dy5, z, xre, xim, bre3, bim3, cre3, cim3, coef_r, p["s5_d"], nseq=nseq, seq=seq, rider=_swap_halves_rider(parts))
    pair = _pair_sums(EARLY_GRADS, parts, sib)
    (dzq, dzf, dzi, dzg, dlb, dgain), others = _hg_bwd(
        dyb, z, o, states, scores, p["lbrow"], p["gain"], nseq=nseq, seq=seq, heads=heads, qoff=qoff,
        rider=_scatter_rider(pair))
    halves = _chip_sums(EARLY_GRADS, pair, others)

    dz = jnp.concatenate([dza, dzq, dzf, dzi, dzg, dzga, dzgb], axis=1)
    gshape = lam_re.shape
    small = {
        "loss": lossv, "g_ffn": dg_ffn, "g_final": dg_final, "b_glu": db_glu, "gain": dgain,
        "lbrow": dlb, "s5_d": dd, "w_conv": cstats[0:CONV_W], "b_conv": cstats[CONV_W:CONV_W + 1],
        "lam_re": dlam[:, 0, :].reshape(gshape), "lam_im": dlam[:, 1, :].reshape(gshape),
        "bb_re": _s5_in_blocks_diag(dbre3), "bb_im": _s5_in_blocks_diag(dbim3),
        "s5_c_re": _s5_out_blocks_diag(dcre3), "s5_c_im": _s5_out_blocks_diag(dcim3),
    }
    small_vec = _pack([small[n] for n in SMALL_PARTS], F32)
    g_win, (small_all, *sibs) = _mm_wgrad_cols(
        "in_wgrad", u, dz, rider=_merge_riders(_gather_all_rider(small_vec), _swap_sums_rider(halves)))
    big = dict(zip(EARLY_GRADS, zip(halves, sibs)))
    small_sum = _sum_over_devices("small_grad_sum", small_vec, small_all)
    sm = dict(zip(SMALL_PARTS, _unpack(small_sum, [small[n].shape for n in SMALL_PARTS])))
    last = [_grad_parts(g_win)]
    pair = _pair_sums(("w_in",), last, _run_rider("grad_swap_halves", _swap_halves_rider(last)))
    (dx, dg_mix), others = _mm_bwd_cols("in_bwd", dz, p["w_in"], epilogue=(norm_bwd, [x, p["g_mix"], dx1], [d]),
                                        rider=_scatter_rider(pair))
    (half,) = _chip_sums(("w_in",), pair, others)
    mid = half.shape[0] // 2
    mix_vec = dg_mix.reshape(SUBLANES, -1)
    top, bottom, mix_all = _run_rider("grad_swap_sums", _merge_riders(_swap_sums_rider([half[:mid], half[mid:]]),
                                                                      _gather_all_rider(mix_vec)))
    big["w_in"] = (half, jnp.concatenate([top, bottom], axis=0))
    sm["g_mix"] = _sum_over_devices("g_mix_sum", mix_vec, mix_all).reshape(dg_mix.shape)
    return dx, big, sm


ANY = pl.BlockSpec(memory_space=pl.ANY)


def _place():
    x, y, c = lax.axis_index("x"), lax.axis_index("y"), lax.axis_index("c")
    chips = [(1 - x, y), (x, 1 - y), (1 - x, 1 - y)]
    return x, y, c, chips


def _remote(src, dst, send_sems, recv_sems, k, to):
    return pltpu.make_async_remote_copy(src_ref=src, dst_ref=dst, send_sem=send_sems.at[k], recv_sem=recv_sems.at[k],
                                        device_id=to, device_id_type=MESH)


def _half(rows, which):
    return pl.ds(pl.multiple_of(which * (rows // 2), SUBLANES), rows // 2)


class _SemView:
    def __init__(self, base, offset):
        self.base, self.offset = base, offset

    @property
    def at(self):
        return self

    def __getitem__(self, k):
        return self.base.at[self.offset + k]


def _merge_riders(first, second):
    na, no, ns = len(first.arrays), len(first.out_shapes), first.nsem

    def split(fn_a, fn_b):
        def run(ins, outs, send_sems, recv_sems):
            fn_a(ins[:na], outs[:no], send_sems, recv_sems)
            fn_b(ins[na:], outs[no:], _SemView(send_sems, ns), _SemView(recv_sems, ns))
        return run

    aliases = dict(first.aliases)
    aliases.update({na + i: no + o for i, o in second.aliases.items()})
    return _Rider(first.arrays + second.arrays, first.out_shapes + second.out_shapes, ns + second.nsem,
                  split(first.start, second.start), split(first.finish, second.finish), aliases)


PREPARE_TILE = 512


def _prepare(x, gain, w_own, arrays, rider):
    t, d = x.shape
    ns = w_own.shape[1]
    n = len(arrays)
    tm = _row_tile(t, PREPARE_TILE)

    def body(place, x_ref, g_ref, w_ref, *refs):
        (u,) = _rms_fwd(x_ref[...], g_ref[...])
        u = u.astype(BF16)
        refs[n][...] = u
        refs[n + 1][...] = _dot(u, w_ref[...])

        @pl.when(pl.program_id(0) == 0)
        def _():
            for i in range(n):
                refs[n + 2 + i][...] = refs[i][...].astype(BF16)

    vm = pl.BlockSpec(memory_space=pltpu.VMEM)
    tok = pl.BlockSpec((tm, d), lambda i, place: (i, 0))
    outs, gathered = _hosted_scalar_call(
        "prepare", body, grid=(t // tm,),
        in_specs=[tok, pl.BlockSpec((1, d), lambda i, place: (0, 0)), vm] + [vm] * n,
        out_specs=[tok, pl.BlockSpec((tm, ns), lambda i, place: (i, place[1]))] + [vm] * n,
        out_shape=[jax.ShapeDtypeStruct((t, d), BF16), jax.ShapeDtypeStruct((t, N_CHIPS * ns), F32)]
        + [jax.ShapeDtypeStruct(a.shape, BF16) for a in arrays],
        operands=[x, gain, w_own] + list(arrays), rider=rider)
    return outs[0], outs[1], outs[2:], gathered


def _in_proj_rest(u, w3, z, rider):
    t, k = u.shape
    ns = w3.shape[2]
    tm = _mm_tile(t, 2 * k * _size(u) + 2 * ns * 4, 2 * k * ns * _size(w3))
    other = lambda j, place: jnp.bitwise_xor(place[1], j + 1)

    def body(place, u_ref, w_ref, z_ref, o_ref):
        o_ref[...] = _dot(u_ref[...], w_ref[...])

    outs, ridden = _hosted_scalar_call(
        "in_proj", body, grid=(N_CHIPS - 1, t // tm),
        in_specs=[pl.BlockSpec((tm, k), lambda j, i, place: (i, 0)),
                  pl.BlockSpec((None, k, ns), lambda j, i, place: (other(j, place), 0, 0)), ANY],
        out_specs=[pl.BlockSpec((tm, ns), lambda j, i, place: (i, other(j, place)))],
        out_shape=[jax.ShapeDtypeStruct(z.shape, z.dtype)], operands=[u, w3, z], rider=rider, aliases={2: 0})
    return outs[0], ridden


def _symmetric_rider(arrays, out_shapes, copies_of, nsem):
    def start(ins, outs, send_sems, recv_sems):
        for cp in copies_of(ins, outs, send_sems, recv_sems):
            cp.start()

    def finish(ins, outs, send_sems, recv_sems):
        for cp in copies_of(ins, outs, send_sems, recv_sems):
            cp.wait()

    return _Rider(arrays, out_shapes, nsem, start, finish)


def _swap_halves_rider(parts):
    def copies_of(ins, outs, send_sems, recv_sems):
        x, y, c, _ = _place()
        return [_remote(ins[a].at[:, _half(g.shape[1], 1 - c), :], outs[a], send_sems, recv_sems, a, (x, y, 1 - c))
                for a, g in enumerate(parts)]

    shapes = [jax.ShapeDtypeStruct((g.shape[0], g.shape[1] // 2, g.shape[2]), g.dtype) for g in parts]
    return _symmetric_rider(parts, shapes, copies_of, len(parts))


def _scatter_rider(parts):
    def copies_of(ins, outs, send_sems, recv_sems):
        x, y, c, chips = _place()
        return [_remote(ins[a].at[2 * cx + cy], outs[a].at[j], send_sems, recv_sems, 3 * a + j, (cx, cy, c))
                for a in range(len(parts)) for j, (cx, cy) in enumerate(chips)]

    shapes = [jax.ShapeDtypeStruct((N_CHIPS - 1,) + h.shape[1:], h.dtype) for h in parts]
    return _symmetric_rider(parts, shapes, copies_of, 3 * len(parts))


def _swap_sums_rider(parts):
    def copies_of(ins, outs, send_sems, recv_sems):
        x, y, c, _ = _place()
        return [_remote(ins[a], outs[a], send_sems, recv_sems, a, (x, y, 1 - c)) for a in range(len(parts))]

    shapes = [jax.ShapeDtypeStruct(g.shape, g.dtype) for g in parts]
    return _symmetric_rider(parts, shapes, copies_of, len(parts))


def _gather_ici_rider(shards):
    def sends(ins, outs, send_sems, recv_sems):
        x, y, c, chips = _place()
        return [_remote(ins[a].at[_half(s.shape[0], c)], outs[a].at[2 * x + y, _half(s.shape[0], c)], send_sems,
                        recv_sems, 3 * a + j, (cx, cy, c)) for a, s in enumerate(shards) for j, (cx, cy) in enumerate(chips)]

    def start(ins, outs, send_sems, recv_sems):
        for cp in sends(ins, outs, send_sems, recv_sems):
            cp.start()

    def finish(ins, outs, send_sems, recv_sems):
        x, y, c, chips = _place()
        for a, s in enumerate(shards):
            for j, (cx, cy) in enumerate(chips):
                landed = outs[a].at[2 * cx + cy, _half(s.shape[0], c)]
                _remote(landed, landed, send_sems, recv_sems, 3 * a + j, (x, y, c)).wait_recv()
        for cp in sends(ins, outs, send_sems, recv_sems):
            cp.wait_send()

    shapes = [jax.ShapeDtypeStruct((N_CHIPS,) + s.shape, s.dtype) for s in shards]
    return _Rider(shards, shapes, 3 * len(shards), start, finish)


def _gather_full_rider(shards):
    n = len(shards)

    def sends(ins, outs, send_sems, recv_sems):
        x, y, c, chips = _place()
        return [_remote(ins[a].at[_half(s.shape[0], c)], outs[a].at[2 * x + y, _half(s.shape[0], c)], send_sems,
                        recv_sems, 6 * a + j, (cx, cy, c)) for a, s in enumerate(shards) for j, (cx, cy) in enumerate(chips)]

    def start(ins, outs, send_sems, recv_sems):
        for cp in sends(ins, outs, send_sems, recv_sems):
            cp.start()

    def finish(ins, outs, send_sems, recv_sems):
        x, y, c, chips = _place()
        passed = []
        for a, s in enumerate(shards):
            for j, (cx, cy) in enumerate(chips):
                landed = outs[a].at[2 * cx + cy, _half(s.shape[0], c)]
                _remote(landed, landed, send_sems, recv_sems, 6 * a + j, (x, y, c)).wait_recv()
                passed.append(_remote(landed, landed, send_sems, recv_sems, 6 * a + 3 + j, (x, y, 1 - c)))
                passed[-1].start()
        for a, s in enumerate(shards):
            for j, (cx, cy) in enumerate(chips):
                other = outs[a].at[2 * cx + cy, _half(s.shape[0], 1 - c)]
                _remote(other, other, send_sems, recv_sems, 6 * a + 3 + j, (x, y, c)).wait_recv()
        for cp in sends(ins, outs, send_sems, recv_sems) + passed:
            cp.wait_send()

    shapes = [jax.ShapeDtypeStruct((N_CHIPS,) + s.shape, s.dtype) for s in shards]
    return _Rider(shards, shapes, 6 * n, start, finish)


def _gather_pass_rider(landed):
    def sends(ins, outs, send_sems, recv_sems):
        x, y, c, chips = _place()
        return [_remote(ins[a].at[2 * cx + cy, _half(g.shape[1], c)], outs[a].at[2 * cx + cy, _half(g.shape[1], c)],
                        send_sems, recv_sems, 3 * a + j, (x, y, 1 - c))
                for a, g in enumerate(landed) for j, (cx, cy) in enumerate(chips)]

    def start(ins, outs, send_sems, recv_sems):
        for cp in sends(ins, outs, send_sems, recv_sems):
            cp.start()

    def finish(ins, outs, send_sems, recv_sems):
        x, y, c, chips = _place()
        for a, g in enumerate(landed):
            for j, (cx, cy) in enumerate(chips):
                other = outs[a].at[2 * cx + cy, _half(g.shape[1], 1 - c)]
                _remote(other, other, send_sems, recv_sems, 3 * a + j, (x, y, c)).wait_recv()
        for cp in sends(ins, outs, send_sems, recv_sems):
            cp.wait_send()

    shapes = [jax.ShapeDtypeStruct(g.shape, g.dtype) for g in landed]
    return _Rider(landed, shapes, 3 * len(landed), start, finish, aliases={a: a for a in range(len(landed))})


def _grad_parts(g):
    return g.reshape((N_CHIPS, -1, g.shape[-1]))


def _place_scalars():
    return jnp.stack([lax.axis_index("c"), 2 * lax.axis_index("x") + lax.axis_index("y")]).astype(jnp.int32)


def _scalar_call(body, name, grid, in_specs, out_specs, out_shape, operands):
    spec = pltpu.PrefetchScalarGridSpec(num_scalar_prefetch=1, grid=grid, in_specs=in_specs, out_specs=out_specs)
    return pl.pallas_call(body, name=name, grid_spec=spec, out_shape=out_shape,
                          compiler_params=_params(*(["arbitrary"] * len(grid))))(_place_scalars(), *operands)


def _pair_sums(names, parts, sib):
    out = []
    for n, g, s in zip(names, parts, sib):
        rh, cols = s.shape[1], s.shape[2]
        tm = _row_tile(rh, ROW_TILE)
        nblk = rh // tm

        def body(place, g_ref, s_ref, o_ref):
            o_ref[...] = (g_ref[...].astype(F32) + s_ref[...].astype(F32)).astype(o_ref.dtype)

        blk = pl.BlockSpec((None, tm, cols), lambda j, i, place: (j, i, 0))
        own = pl.BlockSpec((None, tm, cols), lambda j, i, place, nblk=nblk: (j, place[0] * nblk + i, 0))
        out.append(_scalar_call(body, "grad_pair_sum_" + n, (N_CHIPS, nblk), [own, blk], blk,
                                jax.ShapeDtypeStruct(s.shape, BF16), (g, s)))
    return out


def _chip_sums(names, pair, others):
    out = []
    for n, h, o in zip(names, pair, others):
        rh, cols = h.shape[1], h.shape[2]
        tm = _row_tile(rh, ROW_TILE)

        def body(place, h_ref, a_ref, b_ref, c_ref, o_ref):
            o_ref[...] = (h_ref[...].astype(F32) + a_ref[...].astype(F32)) + b_ref[...].astype(F32) + c_ref[...].astype(F32)

        mine = pl.BlockSpec((None, tm, cols), lambda i, place: (place[1], i, 0))
        other = lambda k: pl.BlockSpec((None, tm, cols), lambda i, place, k=k: (k, i, 0))
        out.append(_scalar_call(body, "grad_chip_sum_" + n, (rh // tm,), [mine, other(0), other(1), other(2)],
                                pl.BlockSpec((tm, cols), lambda i, place: (i, 0)), jax.ShapeDtypeStruct((rh, cols), F32),
                                (h, o, o, o)))
    return out


def _adamw_halves(name, w, m, v, own, sib):
    rh, cols = own.shape
    tm = _row_tile(rh, ADAMW_TILE)
    nblk = rh // tm

    def body(place, w_ref, m_ref, v_ref, own_ref, sib_ref, g_ref, d_ref, m2_ref, v2_ref):
        mine = pl.program_id(0) // nblk == place[0]

        def run(gv):
            g_ref[...] = gv
            d_ref[...], m2_ref[...], v2_ref[...] = _adamw_math(w_ref[...], gv, m_ref[...], v_ref[...])

        @pl.when(mine)
        def _():
            run(own_ref[...])

        @pl.when(jnp.logical_not(mine))
        def _():
            run(sib_ref[...])

    full = pl.BlockSpec((tm, cols), lambda i, place: (i, 0))
    own_spec = pl.BlockSpec((tm, cols), lambda i, place: (jnp.where(i // nblk == place[0], i % nblk, 0), 0))
    sib_spec = pl.BlockSpec((tm, cols), lambda i, place: (jnp.where(i // nblk == place[0], 0, i % nblk), 0))
    return _scalar_call(body, name, (2 * nblk,), [full, full, full, own_spec, sib_spec], [full] * 4,
                        [jax.ShapeDtypeStruct((2 * rh, cols), F32)] * 4, (w, m, v, own, sib))


def _gather_all_rider(v):
    m_per = v.shape[0]

    def rows(ref, px, py, pc):
        return ref.at[pl.ds(pl.multiple_of((4 * px + 2 * py + pc) * m_per, 8), m_per)]

    def first(ins, outs, send_sems, recv_sems):
        x, y, c, chips = _place()
        mine = rows(outs[0], x, y, c)
        return [_remote(ins[0], mine, send_sems, recv_sems, 0, (x, y, 1 - c))] + [
            _remote(ins[0], mine, send_sems, recv_sems, 1 + j, (cx, cy, c)) for j, (cx, cy) in enumerate(chips)]

    def start(ins, outs, send_sems, recv_sems):
        for cp in first(ins, outs, send_sems, recv_sems):
            cp.start()

    def finish(ins, outs, send_sems, recv_sems):
        x, y, c, chips = _place()
        passed = []
        for j, (cx, cy) in enumerate(chips):
            blk = rows(outs[0], cx, cy, c)
            _remote(blk, blk, send_sems, recv_sems, 1 + j, (x, y, c)).wait_recv()
            passed.append(_remote(blk, blk, send_sems, recv_sems, 4 + j, (x, y, 1 - c)))
            passed[j].start()
        sib = rows(outs[0], x, y, 1 - c)
        _remote(sib, sib, send_sems, recv_sems, 0, (x, y, c)).wait_recv()
        for j, (cx, cy) in enumerate(chips):
            blk = rows(outs[0], cx, cy, 1 - c)
            _remote(blk, blk, send_sems, recv_sems, 4 + j, (x, y, c)).wait_recv()
        for cp in first(ins, outs, send_sems, recv_sems) + passed:
            cp.wait_send()

    return _Rider([v], [jax.ShapeDtypeStruct((N_DEV * m_per,) + v.shape[1:], v.dtype)], 7, start, finish)


def _sum_over_devices(name, v, gathered):
    m_per = v.shape[0]
    dev = 4 * lax.axis_index("x") + 2 * lax.axis_index("y") + lax.axis_index("c")
    full = lax.dynamic_update_slice_in_dim(gathered, v, dev * m_per, axis=0)
    return _sum_blocks(name, [full[i * m_per:(i + 1) * m_per] for i in range(N_DEV)], F32)


def _sum_blocks(name, parts, out_dtype):
    rows, cols = parts[0].shape
    tm = _row_tile(rows, ROW_TILE)

    def body(*refs):
        acc = refs[0][...].astype(F32)
        for r in refs[1:-1]:
            acc = acc + r[...].astype(F32)
        refs[-1][...] = acc.astype(refs[-1].dtype)

    spec = pl.BlockSpec((tm, cols), lambda i: (i, 0))
    return pl.pallas_call(
        body, name=name, grid=(rows // tm,), in_specs=[spec] * len(parts), out_specs=spec,
        out_shape=jax.ShapeDtypeStruct((rows, cols), out_dtype), compiler_params=_params("arbitrary"),
    )(*parts)


def _adamw_math(wv, gv, mv, vv):
    m2 = ADAM_B1 * mv + (1.0 - ADAM_B1) * gv
    v2 = ADAM_B2 * vv + (1.0 - ADAM_B2) * (gv * gv)
    delta = -ADAM_LR * ((m2 / (1.0 - ADAM_B1 ** ADAM_STEP)) / (jnp.sqrt(v2 / (1.0 - ADAM_B2 ** ADAM_STEP)) + ADAM_EPS)
                        + ADAM_WD * wv)
    return delta, m2, v2


def _adamw_small(ws, gs, ms, vs):
    n = len(ws)

    def body(*refs):
        for i in range(n):
            res = _adamw_math(refs[i][...], refs[n + i][...], refs[2 * n + i][...], refs[3 * n + i][...])
            for k in range(3):
                refs[(4 + k) * n + i][...] = res[k]

    vm = pl.BlockSpec(memory_space=pltpu.VMEM)
    outs = pl.pallas_call(
        body, name="adamw_small", in_specs=[vm] * (4 * n), out_specs=[vm] * (3 * n),
        out_shape=[jax.ShapeDtypeStruct(a.shape, F32) for a in ws] * 3,
        compiler_params=pltpu.CompilerParams(vmem_limit_bytes=VMEM_LIMIT_BYTES),
    )(*ws, *gs, *ms, *vs)
    return outs[:n], outs[n:2 * n], outs[2 * n:]


PACK_ROWS = 256


def _pack(flat_parts, dtype, lead=()):
    parts = [a.astype(dtype).reshape(lead + (-1,)) for a in flat_parts]
    n = sum(a.shape[-1] for a in parts)
    chunk = PACK_ROWS * LANES
    total = -(-n // chunk) * chunk
    if total > n:
        parts.append(jnp.zeros(lead + (total - n,), dtype))
    return jnp.concatenate(parts, axis=-1).reshape(lead + (total // LANES, LANES))


def _unpack(buf, shapes, lead=()):
    flat = buf.reshape(lead + (-1,))
    out, off = [], 0
    for shp in shapes:
        n = math.prod(shp)
        out.append(lax.slice_in_dim(flat, off, off + n, axis=len(lead)).reshape(lead + tuple(shp)))
        off += n
    return out


BIG = ("w_in", "w_glu", "w_pa", "w_pb", "w_out", "w_up", "w_down")
WEIGHTS = ("g_mix", "w_in", "s5_a_re", "s5_a_im", "s5_log_dt", "s5_b_re", "s5_b_im", "s5_c_re", "s5_c_im", "s5_d",
           "w_glu", "b_glu", "hg_lb_logits", "hg_norm_gain", "w_pa", "w_pb", "w_out", "g_ffn", "w_up", "w_conv",
           "b_conv", "w_down", "g_final")
SMALL = tuple(n for n in WEIGHTS if n not in BIG)
SMALL_PARTS = ("loss", "g_ffn", "g_final", "b_glu", "gain", "lbrow", "s5_d", "w_conv", "b_conv", "lam_re", "lam_im",
               "bb_re", "bb_im", "s5_c_re", "s5_c_im")


def _lower_bound(logits):
    return jnp.cumsum(jax.nn.softmax(logits, axis=0), axis=0)[0:1]


def kernel(x, g_mix, w_in, s5_a_re, s5_a_im, s5_log_dt, s5_b_re, s5_b_im, s5_c_re, s5_c_im, s5_d, w_glu, b_glu, hg_lb_logits, hg_norm_gain, w_pa, w_pb, w_out, g_ffn, w_up, w_conv, b_conv, w_down, g_final, loss_target, m_g_mix, m_w_in, m_s5_a_re, m_s5_a_im, m_s5_log_dt, m_s5_b_re, m_s5_b_im, m_s5_c_re, m_s5_c_im, m_s5_d, m_w_glu, m_b_glu, m_hg_lb_logits, m_hg_norm_gain, m_w_pa, m_w_pb, m_w_out, m_g_ffn, m_w_up, m_w_conv, m_b_conv, m_w_down, m_g_final, v_g_mix, v_w_in, v_s5_a_re, v_s5_a_im, v_s5_log_dt, v_s5_b_re, v_s5_b_im, v_s5_c_re, v_s5_c_im, v_s5_d, v_w_glu, v_b_glu, v_hg_lb_logits, v_hg_norm_gain, v_w_pa, v_w_pb, v_w_out, v_g_ffn, v_w_up, v_w_conv, v_b_conv, v_w_down, v_g_final):
    args = dict(locals())
    w = {n: args[n] for n in WEIGHTS}
    mom = {n: args["m_" + n] for n in WEIGHTS}
    var = {n: args["v_" + n] for n in WEIGHTS}
    nseq, seq, d = x.shape
    xi, yi = lax.axis_index("x"), lax.axis_index("y")
    chip = 2 * xi + yi

    shard = {n: w[n][0] for n in BIG}
    first = [shard["w_in"].astype(BF16), shard["w_glu"].astype(BF16),
             jnp.pad(w_conv[0], ((0, 2 * SUBLANES - CONV_W), (0, 0)))]
    x2 = x.reshape(nseq * seq, d)
    u, z_own, late16, got = _prepare(x2, g_mix, first[0], [shard[n] for n in LATE], _gather_full_rider(first))
    w_in_all, w_glu_all, conv_all = [lax.dynamic_update_index_in_dim(g, s, chip, 0) for g, s in zip(got, first)]
    p = dict(g_mix=g_mix, g_ffn=g_ffn, g_final=g_final.reshape(1, -1), b_glu=b_glu, gain=hg_norm_gain, s5_d=s5_d,
             b_conv=b_conv, lbrow=_lower_bound(hg_lb_logits),
             s5_a_re=s5_a_re[0], s5_a_im=s5_a_im[0], s5_log_dt=s5_log_dt[0], s5_b_re=s5_b_re[0], s5_b_im=s5_b_im[0],
             s5_c_re=s5_c_re[0], s5_c_im=s5_c_im[0], w_in=w_in_all, w_glu=w_glu_all.reshape(-1, w_glu_all.shape[-1]),
             w_conv=conv_all[:, :CONV_W].transpose(1, 0, 2).reshape(CONV_W, -1))

    dx, halves, sm = _local_step(x2, loss_target.reshape(nseq * seq, d), u, z_own, p, dict(zip(LATE, late16)),
                                 nseq=nseq, seq=seq)
    loss = sm["loss"][0, 0]

    grads, delta, new_m, new_v = {}, {}, {}, {}
    for n in BIG:
        shp = shard[n].shape
        grads[n], delta[n], new_m[n], new_v[n] = _adamw_halves("adamw_" + n, shard[n], mom[n].reshape(shp),
                                                               var[n].reshape(shp), *halves[n])

    _, disc_vjp = jax.vjp(_s5_discretize, p["s5_a_re"], p["s5_a_im"], p["s5_log_dt"], p["s5_b_re"], p["s5_b_im"])
    da_re, da_im, dlog_dt, db_re, db_im = disc_vjp((sm["lam_re"], sm["lam_im"], sm["bb_re"], sm["bb_im"]))
    _, lb_vjp = jax.vjp(_lower_bound, hg_lb_logits)
    (dlogits,) = lb_vjp(sm["lbrow"])
    fcols = w_conv.shape[-1]
    grads.update(
        g_mix=sm["g_mix"], g_ffn=sm["g_ffn"], g_final=sm["g_final"].reshape(-1), b_glu=sm["b_glu"],
        hg_norm_gain=sm["gain"], hg_lb_logits=dlogits, s5_d=sm["s5_d"], b_conv=sm["b_conv"],
        w_conv=lax.dynamic_slice_in_dim(sm["w_conv"], chip * fcols, fcols, axis=1),
        s5_a_re=da_re, s5_a_im=da_im, s5_log_dt=dlog_dt, s5_b_re=db_re, s5_b_im=db_im,
        s5_c_re=sm["s5_c_re"], s5_c_im=sm["s5_c_im"])
    grads = {n: grads[n].reshape(w[n].shape) for n in WEIGHTS}

    def natural(a):
        return a.reshape(1, -1) if a.ndim == 1 else (a[0] if a.ndim > 2 else a)

    outs = _adamw_small(*[[natural(src[n]) for n in SMALL] for src in (w, grads, mom, var)])
    for dst, group in zip((delta, new_m, new_v), outs):
        dst.update(zip(SMALL, group))
    res = [loss, dx.reshape(x.shape)]
    for group in (grads, delta, new_m, new_v):
        res += [group[n].reshape(w[n].shape) for n in WEIGHTS]
    return tuple(res)
```

```python
import functools
import math

import jax
import jax.numpy as jnp
from jax import lax
from jax.experimental import pallas as pl
from jax.experimental.pallas import tpu as pltpu

F32 = jnp.float32
BF16 = jnp.bfloat16
MESH = pl.DeviceIdType.MESH

EPS = 1e-6
S5_GROUP = 16
S5_STATE = 64
S5_BLOCK_GROUPS = 8
HEAD = 128
CHUNK = 64
CONV_W = 3
LANES = 128
SUBLANES = 8
GATE_BLOCK = 512
VMEM_LIMIT_BYTES = 56 * 1024 * 1024

ADAM_LR = 0.001
ADAM_B1 = 0.9
ADAM_B2 = 0.999
ADAM_EPS = 1e-08
ADAM_WD = 0.01
ADAM_STEP = 10

N_CHIPS = 4
N_DEV = 8


def _params(*sem):
    return pltpu.CompilerParams(dimension_semantics=sem, vmem_limit_bytes=VMEM_LIMIT_BYTES)


class _Rider:
    def __init__(self, arrays, out_shapes, nsem, start, finish, aliases=None):
        self.arrays, self.out_shapes, self.nsem = list(arrays), list(out_shapes), nsem
        self.start, self.finish, self.aliases = start, finish, dict(aliases or {})


def _hosted_call(name, body, *, grid, in_specs, out_specs, out_shape, operands, scratch_shapes=(), rider=None):
    in_specs, out_specs, out_shape, scratch_shapes = list(in_specs), list(out_specs), list(out_shape), list(scratch_shapes)
    cparams = _params(*(["arbitrary"] * len(grid)))
    if rider is None:
        return pl.pallas_call(body, name=name, grid=grid, in_specs=in_specs, out_specs=out_specs, out_shape=out_shape,
                              scratch_shapes=scratch_shapes, compiler_params=cparams)(*operands)
    n_in, n_out, n_sc = len(in_specs), len(out_specs), len(scratch_shapes)
    r_in, r_out = len(rider.arrays), len(rider.out_shapes)

    def hosted(*refs):
        ins, rins = refs[:n_in], refs[n_in:n_in + r_in]
        outs = refs[n_in + r_in:n_in + r_in + n_out]
        routs = refs[n_in + r_in + n_out:n_in + r_in + n_out + r_out]
        rest = refs[n_in + r_in + n_out + r_out:]
        send_sems, recv_sems = rest[n_sc], rest[n_sc + 1]
        first = functools.reduce(jnp.logical_and, [pl.program_id(i) == 0 for i in range(len(grid))])
        last = functools.reduce(jnp.logical_and, [pl.program_id(i) == grid[i] - 1 for i in range(len(grid))])

        @pl.when(first)
        def _():
            rider.start(rins, routs, send_sems, recv_sems)

        body(*ins, *outs, *rest[:n_sc])

        @pl.when(last)
        def _():
            rider.finish(rins, routs, send_sems, recv_sems)

    res = pl.pallas_call(
        hosted, name=name, grid=grid, in_specs=in_specs + [ANY] * r_in, out_specs=out_specs + [ANY] * r_out,
        out_shape=out_shape + rider.out_shapes,
        scratch_shapes=scratch_shapes + [pltpu.SemaphoreType.DMA((rider.nsem,)), pltpu.SemaphoreType.DMA((rider.nsem,))],
        input_output_aliases={n_in + i: n_out + o for i, o in rider.aliases.items()}, compiler_params=cparams,
    )(*operands, *rider.arrays)
    return res[:n_out], res[n_out:]


def _hosted_scalar_call(name, body, *, grid, in_specs, out_specs, out_shape, operands, rider, aliases=None):
    in_specs, out_specs, out_shape = list(in_specs), list(out_specs), list(out_shape)
    n_in, n_out = len(in_specs), len(out_specs)
    if rider is None:
        spec = pltpu.PrefetchScalarGridSpec(num_scalar_prefetch=1, grid=grid, in_specs=in_specs, out_specs=out_specs)
        res = pl.pallas_call(body, name=name, grid_spec=spec, out_shape=out_shape,
                             input_output_aliases={1 + i: o for i, o in (aliases or {}).items()},
                             compiler_params=_params(*(["arbitrary"] * len(grid))))(_place_scalars(), *operands)
        return res, []
    r_in, r_out = len(rider.arrays), len(rider.out_shapes)

    def hosted(place, *refs):
        ins, rins = refs[:n_in], refs[n_in:n_in + r_in]
        outs = refs[n_in + r_in:n_in + r_in + n_out]
        routs = refs[n_in + r_in + n_out:n_in + r_in + n_out + r_out]
        send_sems, recv_sems = refs[-2], refs[-1]
        first = functools.reduce(jnp.logical_and, [pl.program_id(i) == 0 for i in range(len(grid))])
        last = functools.reduce(jnp.logical_and, [pl.program_id(i) == grid[i] - 1 for i in range(len(grid))])

        @pl.when(first)
        def _():
            rider.start(rins, routs, send_sems, recv_sems)

        body(place, *ins, *outs)

        @pl.when(last)
        def _():
            rider.finish(rins, routs, send_sems, recv_sems)

    spec = pltpu.PrefetchScalarGridSpec(
        num_scalar_prefetch=1, grid=grid, in_specs=in_specs + [ANY] * r_in, out_specs=out_specs + [ANY] * r_out,
        scratch_shapes=[pltpu.SemaphoreType.DMA((rider.nsem,)), pltpu.SemaphoreType.DMA((rider.nsem,))])
    alias = {1 + i: o for i, o in (aliases or {}).items()}
    alias.update({1 + n_in + i: n_out + o for i, o in rider.aliases.items()})
    res = pl.pallas_call(hosted, name=name, grid_spec=spec, out_shape=out_shape + rider.out_shapes,
                         input_output_aliases=alias, compiler_params=_params(*(["arbitrary"] * len(grid))),
                         )(_place_scalars(), *operands, *rider.arrays)
    return res[:n_out], res[n_out:]


def _run_rider(name, rider):
    r_in, r_out = len(rider.arrays), len(rider.out_shapes)

    def body(*refs):
        rins, routs, send_sems, recv_sems = refs[:r_in], refs[r_in:r_in + r_out], refs[-2], refs[-1]
        rider.start(rins, routs, send_sems, recv_sems)
        rider.finish(rins, routs, send_sems, recv_sems)

    return pl.pallas_call(
        body, name=name, in_specs=[ANY] * r_in, out_specs=[ANY] * r_out, out_shape=rider.out_shapes,
        scratch_shapes=[pltpu.SemaphoreType.DMA((rider.nsem,)), pltpu.SemaphoreType.DMA((rider.nsem,))],
        input_output_aliases=rider.aliases,
    )(*rider.arrays)


def _row_tile(rows, cap):
    if rows <= cap:
        return rows
    for t in range(cap - cap % 8, 7, -8):
        if rows % t == 0:
            return t
    raise ValueError(f"no row tile for {rows}")


def _dot(a, b):
    return jnp.dot(a.astype(BF16), b.astype(BF16), preferred_element_type=F32)


def _dot_nt(a, b):
    return lax.dot_general(a.astype(BF16), b.astype(BF16), (((1,), (1,)), ((), ())), preferred_element_type=F32)


def _dot_tn(a, b):
    return lax.dot_general(a.astype(BF16), b.astype(BF16), (((0,), (0,)), ((), ())), preferred_element_type=F32)


def _sigmoid(x):
    return 0.5 * jnp.tanh(0.5 * x) + 0.5


_GELU_C = math.sqrt(2.0 / math.pi)


def _gelu(x):
    return 0.5 * x * (1.0 + jnp.tanh(_GELU_C * (x + 0.044715 * x * x * x)))


def _gelu_grad(x):
    th = jnp.tanh(_GELU_C * (x + 0.044715 * x * x * x))
    return 0.5 * (1.0 + th) + 0.5 * x * (1.0 - th * th) * _GELU_C * (1.0 + 3.0 * 0.044715 * x * x)


def _rowwise(name, fn, ins, outs, accs=(), *, rows, tm, ncol=1, rider=None):
    n_in, n_out = len(ins), len(outs)

    def body(*refs):
        res = fn(*[r[...] for r in refs[:n_in]])
        for r, v in zip(refs[n_in:n_in + n_out], res[:n_out]):
            r[...] = v.astype(r.dtype)
        first = pl.program_id(1) == 0
        for r, v in zip(refs[n_in + n_out:], res[n_out:]):
            @pl.when(first)
            def _():
                r[...] = v

            @pl.when(jnp.logical_not(first))
            def _():
                r[...] += v

    in_specs = []
    for _, width, base, kind in ins:
        if kind == "row":
            in_specs.append(pl.BlockSpec((tm, width), lambda j, i, b=base: (i, b + j)))
        else:
            in_specs.append(pl.BlockSpec((1, width), lambda j, i, b=base: (0, b + j)))
    out_specs = [pl.BlockSpec((tm, width), lambda j, i: (i, j)) for _, width, _ in outs]
    out_specs += [pl.BlockSpec((1, width), lambda j, i: (0, j)) for _, width in accs]
    out_shape = [jax.ShapeDtypeStruct((rows, total), dt) for total, _, dt in outs]
    out_shape += [jax.ShapeDtypeStruct((1, total), F32) for total, _ in accs]
    return _hosted_call(name, body, grid=(ncol, rows // tm), in_specs=in_specs, out_specs=out_specs, out_shape=out_shape,
                        operands=[a for a, _, _, _ in ins], rider=rider)


def _mm(name, a, b, *, mode, grid, a_spec, b_spec, o_spec, out_shape, acc_shape, res=None, res_spec=None,
        pair_axis=None, rider=None, epilogue=None):
    nk = grid[2]
    dot = {"nn": _dot, "nt": _dot_nt, "tn": _dot_tn}[mode]
    a_list = list(a) if isinstance(a, tuple) else [a]
    b_list = list(b) if isinstance(b, tuple) else [b]
    na, nb = len(a_list), len(b_list)
    assert (pair_axis is None) == (na + nb == 2)
    direct = nk == 1 and pair_axis is None
    epi_fn, epi_ins, epi_sums = epilogue if epilogue is not None else (None, [], [])
    n_res = 0 if res is None else 1
    n_epi = len(epi_ins)

    def body(*refs):
        a_refs, b_refs = refs[:na], refs[na:na + nb]
        r_ref = None if res is None else refs[na + nb]
        e_refs = refs[na + nb + n_res:na + nb + n_res + n_epi]
        o_ref = refs[na + nb + n_res + n_epi]
        s_refs = refs[na + nb + n_res + n_epi + 1:na + nb + n_res + n_epi + 1 + len(epi_sums)]
        first_rows = pl.program_id(0) == 0

        def finish(v):
            if res is not None:
                v = v + r_ref[...]
            if epi_fn is None:
                o_ref[...] = v.astype(o_ref.dtype)
                return
            outs = epi_fn(v, *[r[...] for r in e_refs])
            o_ref[...] = outs[0].astype(o_ref.dtype)
            for s_ref, part in zip(s_refs, outs[1:]):
                @pl.when(first_rows)
                def _():
                    s_ref[...] = part

                @pl.when(jnp.logical_not(first_rows))
                def _():
                    s_ref[...] += part

        if direct:
            finish(dot(a_refs[0][...], b_refs[0][...]))
            return
        acc_ref = refs[-1]
        k = pl.program_id(2)

        @pl.when(k == 0)
        def _():
            acc_ref[...] = jnp.zeros_like(acc_ref)

        if pair_axis is None:
            acc_ref[...] += dot(a_refs[0][...], b_refs[0][...])
        else:
            lower = pl.program_id(pair_axis) < grid[pair_axis] // 2

            @pl.when(lower)
            def _():
                acc_ref[...] += dot(a_refs[0][...], b_refs[0][...])

            @pl.when(jnp.logical_not(lower))
            def _():
                acc_ref[...] += dot(a_refs[-1][...], b_refs[-1][...])

        @pl.when(k == nk - 1)
        def _():
            finish(acc_ref[...])

    operands = a_list + b_list + ([] if res is None else [res]) + [arr for arr, _ in epi_ins]
    in_specs = (list(a_spec) if na == 2 else [a_spec]) + (list(b_spec) if nb == 2 else [b_spec])
    in_specs += ([] if res is None else [res_spec]) + [spec for _, spec in epi_ins]
    out_specs = [o_spec] + [pl.BlockSpec((1, c), lambda *_: (0, 0)) for c in epi_sums]
    out_shapes = [out_shape] + [jax.ShapeDtypeStruct((1, c), F32) for c in epi_sums]
    got = _hosted_call(name, body, grid=grid, in_specs=in_specs, out_specs=out_specs, out_shape=out_shapes,
                       scratch_shapes=[] if direct else [pltpu.VMEM(acc_shape, F32)], operands=operands, rider=rider)
    mine, rider_outs = (got, None) if rider is None else got
    mine = mine[0] if epilogue is None else tuple(mine)
    return mine if rider is None else (mine, rider_outs)


MM_TILE_BUDGET_BYTES = 36 * 1024 * 1024
MM_TILE_CAP = 2048
ROW_TILE = 1024
GLU_TILE = 1024
ADAMW_TILE = 256


def _mm_tile(t, row_bytes, fixed_bytes):
    cap = max(16, min(MM_TILE_CAP, (MM_TILE_BUDGET_BYTES - fixed_bytes) // row_bytes))
    return _row_tile(t, cap - cap % 16)


def _size(a):
    return jnp.dtype(a.dtype).itemsize


def _mm_fwd_cols(name, a, w3, out_dtype=F32, rider=None):
    t, k = a.shape
    ns = w3.shape[2]
    tm = _mm_tile(t, 2 * k * _size(a) + 2 * ns * jnp.dtype(out_dtype).itemsize, 2 * k * ns * _size(w3))
    return _mm(name, a, w3, mode="nn", grid=(N_CHIPS, t // tm, 1),
               a_spec=pl.BlockSpec((tm, k), lambda j, i, kk: (i, 0)),
               b_spec=pl.BlockSpec((None, k, ns), lambda j, i, kk: (j, 0, 0)),
               o_spec=pl.BlockSpec((tm, ns), lambda j, i, kk: (i, j)),
               out_shape=jax.ShapeDtypeStruct((t, N_CHIPS * ns), out_dtype), acc_shape=(tm, ns), rider=rider)


def _mm_bwd_cols(name, d, w3, out_dtype=F32, rider=None, epilogue=None):
    pair = isinstance(d, tuple)
    t = d[0].shape[0] if pair else d.shape[0]
    k, ns = w3.shape[1], w3.shape[2]
    dsize = _size(d[0] if pair else d)
    tm = _mm_tile(t, (4 if pair else 2) * ns * dsize + 2 * k * jnp.dtype(out_dtype).itemsize + 4 * k
                  + _row_epilogue(epilogue, 8)[1], 2 * k * ns * _size(w3))
    half = N_CHIPS // 2
    if pair:
        a_spec = (pl.BlockSpec((tm, ns), lambda i, j, kk: (i, jnp.minimum(kk, half - 1))),
                  pl.BlockSpec((tm, ns), lambda i, j, kk: (i, jnp.maximum(kk - half, 0))))
    else:
        a_spec = pl.BlockSpec((tm, ns), lambda i, j, kk: (i, kk))
    return _mm(name, d, w3, mode="nt", grid=(t // tm, 1, N_CHIPS), a_spec=a_spec,
               b_spec=pl.BlockSpec((None, k, ns), lambda i, j, kk: (kk, 0, 0)),
               o_spec=pl.BlockSpec((tm, k), lambda i, j, kk: (i, 0)),
               out_shape=jax.ShapeDtypeStruct((t, k), out_dtype), acc_shape=(tm, k), pair_axis=2 if pair else None,
               rider=rider, epilogue=_row_epilogue(epilogue, tm)[0])


def _mm_wgrad_cols(name, a, d, rider=None):
    pair = isinstance(d, tuple)
    t, k = a.shape
    ns = (2 * d[0].shape[1] if pair else d.shape[1]) // N_CHIPS
    dsize = _size(d[0] if pair else d)
    tk = _mm_tile(t, 2 * k * _size(a) + (4 if pair else 2) * ns * dsize, k * ns * (4 + 2 * 2))
    half = N_CHIPS // 2
    if pair:
        b_spec = (pl.BlockSpec((tk, ns), lambda j, i, kk: (jnp.where(j < half, kk, 0), jnp.minimum(j, half - 1))),
                  pl.BlockSpec((tk, ns), lambda j, i, kk: (jnp.where(j < half, 0, kk), jnp.maximum(j - half, 0))))
    else:
        b_spec = pl.BlockSpec((tk, ns), lambda j, i, kk: (kk, j))
    return _mm(name, a, d, mode="tn", grid=(N_CHIPS, 1, t // tk),
               a_spec=pl.BlockSpec((tk, k), lambda j, i, kk: (kk, 0)), b_spec=b_spec,
               o_spec=pl.BlockSpec((None, k, ns), lambda j, i, kk: (j, 0, 0)),
               out_shape=jax.ShapeDtypeStruct((N_CHIPS, k, ns), BF16), acc_shape=(k, ns),
               pair_axis=0 if pair else None, rider=rider)


MM_BLOCK_CAP = 1408


def _row_epilogue(epilogue, tm):
    if epilogue is None:
        return None, 0
    fn, arrays, sums = epilogue
    specs = [pl.BlockSpec((1, x.shape[1]), lambda i, j, kk: (0, 0)) if x.shape[0] == 1 else
             pl.BlockSpec((tm, x.shape[1]), lambda i, j, kk: (i, 0)) for x in arrays]
    return (fn, list(zip(arrays, specs)), list(sums)), sum(2 * x.shape[1] * _size(x) for x in arrays if x.shape[0] > 1)


def _mm_fwd_rows(name, a, w, res=None, out_dtype=F32, epilogue=None, rider=None):
    t, k = a.shape
    n = w.shape[1]
    tk = k if k <= MM_BLOCK_CAP else MM_BLOCK_CAP
    assert k % tk == 0
    row_bytes = 2 * tk * _size(a) + 2 * n * jnp.dtype(out_dtype).itemsize + (0 if res is None else 2 * n * 4) + 4 * n
    row_bytes += _row_epilogue(epilogue, 8)[1]
    tm = _mm_tile(t, row_bytes, 2 * tk * n * _size(w))
    return _mm(name, a, w, mode="nn", grid=(t // tm, 1, k // tk),
               a_spec=pl.BlockSpec((tm, tk), lambda i, j, kk: (i, kk)),
               b_spec=pl.BlockSpec((tk, n), lambda i, j, kk: (kk, 0)),
               o_spec=pl.BlockSpec((tm, n), lambda i, j, kk: (i, 0)),
               out_shape=jax.ShapeDtypeStruct((t, n), out_dtype), acc_shape=(tm, n),
               res=res, res_spec=None if res is None else pl.BlockSpec((tm, n), lambda i, j, kk: (i, 0)),
               epilogue=_row_epilogue(epilogue, tm)[0], rider=rider)


def _mm_bwd_rows(name, d, w, out_dtype=F32):
    t, n = d.shape
    k = w.shape[0]
    tn = k if k <= MM_BLOCK_CAP else MM_BLOCK_CAP
    assert k % tn == 0
    tm = _mm_tile(t, 2 * n * _size(d) + 2 * tn * jnp.dtype(out_dtype).itemsize, 2 * tn * n * _size(w))
    return _mm(name, d, w, mode="nt", grid=(t // tm, k // tn, 1),
               a_spec=pl.BlockSpec((tm, n), lambda i, j, kk: (i, 0)),
               b_spec=pl.BlockSpec((tn, n), lambda i, j, kk: (j, 0)),
               o_spec=pl.BlockSpec((tm, tn), lambda i, j, kk: (i, j)),
               out_shape=jax.ShapeDtypeStruct((t, k), out_dtype), acc_shape=(tm, tn))


def _mm_wgrad_rows(name, a, d):
    t, k = a.shape
    n = d.shape[1]
    nblk = next(b for b in (1, 2, 4) if (k // b) % LANES == 0 and k // b <= MM_BLOCK_CAP)
    ks = k // nblk
    tk = _mm_tile(t, 2 * ks * _size(a) + 2 * n * _size(d), ks * n * (4 + 2 * 2))
    return _mm(name, a, d, mode="tn", grid=(nblk, 1, t // tk),
               a_spec=pl.BlockSpec((tk, ks), lambda j, i, kk: (kk, j)),
               b_spec=pl.BlockSpec((tk, n), lambda j, i, kk: (kk, 0)),
               o_spec=pl.BlockSpec((ks, n), lambda j, i, kk: (j, 0)),
               out_shape=jax.ShapeDtypeStruct((k, n), BF16), acc_shape=(ks, n))


def _s5_discretize(a_re, a_im, log_dt, b_re, b_im):
    dt = jnp.exp(log_dt)[:, None]
    mag = jnp.exp(a_re * dt)
    ang = a_im * dt
    lb_re = mag * jnp.cos(ang)
    lb_im = mag * jnp.sin(ang)
    den = a_re * a_re + a_im * a_im
    n_re = lb_re - 1.0
    n_im = lb_im
    co_re = ((n_re * a_re + n_im * a_im) / den)[..., None]
    co_im = ((n_im * a_re - n_re * a_im) / den)[..., None]
    bb_re = co_re * b_re - co_im * b_im
    bb_im = co_re * b_im + co_im * b_re
    return lb_re, lb_im, bb_re, bb_im


def _s5_in_blocks(bb):
    g = bb.shape[0]
    nb = g // S5_BLOCK_GROUPS
    t = bb.reshape(nb, S5_BLOCK_GROUPS, S5_STATE, S5_GROUP).transpose(0, 1, 3, 2)
    eye = jnp.eye(S5_BLOCK_GROUPS, dtype=bb.dtype)
    full = t[:, :, :, None, :] * eye[None, :, None, :, None]
    return full.reshape(nb, S5_BLOCK_GROUPS * S5_GROUP, S5_BLOCK_GROUPS * S5_STATE)


def _s5_in_blocks_diag(blocks):
    nb = blocks.shape[0]
    t = blocks.reshape(nb, S5_BLOCK_GROUPS, S5_GROUP, S5_BLOCK_GROUPS, S5_STATE)
    d = jnp.einsum("bghgp->bghp", t)
    return d.transpose(0, 1, 3, 2).reshape(nb * S5_BLOCK_GROUPS, S5_STATE, S5_GROUP)


def _s5_out_blocks(c):
    g = c.shape[0]
    nb = g // S5_BLOCK_GROUPS
    t = c.reshape(nb, S5_BLOCK_GROUPS, S5_GROUP, S5_STATE).transpose(0, 1, 3, 2)
    eye = jnp.eye(S5_BLOCK_GROUPS, dtype=c.dtype)
    full = t[:, :, :, None, :] * eye[None, :, None, :, None]
    return full.reshape(nb, S5_BLOCK_GROUPS * S5_STATE, S5_BLOCK_GROUPS * S5_GROUP)


def _s5_out_blocks_diag(blocks):
    nb = blocks.shape[0]
    t = blocks.reshape(nb, S5_BLOCK_GROUPS, S5_STATE, S5_BLOCK_GROUPS, S5_GROUP)
    d = jnp.einsum("bgpgh->bgph", t)
    return d.transpose(0, 1, 3, 2).reshape(nb * S5_BLOCK_GROUPS, S5_GROUP, S5_STATE)


def _s5_scan_tables(lr, li, reverse):
    def cmul(a, b):
        return a[0] * b[0] - a[1] * b[1], a[0] * b[1] + a[1] * b[0]

    lam = (lr, -li) if reverse else (lr, li)
    pw = [lam]
    for _ in range(SUBLANES - 1):
        pw.append(cmul(pw[-1], lam))
    sub = jnp.arange(SUBLANES)[:, None]
    rows = []
    for s in (1, 2, 4):
        keep = (sub <= SUBLANES - 1 - s) if reverse else (sub >= s)
        rows.append(jnp.where(keep, pw[s - 1][0][None, :], 0.0))
        rows.append(jnp.where(keep, pw[s - 1][1][None, :], 0.0))
    order = list(range(SUBLANES - 1, -1, -1)) if reverse else list(range(SUBLANES))
    rows.append(jnp.stack([pw[i][0] for i in order]))
    rows.append(jnp.stack([pw[i][1] for i in order]))
    return jnp.concatenate(rows, axis=0)


def _s5_scan(vre_ref, vim_ref, coef_ref, seq, width, reverse, xre_ref=None, xim_ref=None):
    nt = seq // SUBLANES
    nl = width // LANES
    per = 2 if xre_ref is None else 4
    sub = lax.broadcasted_iota(jnp.int32, (SUBLANES, LANES), 0)

    def step(k, carry):
        kk = (nt - 1 - k) if reverse else k
        rows = pl.ds(pl.multiple_of(kk * SUBLANES, SUBLANES), SUBLANES)
        out = []
        for j in range(nl):
            lanes = slice(j * LANES, (j + 1) * LANES)
            co = [coef_ref[SUBLANES * q:SUBLANES * (q + 1), lanes] for q in range(8)]
            cr, ci = carry[per * j], carry[per * j + 1]
            vr = vre_ref[rows, lanes]
            vi = vim_ref[rows, lanes]
            for q, s in enumerate((1, 2, 4)):
                sh = SUBLANES - s if reverse else s
                rr = pltpu.roll(vr, sh, 0)
                ri = pltpu.roll(vi, sh, 0)
                ar, ai = co[2 * q], co[2 * q + 1]
                vr, vi = vr + ar * rr - ai * ri, vi + ar * ri + ai * rr
            edge = 0 if reverse else SUBLANES - 1
            cbr = jnp.broadcast_to(cr[edge:edge + 1, :], (SUBLANES, LANES))
            cbi = jnp.broadcast_to(ci[edge:edge + 1, :], (SUBLANES, LANES))
            pr, pi = co[6], co[7]
            vr, vi = vr + pr * cbr - pi * cbi, vi + pr * cbi + pi * cbr
            vre_ref[rows, lanes] = vr
            vim_ref[rows, lanes] = vi
            out += [vr, vi]
            if xre_ref is not None:
                nr = jnp.where(sub == SUBLANES - 1, cbr, pltpu.roll(vr, SUBLANES - 1, 0))
                ni = jnp.where(sub == SUBLANES - 1, cbi, pltpu.roll(vi, SUBLANES - 1, 0))
                xr = xre_ref[rows, lanes]
                xi = xim_ref[rows, lanes]
                out += [carry[per * j + 2] + nr * xr + ni * xi, carry[per * j + 3] + ni * xr - nr * xi]
        return tuple(out)

    zero = jnp.zeros((SUBLANES, LANES), F32)
    res = lax.fori_loop(0, nt, step, (zero,) * (per * nl))
    if xre_ref is None:
        return None
    return jnp.concatenate(
        [jnp.concatenate([jnp.sum(res[per * j + 2], axis=0, keepdims=True) for j in range(nl)], axis=1),
         jnp.concatenate([jnp.sum(res[per * j + 3], axis=0, keepdims=True) for j in range(nl)], axis=1)], axis=0)


def _s5_fwd(z, bre3, bim3, cre3, cim3, coef, dskip, *, nseq, seq, rider=None):
    nb = bre3.shape[0]
    ch, ns = bre3.shape[1], bre3.shape[2]

    def body(za_ref, bre_ref, bim_ref, cre_ref, cim_ref, coef_ref, d_ref, y_ref, xre_ref, xim_ref):
        za = za_ref[...]
        xre_ref[...] = _dot(za, bre_ref[...])
        xim_ref[...] = _dot(za, bim_ref[...])
        _s5_scan(xre_ref, xim_ref, coef_ref, seq, ns, False)
        y_ref[...] = _dot(xre_ref[...], cre_ref[...]) - _dot(xim_ref[...], cim_ref[...]) + d_ref[...] * za

    blk3 = lambda r, c: pl.BlockSpec((None, r, c), lambda b, j: (j, 0, 0))
    return _hosted_call(
        "s5_fwd", body, grid=(nseq, nb),
        in_specs=[pl.BlockSpec((seq, ch), lambda b, j: (b, j)), blk3(ch, ns), blk3(ch, ns), blk3(ns, ch), blk3(ns, ch),
                  pl.BlockSpec((8 * SUBLANES, ns), lambda b, j: (0, j)), pl.BlockSpec((1, ch), lambda b, j: (0, j))],
        out_specs=[pl.BlockSpec((seq, ch), lambda b, j: (b, j)), pl.BlockSpec((seq, ns), lambda b, j: (b, j)),
                   pl.BlockSpec((seq, ns), lambda b, j: (b, j))],
        out_shape=[jax.ShapeDtypeStruct((nseq * seq, nb * ch), F32), jax.ShapeDtypeStruct((nseq * seq, nb * ns), F32),
                   jax.ShapeDtypeStruct((nseq * seq, nb * ns), F32)],
        operands=(z, bre3, bim3, cre3, cim3, coef, dskip), rider=rider)


def _s5_bwd(dy, z, xre, xim, bre3, bim3, cre3, cim3, coef_rev, dskip, *, nseq, seq, rider=None):
    nb = bre3.shape[0]
    ch, ns = bre3.shape[1], bre3.shape[2]

    def body(dy_ref, za_ref, xre_ref, xim_ref, bre_ref, bim_ref, cre_ref, cim_ref, coef_ref, d_ref,
             dza_ref, dbre_ref, dbim_ref, dcre_ref, dcim_ref, dlam_ref, dd_ref, are_ref, aim_ref):
        dy = dy_ref[...]
        za = za_ref[...]
        are_ref[...] = _dot_nt(dy, cre_ref[...])
        aim_ref[...] = -_dot_nt(dy, cim_ref[...])
        dlam = _s5_scan(are_ref, aim_ref, coef_ref, seq, ns, True, xre_ref, xim_ref)
        are = are_ref[...]
        aim = aim_ref[...]
        dza_ref[...] = (_dot_nt(are, bre_ref[...]) + _dot_nt(aim, bim_ref[...]) + d_ref[...] * dy).astype(dza_ref.dtype)
        parts = (_dot_tn(za, are), _dot_tn(za, aim), _dot_tn(xre_ref[...], dy), -_dot_tn(xim_ref[...], dy),
                 dlam, jnp.sum(dy * za, axis=0, keepdims=True))
        first = pl.program_id(1) == 0
        for r, v in zip((dbre_ref, dbim_ref, dcre_ref, dcim_ref, dlam_ref, dd_ref), parts):
            @pl.when(first)
            def _():
                r[...] = v

            @pl.when(jnp.logical_not(first))
            def _():
                r[...] += v

    blk3 = lambda r, c: pl.BlockSpec((None, r, c), lambda j, b: (j, 0, 0))
    tok = lambda c: pl.BlockSpec((seq, c), lambda j, b: (b, j))
    return _hosted_call(
        "s5_bwd", body, grid=(nb, nseq),
        in_specs=[tok(ch), tok(ch), tok(ns), tok(ns), blk3(ch, ns), blk3(ch, ns), blk3(ns, ch), blk3(ns, ch),
                  pl.BlockSpec((8 * SUBLANES, ns), lambda j, b: (0, j)), pl.BlockSpec((1, ch), lambda j, b: (0, j))],
        out_specs=[tok(ch), blk3(ch, ns), blk3(ch, ns), blk3(ns, ch), blk3(ns, ch),
                   pl.BlockSpec((None, 2, ns), lambda j, b: (j, 0, 0)), pl.BlockSpec((1, ch), lambda j, b: (0, j))],
        out_shape=[jax.ShapeDtypeStruct((nseq * seq, nb * ch), BF16),
                   jax.ShapeDtypeStruct((nb, ch, ns), F32), jax.ShapeDtypeStruct((nb, ch, ns), F32),
                   jax.ShapeDtypeStruct((nb, ns, ch), F32), jax.ShapeDtypeStruct((nb, ns, ch), F32),
                   jax.ShapeDtypeStruct((nb, 2, ns), F32), jax.ShapeDtypeStruct((1, nb * ch), F32)],
        scratch_shapes=[pltpu.VMEM((seq, ns), F32), pltpu.VMEM((seq, ns), F32)],
        operands=(dy, z, xre, xim, bre3, bim3, cre3, cim3, coef_rev, dskip), rider=rider)


def _glu_fwd(y, wglu, bglu):
    t, w = y.shape
    tm = _row_tile(t, GLU_TILE)

    def body(y_ref, w_ref, b_ref, a0_ref, gl_ref, a_ref):
        a0 = _gelu(y_ref[...])
        gl = _dot(a0, w_ref[...])
        a0_ref[...] = a0.astype(a0_ref.dtype)
        gl_ref[...] = gl
        a_ref[...] = (a0 * _sigmoid(gl + b_ref[...])).astype(a_ref.dtype)

    tok = pl.BlockSpec((tm, w), lambda i: (i, 0))
    return pl.pallas_call(
        body, name="s5_glu", grid=(t // tm,),
        in_specs=[tok, pl.BlockSpec((w, w), lambda i: (0, 0)), pl.BlockSpec((1, w), lambda i: (0, 0))],
        out_specs=[tok, tok, tok],
        out_shape=[jax.ShapeDtypeStruct((t, w), BF16), jax.ShapeDtypeStruct((t, w), F32), jax.ShapeDtypeStruct((t, w), BF16)],
        compiler_params=_params("arbitrary"),
    )(y, wglu, bglu)


def _glu_bwd(y, gl, bglu, da, wglu):
    t, w = y.shape
    tm = _row_tile(t, GLU_TILE)

    def body(y_ref, gl_ref, b_ref, da_ref, w_ref, dgl_ref, dy_ref, db_ref):
        yv = y_ref[...]
        dav = da_ref[...]
        s = _sigmoid(gl_ref[...] + b_ref[...])
        dgl = dav * _gelu(yv) * s * (1.0 - s)
        dgl_ref[...] = dgl.astype(dgl_ref.dtype)
        dy_ref[...] = (dav * s + _dot_nt(dgl, w_ref[...])) * _gelu_grad(yv)
        part = jnp.sum(dgl, axis=0, keepdims=True)
        first = pl.program_id(0) == 0

        @pl.when(first)
        def _():
            db_ref[...] = part

        @pl.when(jnp.logical_not(first))
        def _():
            db_ref[...] += part

    tok = pl.BlockSpec((tm, w), lambda i: (i, 0))
    vec = pl.BlockSpec((1, w), lambda i: (0, 0))
    return pl.pallas_call(
        body, name="s5_glu_bwd", grid=(t // tm,),
        in_specs=[tok, tok, vec, tok, pl.BlockSpec((w, w), lambda i: (0, 0))], out_specs=[tok, tok, vec],
        out_shape=[jax.ShapeDtypeStruct((t, w), BF16), jax.ShapeDtypeStruct((t, w), F32), jax.ShapeDtypeStruct((1, w), F32)],
        compiler_params=_params("arbitrary"),
    )(y, gl, bglu, da, wglu)


def _cumsum_rows(x, reverse=False):
    n = x.shape[0]
    row = lax.broadcasted_iota(jnp.int32, x.shape, 0)
    s = 1
    while s < n:
        if reverse:
            x = x + jnp.where(row < n - s, pltpu.roll(x, n - s, 0), 0.0)
        else:
            x = x + jnp.where(row >= s, pltpu.roll(x, s, 0), 0.0)
        s *= 2
    return x


def _hg_gates(zq, zf, lb):
    sg = _sigmoid(zf)
    f = lb + (1.0 - lb) * sg
    sq = _sigmoid(zq)
    qa = zq * sq * (HEAD ** -0.5)
    b = _cumsum_rows(jnp.log(f))
    return sg, f, sq, qa, 1.0 - f, b


SUB = 16


def _hg_scores(qa, kk, b):
    c = qa.shape[0]
    row = lax.broadcasted_iota(jnp.int32, qa.shape, 0)
    pos = jnp.bitwise_and(row, SUB - 1)
    dmat = lax.broadcasted_iota(jnp.int32, (c, c), 0) - lax.broadcasted_iota(jnp.int32, (c, c), 1)
    p = jnp.zeros((c, c), F32)
    for d in range(SUB):
        if d == 0:
            fd = qa * kk
        else:
            e = jnp.exp(jnp.minimum(b - pltpu.roll(b, d, 0), 0.0))
            fd = jnp.where(pos >= d, qa * pltpu.roll(kk, d, 0) * e, 0.0)
        p = jnp.where(dmat == d, jnp.sum(fd, axis=1, keepdims=True), p)
    col = lax.broadcasted_iota(jnp.int32, (SUB, c), 1)
    blocks = [jnp.zeros((SUB, c), F32)]
    for r0 in range(SUB, c, SUB):
        beta = b[r0 - 1:r0, :]
        qt = qa[r0:r0 + SUB] * jnp.exp(b[r0:r0 + SUB] - beta)
        kt = kk * jnp.exp(jnp.minimum(beta - b, 0.0))
        blocks.append(jnp.where(col < r0, _dot_nt(qt, kt), 0.0))
    return p + jnp.concatenate(blocks, axis=0)


def _hg_scores_bwd(dp, qa, kk, b):
    c = qa.shape[0]
    row = lax.broadcasted_iota(jnp.int32, qa.shape, 0)
    pos = jnp.bitwise_and(row, SUB - 1)
    dmat = lax.broadcasted_iota(jnp.int32, (c, c), 0) - lax.broadcasted_iota(jnp.int32, (c, c), 1)
    dqa = jnp.zeros_like(qa)
    dkk = jnp.zeros_like(qa)
    db = jnp.zeros_like(qa)
    for d in range(SUB):
        dcol = jnp.sum(jnp.where(dmat == d, dp, 0.0), axis=1, keepdims=True)
        if d == 0:
            dqa = dqa + dcol * kk
            dkk = dkk + dcol * qa
        else:
            e = jnp.exp(jnp.minimum(b - pltpu.roll(b, d, 0), 0.0))
            w = jnp.where(pos >= d, dcol * e, 0.0)
            kr = pltpu.roll(kk, d, 0)
            dqa = dqa + w * kr
            tmp = w * qa
            dkk = dkk + pltpu.roll(tmp, c - d, 0)
            x = tmp * kr
            db = db + x - pltpu.roll(x, c - d, 0)
    col = lax.broadcasted_iota(jnp.int32, (SUB, c), 1)
    dq_blocks = [jnp.zeros((SUB, qa.shape[1]), F32)]
    db_blocks = [jnp.zeros((SUB, qa.shape[1]), F32)]
    for r0 in range(SUB, c, SUB):
        beta = b[r0 - 1:r0, :]
        eq = jnp.exp(b[r0:r0 + SUB] - beta)
        ek = jnp.exp(jnp.minimum(beta - b, 0.0))
        qt = qa[r0:r0 + SUB] * eq
        kt = kk * ek
        dpi = jnp.where(col < r0, dp[r0:r0 + SUB, :], 0.0)
        dqt = _dot(dpi, kt)
        dkt = _dot_tn(dpi, qt)
        dq_blocks.append(dqt * eq)
        db_blocks.append(dqt * qt)
        dkk = dkk + dkt * ek
        db = db - dkt * kt
    return dqa + jnp.concatenate(dq_blocks, axis=0), dkk, db + jnp.concatenate(db_blocks, axis=0)


def _hg_chunks_per_step(seq):
    nc = seq // CHUNK
    cps = next(k for k in (16, 8, 4, 2, 1) if nc % k == 0)
    return nc, cps, nc // cps


def _hg_fwd(z, lbrow, gain, *, nseq, seq, heads, qoff, rider=None):
    nc, cps, nblk = _hg_chunks_per_step(seq)
    blk = cps * CHUNK
    zspec = lambda off: pl.BlockSpec((blk, HEAD), lambda h, b, n, off=off: (b * nblk + n, off + h))

    def body(zq_ref, zf_ref, zi_ref, zg_ref, lb_ref, gn_ref, o_ref, yb_ref, st_ref, sc_ref, state):
        @pl.when(pl.program_id(2) == 0)
        def _():
            state[...] = jnp.zeros_like(state)

        lb = lb_ref[...]
        gain_v = gn_ref[...]

        def chunk(ci, carry):
            rows = pl.ds(pl.multiple_of(ci * CHUNK, CHUNK), CHUNK)
            st = state[...]
            st_ref[ci] = st
            zi = zi_ref[rows, :]
            zg = zg_ref[rows, :]
            _, _, _, qa, kk, b = _hg_gates(zq_ref[rows, :], zf_ref[rows, :], lb)
            scores = _hg_scores(qa, kk, b).astype(BF16)
            sc_ref[rows, :] = scores
            o = _dot_nt(qa * jnp.exp(b), st) + _dot(scores, zi)
            bl = b[CHUNK - 1:CHUNK, :]
            state[...] = st * jnp.exp(bl) + _dot_tn(zi, kk * jnp.exp(bl - b))
            o_ref[rows, :] = o
            r = lax.rsqrt(jnp.mean(o * o, axis=1, keepdims=True) + EPS)
            yb_ref[rows, :] = (o * r * gain_v * zg * _sigmoid(zg)).astype(yb_ref.dtype)
            return carry

        lax.fori_loop(0, cps, chunk, 0, unroll=True)

    tok = pl.BlockSpec((blk, HEAD), lambda h, b, n: (b * nblk + n, h))
    vec = pl.BlockSpec((1, HEAD), lambda h, b, n: (0, h))
    rows = nseq * seq
    return _hosted_call(
        "hgrn2_fwd", body, grid=(heads, nseq, nblk),
        in_specs=[zspec(qoff), zspec(qoff + heads), zspec(qoff + 2 * heads), zspec(qoff + 3 * heads), vec, vec],
        out_specs=[tok, tok, pl.BlockSpec((None, None, cps, HEAD, HEAD), lambda h, b, n: (h, b, n, 0, 0)),
                   pl.BlockSpec((None, blk, CHUNK), lambda h, b, n: (h, b * nblk + n, 0))],
        out_shape=[jax.ShapeDtypeStruct((rows, heads * HEAD), F32), jax.ShapeDtypeStruct((rows, heads * HEAD), BF16),
                   jax.ShapeDtypeStruct((heads, nseq, nc, HEAD, HEAD), F32),
                   jax.ShapeDtypeStruct((heads, rows, CHUNK), BF16)],
        scratch_shapes=[pltpu.VMEM((HEAD, HEAD), F32)], operands=(z, z, z, z, lbrow, gain), rider=rider)


def _hg_bwd(dyb, z, o, states, scores, lbrow, gain, *, nseq, seq, heads, qoff, rider=None):
    nc, cps, nblk = _hg_chunks_per_step(seq)
    blk = cps * CHUNK
    rev = lambda n: nblk - 1 - n
    zspec = lambda off: pl.BlockSpec((blk, HEAD), lambda h, b, n, off=off: (b * nblk + rev(n), off + h))

    def body(dyb_ref, zq_ref, zf_ref, zi_ref, zg_ref, o_ref, st_ref, sc_ref, lb_ref, gn_ref,
             dzq_ref, dzf_ref, dzi_ref, dzg_ref, dlb_ref, dgn_ref, dstate):
        @pl.when(pl.program_id(2) == 0)
        def _():
            dstate[...] = jnp.zeros_like(dstate)

        @pl.when(jnp.logical_and(pl.program_id(1) == 0, pl.program_id(2) == 0))
        def _():
            dlb_ref[...] = jnp.zeros_like(dlb_ref)
            dgn_ref[...] = jnp.zeros_like(dgn_ref)

        lb = lb_ref[...]
        gain_v = gn_ref[...]
        c = CHUNK
        causal = lax.broadcasted_iota(jnp.int32, (c, c), 0) >= lax.broadcasted_iota(jnp.int32, (c, c), 1)

        def chunk(step, carry):
            ci = cps - 1 - step
            rows = pl.ds(pl.multiple_of(ci * CHUNK, CHUNK), CHUNK)
            zq = zq_ref[rows, :]
            zi = zi_ref[rows, :]
            zg = zg_ref[rows, :]
            sg, f, sq, qa, kk, b = _hg_gates(zq, zf_ref[rows, :], lb)
            eb = jnp.exp(b)
            qt = qa * eb
            bl = b[c - 1:c, :]
            ebl = jnp.exp(bl)
            ekb = jnp.exp(bl - b)
            kh = kk * ekb
            st = st_ref[ci]
            dst = dstate[...]
            o = o_ref[rows, :]
            r = lax.rsqrt(jnp.mean(o * o, axis=1, keepdims=True) + EPS)
            oh = o * r
            szg = _sigmoid(zg)
            dyb = dyb_ref[rows, :]
            don = dyb * zg * szg
            dzg_ref[rows, :] = (dyb * oh * gain_v * szg * (1.0 + zg * (1.0 - szg))).astype(dzg_ref.dtype)
            doh = don * gain_v
            do = r * (doh - oh * jnp.mean(doh * oh, axis=1, keepdims=True))
            dqt = _dot(do, st)
            dp = jnp.where(causal, _dot_nt(do, zi), 0.0)
            dzi_ref[rows, :] = (_dot_tn(sc_ref[rows, :], do) + _dot_nt(kh, dst)).astype(dzi_ref.dtype)
            dkh = _dot(zi, dst)
            dbl = jnp.sum(dkh * kh, axis=0, keepdims=True) + jnp.sum(dst * st, axis=0, keepdims=True) * ebl
            dstate[...] = _dot_tn(do, qt) + dst * ebl
            dqa_s, dkk_s, db_s = _hg_scores_bwd(dp, qa, kk, b)
            dqa = dqt * eb + dqa_s
            dkk = dkh * ekb + dkk_s
            db = dqt * qt - dkh * kh + db_s
            row = lax.broadcasted_iota(jnp.int32, db.shape, 0)
            db = db + jnp.where(row == c - 1, dbl, 0.0)
            df = _cumsum_rows(db, reverse=True) / f - dkk
            dzf_ref[rows, :] = (df * (1.0 - lb) * sg * (1.0 - sg)).astype(dzf_ref.dtype)
            dzq_ref[rows, :] = (dqa * (HEAD ** -0.5) * sq * (1.0 + zq * (1.0 - sq))).astype(dzq_ref.dtype)
            dlb_ref[...] += jnp.sum(df * (1.0 - sg), axis=0, keepdims=True)
            dgn_ref[...] += jnp.sum(don * oh, axis=0, keepdims=True)
            return carry

        lax.fori_loop(0, cps, chunk, 0, unroll=True)

    tok = pl.BlockSpec((blk, HEAD), lambda h, b, n: (b * nblk + rev(n), h))
    vec = pl.BlockSpec((1, HEAD), lambda h, b, n: (0, h))
    rows = nseq * seq
    return _hosted_call(
        "hgrn2_bwd", body, grid=(heads, nseq, nblk),
        in_specs=[tok, zspec(qoff), zspec(qoff + heads), zspec(qoff + 2 * heads), zspec(qoff + 3 * heads), tok,
                  pl.BlockSpec((None, None, cps, HEAD, HEAD), lambda h, b, n: (h, b, rev(n), 0, 0)),
                  pl.BlockSpec((None, blk, CHUNK), lambda h, b, n: (h, b * nblk + rev(n), 0)), vec, vec],
        out_specs=[tok, tok, tok, tok, vec, vec],
        out_shape=[jax.ShapeDtypeStruct((rows, heads * HEAD), BF16)] * 4
        + [jax.ShapeDtypeStruct((1, heads * HEAD), F32)] * 2,
        scratch_shapes=[pltpu.VMEM((HEAD, HEAD), F32)],
        operands=(dyb, z, z, z, z, o, states, scores, lbrow, gain), rider=rider)


def _shift_rows(x, k):
    n = x.shape[0]
    r = pltpu.roll(x, k % n, 0)
    sub = lax.broadcasted_iota(jnp.int32, (SUBLANES, x.shape[1]), 0)
    if k > 0:
        return jnp.concatenate([jnp.where(sub >= k, r[0:SUBLANES], 0.0), r[SUBLANES:]], axis=0)
    return jnp.concatenate([r[:n - SUBLANES], jnp.where(sub < SUBLANES + k, r[n - SUBLANES:], 0.0)], axis=0)


def _conv_taps(h, w, bias):
    h1 = _shift_rows(h, 1)
    h2 = _shift_rows(h, 2)
    return h2 * w[0:1, :] + h1 * w[1:2, :] + h * w[2:3, :] + bias, h1, h2


def _conv_fwd(h, wconv, bconv, *, nseq, seq):
    ff2 = h.shape[1]
    ncol = ff2 // 2 // LANES

    def body(hg_ref, hv_ref, wg_ref, wv_ref, bg_ref, bv_ref, a_ref):
        g, _, _ = _conv_taps(hg_ref[...].astype(F32), wg_ref[...], bg_ref[...])
        v, _, _ = _conv_taps(hv_ref[...].astype(F32), wv_ref[...], bv_ref[...])
        a_ref[...] = (g * _sigmoid(g) * v).astype(a_ref.dtype)

    tok = lambda off: pl.BlockSpec((seq, LANES), lambda j, b, off=off: (b, off + j))
    wsp = lambda off: pl.BlockSpec((CONV_W, LANES), lambda j, b, off=off: (0, off + j))
    bsp = lambda off: pl.BlockSpec((1, LANES), lambda j, b, off=off: (0, off + j))
    return pl.pallas_call(
        body, name="conv_fwd", grid=(ncol, nseq),
        in_specs=[tok(0), tok(ncol), wsp(0), wsp(ncol), bsp(0), bsp(ncol)],
        out_specs=tok(0), out_shape=jax.ShapeDtypeStruct((nseq * seq, ff2 // 2), BF16),
        compiler_params=_params("arbitrary", "arbitrary"),
    )(h, h, wconv, wconv, bconv, bconv)


def _conv_bwd(da, h, wconv, bconv, *, nseq, seq):
    ff2 = h.shape[1]
    ncol = ff2 // 2 // LANES

    def half_bwd(d, hcur, h1, h2, w):
        d1 = _shift_rows(d, -1)
        d2 = _shift_rows(d, -2)
        dh = d * w[2:3, :] + d1 * w[1:2, :] + d2 * w[0:1, :]
        stats = jnp.concatenate(
            [jnp.sum(h2 * d, axis=0, keepdims=True), jnp.sum(h1 * d, axis=0, keepdims=True),
             jnp.sum(hcur * d, axis=0, keepdims=True), jnp.sum(d, axis=0, keepdims=True),
             jnp.zeros((SUBLANES - 4, d.shape[1]), F32)], axis=0)
        return dh, stats

    def body(da_ref, hg_ref, hv_ref, wg_ref, wv_ref, bg_ref, bv_ref, dhg_ref, dhv_ref, sg_ref, sv_ref):
        hg = hg_ref[...].astype(F32)
        hv = hv_ref[...].astype(F32)
        wg = wg_ref[...]
        wv = wv_ref[...]
        g, g1, g2 = _conv_taps(hg, wg, bg_ref[...])
        v, v1, v2 = _conv_taps(hv, wv, bv_ref[...])
        da = da_ref[...].astype(F32)
        s = _sigmoid(g)
        dhg, stg = half_bwd(da * v * s * (1.0 + g * (1.0 - s)), hg, g1, g2, wg)
        dhv, stv = half_bwd(da * g * s, hv, v1, v2, wv)
        dhg_ref[...] = dhg.astype(dhg_ref.dtype)
        dhv_ref[...] = dhv.astype(dhv_ref.dtype)
        first = pl.program_id(1) == 0
        for r, val in ((sg_ref, stg), (sv_ref, stv)):
            @pl.when(first)
            def _():
                r[...] = val

            @pl.when(jnp.logical_not(first))
            def _():
                r[...] += val

    tok = lambda off: pl.BlockSpec((seq, LANES), lambda j, b, off=off: (b, off + j))
    wsp = lambda off: pl.BlockSpec((CONV_W, LANES), lambda j, b, off=off: (0, off + j))
    bsp = lambda off: pl.BlockSpec((1, LANES), lambda j, b, off=off: (0, off + j))
    ssp = pl.BlockSpec((SUBLANES, LANES), lambda j, b: (0, j))
    dhg, dhv, stg, stv = pl.pallas_call(
        body, name="conv_bwd", grid=(ncol, nseq),
        in_specs=[tok(0), tok(0), tok(ncol), wsp(0), wsp(ncol), bsp(0), bsp(ncol)],
        out_specs=[tok(0), tok(0), ssp, ssp],
        out_shape=[jax.ShapeDtypeStruct((nseq * seq, ff2 // 2), BF16)] * 2
        + [jax.ShapeDtypeStruct((SUBLANES, ff2 // 2), F32)] * 2,
        compiler_params=_params("arbitrary", "arbitrary"),
    )(da, h, h, wconv, wconv, bconv, bconv)
    return (dhg, dhv), jnp.concatenate([stg, stv], axis=1)


def _rms_fwd(xv, g):
    r = lax.rsqrt(jnp.mean(xv * xv, axis=1, keepdims=True) + EPS)
    return (xv * r * g,)


def _rms_bwd(xv, g, dy, res):
    r = lax.rsqrt(jnp.mean(xv * xv, axis=1, keepdims=True) + EPS)
    xh = xv * r
    dxh = dy * g
    dx = r * (dxh - xh * jnp.mean(dxh * xh, axis=1, keepdims=True)) + res
    return dx, jnp.sum(dy * xh, axis=0, keepdims=True)


def _loss_head(x2, tgt, g):
    d = x2.shape[1]
    r = lax.rsqrt(jnp.mean(x2 * x2, axis=1, keepdims=True) + EPS)
    xh = x2 * r
    err = xh * g - tgt
    dy = err * (1.0 / d)
    dxh = dy * g
    dx = r * (dxh - xh * jnp.mean(dxh * xh, axis=1, keepdims=True))
    loss = 0.5 * jnp.sum(jnp.mean(err * err, axis=1, keepdims=True), axis=0, keepdims=True)
    return dx, jnp.sum(dy * xh, axis=0, keepdims=True), jnp.broadcast_to(loss, (1, LANES))


LATE_A = ("w_down", "w_out")
LATE_B = ("w_up", "w_pa", "w_pb")
LATE = LATE_A + LATE_B
EARLY_GRADS = ("w_down", "w_up", "w_out", "w_pa", "w_pb", "w_glu")
ROW_SHARDED = ("w_glu", "w_out", "w_down")


def _local_step(x, tgt, u, z_own, p, late, *, nseq, seq):
    p = dict(p)
    chip = 2 * lax.axis_index("x") + lax.axis_index("y")
    t, d = x.shape
    s5w = p["s5_d"].shape[1]
    hgw = p["gain"].shape[1]
    heads = hgw // HEAD
    qoff = s5w // LANES
    gblk = (s5w + 4 * hgw) // GATE_BLOCK
    ngb = d // GATE_BLOCK
    tm = _row_tile(t, ROW_TILE)
    row = lambda a, w=None, base=0: (a, a.shape[1] if w is None else w, base, "row")
    vec = lambda a, w=None, base=0: (a, a.shape[1] if w is None else w, base, "vec")
    rw = functools.partial(_rowwise, rows=t, tm=tm)

    z, _ = _in_proj_rest(u, p["w_in"], z_own, None)

    lam_re, lam_im, bb_re, bb_im = _s5_discretize(p["s5_a_re"], p["s5_a_im"], p["s5_log_dt"], p["s5_b_re"], p["s5_b_im"])
    bre3 = _s5_in_blocks(bb_re).astype(BF16)
    bim3 = _s5_in_blocks(bb_im).astype(BF16)
    cre3 = _s5_out_blocks(p["s5_c_re"]).astype(BF16)
    cim3 = _s5_out_blocks(p["s5_c_im"]).astype(BF16)
    coef_f = _s5_scan_tables(lam_re.reshape(-1), lam_im.reshape(-1), False)
    coef_r = _s5_scan_tables(lam_re.reshape(-1), lam_im.reshape(-1), True)
    (o, yb, states, scores), landed_b = _hg_fwd(z, p["lbrow"], p["gain"], nseq=nseq, seq=seq, heads=heads, qoff=qoff,
                                                rider=_gather_ici_rider([late[n] for n in LATE_B]))
    def place_own(names, gathered):
        for n, g in zip(names, gathered):
            full = lax.dynamic_update_index_in_dim(g, late[n], chip, 0)
            p[n] = full.reshape(-1, full.shape[-1]) if n in ROW_SHARDED else full

    nb_late = len(LATE_B)
    (y5, xre, xim), got = _s5_fwd(z, bre3, bim3, cre3, cim3, coef_f, p["s5_d"], nseq=nseq, seq=seq,
                                  rider=_merge_riders(_gather_pass_rider(list(landed_b)),
                                                      _gather_ici_rider([late[n] for n in LATE_A])))
    place_own(LATE_B, got[:nb_late])
    ya0, gl, ya = _glu_fwd(y5, p["w_glu"], p["b_glu"])

    joined = lambda w3: w3.transpose(1, 0, 2).reshape(w3.shape[1], -1)
    split = lambda g: g.reshape(g.shape[0], N_CHIPS, -1).transpose(1, 0, 2)
    wpa, wpb = joined(p["w_pa"]), joined(p["w_pb"])
    pa, got_a = _mm_fwd_rows("proj_a", ya, wpa, out_dtype=BF16, rider=_gather_pass_rider(list(got[nb_late:])))
    place_own(LATE_A, got_a)
    pb = _mm_fwd_rows("proj_b", yb, wpb, out_dtype=BF16)
    gb = GATE_BLOCK
    (m,) = rw("merge", lambda ga, gbv, a, b: (_sigmoid(ga) * a + _sigmoid(gbv) * b,),
              [row(z, gb, gblk), row(z, gb, gblk + ngb), row(pa, gb), row(pb, gb)], [(d, gb, BF16)], ncol=ngb)
    x1 = _mm_fwd_rows("out_proj", m, p["w_out"], res=x)

    (u2,) = rw("rms_ffn", _rms_fwd, [row(x1), vec(p["g_ffn"])], [(d, d, BF16)])
    h = _mm_fwd_cols("up_proj", u2, p["w_up"], out_dtype=BF16)
    a = _conv_fwd(h, p["w_conv"], p["b_conv"], nseq=nseq, seq=seq)
    dx2, dg_final, lossv = _mm_fwd_rows("down_proj", a, p["w_down"], res=x1,
                                        epilogue=(_loss_head, [tgt, p["g_final"]], [d, LANES]))

    norm_bwd = lambda dyv, xv, g, resv: _rms_bwd(xv, g, dyv, resv)
    da = _mm_bwd_rows("down_bwd", dx2, p["w_down"], out_dtype=BF16)
    g_wdown = _mm_wgrad_rows("down_wgrad", a, dx2)
    dh, cstats = _conv_bwd(da, h, p["w_conv"], p["b_conv"], nseq=nseq, seq=seq)
    dx1, dg_ffn = _mm_bwd_cols("up_bwd", dh, p["w_up"], epilogue=(norm_bwd, [x1, p["g_ffn"], dx2], [d]))
    g_wup = _mm_wgrad_cols("up_wgrad", u2, dh)

    dm = _mm_bwd_rows("out_bwd", dx1, p["w_out"], out_dtype=BF16)
    g_wout = _mm_wgrad_rows("out_wgrad", m, dx1)

    def merge_bwd(ga, gbv, av, bv, dmv):
        sa = _sigmoid(ga)
        sb = _sigmoid(gbv)
        return dmv * sa, dmv * sb, dmv * av * sa * (1.0 - sa), dmv * bv * sb * (1.0 - sb)

    dpa, dpb, dzga, dzgb = rw("merge_bwd", merge_bwd,
                              [row(z, gb, gblk), row(z, gb, gblk + ngb), row(pa, gb), row(pb, gb), row(dm, gb)],
                              [(d, gb, BF16)] * 4, ncol=ngb)
    dya = _mm_bwd_rows("proj_a_bwd", dpa, wpa)
    g_wpa = split(_mm_wgrad_rows("proj_a_wgrad", ya, dpa))
    dyb = _mm_bwd_rows("proj_b_bwd", dpb, wpb)
    g_wpb = split(_mm_wgrad_rows("proj_b_wgrad", yb, dpb))

    dgl, dy5, db_glu = _glu_bwd(y5, gl, p["b_glu"], dya, p["w_glu"])
    g_wglu = _mm_wgrad_rows("glu_wgrad", ya0, dgl)
    partial = dict(w_down=g_wdown, w_up=g_wup, w_out=g_wout, w_pa=g_wpa, w_pb=g_wpb, w_glu=g_wglu)
    parts = [_grad_parts(partial[n]) for n in EARLY_GRADS]
    (dza, dbre3, dbim3, dcre3, dcim3, dlam, dd), sib = _s5_bwd(
        dy5, z, xre, xim, bre3, bim3, cre3, cim3, coef_r, p["s5_d"], nseq=nseq, seq=seq, rider=_swap_halves_rider(parts))
    pair = _pair_sums(EARLY_GRADS, parts, sib)
    (dzq, dzf, dzi, dzg, dlb, dgain), others = _hg_bwd(
        dyb, z, o, states, scores, p["lbrow"], p["gain"], nseq=nseq, seq=seq, heads=heads, qoff=qoff,
        rider=_scatter_rider(pair))
    halves = _chip_sums(EARLY_GRADS, pair, others)

    dz = jnp.concatenate([dza, dzq, dzf, dzi, dzg, dzga, dzgb], axis=1)
    gshape = lam_re.shape
    small = {
        "loss": lossv, "g_ffn": dg_ffn, "g_final": dg_final, "b_glu": db_glu, "gain": dgain,
        "lbrow": dlb, "s5_d": dd, "w_conv": cstats[0:CONV_W], "b_conv": cstats[CONV_W:CONV_W + 1],
        "lam_re": dlam[:, 0, :].reshape(gshape), "lam_im": dlam[:, 1, :].reshape(gshape),
        "bb_re": _s5_in_blocks_diag(dbre3), "bb_im": _s5_in_blocks_diag(dbim3),
        "s5_c_re": _s5_out_blocks_diag(dcre3), "s5_c_im": _s5_out_blocks_diag(dcim3),
    }
    small_vec = _pack([small[n] for n in SMALL_PARTS], F32)
    g_win, (small_all, *sibs) = _mm_wgrad_cols(
        "in_wgrad", u, dz, rider=_merge_riders(_gather_all_rider(small_vec), _swap_sums_rider(halves)))
    big = dict(zip(EARLY_GRADS, zip(halves, sibs)))
    small_sum = _sum_over_devices("small_grad_sum", small_vec, small_all)
    sm = dict(zip(SMALL_PARTS, _unpack(small_sum, [small[n].shape for n in SMALL_PARTS])))
    last = [_grad_parts(g_win)]
    pair = _pair_sums(("w_in",), last, _run_rider("grad_swap_halves", _swap_halves_rider(last)))
    (dx, dg_mix), others = _mm_bwd_cols("in_bwd", dz, p["w_in"], epilogue=(norm_bwd, [x, p["g_mix"], dx1], [d]),
                                        rider=_scatter_rider(pair))
    (half,) = _chip_sums(("w_in",), pair, others)
    mid = half.shape[0] // 2
    mix_vec = dg_mix.reshape(SUBLANES, -1)
    top, bottom, mix_all = _run_rider("grad_swap_sums", _merge_riders(_swap_sums_rider([half[:mid], half[mid:]]),
                                                                      _gather_all_rider(mix_vec)))
    big["w_in"] = (half, jnp.concatenate([top, bottom], axis=0))
    sm["g_mix"] = _sum_over_devices("g_mix_sum", mix_vec, mix_all).reshape(dg_mix.shape)
    return dx, big, sm


ANY = pl.BlockSpec(memory_space=pl.ANY)


def _place():
    x, y, c = lax.axis_index("x"), lax.axis_index("y"), lax.axis_index("c")
    chips = [(1 - x, y), (x, 1 - y), (1 - x, 1 - y)]
    return x, y, c, chips


def _remote(src, dst, send_sems, recv_sems, k, to):
    return pltpu.make_async_remote_copy(src_ref=src, dst_ref=dst, send_sem=send_sems.at[k], recv_sem=recv_sems.at[k],
                                        device_id=to, device_id_type=MESH)


def _half(rows, which):
    return pl.ds(pl.multiple_of(which * (rows // 2), SUBLANES), rows // 2)


class _SemView:
    def __init__(self, base, offset):
        self.base, self.offset = base, offset

    @property
    def at(self):
        return self

    def __getitem__(self, k):
        return self.base.at[self.offset + k]


def _merge_riders(first, second):
    na, no, ns = len(first.arrays), len(first.out_shapes), first.nsem

    def split(fn_a, fn_b):
        def run(ins, outs, send_sems, recv_sems):
            fn_a(ins[:na], outs[:no], send_sems, recv_sems)
            fn_b(ins[na:], outs[no:], _SemView(send_sems, ns), _SemView(recv_sems, ns))
        return run

    aliases = dict(first.aliases)
    aliases.update({na + i: no + o for i, o in second.aliases.items()})
    return _Rider(first.arrays + second.arrays, first.out_shapes + second.out_shapes, ns + second.nsem,
                  split(first.start, second.start), split(first.finish, second.finish), aliases)


PREPARE_TILE = 512


def _prepare(x, gain, w_own, arrays, rider):
    t, d = x.shape
    ns = w_own.shape[1]
    n = len(arrays)
    tm = _row_tile(t, PREPARE_TILE)

    def body(place, x_ref, g_ref, w_ref, *refs):
        (u,) = _rms_fwd(x_ref[...], g_ref[...])
        u = u.astype(BF16)
        refs[n][...] = u
        refs[n + 1][...] = _dot(u, w_ref[...])

        @pl.when(pl.program_id(0) == 0)
        def _():
            for i in range(n):
                refs[n + 2 + i][...] = refs[i][...].astype(BF16)

    vm = pl.BlockSpec(memory_space=pltpu.VMEM)
    tok = pl.BlockSpec((tm, d), lambda i, place: (i, 0))
    outs, gathered = _hosted_scalar_call(
        "prepare", body, grid=(t // tm,),
        in_specs=[tok, pl.BlockSpec((1, d), lambda i, place: (0, 0)), vm] + [vm] * n,
        out_specs=[tok, pl.BlockSpec((tm, ns), lambda i, place: (i, place[1]))] + [vm] * n,
        out_shape=[jax.ShapeDtypeStruct((t, d), BF16), jax.ShapeDtypeStruct((t, N_CHIPS * ns), F32)]
        + [jax.ShapeDtypeStruct(a.shape, BF16) for a in arrays],
        operands=[x, gain, w_own] + list(arrays), rider=rider)
    return outs[0], outs[1], outs[2:], gathered


def _in_proj_rest(u, w3, z, rider):
    t, k = u.shape
    ns = w3.shape[2]
    tm = _mm_tile(t, 2 * k * _size(u) + 2 * ns * 4, 2 * k * ns * _size(w3))
    other = lambda j, place: jnp.bitwise_xor(place[1], j + 1)

    def body(place, u_ref, w_ref, z_ref, o_ref):
        o_ref[...] = _dot(u_ref[...], w_ref[...])

    outs, ridden = _hosted_scalar_call(
        "in_proj", body, grid=(N_CHIPS - 1, t // tm),
        in_specs=[pl.BlockSpec((tm, k), lambda j, i, place: (i, 0)),
                  pl.BlockSpec((None, k, ns), lambda j, i, place: (other(j, place), 0, 0)), ANY],
        out_specs=[pl.BlockSpec((tm, ns), lambda j, i, place: (i, other(j, place)))],
        out_shape=[jax.ShapeDtypeStruct(z.shape, z.dtype)], operands=[u, w3, z], rider=rider, aliases={2: 0})
    return outs[0], ridden


def _symmetric_rider(arrays, out_shapes, copies_of, nsem):
    def start(ins, outs, send_sems, recv_sems):
        for cp in copies_of(ins, outs, send_sems, recv_sems):
            cp.start()

    def finish(ins, outs, send_sems, recv_sems):
        for cp in copies_of(ins, outs, send_sems, recv_sems):
            cp.wait()

    return _Rider(arrays, out_shapes, nsem, start, finish)


def _swap_halves_rider(parts):
    def copies_of(ins, outs, send_sems, recv_sems):
        x, y, c, _ = _place()
        return [_remote(ins[a].at[:, _half(g.shape[1], 1 - c), :], outs[a], send_sems, recv_sems, a, (x, y, 1 - c))
                for a, g in enumerate(parts)]

    shapes = [jax.ShapeDtypeStruct((g.shape[0], g.shape[1] // 2, g.shape[2]), g.dtype) for g in parts]
    return _symmetric_rider(parts, shapes, copies_of, len(parts))


def _scatter_rider(parts):
    def copies_of(ins, outs, send_sems, recv_sems):
        x, y, c, chips = _place()
        return [_remote(ins[a].at[2 * cx + cy], outs[a].at[j], send_sems, recv_sems, 3 * a + j, (cx, cy, c))
                for a in range(len(parts)) for j, (cx, cy) in enumerate(chips)]

    shapes = [jax.ShapeDtypeStruct((N_CHIPS - 1,) + h.shape[1:], h.dtype) for h in parts]
    return _symmetric_rider(parts, shapes, copies_of, 3 * len(parts))


def _swap_sums_rider(parts):
    def copies_of(ins, outs, send_sems, recv_sems):
        x, y, c, _ = _place()
        return [_remote(ins[a], outs[a], send_sems, recv_sems, a, (x, y, 1 - c)) for a in range(len(parts))]

    shapes = [jax.ShapeDtypeStruct(g.shape, g.dtype) for g in parts]
    return _symmetric_rider(parts, shapes, copies_of, len(parts))


def _gather_ici_rider(shards):
    def sends(ins, outs, send_sems, recv_sems):
        x, y, c, chips = _place()
        return [_remote(ins[a].at[_half(s.shape[0], c)], outs[a].at[2 * x + y, _half(s.shape[0], c)], send_sems,
                        recv_sems, 3 * a + j, (cx, cy, c)) for a, s in enumerate(shards) for j, (cx, cy) in enumerate(chips)]

    def start(ins, outs, send_sems, recv_sems):
        for cp in sends(ins, outs, send_sems, recv_sems):
            cp.start()

    def finish(ins, outs, send_sems, recv_sems):
        x, y, c, chips = _place()
        for a, s in enumerate(shards):
            for j, (cx, cy) in enumerate(chips):
                landed = outs[a].at[2 * cx + cy, _half(s.shape[0], c)]
                _remote(landed, landed, send_sems, recv_sems, 3 * a + j, (x, y, c)).wait_recv()
        for cp in sends(ins, outs, send_sems, recv_sems):
            cp.wait_send()

    shapes = [jax.ShapeDtypeStruct((N_CHIPS,) + s.shape, s.dtype) for s in shards]
    return _Rider(shards, shapes, 3 * len(shards), start, finish)


def _gather_full_rider(shards):
    n = len(shards)

    def sends(ins, outs, send_sems, recv_sems):
        x, y, c, chips = _place()
        return [_remote(ins[a].at[_half(s.shape[0], c)], outs[a].at[2 * x + y, _half(s.shape[0], c)], send_sems,
                        recv_sems, 6 * a + j, (cx, cy, c)) for a, s in enumerate(shards) for j, (cx, cy) in enumerate(chips)]

    def start(ins, outs, send_sems, recv_sems):
        for cp in sends(ins, outs, send_sems, recv_sems):
            cp.start()

    def finish(ins, outs, send_sems, recv_sems):
        x, y, c, chips = _place()
        passed = []
        for a, s in enumerate(shards):
            for j, (cx, cy) in enumerate(chips):
                landed = outs[a].at[2 * cx + cy, _half(s.shape[0], c)]
                _remote(landed, landed, send_sems, recv_sems, 6 * a + j, (x, y, c)).wait_recv()
                passed.append(_remote(landed, landed, send_sems, recv_sems, 6 * a + 3 + j, (x, y, 1 - c)))
                passed[-1].start()
        for a, s in enumerate(shards):
            for j, (cx, cy) in enumerate(chips):
                other = outs[a].at[2 * cx + cy, _half(s.shape[0], 1 - c)]
                _remote(other, other, send_sems, recv_sems, 6 * a + 3 + j, (x, y, c)).wait_recv()
        for cp in sends(ins, outs, send_sems, recv_sems) + passed:
            cp.wait_send()

    shapes = [jax.ShapeDtypeStruct((N_CHIPS,) + s.shape, s.dtype) for s in shards]
    return _Rider(shards, shapes, 6 * n, start, finish)


def _gather_pass_rider(landed):
    def sends(ins, outs, send_sems, recv_sems):
        x, y, c, chips = _place()
        return [_remote(ins[a].at[2 * cx + cy, _half(g.shape[1], c)], outs[a].at[2 * cx + cy, _half(g.shape[1], c)],
                        send_sems, recv_sems, 3 * a + j, (x, y, 1 - c))
                for a, g in enumerate(landed) for j, (cx, cy) in enumerate(chips)]

    def start(ins, outs, send_sems, recv_sems):
        for cp in sends(ins, outs, send_sems, recv_sems):
            cp.start()

    def finish(ins, outs, send_sems, recv_sems):
        x, y, c, chips = _place()
        for a, g in enumerate(landed):
            for j, (cx, cy) in enumerate(chips):
                other = outs[a].at[2 * cx + cy, _half(g.shape[1], 1 - c)]
                _remote(other, other, send_sems, recv_sems, 3 * a + j, (x, y, c)).wait_recv()
        for cp in sends(ins, outs, send_sems, recv_sems):
            cp.wait_send()

    shapes = [jax.ShapeDtypeStruct(g.shape, g.dtype) for g in landed]
    return _Rider(landed, shapes, 3 * len(landed), start, finish, aliases={a: a for a in range(len(landed))})


def _grad_parts(g):
    return g.reshape((N_CHIPS, -1, g.shape[-1]))


def _place_scalars():
    return jnp.stack([lax.axis_index("c"), 2 * lax.axis_index("x") + lax.axis_index("y")]).astype(jnp.int32)


def _scalar_call(body, name, grid, in_specs, out_specs, out_shape, operands):
    spec = pltpu.PrefetchScalarGridSpec(num_scalar_prefetch=1, grid=grid, in_specs=in_specs, out_specs=out_specs)
    return pl.pallas_call(body, name=name, grid_spec=spec, out_shape=out_shape,
                          compiler_params=_params(*(["arbitrary"] * len(grid))))(_place_scalars(), *operands)


def _pair_sums(names, parts, sib):
    out = []
    for n, g, s in zip(names, parts, sib):
        rh, cols = s.shape[1], s.shape[2]
        tm = _row_tile(rh, ROW_TILE)
        nblk = rh // tm

        def body(place, g_ref, s_ref, o_ref):
            o_ref[...] = (g_ref[...].astype(F32) + s_ref[...].astype(F32)).astype(o_ref.dtype)

        blk = pl.BlockSpec((None, tm, cols), lambda j, i, place: (j, i, 0))
        own = pl.BlockSpec((None, tm, cols), lambda j, i, place, nblk=nblk: (j, place[0] * nblk + i, 0))
        out.append(_scalar_call(body, "grad_pair_sum_" + n, (N_CHIPS, nblk), [own, blk], blk,
                                jax.ShapeDtypeStruct(s.shape, BF16), (g, s)))
    return out


def _chip_sums(names, pair, others):
    out = []
    for n, h, o in zip(names, pair, others):
        rh, cols = h.shape[1], h.shape[2]
        tm = _row_tile(rh, ROW_TILE)

        def body(place, h_ref, a_ref, b_ref, c_ref, o_ref):
            o_ref[...] = (h_ref[...].astype(F32) + a_ref[...].astype(F32)) + b_ref[...].astype(F32) + c_ref[...].astype(F32)

        mine = pl.BlockSpec((None, tm, cols), lambda i, place: (place[1], i, 0))
        other = lambda k: pl.BlockSpec((None, tm, cols), lambda i, place, k=k: (k, i, 0))
        out.append(_scalar_call(body, "grad_chip_sum_" + n, (rh // tm,), [mine, other(0), other(1), other(2)],
                                pl.BlockSpec((tm, cols), lambda i, place: (i, 0)), jax.ShapeDtypeStruct((rh, cols), F32),
                                (h, o, o, o)))
    return out


def _adamw_halves(name, w, m, v, own, sib):
    rh, cols = own.shape
    tm = _row_tile(rh, ADAMW_TILE)
    nblk = rh // tm

    def body(place, w_ref, m_ref, v_ref, own_ref, sib_ref, g_ref, d_ref, m2_ref, v2_ref):
        mine = pl.program_id(0) // nblk == place[0]

        def run(gv):
            g_ref[...] = gv
            d_ref[...], m2_ref[...], v2_ref[...] = _adamw_math(w_ref[...], gv, m_ref[...], v_ref[...])

        @pl.when(mine)
        def _():
            run(own_ref[...])

        @pl.when(jnp.logical_not(mine))
        def _():
            run(sib_ref[...])

    full = pl.BlockSpec((tm, cols), lambda i, place: (i, 0))
    own_spec = pl.BlockSpec((tm, cols), lambda i, place: (jnp.where(i // nblk == place[0], i % nblk, 0), 0))
    sib_spec = pl.BlockSpec((tm, cols), lambda i, place: (jnp.where(i // nblk == place[0], 0, i % nblk), 0))
    return _scalar_call(body, name, (2 * nblk,), [full, full, full, own_spec, sib_spec], [full] * 4,
                        [jax.ShapeDtypeStruct((2 * rh, cols), F32)] * 4, (w, m, v, own, sib))


def _gather_all_rider(v):
    m_per = v.shape[0]

    def rows(ref, px, py, pc):
        return ref.at[pl.ds(pl.multiple_of((4 * px + 2 * py + pc) * m_per, 8), m_per)]

    def first(ins, outs, send_sems, recv_sems):
        x, y, c, chips = _place()
        mine = rows(outs[0], x, y, c)
        return [_remote(ins[0], mine, send_sems, recv_sems, 0, (x, y, 1 - c))] + [
            _remote(ins[0], mine, send_sems, recv_sems, 1 + j, (cx, cy, c)) for j, (cx, cy) in enumerate(chips)]

    def start(ins, outs, send_sems, recv_sems):
        for cp in first(ins, outs, send_sems, recv_sems):
            cp.start()

    def finish(ins, outs, send_sems, recv_sems):
        x, y, c, chips = _place()
        passed = []
        for j, (cx, cy) in enumerate(chips):
            blk = rows(outs[0], cx, cy, c)
            _remote(blk, blk, send_sems, recv_sems, 1 + j, (x, y, c)).wait_recv()
            passed.append(_remote(blk, blk, send_sems, recv_sems, 4 + j, (x, y, 1 - c)))
            passed[j].start()
        sib = rows(outs[0], x, y, 1 - c)
        _remote(sib, sib, send_sems, recv_sems, 0, (x, y, c)).wait_recv()
        for j, (cx, cy) in enumerate(chips):
            blk = rows(outs[0], cx, cy, 1 - c)
            _remote(blk, blk, send_sems, recv_sems, 4 + j, (x, y, c)).wait_recv()
        for cp in first(ins, outs, send_sems, recv_sems) + passed:
            cp.wait_send()

    return _Rider([v], [jax.ShapeDtypeStruct((N_DEV * m_per,) + v.shape[1:], v.dtype)], 7, start, finish)


def _sum_over_devices(name, v, gathered):
    m_per = v.shape[0]
    dev = 4 * lax.axis_index("x") + 2 * lax.axis_index("y") + lax.axis_index("c")
    full = lax.dynamic_update_slice_in_dim(gathered, v, dev * m_per, axis=0)
    return _sum_blocks(name, [full[i * m_per:(i + 1) * m_per] for i in range(N_DEV)], F32)


def _sum_blocks(name, parts, out_dtype):
    rows, cols = parts[0].shape
    tm = _row_tile(rows, ROW_TILE)

    def body(*refs):
        acc = refs[0][...].astype(F32)
        for r in refs[1:-1]:
            acc = acc + r[...].astype(F32)
        refs[-1][...] = acc.astype(refs[-1].dtype)

    spec = pl.BlockSpec((tm, cols), lambda i: (i, 0))
    return pl.pallas_call(
        body, name=name, grid=(rows // tm,), in_specs=[spec] * len(parts), out_specs=spec,
        out_shape=jax.ShapeDtypeStruct((rows, cols), out_dtype), compiler_params=_params("arbitrary"),
    )(*parts)


def _adamw_math(wv, gv, mv, vv):
    m2 = ADAM_B1 * mv + (1.0 - ADAM_B1) * gv
    v2 = ADAM_B2 * vv + (1.0 - ADAM_B2) * (gv * gv)
    delta = -ADAM_LR * ((m2 / (1.0 - ADAM_B1 ** ADAM_STEP)) / (jnp.sqrt(v2 / (1.0 - ADAM_B2 ** ADAM_STEP)) + ADAM_EPS)
                        + ADAM_WD * wv)
    return delta, m2, v2


def _adamw_small(ws, gs, ms, vs):
    n = len(ws)

    def body(*refs):
        for i in range(n):
            res = _adamw_math(refs[i][...], refs[n + i][...], refs[2 * n + i][...], refs[3 * n + i][...])
            for k in range(3):
                refs[(4 + k) * n + i][...] = res[k]

    vm = pl.BlockSpec(memory_space=pltpu.VMEM)
    outs = pl.pallas_call(
        body, name="adamw_small", in_specs=[vm] * (4 * n), out_specs=[vm] * (3 * n),
        out_shape=[jax.ShapeDtypeStruct(a.shape, F32) for a in ws] * 3,
        compiler_params=pltpu.CompilerParams(vmem_limit_bytes=VMEM_LIMIT_BYTES),
    )(*ws, *gs, *ms, *vs)
    return outs[:n], outs[n:2 * n], outs[2 * n:]


PACK_ROWS = 256


def _pack(flat_parts, dtype, lead=()):
    parts = [a.astype(dtype).reshape(lead + (-1,)) for a in flat_parts]
    n = sum(a.shape[-1] for a in parts)
    chunk = PACK_ROWS * LANES
    total = -(-n // chunk) * chunk
    if total > n:
        parts.append(jnp.zeros(lead + (total - n,), dtype))
    return jnp.concatenate(parts, axis=-1).reshape(lead + (total // LANES, LANES))


def _unpack(buf, shapes, lead=()):
    flat = buf.reshape(lead + (-1,))
    out, off = [], 0
    for shp in shapes:
        n = math.prod(shp)
        out.append(lax.slice_in_dim(flat, off, off + n, axis=len(lead)).reshape(lead + tuple(shp)))
        off += n
    return out


BIG = ("w_in", "w_glu", "w_pa", "w_pb", "w_out", "w_up", "w_down")
WEIGHTS = ("g_mix", "w_in", "s5_a_re", "s5_a_im", "s5_log_dt", "s5_b_re", "s5_b_im", "s5_c_re", "s5_c_im", "s5_d",
           "w_glu", "b_glu", "hg_lb_logits", "hg_norm_gain", "w_pa", "w_pb", "w_out", "g_ffn", "w_up", "w_conv",
           "b_conv", "w_down", "g_final")
SMALL = tuple(n for n in WEIGHTS if n not in BIG)
SMALL_PARTS = ("loss", "g_ffn", "g_final", "b_glu", "gain", "lbrow", "s5_d", "w_conv", "b_conv", "lam_re", "lam_im",
               "bb_re", "bb_im", "s5_c_re", "s5_c_im")


def _lower_bound(logits):
    return jnp.cumsum(jax.nn.softmax(logits, axis=0), axis=0)[0:1]


def kernel(x, g_mix, w_in, s5_a_re, s5_a_im, s5_log_dt, s5_b_re, s5_b_im, s5_c_re, s5_c_im, s5_d, w_glu, b_glu, hg_lb_logits, hg_norm_gain, w_pa, w_pb, w_out, g_ffn, w_up, w_conv, b_conv, w_down, g_final, loss_target, m_g_mix, m_w_in, m_s5_a_re, m_s5_a_im, m_s5_log_dt, m_s5_b_re, m_s5_b_im, m_s5_c_re, m_s5_c_im, m_s5_d, m_w_glu, m_b_glu, m_hg_lb_logits, m_hg_norm_gain, m_w_pa, m_w_pb, m_w_out, m_g_ffn, m_w_up, m_w_conv, m_b_conv, m_w_down, m_g_final, v_g_mix, v_w_in, v_s5_a_re, v_s5_a_im, v_s5_log_dt, v_s5_b_re, v_s5_b_im, v_s5_c_re, v_s5_c_im, v_s5_d, v_w_glu, v_b_glu, v_hg_lb_logits, v_hg_norm_gain, v_w_pa, v_w_pb, v_w_out, v_g_ffn, v_w_up, v_w_conv, v_b_conv, v_w_down, v_g_final):
    args = dict(locals())
    w = {n: args[n] for n in WEIGHTS}
    mom = {n: args["m_" + n] for n in WEIGHTS}
    var = {n: args["v_" + n] for n in WEIGHTS}
    nseq, seq, d = x.shape
    xi, yi = lax.axis_index("x"), lax.axis_index("y")
    chip = 2 * xi + yi

    shard = {n: w[n][0] for n in BIG}
    first = [shard["w_in"].astype(BF16), shard["w_glu"].astype(BF16),
             jnp.pad(w_conv[0], ((0, 2 * SUBLANES - CONV_W), (0, 0)))]
    x2 = x.reshape(nseq * seq, d)
    u, z_own, late16, got = _prepare(x2, g_mix, first[0], [shard[n] for n in LATE], _gather_full_rider(first))
    w_in_all, w_glu_all, conv_all = [lax.dynamic_update_index_in_dim(g, s, chip, 0) for g, s in zip(got, first)]
    p = dict(g_mix=g_mix, g_ffn=g_ffn, g_final=g_final.reshape(1, -1), b_glu=b_glu, gain=hg_norm_gain, s5_d=s5_d,
             b_conv=b_conv, lbrow=_lower_bound(hg_lb_logits),
             s5_a_re=s5_a_re[0], s5_a_im=s5_a_im[0], s5_log_dt=s5_log_dt[0], s5_b_re=s5_b_re[0], s5_b_im=s5_b_im[0],
             s5_c_re=s5_c_re[0], s5_c_im=s5_c_im[0], w_in=w_in_all, w_glu=w_glu_all.reshape(-1, w_glu_all.shape[-1]),
             w_conv=conv_all[:, :CONV_W].transpose(1, 0, 2).reshape(CONV_W, -1))

    dx, halves, sm = _local_step(x2, loss_target.reshape(nseq * seq, d), u, z_own, p, dict(zip(LATE, late16)),
                                 nseq=nseq, seq=seq)
    loss = sm["loss"][0, 0]

    grads, delta, new_m, new_v = {}, {}, {}, {}
    for n in BIG:
        shp = shard[n].shape
        grads[n], delta[n], new_m[n], new_v[n] = _adamw_halves("adamw_" + n, shard[n], mom[n].reshape(shp),
                                                               var[n].reshape(shp), *halves[n])

    _, disc_vjp = jax.vjp(_s5_discretize, p["s5_a_re"], p["s5_a_im"], p["s5_log_dt"], p["s5_b_re"], p["s5_b_im"])
    da_re, da_im, dlog_dt, db_re, db_im = disc_vjp((sm["lam_re"], sm["lam_im"], sm["bb_re"], sm["bb_im"]))
    _, lb_vjp = jax.vjp(_lower_bound, hg_lb_logits)
    (dlogits,) = lb_vjp(sm["lbrow"])
    fcols = w_conv.shape[-1]
    grads.update(
        g_mix=sm["g_mix"], g_ffn=sm["g_ffn"], g_final=sm["g_final"].reshape(-1), b_glu=sm["b_glu"],
        hg_norm_gain=sm["gain"], hg_lb_logits=dlogits, s5_d=sm["s5_d"], b_conv=sm["b_conv"],
        w_conv=lax.dynamic_slice_in_dim(sm["w_conv"], chip * fcols, fcols, axis=1),
        s5_a_re=da_re, s5_a_im=da_im, s5_log_dt=dlog_dt, s5_b_re=db_re, s5_b_im=db_im,
        s5_c_re=sm["s5_c_re"], s5_c_im=sm["s5_c_im"])
    grads = {n: grads[n].reshape(w[n].shape) for n in WEIGHTS}

    def natural(a):
        return a.reshape(1, -1) if a.ndim == 1 else (a[0] if a.ndim > 2 else a)

    outs = _adamw_small(*[[natural(src[n]) for n in SMALL] for src in (w, grads, mom, var)])
    for dst, group in zip((delta, new_m, new_v), outs):
        dst.update(zip(SMALL, group))
    res = [loss, dx.reshape(x.shape)]
    for group in (grads, delta, new_m, new_v):
        res += [group[n].reshape(w[n].shape) for n in WEIGHTS]
    return tuple(res)
```

```python
import functools
import math

import jax
import jax.numpy as jnp
from jax import lax
from jax.experimental import pallas as pl
from jax.experimental.pallas import tpu as pltpu

F32 = jnp.float32
BF16 = jnp.bfloat16
MESH = pl.DeviceIdType.MESH

EPS = 1e-6
S5_GROUP = 16
S5_STATE = 64
S5_BLOCK_GROUPS = 8
HEAD = 128
CHUNK = 64
CONV_W = 3
LANES = 128
SUBLANES = 8
GATE_BLOCK = 512
VMEM_LIMIT_BYTES = 56 * 1024 * 1024

ADAM_LR = 0.001
ADAM_B1 = 0.9
ADAM_B2 = 0.999
ADAM_EPS = 1e-08
ADAM_WD = 0.01
ADAM_STEP = 10

N_CHIPS = 4
N_DEV = 8


def _params(*sem):
    return pltpu.CompilerParams(dimension_semantics=sem, vmem_limit_bytes=VMEM_LIMIT_BYTES)


class _Rider:
    def __init__(self, arrays, out_shapes, nsem, start, finish, aliases=None):
        self.arrays, self.out_shapes, self.nsem = list(arrays), list(out_shapes), nsem
        self.start, self.finish, self.aliases = start, finish, dict(aliases or {})


def _hosted_call(name, body, *, grid, in_specs, out_specs, out_shape, operands, scratch_shapes=(), rider=None):
    in_specs, out_specs, out_shape, scratch_shapes = list(in_specs), list(out_specs), list(out_shape), list(scratch_shapes)
    cparams = _params(*(["arbitrary"] * len(grid)))
    if rider is None:
        return pl.pallas_call(body, name=name, grid=grid, in_specs=in_specs, out_specs=out_specs, out_shape=out_shape,
                              scratch_shapes=scratch_shapes, compiler_params=cparams)(*operands)
    n_in, n_out, n_sc = len(in_specs), len(out_specs), len(scratch_shapes)
    r_in, r_out = len(rider.arrays), len(rider.out_shapes)

    def hosted(*refs):
        ins, rins = refs[:n_in], refs[n_in:n_in + r_in]
        outs = refs[n_in + r_in:n_in + r_in + n_out]
        routs = refs[n_in + r_in + n_out:n_in + r_in + n_out + r_out]
        rest = refs[n_in + r_in + n_out + r_out:]
        send_sems, recv_sems = rest[n_sc], rest[n_sc + 1]
        first = functools.reduce(jnp.logical_and, [pl.program_id(i) == 0 for i in range(len(grid))])
        last = functools.reduce(jnp.logical_and, [pl.program_id(i) == grid[i] - 1 for i in range(len(grid))])

        @pl.when(first)
        def _():
            rider.start(rins, routs, send_sems, recv_sems)

        body(*ins, *outs, *rest[:n_sc])

        @pl.when(last)
        def _():
            rider.finish(rins, routs, send_sems, recv_sems)

    res = pl.pallas_call(
        hosted, name=name, grid=grid, in_specs=in_specs + [ANY] * r_in, out_specs=out_specs + [ANY] * r_out,
        out_shape=out_shape + rider.out_shapes,
        scratch_shapes=scratch_shapes + [pltpu.SemaphoreType.DMA((rider.nsem,)), pltpu.SemaphoreType.DMA((rider.nsem,))],
        input_output_aliases={n_in + i: n_out + o for i, o in rider.aliases.items()}, compiler_params=cparams,
    )(*operands, *rider.arrays)
    return res[:n_out], res[n_out:]


def _hosted_scalar_call(name, body, *, grid, in_specs, out_specs, out_shape, operands, rider, aliases=None):
    in_specs, out_specs, out_shape = list(in_specs), list(out_specs), list(out_shape)
    n_in, n_out = len(in_specs), len(out_specs)
    if rider is None:
        spec = pltpu.PrefetchScalarGridSpec(num_scalar_prefetch=1, grid=grid, in_specs=in_specs, out_specs=out_specs)
        res = pl.pallas_call(body, name=name, grid_spec=spec, out_shape=out_shape,
                             input_output_aliases={1 + i: o for i, o in (aliases or {}).items()},
                             compiler_params=_params(*(["arbitrary"] * len(grid))))(_place_scalars(), *operands)
        return res, []
    r_in, r_out = len(rider.arrays), len(rider.out_shapes)

    def hosted(place, *refs):
        ins, rins = refs[:n_in], refs[n_in:n_in + r_in]
        outs = refs[n_in + r_in:n_in + r_in + n_out]
        routs = refs[n_in + r_in + n_out:n_in + r_in + n_out + r_out]
        send_sems, recv_sems = refs[-2], refs[-1]
        first = functools.reduce(jnp.logical_and, [pl.program_id(i) == 0 for i in range(len(grid))])
        last = functools.reduce(jnp.logical_and, [pl.program_id(i) == grid[i] - 1 for i in range(len(grid))])

        @pl.when(first)
        def _():
            rider.start(rins, routs, send_sems, recv_sems)

        body(place, *ins, *outs)

        @pl.when(last)
        def _():
            rider.finish(rins, routs, send_sems, recv_sems)

    spec = pltpu.PrefetchScalarGridSpec(
        num_scalar_prefetch=1, grid=grid, in_specs=in_specs + [ANY] * r_in, out_specs=out_specs + [ANY] * r_out,
        scratch_shapes=[pltpu.SemaphoreType.DMA((rider.nsem,)), pltpu.SemaphoreType.DMA((rider.nsem,))])
    alias = {1 + i: o for i, o in (aliases or {}).items()}
    alias.update({1 + n_in + i: n_out + o for i, o in rider.aliases.items()})
    res = pl.pallas_call(hosted, name=name, grid_spec=spec, out_shape=out_shape + rider.out_shapes,
                         input_output_aliases=alias, compiler_params=_params(*(["arbitrary"] * len(grid))),
                         )(_place_scalars(), *operands, *rider.arrays)
    return res[:n_out], res[n_out:]


def _run_rider(name, rider):
    r_in, r_out = len(rider.arrays), len(rider.out_shapes)

    def body(*refs):
        rins, routs, send_sems, recv_sems = refs[:r_in], refs[r_in:r_in + r_out], refs[-2], refs[-1]
        rider.start(rins, routs, send_sems, recv_sems)
        rider.finish(rins, routs, send_sems, recv_sems)

    return pl.pallas_call(
        body, name=name, in_specs=[ANY] * r_in, out_specs=[ANY] * r_out, out_shape=rider.out_shapes,
        scratch_shapes=[pltpu.SemaphoreType.DMA((rider.nsem,)), pltpu.SemaphoreType.DMA((rider.nsem,))],
        input_output_aliases=rider.aliases,
    )(*rider.arrays)


def _row_tile(rows, cap):
    if rows <= cap:
        return rows
    for t in range(cap - cap % 8, 7, -8):
        if rows % t == 0:
            return t
    raise ValueError(f"no row tile for {rows}")


def _dot(a, b):
    return jnp.dot(a.astype(BF16), b.astype(BF16), preferred_element_type=F32)


def _dot_nt(a, b):
    return lax.dot_general(a.astype(BF16), b.astype(BF16), (((1,), (1,)), ((), ())), preferred_element_type=F32)


def _dot_tn(a, b):
    return lax.dot_general(a.astype(BF16), b.astype(BF16), (((0,), (0,)), ((), ())), preferred_element_type=F32)


def _sigmoid(x):
    return 0.5 * jnp.tanh(0.5 * x) + 0.5


_GELU_C = math.sqrt(2.0 / math.pi)


def _gelu(x):
    return 0.5 * x * (1.0 + jnp.tanh(_GELU_C * (x + 0.044715 * x * x * x)))


def _gelu_grad(x):
    th = jnp.tanh(_GELU_C * (x + 0.044715 * x * x * x))
    return 0.5 * (1.0 + th) + 0.5 * x * (1.0 - th * th) * _GELU_C * (1.0 + 3.0 * 0.044715 * x * x)


def _rowwise(name, fn, ins, outs, accs=(), *, rows, tm, ncol=1, rider=None):
    n_in, n_out = len(ins), len(outs)

    def body(*refs):
        res = fn(*[r[...] for r in refs[:n_in]])
        for r, v in zip(refs[n_in:n_in + n_out], res[:n_out]):
            r[...] = v.astype(r.dtype)
        first = pl.program_id(1) == 0
        for r, v in zip(refs[n_in + n_out:], res[n_out:]):
            @pl.when(first)
            def _():
                r[...] = v

            @pl.when(jnp.logical_not(first))
            def _():
                r[...] += v

    in_specs = []
    for _, width, base, kind in ins:
        if kind == "row":
            in_specs.append(pl.BlockSpec((tm, width), lambda j, i, b=base: (i, b + j)))
        else:
            in_specs.append(pl.BlockSpec((1, width), lambda j, i, b=base: (0, b + j)))
    out_specs = [pl.BlockSpec((tm, width), lambda j, i: (i, j)) for _, width, _ in outs]
    out_specs += [pl.BlockSpec((1, width), lambda j, i: (0, j)) for _, width in accs]
    out_shape = [jax.ShapeDtypeStruct((rows, total), dt) for total, _, dt in outs]
    out_shape += [jax.ShapeDtypeStruct((1, total), F32) for total, _ in accs]
    return _hosted_call(name, body, grid=(ncol, rows // tm), in_specs=in_specs, out_specs=out_specs, out_shape=out_shape,
                        operands=[a for a, _, _, _ in ins], rider=rider)


def _mm(name, a, b, *, mode, grid, a_spec, b_spec, o_spec, out_shape, acc_shape, res=None, res_spec=None,
        pair_axis=None, rider=None, epilogue=None):
    nk = grid[2]
    dot = {"nn": _dot, "nt": _dot_nt, "tn": _dot_tn}[mode]
    a_list = list(a) if isinstance(a, tuple) else [a]
    b_list = list(b) if isinstance(b, tuple) else [b]
    na, nb = len(a_list), len(b_list)
    assert (pair_axis is None) == (na + nb == 2)
    direct = nk == 1 and pair_axis is None
    epi_fn, epi_ins, epi_sums = epilogue if epilogue is not None else (None, [], [])
    n_res = 0 if res is None else 1
    n_epi = len(epi_ins)

    def body(*refs):
        a_refs, b_refs = refs[:na], refs[na:na + nb]
        r_ref = None if res is None else refs[na + nb]
        e_refs = refs[na + nb + n_res:na + nb + n_res + n_epi]
        o_ref = refs[na + nb + n_res + n_epi]
        s_refs = refs[na + nb + n_res + n_epi + 1:na + nb + n_res + n_epi + 1 + len(epi_sums)]
        first_rows = pl.program_id(0) == 0

        def finish(v):
            if res is not None:
                v = v + r_ref[...]
            if epi_fn is None:
                o_ref[...] = v.astype(o_ref.dtype)
                return
            outs = epi_fn(v, *[r[...] for r in e_refs])
            o_ref[...] = outs[0].astype(o_ref.dtype)
            for s_ref, part in zip(s_refs, outs[1:]):
                @pl.when(first_rows)
                def _():
                    s_ref[...] = part

                @pl.when(jnp.logical_not(first_rows))
                def _():
                    s_ref[...] += part

        if direct:
            finish(dot(a_refs[0][...], b_refs[0][...]))
            return
        acc_ref = refs[-1]
        k = pl.program_id(2)

        @pl.when(k == 0)
        def _():
            acc_ref[...] = jnp.zeros_like(acc_ref)

        if pair_axis is None:
            acc_ref[...] += dot(a_refs[0][...], b_refs[0][...])
        else:
            lower = pl.program_id(pair_axis) < grid[pair_axis] // 2

            @pl.when(lower)
            def _():
                acc_ref[...] += dot(a_refs[0][...], b_refs[0][...])

            @pl.when(jnp.logical_not(lower))
            def _():
                acc_ref[...] += dot(a_refs[-1][...], b_refs[-1][...])

        @pl.when(k == nk - 1)
        def _():
            finish(acc_ref[...])

    operands = a_list + b_list + ([] if res is None else [res]) + [arr for arr, _ in epi_ins]
    in_specs = (list(a_spec) if na == 2 else [a_spec]) + (list(b_spec) if nb == 2 else [b_spec])
    in_specs += ([] if res is None else [res_spec]) + [spec for _, spec in epi_ins]
    out_specs = [o_spec] + [pl.BlockSpec((1, c), lambda *_: (0, 0)) for c in epi_sums]
    out_shapes = [out_shape] + [jax.ShapeDtypeStruct((1, c), F32) for c in epi_sums]
    got = _hosted_call(name, body, grid=grid, in_specs=in_specs, out_specs=out_specs, out_shape=out_shapes,
                       scratch_shapes=[] if direct else [pltpu.VMEM(acc_shape, F32)], operands=operands, rider=rider)
    mine, rider_outs = (got, None) if rider is None else got
    mine = mine[0] if epilogue is None else tuple(mine)
    return mine if rider is None else (mine, rider_outs)


MM_TILE_BUDGET_BYTES = 36 * 1024 * 1024
MM_TILE_CAP = 2048
ROW_TILE = 1024
GLU_TILE = 1024
ADAMW_TILE = 256


def _mm_tile(t, row_bytes, fixed_bytes):
    cap = max(16, min(MM_TILE_CAP, (MM_TILE_BUDGET_BYTES - fixed_bytes) // row_bytes))
    return _row_tile(t, cap - cap % 16)


def _size(a):
    return jnp.dtype(a.dtype).itemsize


def _mm_fwd_cols(name, a, w3, out_dtype=F32, rider=None):
    t, k = a.shape
    ns = w3.shape[2]
    tm = _mm_tile(t, 2 * k * _size(a) + 2 * ns * jnp.dtype(out_dtype).itemsize, 2 * k * ns * _size(w3))
    return _mm(name, a, w3, mode="nn", grid=(N_CHIPS, t // tm, 1),
               a_spec=pl.BlockSpec((tm, k), lambda j, i, kk: (i, 0)),
               b_spec=pl.BlockSpec((None, k, ns), lambda j, i, kk: (j, 0, 0)),
               o_spec=pl.BlockSpec((tm, ns), lambda j, i, kk: (i, j)),
               out_shape=jax.ShapeDtypeStruct((t, N_CHIPS * ns), out_dtype), acc_shape=(tm, ns), rider=rider)


def _mm_bwd_cols(name, d, w3, out_dtype=F32, rider=None, epilogue=None):
    pair = isinstance(d, tuple)
    t = d[0].shape[0] if pair else d.shape[0]
    k, ns = w3.shape[1], w3.shape[2]
    dsize = _size(d[0] if pair else d)
    tm = _mm_tile(t, (4 if pair else 2) * ns * dsize + 2 * k * jnp.dtype(out_dtype).itemsize + 4 * k
                  + _row_epilogue(epilogue, 8)[1], 2 * k * ns * _size(w3))
    half = N_CHIPS // 2
    if pair:
        a_spec = (pl.BlockSpec((tm, ns), lambda i, j, kk: (i, jnp.minimum(kk, half - 1))),
                  pl.BlockSpec((tm, ns), lambda i, j, kk: (i, jnp.maximum(kk - half, 0))))
    else:
        a_spec = pl.BlockSpec((tm, ns), lambda i, j, kk: (i, kk))
    return _mm(name, d, w3, mode="nt", grid=(t // tm, 1, N_CHIPS), a_spec=a_spec,
               b_spec=pl.BlockSpec((None, k, ns), lambda i, j, kk: (kk, 0, 0)),
               o_spec=pl.BlockSpec((tm, k), lambda i, j, kk: (i, 0)),
               out_shape=jax.ShapeDtypeStruct((t, k), out_dtype), acc_shape=(tm, k), pair_axis=2 if pair else None,
               rider=rider, epilogue=_row_epilogue(epilogue, tm)[0])


def _mm_wgrad_cols(name, a, d, rider=None):
    pair = isinstance(d, tuple)
    t, k = a.shape
    ns = (2 * d[0].shape[1] if pair else d.shape[1]) // N_CHIPS
    dsize = _size(d[0] if pair else d)
    tk = _mm_tile(t, 2 * k * _size(a) + (4 if pair else 2) * ns * dsize, k * ns * (4 + 2 * 2))
    half = N_CHIPS // 2
    if pair:
        b_spec = (pl.BlockSpec((tk, ns), lambda j, i, kk: (jnp.where(j < half, kk, 0), jnp.minimum(j, half - 1))),
                  pl.BlockSpec((tk, ns), lambda j, i, kk: (jnp.where(j < half, 0, kk), jnp.maximum(j - half, 0))))
    else:
        b_spec = pl.BlockSpec((tk, ns), lambda j, i, kk: (kk, j))
    return _mm(name, a, d, mode="tn", grid=(N_CHIPS, 1, t // tk),
               a_spec=pl.BlockSpec((tk, k), lambda j, i, kk: (kk, 0)), b_spec=b_spec,
               o_spec=pl.BlockSpec((None, k, ns), lambda j, i, kk: (j, 0, 0)),
               out_shape=jax.ShapeDtypeStruct((N_CHIPS, k, ns), BF16), acc_shape=(k, ns),
               pair_axis=0 if pair else None, rider=rider)


MM_BLOCK_CAP = 1408


def _row_epilogue(epilogue, tm):
    if epilogue is None:
        return None, 0
    fn, arrays, sums = epilogue
    specs = [pl.BlockSpec((1, x.shape[1]), lambda i, j, kk: (0, 0)) if x.shape[0] == 1 else
             pl.BlockSpec((tm, x.shape[1]), lambda i, j, kk: (i, 0)) for x in arrays]
    return (fn, list(zip(arrays, specs)), list(sums)), sum(2 * x.shape[1] * _size(x) for x in arrays if x.shape[0] > 1)


def _mm_fwd_rows(name, a, w, res=None, out_dtype=F32, epilogue=None, rider=None):
    t, k = a.shape
    n = w.shape[1]
    tk = k if k <= MM_BLOCK_CAP else MM_BLOCK_CAP
    assert k % tk == 0
    row_bytes = 2 * tk * _size(a) + 2 * n * jnp.dtype(out_dtype).itemsize + (0 if res is None else 2 * n * 4) + 4 * n
    row_bytes += _row_epilogue(epilogue, 8)[1]
    tm = _mm_tile(t, row_bytes, 2 * tk * n * _size(w))
    return _mm(name, a, w, mode="nn", grid=(t // tm, 1, k // tk),
               a_spec=pl.BlockSpec((tm, tk), lambda i, j, kk: (i, kk)),
               b_spec=pl.BlockSpec((tk, n), lambda i, j, kk: (kk, 0)),
               o_spec=pl.BlockSpec((tm, n), lambda i, j, kk: (i, 0)),
               out_shape=jax.ShapeDtypeStruct((t, n), out_dtype), acc_shape=(tm, n),
               res=res, res_spec=None if res is None else pl.BlockSpec((tm, n), lambda i, j, kk: (i, 0)),
               epilogue=_row_epilogue(epilogue, tm)[0], rider=rider)


def _mm_bwd_rows(name, d, w, out_dtype=F32):
    t, n = d.shape
    k = w.shape[0]
    tn = k if k <= MM_BLOCK_CAP else MM_BLOCK_CAP
    assert k % tn == 0
    tm = _mm_tile(t, 2 * n * _size(d) + 2 * tn * jnp.dtype(out_dtype).itemsize, 2 * tn * n * _size(w))
    return _mm(name, d, w, mode="nt", grid=(t // tm, k // tn, 1),
               a_spec=pl.BlockSpec((tm, n), lambda i, j, kk: (i, 0)),
               b_spec=pl.BlockSpec((tn, n), lambda i, j, kk: (j, 0)),
               o_spec=pl.BlockSpec((tm, tn), lambda i, j, kk: (i, j)),
               out_shape=jax.ShapeDtypeStruct((t, k), out_dtype), acc_shape=(tm, tn))


def _mm_wgrad_rows(name, a, d):
    t, k = a.shape
    n = d.shape[1]
    nblk = next(b for b in (1, 2, 4) if (k // b) % LANES == 0 and k // b <= MM_BLOCK_CAP)
    ks = k // nblk
    tk = _mm_tile(t, 2 * ks * _size(a) + 2 * n * _size(d), ks * n * (4 + 2 * 2))
    return _mm(name, a, d, mode="tn", grid=(nblk, 1, t // tk),
               a_spec=pl.BlockSpec((tk, ks), lambda j, i, kk: (kk, j)),
               b_spec=pl.BlockSpec((tk, n), lambda j, i, kk: (kk, 0)),
               o_spec=pl.BlockSpec((ks, n), lambda j, i, kk: (j, 0)),
               out_shape=jax.ShapeDtypeStruct((k, n), BF16), acc_shape=(ks, n))


def _s5_discretize(a_re, a_im, log_dt, b_re, b_im):
    dt = jnp.exp(log_dt)[:, None]
    mag = jnp.exp(a_re * dt)
    ang = a_im * dt
    lb_re = mag * jnp.cos(ang)
    lb_im = mag * jnp.sin(ang)
    den = a_re * a_re + a_im * a_im
    n_re = lb_re - 1.0
    n_im = lb_im
    co_re = ((n_re * a_re + n_im * a_im) / den)[..., None]
    co_im = ((n_im * a_re - n_re * a_im) / den)[..., None]
    bb_re = co_re * b_re - co_im * b_im
    bb_im = co_re * b_im + co_im * b_re
    return lb_re, lb_im, bb_re, bb_im


def _s5_in_blocks(bb):
    g = bb.shape[0]
    nb = g // S5_BLOCK_GROUPS
    t = bb.reshape(nb, S5_BLOCK_GROUPS, S5_STATE, S5_GROUP).transpose(0, 1, 3, 2)
    eye = jnp.eye(S5_BLOCK_GROUPS, dtype=bb.dtype)
    full = t[:, :, :, None, :] * eye[None, :, None, :, None]
    return full.reshape(nb, S5_BLOCK_GROUPS * S5_GROUP, S5_BLOCK_GROUPS * S5_STATE)


def _s5_in_blocks_diag(blocks):
    nb = blocks.shape[0]
    t = blocks.reshape(nb, S5_BLOCK_GROUPS, S5_GROUP, S5_BLOCK_GROUPS, S5_STATE)
    d = jnp.einsum("bghgp->bghp", t)
    return d.transpose(0, 1, 3, 2).reshape(nb * S5_BLOCK_GROUPS, S5_STATE, S5_GROUP)


def _s5_out_blocks(c):
    g = c.shape[0]
    nb = g // S5_BLOCK_GROUPS
    t = c.reshape(nb, S5_BLOCK_GROUPS, S5_GROUP, S5_STATE).transpose(0, 1, 3, 2)
    eye = jnp.eye(S5_BLOCK_GROUPS, dtype=c.dtype)
    full = t[:, :, :, None, :] * eye[None, :, None, :, None]
    return full.reshape(nb, S5_BLOCK_GROUPS * S5_STATE, S5_BLOCK_GROUPS * S5_GROUP)


def _s5_out_blocks_diag(blocks):
    nb = blocks.shape[0]
    t = blocks.reshape(nb, S5_BLOCK_GROUPS, S5_STATE, S5_BLOCK_GROUPS, S5_GROUP)
    d = jnp.einsum("bgpgh->bgph", t)
    return d.transpose(0, 1, 3, 2).reshape(nb * S5_BLOCK_GROUPS, S5_GROUP, S5_STATE)


def _s5_scan_tables(lr, li, reverse):
    def cmul(a, b):
        return a[0] * b[0] - a[1] * b[1], a[0] * b[1] + a[1] * b[0]

    lam = (lr, -li) if reverse else (lr, li)
    pw = [lam]
    for _ in range(SUBLANES - 1):
        pw.append(cmul(pw[-1], lam))
    sub = jnp.arange(SUBLANES)[:, None]
    rows = []
    for s in (1, 2, 4):
        keep = (sub <= SUBLANES - 1 - s) if reverse else (sub >= s)
        rows.append(jnp.where(keep, pw[s - 1][0][None, :], 0.0))
        rows.append(jnp.where(keep, pw[s - 1][1][None, :], 0.0))
    order = list(range(SUBLANES - 1, -1, -1)) if reverse else list(range(SUBLANES))
    rows.append(jnp.stack([pw[i][0] for i in order]))
    rows.append(jnp.stack([pw[i][1] for i in order]))
    return jnp.concatenate(rows, axis=0)


def _s5_scan(vre_ref, vim_ref, coef_ref, seq, width, reverse, xre_ref=None, xim_ref=None):
    nt = seq // SUBLANES
    nl = width // LANES
    per = 2 if xre_ref is None else 4
    sub = lax.broadcasted_iota(jnp.int32, (SUBLANES, LANES), 0)

    def step(k, carry):
        kk = (nt - 1 - k) if reverse else k
        rows = pl.ds(pl.multiple_of(kk * SUBLANES, SUBLANES), SUBLANES)
        out = []
        for j in range(nl):
            lanes = slice(j * LANES, (j + 1) * LANES)
            co = [coef_ref[SUBLANES * q:SUBLANES * (q + 1), lanes] for q in range(8)]
            cr, ci = carry[per * j], carry[per * j + 1]
            vr = vre_ref[rows, lanes]
            vi = vim_ref[rows, lanes]
            for q, s in enumerate((1, 2, 4)):
                sh = SUBLANES - s if reverse else s
                rr = pltpu.roll(vr, sh, 0)
                ri = pltpu.roll(vi, sh, 0)
                ar, ai = co[2 * q], co[2 * q + 1]
                vr, vi = vr + ar * rr - ai * ri, vi + ar * ri + ai * rr
            edge = 0 if reverse else SUBLANES - 1
            cbr = jnp.broadcast_to(cr[edge:edge + 1, :], (SUBLANES, LANES))
            cbi = jnp.broadcast_to(ci[edge:edge + 1, :], (SUBLANES, LANES))
            pr, pi = co[6], co[7]
            vr, vi = vr + pr * cbr - pi * cbi, vi + pr * cbi + pi * cbr
            vre_ref[rows, lanes] = vr
            vim_ref[rows, lanes] = vi
            out += [vr, vi]
            if xre_ref is not None:
                nr = jnp.where(sub == SUBLANES - 1, cbr, pltpu.roll(vr, SUBLANES - 1, 0))
                ni = jnp.where(sub == SUBLANES - 1, cbi, pltpu.roll(vi, SUBLANES - 1, 0))
                xr = xre_ref[rows, lanes]
                xi = xim_ref[rows, lanes]
                out += [carry[per * j + 2] + nr * xr + ni * xi, carry[per * j + 3] + ni * xr - nr * xi]
        return tuple(out)

    zero = jnp.zeros((SUBLANES, LANES), F32)
    res = lax.fori_loop(0, nt, step, (zero,) * (per * nl))
    if xre_ref is None:
        return None
    return jnp.concatenate(
        [jnp.concatenate([jnp.sum(res[per * j + 2], axis=0, keepdims=True) for j in range(nl)], axis=1),
         jnp.concatenate([jnp.sum(res[per * j + 3], axis=0, keepdims=True) for j in range(nl)], axis=1)], axis=0)


def _s5_fwd(z, bre3, bim3, cre3, cim3, coef, dskip, *, nseq, seq, rider=None):
    nb = bre3.shape[0]
    ch, ns = bre3.shape[1], bre3.shape[2]

    def body(za_ref, bre_ref, bim_ref, cre_ref, cim_ref, coef_ref, d_ref, y_ref, xre_ref, xim_ref):
        za = za_ref[...]
        xre_ref[...] = _dot(za, bre_ref[...])
        xim_ref[...] = _dot(za, bim_ref[...])
        _s5_scan(xre_ref, xim_ref, coef_ref, seq, ns, False)
        y_ref[...] = _dot(xre_ref[...], cre_ref[...]) - _dot(xim_ref[...], cim_ref[...]) + d_ref[...] * za

    blk3 = lambda r, c: pl.BlockSpec((None, r, c), lambda b, j: (j, 0, 0))
    return _hosted_call(
        "s5_fwd", body, grid=(nseq, nb),
        in_specs=[pl.BlockSpec((seq, ch), lambda b, j: (b, j)), blk3(ch, ns), blk3(ch, ns), blk3(ns, ch), blk3(ns, ch),
                  pl.BlockSpec((8 * SUBLANES, ns), lambda b, j: (0, j)), pl.BlockSpec((1, ch), lambda b, j: (0, j))],
        out_specs=[pl.BlockSpec((seq, ch), lambda b, j: (b, j)), pl.BlockSpec((seq, ns), lambda b, j: (b, j)),
                   pl.BlockSpec((seq, ns), lambda b, j: (b, j))],
        out_shape=[jax.ShapeDtypeStruct((nseq * seq, nb * ch), F32), jax.ShapeDtypeStruct((nseq * seq, nb * ns), F32),
                   jax.ShapeDtypeStruct((nseq * seq, nb * ns), F32)],
        operands=(z, bre3, bim3, cre3, cim3, coef, dskip), rider=rider)


def _s5_bwd(dy, z, xre, xim, bre3, bim3, cre3, cim3, coef_rev, dskip, *, nseq, seq, rider=None):
    nb = bre3.shape[0]
    ch, ns = bre3.shape[1], bre3.shape[2]

    def body(dy_ref, za_ref, xre_ref, xim_ref, bre_ref, bim_ref, cre_ref, cim_ref, coef_ref, d_ref,
             dza_ref, dbre_ref, dbim_ref, dcre_ref, dcim_ref, dlam_ref, dd_ref, are_ref, aim_ref):
        dy = dy_ref[...]
        za = za_ref[...]
        are_ref[...] = _dot_nt(dy, cre_ref[...])
        aim_ref[...] = -_dot_nt(dy, cim_ref[...])
        dlam = _s5_scan(are_ref, aim_ref, coef_ref, seq, ns, True, xre_ref, xim_ref)
        are = are_ref[...]
        aim = aim_ref[...]
        dza_ref[...] = (_dot_nt(are, bre_ref[...]) + _dot_nt(aim, bim_ref[...]) + d_ref[...] * dy).astype(dza_ref.dtype)
        parts = (_dot_tn(za, are), _dot_tn(za, aim), _dot_tn(xre_ref[...], dy), -_dot_tn(xim_ref[...], dy),
                 dlam, jnp.sum(dy * za, axis=0, keepdims=True))
        first = pl.program_id(1) == 0
        for r, v in zip((dbre_ref, dbim_ref, dcre_ref, dcim_ref, dlam_ref, dd_ref), parts):
            @pl.when(first)
            def _():
                r[...] = v

            @pl.when(jnp.logical_not(first))
            def _():
                r[...] += v

    blk3 = lambda r, c: pl.BlockSpec((None, r, c), lambda j, b: (j, 0, 0))
    tok = lambda c: pl.BlockSpec((seq, c), lambda j, b: (b, j))
    return _hosted_call(
        "s5_bwd", body, grid=(nb, nseq),
        in_specs=[tok(ch), tok(ch), tok(ns), tok(ns), blk3(ch, ns), blk3(ch, ns), blk3(ns, ch), blk3(ns, ch),
                  pl.BlockSpec((8 * SUBLANES, ns), lambda j, b: (0, j)), pl.BlockSpec((1, ch), lambda j, b: (0, j))],
        out_specs=[tok(ch), blk3(ch, ns), blk3(ch, ns), blk3(ns, ch), blk3(ns, ch),
                   pl.BlockSpec((None, 2, ns), lambda j, b: (j, 0, 0)), pl.BlockSpec((1, ch), lambda j, b: (0, j))],
        out_shape=[jax.ShapeDtypeStruct((nseq * seq, nb * ch), BF16),
                   jax.ShapeDtypeStruct((nb, ch, ns), F32), jax.ShapeDtypeStruct((nb, ch, ns), F32),
                   jax.ShapeDtypeStruct((nb, ns, ch), F32), jax.ShapeDtypeStruct((nb, ns, ch), F32),
                   jax.ShapeDtypeStruct((nb, 2, ns), F32), jax.ShapeDtypeStruct((1, nb * ch), F32)],
        scratch_shapes=[pltpu.VMEM((seq, ns), F32), pltpu.VMEM((seq, ns), F32)],
        operands=(dy, z, xre, xim, bre3, bim3, cre3, cim3, coef_rev, dskip), rider=rider)


def _glu_fwd(y, wglu, bglu):
    t, w = y.shape
    tm = _row_tile(t, GLU_TILE)

    def body(y_ref, w_ref, b_ref, a0_ref, gl_ref, a_ref):
        a0 = _gelu(y_ref[...])
        gl = _dot(a0, w_ref[...])
        a0_ref[...] = a0.astype(a0_ref.dtype)
        gl_ref[...] = gl
        a_ref[...] = (a0 * _sigmoid(gl + b_ref[...])).astype(a_ref.dtype)

    tok = pl.BlockSpec((tm, w), lambda i: (i, 0))
    return pl.pallas_call(
        body, name="s5_glu", grid=(t // tm,),
        in_specs=[tok, pl.BlockSpec((w, w), lambda i: (0, 0)), pl.BlockSpec((1, w), lambda i: (0, 0))],
        out_specs=[tok, tok, tok],
        out_shape=[jax.ShapeDtypeStruct((t, w), BF16), jax.ShapeDtypeStruct((t, w), F32), jax.ShapeDtypeStruct((t, w), BF16)],
        compiler_params=_params("arbitrary"),
    )(y, wglu, bglu)


def _glu_bwd(y, gl, bglu, da, wglu):
    t, w = y.shape
    tm = _row_tile(t, GLU_TILE)

    def body(y_ref, gl_ref, b_ref, da_ref, w_ref, dgl_ref, dy_ref, db_ref):
        yv = y_ref[...]
        dav = da_ref[...]
        s = _sigmoid(gl_ref[...] + b_ref[...])
        dgl = dav * _gelu(yv) * s * (1.0 - s)
        dgl_ref[...] = dgl.astype(dgl_ref.dtype)
        dy_ref[...] = (dav * s + _dot_nt(dgl, w_ref[...])) * _gelu_grad(yv)
        part = jnp.sum(dgl, axis=0, keepdims=True)
        first = pl.program_id(0) == 0

        @pl.when(first)
        def _():
            db_ref[...] = part

        @pl.when(jnp.logical_not(first))
        def _():
            db_ref[...] += part

    tok = pl.BlockSpec((tm, w), lambda i: (i, 0))
    vec = pl.BlockSpec((1, w), lambda i: (0, 0))
    return pl.pallas_call(
        body, name="s5_glu_bwd", grid=(t // tm,),
        in_specs=[tok, tok, vec, tok, pl.BlockSpec((w, w), lambda i: (0, 0))], out_specs=[tok, tok, vec],
        out_shape=[jax.ShapeDtypeStruct((t, w), BF16), jax.ShapeDtypeStruct((t, w), F32), jax.ShapeDtypeStruct((1, w), F32)],
        compiler_params=_params("arbitrary"),
    )(y, gl, bglu, da, wglu)


def _cumsum_rows(x, reverse=False):
    n = x.shape[0]
    row = lax.broadcasted_iota(jnp.int32, x.shape, 0)
    s = 1
    while s < n:
        if reverse:
            x = x + jnp.where(row < n - s, pltpu.roll(x, n - s, 0), 0.0)
        else:
            x = x + jnp.where(row >= s, pltpu.roll(x, s, 0), 0.0)
        s *= 2
    return x


def _hg_gates(zq, zf, lb):
    sg = _sigmoid(zf)
    f = lb + (1.0 - lb) * sg
    sq = _sigmoid(zq)
    qa = zq * sq * (HEAD ** -0.5)
    b = _cumsum_rows(jnp.log(f))
    return sg, f, sq, qa, 1.0 - f, b


SUB = 16


def _hg_scores(qa, kk, b):
    c = qa.shape[0]
    row = lax.broadcasted_iota(jnp.int32, qa.shape, 0)
    pos = jnp.bitwise_and(row, SUB - 1)
    dmat = lax.broadcasted_iota(jnp.int32, (c, c), 0) - lax.broadcasted_iota(jnp.int32, (c, c), 1)
    p = jnp.zeros((c, c), F32)
    for d in range(SUB):
        if d == 0:
            fd = qa * kk
        else:
            e = jnp.exp(jnp.minimum(b - pltpu.roll(b, d, 0), 0.0))
            fd = jnp.where(pos >= d, qa * pltpu.roll(kk, d, 0) * e, 0.0)
        p = jnp.where(dmat == d, jnp.sum(fd, axis=1, keepdims=True), p)
    col = lax.broadcasted_iota(jnp.int32, (SUB, c), 1)
    blocks = [jnp.zeros((SUB, c), F32)]
    for r0 in range(SUB, c, SUB):
        beta = b[r0 - 1:r0, :]
        qt = qa[r0:r0 + SUB] * jnp.exp(b[r0:r0 + SUB] - beta)
        kt = kk * jnp.exp(jnp.minimum(beta - b, 0.0))
        blocks.append(jnp.where(col < r0, _dot_nt(qt, kt), 0.0))
    return p + jnp.concatenate(blocks, axis=0)


def _hg_scores_bwd(dp, qa, kk, b):
    c = qa.shape[0]
    row = lax.broadcasted_iota(jnp.int32, qa.shape, 0)
    pos = jnp.bitwise_and(row, SUB - 1)
    dmat = lax.broadcasted_iota(jnp.int32, (c, c), 0) - lax.broadcasted_iota(jnp.int32, (c, c), 1)
    parts = []
    for r0 in range(0, c, SUB):
        rows = slice(r0, r0 + SUB)
        qg, kg, bg, dpg, dmg, pg = qa[rows], kk[rows], b[rows], dp[rows], dmat[rows], pos[rows]
        dqg = jnp.zeros_like(qg)
        dkg = jnp.zeros_like(qg)
        dbg = jnp.zeros_like(qg)
        for d in range(SUB):
            dcol = jnp.sum(jnp.where(dmg == d, dpg, 0.0), axis=1, keepdims=True)
            if d == 0:
                dqg = dqg + dcol * kg
                dkg = dkg + dcol * qg
            else:
                e = jnp.exp(jnp.minimum(bg - pltpu.roll(bg, d, 0), 0.0))
                w = jnp.where(pg >= d, dcol * e, 0.0)
                kr = pltpu.roll(kg, d, 0)
                dqg = dqg + w * kr
                tmp = w * qg
                dkg = dkg + pltpu.roll(tmp, SUB - d, 0)
                x = tmp * kr
                dbg = dbg + x - pltpu.roll(x, SUB - d, 0)
        parts.append((dqg, dkg, dbg))
    dqa, dkk, db = [jnp.concatenate([pt[i] for pt in parts], axis=0) for i in range(3)]
    col = lax.broadcasted_iota(jnp.int32, (SUB, c), 1)
    dq_blocks = [jnp.zeros((SUB, qa.shape[1]), F32)]
    db_blocks = [jnp.zeros((SUB, qa.shape[1]), F32)]
    for r0 in range(SUB, c, SUB):
        beta = b[r0 - 1:r0, :]
        eq = jnp.exp(b[r0:r0 + SUB] - beta)
        ek = jnp.exp(jnp.minimum(beta - b, 0.0))
        qt = qa[r0:r0 + SUB] * eq
        kt = kk * ek
        dpi = jnp.where(col < r0, dp[r0:r0 + SUB, :], 0.0)
        dqt = _dot(dpi, kt)
        dkt = _dot_tn(dpi, qt)
        dq_blocks.append(dqt * eq)
        db_blocks.append(dqt * qt)
        dkk = dkk + dkt * ek
        db = db - dkt * kt
    return dqa + jnp.concatenate(dq_blocks, axis=0), dkk, db + jnp.concatenate(db_blocks, axis=0)


def _hg_chunks_per_step(seq):
    nc = seq // CHUNK
    cps = next(k for k in (16, 8, 4, 2, 1) if nc % k == 0)
    return nc, cps, nc // cps


def _hg_fwd(z, lbrow, gain, *, nseq, seq, heads, qoff, rider=None):
    nc, cps, nblk = _hg_chunks_per_step(seq)
    blk = cps * CHUNK
    zspec = lambda off: pl.BlockSpec((blk, HEAD), lambda h, b, n, off=off: (b * nblk + n, off + h))

    def body(zq_ref, zf_ref, zi_ref, zg_ref, lb_ref, gn_ref, o_ref, yb_ref, st_ref, sc_ref, state):
        @pl.when(pl.program_id(2) == 0)
        def _():
            state[...] = jnp.zeros_like(state)

        lb = lb_ref[...]
        gain_v = gn_ref[...]

        def chunk(ci, carry):
            rows = pl.ds(pl.multiple_of(ci * CHUNK, CHUNK), CHUNK)
            st = state[...]
            st_ref[ci] = st
            zi = zi_ref[rows, :]
            zg = zg_ref[rows, :]
            _, _, _, qa, kk, b = _hg_gates(zq_ref[rows, :], zf_ref[rows, :], lb)
            scores = _hg_scores(qa, kk, b).astype(BF16)
            sc_ref[rows, :] = scores
            o = _dot_nt(qa * jnp.exp(b), st) + _dot(scores, zi)
            bl = b[CHUNK - 1:CHUNK, :]
            state[...] = st * jnp.exp(bl) + _dot_tn(zi, kk * jnp.exp(bl - b))
            o_ref[rows, :] = o
            r = lax.rsqrt(jnp.mean(o * o, axis=1, keepdims=True) + EPS)
            yb_ref[rows, :] = (o * r * gain_v * zg * _sigmoid(zg)).astype(yb_ref.dtype)
            return carry

        lax.fori_loop(0, cps, chunk, 0, unroll=True)

    tok = pl.BlockSpec((blk, HEAD), lambda h, b, n: (b * nblk + n, h))
    vec = pl.BlockSpec((1, HEAD), lambda h, b, n: (0, h))
    rows = nseq * seq
    return _hosted_call(
        "hgrn2_fwd", body, grid=(heads, nseq, nblk),
        in_specs=[zspec(qoff), zspec(qoff + heads), zspec(qoff + 2 * heads), zspec(qoff + 3 * heads), vec, vec],
        out_specs=[tok, tok, pl.BlockSpec((None, None, cps, HEAD, HEAD), lambda h, b, n: (h, b, n, 0, 0)),
                   pl.BlockSpec((None, blk, CHUNK), lambda h, b, n: (h, b * nblk + n, 0))],
        out_shape=[jax.ShapeDtypeStruct((rows, heads * HEAD), F32), jax.ShapeDtypeStruct((rows, heads * HEAD), BF16),
                   jax.ShapeDtypeStruct((heads, nseq, nc, HEAD, HEAD), F32),
                   jax.ShapeDtypeStruct((heads, rows, CHUNK), BF16)],
        scratch_shapes=[pltpu.VMEM((HEAD, HEAD), F32)], operands=(z, z, z, z, lbrow, gain), rider=rider)


def _hg_bwd(dyb, z, o, states, scores, lbrow, gain, *, nseq, seq, heads, qoff, rider=None):
    nc, cps, nblk = _hg_chunks_per_step(seq)
    blk = cps * CHUNK
    rev = lambda n: nblk - 1 - n
    zspec = lambda off: pl.BlockSpec((blk, HEAD), lambda h, b, n, off=off: (b * nblk + rev(n), off + h))

    def body(dyb_ref, zq_ref, zf_ref, zi_ref, zg_ref, o_ref, st_ref, sc_ref, lb_ref, gn_ref,
             dzq_ref, dzf_ref, dzi_ref, dzg_ref, dlb_ref, dgn_ref, dstate):
        @pl.when(pl.program_id(2) == 0)
        def _():
            dstate[...] = jnp.zeros_like(dstate)

        @pl.when(jnp.logical_and(pl.program_id(1) == 0, pl.program_id(2) == 0))
        def _():
            dlb_ref[...] = jnp.zeros_like(dlb_ref)
            dgn_ref[...] = jnp.zeros_like(dgn_ref)

        lb = lb_ref[...]
        gain_v = gn_ref[...]
        c = CHUNK
        causal = lax.broadcasted_iota(jnp.int32, (c, c), 0) >= lax.broadcasted_iota(jnp.int32, (c, c), 1)

        def chunk(step, carry):
            ci = cps - 1 - step
            rows = pl.ds(pl.multiple_of(ci * CHUNK, CHUNK), CHUNK)
            zq = zq_ref[rows, :]
            zi = zi_ref[rows, :]
            zg = zg_ref[rows, :]
            sg, f, sq, qa, kk, b = _hg_gates(zq, zf_ref[rows, :], lb)
            eb = jnp.exp(b)
            qt = qa * eb
            bl = b[c - 1:c, :]
            ebl = jnp.exp(bl)
            ekb = jnp.exp(bl - b)
            kh = kk * ekb
            st = st_ref[ci]
            dst = dstate[...]
            o = o_ref[rows, :]
            r = lax.rsqrt(jnp.mean(o * o, axis=1, keepdims=True) + EPS)
            oh = o * r
            szg = _sigmoid(zg)
            dyb = dyb_ref[rows, :]
            don = dyb * zg * szg
            dzg_ref[rows, :] = (dyb * oh * gain_v * szg * (1.0 + zg * (1.0 - szg))).astype(dzg_ref.dtype)
            doh = don * gain_v
            do = r * (doh - oh * jnp.mean(doh * oh, axis=1, keepdims=True))
            dqt = _dot(do, st)
            dp = jnp.where(causal, _dot_nt(do, zi), 0.0)
            dzi_ref[rows, :] = (_dot_tn(sc_ref[rows, :], do) + _dot_nt(kh, dst)).astype(dzi_ref.dtype)
            dkh = _dot(zi, dst)
            dbl = jnp.sum(dkh * kh, axis=0, keepdims=True) + jnp.sum(dst * st, axis=0, keepdims=True) * ebl
            dstate[...] = _dot_tn(do, qt) + dst * ebl
            dqa_s, dkk_s, db_s = _hg_scores_bwd(dp, qa, kk, b)
            dqa = dqt * eb + dqa_s
            dkk = dkh * ekb + dkk_s
            db = dqt * qt - dkh * kh + db_s
            row = lax.broadcasted_iota(jnp.int32, db.shape, 0)
            db = db + jnp.where(row == c - 1, dbl, 0.0)
            df = _cumsum_rows(db, reverse=True) / f - dkk
            dzf_ref[rows, :] = (df * (1.0 - lb) * sg * (1.0 - sg)).astype(dzf_ref.dtype)
            dzq_ref[rows, :] = (dqa * (HEAD ** -0.5) * sq * (1.0 + zq * (1.0 - sq))).astype(dzq_ref.dtype)
            dlb_ref[...] += jnp.sum(df * (1.0 - sg), axis=0, keepdims=True)
            dgn_ref[...] += jnp.sum(don * oh, axis=0, keepdims=True)
            return carry

        lax.fori_loop(0, cps, chunk, 0, unroll=True)

    tok = pl.BlockSpec((blk, HEAD), lambda h, b, n: (b * nblk + rev(n), h))
    vec = pl.BlockSpec((1, HEAD), lambda h, b, n: (0, h))
    rows = nseq * seq
    return _hosted_call(
        "hgrn2_bwd", body, grid=(heads, nseq, nblk),
        in_specs=[tok, zspec(qoff), zspec(qoff + heads), zspec(qoff + 2 * heads), zspec(qoff + 3 * heads), tok,
                  pl.BlockSpec((None, None, cps, HEAD, HEAD), lambda h, b, n: (h, b, rev(n), 0, 0)),
                  pl.BlockSpec((None, blk, CHUNK), lambda h, b, n: (h, b * nblk + rev(n), 0)), vec, vec],
        out_specs=[tok, tok, tok, tok, vec, vec],
        out_shape=[jax.ShapeDtypeStruct((rows, heads * HEAD), BF16)] * 4
        + [jax.ShapeDtypeStruct((1, heads * HEAD), F32)] * 2,
        scratch_shapes=[pltpu.VMEM((HEAD, HEAD), F32)],
        operands=(dyb, z, z, z, z, o, states, scores, lbrow, gain), rider=rider)


def _shift_rows(x, k):
    n = x.shape[0]
    r = pltpu.roll(x, k % n, 0)
    sub = lax.broadcasted_iota(jnp.int32, (SUBLANES, x.shape[1]), 0)
    if k > 0:
        return jnp.concatenate([jnp.where(sub >= k, r[0:SUBLANES], 0.0), r[SUBLANES:]], axis=0)
    return jnp.concatenate([r[:n - SUBLANES], jnp.where(sub < SUBLANES + k, r[n - SUBLANES:], 0.0)], axis=0)


def _conv_taps(h, w, bias):
    h1 = _shift_rows(h, 1)
    h2 = _shift_rows(h, 2)
    return h2 * w[0:1, :] + h1 * w[1:2, :] + h * w[2:3, :] + bias, h1, h2


def _conv_fwd(h, wconv, bconv, *, nseq, seq):
    ff2 = h.shape[1]
    ncol = ff2 // 2 // LANES

    def body(hg_ref, hv_ref, wg_ref, wv_ref, bg_ref, bv_ref, a_ref):
        g, _, _ = _conv_taps(hg_ref[...].astype(F32), wg_ref[...], bg_ref[...])
        v, _, _ = _conv_taps(hv_ref[...].astype(F32), wv_ref[...], bv_ref[...])
        a_ref[...] = (g * _sigmoid(g) * v).astype(a_ref.dtype)

    tok = lambda off: pl.BlockSpec((seq, LANES), lambda j, b, off=off: (b, off + j))
    wsp = lambda off: pl.BlockSpec((CONV_W, LANES), lambda j, b, off=off: (0, off + j))
    bsp = lambda off: pl.BlockSpec((1, LANES), lambda j, b, off=off: (0, off + j))
    return pl.pallas_call(
        body, name="conv_fwd", grid=(ncol, nseq),
        in_specs=[tok(0), tok(ncol), wsp(0), wsp(ncol), bsp(0), bsp(ncol)],
        out_specs=tok(0), out_shape=jax.ShapeDtypeStruct((nseq * seq, ff2 // 2), BF16),
        compiler_params=_params("arbitrary", "arbitrary"),
    )(h, h, wconv, wconv, bconv, bconv)


def _conv_bwd(da, h, wconv, bconv, *, nseq, seq):
    ff2 = h.shape[1]
    ncol = ff2 // 2 // LANES

    def half_bwd(d, hcur, h1, h2, w):
        d1 = _shift_rows(d, -1)
        d2 = _shift_rows(d, -2)
        dh = d * w[2:3, :] + d1 * w[1:2, :] + d2 * w[0:1, :]
        stats = jnp.concatenate(
            [jnp.sum(h2 * d, axis=0, keepdims=True), jnp.sum(h1 * d, axis=0, keepdims=True),
             jnp.sum(hcur * d, axis=0, keepdims=True), jnp.sum(d, axis=0, keepdims=True),
             jnp.zeros((SUBLANES - 4, d.shape[1]), F32)], axis=0)
        return dh, stats

    def body(da_ref, hg_ref, hv_ref, wg_ref, wv_ref, bg_ref, bv_ref, dhg_ref, dhv_ref, sg_ref, sv_ref):
        hg = hg_ref[...].astype(F32)
        hv = hv_ref[...].astype(F32)
        wg = wg_ref[...]
        wv = wv_ref[...]
        g, g1, g2 = _conv_taps(hg, wg, bg_ref[...])
        v, v1, v2 = _conv_taps(hv, wv, bv_ref[...])
        da = da_ref[...].astype(F32)
        s = _sigmoid(g)
        dhg, stg = half_bwd(da * v * s * (1.0 + g * (1.0 - s)), hg, g1, g2, wg)
        dhv, stv = half_bwd(da * g * s, hv, v1, v2, wv)
        dhg_ref[...] = dhg.astype(dhg_ref.dtype)
        dhv_ref[...] = dhv.astype(dhv_ref.dtype)
        first = pl.program_id(1) == 0
        for r, val in ((sg_ref, stg), (sv_ref, stv)):
            @pl.when(first)
            def _():
                r[...] = val

            @pl.when(jnp.logical_not(first))
            def _():
                r[...] += val

    tok = lambda off: pl.BlockSpec((seq, LANES), lambda j, b, off=off: (b, off + j))
    wsp = lambda off: pl.BlockSpec((CONV_W, LANES), lambda j, b, off=off: (0, off + j))
    bsp = lambda off: pl.BlockSpec((1, LANES), lambda j, b, off=off: (0, off + j))
    ssp = pl.BlockSpec((SUBLANES, LANES), lambda j, b: (0, j))
    dhg, dhv, stg, stv = pl.pallas_call(
        body, name="conv_bwd", grid=(ncol, nseq),
        in_specs=[tok(0), tok(0), tok(ncol), wsp(0), wsp(ncol), bsp(0), bsp(ncol)],
        out_specs=[tok(0), tok(0), ssp, ssp],
        out_shape=[jax.ShapeDtypeStruct((nseq * seq, ff2 // 2), BF16)] * 2
        + [jax.ShapeDtypeStruct((SUBLANES, ff2 // 2), F32)] * 2,
        compiler_params=_params("arbitrary", "arbitrary"),
    )(da, h, h, wconv, wconv, bconv, bconv)
    return (dhg, dhv), jnp.concatenate([stg, stv], axis=1)


def _rms_fwd(xv, g):
    r = lax.rsqrt(jnp.mean(xv * xv, axis=1, keepdims=True) + EPS)
    return (xv * r * g,)


def _rms_bwd(xv, g, dy, res):
    r = lax.rsqrt(jnp.mean(xv * xv, axis=1, keepdims=True) + EPS)
    xh = xv * r
    dxh = dy * g
    dx = r * (dxh - xh * jnp.mean(dxh * xh, axis=1, keepdims=True)) + res
    return dx, jnp.sum(dy * xh, axis=0, keepdims=True)


def _loss_head(x2, tgt, g):
    d = x2.shape[1]
    r = lax.rsqrt(jnp.mean(x2 * x2, axis=1, keepdims=True) + EPS)
    xh = x2 * r
    err = xh * g - tgt
    dy = err * (1.0 / d)
    dxh = dy * g
    dx = r * (dxh - xh * jnp.mean(dxh * xh, axis=1, keepdims=True))
    loss = 0.5 * jnp.sum(jnp.mean(err * err, axis=1, keepdims=True), axis=0, keepdims=True)
    return dx, jnp.sum(dy * xh, axis=0, keepdims=True), jnp.broadcast_to(loss, (1, LANES))


LATE_A = ("w_down", "w_out")
LATE_B = ("w_up", "w_pa", "w_pb")
LATE = LATE_A + LATE_B
EARLY_GRADS = ("w_down", "w_up", "w_out", "w_pa", "w_pb", "w_glu")
ROW_SHARDED = ("w_glu", "w_out", "w_down")


def _local_step(x, tgt, u, z_own, p, late, *, nseq, seq):
    p = dict(p)
    chip = 2 * lax.axis_index("x") + lax.axis_index("y")
    t, d = x.shape
    s5w = p["s5_d"].shape[1]
    hgw = p["gain"].shape[1]
    heads = hgw // HEAD
    qoff = s5w // LANES
    gblk = (s5w + 4 * hgw) // GATE_BLOCK
    ngb = d // GATE_BLOCK
    tm = _row_tile(t, ROW_TILE)
    row = lambda a, w=None, base=0: (a, a.shape[1] if w is None else w, base, "row")
    vec = lambda a, w=None, base=0: (a, a.shape[1] if w is None else w, base, "vec")
    rw = functools.partial(_rowwise, rows=t, tm=tm)

    z, _ = _in_proj_rest(u, p["w_in"], z_own, None)

    lam_re, lam_im, bb_re, bb_im = _s5_discretize(p["s5_a_re"], p["s5_a_im"], p["s5_log_dt"], p["s5_b_re"], p["s5_b_im"])
    bre3 = _s5_in_blocks(bb_re).astype(BF16)
    bim3 = _s5_in_blocks(bb_im).astype(BF16)
    cre3 = _s5_out_blocks(p["s5_c_re"]).astype(BF16)
    cim3 = _s5_out_blocks(p["s5_c_im"]).astype(BF16)
    coef_f = _s5_scan_tables(lam_re.reshape(-1), lam_im.reshape(-1), False)
    coef_r = _s5_scan_tables(lam_re.reshape(-1), lam_im.reshape(-1), True)
    (o, yb, states, scores), landed_b = _hg_fwd(z, p["lbrow"], p["gain"], nseq=nseq, seq=seq, heads=heads, qoff=qoff,
                                                rider=_gather_ici_rider([late[n] for n in LATE_B]))
    def place_own(names, gathered):
        for n, g in zip(names, gathered):
            full = lax.dynamic_update_index_in_dim(g, late[n], chip, 0)
            p[n] = full.reshape(-1, full.shape[-1]) if n in ROW_SHARDED else full

    nb_late = len(LATE_B)
    (y5, xre, xim), got = _s5_fwd(z, bre3, bim3, cre3, cim3, coef_f, p["s5_d"], nseq=nseq, seq=seq,
                                  rider=_merge_riders(_gather_pass_rider(list(landed_b)),
                                                      _gather_ici_rider([late[n] for n in LATE_A])))
    place_own(LATE_B, got[:nb_late])
    ya0, gl, ya = _glu_fwd(y5, p["w_glu"], p["b_glu"])

    joined = lambda w3: w3.transpose(1, 0, 2).reshape(w3.shape[1], -1)
    split = lambda g: g.reshape(g.shape[0], N_CHIPS, -1).transpose(1, 0, 2)
    wpa, wpb = joined(p["w_pa"]), joined(p["w_pb"])
    pa, got_a = _mm_fwd_rows("proj_a", ya, wpa, out_dtype=BF16, rider=_gather_pass_rider(list(got[nb_late:])))
    place_own(LATE_A, got_a)
    pb = _mm_fwd_rows("proj_b", yb, wpb, out_dtype=BF16)
    gb = GATE_BLOCK
    (m,) = rw("merge", lambda ga, gbv, a, b: (_sigmoid(ga) * a + _sigmoid(gbv) * b,),
              [row(z, gb, gblk), row(z, gb, gblk + ngb), row(pa, gb), row(pb, gb)], [(d, gb, BF16)], ncol=ngb)
    x1 = _mm_fwd_rows("out_proj", m, p["w_out"], res=x)

    (u2,) = rw("rms_ffn", _rms_fwd, [row(x1), vec(p["g_ffn"])], [(d, d, BF16)])
    h = _mm_fwd_cols("up_proj", u2, p["w_up"], out_dtype=BF16)
    a = _conv_fwd(h, p["w_conv"], p["b_conv"], nseq=nseq, seq=seq)
    dx2, dg_final, lossv = _mm_fwd_rows("down_proj", a, p["w_down"], res=x1,
                                        epilogue=(_loss_head, [tgt, p["g_final"]], [d, LANES]))

    norm_bwd = lambda dyv, xv, g, resv: _rms_bwd(xv, g, dyv, resv)
    da = _mm_bwd_rows("down_bwd", dx2, p["w_down"], out_dtype=BF16)
    g_wdown = _mm_wgrad_rows("down_wgrad", a, dx2)
    dh, cstats = _conv_bwd(da, h, p["w_conv"], p["b_conv"], nseq=nseq, seq=seq)
    dx1, dg_ffn = _mm_bwd_cols("up_bwd", dh, p["w_up"], epilogue=(norm_bwd, [x1, p["g_ffn"], dx2], [d]))
    g_wup = _mm_wgrad_cols("up_wgrad", u2, dh)

    dm = _mm_bwd_rows("out_bwd", dx1, p["w_out"], out_dtype=BF16)
    g_wout = _mm_wgrad_rows("out_wgrad", m, dx1)

    def merge_bwd(ga, gbv, av, bv, dmv):
        sa = _sigmoid(ga)
        sb = _sigmoid(gbv)
        return dmv * sa, dmv * sb, dmv * av * sa * (1.0 - sa), dmv * bv * sb * (1.0 - sb)

    dpa, dpb, dzga, dzgb = rw("merge_bwd", merge_bwd,
                              [row(z, gb, gblk), row(z, gb, gblk + ngb), row(pa, gb), row(pb, gb), row(dm, gb)],
                              [(d, gb, BF16)] * 4, ncol=ngb)
    dya = _mm_bwd_rows("proj_a_bwd", dpa, wpa)
    g_wpa = split(_mm_wgrad_rows("proj_a_wgrad", ya, dpa))
    dyb = _mm_bwd_rows("proj_b_bwd", dpb, wpb)
    g_wpb = split(_mm_wgrad_rows("proj_b_wgrad", yb, dpb))

    dgl, dy5, db_glu = _glu_bwd(y5, gl, p["b_glu"], dya, p["w_glu"])
    g_wglu = _mm_wgrad_rows("glu_wgrad", ya0, dgl)
    partial = dict(w_down=g_wdown, w_up=g_wup, w_out=g_wout, w_pa=g_wpa, w_pb=g_wpb, w_glu=g_wglu)
    parts = [_grad_parts(partial[n]) for n in EARLY_GRADS]
    (dza, dbre3, dbim3, dcre3, dcim3, dlam, dd), sib = _s5_bwd(
        dy5, z, xre, xim, bre3, bim3, cre3, cim3, coef_r, p["s5_d"], nseq=nseq, seq=seq, rider=_swap_halves_rider(parts))
    pair = _pair_sums(EARLY_GRADS, parts, sib)
    (dzq, dzf, dzi, dzg, dlb, dgain), others = _hg_bwd(
        dyb, z, o, states, scores, p["lbrow"], p["gain"], nseq=nseq, seq=seq, heads=heads, qoff=qoff,
        rider=_scatter_rider(pair))
    halves = _chip_sums(EARLY_GRADS, pair, others)

    dz = jnp.concatenate([dza, dzq, dzf, dzi, dzg, dzga, dzgb], axis=1)
    gshape = lam_re.shape
    small = {
        "loss": lossv, "g_ffn": dg_ffn, "g_final": dg_final, "b_glu": db_glu, "gain": dgain,
        "lbrow": dlb, "s5_d": dd, "w_conv": cstats[0:CONV_W], "b_conv": cstats[CONV_W:CONV_W + 1],
        "lam_re": dlam[:, 0, :].reshape(gshape), "lam_im": dlam[:, 1, :].reshape(gshape),
        "bb_re": _s5_in_blocks_diag(dbre3), "bb_im": _s5_in_blocks_diag(dbim3),
        "s5_c_re": _s5_out_blocks_diag(dcre3), "s5_c_im": _s5_out_blocks_diag(dcim3),
    }
    small_vec = _pack([small[n] for n in SMALL_PARTS], F32)
    g_win, (small_all, *sibs) = _mm_wgrad_cols(
        "in_wgrad", u, dz, rider=_merge_riders(_gather_all_rider(small_vec), _swap_sums_rider(halves)))
    big = dict(zip(EARLY_GRADS, zip(halves, sibs)))
    small_sum = _sum_over_devices("small_grad_sum", small_vec, small_all)
    sm = dict(zip(SMALL_PARTS, _unpack(small_sum, [small[n].shape for n in SMALL_PARTS])))
    last = [_grad_parts(g_win)]
    pair = _pair_sums(("w_in",), last, _run_rider("grad_swap_halves", _swap_halves_rider(last)))
    (dx, dg_mix), others = _mm_bwd_cols("in_bwd", dz, p["w_in"], epilogue=(norm_bwd, [x, p["g_mix"], dx1], [d]),
                                        rider=_scatter_rider(pair))
    (half,) = _chip_sums(("w_in",), pair, others)
    mid = half.shape[0] // 2
    mix_vec = dg_mix.reshape(SUBLANES, -1)
    top, bottom, mix_all = _run_rider("grad_swap_sums", _merge_riders(_swap_sums_rider([half[:mid], half[mid:]]),
                                                                      _gather_all_rider(mix_vec)))
    big["w_in"] = (half, jnp.concatenate([top, bottom], axis=0))
    sm["g_mix"] = _sum_over_devices("g_mix_sum", mix_vec, mix_all).reshape(dg_mix.shape)
    return dx, big, sm


ANY = pl.BlockSpec(memory_space=pl.ANY)


def _place():
    x, y, c = lax.axis_index("x"), lax.axis_index("y"), lax.axis_index("c")
    chips = [(1 - x, y), (x, 1 - y), (1 - x, 1 - y)]
    return x, y, c, chips


def _remote(src, dst, send_sems, recv_sems, k, to):
    return pltpu.make_async_remote_copy(src_ref=src, dst_ref=dst, send_sem=send_sems.at[k], recv_sem=recv_sems.at[k],
                                        device_id=to, device_id_type=MESH)


def _half(rows, which):
    return pl.ds(pl.multiple_of(which * (rows // 2), SUBLANES), rows // 2)


class _SemView:
    def __init__(self, base, offset):
        self.base, self.offset = base, offset

    @property
    def at(self):
        return self

    def __getitem__(self, k):
        return self.base.at[self.offset + k]


def _merge_riders(first, second):
    na, no, ns = len(first.arrays), len(first.out_shapes), first.nsem

    def split(fn_a, fn_b):
        def run(ins, outs, send_sems, recv_sems):
            fn_a(ins[:na], outs[:no], send_sems, recv_sems)
            fn_b(ins[na:], outs[no:], _SemView(send_sems, ns), _SemView(recv_sems, ns))
        return run

    aliases = dict(first.aliases)
    aliases.update({na + i: no + o for i, o in second.aliases.items()})
    return _Rider(first.arrays + second.arrays, first.out_shapes + second.out_shapes, ns + second.nsem,
                  split(first.start, second.start), split(first.finish, second.finish), aliases)


PREPARE_TILE = 512


def _prepare(x, gain, w_own, arrays, rider):
    t, d = x.shape
    ns = w_own.shape[1]
    n = len(arrays)
    tm = _row_tile(t, PREPARE_TILE)

    def body(place, x_ref, g_ref, w_ref, *refs):
        (u,) = _rms_fwd(x_ref[...], g_ref[...])
        u = u.astype(BF16)
        refs[n][...] = u
        refs[n + 1][...] = _dot(u, w_ref[...])

        @pl.when(pl.program_id(0) == 0)
        def _():
            for i in range(n):
                refs[n + 2 + i][...] = refs[i][...].astype(BF16)

    vm = pl.BlockSpec(memory_space=pltpu.VMEM)
    tok = pl.BlockSpec((tm, d), lambda i, place: (i, 0))
    outs, gathered = _hosted_scalar_call(
        "prepare", body, grid=(t // tm,),
        in_specs=[tok, pl.BlockSpec((1, d), lambda i, place: (0, 0)), vm] + [vm] * n,
        out_specs=[tok, pl.BlockSpec((tm, ns), lambda i, place: (i, place[1]))] + [vm] * n,
        out_shape=[jax.ShapeDtypeStruct((t, d), BF16), jax.ShapeDtypeStruct((t, N_CHIPS * ns), F32)]
        + [jax.ShapeDtypeStruct(a.shape, BF16) for a in arrays],
        operands=[x, gain, w_own] + list(arrays), rider=rider)
    return outs[0], outs[1], outs[2:], gathered


def _in_proj_rest(u, w3, z, rider):
    t, k = u.shape
    ns = w3.shape[2]
    tm = _mm_tile(t, 2 * k * _size(u) + 2 * ns * 4, 2 * k * ns * _size(w3))
    other = lambda j, place: jnp.bitwise_xor(place[1], j + 1)

    def body(place, u_ref, w_ref, z_ref, o_ref):
        o_ref[...] = _dot(u_ref[...], w_ref[...])

    outs, ridden = _hosted_scalar_call(
        "in_proj", body, grid=(N_CHIPS - 1, t // tm),
        in_specs=[pl.BlockSpec((tm, k), lambda j, i, place: (i, 0)),
                  pl.BlockSpec((None, k, ns), lambda j, i, place: (other(j, place), 0, 0)), ANY],
        out_specs=[pl.BlockSpec((tm, ns), lambda j, i, place: (i, other(j, place)))],
        out_shape=[jax.ShapeDtypeStruct(z.shape, z.dtype)], operands=[u, w3, z], rider=rider, aliases={2: 0})
    return outs[0], ridden


def _symmetric_rider(arrays, out_shapes, copies_of, nsem):
    def start(ins, outs, send_sems, recv_sems):
        for cp in copies_of(ins, outs, send_sems, recv_sems):
            cp.start()

    def finish(ins, outs, send_sems, recv_sems):
        for cp in copies_of(ins, outs, send_sems, recv_sems):
            cp.wait()

    return _Rider(arrays, out_shapes, nsem, start, finish)


def _swap_halves_rider(parts):
    def copies_of(ins, outs, send_sems, recv_sems):
        x, y, c, _ = _place()
        return [_remote(ins[a].at[:, _half(g.shape[1], 1 - c), :], outs[a], send_sems, recv_sems, a, (x, y, 1 - c))
                for a, g in enumerate(parts)]

    shapes = [jax.ShapeDtypeStruct((g.shape[0], g.shape[1] // 2, g.shape[2]), g.dtype) for g in parts]
    return _symmetric_rider(parts, shapes, copies_of, len(parts))


def _scatter_rider(parts):
    def copies_of(ins, outs, send_sems, recv_sems):
        x, y, c, chips = _place()
        return [_remote(ins[a].at[2 * cx + cy], outs[a].at[j], send_sems, recv_sems, 3 * a + j, (cx, cy, c))
                for a in range(len(parts)) for j, (cx, cy) in enumerate(chips)]

    shapes = [jax.ShapeDtypeStruct((N_CHIPS - 1,) + h.shape[1:], h.dtype) for h in parts]
    return _symmetric_rider(parts, shapes, copies_of, 3 * len(parts))


def _swap_sums_rider(parts):
    def copies_of(ins, outs, send_sems, recv_sems):
        x, y, c, _ = _place()
        return [_remote(ins[a], outs[a], send_sems, recv_sems, a, (x, y, 1 - c)) for a in range(len(parts))]

    shapes = [jax.ShapeDtypeStruct(g.shape, g.dtype) for g in parts]
    return _symmetric_rider(parts, shapes, copies_of, len(parts))


def _gather_ici_rider(shards):
    def sends(ins, outs, send_sems, recv_sems):
        x, y, c, chips = _place()
        return [_remote(ins[a].at[_half(s.shape[0], c)], outs[a].at[2 * x + y, _half(s.shape[0], c)], send_sems,
                        recv_sems, 3 * a + j, (cx, cy, c)) for a, s in enumerate(shards) for j, (cx, cy) in enumerate(chips)]

    def start(ins, outs, send_sems, recv_sems):
        for cp in sends(ins, outs, send_sems, recv_sems):
            cp.start()

    def finish(ins, outs, send_sems, recv_sems):
        x, y, c, chips = _place()
        for a, s in enumerate(shards):
            for j, (cx, cy) in enumerate(chips):
                landed = outs[a].at[2 * cx + cy, _half(s.shape[0], c)]
                _remote(landed, landed, send_sems, recv_sems, 3 * a + j, (x, y, c)).wait_recv()
        for cp in sends(ins, outs, send_sems, recv_sems):
            cp.wait_send()

    shapes = [jax.ShapeDtypeStruct((N_CHIPS,) + s.shape, s.dtype) for s in shards]
    return _Rider(shards, shapes, 3 * len(shards), start, finish)


def _gather_full_rider(shards):
    n = len(shards)

    def sends(ins, outs, send_sems, recv_sems):
        x, y, c, chips = _place()
        return [_remote(ins[a].at[_half(s.shape[0], c)], outs[a].at[2 * x + y, _half(s.shape[0], c)], send_sems,
                        recv_sems, 6 * a + j, (cx, cy, c)) for a, s in enumerate(shards) for j, (cx, cy) in enumerate(chips)]

    def start(ins, outs, send_sems, recv_sems):
        for cp in sends(ins, outs, send_sems, recv_sems):
            cp.start()

    def finish(ins, outs, send_sems, recv_sems):
        x, y, c, chips = _place()
        passed = []
        for a, s in enumerate(shards):
            for j, (cx, cy) in enumerate(chips):
                landed = outs[a].at[2 * cx + cy, _half(s.shape[0], c)]
                _remote(landed, landed, send_sems, recv_sems, 6 * a + j, (x, y, c)).wait_recv()
                passed.append(_remote(landed, landed, send_sems, recv_sems, 6 * a + 3 + j, (x, y, 1 - c)))
                passed[-1].start()
        for a, s in enumerate(shards):
            for j, (cx, cy) in enumerate(chips):
                other = outs[a].at[2 * cx + cy, _half(s.shape[0], 1 - c)]
                _remote(other, other, send_sems, recv_sems, 6 * a + 3 + j, (x, y, c)).wait_recv()
        for cp in sends(ins, outs, send_sems, recv_sems) + passed:
            cp.wait_send()

    shapes = [jax.ShapeDtypeStruct((N_CHIPS,) + s.shape, s.dtype) for s in shards]
    return _Rider(shards, shapes, 6 * n, start, finish)


def _gather_pass_rider(landed):
    def sends(ins, outs, send_sems, recv_sems):
        x, y, c, chips = _place()
        return [_remote(ins[a].at[2 * cx + cy, _half(g.shape[1], c)], outs[a].at[2 * cx + cy, _half(g.shape[1], c)],
                        send_sems, recv_sems, 3 * a + j, (x, y, 1 - c))
                for a, g in enumerate(landed) for j, (cx, cy) in enumerate(chips)]

    def start(ins, outs, send_sems, recv_sems):
        for cp in sends(ins, outs, send_sems, recv_sems):
            cp.start()

    def finish(ins, outs, send_sems, recv_sems):
        x, y, c, chips = _place()
        for a, g in enumerate(landed):
            for j, (cx, cy) in enumerate(chips):
                other = outs[a].at[2 * cx + cy, _half(g.shape[1], 1 - c)]
                _remote(other, other, send_sems, recv_sems, 3 * a + j, (x, y, c)).wait_recv()
        for cp in sends(ins, outs, send_sems, recv_sems):
            cp.wait_send()

    shapes = [jax.ShapeDtypeStruct(g.shape, g.dtype) for g in landed]
    return _Rider(landed, shapes, 3 * len(landed), start, finish, aliases={a: a for a in range(len(landed))})


def _grad_parts(g):
    return g.reshape((N_CHIPS, -1, g.shape[-1]))


def _place_scalars():
    return jnp.stack([lax.axis_index("c"), 2 * lax.axis_index("x") + lax.axis_index("y")]).astype(jnp.int32)


def _scalar_call(body, name, grid, in_specs, out_specs, out_shape, operands):
    spec = pltpu.PrefetchScalarGridSpec(num_scalar_prefetch=1, grid=grid, in_specs=in_specs, out_specs=out_specs)
    return pl.pallas_call(body, name=name, grid_spec=spec, out_shape=out_shape,
                          compiler_params=_params(*(["arbitrary"] * len(grid))))(_place_scalars(), *operands)


def _pair_sums(names, parts, sib):
    out = []
    for n, g, s in zip(names, parts, sib):
        rh, cols = s.shape[1], s.shape[2]
        tm = _row_tile(rh, ROW_TILE)
        nblk = rh // tm

        def body(place, g_ref, s_ref, o_ref):
            o_ref[...] = (g_ref[...].astype(F32) + s_ref[...].astype(F32)).astype(o_ref.dtype)

        blk = pl.BlockSpec((None, tm, cols), lambda j, i, place: (j, i, 0))
        own = pl.BlockSpec((None, tm, cols), lambda j, i, place, nblk=nblk: (j, place[0] * nblk + i, 0))
        out.append(_scalar_call(body, "grad_pair_sum_" + n, (N_CHIPS, nblk), [own, blk], blk,
                                jax.ShapeDtypeStruct(s.shape, BF16), (g, s)))
    return out


def _chip_sums(names, pair, others):
    out = []
    for n, h, o in zip(names, pair, others):
        rh, cols = h.shape[1], h.shape[2]
        tm = _row_tile(rh, ROW_TILE)

        def body(place, h_ref, a_ref, b_ref, c_ref, o_ref):
            o_ref[...] = (h_ref[...].astype(F32) + a_ref[...].astype(F32)) + b_ref[...].astype(F32) + c_ref[...].astype(F32)

        mine = pl.BlockSpec((None, tm, cols), lambda i, place: (place[1], i, 0))
        other = lambda k: pl.BlockSpec((None, tm, cols), lambda i, place, k=k: (k, i, 0))
        out.append(_scalar_call(body, "grad_chip_sum_" + n, (rh // tm,), [mine, other(0), other(1), other(2)],
                                pl.BlockSpec((tm, cols), lambda i, place: (i, 0)), jax.ShapeDtypeStruct((rh, cols), F32),
                                (h, o, o, o)))
    return out


def _adamw_halves(name, w, m, v, own, sib):
    rh, cols = own.shape
    tm = _row_tile(rh, ADAMW_TILE)
    nblk = rh // tm

    def body(place, w_ref, m_ref, v_ref, own_ref, sib_ref, g_ref, d_ref, m2_ref, v2_ref):
        mine = pl.program_id(0) // nblk == place[0]

        def run(gv):
            g_ref[...] = gv
            d_ref[...], m2_ref[...], v2_ref[...] = _adamw_math(w_ref[...], gv, m_ref[...], v_ref[...])

        @pl.when(mine)
        def _():
            run(own_ref[...])

        @pl.when(jnp.logical_not(mine))
        def _():
            run(sib_ref[...])

    full = pl.BlockSpec((tm, cols), lambda i, place: (i, 0))
    own_spec = pl.BlockSpec((tm, cols), lambda i, place: (jnp.where(i // nblk == place[0], i % nblk, 0), 0))
    sib_spec = pl.BlockSpec((tm, cols), lambda i, place: (jnp.where(i // nblk == place[0], 0, i % nblk), 0))
    return _scalar_call(body, name, (2 * nblk,), [full, full, full, own_spec, sib_spec], [full] * 4,
                        [jax.ShapeDtypeStruct((2 * rh, cols), F32)] * 4, (w, m, v, own, sib))


def _gather_all_rider(v):
    m_per = v.shape[0]

    def rows(ref, px, py, pc):
        return ref.at[pl.ds(pl.multiple_of((4 * px + 2 * py + pc) * m_per, 8), m_per)]

    def first(ins, outs, send_sems, recv_sems):
        x, y, c, chips = _place()
        mine = rows(outs[0], x, y, c)
        return [_remote(ins[0], mine, send_sems, recv_sems, 0, (x, y, 1 - c))] + [
            _remote(ins[0], mine, send_sems, recv_sems, 1 + j, (cx, cy, c)) for j, (cx, cy) in enumerate(chips)]

    def start(ins, outs, send_sems, recv_sems):
        for cp in first(ins, outs, send_sems, recv_sems):
            cp.start()

    def finish(ins, outs, send_sems, recv_sems):
        x, y, c, chips = _place()
        passed = []
        for j, (cx, cy) in enumerate(chips):
            blk = rows(outs[0], cx, cy, c)
            _remote(blk, blk, send_sems, recv_sems, 1 + j, (x, y, c)).wait_recv()
            passed.append(_remote(blk, blk, send_sems, recv_sems, 4 + j, (x, y, 1 - c)))
            passed[j].start()
        sib = rows(outs[0], x, y, 1 - c)
        _remote(sib, sib, send_sems, recv_sems, 0, (x, y, c)).wait_recv()
        for j, (cx, cy) in enumerate(chips):
            blk = rows(outs[0], cx, cy, 1 - c)
            _remote(blk, blk, send_sems, recv_sems, 4 + j, (x, y, c)).wait_recv()
        for cp in first(ins, outs, send_sems, recv_sems) + passed:
            cp.wait_send()

    return _Rider([v], [jax.ShapeDtypeStruct((N_DEV * m_per,) + v.shape[1:], v.dtype)], 7, start, finish)


def _sum_over_devices(name, v, gathered):
    m_per = v.shape[0]
    dev = 4 * lax.axis_index("x") + 2 * lax.axis_index("y") + lax.axis_index("c")
    full = lax.dynamic_update_slice_in_dim(gathered, v, dev * m_per, axis=0)
    return _sum_blocks(name, [full[i * m_per:(i + 1) * m_per] for i in range(N_DEV)], F32)


def _sum_blocks(name, parts, out_dtype):
    rows, cols = parts[0].shape
    tm = _row_tile(rows, ROW_TILE)

    def body(*refs):
        acc = refs[0][...].astype(F32)
        for r in refs[1:-1]:
            acc = acc + r[...].astype(F32)
        refs[-1][...] = acc.astype(refs[-1].dtype)

    spec = pl.BlockSpec((tm, cols), lambda i: (i, 0))
    return pl.pallas_call(
        body, name=name, grid=(rows // tm,), in_specs=[spec] * len(parts), out_specs=spec,
        out_shape=jax.ShapeDtypeStruct((rows, cols), out_dtype), compiler_params=_params("arbitrary"),
    )(*parts)


def _adamw_math(wv, gv, mv, vv):
    m2 = ADAM_B1 * mv + (1.0 - ADAM_B1) * gv
    v2 = ADAM_B2 * vv + (1.0 - ADAM_B2) * (gv * gv)
    delta = -ADAM_LR * ((m2 / (1.0 - ADAM_B1 ** ADAM_STEP)) / (jnp.sqrt(v2 / (1.0 - ADAM_B2 ** ADAM_STEP)) + ADAM_EPS)
                        + ADAM_WD * wv)
    return delta, m2, v2


def _adamw_small(ws, gs, ms, vs):
    n = len(ws)

    def body(*refs):
        for i in range(n):
            res = _adamw_math(refs[i][...], refs[n + i][...], refs[2 * n + i][...], refs[3 * n + i][...])
            for k in range(3):
                refs[(4 + k) * n + i][...] = res[k]

    vm = pl.BlockSpec(memory_space=pltpu.VMEM)
    outs = pl.pallas_call(
        body, name="adamw_small", in_specs=[vm] * (4 * n), out_specs=[vm] * (3 * n),
        out_shape=[jax.ShapeDtypeStruct(a.shape, F32) for a in ws] * 3,
        compiler_params=pltpu.CompilerParams(vmem_limit_bytes=VMEM_LIMIT_BYTES),
    )(*ws, *gs, *ms, *vs)
    return outs[:n], outs[n:2 * n], outs[2 * n:]


PACK_ROWS = 256


def _pack(flat_parts, dtype, lead=()):
    parts = [a.astype(dtype).reshape(lead + (-1,)) for a in flat_parts]
    n = sum(a.shape[-1] for a in parts)
    chunk = PACK_ROWS * LANES
    total = -(-n // chunk) * chunk
    if total > n:
        parts.append(jnp.zeros(lead + (total - n,), dtype))
    return jnp.concatenate(parts, axis=-1).reshape(lead + (total // LANES, LANES))


def _unpack(buf, shapes, lead=()):
    flat = buf.reshape(lead + (-1,))
    out, off = [], 0
    for shp in shapes:
        n = math.prod(shp)
        out.append(lax.slice_in_dim(flat, off, off + n, axis=len(lead)).reshape(lead + tuple(shp)))
        off += n
    return out


BIG = ("w_in", "w_glu", "w_pa", "w_pb", "w_out", "w_up", "w_down")
WEIGHTS = ("g_mix", "w_in", "s5_a_re", "s5_a_im", "s5_log_dt", "s5_b_re", "s5_b_im", "s5_c_re", "s5_c_im", "s5_d",
           "w_glu", "b_glu", "hg_lb_logits", "hg_norm_gain", "w_pa", "w_pb", "w_out", "g_ffn", "w_up", "w_conv",
           "b_conv", "w_down", "g_final")
SMALL = tuple(n for n in WEIGHTS if n not in BIG)
SMALL_PARTS = ("loss", "g_ffn", "g_final", "b_glu", "gain", "lbrow", "s5_d", "w_conv", "b_conv", "lam_re", "lam_im",
               "bb_re", "bb_im", "s5_c_re", "s5_c_im")


def _lower_bound(logits):
    return jnp.cumsum(jax.nn.softmax(logits, axis=0), axis=0)[0:1]


def kernel(x, g_mix, w_in, s5_a_re, s5_a_im, s5_log_dt, s5_b_re, s5_b_im, s5_c_re, s5_c_im, s5_d, w_glu, b_glu, hg_lb_logits, hg_norm_gain, w_pa, w_pb, w_out, g_ffn, w_up, w_conv, b_conv, w_down, g_final, loss_target, m_g_mix, m_w_in, m_s5_a_re, m_s5_a_im, m_s5_log_dt, m_s5_b_re, m_s5_b_im, m_s5_c_re, m_s5_c_im, m_s5_d, m_w_glu, m_b_glu, m_hg_lb_logits, m_hg_norm_gain, m_w_pa, m_w_pb, m_w_out, m_g_ffn, m_w_up, m_w_conv, m_b_conv, m_w_down, m_g_final, v_g_mix, v_w_in, v_s5_a_re, v_s5_a_im, v_s5_log_dt, v_s5_b_re, v_s5_b_im, v_s5_c_re, v_s5_c_im, v_s5_d, v_w_glu, v_b_glu, v_hg_lb_logits, v_hg_norm_gain, v_w_pa, v_w_pb, v_w_out, v_g_ffn, v_w_up, v_w_conv, v_b_conv, v_w_down, v_g_final):
    args = dict(locals())
    w = {n: args[n] for n in WEIGHTS}
    mom = {n: args["m_" + n] for n in WEIGHTS}
    var = {n: args["v_" + n] for n in WEIGHTS}
    nseq, seq, d = x.shape
    xi, yi = lax.axis_index("x"), lax.axis_index("y")
    chip = 2 * xi + yi

    shard = {n: w[n][0] for n in BIG}
    first = [shard["w_in"].astype(BF16), shard["w_glu"].astype(BF16),
             jnp.pad(w_conv[0], ((0, 2 * SUBLANES - CONV_W), (0, 0)))]
    x2 = x.reshape(nseq * seq, d)
    u, z_own, late16, got = _prepare(x2, g_mix, first[0], [shard[n] for n in LATE], _gather_full_rider(first))
    w_in_all, w_glu_all, conv_all = [lax.dynamic_update_index_in_dim(g, s, chip, 0) for g, s in zip(got, first)]
    p = dict(g_mix=g_mix, g_ffn=g_ffn, g_final=g_final.reshape(1, -1), b_glu=b_glu, gain=hg_norm_gain, s5_d=s5_d,
             b_conv=b_conv, lbrow=_lower_bound(hg_lb_logits),
             s5_a_re=s5_a_re[0], s5_a_im=s5_a_im[0], s5_log_dt=s5_log_dt[0], s5_b_re=s5_b_re[0], s5_b_im=s5_b_im[0],
             s5_c_re=s5_c_re[0], s5_c_im=s5_c_im[0], w_in=w_in_all, w_glu=w_glu_all.reshape(-1, w_glu_all.shape[-1]),
             w_conv=conv_all[:, :CONV_W].transpose(1, 0, 2).reshape(CONV_W, -1))

    dx, halves, sm = _local_step(x2, loss_target.reshape(nseq * seq, d), u, z_own, p, dict(zip(LATE, late16)),
                                 nseq=nseq, seq=seq)
    loss = sm["loss"][0, 0]

    grads, delta, new_m, new_v = {}, {}, {}, {}
    for n in BIG:
        shp = shard[n].shape
        grads[n], delta[n], new_m[n], new_v[n] = _adamw_halves("adamw_" + n, shard[n], mom[n].reshape(shp),
                                                               var[n].reshape(shp), *halves[n])

    _, disc_vjp = jax.vjp(_s5_discretize, p["s5_a_re"], p["s5_a_im"], p["s5_log_dt"], p["s5_b_re"], p["s5_b_im"])
    da_re, da_im, dlog_dt, db_re, db_im = disc_vjp((sm["lam_re"], sm["lam_im"], sm["bb_re"], sm["bb_im"]))
    _, lb_vjp = jax.vjp(_lower_bound, hg_lb_logits)
    (dlogits,) = lb_vjp(sm["lbrow"])
    fcols = w_conv.shape[-1]
    grads.update(
        g_mix=sm["g_mix"], g_ffn=sm["g_ffn"], g_final=sm["g_final"].reshape(-1), b_glu=sm["b_glu"],
        hg_norm_gain=sm["gain"], hg_lb_logits=dlogits, s5_d=sm["s5_d"], b_conv=sm["b_conv"],
        w_conv=lax.dynamic_slice_in_dim(sm["w_conv"], chip * fcols, fcols, axis=1),
        s5_a_re=da_re, s5_a_im=da_im, s5_log_dt=dlog_dt, s5_b_re=db_re, s5_b_im=db_im,
        s5_c_re=sm["s5_c_re"], s5_c_im=sm["s5_c_im"])
    grads = {n: grads[n].reshape(w[n].shape) for n in WEIGHTS}

    def natural(a):
        return a.reshape(1, -1) if a.ndim == 1 else (a[0] if a.ndim > 2 else a)

    outs = _adamw_small(*[[natural(src[n]) for n in SMALL] for src in (w, grads, mom, var)])
    for dst, group in zip((delta, new_m, new_v), outs):
        dst.update(zip(SMALL, group))
    res = [loss, dx.reshape(x.shape)]
    for group in (grads, delta, new_m, new_v):
        res += [group[n].reshape(w[n].shape) for n in WEIGHTS]
    return tuple(res)
```
